```python
import jax, jax.numpy as jnp
from jax import lax
import numpy as np

D_MODEL = 1024
BATCH = 8
SEQ = 4096
DEPTH = 1

CHUNK = 64
N_META = 16
Q_BLOCK = 128
N_PAD = Q_BLOCK - N_META
PREFIX = N_PAD + N_META
MIX_WIDTH = D_MODEL
RET_HEADS = 4
RET_DV = (MIX_WIDTH // 2) // RET_HEADS
RET_DK = RET_DV // 2
FOX_HEADS = 8
FOX_DH = (MIX_WIDTH // 2) // FOX_HEADS
D_FF = 2816
CONV_W = 3
ROPE_BASE = 10000.0
EPS = 1e-6
NEG = -1e30

RET_QK = RET_HEADS * RET_DK
RET_V = RET_HEADS * RET_DV
FOX_W = FOX_HEADS * FOX_DH
SPLIT_POINTS = (RET_QK, 2 * RET_QK, 2 * RET_QK + RET_V, 2 * RET_QK + 2 * RET_V,
                2 * RET_QK + 2 * RET_V + FOX_W, 2 * RET_QK + 2 * RET_V + 2 * FOX_W,
                2 * RET_QK + 2 * RET_V + 3 * FOX_W)
IN_WIDTH = 2 * RET_QK + 2 * RET_V + 3 * FOX_W + FOX_HEADS

kernel_name = "hymba_retention_fox_convffn_block"


def rms_norm(x, g):
    xf = x.astype(jnp.float32)
    y = xf * lax.rsqrt(jnp.mean(xf * xf, axis=-1, keepdims=True) + EPS)
    return (y * g.astype(jnp.float32)).astype(x.dtype)


def rotary(x, pos):
    half = x.shape[-1] // 2
    inv = 1.0 / (ROPE_BASE ** (jnp.arange(half, dtype=jnp.float32) / half))
    ang = pos.astype(jnp.float32)[:, None] * inv[None, :]
    cos = jnp.cos(ang)[None, :, None, :]
    sin = jnp.sin(ang)[None, :, None, :]
    x1 = x[..., :half].astype(jnp.float32)
    x2 = x[..., half:].astype(jnp.float32)
    return jnp.concatenate([x1 * cos - x2 * sin, x1 * sin + x2 * cos], axis=-1)


def retention(q, k, v, valid):
    B, L, H, dk = q.shape
    dv = v.shape[-1]
    nc = L // CHUNK
    f32 = jnp.float32
    log_g = jnp.log1p(-jnp.exp2(-5.0 - jnp.arange(H, dtype=f32)))
    pos = jnp.arange(L)
    qr = rotary(q, pos)
    kr = rotary(k, pos) * (dk ** -0.5) * valid.astype(f32)[None, :, None, None]
    qc = qr.reshape(B, nc, CHUNK, H, dk)
    kc = kr.reshape(B, nc, CHUNK, H, dk)
    vc = v.astype(f32).reshape(B, nc, CHUNK, H, dv)
    n = jnp.arange(CHUNK, dtype=f32)
    d_intra = jnp.exp(jnp.abs(n[:, None] - n[None, :])[None] * log_g[:, None, None])
    s = jnp.einsum('bcnhd,bcmhd->bchnm', qc, kc) * d_intra[None, None]
    intra = jnp.einsum('bchnm,bcmhe->bcnhe', s, vc)
    w_k = jnp.exp((CHUNK - 1.0 - n)[:, None] * log_g[None, :])
    u = jnp.einsum('bcmhd,bcmhe->bchde', kc * w_k[None, None, :, :, None], vc)
    g_chunk = jnp.exp(CHUNK * log_g)[None, :, None, None]

    def step(r, u_i):
        return g_chunk * r + u_i, r

    _, r_prev = lax.scan(step, jnp.zeros((B, H, dk, dv), f32), jnp.moveaxis(u, 1, 0))
    r_prev = jnp.moveaxis(r_prev, 0, 1)
    w_q = jnp.exp((n + 1.0)[:, None] * log_g[None, :])
    inter = jnp.einsum('bcnhd,bchde->bcnhe', qc * w_q[None, None, :, :, None], r_prev)
    return (intra + inter).reshape(B, L, H, dv)


def forgetting_attention(q, k, v, log_f, valid):
    B, L, H, dh = q.shape
    f32 = jnp.float32
    scale = dh ** -0.5
    c = jnp.cumsum(log_f.astype(f32), axis=1).transpose(0, 2, 1)
    qf, kf, vf = q.astype(f32), k.astype(f32), v.astype(f32)
    pos = jnp.arange(L)
    outs = []
    for blk in range(L // Q_BLOCK):
        q0, q1 = blk * Q_BLOCK, (blk + 1) * Q_BLOCK
        logits = jnp.einsum('bqhd,bkhd->bhqk', qf[:, q0:q1], kf[:, :q1]) * scale
        bias = c[:, :, q0:q1, None] - c[:, :, None, :q1]
        mask = (pos[None, :q1] <= pos[q0:q1, None]) & valid[None, :q1]
        logits = jnp.where(mask[None, None], logits + bias, NEG)
        p = jax.nn.softmax(logits, axis=-1)
        outs.append(jnp.einsum('bhqk,bkhd->bqhd', p, vf[:, :q1]))
    return jnp.concatenate(outs, axis=1)


def hybrid_mixer(h, w_in, forget_b, ret_norm_g, w_out, valid):
    B, L, _ = h.shape
    proj = jnp.einsum('bld,de->ble', h, w_in)
    rq, rk, rv, rg, fq, fk, fv, ff = jnp.split(proj, SPLIT_POINTS, axis=-1)
    o_r = retention(rq.reshape(B, L, RET_HEADS, RET_DK), rk.reshape(B, L, RET_HEADS, RET_DK),
                    rv.reshape(B, L, RET_HEADS, RET_DV), valid)
    o_r = o_r * lax.rsqrt(jnp.mean(o_r * o_r, axis=-1, keepdims=True) + EPS)
    o_r = o_r.reshape(B, L, RET_V) * ret_norm_g.astype(jnp.float32) * jax.nn.silu(rg.astype(jnp.float32))
    log_f = jax.nn.log_sigmoid(ff.astype(jnp.float32) + forget_b.astype(jnp.float32))
    o_f = forgetting_attention(fq.reshape(B, L, FOX_HEADS, FOX_DH), fk.reshape(B, L, FOX_HEADS, FOX_DH),
                               fv.reshape(B, L, FOX_HEADS, FOX_DH), log_f, valid).reshape(B, L, FOX_W)
    mixed = jnp.concatenate([o_r, o_f], axis=-1).astype(h.dtype)
    return jnp.einsum('ble,ed->bld', mixed, w_out)


def conv_ffn(h, w_up, conv_w, conv_b, w_down, valid):
    L = h.shape[1]
    up = jnp.einsum('bld,df->blf', h, w_up)
    a, b = jnp.split(up, 2, axis=-1)
    a = a * valid.astype(a.dtype)[None, :, None]
    a_pad = jnp.pad(a, ((0, 0), (CONV_W - 1, 0), (0, 0)))
    acc = conv_b
    for j in range(CONV_W):
        acc = acc + a_pad[:, j:j + L] * conv_w[j]
    return jnp.einsum('blf,fd->bld', jax.nn.silu(acc) * b, w_down)


def _fwd_setup_inputs(seed: int = 0) -> dict:
    key = jax.random.key(seed)
    ks = jax.random.split(key, 13)
    f32 = jnp.float32
    return {
        "x": jax.random.normal(ks[0], (BATCH, SEQ, D_MODEL), f32),
        "meta_tokens": jax.random.normal(ks[1], (N_META, D_MODEL), f32),
        "attn_norm_g": 1.0 + 0.02 * jax.random.normal(ks[2], (DEPTH, D_MODEL), f32),
        "w_in": jax.random.normal(ks[3], (DEPTH, D_MODEL, IN_WIDTH), f32) * D_MODEL ** -0.5,
        "fox_forget_b": jax.random.uniform(ks[4], (DEPTH, FOX_HEADS), f32, 1.0, 5.0),
        "ret_norm_g": 1.0 + 0.02 * jax.random.normal(ks[5], (DEPTH, RET_V), f32),
        "w_out": jax.random.normal(ks[6], (DEPTH, MIX_WIDTH, D_MODEL), f32) * MIX_WIDTH ** -0.5,
        "ffn_norm_g": 1.0 + 0.02 * jax.random.normal(ks[7], (DEPTH, D_MODEL), f32),
        "w_up": jax.random.normal(ks[8], (DEPTH, D_MODEL, 2 * D_FF), f32) * D_MODEL ** -0.5,
        "conv_w": jax.random.normal(ks[9], (DEPTH, CONV_W, D_FF), f32) * CONV_W ** -0.5,
        "conv_b": 0.02 * jax.random.normal(ks[10], (DEPTH, D_FF), f32),
        "w_down": jax.random.normal(ks[11], (DEPTH, D_FF, D_MODEL), f32) * D_FF ** -0.5,
        "final_norm_g": 1.0 + 0.02 * jax.random.normal(ks[12], (D_MODEL,), f32),
    }


def _fwd_reference(x, meta_tokens, attn_norm_g, w_in, fox_forget_b, ret_norm_g, w_out,
              ffn_norm_g, w_up, conv_w, conv_b, w_down, final_norm_g):
    B = x.shape[0]
    pad = jnp.zeros((B, N_PAD, D_MODEL), x.dtype)
    meta = jnp.broadcast_to(meta_tokens.astype(x.dtype)[None], (B, N_META, D_MODEL))
    h = jnp.concatenate([pad, meta, x], axis=1)
    valid = jnp.arange(h.shape[1]) >= N_PAD
    for layer in range(DEPTH):
        h = h + hybrid_mixer(rms_norm(h, attn_norm_g[layer]), w_in[layer], fox_forget_b[layer],
                             ret_norm_g[layer], w_out[layer], valid)
        h = h + conv_ffn(rms_norm(h, ffn_norm_g[layer]), w_up[layer], conv_w[layer], conv_b[layer],
                         w_down[layer], valid)
    return rms_norm(h, final_norm_g)[:, PREFIX:]


import jax as _jax
import jax.numpy as _jnp

TWIN_FORMAT = 'train_step'
FWD_PARAMS = ['x', 'meta_tokens', 'attn_norm_g', 'w_in', 'fox_forget_b', 'ret_norm_g', 'w_out', 'ffn_norm_g', 'w_up', 'conv_w', 'conv_b', 'w_down', 'final_norm_g']
TWIN_WEIGHTS = ['meta_tokens', 'attn_norm_g', 'w_in', 'fox_forget_b', 'ret_norm_g', 'w_out', 'ffn_norm_g', 'w_up', 'conv_w', 'conv_b', 'w_down', 'final_norm_g']
TWIN_DIFF_INPUT = 'x'
TWIN_INPUTS = ['x', 'meta_tokens', 'attn_norm_g', 'w_in', 'fox_forget_b', 'ret_norm_g', 'w_out', 'ffn_norm_g', 'w_up', 'conv_w', 'conv_b', 'w_down', 'final_norm_g', 'loss_target', 'm_meta_tokens', 'm_attn_norm_g', 'm_w_in', 'm_fox_forget_b', 'm_ret_norm_g', 'm_w_out', 'm_ffn_norm_g', 'm_w_up', 'm_conv_w', 'm_conv_b', 'm_w_down', 'm_final_norm_g', 'v_meta_tokens', 'v_attn_norm_g', 'v_w_in', 'v_fox_forget_b', 'v_ret_norm_g', 'v_w_out', 'v_ffn_norm_g', 'v_w_up', 'v_conv_w', 'v_conv_b', 'v_w_down', 'v_final_norm_g']
TWIN_OUTPUTS = ['loss', 'grad_x', 'grad_meta_tokens', 'grad_attn_norm_g', 'grad_w_in', 'grad_fox_forget_b', 'grad_ret_norm_g', 'grad_w_out', 'grad_ffn_norm_g', 'grad_w_up', 'grad_conv_w', 'grad_conv_b', 'grad_w_down', 'grad_final_norm_g', 'delta_meta_tokens', 'delta_attn_norm_g', 'delta_w_in', 'delta_fox_forget_b', 'delta_ret_norm_g', 'delta_w_out', 'delta_ffn_norm_g', 'delta_w_up', 'delta_conv_w', 'delta_conv_b', 'delta_w_down', 'delta_final_norm_g', 'new_m_meta_tokens', 'new_m_attn_norm_g', 'new_m_w_in', 'new_m_fox_forget_b', 'new_m_ret_norm_g', 'new_m_w_out', 'new_m_ffn_norm_g', 'new_m_w_up', 'new_m_conv_w', 'new_m_conv_b', 'new_m_w_down', 'new_m_final_norm_g', 'new_v_meta_tokens', 'new_v_attn_norm_g', 'new_v_w_in', 'new_v_fox_forget_b', 'new_v_ret_norm_g', 'new_v_w_out', 'new_v_ffn_norm_g', 'new_v_w_up', 'new_v_conv_w', 'new_v_conv_b', 'new_v_w_down', 'new_v_final_norm_g']
TWIN_LEAF_KINDS = {'loss': 'loss', 'grad_x': 'grad_x', 'grad_meta_tokens': 'grad_w', 'grad_attn_norm_g': 'grad_w', 'grad_w_in': 'grad_w', 'grad_fox_forget_b': 'grad_w', 'grad_ret_norm_g': 'grad_w', 'grad_w_out': 'grad_w', 'grad_ffn_norm_g': 'grad_w', 'grad_w_up': 'grad_w', 'grad_conv_w': 'grad_w', 'grad_conv_b': 'grad_w', 'grad_w_down': 'grad_w', 'grad_final_norm_g': 'grad_w', 'delta_meta_tokens': 'delta_w', 'delta_attn_norm_g': 'delta_w', 'delta_w_in': 'delta_w', 'delta_fox_forget_b': 'delta_w', 'delta_ret_norm_g': 'delta_w', 'delta_w_out': 'delta_w', 'delta_ffn_norm_g': 'delta_w', 'delta_w_up': 'delta_w', 'delta_conv_w': 'delta_w', 'delta_conv_b': 'delta_w', 'delta_w_down': 'delta_w', 'delta_final_norm_g': 'delta_w', 'new_m_meta_tokens': 'new_m', 'new_m_attn_norm_g': 'new_m', 'new_m_w_in': 'new_m', 'new_m_fox_forget_b': 'new_m', 'new_m_ret_norm_g': 'new_m', 'new_m_w_out': 'new_m', 'new_m_ffn_norm_g': 'new_m', 'new_m_w_up': 'new_m', 'new_m_conv_w': 'new_m', 'new_m_conv_b': 'new_m', 'new_m_w_down': 'new_m', 'new_m_final_norm_g': 'new_m', 'new_v_meta_tokens': 'new_v', 'new_v_attn_norm_g': 'new_v', 'new_v_w_in': 'new_v', 'new_v_fox_forget_b': 'new_v', 'new_v_ret_norm_g': 'new_v', 'new_v_w_out': 'new_v', 'new_v_ffn_norm_g': 'new_v', 'new_v_w_up': 'new_v', 'new_v_conv_w': 'new_v', 'new_v_conv_b': 'new_v', 'new_v_w_down': 'new_v', 'new_v_final_norm_g': 'new_v'}


def _forward(args):
    return _fwd_reference(*[args[k] for k in FWD_PARAMS])


def _output_shape():
    out = _jax.eval_shape(lambda: _forward(_fwd_setup_inputs(0)))
    return out.shape, out.dtype

N_MICROBATCH = 1
ADAM_LR = 0.001
ADAM_B1 = 0.9
ADAM_B2 = 0.999
ADAM_EPS = 1e-08
ADAM_WD = 0.01
ADAM_STEP = 10
PER_EXAMPLE_BATCH_AXIS = {'x': 0, 'loss_target': 0}
SHARED_INPUTS = []
_WEIGHT_DTYPES = {'meta_tokens': _jnp.float32, 'attn_norm_g': _jnp.float32, 'w_in': _jnp.float32, 'fox_forget_b': _jnp.float32, 'ret_norm_g': _jnp.float32, 'w_out': _jnp.float32, 'ffn_norm_g': _jnp.float32, 'w_up': _jnp.float32, 'conv_w': _jnp.float32, 'conv_b': _jnp.float32, 'w_down': _jnp.float32, 'final_norm_g': _jnp.float32}
MOMENT_SCALE = {'meta_tokens': 6.378942e-03, 'attn_norm_g': 1.694219e-01, 'w_in': 9.608770e-02, 'fox_forget_b': 2.275179e-01, 'ret_norm_g': 1.103079e-01, 'w_out': 8.784188e-02, 'ffn_norm_g': 1.258280e-01, 'w_up': 5.236853e-02, 'conv_w': 5.265906e-02, 'conv_b': 4.962219e-02, 'w_down': 8.558007e-02, 'final_norm_g': 3.200234e+01}


def _to_microbatches(a, axis):
    t = _jnp.moveaxis(a, axis, 0)
    t = t.reshape((N_MICROBATCH, t.shape[0] // N_MICROBATCH) + t.shape[1:])
    return _jnp.moveaxis(t, 1, axis + 1)


def setup_inputs(seed: int = 0) -> dict:
    inp = _fwd_setup_inputs(seed)
    key = _jax.random.fold_in(_jax.random.key(seed), 7919)
    shape, _ = _output_shape()
    out = dict(inp)
    out["loss_target"] = _jax.random.normal(_jax.random.fold_in(key, 0), shape, _jnp.float32)
    for i, name in enumerate(TWIN_WEIGHTS):
        w = inp[name].astype(_jnp.float32)
        if MOMENT_SCALE is None:
            s = _jnp.sqrt(_jnp.mean(_jnp.square(w)) + 1e-30)
        else:
            s = MOMENT_SCALE[name]
        km, kv = _jax.random.split(_jax.random.fold_in(key, i + 1))
        out[name] = w
        out["m_" + name] = s * _jax.random.normal(km, w.shape, _jnp.float32)
        out["v_" + name] = (s * s) * _jax.random.uniform(kv, w.shape, _jnp.float32, 0.5, 1.5)
    if N_MICROBATCH > 1:
        for name, axis in PER_EXAMPLE_BATCH_AXIS.items():
            out[name] = _to_microbatches(out[name], axis)
    return {'x': out['x'], 'meta_tokens': out['meta_tokens'], 'attn_norm_g': out['attn_norm_g'], 'w_in': out['w_in'], 'fox_forget_b': out['fox_forget_b'], 'ret_norm_g': out['ret_norm_g'], 'w_out': out['w_out'], 'ffn_norm_g': out['ffn_norm_g'], 'w_up': out['w_up'], 'conv_w': out['conv_w'], 'conv_b': out['conv_b'], 'w_down': out['w_down'], 'final_norm_g': out['final_norm_g'], 'loss_target': out['loss_target'], 'm_meta_tokens': out['m_meta_tokens'], 'm_attn_norm_g': out['m_attn_norm_g'], 'm_w_in': out['m_w_in'], 'm_fox_forget_b': out['m_fox_forget_b'], 'm_ret_norm_g': out['m_ret_norm_g'], 'm_w_out': out['m_w_out'], 'm_ffn_norm_g': out['m_ffn_norm_g'], 'm_w_up': out['m_w_up'], 'm_conv_w': out['m_conv_w'], 'm_conv_b': out['m_conv_b'], 'm_w_down': out['m_w_down'], 'm_final_norm_g': out['m_final_norm_g'], 'v_meta_tokens': out['v_meta_tokens'], 'v_attn_norm_g': out['v_attn_norm_g'], 'v_w_in': out['v_w_in'], 'v_fox_forget_b': out['v_fox_forget_b'], 'v_ret_norm_g': out['v_ret_norm_g'], 'v_w_out': out['v_w_out'], 'v_ffn_norm_g': out['v_ffn_norm_g'], 'v_w_up': out['v_w_up'], 'v_conv_w': out['v_conv_w'], 'v_conv_b': out['v_conv_b'], 'v_w_down': out['v_w_down'], 'v_final_norm_g': out['v_final_norm_g']}


def _loss(weights, diff, rest, loss_target):
    with _jax.named_scope("forward"):
        args = {**rest, TWIN_DIFF_INPUT: diff, **{k: w.astype(_WEIGHT_DTYPES[k]) for k, w in weights.items()}}
        y = _forward(args)
    with _jax.named_scope("loss_head"):
        err = _jnp.square(y.astype(_jnp.float32) - loss_target)
        return 0.5 * _jnp.sum(_jnp.mean(err, axis=-1)) if err.ndim else 0.5 * err


def _adamw(w, g, m, v):
    m = ADAM_B1 * m + (1.0 - ADAM_B1) * g
    v = ADAM_B2 * v + (1.0 - ADAM_B2) * _jnp.square(g)
    m_hat = m / (1.0 - ADAM_B1 ** ADAM_STEP)
    v_hat = v / (1.0 - ADAM_B2 ** ADAM_STEP)
    delta = -ADAM_LR * (m_hat / (_jnp.sqrt(v_hat) + ADAM_EPS) + ADAM_WD * w)
    return delta, m, v


def reference(x, meta_tokens, attn_norm_g, w_in, fox_forget_b, ret_norm_g, w_out, ffn_norm_g, w_up, conv_w, conv_b, w_down, final_norm_g, loss_target, m_meta_tokens, m_attn_norm_g, m_w_in, m_fox_forget_b, m_ret_norm_g, m_w_out, m_ffn_norm_g, m_w_up, m_conv_w, m_conv_b, m_w_down, m_final_norm_g, v_meta_tokens, v_attn_norm_g, v_w_in, v_fox_forget_b, v_ret_norm_g, v_w_out, v_ffn_norm_g, v_w_up, v_conv_w, v_conv_b, v_w_down, v_final_norm_g):
    given = dict(x=x, meta_tokens=meta_tokens, attn_norm_g=attn_norm_g, w_in=w_in, fox_forget_b=fox_forget_b, ret_norm_g=ret_norm_g, w_out=w_out, ffn_norm_g=ffn_norm_g, w_up=w_up, conv_w=conv_w, conv_b=conv_b, w_down=w_down, final_norm_g=final_norm_g, loss_target=loss_target, m_meta_tokens=m_meta_tokens, m_attn_norm_g=m_attn_norm_g, m_w_in=m_w_in, m_fox_forget_b=m_fox_forget_b, m_ret_norm_g=m_ret_norm_g, m_w_out=m_w_out, m_ffn_norm_g=m_ffn_norm_g, m_w_up=m_w_up, m_conv_w=m_conv_w, m_conv_b=m_conv_b, m_w_down=m_w_down, m_final_norm_g=m_final_norm_g, v_meta_tokens=v_meta_tokens, v_attn_norm_g=v_attn_norm_g, v_w_in=v_w_in, v_fox_forget_b=v_fox_forget_b, v_ret_norm_g=v_ret_norm_g, v_w_out=v_w_out, v_ffn_norm_g=v_ffn_norm_g, v_w_up=v_w_up, v_conv_w=v_conv_w, v_conv_b=v_conv_b, v_w_down=v_w_down, v_final_norm_g=v_final_norm_g)
    weights = {n: given[n] for n in TWIN_WEIGHTS}
    shared = {n: given[n] for n in SHARED_INPUTS}
    per_example = {n: given[n] for n in ['x']}
    grad_fn = _jax.value_and_grad(_loss, argnums=(0, 1))

    def one_microbatch(ex, loss_target):
        ex = dict(ex)
        diff = ex.pop(TWIN_DIFF_INPUT)
        return grad_fn(weights, diff, {**shared, **ex}, loss_target)

    if N_MICROBATCH == 1:
        loss, (grad_w, grad_x) = one_microbatch(per_example, given["loss_target"])
    else:
        def body(carry, xs):
            loss_sum, grad_sum = carry
            l_k, (gw_k, gx_k) = one_microbatch(xs[0], xs[1])
            with _jax.named_scope("update"):
                return (loss_sum + l_k, _jax.tree.map(_jnp.add, grad_sum, gw_k)), gx_k

        init = (_jnp.zeros((), _jnp.float32), _jax.tree.map(_jnp.zeros_like, weights))
        (loss, grad_w), grad_x = _jax.lax.scan(body, init, (per_example, given["loss_target"]))
    with _jax.named_scope("update"):
        delta_w, new_m, new_v = {}, {}, {}
        for n in TWIN_WEIGHTS:
            delta_w[n], new_m[n], new_v[n] = _adamw(weights[n], grad_w[n], given["m_" + n], given["v_" + n])
    return (loss, grad_x, *[grad_w[n] for n in TWIN_WEIGHTS], *[delta_w[n] for n in TWIN_WEIGHTS],
            *[new_m[n] for n in TWIN_WEIGHTS], *[new_v[n] for n in TWIN_WEIGHTS])
```

```python
import functools

import numpy as np
import jax
import jax.numpy as jnp
from jax import lax
from jax.experimental import pallas as pl
from jax.experimental.pallas import tpu as pltpu

F32 = jnp.float32
BF16 = jnp.bfloat16

D_MODEL = 1024
N_META = 16
BLK = 128
CHUNK = 64
N_PAD = BLK - N_META
PREFIX = BLK
RET_HEADS = 4
FOX_HEADS = 8
HEAD_LANES = 64
D_FF = 2816
ROPE_BASE = 10000.0
EPS = 1e-6
NEG = -1e30
RET_W = 1536
FOX_W = 1536
MAIN_W = RET_W + FOX_W
IN_WIDTH = MAIN_W + FOX_HEADS
N_CHIPS = 4
N_DEV = 8

ADAM_LR = 0.001
ADAM_B1 = 0.9
ADAM_B2 = 0.999
ADAM_EPS = 1e-08
ADAM_WD = 0.01
ADAM_STEP = 10

MESH = pl.DeviceIdType.MESH
VMEM_LIMIT_MB = 56

_NT = (((1,), (1,)), ((), ()))
_TN = (((0,), (0,)), ((), ()))


def _dot(a, b):
    return jnp.dot(a, b, preferred_element_type=F32)


def _dot_nt(a, b):
    return lax.dot_general(a, b, _NT, preferred_element_type=F32)


def _dot_tn(a, b):
    return lax.dot_general(a, b, _TN, preferred_element_type=F32)


def _params(dims=None, vmem_mb=VMEM_LIMIT_MB):
    kw = dict(vmem_limit_bytes=vmem_mb << 20)
    if dims is not None:
        kw["dimension_semantics"] = dims
    return pltpu.CompilerParams(**kw)


def _row_tile(n, prefs=(384, 256, 128)):
    for t in prefs:
        if n % t == 0:
            return t
    raise ValueError(f"no row tile for {n}")


def _iota(shape, dim):
    return lax.broadcasted_iota(jnp.int32, shape, dim)


def _pick_row(tile, row):
    sub = _iota(tile.shape, 0)
    return jnp.sum(jnp.where(sub == row, tile, 0.0), axis=0, keepdims=True)


def _split3(x):
    hi = x.astype(BF16)
    r1 = x - hi.astype(F32)
    mid = r1.astype(BF16)
    lo = (r1 - mid.astype(F32)).astype(BF16)
    return hi, mid, lo


def _full(shape):
    nd = len(shape)
    return pl.BlockSpec(shape, lambda *_: (0,) * nd)


def _in_perm():
    cols = list(range(RET_W))
    for p in range(FOX_HEADS // 2):
        for part in range(3):
            start = RET_W + part * 512 + p * BLK
            cols += list(range(start, start + BLK))
    return np.asarray(cols, np.int32)


def _rotary_tables(L):
    half = HEAD_LANES // 2
    inv = 1.0 / (ROPE_BASE ** (jnp.arange(half, dtype=F32) / half))
    ang = jnp.arange(L).astype(F32)[:, None] * inv[None, :]
    cos, sin = jnp.cos(ang), jnp.sin(ang)
    cos_t = jnp.tile(cos, (1, 4))
    sin_t = jnp.tile(jnp.concatenate([-sin, sin], axis=1), (1, 2))
    return cos_t, sin_t


def _decay_tables():
    gam = 1.0 - 2.0 ** (-5.0 - np.arange(RET_HEADS, dtype=np.float64))
    n = np.arange(BLK)
    same_or_past = (n[:, None] // CHUNK) >= (n[None, :] // CHUNK)
    dist = np.abs(n[:, None] - n[None, :])
    dmat = np.stack([np.where(same_or_past, g ** dist, 0.0) for g in gam]).astype(np.float32)
    lane_head = np.arange(BLK) // HEAD_LANES
    wq = np.stack([gam[2 * p + lane_head][None, :] ** (n[:, None] + 1.0) for p in range(2)]).astype(np.float32)
    wk = np.stack([gam[2 * p + lane_head][None, :] ** (BLK - 1.0 - n[:, None]) for p in range(2)]).astype(np.float32)
    g_blk = tuple(float(g ** BLK) for g in gam)
    return jnp.asarray(dmat), jnp.asarray(wq), jnp.asarray(wk), g_blk


def _rms_inproj(h0, g, w_main, w_ff):
    L = h0.shape[0]
    tm = _row_tile(L)

    def body(h_ref, g_ref, wm_ref, wf_ref, n_ref, p_ref, ff_ref):
        h = h_ref[...]
        r = lax.rsqrt(jnp.mean(h * h, axis=-1, keepdims=True) + EPS)
        n = (h * r * g_ref[...]).astype(BF16)
        n_ref[...] = n
        p_ref[...] = _dot(n, wm_ref[...]).astype(BF16)
        ff_ref[...] = _dot(n, wf_ref[...])

    return pl.pallas_call(
        body, name="f_inproj", grid=(L // tm,),
        in_specs=[pl.BlockSpec((tm, D_MODEL), lambda i: (i, 0)), _full((1, D_MODEL)),
                  _full((D_MODEL, MAIN_W)), _full((D_MODEL, BLK))],
        out_specs=[pl.BlockSpec((tm, D_MODEL), lambda i: (i, 0)), pl.BlockSpec((tm, MAIN_W), lambda i: (i, 0)),
                   pl.BlockSpec((tm, BLK), lambda i: (i, 0))],
        out_shape=[jax.ShapeDtypeStruct((L, D_MODEL), BF16), jax.ShapeDtypeStruct((L, MAIN_W), BF16),
                   jax.ShapeDtypeStruct((L, BLK), F32)],
        compiler_params=_params(("parallel",)),
    )(h0, g, w_main, w_ff)


def _fox_prep(ff, fb):
    L = ff.shape[0]
    nblk = L // BLK

    def body(ff_ref, b_ref, c_ref, ct_ref, carry):
        i = pl.program_id(0)

        @pl.when(i == 0)
        def _():
            carry[...] = jnp.zeros_like(carry)

        z = ff_ref[...] + b_ref[...]
        lf = jnp.minimum(z, 0.0) - jnp.log1p(jnp.exp(-jnp.abs(z)))
        lf = jnp.where(_iota((BLK, BLK), 1) < FOX_HEADS, lf, 0.0)
        tri = (_iota((BLK, BLK), 0) >= _iota((BLK, BLK), 1)).astype(BF16)
        hi, mid, lo = _split3(lf)
        cs = _dot(tri, hi) + _dot(tri, mid) + _dot(tri, lo) + carry[...]
        c_ref[...] = cs
        ct_ref[0] = cs.T[0:8, :]
        carry[...] = carry[...] + jnp.sum(lf, axis=0, keepdims=True)

    return pl.pallas_call(
        body, name="f_foxprep", grid=(nblk,),
        in_specs=[pl.BlockSpec((BLK, BLK), lambda i: (i, 0)), _full((1, BLK))],
        out_specs=[pl.BlockSpec((BLK, BLK), lambda i: (i, 0)), pl.BlockSpec((1, 8, BLK), lambda i: (i, 0, 0))],
        out_shape=[jax.ShapeDtypeStruct((L, BLK), F32), jax.ShapeDtypeStruct((nblk, 8, BLK), F32)],
        scratch_shapes=[pltpu.VMEM((1, BLK), F32)],
        compiler_params=_params(("arbitrary",)),
    )(ff, fb)


def _rot_fns(cos, sin):
    lane = _iota((BLK, BLK), 1)
    first = (lane & (HEAD_LANES - 1)) < HEAD_LANES // 2

    def swap(x):
        return jnp.where(first, pltpu.roll(x, BLK - 32, 1), pltpu.roll(x, 32, 1))

    def rot(x):
        return x * cos + swap(x) * sin

    def rot_t(dy):
        return dy * cos + swap(dy * sin)

    return rot, rot_t


def _retention_fwd(proj, cos_t, sin_t, ret_g):
    L = proj.shape[0]
    nblk = L // BLK
    dmat, wq_t, wk_t, g_blk = _decay_tables()

    def body(q_ref, k_ref, v_ref, gate_ref, cos_ref, sin_ref, d_ref, wq_ref, wk_ref, rg_ref,
             mix_ref, o_ref, rs_ref, state):
        i = pl.program_id(0)

        @pl.when(i == 0)
        def _():
            state[...] = jnp.zeros_like(state)

        rot, _ = _rot_fns(cos_ref[...], sin_ref[...])
        lane = _iota((BLK, BLK), 1)
        sub = _iota((BLK, BLK), 0)
        for p in range(2):
            qr = rot(q_ref[:, p * BLK:(p + 1) * BLK].astype(F32))
            kr = rot(k_ref[:, p * BLK:(p + 1) * BLK].astype(F32)) * (HEAD_LANES ** -0.5)
            kr_b = kr.astype(BF16)
            qw = (qr * wq_ref[p]).astype(BF16)
            kw = (kr * wk_ref[p]).astype(BF16)
            for e in range(2):
                h = 2 * p + e
                cols = slice(h * BLK, (h + 1) * BLK)
                qm = jnp.where((lane >> 6) == e, qr, 0.0).astype(BF16)
                s = _dot_nt(qm, kr_b) * d_ref[h]
                vh = v_ref[:, cols]
                st = state[h]
                rs_ref[0, h] = st
                o = _dot(s.astype(BF16), vh) + _dot(qw, st.astype(BF16))
                u = jnp.where((sub >> 6) == e, _dot_tn(kw, vh), 0.0)
                state[h] = g_blk[h] * st + u
                rn = lax.rsqrt(jnp.mean(o * o, axis=-1, keepdims=True) + EPS)
                gate = gate_ref[:, cols].astype(F32)
                o_ref[:, cols] = o
                mix_ref[:, cols] = (o * rn * rg_ref[:, cols] * (gate * jax.nn.sigmoid(gate))).astype(BF16)

    row = lambda c: (lambda i: (i, c))
    return pl.pallas_call(
        body, name="f_retention", grid=(nblk,),
        in_specs=[pl.BlockSpec((BLK, 256), row(0)), pl.BlockSpec((BLK, 256), row(1)),
                  pl.BlockSpec((BLK, 512), row(1)), pl.BlockSpec((BLK, 512), row(2)),
                  pl.BlockSpec((BLK, BLK), row(0)), pl.BlockSpec((BLK, BLK), row(0)),
                  _full((RET_HEADS, BLK, BLK)), _full((2, BLK, BLK)), _full((2, BLK, BLK)), _full((1, 512))],
        out_specs=[pl.BlockSpec((BLK, 512), row(0)), pl.BlockSpec((BLK, 512), row(0)),
                   pl.BlockSpec((1, RET_HEADS, BLK, BLK), lambda i: (i, 0, 0, 0))],
        out_shape=[jax.ShapeDtypeStruct((L, 512), BF16), jax.ShapeDtypeStruct((L, 512), F32),
                   jax.ShapeDtypeStruct((nblk, RET_HEADS, BLK, BLK), F32)],
        scratch_shapes=[pltpu.VMEM((RET_HEADS, BLK, BLK), F32)],
        compiler_params=_params(("arbitrary",)),
    )(proj, proj, proj, proj, cos_t, sin_t, dmat, wq_t, wk_t, ret_g)


def _fox_masks(p):
    lane = _iota((BLK, BLK), 1)
    sub = _iota((BLK, BLK), 0)
    return lane, sub, [(lane >> 6) == e for e in range(2)]


def _fox_fwd(proj, c, ctb):
    L = proj.shape[0]
    nblk = L // BLK

    def body(qkv_ref, c_ref, ct_ref, of_ref, lse_ref, vt, csb):
        p = pl.program_id(0)

        @pl.when(p == 0)
        def _():
            lse_ref[...] = jnp.zeros_like(lse_ref)

        lane, sub, lane_e = _fox_masks(p)
        sub8 = _iota((8, BLK), 0)

        def pre(j, carry):
            off = pl.multiple_of(j * BLK, BLK)
            vt[j] = qkv_ref[pl.ds(off, BLK), 2 * BLK:3 * BLK].astype(F32).T.astype(BF16)
            ct = c_ref[pl.ds(off, BLK), :]
            for e in range(2):
                col = jnp.sum(jnp.where(lane == 2 * p + e, ct, 0.0), axis=1, keepdims=True)
                csb[e, j] = jnp.broadcast_to(col, (BLK, BLK))
            return carry

        lax.fori_loop(0, nblk, pre, 0)

        def q_loop(qi, carry):
            qoff = pl.multiple_of(qi * BLK, BLK)
            qs = qkv_ref[pl.ds(qoff, BLK), 0:BLK].astype(F32) * (HEAD_LANES ** -0.5)
            qm = [jnp.where(lane_e[e], qs, 0.0).astype(BF16) for e in range(2)]
            ctile = ct_ref[qi]
            ct_row = [_pick_row(ctile, 2 * p + e) for e in range(2)]
            qpos = qi * BLK + lane

            def k_loop(kj, st):
                koff = pl.multiple_of(kj * BLK, BLK)
                kt = qkv_ref[pl.ds(koff, BLK), BLK:2 * BLK]
                kpos = kj * BLK + sub
                ok = (kpos <= qpos) & (kpos >= N_PAD)
                vtj = vt[kj]
                out = []
                for e in range(2):
                    m, l, acc = st[3 * e:3 * e + 3]
                    s = _dot_nt(kt, qm[e]) + ct_row[e] - csb[e, kj]
                    s = jnp.where(ok, s, NEG)
                    m_new = jnp.maximum(m, jnp.max(s, axis=0, keepdims=True))
                    alpha = jnp.exp(m - m_new)
                    pr = jnp.exp(s - m_new)
                    l = alpha * l + jnp.sum(pr, axis=0, keepdims=True)
                    acc = alpha * acc + _dot(vtj, pr.astype(BF16))
                    out += [m_new, l, acc]
                return tuple(out)

            init = (jnp.full((1, BLK), NEG, F32), jnp.zeros((1, BLK), F32), jnp.zeros((BLK, BLK), F32)) * 2
            st = lax.fori_loop(0, qi + 1, k_loop, init)
            o_t = jnp.where(sub < HEAD_LANES, st[2] * (1.0 / st[1]), st[5] * (1.0 / st[4]))
            of_ref[pl.ds(qoff, BLK), :] = o_t.T.astype(BF16)
            lse = [st[3 * e] + jnp.log(st[3 * e + 1]) for e in range(2)]
            lse_ref[qi] = lse_ref[qi] + (jnp.where(sub8 == 2 * p, lse[0], 0.0) + jnp.where(sub8 == 2 * p + 1, lse[1], 0.0))
            return carry

        lax.fori_loop(0, nblk, q_loop, 0)

    return pl.pallas_call(
        body, name="f_fox", grid=(FOX_HEADS // 2,),
        in_specs=[pl.BlockSpec((L, 384), lambda p: (0, RET_W // 384 + p)), _full((L, BLK)), _full((nblk, 8, BLK))],
        out_specs=[pl.BlockSpec((L, BLK), lambda p: (0, p)), _full((nblk, 8, BLK))],
        out_shape=[jax.ShapeDtypeStruct((L, 512), BF16), jax.ShapeDtypeStruct((nblk, 8, BLK), F32)],
        scratch_shapes=[pltpu.VMEM((nblk, BLK, BLK), BF16), pltpu.VMEM((2, nblk, BLK, BLK), F32)],
        compiler_params=_params(("arbitrary",)),
    )(proj, c, ctb)


def _outproj_up(mix_r, o_f, h0, w_out, ffn_g, w_up):
    L = h0.shape[0]
    tm = _row_tile(L)
    shard = w_up.shape[2]

    def body(mr_ref, of_ref, h0_ref, wo_ref, g_ref, wu_ref, h1_ref, n2_ref, up_ref):
        h1 = h0_ref[...] + _dot(mr_ref[...], wo_ref[0:512, :]) + _dot(of_ref[...], wo_ref[512:1024, :])
        h1_ref[...] = h1
        r = lax.rsqrt(jnp.mean(h1 * h1, axis=-1, keepdims=True) + EPS)
        n2 = (h1 * r * g_ref[...]).astype(BF16)
        n2_ref[...] = n2
        for j in range(N_CHIPS):
            up_ref[:, j * shard:(j + 1) * shard] = _dot(n2, wu_ref[j]).astype(BF16)

    rows = lambda w: pl.BlockSpec((tm, w), lambda i: (i, 0))
    return pl.pallas_call(
        body, name="f_outproj_up", grid=(L // tm,),
        in_specs=[rows(512), rows(512), rows(D_MODEL), _full((D_MODEL, D_MODEL)), _full((1, D_MODEL)),
                  _full((N_CHIPS, D_MODEL, shard))],
        out_specs=[rows(D_MODEL), rows(D_MODEL), rows(2 * D_FF)],
        out_shape=[jax.ShapeDtypeStruct((L, D_MODEL), F32), jax.ShapeDtypeStruct((L, D_MODEL), BF16),
                   jax.ShapeDtypeStruct((L, 2 * D_FF), BF16)],
        compiler_params=_params(("parallel",)),
    )(mix_r, o_f, h0, w_out, ffn_g, w_up)


def _conv_acc(a_ref, halo_ref, cw_refs, cb_ref, i, tm):
    sub = _iota((tm, 1), 0)
    a = jnp.where(i * tm + sub >= N_PAD, a_ref[...].astype(F32), 0.0)
    halo = halo_ref[...].astype(F32)
    hrow = i * tm - 8 + _iota((8, 1), 0)
    halo = jnp.where((hrow >= N_PAD) & (i > 0), halo, 0.0)
    a1 = jnp.where(sub == 0, _pick_row(halo, 7), pltpu.roll(a, 1, 0))
    a2 = jnp.where(sub == 0, _pick_row(halo, 6), jnp.where(sub == 1, _pick_row(halo, 7), pltpu.roll(a, 2, 0)))
    acc = cb_ref[...] + a2 * cw_refs[0][...]
    acc = acc + a1 * cw_refs[1][...]
    acc = acc + a * cw_refs[2][...]
    return a, a1, a2, acc


def _ffn_down_loss(up, conv_w, conv_b, w_down, h1, final_g, target):
    L = h1.shape[0]
    tm = _row_tile(L)
    cw = [conv_w[j:j + 1] for j in range(3)]

    def body(a_ref, halo_ref, b_ref, cw0, cw1, cw2, cb_ref, wd_ref, h1_ref, gf_ref, t_ref,
             g_ref, dh_ref, dhb_ref, dgf_ref, loss_ref):
        i = pl.program_id(0)

        @pl.when(i == 0)
        def _():
            dgf_ref[...] = jnp.zeros_like(dgf_ref)
            loss_ref[...] = jnp.zeros_like(loss_ref)

        _, _, _, acc = _conv_acc(a_ref, halo_ref, (cw0, cw1, cw2), cb_ref, i, tm)
        g = (acc * jax.nn.sigmoid(acc) * b_ref[...].astype(F32)).astype(BF16)
        g_ref[...] = g
        h2 = h1_ref[...] + _dot(g, wd_ref[...])
        r = lax.rsqrt(jnp.mean(h2 * h2, axis=-1, keepdims=True) + EPS)
        yn = h2 * r
        gf = gf_ref[...]
        live = i * tm + _iota((tm, 1), 0) >= PREFIX
        err = jnp.where(live, yn * gf - t_ref[...], 0.0)
        loss_ref[...] = loss_ref[...] + 0.5 * jnp.sum(jnp.mean(err * err, axis=-1, keepdims=True))
        dy = err * (1.0 / D_MODEL)
        dgf_ref[...] = dgf_ref[...] + jnp.sum(dy * yn, axis=0, keepdims=True)
        dyn = dy * gf
        dh = r * (dyn - yn * jnp.mean(dyn * yn, axis=-1, keepdims=True))
        dh_ref[...] = dh
        dhb_ref[...] = dh.astype(BF16)

    rows = lambda w, c=0: pl.BlockSpec((tm, w), lambda i: (i, c))
    halo = pl.BlockSpec((8, D_FF), lambda i: (jnp.maximum(i * (tm // 8) - 1, 0), 0))
    return pl.pallas_call(
        body, name="f_ffn_down_loss", grid=(L // tm,),
        in_specs=[rows(D_FF), halo, rows(D_FF, 1), _full((1, D_FF)), _full((1, D_FF)), _full((1, D_FF)),
                  _full((1, D_FF)), _full((D_FF, D_MODEL)), rows(D_MODEL), _full((1, D_MODEL)), rows(D_MODEL)],
        out_specs=[rows(D_FF), rows(D_MODEL), rows(D_MODEL), _full((1, D_MODEL)), _full((1, BLK))],
        out_shape=[jax.ShapeDtypeStruct((L, D_FF), BF16), jax.ShapeDtypeStruct((L, D_MODEL), F32),
                   jax.ShapeDtypeStruct((L, D_MODEL), BF16), jax.ShapeDtypeStruct((1, D_MODEL), F32),
                   jax.ShapeDtypeStruct((1, BLK), F32)],
        compiler_params=_params(("arbitrary",)),
    )(up, up, up, cw[0], cw[1], cw[2], conv_b, w_down, h1, final_g, target)


def _ffn_bwd_gate(dh2b, w_down, up, conv_w, conv_b):
    L = dh2b.shape[0]
    tm = _row_tile(L)
    cw = [conv_w[j:j + 1] for j in range(3)]

    def body(dh_ref, wd_ref, a_ref, halo_ref, b_ref, cw0, cw1, cw2, cb_ref, dacc_ref, db_ref, dcw_ref):
        i = pl.program_id(0)

        @pl.when(i == 0)
        def _():
            dcw_ref[...] = jnp.zeros_like(dcw_ref)

        a, a1, a2, acc = _conv_acc(a_ref, halo_ref, (cw0, cw1, cw2), cb_ref, i, tm)
        dg = _dot_nt(dh_ref[...], wd_ref[...])
        sg = jax.nn.sigmoid(acc)
        db_ref[...] = (dg * acc * sg).astype(BF16)
        dacc = dg * b_ref[...].astype(F32) * (sg * (1.0 + acc * (1.0 - sg)))
        dacc_ref[...] = dacc.astype(BF16)
        sub8 = _iota((8, 1), 0)
        rows = [jnp.sum(dacc * t, axis=0, keepdims=True) for t in (a2, a1, a)] + [jnp.sum(dacc, axis=0, keepdims=True)]
        upd = jnp.zeros((8, D_FF), F32)
        for j, rj in enumerate(rows):
            upd = upd + jnp.where(sub8 == j, rj, 0.0)
        dcw_ref[...] = dcw_ref[...] + upd

    rows = lambda w, c=0: pl.BlockSpec((tm, w), lambda i: (i, c))
    halo = pl.BlockSpec((8, D_FF), lambda i: (jnp.maximum(i * (tm // 8) - 1, 0), 0))
    return pl.pallas_call(
        body, name="b_ffn_gate", grid=(L // tm,),
        in_specs=[rows(D_MODEL), _full((D_FF, D_MODEL)), rows(D_FF), halo, rows(D_FF, 1),
                  _full((1, D_FF)), _full((1, D_FF)), _full((1, D_FF)), _full((1, D_FF))],
        out_specs=[rows(D_FF), rows(D_FF), _full((8, D_FF))],
        out_shape=[jax.ShapeDtypeStruct((L, D_FF), BF16), jax.ShapeDtypeStruct((L, D_FF), BF16),
                   jax.ShapeDtypeStruct((8, D_FF), F32)],
        compiler_params=_params(("arbitrary",)),
    )(dh2b, w_down, up, up, up, cw[0], cw[1], cw[2], conv_b)


def _ffn_bwd_up(dacc, db, conv_w, w_up, h1, ffn_g, dh2, w_out):
    L = h1.shape[0]
    tm = _row_tile(L)
    nt = L // tm
    shard = w_up.shape[2]
    cw = [conv_w[j:j + 1] for j in range(3)]

    def body(da_ref, halo_ref, db_ref, cw0, cw1, cw2, wu_ref, h1_ref, g_ref, dh2_ref, wo_ref,
             dup_ref, dh1_ref, dh1b_ref, dmix_ref, dg_ref):
        i = pl.program_id(0)

        @pl.when(i == 0)
        def _():
            dg_ref[...] = jnp.zeros_like(dg_ref)

        sub = _iota((tm, 1), 0)
        d0 = da_ref[...].astype(F32)
        halo = jnp.where(i < nt - 1, halo_ref[...].astype(F32), 0.0)
        d1 = jnp.where(sub == tm - 1, _pick_row(halo, 0), pltpu.roll(d0, tm - 1, 0))
        d2 = jnp.where(sub == tm - 2, _pick_row(halo, 0),
                       jnp.where(sub == tm - 1, _pick_row(halo, 1), pltpu.roll(d0, tm - 2, 0)))
        da = d0 * cw2[...] + d1 * cw1[...] + d2 * cw0[...]
        da = jnp.where(i * tm + sub >= N_PAD, da, 0.0).astype(BF16)
        dup_ref[:, 0:D_FF] = da
        dbv = db_ref[...]
        dup_ref[:, D_FF:2 * D_FF] = dbv
        dn = jnp.zeros((tm, D_MODEL), F32)
        for j in range(N_CHIPS):
            src = da if j < 2 else dbv
            lo = (j % 2) * shard
            dn = dn + _dot_nt(src[:, lo:lo + shard], wu_ref[j])
        h1 = h1_ref[...]
        r = lax.rsqrt(jnp.mean(h1 * h1, axis=-1, keepdims=True) + EPS)
        yn = h1 * r
        dg_ref[...] = dg_ref[...] + jnp.sum(dn * yn, axis=0, keepdims=True)
        dyn = dn * g_ref[...]
        dh1 = dh2_ref[...] + r * (dyn - yn * jnp.mean(dyn * yn, axis=-1, keepdims=True))
        dh1_ref[...] = dh1
        dh1b = dh1.astype(BF16)
        dh1b_ref[...] = dh1b
        dmix_ref[...] = _dot_nt(dh1b, wo_ref[...]).astype(BF16)

    rows = lambda w: pl.BlockSpec((tm, w), lambda i: (i, 0))
    halo = pl.BlockSpec((8, D_FF), lambda i: (jnp.minimum((i + 1) * (tm // 8), L // 8 - 1), 0))
    return pl.pallas_call(
        body, name="b_ffn_up", grid=(nt,),
        in_specs=[rows(D_FF), halo, rows(D_FF), _full((1, D_FF)), _full((1, D_FF)), _full((1, D_FF)),
                  _full((N_CHIPS, D_MODEL, shard)), rows(D_MODEL), _full((1, D_MODEL)), rows(D_MODEL),
                  _full((D_MODEL, D_MODEL))],
        out_specs=[rows(2 * D_FF), rows(D_MODEL), rows(D_MODEL), rows(D_MODEL), _full((1, D_MODEL))],
        out_shape=[jax.ShapeDtypeStruct((L, 2 * D_FF), BF16), jax.ShapeDtypeStruct((L, D_MODEL), F32),
                   jax.ShapeDtypeStruct((L, D_MODEL), BF16), jax.ShapeDtypeStruct((L, D_MODEL), BF16),
                   jax.ShapeDtypeStruct((1, D_MODEL), F32)],
        compiler_params=_params(("arbitrary",)),
    )(dacc, dacc, db, cw[0], cw[1], cw[2], w_up, h1, ffn_g, dh2, w_out)


def _wgrad(a, b, name, tn=None):
    L, K = a.shape
    N = b.shape[1]
    tn = N if tn is None else tn
    tl = _row_tile(L)

    def body(a_ref, b_ref, o_ref):
        @pl.when(pl.program_id(1) == 0)
        def _():
            o_ref[...] = jnp.zeros_like(o_ref)

        o_ref[0] = o_ref[0] + _dot_tn(a_ref[...], b_ref[...])

    return pl.pallas_call(
        body, name=name, grid=(N // tn, L // tl),
        in_specs=[pl.BlockSpec((tl, K), lambda n, l: (l, 0)), pl.BlockSpec((tl, tn), lambda n, l: (l, n))],
        out_specs=pl.BlockSpec((1, K, tn), lambda n, l: (n, 0, 0)),
        out_shape=jax.ShapeDtypeStruct((N // tn, K, tn), F32),
        compiler_params=_params(("parallel", "arbitrary")),
    )(a, b)


def _retention_bwd(dmix, o, proj, cos_t, sin_t, ret_g, states):
    L = proj.shape[0]
    nblk = L // BLK
    dmat, wq_t, wk_t, g_blk = _decay_tables()

    def body(dm_ref, o_ref, q_ref, k_ref, v_ref, gate_ref, cos_ref, sin_ref, d_ref, wq_ref, wk_ref, rg_ref, rs_ref,
             dp_ref, drg_ref, gstate):
        i = pl.program_id(0)

        @pl.when(i == 0)
        def _():
            gstate[...] = jnp.zeros_like(gstate)
            drg_ref[...] = jnp.zeros_like(drg_ref)

        rot, rot_t = _rot_fns(cos_ref[...], sin_ref[...])
        lane = _iota((BLK, BLK), 1)
        sub = _iota((BLK, BLK), 0)
        scale = HEAD_LANES ** -0.5
        for p in range(2):
            qr = rot(q_ref[:, p * BLK:(p + 1) * BLK].astype(F32))
            kr = rot(k_ref[:, p * BLK:(p + 1) * BLK].astype(F32)) * scale
            kr_b = kr.astype(BF16)
            qw = (qr * wq_ref[p]).astype(BF16)
            kw = (kr * wk_ref[p]).astype(BF16)
            dqr = jnp.zeros((BLK, BLK), F32)
            dkr = jnp.zeros((BLK, BLK), F32)
            for e in range(2):
                h = 2 * p + e
                cols = slice(h * BLK, (h + 1) * BLK)
                head_lanes = (lane >> 6) == e
                o = o_ref[:, cols]
                rn = lax.rsqrt(jnp.mean(o * o, axis=-1, keepdims=True) + EPS)
                y = o * rn
                gate = gate_ref[:, cols].astype(F32)
                sg = jax.nn.sigmoid(gate)
                dm = dm_ref[:, cols].astype(F32)
                rgain = rg_ref[:, cols]
                drg_ref[:, cols] = drg_ref[:, cols] + jnp.sum(dm * y * (gate * sg), axis=0, keepdims=True)
                dp_ref[:, 1024 + h * BLK:1024 + (h + 1) * BLK] = (
                    dm * y * rgain * (sg * (1.0 + gate * (1.0 - sg)))).astype(BF16)
                dy = dm * rgain * (gate * sg)
                do = (rn * (dy - y * jnp.mean(dy * y, axis=-1, keepdims=True))).astype(BF16)
                vh = v_ref[:, cols]
                qm = jnp.where(head_lanes, qr, 0.0).astype(BF16)
                dmh = d_ref[h]
                s = (_dot_nt(qm, kr_b) * dmh).astype(BF16)
                ds = (_dot_nt(do, vh) * dmh).astype(BF16)
                st = rs_ref[0, h].astype(BF16)
                gs = gstate[h]
                gs_b = gs.astype(BF16)
                dqr = dqr + jnp.where(head_lanes, _dot(ds, kr_b), 0.0) + _dot_nt(do, st) * wq_ref[p]
                dkr = dkr + _dot_tn(ds, qm) + _dot_nt(vh, gs_b) * wk_ref[p]
                dp_ref[:, 512 + h * BLK:512 + (h + 1) * BLK] = (_dot_tn(s, do) + _dot(kw, gs_b)).astype(BF16)
                dr = jnp.where((sub >> 6) == e, _dot_tn(qw, do), 0.0)
                gstate[h] = dr + g_blk[h] * gs
            dp_ref[:, p * BLK:(p + 1) * BLK] = rot_t(dqr).astype(BF16)
            dp_ref[:, 256 + p * BLK:256 + (p + 1) * BLK] = (rot_t(dkr) * scale).astype(BF16)

    row = lambda c: (lambda i: (nblk - 1 - i, c))
    return pl.pallas_call(
        body, name="b_retention", grid=(nblk,),
        in_specs=[pl.BlockSpec((BLK, 512), row(0)), pl.BlockSpec((BLK, 512), row(0)),
                  pl.BlockSpec((BLK, 256), row(0)), pl.BlockSpec((BLK, 256), row(1)),
                  pl.BlockSpec((BLK, 512), row(1)), pl.BlockSpec((BLK, 512), row(2)),
                  pl.BlockSpec((BLK, BLK), row(0)), pl.BlockSpec((BLK, BLK), row(0)),
                  _full((RET_HEADS, BLK, BLK)), _full((2, BLK, BLK)), _full((2, BLK, BLK)), _full((1, 512)),
                  pl.BlockSpec((1, RET_HEADS, BLK, BLK), lambda i: (nblk - 1 - i, 0, 0, 0))],
        out_specs=[pl.BlockSpec((BLK, RET_W), row(0)), _full((1, 512))],
        out_shape=[jax.ShapeDtypeStruct((L, RET_W), BF16), jax.ShapeDtypeStruct((1, 512), F32)],
        scratch_shapes=[pltpu.VMEM((RET_HEADS, BLK, BLK), F32)],
        compiler_params=_params(("arbitrary",)),
    )(dmix, o, proj, proj, proj, proj, cos_t, sin_t, dmat, wq_t, wk_t, ret_g, states)


def _fox_delta(dmix, o_f):
    L = o_f.shape[0]
    nblk = L // BLK

    def body(do_ref, o_ref, d_ref):
        prod = do_ref[...].astype(F32) * o_ref[...].astype(F32)
        sel = ((_iota((8, 512), 1) >> 6) == _iota((8, 512), 0)).astype(BF16)
        hi = prod.astype(BF16)
        lo = (prod - hi.astype(F32)).astype(BF16)
        d_ref[0] = _dot_nt(sel, hi) + _dot_nt(sel, lo)

    return pl.pallas_call(
        body, name="b_foxdelta", grid=(nblk,),
        in_specs=[pl.BlockSpec((BLK, 512), lambda i: (i, 1)), pl.BlockSpec((BLK, 512), lambda i: (i, 0))],
        out_specs=pl.BlockSpec((1, 8, BLK), lambda i: (i, 0, 0)),
        out_shape=jax.ShapeDtypeStruct((nblk, 8, BLK), F32),
        compiler_params=_params(("parallel",)),
    )(dmix, o_f)


def _fox_bwd(proj, dmix, c, ctb, lse, delta):
    L = proj.shape[0]
    nblk = L // BLK

    def body(qkv_ref, do_ref, c_ref, ct_ref, lse_ref, dl_ref, dp_ref, dc_ref, dcq_ref, dq_acc):
        p = pl.program_id(0)

        @pl.when(p == 0)
        def _():
            dc_ref[...] = jnp.zeros_like(dc_ref)
            dcq_ref[...] = jnp.zeros_like(dcq_ref)

        lane, sub, lane_e = _fox_masks(p)
        sub8 = _iota((8, BLK), 0)
        scale = HEAD_LANES ** -0.5

        def zero(j, carry):
            dq_acc[j] = jnp.zeros((BLK, BLK), F32)
            return carry

        lax.fori_loop(0, nblk, zero, 0)

        def k_loop(kj, carry):
            koff = pl.multiple_of(kj * BLK, BLK)
            kt = qkv_ref[pl.ds(koff, BLK), BLK:2 * BLK]
            vt = qkv_ref[pl.ds(koff, BLK), 2 * BLK:3 * BLK]
            km = [jnp.where(lane_e[e], kt, jnp.zeros_like(kt)) for e in range(2)]
            ct = c_ref[pl.ds(koff, BLK), :]
            cs = [jnp.broadcast_to(jnp.sum(jnp.where(lane == 2 * p + e, ct, 0.0), axis=1, keepdims=True), (BLK, BLK))
                  for e in range(2)]
            kpos = kj * BLK + sub

            def q_loop(qi, st):
                dk, dv, dcs0, dcs1 = st
                dcs = [dcs0, dcs1]
                qoff = pl.multiple_of(qi * BLK, BLK)
                qs = qkv_ref[pl.ds(qoff, BLK), 0:BLK].astype(F32) * scale
                dot_ = do_ref[pl.ds(qoff, BLK), :]
                ok = (kpos <= qi * BLK + lane) & (kpos >= N_PAD)
                ctile, ltile, dtile = ct_ref[qi], lse_ref[qi], dl_ref[qi]
                dq = jnp.zeros((BLK, BLK), F32)
                dcq = jnp.zeros((8, BLK), F32)
                for e in range(2):
                    h = 2 * p + e
                    qm = jnp.where(lane_e[e], qs, 0.0).astype(BF16)
                    dom = jnp.where(lane_e[e], dot_, jnp.zeros_like(dot_))
                    s = _dot_nt(kt, qm) + _pick_row(ctile, h) - cs[e]
                    s = jnp.where(ok, s, NEG)
                    pr = jnp.exp(s - _pick_row(ltile, h))
                    pr_b = pr.astype(BF16)
                    dv = dv + _dot(pr_b, dom)
                    dsv = pr * (_dot_nt(vt, dom) - _pick_row(dtile, h))
                    ds_b = dsv.astype(BF16)
                    dk = dk + _dot(ds_b, qm)
                    dq = dq + _dot_tn(ds_b, km[e])
                    dcs[e] = dcs[e] + dsv
                    dcq = dcq + jnp.where(sub8 == h, jnp.sum(dsv, axis=0, keepdims=True), 0.0)
                dq_acc[qi] = dq_acc[qi] + dq
                dcq_ref[qi] = dcq_ref[qi] + dcq
                return dk, dv, dcs[0], dcs[1]

            z = jnp.zeros((BLK, BLK), F32)
            dk, dv, dcs0, dcs1 = lax.fori_loop(kj, nblk, q_loop, (z, z, z, z))
            dp_ref[pl.ds(koff, BLK), BLK:2 * BLK] = dk.astype(BF16)
            dp_ref[pl.ds(koff, BLK), 2 * BLK:3 * BLK] = dv.astype(BF16)
            upd = jnp.zeros((BLK, BLK), F32)
            for e, d in enumerate((dcs0, dcs1)):
                upd = upd + jnp.where(lane == 2 * p + e, -jnp.sum(d, axis=1, keepdims=True), 0.0)
            dc_ref[pl.ds(koff, BLK), :] = dc_ref[pl.ds(koff, BLK), :] + upd
            return carry

        lax.fori_loop(0, nblk, k_loop, 0)

        def flush(j, carry):
            off = pl.multiple_of(j * BLK, BLK)
            dp_ref[pl.ds(off, BLK), 0:BLK] = (dq_acc[j] * scale).astype(BF16)
            return carry

        lax.fori_loop(0, nblk, flush, 0)

    stat = _full((nblk, 8, BLK))
    return pl.pallas_call(
        body, name="b_fox", grid=(FOX_HEADS // 2,),
        in_specs=[pl.BlockSpec((L, 384), lambda p: (0, RET_W // 384 + p)), pl.BlockSpec((L, BLK), lambda p: (0, 4 + p)),
                  _full((L, BLK)), stat, stat, stat],
        out_specs=[pl.BlockSpec((L, 384), lambda p: (0, p)), _full((L, BLK)), stat],
        out_shape=[jax.ShapeDtypeStruct((L, FOX_W), BF16), jax.ShapeDtypeStruct((L, BLK), F32),
                   jax.ShapeDtypeStruct((nblk, 8, BLK), F32)],
        scratch_shapes=[pltpu.VMEM((nblk, BLK, BLK), F32)],
        compiler_params=_params(("arbitrary",)),
    )(proj, dmix, c, ctb, lse, delta)


def _fox_post(dc, dcq, ff, fb):
    L = dc.shape[0]
    nblk = L // BLK

    def body(dc_ref, dcq_ref, ff_ref, b_ref, dff_ref, dffb_ref, dfb_ref, carry):
        i = pl.program_id(0)

        @pl.when(i == 0)
        def _():
            carry[...] = jnp.zeros_like(carry)
            dfb_ref[...] = jnp.zeros_like(dfb_ref)

        d = dc_ref[...] + jnp.concatenate([dcq_ref[0], jnp.zeros((BLK - 8, BLK), F32)], axis=0).T
        tri = (_iota((BLK, BLK), 0) <= _iota((BLK, BLK), 1)).astype(BF16)
        hi, mid, lo = _split3(d)
        dlf = _dot(tri, hi) + _dot(tri, mid) + _dot(tri, lo) + carry[...]
        carry[...] = carry[...] + jnp.sum(d, axis=0, keepdims=True)
        z = ff_ref[...] + b_ref[...]
        dff = jnp.where(_iota((BLK, BLK), 1) < FOX_HEADS, dlf * jax.nn.sigmoid(-z), 0.0)
        dff_ref[...] = dff
        dffb_ref[...] = dff.astype(BF16)
        dfb_ref[...] = dfb_ref[...] + jnp.sum(dff, axis=0, keepdims=True)

    rev = lambda i: (nblk - 1 - i, 0)
    return pl.pallas_call(
        body, name="b_foxpost", grid=(nblk,),
        in_specs=[pl.BlockSpec((BLK, BLK), rev), pl.BlockSpec((1, 8, BLK), lambda i: (nblk - 1 - i, 0, 0)),
                  pl.BlockSpec((BLK, BLK), rev), _full((1, BLK))],
        out_specs=[pl.BlockSpec((BLK, BLK), rev), pl.BlockSpec((BLK, BLK), rev), _full((1, BLK))],
        out_shape=[jax.ShapeDtypeStruct((L, BLK), F32), jax.ShapeDtypeStruct((L, BLK), BF16),
                   jax.ShapeDtypeStruct((1, BLK), F32)],
        scratch_shapes=[pltpu.VMEM((1, BLK), F32)],
        compiler_params=_params(("arbitrary",)),
    )(dc, dcq, ff, fb)


def _inproj_bwd(dpr, dpf, dffb, w_main, w_ff, h0, g, dh1):
    L = h0.shape[0]
    tm = _row_tile(L)

    def body(dpr_ref, dpf_ref, dff_ref, wm_ref, wf_ref, h_ref, g_ref, dh1_ref, dh0_ref, dg_ref):
        @pl.when(pl.program_id(0) == 0)
        def _():
            dg_ref[...] = jnp.zeros_like(dg_ref)

        dn = (_dot_nt(dpr_ref[...], wm_ref[:, 0:RET_W]) + _dot_nt(dpf_ref[...], wm_ref[:, RET_W:MAIN_W])
              + _dot_nt(dff_ref[...], wf_ref[...]))
        h = h_ref[...]
        r = lax.rsqrt(jnp.mean(h * h, axis=-1, keepdims=True) + EPS)
        yn = h * r
        dg_ref[...] = dg_ref[...] + jnp.sum(dn * yn, axis=0, keepdims=True)
        dyn = dn * g_ref[...]
        dh0_ref[...] = dh1_ref[...] + r * (dyn - yn * jnp.mean(dyn * yn, axis=-1, keepdims=True))

    rows = lambda w: pl.BlockSpec((tm, w), lambda i: (i, 0))
    return pl.pallas_call(
        body, name="b_inproj", grid=(L // tm,),
        in_specs=[rows(RET_W), rows(FOX_W), rows(BLK), _full((D_MODEL, MAIN_W)), _full((D_MODEL, BLK)),
                  rows(D_MODEL), _full((1, D_MODEL)), rows(D_MODEL)],
        out_specs=[rows(D_MODEL), _full((1, D_MODEL))],
        out_shape=[jax.ShapeDtypeStruct((L, D_MODEL), F32), jax.ShapeDtypeStruct((1, D_MODEL), F32)],
        compiler_params=_params(("arbitrary",)),
    )(dpr, dpf, dffb, w_main, w_ff, h0, g, dh1)


def _local_step(x, target, meta, attn_g, w_main, w_ff, fox_b, ret_g, w_out, ffn_g, w_up, conv_w, conv_b, w_down, final_g):
    S = x.shape[0]
    L = S + PREFIX
    h0 = jnp.concatenate([jnp.zeros((N_PAD, D_MODEL), F32), meta, x], axis=0)
    tgt = jnp.concatenate([jnp.zeros((PREFIX, D_MODEL), F32), target], axis=0)
    fb = jnp.pad(fox_b, ((0, 0), (0, BLK - FOX_HEADS)))
    cos_t, sin_t = _rotary_tables(L)

    n1, proj, ff = _rms_inproj(h0, attn_g, w_main, w_ff)
    c, ctb = _fox_prep(ff, fb)
    mix_r, o_ret, states = _retention_fwd(proj, cos_t, sin_t, ret_g)
    o_f, lse = _fox_fwd(proj, c, ctb)
    h1, n2, up = _outproj_up(mix_r, o_f, h0, w_out, ffn_g, w_up)
    g_act, dh2, dh2b, d_final_g, loss = _ffn_down_loss(up, conv_w, conv_b, w_down, h1, final_g, tgt)

    dacc, db, dconv = _ffn_bwd_gate(dh2b, w_down, up, conv_w, conv_b)
    dup, dh1, dh1b, dmix, d_ffn_g = _ffn_bwd_up(dacc, db, conv_w, w_up, h1, ffn_g, dh2, w_out)
    d_w_down = _wgrad(g_act, dh2b, "wgrad_down", tn=None)[0]
    d_w_up = _wgrad(n2, dup, "wgrad_up", tn=w_up.shape[2])
    d_w_out = jnp.concatenate([_wgrad(mix_r, dh1b, "wgrad_out_r")[0], _wgrad(o_f, dh1b, "wgrad_out_f")[0]], axis=0)

    dpr, d_ret_g = _retention_bwd(dmix, o_ret, proj, cos_t, sin_t, ret_g, states)
    delta = _fox_delta(dmix, o_f)
    dpf, dc, dcq = _fox_bwd(proj, dmix, c, ctb, lse, delta)
    dff, dffb, d_fox_b = _fox_post(dc, dcq, ff, fb)
    dh0, d_attn_g = _inproj_bwd(dpr, dpf, dffb, w_main, w_ff, h0, attn_g, dh1)
    d_w_main = jnp.concatenate([_wgrad(n1, dpr, "wgrad_in_r")[0], _wgrad(n1, dpf, "wgrad_in_f")[0]], axis=1)
    d_w_ff = _wgrad(n1, dffb, "wgrad_in_ff")[0]

    return dict(
        loss=loss[0, 0], dx=dh0[PREFIX:], dmeta=dh0[N_PAD:PREFIX], attn_g=d_attn_g, w_main=d_w_main,
        w_ff=d_w_ff[:, :FOX_HEADS], fox_b=d_fox_b[:, :FOX_HEADS], ret_g=d_ret_g, w_out=d_w_out, ffn_g=d_ffn_g,
        w_up=d_w_up, conv_w=dconv[0:3], conv_b=dconv[3:4], w_down=d_w_down, final_g=d_final_g)


_ANY = pl.BlockSpec(memory_space=pl.ANY)


def _place():
    return lax.axis_index("x"), lax.axis_index("y"), lax.axis_index("c")


def _other_chips(x, y):
    return [(1 - x, y), (x, 1 - y), (1 - x, 1 - y)]


def _chip_allgather(arrays):
    n = len(arrays)

    def body(*refs):
        ins, outs = refs[:n], refs[n:2 * n]
        send, recv, loc = refs[2 * n:]
        x, y, c = _place()
        mine = 2 * x + y
        peers = _other_chips(x, y)

        def remote(a, k, slot):
            return pltpu.make_async_remote_copy(
                src_ref=ins[a], dst_ref=outs[a].at[slot], send_sem=send.at[3 * a + k], recv_sem=recv.at[3 * a + k],
                device_id=(peers[k][0], peers[k][1], c), device_id_type=MESH)

        local = [pltpu.make_async_copy(ins[a], outs[a].at[mine], loc.at[a]) for a in range(n)]
        sends = [remote(a, k, mine) for a in range(n) for k in range(3)]
        for cp in local + sends:
            cp.start()
        for a in range(n):
            for k in range(3):
                remote(a, k, 2 * peers[k][0] + peers[k][1]).wait_recv()
        for cp in sends:
            cp.wait_send()
        for cp in local:
            cp.wait()

    return pl.pallas_call(
        body, name="ag_weights", in_specs=[_ANY] * n, out_specs=[_ANY] * n,
        out_shape=[jax.ShapeDtypeStruct((N_CHIPS,) + a.shape, a.dtype) for a in arrays],
        scratch_shapes=[pltpu.SemaphoreType.DMA((3 * n,)), pltpu.SemaphoreType.DMA((3 * n,)),
                        pltpu.SemaphoreType.DMA((n,))],
    )(*arrays)


def _sibling_exchange(grads, small):
    n = len(grads)

    def body(*refs):
        ins, small_in = refs[:n], refs[n]
        outs, small_out = refs[n + 1:2 * n + 1], refs[2 * n + 1]
        send, recv, s_send, s_recv, loc = refs[2 * n + 2:]
        x, y, c = _place()
        me = 4 * x + 2 * y + c

        def half_copy(a, which):
            half = ins[a].shape[1] // 2
            return pltpu.make_async_remote_copy(
                src_ref=ins[a].at[pl.ds(0, N_CHIPS), pl.ds(which * half, half)], dst_ref=outs[a],
                send_sem=send.at[a], recv_sem=recv.at[a], device_id=(x, y, 1 - c), device_id_type=MESH)

        def peer_of(r):
            return tuple(1 - v if (r >> b) & 1 else v for v, b in ((x, 2), (y, 1), (c, 0)))

        def small_copy(r, slot):
            return pltpu.make_async_remote_copy(
                src_ref=small_in, dst_ref=small_out.at[slot], send_sem=s_send.at[r - 1], recv_sem=s_recv.at[r - 1],
                device_id=peer_of(r), device_id_type=MESH)

        local = pltpu.make_async_copy(small_in, small_out.at[me], loc.at[0])
        sends = [half_copy(a, 1 - c) for a in range(n)] + [small_copy(r, me) for r in range(1, N_DEV)]
        local.start()
        for cp in sends:
            cp.start()
        for r in range(1, N_DEV):
            px, py, pc = peer_of(r)
            small_copy(r, 4 * px + 2 * py + pc).wait_recv()
        for a in range(n):
            half_copy(a, c).wait_recv()
        for cp in sends:
            cp.wait_send()
        local.wait()

    rows = small.shape[0]
    return pl.pallas_call(
        body, name="rs_sibling", in_specs=[_ANY] * (n + 1), out_specs=[_ANY] * (n + 1),
        out_shape=[jax.ShapeDtypeStruct((N_CHIPS, g.shape[1] // 2, g.shape[2]), g.dtype) for g in grads]
        + [jax.ShapeDtypeStruct((N_DEV, rows, small.shape[1]), small.dtype)],
        scratch_shapes=[pltpu.SemaphoreType.DMA((n,)), pltpu.SemaphoreType.DMA((n,)),
                        pltpu.SemaphoreType.DMA((N_DEV - 1,)), pltpu.SemaphoreType.DMA((N_DEV - 1,)),
                        pltpu.SemaphoreType.DMA((1,))],
    )(*grads, small)


def _chip_reduce_scatter(parts):
    n = len(parts)

    def body(*refs):
        ins, outs = refs[:n], refs[n:2 * n]
        send, recv, loc = refs[2 * n:]
        x, y, c = _place()
        mine = 2 * x + y
        peers = _other_chips(x, y)

        def remote(a, k, src_slot, dst_slot):
            return pltpu.make_async_remote_copy(
                src_ref=ins[a].at[src_slot], dst_ref=outs[a].at[dst_slot], send_sem=send.at[3 * a + k],
                recv_sem=recv.at[3 * a + k], device_id=(peers[k][0], peers[k][1], c), device_id_type=MESH)

        local = [pltpu.make_async_copy(ins[a].at[mine], outs[a].at[mine], loc.at[a]) for a in range(n)]
        sends = [remote(a, k, 2 * peers[k][0] + peers[k][1], mine) for a in range(n) for k in range(3)]
        for cp in local + sends:
            cp.start()
        for a in range(n):
            for k in range(3):
                theirs = 2 * peers[k][0] + peers[k][1]
                remote(a, k, theirs, theirs).wait_recv()
        for cp in sends:
            cp.wait_send()
        for cp in local:
            cp.wait()

    return pl.pallas_call(
        body, name="rs_chips", in_specs=[_ANY] * n, out_specs=[_ANY] * n,
        out_shape=[jax.ShapeDtypeStruct(p.shape, p.dtype) for p in parts],
        scratch_shapes=[pltpu.SemaphoreType.DMA((3 * n,)), pltpu.SemaphoreType.DMA((3 * n,)),
                        pltpu.SemaphoreType.DMA((n,))],
    )(*parts)


def _sibling_allgather(halves):
    n = len(halves)

    def body(*refs):
        ins, outs = refs[:n], refs[n:2 * n]
        send, recv, loc = refs[2 * n:]
        x, y, c = _place()

        def rows(a, which):
            half = ins[a].shape[0]
            return outs[a].at[pl.ds(which * half, half)]

        def remote(a, which):
            return pltpu.make_async_remote_copy(
                src_ref=ins[a], dst_ref=rows(a, which), send_sem=send.at[a], recv_sem=recv.at[a],
                device_id=(x, y, 1 - c), device_id_type=MESH)

        local = [pltpu.make_async_copy(ins[a], rows(a, c), loc.at[a]) for a in range(n)]
        sends = [remote(a, c) for a in range(n)]
        for cp in local + sends:
            cp.start()
        for a in range(n):
            remote(a, 1 - c).wait_recv()
        for cp in sends:
            cp.wait_send()
        for cp in local:
            cp.wait()

    return pl.pallas_call(
        body, name="ag_sibling", in_specs=[_ANY] * n, out_specs=[_ANY] * n,
        out_shape=[jax.ShapeDtypeStruct((2 * h.shape[0], h.shape[1]), h.dtype) for h in halves],
        scratch_shapes=[pltpu.SemaphoreType.DMA((n,)), pltpu.SemaphoreType.DMA((n,)), pltpu.SemaphoreType.DMA((n,))],
    )(*halves)


def _pair_add(full, recv, core, name):
    _, R, C = full.shape
    half = R // 2

    def body(core_ref, a_ref, b_ref, o_ref):
        o_ref[...] = a_ref[...] + b_ref[...]

    return pl.pallas_call(
        body, name=name,
        grid_spec=pltpu.PrefetchScalarGridSpec(
            num_scalar_prefetch=1, grid=(N_CHIPS,),
            in_specs=[pl.BlockSpec((1, half, C), lambda j, core_ref: (j, core_ref[0], 0)),
                      pl.BlockSpec((1, half, C), lambda j, core_ref: (j, 0, 0))],
            out_specs=pl.BlockSpec((1, half, C), lambda j, core_ref: (j, 0, 0))),
        out_shape=jax.ShapeDtypeStruct((N_CHIPS, half, C), full.dtype),
        compiler_params=_params(("parallel",)),
    )(core, full, recv)


def _sum_slots(q, name, tiles=4):
    n, R, C = q.shape
    tr = R // tiles

    def body(q_ref, o_ref):
        acc = q_ref[0]
        for j in range(1, n):
            acc = acc + q_ref[j]
        o_ref[...] = acc

    return pl.pallas_call(
        body, name=name, grid=(tiles,),
        in_specs=[pl.BlockSpec((n, tr, C), lambda i: (0, i, 0))],
        out_specs=pl.BlockSpec((tr, C), lambda i: (i, 0)),
        out_shape=jax.ShapeDtypeStruct((R, C), q.dtype),
        compiler_params=_params(("parallel",)),
    )(q)


def _adamw(w, g, m, v, name, tiles=4):
    R, C = w.shape
    tr = R // tiles

    def body(w_ref, g_ref, m_ref, v_ref, d_ref, m2_ref, v2_ref):
        g_ = g_ref[...]
        m2 = ADAM_B1 * m_ref[...] + (1.0 - ADAM_B1) * g_
        v2 = ADAM_B2 * v_ref[...] + (1.0 - ADAM_B2) * (g_ * g_)
        m_hat = m2 / (1.0 - ADAM_B1 ** ADAM_STEP)
        v_hat = v2 / (1.0 - ADAM_B2 ** ADAM_STEP)
        d_ref[...] = -ADAM_LR * (m_hat / (jnp.sqrt(v_hat) + ADAM_EPS) + ADAM_WD * w_ref[...])
        m2_ref[...] = m2
        v2_ref[...] = v2

    spec = pl.BlockSpec((tr, C), lambda i: (i, 0))
    return pl.pallas_call(
        body, name=name, grid=(tiles,), in_specs=[spec] * 4, out_specs=[spec] * 3,
        out_shape=[jax.ShapeDtypeStruct((R, C), F32)] * 3,
        compiler_params=_params(("parallel",)),
    )(w, g, m, v)


def _pack_rows(pieces, rows):
    flat = jnp.concatenate([jnp.pad(p.reshape(-1).astype(F32), (0, (-p.size) % D_MODEL)) for p in pieces])
    return jnp.pad(flat, (0, rows * D_MODEL - flat.size)).reshape(rows, D_MODEL)


def _unpack_rows(pack, shapes):
    flat = pack.reshape(-1)
    out, off = [], 0
    for shp in shapes:
        size = int(np.prod(shp))
        out.append(flat[off:off + size].reshape(shp))
        off += size + (-size) % D_MODEL
    return out


def _kernel_order(w):
    parts = [w[:, 0:RET_W]]
    for p in range(FOX_HEADS // 2):
        parts += [w[:, RET_W + part * 512 + p * BLK:RET_W + part * 512 + (p + 1) * BLK] for part in range(3)]
    return jnp.concatenate(parts, axis=1)


def _reference_order(g_main, g_ff):
    parts = [g_main[:, 0:RET_W]]
    for part in range(3):
        parts += [g_main[:, RET_W + 384 * p + part * BLK:RET_W + 384 * p + (part + 1) * BLK] for p in range(FOX_HEADS // 2)]
    return jnp.concatenate(parts + [g_ff], axis=1)


def kernel(x, meta_tokens, attn_norm_g, w_in, fox_forget_b, ret_norm_g, w_out, ffn_norm_g, w_up, conv_w, conv_b, w_down, final_norm_g, loss_target, m_meta_tokens, m_attn_norm_g, m_w_in, m_fox_forget_b, m_ret_norm_g, m_w_out, m_ffn_norm_g, m_w_up, m_conv_w, m_conv_b, m_w_down, m_final_norm_g, v_meta_tokens, v_attn_norm_g, v_w_in, v_fox_forget_b, v_ret_norm_g, v_w_out, v_ffn_norm_g, v_w_up, v_conv_w, v_conv_b, v_w_down, v_final_norm_g):
    chip = 2 * lax.axis_index("x") + lax.axis_index("y")
    core = lax.axis_index("c")
    meta_w, conv_sw = meta_tokens.shape[1], conv_w.shape[2]

    small_w = _pack_rows([meta_tokens, conv_w[0]], 8)
    g_in, g_out, g_up, g_down, g_small = _chip_allgather(
        [w_in[0].astype(BF16), w_out[0].astype(BF16), w_up[0].astype(BF16), w_down[0].astype(BF16), small_w])
    w_in_full = g_in.transpose(1, 0, 2).reshape(D_MODEL, IN_WIDTH)
    w_main = _kernel_order(w_in_full)
    w_ff = jnp.pad(w_in_full[:, MAIN_W:], ((0, 0), (0, BLK - FOX_HEADS)))
    small_parts = [_unpack_rows(g_small[j], [meta_tokens.shape, conv_w.shape[1:]]) for j in range(N_CHIPS)]
    meta_full = jnp.concatenate([sp[0] for sp in small_parts], axis=1)
    conv_w_full = jnp.concatenate([sp[1] for sp in small_parts], axis=1)

    out = _local_step(x[0], loss_target[0], meta_full, attn_norm_g, w_main, w_ff, fox_forget_b, ret_norm_g,
                      g_out.reshape(D_MODEL, D_MODEL), ffn_norm_g, g_up, conv_w_full, conv_b,
                      g_down.reshape(D_FF, D_MODEL), final_norm_g[None])

    big = [_reference_order(out["w_main"], out["w_ff"]).reshape(D_MODEL, N_CHIPS, -1).transpose(1, 0, 2),
           out["w_out"].reshape(N_CHIPS, -1, D_MODEL), out["w_up"], out["w_down"].reshape(N_CHIPS, -1, D_MODEL)]
    small_shapes = [(1, D_MODEL), (1, D_MODEL), (1, D_MODEL), (1, 512 + FOX_HEADS + 1), (1, D_FF), (N_META, D_MODEL), (3, D_FF)]
    small = _pack_rows([out["attn_g"], out["ffn_g"], out["final_g"],
                        jnp.concatenate([out["ret_g"], out["fox_b"], out["loss"].reshape(1, 1)], axis=1),
                        out["conv_b"], out["dmeta"], out["conv_w"]], 32)
    *from_sibling, small_all = _sibling_exchange(big, small)
    core_idx = core.reshape(1).astype(jnp.int32)
    names = ("in", "out", "up", "down")
    chip_sums = [_pair_add(g, r, core_idx, "pair_add_" + nm) for g, r, nm in zip(big, from_sibling, names)]
    from_chips = _chip_reduce_scatter(chip_sums)
    halves = [_sum_slots(q, "sum_chips_" + nm) for q, nm in zip(from_chips, names)]
    grad_in, grad_out, grad_up, grad_down = _sibling_allgather(halves)
    s_attn, s_ffn, s_final, s_misc, s_conv_b, s_meta, s_conv_w = _unpack_rows(
        _sum_slots(small_all, "sum_small", tiles=1), small_shapes)
    loss = s_misc[0, 512 + FOX_HEADS]
    small_grads = [lax.dynamic_slice_in_dim(s_meta, chip * meta_w, meta_w, axis=1), s_attn, s_misc[:, 512:512 + FOX_HEADS],
                   s_misc[:, :512], s_ffn, lax.dynamic_slice_in_dim(s_conv_w, chip * conv_sw, conv_sw, axis=1)[None],
                   s_conv_b, s_final[0]]

    big_w = [(w_in, m_w_in, v_w_in, grad_in, "adamw_in"), (w_out, m_w_out, v_w_out, grad_out, "adamw_out"),
             (w_up, m_w_up, v_w_up, grad_up, "adamw_up"), (w_down, m_w_down, v_w_down, grad_down, "adamw_down")]
    big_res = [[g[None]] + [r[None] for r in _adamw(w[0], g, m[0], v[0], nm)] for w, m, v, g, nm in big_w]
    small_w_list = [meta_tokens, attn_norm_g, fox_forget_b, ret_norm_g, ffn_norm_g, conv_w, conv_b, final_norm_g]
    small_m = [m_meta_tokens, m_attn_norm_g, m_fox_forget_b, m_ret_norm_g, m_ffn_norm_g, m_conv_w, m_conv_b, m_final_norm_g]
    small_v = [v_meta_tokens, v_attn_norm_g, v_fox_forget_b, v_ret_norm_g, v_ffn_norm_g, v_conv_w, v_conv_b, v_final_norm_g]
    shapes = [a.shape for a in small_w_list]
    packs = [_pack_rows(lst, 16) for lst in (small_w_list, small_grads, small_m, small_v)]
    small_res = [_unpack_rows(r, shapes) for r in _adamw(*packs, "adamw_small", tiles=1)]
    small_grads = [g.reshape(s) for g, s in zip(small_grads, shapes)]

    def ordered(kind):
        sm = small_grads if kind == 0 else small_res[kind - 1]
        bg = [r[kind] for r in big_res]
        return [sm[0], sm[1], bg[0], sm[2], sm[3], bg[1], sm[4], bg[2], sm[5], sm[6], bg[3], sm[7]]

    return (loss, out["dx"][None], *ordered(0), *ordered(1), *ordered(2), *ordered(3))
```

```python
import functools

import numpy as np
import jax
import jax.numpy as jnp
from jax import lax
from jax.experimental import pallas as pl
from jax.experimental.pallas import tpu as pltpu

F32 = jnp.float32
BF16 = jnp.bfloat16

D_MODEL = 1024
N_META = 16
BLK = 128
UNIT = 2 * BLK
CHUNK = 64
N_PAD = BLK - N_META
PREFIX = BLK
RET_HEADS = 4
FOX_HEADS = 8
HEAD_LANES = 64
D_FF = 2816
ROPE_BASE = 10000.0
EPS = 1e-6
NEG = -1e30
RET_W = 1536
FOX_W = 1536
MAIN_W = RET_W + FOX_W
IN_WIDTH = MAIN_W + FOX_HEADS
N_CHIPS = 4
N_DEV = 8

ADAM_LR = 0.001
ADAM_B1 = 0.9
ADAM_B2 = 0.999
ADAM_EPS = 1e-08
ADAM_WD = 0.01
ADAM_STEP = 10

MESH = pl.DeviceIdType.MESH
VMEM_LIMIT_MB = 56

_NT = (((1,), (1,)), ((), ()))
_TN = (((0,), (0,)), ((), ()))


def _dot(a, b):
    return jnp.dot(a, b, preferred_element_type=F32)


def _dot_nt(a, b):
    return lax.dot_general(a, b, _NT, preferred_element_type=F32)


def _dot_tn(a, b):
    return lax.dot_general(a, b, _TN, preferred_element_type=F32)


def _params(dims=None, vmem_mb=VMEM_LIMIT_MB):
    kw = dict(vmem_limit_bytes=vmem_mb << 20)
    if dims is not None:
        kw["dimension_semantics"] = dims
    return pltpu.CompilerParams(**kw)


def _row_tile(n, prefs=(384, 256, 128)):
    for t in prefs:
        if n % t == 0:
            return t
    raise ValueError(f"no row tile for {n}")


def _iota(shape, dim):
    return lax.broadcasted_iota(jnp.int32, shape, dim)


def _pick_row(tile, row):
    sub = _iota(tile.shape, 0)
    return jnp.sum(jnp.where(sub == row, tile, 0.0), axis=0, keepdims=True)


def _split3(x):
    hi = x.astype(BF16)
    r1 = x - hi.astype(F32)
    mid = r1.astype(BF16)
    lo = (r1 - mid.astype(F32)).astype(BF16)
    return hi, mid, lo


def _full(shape):
    nd = len(shape)
    return pl.BlockSpec(shape, lambda *_: (0,) * nd)


def _in_perm():
    cols = list(range(RET_W))
    for p in range(FOX_HEADS // 2):
        for part in range(3):
            start = RET_W + part * 512 + p * BLK
            cols += list(range(start, start + BLK))
    return np.asarray(cols, np.int32)


def _rotary_tables(L):
    half = HEAD_LANES // 2
    inv = 1.0 / (ROPE_BASE ** (jnp.arange(half, dtype=F32) / half))
    ang = jnp.arange(L).astype(F32)[:, None] * inv[None, :]
    cos, sin = jnp.cos(ang), jnp.sin(ang)
    cos_t = jnp.tile(cos, (1, 4))
    sin_t = jnp.tile(jnp.concatenate([-sin, sin], axis=1), (1, 2))
    return cos_t, sin_t


def _decay_tables():
    gam = 1.0 - 2.0 ** (-5.0 - np.arange(RET_HEADS, dtype=np.float64))
    n = np.arange(BLK)
    same_or_past = (n[:, None] // CHUNK) >= (n[None, :] // CHUNK)
    dist = np.abs(n[:, None] - n[None, :])
    dmat = np.stack([np.where(same_or_past, g ** dist, 0.0) for g in gam]).astype(np.float32)
    lane_head = np.arange(BLK) // HEAD_LANES
    wq = np.stack([gam[2 * p + lane_head][None, :] ** (n[:, None] + 1.0) for p in range(2)]).astype(np.float32)
    wk = np.stack([gam[2 * p + lane_head][None, :] ** (BLK - 1.0 - n[:, None]) for p in range(2)]).astype(np.float32)
    g_blk = tuple(float(g ** BLK) for g in gam)
    return jnp.asarray(dmat), jnp.asarray(wq), jnp.asarray(wk), g_blk


def _rms_inproj(h0, g, w_main, w_ff):
    L = h0.shape[0]
    tm = _row_tile(L)

    def body(h_ref, g_ref, wm_ref, wf_ref, n_ref, p_ref, ff_ref):
        h = h_ref[...]
        r = lax.rsqrt(jnp.mean(h * h, axis=-1, keepdims=True) + EPS)
        n = (h * r * g_ref[...]).astype(BF16)
        n_ref[...] = n
        p_ref[...] = _dot(n, wm_ref[...]).astype(BF16)
        ff_ref[...] = _dot(n, wf_ref[...])

    return pl.pallas_call(
        body, name="f_inproj", grid=(L // tm,),
        in_specs=[pl.BlockSpec((tm, D_MODEL), lambda i: (i, 0)), _full((1, D_MODEL)),
                  _full((D_MODEL, MAIN_W)), _full((D_MODEL, BLK))],
        out_specs=[pl.BlockSpec((tm, D_MODEL), lambda i: (i, 0)), pl.BlockSpec((tm, MAIN_W), lambda i: (i, 0)),
                   pl.BlockSpec((tm, BLK), lambda i: (i, 0))],
        out_shape=[jax.ShapeDtypeStruct((L, D_MODEL), BF16), jax.ShapeDtypeStruct((L, MAIN_W), BF16),
                   jax.ShapeDtypeStruct((L, BLK), F32)],
        compiler_params=_params(("parallel",)),
    )(h0, g, w_main, w_ff)


def _fox_prep(ff, fb):
    L = ff.shape[0]
    nblk = L // BLK

    def body(ff_ref, b_ref, c_ref, ct_ref, carry):
        i = pl.program_id(0)

        @pl.when(i == 0)
        def _():
            carry[...] = jnp.zeros_like(carry)

        z = ff_ref[...] + b_ref[...]
        lf = jnp.minimum(z, 0.0) - jnp.log1p(jnp.exp(-jnp.abs(z)))
        lf = jnp.where(_iota((BLK, BLK), 1) < FOX_HEADS, lf, 0.0)
        tri = (_iota((BLK, BLK), 0) >= _iota((BLK, BLK), 1)).astype(BF16)
        hi, mid, lo = _split3(lf)
        cs = _dot(tri, hi) + _dot(tri, mid) + _dot(tri, lo) + carry[...]
        c_ref[...] = cs
        ct_ref[0] = cs.T[0:8, :]
        carry[...] = carry[...] + jnp.sum(lf, axis=0, keepdims=True)

    return pl.pallas_call(
        body, name="f_foxprep", grid=(nblk,),
        in_specs=[pl.BlockSpec((BLK, BLK), lambda i: (i, 0)), _full((1, BLK))],
        out_specs=[pl.BlockSpec((BLK, BLK), lambda i: (i, 0)), pl.BlockSpec((1, 8, BLK), lambda i: (i, 0, 0))],
        out_shape=[jax.ShapeDtypeStruct((L, BLK), F32), jax.ShapeDtypeStruct((nblk, 8, BLK), F32)],
        scratch_shapes=[pltpu.VMEM((1, BLK), F32)],
        compiler_params=_params(("arbitrary",)),
    )(ff, fb)


def _rot_fns(cos, sin):
    lane = _iota((BLK, BLK), 1)
    first = (lane & (HEAD_LANES - 1)) < HEAD_LANES // 2

    def swap(x):
        return jnp.where(first, pltpu.roll(x, BLK - 32, 1), pltpu.roll(x, 32, 1))

    def rot(x):
        return x * cos + swap(x) * sin

    def rot_t(dy):
        return dy * cos + swap(dy * sin)

    return rot, rot_t


def _retention_fwd(proj, cos_t, sin_t, ret_g):
    L = proj.shape[0]
    nblk = L // BLK
    dmat, wq_t, wk_t, g_blk = _decay_tables()

    def body(q_ref, k_ref, v_ref, gate_ref, cos_ref, sin_ref, d_ref, wq_ref, wk_ref, rg_ref,
             mix_ref, o_ref, rs_ref, state):
        i = pl.program_id(0)

        @pl.when(i == 0)
        def _():
            state[...] = jnp.zeros_like(state)

        rot, _ = _rot_fns(cos_ref[...], sin_ref[...])
        lane = _iota((BLK, BLK), 1)
        sub = _iota((BLK, BLK), 0)
        for p in range(2):
            qr = rot(q_ref[:, p * BLK:(p + 1) * BLK].astype(F32))
            kr = rot(k_ref[:, p * BLK:(p + 1) * BLK].astype(F32)) * (HEAD_LANES ** -0.5)
            kr_b = kr.astype(BF16)
            qw = (qr * wq_ref[p]).astype(BF16)
            kw = (kr * wk_ref[p]).astype(BF16)
            for e in range(2):
                h = 2 * p + e
                cols = slice(h * BLK, (h + 1) * BLK)
                qm = jnp.where((lane >> 6) == e, qr, 0.0).astype(BF16)
                s = _dot_nt(qm, kr_b) * d_ref[h]
                vh = v_ref[:, cols]
                st = state[h]
                rs_ref[0, h] = st
                o = _dot(s.astype(BF16), vh) + _dot(qw, st.astype(BF16))
                u = jnp.where((sub >> 6) == e, _dot_tn(kw, vh), 0.0)
                state[h] = g_blk[h] * st + u
                rn = lax.rsqrt(jnp.mean(o * o, axis=-1, keepdims=True) + EPS)
                gate = gate_ref[:, cols].astype(F32)
                o_ref[:, cols] = o
                mix_ref[:, cols] = (o * rn * rg_ref[:, cols] * (gate * jax.nn.sigmoid(gate))).astype(BF16)

    row = lambda c: (lambda i: (i, c))
    return pl.pallas_call(
        body, name="f_retention", grid=(nblk,),
        in_specs=[pl.BlockSpec((BLK, 256), row(0)), pl.BlockSpec((BLK, 256), row(1)),
                  pl.BlockSpec((BLK, 512), row(1)), pl.BlockSpec((BLK, 512), row(2)),
                  pl.BlockSpec((BLK, BLK), row(0)), pl.BlockSpec((BLK, BLK), row(0)),
                  _full((RET_HEADS, BLK, BLK)), _full((2, BLK, BLK)), _full((2, BLK, BLK)), _full((1, 512))],
        out_specs=[pl.BlockSpec((BLK, 512), row(0)), pl.BlockSpec((BLK, 512), row(0)),
                   pl.BlockSpec((1, RET_HEADS, BLK, BLK), lambda i: (i, 0, 0, 0))],
        out_shape=[jax.ShapeDtypeStruct((L, 512), BF16), jax.ShapeDtypeStruct((L, 512), F32),
                   jax.ShapeDtypeStruct((nblk, RET_HEADS, BLK, BLK), F32)],
        scratch_shapes=[pltpu.VMEM((RET_HEADS, BLK, BLK), F32)],
        compiler_params=_params(("arbitrary",)),
    )(proj, proj, proj, proj, cos_t, sin_t, dmat, wq_t, wk_t, ret_g)


def _fox_units(L):
    nblk = L // BLK
    assert L % BLK == 0 and nblk % 2 == 1, "sequence must be one 128-row block plus whole 256-row tiles"
    return nblk, (nblk - 1) // 2


def _fox_tile_masks():
    sub, lane = _iota((BLK, BLK), 0), _iota((BLK, BLK), 1)
    return dict(first=(sub <= lane) & (sub >= N_PAD), valid=_iota((BLK, UNIT), 0) >= N_PAD,
                diag=_iota((UNIT, UNIT), 0) <= _iota((UNIT, UNIT), 1))


def _fox_fwd(proj, c, ctb):
    L = proj.shape[0]
    nblk, nu = _fox_units(L)
    scale = HEAD_LANES ** -0.5

    def body(qkv_ref, c_ref, ct_ref, of_ref, lse_ref, vt, csb):
        p = pl.program_id(0)

        @pl.when(p == 0)
        def _():
            lse_ref[...] = jnp.zeros_like(lse_ref)

        lane = _iota((BLK, BLK), 1)
        sub8 = _iota((8, BLK), 0)
        masks = _fox_tile_masks()

        def pre(j, carry):
            off = pl.multiple_of(j * BLK, BLK)
            vt[j] = qkv_ref[pl.ds(off, BLK), 2 * BLK:3 * BLK].astype(F32).T.astype(BF16)
            ct = c_ref[pl.ds(off, BLK), :]
            for e in range(2):
                col = jnp.sum(jnp.where(lane == 2 * p + e, ct, 0.0), axis=1, keepdims=True)
                csb[e, j] = jnp.broadcast_to(col, (BLK, UNIT))
            return carry

        lax.fori_loop(0, nblk, pre, 0)

        def attend(qblk, nq, n_whole):
            qlen = nq * BLK
            qoff = pl.multiple_of(qblk * BLK, BLK)
            qs = qkv_ref[pl.ds(qoff, qlen), 0:BLK].astype(F32) * scale
            qlane = _iota((qlen, BLK), 1)
            qm = [jnp.where((qlane >> 6) == e, qs, 0.0).astype(BF16) for e in range(2)]
            ct_row = [jnp.concatenate([_pick_row(ct_ref[qblk + a], 2 * p + e) for a in range(nq)], axis=1)
                      for e in range(2)]

            def step(kblk, nk, mask, st):
                koff = pl.multiple_of(kblk * BLK, BLK)
                kt = qkv_ref[pl.ds(koff, nk * BLK), BLK:2 * BLK]
                out = []
                for e in range(2):
                    m, l, acc = st[3 * e:3 * e + 3]
                    s = _dot_nt(kt, qm[e])
                    t = jnp.concatenate([s[b * BLK:(b + 1) * BLK] - csb[e, kblk + b, :, 0:qlen] for b in range(nk)], axis=0)
                    if mask is not None:
                        t = jnp.where(mask, t, NEG)
                    m_new = jnp.maximum(m, jnp.max(t, axis=0, keepdims=True) + ct_row[e])
                    alpha = jnp.exp(m - m_new)
                    pr = jnp.exp(t - (m_new - ct_row[e]))
                    l = alpha * l + jnp.sum(pr, axis=0, keepdims=True)
                    pr_b = pr.astype(BF16)
                    pv = _dot(vt[kblk, e * HEAD_LANES:(e + 1) * HEAD_LANES, :], pr_b[0:BLK])
                    for b in range(1, nk):
                        pv = pv + _dot(vt[kblk + b, e * HEAD_LANES:(e + 1) * HEAD_LANES, :], pr_b[b * BLK:(b + 1) * BLK])
                    out += [m_new, l, alpha * acc + pv]
                return tuple(out)

            st = (jnp.full((1, qlen), NEG, F32), jnp.zeros((1, qlen), F32), jnp.zeros((HEAD_LANES, qlen), F32)) * 2
            if nq == 1:
                st = step(0, 1, masks["first"], st)
            else:
                st = step(0, 1, masks["valid"], st)
                st = lax.fori_loop(0, n_whole, lambda j, s_: step(1 + 2 * j, 2, None, s_), st)
                st = step(qblk, 2, masks["diag"], st)
            o_t = jnp.concatenate([st[2] * (1.0 / st[1]), st[5] * (1.0 / st[4])], axis=0)
            of_ref[pl.ds(qoff, qlen), :] = o_t.T.astype(BF16)
            lse = [st[3 * e] + jnp.log(st[3 * e + 1]) for e in range(2)]
            for a in range(nq):
                rows = [lse[e][:, a * BLK:(a + 1) * BLK] for e in range(2)]
                lse_ref[qblk + a] = lse_ref[qblk + a] + (
                    jnp.where(sub8 == 2 * p, rows[0], 0.0) + jnp.where(sub8 == 2 * p + 1, rows[1], 0.0))

        attend(0, 1, 0)

        def q_loop(u, carry):
            attend(1 + 2 * u, 2, u)
            return carry

        lax.fori_loop(0, nu, q_loop, 0)

    return pl.pallas_call(
        body, name="f_fox", grid=(FOX_HEADS // 2,),
        in_specs=[pl.BlockSpec((L, 384), lambda p: (0, RET_W // 384 + p)), _full((L, BLK)), _full((nblk, 8, BLK))],
        out_specs=[pl.BlockSpec((L, BLK), lambda p: (0, p)), _full((nblk, 8, BLK))],
        out_shape=[jax.ShapeDtypeStruct((L, 512), BF16), jax.ShapeDtypeStruct((nblk, 8, BLK), F32)],
        scratch_shapes=[pltpu.VMEM((nblk, BLK, BLK), BF16), pltpu.VMEM((2, nblk, BLK, UNIT), F32)],
        compiler_params=_params(("arbitrary",)),
    )(proj, c, ctb)


def _outproj_up(mix_r, o_f, h0, w_out, ffn_g, w_up):
    L = h0.shape[0]
    tm = _row_tile(L)
    shard = w_up.shape[2]

    def body(mr_ref, of_ref, h0_ref, wo_ref, g_ref, wu_ref, h1_ref, n2_ref, up_ref):
        h1 = h0_ref[...] + _dot(mr_ref[...], wo_ref[0:512, :]) + _dot(of_ref[...], wo_ref[512:1024, :])
        h1_ref[...] = h1
        r = lax.rsqrt(jnp.mean(h1 * h1, axis=-1, keepdims=True) + EPS)
        n2 = (h1 * r * g_ref[...]).astype(BF16)
        n2_ref[...] = n2
        for j in range(N_CHIPS):
            up_ref[:, j * shard:(j + 1) * shard] = _dot(n2, wu_ref[j]).astype(BF16)

    rows = lambda w: pl.BlockSpec((tm, w), lambda i: (i, 0))
    return pl.pallas_call(
        body, name="f_outproj_up", grid=(L // tm,),
        in_specs=[rows(512), rows(512), rows(D_MODEL), _full((D_MODEL, D_MODEL)), _full((1, D_MODEL)),
                  _full((N_CHIPS, D_MODEL, shard))],
        out_specs=[rows(D_MODEL), rows(D_MODEL), rows(2 * D_FF)],
        out_shape=[jax.ShapeDtypeStruct((L, D_MODEL), F32), jax.ShapeDtypeStruct((L, D_MODEL), BF16),
                   jax.ShapeDtypeStruct((L, 2 * D_FF), BF16)],
        compiler_params=_params(("parallel",)),
    )(mix_r, o_f, h0, w_out, ffn_g, w_up)


def _conv_acc(a_ref, halo_ref, cw_refs, cb_ref, i, tm):
    sub = _iota((tm, 1), 0)
    a = jnp.where(i * tm + sub >= N_PAD, a_ref[...].astype(F32), 0.0)
    halo = halo_ref[...].astype(F32)
    hrow = i * tm - 8 + _iota((8, 1), 0)
    halo = jnp.where((hrow >= N_PAD) & (i > 0), halo, 0.0)
    a1 = jnp.where(sub == 0, _pick_row(halo, 7), pltpu.roll(a, 1, 0))
    a2 = jnp.where(sub == 0, _pick_row(halo, 6), jnp.where(sub == 1, _pick_row(halo, 7), pltpu.roll(a, 2, 0)))
    acc = cb_ref[...] + a2 * cw_refs[0][...]
    acc = acc + a1 * cw_refs[1][...]
    acc = acc + a * cw_refs[2][...]
    return a, a1, a2, acc


def _ffn_down_loss(up, conv_w, conv_b, w_down, h1, final_g, target):
    L = h1.shape[0]
    tm = _row_tile(L)
    cw = [conv_w[j:j + 1] for j in range(3)]

    def body(a_ref, halo_ref, b_ref, cw0, cw1, cw2, cb_ref, wd_ref, h1_ref, gf_ref, t_ref,
             g_ref, dh_ref, dhb_ref, dgf_ref, loss_ref):
        i = pl.program_id(0)

        @pl.when(i == 0)
        def _():
            dgf_ref[...] = jnp.zeros_like(dgf_ref)
            loss_ref[...] = jnp.zeros_like(loss_ref)

        _, _, _, acc = _conv_acc(a_ref, halo_ref, (cw0, cw1, cw2), cb_ref, i, tm)
        g = (acc * jax.nn.sigmoid(acc) * b_ref[...].astype(F32)).astype(BF16)
        g_ref[...] = g
        h2 = h1_ref[...] + _dot(g, wd_ref[...])
        r = lax.rsqrt(jnp.mean(h2 * h2, axis=-1, keepdims=True) + EPS)
        yn = h2 * r
        gf = gf_ref[...]
        live = i * tm + _iota((tm, 1), 0) >= PREFIX
        err = jnp.where(live, yn * gf - t_ref[...], 0.0)
        loss_ref[...] = loss_ref[...] + 0.5 * jnp.sum(jnp.mean(err * err, axis=-1, keepdims=True))
        dy = err * (1.0 / D_MODEL)
        dgf_ref[...] = dgf_ref[...] + jnp.sum(dy * yn, axis=0, keepdims=True)
        dyn = dy * gf
        dh = r * (dyn - yn * jnp.mean(dyn * yn, axis=-1, keepdims=True))
        dh_ref[...] = dh
        dhb_ref[...] = dh.astype(BF16)

    rows = lambda w, c=0: pl.BlockSpec((tm, w), lambda i: (i, c))
    halo = pl.BlockSpec((8, D_FF), lambda i: (jnp.maximum(i * (tm // 8) - 1, 0), 0))
    return pl.pallas_call(
        body, name="f_ffn_down_loss", grid=(L // tm,),
        in_specs=[rows(D_FF), halo, rows(D_FF, 1), _full((1, D_FF)), _full((1, D_FF)), _full((1, D_FF)),
                  _full((1, D_FF)), _full((D_FF, D_MODEL)), rows(D_MODEL), _full((1, D_MODEL)), rows(D_MODEL)],
        out_specs=[rows(D_FF), rows(D_MODEL), rows(D_MODEL), _full((1, D_MODEL)), _full((1, BLK))],
        out_shape=[jax.ShapeDtypeStruct((L, D_FF), BF16), jax.ShapeDtypeStruct((L, D_MODEL), F32),
                   jax.ShapeDtypeStruct((L, D_MODEL), BF16), jax.ShapeDtypeStruct((1, D_MODEL), F32),
                   jax.ShapeDtypeStruct((1, BLK), F32)],
        compiler_params=_params(("arbitrary",)),
    )(up, up, up, cw[0], cw[1], cw[2], conv_b, w_down, h1, final_g, target)


def _ffn_bwd_gate(dh2b, w_down, up, conv_w, conv_b):
    L = dh2b.shape[0]
    tm = _row_tile(L)
    cw = [conv_w[j:j + 1] for j in range(3)]

    def body(dh_ref, wd_ref, a_ref, halo_ref, b_ref, cw0, cw1, cw2, cb_ref, dacc_ref, db_ref, dcw_ref):
        i = pl.program_id(0)

        @pl.when(i == 0)
        def _():
            dcw_ref[...] = jnp.zeros_like(dcw_ref)

        a, a1, a2, acc = _conv_acc(a_ref, halo_ref, (cw0, cw1, cw2), cb_ref, i, tm)
        dg = _dot_nt(dh_ref[...], wd_ref[...])
        sg = jax.nn.sigmoid(acc)
        db_ref[...] = (dg * acc * sg).astype(BF16)
        dacc = dg * b_ref[...].astype(F32) * (sg * (1.0 + acc * (1.0 - sg)))
        dacc_ref[...] = dacc.astype(BF16)
        sub8 = _iota((8, 1), 0)
        rows = [jnp.sum(dacc * t, axis=0, keepdims=True) for t in (a2, a1, a)] + [jnp.sum(dacc, axis=0, keepdims=True)]
        upd = jnp.zeros((8, D_FF), F32)
        for j, rj in enumerate(rows):
            upd = upd + jnp.where(sub8 == j, rj, 0.0)
        dcw_ref[...] = dcw_ref[...] + upd

    rows = lambda w, c=0: pl.BlockSpec((tm, w), lambda i: (i, c))
    halo = pl.BlockSpec((8, D_FF), lambda i: (jnp.maximum(i * (tm // 8) - 1, 0), 0))
    return pl.pallas_call(
        body, name="b_ffn_gate", grid=(L // tm,),
        in_specs=[rows(D_MODEL), _full((D_FF, D_MODEL)), rows(D_FF), halo, rows(D_FF, 1),
                  _full((1, D_FF)), _full((1, D_FF)), _full((1, D_FF)), _full((1, D_FF))],
        out_specs=[rows(D_FF), rows(D_FF), _full((8, D_FF))],
        out_shape=[jax.ShapeDtypeStruct((L, D_FF), BF16), jax.ShapeDtypeStruct((L, D_FF), BF16),
                   jax.ShapeDtypeStruct((8, D_FF), F32)],
        compiler_params=_params(("arbitrary",)),
    )(dh2b, w_down, up, up, up, cw[0], cw[1], cw[2], conv_b)


def _ffn_bwd_up(dacc, db, conv_w, w_up, h1, ffn_g, dh2, w_out):
    L = h1.shape[0]
    tm = _row_tile(L)
    nt = L // tm
    shard = w_up.shape[2]
    cw = [conv_w[j:j + 1] for j in range(3)]

    def body(da_ref, halo_ref, db_ref, cw0, cw1, cw2, wu_ref, h1_ref, g_ref, dh2_ref, wo_ref,
             dup_ref, dh1_ref, dh1b_ref, dmix_ref, dg_ref):
        i = pl.program_id(0)

        @pl.when(i == 0)
        def _():
            dg_ref[...] = jnp.zeros_like(dg_ref)

        sub = _iota((tm, 1), 0)
        d0 = da_ref[...].astype(F32)
        halo = jnp.where(i < nt - 1, halo_ref[...].astype(F32), 0.0)
        d1 = jnp.where(sub == tm - 1, _pick_row(halo, 0), pltpu.roll(d0, tm - 1, 0))
        d2 = jnp.where(sub == tm - 2, _pick_row(halo, 0),
                       jnp.where(sub == tm - 1, _pick_row(halo, 1), pltpu.roll(d0, tm - 2, 0)))
        da = d0 * cw2[...] + d1 * cw1[...] + d2 * cw0[...]
        da = jnp.where(i * tm + sub >= N_PAD, da, 0.0).astype(BF16)
        dup_ref[:, 0:D_FF] = da
        dbv = db_ref[...]
        dup_ref[:, D_FF:2 * D_FF] = dbv
        dn = jnp.zeros((tm, D_MODEL), F32)
        for j in range(N_CHIPS):
            src = da if j < 2 else dbv
            lo = (j % 2) * shard
            dn = dn + _dot_nt(src[:, lo:lo + shard], wu_ref[j])
        h1 = h1_ref[...]
        r = lax.rsqrt(jnp.mean(h1 * h1, axis=-1, keepdims=True) + EPS)
        yn = h1 * r
        dg_ref[...] = dg_ref[...] + jnp.sum(dn * yn, axis=0, keepdims=True)
        dyn = dn * g_ref[...]
        dh1 = dh2_ref[...] + r * (dyn - yn * jnp.mean(dyn * yn, axis=-1, keepdims=True))
        dh1_ref[...] = dh1
        dh1b = dh1.astype(BF16)
        dh1b_ref[...] = dh1b
        dmix_ref[...] = _dot_nt(dh1b, wo_ref[...]).astype(BF16)

    rows = lambda w: pl.BlockSpec((tm, w), lambda i: (i, 0))
    halo = pl.BlockSpec((8, D_FF), lambda i: (jnp.minimum((i + 1) * (tm // 8), L // 8 - 1), 0))
    return pl.pallas_call(
        body, name="b_ffn_up", grid=(nt,),
        in_specs=[rows(D_FF), halo, rows(D_FF), _full((1, D_FF)), _full((1, D_FF)), _full((1, D_FF)),
                  _full((N_CHIPS, D_MODEL, shard)), rows(D_MODEL), _full((1, D_MODEL)), rows(D_MODEL),
                  _full((D_MODEL, D_MODEL))],
        out_specs=[rows(2 * D_FF), rows(D_MODEL), rows(D_MODEL), rows(D_MODEL), _full((1, D_MODEL))],
        out_shape=[jax.ShapeDtypeStruct((L, 2 * D_FF), BF16), jax.ShapeDtypeStruct((L, D_MODEL), F32),
                   jax.ShapeDtypeStruct((L, D_MODEL), BF16), jax.ShapeDtypeStruct((L, D_MODEL), BF16),
                   jax.ShapeDtypeStruct((1, D_MODEL), F32)],
        compiler_params=_params(("arbitrary",)),
    )(dacc, dacc, db, cw[0], cw[1], cw[2], w_up, h1, ffn_g, dh2, w_out)


def _wgrad(a, b, name, tn=None):
    L, K = a.shape
    N = b.shape[1]
    tn = N if tn is None else tn
    tl = _row_tile(L)

    def body(a_ref, b_ref, o_ref):
        @pl.when(pl.program_id(1) == 0)
        def _():
            o_ref[...] = jnp.zeros_like(o_ref)

        o_ref[0] = o_ref[0] + _dot_tn(a_ref[...], b_ref[...])

    return pl.pallas_call(
        body, name=name, grid=(N // tn, L // tl),
        in_specs=[pl.BlockSpec((tl, K), lambda n, l: (l, 0)), pl.BlockSpec((tl, tn), lambda n, l: (l, n))],
        out_specs=pl.BlockSpec((1, K, tn), lambda n, l: (n, 0, 0)),
        out_shape=jax.ShapeDtypeStruct((N // tn, K, tn), F32),
        compiler_params=_params(("parallel", "arbitrary")),
    )(a, b)


def _retention_bwd(dmix, o, proj, cos_t, sin_t, ret_g, states):
    L = proj.shape[0]
    nblk = L // BLK
    dmat, wq_t, wk_t, g_blk = _decay_tables()

    def body(dm_ref, o_ref, q_ref, k_ref, v_ref, gate_ref, cos_ref, sin_ref, d_ref, wq_ref, wk_ref, rg_ref, rs_ref,
             dp_ref, drg_ref, gstate):
        i = pl.program_id(0)

        @pl.when(i == 0)
        def _():
            gstate[...] = jnp.zeros_like(gstate)
            drg_ref[...] = jnp.zeros_like(drg_ref)

        rot, rot_t = _rot_fns(cos_ref[...], sin_ref[...])
        lane = _iota((BLK, BLK), 1)
        sub = _iota((BLK, BLK), 0)
        scale = HEAD_LANES ** -0.5
        for p in range(2):
            qr = rot(q_ref[:, p * BLK:(p + 1) * BLK].astype(F32))
            kr = rot(k_ref[:, p * BLK:(p + 1) * BLK].astype(F32)) * scale
            kr_b = kr.astype(BF16)
            qw = (qr * wq_ref[p]).astype(BF16)
            kw = (kr * wk_ref[p]).astype(BF16)
            dqr = jnp.zeros((BLK, BLK), F32)
            dkr = jnp.zeros((BLK, BLK), F32)
            for e in range(2):
                h = 2 * p + e
                cols = slice(h * BLK, (h + 1) * BLK)
                head_lanes = (lane >> 6) == e
                o = o_ref[:, cols]
                rn = lax.rsqrt(jnp.mean(o * o, axis=-1, keepdims=True) + EPS)
                y = o * rn
                gate = gate_ref[:, cols].astype(F32)
                sg = jax.nn.sigmoid(gate)
                dm = dm_ref[:, cols].astype(F32)
                rgain = rg_ref[:, cols]
                drg_ref[:, cols] = drg_ref[:, cols] + jnp.sum(dm * y * (gate * sg), axis=0, keepdims=True)
                dp_ref[:, 1024 + h * BLK:1024 + (h + 1) * BLK] = (
                    dm * y * rgain * (sg * (1.0 + gate * (1.0 - sg)))).astype(BF16)
                dy = dm * rgain * (gate * sg)
                do = (rn * (dy - y * jnp.mean(dy * y, axis=-1, keepdims=True))).astype(BF16)
                vh = v_ref[:, cols]
                qm = jnp.where(head_lanes, qr, 0.0).astype(BF16)
                dmh = d_ref[h]
                s = (_dot_nt(qm, kr_b) * dmh).astype(BF16)
                ds = (_dot_nt(do, vh) * dmh).astype(BF16)
                st = rs_ref[0, h].astype(BF16)
                gs = gstate[h]
                gs_b = gs.astype(BF16)
                dqr = dqr + jnp.where(head_lanes, _dot(ds, kr_b), 0.0) + _dot_nt(do, st) * wq_ref[p]
                dkr = dkr + _dot_tn(ds, qm) + _dot_nt(vh, gs_b) * wk_ref[p]
                dp_ref[:, 512 + h * BLK:512 + (h + 1) * BLK] = (_dot_tn(s, do) + _dot(kw, gs_b)).astype(BF16)
                dr = jnp.where((sub >> 6) == e, _dot_tn(qw, do), 0.0)
                gstate[h] = dr + g_blk[h] * gs
            dp_ref[:, p * BLK:(p + 1) * BLK] = rot_t(dqr).astype(BF16)
            dp_ref[:, 256 + p * BLK:256 + (p + 1) * BLK] = (rot_t(dkr) * scale).astype(BF16)

    row = lambda c: (lambda i: (nblk - 1 - i, c))
    return pl.pallas_call(
        body, name="b_retention", grid=(nblk,),
        in_specs=[pl.BlockSpec((BLK, 512), row(0)), pl.BlockSpec((BLK, 512), row(0)),
                  pl.BlockSpec((BLK, 256), row(0)), pl.BlockSpec((BLK, 256), row(1)),
                  pl.BlockSpec((BLK, 512), row(1)), pl.BlockSpec((BLK, 512), row(2)),
                  pl.BlockSpec((BLK, BLK), row(0)), pl.BlockSpec((BLK, BLK), row(0)),
                  _full((RET_HEADS, BLK, BLK)), _full((2, BLK, BLK)), _full((2, BLK, BLK)), _full((1, 512)),
                  pl.BlockSpec((1, RET_HEADS, BLK, BLK), lambda i: (nblk - 1 - i, 0, 0, 0))],
        out_specs=[pl.BlockSpec((BLK, RET_W), row(0)), _full((1, 512))],
        out_shape=[jax.ShapeDtypeStruct((L, RET_W), BF16), jax.ShapeDtypeStruct((1, 512), F32)],
        scratch_shapes=[pltpu.VMEM((RET_HEADS, BLK, BLK), F32)],
        compiler_params=_params(("arbitrary",)),
    )(dmix, o, proj, proj, proj, proj, cos_t, sin_t, dmat, wq_t, wk_t, ret_g, states)


def _fox_delta(dmix, o_f):
    L = o_f.shape[0]
    nblk = L // BLK

    def body(do_ref, o_ref, d_ref):
        prod = do_ref[...].astype(F32) * o_ref[...].astype(F32)
        sel = ((_iota((8, 512), 1) >> 6) == _iota((8, 512), 0)).astype(BF16)
        hi = prod.astype(BF16)
        lo = (prod - hi.astype(F32)).astype(BF16)
        d_ref[0] = _dot_nt(sel, hi) + _dot_nt(sel, lo)

    return pl.pallas_call(
        body, name="b_foxdelta", grid=(nblk,),
        in_specs=[pl.BlockSpec((BLK, 512), lambda i: (i, 1)), pl.BlockSpec((BLK, 512), lambda i: (i, 0))],
        out_specs=pl.BlockSpec((1, 8, BLK), lambda i: (i, 0, 0)),
        out_shape=jax.ShapeDtypeStruct((nblk, 8, BLK), F32),
        compiler_params=_params(("parallel",)),
    )(dmix, o_f)


def _fox_bwd(proj, dmix, c, ctb, lse, delta):
    L = proj.shape[0]
    nblk, nu = _fox_units(L)
    scale = HEAD_LANES ** -0.5

    def body(qkv_ref, do_ref, c_ref, ct_ref, lse_ref, dl_ref, dp_ref, dc_ref, dcq_ref,
             ktt, dqt, dk_acc, dv_acc, dcs_acc):
        p = pl.program_id(0)

        @pl.when(p == 0)
        def _():
            dc_ref[...] = jnp.zeros_like(dc_ref)
            dcq_ref[...] = jnp.zeros_like(dcq_ref)

        lane = _iota((BLK, BLK), 1)
        sub8 = _iota((8, BLK), 0)
        masks = _fox_tile_masks()

        def pre(j, carry):
            off = pl.multiple_of(j * BLK, BLK)
            ktt[j] = qkv_ref[pl.ds(off, BLK), BLK:2 * BLK].astype(F32).T.astype(BF16)
            dqt[j] = jnp.zeros((BLK, BLK), F32)
            return carry

        lax.fori_loop(0, nblk, pre, 0)

        def kv_pass(kblk, nk, n_later):
            klen = nk * BLK
            koff = pl.multiple_of(kblk * BLK, BLK)
            kt = qkv_ref[pl.ds(koff, klen), BLK:2 * BLK]
            vtile = qkv_ref[pl.ds(koff, klen), 2 * BLK:3 * BLK]
            ct = c_ref[pl.ds(koff, klen), :]
            klane = _iota((klen, BLK), 1)
            cs = [jnp.broadcast_to(jnp.sum(jnp.where(klane == 2 * p + e, ct, 0.0), axis=1, keepdims=True), (klen, UNIT))
                  for e in range(2)]
            dk_acc[0:klen] = jnp.zeros((klen, BLK), F32)
            dv_acc[0:klen] = jnp.zeros((klen, BLK), F32)
            for e in range(2):
                dcs_acc[e, 0:klen] = jnp.zeros((klen, BLK), F32)

            def tile(qblk, nq, mask):
                qlen = nq * BLK
                qoff = pl.multiple_of(qblk * BLK, BLK)
                qs = qkv_ref[pl.ds(qoff, qlen), 0:BLK].astype(F32) * scale
                dot_ = do_ref[pl.ds(qoff, qlen), :]
                qlane = _iota((qlen, BLK), 1)
                stats = [[ref[qblk + a] for a in range(nq)] for ref in (ct_ref, lse_ref, dl_ref)]
                for e in range(2):
                    h = 2 * p + e
                    head = (qlane >> 6) == e
                    ct_row, lse_row, dl_row = [jnp.concatenate([_pick_row(t, h) for t in ts], axis=1) for ts in stats]
                    qm = jnp.where(head, qs, 0.0).astype(BF16)
                    dom = jnp.where(head, dot_, jnp.zeros_like(dot_))
                    t = _dot_nt(kt, qm) - cs[e][:, 0:qlen]
                    if mask is not None:
                        t = jnp.where(mask, t, NEG)
                    pr = jnp.exp(t + (ct_row - lse_row))
                    dv_acc[0:klen] = dv_acc[0:klen] + _dot(pr.astype(BF16), dom)
                    dsv = pr * (_dot_nt(vtile, dom) - dl_row)
                    ds_b = dsv.astype(BF16)
                    dk_acc[0:klen] = dk_acc[0:klen] + _dot(ds_b, qm)
                    rows = slice(e * HEAD_LANES, (e + 1) * HEAD_LANES)
                    dq_t = _dot(ktt[kblk, rows, :], ds_b[0:BLK])
                    for b in range(1, nk):
                        dq_t = dq_t + _dot(ktt[kblk + b, rows, :], ds_b[b * BLK:(b + 1) * BLK])
                    key_side = dsv[:, 0:BLK]
                    for a in range(1, nq):
                        key_side = key_side + dsv[:, a * BLK:(a + 1) * BLK]
                    dcs_acc[e, 0:klen] = dcs_acc[e, 0:klen] + key_side
                    query_side = jnp.sum(dsv, axis=0, keepdims=True)
                    for a in range(nq):
                        cols = slice(a * BLK, (a + 1) * BLK)
                        dqt[qblk + a, rows, :] = dqt[qblk + a, rows, :] + dq_t[:, cols]
                        dcq_ref[qblk + a] = dcq_ref[qblk + a] + jnp.where(sub8 == h, query_side[:, cols], 0.0)

            def later(i, carry):
                tile(kblk + nk + 2 * i, 2, masks["valid"] if nk == 1 else None)
                return carry

            tile(kblk, nk, masks["first"] if nk == 1 else masks["diag"])
            lax.fori_loop(0, n_later, later, 0)
            dp_ref[pl.ds(koff, klen), BLK:2 * BLK] = dk_acc[0:klen].astype(BF16)
            dp_ref[pl.ds(koff, klen), 2 * BLK:3 * BLK] = dv_acc[0:klen].astype(BF16)
            upd = jnp.zeros((klen, BLK), F32)
            for e in range(2):
                upd = upd + jnp.where(klane == 2 * p + e, -jnp.sum(dcs_acc[e, 0:klen], axis=1, keepdims=True), 0.0)
            dc_ref[pl.ds(koff, klen), :] = dc_ref[pl.ds(koff, klen), :] + upd

        kv_pass(0, 1, nu)

        def k_loop(u, carry):
            kv_pass(1 + 2 * u, 2, nu - 1 - u)
            return carry

        lax.fori_loop(0, nu, k_loop, 0)

        def flush(j, carry):
            off = pl.multiple_of(j * BLK, BLK)
            dp_ref[pl.ds(off, BLK), 0:BLK] = (dqt[j].T * scale).astype(BF16)
            return carry

        lax.fori_loop(0, nblk, flush, 0)

    stat = _full((nblk, 8, BLK))
    return pl.pallas_call(
        body, name="b_fox", grid=(FOX_HEADS // 2,),
        in_specs=[pl.BlockSpec((L, 384), lambda p: (0, RET_W // 384 + p)), pl.BlockSpec((L, BLK), lambda p: (0, 4 + p)),
                  _full((L, BLK)), stat, stat, stat],
        out_specs=[pl.BlockSpec((L, 384), lambda p: (0, p)), _full((L, BLK)), stat],
        out_shape=[jax.ShapeDtypeStruct((L, FOX_W), BF16), jax.ShapeDtypeStruct((L, BLK), F32),
                   jax.ShapeDtypeStruct((nblk, 8, BLK), F32)],
        scratch_shapes=[pltpu.VMEM((nblk, BLK, BLK), BF16), pltpu.VMEM((nblk, BLK, BLK), F32),
                        pltpu.VMEM((UNIT, BLK), F32), pltpu.VMEM((UNIT, BLK), F32), pltpu.VMEM((2, UNIT, BLK), F32)],
        compiler_params=_params(("arbitrary",)),
    )(proj, dmix, c, ctb, lse, delta)


def _fox_post(dc, dcq, ff, fb):
    L = dc.shape[0]
    nblk = L // BLK

    def body(dc_ref, dcq_ref, ff_ref, b_ref, dff_ref, dffb_ref, dfb_ref, carry):
        i = pl.program_id(0)

        @pl.when(i == 0)
        def _():
            carry[...] = jnp.zeros_like(carry)
            dfb_ref[...] = jnp.zeros_like(dfb_ref)

        d = dc_ref[...] + jnp.concatenate([dcq_ref[0], jnp.zeros((BLK - 8, BLK), F32)], axis=0).T
        tri = (_iota((BLK, BLK), 0) <= _iota((BLK, BLK), 1)).astype(BF16)
        hi, mid, lo = _split3(d)
        dlf = _dot(tri, hi) + _dot(tri, mid) + _dot(tri, lo) + carry[...]
        carry[...] = carry[...] + jnp.sum(d, axis=0, keepdims=True)
        z = ff_ref[...] + b_ref[...]
        dff = jnp.where(_iota((BLK, BLK), 1) < FOX_HEADS, dlf * jax.nn.sigmoid(-z), 0.0)
        dff_ref[...] = dff
        dffb_ref[...] = dff.astype(BF16)
        dfb_ref[...] = dfb_ref[...] + jnp.sum(dff, axis=0, keepdims=True)

    rev = lambda i: (nblk - 1 - i, 0)
    return pl.pallas_call(
        body, name="b_foxpost", grid=(nblk,),
        in_specs=[pl.BlockSpec((BLK, BLK), rev), pl.BlockSpec((1, 8, BLK), lambda i: (nblk - 1 - i, 0, 0)),
                  pl.BlockSpec((BLK, BLK), rev), _full((1, BLK))],
        out_specs=[pl.BlockSpec((BLK, BLK), rev), pl.BlockSpec((BLK, BLK), rev), _full((1, BLK))],
        out_shape=[jax.ShapeDtypeStruct((L, BLK), F32), jax.ShapeDtypeStruct((L, BLK), BF16),
                   jax.ShapeDtypeStruct((1, BLK), F32)],
        scratch_shapes=[pltpu.VMEM((1, BLK), F32)],
        compiler_params=_params(("arbitrary",)),
    )(dc, dcq, ff, fb)


def _inproj_bwd(dpr, dpf, dffb, w_main, w_ff, h0, g, dh1):
    L = h0.shape[0]
    tm = _row_tile(L)

    def body(dpr_ref, dpf_ref, dff_ref, wm_ref, wf_ref, h_ref, g_ref, dh1_ref, dh0_ref, dg_ref):
        @pl.when(pl.program_id(0) == 0)
        def _():
            dg_ref[...] = jnp.zeros_like(dg_ref)

        dn = (_dot_nt(dpr_ref[...], wm_ref[:, 0:RET_W]) + _dot_nt(dpf_ref[...], wm_ref[:, RET_W:MAIN_W])
              + _dot_nt(dff_ref[...], wf_ref[...]))
        h = h_ref[...]
        r = lax.rsqrt(jnp.mean(h * h, axis=-1, keepdims=True) + EPS)
        yn = h * r
        dg_ref[...] = dg_ref[...] + jnp.sum(dn * yn, axis=0, keepdims=True)
        dyn = dn * g_ref[...]
        dh0_ref[...] = dh1_ref[...] + r * (dyn - yn * jnp.mean(dyn * yn, axis=-1, keepdims=True))

    rows = lambda w: pl.BlockSpec((tm, w), lambda i: (i, 0))
    return pl.pallas_call(
        body, name="b_inproj", grid=(L // tm,),
        in_specs=[rows(RET_W), rows(FOX_W), rows(BLK), _full((D_MODEL, MAIN_W)), _full((D_MODEL, BLK)),
                  rows(D_MODEL), _full((1, D_MODEL)), rows(D_MODEL)],
        out_specs=[rows(D_MODEL), _full((1, D_MODEL))],
        out_shape=[jax.ShapeDtypeStruct((L, D_MODEL), F32), jax.ShapeDtypeStruct((1, D_MODEL), F32)],
        compiler_params=_params(("arbitrary",)),
    )(dpr, dpf, dffb, w_main, w_ff, h0, g, dh1)


def _local_step(x, target, meta, attn_g, w_main, w_ff, fox_b, ret_g, w_out, ffn_g, w_up, conv_w, conv_b, w_down, final_g):
    S = x.shape[0]
    L = S + PREFIX
    h0 = jnp.concatenate([jnp.zeros((N_PAD, D_MODEL), F32), meta, x], axis=0)
    tgt = jnp.concatenate([jnp.zeros((PREFIX, D_MODEL), F32), target], axis=0)
    fb = jnp.pad(fox_b, ((0, 0), (0, BLK - FOX_HEADS)))
    cos_t, sin_t = _rotary_tables(L)

    n1, proj, ff = _rms_inproj(h0, attn_g, w_main, w_ff)
    c, ctb = _fox_prep(ff, fb)
    mix_r, o_ret, states = _retention_fwd(proj, cos_t, sin_t, ret_g)
    o_f, lse = _fox_fwd(proj, c, ctb)
    h1, n2, up = _outproj_up(mix_r, o_f, h0, w_out, ffn_g, w_up)
    g_act, dh2, dh2b, d_final_g, loss = _ffn_down_loss(up, conv_w, conv_b, w_down, h1, final_g, tgt)

    dacc, db, dconv = _ffn_bwd_gate(dh2b, w_down, up, conv_w, conv_b)
    dup, dh1, dh1b, dmix, d_ffn_g = _ffn_bwd_up(dacc, db, conv_w, w_up, h1, ffn_g, dh2, w_out)
    d_w_down = _wgrad(g_act, dh2b, "wgrad_down", tn=None)[0]
    d_w_up = _wgrad(n2, dup, "wgrad_up", tn=w_up.shape[2])
    d_w_out = jnp.concatenate([_wgrad(mix_r, dh1b, "wgrad_out_r")[0], _wgrad(o_f, dh1b, "wgrad_out_f")[0]], axis=0)

    dpr, d_ret_g = _retention_bwd(dmix, o_ret, proj, cos_t, sin_t, ret_g, states)
    delta = _fox_delta(dmix, o_f)
    dpf, dc, dcq = _fox_bwd(proj, dmix, c, ctb, lse, delta)
    dff, dffb, d_fox_b = _fox_post(dc, dcq, ff, fb)
    dh0, d_attn_g = _inproj_bwd(dpr, dpf, dffb, w_main, w_ff, h0, attn_g, dh1)
    d_w_main = jnp.concatenate([_wgrad(n1, dpr, "wgrad_in_r")[0], _wgrad(n1, dpf, "wgrad_in_f")[0]], axis=1)
    d_w_ff = _wgrad(n1, dffb, "wgrad_in_ff")[0]

    return dict(
        loss=loss[0, 0], dx=dh0[PREFIX:], dmeta=dh0[N_PAD:PREFIX], attn_g=d_attn_g, w_main=d_w_main,
        w_ff=d_w_ff[:, :FOX_HEADS], fox_b=d_fox_b[:, :FOX_HEADS], ret_g=d_ret_g, w_out=d_w_out, ffn_g=d_ffn_g,
        w_up=d_w_up, conv_w=dconv[0:3], conv_b=dconv[3:4], w_down=d_w_down, final_g=d_final_g)


_ANY = pl.BlockSpec(memory_space=pl.ANY)


def _place():
    return lax.axis_index("x"), lax.axis_index("y"), lax.axis_index("c")


def _other_chips(x, y):
    return [(1 - x, y), (x, 1 - y), (1 - x, 1 - y)]


def _chip_allgather(arrays):
    n = len(arrays)

    def body(*refs):
        ins, outs = refs[:n], refs[n:2 * n]
        send, recv, loc = refs[2 * n:]
        x, y, c = _place()
        mine = 2 * x + y
        peers = _other_chips(x, y)

        def remote(a, k, slot):
            return pltpu.make_async_remote_copy(
                src_ref=ins[a], dst_ref=outs[a].at[slot], send_sem=send.at[3 * a + k], recv_sem=recv.at[3 * a + k],
                device_id=(peers[k][0], peers[k][1], c), device_id_type=MESH)

        local = [pltpu.make_async_copy(ins[a], outs[a].at[mine], loc.at[a]) for a in range(n)]
        sends = [remote(a, k, mine) for a in range(n) for k in range(3)]
        for cp in local + sends:
            cp.start()
        for a in range(n):
            for k in range(3):
                remote(a, k, 2 * peers[k][0] + peers[k][1]).wait_recv()
        for cp in sends:
            cp.wait_send()
        for cp in local:
            cp.wait()

    return pl.pallas_call(
        body, name="ag_weights", in_specs=[_ANY] * n, out_specs=[_ANY] * n,
        out_shape=[jax.ShapeDtypeStruct((N_CHIPS,) + a.shape, a.dtype) for a in arrays],
        scratch_shapes=[pltpu.SemaphoreType.DMA((3 * n,)), pltpu.SemaphoreType.DMA((3 * n,)),
                        pltpu.SemaphoreType.DMA((n,))],
    )(*arrays)


def _sibling_exchange(grads, small):
    n = len(grads)

    def body(*refs):
        ins, small_in = refs[:n], refs[n]
        outs, small_out = refs[n + 1:2 * n + 1], refs[2 * n + 1]
        send, recv, s_send, s_recv, loc = refs[2 * n + 2:]
        x, y, c = _place()
        me = 4 * x + 2 * y + c

        def half_copy(a, which):
            half = ins[a].shape[1] // 2
            return pltpu.make_async_remote_copy(
                src_ref=ins[a].at[pl.ds(0, N_CHIPS), pl.ds(which * half, half)], dst_ref=outs[a],
                send_sem=send.at[a], recv_sem=recv.at[a], device_id=(x, y, 1 - c), device_id_type=MESH)

        def peer_of(r):
            return tuple(1 - v if (r >> b) & 1 else v for v, b in ((x, 2), (y, 1), (c, 0)))

        def small_copy(r, slot):
            return pltpu.make_async_remote_copy(
                src_ref=small_in, dst_ref=small_out.at[slot], send_sem=s_send.at[r - 1], recv_sem=s_recv.at[r - 1],
                device_id=peer_of(r), device_id_type=MESH)

        local = pltpu.make_async_copy(small_in, small_out.at[me], loc.at[0])
        sends = [half_copy(a, 1 - c) for a in range(n)] + [small_copy(r, me) for r in range(1, N_DEV)]
        local.start()
        for cp in sends:
            cp.start()
        for r in range(1, N_DEV):
            px, py, pc = peer_of(r)
            small_copy(r, 4 * px + 2 * py + pc).wait_recv()
        for a in range(n):
            half_copy(a, c).wait_recv()
        for cp in sends:
            cp.wait_send()
        local.wait()

    rows = small.shape[0]
    return pl.pallas_call(
        body, name="rs_sibling", in_specs=[_ANY] * (n + 1), out_specs=[_ANY] * (n + 1),
        out_shape=[jax.ShapeDtypeStruct((N_CHIPS, g.shape[1] // 2, g.shape[2]), g.dtype) for g in grads]
        + [jax.ShapeDtypeStruct((N_DEV, rows, small.shape[1]), small.dtype)],
        scratch_shapes=[pltpu.SemaphoreType.DMA((n,)), pltpu.SemaphoreType.DMA((n,)),
                        pltpu.SemaphoreType.DMA((N_DEV - 1,)), pltpu.SemaphoreType.DMA((N_DEV - 1,)),
                        pltpu.SemaphoreType.DMA((1,))],
    )(*grads, small)


def _chip_reduce_scatter(parts):
    n = len(parts)

    def body(*refs):
        ins, outs = refs[:n], refs[n:2 * n]
        send, recv, loc = refs[2 * n:]
        x, y, c = _place()
        mine = 2 * x + y
        peers = _other_chips(x, y)

        def remote(a, k, src_slot, dst_slot):
            return pltpu.make_async_remote_copy(
                src_ref=ins[a].at[src_slot], dst_ref=outs[a].at[dst_slot], send_sem=send.at[3 * a + k],
                recv_sem=recv.at[3 * a + k], device_id=(peers[k][0], peers[k][1], c), device_id_type=MESH)

        local = [pltpu.make_async_copy(ins[a].at[mine], outs[a].at[mine], loc.at[a]) for a in range(n)]
        sends = [remote(a, k, 2 * peers[k][0] + peers[k][1], mine) for a in range(n) for k in range(3)]
        for cp in local + sends:
            cp.start()
        for a in range(n):
            for k in range(3):
                theirs = 2 * peers[k][0] + peers[k][1]
                remote(a, k, theirs, theirs).wait_recv()
        for cp in sends:
            cp.wait_send()
        for cp in local:
            cp.wait()

    return pl.pallas_call(
        body, name="rs_chips", in_specs=[_ANY] * n, out_specs=[_ANY] * n,
        out_shape=[jax.ShapeDtypeStruct(p.shape, p.dtype) for p in parts],
        scratch_shapes=[pltpu.SemaphoreType.DMA((3 * n,)), pltpu.SemaphoreType.DMA((3 * n,)),
                        pltpu.SemaphoreType.DMA((n,))],
    )(*parts)


def _sibling_allgather(halves):
    n = len(halves)

    def body(*refs):
        ins, outs = refs[:n], refs[n:2 * n]
        send, recv, loc = refs[2 * n:]
        x, y, c = _place()

        def rows(a, which):
            half = ins[a].shape[0]
            return outs[a].at[pl.ds(which * half, half)]

        def remote(a, which):
            return pltpu.make_async_remote_copy(
                src_ref=ins[a], dst_ref=rows(a, which), send_sem=send.at[a], recv_sem=recv.at[a],
                device_id=(x, y, 1 - c), device_id_type=MESH)

        local = [pltpu.make_async_copy(ins[a], rows(a, c), loc.at[a]) for a in range(n)]
        sends = [remote(a, c) for a in range(n)]
        for cp in local + sends:
            cp.start()
        for a in range(n):
            remote(a, 1 - c).wait_recv()
        for cp in sends:
            cp.wait_send()
        for cp in local:
            cp.wait()

    return pl.pallas_call(
        body, name="ag_sibling", in_specs=[_ANY] * n, out_specs=[_ANY] * n,
        out_shape=[jax.ShapeDtypeStruct((2 * h.shape[0], h.shape[1]), h.dtype) for h in halves],
        scratch_shapes=[pltpu.SemaphoreType.DMA((n,)), pltpu.SemaphoreType.DMA((n,)), pltpu.SemaphoreType.DMA((n,))],
    )(*halves)


def _pair_add(full, recv, core, name):
    _, R, C = full.shape
    half = R // 2

    def body(core_ref, a_ref, b_ref, o_ref):
        o_ref[...] = a_ref[...] + b_ref[...]

    return pl.pallas_call(
        body, name=name,
        grid_spec=pltpu.PrefetchScalarGridSpec(
            num_scalar_prefetch=1, grid=(N_CHIPS,),
            in_specs=[pl.BlockSpec((1, half, C), lambda j, core_ref: (j, core_ref[0], 0)),
                      pl.BlockSpec((1, half, C), lambda j, core_ref: (j, 0, 0))],
            out_specs=pl.BlockSpec((1, half, C), lambda j, core_ref: (j, 0, 0))),
        out_shape=jax.ShapeDtypeStruct((N_CHIPS, half, C), full.dtype),
        compiler_params=_params(("parallel",)),
    )(core, full, recv)


def _sum_slots(q, name, tiles=4):
    n, R, C = q.shape
    tr = R // tiles

    def body(q_ref, o_ref):
        acc = q_ref[0]
        for j in range(1, n):
            acc = acc + q_ref[j]
        o_ref[...] = acc

    return pl.pallas_call(
        body, name=name, grid=(tiles,),
        in_specs=[pl.BlockSpec((n, tr, C), lambda i: (0, i, 0))],
        out_specs=pl.BlockSpec((tr, C), lambda i: (i, 0)),
        out_shape=jax.ShapeDtypeStruct((R, C), q.dtype),
        compiler_params=_params(("parallel",)),
    )(q)


def _adamw(w, g, m, v, name, tiles=4):
    R, C = w.shape
    tr = R // tiles

    def body(w_ref, g_ref, m_ref, v_ref, d_ref, m2_ref, v2_ref):
        g_ = g_ref[...]
        m2 = ADAM_B1 * m_ref[...] + (1.0 - ADAM_B1) * g_
        v2 = ADAM_B2 * v_ref[...] + (1.0 - ADAM_B2) * (g_ * g_)
        m_hat = m2 / (1.0 - ADAM_B1 ** ADAM_STEP)
        v_hat = v2 / (1.0 - ADAM_B2 ** ADAM_STEP)
        d_ref[...] = -ADAM_LR * (m_hat / (jnp.sqrt(v_hat) + ADAM_EPS) + ADAM_WD * w_ref[...])
        m2_ref[...] = m2
        v2_ref[...] = v2

    spec = pl.BlockSpec((tr, C), lambda i: (i, 0))
    return pl.pallas_call(
        body, name=name, grid=(tiles,), in_specs=[spec] * 4, out_specs=[spec] * 3,
        out_shape=[jax.ShapeDtypeStruct((R, C), F32)] * 3,
        compiler_params=_params(("parallel",)),
    )(w, g, m, v)


def _pack_rows(pieces, rows):
    flat = jnp.concatenate([jnp.pad(p.reshape(-1).astype(F32), (0, (-p.size) % D_MODEL)) for p in pieces])
    return jnp.pad(flat, (0, rows * D_MODEL - flat.size)).reshape(rows, D_MODEL)


def _unpack_rows(pack, shapes):
    flat = pack.reshape(-1)
    out, off = [], 0
    for shp in shapes:
        size = int(np.prod(shp))
        out.append(flat[off:off + size].reshape(shp))
        off += size + (-size) % D_MODEL
    return out


def _kernel_order(w):
    parts = [w[:, 0:RET_W]]
    for p in range(FOX_HEADS // 2):
        parts += [w[:, RET_W + part * 512 + p * BLK:RET_W + part * 512 + (p + 1) * BLK] for part in range(3)]
    return jnp.concatenate(parts, axis=1)


def _reference_order(g_main, g_ff):
    parts = [g_main[:, 0:RET_W]]
    for part in range(3):
        parts += [g_main[:, RET_W + 384 * p + part * BLK:RET_W + 384 * p + (part + 1) * BLK] for p in range(FOX_HEADS // 2)]
    return jnp.concatenate(parts + [g_ff], axis=1)


def kernel(x, meta_tokens, attn_norm_g, w_in, fox_forget_b, ret_norm_g, w_out, ffn_norm_g, w_up, conv_w, conv_b, w_down, final_norm_g, loss_target, m_meta_tokens, m_attn_norm_g, m_w_in, m_fox_forget_b, m_ret_norm_g, m_w_out, m_ffn_norm_g, m_w_up, m_conv_w, m_conv_b, m_w_down, m_final_norm_g, v_meta_tokens, v_attn_norm_g, v_w_in, v_fox_forget_b, v_ret_norm_g, v_w_out, v_ffn_norm_g, v_w_up, v_conv_w, v_conv_b, v_w_down, v_final_norm_g):
    chip = 2 * lax.axis_index("x") + lax.axis_index("y")
    core = lax.axis_index("c")
    meta_w, conv_sw = meta_tokens.shape[1], conv_w.shape[2]

    small_w = _pack_rows([meta_tokens, conv_w[0]], 8)
    g_in, g_out, g_up, g_down, g_small = _chip_allgather(
        [w_in[0].astype(BF16), w_out[0].astype(BF16), w_up[0].astype(BF16), w_down[0].astype(BF16), small_w])
    w_in_full = g_in.transpose(1, 0, 2).reshape(D_MODEL, IN_WIDTH)
    w_main = _kernel_order(w_in_full)
    w_ff = jnp.pad(w_in_full[:, MAIN_W:], ((0, 0), (0, BLK - FOX_HEADS)))
    small_parts = [_unpack_rows(g_small[j], [meta_tokens.shape, conv_w.shape[1:]]) for j in range(N_CHIPS)]
    meta_full = jnp.concatenate([sp[0] for sp in small_parts], axis=1)
    conv_w_full = jnp.concatenate([sp[1] for sp in small_parts], axis=1)

    out = _local_step(x[0], loss_target[0], meta_full, attn_norm_g, w_main, w_ff, fox_forget_b, ret_norm_g,
                      g_out.reshape(D_MODEL, D_MODEL), ffn_norm_g, g_up, conv_w_full, conv_b,
                      g_down.reshape(D_FF, D_MODEL), final_norm_g[None])

    big = [_reference_order(out["w_main"], out["w_ff"]).reshape(D_MODEL, N_CHIPS, -1).transpose(1, 0, 2),
           out["w_out"].reshape(N_CHIPS, -1, D_MODEL), out["w_up"], out["w_down"].reshape(N_CHIPS, -1, D_MODEL)]
    small_shapes = [(1, D_MODEL), (1, D_MODEL), (1, D_MODEL), (1, 512 + FOX_HEADS + 1), (1, D_FF), (N_META, D_MODEL), (3, D_FF)]
    small = _pack_rows([out["attn_g"], out["ffn_g"], out["final_g"],
                        jnp.concatenate([out["ret_g"], out["fox_b"], out["loss"].reshape(1, 1)], axis=1),
                        out["conv_b"], out["dmeta"], out["conv_w"]], 32)
    *from_sibling, small_all = _sibling_exchange(big, small)
    core_idx = core.reshape(1).astype(jnp.int32)
    names = ("in", "out", "up", "down")
    chip_sums = [_pair_add(g, r, core_idx, "pair_add_" + nm) for g, r, nm in zip(big, from_sibling, names)]
    from_chips = _chip_reduce_scatter(chip_sums)
    halves = [_sum_slots(q, "sum_chips_" + nm) for q, nm in zip(from_chips, names)]
    grad_in, grad_out, grad_up, grad_down = _sibling_allgather(halves)
    s_attn, s_ffn, s_final, s_misc, s_conv_b, s_meta, s_conv_w = _unpack_rows(
        _sum_slots(small_all, "sum_small", tiles=1), small_shapes)
    loss = s_misc[0, 512 + FOX_HEADS]
    small_grads = [lax.dynamic_slice_in_dim(s_meta, chip * meta_w, meta_w, axis=1), s_attn, s_misc[:, 512:512 + FOX_HEADS],
                   s_misc[:, :512], s_ffn, lax.dynamic_slice_in_dim(s_conv_w, chip * conv_sw, conv_sw, axis=1)[None],
                   s_conv_b, s_final[0]]

    big_w = [(w_in, m_w_in, v_w_in, grad_in, "adamw_in"), (w_out, m_w_out, v_w_out, grad_out, "adamw_out"),
             (w_up, m_w_up, v_w_up, grad_up, "adamw_up"), (w_down, m_w_down, v_w_down, grad_down, "adamw_down")]
    big_res = [[g[None]] + [r[None] for r in _adamw(w[0], g, m[0], v[0], nm)] for w, m, v, g, nm in big_w]
    small_w_list = [meta_tokens, attn_norm_g, fox_forget_b, ret_norm_g, ffn_norm_g, conv_w, conv_b, final_norm_g]
    small_m = [m_meta_tokens, m_attn_norm_g, m_fox_forget_b, m_ret_norm_g, m_ffn_norm_g, m_conv_w, m_conv_b, m_final_norm_g]
    small_v = [v_meta_tokens, v_attn_norm_g, v_fox_forget_b, v_ret_norm_g, v_ffn_norm_g, v_conv_w, v_conv_b, v_final_norm_g]
    shapes = [a.shape for a in small_w_list]
    packs = [_pack_rows(lst, 16) for lst in (small_w_list, small_grads, small_m, small_v)]
    small_res = [_unpack_rows(r, shapes) for r in _adamw(*packs, "adamw_small", tiles=1)]
    small_grads = [g.reshape(s) for g, s in zip(small_grads, shapes)]

    def ordered(kind):
        sm = small_grads if kind == 0 else small_res[kind - 1]
        bg = [r[kind] for r in big_res]
        return [sm[0], sm[1], bg[0], sm[2], sm[3], bg[1], sm[4], bg[2], sm[5], sm[6], bg[3], sm[7]]

    return (loss, out["dx"][None], *ordered(0), *ordered(1), *ordered(2), *ordered(3))
```

```python
import functools

import numpy as np
import jax
import jax.numpy as jnp
from jax import lax
from jax.experimental import pallas as pl
from jax.experimental.pallas import tpu as pltpu

F32 = jnp.float32
BF16 = jnp.bfloat16

D_MODEL = 1024
N_META = 16
BLK = 128
UNIT = 2 * BLK
CHUNK = 64
N_PAD = BLK - N_META
PREFIX = BLK
RET_HEADS = 4
FOX_HEADS = 8
HEAD_LANES = 64
D_FF = 2816
ROPE_BASE = 10000.0
EPS = 1e-6
NEG = -1e30
RET_W = 1536
FOX_W = 1536
MAIN_W = RET_W + FOX_W
IN_WIDTH = MAIN_W + FOX_HEADS
N_CHIPS = 4
N_DEV = 8

ADAM_LR = 0.001
ADAM_B1 = 0.9
ADAM_B2 = 0.999
ADAM_EPS = 1e-08
ADAM_WD = 0.01
ADAM_STEP = 10

MESH = pl.DeviceIdType.MESH
VMEM_LIMIT_MB = 56

_NT = (((1,), (1,)), ((), ()))
_TN = (((0,), (0,)), ((), ()))


def _dot(a, b):
    return jnp.dot(a, b, preferred_element_type=F32)


def _dot_nt(a, b):
    return lax.dot_general(a, b, _NT, preferred_element_type=F32)


def _dot_tn(a, b):
    return lax.dot_general(a, b, _TN, preferred_element_type=F32)


def _params(dims=None, vmem_mb=VMEM_LIMIT_MB):
    kw = dict(vmem_limit_bytes=vmem_mb << 20)
    if dims is not None:
        kw["dimension_semantics"] = dims
    return pltpu.CompilerParams(**kw)


def _row_tile(n, prefs=(384, 256, 128)):
    for t in prefs:
        if n % t == 0:
            return t
    raise ValueError(f"no row tile for {n}")


def _iota(shape, dim):
    return lax.broadcasted_iota(jnp.int32, shape, dim)


def _pick_row(tile, row):
    sub = _iota(tile.shape, 0)
    return jnp.sum(jnp.where(sub == row, tile, 0.0), axis=0, keepdims=True)


def _split3(x):
    hi = x.astype(BF16)
    r1 = x - hi.astype(F32)
    mid = r1.astype(BF16)
    lo = (r1 - mid.astype(F32)).astype(BF16)
    return hi, mid, lo


def _full(shape):
    nd = len(shape)
    return pl.BlockSpec(shape, lambda *_: (0,) * nd)


def _in_perm():
    cols = list(range(RET_W))
    for p in range(FOX_HEADS // 2):
        for part in range(3):
            start = RET_W + part * 512 + p * BLK
            cols += list(range(start, start + BLK))
    return np.asarray(cols, np.int32)


def _rotary_tables(L):
    half = HEAD_LANES // 2
    inv = 1.0 / (ROPE_BASE ** (jnp.arange(half, dtype=F32) / half))
    ang = jnp.arange(L).astype(F32)[:, None] * inv[None, :]
    cos, sin = jnp.cos(ang), jnp.sin(ang)
    cos_t = jnp.tile(cos, (1, 4))
    sin_t = jnp.tile(jnp.concatenate([-sin, sin], axis=1), (1, 2))
    return cos_t, sin_t


def _decay_tables():
    gam = 1.0 - 2.0 ** (-5.0 - np.arange(RET_HEADS, dtype=np.float64))
    n = np.arange(BLK)
    same_or_past = (n[:, None] // CHUNK) >= (n[None, :] // CHUNK)
    dist = np.abs(n[:, None] - n[None, :])
    dmat = np.stack([np.where(same_or_past, g ** dist, 0.0) for g in gam]).astype(np.float32)
    lane_head = np.arange(BLK) // HEAD_LANES
    wq = np.stack([gam[2 * p + lane_head][None, :] ** (n[:, None] + 1.0) for p in range(2)]).astype(np.float32)
    wk = np.stack([gam[2 * p + lane_head][None, :] ** (BLK - 1.0 - n[:, None]) for p in range(2)]).astype(np.float32)
    g_blk = tuple(float(g ** BLK) for g in gam)
    return jnp.asarray(dmat), jnp.asarray(wq), jnp.asarray(wk), g_blk


def _rms_inproj(h0, g, w_main, w_ff):
    L = h0.shape[0]
    tm = _row_tile(L)

    def body(h_ref, g_ref, wm_ref, wf_ref, n_ref, p_ref, ff_ref):
        h = h_ref[...]
        r = lax.rsqrt(jnp.mean(h * h, axis=-1, keepdims=True) + EPS)
        n = (h * r * g_ref[...]).astype(BF16)
        n_ref[...] = n
        p_ref[...] = _dot(n, wm_ref[...]).astype(BF16)
        ff_ref[...] = _dot(n, wf_ref[...])

    return pl.pallas_call(
        body, name="f_inproj", grid=(L // tm,),
        in_specs=[pl.BlockSpec((tm, D_MODEL), lambda i: (i, 0)), _full((1, D_MODEL)),
                  _full((D_MODEL, MAIN_W)), _full((D_MODEL, BLK))],
        out_specs=[pl.BlockSpec((tm, D_MODEL), lambda i: (i, 0)), pl.BlockSpec((tm, MAIN_W), lambda i: (i, 0)),
                   pl.BlockSpec((tm, BLK), lambda i: (i, 0))],
        out_shape=[jax.ShapeDtypeStruct((L, D_MODEL), BF16), jax.ShapeDtypeStruct((L, MAIN_W), BF16),
                   jax.ShapeDtypeStruct((L, BLK), F32)],
        compiler_params=_params(("parallel",)),
    )(h0, g, w_main, w_ff)


def _fox_prep(ff, fb):
    L = ff.shape[0]
    nblk = L // BLK

    def body(ff_ref, b_ref, c_ref, ct_ref, carry):
        i = pl.program_id(0)

        @pl.when(i == 0)
        def _():
            carry[...] = jnp.zeros_like(carry)

        z = ff_ref[...] + b_ref[...]
        lf = jnp.minimum(z, 0.0) - jnp.log1p(jnp.exp(-jnp.abs(z)))
        lf = jnp.where(_iota((BLK, BLK), 1) < FOX_HEADS, lf, 0.0)
        tri = (_iota((BLK, BLK), 0) >= _iota((BLK, BLK), 1)).astype(BF16)
        hi, mid, lo = _split3(lf)
        cs = _dot(tri, hi) + _dot(tri, mid) + _dot(tri, lo) + carry[...]
        c_ref[...] = cs
        ct_ref[0] = cs.T[0:8, :]
        carry[...] = carry[...] + jnp.sum(lf, axis=0, keepdims=True)

    return pl.pallas_call(
        body, name="f_foxprep", grid=(nblk,),
        in_specs=[pl.BlockSpec((BLK, BLK), lambda i: (i, 0)), _full((1, BLK))],
        out_specs=[pl.BlockSpec((BLK, BLK), lambda i: (i, 0)), pl.BlockSpec((1, 8, BLK), lambda i: (i, 0, 0))],
        out_shape=[jax.ShapeDtypeStruct((L, BLK), F32), jax.ShapeDtypeStruct((nblk, 8, BLK), F32)],
        scratch_shapes=[pltpu.VMEM((1, BLK), F32)],
        compiler_params=_params(("arbitrary",)),
    )(ff, fb)


def _rot_fns(cos, sin):
    lane = _iota((BLK, BLK), 1)
    first = (lane & (HEAD_LANES - 1)) < HEAD_LANES // 2

    def swap(x):
        return jnp.where(first, pltpu.roll(x, BLK - 32, 1), pltpu.roll(x, 32, 1))

    def rot(x):
        return x * cos + swap(x) * sin

    def rot_t(dy):
        return dy * cos + swap(dy * sin)

    return rot, rot_t


def _retention_fwd(proj, cos_t, sin_t, ret_g):
    L = proj.shape[0]
    nblk = L // BLK
    dmat, wq_t, wk_t, g_blk = _decay_tables()

    def body(q_ref, k_ref, v_ref, gate_ref, cos_ref, sin_ref, d_ref, wq_ref, wk_ref, rg_ref,
             mix_ref, o_ref, rs_ref, state):
        i = pl.program_id(0)

        @pl.when(i == 0)
        def _():
            state[...] = jnp.zeros_like(state)

        rot, _ = _rot_fns(cos_ref[...], sin_ref[...])
        lane = _iota((BLK, BLK), 1)
        sub = _iota((BLK, BLK), 0)
        for p in range(2):
            qr = rot(q_ref[:, p * BLK:(p + 1) * BLK].astype(F32))
            kr = rot(k_ref[:, p * BLK:(p + 1) * BLK].astype(F32)) * (HEAD_LANES ** -0.5)
            kr_b = kr.astype(BF16)
            qw = (qr * wq_ref[p]).astype(BF16)
            kw = (kr * wk_ref[p]).astype(BF16)
            for e in range(2):
                h = 2 * p + e
                cols = slice(h * BLK, (h + 1) * BLK)
                qm = jnp.where((lane >> 6) == e, qr, 0.0).astype(BF16)
                s = _dot_nt(qm, kr_b) * d_ref[h]
                vh = v_ref[:, cols]
                st = state[h]
                rs_ref[0, h] = st
                o = _dot(s.astype(BF16), vh) + _dot(qw, st.astype(BF16))
                u = jnp.where((sub >> 6) == e, _dot_tn(kw, vh), 0.0)
                state[h] = g_blk[h] * st + u
                rn = lax.rsqrt(jnp.mean(o * o, axis=-1, keepdims=True) + EPS)
                gate = gate_ref[:, cols].astype(F32)
                o_ref[:, cols] = o
                mix_ref[:, cols] = (o * rn * rg_ref[:, cols] * (gate * jax.nn.sigmoid(gate))).astype(BF16)

    row = lambda c: (lambda i: (i, c))
    return pl.pallas_call(
        body, name="f_retention", grid=(nblk,),
        in_specs=[pl.BlockSpec((BLK, 256), row(0)), pl.BlockSpec((BLK, 256), row(1)),
                  pl.BlockSpec((BLK, 512), row(1)), pl.BlockSpec((BLK, 512), row(2)),
                  pl.BlockSpec((BLK, BLK), row(0)), pl.BlockSpec((BLK, BLK), row(0)),
                  _full((RET_HEADS, BLK, BLK)), _full((2, BLK, BLK)), _full((2, BLK, BLK)), _full((1, 512))],
        out_specs=[pl.BlockSpec((BLK, 512), row(0)), pl.BlockSpec((BLK, 512), row(0)),
                   pl.BlockSpec((1, RET_HEADS, BLK, BLK), lambda i: (i, 0, 0, 0))],
        out_shape=[jax.ShapeDtypeStruct((L, 512), BF16), jax.ShapeDtypeStruct((L, 512), F32),
                   jax.ShapeDtypeStruct((nblk, RET_HEADS, BLK, BLK), F32)],
        scratch_shapes=[pltpu.VMEM((RET_HEADS, BLK, BLK), F32)],
        compiler_params=_params(("arbitrary",)),
    )(proj, proj, proj, proj, cos_t, sin_t, dmat, wq_t, wk_t, ret_g)


def _fox_units(L):
    nblk = L // BLK
    assert L % BLK == 0 and nblk % 2 == 1, "sequence must be one 128-row block plus whole 256-row tiles"
    return nblk, (nblk - 1) // 2


def _fox_tile_masks():
    sub, lane = _iota((BLK, BLK), 0), _iota((BLK, BLK), 1)
    return dict(first=(sub <= lane) & (sub >= N_PAD), valid=_iota((BLK, UNIT), 0) >= N_PAD,
                diag=_iota((UNIT, UNIT), 0) <= _iota((UNIT, UNIT), 1))


def _fox_fwd(proj, c, ctb):
    L = proj.shape[0]
    nblk, nu = _fox_units(L)
    scale = HEAD_LANES ** -0.5

    def body(qkv_ref, c_ref, ct_ref, of_ref, lse_ref, vt, csb):
        p = pl.program_id(0)

        @pl.when(p == 0)
        def _():
            lse_ref[...] = jnp.zeros_like(lse_ref)

        lane = _iota((BLK, BLK), 1)
        sub8 = _iota((8, BLK), 0)
        masks = _fox_tile_masks()

        def pre(j, carry):
            off = pl.multiple_of(j * BLK, BLK)
            vt[j] = qkv_ref[pl.ds(off, BLK), 2 * BLK:3 * BLK].astype(F32).T.astype(BF16)
            ct = c_ref[pl.ds(off, BLK), :]
            for e in range(2):
                col = jnp.sum(jnp.where(lane == 2 * p + e, ct, 0.0), axis=1, keepdims=True)
                csb[e, j] = jnp.broadcast_to(col, (BLK, UNIT))
            return carry

        lax.fori_loop(0, nblk, pre, 0)

        def attend(qblk, nq, n_whole):
            qlen = nq * BLK
            qoff = pl.multiple_of(qblk * BLK, BLK)
            qs = qkv_ref[pl.ds(qoff, qlen), 0:BLK].astype(F32) * scale
            qlane = _iota((qlen, BLK), 1)
            qm = [jnp.where((qlane >> 6) == e, qs, 0.0).astype(BF16) for e in range(2)]
            ct_row = [jnp.concatenate([_pick_row(ct_ref[qblk + a], 2 * p + e) for a in range(nq)], axis=1)
                      for e in range(2)]

            def step(kblk, nk, mask, st):
                koff = pl.multiple_of(kblk * BLK, BLK)
                kt = qkv_ref[pl.ds(koff, nk * BLK), BLK:2 * BLK]
                out = []
                for e in range(2):
                    m, l, acc = st[3 * e:3 * e + 3]
                    s = _dot_nt(kt, qm[e])
                    t = jnp.concatenate([s[b * BLK:(b + 1) * BLK] - csb[e, kblk + b, :, 0:qlen] for b in range(nk)], axis=0)
                    if mask is not None:
                        t = jnp.where(mask, t, NEG)
                    m_new = jnp.maximum(m, jnp.max(t, axis=0, keepdims=True) + ct_row[e])
                    alpha = jnp.exp(m - m_new)
                    pr = jnp.exp(t - (m_new - ct_row[e]))
                    l = alpha * l + jnp.sum(pr, axis=0, keepdims=True)
                    pr_b = pr.astype(BF16)
                    pv = _dot(vt[kblk, e * HEAD_LANES:(e + 1) * HEAD_LANES, :], pr_b[0:BLK])
                    for b in range(1, nk):
                        pv = pv + _dot(vt[kblk + b, e * HEAD_LANES:(e + 1) * HEAD_LANES, :], pr_b[b * BLK:(b + 1) * BLK])
                    out += [m_new, l, alpha * acc + pv]
                return tuple(out)

            st = (jnp.full((1, qlen), NEG, F32), jnp.zeros((1, qlen), F32), jnp.zeros((HEAD_LANES, qlen), F32)) * 2
            if nq == 1:
                st = step(0, 1, masks["first"], st)
            else:
                st = step(0, 1, masks["valid"], st)
                st = lax.fori_loop(0, n_whole, lambda j, s_: step(1 + 2 * j, 2, None, s_), st)
                st = step(qblk, 2, masks["diag"], st)
            o_t = jnp.concatenate([st[2] * (1.0 / st[1]), st[5] * (1.0 / st[4])], axis=0)
            of_ref[pl.ds(qoff, qlen), :] = o_t.T.astype(BF16)
            lse = [st[3 * e] + jnp.log(st[3 * e + 1]) for e in range(2)]
            for a in range(nq):
                rows = [lse[e][:, a * BLK:(a + 1) * BLK] for e in range(2)]
                lse_ref[qblk + a] = lse_ref[qblk + a] + (
                    jnp.where(sub8 == 2 * p, rows[0], 0.0) + jnp.where(sub8 == 2 * p + 1, rows[1], 0.0))

        attend(0, 1, 0)

        def q_loop(u, carry):
            attend(1 + 2 * u, 2, u)
            return carry

        lax.fori_loop(0, nu, q_loop, 0)

    return pl.pallas_call(
        body, name="f_fox", grid=(FOX_HEADS // 2,),
        in_specs=[pl.BlockSpec((L, 384), lambda p: (0, RET_W // 384 + p)), _full((L, BLK)), _full((nblk, 8, BLK))],
        out_specs=[pl.BlockSpec((L, BLK), lambda p: (0, p)), _full((nblk, 8, BLK))],
        out_shape=[jax.ShapeDtypeStruct((L, 512), BF16), jax.ShapeDtypeStruct((nblk, 8, BLK), F32)],
        scratch_shapes=[pltpu.VMEM((nblk, BLK, BLK), BF16), pltpu.VMEM((2, nblk, BLK, UNIT), F32)],
        compiler_params=_params(("arbitrary",)),
    )(proj, c, ctb)


def _outproj_up(mix_r, o_f, h0, w_out, ffn_g, w_up):
    L = h0.shape[0]
    tm = _row_tile(L)
    shard = w_up.shape[2]

    def body(mr_ref, of_ref, h0_ref, wo_ref, g_ref, wu_ref, h1_ref, n2_ref, up_ref):
        h1 = h0_ref[...] + _dot(mr_ref[...], wo_ref[0:512, :]) + _dot(of_ref[...], wo_ref[512:1024, :])
        h1_ref[...] = h1
        r = lax.rsqrt(jnp.mean(h1 * h1, axis=-1, keepdims=True) + EPS)
        n2 = (h1 * r * g_ref[...]).astype(BF16)
        n2_ref[...] = n2
        for j in range(N_CHIPS):
            up_ref[:, j * shard:(j + 1) * shard] = _dot(n2, wu_ref[j]).astype(BF16)

    rows = lambda w: pl.BlockSpec((tm, w), lambda i: (i, 0))
    return pl.pallas_call(
        body, name="f_outproj_up", grid=(L // tm,),
        in_specs=[rows(512), rows(512), rows(D_MODEL), _full((D_MODEL, D_MODEL)), _full((1, D_MODEL)),
                  _full((N_CHIPS, D_MODEL, shard))],
        out_specs=[rows(D_MODEL), rows(D_MODEL), rows(2 * D_FF)],
        out_shape=[jax.ShapeDtypeStruct((L, D_MODEL), F32), jax.ShapeDtypeStruct((L, D_MODEL), BF16),
                   jax.ShapeDtypeStruct((L, 2 * D_FF), BF16)],
        compiler_params=_params(("parallel",)),
    )(mix_r, o_f, h0, w_out, ffn_g, w_up)


def _conv_acc(a_ref, halo_ref, cw_refs, cb_ref, i, tm):
    sub = _iota((tm, 1), 0)
    a = jnp.where(i * tm + sub >= N_PAD, a_ref[...].astype(F32), 0.0)
    halo = halo_ref[...].astype(F32)
    hrow = i * tm - 8 + _iota((8, 1), 0)
    halo = jnp.where((hrow >= N_PAD) & (i > 0), halo, 0.0)
    a1 = jnp.where(sub == 0, _pick_row(halo, 7), pltpu.roll(a, 1, 0))
    a2 = jnp.where(sub == 0, _pick_row(halo, 6), jnp.where(sub == 1, _pick_row(halo, 7), pltpu.roll(a, 2, 0)))
    acc = cb_ref[...] + a2 * cw_refs[0][...]
    acc = acc + a1 * cw_refs[1][...]
    acc = acc + a * cw_refs[2][...]
    return a, a1, a2, acc


def _ffn_down_loss(up, conv_w, conv_b, w_down, h1, final_g, target):
    L = h1.shape[0]
    tm = _row_tile(L)
    cw = [conv_w[j:j + 1] for j in range(3)]

    def body(a_ref, halo_ref, b_ref, cw0, cw1, cw2, cb_ref, wd_ref, h1_ref, gf_ref, t_ref,
             g_ref, dh_ref, dhb_ref, dgf_ref, loss_ref):
        i = pl.program_id(0)

        @pl.when(i == 0)
        def _():
            dgf_ref[...] = jnp.zeros_like(dgf_ref)
            loss_ref[...] = jnp.zeros_like(loss_ref)

        _, _, _, acc = _conv_acc(a_ref, halo_ref, (cw0, cw1, cw2), cb_ref, i, tm)
        g = (acc * jax.nn.sigmoid(acc) * b_ref[...].astype(F32)).astype(BF16)
        g_ref[...] = g
        h2 = h1_ref[...] + _dot(g, wd_ref[...])
        r = lax.rsqrt(jnp.mean(h2 * h2, axis=-1, keepdims=True) + EPS)
        yn = h2 * r
        gf = gf_ref[...]
        live = i * tm + _iota((tm, 1), 0) >= PREFIX
        err = jnp.where(live, yn * gf - t_ref[...], 0.0)
        loss_ref[...] = loss_ref[...] + 0.5 * jnp.sum(jnp.mean(err * err, axis=-1, keepdims=True))
        dy = err * (1.0 / D_MODEL)
        dgf_ref[...] = dgf_ref[...] + jnp.sum(dy * yn, axis=0, keepdims=True)
        dyn = dy * gf
        dh = r * (dyn - yn * jnp.mean(dyn * yn, axis=-1, keepdims=True))
        dh_ref[...] = dh
        dhb_ref[...] = dh.astype(BF16)

    rows = lambda w, c=0: pl.BlockSpec((tm, w), lambda i: (i, c))
    halo = pl.BlockSpec((8, D_FF), lambda i: (jnp.maximum(i * (tm // 8) - 1, 0), 0))
    return pl.pallas_call(
        body, name="f_ffn_down_loss", grid=(L // tm,),
        in_specs=[rows(D_FF), halo, rows(D_FF, 1), _full((1, D_FF)), _full((1, D_FF)), _full((1, D_FF)),
                  _full((1, D_FF)), _full((D_FF, D_MODEL)), rows(D_MODEL), _full((1, D_MODEL)), rows(D_MODEL)],
        out_specs=[rows(D_FF), rows(D_MODEL), rows(D_MODEL), _full((1, D_MODEL)), _full((1, BLK))],
        out_shape=[jax.ShapeDtypeStruct((L, D_FF), BF16), jax.ShapeDtypeStruct((L, D_MODEL), F32),
                   jax.ShapeDtypeStruct((L, D_MODEL), BF16), jax.ShapeDtypeStruct((1, D_MODEL), F32),
                   jax.ShapeDtypeStruct((1, BLK), F32)],
        compiler_params=_params(("arbitrary",)),
    )(up, up, up, cw[0], cw[1], cw[2], conv_b, w_down, h1, final_g, target)


def _ffn_bwd_gate(dh2b, w_down, up, conv_w, conv_b):
    L = dh2b.shape[0]
    tm = _row_tile(L)
    cw = [conv_w[j:j + 1] for j in range(3)]

    def body(dh_ref, wd_ref, a_ref, halo_ref, b_ref, cw0, cw1, cw2, cb_ref, dacc_ref, db_ref, dcw_ref):
        i = pl.program_id(0)

        @pl.when(i == 0)
        def _():
            dcw_ref[...] = jnp.zeros_like(dcw_ref)

        a, a1, a2, acc = _conv_acc(a_ref, halo_ref, (cw0, cw1, cw2), cb_ref, i, tm)
        dg = _dot_nt(dh_ref[...], wd_ref[...])
        sg = jax.nn.sigmoid(acc)
        db_ref[...] = (dg * acc * sg).astype(BF16)
        dacc = dg * b_ref[...].astype(F32) * (sg * (1.0 + acc * (1.0 - sg)))
        dacc_ref[...] = dacc.astype(BF16)
        sub8 = _iota((8, 1), 0)
        rows = [jnp.sum(dacc * t, axis=0, keepdims=True) for t in (a2, a1, a)] + [jnp.sum(dacc, axis=0, keepdims=True)]
        upd = jnp.zeros((8, D_FF), F32)
        for j, rj in enumerate(rows):
            upd = upd + jnp.where(sub8 == j, rj, 0.0)
        dcw_ref[...] = dcw_ref[...] + upd

    rows = lambda w, c=0: pl.BlockSpec((tm, w), lambda i: (i, c))
    halo = pl.BlockSpec((8, D_FF), lambda i: (jnp.maximum(i * (tm // 8) - 1, 0), 0))
    return pl.pallas_call(
        body, name="b_ffn_gate", grid=(L // tm,),
        in_specs=[rows(D_MODEL), _full((D_FF, D_MODEL)), rows(D_FF), halo, rows(D_FF, 1),
                  _full((1, D_FF)), _full((1, D_FF)), _full((1, D_FF)), _full((1, D_FF))],
        out_specs=[rows(D_FF), rows(D_FF), _full((8, D_FF))],
        out_shape=[jax.ShapeDtypeStruct((L, D_FF), BF16), jax.ShapeDtypeStruct((L, D_FF), BF16),
                   jax.ShapeDtypeStruct((8, D_FF), F32)],
        compiler_params=_params(("arbitrary",)),
    )(dh2b, w_down, up, up, up, cw[0], cw[1], cw[2], conv_b)


def _ffn_bwd_up(dacc, db, conv_w, w_up, h1, ffn_g, dh2, w_out):
    L = h1.shape[0]
    tm = _row_tile(L)
    nt = L // tm
    shard = w_up.shape[2]
    cw = [conv_w[j:j + 1] for j in range(3)]

    def body(da_ref, halo_ref, db_ref, cw0, cw1, cw2, wu_ref, h1_ref, g_ref, dh2_ref, wo_ref,
             dup_ref, dh1_ref, dh1b_ref, dmix_ref, dg_ref):
        i = pl.program_id(0)

        @pl.when(i == 0)
        def _():
            dg_ref[...] = jnp.zeros_like(dg_ref)

        sub = _iota((tm, 1), 0)
        d0 = da_ref[...].astype(F32)
        halo = jnp.where(i < nt - 1, halo_ref[...].astype(F32), 0.0)
        d1 = jnp.where(sub == tm - 1, _pick_row(halo, 0), pltpu.roll(d0, tm - 1, 0))
        d2 = jnp.where(sub == tm - 2, _pick_row(halo, 0),
                       jnp.where(sub == tm - 1, _pick_row(halo, 1), pltpu.roll(d0, tm - 2, 0)))
        da = d0 * cw2[...] + d1 * cw1[...] + d2 * cw0[...]
        da = jnp.where(i * tm + sub >= N_PAD, da, 0.0).astype(BF16)
        dup_ref[:, 0:D_FF] = da
        dbv = db_ref[...]
        dup_ref[:, D_FF:2 * D_FF] = dbv
        dn = jnp.zeros((tm, D_MODEL), F32)
        for j in range(N_CHIPS):
            src = da if j < 2 else dbv
            lo = (j % 2) * shard
            dn = dn + _dot_nt(src[:, lo:lo + shard], wu_ref[j])
        h1 = h1_ref[...]
        r = lax.rsqrt(jnp.mean(h1 * h1, axis=-1, keepdims=True) + EPS)
        yn = h1 * r
        dg_ref[...] = dg_ref[...] + jnp.sum(dn * yn, axis=0, keepdims=True)
        dyn = dn * g_ref[...]
        dh1 = dh2_ref[...] + r * (dyn - yn * jnp.mean(dyn * yn, axis=-1, keepdims=True))
        dh1_ref[...] = dh1
        dh1b = dh1.astype(BF16)
        dh1b_ref[...] = dh1b
        dmix_ref[...] = _dot_nt(dh1b, wo_ref[...]).astype(BF16)

    rows = lambda w: pl.BlockSpec((tm, w), lambda i: (i, 0))
    halo = pl.BlockSpec((8, D_FF), lambda i: (jnp.minimum((i + 1) * (tm // 8), L // 8 - 1), 0))
    return pl.pallas_call(
        body, name="b_ffn_up", grid=(nt,),
        in_specs=[rows(D_FF), halo, rows(D_FF), _full((1, D_FF)), _full((1, D_FF)), _full((1, D_FF)),
                  _full((N_CHIPS, D_MODEL, shard)), rows(D_MODEL), _full((1, D_MODEL)), rows(D_MODEL),
                  _full((D_MODEL, D_MODEL))],
        out_specs=[rows(2 * D_FF), rows(D_MODEL), rows(D_MODEL), rows(D_MODEL), _full((1, D_MODEL))],
        out_shape=[jax.ShapeDtypeStruct((L, 2 * D_FF), BF16), jax.ShapeDtypeStruct((L, D_MODEL), F32),
                   jax.ShapeDtypeStruct((L, D_MODEL), BF16), jax.ShapeDtypeStruct((L, D_MODEL), BF16),
                   jax.ShapeDtypeStruct((1, D_MODEL), F32)],
        compiler_params=_params(("arbitrary",)),
    )(dacc, dacc, db, cw[0], cw[1], cw[2], w_up, h1, ffn_g, dh2, w_out)


def _wgrad(a, b, name, tn=None):
    L, K = a.shape
    N = b.shape[1]
    tn = N if tn is None else tn
    tl = _row_tile(L)

    def body(a_ref, b_ref, o_ref):
        @pl.when(pl.program_id(1) == 0)
        def _():
            o_ref[...] = jnp.zeros_like(o_ref)

        o_ref[0] = o_ref[0] + _dot_tn(a_ref[...], b_ref[...])

    return pl.pallas_call(
        body, name=name, grid=(N // tn, L // tl),
        in_specs=[pl.BlockSpec((tl, K), lambda n, l: (l, 0)), pl.BlockSpec((tl, tn), lambda n, l: (l, n))],
        out_specs=pl.BlockSpec((1, K, tn), lambda n, l: (n, 0, 0)),
        out_shape=jax.ShapeDtypeStruct((N // tn, K, tn), F32),
        compiler_params=_params(("parallel", "arbitrary")),
    )(a, b)


def _retention_bwd(dmix, o, proj, cos_t, sin_t, ret_g, states):
    L = proj.shape[0]
    nblk = L // BLK
    dmat, wq_t, wk_t, g_blk = _decay_tables()

    def body(dm_ref, o_ref, q_ref, k_ref, v_ref, gate_ref, cos_ref, sin_ref, d_ref, wq_ref, wk_ref, rg_ref, rs_ref,
             dp_ref, drg_ref, gstate):
        i = pl.program_id(0)

        @pl.when(i == 0)
        def _():
            gstate[...] = jnp.zeros_like(gstate)
            drg_ref[...] = jnp.zeros_like(drg_ref)

        rot, rot_t = _rot_fns(cos_ref[...], sin_ref[...])
        lane = _iota((BLK, BLK), 1)
        sub = _iota((BLK, BLK), 0)
        scale = HEAD_LANES ** -0.5
        for p in range(2):
            qr = rot(q_ref[:, p * BLK:(p + 1) * BLK].astype(F32))
            kr = rot(k_ref[:, p * BLK:(p + 1) * BLK].astype(F32)) * scale
            kr_b = kr.astype(BF16)
            qw = (qr * wq_ref[p]).astype(BF16)
            kw = (kr * wk_ref[p]).astype(BF16)
            dqr = jnp.zeros((BLK, BLK), F32)
            dkr = jnp.zeros((BLK, BLK), F32)
            for e in range(2):
                h = 2 * p + e
                cols = slice(h * BLK, (h + 1) * BLK)
                head_lanes = (lane >> 6) == e
                o = o_ref[:, cols]
                rn = lax.rsqrt(jnp.mean(o * o, axis=-1, keepdims=True) + EPS)
                y = o * rn
                gate = gate_ref[:, cols].astype(F32)
                sg = jax.nn.sigmoid(gate)
                dm = dm_ref[:, cols].astype(F32)
                rgain = rg_ref[:, cols]
                drg_ref[:, cols] = drg_ref[:, cols] + jnp.sum(dm * y * (gate * sg), axis=0, keepdims=True)
                dp_ref[:, 1024 + h * BLK:1024 + (h + 1) * BLK] = (
                    dm * y * rgain * (sg * (1.0 + gate * (1.0 - sg)))).astype(BF16)
                dy = dm * rgain * (gate * sg)
                do = (rn * (dy - y * jnp.mean(dy * y, axis=-1, keepdims=True))).astype(BF16)
                vh = v_ref[:, cols]
                qm = jnp.where(head_lanes, qr, 0.0).astype(BF16)
                dmh = d_ref[h]
                s = (_dot_nt(qm, kr_b) * dmh).astype(BF16)
                ds = (_dot_nt(do, vh) * dmh).astype(BF16)
                st = rs_ref[0, h].astype(BF16)
                gs = gstate[h]
                gs_b = gs.astype(BF16)
                dqr = dqr + jnp.where(head_lanes, _dot(ds, kr_b), 0.0) + _dot_nt(do, st) * wq_ref[p]
                dkr = dkr + _dot_tn(ds, qm) + _dot_nt(vh, gs_b) * wk_ref[p]
                dp_ref[:, 512 + h * BLK:512 + (h + 1) * BLK] = (_dot_tn(s, do) + _dot(kw, gs_b)).astype(BF16)
                dr = jnp.where((sub >> 6) == e, _dot_tn(qw, do), 0.0)
                gstate[h] = dr + g_blk[h] * gs
            dp_ref[:, p * BLK:(p + 1) * BLK] = rot_t(dqr).astype(BF16)
            dp_ref[:, 256 + p * BLK:256 + (p + 1) * BLK] = (rot_t(dkr) * scale).astype(BF16)

    row = lambda c: (lambda i: (nblk - 1 - i, c))
    return pl.pallas_call(
        body, name="b_retention", grid=(nblk,),
        in_specs=[pl.BlockSpec((BLK, 512), row(0)), pl.BlockSpec((BLK, 512), row(0)),
                  pl.BlockSpec((BLK, 256), row(0)), pl.BlockSpec((BLK, 256), row(1)),
                  pl.BlockSpec((BLK, 512), row(1)), pl.BlockSpec((BLK, 512), row(2)),
                  pl.BlockSpec((BLK, BLK), row(0)), pl.BlockSpec((BLK, BLK), row(0)),
                  _full((RET_HEADS, BLK, BLK)), _full((2, BLK, BLK)), _full((2, BLK, BLK)), _full((1, 512)),
                  pl.BlockSpec((1, RET_HEADS, BLK, BLK), lambda i: (nblk - 1 - i, 0, 0, 0))],
        out_specs=[pl.BlockSpec((BLK, RET_W), row(0)), _full((1, 512))],
        out_shape=[jax.ShapeDtypeStruct((L, RET_W), BF16), jax.ShapeDtypeStruct((1, 512), F32)],
        scratch_shapes=[pltpu.VMEM((RET_HEADS, BLK, BLK), F32)],
        compiler_params=_params(("arbitrary",)),
    )(dmix, o, proj, proj, proj, proj, cos_t, sin_t, dmat, wq_t, wk_t, ret_g, states)


def _fox_delta(dmix, o_f):
    L = o_f.shape[0]
    nblk = L // BLK

    def body(do_ref, o_ref, d_ref):
        prod = do_ref[...].astype(F32) * o_ref[...].astype(F32)
        sel = ((_iota((8, 512), 1) >> 6) == _iota((8, 512), 0)).astype(BF16)
        hi = prod.astype(BF16)
        lo = (prod - hi.astype(F32)).astype(BF16)
        d_ref[0] = _dot_nt(sel, hi) + _dot_nt(sel, lo)

    return pl.pallas_call(
        body, name="b_foxdelta", grid=(nblk,),
        in_specs=[pl.BlockSpec((BLK, 512), lambda i: (i, 1)), pl.BlockSpec((BLK, 512), lambda i: (i, 0))],
        out_specs=pl.BlockSpec((1, 8, BLK), lambda i: (i, 0, 0)),
        out_shape=jax.ShapeDtypeStruct((nblk, 8, BLK), F32),
        compiler_params=_params(("parallel",)),
    )(dmix, o_f)


def _fox_bwd(proj, dmix, c, ctb, lse, delta):
    L = proj.shape[0]
    nblk, nu = _fox_units(L)
    scale = HEAD_LANES ** -0.5

    def body(qkv_ref, do_ref, c_ref, ct_ref, lse_ref, dl_ref, dp_ref, dc_ref, dcq_ref,
             ktt, dqt, dk_acc, dv_acc, dcs_acc):
        p = pl.program_id(0)

        @pl.when(p == 0)
        def _():
            dc_ref[...] = jnp.zeros_like(dc_ref)
            dcq_ref[...] = jnp.zeros_like(dcq_ref)

        lane = _iota((BLK, BLK), 1)
        sub8 = _iota((8, BLK), 0)
        masks = _fox_tile_masks()

        def pre(j, carry):
            off = pl.multiple_of(j * BLK, BLK)
            ktt[j] = qkv_ref[pl.ds(off, BLK), BLK:2 * BLK].astype(F32).T.astype(BF16)
            dqt[j] = jnp.zeros((BLK, BLK), F32)
            return carry

        lax.fori_loop(0, nblk, pre, 0)

        def kv_pass(kblk, nk, n_later):
            klen = nk * BLK
            koff = pl.multiple_of(kblk * BLK, BLK)
            kt = qkv_ref[pl.ds(koff, klen), BLK:2 * BLK]
            vtile = qkv_ref[pl.ds(koff, klen), 2 * BLK:3 * BLK]
            ct = c_ref[pl.ds(koff, klen), :]
            klane = _iota((klen, BLK), 1)
            cs = [jnp.broadcast_to(jnp.sum(jnp.where(klane == 2 * p + e, ct, 0.0), axis=1, keepdims=True), (klen, UNIT))
                  for e in range(2)]
            dk_acc[0:klen] = jnp.zeros((klen, BLK), F32)
            dv_acc[0:klen] = jnp.zeros((klen, BLK), F32)
            for e in range(2):
                dcs_acc[e, 0:klen] = jnp.zeros((klen, BLK), F32)

            def tile(qblk, nq, mask):
                qlen = nq * BLK
                qoff = pl.multiple_of(qblk * BLK, BLK)
                qs = qkv_ref[pl.ds(qoff, qlen), 0:BLK].astype(F32) * scale
                dot_ = do_ref[pl.ds(qoff, qlen), :]
                qlane = _iota((qlen, BLK), 1)
                stats = [[ref[qblk + a] for a in range(nq)] for ref in (ct_ref, lse_ref, dl_ref)]
                for e in range(2):
                    h = 2 * p + e
                    head = (qlane >> 6) == e
                    ct_row, lse_row, dl_row = [jnp.concatenate([_pick_row(t, h) for t in ts], axis=1) for ts in stats]
                    qm = jnp.where(head, qs, 0.0).astype(BF16)
                    dom = jnp.where(head, dot_, jnp.zeros_like(dot_))
                    t = _dot_nt(kt, qm) - cs[e][:, 0:qlen]
                    if mask is not None:
                        t = jnp.where(mask, t, NEG)
                    pr = jnp.exp(t + (ct_row - lse_row))
                    dv_acc[0:klen] = dv_acc[0:klen] + _dot(pr.astype(BF16), dom)
                    dsv = pr * (_dot_nt(vtile, dom) - dl_row)
                    ds_b = dsv.astype(BF16)
                    dk_acc[0:klen] = dk_acc[0:klen] + _dot(ds_b, qm)
                    rows = slice(e * HEAD_LANES, (e + 1) * HEAD_LANES)
                    dq_t = _dot(ktt[kblk, rows, :], ds_b[0:BLK])
                    for b in range(1, nk):
                        dq_t = dq_t + _dot(ktt[kblk + b, rows, :], ds_b[b * BLK:(b + 1) * BLK])
                    key_side = dsv[:, 0:BLK]
                    for a in range(1, nq):
                        key_side = key_side + dsv[:, a * BLK:(a + 1) * BLK]
                    dcs_acc[e, 0:klen] = dcs_acc[e, 0:klen] + key_side
                    query_side = jnp.sum(dsv, axis=0, keepdims=True)
                    for a in range(nq):
                        cols = slice(a * BLK, (a + 1) * BLK)
                        dqt[qblk + a, rows, :] = dqt[qblk + a, rows, :] + dq_t[:, cols]
                        dcq_ref[qblk + a] = dcq_ref[qblk + a] + jnp.where(sub8 == h, query_side[:, cols], 0.0)

            def later(i, carry):
                tile(kblk + nk + 2 * i, 2, masks["valid"] if nk == 1 else None)
                return carry

            tile(kblk, nk, masks["first"] if nk == 1 else masks["diag"])
            lax.fori_loop(0, n_later, later, 0)
            dp_ref[pl.ds(koff, klen), BLK:2 * BLK] = dk_acc[0:klen].astype(BF16)
            dp_ref[pl.ds(koff, klen), 2 * BLK:3 * BLK] = dv_acc[0:klen].astype(BF16)
            upd = jnp.zeros((klen, BLK), F32)
            for e in range(2):
                upd = upd + jnp.where(klane == 2 * p + e, -jnp.sum(dcs_acc[e, 0:klen], axis=1, keepdims=True), 0.0)
            dc_ref[pl.ds(koff, klen), :] = dc_ref[pl.ds(koff, klen), :] + upd

        kv_pass(0, 1, nu)

        def k_loop(u, carry):
            kv_pass(1 + 2 * u, 2, nu - 1 - u)
            return carry

        lax.fori_loop(0, nu, k_loop, 0)

        def flush(j, carry):
            off = pl.multiple_of(j * BLK, BLK)
            dp_ref[pl.ds(off, BLK), 0:BLK] = (dqt[j].T * scale).astype(BF16)
            return carry

        lax.fori_loop(0, nblk, flush, 0)

    stat = _full((nblk, 8, BLK))
    return pl.pallas_call(
        body, name="b_fox", grid=(FOX_HEADS // 2,),
        in_specs=[pl.BlockSpec((L, 384), lambda p: (0, RET_W // 384 + p)), pl.BlockSpec((L, BLK), lambda p: (0, 4 + p)),
                  _full((L, BLK)), stat, stat, stat],
        out_specs=[pl.BlockSpec((L, 384), lambda p: (0, p)), _full((L, BLK)), stat],
        out_shape=[jax.ShapeDtypeStruct((L, FOX_W), BF16), jax.ShapeDtypeStruct((L, BLK), F32),
                   jax.ShapeDtypeStruct((nblk, 8, BLK), F32)],
        scratch_shapes=[pltpu.VMEM((nblk, BLK, BLK), BF16), pltpu.VMEM((nblk, BLK, BLK), F32),
                        pltpu.VMEM((UNIT, BLK), F32), pltpu.VMEM((UNIT, BLK), F32), pltpu.VMEM((2, UNIT, BLK), F32)],
        compiler_params=_params(("arbitrary",)),
    )(proj, dmix, c, ctb, lse, delta)


def _fox_post(dc, dcq, ff, fb):
    L = dc.shape[0]
    nblk = L // BLK

    def body(dc_ref, dcq_ref, ff_ref, b_ref, dff_ref, dffb_ref, dfb_ref, carry):
        i = pl.program_id(0)

        @pl.when(i == 0)
        def _():
            carry[...] = jnp.zeros_like(carry)
            dfb_ref[...] = jnp.zeros_like(dfb_ref)

        d = dc_ref[...] + jnp.concatenate([dcq_ref[0], jnp.zeros((BLK - 8, BLK), F32)], axis=0).T
        tri = (_iota((BLK, BLK), 0) <= _iota((BLK, BLK), 1)).astype(BF16)
        hi, mid, lo = _split3(d)
        dlf = _dot(tri, hi) + _dot(tri, mid) + _dot(tri, lo) + carry[...]
        carry[...] = carry[...] + jnp.sum(d, axis=0, keepdims=True)
        z = ff_ref[...] + b_ref[...]
        dff = jnp.where(_iota((BLK, BLK), 1) < FOX_HEADS, dlf * jax.nn.sigmoid(-z), 0.0)
        dff_ref[...] = dff
        dffb_ref[...] = dff.astype(BF16)
        dfb_ref[...] = dfb_ref[...] + jnp.sum(dff, axis=0, keepdims=True)

    rev = lambda i: (nblk - 1 - i, 0)
    return pl.pallas_call(
        body, name="b_foxpost", grid=(nblk,),
        in_specs=[pl.BlockSpec((BLK, BLK), rev), pl.BlockSpec((1, 8, BLK), lambda i: (nblk - 1 - i, 0, 0)),
                  pl.BlockSpec((BLK, BLK), rev), _full((1, BLK))],
        out_specs=[pl.BlockSpec((BLK, BLK), rev), pl.BlockSpec((BLK, BLK), rev), _full((1, BLK))],
        out_shape=[jax.ShapeDtypeStruct((L, BLK), F32), jax.ShapeDtypeStruct((L, BLK), BF16),
                   jax.ShapeDtypeStruct((1, BLK), F32)],
        scratch_shapes=[pltpu.VMEM((1, BLK), F32)],
        compiler_params=_params(("arbitrary",)),
    )(dc, dcq, ff, fb)


def _inproj_bwd(dpr, dpf, dffb, w_main, w_ff, h0, g, dh1):
    L = h0.shape[0]
    tm = _row_tile(L)

    def body(dpr_ref, dpf_ref, dff_ref, wm_ref, wf_ref, h_ref, g_ref, dh1_ref, dh0_ref, dg_ref):
        @pl.when(pl.program_id(0) == 0)
        def _():
            dg_ref[...] = jnp.zeros_like(dg_ref)

        dn = (_dot_nt(dpr_ref[...], wm_ref[:, 0:RET_W]) + _dot_nt(dpf_ref[...], wm_ref[:, RET_W:MAIN_W])
              + _dot_nt(dff_ref[...], wf_ref[...]))
        h = h_ref[...]
        r = lax.rsqrt(jnp.mean(h * h, axis=-1, keepdims=True) + EPS)
        yn = h * r
        dg_ref[...] = dg_ref[...] + jnp.sum(dn * yn, axis=0, keepdims=True)
        dyn = dn * g_ref[...]
        dh0_ref[...] = dh1_ref[...] + r * (dyn - yn * jnp.mean(dyn * yn, axis=-1, keepdims=True))

    rows = lambda w: pl.BlockSpec((tm, w), lambda i: (i, 0))
    return pl.pallas_call(
        body, name="b_inproj", grid=(L // tm,),
        in_specs=[rows(RET_W), rows(FOX_W), rows(BLK), _full((D_MODEL, MAIN_W)), _full((D_MODEL, BLK)),
                  rows(D_MODEL), _full((1, D_MODEL)), rows(D_MODEL)],
        out_specs=[rows(D_MODEL), _full((1, D_MODEL))],
        out_shape=[jax.ShapeDtypeStruct((L, D_MODEL), F32), jax.ShapeDtypeStruct((1, D_MODEL), F32)],
        compiler_params=_params(("arbitrary",)),
    )(dpr, dpf, dffb, w_main, w_ff, h0, g, dh1)


def _local_step(x, target, meta, attn_g, w_main, w_ff, fox_b, ret_g, w_out, ffn_g, w_up, conv_w, conv_b, w_down, final_g):
    S = x.shape[0]
    L = S + PREFIX
    h0 = jnp.concatenate([jnp.zeros((N_PAD, D_MODEL), F32), meta, x], axis=0)
    tgt = jnp.concatenate([jnp.zeros((PREFIX, D_MODEL), F32), target], axis=0)
    fb = jnp.pad(fox_b, ((0, 0), (0, BLK - FOX_HEADS)))
    cos_t, sin_t = _rotary_tables(L)

    n1, proj, ff = _rms_inproj(h0, attn_g, w_main, w_ff)
    c, ctb = _fox_prep(ff, fb)
    mix_r, o_ret, states = _retention_fwd(proj, cos_t, sin_t, ret_g)
    o_f, lse = _fox_fwd(proj, c, ctb)
    h1, n2, up = _outproj_up(mix_r, o_f, h0, w_out, ffn_g, w_up)
    g_act, dh2, dh2b, d_final_g, loss = _ffn_down_loss(up, conv_w, conv_b, w_down, h1, final_g, tgt)

    dacc, db, dconv = _ffn_bwd_gate(dh2b, w_down, up, conv_w, conv_b)
    dup, dh1, dh1b, dmix, d_ffn_g = _ffn_bwd_up(dacc, db, conv_w, w_up, h1, ffn_g, dh2, w_out)
    d_w_down = _wgrad(g_act, dh2b, "wgrad_down", tn=None)[0]
    d_w_up = _wgrad(n2, dup, "wgrad_up", tn=w_up.shape[2])
    d_w_out = jnp.concatenate([_wgrad(mix_r, dh1b, "wgrad_out_r")[0], _wgrad(o_f, dh1b, "wgrad_out_f")[0]], axis=0)

    dpr, d_ret_g = _retention_bwd(dmix, o_ret, proj, cos_t, sin_t, ret_g, states)
    delta = _fox_delta(dmix, o_f)
    dpf, dc, dcq = _fox_bwd(proj, dmix, c, ctb, lse, delta)
    dff, dffb, d_fox_b = _fox_post(dc, dcq, ff, fb)
    dh0, d_attn_g = _inproj_bwd(dpr, dpf, dffb, w_main, w_ff, h0, attn_g, dh1)
    d_w_main = jnp.concatenate([_wgrad(n1, dpr, "wgrad_in_r")[0], _wgrad(n1, dpf, "wgrad_in_f")[0]], axis=1)
    d_w_ff = _wgrad(n1, dffb, "wgrad_in_ff")[0]

    return dict(
        loss=loss[0, 0], dx=dh0[PREFIX:], dmeta=dh0[N_PAD:PREFIX], attn_g=d_attn_g, w_main=d_w_main,
        w_ff=d_w_ff[:, :FOX_HEADS], fox_b=d_fox_b[:, :FOX_HEADS], ret_g=d_ret_g, w_out=d_w_out, ffn_g=d_ffn_g,
        w_up=d_w_up, conv_w=dconv[0:3], conv_b=dconv[3:4], w_down=d_w_down, final_g=d_final_g)


_ANY = pl.BlockSpec(memory_space=pl.ANY)


def _place():
    return lax.axis_index("x"), lax.axis_index("y"), lax.axis_index("c")


def _other_chips(x, y):
    return [(1 - x, y), (x, 1 - y), (1 - x, 1 - y)]


def _chip_allgather(arrays):
    n = len(arrays)

    def body(*refs):
        ins, outs = refs[:n], refs[n:2 * n]
        send, recv, loc = refs[2 * n:]
        x, y, c = _place()
        mine = 2 * x + y
        peers = _other_chips(x, y)

        def remote(a, k, slot):
            return pltpu.make_async_remote_copy(
                src_ref=ins[a], dst_ref=outs[a].at[slot], send_sem=send.at[3 * a + k], recv_sem=recv.at[3 * a + k],
                device_id=(peers[k][0], peers[k][1], c), device_id_type=MESH)

        local = [pltpu.make_async_copy(ins[a], outs[a].at[mine], loc.at[a]) for a in range(n)]
        sends = [remote(a, k, mine) for a in range(n) for k in range(3)]
        for cp in local + sends:
            cp.start()
        for a in range(n):
            for k in range(3):
                remote(a, k, 2 * peers[k][0] + peers[k][1]).wait_recv()
        for cp in sends:
            cp.wait_send()
        for cp in local:
            cp.wait()

    return pl.pallas_call(
        body, name="ag_weights", in_specs=[_ANY] * n, out_specs=[_ANY] * n,
        out_shape=[jax.ShapeDtypeStruct((N_CHIPS,) + a.shape, a.dtype) for a in arrays],
        scratch_shapes=[pltpu.SemaphoreType.DMA((3 * n,)), pltpu.SemaphoreType.DMA((3 * n,)),
                        pltpu.SemaphoreType.DMA((n,))],
    )(*arrays)


def _sibling_exchange(grads, small):
    n = len(grads)

    def body(*refs):
        ins, small_in = refs[:n], refs[n]
        outs, small_out = refs[n + 1:2 * n + 1], refs[2 * n + 1]
        send, recv, s_send, s_recv, loc = refs[2 * n + 2:]
        x, y, c = _place()
        me = 4 * x + 2 * y + c

        def half_copy(a, which):
            half = ins[a].shape[1] // 2
            return pltpu.make_async_remote_copy(
                src_ref=ins[a].at[pl.ds(0, N_CHIPS), pl.ds(which * half, half)], dst_ref=outs[a],
                send_sem=send.at[a], recv_sem=recv.at[a], device_id=(x, y, 1 - c), device_id_type=MESH)

        def peer_of(r):
            return tuple(1 - v if (r >> b) & 1 else v for v, b in ((x, 2), (y, 1), (c, 0)))

        def small_copy(r, slot):
            return pltpu.make_async_remote_copy(
                src_ref=small_in, dst_ref=small_out.at[slot], send_sem=s_send.at[r - 1], recv_sem=s_recv.at[r - 1],
                device_id=peer_of(r), device_id_type=MESH)

        local = pltpu.make_async_copy(small_in, small_out.at[me], loc.at[0])
        sends = [half_copy(a, 1 - c) for a in range(n)] + [small_copy(r, me) for r in range(1, N_DEV)]
        local.start()
        for cp in sends:
            cp.start()
        for r in range(1, N_DEV):
            px, py, pc = peer_of(r)
            small_copy(r, 4 * px + 2 * py + pc).wait_recv()
        for a in range(n):
            half_copy(a, c).wait_recv()
        for cp in sends:
            cp.wait_send()
        local.wait()

    rows = small.shape[0]
    return pl.pallas_call(
        body, name="rs_sibling", in_specs=[_ANY] * (n + 1), out_specs=[_ANY] * (n + 1),
        out_shape=[jax.ShapeDtypeStruct((N_CHIPS, g.shape[1] // 2, g.shape[2]), g.dtype) for g in grads]
        + [jax.ShapeDtypeStruct((N_DEV, rows, small.shape[1]), small.dtype)],
        scratch_shapes=[pltpu.SemaphoreType.DMA((n,)), pltpu.SemaphoreType.DMA((n,)),
                        pltpu.SemaphoreType.DMA((N_DEV - 1,)), pltpu.SemaphoreType.DMA((N_DEV - 1,)),
                        pltpu.SemaphoreType.DMA((1,))],
    )(*grads, small)


def _chip_reduce_scatter(parts):
    n = len(parts)

    def body(*refs):
        ins, outs = refs[:n], refs[n:2 * n]
        send, recv, loc = refs[2 * n:]
        x, y, c = _place()
        mine = 2 * x + y
        peers = _other_chips(x, y)

        def remote(a, k, src_slot, dst_slot):
            return pltpu.make_async_remote_copy(
                src_ref=ins[a].at[src_slot], dst_ref=outs[a].at[dst_slot], send_sem=send.at[3 * a + k],
                recv_sem=recv.at[3 * a + k], device_id=(peers[k][0], peers[k][1], c), device_id_type=MESH)

        local = [pltpu.make_async_copy(ins[a].at[mine], outs[a].at[mine], loc.at[a]) for a in range(n)]
        sends = [remote(a, k, 2 * peers[k][0] + peers[k][1], mine) for a in range(n) for k in range(3)]
        for cp in local + sends:
            cp.start()
        for a in range(n):
            for k in range(3):
                theirs = 2 * peers[k][0] + peers[k][1]
                remote(a, k, theirs, theirs).wait_recv()
        for cp in sends:
            cp.wait_send()
        for cp in local:
            cp.wait()

    return pl.pallas_call(
        body, name="rs_chips", in_specs=[_ANY] * n, out_specs=[_ANY] * n,
        out_shape=[jax.ShapeDtypeStruct(p.shape, p.dtype) for p in parts],
        scratch_shapes=[pltpu.SemaphoreType.DMA((3 * n,)), pltpu.SemaphoreType.DMA((3 * n,)),
                        pltpu.SemaphoreType.DMA((n,))],
    )(*parts)


def _sibling_allgather(halves):
    n = len(halves)

    def body(*refs):
        ins, outs = refs[:n], refs[n:2 * n]
        send, recv, loc = refs[2 * n:]
        x, y, c = _place()

        def rows(a, which):
            return outs[a].at[which]

        def remote(a, which):
            return pltpu.make_async_remote_copy(
                src_ref=ins[a], dst_ref=rows(a, which), send_sem=send.at[a], recv_sem=recv.at[a],
                device_id=(x, y, 1 - c), device_id_type=MESH)

        local = [pltpu.make_async_copy(ins[a], rows(a, c), loc.at[a]) for a in range(n)]
        sends = [remote(a, c) for a in range(n)]
        for cp in local + sends:
            cp.start()
        for a in range(n):
            remote(a, 1 - c).wait_recv()
        for cp in sends:
            cp.wait_send()
        for cp in local:
            cp.wait()

    outs = pl.pallas_call(
        body, name="ag_sibling", in_specs=[_ANY] * n, out_specs=[_ANY] * n,
        out_shape=[jax.ShapeDtypeStruct((2,) + h.shape, h.dtype) for h in halves],
        scratch_shapes=[pltpu.SemaphoreType.DMA((n,)), pltpu.SemaphoreType.DMA((n,)), pltpu.SemaphoreType.DMA((n,))],
    )(*halves)
    return [o.reshape(2 * o.shape[1], o.shape[2]) for o in outs]


def _pair_add(full, recv, core, name):
    _, R, C = full.shape
    half = R // 2

    def body(core_ref, a_ref, b_ref, o_ref):
        o_ref[...] = (a_ref[...] + b_ref[...]).astype(BF16)

    return pl.pallas_call(
        body, name=name,
        grid_spec=pltpu.PrefetchScalarGridSpec(
            num_scalar_prefetch=1, grid=(N_CHIPS,),
            in_specs=[pl.BlockSpec((1, half, C), lambda j, core_ref: (j, core_ref[0], 0)),
                      pl.BlockSpec((1, half, C), lambda j, core_ref: (j, 0, 0))],
            out_specs=pl.BlockSpec((1, half, C), lambda j, core_ref: (j, 0, 0))),
        out_shape=jax.ShapeDtypeStruct((N_CHIPS, half, C), BF16),
        compiler_params=_params(("parallel",)),
    )(core, full, recv)


def _sum_slots(q, name, tiles=2):
    n, R, C = q.shape
    tr = R // tiles

    def body(q_ref, o_ref):
        acc = q_ref[0].astype(F32)
        for j in range(1, n):
            acc = acc + q_ref[j].astype(F32)
        o_ref[...] = acc

    return pl.pallas_call(
        body, name=name, grid=(tiles,),
        in_specs=[pl.BlockSpec((n, tr, C), lambda i: (0, i, 0))],
        out_specs=pl.BlockSpec((tr, C), lambda i: (i, 0)),
        out_shape=jax.ShapeDtypeStruct((R, C), F32),
        compiler_params=_params(("parallel",)),
    )(q)


def _adamw(w, g, m, v, name, tiles=4):
    R, C = w.shape
    tr = R // tiles

    def body(w_ref, g_ref, m_ref, v_ref, d_ref, m2_ref, v2_ref):
        g_ = g_ref[...]
        m2 = ADAM_B1 * m_ref[...] + (1.0 - ADAM_B1) * g_
        v2 = ADAM_B2 * v_ref[...] + (1.0 - ADAM_B2) * (g_ * g_)
        m_hat = m2 / (1.0 - ADAM_B1 ** ADAM_STEP)
        v_hat = v2 / (1.0 - ADAM_B2 ** ADAM_STEP)
        d_ref[...] = -ADAM_LR * (m_hat / (jnp.sqrt(v_hat) + ADAM_EPS) + ADAM_WD * w_ref[...])
        m2_ref[...] = m2
        v2_ref[...] = v2

    spec = pl.BlockSpec((tr, C), lambda i: (i, 0))
    return pl.pallas_call(
        body, name=name, grid=(tiles,), in_specs=[spec] * 4, out_specs=[spec] * 3,
        out_shape=[jax.ShapeDtypeStruct((R, C), F32)] * 3,
        compiler_params=_params(("parallel",)),
    )(w, g, m, v)


def _pack_rows(pieces, rows):
    flat = jnp.concatenate([jnp.pad(p.reshape(-1).astype(F32), (0, (-p.size) % D_MODEL)) for p in pieces])
    return jnp.pad(flat, (0, rows * D_MODEL - flat.size)).reshape(rows, D_MODEL)


def _unpack_rows(pack, shapes):
    flat = pack.reshape(-1)
    out, off = [], 0
    for shp in shapes:
        size = int(np.prod(shp))
        out.append(flat[off:off + size].reshape(shp))
        off += size + (-size) % D_MODEL
    return out


def _kernel_order(w):
    parts = [w[:, 0:RET_W]]
    for p in range(FOX_HEADS // 2):
        parts += [w[:, RET_W + part * 512 + p * BLK:RET_W + part * 512 + (p + 1) * BLK] for part in range(3)]
    return jnp.concatenate(parts, axis=1)


def _reference_order(g_main, g_ff):
    parts = [g_main[:, 0:RET_W]]
    for part in range(3):
        parts += [g_main[:, RET_W + 384 * p + part * BLK:RET_W + 384 * p + (part + 1) * BLK] for p in range(FOX_HEADS // 2)]
    return jnp.concatenate(parts + [g_ff], axis=1)


def kernel(x, meta_tokens, attn_norm_g, w_in, fox_forget_b, ret_norm_g, w_out, ffn_norm_g, w_up, conv_w, conv_b, w_down, final_norm_g, loss_target, m_meta_tokens, m_attn_norm_g, m_w_in, m_fox_forget_b, m_ret_norm_g, m_w_out, m_ffn_norm_g, m_w_up, m_conv_w, m_conv_b, m_w_down, m_final_norm_g, v_meta_tokens, v_attn_norm_g, v_w_in, v_fox_forget_b, v_ret_norm_g, v_w_out, v_ffn_norm_g, v_w_up, v_conv_w, v_conv_b, v_w_down, v_final_norm_g):
    chip = 2 * lax.axis_index("x") + lax.axis_index("y")
    core = lax.axis_index("c")
    meta_w, conv_sw = meta_tokens.shape[1], conv_w.shape[2]

    small_w = _pack_rows([meta_tokens, conv_w[0]], 8)
    g_in, g_out, g_up, g_down, g_small = _chip_allgather(
        [w_in[0].astype(BF16), w_out[0].astype(BF16), w_up[0].astype(BF16), w_down[0].astype(BF16), small_w])
    w_in_full = g_in.transpose(1, 0, 2).reshape(D_MODEL, IN_WIDTH)
    w_main = _kernel_order(w_in_full)
    w_ff = jnp.pad(w_in_full[:, MAIN_W:], ((0, 0), (0, BLK - FOX_HEADS)))
    small_parts = [_unpack_rows(g_small[j], [meta_tokens.shape, conv_w.shape[1:]]) for j in range(N_CHIPS)]
    meta_full = jnp.concatenate([sp[0] for sp in small_parts], axis=1)
    conv_w_full = jnp.concatenate([sp[1] for sp in small_parts], axis=1)

    out = _local_step(x[0], loss_target[0], meta_full, attn_norm_g, w_main, w_ff, fox_forget_b, ret_norm_g,
                      g_out.reshape(D_MODEL, D_MODEL), ffn_norm_g, g_up, conv_w_full, conv_b,
                      g_down.reshape(D_FF, D_MODEL), final_norm_g[None])

    big = [_reference_order(out["w_main"], out["w_ff"]).reshape(D_MODEL, N_CHIPS, -1).transpose(1, 0, 2),
           out["w_out"].reshape(N_CHIPS, -1, D_MODEL), out["w_up"], out["w_down"].reshape(N_CHIPS, -1, D_MODEL)]
    small_shapes = [(1, D_MODEL), (1, D_MODEL), (1, D_MODEL), (1, 512 + FOX_HEADS + 1), (1, D_FF), (N_META, D_MODEL), (3, D_FF)]
    small = _pack_rows([out["attn_g"], out["ffn_g"], out["final_g"],
                        jnp.concatenate([out["ret_g"], out["fox_b"], out["loss"].reshape(1, 1)], axis=1),
                        out["conv_b"], out["dmeta"], out["conv_w"]], 32)
    *from_sibling, small_all = _sibling_exchange(big, small)
    core_idx = core.reshape(1).astype(jnp.int32)
    names = ("in", "out", "up", "down")
    chip_sums = [_pair_add(g, r, core_idx, "pair_add_" + nm) for g, r, nm in zip(big, from_sibling, names)]
    from_chips = _chip_reduce_scatter(chip_sums)
    halves = [_sum_slots(q, "sum_chips_" + nm) for q, nm in zip(from_chips, names)]
    grad_in, grad_out, grad_up, grad_down = _sibling_allgather(halves)
    s_attn, s_ffn, s_final, s_misc, s_conv_b, s_meta, s_conv_w = _unpack_rows(
        _sum_slots(small_all, "sum_small", tiles=1), small_shapes)
    loss = s_misc[0, 512 + FOX_HEADS]
    small_grads = [lax.dynamic_slice_in_dim(s_meta, chip * meta_w, meta_w, axis=1), s_attn, s_misc[:, 512:512 + FOX_HEADS],
                   s_misc[:, :512], s_ffn, lax.dynamic_slice_in_dim(s_conv_w, chip * conv_sw, conv_sw, axis=1)[None],
                   s_conv_b, s_final[0]]

    big_w = [(w_in, m_w_in, v_w_in, grad_in, "adamw_in"), (w_out, m_w_out, v_w_out, grad_out, "adamw_out"),
             (w_up, m_w_up, v_w_up, grad_up, "adamw_up"), (w_down, m_w_down, v_w_down, grad_down, "adamw_down")]
    big_res = [[g[None]] + [r[None] for r in _adamw(w[0], g, m[0], v[0], nm)] for w, m, v, g, nm in big_w]
    small_w_list = [meta_tokens, attn_norm_g, fox_forget_b, ret_norm_g, ffn_norm_g, conv_w, conv_b, final_norm_g]
    small_m = [m_meta_tokens, m_attn_norm_g, m_fox_forget_b, m_ret_norm_g, m_ffn_norm_g, m_conv_w, m_conv_b, m_final_norm_g]
    small_v = [v_meta_tokens, v_attn_norm_g, v_fox_forget_b, v_ret_norm_g, v_ffn_norm_g, v_conv_w, v_conv_b, v_final_norm_g]
    shapes = [a.shape for a in small_w_list]
    packs = [_pack_rows(lst, 16) for lst in (small_w_list, small_grads, small_m, small_v)]
    small_res = [_unpack_rows(r, shapes) for r in _adamw(*packs, "adamw_small", tiles=1)]
    small_grads = [g.reshape(s) for g, s in zip(small_grads, shapes)]

    def ordered(kind):
        sm = small_grads if kind == 0 else small_res[kind - 1]
        bg = [r[kind] for r in big_res]
        return [sm[0], sm[1], bg[0], sm[2], sm[3], bg[1], sm[4], bg[2], sm[5], sm[6], bg[3], sm[7]]

    return (loss, out["dx"][None], *ordered(0), *ordered(1), *ordered(2), *ordered(3))
```

```python
import functools

import numpy as np
import jax
import jax.numpy as jnp
from jax import lax
from jax.experimental import pallas as pl
from jax.experimental.pallas import tpu as pltpu

F32 = jnp.float32
BF16 = jnp.bfloat16

D_MODEL = 1024
N_META = 16
BLK = 128
UNIT = 2 * BLK
CHUNK = 64
N_PAD = BLK - N_META
PREFIX = BLK
RET_HEADS = 4
FOX_HEADS = 8
HEAD_LANES = 64
D_FF = 2816
ROPE_BASE = 10000.0
EPS = 1e-6
NEG = -1e30
RET_W = 1536
FOX_W = 1536
MAIN_W = RET_W + FOX_W
IN_WIDTH = MAIN_W + FOX_HEADS
N_CHIPS = 4
N_DEV = 8

ADAM_LR = 0.001
ADAM_B1 = 0.9
ADAM_B2 = 0.999
ADAM_EPS = 1e-08
ADAM_WD = 0.01
ADAM_STEP = 10

MESH = pl.DeviceIdType.MESH
VMEM_LIMIT_MB = 56

_NT = (((1,), (1,)), ((), ()))
_TN = (((0,), (0,)), ((), ()))


def _dot(a, b):
    return jnp.dot(a, b, preferred_element_type=F32)


def _dot_nt(a, b):
    return lax.dot_general(a, b, _NT, preferred_element_type=F32)


def _dot_tn(a, b):
    return lax.dot_general(a, b, _TN, preferred_element_type=F32)


def _params(dims=None, vmem_mb=VMEM_LIMIT_MB):
    kw = dict(vmem_limit_bytes=vmem_mb << 20)
    if dims is not None:
        kw["dimension_semantics"] = dims
    return pltpu.CompilerParams(**kw)


def _row_tile(n, prefs=(384, 256, 128)):
    for t in prefs:
        if n % t == 0:
            return t
    raise ValueError(f"no row tile for {n}")


def _iota(shape, dim):
    return lax.broadcasted_iota(jnp.int32, shape, dim)


def _pick_row(tile, row):
    sub = _iota(tile.shape, 0)
    return jnp.sum(jnp.where(sub == row, tile, 0.0), axis=0, keepdims=True)


def _split3(x):
    hi = x.astype(BF16)
    r1 = x - hi.astype(F32)
    mid = r1.astype(BF16)
    lo = (r1 - mid.astype(F32)).astype(BF16)
    return hi, mid, lo


def _full(shape):
    nd = len(shape)
    return pl.BlockSpec(shape, lambda *_: (0,) * nd)


def _in_perm():
    cols = list(range(RET_W))
    for p in range(FOX_HEADS // 2):
        for part in range(3):
            start = RET_W + part * 512 + p * BLK
            cols += list(range(start, start + BLK))
    return np.asarray(cols, np.int32)


def _rotary_tables(L):
    half = HEAD_LANES // 2
    inv = 1.0 / (ROPE_BASE ** (jnp.arange(half, dtype=F32) / half))
    ang = jnp.arange(L).astype(F32)[:, None] * inv[None, :]
    cos, sin = jnp.cos(ang), jnp.sin(ang)
    cos_t = jnp.tile(cos, (1, 4))
    sin_t = jnp.tile(jnp.concatenate([-sin, sin], axis=1), (1, 2))
    return cos_t, sin_t


def _decay_tables():
    gam = 1.0 - 2.0 ** (-5.0 - np.arange(RET_HEADS, dtype=np.float64))
    n = np.arange(BLK)
    same_or_past = (n[:, None] // CHUNK) >= (n[None, :] // CHUNK)
    dist = np.abs(n[:, None] - n[None, :])
    dmat = np.stack([np.where(same_or_past, g ** dist, 0.0) for g in gam]).astype(np.float32)
    lane_head = np.arange(BLK) // HEAD_LANES
    wq = np.stack([gam[2 * p + lane_head][None, :] ** (n[:, None] + 1.0) for p in range(2)]).astype(np.float32)
    wk = np.stack([gam[2 * p + lane_head][None, :] ** (BLK - 1.0 - n[:, None]) for p in range(2)]).astype(np.float32)
    g_blk = tuple(float(g ** BLK) for g in gam)
    return jnp.asarray(dmat), jnp.asarray(wq), jnp.asarray(wk), g_blk


def _rms_inproj(h0, g, w_main, w_ff):
    L = h0.shape[0]
    tm = _row_tile(L)

    def body(h_ref, g_ref, wm_ref, wf_ref, n_ref, p_ref, ff_ref):
        h = h_ref[...]
        r = lax.rsqrt(jnp.mean(h * h, axis=-1, keepdims=True) + EPS)
        n = (h * r * g_ref[...]).astype(BF16)
        n_ref[...] = n
        p_ref[...] = _dot(n, wm_ref[...]).astype(BF16)
        ff_ref[...] = _dot(n, wf_ref[...])

    return pl.pallas_call(
        body, name="f_inproj", grid=(L // tm,),
        in_specs=[pl.BlockSpec((tm, D_MODEL), lambda i: (i, 0)), _full((1, D_MODEL)),
                  _full((D_MODEL, MAIN_W)), _full((D_MODEL, BLK))],
        out_specs=[pl.BlockSpec((tm, D_MODEL), lambda i: (i, 0)), pl.BlockSpec((tm, MAIN_W), lambda i: (i, 0)),
                   pl.BlockSpec((tm, BLK), lambda i: (i, 0))],
        out_shape=[jax.ShapeDtypeStruct((L, D_MODEL), BF16), jax.ShapeDtypeStruct((L, MAIN_W), BF16),
                   jax.ShapeDtypeStruct((L, BLK), F32)],
        compiler_params=_params(("parallel",)),
    )(h0, g, w_main, w_ff)


def _fox_prep(ff, fb):
    L = ff.shape[0]
    nblk = L // BLK

    def body(ff_ref, b_ref, c_ref, ct_ref, carry):
        i = pl.program_id(0)

        @pl.when(i == 0)
        def _():
            carry[...] = jnp.zeros_like(carry)

        z = ff_ref[...] + b_ref[...]
        lf = jnp.minimum(z, 0.0) - jnp.log1p(jnp.exp(-jnp.abs(z)))
        lf = jnp.where(_iota((BLK, BLK), 1) < FOX_HEADS, lf, 0.0)
        tri = (_iota((BLK, BLK), 0) >= _iota((BLK, BLK), 1)).astype(BF16)
        hi, mid, lo = _split3(lf)
        cs = _dot(tri, hi) + _dot(tri, mid) + _dot(tri, lo) + carry[...]
        c_ref[...] = cs
        ct_ref[0] = cs.T[0:8, :]
        carry[...] = carry[...] + jnp.sum(lf, axis=0, keepdims=True)

    return pl.pallas_call(
        body, name="f_foxprep", grid=(nblk,),
        in_specs=[pl.BlockSpec((BLK, BLK), lambda i: (i, 0)), _full((1, BLK))],
        out_specs=[pl.BlockSpec((BLK, BLK), lambda i: (i, 0)), pl.BlockSpec((1, 8, BLK), lambda i: (i, 0, 0))],
        out_shape=[jax.ShapeDtypeStruct((L, BLK), F32), jax.ShapeDtypeStruct((nblk, 8, BLK), F32)],
        scratch_shapes=[pltpu.VMEM((1, BLK), F32)],
        compiler_params=_params(("arbitrary",)),
    )(ff, fb)


def _rot_fns(cos, sin):
    lane = _iota((BLK, BLK), 1)
    first = (lane & (HEAD_LANES - 1)) < HEAD_LANES // 2

    def swap(x):
        return jnp.where(first, pltpu.roll(x, BLK - 32, 1), pltpu.roll(x, 32, 1))

    def rot(x):
        return x * cos + swap(x) * sin

    def rot_t(dy):
        return dy * cos + swap(dy * sin)

    return rot, rot_t


def _retention_fwd(proj, cos_t, sin_t, ret_g):
    L = proj.shape[0]
    nblk = L // BLK
    dmat, wq_t, wk_t, g_blk = _decay_tables()

    def body(q_ref, k_ref, v_ref, gate_ref, cos_ref, sin_ref, d_ref, wq_ref, wk_ref, rg_ref,
             mix_ref, o_ref, rs_ref, state):
        i = pl.program_id(0)

        @pl.when(i == 0)
        def _():
            state[...] = jnp.zeros_like(state)

        rot, _ = _rot_fns(cos_ref[...], sin_ref[...])
        lane = _iota((BLK, BLK), 1)
        sub = _iota((BLK, BLK), 0)
        for p in range(2):
            qr = rot(q_ref[:, p * BLK:(p + 1) * BLK].astype(F32))
            kr = rot(k_ref[:, p * BLK:(p + 1) * BLK].astype(F32)) * (HEAD_LANES ** -0.5)
            kr_b = kr.astype(BF16)
            qw = (qr * wq_ref[p]).astype(BF16)
            kw = (kr * wk_ref[p]).astype(BF16)
            for e in range(2):
                h = 2 * p + e
                cols = slice(h * BLK, (h + 1) * BLK)
                qm = jnp.where((lane >> 6) == e, qr, 0.0).astype(BF16)
                s = _dot_nt(qm, kr_b) * d_ref[h]
                vh = v_ref[:, cols]
                st = state[h]
                rs_ref[0, h] = st
                o = _dot(s.astype(BF16), vh) + _dot(qw, st.astype(BF16))
                u = jnp.where((sub >> 6) == e, _dot_tn(kw, vh), 0.0)
                state[h] = g_blk[h] * st + u
                rn = lax.rsqrt(jnp.mean(o * o, axis=-1, keepdims=True) + EPS)
                gate = gate_ref[:, cols].astype(F32)
                o_ref[:, cols] = o
                mix_ref[:, cols] = (o * rn * rg_ref[:, cols] * (gate * jax.nn.sigmoid(gate))).astype(BF16)

    row = lambda c: (lambda i: (i, c))
    return pl.pallas_call(
        body, name="f_retention", grid=(nblk,),
        in_specs=[pl.BlockSpec((BLK, 256), row(0)), pl.BlockSpec((BLK, 256), row(1)),
                  pl.BlockSpec((BLK, 512), row(1)), pl.BlockSpec((BLK, 512), row(2)),
                  pl.BlockSpec((BLK, BLK), row(0)), pl.BlockSpec((BLK, BLK), row(0)),
                  _full((RET_HEADS, BLK, BLK)), _full((2, BLK, BLK)), _full((2, BLK, BLK)), _full((1, 512))],
        out_specs=[pl.BlockSpec((BLK, 512), row(0)), pl.BlockSpec((BLK, 512), row(0)),
                   pl.BlockSpec((1, RET_HEADS, BLK, BLK), lambda i: (i, 0, 0, 0))],
        out_shape=[jax.ShapeDtypeStruct((L, 512), BF16), jax.ShapeDtypeStruct((L, 512), F32),
                   jax.ShapeDtypeStruct((nblk, RET_HEADS, BLK, BLK), F32)],
        scratch_shapes=[pltpu.VMEM((RET_HEADS, BLK, BLK), F32)],
        compiler_params=_params(("arbitrary",)),
    )(proj, proj, proj, proj, cos_t, sin_t, dmat, wq_t, wk_t, ret_g)


def _fox_units(L):
    nblk = L // BLK
    assert L % BLK == 0 and nblk % 2 == 1, "sequence must be one 128-row block plus whole 256-row tiles"
    return nblk, (nblk - 1) // 2


def _fox_tile_masks():
    sub, lane = _iota((BLK, BLK), 0), _iota((BLK, BLK), 1)
    return dict(first=(sub <= lane) & (sub >= N_PAD), valid=_iota((BLK, UNIT), 0) >= N_PAD,
                diag=_iota((UNIT, UNIT), 0) <= _iota((UNIT, UNIT), 1))


def _fox_fwd(proj, c, ctb):
    L = proj.shape[0]
    nblk, nu = _fox_units(L)
    scale = HEAD_LANES ** -0.5

    def body(qkv_ref, c_ref, ct_ref, of_ref, lse_ref, vt, csb):
        p = pl.program_id(0)

        @pl.when(p == 0)
        def _():
            lse_ref[...] = jnp.zeros_like(lse_ref)

        lane = _iota((BLK, BLK), 1)
        sub8 = _iota((8, BLK), 0)
        masks = _fox_tile_masks()

        def pre(j, carry):
            off = pl.multiple_of(j * BLK, BLK)
            vt[j] = qkv_ref[pl.ds(off, BLK), 2 * BLK:3 * BLK].astype(F32).T.astype(BF16)
            ct = c_ref[pl.ds(off, BLK), :]
            for e in range(2):
                col = jnp.sum(jnp.where(lane == 2 * p + e, ct, 0.0), axis=1, keepdims=True)
                csb[e, j] = jnp.broadcast_to(col, (BLK, UNIT))
            return carry

        lax.fori_loop(0, nblk, pre, 0)

        def attend(qblk, nq, n_whole):
            qlen = nq * BLK
            qoff = pl.multiple_of(qblk * BLK, BLK)
            qs = qkv_ref[pl.ds(qoff, qlen), 0:BLK].astype(F32) * scale
            qlane = _iota((qlen, BLK), 1)
            qm = [jnp.where((qlane >> 6) == e, qs, 0.0).astype(BF16) for e in range(2)]
            ct_row = [jnp.concatenate([_pick_row(ct_ref[qblk + a], 2 * p + e) for a in range(nq)], axis=1)
                      for e in range(2)]

            def step(kblk, nk, mask, st):
                koff = pl.multiple_of(kblk * BLK, BLK)
                kt = qkv_ref[pl.ds(koff, nk * BLK), BLK:2 * BLK]
                out = []
                for e in range(2):
                    m, l, acc = st[3 * e:3 * e + 3]
                    s = _dot_nt(kt, qm[e])
                    t = jnp.concatenate([s[b * BLK:(b + 1) * BLK] - csb[e, kblk + b, :, 0:qlen] for b in range(nk)], axis=0)
                    if mask is not None:
                        t = jnp.where(mask, t, NEG)
                    m_new = jnp.maximum(m, jnp.max(t, axis=0, keepdims=True) + ct_row[e])
                    alpha = jnp.exp(m - m_new)
                    pr = jnp.exp(t - (m_new - ct_row[e]))
                    l = alpha * l + jnp.sum(pr, axis=0, keepdims=True)
                    pr_b = pr.astype(BF16)
                    pv = _dot(vt[kblk, e * HEAD_LANES:(e + 1) * HEAD_LANES, :], pr_b[0:BLK])
                    for b in range(1, nk):
                        pv = pv + _dot(vt[kblk + b, e * HEAD_LANES:(e + 1) * HEAD_LANES, :], pr_b[b * BLK:(b + 1) * BLK])
                    out += [m_new, l, alpha * acc + pv]
                return tuple(out)

            st = (jnp.full((1, qlen), NEG, F32), jnp.zeros((1, qlen), F32), jnp.zeros((HEAD_LANES, qlen), F32)) * 2
            if nq == 1:
                st = step(0, 1, masks["first"], st)
            else:
                st = step(0, 1, masks["valid"], st)
                st = lax.fori_loop(0, n_whole, lambda j, s_: step(1 + 2 * j, 2, None, s_), st)
                st = step(qblk, 2, masks["diag"], st)
            o_t = jnp.concatenate([st[2] * (1.0 / st[1]), st[5] * (1.0 / st[4])], axis=0)
            of_ref[pl.ds(qoff, qlen), :] = o_t.T.astype(BF16)
            lse = [st[3 * e] + jnp.log(st[3 * e + 1]) for e in range(2)]
            for a in range(nq):
                rows = [lse[e][:, a * BLK:(a + 1) * BLK] for e in range(2)]
                lse_ref[qblk + a] = lse_ref[qblk + a] + (
                    jnp.where(sub8 == 2 * p, rows[0], 0.0) + jnp.where(sub8 == 2 * p + 1, rows[1], 0.0))

        attend(0, 1, 0)

        def q_loop(u, carry):
            attend(1 + 2 * u, 2, u)
            return carry

        lax.fori_loop(0, nu, q_loop, 0)

    return pl.pallas_call(
        body, name="f_fox", grid=(FOX_HEADS // 2,),
        in_specs=[pl.BlockSpec((L, 384), lambda p: (0, RET_W // 384 + p)), _full((L, BLK)), _full((nblk, 8, BLK))],
        out_specs=[pl.BlockSpec((L, BLK), lambda p: (0, p)), _full((nblk, 8, BLK))],
        out_shape=[jax.ShapeDtypeStruct((L, 512), BF16), jax.ShapeDtypeStruct((nblk, 8, BLK), F32)],
        scratch_shapes=[pltpu.VMEM((nblk, BLK, BLK), BF16), pltpu.VMEM((2, nblk, BLK, UNIT), F32)],
        compiler_params=_params(("arbitrary",)),
    )(proj, c, ctb)


def _outproj_up(mix_r, o_f, h0, w_out, ffn_g, w_up):
    L = h0.shape[0]
    tm = _row_tile(L)
    shard = w_up.shape[2]

    def body(mr_ref, of_ref, h0_ref, wo_ref, g_ref, wu_ref, h1_ref, n2_ref, up_ref):
        h1 = h0_ref[...] + _dot(mr_ref[...], wo_ref[0:512, :]) + _dot(of_ref[...], wo_ref[512:1024, :])
        h1_ref[...] = h1
        r = lax.rsqrt(jnp.mean(h1 * h1, axis=-1, keepdims=True) + EPS)
        n2 = (h1 * r * g_ref[...]).astype(BF16)
        n2_ref[...] = n2
        for j in range(N_CHIPS):
            up_ref[:, j * shard:(j + 1) * shard] = _dot(n2, wu_ref[j]).astype(BF16)

    rows = lambda w: pl.BlockSpec((tm, w), lambda i: (i, 0))
    return pl.pallas_call(
        body, name="f_outproj_up", grid=(L // tm,),
        in_specs=[rows(512), rows(512), rows(D_MODEL), _full((D_MODEL, D_MODEL)), _full((1, D_MODEL)),
                  _full((N_CHIPS, D_MODEL, shard))],
        out_specs=[rows(D_MODEL), rows(D_MODEL), rows(2 * D_FF)],
        out_shape=[jax.ShapeDtypeStruct((L, D_MODEL), F32), jax.ShapeDtypeStruct((L, D_MODEL), BF16),
                   jax.ShapeDtypeStruct((L, 2 * D_FF), BF16)],
        compiler_params=_params(("parallel",)),
    )(mix_r, o_f, h0, w_out, ffn_g, w_up)


def _conv_acc(a_ref, halo_ref, cw_refs, cb_ref, i, tm):
    sub = _iota((tm, 1), 0)
    a = jnp.where(i * tm + sub >= N_PAD, a_ref[...].astype(F32), 0.0)
    halo = halo_ref[...].astype(F32)
    hrow = i * tm - 8 + _iota((8, 1), 0)
    halo = jnp.where((hrow >= N_PAD) & (i > 0), halo, 0.0)
    a1 = jnp.where(sub == 0, _pick_row(halo, 7), pltpu.roll(a, 1, 0))
    a2 = jnp.where(sub == 0, _pick_row(halo, 6), jnp.where(sub == 1, _pick_row(halo, 7), pltpu.roll(a, 2, 0)))
    acc = cb_ref[...] + a2 * cw_refs[0][...]
    acc = acc + a1 * cw_refs[1][...]
    acc = acc + a * cw_refs[2][...]
    return a, a1, a2, acc


def _ffn_down_loss(up, conv_w, conv_b, w_down, h1, final_g, target):
    L = h1.shape[0]
    tm = _row_tile(L)
    cw = [conv_w[j:j + 1] for j in range(3)]

    def body(a_ref, halo_ref, b_ref, cw0, cw1, cw2, cb_ref, wd_ref, h1_ref, gf_ref, t_ref,
             g_ref, dh_ref, dhb_ref, dgf_ref, loss_ref):
        i = pl.program_id(0)

        @pl.when(i == 0)
        def _():
            dgf_ref[...] = jnp.zeros_like(dgf_ref)
            loss_ref[...] = jnp.zeros_like(loss_ref)

        _, _, _, acc = _conv_acc(a_ref, halo_ref, (cw0, cw1, cw2), cb_ref, i, tm)
        g = (acc * jax.nn.sigmoid(acc) * b_ref[...].astype(F32)).astype(BF16)
        g_ref[...] = g
        h2 = h1_ref[...] + _dot(g, wd_ref[...])
        r = lax.rsqrt(jnp.mean(h2 * h2, axis=-1, keepdims=True) + EPS)
        yn = h2 * r
        gf = gf_ref[...]
        live = i * tm + _iota((tm, 1), 0) >= PREFIX
        err = jnp.where(live, yn * gf - t_ref[...], 0.0)
        loss_ref[...] = loss_ref[...] + 0.5 * jnp.sum(jnp.mean(err * err, axis=-1, keepdims=True))
        dy = err * (1.0 / D_MODEL)
        dgf_ref[...] = dgf_ref[...] + jnp.sum(dy * yn, axis=0, keepdims=True)
        dyn = dy * gf
        dh = r * (dyn - yn * jnp.mean(dyn * yn, axis=-1, keepdims=True))
        dh_ref[...] = dh
        dhb_ref[...] = dh.astype(BF16)

    rows = lambda w, c=0: pl.BlockSpec((tm, w), lambda i: (i, c))
    halo = pl.BlockSpec((8, D_FF), lambda i: (jnp.maximum(i * (tm // 8) - 1, 0), 0))
    return pl.pallas_call(
        body, name="f_ffn_down_loss", grid=(L // tm,),
        in_specs=[rows(D_FF), halo, rows(D_FF, 1), _full((1, D_FF)), _full((1, D_FF)), _full((1, D_FF)),
                  _full((1, D_FF)), _full((D_FF, D_MODEL)), rows(D_MODEL), _full((1, D_MODEL)), rows(D_MODEL)],
        out_specs=[rows(D_FF), rows(D_MODEL), rows(D_MODEL), _full((1, D_MODEL)), _full((1, BLK))],
        out_shape=[jax.ShapeDtypeStruct((L, D_FF), BF16), jax.ShapeDtypeStruct((L, D_MODEL), F32),
                   jax.ShapeDtypeStruct((L, D_MODEL), BF16), jax.ShapeDtypeStruct((1, D_MODEL), F32),
                   jax.ShapeDtypeStruct((1, BLK), F32)],
        compiler_params=_params(("arbitrary",)),
    )(up, up, up, cw[0], cw[1], cw[2], conv_b, w_down, h1, final_g, target)


def _ffn_bwd_gate(dh2b, w_down, up, conv_w, conv_b):
    L = dh2b.shape[0]
    tm = _row_tile(L)
    cw = [conv_w[j:j + 1] for j in range(3)]

    def body(dh_ref, wd_ref, a_ref, halo_ref, b_ref, cw0, cw1, cw2, cb_ref, dacc_ref, db_ref, dcw_ref):
        i = pl.program_id(0)

        @pl.when(i == 0)
        def _():
            dcw_ref[...] = jnp.zeros_like(dcw_ref)

        a, a1, a2, acc = _conv_acc(a_ref, halo_ref, (cw0, cw1, cw2), cb_ref, i, tm)
        dg = _dot_nt(dh_ref[...], wd_ref[...])
        sg = jax.nn.sigmoid(acc)
        db_ref[...] = (dg * acc * sg).astype(BF16)
        dacc = dg * b_ref[...].astype(F32) * (sg * (1.0 + acc * (1.0 - sg)))
        dacc_ref[...] = dacc.astype(BF16)
        sub8 = _iota((8, 1), 0)
        rows = [jnp.sum(dacc * t, axis=0, keepdims=True) for t in (a2, a1, a)] + [jnp.sum(dacc, axis=0, keepdims=True)]
        upd = jnp.zeros((8, D_FF), F32)
        for j, rj in enumerate(rows):
            upd = upd + jnp.where(sub8 == j, rj, 0.0)
        dcw_ref[...] = dcw_ref[...] + upd

    rows = lambda w, c=0: pl.BlockSpec((tm, w), lambda i: (i, c))
    halo = pl.BlockSpec((8, D_FF), lambda i: (jnp.maximum(i * (tm // 8) - 1, 0), 0))
    return pl.pallas_call(
        body, name="b_ffn_gate", grid=(L // tm,),
        in_specs=[rows(D_MODEL), _full((D_FF, D_MODEL)), rows(D_FF), halo, rows(D_FF, 1),
                  _full((1, D_FF)), _full((1, D_FF)), _full((1, D_FF)), _full((1, D_FF))],
        out_specs=[rows(D_FF), rows(D_FF), _full((8, D_FF))],
        out_shape=[jax.ShapeDtypeStruct((L, D_FF), BF16), jax.ShapeDtypeStruct((L, D_FF), BF16),
                   jax.ShapeDtypeStruct((8, D_FF), F32)],
        compiler_params=_params(("arbitrary",)),
    )(dh2b, w_down, up, up, up, cw[0], cw[1], cw[2], conv_b)


def _ffn_bwd_up(dacc, db, conv_w, w_up, h1, ffn_g, dh2, w_out):
    L = h1.shape[0]
    tm = _row_tile(L)
    nt = L // tm
    shard = w_up.shape[2]
    cw = [conv_w[j:j + 1] for j in range(3)]

    def body(da_ref, halo_ref, db_ref, cw0, cw1, cw2, wu_ref, h1_ref, g_ref, dh2_ref, wo_ref,
             dup_ref, dh1_ref, dh1b_ref, dmix_ref, dg_ref):
        i = pl.program_id(0)

        @pl.when(i == 0)
        def _():
            dg_ref[...] = jnp.zeros_like(dg_ref)

        sub = _iota((tm, 1), 0)
        d0 = da_ref[...].astype(F32)
        halo = jnp.where(i < nt - 1, halo_ref[...].astype(F32), 0.0)
        d1 = jnp.where(sub == tm - 1, _pick_row(halo, 0), pltpu.roll(d0, tm - 1, 0))
        d2 = jnp.where(sub == tm - 2, _pick_row(halo, 0),
                       jnp.where(sub == tm - 1, _pick_row(halo, 1), pltpu.roll(d0, tm - 2, 0)))
        da = d0 * cw2[...] + d1 * cw1[...] + d2 * cw0[...]
        da = jnp.where(i * tm + sub >= N_PAD, da, 0.0).astype(BF16)
        dup_ref[:, 0:D_FF] = da
        dbv = db_ref[...]
        dup_ref[:, D_FF:2 * D_FF] = dbv
        dn = jnp.zeros((tm, D_MODEL), F32)
        for j in range(N_CHIPS):
            src = da if j < 2 else dbv
            lo = (j % 2) * shard
            dn = dn + _dot_nt(src[:, lo:lo + shard], wu_ref[j])
        h1 = h1_ref[...]
        r = lax.rsqrt(jnp.mean(h1 * h1, axis=-1, keepdims=True) + EPS)
        yn = h1 * r
        dg_ref[...] = dg_ref[...] + jnp.sum(dn * yn, axis=0, keepdims=True)
        dyn = dn * g_ref[...]
        dh1 = dh2_ref[...] + r * (dyn - yn * jnp.mean(dyn * yn, axis=-1, keepdims=True))
        dh1_ref[...] = dh1
        dh1b = dh1.astype(BF16)
        dh1b_ref[...] = dh1b
        dmix_ref[...] = _dot_nt(dh1b, wo_ref[...]).astype(BF16)

    rows = lambda w: pl.BlockSpec((tm, w), lambda i: (i, 0))
    halo = pl.BlockSpec((8, D_FF), lambda i: (jnp.minimum((i + 1) * (tm // 8), L // 8 - 1), 0))
    return pl.pallas_call(
        body, name="b_ffn_up", grid=(nt,),
        in_specs=[rows(D_FF), halo, rows(D_FF), _full((1, D_FF)), _full((1, D_FF)), _full((1, D_FF)),
                  _full((N_CHIPS, D_MODEL, shard)), rows(D_MODEL), _full((1, D_MODEL)), rows(D_MODEL),
                  _full((D_MODEL, D_MODEL))],
        out_specs=[rows(2 * D_FF), rows(D_MODEL), rows(D_MODEL), rows(D_MODEL), _full((1, D_MODEL))],
        out_shape=[jax.ShapeDtypeStruct((L, 2 * D_FF), BF16), jax.ShapeDtypeStruct((L, D_MODEL), F32),
                   jax.ShapeDtypeStruct((L, D_MODEL), BF16), jax.ShapeDtypeStruct((L, D_MODEL), BF16),
                   jax.ShapeDtypeStruct((1, D_MODEL), F32)],
        compiler_params=_params(("arbitrary",)),
    )(dacc, dacc, db, cw[0], cw[1], cw[2], w_up, h1, ffn_g, dh2, w_out)


def _wgrad(a, b, name, tn=None):
    L, K = a.shape
    N = b.shape[1]
    tn = N if tn is None else tn
    tl = _row_tile(L)

    def body(a_ref, b_ref, o_ref):
        @pl.when(pl.program_id(1) == 0)
        def _():
            o_ref[...] = jnp.zeros_like(o_ref)

        o_ref[0] = o_ref[0] + _dot_tn(a_ref[...], b_ref[...])

    return pl.pallas_call(
        body, name=name, grid=(N // tn, L // tl),
        in_specs=[pl.BlockSpec((tl, K), lambda n, l: (l, 0)), pl.BlockSpec((tl, tn), lambda n, l: (l, n))],
        out_specs=pl.BlockSpec((1, K, tn), lambda n, l: (n, 0, 0)),
        out_shape=jax.ShapeDtypeStruct((N // tn, K, tn), F32),
        compiler_params=_params(("parallel", "arbitrary")),
    )(a, b)


def _retention_bwd(dmix, o, proj, cos_t, sin_t, ret_g, states):
    L = proj.shape[0]
    nblk = L // BLK
    dmat, wq_t, wk_t, g_blk = _decay_tables()

    def body(dm_ref, o_ref, q_ref, k_ref, v_ref, gate_ref, cos_ref, sin_ref, d_ref, wq_ref, wk_ref, rg_ref, rs_ref,
             dp_ref, drg_ref, gstate):
        i = pl.program_id(0)

        @pl.when(i == 0)
        def _():
            gstate[...] = jnp.zeros_like(gstate)
            drg_ref[...] = jnp.zeros_like(drg_ref)

        rot, rot_t = _rot_fns(cos_ref[...], sin_ref[...])
        lane = _iota((BLK, BLK), 1)
        sub = _iota((BLK, BLK), 0)
        scale = HEAD_LANES ** -0.5
        for p in range(2):
            qr = rot(q_ref[:, p * BLK:(p + 1) * BLK].astype(F32))
            kr = rot(k_ref[:, p * BLK:(p + 1) * BLK].astype(F32)) * scale
            kr_b = kr.astype(BF16)
            qw = (qr * wq_ref[p]).astype(BF16)
            kw = (kr * wk_ref[p]).astype(BF16)
            dqr = jnp.zeros((BLK, BLK), F32)
            dkr = jnp.zeros((BLK, BLK), F32)
            for e in range(2):
                h = 2 * p + e
                cols = slice(h * BLK, (h + 1) * BLK)
                head_lanes = (lane >> 6) == e
                o = o_ref[:, cols]
                rn = lax.rsqrt(jnp.mean(o * o, axis=-1, keepdims=True) + EPS)
                y = o * rn
                gate = gate_ref[:, cols].astype(F32)
                sg = jax.nn.sigmoid(gate)
                dm = dm_ref[:, cols].astype(F32)
                rgain = rg_ref[:, cols]
                drg_ref[:, cols] = drg_ref[:, cols] + jnp.sum(dm * y * (gate * sg), axis=0, keepdims=True)
                dp_ref[:, 1024 + h * BLK:1024 + (h + 1) * BLK] = (
                    dm * y * rgain * (sg * (1.0 + gate * (1.0 - sg)))).astype(BF16)
                dy = dm * rgain * (gate * sg)
                do = (rn * (dy - y * jnp.mean(dy * y, axis=-1, keepdims=True))).astype(BF16)
                vh = v_ref[:, cols]
                qm = jnp.where(head_lanes, qr, 0.0).astype(BF16)
                dmh = d_ref[h]
                s = (_dot_nt(qm, kr_b) * dmh).astype(BF16)
                ds = (_dot_nt(do, vh) * dmh).astype(BF16)
                st = rs_ref[0, h].astype(BF16)
                gs = gstate[h]
                gs_b = gs.astype(BF16)
                dqr = dqr + jnp.where(head_lanes, _dot(ds, kr_b), 0.0) + _dot_nt(do, st) * wq_ref[p]
                dkr = dkr + _dot_tn(ds, qm) + _dot_nt(vh, gs_b) * wk_ref[p]
                dp_ref[:, 512 + h * BLK:512 + (h + 1) * BLK] = (_dot_tn(s, do) + _dot(kw, gs_b)).astype(BF16)
                dr = jnp.where((sub >> 6) == e, _dot_tn(qw, do), 0.0)
                gstate[h] = dr + g_blk[h] * gs
            dp_ref[:, p * BLK:(p + 1) * BLK] = rot_t(dqr).astype(BF16)
            dp_ref[:, 256 + p * BLK:256 + (p + 1) * BLK] = (rot_t(dkr) * scale).astype(BF16)

    row = lambda c: (lambda i: (nblk - 1 - i, c))
    return pl.pallas_call(
        body, name="b_retention", grid=(nblk,),
        in_specs=[pl.BlockSpec((BLK, 512), row(0)), pl.BlockSpec((BLK, 512), row(0)),
                  pl.BlockSpec((BLK, 256), row(0)), pl.BlockSpec((BLK, 256), row(1)),
                  pl.BlockSpec((BLK, 512), row(1)), pl.BlockSpec((BLK, 512), row(2)),
                  pl.BlockSpec((BLK, BLK), row(0)), pl.BlockSpec((BLK, BLK), row(0)),
                  _full((RET_HEADS, BLK, BLK)), _full((2, BLK, BLK)), _full((2, BLK, BLK)), _full((1, 512)),
                  pl.BlockSpec((1, RET_HEADS, BLK, BLK), lambda i: (nblk - 1 - i, 0, 0, 0))],
        out_specs=[pl.BlockSpec((BLK, RET_W), row(0)), _full((1, 512))],
        out_shape=[jax.ShapeDtypeStruct((L, RET_W), BF16), jax.ShapeDtypeStruct((1, 512), F32)],
        scratch_shapes=[pltpu.VMEM((RET_HEADS, BLK, BLK), F32)],
        compiler_params=_params(("arbitrary",)),
    )(dmix, o, proj, proj, proj, proj, cos_t, sin_t, dmat, wq_t, wk_t, ret_g, states)


def _fox_delta(dmix, o_f):
    L = o_f.shape[0]
    nblk = L // BLK

    def body(do_ref, o_ref, d_ref):
        prod = do_ref[...].astype(F32) * o_ref[...].astype(F32)
        sel = ((_iota((8, 512), 1) >> 6) == _iota((8, 512), 0)).astype(BF16)
        hi = prod.astype(BF16)
        lo = (prod - hi.astype(F32)).astype(BF16)
        d_ref[0] = _dot_nt(sel, hi) + _dot_nt(sel, lo)

    return pl.pallas_call(
        body, name="b_foxdelta", grid=(nblk,),
        in_specs=[pl.BlockSpec((BLK, 512), lambda i: (i, 1)), pl.BlockSpec((BLK, 512), lambda i: (i, 0))],
        out_specs=pl.BlockSpec((1, 8, BLK), lambda i: (i, 0, 0)),
        out_shape=jax.ShapeDtypeStruct((nblk, 8, BLK), F32),
        compiler_params=_params(("parallel",)),
    )(dmix, o_f)


def _fox_bwd(proj, dmix, c, ctb, lse, delta):
    L = proj.shape[0]
    nblk, nu = _fox_units(L)
    scale = HEAD_LANES ** -0.5

    def body(qkv_ref, do_ref, c_ref, ct_ref, lse_ref, dl_ref, dp_ref, dc_ref, dcq_ref,
             ktt, dqt, dk_acc, dv_acc, dcs_acc):
        p = pl.program_id(0)

        @pl.when(p == 0)
        def _():
            dc_ref[...] = jnp.zeros_like(dc_ref)
            dcq_ref[...] = jnp.zeros_like(dcq_ref)

        lane = _iota((BLK, BLK), 1)
        sub8 = _iota((8, BLK), 0)
        masks = _fox_tile_masks()

        def pre(j, carry):
            off = pl.multiple_of(j * BLK, BLK)
            ktt[j] = qkv_ref[pl.ds(off, BLK), BLK:2 * BLK].astype(F32).T.astype(BF16)
            dqt[j] = jnp.zeros((BLK, BLK), F32)
            return carry

        lax.fori_loop(0, nblk, pre, 0)

        def kv_pass(kblk, nk, n_later):
            klen = nk * BLK
            koff = pl.multiple_of(kblk * BLK, BLK)
            kt = qkv_ref[pl.ds(koff, klen), BLK:2 * BLK]
            vtile = qkv_ref[pl.ds(koff, klen), 2 * BLK:3 * BLK]
            ct = c_ref[pl.ds(koff, klen), :]
            klane = _iota((klen, BLK), 1)
            cs = [jnp.broadcast_to(jnp.sum(jnp.where(klane == 2 * p + e, ct, 0.0), axis=1, keepdims=True), (klen, UNIT))
                  for e in range(2)]
            dk_acc[0:klen] = jnp.zeros((klen, BLK), F32)
            dv_acc[0:klen] = jnp.zeros((klen, BLK), F32)
            for e in range(2):
                dcs_acc[e, 0:klen] = jnp.zeros((klen, BLK), F32)

            def tile(qblk, nq, mask):
                qlen = nq * BLK
                qoff = pl.multiple_of(qblk * BLK, BLK)
                qs = qkv_ref[pl.ds(qoff, qlen), 0:BLK].astype(F32) * scale
                dot_ = do_ref[pl.ds(qoff, qlen), :]
                qlane = _iota((qlen, BLK), 1)
                stats = [[ref[qblk + a] for a in range(nq)] for ref in (ct_ref, lse_ref, dl_ref)]
                for e in range(2):
                    h = 2 * p + e
                    head = (qlane >> 6) == e
                    ct_row, lse_row, dl_row = [jnp.concatenate([_pick_row(t, h) for t in ts], axis=1) for ts in stats]
                    qm = jnp.where(head, qs, 0.0).astype(BF16)
                    dom = jnp.where(head, dot_, jnp.zeros_like(dot_))
                    t = _dot_nt(kt, qm) - cs[e][:, 0:qlen]
                    if mask is not None:
                        t = jnp.where(mask, t, NEG)
                    pr = jnp.exp(t + (ct_row - lse_row))
                    dv_acc[0:klen] = dv_acc[0:klen] + _dot(pr.astype(BF16), dom)
                    dsv = pr * (_dot_nt(vtile, dom) - dl_row)
                    ds_b = dsv.astype(BF16)
                    dk_acc[0:klen] = dk_acc[0:klen] + _dot(ds_b, qm)
                    rows = slice(e * HEAD_LANES, (e + 1) * HEAD_LANES)
                    dq_t = _dot(ktt[kblk, rows, :], ds_b[0:BLK])
                    for b in range(1, nk):
                        dq_t = dq_t + _dot(ktt[kblk + b, rows, :], ds_b[b * BLK:(b + 1) * BLK])
                    key_side = dsv[:, 0:BLK]
                    for a in range(1, nq):
                        key_side = key_side + dsv[:, a * BLK:(a + 1) * BLK]
                    dcs_acc[e, 0:klen] = dcs_acc[e, 0:klen] + key_side
                    query_side = jnp.sum(dsv, axis=0, keepdims=True)
                    for a in range(nq):
                        cols = slice(a * BLK, (a + 1) * BLK)
                        dqt[qblk + a, rows, :] = dqt[qblk + a, rows, :] + dq_t[:, cols]
                        dcq_ref[qblk + a] = dcq_ref[qblk + a] + jnp.where(sub8 == h, query_side[:, cols], 0.0)

            def later(i, carry):
                tile(kblk + nk + 2 * i, 2, masks["valid"] if nk == 1 else None)
                return carry

            tile(kblk, nk, masks["first"] if nk == 1 else masks["diag"])
            lax.fori_loop(0, n_later, later, 0)
            dp_ref[pl.ds(koff, klen), BLK:2 * BLK] = dk_acc[0:klen].astype(BF16)
            dp_ref[pl.ds(koff, klen), 2 * BLK:3 * BLK] = dv_acc[0:klen].astype(BF16)
            upd = jnp.zeros((klen, BLK), F32)
            for e in range(2):
                upd = upd + jnp.where(klane == 2 * p + e, -jnp.sum(dcs_acc[e, 0:klen], axis=1, keepdims=True), 0.0)
            dc_ref[pl.ds(koff, klen), :] = dc_ref[pl.ds(koff, klen), :] + upd

        kv_pass(0, 1, nu)

        def k_loop(u, carry):
            kv_pass(1 + 2 * u, 2, nu - 1 - u)
            return carry

        lax.fori_loop(0, nu, k_loop, 0)

        def flush(j, carry):
            off = pl.multiple_of(j * BLK, BLK)
            dp_ref[pl.ds(off, BLK), 0:BLK] = (dqt[j].T * scale).astype(BF16)
            return carry

        lax.fori_loop(0, nblk, flush, 0)

    stat = _full((nblk, 8, BLK))
    return pl.pallas_call(
        body, name="b_fox", grid=(FOX_HEADS // 2,),
        in_specs=[pl.BlockSpec((L, 384), lambda p: (0, RET_W // 384 + p)), pl.BlockSpec((L, BLK), lambda p: (0, 4 + p)),
                  _full((L, BLK)), stat, stat, stat],
        out_specs=[pl.BlockSpec((L, 384), lambda p: (0, p)), _full((L, BLK)), stat],
        out_shape=[jax.ShapeDtypeStruct((L, FOX_W), BF16), jax.ShapeDtypeStruct((L, BLK), F32),
                   jax.ShapeDtypeStruct((nblk, 8, BLK), F32)],
        scratch_shapes=[pltpu.VMEM((nblk, BLK, BLK), BF16), pltpu.VMEM((nblk, BLK, BLK), F32),
                        pltpu.VMEM((UNIT, BLK), F32), pltpu.VMEM((UNIT, BLK), F32), pltpu.VMEM((2, UNIT, BLK), F32)],
        compiler_params=_params(("arbitrary",)),
    )(proj, dmix, c, ctb, lse, delta)


def _fox_post(dc, dcq, ff, fb):
    L = dc.shape[0]
    nblk = L // BLK

    def body(dc_ref, dcq_ref, ff_ref, b_ref, dff_ref, dffb_ref, dfb_ref, carry):
        i = pl.program_id(0)

        @pl.when(i == 0)
        def _():
            carry[...] = jnp.zeros_like(carry)
            dfb_ref[...] = jnp.zeros_like(dfb_ref)

        d = dc_ref[...] + jnp.concatenate([dcq_ref[0], jnp.zeros((BLK - 8, BLK), F32)], axis=0).T
        tri = (_iota((BLK, BLK), 0) <= _iota((BLK, BLK), 1)).astype(BF16)
        hi, mid, lo = _split3(d)
        dlf = _dot(tri, hi) + _dot(tri, mid) + _dot(tri, lo) + carry[...]
        carry[...] = carry[...] + jnp.sum(d, axis=0, keepdims=True)
        z = ff_ref[...] + b_ref[...]
        dff = jnp.where(_iota((BLK, BLK), 1) < FOX_HEADS, dlf * jax.nn.sigmoid(-z), 0.0)
        dff_ref[...] = dff
        dffb_ref[...] = dff.astype(BF16)
        dfb_ref[...] = dfb_ref[...] + jnp.sum(dff, axis=0, keepdims=True)

    rev = lambda i: (nblk - 1 - i, 0)
    return pl.pallas_call(
        body, name="b_foxpost", grid=(nblk,),
        in_specs=[pl.BlockSpec((BLK, BLK), rev), pl.BlockSpec((1, 8, BLK), lambda i: (nblk - 1 - i, 0, 0)),
                  pl.BlockSpec((BLK, BLK), rev), _full((1, BLK))],
        out_specs=[pl.BlockSpec((BLK, BLK), rev), pl.BlockSpec((BLK, BLK), rev), _full((1, BLK))],
        out_shape=[jax.ShapeDtypeStruct((L, BLK), F32), jax.ShapeDtypeStruct((L, BLK), BF16),
                   jax.ShapeDtypeStruct((1, BLK), F32)],
        scratch_shapes=[pltpu.VMEM((1, BLK), F32)],
        compiler_params=_params(("arbitrary",)),
    )(dc, dcq, ff, fb)


def _inproj_bwd(dpr, dpf, dffb, w_main, w_ff, h0, g, dh1):
    L = h0.shape[0]
    tm = _row_tile(L)

    def body(dpr_ref, dpf_ref, dff_ref, wm_ref, wf_ref, h_ref, g_ref, dh1_ref, dh0_ref, dg_ref):
        @pl.when(pl.program_id(0) == 0)
        def _():
            dg_ref[...] = jnp.zeros_like(dg_ref)

        dn = (_dot_nt(dpr_ref[...], wm_ref[:, 0:RET_W]) + _dot_nt(dpf_ref[...], wm_ref[:, RET_W:MAIN_W])
              + _dot_nt(dff_ref[...], wf_ref[...]))
        h = h_ref[...]
        r = lax.rsqrt(jnp.mean(h * h, axis=-1, keepdims=True) + EPS)
        yn = h * r
        dg_ref[...] = dg_ref[...] + jnp.sum(dn * yn, axis=0, keepdims=True)
        dyn = dn * g_ref[...]
        dh0_ref[...] = dh1_ref[...] + r * (dyn - yn * jnp.mean(dyn * yn, axis=-1, keepdims=True))

    rows = lambda w: pl.BlockSpec((tm, w), lambda i: (i, 0))
    return pl.pallas_call(
        body, name="b_inproj", grid=(L // tm,),
        in_specs=[rows(RET_W), rows(FOX_W), rows(BLK), _full((D_MODEL, MAIN_W)), _full((D_MODEL, BLK)),
                  rows(D_MODEL), _full((1, D_MODEL)), rows(D_MODEL)],
        out_specs=[rows(D_MODEL), _full((1, D_MODEL))],
        out_shape=[jax.ShapeDtypeStruct((L, D_MODEL), F32), jax.ShapeDtypeStruct((1, D_MODEL), F32)],
        compiler_params=_params(("arbitrary",)),
    )(dpr, dpf, dffb, w_main, w_ff, h0, g, dh1)


def _local_step(x, target, meta, attn_g, w_main, w_ff, fox_b, ret_g, w_out, ffn_g, w_up, conv_w, conv_b, w_down, final_g):
    S = x.shape[0]
    L = S + PREFIX
    h0 = jnp.concatenate([jnp.zeros((N_PAD, D_MODEL), F32), meta, x], axis=0)
    tgt = jnp.concatenate([jnp.zeros((PREFIX, D_MODEL), F32), target], axis=0)
    fb = jnp.pad(fox_b, ((0, 0), (0, BLK - FOX_HEADS)))
    cos_t, sin_t = _rotary_tables(L)

    n1, proj, ff = _rms_inproj(h0, attn_g, w_main, w_ff)
    c, ctb = _fox_prep(ff, fb)
    mix_r, o_ret, states = _retention_fwd(proj, cos_t, sin_t, ret_g)
    o_f, lse = _fox_fwd(proj, c, ctb)
    h1, n2, up = _outproj_up(mix_r, o_f, h0, w_out, ffn_g, w_up)
    g_act, dh2, dh2b, d_final_g, loss = _ffn_down_loss(up, conv_w, conv_b, w_down, h1, final_g, tgt)

    dacc, db, dconv = _ffn_bwd_gate(dh2b, w_down, up, conv_w, conv_b)
    dup, dh1, dh1b, dmix, d_ffn_g = _ffn_bwd_up(dacc, db, conv_w, w_up, h1, ffn_g, dh2, w_out)
    d_w_down = _wgrad(g_act, dh2b, "wgrad_down", tn=None)[0]
    d_w_up = _wgrad(n2, dup, "wgrad_up", tn=w_up.shape[2])
    d_w_out = jnp.concatenate([_wgrad(mix_r, dh1b, "wgrad_out_r")[0], _wgrad(o_f, dh1b, "wgrad_out_f")[0]], axis=0)

    dpr, d_ret_g = _retention_bwd(dmix, o_ret, proj, cos_t, sin_t, ret_g, states)
    delta = _fox_delta(dmix, o_f)
    dpf, dc, dcq = _fox_bwd(proj, dmix, c, ctb, lse, delta)
    dff, dffb, d_fox_b = _fox_post(dc, dcq, ff, fb)
    dh0, d_attn_g = _inproj_bwd(dpr, dpf, dffb, w_main, w_ff, h0, attn_g, dh1)
    d_w_main = jnp.concatenate([_wgrad(n1, dpr, "wgrad_in_r")[0], _wgrad(n1, dpf, "wgrad_in_f")[0]], axis=1)
    d_w_ff = _wgrad(n1, dffb, "wgrad_in_ff")[0]

    return dict(
        loss=loss[0, 0], dx=dh0[PREFIX:], dmeta=dh0[N_PAD:PREFIX], attn_g=d_attn_g, w_main=d_w_main,
        w_ff=d_w_ff[:, :FOX_HEADS], fox_b=d_fox_b[:, :FOX_HEADS], ret_g=d_ret_g, w_out=d_w_out, ffn_g=d_ffn_g,
        w_up=d_w_up, conv_w=dconv[0:3], conv_b=dconv[3:4], w_down=d_w_down, final_g=d_final_g)


_ANY = pl.BlockSpec(memory_space=pl.ANY)


def _place():
    return lax.axis_index("x"), lax.axis_index("y"), lax.axis_index("c")


def _other_chips(x, y):
    return [(1 - x, y), (x, 1 - y), (1 - x, 1 - y)]


def _chip_allgather(arrays):
    n = len(arrays)

    def body(*refs):
        ins, outs = refs[:n], refs[n:2 * n]
        send, recv, loc = refs[2 * n:]
        x, y, c = _place()
        mine = 2 * x + y
        peers = _other_chips(x, y)

        def remote(a, k, slot):
            return pltpu.make_async_remote_copy(
                src_ref=ins[a], dst_ref=outs[a].at[slot], send_sem=send.at[3 * a + k], recv_sem=recv.at[3 * a + k],
                device_id=(peers[k][0], peers[k][1], c), device_id_type=MESH)

        local = [pltpu.make_async_copy(ins[a], outs[a].at[mine], loc.at[a]) for a in range(n)]
        sends = [remote(a, k, mine) for a in range(n) for k in range(3)]
        for cp in local + sends:
            cp.start()
        for a in range(n):
            for k in range(3):
                remote(a, k, 2 * peers[k][0] + peers[k][1]).wait_recv()
        for cp in sends:
            cp.wait_send()
        for cp in local:
            cp.wait()

    return pl.pallas_call(
        body, name="ag_weights", in_specs=[_ANY] * n, out_specs=[_ANY] * n,
        out_shape=[jax.ShapeDtypeStruct((N_CHIPS,) + a.shape, a.dtype) for a in arrays],
        scratch_shapes=[pltpu.SemaphoreType.DMA((3 * n,)), pltpu.SemaphoreType.DMA((3 * n,)),
                        pltpu.SemaphoreType.DMA((n,))],
    )(*arrays)


def _sibling_exchange(grads, small):
    n = len(grads)

    def body(*refs):
        ins, small_in = refs[:n], refs[n]
        outs, small_out = refs[n + 1:2 * n + 1], refs[2 * n + 1]
        send, recv, s_send, s_recv, loc = refs[2 * n + 2:]
        x, y, c = _place()
        me = 4 * x + 2 * y + c

        def half_copy(a, which):
            half = ins[a].shape[1] // 2
            return pltpu.make_async_remote_copy(
                src_ref=ins[a].at[pl.ds(0, N_CHIPS), pl.ds(which * half, half)], dst_ref=outs[a],
                send_sem=send.at[a], recv_sem=recv.at[a], device_id=(x, y, 1 - c), device_id_type=MESH)

        def peer_of(r):
            return tuple(1 - v if (r >> b) & 1 else v for v, b in ((x, 2), (y, 1), (c, 0)))

        def small_copy(r, slot):
            return pltpu.make_async_remote_copy(
                src_ref=small_in, dst_ref=small_out.at[slot], send_sem=s_send.at[r - 1], recv_sem=s_recv.at[r - 1],
                device_id=peer_of(r), device_id_type=MESH)

        local = pltpu.make_async_copy(small_in, small_out.at[me], loc.at[0])
        sends = [half_copy(a, 1 - c) for a in range(n)] + [small_copy(r, me) for r in range(1, N_DEV)]
        local.start()
        for cp in sends:
            cp.start()
        for r in range(1, N_DEV):
            px, py, pc = peer_of(r)
            small_copy(r, 4 * px + 2 * py + pc).wait_recv()
        for a in range(n):
            half_copy(a, c).wait_recv()
        for cp in sends:
            cp.wait_send()
        local.wait()

    rows = small.shape[0]
    return pl.pallas_call(
        body, name="rs_sibling", in_specs=[_ANY] * (n + 1), out_specs=[_ANY] * (n + 1),
        out_shape=[jax.ShapeDtypeStruct((N_CHIPS, g.shape[1] // 2, g.shape[2]), g.dtype) for g in grads]
        + [jax.ShapeDtypeStruct((N_DEV, rows, small.shape[1]), small.dtype)],
        scratch_shapes=[pltpu.SemaphoreType.DMA((n,)), pltpu.SemaphoreType.DMA((n,)),
                        pltpu.SemaphoreType.DMA((N_DEV - 1,)), pltpu.SemaphoreType.DMA((N_DEV - 1,)),
                        pltpu.SemaphoreType.DMA((1,))],
    )(*grads, small)


def _chip_reduce_scatter(parts):
    n = len(parts)

    def body(*refs):
        ins, outs = refs[:n], refs[n:2 * n]
        send, recv = refs[2 * n:]
        x, y, c = _place()
        peers = _other_chips(x, y)

        def remote(a, k):
            return pltpu.make_async_remote_copy(
                src_ref=ins[a].at[2 * peers[k][0] + peers[k][1]], dst_ref=outs[a].at[k], send_sem=send.at[3 * a + k],
                recv_sem=recv.at[3 * a + k], device_id=(peers[k][0], peers[k][1], c), device_id_type=MESH)

        copies = [remote(a, k) for a in range(n) for k in range(3)]
        for cp in copies:
            cp.start()
        for cp in copies:
            cp.wait_recv()
        for cp in copies:
            cp.wait_send()

    return pl.pallas_call(
        body, name="rs_chips", in_specs=[_ANY] * n, out_specs=[_ANY] * n,
        out_shape=[jax.ShapeDtypeStruct((3,) + p.shape[1:], p.dtype) for p in parts],
        scratch_shapes=[pltpu.SemaphoreType.DMA((3 * n,)), pltpu.SemaphoreType.DMA((3 * n,))],
    )(*parts)


def _sibling_allgather(bufs):
    n = len(bufs)

    def body(*refs):
        outs = refs[n:2 * n]
        send, recv = refs[2 * n:]
        x, y, c = _place()

        def remote(a, which):
            return pltpu.make_async_remote_copy(
                src_ref=outs[a].at[which], dst_ref=outs[a].at[which], send_sem=send.at[a], recv_sem=recv.at[a],
                device_id=(x, y, 1 - c), device_id_type=MESH)

        sends = [remote(a, c) for a in range(n)]
        for cp in sends:
            cp.start()
        for a in range(n):
            remote(a, 1 - c).wait_recv()
        for cp in sends:
            cp.wait_send()

    outs = pl.pallas_call(
        body, name="ag_sibling", in_specs=[_ANY] * n, out_specs=[_ANY] * n,
        out_shape=[jax.ShapeDtypeStruct(b.shape, b.dtype) for b in bufs],
        input_output_aliases={a: a for a in range(n)},
        scratch_shapes=[pltpu.SemaphoreType.DMA((n,)), pltpu.SemaphoreType.DMA((n,))],
    )(*bufs)
    return [o.reshape(2 * o.shape[1], o.shape[2]) for o in outs]


def _pair_add(full, recv, core, name):
    _, R, C = full.shape
    half = R // 2

    def body(core_ref, a_ref, b_ref, o_ref):
        o_ref[...] = (a_ref[...] + b_ref[...]).astype(BF16)

    return pl.pallas_call(
        body, name=name,
        grid_spec=pltpu.PrefetchScalarGridSpec(
            num_scalar_prefetch=1, grid=(N_CHIPS,),
            in_specs=[pl.BlockSpec((1, half, C), lambda j, core_ref: (j, core_ref[0], 0)),
                      pl.BlockSpec((1, half, C), lambda j, core_ref: (j, 0, 0))],
            out_specs=pl.BlockSpec((1, half, C), lambda j, core_ref: (j, 0, 0))),
        out_shape=jax.ShapeDtypeStruct((N_CHIPS, half, C), BF16),
        compiler_params=_params(("parallel",)),
    )(core, full, recv)


def _sum_slots(q, name, tiles=2):
    n, R, C = q.shape
    tr = R // tiles

    def body(q_ref, o_ref):
        acc = q_ref[0].astype(F32)
        for j in range(1, n):
            acc = acc + q_ref[j].astype(F32)
        o_ref[...] = acc

    return pl.pallas_call(
        body, name=name, grid=(tiles,),
        in_specs=[pl.BlockSpec((n, tr, C), lambda i: (0, i, 0))],
        out_specs=pl.BlockSpec((tr, C), lambda i: (i, 0)),
        out_shape=jax.ShapeDtypeStruct((R, C), F32),
        compiler_params=_params(("parallel",)),
    )(q)


def _sum_partials(own_all, recv, place, name, tiles=2):
    _, R, C = own_all.shape
    tr = R // tiles

    def body(place_ref, own_ref, r_ref, o_ref):
        acc = own_ref[0].astype(F32)
        for k in range(3):
            acc = acc + r_ref[k].astype(F32)
        o_ref[0] = acc

    return pl.pallas_call(
        body, name=name,
        grid_spec=pltpu.PrefetchScalarGridSpec(
            num_scalar_prefetch=1, grid=(tiles,),
            in_specs=[pl.BlockSpec((1, tr, C), lambda i, place_ref: (place_ref[0], i, 0)),
                      pl.BlockSpec((3, tr, C), lambda i, place_ref: (0, i, 0))],
            out_specs=pl.BlockSpec((1, tr, C), lambda i, place_ref: (place_ref[1], i, 0))),
        out_shape=jax.ShapeDtypeStruct((2, R, C), F32),
        compiler_params=_params(("parallel",)),
    )(place, own_all, recv)


def _adamw(w, g, m, v, name, tiles=4):
    R, C = w.shape
    tr = R // tiles

    def body(w_ref, g_ref, m_ref, v_ref, d_ref, m2_ref, v2_ref):
        g_ = g_ref[...]
        m2 = ADAM_B1 * m_ref[...] + (1.0 - ADAM_B1) * g_
        v2 = ADAM_B2 * v_ref[...] + (1.0 - ADAM_B2) * (g_ * g_)
        m_hat = m2 / (1.0 - ADAM_B1 ** ADAM_STEP)
        v_hat = v2 / (1.0 - ADAM_B2 ** ADAM_STEP)
        d_ref[...] = -ADAM_LR * (m_hat / (jnp.sqrt(v_hat) + ADAM_EPS) + ADAM_WD * w_ref[...])
        m2_ref[...] = m2
        v2_ref[...] = v2

    spec = pl.BlockSpec((tr, C), lambda i: (i, 0))
    return pl.pallas_call(
        body, name=name, grid=(tiles,), in_specs=[spec] * 4, out_specs=[spec] * 3,
        out_shape=[jax.ShapeDtypeStruct((R, C), F32)] * 3,
        compiler_params=_params(("parallel",)),
    )(w, g, m, v)


def _pack_rows(pieces, rows):
    flat = jnp.concatenate([jnp.pad(p.reshape(-1).astype(F32), (0, (-p.size) % D_MODEL)) for p in pieces])
    return jnp.pad(flat, (0, rows * D_MODEL - flat.size)).reshape(rows, D_MODEL)


def _unpack_rows(pack, shapes):
    flat = pack.reshape(-1)
    out, off = [], 0
    for shp in shapes:
        size = int(np.prod(shp))
        out.append(flat[off:off + size].reshape(shp))
        off += size + (-size) % D_MODEL
    return out


def _kernel_order(w):
    parts = [w[:, 0:RET_W]]
    for p in range(FOX_HEADS // 2):
        parts += [w[:, RET_W + part * 512 + p * BLK:RET_W + part * 512 + (p + 1) * BLK] for part in range(3)]
    return jnp.concatenate(parts, axis=1)


def _reference_order(g_main, g_ff):
    parts = [g_main[:, 0:RET_W]]
    for part in range(3):
        parts += [g_main[:, RET_W + 384 * p + part * BLK:RET_W + 384 * p + (part + 1) * BLK] for p in range(FOX_HEADS // 2)]
    return jnp.concatenate(parts + [g_ff], axis=1)


def kernel(x, meta_tokens, attn_norm_g, w_in, fox_forget_b, ret_norm_g, w_out, ffn_norm_g, w_up, conv_w, conv_b, w_down, final_norm_g, loss_target, m_meta_tokens, m_attn_norm_g, m_w_in, m_fox_forget_b, m_ret_norm_g, m_w_out, m_ffn_norm_g, m_w_up, m_conv_w, m_conv_b, m_w_down, m_final_norm_g, v_meta_tokens, v_attn_norm_g, v_w_in, v_fox_forget_b, v_ret_norm_g, v_w_out, v_ffn_norm_g, v_w_up, v_conv_w, v_conv_b, v_w_down, v_final_norm_g):
    chip = 2 * lax.axis_index("x") + lax.axis_index("y")
    core = lax.axis_index("c")
    meta_w, conv_sw = meta_tokens.shape[1], conv_w.shape[2]

    small_w = _pack_rows([meta_tokens, conv_w[0]], 8)
    g_in, g_out, g_up, g_down, g_small = _chip_allgather(
        [w_in[0].astype(BF16), w_out[0].astype(BF16), w_up[0].astype(BF16), w_down[0].astype(BF16), small_w])
    w_in_full = g_in.transpose(1, 0, 2).reshape(D_MODEL, IN_WIDTH)
    w_main = _kernel_order(w_in_full)
    w_ff = jnp.pad(w_in_full[:, MAIN_W:], ((0, 0), (0, BLK - FOX_HEADS)))
    small_parts = [_unpack_rows(g_small[j], [meta_tokens.shape, conv_w.shape[1:]]) for j in range(N_CHIPS)]
    meta_full = jnp.concatenate([sp[0] for sp in small_parts], axis=1)
    conv_w_full = jnp.concatenate([sp[1] for sp in small_parts], axis=1)

    out = _local_step(x[0], loss_target[0], meta_full, attn_norm_g, w_main, w_ff, fox_forget_b, ret_norm_g,
                      g_out.reshape(D_MODEL, D_MODEL), ffn_norm_g, g_up, conv_w_full, conv_b,
                      g_down.reshape(D_FF, D_MODEL), final_norm_g[None])

    big = [_reference_order(out["w_main"], out["w_ff"]).reshape(D_MODEL, N_CHIPS, -1).transpose(1, 0, 2),
           out["w_out"].reshape(N_CHIPS, -1, D_MODEL), out["w_up"], out["w_down"].reshape(N_CHIPS, -1, D_MODEL)]
    small_shapes = [(1, D_MODEL), (1, D_MODEL), (1, D_MODEL), (1, 512 + FOX_HEADS + 1), (1, D_FF), (N_META, D_MODEL), (3, D_FF)]
    small = _pack_rows([out["attn_g"], out["ffn_g"], out["final_g"],
                        jnp.concatenate([out["ret_g"], out["fox_b"], out["loss"].reshape(1, 1)], axis=1),
                        out["conv_b"], out["dmeta"], out["conv_w"]], 32)
    *from_sibling, small_all = _sibling_exchange(big, small)
    core_idx = core.reshape(1).astype(jnp.int32)
    names = ("in", "out", "up", "down")
    chip_sums = [_pair_add(g, r, core_idx, "pair_add_" + nm) for g, r, nm in zip(big, from_sibling, names)]
    from_chips = _chip_reduce_scatter(chip_sums)
    place = jnp.stack([chip, core]).astype(jnp.int32)
    totals = [_sum_partials(s, q, place, "sum_chips_" + nm) for s, q, nm in zip(chip_sums, from_chips, names)]
    grad_in, grad_out, grad_up, grad_down = _sibling_allgather(totals)
    s_attn, s_ffn, s_final, s_misc, s_conv_b, s_meta, s_conv_w = _unpack_rows(
        _sum_slots(small_all, "sum_small", tiles=1), small_shapes)
    loss = s_misc[0, 512 + FOX_HEADS]
    small_grads = [lax.dynamic_slice_in_dim(s_meta, chip * meta_w, meta_w, axis=1), s_attn, s_misc[:, 512:512 + FOX_HEADS],
                   s_misc[:, :512], s_ffn, lax.dynamic_slice_in_dim(s_conv_w, chip * conv_sw, conv_sw, axis=1)[None],
                   s_conv_b, s_final[0]]

    big_w = [(w_in, m_w_in, v_w_in, grad_in, "adamw_in"), (w_out, m_w_out, v_w_out, grad_out, "adamw_out"),
             (w_up, m_w_up, v_w_up, grad_up, "adamw_up"), (w_down, m_w_down, v_w_down, grad_down, "adamw_down")]
    big_res = [[g[None]] + [r[None] for r in _adamw(w[0], g, m[0], v[0], nm)] for w, m, v, g, nm in big_w]
    small_w_list = [meta_tokens, attn_norm_g, fox_forget_b, ret_norm_g, ffn_norm_g, conv_w, conv_b, final_norm_g]
    small_m = [m_meta_tokens, m_attn_norm_g, m_fox_forget_b, m_ret_norm_g, m_ffn_norm_g, m_conv_w, m_conv_b, m_final_norm_g]
    small_v = [v_meta_tokens, v_attn_norm_g, v_fox_forget_b, v_ret_norm_g, v_ffn_norm_g, v_conv_w, v_conv_b, v_final_norm_g]
    shapes = [a.shape for a in small_w_list]
    packs = [_pack_rows(lst, 16) for lst in (small_w_list, small_grads, small_m, small_v)]
    small_res = [_unpack_rows(r, shapes) for r in _adamw(*packs, "adamw_small", tiles=1)]
    small_grads = [g.reshape(s) for g, s in zip(small_grads, shapes)]

    def ordered(kind):
        sm = small_grads if kind == 0 else small_res[kind - 1]
        bg = [r[kind] for r in big_res]
        return [sm[0], sm[1], bg[0], sm[2], sm[3], bg[1], sm[4], bg[2], sm[5], sm[6], bg[3], sm[7]]

    return (loss, out["dx"][None], *ordered(0), *ordered(1), *ordered(2), *ordered(3))
```

```python
import functools

import numpy as np
import jax
import jax.numpy as jnp
from jax import lax
from jax.experimental import pallas as pl
from jax.experimental.pallas import tpu as pltpu

F32 = jnp.float32
BF16 = jnp.bfloat16

D_MODEL = 1024
N_META = 16
BLK = 128
UNIT = 2 * BLK
WIDE = 4
CHUNK = 64
N_PAD = BLK - N_META
PREFIX = BLK
RET_HEADS = 4
FOX_HEADS = 8
HEAD_LANES = 64
D_FF = 2816
ROPE_BASE = 10000.0
EPS = 1e-6
NEG = -1e30
RET_W = 1536
FOX_W = 1536
MAIN_W = RET_W + FOX_W
IN_WIDTH = MAIN_W + FOX_HEADS
N_CHIPS = 4
N_DEV = 8

ADAM_LR = 0.001
ADAM_B1 = 0.9
ADAM_B2 = 0.999
ADAM_EPS = 1e-08
ADAM_WD = 0.01
ADAM_STEP = 10

MESH = pl.DeviceIdType.MESH
VMEM_LIMIT_MB = 56

_NT = (((1,), (1,)), ((), ()))
_TN = (((0,), (0,)), ((), ()))


def _dot(a, b):
    return jnp.dot(a, b, preferred_element_type=F32)


def _dot_nt(a, b):
    return lax.dot_general(a, b, _NT, preferred_element_type=F32)


def _dot_tn(a, b):
    return lax.dot_general(a, b, _TN, preferred_element_type=F32)


def _params(dims=None, vmem_mb=VMEM_LIMIT_MB):
    kw = dict(vmem_limit_bytes=vmem_mb << 20)
    if dims is not None:
        kw["dimension_semantics"] = dims
    return pltpu.CompilerParams(**kw)


def _row_tile(n, prefs=(384, 256, 128)):
    for t in prefs:
        if n % t == 0:
            return t
    raise ValueError(f"no row tile for {n}")


def _iota(shape, dim):
    return lax.broadcasted_iota(jnp.int32, shape, dim)


def _pick_row(tile, row):
    sub = _iota(tile.shape, 0)
    return jnp.sum(jnp.where(sub == row, tile, 0.0), axis=0, keepdims=True)


def _split3(x):
    hi = x.astype(BF16)
    r1 = x - hi.astype(F32)
    mid = r1.astype(BF16)
    lo = (r1 - mid.astype(F32)).astype(BF16)
    return hi, mid, lo


def _full(shape):
    nd = len(shape)
    return pl.BlockSpec(shape, lambda *_: (0,) * nd)


def _in_perm():
    cols = list(range(RET_W))
    for p in range(FOX_HEADS // 2):
        for part in range(3):
            start = RET_W + part * 512 + p * BLK
            cols += list(range(start, start + BLK))
    return np.asarray(cols, np.int32)


def _rotary_tables(L):
    half = HEAD_LANES // 2
    inv = 1.0 / (ROPE_BASE ** (jnp.arange(half, dtype=F32) / half))
    ang = jnp.arange(L).astype(F32)[:, None] * inv[None, :]
    cos, sin = jnp.cos(ang), jnp.sin(ang)
    cos_t = jnp.tile(cos, (1, 4))
    sin_t = jnp.tile(jnp.concatenate([-sin, sin], axis=1), (1, 2))
    return cos_t, sin_t


def _decay_tables():
    gam = 1.0 - 2.0 ** (-5.0 - np.arange(RET_HEADS, dtype=np.float64))
    n = np.arange(BLK)
    same_or_past = (n[:, None] // CHUNK) >= (n[None, :] // CHUNK)
    dist = np.abs(n[:, None] - n[None, :])
    dmat = np.stack([np.where(same_or_past, g ** dist, 0.0) for g in gam]).astype(np.float32)
    lane_head = np.arange(BLK) // HEAD_LANES
    wq = np.stack([gam[2 * p + lane_head][None, :] ** (n[:, None] + 1.0) for p in range(2)]).astype(np.float32)
    wk = np.stack([gam[2 * p + lane_head][None, :] ** (BLK - 1.0 - n[:, None]) for p in range(2)]).astype(np.float32)
    g_blk = tuple(float(g ** BLK) for g in gam)
    return jnp.asarray(dmat), jnp.asarray(wq), jnp.asarray(wk), g_blk


def _rms_inproj(h0, g, w_main, w_ff):
    L = h0.shape[0]
    tm = _row_tile(L)

    def body(h_ref, g_ref, wm_ref, wf_ref, n_ref, p_ref, ff_ref):
        h = h_ref[...]
        r = lax.rsqrt(jnp.mean(h * h, axis=-1, keepdims=True) + EPS)
        n = (h * r * g_ref[...]).astype(BF16)
        n_ref[...] = n
        p_ref[...] = _dot(n, wm_ref[...]).astype(BF16)
        ff_ref[...] = _dot(n, wf_ref[...])

    return pl.pallas_call(
        body, name="f_inproj", grid=(L // tm,),
        in_specs=[pl.BlockSpec((tm, D_MODEL), lambda i: (i, 0)), _full((1, D_MODEL)),
                  _full((D_MODEL, MAIN_W)), _full((D_MODEL, BLK))],
        out_specs=[pl.BlockSpec((tm, D_MODEL), lambda i: (i, 0)), pl.BlockSpec((tm, MAIN_W), lambda i: (i, 0)),
                   pl.BlockSpec((tm, BLK), lambda i: (i, 0))],
        out_shape=[jax.ShapeDtypeStruct((L, D_MODEL), BF16), jax.ShapeDtypeStruct((L, MAIN_W), BF16),
                   jax.ShapeDtypeStruct((L, BLK), F32)],
        compiler_params=_params(("parallel",)),
    )(h0, g, w_main, w_ff)


def _fox_prep(ff, fb):
    L = ff.shape[0]
    nblk = L // BLK

    def body(ff_ref, b_ref, c_ref, ct_ref, carry):
        i = pl.program_id(0)

        @pl.when(i == 0)
        def _():
            carry[...] = jnp.zeros_like(carry)

        z = ff_ref[...] + b_ref[...]
        lf = jnp.minimum(z, 0.0) - jnp.log1p(jnp.exp(-jnp.abs(z)))
        lf = jnp.where(_iota((BLK, BLK), 1) < FOX_HEADS, lf, 0.0)
        tri = (_iota((BLK, BLK), 0) >= _iota((BLK, BLK), 1)).astype(BF16)
        hi, mid, lo = _split3(lf)
        cs = _dot(tri, hi) + _dot(tri, mid) + _dot(tri, lo) + carry[...]
        c_ref[...] = cs
        ct_ref[0] = cs.T[0:8, :]
        carry[...] = carry[...] + jnp.sum(lf, axis=0, keepdims=True)

    return pl.pallas_call(
        body, name="f_foxprep", grid=(nblk,),
        in_specs=[pl.BlockSpec((BLK, BLK), lambda i: (i, 0)), _full((1, BLK))],
        out_specs=[pl.BlockSpec((BLK, BLK), lambda i: (i, 0)), pl.BlockSpec((1, 8, BLK), lambda i: (i, 0, 0))],
        out_shape=[jax.ShapeDtypeStruct((L, BLK), F32), jax.ShapeDtypeStruct((nblk, 8, BLK), F32)],
        scratch_shapes=[pltpu.VMEM((1, BLK), F32)],
        compiler_params=_params(("arbitrary",)),
    )(ff, fb)


def _rot_fns(cos, sin):
    lane = _iota((BLK, BLK), 1)
    first = (lane & (HEAD_LANES - 1)) < HEAD_LANES // 2

    def swap(x):
        return jnp.where(first, pltpu.roll(x, BLK - 32, 1), pltpu.roll(x, 32, 1))

    def rot(x):
        return x * cos + swap(x) * sin

    def rot_t(dy):
        return dy * cos + swap(dy * sin)

    return rot, rot_t


def _retention_fwd(proj, cos_t, sin_t, ret_g):
    L = proj.shape[0]
    nblk = L // BLK
    dmat, wq_t, wk_t, g_blk = _decay_tables()

    def body(q_ref, k_ref, v_ref, gate_ref, cos_ref, sin_ref, d_ref, wq_ref, wk_ref, rg_ref,
             mix_ref, o_ref, rs_ref, state):
        i = pl.program_id(0)

        @pl.when(i == 0)
        def _():
            state[...] = jnp.zeros_like(state)

        rot, _ = _rot_fns(cos_ref[...], sin_ref[...])
        lane = _iota((BLK, BLK), 1)
        sub = _iota((BLK, BLK), 0)
        for p in range(2):
            qr = rot(q_ref[:, p * BLK:(p + 1) * BLK].astype(F32))
            kr = rot(k_ref[:, p * BLK:(p + 1) * BLK].astype(F32)) * (HEAD_LANES ** -0.5)
            kr_b = kr.astype(BF16)
            qw = (qr * wq_ref[p]).astype(BF16)
            kw = (kr * wk_ref[p]).astype(BF16)
            for e in range(2):
                h = 2 * p + e
                cols = slice(h * BLK, (h + 1) * BLK)
                qm = jnp.where((lane >> 6) == e, qr, 0.0).astype(BF16)
                s = _dot_nt(qm, kr_b) * d_ref[h]
                vh = v_ref[:, cols]
                st = state[h]
                rs_ref[0, h] = st
                o = _dot(s.astype(BF16), vh) + _dot(qw, st.astype(BF16))
                u = jnp.where((sub >> 6) == e, _dot_tn(kw, vh), 0.0)
                state[h] = g_blk[h] * st + u
                rn = lax.rsqrt(jnp.mean(o * o, axis=-1, keepdims=True) + EPS)
                gate = gate_ref[:, cols].astype(F32)
                o_ref[:, cols] = o
                mix_ref[:, cols] = (o * rn * rg_ref[:, cols] * (gate * jax.nn.sigmoid(gate))).astype(BF16)

    row = lambda c: (lambda i: (i, c))
    return pl.pallas_call(
        body, name="f_retention", grid=(nblk,),
        in_specs=[pl.BlockSpec((BLK, 256), row(0)), pl.BlockSpec((BLK, 256), row(1)),
                  pl.BlockSpec((BLK, 512), row(1)), pl.BlockSpec((BLK, 512), row(2)),
                  pl.BlockSpec((BLK, BLK), row(0)), pl.BlockSpec((BLK, BLK), row(0)),
                  _full((RET_HEADS, BLK, BLK)), _full((2, BLK, BLK)), _full((2, BLK, BLK)), _full((1, 512))],
        out_specs=[pl.BlockSpec((BLK, 512), row(0)), pl.BlockSpec((BLK, 512), row(0)),
                   pl.BlockSpec((1, RET_HEADS, BLK, BLK), lambda i: (i, 0, 0, 0))],
        out_shape=[jax.ShapeDtypeStruct((L, 512), BF16), jax.ShapeDtypeStruct((L, 512), F32),
                   jax.ShapeDtypeStruct((nblk, RET_HEADS, BLK, BLK), F32)],
        scratch_shapes=[pltpu.VMEM((RET_HEADS, BLK, BLK), F32)],
        compiler_params=_params(("arbitrary",)),
    )(proj, proj, proj, proj, cos_t, sin_t, dmat, wq_t, wk_t, ret_g)


def _fox_units(L):
    nblk = L // BLK
    assert L % BLK == 0 and nblk % 2 == 1, "sequence must be one 128-row block plus whole 256-row tiles"
    return nblk, (nblk - 1) // 2


def _fox_tile_masks():
    sub, lane = _iota((BLK, BLK), 0), _iota((BLK, BLK), 1)
    return dict(first=(sub <= lane) & (sub >= N_PAD), valid=_iota((BLK, UNIT), 0) >= N_PAD,
                diag=_iota((UNIT, UNIT), 0) <= _iota((UNIT, UNIT), 1))


def _fox_fwd(proj, c, ctb):
    L = proj.shape[0]
    nblk, nu = _fox_units(L)
    scale = HEAD_LANES ** -0.5

    def body(qkv_ref, c_ref, ct_ref, of_ref, lse_ref, vt, csb):
        p = pl.program_id(0)

        @pl.when(p == 0)
        def _():
            lse_ref[...] = jnp.zeros_like(lse_ref)

        lane = _iota((BLK, BLK), 1)
        sub8 = _iota((8, BLK), 0)
        masks = _fox_tile_masks()

        def pre(j, carry):
            off = pl.multiple_of(j * BLK, BLK)
            vt[j] = qkv_ref[pl.ds(off, BLK), 2 * BLK:3 * BLK].astype(F32).T.astype(BF16)
            ct = c_ref[pl.ds(off, BLK), :]
            for e in range(2):
                col = jnp.sum(jnp.where(lane == 2 * p + e, ct, 0.0), axis=1, keepdims=True)
                csb[e, j] = jnp.broadcast_to(col, (BLK, UNIT))
            return carry

        lax.fori_loop(0, nblk, pre, 0)

        def attend(qblk, nq, n_whole):
            qlen = nq * BLK
            qoff = pl.multiple_of(qblk * BLK, BLK)
            qs = qkv_ref[pl.ds(qoff, qlen), 0:BLK].astype(F32) * scale
            qlane = _iota((qlen, BLK), 1)
            qm = [jnp.where((qlane >> 6) == e, qs, 0.0).astype(BF16) for e in range(2)]
            ct_row = [jnp.concatenate([_pick_row(ct_ref[qblk + a], 2 * p + e) for a in range(nq)], axis=1)
                      for e in range(2)]

            def step(kblk, nk, mask, st):
                koff = pl.multiple_of(kblk * BLK, BLK)
                kt = qkv_ref[pl.ds(koff, nk * BLK), BLK:2 * BLK]
                out = []
                for e in range(2):
                    m, l, acc = st[3 * e:3 * e + 3]
                    s = _dot_nt(kt, qm[e])
                    t = jnp.concatenate([s[b * BLK:(b + 1) * BLK] - csb[e, kblk + b, :, 0:qlen] for b in range(nk)], axis=0)
                    if mask is not None:
                        t = jnp.where(mask, t, NEG)
                    m_new = jnp.maximum(m, jnp.max(t, axis=0, keepdims=True) + ct_row[e])
                    alpha = jnp.exp(m - m_new)
                    pr = jnp.exp(t - (m_new - ct_row[e]))
                    l = alpha * l + jnp.sum(pr, axis=0, keepdims=True)
                    pr_b = pr.astype(BF16)
                    pv = _dot(vt[kblk, e * HEAD_LANES:(e + 1) * HEAD_LANES, :], pr_b[0:BLK])
                    for b in range(1, nk):
                        pv = pv + _dot(vt[kblk + b, e * HEAD_LANES:(e + 1) * HEAD_LANES, :], pr_b[b * BLK:(b + 1) * BLK])
                    out += [m_new, l, alpha * acc + pv]
                return tuple(out)

            st = (jnp.full((1, qlen), NEG, F32), jnp.zeros((1, qlen), F32), jnp.zeros((HEAD_LANES, qlen), F32)) * 2
            if nq == 1:
                st = step(0, 1, masks["first"], st)
            else:
                st = step(0, 1, masks["valid"], st)
                n_wide = n_whole // WIDE
                st = lax.fori_loop(0, n_wide, lambda j, s_: step(1 + 2 * WIDE * j, 2 * WIDE, None, s_), st)
                st = lax.fori_loop(0, n_whole & (WIDE - 1),
                                   lambda j, s_: step(1 + 2 * WIDE * n_wide + 2 * j, 2, None, s_), st)
                st = step(qblk, 2, masks["diag"], st)
            o_t = jnp.concatenate([st[2] * (1.0 / st[1]), st[5] * (1.0 / st[4])], axis=0)
            of_ref[pl.ds(qoff, qlen), :] = o_t.T.astype(BF16)
            lse = [st[3 * e] + jnp.log(st[3 * e + 1]) for e in range(2)]
            for a in range(nq):
                rows = [lse[e][:, a * BLK:(a + 1) * BLK] for e in range(2)]
                lse_ref[qblk + a] = lse_ref[qblk + a] + (
                    jnp.where(sub8 == 2 * p, rows[0], 0.0) + jnp.where(sub8 == 2 * p + 1, rows[1], 0.0))

        attend(0, 1, 0)

        def q_loop(u, carry):
            attend(1 + 2 * u, 2, u)
            return carry

        lax.fori_loop(0, nu, q_loop, 0)

    return pl.pallas_call(
        body, name="f_fox", grid=(FOX_HEADS // 2,),
        in_specs=[pl.BlockSpec((L, 384), lambda p: (0, RET_W // 384 + p)), _full((L, BLK)), _full((nblk, 8, BLK))],
        out_specs=[pl.BlockSpec((L, BLK), lambda p: (0, p)), _full((nblk, 8, BLK))],
        out_shape=[jax.ShapeDtypeStruct((L, 512), BF16), jax.ShapeDtypeStruct((nblk, 8, BLK), F32)],
        scratch_shapes=[pltpu.VMEM((nblk, BLK, BLK), BF16), pltpu.VMEM((2, nblk, BLK, UNIT), F32)],
        compiler_params=_params(("arbitrary",)),
    )(proj, c, ctb)


def _outproj_up(mix_r, o_f, h0, w_out, ffn_g, w_up):
    L = h0.shape[0]
    tm = _row_tile(L)
    shard = w_up.shape[2]

    def body(mr_ref, of_ref, h0_ref, wo_ref, g_ref, wu_ref, h1_ref, n2_ref, up_ref):
        h1 = h0_ref[...] + _dot(mr_ref[...], wo_ref[0:512, :]) + _dot(of_ref[...], wo_ref[512:1024, :])
        h1_ref[...] = h1
        r = lax.rsqrt(jnp.mean(h1 * h1, axis=-1, keepdims=True) + EPS)
        n2 = (h1 * r * g_ref[...]).astype(BF16)
        n2_ref[...] = n2
        for j in range(N_CHIPS):
            up_ref[:, j * shard:(j + 1) * shard] = _dot(n2, wu_ref[j]).astype(BF16)

    rows = lambda w: pl.BlockSpec((tm, w), lambda i: (i, 0))
    return pl.pallas_call(
        body, name="f_outproj_up", grid=(L // tm,),
        in_specs=[rows(512), rows(512), rows(D_MODEL), _full((D_MODEL, D_MODEL)), _full((1, D_MODEL)),
                  _full((N_CHIPS, D_MODEL, shard))],
        out_specs=[rows(D_MODEL), rows(D_MODEL), rows(2 * D_FF)],
        out_shape=[jax.ShapeDtypeStruct((L, D_MODEL), F32), jax.ShapeDtypeStruct((L, D_MODEL), BF16),
                   jax.ShapeDtypeStruct((L, 2 * D_FF), BF16)],
        compiler_params=_params(("parallel",)),
    )(mix_r, o_f, h0, w_out, ffn_g, w_up)


def _conv_acc(a_ref, halo_ref, cw_refs, cb_ref, i, tm):
    sub = _iota((tm, 1), 0)
    a = jnp.where(i * tm + sub >= N_PAD, a_ref[...].astype(F32), 0.0)
    halo = halo_ref[...].astype(F32)
    hrow = i * tm - 8 + _iota((8, 1), 0)
    halo = jnp.where((hrow >= N_PAD) & (i > 0), halo, 0.0)
    a1 = jnp.where(sub == 0, _pick_row(halo, 7), pltpu.roll(a, 1, 0))
    a2 = jnp.where(sub == 0, _pick_row(halo, 6), jnp.where(sub == 1, _pick_row(halo, 7), pltpu.roll(a, 2, 0)))
    acc = cb_ref[...] + a2 * cw_refs[0][...]
    acc = acc + a1 * cw_refs[1][...]
    acc = acc + a * cw_refs[2][...]
    return a, a1, a2, acc


def _ffn_down_loss(up, conv_w, conv_b, w_down, h1, final_g, target):
    L = h1.shape[0]
    tm = _row_tile(L)
    cw = [conv_w[j:j + 1] for j in range(3)]

    def body(a_ref, halo_ref, b_ref, cw0, cw1, cw2, cb_ref, wd_ref, h1_ref, gf_ref, t_ref,
             g_ref, dh_ref, dhb_ref, dgf_ref, loss_ref):
        i = pl.program_id(0)

        @pl.when(i == 0)
        def _():
            dgf_ref[...] = jnp.zeros_like(dgf_ref)
            loss_ref[...] = jnp.zeros_like(loss_ref)

        _, _, _, acc = _conv_acc(a_ref, halo_ref, (cw0, cw1, cw2), cb_ref, i, tm)
        g = (acc * jax.nn.sigmoid(acc) * b_ref[...].astype(F32)).astype(BF16)
        g_ref[...] = g
        h2 = h1_ref[...] + _dot(g, wd_ref[...])
        r = lax.rsqrt(jnp.mean(h2 * h2, axis=-1, keepdims=True) + EPS)
        yn = h2 * r
        gf = gf_ref[...]
        live = i * tm + _iota((tm, 1), 0) >= PREFIX
        err = jnp.where(live, yn * gf - t_ref[...], 0.0)
        loss_ref[...] = loss_ref[...] + 0.5 * jnp.sum(jnp.mean(err * err, axis=-1, keepdims=True))
        dy = err * (1.0 / D_MODEL)
        dgf_ref[...] = dgf_ref[...] + jnp.sum(dy * yn, axis=0, keepdims=True)
        dyn = dy * gf
        dh = r * (dyn - yn * jnp.mean(dyn * yn, axis=-1, keepdims=True))
        dh_ref[...] = dh
        dhb_ref[...] = dh.astype(BF16)

    rows = lambda w, c=0: pl.BlockSpec((tm, w), lambda i: (i, c))
    halo = pl.BlockSpec((8, D_FF), lambda i: (jnp.maximum(i * (tm // 8) - 1, 0), 0))
    return pl.pallas_call(
        body, name="f_ffn_down_loss", grid=(L // tm,),
        in_specs=[rows(D_FF), halo, rows(D_FF, 1), _full((1, D_FF)), _full((1, D_FF)), _full((1, D_FF)),
                  _full((1, D_FF)), _full((D_FF, D_MODEL)), rows(D_MODEL), _full((1, D_MODEL)), rows(D_MODEL)],
        out_specs=[rows(D_FF), rows(D_MODEL), rows(D_MODEL), _full((1, D_MODEL)), _full((1, BLK))],
        out_shape=[jax.ShapeDtypeStruct((L, D_FF), BF16), jax.ShapeDtypeStruct((L, D_MODEL), F32),
                   jax.ShapeDtypeStruct((L, D_MODEL), BF16), jax.ShapeDtypeStruct((1, D_MODEL), F32),
                   jax.ShapeDtypeStruct((1, BLK), F32)],
        compiler_params=_params(("arbitrary",)),
    )(up, up, up, cw[0], cw[1], cw[2], conv_b, w_down, h1, final_g, target)


def _ffn_bwd_gate(dh2b, w_down, up, conv_w, conv_b):
    L = dh2b.shape[0]
    tm = _row_tile(L)
    cw = [conv_w[j:j + 1] for j in range(3)]

    def body(dh_ref, wd_ref, a_ref, halo_ref, b_ref, cw0, cw1, cw2, cb_ref, dacc_ref, db_ref, dcw_ref):
        i = pl.program_id(0)

        @pl.when(i == 0)
        def _():
            dcw_ref[...] = jnp.zeros_like(dcw_ref)

        a, a1, a2, acc = _conv_acc(a_ref, halo_ref, (cw0, cw1, cw2), cb_ref, i, tm)
        dg = _dot_nt(dh_ref[...], wd_ref[...])
        sg = jax.nn.sigmoid(acc)
        db_ref[...] = (dg * acc * sg).astype(BF16)
        dacc = dg * b_ref[...].astype(F32) * (sg * (1.0 + acc * (1.0 - sg)))
        dacc_ref[...] = dacc.astype(BF16)
        sub8 = _iota((8, 1), 0)
        rows = [jnp.sum(dacc * t, axis=0, keepdims=True) for t in (a2, a1, a)] + [jnp.sum(dacc, axis=0, keepdims=True)]
        upd = jnp.zeros((8, D_FF), F32)
        for j, rj in enumerate(rows):
            upd = upd + jnp.where(sub8 == j, rj, 0.0)
        dcw_ref[...] = dcw_ref[...] + upd

    rows = lambda w, c=0: pl.BlockSpec((tm, w), lambda i: (i, c))
    halo = pl.BlockSpec((8, D_FF), lambda i: (jnp.maximum(i * (tm // 8) - 1, 0), 0))
    return pl.pallas_call(
        body, name="b_ffn_gate", grid=(L // tm,),
        in_specs=[rows(D_MODEL), _full((D_FF, D_MODEL)), rows(D_FF), halo, rows(D_FF, 1),
                  _full((1, D_FF)), _full((1, D_FF)), _full((1, D_FF)), _full((1, D_FF))],
        out_specs=[rows(D_FF), rows(D_FF), _full((8, D_FF))],
        out_shape=[jax.ShapeDtypeStruct((L, D_FF), BF16), jax.ShapeDtypeStruct((L, D_FF), BF16),
                   jax.ShapeDtypeStruct((8, D_FF), F32)],
        compiler_params=_params(("arbitrary",)),
    )(dh2b, w_down, up, up, up, cw[0], cw[1], cw[2], conv_b)


def _ffn_bwd_up(dacc, db, conv_w, w_up, h1, ffn_g, dh2, w_out):
    L = h1.shape[0]
    tm = _row_tile(L)
    nt = L // tm
    shard = w_up.shape[2]
    cw = [conv_w[j:j + 1] for j in range(3)]

    def body(da_ref, halo_ref, db_ref, cw0, cw1, cw2, wu_ref, h1_ref, g_ref, dh2_ref, wo_ref,
             dup_ref, dh1_ref, dh1b_ref, dmix_ref, dg_ref):
        i = pl.program_id(0)

        @pl.when(i == 0)
        def _():
            dg_ref[...] = jnp.zeros_like(dg_ref)

        sub = _iota((tm, 1), 0)
        d0 = da_ref[...].astype(F32)
        halo = jnp.where(i < nt - 1, halo_ref[...].astype(F32), 0.0)
        d1 = jnp.where(sub == tm - 1, _pick_row(halo, 0), pltpu.roll(d0, tm - 1, 0))
        d2 = jnp.where(sub == tm - 2, _pick_row(halo, 0),
                       jnp.where(sub == tm - 1, _pick_row(halo, 1), pltpu.roll(d0, tm - 2, 0)))
        da = d0 * cw2[...] + d1 * cw1[...] + d2 * cw0[...]
        da = jnp.where(i * tm + sub >= N_PAD, da, 0.0).astype(BF16)
        dup_ref[:, 0:D_FF] = da
        dbv = db_ref[...]
        dup_ref[:, D_FF:2 * D_FF] = dbv
        dn = jnp.zeros((tm, D_MODEL), F32)
        for j in range(N_CHIPS):
            src = da if j < 2 else dbv
            lo = (j % 2) * shard
            dn = dn + _dot_nt(src[:, lo:lo + shard], wu_ref[j])
        h1 = h1_ref[...]
        r = lax.rsqrt(jnp.mean(h1 * h1, axis=-1, keepdims=True) + EPS)
        yn = h1 * r
        dg_ref[...] = dg_ref[...] + jnp.sum(dn * yn, axis=0, keepdims=True)
        dyn = dn * g_ref[...]
        dh1 = dh2_ref[...] + r * (dyn - yn * jnp.mean(dyn * yn, axis=-1, keepdims=True))
        dh1_ref[...] = dh1
        dh1b = dh1.astype(BF16)
        dh1b_ref[...] = dh1b
        dmix_ref[...] = _dot_nt(dh1b, wo_ref[...]).astype(BF16)

    rows = lambda w: pl.BlockSpec((tm, w), lambda i: (i, 0))
    halo = pl.BlockSpec((8, D_FF), lambda i: (jnp.minimum((i + 1) * (tm // 8), L // 8 - 1), 0))
    return pl.pallas_call(
        body, name="b_ffn_up", grid=(nt,),
        in_specs=[rows(D_FF), halo, rows(D_FF), _full((1, D_FF)), _full((1, D_FF)), _full((1, D_FF)),
                  _full((N_CHIPS, D_MODEL, shard)), rows(D_MODEL), _full((1, D_MODEL)), rows(D_MODEL),
                  _full((D_MODEL, D_MODEL))],
        out_specs=[rows(2 * D_FF), rows(D_MODEL), rows(D_MODEL), rows(D_MODEL), _full((1, D_MODEL))],
        out_shape=[jax.ShapeDtypeStruct((L, 2 * D_FF), BF16), jax.ShapeDtypeStruct((L, D_MODEL), F32),
                   jax.ShapeDtypeStruct((L, D_MODEL), BF16), jax.ShapeDtypeStruct((L, D_MODEL), BF16),
                   jax.ShapeDtypeStruct((1, D_MODEL), F32)],
        compiler_params=_params(("arbitrary",)),
    )(dacc, dacc, db, cw[0], cw[1], cw[2], w_up, h1, ffn_g, dh2, w_out)


def _wgrad(a, b, name, tn=None):
    L, K = a.shape
    N = b.shape[1]
    tn = N if tn is None else tn
    tl = _row_tile(L)

    def body(a_ref, b_ref, o_ref):
        @pl.when(pl.program_id(1) == 0)
        def _():
            o_ref[...] = jnp.zeros_like(o_ref)

        o_ref[0] = o_ref[0] + _dot_tn(a_ref[...], b_ref[...])

    return pl.pallas_call(
        body, name=name, grid=(N // tn, L // tl),
        in_specs=[pl.BlockSpec((tl, K), lambda n, l: (l, 0)), pl.BlockSpec((tl, tn), lambda n, l: (l, n))],
        out_specs=pl.BlockSpec((1, K, tn), lambda n, l: (n, 0, 0)),
        out_shape=jax.ShapeDtypeStruct((N // tn, K, tn), F32),
        compiler_params=_params(("parallel", "arbitrary")),
    )(a, b)


def _retention_bwd(dmix, o, proj, cos_t, sin_t, ret_g, states):
    L = proj.shape[0]
    nblk = L // BLK
    dmat, wq_t, wk_t, g_blk = _decay_tables()

    def body(dm_ref, o_ref, q_ref, k_ref, v_ref, gate_ref, cos_ref, sin_ref, d_ref, wq_ref, wk_ref, rg_ref, rs_ref,
             dp_ref, drg_ref, gstate):
        i = pl.program_id(0)

        @pl.when(i == 0)
        def _():
            gstate[...] = jnp.zeros_like(gstate)
            drg_ref[...] = jnp.zeros_like(drg_ref)

        rot, rot_t = _rot_fns(cos_ref[...], sin_ref[...])
        lane = _iota((BLK, BLK), 1)
        sub = _iota((BLK, BLK), 0)
        scale = HEAD_LANES ** -0.5
        for p in range(2):
            qr = rot(q_ref[:, p * BLK:(p + 1) * BLK].astype(F32))
            kr = rot(k_ref[:, p * BLK:(p + 1) * BLK].astype(F32)) * scale
            kr_b = kr.astype(BF16)
            qw = (qr * wq_ref[p]).astype(BF16)
            kw = (kr * wk_ref[p]).astype(BF16)
            dqr = jnp.zeros((BLK, BLK), F32)
            dkr = jnp.zeros((BLK, BLK), F32)
            for e in range(2):
                h = 2 * p + e
                cols = slice(h * BLK, (h + 1) * BLK)
                head_lanes = (lane >> 6) == e
                o = o_ref[:, cols]
                rn = lax.rsqrt(jnp.mean(o * o, axis=-1, keepdims=True) + EPS)
                y = o * rn
                gate = gate_ref[:, cols].astype(F32)
                sg = jax.nn.sigmoid(gate)
                dm = dm_ref[:, cols].astype(F32)
                rgain = rg_ref[:, cols]
                drg_ref[:, cols] = drg_ref[:, cols] + jnp.sum(dm * y * (gate * sg), axis=0, keepdims=True)
                dp_ref[:, 1024 + h * BLK:1024 + (h + 1) * BLK] = (
                    dm * y * rgain * (sg * (1.0 + gate * (1.0 - sg)))).astype(BF16)
                dy = dm * rgain * (gate * sg)
                do = (rn * (dy - y * jnp.mean(dy * y, axis=-1, keepdims=True))).astype(BF16)
                vh = v_ref[:, cols]
                qm = jnp.where(head_lanes, qr, 0.0).astype(BF16)
                dmh = d_ref[h]
                s = (_dot_nt(qm, kr_b) * dmh).astype(BF16)
                ds = (_dot_nt(do, vh) * dmh).astype(BF16)
                st = rs_ref[0, h].astype(BF16)
                gs = gstate[h]
                gs_b = gs.astype(BF16)
                dqr = dqr + jnp.where(head_lanes, _dot(ds, kr_b), 0.0) + _dot_nt(do, st) * wq_ref[p]
                dkr = dkr + _dot_tn(ds, qm) + _dot_nt(vh, gs_b) * wk_ref[p]
                dp_ref[:, 512 + h * BLK:512 + (h + 1) * BLK] = (_dot_tn(s, do) + _dot(kw, gs_b)).astype(BF16)
                dr = jnp.where((sub >> 6) == e, _dot_tn(qw, do), 0.0)
                gstate[h] = dr + g_blk[h] * gs
            dp_ref[:, p * BLK:(p + 1) * BLK] = rot_t(dqr).astype(BF16)
            dp_ref[:, 256 + p * BLK:256 + (p + 1) * BLK] = (rot_t(dkr) * scale).astype(BF16)

    row = lambda c: (lambda i: (nblk - 1 - i, c))
    return pl.pallas_call(
        body, name="b_retention", grid=(nblk,),
        in_specs=[pl.BlockSpec((BLK, 512), row(0)), pl.BlockSpec((BLK, 512), row(0)),
                  pl.BlockSpec((BLK, 256), row(0)), pl.BlockSpec((BLK, 256), row(1)),
                  pl.BlockSpec((BLK, 512), row(1)), pl.BlockSpec((BLK, 512), row(2)),
                  pl.BlockSpec((BLK, BLK), row(0)), pl.BlockSpec((BLK, BLK), row(0)),
                  _full((RET_HEADS, BLK, BLK)), _full((2, BLK, BLK)), _full((2, BLK, BLK)), _full((1, 512)),
                  pl.BlockSpec((1, RET_HEADS, BLK, BLK), lambda i: (nblk - 1 - i, 0, 0, 0))],
        out_specs=[pl.BlockSpec((BLK, RET_W), row(0)), _full((1, 512))],
        out_shape=[jax.ShapeDtypeStruct((L, RET_W), BF16), jax.ShapeDtypeStruct((1, 512), F32)],
        scratch_shapes=[pltpu.VMEM((RET_HEADS, BLK, BLK), F32)],
        compiler_params=_params(("arbitrary",)),
    )(dmix, o, proj, proj, proj, proj, cos_t, sin_t, dmat, wq_t, wk_t, ret_g, states)


def _fox_delta(dmix, o_f):
    L = o_f.shape[0]
    nblk = L // BLK

    def body(do_ref, o_ref, d_ref):
        prod = do_ref[...].astype(F32) * o_ref[...].astype(F32)
        sel = ((_iota((8, 512), 1) >> 6) == _iota((8, 512), 0)).astype(BF16)
        hi = prod.astype(BF16)
        lo = (prod - hi.astype(F32)).astype(BF16)
        d_ref[0] = _dot_nt(sel, hi) + _dot_nt(sel, lo)

    return pl.pallas_call(
        body, name="b_foxdelta", grid=(nblk,),
        in_specs=[pl.BlockSpec((BLK, 512), lambda i: (i, 1)), pl.BlockSpec((BLK, 512), lambda i: (i, 0))],
        out_specs=pl.BlockSpec((1, 8, BLK), lambda i: (i, 0, 0)),
        out_shape=jax.ShapeDtypeStruct((nblk, 8, BLK), F32),
        compiler_params=_params(("parallel",)),
    )(dmix, o_f)


def _fox_bwd(proj, dmix, c, ctb, lse, delta):
    L = proj.shape[0]
    nblk, nu = _fox_units(L)
    scale = HEAD_LANES ** -0.5

    def body(qkv_ref, do_ref, c_ref, ct_ref, lse_ref, dl_ref, dp_ref, dc_ref, dcq_ref,
             ktt, dqt, dk_acc, dv_acc, dcs_acc):
        p = pl.program_id(0)

        @pl.when(p == 0)
        def _():
            dc_ref[...] = jnp.zeros_like(dc_ref)
            dcq_ref[...] = jnp.zeros_like(dcq_ref)

        lane = _iota((BLK, BLK), 1)
        sub8 = _iota((8, BLK), 0)
        masks = _fox_tile_masks()

        def pre(j, carry):
            off = pl.multiple_of(j * BLK, BLK)
            ktt[j] = qkv_ref[pl.ds(off, BLK), BLK:2 * BLK].astype(F32).T.astype(BF16)
            dqt[j] = jnp.zeros((BLK, BLK), F32)
            return carry

        lax.fori_loop(0, nblk, pre, 0)

        def kv_pass(kblk, nk, n_later):
            klen = nk * BLK
            koff = pl.multiple_of(kblk * BLK, BLK)
            kt = qkv_ref[pl.ds(koff, klen), BLK:2 * BLK]
            vtile = qkv_ref[pl.ds(koff, klen), 2 * BLK:3 * BLK]
            ct = c_ref[pl.ds(koff, klen), :]
            klane = _iota((klen, BLK), 1)
            cs = [jnp.broadcast_to(jnp.sum(jnp.where(klane == 2 * p + e, ct, 0.0), axis=1, keepdims=True),
                                   (klen, WIDE * UNIT)) for e in range(2)]
            dk_acc[0:klen] = jnp.zeros((klen, BLK), F32)
            dv_acc[0:klen] = jnp.zeros((klen, BLK), F32)
            for e in range(2):
                dcs_acc[e, 0:klen] = jnp.zeros((klen, BLK), F32)

            def tile(qblk, nq, mask):
                qlen = nq * BLK
                if mask == "valid":
                    mask = _iota((klen, qlen), 0) >= N_PAD
                qoff = pl.multiple_of(qblk * BLK, BLK)
                qs = qkv_ref[pl.ds(qoff, qlen), 0:BLK].astype(F32) * scale
                dot_ = do_ref[pl.ds(qoff, qlen), :]
                qlane = _iota((qlen, BLK), 1)
                stats = [[ref[qblk + a] for a in range(nq)] for ref in (ct_ref, lse_ref, dl_ref)]
                for e in range(2):
                    h = 2 * p + e
                    head = (qlane >> 6) == e
                    ct_row, lse_row, dl_row = [jnp.concatenate([_pick_row(t, h) for t in ts], axis=1) for ts in stats]
                    qm = jnp.where(head, qs, 0.0).astype(BF16)
                    dom = jnp.where(head, dot_, jnp.zeros_like(dot_))
                    t = _dot_nt(kt, qm) - cs[e][:, 0:qlen]
                    if mask is not None:
                        t = jnp.where(mask, t, NEG)
                    pr = jnp.exp(t + (ct_row - lse_row))
                    dv_acc[0:klen] = dv_acc[0:klen] + _dot(pr.astype(BF16), dom)
                    dsv = pr * (_dot_nt(vtile, dom) - dl_row)
                    ds_b = dsv.astype(BF16)
                    dk_acc[0:klen] = dk_acc[0:klen] + _dot(ds_b, qm)
                    rows = slice(e * HEAD_LANES, (e + 1) * HEAD_LANES)
                    dq_t = _dot(ktt[kblk, rows, :], ds_b[0:BLK])
                    for b in range(1, nk):
                        dq_t = dq_t + _dot(ktt[kblk + b, rows, :], ds_b[b * BLK:(b + 1) * BLK])
                    key_side = dsv[:, 0:BLK]
                    for a in range(1, nq):
                        key_side = key_side + dsv[:, a * BLK:(a + 1) * BLK]
                    dcs_acc[e, 0:klen] = dcs_acc[e, 0:klen] + key_side
                    query_side = jnp.sum(dsv, axis=0, keepdims=True)
                    for a in range(nq):
                        cols = slice(a * BLK, (a + 1) * BLK)
                        dqt[qblk + a, rows, :] = dqt[qblk + a, rows, :] + dq_t[:, cols]
                        dcq_ref[qblk + a] = dcq_ref[qblk + a] + jnp.where(sub8 == h, query_side[:, cols], 0.0)

            later_mask = "valid" if nk == 1 else None
            n_later = jnp.asarray(n_later, jnp.int32)
            n_wide = n_later // WIDE

            def later_wide(i, carry):
                tile(kblk + nk + 2 * WIDE * i, 2 * WIDE, later_mask)
                return carry

            def later(i, carry):
                tile(kblk + nk + 2 * WIDE * n_wide + 2 * i, 2, later_mask)
                return carry

            tile(kblk, nk, masks["first"] if nk == 1 else masks["diag"])
            lax.fori_loop(0, n_wide, later_wide, 0)
            lax.fori_loop(0, n_later & (WIDE - 1), later, 0)
            dp_ref[pl.ds(koff, klen), BLK:2 * BLK] = dk_acc[0:klen].astype(BF16)
            dp_ref[pl.ds(koff, klen), 2 * BLK:3 * BLK] = dv_acc[0:klen].astype(BF16)
            upd = jnp.zeros((klen, BLK), F32)
            for e in range(2):
                upd = upd + jnp.where(klane == 2 * p + e, -jnp.sum(dcs_acc[e, 0:klen], axis=1, keepdims=True), 0.0)
            dc_ref[pl.ds(koff, klen), :] = dc_ref[pl.ds(koff, klen), :] + upd

        kv_pass(0, 1, nu)

        def k_loop(u, carry):
            kv_pass(1 + 2 * u, 2, nu - 1 - u)
            return carry

        lax.fori_loop(0, nu, k_loop, 0)

        def flush(j, carry):
            off = pl.multiple_of(j * BLK, BLK)
            dp_ref[pl.ds(off, BLK), 0:BLK] = (dqt[j].T * scale).astype(BF16)
            return carry

        lax.fori_loop(0, nblk, flush, 0)

    stat = _full((nblk, 8, BLK))
    return pl.pallas_call(
        body, name="b_fox", grid=(FOX_HEADS // 2,),
        in_specs=[pl.BlockSpec((L, 384), lambda p: (0, RET_W // 384 + p)), pl.BlockSpec((L, BLK), lambda p: (0, 4 + p)),
                  _full((L, BLK)), stat, stat, stat],
        out_specs=[pl.BlockSpec((L, 384), lambda p: (0, p)), _full((L, BLK)), stat],
        out_shape=[jax.ShapeDtypeStruct((L, FOX_W), BF16), jax.ShapeDtypeStruct((L, BLK), F32),
                   jax.ShapeDtypeStruct((nblk, 8, BLK), F32)],
        scratch_shapes=[pltpu.VMEM((nblk, BLK, BLK), BF16), pltpu.VMEM((nblk, BLK, BLK), F32),
                        pltpu.VMEM((UNIT, BLK), F32), pltpu.VMEM((UNIT, BLK), F32), pltpu.VMEM((2, UNIT, BLK), F32)],
        compiler_params=_params(("arbitrary",)),
    )(proj, dmix, c, ctb, lse, delta)


def _fox_post(dc, dcq, ff, fb):
    L = dc.shape[0]
    nblk = L // BLK

    def body(dc_ref, dcq_ref, ff_ref, b_ref, dff_ref, dffb_ref, dfb_ref, carry):
        i = pl.program_id(0)

        @pl.when(i == 0)
        def _():
            carry[...] = jnp.zeros_like(carry)
            dfb_ref[...] = jnp.zeros_like(dfb_ref)

        d = dc_ref[...] + jnp.concatenate([dcq_ref[0], jnp.zeros((BLK - 8, BLK), F32)], axis=0).T
        tri = (_iota((BLK, BLK), 0) <= _iota((BLK, BLK), 1)).astype(BF16)
        hi, mid, lo = _split3(d)
        dlf = _dot(tri, hi) + _dot(tri, mid) + _dot(tri, lo) + carry[...]
        carry[...] = carry[...] + jnp.sum(d, axis=0, keepdims=True)
        z = ff_ref[...] + b_ref[...]
        dff = jnp.where(_iota((BLK, BLK), 1) < FOX_HEADS, dlf * jax.nn.sigmoid(-z), 0.0)
        dff_ref[...] = dff
        dffb_ref[...] = dff.astype(BF16)
        dfb_ref[...] = dfb_ref[...] + jnp.sum(dff, axis=0, keepdims=True)

    rev = lambda i: (nblk - 1 - i, 0)
    return pl.pallas_call(
        body, name="b_foxpost", grid=(nblk,),
        in_specs=[pl.BlockSpec((BLK, BLK), rev), pl.BlockSpec((1, 8, BLK), lambda i: (nblk - 1 - i, 0, 0)),
                  pl.BlockSpec((BLK, BLK), rev), _full((1, BLK))],
        out_specs=[pl.BlockSpec((BLK, BLK), rev), pl.BlockSpec((BLK, BLK), rev), _full((1, BLK))],
        out_shape=[jax.ShapeDtypeStruct((L, BLK), F32), jax.ShapeDtypeStruct((L, BLK), BF16),
                   jax.ShapeDtypeStruct((1, BLK), F32)],
        scratch_shapes=[pltpu.VMEM((1, BLK), F32)],
        compiler_params=_params(("arbitrary",)),
    )(dc, dcq, ff, fb)


def _inproj_bwd(dpr, dpf, dffb, w_main, w_ff, h0, g, dh1):
    L = h0.shape[0]
    tm = _row_tile(L)

    def body(dpr_ref, dpf_ref, dff_ref, wm_ref, wf_ref, h_ref, g_ref, dh1_ref, dh0_ref, dg_ref):
        @pl.when(pl.program_id(0) == 0)
        def _():
            dg_ref[...] = jnp.zeros_like(dg_ref)

        dn = (_dot_nt(dpr_ref[...], wm_ref[:, 0:RET_W]) + _dot_nt(dpf_ref[...], wm_ref[:, RET_W:MAIN_W])
              + _dot_nt(dff_ref[...], wf_ref[...]))
        h = h_ref[...]
        r = lax.rsqrt(jnp.mean(h * h, axis=-1, keepdims=True) + EPS)
        yn = h * r
        dg_ref[...] = dg_ref[...] + jnp.sum(dn * yn, axis=0, keepdims=True)
        dyn = dn * g_ref[...]
        dh0_ref[...] = dh1_ref[...] + r * (dyn - yn * jnp.mean(dyn * yn, axis=-1, keepdims=True))

    rows = lambda w: pl.BlockSpec((tm, w), lambda i: (i, 0))
    return pl.pallas_call(
        body, name="b_inproj", grid=(L // tm,),
        in_specs=[rows(RET_W), rows(FOX_W), rows(BLK), _full((D_MODEL, MAIN_W)), _full((D_MODEL, BLK)),
                  rows(D_MODEL), _full((1, D_MODEL)), rows(D_MODEL)],
        out_specs=[rows(D_MODEL), _full((1, D_MODEL))],
        out_shape=[jax.ShapeDtypeStruct((L, D_MODEL), F32), jax.ShapeDtypeStruct((1, D_MODEL), F32)],
        compiler_params=_params(("arbitrary",)),
    )(dpr, dpf, dffb, w_main, w_ff, h0, g, dh1)


def _local_step(x, target, meta, attn_g, w_main, w_ff, fox_b, ret_g, w_out, ffn_g, w_up, conv_w, conv_b, w_down, final_g):
    S = x.shape[0]
    L = S + PREFIX
    h0 = jnp.concatenate([jnp.zeros((N_PAD, D_MODEL), F32), meta, x], axis=0)
    tgt = jnp.concatenate([jnp.zeros((PREFIX, D_MODEL), F32), target], axis=0)
    fb = jnp.pad(fox_b, ((0, 0), (0, BLK - FOX_HEADS)))
    cos_t, sin_t = _rotary_tables(L)

    n1, proj, ff = _rms_inproj(h0, attn_g, w_main, w_ff)
    c, ctb = _fox_prep(ff, fb)
    mix_r, o_ret, states = _retention_fwd(proj, cos_t, sin_t, ret_g)
    o_f, lse = _fox_fwd(proj, c, ctb)
    h1, n2, up = _outproj_up(mix_r, o_f, h0, w_out, ffn_g, w_up)
    g_act, dh2, dh2b, d_final_g, loss = _ffn_down_loss(up, conv_w, conv_b, w_down, h1, final_g, tgt)

    dacc, db, dconv = _ffn_bwd_gate(dh2b, w_down, up, conv_w, conv_b)
    dup, dh1, dh1b, dmix, d_ffn_g = _ffn_bwd_up(dacc, db, conv_w, w_up, h1, ffn_g, dh2, w_out)
    d_w_down = _wgrad(g_act, dh2b, "wgrad_down", tn=None)[0]
    d_w_up = _wgrad(n2, dup, "wgrad_up", tn=w_up.shape[2])
    d_w_out = jnp.concatenate([_wgrad(mix_r, dh1b, "wgrad_out_r")[0], _wgrad(o_f, dh1b, "wgrad_out_f")[0]], axis=0)

    dpr, d_ret_g = _retention_bwd(dmix, o_ret, proj, cos_t, sin_t, ret_g, states)
    delta = _fox_delta(dmix, o_f)
    dpf, dc, dcq = _fox_bwd(proj, dmix, c, ctb, lse, delta)
    dff, dffb, d_fox_b = _fox_post(dc, dcq, ff, fb)
    dh0, d_attn_g = _inproj_bwd(dpr, dpf, dffb, w_main, w_ff, h0, attn_g, dh1)
    d_w_main = jnp.concatenate([_wgrad(n1, dpr, "wgrad_in_r")[0], _wgrad(n1, dpf, "wgrad_in_f")[0]], axis=1)
    d_w_ff = _wgrad(n1, dffb, "wgrad_in_ff")[0]

    return dict(
        loss=loss[0, 0], dx=dh0[PREFIX:], dmeta=dh0[N_PAD:PREFIX], attn_g=d_attn_g, w_main=d_w_main,
        w_ff=d_w_ff[:, :FOX_HEADS], fox_b=d_fox_b[:, :FOX_HEADS], ret_g=d_ret_g, w_out=d_w_out, ffn_g=d_ffn_g,
        w_up=d_w_up, conv_w=dconv[0:3], conv_b=dconv[3:4], w_down=d_w_down, final_g=d_final_g)


_ANY = pl.BlockSpec(memory_space=pl.ANY)


def _place():
    return lax.axis_index("x"), lax.axis_index("y"), lax.axis_index("c")


def _other_chips(x, y):
    return [(1 - x, y), (x, 1 - y), (1 - x, 1 - y)]


def _chip_allgather(arrays):
    n = len(arrays)

    def body(*refs):
        ins, outs = refs[:n], refs[n:2 * n]
        send, recv, loc = refs[2 * n:]
        x, y, c = _place()
        mine = 2 * x + y
        peers = _other_chips(x, y)

        def remote(a, k, slot):
            return pltpu.make_async_remote_copy(
                src_ref=ins[a], dst_ref=outs[a].at[slot], send_sem=send.at[3 * a + k], recv_sem=recv.at[3 * a + k],
                device_id=(peers[k][0], peers[k][1], c), device_id_type=MESH)

        local = [pltpu.make_async_copy(ins[a], outs[a].at[mine], loc.at[a]) for a in range(n)]
        sends = [remote(a, k, mine) for a in range(n) for k in range(3)]
        for cp in local + sends:
            cp.start()
        for a in range(n):
            for k in range(3):
                remote(a, k, 2 * peers[k][0] + peers[k][1]).wait_recv()
        for cp in sends:
            cp.wait_send()
        for cp in local:
            cp.wait()

    return pl.pallas_call(
        body, name="ag_weights", in_specs=[_ANY] * n, out_specs=[_ANY] * n,
        out_shape=[jax.ShapeDtypeStruct((N_CHIPS,) + a.shape, a.dtype) for a in arrays],
        scratch_shapes=[pltpu.SemaphoreType.DMA((3 * n,)), pltpu.SemaphoreType.DMA((3 * n,)),
                        pltpu.SemaphoreType.DMA((n,))],
    )(*arrays)


def _sibling_exchange(grads, small):
    n = len(grads)

    def body(*refs):
        ins, small_in = refs[:n], refs[n]
        outs, small_out = refs[n + 1:2 * n + 1], refs[2 * n + 1]
        send, recv, s_send, s_recv, loc = refs[2 * n + 2:]
        x, y, c = _place()
        me = 4 * x + 2 * y + c

        def half_copy(a, which):
            half = ins[a].shape[1] // 2
            return pltpu.make_async_remote_copy(
                src_ref=ins[a].at[pl.ds(0, N_CHIPS), pl.ds(which * half, half)], dst_ref=outs[a],
                send_sem=send.at[a], recv_sem=recv.at[a], device_id=(x, y, 1 - c), device_id_type=MESH)

        def peer_of(r):
            return tuple(1 - v if (r >> b) & 1 else v for v, b in ((x, 2), (y, 1), (c, 0)))

        def small_copy(r, slot):
            return pltpu.make_async_remote_copy(
                src_ref=small_in, dst_ref=small_out.at[slot], send_sem=s_send.at[r - 1], recv_sem=s_recv.at[r - 1],
                device_id=peer_of(r), device_id_type=MESH)

        local = pltpu.make_async_copy(small_in, small_out.at[me], loc.at[0])
        sends = [half_copy(a, 1 - c) for a in range(n)] + [small_copy(r, me) for r in range(1, N_DEV)]
        local.start()
        for cp in sends:
            cp.start()
        for r in range(1, N_DEV):
            px, py, pc = peer_of(r)
            small_copy(r, 4 * px + 2 * py + pc).wait_recv()
        for a in range(n):
            half_copy(a, c).wait_recv()
        for cp in sends:
            cp.wait_send()
        local.wait()

    rows = small.shape[0]
    return pl.pallas_call(
        body, name="rs_sibling", in_specs=[_ANY] * (n + 1), out_specs=[_ANY] * (n + 1),
        out_shape=[jax.ShapeDtypeStruct((N_CHIPS, g.shape[1] // 2, g.shape[2]), g.dtype) for g in grads]
        + [jax.ShapeDtypeStruct((N_DEV, rows, small.shape[1]), small.dtype)],
        scratch_shapes=[pltpu.SemaphoreType.DMA((n,)), pltpu.SemaphoreType.DMA((n,)),
                        pltpu.SemaphoreType.DMA((N_DEV - 1,)), pltpu.SemaphoreType.DMA((N_DEV - 1,)),
                        pltpu.SemaphoreType.DMA((1,))],
    )(*grads, small)


def _chip_reduce_scatter(parts):
    n = len(parts)

    def body(*refs):
        ins, outs = refs[:n], refs[n:2 * n]
        send, recv = refs[2 * n:]
        x, y, c = _place()
        peers = _other_chips(x, y)

        def remote(a, k):
            return pltpu.make_async_remote_copy(
                src_ref=ins[a].at[2 * peers[k][0] + peers[k][1]], dst_ref=outs[a].at[k], send_sem=send.at[3 * a + k],
                recv_sem=recv.at[3 * a + k], device_id=(peers[k][0], peers[k][1], c), device_id_type=MESH)

        copies = [remote(a, k) for a in range(n) for k in range(3)]
        for cp in copies:
            cp.start()
        for cp in copies:
            cp.wait_recv()
        for cp in copies:
            cp.wait_send()

    return pl.pallas_call(
        body, name="rs_chips", in_specs=[_ANY] * n, out_specs=[_ANY] * n,
        out_shape=[jax.ShapeDtypeStruct((3,) + p.shape[1:], p.dtype) for p in parts],
        scratch_shapes=[pltpu.SemaphoreType.DMA((3 * n,)), pltpu.SemaphoreType.DMA((3 * n,))],
    )(*parts)


def _sibling_allgather(bufs):
    n = len(bufs)

    def body(*refs):
        outs = refs[n:2 * n]
        send, recv = refs[2 * n:]
        x, y, c = _place()

        def remote(a, which):
            return pltpu.make_async_remote_copy(
                src_ref=outs[a].at[which], dst_ref=outs[a].at[which], send_sem=send.at[a], recv_sem=recv.at[a],
                device_id=(x, y, 1 - c), device_id_type=MESH)

        sends = [remote(a, c) for a in range(n)]
        for cp in sends:
            cp.start()
        for a in range(n):
            remote(a, 1 - c).wait_recv()
        for cp in sends:
            cp.wait_send()

    outs = pl.pallas_call(
        body, name="ag_sibling", in_specs=[_ANY] * n, out_specs=[_ANY] * n,
        out_shape=[jax.ShapeDtypeStruct(b.shape, b.dtype) for b in bufs],
        input_output_aliases={a: a for a in range(n)},
        scratch_shapes=[pltpu.SemaphoreType.DMA((n,)), pltpu.SemaphoreType.DMA((n,))],
    )(*bufs)
    return [o.reshape(2 * o.shape[1], o.shape[2]) for o in outs]


def _pair_add(full, recv, core, name):
    _, R, C = full.shape
    half = R // 2

    def body(core_ref, a_ref, b_ref, o_ref):
        o_ref[...] = (a_ref[...] + b_ref[...]).astype(BF16)

    return pl.pallas_call(
        body, name=name,
        grid_spec=pltpu.PrefetchScalarGridSpec(
            num_scalar_prefetch=1, grid=(N_CHIPS,),
            in_specs=[pl.BlockSpec((1, half, C), lambda j, core_ref: (j, core_ref[0], 0)),
                      pl.BlockSpec((1, half, C), lambda j, core_ref: (j, 0, 0))],
            out_specs=pl.BlockSpec((1, half, C), lambda j, core_ref: (j, 0, 0))),
        out_shape=jax.ShapeDtypeStruct((N_CHIPS, half, C), BF16),
        compiler_params=_params(("parallel",)),
    )(core, full, recv)


def _sum_slots(q, name, tiles=2):
    n, R, C = q.shape
    tr = R // tiles

    def body(q_ref, o_ref):
        acc = q_ref[0].astype(F32)
        for j in range(1, n):
            acc = acc + q_ref[j].astype(F32)
        o_ref[...] = acc

    return pl.pallas_call(
        body, name=name, grid=(tiles,),
        in_specs=[pl.BlockSpec((n, tr, C), lambda i: (0, i, 0))],
        out_specs=pl.BlockSpec((tr, C), lambda i: (i, 0)),
        out_shape=jax.ShapeDtypeStruct((R, C), F32),
        compiler_params=_params(("parallel",)),
    )(q)


def _sum_partials(own_all, recv, place, name, tiles=2):
    _, R, C = own_all.shape
    tr = R // tiles

    def body(place_ref, own_ref, r_ref, o_ref):
        acc = own_ref[0].astype(F32)
        for k in range(3):
            acc = acc + r_ref[k].astype(F32)
        o_ref[0] = acc

    return pl.pallas_call(
        body, name=name,
        grid_spec=pltpu.PrefetchScalarGridSpec(
            num_scalar_prefetch=1, grid=(tiles,),
            in_specs=[pl.BlockSpec((1, tr, C), lambda i, place_ref: (place_ref[0], i, 0)),
                      pl.BlockSpec((3, tr, C), lambda i, place_ref: (0, i, 0))],
            out_specs=pl.BlockSpec((1, tr, C), lambda i, place_ref: (place_ref[1], i, 0))),
        out_shape=jax.ShapeDtypeStruct((2, R, C), F32),
        compiler_params=_params(("parallel",)),
    )(place, own_all, recv)


def _adamw(w, g, m, v, name, tiles=4):
    R, C = w.shape
    tr = R // tiles

    def body(w_ref, g_ref, m_ref, v_ref, d_ref, m2_ref, v2_ref):
        g_ = g_ref[...]
        m2 = ADAM_B1 * m_ref[...] + (1.0 - ADAM_B1) * g_
        v2 = ADAM_B2 * v_ref[...] + (1.0 - ADAM_B2) * (g_ * g_)
        m_hat = m2 / (1.0 - ADAM_B1 ** ADAM_STEP)
        v_hat = v2 / (1.0 - ADAM_B2 ** ADAM_STEP)
        d_ref[...] = -ADAM_LR * (m_hat / (jnp.sqrt(v_hat) + ADAM_EPS) + ADAM_WD * w_ref[...])
        m2_ref[...] = m2
        v2_ref[...] = v2

    spec = pl.BlockSpec((tr, C), lambda i: (i, 0))
    return pl.pallas_call(
        body, name=name, grid=(tiles,), in_specs=[spec] * 4, out_specs=[spec] * 3,
        out_shape=[jax.ShapeDtypeStruct((R, C), F32)] * 3,
        compiler_params=_params(("parallel",)),
    )(w, g, m, v)


def _pack_rows(pieces, rows):
    flat = jnp.concatenate([jnp.pad(p.reshape(-1).astype(F32), (0, (-p.size) % D_MODEL)) for p in pieces])
    return jnp.pad(flat, (0, rows * D_MODEL - flat.size)).reshape(rows, D_MODEL)


def _unpack_rows(pack, shapes):
    flat = pack.reshape(-1)
    out, off = [], 0
    for shp in shapes:
        size = int(np.prod(shp))
        out.append(flat[off:off + size].reshape(shp))
        off += size + (-size) % D_MODEL
    return out


def _kernel_order(w):
    parts = [w[:, 0:RET_W]]
    for p in range(FOX_HEADS // 2):
        parts += [w[:, RET_W + part * 512 + p * BLK:RET_W + part * 512 + (p + 1) * BLK] for part in range(3)]
    return jnp.concatenate(parts, axis=1)


def _reference_order(g_main, g_ff):
    parts = [g_main[:, 0:RET_W]]
    for part in range(3):
        parts += [g_main[:, RET_W + 384 * p + part * BLK:RET_W + 384 * p + (part + 1) * BLK] for p in range(FOX_HEADS // 2)]
    return jnp.concatenate(parts + [g_ff], axis=1)


def kernel(x, meta_tokens, attn_norm_g, w_in, fox_forget_b, ret_norm_g, w_out, ffn_norm_g, w_up, conv_w, conv_b, w_down, final_norm_g, loss_target, m_meta_tokens, m_attn_norm_g, m_w_in, m_fox_forget_b, m_ret_norm_g, m_w_out, m_ffn_norm_g, m_w_up, m_conv_w, m_conv_b, m_w_down, m_final_norm_g, v_meta_tokens, v_attn_norm_g, v_w_in, v_fox_forget_b, v_ret_norm_g, v_w_out, v_ffn_norm_g, v_w_up, v_conv_w, v_conv_b, v_w_down, v_final_norm_g):
    chip = 2 * lax.axis_index("x") + lax.axis_index("y")
    core = lax.axis_index("c")
    meta_w, conv_sw = meta_tokens.shape[1], conv_w.shape[2]

    small_w = _pack_rows([meta_tokens, conv_w[0]], 8)
    g_in, g_out, g_up, g_down, g_small = _chip_allgather(
        [w_in[0].astype(BF16), w_out[0].astype(BF16), w_up[0].astype(BF16), w_down[0].astype(BF16), small_w])
    w_in_full = g_in.transpose(1, 0, 2).reshape(D_MODEL, IN_WIDTH)
    w_main = _kernel_order(w_in_full)
    w_ff = jnp.pad(w_in_full[:, MAIN_W:], ((0, 0), (0, BLK - FOX_HEADS)))
    small_parts = [_unpack_rows(g_small[j], [meta_tokens.shape, conv_w.shape[1:]]) for j in range(N_CHIPS)]
    meta_full = jnp.concatenate([sp[0] for sp in small_parts], axis=1)
    conv_w_full = jnp.concatenate([sp[1] for sp in small_parts], axis=1)

    out = _local_step(x[0], loss_target[0], meta_full, attn_norm_g, w_main, w_ff, fox_forget_b, ret_norm_g,
                      g_out.reshape(D_MODEL, D_MODEL), ffn_norm_g, g_up, conv_w_full, conv_b,
                      g_down.reshape(D_FF, D_MODEL), final_norm_g[None])

    big = [_reference_order(out["w_main"], out["w_ff"]).reshape(D_MODEL, N_CHIPS, -1).transpose(1, 0, 2),
           out["w_out"].reshape(N_CHIPS, -1, D_MODEL), out["w_up"], out["w_down"].reshape(N_CHIPS, -1, D_MODEL)]
    small_shapes = [(1, D_MODEL), (1, D_MODEL), (1, D_MODEL), (1, 512 + FOX_HEADS + 1), (1, D_FF), (N_META, D_MODEL), (3, D_FF)]
    small = _pack_rows([out["attn_g"], out["ffn_g"], out["final_g"],
                        jnp.concatenate([out["ret_g"], out["fox_b"], out["loss"].reshape(1, 1)], axis=1),
                        out["conv_b"], out["dmeta"], out["conv_w"]], 32)
    *from_sibling, small_all = _sibling_exchange(big, small)
    core_idx = core.reshape(1).astype(jnp.int32)
    names = ("in", "out", "up", "down")
    chip_sums = [_pair_add(g, r, core_idx, "pair_add_" + nm) for g, r, nm in zip(big, from_sibling, names)]
    from_chips = _chip_reduce_scatter(chip_sums)
    place = jnp.stack([chip, core]).astype(jnp.int32)
    totals = [_sum_partials(s, q, place, "sum_chips_" + nm) for s, q, nm in zip(chip_sums, from_chips, names)]
    grad_in, grad_out, grad_up, grad_down = _sibling_allgather(totals)
    s_attn, s_ffn, s_final, s_misc, s_conv_b, s_meta, s_conv_w = _unpack_rows(
        _sum_slots(small_all, "sum_small", tiles=1), small_shapes)
    loss = s_misc[0, 512 + FOX_HEADS]
    small_grads = [lax.dynamic_slice_in_dim(s_meta, chip * meta_w, meta_w, axis=1), s_attn, s_misc[:, 512:512 + FOX_HEADS],
                   s_misc[:, :512], s_ffn, lax.dynamic_slice_in_dim(s_conv_w, chip * conv_sw, conv_sw, axis=1)[None],
                   s_conv_b, s_final[0]]

    big_w = [(w_in, m_w_in, v_w_in, grad_in, "adamw_in"), (w_out, m_w_out, v_w_out, grad_out, "adamw_out"),
             (w_up, m_w_up, v_w_up, grad_up, "adamw_up"), (w_down, m_w_down, v_w_down, grad_down, "adamw_down")]
    big_res = [[g[None]] + [r[None] for r in _adamw(w[0], g, m[0], v[0], nm)] for w, m, v, g, nm in big_w]
    small_w_list = [meta_tokens, attn_norm_g, fox_forget_b, ret_norm_g, ffn_norm_g, conv_w, conv_b, final_norm_g]
    small_m = [m_meta_tokens, m_attn_norm_g, m_fox_forget_b, m_ret_norm_g, m_ffn_norm_g, m_conv_w, m_conv_b, m_final_norm_g]
    small_v = [v_meta_tokens, v_attn_norm_g, v_fox_forget_b, v_ret_norm_g, v_ffn_norm_g, v_conv_w, v_conv_b, v_final_norm_g]
    shapes = [a.shape for a in small_w_list]
    packs = [_pack_rows(lst, 16) for lst in (small_w_list, small_grads, small_m, small_v)]
    small_res = [_unpack_rows(r, shapes) for r in _adamw(*packs, "adamw_small", tiles=1)]
    small_grads = [g.reshape(s) for g, s in zip(small_grads, shapes)]

    def ordered(kind):
        sm = small_grads if kind == 0 else small_res[kind - 1]
        bg = [r[kind] for r in big_res]
        return [sm[0], sm[1], bg[0], sm[2], sm[3], bg[1], sm[4], bg[2], sm[5], sm[6], bg[3], sm[7]]

    return (loss, out["dx"][None], *ordered(0), *ordered(1), *ordered(2), *ordered(3))
```

```python
import functools

import numpy as np
import jax
import jax.numpy as jnp
from jax import lax
from jax.experimental import pallas as pl
from jax.experimental.pallas import tpu as pltpu

F32 = jnp.float32
BF16 = jnp.bfloat16

D_MODEL = 1024
N_META = 16
BLK = 128
UNIT = 2 * BLK
WIDE = 4
CHUNK = 64
N_PAD = BLK - N_META
PREFIX = BLK
RET_HEADS = 4
FOX_HEADS = 8
HEAD_LANES = 64
D_FF = 2816
ROPE_BASE = 10000.0
EPS = 1e-6
NEG = -1e30
RET_W = 1536
FOX_W = 1536
MAIN_W = RET_W + FOX_W
IN_WIDTH = MAIN_W + FOX_HEADS
N_CHIPS = 4
N_DEV = 8

ADAM_LR = 0.001
ADAM_B1 = 0.9
ADAM_B2 = 0.999
ADAM_EPS = 1e-08
ADAM_WD = 0.01
ADAM_STEP = 10

MESH = pl.DeviceIdType.MESH
VMEM_LIMIT_MB = 56

_NT = (((1,), (1,)), ((), ()))
_TN = (((0,), (0,)), ((), ()))


def _dot(a, b):
    return jnp.dot(a, b, preferred_element_type=F32)


def _dot_nt(a, b):
    return lax.dot_general(a, b, _NT, preferred_element_type=F32)


def _dot_tn(a, b):
    return lax.dot_general(a, b, _TN, preferred_element_type=F32)


def _params(dims=None, vmem_mb=VMEM_LIMIT_MB):
    kw = dict(vmem_limit_bytes=vmem_mb << 20)
    if dims is not None:
        kw["dimension_semantics"] = dims
    return pltpu.CompilerParams(**kw)


def _row_tile(n, prefs=(384, 256, 128)):
    for t in prefs:
        if n % t == 0:
            return t
    raise ValueError(f"no row tile for {n}")


def _iota(shape, dim):
    return lax.broadcasted_iota(jnp.int32, shape, dim)


def _pick_row(tile, row):
    sub = _iota(tile.shape, 0)
    return jnp.sum(jnp.where(sub == row, tile, 0.0), axis=0, keepdims=True)


def _split3(x):
    hi = x.astype(BF16)
    r1 = x - hi.astype(F32)
    mid = r1.astype(BF16)
    lo = (r1 - mid.astype(F32)).astype(BF16)
    return hi, mid, lo


def _full(shape):
    nd = len(shape)
    return pl.BlockSpec(shape, lambda *_: (0,) * nd)


def _in_perm():
    cols = list(range(RET_W))
    for p in range(FOX_HEADS // 2):
        for part in range(3):
            start = RET_W + part * 512 + p * BLK
            cols += list(range(start, start + BLK))
    return np.asarray(cols, np.int32)


def _rotary_tables(L):
    half = HEAD_LANES // 2
    inv = 1.0 / (ROPE_BASE ** (jnp.arange(half, dtype=F32) / half))
    ang = jnp.arange(L).astype(F32)[:, None] * inv[None, :]
    cos, sin = jnp.cos(ang), jnp.sin(ang)
    cos_t = jnp.tile(cos, (1, 4))
    sin_t = jnp.tile(jnp.concatenate([-sin, sin], axis=1), (1, 2))
    return cos_t, sin_t


def _decay_tables():
    gam = 1.0 - 2.0 ** (-5.0 - np.arange(RET_HEADS, dtype=np.float64))
    n = np.arange(BLK)
    same_or_past = (n[:, None] // CHUNK) >= (n[None, :] // CHUNK)
    dist = np.abs(n[:, None] - n[None, :])
    dmat = np.stack([np.where(same_or_past, g ** dist, 0.0) for g in gam]).astype(np.float32)
    lane_head = np.arange(BLK) // HEAD_LANES
    wq = np.stack([gam[2 * p + lane_head][None, :] ** (n[:, None] + 1.0) for p in range(2)]).astype(np.float32)
    wk = np.stack([gam[2 * p + lane_head][None, :] ** (BLK - 1.0 - n[:, None]) for p in range(2)]).astype(np.float32)
    g_blk = tuple(float(g ** BLK) for g in gam)
    return jnp.asarray(dmat), jnp.asarray(wq), jnp.asarray(wk), g_blk


def _rms_inproj(h0, g, w_main, w_ff):
    L = h0.shape[0]
    tm = _row_tile(L)

    def body(h_ref, g_ref, wm_ref, wf_ref, n_ref, p_ref, ff_ref):
        h = h_ref[...]
        r = lax.rsqrt(jnp.mean(h * h, axis=-1, keepdims=True) + EPS)
        n = (h * r * g_ref[...]).astype(BF16)
        n_ref[...] = n
        p_ref[...] = _dot(n, wm_ref[...]).astype(BF16)
        ff_ref[...] = _dot(n, wf_ref[...])

    return pl.pallas_call(
        body, name="f_inproj", grid=(L // tm,),
        in_specs=[pl.BlockSpec((tm, D_MODEL), lambda i: (i, 0)), _full((1, D_MODEL)),
                  _full((D_MODEL, MAIN_W)), _full((D_MODEL, BLK))],
        out_specs=[pl.BlockSpec((tm, D_MODEL), lambda i: (i, 0)), pl.BlockSpec((tm, MAIN_W), lambda i: (i, 0)),
                   pl.BlockSpec((tm, BLK), lambda i: (i, 0))],
        out_shape=[jax.ShapeDtypeStruct((L, D_MODEL), BF16), jax.ShapeDtypeStruct((L, MAIN_W), BF16),
                   jax.ShapeDtypeStruct((L, BLK), F32)],
        compiler_params=_params(("parallel",)),
    )(h0, g, w_main, w_ff)


def _fox_prep(ff, fb):
    L = ff.shape[0]
    nblk = L // BLK

    def body(ff_ref, b_ref, c_ref, ct_ref, carry):
        i = pl.program_id(0)

        @pl.when(i == 0)
        def _():
            carry[...] = jnp.zeros_like(carry)

        z = ff_ref[...] + b_ref[...]
        lf = jnp.minimum(z, 0.0) - jnp.log1p(jnp.exp(-jnp.abs(z)))
        lf = jnp.where(_iota((BLK, BLK), 1) < FOX_HEADS, lf, 0.0)
        tri = (_iota((BLK, BLK), 0) >= _iota((BLK, BLK), 1)).astype(BF16)
        hi, mid, lo = _split3(lf)
        cs = _dot(tri, hi) + _dot(tri, mid) + _dot(tri, lo) + carry[...]
        c_ref[...] = cs
        ct_ref[0] = cs.T[0:8, :]
        carry[...] = carry[...] + jnp.sum(lf, axis=0, keepdims=True)

    return pl.pallas_call(
        body, name="f_foxprep", grid=(nblk,),
        in_specs=[pl.BlockSpec((BLK, BLK), lambda i: (i, 0)), _full((1, BLK))],
        out_specs=[pl.BlockSpec((BLK, BLK), lambda i: (i, 0)), pl.BlockSpec((1, 8, BLK), lambda i: (i, 0, 0))],
        out_shape=[jax.ShapeDtypeStruct((L, BLK), F32), jax.ShapeDtypeStruct((nblk, 8, BLK), F32)],
        scratch_shapes=[pltpu.VMEM((1, BLK), F32)],
        compiler_params=_params(("arbitrary",)),
    )(ff, fb)


def _rot_fns(cos, sin):
    lane = _iota((BLK, BLK), 1)
    first = (lane & (HEAD_LANES - 1)) < HEAD_LANES // 2

    def swap(x):
        return jnp.where(first, pltpu.roll(x, BLK - 32, 1), pltpu.roll(x, 32, 1))

    def rot(x):
        return x * cos + swap(x) * sin

    def rot_t(dy):
        return dy * cos + swap(dy * sin)

    return rot, rot_t


def _retention_fwd(proj, cos_t, sin_t, ret_g):
    L = proj.shape[0]
    nblk = L // BLK
    dmat, wq_t, wk_t, g_blk = _decay_tables()

    def body(q_ref, k_ref, v_ref, gate_ref, cos_ref, sin_ref, d_ref, wq_ref, wk_ref, rg_ref,
             mix_ref, o_ref, rs_ref, state):
        i = pl.program_id(0)

        @pl.when(i == 0)
        def _():
            state[...] = jnp.zeros_like(state)

        rot, _ = _rot_fns(cos_ref[...], sin_ref[...])
        lane = _iota((BLK, BLK), 1)
        sub = _iota((BLK, BLK), 0)
        for p in range(2):
            qr = rot(q_ref[:, p * BLK:(p + 1) * BLK].astype(F32))
            kr = rot(k_ref[:, p * BLK:(p + 1) * BLK].astype(F32)) * (HEAD_LANES ** -0.5)
            kr_b = kr.astype(BF16)
            qw = (qr * wq_ref[p]).astype(BF16)
            kw = (kr * wk_ref[p]).astype(BF16)
            for e in range(2):
                h = 2 * p + e
                cols = slice(h * BLK, (h + 1) * BLK)
                qm = jnp.where((lane >> 6) == e, qr, 0.0).astype(BF16)
                s = _dot_nt(qm, kr_b) * d_ref[h]
                vh = v_ref[:, cols]
                st = state[h]
                rs_ref[0, h] = st
                o = _dot(s.astype(BF16), vh) + _dot(qw, st.astype(BF16))
                u = jnp.where((sub >> 6) == e, _dot_tn(kw, vh), 0.0)
                state[h] = g_blk[h] * st + u
                rn = lax.rsqrt(jnp.mean(o * o, axis=-1, keepdims=True) + EPS)
                gate = gate_ref[:, cols].astype(F32)
                o_ref[:, cols] = o
                mix_ref[:, cols] = (o * rn * rg_ref[:, cols] * (gate * jax.nn.sigmoid(gate))).astype(BF16)

    row = lambda c: (lambda i: (i, c))
    return pl.pallas_call(
        body, name="f_retention", grid=(nblk,),
        in_specs=[pl.BlockSpec((BLK, 256), row(0)), pl.BlockSpec((BLK, 256), row(1)),
                  pl.BlockSpec((BLK, 512), row(1)), pl.BlockSpec((BLK, 512), row(2)),
                  pl.BlockSpec((BLK, BLK), row(0)), pl.BlockSpec((BLK, BLK), row(0)),
                  _full((RET_HEADS, BLK, BLK)), _full((2, BLK, BLK)), _full((2, BLK, BLK)), _full((1, 512))],
        out_specs=[pl.BlockSpec((BLK, 512), row(0)), pl.BlockSpec((BLK, 512), row(0)),
                   pl.BlockSpec((1, RET_HEADS, BLK, BLK), lambda i: (i, 0, 0, 0))],
        out_shape=[jax.ShapeDtypeStruct((L, 512), BF16), jax.ShapeDtypeStruct((L, 512), F32),
                   jax.ShapeDtypeStruct((nblk, RET_HEADS, BLK, BLK), F32)],
        scratch_shapes=[pltpu.VMEM((RET_HEADS, BLK, BLK), F32)],
        compiler_params=_params(("arbitrary",)),
    )(proj, proj, proj, proj, cos_t, sin_t, dmat, wq_t, wk_t, ret_g)


def _fox_units(L):
    nblk = L // BLK
    assert L % BLK == 0 and nblk % 2 == 1, "sequence must be one 128-row block plus whole 256-row tiles"
    return nblk, (nblk - 1) // 2


def _fox_tile_masks():
    sub, lane = _iota((BLK, BLK), 0), _iota((BLK, BLK), 1)
    return dict(first=(sub <= lane) & (sub >= N_PAD), valid=_iota((BLK, UNIT), 0) >= N_PAD,
                diag=_iota((UNIT, UNIT), 0) <= _iota((UNIT, UNIT), 1))


def _fox_fwd(proj, c, ctb, gather=()):
    L = proj.shape[0]
    nblk, nu = _fox_units(L)
    scale = HEAD_LANES ** -0.5
    ng = len(gather)

    def body(qkv_ref, c_ref, ct_ref, *rest):
        g_in, (of_ref, lse_ref), g_out = rest[:ng], rest[ng:ng + 2], rest[ng + 2:2 * ng + 2]
        vt, csb = rest[2 * ng + 2:2 * ng + 4]
        p = pl.program_id(0)

        @pl.when(p == 0)
        def _():
            lse_ref[...] = jnp.zeros_like(lse_ref)
            if ng:
                local, sends, _ = _allgather_copies(g_in, g_out, *rest[2 * ng + 4:])
                for cp in local + sends:
                    cp.start()

        lane = _iota((BLK, BLK), 1)
        sub8 = _iota((8, BLK), 0)
        masks = _fox_tile_masks()

        def pre(j, carry):
            off = pl.multiple_of(j * BLK, BLK)
            vt[j] = qkv_ref[pl.ds(off, BLK), 2 * BLK:3 * BLK].astype(F32).T.astype(BF16)
            ct = c_ref[pl.ds(off, BLK), :]
            for e in range(2):
                col = jnp.sum(jnp.where(lane == 2 * p + e, ct, 0.0), axis=1, keepdims=True)
                csb[e, j] = jnp.broadcast_to(col, (BLK, UNIT))
            return carry

        lax.fori_loop(0, nblk, pre, 0)

        def attend(qblk, nq, n_whole):
            qlen = nq * BLK
            qoff = pl.multiple_of(qblk * BLK, BLK)
            qs = qkv_ref[pl.ds(qoff, qlen), 0:BLK].astype(F32) * scale
            qlane = _iota((qlen, BLK), 1)
            qm = [jnp.where((qlane >> 6) == e, qs, 0.0).astype(BF16) for e in range(2)]
            ct_row = [jnp.concatenate([_pick_row(ct_ref[qblk + a], 2 * p + e) for a in range(nq)], axis=1)
                      for e in range(2)]

            def step(kblk, nk, mask, st):
                koff = pl.multiple_of(kblk * BLK, BLK)
                kt = qkv_ref[pl.ds(koff, nk * BLK), BLK:2 * BLK]
                out = []
                for e in range(2):
                    m, l, acc = st[3 * e:3 * e + 3]
                    s = _dot_nt(kt, qm[e])
                    t = jnp.concatenate([s[b * BLK:(b + 1) * BLK] - csb[e, kblk + b, :, 0:qlen] for b in range(nk)], axis=0)
                    if mask is not None:
                        t = jnp.where(mask, t, NEG)
                    m_new = jnp.maximum(m, jnp.max(t, axis=0, keepdims=True) + ct_row[e])
                    alpha = jnp.exp(m - m_new)
                    pr = jnp.exp(t - (m_new - ct_row[e]))
                    l = alpha * l + jnp.sum(pr, axis=0, keepdims=True)
                    pr_b = pr.astype(BF16)
                    pv = _dot(vt[kblk, e * HEAD_LANES:(e + 1) * HEAD_LANES, :], pr_b[0:BLK])
                    for b in range(1, nk):
                        pv = pv + _dot(vt[kblk + b, e * HEAD_LANES:(e + 1) * HEAD_LANES, :], pr_b[b * BLK:(b + 1) * BLK])
                    out += [m_new, l, alpha * acc + pv]
                return tuple(out)

            st = (jnp.full((1, qlen), NEG, F32), jnp.zeros((1, qlen), F32), jnp.zeros((HEAD_LANES, qlen), F32)) * 2
            if nq == 1:
                st = step(0, 1, masks["first"], st)
            else:
                st = step(0, 1, masks["valid"], st)
                n_wide = n_whole // WIDE
                st = lax.fori_loop(0, n_wide, lambda j, s_: step(1 + 2 * WIDE * j, 2 * WIDE, None, s_), st)
                st = lax.fori_loop(0, n_whole & (WIDE - 1),
                                   lambda j, s_: step(1 + 2 * WIDE * n_wide + 2 * j, 2, None, s_), st)
                st = step(qblk, 2, masks["diag"], st)
            o_t = jnp.concatenate([st[2] * (1.0 / st[1]), st[5] * (1.0 / st[4])], axis=0)
            of_ref[pl.ds(qoff, qlen), :] = o_t.T.astype(BF16)
            lse = [st[3 * e] + jnp.log(st[3 * e + 1]) for e in range(2)]
            for a in range(nq):
                rows = [lse[e][:, a * BLK:(a + 1) * BLK] for e in range(2)]
                lse_ref[qblk + a] = lse_ref[qblk + a] + (
                    jnp.where(sub8 == 2 * p, rows[0], 0.0) + jnp.where(sub8 == 2 * p + 1, rows[1], 0.0))

        attend(0, 1, 0)

        def q_loop(u, carry):
            attend(1 + 2 * u, 2, u)
            return carry

        lax.fori_loop(0, nu, q_loop, 0)

        if ng:
            @pl.when(p == FOX_HEADS // 2 - 1)
            def _():
                local, sends, recvs = _allgather_copies(g_in, g_out, *rest[2 * ng + 4:])
                for cp in recvs:
                    cp.wait_recv()
                for cp in sends:
                    cp.wait_send()
                for cp in local:
                    cp.wait()

    return pl.pallas_call(
        body, name="f_fox", grid=(FOX_HEADS // 2,),
        in_specs=[pl.BlockSpec((L, 384), lambda p: (0, RET_W // 384 + p)), _full((L, BLK)), _full((nblk, 8, BLK))]
        + [_ANY] * ng,
        out_specs=[pl.BlockSpec((L, BLK), lambda p: (0, p)), _full((nblk, 8, BLK))] + [_ANY] * ng,
        out_shape=[jax.ShapeDtypeStruct((L, 512), BF16), jax.ShapeDtypeStruct((nblk, 8, BLK), F32)]
        + [jax.ShapeDtypeStruct((N_CHIPS,) + a.shape, a.dtype) for a in gather],
        scratch_shapes=[pltpu.VMEM((nblk, BLK, BLK), BF16), pltpu.VMEM((2, nblk, BLK, UNIT), F32)]
        + _allgather_semaphores(ng),
        compiler_params=_params(("arbitrary",)),
    )(proj, c, ctb, *gather)


def _outproj_up(mix_r, o_f, h0, w_out, ffn_g, w_up):
    L = h0.shape[0]
    tm = _row_tile(L)
    shard = w_up.shape[2]

    def body(mr_ref, of_ref, h0_ref, wo_ref, g_ref, wu_ref, h1_ref, n2_ref, up_ref):
        h1 = h0_ref[...] + _dot(mr_ref[...], wo_ref[0:512, :]) + _dot(of_ref[...], wo_ref[512:1024, :])
        h1_ref[...] = h1
        r = lax.rsqrt(jnp.mean(h1 * h1, axis=-1, keepdims=True) + EPS)
        n2 = (h1 * r * g_ref[...]).astype(BF16)
        n2_ref[...] = n2
        for j in range(N_CHIPS):
            up_ref[:, j * shard:(j + 1) * shard] = _dot(n2, wu_ref[j]).astype(BF16)

    rows = lambda w: pl.BlockSpec((tm, w), lambda i: (i, 0))
    return pl.pallas_call(
        body, name="f_outproj_up", grid=(L // tm,),
        in_specs=[rows(512), rows(512), rows(D_MODEL), _full((D_MODEL, D_MODEL)), _full((1, D_MODEL)),
                  _full((N_CHIPS, D_MODEL, shard))],
        out_specs=[rows(D_MODEL), rows(D_MODEL), rows(2 * D_FF)],
        out_shape=[jax.ShapeDtypeStruct((L, D_MODEL), F32), jax.ShapeDtypeStruct((L, D_MODEL), BF16),
                   jax.ShapeDtypeStruct((L, 2 * D_FF), BF16)],
        compiler_params=_params(("parallel",)),
    )(mix_r, o_f, h0, w_out, ffn_g, w_up)


def _conv_acc(a_ref, halo_ref, cw_refs, cb_ref, i, tm):
    sub = _iota((tm, 1), 0)
    a = jnp.where(i * tm + sub >= N_PAD, a_ref[...].astype(F32), 0.0)
    halo = halo_ref[...].astype(F32)
    hrow = i * tm - 8 + _iota((8, 1), 0)
    halo = jnp.where((hrow >= N_PAD) & (i > 0), halo, 0.0)
    a1 = jnp.where(sub == 0, _pick_row(halo, 7), pltpu.roll(a, 1, 0))
    a2 = jnp.where(sub == 0, _pick_row(halo, 6), jnp.where(sub == 1, _pick_row(halo, 7), pltpu.roll(a, 2, 0)))
    acc = cb_ref[...] + a2 * cw_refs[0][...]
    acc = acc + a1 * cw_refs[1][...]
    acc = acc + a * cw_refs[2][...]
    return a, a1, a2, acc


def _ffn_down_loss(up, conv_w, conv_b, w_down, h1, final_g, target):
    L = h1.shape[0]
    tm = _row_tile(L)
    cw = [conv_w[j:j + 1] for j in range(3)]

    def body(a_ref, halo_ref, b_ref, cw0, cw1, cw2, cb_ref, wd_ref, h1_ref, gf_ref, t_ref,
             g_ref, dh_ref, dhb_ref, dgf_ref, loss_ref):
        i = pl.program_id(0)

        @pl.when(i == 0)
        def _():
            dgf_ref[...] = jnp.zeros_like(dgf_ref)
            loss_ref[...] = jnp.zeros_like(loss_ref)

        _, _, _, acc = _conv_acc(a_ref, halo_ref, (cw0, cw1, cw2), cb_ref, i, tm)
        g = (acc * jax.nn.sigmoid(acc) * b_ref[...].astype(F32)).astype(BF16)
        g_ref[...] = g
        h2 = h1_ref[...] + _dot(g, wd_ref[...])
        r = lax.rsqrt(jnp.mean(h2 * h2, axis=-1, keepdims=True) + EPS)
        yn = h2 * r
        gf = gf_ref[...]
        live = i * tm + _iota((tm, 1), 0) >= PREFIX
        err = jnp.where(live, yn * gf - t_ref[...], 0.0)
        loss_ref[...] = loss_ref[...] + 0.5 * jnp.sum(jnp.mean(err * err, axis=-1, keepdims=True))
        dy = err * (1.0 / D_MODEL)
        dgf_ref[...] = dgf_ref[...] + jnp.sum(dy * yn, axis=0, keepdims=True)
        dyn = dy * gf
        dh = r * (dyn - yn * jnp.mean(dyn * yn, axis=-1, keepdims=True))
        dh_ref[...] = dh
        dhb_ref[...] = dh.astype(BF16)

    rows = lambda w, c=0: pl.BlockSpec((tm, w), lambda i: (i, c))
    halo = pl.BlockSpec((8, D_FF), lambda i: (jnp.maximum(i * (tm // 8) - 1, 0), 0))
    return pl.pallas_call(
        body, name="f_ffn_down_loss", grid=(L // tm,),
        in_specs=[rows(D_FF), halo, rows(D_FF, 1), _full((1, D_FF)), _full((1, D_FF)), _full((1, D_FF)),
                  _full((1, D_FF)), _full((D_FF, D_MODEL)), rows(D_MODEL), _full((1, D_MODEL)), rows(D_MODEL)],
        out_specs=[rows(D_FF), rows(D_MODEL), rows(D_MODEL), _full((1, D_MODEL)), _full((1, BLK))],
        out_shape=[jax.ShapeDtypeStruct((L, D_FF), BF16), jax.ShapeDtypeStruct((L, D_MODEL), F32),
                   jax.ShapeDtypeStruct((L, D_MODEL), BF16), jax.ShapeDtypeStruct((1, D_MODEL), F32),
                   jax.ShapeDtypeStruct((1, BLK), F32)],
        compiler_params=_params(("arbitrary",)),
    )(up, up, up, cw[0], cw[1], cw[2], conv_b, w_down, h1, final_g, target)


def _ffn_bwd_gate(dh2b, w_down, up, conv_w, conv_b):
    L = dh2b.shape[0]
    tm = _row_tile(L)
    cw = [conv_w[j:j + 1] for j in range(3)]

    def body(dh_ref, wd_ref, a_ref, halo_ref, b_ref, cw0, cw1, cw2, cb_ref, dacc_ref, db_ref, dcw_ref):
        i = pl.program_id(0)

        @pl.when(i == 0)
        def _():
            dcw_ref[...] = jnp.zeros_like(dcw_ref)

        a, a1, a2, acc = _conv_acc(a_ref, halo_ref, (cw0, cw1, cw2), cb_ref, i, tm)
        dg = _dot_nt(dh_ref[...], wd_ref[...])
        sg = jax.nn.sigmoid(acc)
        db_ref[...] = (dg * acc * sg).astype(BF16)
        dacc = dg * b_ref[...].astype(F32) * (sg * (1.0 + acc * (1.0 - sg)))
        dacc_ref[...] = dacc.astype(BF16)
        sub8 = _iota((8, 1), 0)
        rows = [jnp.sum(dacc * t, axis=0, keepdims=True) for t in (a2, a1, a)] + [jnp.sum(dacc, axis=0, keepdims=True)]
        upd = jnp.zeros((8, D_FF), F32)
        for j, rj in enumerate(rows):
            upd = upd + jnp.where(sub8 == j, rj, 0.0)
        dcw_ref[...] = dcw_ref[...] + upd

    rows = lambda w, c=0: pl.BlockSpec((tm, w), lambda i: (i, c))
    halo = pl.BlockSpec((8, D_FF), lambda i: (jnp.maximum(i * (tm // 8) - 1, 0), 0))
    return pl.pallas_call(
        body, name="b_ffn_gate", grid=(L // tm,),
        in_specs=[rows(D_MODEL), _full((D_FF, D_MODEL)), rows(D_FF), halo, rows(D_FF, 1),
                  _full((1, D_FF)), _full((1, D_FF)), _full((1, D_FF)), _full((1, D_FF))],
        out_specs=[rows(D_FF), rows(D_FF), _full((8, D_FF))],
        out_shape=[jax.ShapeDtypeStruct((L, D_FF), BF16), jax.ShapeDtypeStruct((L, D_FF), BF16),
                   jax.ShapeDtypeStruct((8, D_FF), F32)],
        compiler_params=_params(("arbitrary",)),
    )(dh2b, w_down, up, up, up, cw[0], cw[1], cw[2], conv_b)


def _ffn_bwd_up(dacc, db, conv_w, w_up, h1, ffn_g, dh2, w_out):
    L = h1.shape[0]
    tm = _row_tile(L)
    nt = L // tm
    shard = w_up.shape[2]
    cw = [conv_w[j:j + 1] for j in range(3)]

    def body(da_ref, halo_ref, db_ref, cw0, cw1, cw2, wu_ref, h1_ref, g_ref, dh2_ref, wo_ref,
             dup_ref, dh1_ref, dh1b_ref, dmix_ref, dg_ref):
        i = pl.program_id(0)

        @pl.when(i == 0)
        def _():
            dg_ref[...] = jnp.zeros_like(dg_ref)

        sub = _iota((tm, 1), 0)
        d0 = da_ref[...].astype(F32)
        halo = jnp.where(i < nt - 1, halo_ref[...].astype(F32), 0.0)
        d1 = jnp.where(sub == tm - 1, _pick_row(halo, 0), pltpu.roll(d0, tm - 1, 0))
        d2 = jnp.where(sub == tm - 2, _pick_row(halo, 0),
                       jnp.where(sub == tm - 1, _pick_row(halo, 1), pltpu.roll(d0, tm - 2, 0)))
        da = d0 * cw2[...] + d1 * cw1[...] + d2 * cw0[...]
        da = jnp.where(i * tm + sub >= N_PAD, da, 0.0).astype(BF16)
        dup_ref[:, 0:D_FF] = da
        dbv = db_ref[...]
        dup_ref[:, D_FF:2 * D_FF] = dbv
        dn = jnp.zeros((tm, D_MODEL), F32)
        for j in range(N_CHIPS):
            src = da if j < 2 else dbv
            lo = (j % 2) * shard
            dn = dn + _dot_nt(src[:, lo:lo + shard], wu_ref[j])
        h1 = h1_ref[...]
        r = lax.rsqrt(jnp.mean(h1 * h1, axis=-1, keepdims=True) + EPS)
        yn = h1 * r
        dg_ref[...] = dg_ref[...] + jnp.sum(dn * yn, axis=0, keepdims=True)
        dyn = dn * g_ref[...]
        dh1 = dh2_ref[...] + r * (dyn - yn * jnp.mean(dyn * yn, axis=-1, keepdims=True))
        dh1_ref[...] = dh1
        dh1b = dh1.astype(BF16)
        dh1b_ref[...] = dh1b
        dmix_ref[...] = _dot_nt(dh1b, wo_ref[...]).astype(BF16)

    rows = lambda w: pl.BlockSpec((tm, w), lambda i: (i, 0))
    halo = pl.BlockSpec((8, D_FF), lambda i: (jnp.minimum((i + 1) * (tm // 8), L // 8 - 1), 0))
    return pl.pallas_call(
        body, name="b_ffn_up", grid=(nt,),
        in_specs=[rows(D_FF), halo, rows(D_FF), _full((1, D_FF)), _full((1, D_FF)), _full((1, D_FF)),
                  _full((N_CHIPS, D_MODEL, shard)), rows(D_MODEL), _full((1, D_MODEL)), rows(D_MODEL),
                  _full((D_MODEL, D_MODEL))],
        out_specs=[rows(2 * D_FF), rows(D_MODEL), rows(D_MODEL), rows(D_MODEL), _full((1, D_MODEL))],
        out_shape=[jax.ShapeDtypeStruct((L, 2 * D_FF), BF16), jax.ShapeDtypeStruct((L, D_MODEL), F32),
                   jax.ShapeDtypeStruct((L, D_MODEL), BF16), jax.ShapeDtypeStruct((L, D_MODEL), BF16),
                   jax.ShapeDtypeStruct((1, D_MODEL), F32)],
        compiler_params=_params(("arbitrary",)),
    )(dacc, dacc, db, cw[0], cw[1], cw[2], w_up, h1, ffn_g, dh2, w_out)


def _wgrad(a, b, name, tn=None):
    L, K = a.shape
    N = b.shape[1]
    tn = N if tn is None else tn
    tl = _row_tile(L)

    def body(a_ref, b_ref, o_ref):
        @pl.when(pl.program_id(1) == 0)
        def _():
            o_ref[...] = jnp.zeros_like(o_ref)

        o_ref[0] = o_ref[0] + _dot_tn(a_ref[...], b_ref[...])

    return pl.pallas_call(
        body, name=name, grid=(N // tn, L // tl),
        in_specs=[pl.BlockSpec((tl, K), lambda n, l: (l, 0)), pl.BlockSpec((tl, tn), lambda n, l: (l, n))],
        out_specs=pl.BlockSpec((1, K, tn), lambda n, l: (n, 0, 0)),
        out_shape=jax.ShapeDtypeStruct((N // tn, K, tn), F32),
        compiler_params=_params(("parallel", "arbitrary")),
    )(a, b)


def _retention_bwd(dmix, o, proj, cos_t, sin_t, ret_g, states):
    L = proj.shape[0]
    nblk = L // BLK
    dmat, wq_t, wk_t, g_blk = _decay_tables()

    def body(dm_ref, o_ref, q_ref, k_ref, v_ref, gate_ref, cos_ref, sin_ref, d_ref, wq_ref, wk_ref, rg_ref, rs_ref,
             dp_ref, drg_ref, gstate):
        i = pl.program_id(0)

        @pl.when(i == 0)
        def _():
            gstate[...] = jnp.zeros_like(gstate)
            drg_ref[...] = jnp.zeros_like(drg_ref)

        rot, rot_t = _rot_fns(cos_ref[...], sin_ref[...])
        lane = _iota((BLK, BLK), 1)
        sub = _iota((BLK, BLK), 0)
        scale = HEAD_LANES ** -0.5
        for p in range(2):
            qr = rot(q_ref[:, p * BLK:(p + 1) * BLK].astype(F32))
            kr = rot(k_ref[:, p * BLK:(p + 1) * BLK].astype(F32)) * scale
            kr_b = kr.astype(BF16)
            qw = (qr * wq_ref[p]).astype(BF16)
            kw = (kr * wk_ref[p]).astype(BF16)
            dqr = jnp.zeros((BLK, BLK), F32)
            dkr = jnp.zeros((BLK, BLK), F32)
            for e in range(2):
                h = 2 * p + e
                cols = slice(h * BLK, (h + 1) * BLK)
                head_lanes = (lane >> 6) == e
                o = o_ref[:, cols]
                rn = lax.rsqrt(jnp.mean(o * o, axis=-1, keepdims=True) + EPS)
                y = o * rn
                gate = gate_ref[:, cols].astype(F32)
                sg = jax.nn.sigmoid(gate)
                dm = dm_ref[:, cols].astype(F32)
                rgain = rg_ref[:, cols]
                drg_ref[:, cols] = drg_ref[:, cols] + jnp.sum(dm * y * (gate * sg), axis=0, keepdims=True)
                dp_ref[:, 1024 + h * BLK:1024 + (h + 1) * BLK] = (
                    dm * y * rgain * (sg * (1.0 + gate * (1.0 - sg)))).astype(BF16)
                dy = dm * rgain * (gate * sg)
                do = (rn * (dy - y * jnp.mean(dy * y, axis=-1, keepdims=True))).astype(BF16)
                vh = v_ref[:, cols]
                qm = jnp.where(head_lanes, qr, 0.0).astype(BF16)
                dmh = d_ref[h]
                s = (_dot_nt(qm, kr_b) * dmh).astype(BF16)
                ds = (_dot_nt(do, vh) * dmh).astype(BF16)
                st = rs_ref[0, h].astype(BF16)
                gs = gstate[h]
                gs_b = gs.astype(BF16)
                dqr = dqr + jnp.where(head_lanes, _dot(ds, kr_b), 0.0) + _dot_nt(do, st) * wq_ref[p]
                dkr = dkr + _dot_tn(ds, qm) + _dot_nt(vh, gs_b) * wk_ref[p]
                dp_ref[:, 512 + h * BLK:512 + (h + 1) * BLK] = (_dot_tn(s, do) + _dot(kw, gs_b)).astype(BF16)
                dr = jnp.where((sub >> 6) == e, _dot_tn(qw, do), 0.0)
                gstate[h] = dr + g_blk[h] * gs
            dp_ref[:, p * BLK:(p + 1) * BLK] = rot_t(dqr).astype(BF16)
            dp_ref[:, 256 + p * BLK:256 + (p + 1) * BLK] = (rot_t(dkr) * scale).astype(BF16)

    row = lambda c: (lambda i: (nblk - 1 - i, c))
    return pl.pallas_call(
        body, name="b_retention", grid=(nblk,),
        in_specs=[pl.BlockSpec((BLK, 512), row(0)), pl.BlockSpec((BLK, 512), row(0)),
                  pl.BlockSpec((BLK, 256), row(0)), pl.BlockSpec((BLK, 256), row(1)),
                  pl.BlockSpec((BLK, 512), row(1)), pl.BlockSpec((BLK, 512), row(2)),
                  pl.BlockSpec((BLK, BLK), row(0)), pl.BlockSpec((BLK, BLK), row(0)),
                  _full((RET_HEADS, BLK, BLK)), _full((2, BLK, BLK)), _full((2, BLK, BLK)), _full((1, 512)),
                  pl.BlockSpec((1, RET_HEADS, BLK, BLK), lambda i: (nblk - 1 - i, 0, 0, 0))],
        out_specs=[pl.BlockSpec((BLK, RET_W), row(0)), _full((1, 512))],
        out_shape=[jax.ShapeDtypeStruct((L, RET_W), BF16), jax.ShapeDtypeStruct((1, 512), F32)],
        scratch_shapes=[pltpu.VMEM((RET_HEADS, BLK, BLK), F32)],
        compiler_params=_params(("arbitrary",)),
    )(dmix, o, proj, proj, proj, proj, cos_t, sin_t, dmat, wq_t, wk_t, ret_g, states)


def _fox_delta(dmix, o_f):
    L = o_f.shape[0]
    nblk = L // BLK

    def body(do_ref, o_ref, d_ref):
        prod = do_ref[...].astype(F32) * o_ref[...].astype(F32)
        sel = ((_iota((8, 512), 1) >> 6) == _iota((8, 512), 0)).astype(BF16)
        hi = prod.astype(BF16)
        lo = (prod - hi.astype(F32)).astype(BF16)
        d_ref[0] = _dot_nt(sel, hi) + _dot_nt(sel, lo)

    return pl.pallas_call(
        body, name="b_foxdelta", grid=(nblk,),
        in_specs=[pl.BlockSpec((BLK, 512), lambda i: (i, 1)), pl.BlockSpec((BLK, 512), lambda i: (i, 0))],
        out_specs=pl.BlockSpec((1, 8, BLK), lambda i: (i, 0, 0)),
        out_shape=jax.ShapeDtypeStruct((nblk, 8, BLK), F32),
        compiler_params=_params(("parallel",)),
    )(dmix, o_f)


def _fox_bwd(proj, dmix, c, ctb, lse, delta, scatter=()):
    L = proj.shape[0]
    nblk, nu = _fox_units(L)
    scale = HEAD_LANES ** -0.5
    ns = len(scatter)

    def body(qkv_ref, do_ref, c_ref, ct_ref, lse_ref, dl_ref, *rest):
        s_in, (dp_ref, dc_ref, dcq_ref), s_out = rest[:ns], rest[ns:ns + 3], rest[ns + 3:2 * ns + 3]
        ktt, dqt, dk_acc, dv_acc, dcs_acc = rest[2 * ns + 3:2 * ns + 8]
        p = pl.program_id(0)

        @pl.when(p == 0)
        def _():
            dc_ref[...] = jnp.zeros_like(dc_ref)
            dcq_ref[...] = jnp.zeros_like(dcq_ref)
            if ns:
                for cp in _scatter_copies(s_in, s_out, *rest[2 * ns + 8:]):
                    cp.start()

        lane = _iota((BLK, BLK), 1)
        sub8 = _iota((8, BLK), 0)
        masks = _fox_tile_masks()

        def pre(j, carry):
            off = pl.multiple_of(j * BLK, BLK)
            ktt[j] = qkv_ref[pl.ds(off, BLK), BLK:2 * BLK].astype(F32).T.astype(BF16)
            dqt[j] = jnp.zeros((BLK, BLK), F32)
            return carry

        lax.fori_loop(0, nblk, pre, 0)

        def kv_pass(kblk, nk, n_later):
            klen = nk * BLK
            koff = pl.multiple_of(kblk * BLK, BLK)
            kt = qkv_ref[pl.ds(koff, klen), BLK:2 * BLK]
            vtile = qkv_ref[pl.ds(koff, klen), 2 * BLK:3 * BLK]
            ct = c_ref[pl.ds(koff, klen), :]
            klane = _iota((klen, BLK), 1)
            cs = [jnp.broadcast_to(jnp.sum(jnp.where(klane == 2 * p + e, ct, 0.0), axis=1, keepdims=True),
                                   (klen, WIDE * UNIT)) for e in range(2)]
            dk_acc[0:klen] = jnp.zeros((klen, BLK), F32)
            dv_acc[0:klen] = jnp.zeros((klen, BLK), F32)
            for e in range(2):
                dcs_acc[e, 0:klen] = jnp.zeros((klen, BLK), F32)

            def tile(qblk, nq, mask):
                qlen = nq * BLK
                if mask == "valid":
                    mask = _iota((klen, qlen), 0) >= N_PAD
                qoff = pl.multiple_of(qblk * BLK, BLK)
                qs = qkv_ref[pl.ds(qoff, qlen), 0:BLK].astype(F32) * scale
                dot_ = do_ref[pl.ds(qoff, qlen), :]
                qlane = _iota((qlen, BLK), 1)
                stats = [[ref[qblk + a] for a in range(nq)] for ref in (ct_ref, lse_ref, dl_ref)]
                for e in range(2):
                    h = 2 * p + e
                    head = (qlane >> 6) == e
                    ct_row, lse_row, dl_row = [jnp.concatenate([_pick_row(t, h) for t in ts], axis=1) for ts in stats]
                    qm = jnp.where(head, qs, 0.0).astype(BF16)
                    dom = jnp.where(head, dot_, jnp.zeros_like(dot_))
                    t = _dot_nt(kt, qm) - cs[e][:, 0:qlen]
                    if mask is not None:
                        t = jnp.where(mask, t, NEG)
                    pr = jnp.exp(t + (ct_row - lse_row))
                    dv_acc[0:klen] = dv_acc[0:klen] + _dot(pr.astype(BF16), dom)
                    dsv = pr * (_dot_nt(vtile, dom) - dl_row)
                    ds_b = dsv.astype(BF16)
                    dk_acc[0:klen] = dk_acc[0:klen] + _dot(ds_b, qm)
                    rows = slice(e * HEAD_LANES, (e + 1) * HEAD_LANES)
                    dq_t = _dot(ktt[kblk, rows, :], ds_b[0:BLK])
                    for b in range(1, nk):
                        dq_t = dq_t + _dot(ktt[kblk + b, rows, :], ds_b[b * BLK:(b + 1) * BLK])
                    key_side = dsv[:, 0:BLK]
                    for a in range(1, nq):
                        key_side = key_side + dsv[:, a * BLK:(a + 1) * BLK]
                    dcs_acc[e, 0:klen] = dcs_acc[e, 0:klen] + key_side
                    query_side = jnp.sum(dsv, axis=0, keepdims=True)
                    for a in range(nq):
                        cols = slice(a * BLK, (a + 1) * BLK)
                        dqt[qblk + a, rows, :] = dqt[qblk + a, rows, :] + dq_t[:, cols]
                        dcq_ref[qblk + a] = dcq_ref[qblk + a] + jnp.where(sub8 == h, query_side[:, cols], 0.0)

            later_mask = "valid" if nk == 1 else None
            n_later = jnp.asarray(n_later, jnp.int32)
            n_wide = n_later // WIDE

            def later_wide(i, carry):
                tile(kblk + nk + 2 * WIDE * i, 2 * WIDE, later_mask)
                return carry

            def later(i, carry):
                tile(kblk + nk + 2 * WIDE * n_wide + 2 * i, 2, later_mask)
                return carry

            tile(kblk, nk, masks["first"] if nk == 1 else masks["diag"])
            lax.fori_loop(0, n_wide, later_wide, 0)
            lax.fori_loop(0, n_later & (WIDE - 1), later, 0)
            dp_ref[pl.ds(koff, klen), BLK:2 * BLK] = dk_acc[0:klen].astype(BF16)
            dp_ref[pl.ds(koff, klen), 2 * BLK:3 * BLK] = dv_acc[0:klen].astype(BF16)
            upd = jnp.zeros((klen, BLK), F32)
            for e in range(2):
                upd = upd + jnp.where(klane == 2 * p + e, -jnp.sum(dcs_acc[e, 0:klen], axis=1, keepdims=True), 0.0)
            dc_ref[pl.ds(koff, klen), :] = dc_ref[pl.ds(koff, klen), :] + upd

        kv_pass(0, 1, nu)

        def k_loop(u, carry):
            kv_pass(1 + 2 * u, 2, nu - 1 - u)
            return carry

        lax.fori_loop(0, nu, k_loop, 0)

        def flush(j, carry):
            off = pl.multiple_of(j * BLK, BLK)
            dp_ref[pl.ds(off, BLK), 0:BLK] = (dqt[j].T * scale).astype(BF16)
            return carry

        lax.fori_loop(0, nblk, flush, 0)

        if ns:
            @pl.when(p == FOX_HEADS // 2 - 1)
            def _():
                copies = _scatter_copies(s_in, s_out, *rest[2 * ns + 8:])
                for cp in copies:
                    cp.wait_recv()
                for cp in copies:
                    cp.wait_send()

    stat = _full((nblk, 8, BLK))
    return pl.pallas_call(
        body, name="b_fox", grid=(FOX_HEADS // 2,),
        in_specs=[pl.BlockSpec((L, 384), lambda p: (0, RET_W // 384 + p)), pl.BlockSpec((L, BLK), lambda p: (0, 4 + p)),
                  _full((L, BLK)), stat, stat, stat] + [_ANY] * ns,
        out_specs=[pl.BlockSpec((L, 384), lambda p: (0, p)), _full((L, BLK)), stat] + [_ANY] * ns,
        out_shape=[jax.ShapeDtypeStruct((L, FOX_W), BF16), jax.ShapeDtypeStruct((L, BLK), F32),
                   jax.ShapeDtypeStruct((nblk, 8, BLK), F32)] + _scatter_shapes(scatter),
        scratch_shapes=[pltpu.VMEM((nblk, BLK, BLK), BF16), pltpu.VMEM((nblk, BLK, BLK), F32),
                        pltpu.VMEM((UNIT, BLK), F32), pltpu.VMEM((UNIT, BLK), F32), pltpu.VMEM((2, UNIT, BLK), F32)]
        + _scatter_semaphores(ns),
        compiler_params=_params(("arbitrary",)),
    )(proj, dmix, c, ctb, lse, delta, *scatter)


def _fox_post(dc, dcq, ff, fb):
    L = dc.shape[0]
    nblk = L // BLK

    def body(dc_ref, dcq_ref, ff_ref, b_ref, dff_ref, dffb_ref, dfb_ref, carry):
        i = pl.program_id(0)

        @pl.when(i == 0)
        def _():
            carry[...] = jnp.zeros_like(carry)
            dfb_ref[...] = jnp.zeros_like(dfb_ref)

        d = dc_ref[...] + jnp.concatenate([dcq_ref[0], jnp.zeros((BLK - 8, BLK), F32)], axis=0).T
        tri = (_iota((BLK, BLK), 0) <= _iota((BLK, BLK), 1)).astype(BF16)
        hi, mid, lo = _split3(d)
        dlf = _dot(tri, hi) + _dot(tri, mid) + _dot(tri, lo) + carry[...]
        carry[...] = carry[...] + jnp.sum(d, axis=0, keepdims=True)
        z = ff_ref[...] + b_ref[...]
        dff = jnp.where(_iota((BLK, BLK), 1) < FOX_HEADS, dlf * jax.nn.sigmoid(-z), 0.0)
        dff_ref[...] = dff
        dffb_ref[...] = dff.astype(BF16)
        dfb_ref[...] = dfb_ref[...] + jnp.sum(dff, axis=0, keepdims=True)

    rev = lambda i: (nblk - 1 - i, 0)
    return pl.pallas_call(
        body, name="b_foxpost", grid=(nblk,),
        in_specs=[pl.BlockSpec((BLK, BLK), rev), pl.BlockSpec((1, 8, BLK), lambda i: (nblk - 1 - i, 0, 0)),
                  pl.BlockSpec((BLK, BLK), rev), _full((1, BLK))],
        out_specs=[pl.BlockSpec((BLK, BLK), rev), pl.BlockSpec((BLK, BLK), rev), _full((1, BLK))],
        out_shape=[jax.ShapeDtypeStruct((L, BLK), F32), jax.ShapeDtypeStruct((L, BLK), BF16),
                   jax.ShapeDtypeStruct((1, BLK), F32)],
        scratch_shapes=[pltpu.VMEM((1, BLK), F32)],
        compiler_params=_params(("arbitrary",)),
    )(dc, dcq, ff, fb)


def _inproj_bwd(dpr, dpf, dffb, w_main, w_ff, h0, g, dh1):
    L = h0.shape[0]
    tm = _row_tile(L)

    def body(dpr_ref, dpf_ref, dff_ref, wm_ref, wf_ref, h_ref, g_ref, dh1_ref, dh0_ref, dg_ref):
        @pl.when(pl.program_id(0) == 0)
        def _():
            dg_ref[...] = jnp.zeros_like(dg_ref)

        dn = (_dot_nt(dpr_ref[...], wm_ref[:, 0:RET_W]) + _dot_nt(dpf_ref[...], wm_ref[:, RET_W:MAIN_W])
              + _dot_nt(dff_ref[...], wf_ref[...]))
        h = h_ref[...]
        r = lax.rsqrt(jnp.mean(h * h, axis=-1, keepdims=True) + EPS)
        yn = h * r
        dg_ref[...] = dg_ref[...] + jnp.sum(dn * yn, axis=0, keepdims=True)
        dyn = dn * g_ref[...]
        dh0_ref[...] = dh1_ref[...] + r * (dyn - yn * jnp.mean(dyn * yn, axis=-1, keepdims=True))

    rows = lambda w: pl.BlockSpec((tm, w), lambda i: (i, 0))
    return pl.pallas_call(
        body, name="b_inproj", grid=(L // tm,),
        in_specs=[rows(RET_W), rows(FOX_W), rows(BLK), _full((D_MODEL, MAIN_W)), _full((D_MODEL, BLK)),
                  rows(D_MODEL), _full((1, D_MODEL)), rows(D_MODEL)],
        out_specs=[rows(D_MODEL), _full((1, D_MODEL))],
        out_shape=[jax.ShapeDtypeStruct((L, D_MODEL), F32), jax.ShapeDtypeStruct((1, D_MODEL), F32)],
        compiler_params=_params(("arbitrary",)),
    )(dpr, dpf, dffb, w_main, w_ff, h0, g, dh1)


def _local_step(x, target, meta, attn_g, w_main, w_ff, fox_b, ret_g, w_out, ffn_g, w_up, conv_w, conv_b, w_down, final_g,
                late=None, mid=None):
    S = x.shape[0]
    L = S + PREFIX
    h0 = jnp.concatenate([jnp.zeros((N_PAD, D_MODEL), F32), meta, x], axis=0)
    tgt = jnp.concatenate([jnp.zeros((PREFIX, D_MODEL), F32), target], axis=0)
    fb = jnp.pad(fox_b, ((0, 0), (0, BLK - FOX_HEADS)))
    cos_t, sin_t = _rotary_tables(L)

    n1, proj, ff = _rms_inproj(h0, attn_g, w_main, w_ff)
    c, ctb = _fox_prep(ff, fb)
    mix_r, o_ret, states = _retention_fwd(proj, cos_t, sin_t, ret_g)
    if late is None:
        o_f, lse = _fox_fwd(proj, c, ctb)
    else:
        o_f, lse, *gathered = _fox_fwd(proj, c, ctb, gather=late[0])
        w_out, w_up, w_down = late[1](gathered)
    h1, n2, up = _outproj_up(mix_r, o_f, h0, w_out, ffn_g, w_up)
    g_act, dh2, dh2b, d_final_g, loss = _ffn_down_loss(up, conv_w, conv_b, w_down, h1, final_g, tgt)

    dacc, db, dconv = _ffn_bwd_gate(dh2b, w_down, up, conv_w, conv_b)
    dup, dh1, dh1b, dmix, d_ffn_g = _ffn_bwd_up(dacc, db, conv_w, w_up, h1, ffn_g, dh2, w_out)
    d_w_down = _wgrad(g_act, dh2b, "wgrad_down", tn=None)[0]
    d_w_up = _wgrad(n2, dup, "wgrad_up", tn=w_up.shape[2])
    d_w_out = jnp.concatenate([_wgrad(mix_r, dh1b, "wgrad_out_r")[0], _wgrad(o_f, dh1b, "wgrad_out_f")[0]], axis=0)

    dpr, d_ret_g = _retention_bwd(dmix, o_ret, proj, cos_t, sin_t, ret_g, states)
    delta = _fox_delta(dmix, o_f)
    scatter = () if mid is None else mid(d_w_out, d_w_up, d_w_down)
    dpf, dc, dcq, *received = _fox_bwd(proj, dmix, c, ctb, lse, delta, scatter=scatter)
    dff, dffb, d_fox_b = _fox_post(dc, dcq, ff, fb)
    dh0, d_attn_g = _inproj_bwd(dpr, dpf, dffb, w_main, w_ff, h0, attn_g, dh1)
    d_w_main = jnp.concatenate([_wgrad(n1, dpr, "wgrad_in_r")[0], _wgrad(n1, dpf, "wgrad_in_f")[0]], axis=1)
    d_w_ff = _wgrad(n1, dffb, "wgrad_in_ff")[0]

    return dict(
        loss=loss[0, 0], dx=dh0[PREFIX:], dmeta=dh0[N_PAD:PREFIX], attn_g=d_attn_g, w_main=d_w_main,
        w_ff=d_w_ff[:, :FOX_HEADS], fox_b=d_fox_b[:, :FOX_HEADS], ret_g=d_ret_g, w_out=d_w_out, ffn_g=d_ffn_g,
        w_up=d_w_up, conv_w=dconv[0:3], conv_b=dconv[3:4], w_down=d_w_down, final_g=d_final_g,
        scatter=scatter, received=received)


_ANY = pl.BlockSpec(memory_space=pl.ANY)


def _place():
    return lax.axis_index("x"), lax.axis_index("y"), lax.axis_index("c")


def _other_chips(x, y):
    return [(1 - x, y), (x, 1 - y), (1 - x, 1 - y)]


def _allgather_semaphores(n):
    if n == 0:
        return []
    return [pltpu.SemaphoreType.DMA((3 * n,)), pltpu.SemaphoreType.DMA((3 * n,)), pltpu.SemaphoreType.DMA((n,))]


def _allgather_copies(ins, outs, send, recv, loc):
    n = len(ins)
    x, y, c = _place()
    mine = 2 * x + y
    peers = _other_chips(x, y)

    def remote(a, k, slot):
        return pltpu.make_async_remote_copy(
            src_ref=ins[a], dst_ref=outs[a].at[slot], send_sem=send.at[3 * a + k], recv_sem=recv.at[3 * a + k],
            device_id=(peers[k][0], peers[k][1], c), device_id_type=MESH)

    local = [pltpu.make_async_copy(ins[a], outs[a].at[mine], loc.at[a]) for a in range(n)]
    sends = [remote(a, k, mine) for a in range(n) for k in range(3)]
    recvs = [remote(a, k, 2 * peers[k][0] + peers[k][1]) for a in range(n) for k in range(3)]
    return local, sends, recvs


def _chip_allgather(arrays):
    n = len(arrays)

    def body(*refs):
        local, sends, recvs = _allgather_copies(refs[:n], refs[n:2 * n], *refs[2 * n:])
        for cp in local + sends:
            cp.start()
        for cp in recvs:
            cp.wait_recv()
        for cp in sends:
            cp.wait_send()
        for cp in local:
            cp.wait()

    return pl.pallas_call(
        body, name="ag_weights", in_specs=[_ANY] * n, out_specs=[_ANY] * n,
        out_shape=[jax.ShapeDtypeStruct((N_CHIPS,) + a.shape, a.dtype) for a in arrays],
        scratch_shapes=_allgather_semaphores(n),
    )(*arrays)


def _sibling_exchange(grads, small):
    n = len(grads)

    def body(*refs):
        ins, small_in = refs[:n], refs[n]
        outs, small_out = refs[n + 1:2 * n + 1], refs[2 * n + 1]
        send, recv, s_send, s_recv, loc = refs[2 * n + 2:]
        x, y, c = _place()
        me = 4 * x + 2 * y + c

        def half_copy(a, which):
            half = ins[a].shape[1] // 2
            return pltpu.make_async_remote_copy(
                src_ref=ins[a].at[pl.ds(0, N_CHIPS), pl.ds(which * half, half)], dst_ref=outs[a],
                send_sem=send.at[a], recv_sem=recv.at[a], device_id=(x, y, 1 - c), device_id_type=MESH)

        def peer_of(r):
            return tuple(1 - v if (r >> b) & 1 else v for v, b in ((x, 2), (y, 1), (c, 0)))

        def small_copy(r, slot):
            return pltpu.make_async_remote_copy(
                src_ref=small_in, dst_ref=small_out.at[slot], send_sem=s_send.at[r - 1], recv_sem=s_recv.at[r - 1],
                device_id=peer_of(r), device_id_type=MESH)

        local = pltpu.make_async_copy(small_in, small_out.at[me], loc.at[0])
        sends = [half_copy(a, 1 - c) for a in range(n)] + [small_copy(r, me) for r in range(1, N_DEV)]
        local.start()
        for cp in sends:
            cp.start()
        for r in range(1, N_DEV):
            px, py, pc = peer_of(r)
            small_copy(r, 4 * px + 2 * py + pc).wait_recv()
        for a in range(n):
            half_copy(a, c).wait_recv()
        for cp in sends:
            cp.wait_send()
        local.wait()

    rows = small.shape[0]
    return pl.pallas_call(
        body, name="rs_sibling", in_specs=[_ANY] * (n + 1), out_specs=[_ANY] * (n + 1),
        out_shape=[jax.ShapeDtypeStruct((N_CHIPS, g.shape[1] // 2, g.shape[2]), g.dtype) for g in grads]
        + [jax.ShapeDtypeStruct((N_DEV, rows, small.shape[1]), small.dtype)],
        scratch_shapes=[pltpu.SemaphoreType.DMA((n,)), pltpu.SemaphoreType.DMA((n,)),
                        pltpu.SemaphoreType.DMA((N_DEV - 1,)), pltpu.SemaphoreType.DMA((N_DEV - 1,)),
                        pltpu.SemaphoreType.DMA((1,))],
    )(*grads, small)


def _sibling_halves(grads):
    n = len(grads)

    def body(*refs):
        ins, outs = refs[:n], refs[n:2 * n]
        send, recv = refs[2 * n:]
        x, y, c = _place()

        def half_copy(a, which):
            half = ins[a].shape[1] // 2
            return pltpu.make_async_remote_copy(
                src_ref=ins[a].at[pl.ds(0, N_CHIPS), pl.ds(which * half, half)], dst_ref=outs[a],
                send_sem=send.at[a], recv_sem=recv.at[a], device_id=(x, y, 1 - c), device_id_type=MESH)

        sends = [half_copy(a, 1 - c) for a in range(n)]
        for cp in sends:
            cp.start()
        for a in range(n):
            half_copy(a, c).wait_recv()
        for cp in sends:
            cp.wait_send()

    return pl.pallas_call(
        body, name="rs_sibling_early", in_specs=[_ANY] * n, out_specs=[_ANY] * n,
        out_shape=[jax.ShapeDtypeStruct((N_CHIPS, g.shape[1] // 2, g.shape[2]), g.dtype) for g in grads],
        scratch_shapes=[pltpu.SemaphoreType.DMA((n,)), pltpu.SemaphoreType.DMA((n,))],
    )(*grads)


def _chip_reduce_scatter(parts):
    n = len(parts)

    def body(*refs):
        copies = _scatter_copies(refs[:n], refs[n:2 * n], *refs[2 * n:])
        for cp in copies:
            cp.start()
        for cp in copies:
            cp.wait_recv()
        for cp in copies:
            cp.wait_send()

    return pl.pallas_call(
        body, name="rs_chips", in_specs=[_ANY] * n, out_specs=[_ANY] * n,
        out_shape=_scatter_shapes(parts), scratch_shapes=_scatter_semaphores(n),
    )(*parts)


def _scatter_shapes(parts):
    return [jax.ShapeDtypeStruct((3,) + p.shape[1:], p.dtype) for p in parts]


def _scatter_semaphores(n):
    return [pltpu.SemaphoreType.DMA((3 * n,)), pltpu.SemaphoreType.DMA((3 * n,))] if n else []


def _scatter_copies(ins, outs, send, recv):
    x, y, c = _place()
    peers = _other_chips(x, y)
    return [pltpu.make_async_remote_copy(
        src_ref=ins[a].at[2 * peers[k][0] + peers[k][1]], dst_ref=outs[a].at[k], send_sem=send.at[3 * a + k],
        recv_sem=recv.at[3 * a + k], device_id=(peers[k][0], peers[k][1], c), device_id_type=MESH)
        for a in range(len(ins)) for k in range(3)]


def _sibling_allgather(bufs):
    n = len(bufs)

    def body(*refs):
        outs = refs[n:2 * n]
        send, recv = refs[2 * n:]
        x, y, c = _place()

        def remote(a, which):
            return pltpu.make_async_remote_copy(
                src_ref=outs[a].at[which], dst_ref=outs[a].at[which], send_sem=send.at[a], recv_sem=recv.at[a],
                device_id=(x, y, 1 - c), device_id_type=MESH)

        sends = [remote(a, c) for a in range(n)]
        for cp in sends:
            cp.start()
        for a in range(n):
            remote(a, 1 - c).wait_recv()
        for cp in sends:
            cp.wait_send()

    outs = pl.pallas_call(
        body, name="ag_sibling", in_specs=[_ANY] * n, out_specs=[_ANY] * n,
        out_shape=[jax.ShapeDtypeStruct(b.shape, b.dtype) for b in bufs],
        input_output_aliases={a: a for a in range(n)},
        scratch_shapes=[pltpu.SemaphoreType.DMA((n,)), pltpu.SemaphoreType.DMA((n,))],
    )(*bufs)
    return [o.reshape(2 * o.shape[1], o.shape[2]) for o in outs]


def _pair_add(full, recv, core, name):
    _, R, C = full.shape
    half = R // 2

    def body(core_ref, a_ref, b_ref, o_ref):
        o_ref[...] = (a_ref[...] + b_ref[...]).astype(BF16)

    return pl.pallas_call(
        body, name=name,
        grid_spec=pltpu.PrefetchScalarGridSpec(
            num_scalar_prefetch=1, grid=(N_CHIPS,),
            in_specs=[pl.BlockSpec((1, half, C), lambda j, core_ref: (j, core_ref[0], 0)),
                      pl.BlockSpec((1, half, C), lambda j, core_ref: (j, 0, 0))],
            out_specs=pl.BlockSpec((1, half, C), lambda j, core_ref: (j, 0, 0))),
        out_shape=jax.ShapeDtypeStruct((N_CHIPS, half, C), BF16),
        compiler_params=_params(("parallel",)),
    )(core, full, recv)


def _sum_slots(q, name, tiles=2):
    n, R, C = q.shape
    tr = R // tiles

    def body(q_ref, o_ref):
        acc = q_ref[0].astype(F32)
        for j in range(1, n):
            acc = acc + q_ref[j].astype(F32)
        o_ref[...] = acc

    return pl.pallas_call(
        body, name=name, grid=(tiles,),
        in_specs=[pl.BlockSpec((n, tr, C), lambda i: (0, i, 0))],
        out_specs=pl.BlockSpec((tr, C), lambda i: (i, 0)),
        out_shape=jax.ShapeDtypeStruct((R, C), F32),
        compiler_params=_params(("parallel",)),
    )(q)


def _sum_partials(own_all, recv, place, name, tiles=2):
    _, R, C = own_all.shape
    tr = R // tiles

    def body(place_ref, own_ref, r_ref, o_ref):
        acc = own_ref[0].astype(F32)
        for k in range(3):
            acc = acc + r_ref[k].astype(F32)
        o_ref[0] = acc

    return pl.pallas_call(
        body, name=name,
        grid_spec=pltpu.PrefetchScalarGridSpec(
            num_scalar_prefetch=1, grid=(tiles,),
            in_specs=[pl.BlockSpec((1, tr, C), lambda i, place_ref: (place_ref[0], i, 0)),
                      pl.BlockSpec((3, tr, C), lambda i, place_ref: (0, i, 0))],
            out_specs=pl.BlockSpec((1, tr, C), lambda i, place_ref: (place_ref[1], i, 0))),
        out_shape=jax.ShapeDtypeStruct((2, R, C), F32),
        compiler_params=_params(("parallel",)),
    )(place, own_all, recv)


def _adamw(w, g, m, v, name, tiles=4):
    R, C = w.shape
    tr = R // tiles

    def body(w_ref, g_ref, m_ref, v_ref, d_ref, m2_ref, v2_ref):
        g_ = g_ref[...]
        m2 = ADAM_B1 * m_ref[...] + (1.0 - ADAM_B1) * g_
        v2 = ADAM_B2 * v_ref[...] + (1.0 - ADAM_B2) * (g_ * g_)
        m_hat = m2 / (1.0 - ADAM_B1 ** ADAM_STEP)
        v_hat = v2 / (1.0 - ADAM_B2 ** ADAM_STEP)
        d_ref[...] = -ADAM_LR * (m_hat / (jnp.sqrt(v_hat) + ADAM_EPS) + ADAM_WD * w_ref[...])
        m2_ref[...] = m2
        v2_ref[...] = v2

    spec = pl.BlockSpec((tr, C), lambda i: (i, 0))
    return pl.pallas_call(
        body, name=name, grid=(tiles,), in_specs=[spec] * 4, out_specs=[spec] * 3,
        out_shape=[jax.ShapeDtypeStruct((R, C), F32)] * 3,
        compiler_params=_params(("parallel",)),
    )(w, g, m, v)


def _pack_rows(pieces, rows):
    flat = jnp.concatenate([jnp.pad(p.reshape(-1).astype(F32), (0, (-p.size) % D_MODEL)) for p in pieces])
    return jnp.pad(flat, (0, rows * D_MODEL - flat.size)).reshape(rows, D_MODEL)


def _unpack_rows(pack, shapes):
    flat = pack.reshape(-1)
    out, off = [], 0
    for shp in shapes:
        size = int(np.prod(shp))
        out.append(flat[off:off + size].reshape(shp))
        off += size + (-size) % D_MODEL
    return out


def _kernel_order(w):
    parts = [w[:, 0:RET_W]]
    for p in range(FOX_HEADS // 2):
        parts += [w[:, RET_W + part * 512 + p * BLK:RET_W + part * 512 + (p + 1) * BLK] for part in range(3)]
    return jnp.concatenate(parts, axis=1)


def _reference_order(g_main, g_ff):
    parts = [g_main[:, 0:RET_W]]
    for part in range(3):
        parts += [g_main[:, RET_W + 384 * p + part * BLK:RET_W + 384 * p + (part + 1) * BLK] for p in range(FOX_HEADS // 2)]
    return jnp.concatenate(parts + [g_ff], axis=1)


def kernel(x, meta_tokens, attn_norm_g, w_in, fox_forget_b, ret_norm_g, w_out, ffn_norm_g, w_up, conv_w, conv_b, w_down, final_norm_g, loss_target, m_meta_tokens, m_attn_norm_g, m_w_in, m_fox_forget_b, m_ret_norm_g, m_w_out, m_ffn_norm_g, m_w_up, m_conv_w, m_conv_b, m_w_down, m_final_norm_g, v_meta_tokens, v_attn_norm_g, v_w_in, v_fox_forget_b, v_ret_norm_g, v_w_out, v_ffn_norm_g, v_w_up, v_conv_w, v_conv_b, v_w_down, v_final_norm_g):
    chip = 2 * lax.axis_index("x") + lax.axis_index("y")
    core = lax.axis_index("c")
    meta_w, conv_sw = meta_tokens.shape[1], conv_w.shape[2]

    small_w = _pack_rows([meta_tokens, conv_w[0]], 8)
    g_in, g_small = _chip_allgather([w_in[0].astype(BF16), small_w])
    w_in_full = g_in.transpose(1, 0, 2).reshape(D_MODEL, IN_WIDTH)
    w_main = _kernel_order(w_in_full)
    w_ff = jnp.pad(w_in_full[:, MAIN_W:], ((0, 0), (0, BLK - FOX_HEADS)))
    small_parts = [_unpack_rows(g_small[j], [meta_tokens.shape, conv_w.shape[1:]]) for j in range(N_CHIPS)]
    meta_full = jnp.concatenate([sp[0] for sp in small_parts], axis=1)
    conv_w_full = jnp.concatenate([sp[1] for sp in small_parts], axis=1)

    core_idx = core.reshape(1).astype(jnp.int32)
    place = jnp.stack([chip, core]).astype(jnp.int32)

    def assemble(gathered):
        g_out, g_up, g_down = gathered
        return g_out.reshape(D_MODEL, D_MODEL), g_up, g_down.reshape(D_FF, D_MODEL)

    def early_reduce(d_w_out, d_w_up, d_w_down):
        early = [d_w_out.reshape(N_CHIPS, -1, D_MODEL), d_w_up, d_w_down.reshape(N_CHIPS, -1, D_MODEL)]
        from_sib = _sibling_halves(early)
        return [_pair_add(g, r, core_idx, "pair_add_" + nm) for g, r, nm in zip(early, from_sib, ("out", "up", "down"))]

    out = _local_step(x[0], loss_target[0], meta_full, attn_norm_g, w_main, w_ff, fox_forget_b, ret_norm_g,
                      None, ffn_norm_g, None, conv_w_full, conv_b, None, final_norm_g[None],
                      late=([w_out[0].astype(BF16), w_up[0].astype(BF16), w_down[0].astype(BF16)], assemble),
                      mid=early_reduce)

    g_in_full = _reference_order(out["w_main"], out["w_ff"]).reshape(D_MODEL, N_CHIPS, -1).transpose(1, 0, 2)
    small_shapes = [(1, D_MODEL), (1, D_MODEL), (1, D_MODEL), (1, 512 + FOX_HEADS + 1), (1, D_FF), (N_META, D_MODEL), (3, D_FF)]
    small = _pack_rows([out["attn_g"], out["ffn_g"], out["final_g"],
                        jnp.concatenate([out["ret_g"], out["fox_b"], out["loss"].reshape(1, 1)], axis=1),
                        out["conv_b"], out["dmeta"], out["conv_w"]], 32)
    from_sibling_in, small_all = _sibling_exchange([g_in_full], small)
    sum_in = _pair_add(g_in_full, from_sibling_in, core_idx, "pair_add_in")
    (from_chips_in,) = _chip_reduce_scatter([sum_in])
    chip_sums = [sum_in] + list(out["scatter"])
    from_chips = [from_chips_in] + list(out["received"])
    names = ("in", "out", "up", "down")
    totals = [_sum_partials(s, q, place, "sum_chips_" + nm) for s, q, nm in zip(chip_sums, from_chips, names)]
    grad_in, grad_out, grad_up, grad_down = _sibling_allgather(totals)
    s_attn, s_ffn, s_final, s_misc, s_conv_b, s_meta, s_conv_w = _unpack_rows(
        _sum_slots(small_all, "sum_small", tiles=1), small_shapes)
    loss = s_misc[0, 512 + FOX_HEADS]
    small_grads = [lax.dynamic_slice_in_dim(s_meta, chip * meta_w, meta_w, axis=1), s_attn, s_misc[:, 512:512 + FOX_HEADS],
                   s_misc[:, :512], s_ffn, lax.dynamic_slice_in_dim(s_conv_w, chip * conv_sw, conv_sw, axis=1)[None],
                   s_conv_b, s_final[0]]

    big_w = [(w_in, m_w_in, v_w_in, grad_in, "adamw_in"), (w_out, m_w_out, v_w_out, grad_out, "adamw_out"),
             (w_up, m_w_up, v_w_up, grad_up, "adamw_up"), (w_down, m_w_down, v_w_down, grad_down, "adamw_down")]
    big_res = [[g[None]] + [r[None] for r in _adamw(w[0], g, m[0], v[0], nm)] for w, m, v, g, nm in big_w]
    small_w_list = [meta_tokens, attn_norm_g, fox_forget_b, ret_norm_g, ffn_norm_g, conv_w, conv_b, final_norm_g]
    small_m = [m_meta_tokens, m_attn_norm_g, m_fox_forget_b, m_ret_norm_g, m_ffn_norm_g, m_conv_w, m_conv_b, m_final_norm_g]
    small_v = [v_meta_tokens, v_attn_norm_g, v_fox_forget_b, v_ret_norm_g, v_ffn_norm_g, v_conv_w, v_conv_b, v_final_norm_g]
    shapes = [a.shape for a in small_w_list]
    packs = [_pack_rows(lst, 16) for lst in (small_w_list, small_grads, small_m, small_v)]
    small_res = [_unpack_rows(r, shapes) for r in _adamw(*packs, "adamw_small", tiles=1)]
    small_grads = [g.reshape(s) for g, s in zip(small_grads, shapes)]

    def ordered(kind):
        sm = small_grads if kind == 0 else small_res[kind - 1]
        bg = [r[kind] for r in big_res]
        return [sm[0], sm[1], bg[0], sm[2], sm[3], bg[1], sm[4], bg[2], sm[5], sm[6], bg[3], sm[7]]

    return (loss, out["dx"][None], *ordered(0), *ordered(1), *ordered(2), *ordered(3))
```

```python
import functools

import numpy as np
import jax
import jax.numpy as jnp
from jax import lax
from jax.experimental import pallas as pl
from jax.experimental.pallas import tpu as pltpu

F32 = jnp.float32
BF16 = jnp.bfloat16

D_MODEL = 1024
N_META = 16
BLK = 128
UNIT = 2 * BLK
WIDE = 4
CHUNK = 64
N_PAD = BLK - N_META
PREFIX = BLK
RET_HEADS = 4
FOX_HEADS = 8
HEAD_LANES = 64
D_FF = 2816
ROPE_BASE = 10000.0
EPS = 1e-6
NEG = -1e30
RET_W = 1536
FOX_W = 1536
MAIN_W = RET_W + FOX_W
IN_WIDTH = MAIN_W + FOX_HEADS
N_CHIPS = 4
N_DEV = 8

ADAM_LR = 0.001
ADAM_B1 = 0.9
ADAM_B2 = 0.999
ADAM_EPS = 1e-08
ADAM_WD = 0.01
ADAM_STEP = 10

MESH = pl.DeviceIdType.MESH
VMEM_LIMIT_MB = 56

_NT = (((1,), (1,)), ((), ()))
_TN = (((0,), (0,)), ((), ()))


def _dot(a, b):
    return jnp.dot(a, b, preferred_element_type=F32)


def _dot_nt(a, b):
    return lax.dot_general(a, b, _NT, preferred_element_type=F32)


def _dot_tn(a, b):
    return lax.dot_general(a, b, _TN, preferred_element_type=F32)


def _params(dims=None, vmem_mb=VMEM_LIMIT_MB):
    kw = dict(vmem_limit_bytes=vmem_mb << 20)
    if dims is not None:
        kw["dimension_semantics"] = dims
    return pltpu.CompilerParams(**kw)


def _row_tile(n, prefs=(384, 256, 128)):
    for t in prefs:
        if n % t == 0:
            return t
    raise ValueError(f"no row tile for {n}")


def _iota(shape, dim):
    return lax.broadcasted_iota(jnp.int32, shape, dim)


def _pick_row(tile, row):
    sub = _iota(tile.shape, 0)
    return jnp.sum(jnp.where(sub == row, tile, 0.0), axis=0, keepdims=True)


def _split3(x):
    hi = x.astype(BF16)
    r1 = x - hi.astype(F32)
    mid = r1.astype(BF16)
    lo = (r1 - mid.astype(F32)).astype(BF16)
    return hi, mid, lo


def _full(shape):
    nd = len(shape)
    return pl.BlockSpec(shape, lambda *_: (0,) * nd)


def _in_perm():
    cols = list(range(RET_W))
    for p in range(FOX_HEADS // 2):
        for part in range(3):
            start = RET_W + part * 512 + p * BLK
            cols += list(range(start, start + BLK))
    return np.asarray(cols, np.int32)


def _rotary_tables(L):
    half = HEAD_LANES // 2
    inv = 1.0 / (ROPE_BASE ** (jnp.arange(half, dtype=F32) / half))
    ang = jnp.arange(L).astype(F32)[:, None] * inv[None, :]
    cos, sin = jnp.cos(ang), jnp.sin(ang)
    cos_t = jnp.tile(cos, (1, 4))
    sin_t = jnp.tile(jnp.concatenate([-sin, sin], axis=1), (1, 2))
    return cos_t, sin_t


def _decay_tables():
    gam = 1.0 - 2.0 ** (-5.0 - np.arange(RET_HEADS, dtype=np.float64))
    n = np.arange(BLK)
    same_or_past = (n[:, None] // CHUNK) >= (n[None, :] // CHUNK)
    dist = np.abs(n[:, None] - n[None, :])
    dmat = np.stack([np.where(same_or_past, g ** dist, 0.0) for g in gam]).astype(np.float32)
    lane_head = np.arange(BLK) // HEAD_LANES
    wq = np.stack([gam[2 * p + lane_head][None, :] ** (n[:, None] + 1.0) for p in range(2)]).astype(np.float32)
    wk = np.stack([gam[2 * p + lane_head][None, :] ** (BLK - 1.0 - n[:, None]) for p in range(2)]).astype(np.float32)
    g_blk = tuple(float(g ** BLK) for g in gam)
    return jnp.asarray(dmat), jnp.asarray(wq), jnp.asarray(wk), g_blk


def _rms_inproj(h0, g, w_main, w_ff):
    L = h0.shape[0]
    tm = _row_tile(L)

    def body(h_ref, g_ref, wm_ref, wf_ref, n_ref, p_ref, ff_ref):
        h = h_ref[...]
        r = lax.rsqrt(jnp.mean(h * h, axis=-1, keepdims=True) + EPS)
        n = (h * r * g_ref[...]).astype(BF16)
        n_ref[...] = n
        p_ref[...] = _dot(n, wm_ref[...]).astype(BF16)
        ff_ref[...] = _dot(n, wf_ref[...])

    return pl.pallas_call(
        body, name="f_inproj", grid=(L // tm,),
        in_specs=[pl.BlockSpec((tm, D_MODEL), lambda i: (i, 0)), _full((1, D_MODEL)),
                  _full((D_MODEL, MAIN_W)), _full((D_MODEL, BLK))],
        out_specs=[pl.BlockSpec((tm, D_MODEL), lambda i: (i, 0)), pl.BlockSpec((tm, MAIN_W), lambda i: (i, 0)),
                   pl.BlockSpec((tm, BLK), lambda i: (i, 0))],
        out_shape=[jax.ShapeDtypeStruct((L, D_MODEL), BF16), jax.ShapeDtypeStruct((L, MAIN_W), BF16),
                   jax.ShapeDtypeStruct((L, BLK), F32)],
        compiler_params=_params(("parallel",)),
    )(h0, g, w_main, w_ff)


def _fox_prep(ff, fb):
    L = ff.shape[0]
    nblk = L // BLK

    def body(ff_ref, b_ref, c_ref, ct_ref, carry):
        i = pl.program_id(0)

        @pl.when(i == 0)
        def _():
            carry[...] = jnp.zeros_like(carry)

        z = ff_ref[...] + b_ref[...]
        lf = jnp.minimum(z, 0.0) - jnp.log1p(jnp.exp(-jnp.abs(z)))
        lf = jnp.where(_iota((BLK, BLK), 1) < FOX_HEADS, lf, 0.0)
        tri = (_iota((BLK, BLK), 0) >= _iota((BLK, BLK), 1)).astype(BF16)
        hi, mid, lo = _split3(lf)
        cs = _dot(tri, hi) + _dot(tri, mid) + _dot(tri, lo) + carry[...]
        c_ref[...] = cs
        ct_ref[0] = cs.T[0:8, :]
        carry[...] = carry[...] + jnp.sum(lf, axis=0, keepdims=True)

    return pl.pallas_call(
        body, name="f_foxprep", grid=(nblk,),
        in_specs=[pl.BlockSpec((BLK, BLK), lambda i: (i, 0)), _full((1, BLK))],
        out_specs=[pl.BlockSpec((BLK, BLK), lambda i: (i, 0)), pl.BlockSpec((1, 8, BLK), lambda i: (i, 0, 0))],
        out_shape=[jax.ShapeDtypeStruct((L, BLK), F32), jax.ShapeDtypeStruct((nblk, 8, BLK), F32)],
        scratch_shapes=[pltpu.VMEM((1, BLK), F32)],
        compiler_params=_params(("arbitrary",)),
    )(ff, fb)


def _rot_fns(cos, sin):
    lane = _iota((BLK, BLK), 1)
    first = (lane & (HEAD_LANES - 1)) < HEAD_LANES // 2

    def swap(x):
        return jnp.where(first, pltpu.roll(x, BLK - 32, 1), pltpu.roll(x, 32, 1))

    def rot(x):
        return x * cos + swap(x) * sin

    def rot_t(dy):
        return dy * cos + swap(dy * sin)

    return rot, rot_t


def _retention_fwd(proj, cos_t, sin_t, ret_g):
    L = proj.shape[0]
    nblk = L // BLK
    dmat, wq_t, wk_t, g_blk = _decay_tables()

    def body(q_ref, k_ref, v_ref, gate_ref, cos_ref, sin_ref, d_ref, wq_ref, wk_ref, rg_ref,
             mix_ref, o_ref, rs_ref, state):
        i = pl.program_id(0)

        @pl.when(i == 0)
        def _():
            state[...] = jnp.zeros_like(state)

        rot, _ = _rot_fns(cos_ref[...], sin_ref[...])
        lane = _iota((BLK, BLK), 1)
        sub = _iota((BLK, BLK), 0)
        for p in range(2):
            qr = rot(q_ref[:, p * BLK:(p + 1) * BLK].astype(F32))
            kr = rot(k_ref[:, p * BLK:(p + 1) * BLK].astype(F32)) * (HEAD_LANES ** -0.5)
            kr_b = kr.astype(BF16)
            qw = (qr * wq_ref[p]).astype(BF16)
            kw = (kr * wk_ref[p]).astype(BF16)
            for e in range(2):
                h = 2 * p + e
                cols = slice(h * BLK, (h + 1) * BLK)
                qm = jnp.where((lane >> 6) == e, qr, 0.0).astype(BF16)
                s = _dot_nt(qm, kr_b) * d_ref[h]
                vh = v_ref[:, cols]
                st = state[h]
                rs_ref[0, h] = st
                o = _dot(s.astype(BF16), vh) + _dot(qw, st.astype(BF16))
                u = jnp.where((sub >> 6) == e, _dot_tn(kw, vh), 0.0)
                state[h] = g_blk[h] * st + u
                rn = lax.rsqrt(jnp.mean(o * o, axis=-1, keepdims=True) + EPS)
                gate = gate_ref[:, cols].astype(F32)
                o_ref[:, cols] = o
                mix_ref[:, cols] = (o * rn * rg_ref[:, cols] * (gate * jax.nn.sigmoid(gate))).astype(BF16)

    row = lambda c: (lambda i: (i, c))
    return pl.pallas_call(
        body, name="f_retention", grid=(nblk,),
        in_specs=[pl.BlockSpec((BLK, 256), row(0)), pl.BlockSpec((BLK, 256), row(1)),
                  pl.BlockSpec((BLK, 512), row(1)), pl.BlockSpec((BLK, 512), row(2)),
                  pl.BlockSpec((BLK, BLK), row(0)), pl.BlockSpec((BLK, BLK), row(0)),
                  _full((RET_HEADS, BLK, BLK)), _full((2, BLK, BLK)), _full((2, BLK, BLK)), _full((1, 512))],
        out_specs=[pl.BlockSpec((BLK, 512), row(0)), pl.BlockSpec((BLK, 512), row(0)),
                   pl.BlockSpec((1, RET_HEADS, BLK, BLK), lambda i: (i, 0, 0, 0))],
        out_shape=[jax.ShapeDtypeStruct((L, 512), BF16), jax.ShapeDtypeStruct((L, 512), F32),
                   jax.ShapeDtypeStruct((nblk, RET_HEADS, BLK, BLK), F32)],
        scratch_shapes=[pltpu.VMEM((RET_HEADS, BLK, BLK), F32)],
        compiler_params=_params(("arbitrary",)),
    )(proj, proj, proj, proj, cos_t, sin_t, dmat, wq_t, wk_t, ret_g)


def _fox_units(L):
    nblk = L // BLK
    assert L % BLK == 0 and nblk % 2 == 1, "sequence must be one 128-row block plus whole 256-row tiles"
    return nblk, (nblk - 1) // 2


def _fox_tile_masks():
    sub, lane = _iota((BLK, BLK), 0), _iota((BLK, BLK), 1)
    return dict(first=(sub <= lane) & (sub >= N_PAD), valid=_iota((BLK, UNIT), 0) >= N_PAD,
                diag=_iota((UNIT, UNIT), 0) <= _iota((UNIT, UNIT), 1))


def _fox_fwd(proj, c, ctb, gather=()):
    L = proj.shape[0]
    nblk, nu = _fox_units(L)
    scale = HEAD_LANES ** -0.5
    ng = len(gather)

    def body(qkv_ref, c_ref, ct_ref, *rest):
        g_in, (of_ref, lse_ref), g_out = rest[:ng], rest[ng:ng + 2], rest[ng + 2:2 * ng + 2]
        vt, csb = rest[2 * ng + 2:2 * ng + 4]
        p = pl.program_id(0)

        @pl.when(p == 0)
        def _():
            lse_ref[...] = jnp.zeros_like(lse_ref)
            if ng:
                local, sends, _ = _allgather_copies(g_in, g_out, *rest[2 * ng + 4:])
                for cp in local + sends:
                    cp.start()

        lane = _iota((BLK, BLK), 1)
        sub8 = _iota((8, BLK), 0)
        masks = _fox_tile_masks()

        def pre(j, carry):
            off = pl.multiple_of(j * BLK, BLK)
            vt[j] = qkv_ref[pl.ds(off, BLK), 2 * BLK:3 * BLK].astype(F32).T.astype(BF16)
            ct = c_ref[pl.ds(off, BLK), :]
            for e in range(2):
                col = jnp.sum(jnp.where(lane == 2 * p + e, ct, 0.0), axis=1, keepdims=True)
                csb[e, j] = jnp.broadcast_to(col, (BLK, UNIT))
            return carry

        lax.fori_loop(0, nblk, pre, 0)

        def attend(qblk, nq, n_whole):
            qlen = nq * BLK
            qoff = pl.multiple_of(qblk * BLK, BLK)
            qs = qkv_ref[pl.ds(qoff, qlen), 0:BLK].astype(F32) * scale
            qlane = _iota((qlen, BLK), 1)
            qm = [jnp.where((qlane >> 6) == e, qs, 0.0).astype(BF16) for e in range(2)]
            ct_row = [jnp.concatenate([_pick_row(ct_ref[qblk + a], 2 * p + e) for a in range(nq)], axis=1)
                      for e in range(2)]

            def step(kblk, nk, mask, st):
                koff = pl.multiple_of(kblk * BLK, BLK)
                kt = qkv_ref[pl.ds(koff, nk * BLK), BLK:2 * BLK]
                out = []
                for e in range(2):
                    m, l, acc = st[3 * e:3 * e + 3]
                    s = _dot_nt(kt, qm[e])
                    t = jnp.concatenate([s[b * BLK:(b + 1) * BLK] - csb[e, kblk + b, :, 0:qlen] for b in range(nk)], axis=0)
                    if mask is not None:
                        t = jnp.where(mask, t, NEG)
                    m_new = jnp.maximum(m, jnp.max(t, axis=0, keepdims=True) + ct_row[e])
                    alpha = jnp.exp(m - m_new)
                    pr = jnp.exp(t - (m_new - ct_row[e]))
                    l = alpha * l + jnp.sum(pr, axis=0, keepdims=True)
                    pr_b = pr.astype(BF16)
                    pv = _dot(vt[kblk, e * HEAD_LANES:(e + 1) * HEAD_LANES, :], pr_b[0:BLK])
                    for b in range(1, nk):
                        pv = pv + _dot(vt[kblk + b, e * HEAD_LANES:(e + 1) * HEAD_LANES, :], pr_b[b * BLK:(b + 1) * BLK])
                    out += [m_new, l, alpha * acc + pv]
                return tuple(out)

            st = (jnp.full((1, qlen), NEG, F32), jnp.zeros((1, qlen), F32), jnp.zeros((HEAD_LANES, qlen), F32)) * 2
            if nq == 1:
                st = step(0, 1, masks["first"], st)
            else:
                st = step(0, 1, masks["valid"], st)
                n_wide = n_whole // WIDE
                st = lax.fori_loop(0, n_wide, lambda j, s_: step(1 + 2 * WIDE * j, 2 * WIDE, None, s_), st)
                rest = 1 + 2 * WIDE * n_wide
                st = lax.cond((n_whole & 2) != 0, lambda s_: step(rest, 4, None, s_), lambda s_: s_, st)
                st = lax.cond((n_whole & 1) != 0, lambda s_: step(rest + 2 * (n_whole & 2), 2, None, s_),
                              lambda s_: s_, st)
                st = step(qblk, 2, masks["diag"], st)
            o_t = jnp.concatenate([st[2] * (1.0 / st[1]), st[5] * (1.0 / st[4])], axis=0)
            of_ref[pl.ds(qoff, qlen), :] = o_t.T.astype(BF16)
            lse = [st[3 * e] + jnp.log(st[3 * e + 1]) for e in range(2)]
            for a in range(nq):
                rows = [lse[e][:, a * BLK:(a + 1) * BLK] for e in range(2)]
                lse_ref[qblk + a] = lse_ref[qblk + a] + (
                    jnp.where(sub8 == 2 * p, rows[0], 0.0) + jnp.where(sub8 == 2 * p + 1, rows[1], 0.0))

        attend(0, 1, 0)

        def q_loop(u, carry):
            attend(1 + 2 * u, 2, u)
            return carry

        lax.fori_loop(0, nu, q_loop, 0)

        if ng:
            @pl.when(p == FOX_HEADS // 2 - 1)
            def _():
                local, sends, recvs = _allgather_copies(g_in, g_out, *rest[2 * ng + 4:])
                for cp in recvs:
                    cp.wait_recv()
                for cp in sends:
                    cp.wait_send()
                for cp in local:
                    cp.wait()

    return pl.pallas_call(
        body, name="f_fox", grid=(FOX_HEADS // 2,),
        in_specs=[pl.BlockSpec((L, 384), lambda p: (0, RET_W // 384 + p)), _full((L, BLK)), _full((nblk, 8, BLK))]
        + [_ANY] * ng,
        out_specs=[pl.BlockSpec((L, BLK), lambda p: (0, p)), _full((nblk, 8, BLK))] + [_ANY] * ng,
        out_shape=[jax.ShapeDtypeStruct((L, 512), BF16), jax.ShapeDtypeStruct((nblk, 8, BLK), F32)]
        + [jax.ShapeDtypeStruct((N_CHIPS,) + a.shape, a.dtype) for a in gather],
        scratch_shapes=[pltpu.VMEM((nblk, BLK, BLK), BF16), pltpu.VMEM((2, nblk, BLK, UNIT), F32)]
        + _allgather_semaphores(ng),
        compiler_params=_params(("arbitrary",)),
    )(proj, c, ctb, *gather)


def _outproj_up(mix_r, o_f, h0, w_out, ffn_g, w_up):
    L = h0.shape[0]
    tm = _row_tile(L)
    shard = w_up.shape[2]

    def body(mr_ref, of_ref, h0_ref, wo_ref, g_ref, wu_ref, h1_ref, n2_ref, up_ref):
        h1 = h0_ref[...] + _dot(mr_ref[...], wo_ref[0:512, :]) + _dot(of_ref[...], wo_ref[512:1024, :])
        h1_ref[...] = h1
        r = lax.rsqrt(jnp.mean(h1 * h1, axis=-1, keepdims=True) + EPS)
        n2 = (h1 * r * g_ref[...]).astype(BF16)
        n2_ref[...] = n2
        for j in range(N_CHIPS):
            up_ref[:, j * shard:(j + 1) * shard] = _dot(n2, wu_ref[j]).astype(BF16)

    rows = lambda w: pl.BlockSpec((tm, w), lambda i: (i, 0))
    return pl.pallas_call(
        body, name="f_outproj_up", grid=(L // tm,),
        in_specs=[rows(512), rows(512), rows(D_MODEL), _full((D_MODEL, D_MODEL)), _full((1, D_MODEL)),
                  _full((N_CHIPS, D_MODEL, shard))],
        out_specs=[rows(D_MODEL), rows(D_MODEL), rows(2 * D_FF)],
        out_shape=[jax.ShapeDtypeStruct((L, D_MODEL), F32), jax.ShapeDtypeStruct((L, D_MODEL), BF16),
                   jax.ShapeDtypeStruct((L, 2 * D_FF), BF16)],
        compiler_params=_params(("parallel",)),
    )(mix_r, o_f, h0, w_out, ffn_g, w_up)


def _conv_acc(a_ref, halo_ref, cw_refs, cb_ref, i, tm):
    sub = _iota((tm, 1), 0)
    a = jnp.where(i * tm + sub >= N_PAD, a_ref[...].astype(F32), 0.0)
    halo = halo_ref[...].astype(F32)
    hrow = i * tm - 8 + _iota((8, 1), 0)
    halo = jnp.where((hrow >= N_PAD) & (i > 0), halo, 0.0)
    a1 = jnp.where(sub == 0, _pick_row(halo, 7), pltpu.roll(a, 1, 0))
    a2 = jnp.where(sub == 0, _pick_row(halo, 6), jnp.where(sub == 1, _pick_row(halo, 7), pltpu.roll(a, 2, 0)))
    acc = cb_ref[...] + a2 * cw_refs[0][...]
    acc = acc + a1 * cw_refs[1][...]
    acc = acc + a * cw_refs[2][...]
    return a, a1, a2, acc


def _ffn_down_loss(up, conv_w, conv_b, w_down, h1, final_g, target):
    L = h1.shape[0]
    tm = _row_tile(L)
    cw = [conv_w[j:j + 1] for j in range(3)]

    def body(a_ref, halo_ref, b_ref, cw0, cw1, cw2, cb_ref, wd_ref, h1_ref, gf_ref, t_ref,
             g_ref, dh_ref, dhb_ref, dgf_ref, loss_ref):
        i = pl.program_id(0)

        @pl.when(i == 0)
        def _():
            dgf_ref[...] = jnp.zeros_like(dgf_ref)
            loss_ref[...] = jnp.zeros_like(loss_ref)

        _, _, _, acc = _conv_acc(a_ref, halo_ref, (cw0, cw1, cw2), cb_ref, i, tm)
        g = (acc * jax.nn.sigmoid(acc) * b_ref[...].astype(F32)).astype(BF16)
        g_ref[...] = g
        h2 = h1_ref[...] + _dot(g, wd_ref[...])
        r = lax.rsqrt(jnp.mean(h2 * h2, axis=-1, keepdims=True) + EPS)
        yn = h2 * r
        gf = gf_ref[...]
        live = i * tm + _iota((tm, 1), 0) >= PREFIX
        err = jnp.where(live, yn * gf - t_ref[...], 0.0)
        loss_ref[...] = loss_ref[...] + 0.5 * jnp.sum(jnp.mean(err * err, axis=-1, keepdims=True))
        dy = err * (1.0 / D_MODEL)
        dgf_ref[...] = dgf_ref[...] + jnp.sum(dy * yn, axis=0, keepdims=True)
        dyn = dy * gf
        dh = r * (dyn - yn * jnp.mean(dyn * yn, axis=-1, keepdims=True))
        dh_ref[...] = dh
        dhb_ref[...] = dh.astype(BF16)

    rows = lambda w, c=0: pl.BlockSpec((tm, w), lambda i: (i, c))
    halo = pl.BlockSpec((8, D_FF), lambda i: (jnp.maximum(i * (tm // 8) - 1, 0), 0))
    return pl.pallas_call(
        body, name="f_ffn_down_loss", grid=(L // tm,),
        in_specs=[rows(D_FF), halo, rows(D_FF, 1), _full((1, D_FF)), _full((1, D_FF)), _full((1, D_FF)),
                  _full((1, D_FF)), _full((D_FF, D_MODEL)), rows(D_MODEL), _full((1, D_MODEL)), rows(D_MODEL)],
        out_specs=[rows(D_FF), rows(D_MODEL), rows(D_MODEL), _full((1, D_MODEL)), _full((1, BLK))],
        out_shape=[jax.ShapeDtypeStruct((L, D_FF), BF16), jax.ShapeDtypeStruct((L, D_MODEL), F32),
                   jax.ShapeDtypeStruct((L, D_MODEL), BF16), jax.ShapeDtypeStruct((1, D_MODEL), F32),
                   jax.ShapeDtypeStruct((1, BLK), F32)],
        compiler_params=_params(("arbitrary",)),
    )(up, up, up, cw[0], cw[1], cw[2], conv_b, w_down, h1, final_g, target)


def _ffn_bwd_gate(dh2b, w_down, up, conv_w, conv_b):
    L = dh2b.shape[0]
    tm = _row_tile(L)
    cw = [conv_w[j:j + 1] for j in range(3)]

    def body(dh_ref, wd_ref, a_ref, halo_ref, b_ref, cw0, cw1, cw2, cb_ref, dacc_ref, db_ref, dcw_ref):
        i = pl.program_id(0)

        @pl.when(i == 0)
        def _():
            dcw_ref[...] = jnp.zeros_like(dcw_ref)

        a, a1, a2, acc = _conv_acc(a_ref, halo_ref, (cw0, cw1, cw2), cb_ref, i, tm)
        dg = _dot_nt(dh_ref[...], wd_ref[...])
        sg = jax.nn.sigmoid(acc)
        db_ref[...] = (dg * acc * sg).astype(BF16)
        dacc = dg * b_ref[...].astype(F32) * (sg * (1.0 + acc * (1.0 - sg)))
        dacc_ref[...] = dacc.astype(BF16)
        sub8 = _iota((8, 1), 0)
        rows = [jnp.sum(dacc * t, axis=0, keepdims=True) for t in (a2, a1, a)] + [jnp.sum(dacc, axis=0, keepdims=True)]
        upd = jnp.zeros((8, D_FF), F32)
        for j, rj in enumerate(rows):
            upd = upd + jnp.where(sub8 == j, rj, 0.0)
        dcw_ref[...] = dcw_ref[...] + upd

    rows = lambda w, c=0: pl.BlockSpec((tm, w), lambda i: (i, c))
    halo = pl.BlockSpec((8, D_FF), lambda i: (jnp.maximum(i * (tm // 8) - 1, 0), 0))
    return pl.pallas_call(
        body, name="b_ffn_gate", grid=(L // tm,),
        in_specs=[rows(D_MODEL), _full((D_FF, D_MODEL)), rows(D_FF), halo, rows(D_FF, 1),
                  _full((1, D_FF)), _full((1, D_FF)), _full((1, D_FF)), _full((1, D_FF))],
        out_specs=[rows(D_FF), rows(D_FF), _full((8, D_FF))],
        out_shape=[jax.ShapeDtypeStruct((L, D_FF), BF16), jax.ShapeDtypeStruct((L, D_FF), BF16),
                   jax.ShapeDtypeStruct((8, D_FF), F32)],
        compiler_params=_params(("arbitrary",)),
    )(dh2b, w_down, up, up, up, cw[0], cw[1], cw[2], conv_b)


def _ffn_bwd_up(dacc, db, conv_w, w_up, h1, ffn_g, dh2, w_out):
    L = h1.shape[0]
    tm = _row_tile(L)
    nt = L // tm
    shard = w_up.shape[2]
    cw = [conv_w[j:j + 1] for j in range(3)]

    def body(da_ref, halo_ref, db_ref, cw0, cw1, cw2, wu_ref, h1_ref, g_ref, dh2_ref, wo_ref,
             dup_ref, dh1_ref, dh1b_ref, dmix_ref, dg_ref):
        i = pl.program_id(0)

        @pl.when(i == 0)
        def _():
            dg_ref[...] = jnp.zeros_like(dg_ref)

        sub = _iota((tm, 1), 0)
        d0 = da_ref[...].astype(F32)
        halo = jnp.where(i < nt - 1, halo_ref[...].astype(F32), 0.0)
        d1 = jnp.where(sub == tm - 1, _pick_row(halo, 0), pltpu.roll(d0, tm - 1, 0))
        d2 = jnp.where(sub == tm - 2, _pick_row(halo, 0),
                       jnp.where(sub == tm - 1, _pick_row(halo, 1), pltpu.roll(d0, tm - 2, 0)))
        da = d0 * cw2[...] + d1 * cw1[...] + d2 * cw0[...]
        da = jnp.where(i * tm + sub >= N_PAD, da, 0.0).astype(BF16)
        dup_ref[:, 0:D_FF] = da
        dbv = db_ref[...]
        dup_ref[:, D_FF:2 * D_FF] = dbv
        dn = jnp.zeros((tm, D_MODEL), F32)
        for j in range(N_CHIPS):
            src = da if j < 2 else dbv
            lo = (j % 2) * shard
            dn = dn + _dot_nt(src[:, lo:lo + shard], wu_ref[j])
        h1 = h1_ref[...]
        r = lax.rsqrt(jnp.mean(h1 * h1, axis=-1, keepdims=True) + EPS)
        yn = h1 * r
        dg_ref[...] = dg_ref[...] + jnp.sum(dn * yn, axis=0, keepdims=True)
        dyn = dn * g_ref[...]
        dh1 = dh2_ref[...] + r * (dyn - yn * jnp.mean(dyn * yn, axis=-1, keepdims=True))
        dh1_ref[...] = dh1
        dh1b = dh1.astype(BF16)
        dh1b_ref[...] = dh1b
        dmix_ref[...] = _dot_nt(dh1b, wo_ref[...]).astype(BF16)

    rows = lambda w: pl.BlockSpec((tm, w), lambda i: (i, 0))
    halo = pl.BlockSpec((8, D_FF), lambda i: (jnp.minimum((i + 1) * (tm // 8), L // 8 - 1), 0))
    return pl.pallas_call(
        body, name="b_ffn_up", grid=(nt,),
        in_specs=[rows(D_FF), halo, rows(D_FF), _full((1, D_FF)), _full((1, D_FF)), _full((1, D_FF)),
                  _full((N_CHIPS, D_MODEL, shard)), rows(D_MODEL), _full((1, D_MODEL)), rows(D_MODEL),
                  _full((D_MODEL, D_MODEL))],
        out_specs=[rows(2 * D_FF), rows(D_MODEL), rows(D_MODEL), rows(D_MODEL), _full((1, D_MODEL))],
        out_shape=[jax.ShapeDtypeStruct((L, 2 * D_FF), BF16), jax.ShapeDtypeStruct((L, D_MODEL), F32),
                   jax.ShapeDtypeStruct((L, D_MODEL), BF16), jax.ShapeDtypeStruct((L, D_MODEL), BF16),
                   jax.ShapeDtypeStruct((1, D_MODEL), F32)],
        compiler_params=_params(("arbitrary",)),
    )(dacc, dacc, db, cw[0], cw[1], cw[2], w_up, h1, ffn_g, dh2, w_out)


def _wgrad(a, b, name, tn=None, tk=None):
    L, K = a.shape
    N = b.shape[1]
    tn = N if tn is None else tn
    tk = K if tk is None else tk
    tl = _row_tile(L, (1408, 768, 512, 256, 128))

    def body(a_ref, b_ref, o_ref):
        @pl.when(pl.program_id(2) == 0)
        def _():
            o_ref[...] = jnp.zeros_like(o_ref)

        o_ref[0] = o_ref[0] + _dot_tn(a_ref[...], b_ref[...])

    return pl.pallas_call(
        body, name=name, grid=(N // tn, K // tk, L // tl),
        in_specs=[pl.BlockSpec((tl, tk), lambda n, k, l: (l, k)), pl.BlockSpec((tl, tn), lambda n, k, l: (l, n))],
        out_specs=pl.BlockSpec((1, tk, tn), lambda n, k, l: (n, k, 0)),
        out_shape=jax.ShapeDtypeStruct((N // tn, K, tn), F32),
        compiler_params=_params(("parallel", "parallel", "arbitrary")),
    )(a, b)


def _retention_bwd(dmix, o, proj, cos_t, sin_t, ret_g, states):
    L = proj.shape[0]
    nblk = L // BLK
    dmat, wq_t, wk_t, g_blk = _decay_tables()

    def body(dm_ref, o_ref, q_ref, k_ref, v_ref, gate_ref, cos_ref, sin_ref, d_ref, wq_ref, wk_ref, rg_ref, rs_ref,
             dp_ref, drg_ref, gstate):
        i = pl.program_id(0)

        @pl.when(i == 0)
        def _():
            gstate[...] = jnp.zeros_like(gstate)
            drg_ref[...] = jnp.zeros_like(drg_ref)

        rot, rot_t = _rot_fns(cos_ref[...], sin_ref[...])
        lane = _iota((BLK, BLK), 1)
        sub = _iota((BLK, BLK), 0)
        scale = HEAD_LANES ** -0.5
        for p in range(2):
            qr = rot(q_ref[:, p * BLK:(p + 1) * BLK].astype(F32))
            kr = rot(k_ref[:, p * BLK:(p + 1) * BLK].astype(F32)) * scale
            kr_b = kr.astype(BF16)
            qw = (qr * wq_ref[p]).astype(BF16)
            kw = (kr * wk_ref[p]).astype(BF16)
            dqr = jnp.zeros((BLK, BLK), F32)
            dkr = jnp.zeros((BLK, BLK), F32)
            for e in range(2):
                h = 2 * p + e
                cols = slice(h * BLK, (h + 1) * BLK)
                head_lanes = (lane >> 6) == e
                o = o_ref[:, cols]
                rn = lax.rsqrt(jnp.mean(o * o, axis=-1, keepdims=True) + EPS)
                y = o * rn
                gate = gate_ref[:, cols].astype(F32)
                sg = jax.nn.sigmoid(gate)
                dm = dm_ref[:, cols].astype(F32)
                rgain = rg_ref[:, cols]
                drg_ref[:, cols] = drg_ref[:, cols] + jnp.sum(dm * y * (gate * sg), axis=0, keepdims=True)
                dp_ref[:, 1024 + h * BLK:1024 + (h + 1) * BLK] = (
                    dm * y * rgain * (sg * (1.0 + gate * (1.0 - sg)))).astype(BF16)
                dy = dm * rgain * (gate * sg)
                do = (rn * (dy - y * jnp.mean(dy * y, axis=-1, keepdims=True))).astype(BF16)
                vh = v_ref[:, cols]
                qm = jnp.where(head_lanes, qr, 0.0).astype(BF16)
                dmh = d_ref[h]
                s = (_dot_nt(qm, kr_b) * dmh).astype(BF16)
                ds = (_dot_nt(do, vh) * dmh).astype(BF16)
                st = rs_ref[0, h].astype(BF16)
                gs = gstate[h]
                gs_b = gs.astype(BF16)
                dqr = dqr + jnp.where(head_lanes, _dot(ds, kr_b), 0.0) + _dot_nt(do, st) * wq_ref[p]
                dkr = dkr + _dot_tn(ds, qm) + _dot_nt(vh, gs_b) * wk_ref[p]
                dp_ref[:, 512 + h * BLK:512 + (h + 1) * BLK] = (_dot_tn(s, do) + _dot(kw, gs_b)).astype(BF16)
                dr = jnp.where((sub >> 6) == e, _dot_tn(qw, do), 0.0)
                gstate[h] = dr + g_blk[h] * gs
            dp_ref[:, p * BLK:(p + 1) * BLK] = rot_t(dqr).astype(BF16)
            dp_ref[:, 256 + p * BLK:256 + (p + 1) * BLK] = (rot_t(dkr) * scale).astype(BF16)

    row = lambda c: (lambda i: (nblk - 1 - i, c))
    return pl.pallas_call(
        body, name="b_retention", grid=(nblk,),
        in_specs=[pl.BlockSpec((BLK, 512), row(0)), pl.BlockSpec((BLK, 512), row(0)),
                  pl.BlockSpec((BLK, 256), row(0)), pl.BlockSpec((BLK, 256), row(1)),
                  pl.BlockSpec((BLK, 512), row(1)), pl.BlockSpec((BLK, 512), row(2)),
                  pl.BlockSpec((BLK, BLK), row(0)), pl.BlockSpec((BLK, BLK), row(0)),
                  _full((RET_HEADS, BLK, BLK)), _full((2, BLK, BLK)), _full((2, BLK, BLK)), _full((1, 512)),
                  pl.BlockSpec((1, RET_HEADS, BLK, BLK), lambda i: (nblk - 1 - i, 0, 0, 0))],
        out_specs=[pl.BlockSpec((BLK, RET_W), row(0)), _full((1, 512))],
        out_shape=[jax.ShapeDtypeStruct((L, RET_W), BF16), jax.ShapeDtypeStruct((1, 512), F32)],
        scratch_shapes=[pltpu.VMEM((RET_HEADS, BLK, BLK), F32)],
        compiler_params=_params(("arbitrary",)),
    )(dmix, o, proj, proj, proj, proj, cos_t, sin_t, dmat, wq_t, wk_t, ret_g, states)


def _fox_delta(dmix, o_f):
    L = o_f.shape[0]
    nblk = L // BLK

    def body(do_ref, o_ref, d_ref):
        prod = do_ref[...].astype(F32) * o_ref[...].astype(F32)
        sel = ((_iota((8, 512), 1) >> 6) == _iota((8, 512), 0)).astype(BF16)
        hi = prod.astype(BF16)
        lo = (prod - hi.astype(F32)).astype(BF16)
        d_ref[0] = _dot_nt(sel, hi) + _dot_nt(sel, lo)

    return pl.pallas_call(
        body, name="b_foxdelta", grid=(nblk,),
        in_specs=[pl.BlockSpec((BLK, 512), lambda i: (i, 1)), pl.BlockSpec((BLK, 512), lambda i: (i, 0))],
        out_specs=pl.BlockSpec((1, 8, BLK), lambda i: (i, 0, 0)),
        out_shape=jax.ShapeDtypeStruct((nblk, 8, BLK), F32),
        compiler_params=_params(("parallel",)),
    )(dmix, o_f)


def _fox_bwd(proj, dmix, c, ctb, lse, delta, scatter=()):
    L = proj.shape[0]
    nblk, nu = _fox_units(L)
    scale = HEAD_LANES ** -0.5
    ns = len(scatter)

    def body(qkv_ref, do_ref, c_ref, ct_ref, lse_ref, dl_ref, *rest):
        s_in, (dp_ref, dc_ref, dcq_ref), s_out = rest[:ns], rest[ns:ns + 3], rest[ns + 3:2 * ns + 3]
        ktt, dqt, dk_acc, dv_acc, dcs_acc = rest[2 * ns + 3:2 * ns + 8]
        p = pl.program_id(0)

        @pl.when(p == 0)
        def _():
            dc_ref[...] = jnp.zeros_like(dc_ref)
            dcq_ref[...] = jnp.zeros_like(dcq_ref)
            if ns:
                for cp in _scatter_copies(s_in, s_out, *rest[2 * ns + 8:]):
                    cp.start()

        lane = _iota((BLK, BLK), 1)
        sub8 = _iota((8, BLK), 0)
        masks = _fox_tile_masks()

        def pre(j, carry):
            off = pl.multiple_of(j * BLK, BLK)
            ktt[j] = qkv_ref[pl.ds(off, BLK), BLK:2 * BLK].astype(F32).T.astype(BF16)
            dqt[j] = jnp.zeros((BLK, BLK), F32)
            return carry

        lax.fori_loop(0, nblk, pre, 0)

        def kv_pass(kblk, nk, n_later):
            klen = nk * BLK
            koff = pl.multiple_of(kblk * BLK, BLK)
            kt = qkv_ref[pl.ds(koff, klen), BLK:2 * BLK]
            vtile = qkv_ref[pl.ds(koff, klen), 2 * BLK:3 * BLK]
            ct = c_ref[pl.ds(koff, klen), :]
            klane = _iota((klen, BLK), 1)
            cs = [jnp.broadcast_to(jnp.sum(jnp.where(klane == 2 * p + e, ct, 0.0), axis=1, keepdims=True),
                                   (klen, WIDE * UNIT)) for e in range(2)]
            dk_acc[0:klen] = jnp.zeros((klen, BLK), F32)
            dv_acc[0:klen] = jnp.zeros((klen, BLK), F32)
            for e in range(2):
                dcs_acc[e, 0:klen] = jnp.zeros((klen, BLK), F32)

            def tile(qblk, nq, mask):
                qlen = nq * BLK
                if mask == "valid":
                    mask = _iota((klen, qlen), 0) >= N_PAD
                qoff = pl.multiple_of(qblk * BLK, BLK)
                qs = qkv_ref[pl.ds(qoff, qlen), 0:BLK].astype(F32) * scale
                dot_ = do_ref[pl.ds(qoff, qlen), :]
                qlane = _iota((qlen, BLK), 1)
                stats = [[ref[qblk + a] for a in range(nq)] for ref in (ct_ref, lse_ref, dl_ref)]
                for e in range(2):
                    h = 2 * p + e
                    head = (qlane >> 6) == e
                    ct_row, lse_row, dl_row = [jnp.concatenate([_pick_row(t, h) for t in ts], axis=1) for ts in stats]
                    qm = jnp.where(head, qs, 0.0).astype(BF16)
                    dom = jnp.where(head, dot_, jnp.zeros_like(dot_))
                    t = _dot_nt(kt, qm) - cs[e][:, 0:qlen]
                    if mask is not None:
                        t = jnp.where(mask, t, NEG)
                    pr = jnp.exp(t + (ct_row - lse_row))
                    dv_acc[0:klen] = dv_acc[0:klen] + _dot(pr.astype(BF16), dom)
                    dsv = pr * (_dot_nt(vtile, dom) - dl_row)
                    ds_b = dsv.astype(BF16)
                    dk_acc[0:klen] = dk_acc[0:klen] + _dot(ds_b, qm)
                    rows = slice(e * HEAD_LANES, (e + 1) * HEAD_LANES)
                    dq_t = _dot(ktt[kblk, rows, :], ds_b[0:BLK])
                    for b in range(1, nk):
                        dq_t = dq_t + _dot(ktt[kblk + b, rows, :], ds_b[b * BLK:(b + 1) * BLK])
                    key_side = dsv[:, 0:BLK]
                    for a in range(1, nq):
                        key_side = key_side + dsv[:, a * BLK:(a + 1) * BLK]
                    dcs_acc[e, 0:klen] = dcs_acc[e, 0:klen] + key_side
                    query_side = jnp.sum(dsv, axis=0, keepdims=True)
                    for a in range(nq):
                        cols = slice(a * BLK, (a + 1) * BLK)
                        dqt[qblk + a, rows, :] = dqt[qblk + a, rows, :] + dq_t[:, cols]
                        dcq_ref[qblk + a] = dcq_ref[qblk + a] + jnp.where(sub8 == h, query_side[:, cols], 0.0)

            later_mask = "valid" if nk == 1 else None
            n_later = jnp.asarray(n_later, jnp.int32)
            n_wide = n_later // WIDE

            def later_wide(i, carry):
                tile(kblk + nk + 2 * WIDE * i, 2 * WIDE, later_mask)
                return carry

            tile(kblk, nk, masks["first"] if nk == 1 else masks["diag"])
            lax.fori_loop(0, n_wide, later_wide, 0)
            rest = kblk + nk + 2 * WIDE * n_wide

            @pl.when((n_later & 2) != 0)
            def _():
                tile(rest, 4, later_mask)

            @pl.when((n_later & 1) != 0)
            def _():
                tile(rest + 2 * (n_later & 2), 2, later_mask)
            dp_ref[pl.ds(koff, klen), BLK:2 * BLK] = dk_acc[0:klen].astype(BF16)
            dp_ref[pl.ds(koff, klen), 2 * BLK:3 * BLK] = dv_acc[0:klen].astype(BF16)
            upd = jnp.zeros((klen, BLK), F32)
            for e in range(2):
                upd = upd + jnp.where(klane == 2 * p + e, -jnp.sum(dcs_acc[e, 0:klen], axis=1, keepdims=True), 0.0)
            dc_ref[pl.ds(koff, klen), :] = dc_ref[pl.ds(koff, klen), :] + upd

        kv_pass(0, 1, nu)

        def k_loop(u, carry):
            kv_pass(1 + 2 * u, 2, nu - 1 - u)
            return carry

        lax.fori_loop(0, nu, k_loop, 0)

        def flush(j, carry):
            off = pl.multiple_of(j * BLK, BLK)
            dp_ref[pl.ds(off, BLK), 0:BLK] = (dqt[j].T * scale).astype(BF16)
            return carry

        lax.fori_loop(0, nblk, flush, 0)

        if ns:
            @pl.when(p == FOX_HEADS // 2 - 1)
            def _():
                copies = _scatter_copies(s_in, s_out, *rest[2 * ns + 8:])
                for cp in copies:
                    cp.wait_recv()
                for cp in copies:
                    cp.wait_send()

    stat = _full((nblk, 8, BLK))
    return pl.pallas_call(
        body, name="b_fox", grid=(FOX_HEADS // 2,),
        in_specs=[pl.BlockSpec((L, 384), lambda p: (0, RET_W // 384 + p)), pl.BlockSpec((L, BLK), lambda p: (0, 4 + p)),
                  _full((L, BLK)), stat, stat, stat] + [_ANY] * ns,
        out_specs=[pl.BlockSpec((L, 384), lambda p: (0, p)), _full((L, BLK)), stat] + [_ANY] * ns,
        out_shape=[jax.ShapeDtypeStruct((L, FOX_W), BF16), jax.ShapeDtypeStruct((L, BLK), F32),
                   jax.ShapeDtypeStruct((nblk, 8, BLK), F32)] + _scatter_shapes(scatter),
        scratch_shapes=[pltpu.VMEM((nblk, BLK, BLK), BF16), pltpu.VMEM((nblk, BLK, BLK), F32),
                        pltpu.VMEM((UNIT, BLK), F32), pltpu.VMEM((UNIT, BLK), F32), pltpu.VMEM((2, UNIT, BLK), F32)]
        + _scatter_semaphores(ns),
        compiler_params=_params(("arbitrary",)),
    )(proj, dmix, c, ctb, lse, delta, *scatter)


def _fox_post(dc, dcq, ff, fb):
    L = dc.shape[0]
    nblk = L // BLK

    def body(dc_ref, dcq_ref, ff_ref, b_ref, dff_ref, dffb_ref, dfb_ref, carry):
        i = pl.program_id(0)

        @pl.when(i == 0)
        def _():
            carry[...] = jnp.zeros_like(carry)
            dfb_ref[...] = jnp.zeros_like(dfb_ref)

        d = dc_ref[...] + jnp.concatenate([dcq_ref[0], jnp.zeros((BLK - 8, BLK), F32)], axis=0).T
        tri = (_iota((BLK, BLK), 0) <= _iota((BLK, BLK), 1)).astype(BF16)
        hi, mid, lo = _split3(d)
        dlf = _dot(tri, hi) + _dot(tri, mid) + _dot(tri, lo) + carry[...]
        carry[...] = carry[...] + jnp.sum(d, axis=0, keepdims=True)
        z = ff_ref[...] + b_ref[...]
        dff = jnp.where(_iota((BLK, BLK), 1) < FOX_HEADS, dlf * jax.nn.sigmoid(-z), 0.0)
        dff_ref[...] = dff
        dffb_ref[...] = dff.astype(BF16)
        dfb_ref[...] = dfb_ref[...] + jnp.sum(dff, axis=0, keepdims=True)

    rev = lambda i: (nblk - 1 - i, 0)
    return pl.pallas_call(
        body, name="b_foxpost", grid=(nblk,),
        in_specs=[pl.BlockSpec((BLK, BLK), rev), pl.BlockSpec((1, 8, BLK), lambda i: (nblk - 1 - i, 0, 0)),
                  pl.BlockSpec((BLK, BLK), rev), _full((1, BLK))],
        out_specs=[pl.BlockSpec((BLK, BLK), rev), pl.BlockSpec((BLK, BLK), rev), _full((1, BLK))],
        out_shape=[jax.ShapeDtypeStruct((L, BLK), F32), jax.ShapeDtypeStruct((L, BLK), BF16),
                   jax.ShapeDtypeStruct((1, BLK), F32)],
        scratch_shapes=[pltpu.VMEM((1, BLK), F32)],
        compiler_params=_params(("arbitrary",)),
    )(dc, dcq, ff, fb)


def _inproj_bwd(dpr, dpf, dffb, w_main, w_ff, h0, g, dh1):
    L = h0.shape[0]
    tm = _row_tile(L)

    def body(dpr_ref, dpf_ref, dff_ref, wm_ref, wf_ref, h_ref, g_ref, dh1_ref, dh0_ref, dg_ref):
        @pl.when(pl.program_id(0) == 0)
        def _():
            dg_ref[...] = jnp.zeros_like(dg_ref)

        dn = (_dot_nt(dpr_ref[...], wm_ref[:, 0:RET_W]) + _dot_nt(dpf_ref[...], wm_ref[:, RET_W:MAIN_W])
              + _dot_nt(dff_ref[...], wf_ref[...]))
        h = h_ref[...]
        r = lax.rsqrt(jnp.mean(h * h, axis=-1, keepdims=True) + EPS)
        yn = h * r
        dg_ref[...] = dg_ref[...] + jnp.sum(dn * yn, axis=0, keepdims=True)
        dyn = dn * g_ref[...]
        dh0_ref[...] = dh1_ref[...] + r * (dyn - yn * jnp.mean(dyn * yn, axis=-1, keepdims=True))

    rows = lambda w: pl.BlockSpec((tm, w), lambda i: (i, 0))
    return pl.pallas_call(
        body, name="b_inproj", grid=(L // tm,),
        in_specs=[rows(RET_W), rows(FOX_W), rows(BLK), _full((D_MODEL, MAIN_W)), _full((D_MODEL, BLK)),
                  rows(D_MODEL), _full((1, D_MODEL)), rows(D_MODEL)],
        out_specs=[rows(D_MODEL), _full((1, D_MODEL))],
        out_shape=[jax.ShapeDtypeStruct((L, D_MODEL), F32), jax.ShapeDtypeStruct((1, D_MODEL), F32)],
        compiler_params=_params(("arbitrary",)),
    )(dpr, dpf, dffb, w_main, w_ff, h0, g, dh1)


def _local_step(x, target, meta, attn_g, w_main, w_ff, fox_b, ret_g, w_out, ffn_g, w_up, conv_w, conv_b, w_down, final_g,
                late=None, mid=None):
    S = x.shape[0]
    L = S + PREFIX
    h0 = jnp.concatenate([jnp.zeros((N_PAD, D_MODEL), F32), meta, x], axis=0)
    tgt = jnp.concatenate([jnp.zeros((PREFIX, D_MODEL), F32), target], axis=0)
    fb = jnp.pad(fox_b, ((0, 0), (0, BLK - FOX_HEADS)))
    cos_t, sin_t = _rotary_tables(L)

    n1, proj, ff = _rms_inproj(h0, attn_g, w_main, w_ff)
    c, ctb = _fox_prep(ff, fb)
    mix_r, o_ret, states = _retention_fwd(proj, cos_t, sin_t, ret_g)
    if late is None:
        o_f, lse = _fox_fwd(proj, c, ctb)
    else:
        o_f, lse, *gathered = _fox_fwd(proj, c, ctb, gather=late[0])
        w_out, w_up, w_down = late[1](gathered)
    h1, n2, up = _outproj_up(mix_r, o_f, h0, w_out, ffn_g, w_up)
    g_act, dh2, dh2b, d_final_g, loss = _ffn_down_loss(up, conv_w, conv_b, w_down, h1, final_g, tgt)

    dacc, db, dconv = _ffn_bwd_gate(dh2b, w_down, up, conv_w, conv_b)
    dup, dh1, dh1b, dmix, d_ffn_g = _ffn_bwd_up(dacc, db, conv_w, w_up, h1, ffn_g, dh2, w_out)
    d_w_down = _wgrad(g_act, dh2b, "wgrad_down", tk=D_FF // 2)[0]
    d_w_up = _wgrad(n2, dup, "wgrad_up", tn=w_up.shape[2])
    d_w_out = jnp.concatenate([_wgrad(mix_r, dh1b, "wgrad_out_r")[0], _wgrad(o_f, dh1b, "wgrad_out_f")[0]], axis=0)

    dpr, d_ret_g = _retention_bwd(dmix, o_ret, proj, cos_t, sin_t, ret_g, states)
    delta = _fox_delta(dmix, o_f)
    scatter = () if mid is None else mid(d_w_out, d_w_up, d_w_down)
    dpf, dc, dcq, *received = _fox_bwd(proj, dmix, c, ctb, lse, delta, scatter=scatter)
    dff, dffb, d_fox_b = _fox_post(dc, dcq, ff, fb)
    dh0, d_attn_g = _inproj_bwd(dpr, dpf, dffb, w_main, w_ff, h0, attn_g, dh1)
    d_w_main = jnp.concatenate([_wgrad(n1, dpr, "wgrad_in_r")[0], _wgrad(n1, dpf, "wgrad_in_f")[0]], axis=1)
    d_w_ff = _wgrad(n1, dffb, "wgrad_in_ff")[0]

    return dict(
        loss=loss[0, 0], dx=dh0[PREFIX:], dmeta=dh0[N_PAD:PREFIX], attn_g=d_attn_g, w_main=d_w_main,
        w_ff=d_w_ff[:, :FOX_HEADS], fox_b=d_fox_b[:, :FOX_HEADS], ret_g=d_ret_g, w_out=d_w_out, ffn_g=d_ffn_g,
        w_up=d_w_up, conv_w=dconv[0:3], conv_b=dconv[3:4], w_down=d_w_down, final_g=d_final_g,
        scatter=scatter, received=received)


_ANY = pl.BlockSpec(memory_space=pl.ANY)


def _place():
    return lax.axis_index("x"), lax.axis_index("y"), lax.axis_index("c")


def _other_chips(x, y):
    return [(1 - x, y), (x, 1 - y), (1 - x, 1 - y)]


def _allgather_semaphores(n):
    if n == 0:
        return []
    return [pltpu.SemaphoreType.DMA((3 * n,)), pltpu.SemaphoreType.DMA((3 * n,)), pltpu.SemaphoreType.DMA((n,))]


def _allgather_copies(ins, outs, send, recv, loc):
    n = len(ins)
    x, y, c = _place()
    mine = 2 * x + y
    peers = _other_chips(x, y)

    def remote(a, k, slot):
        return pltpu.make_async_remote_copy(
            src_ref=ins[a], dst_ref=outs[a].at[slot], send_sem=send.at[3 * a + k], recv_sem=recv.at[3 * a + k],
            device_id=(peers[k][0], peers[k][1], c), device_id_type=MESH)

    local = [pltpu.make_async_copy(ins[a], outs[a].at[mine], loc.at[a]) for a in range(n)]
    sends = [remote(a, k, mine) for a in range(n) for k in range(3)]
    recvs = [remote(a, k, 2 * peers[k][0] + peers[k][1]) for a in range(n) for k in range(3)]
    return local, sends, recvs


def _chip_allgather(arrays):
    n = len(arrays)

    def body(*refs):
        local, sends, recvs = _allgather_copies(refs[:n], refs[n:2 * n], *refs[2 * n:])
        for cp in local + sends:
            cp.start()
        for cp in recvs:
            cp.wait_recv()
        for cp in sends:
            cp.wait_send()
        for cp in local:
            cp.wait()

    return pl.pallas_call(
        body, name="ag_weights", in_specs=[_ANY] * n, out_specs=[_ANY] * n,
        out_shape=[jax.ShapeDtypeStruct((N_CHIPS,) + a.shape, a.dtype) for a in arrays],
        scratch_shapes=_allgather_semaphores(n),
    )(*arrays)


def _sibling_exchange(grads, small):
    n = len(grads)

    def body(*refs):
        ins, small_in = refs[:n], refs[n]
        outs, small_out = refs[n + 1:2 * n + 1], refs[2 * n + 1]
        send, recv, s_send, s_recv, loc = refs[2 * n + 2:]
        x, y, c = _place()
        me = 4 * x + 2 * y + c

        def half_copy(a, which):
            half = ins[a].shape[1] // 2
            return pltpu.make_async_remote_copy(
                src_ref=ins[a].at[pl.ds(0, N_CHIPS), pl.ds(which * half, half)], dst_ref=outs[a],
                send_sem=send.at[a], recv_sem=recv.at[a], device_id=(x, y, 1 - c), device_id_type=MESH)

        def peer_of(r):
            return tuple(1 - v if (r >> b) & 1 else v for v, b in ((x, 2), (y, 1), (c, 0)))

        def small_copy(r, slot):
            return pltpu.make_async_remote_copy(
                src_ref=small_in, dst_ref=small_out.at[slot], send_sem=s_send.at[r - 1], recv_sem=s_recv.at[r - 1],
                device_id=peer_of(r), device_id_type=MESH)

        local = pltpu.make_async_copy(small_in, small_out.at[me], loc.at[0])
        sends = [half_copy(a, 1 - c) for a in range(n)] + [small_copy(r, me) for r in range(1, N_DEV)]
        local.start()
        for cp in sends:
            cp.start()
        for r in range(1, N_DEV):
            px, py, pc = peer_of(r)
            small_copy(r, 4 * px + 2 * py + pc).wait_recv()
        for a in range(n):
            half_copy(a, c).wait_recv()
        for cp in sends:
            cp.wait_send()
        local.wait()

    rows = small.shape[0]
    return pl.pallas_call(
        body, name="rs_sibling", in_specs=[_ANY] * (n + 1), out_specs=[_ANY] * (n + 1),
        out_shape=[jax.ShapeDtypeStruct((N_CHIPS, g.shape[1] // 2, g.shape[2]), g.dtype) for g in grads]
        + [jax.ShapeDtypeStruct((N_DEV, rows, small.shape[1]), small.dtype)],
        scratch_shapes=[pltpu.SemaphoreType.DMA((n,)), pltpu.SemaphoreType.DMA((n,)),
                        pltpu.SemaphoreType.DMA((N_DEV - 1,)), pltpu.SemaphoreType.DMA((N_DEV - 1,)),
                        pltpu.SemaphoreType.DMA((1,))],
    )(*grads, small)


def _sibling_halves(grads):
    n = len(grads)

    def body(*refs):
        ins, outs = refs[:n], refs[n:2 * n]
        send, recv = refs[2 * n:]
        x, y, c = _place()

        def half_copy(a, which):
            half = ins[a].shape[1] // 2
            return pltpu.make_async_remote_copy(
                src_ref=ins[a].at[pl.ds(0, N_CHIPS), pl.ds(which * half, half)], dst_ref=outs[a],
                send_sem=send.at[a], recv_sem=recv.at[a], device_id=(x, y, 1 - c), device_id_type=MESH)

        sends = [half_copy(a, 1 - c) for a in range(n)]
        for cp in sends:
            cp.start()
        for a in range(n):
            half_copy(a, c).wait_recv()
        for cp in sends:
            cp.wait_send()

    return pl.pallas_call(
        body, name="rs_sibling_early", in_specs=[_ANY] * n, out_specs=[_ANY] * n,
        out_shape=[jax.ShapeDtypeStruct((N_CHIPS, g.shape[1] // 2, g.shape[2]), g.dtype) for g in grads],
        scratch_shapes=[pltpu.SemaphoreType.DMA((n,)), pltpu.SemaphoreType.DMA((n,))],
    )(*grads)


def _chip_reduce_scatter(parts):
    n = len(parts)

    def body(*refs):
        copies = _scatter_copies(refs[:n], refs[n:2 * n], *refs[2 * n:])
        for cp in copies:
            cp.start()
        for cp in copies:
            cp.wait_recv()
        for cp in copies:
            cp.wait_send()

    return pl.pallas_call(
        body, name="rs_chips", in_specs=[_ANY] * n, out_specs=[_ANY] * n,
        out_shape=_scatter_shapes(parts), scratch_shapes=_scatter_semaphores(n),
    )(*parts)


def _scatter_shapes(parts):
    return [jax.ShapeDtypeStruct((3,) + p.shape[1:], p.dtype) for p in parts]


def _scatter_semaphores(n):
    return [pltpu.SemaphoreType.DMA((3 * n,)), pltpu.SemaphoreType.DMA((3 * n,))] if n else []


def _scatter_copies(ins, outs, send, recv):
    x, y, c = _place()
    peers = _other_chips(x, y)
    return [pltpu.make_async_remote_copy(
        src_ref=ins[a].at[2 * peers[k][0] + peers[k][1]], dst_ref=outs[a].at[k], send_sem=send.at[3 * a + k],
        recv_sem=recv.at[3 * a + k], device_id=(peers[k][0], peers[k][1], c), device_id_type=MESH)
        for a in range(len(ins)) for k in range(3)]


def _sibling_allgather(bufs):
    n = len(bufs)

    def body(*refs):
        outs = refs[n:2 * n]
        send, recv = refs[2 * n:]
        x, y, c = _place()

        def remote(a, which):
            return pltpu.make_async_remote_copy(
                src_ref=outs[a].at[which], dst_ref=outs[a].at[which], send_sem=send.at[a], recv_sem=recv.at[a],
                device_id=(x, y, 1 - c), device_id_type=MESH)

        sends = [remote(a, c) for a in range(n)]
        for cp in sends:
            cp.start()
        for a in range(n):
            remote(a, 1 - c).wait_recv()
        for cp in sends:
            cp.wait_send()

    outs = pl.pallas_call(
        body, name="ag_sibling", in_specs=[_ANY] * n, out_specs=[_ANY] * n,
        out_shape=[jax.ShapeDtypeStruct(b.shape, b.dtype) for b in bufs],
        input_output_aliases={a: a for a in range(n)},
        scratch_shapes=[pltpu.SemaphoreType.DMA((n,)), pltpu.SemaphoreType.DMA((n,))],
    )(*bufs)
    return [o.reshape(2 * o.shape[1], o.shape[2]) for o in outs]


def _pair_add(full, recv, core, name):
    _, R, C = full.shape
    half = R // 2

    def body(core_ref, a_ref, b_ref, o_ref):
        o_ref[...] = (a_ref[...] + b_ref[...]).astype(BF16)

    return pl.pallas_call(
        body, name=name,
        grid_spec=pltpu.PrefetchScalarGridSpec(
            num_scalar_prefetch=1, grid=(N_CHIPS,),
            in_specs=[pl.BlockSpec((1, half, C), lambda j, core_ref: (j, core_ref[0], 0)),
                      pl.BlockSpec((1, half, C), lambda j, core_ref: (j, 0, 0))],
            out_specs=pl.BlockSpec((1, half, C), lambda j, core_ref: (j, 0, 0))),
        out_shape=jax.ShapeDtypeStruct((N_CHIPS, half, C), BF16),
        compiler_params=_params(("parallel",)),
    )(core, full, recv)


def _sum_slots(q, name, tiles=2):
    n, R, C = q.shape
    tr = R // tiles

    def body(q_ref, o_ref):
        acc = q_ref[0].astype(F32)
        for j in range(1, n):
            acc = acc + q_ref[j].astype(F32)
        o_ref[...] = acc

    return pl.pallas_call(
        body, name=name, grid=(tiles,),
        in_specs=[pl.BlockSpec((n, tr, C), lambda i: (0, i, 0))],
        out_specs=pl.BlockSpec((tr, C), lambda i: (i, 0)),
        out_shape=jax.ShapeDtypeStruct((R, C), F32),
        compiler_params=_params(("parallel",)),
    )(q)


def _sum_partials(own_all, recv, place, name, tiles=2):
    _, R, C = own_all.shape
    tr = R // tiles

    def body(place_ref, own_ref, r_ref, o_ref):
        acc = own_ref[0].astype(F32)
        for k in range(3):
            acc = acc + r_ref[k].astype(F32)
        o_ref[0] = acc

    return pl.pallas_call(
        body, name=name,
        grid_spec=pltpu.PrefetchScalarGridSpec(
            num_scalar_prefetch=1, grid=(tiles,),
            in_specs=[pl.BlockSpec((1, tr, C), lambda i, place_ref: (place_ref[0], i, 0)),
                      pl.BlockSpec((3, tr, C), lambda i, place_ref: (0, i, 0))],
            out_specs=pl.BlockSpec((1, tr, C), lambda i, place_ref: (place_ref[1], i, 0))),
        out_shape=jax.ShapeDtypeStruct((2, R, C), F32),
        compiler_params=_params(("parallel",)),
    )(place, own_all, recv)


def _adamw(w, g, m, v, name, tiles=4):
    R, C = w.shape
    tr = R // tiles

    def body(w_ref, g_ref, m_ref, v_ref, go_ref, d_ref, m2_ref, v2_ref):
        g_ = g_ref[...]
        go_ref[...] = g_
        m2 = ADAM_B1 * m_ref[...] + (1.0 - ADAM_B1) * g_
        v2 = ADAM_B2 * v_ref[...] + (1.0 - ADAM_B2) * (g_ * g_)
        m_hat = m2 / (1.0 - ADAM_B1 ** ADAM_STEP)
        v_hat = v2 / (1.0 - ADAM_B2 ** ADAM_STEP)
        d_ref[...] = -ADAM_LR * (m_hat / (jnp.sqrt(v_hat) + ADAM_EPS) + ADAM_WD * w_ref[...])
        m2_ref[...] = m2
        v2_ref[...] = v2

    spec = pl.BlockSpec((tr, C), lambda i: (i, 0))
    return pl.pallas_call(
        body, name=name, grid=(tiles,), in_specs=[spec] * 4, out_specs=[spec] * 4,
        out_shape=[jax.ShapeDtypeStruct((R, C), F32)] * 4,
        compiler_params=_params(("parallel",)),
    )(w, g, m, v)


def _pack_rows(pieces, rows):
    flat = jnp.concatenate([jnp.pad(p.reshape(-1).astype(F32), (0, (-p.size) % D_MODEL)) for p in pieces])
    return jnp.pad(flat, (0, rows * D_MODEL - flat.size)).reshape(rows, D_MODEL)


def _unpack_rows(pack, shapes):
    flat = pack.reshape(-1)
    out, off = [], 0
    for shp in shapes:
        size = int(np.prod(shp))
        out.append(flat[off:off + size].reshape(shp))
        off += size + (-size) % D_MODEL
    return out


def _kernel_order(w):
    parts = [w[:, 0:RET_W]]
    for p in range(FOX_HEADS // 2):
        parts += [w[:, RET_W + part * 512 + p * BLK:RET_W + part * 512 + (p + 1) * BLK] for part in range(3)]
    return jnp.concatenate(parts, axis=1)


def _reference_order(g_main, g_ff):
    parts = [g_main[:, 0:RET_W]]
    for part in range(3):
        parts += [g_main[:, RET_W + 384 * p + part * BLK:RET_W + 384 * p + (part + 1) * BLK] for p in range(FOX_HEADS // 2)]
    return jnp.concatenate(parts + [g_ff], axis=1)


def kernel(x, meta_tokens, attn_norm_g, w_in, fox_forget_b, ret_norm_g, w_out, ffn_norm_g, w_up, conv_w, conv_b, w_down, final_norm_g, loss_target, m_meta_tokens, m_attn_norm_g, m_w_in, m_fox_forget_b, m_ret_norm_g, m_w_out, m_ffn_norm_g, m_w_up, m_conv_w, m_conv_b, m_w_down, m_final_norm_g, v_meta_tokens, v_attn_norm_g, v_w_in, v_fox_forget_b, v_ret_norm_g, v_w_out, v_ffn_norm_g, v_w_up, v_conv_w, v_conv_b, v_w_down, v_final_norm_g):
    chip = 2 * lax.axis_index("x") + lax.axis_index("y")
    core = lax.axis_index("c")
    meta_w, conv_sw = meta_tokens.shape[1], conv_w.shape[2]

    small_w = _pack_rows([meta_tokens, conv_w[0]], 8)
    g_in, g_small = _chip_allgather([w_in[0].astype(BF16), small_w])
    w_in_full = g_in.transpose(1, 0, 2).reshape(D_MODEL, IN_WIDTH)
    w_main = _kernel_order(w_in_full)
    w_ff = jnp.pad(w_in_full[:, MAIN_W:], ((0, 0), (0, BLK - FOX_HEADS)))
    small_parts = [_unpack_rows(g_small[j], [meta_tokens.shape, conv_w.shape[1:]]) for j in range(N_CHIPS)]
    meta_full = jnp.concatenate([sp[0] for sp in small_parts], axis=1)
    conv_w_full = jnp.concatenate([sp[1] for sp in small_parts], axis=1)

    core_idx = core.reshape(1).astype(jnp.int32)
    place = jnp.stack([chip, core]).astype(jnp.int32)

    def assemble(gathered):
        g_out, g_up, g_down = gathered
        return g_out.reshape(D_MODEL, D_MODEL), g_up, g_down.reshape(D_FF, D_MODEL)

    def early_reduce(d_w_out, d_w_up, d_w_down):
        early = [d_w_out.reshape(N_CHIPS, -1, D_MODEL), d_w_up, d_w_down.reshape(N_CHIPS, -1, D_MODEL)]
        from_sib = _sibling_halves(early)
        return [_pair_add(g, r, core_idx, "pair_add_" + nm) for g, r, nm in zip(early, from_sib, ("out", "up", "down"))]

    out = _local_step(x[0], loss_target[0], meta_full, attn_norm_g, w_main, w_ff, fox_forget_b, ret_norm_g,
                      None, ffn_norm_g, None, conv_w_full, conv_b, None, final_norm_g[None],
                      late=([w_out[0].astype(BF16), w_up[0].astype(BF16), w_down[0].astype(BF16)], assemble),
                      mid=early_reduce)

    g_in_full = _reference_order(out["w_main"], out["w_ff"]).reshape(D_MODEL, N_CHIPS, -1).transpose(1, 0, 2)
    small_shapes = [(1, D_MODEL), (1, D_MODEL), (1, D_MODEL), (1, 512 + FOX_HEADS + 1), (1, D_FF), (N_META, D_MODEL), (3, D_FF)]
    small = _pack_rows([out["attn_g"], out["ffn_g"], out["final_g"],
                        jnp.concatenate([out["ret_g"], out["fox_b"], out["loss"].reshape(1, 1)], axis=1),
                        out["conv_b"], out["dmeta"], out["conv_w"]], 32)
    from_sibling_in, small_all = _sibling_exchange([g_in_full], small)
    sum_in = _pair_add(g_in_full, from_sibling_in, core_idx, "pair_add_in")
    (from_chips_in,) = _chip_reduce_scatter([sum_in])
    chip_sums = [sum_in] + list(out["scatter"])
    from_chips = [from_chips_in] + list(out["received"])
    names = ("in", "out", "up", "down")
    totals = [_sum_partials(s, q, place, "sum_chips_" + nm) for s, q, nm in zip(chip_sums, from_chips, names)]
    grad_in, grad_out, grad_up, grad_down = _sibling_allgather(totals)
    s_attn, s_ffn, s_final, s_misc, s_conv_b, s_meta, s_conv_w = _unpack_rows(
        _sum_slots(small_all, "sum_small", tiles=1), small_shapes)
    loss = s_misc[0, 512 + FOX_HEADS]
    small_grads = [lax.dynamic_slice_in_dim(s_meta, chip * meta_w, meta_w, axis=1), s_attn, s_misc[:, 512:512 + FOX_HEADS],
                   s_misc[:, :512], s_ffn, lax.dynamic_slice_in_dim(s_conv_w, chip * conv_sw, conv_sw, axis=1)[None],
                   s_conv_b, s_final[0]]

    big_w = [(w_in, m_w_in, v_w_in, grad_in, "adamw_in"), (w_out, m_w_out, v_w_out, grad_out, "adamw_out"),
             (w_up, m_w_up, v_w_up, grad_up, "adamw_up"), (w_down, m_w_down, v_w_down, grad_down, "adamw_down")]
    big_res = [[r[None] for r in _adamw(w[0], g, m[0], v[0], nm)] for w, m, v, g, nm in big_w]
    small_w_list = [meta_tokens, attn_norm_g, fox_forget_b, ret_norm_g, ffn_norm_g, conv_w, conv_b, final_norm_g]
    small_m = [m_meta_tokens, m_attn_norm_g, m_fox_forget_b, m_ret_norm_g, m_ffn_norm_g, m_conv_w, m_conv_b, m_final_norm_g]
    small_v = [v_meta_tokens, v_attn_norm_g, v_fox_forget_b, v_ret_norm_g, v_ffn_norm_g, v_conv_w, v_conv_b, v_final_norm_g]
    shapes = [a.shape for a in small_w_list]
    packs = [_pack_rows(lst, 16) for lst in (small_w_list, small_grads, small_m, small_v)]
    small_res = [_unpack_rows(r, shapes) for r in _adamw(*packs, "adamw_small", tiles=1)[1:]]
    small_grads = [g.reshape(s) for g, s in zip(small_grads, shapes)]

    def ordered(kind):
        sm = small_grads if kind == 0 else small_res[kind - 1]
        bg = [r[kind] for r in big_res]
        return [sm[0], sm[1], bg[0], sm[2], sm[3], bg[1], sm[4], bg[2], sm[5], sm[6], bg[3], sm[7]]

    return (loss, out["dx"][None], *ordered(0), *ordered(1), *ordered(2), *ordered(3))
```

```python
import functools

import numpy as np
import jax
import jax.numpy as jnp
from jax import lax
from jax.experimental import pallas as pl
from jax.experimental.pallas import tpu as pltpu

F32 = jnp.float32
BF16 = jnp.bfloat16

D_MODEL = 1024
N_META = 16
BLK = 128
UNIT = 2 * BLK
WIDE = 4
CHUNK = 64
N_PAD = BLK - N_META
PREFIX = BLK
RET_HEADS = 4
FOX_HEADS = 8
HEAD_LANES = 64
D_FF = 2816
ROPE_BASE = 10000.0
EPS = 1e-6
NEG = -1e30
RET_W = 1536
FOX_W = 1536
MAIN_W = RET_W + FOX_W
IN_WIDTH = MAIN_W + FOX_HEADS
N_CHIPS = 4
N_DEV = 8

ADAM_LR = 0.001
ADAM_B1 = 0.9
ADAM_B2 = 0.999
ADAM_EPS = 1e-08
ADAM_WD = 0.01
ADAM_STEP = 10

MESH = pl.DeviceIdType.MESH
VMEM_LIMIT_MB = 56

_NT = (((1,), (1,)), ((), ()))
_TN = (((0,), (0,)), ((), ()))


def _dot(a, b):
    return jnp.dot(a, b, preferred_element_type=F32)


def _dot_nt(a, b):
    return lax.dot_general(a, b, _NT, preferred_element_type=F32)


def _dot_tn(a, b):
    return lax.dot_general(a, b, _TN, preferred_element_type=F32)


def _params(dims=None, vmem_mb=VMEM_LIMIT_MB):
    kw = dict(vmem_limit_bytes=vmem_mb << 20)
    if dims is not None:
        kw["dimension_semantics"] = dims
    return pltpu.CompilerParams(**kw)


def _row_tile(n, prefs=(384, 256, 128)):
    for t in prefs:
        if n % t == 0:
            return t
    raise ValueError(f"no row tile for {n}")


def _iota(shape, dim):
    return lax.broadcasted_iota(jnp.int32, shape, dim)


def _pick_row(tile, row):
    sub = _iota(tile.shape, 0)
    return jnp.sum(jnp.where(sub == row, tile, 0.0), axis=0, keepdims=True)


def _split3(x):
    hi = x.astype(BF16)
    r1 = x - hi.astype(F32)
    mid = r1.astype(BF16)
    lo = (r1 - mid.astype(F32)).astype(BF16)
    return hi, mid, lo


def _full(shape):
    nd = len(shape)
    return pl.BlockSpec(shape, lambda *_: (0,) * nd)


def _in_perm():
    cols = list(range(RET_W))
    for p in range(FOX_HEADS // 2):
        for part in range(3):
            start = RET_W + part * 512 + p * BLK
            cols += list(range(start, start + BLK))
    return np.asarray(cols, np.int32)


def _rotary_tables(L):
    half = HEAD_LANES // 2
    inv = 1.0 / (ROPE_BASE ** (jnp.arange(half, dtype=F32) / half))
    ang = jnp.arange(L).astype(F32)[:, None] * inv[None, :]
    cos, sin = jnp.cos(ang), jnp.sin(ang)
    cos_t = jnp.tile(cos, (1, 4))
    sin_t = jnp.tile(jnp.concatenate([-sin, sin], axis=1), (1, 2))
    return cos_t, sin_t


def _decay_tables():
    gam = 1.0 - 2.0 ** (-5.0 - np.arange(RET_HEADS, dtype=np.float64))
    n = np.arange(BLK)
    same_or_past = (n[:, None] // CHUNK) >= (n[None, :] // CHUNK)
    dist = np.abs(n[:, None] - n[None, :])
    dmat = np.stack([np.where(same_or_past, g ** dist, 0.0) for g in gam]).astype(np.float32)
    lane_head = np.arange(BLK) // HEAD_LANES
    wq = np.stack([gam[2 * p + lane_head][None, :] ** (n[:, None] + 1.0) for p in range(2)]).astype(np.float32)
    wk = np.stack([gam[2 * p + lane_head][None, :] ** (BLK - 1.0 - n[:, None]) for p in range(2)]).astype(np.float32)
    g_blk = tuple(float(g ** BLK) for g in gam)
    return jnp.asarray(dmat), jnp.asarray(wq), jnp.asarray(wk), g_blk


def _rms_inproj(h0, g, w_main, w_ff):
    L = h0.shape[0]
    tm = _row_tile(L)

    def body(h_ref, g_ref, wm_ref, wf_ref, n_ref, p_ref, ff_ref):
        h = h_ref[...]
        r = lax.rsqrt(jnp.mean(h * h, axis=-1, keepdims=True) + EPS)
        n = (h * r * g_ref[...]).astype(BF16)
        n_ref[...] = n
        p_ref[...] = _dot(n, wm_ref[...]).astype(BF16)
        ff_ref[...] = _dot(n, wf_ref[...])

    return pl.pallas_call(
        body, name="f_inproj", grid=(L // tm,),
        in_specs=[pl.BlockSpec((tm, D_MODEL), lambda i: (i, 0)), _full((1, D_MODEL)),
                  _full((D_MODEL, MAIN_W)), _full((D_MODEL, BLK))],
        out_specs=[pl.BlockSpec((tm, D_MODEL), lambda i: (i, 0)), pl.BlockSpec((tm, MAIN_W), lambda i: (i, 0)),
                   pl.BlockSpec((tm, BLK), lambda i: (i, 0))],
        out_shape=[jax.ShapeDtypeStruct((L, D_MODEL), BF16), jax.ShapeDtypeStruct((L, MAIN_W), BF16),
                   jax.ShapeDtypeStruct((L, BLK), F32)],
        compiler_params=_params(("parallel",)),
    )(h0, g, w_main, w_ff)


def _block_group(nblk):
    return 3 if nblk % 3 == 0 else 1


def _fox_prep(ff, fb):
    L = ff.shape[0]
    nblk = L // BLK
    G = _block_group(nblk)

    def body(ff_ref, b_ref, c_ref, ct_ref, carry):
        @pl.when(pl.program_id(0) == 0)
        def _():
            carry[...] = jnp.zeros_like(carry)

        tri = (_iota((BLK, BLK), 0) >= _iota((BLK, BLK), 1)).astype(BF16)
        live = _iota((BLK, BLK), 1) < FOX_HEADS
        run = carry[...]
        for b in range(G):
            z = ff_ref[b * BLK:(b + 1) * BLK, :] + b_ref[...]
            lf = jnp.where(live, jnp.minimum(z, 0.0) - jnp.log1p(jnp.exp(-jnp.abs(z))), 0.0)
            hi, mid, lo = _split3(lf)
            cs = _dot(tri, hi) + _dot(tri, mid) + _dot(tri, lo) + run
            c_ref[b * BLK:(b + 1) * BLK, :] = cs
            ct_ref[b] = cs.T[0:8, :]
            run = run + jnp.sum(lf, axis=0, keepdims=True)
        carry[...] = run

    return pl.pallas_call(
        body, name="f_foxprep", grid=(nblk // G,),
        in_specs=[pl.BlockSpec((G * BLK, BLK), lambda i: (i, 0)), _full((1, BLK))],
        out_specs=[pl.BlockSpec((G * BLK, BLK), lambda i: (i, 0)), pl.BlockSpec((G, 8, BLK), lambda i: (i, 0, 0))],
        out_shape=[jax.ShapeDtypeStruct((L, BLK), F32), jax.ShapeDtypeStruct((nblk, 8, BLK), F32)],
        scratch_shapes=[pltpu.VMEM((1, BLK), F32)],
        compiler_params=_params(("arbitrary",)),
    )(ff, fb)


def _rot_fns(cos, sin):
    lane = _iota((BLK, BLK), 1)
    first = (lane & (HEAD_LANES - 1)) < HEAD_LANES // 2

    def swap(x):
        return jnp.where(first, pltpu.roll(x, BLK - 32, 1), pltpu.roll(x, 32, 1))

    def rot(x):
        return x * cos + swap(x) * sin

    def rot_t(dy):
        return dy * cos + swap(dy * sin)

    return rot, rot_t


def _retention_fwd(proj, cos_t, sin_t, ret_g):
    L = proj.shape[0]
    nblk = L // BLK
    dmat, wq_t, wk_t, g_blk = _decay_tables()

    def body(q_ref, k_ref, v_ref, gate_ref, cos_ref, sin_ref, d_ref, wq_ref, wk_ref, rg_ref,
             mix_ref, o_ref, rs_ref, state):
        i = pl.program_id(0)

        @pl.when(i == 0)
        def _():
            state[...] = jnp.zeros_like(state)

        rot, _ = _rot_fns(cos_ref[...], sin_ref[...])
        lane = _iota((BLK, BLK), 1)
        sub = _iota((BLK, BLK), 0)
        for p in range(2):
            qr = rot(q_ref[:, p * BLK:(p + 1) * BLK].astype(F32))
            kr = rot(k_ref[:, p * BLK:(p + 1) * BLK].astype(F32)) * (HEAD_LANES ** -0.5)
            kr_b = kr.astype(BF16)
            qw = (qr * wq_ref[p]).astype(BF16)
            kw = (kr * wk_ref[p]).astype(BF16)
            for e in range(2):
                h = 2 * p + e
                cols = slice(h * BLK, (h + 1) * BLK)
                qm = jnp.where((lane >> 6) == e, qr, 0.0).astype(BF16)
                s = _dot_nt(qm, kr_b) * d_ref[h]
                vh = v_ref[:, cols]
                st = state[h]
                rs_ref[0, h] = st
                o = _dot(s.astype(BF16), vh) + _dot(qw, st.astype(BF16))
                u = jnp.where((sub >> 6) == e, _dot_tn(kw, vh), 0.0)
                state[h] = g_blk[h] * st + u
                rn = lax.rsqrt(jnp.mean(o * o, axis=-1, keepdims=True) + EPS)
                gate = gate_ref[:, cols].astype(F32)
                o_ref[:, cols] = o
                mix_ref[:, cols] = (o * rn * rg_ref[:, cols] * (gate * jax.nn.sigmoid(gate))).astype(BF16)

    row = lambda c: (lambda i: (i, c))
    return pl.pallas_call(
        body, name="f_retention", grid=(nblk,),
        in_specs=[pl.BlockSpec((BLK, 256), row(0)), pl.BlockSpec((BLK, 256), row(1)),
                  pl.BlockSpec((BLK, 512), row(1)), pl.BlockSpec((BLK, 512), row(2)),
                  pl.BlockSpec((BLK, BLK), row(0)), pl.BlockSpec((BLK, BLK), row(0)),
                  _full((RET_HEADS, BLK, BLK)), _full((2, BLK, BLK)), _full((2, BLK, BLK)), _full((1, 512))],
        out_specs=[pl.BlockSpec((BLK, 512), row(0)), pl.BlockSpec((BLK, 512), row(0)),
                   pl.BlockSpec((1, RET_HEADS, BLK, BLK), lambda i: (i, 0, 0, 0))],
        out_shape=[jax.ShapeDtypeStruct((L, 512), BF16), jax.ShapeDtypeStruct((L, 512), F32),
                   jax.ShapeDtypeStruct((nblk, RET_HEADS, BLK, BLK), F32)],
        scratch_shapes=[pltpu.VMEM((RET_HEADS, BLK, BLK), F32)],
        compiler_params=_params(("arbitrary",)),
    )(proj, proj, proj, proj, cos_t, sin_t, dmat, wq_t, wk_t, ret_g)


def _fox_units(L):
    nblk = L // BLK
    assert L % BLK == 0 and nblk % 2 == 1, "sequence must be one 128-row block plus whole 256-row tiles"
    return nblk, (nblk - 1) // 2


def _fox_tile_masks():
    sub, lane = _iota((BLK, BLK), 0), _iota((BLK, BLK), 1)
    valid = _iota((BLK, UNIT), 0) >= N_PAD
    diag = _iota((UNIT, UNIT), 0) <= _iota((UNIT, UNIT), 1)
    r, q = _iota((BLK + UNIT, UNIT), 0), _iota((BLK + UNIT, UNIT), 1)
    first_and_diag = ((r < BLK) & (r >= N_PAD)) | ((r >= BLK) & (r - BLK <= q))
    return dict(first=(sub <= lane) & (sub >= N_PAD), valid=valid, diag=diag, first_and_diag=first_and_diag)


def _fox_fwd(proj, c, ctb, gather=()):
    L = proj.shape[0]
    nblk, nu = _fox_units(L)
    scale = HEAD_LANES ** -0.5
    ng = len(gather)

    def body(qkv_ref, c_ref, ct_ref, *rest):
        g_in, (of_ref, lse_ref), g_out = rest[:ng], rest[ng:ng + 2], rest[ng + 2:2 * ng + 2]
        vt, csb = rest[2 * ng + 2:2 * ng + 4]
        p = pl.program_id(0)

        @pl.when(p == 0)
        def _():
            lse_ref[...] = jnp.zeros_like(lse_ref)
            if ng:
                local, sends, _ = _allgather_copies(g_in, g_out, *rest[2 * ng + 4:])
                for cp in local + sends:
                    cp.start()

        lane = _iota((BLK, BLK), 1)
        sub8 = _iota((8, BLK), 0)
        masks = _fox_tile_masks()

        def pre(j, carry):
            off = pl.multiple_of(j * BLK, BLK)
            vt[j] = qkv_ref[pl.ds(off, BLK), 2 * BLK:3 * BLK].astype(F32).T.astype(BF16)
            ct = c_ref[pl.ds(off, BLK), :]
            for e in range(2):
                col = jnp.sum(jnp.where(lane == 2 * p + e, ct, 0.0), axis=1, keepdims=True)
                csb[e, j] = jnp.broadcast_to(col, (BLK, UNIT))
            return carry

        lax.fori_loop(0, nblk, pre, 0)

        def attend(qblk, nq, n_whole):
            qlen = nq * BLK
            qoff = pl.multiple_of(qblk * BLK, BLK)
            qs = qkv_ref[pl.ds(qoff, qlen), 0:BLK].astype(F32) * scale
            qlane = _iota((qlen, BLK), 1)
            qm = [jnp.where((qlane >> 6) == e, qs, 0.0).astype(BF16) for e in range(2)]
            ct_row = [jnp.concatenate([_pick_row(ct_ref[qblk + a], 2 * p + e) for a in range(nq)], axis=1)
                      for e in range(2)]

            def step(segs, mask, st):
                blocks = [kblk + b for kblk, nk in segs for b in range(nk)]
                kt = [qkv_ref[pl.ds(pl.multiple_of(kblk * BLK, BLK), nk * BLK), BLK:2 * BLK] for kblk, nk in segs]
                kt = kt[0] if len(kt) == 1 else jnp.concatenate(kt, axis=0)
                out = []
                for e in range(2):
                    m, l, acc = st[3 * e:3 * e + 3]
                    s = _dot_nt(kt, qm[e])
                    t = jnp.concatenate([s[b * BLK:(b + 1) * BLK] - csb[e, blk, :, 0:qlen]
                                         for b, blk in enumerate(blocks)], axis=0)
                    if mask is not None:
                        t = jnp.where(mask, t, NEG)
                    m_new = jnp.maximum(m, jnp.max(t, axis=0, keepdims=True) + ct_row[e])
                    alpha = jnp.exp(m - m_new)
                    pr = jnp.exp(t - (m_new - ct_row[e]))
                    l = alpha * l + jnp.sum(pr, axis=0, keepdims=True)
                    pr_b = pr.astype(BF16)
                    pv = None
                    for b, blk in enumerate(blocks):
                        part = _dot(vt[blk, e * HEAD_LANES:(e + 1) * HEAD_LANES, :], pr_b[b * BLK:(b + 1) * BLK])
                        pv = part if pv is None else pv + part
                    out += [m_new, l, alpha * acc + pv]
                return tuple(out)

            st = (jnp.full((1, qlen), NEG, F32), jnp.zeros((1, qlen), F32), jnp.zeros((HEAD_LANES, qlen), F32)) * 2
            if nq == 1:
                st = step([(0, 1)], masks["first"], st)
            else:
                st = step([(0, 1), (qblk, 2)], masks["first_and_diag"], st)
                n_wide = n_whole // WIDE
                st = lax.fori_loop(0, n_wide, lambda j, s_: step([(1 + 2 * WIDE * j, 2 * WIDE)], None, s_), st)
                rest = 1 + 2 * WIDE * n_wide
                st = lax.cond((n_whole & 2) != 0, lambda s_: step([(rest, 4)], None, s_), lambda s_: s_, st)
                st = lax.cond((n_whole & 1) != 0, lambda s_: step([(rest + 2 * (n_whole & 2), 2)], None, s_),
                              lambda s_: s_, st)
            o_t = jnp.concatenate([st[2] * (1.0 / st[1]), st[5] * (1.0 / st[4])], axis=0)
            of_ref[pl.ds(qoff, qlen), :] = o_t.T.astype(BF16)
            lse = [st[3 * e] + jnp.log(st[3 * e + 1]) for e in range(2)]
            for a in range(nq):
                rows = [lse[e][:, a * BLK:(a + 1) * BLK] for e in range(2)]
                lse_ref[qblk + a] = lse_ref[qblk + a] + (
                    jnp.where(sub8 == 2 * p, rows[0], 0.0) + jnp.where(sub8 == 2 * p + 1, rows[1], 0.0))

        attend(0, 1, 0)

        def q_loop(u, carry):
            attend(1 + 2 * u, 2, u)
            return carry

        lax.fori_loop(0, nu, q_loop, 0)

        if ng:
            @pl.when(p == FOX_HEADS // 2 - 1)
            def _():
                local, sends, recvs = _allgather_copies(g_in, g_out, *rest[2 * ng + 4:])
                for cp in recvs:
                    cp.wait_recv()
                for cp in sends:
                    cp.wait_send()
                for cp in local:
                    cp.wait()

    return pl.pallas_call(
        body, name="f_fox", grid=(FOX_HEADS // 2,),
        in_specs=[pl.BlockSpec((L, 384), lambda p: (0, RET_W // 384 + p)), _full((L, BLK)), _full((nblk, 8, BLK))]
        + [_ANY] * ng,
        out_specs=[pl.BlockSpec((L, BLK), lambda p: (0, p)), _full((nblk, 8, BLK))] + [_ANY] * ng,
        out_shape=[jax.ShapeDtypeStruct((L, 512), BF16), jax.ShapeDtypeStruct((nblk, 8, BLK), F32)]
        + [jax.ShapeDtypeStruct((N_CHIPS,) + a.shape, a.dtype) for a in gather],
        scratch_shapes=[pltpu.VMEM((nblk, BLK, BLK), BF16), pltpu.VMEM((2, nblk, BLK, UNIT), F32)]
        + _allgather_semaphores(ng),
        compiler_params=_params(("arbitrary",)),
    )(proj, c, ctb, *gather)


def _outproj_up(mix_r, o_f, h0, w_out, ffn_g, w_up):
    L = h0.shape[0]
    tm = _row_tile(L)
    shard = w_up.shape[2]

    def body(mr_ref, of_ref, h0_ref, wo_ref, g_ref, wu_ref, h1_ref, n2_ref, up_ref):
        h1 = h0_ref[...] + _dot(mr_ref[...], wo_ref[0:512, :]) + _dot(of_ref[...], wo_ref[512:1024, :])
        h1_ref[...] = h1
        r = lax.rsqrt(jnp.mean(h1 * h1, axis=-1, keepdims=True) + EPS)
        n2 = (h1 * r * g_ref[...]).astype(BF16)
        n2_ref[...] = n2
        for j in range(N_CHIPS):
            up_ref[:, j * shard:(j + 1) * shard] = _dot(n2, wu_ref[j]).astype(BF16)

    rows = lambda w: pl.BlockSpec((tm, w), lambda i: (i, 0))
    return pl.pallas_call(
        body, name="f_outproj_up", grid=(L // tm,),
        in_specs=[rows(512), rows(512), rows(D_MODEL), _full((D_MODEL, D_MODEL)), _full((1, D_MODEL)),
                  _full((N_CHIPS, D_MODEL, shard))],
        out_specs=[rows(D_MODEL), rows(D_MODEL), rows(2 * D_FF)],
        out_shape=[jax.ShapeDtypeStruct((L, D_MODEL), F32), jax.ShapeDtypeStruct((L, D_MODEL), BF16),
                   jax.ShapeDtypeStruct((L, 2 * D_FF), BF16)],
        compiler_params=_params(("parallel",)),
    )(mix_r, o_f, h0, w_out, ffn_g, w_up)


def _conv_acc(a_ref, halo_ref, cw_refs, cb_ref, i, tm):
    sub = _iota((tm, 1), 0)
    a = jnp.where(i * tm + sub >= N_PAD, a_ref[...].astype(F32), 0.0)
    halo = halo_ref[...].astype(F32)
    hrow = i * tm - 8 + _iota((8, 1), 0)
    halo = jnp.where((hrow >= N_PAD) & (i > 0), halo, 0.0)
    a1 = jnp.where(sub == 0, _pick_row(halo, 7), pltpu.roll(a, 1, 0))
    a2 = jnp.where(sub == 0, _pick_row(halo, 6), jnp.where(sub == 1, _pick_row(halo, 7), pltpu.roll(a, 2, 0)))
    acc = cb_ref[...] + a2 * cw_refs[0][...]
    acc = acc + a1 * cw_refs[1][...]
    acc = acc + a * cw_refs[2][...]
    return a, a1, a2, acc


def _ffn_down_loss(up, conv_w, conv_b, w_down, h1, final_g, target):
    L = h1.shape[0]
    tm = _row_tile(L)
    cw = [conv_w[j:j + 1] for j in range(3)]

    def body(a_ref, halo_ref, b_ref, cw0, cw1, cw2, cb_ref, wd_ref, h1_ref, gf_ref, t_ref,
             g_ref, dh_ref, dhb_ref, dgf_ref, loss_ref):
        i = pl.program_id(0)

        @pl.when(i == 0)
        def _():
            dgf_ref[...] = jnp.zeros_like(dgf_ref)
            loss_ref[...] = jnp.zeros_like(loss_ref)

        _, _, _, acc = _conv_acc(a_ref, halo_ref, (cw0, cw1, cw2), cb_ref, i, tm)
        g = (acc * jax.nn.sigmoid(acc) * b_ref[...].astype(F32)).astype(BF16)
        g_ref[...] = g
        h2 = h1_ref[...] + _dot(g, wd_ref[...])
        r = lax.rsqrt(jnp.mean(h2 * h2, axis=-1, keepdims=True) + EPS)
        yn = h2 * r
        gf = gf_ref[...]
        live = i * tm + _iota((tm, 1), 0) >= PREFIX
        err = jnp.where(live, yn * gf - t_ref[...], 0.0)
        loss_ref[...] = loss_ref[...] + 0.5 * jnp.sum(jnp.mean(err * err, axis=-1, keepdims=True))
        dy = err * (1.0 / D_MODEL)
        dgf_ref[...] = dgf_ref[...] + jnp.sum(dy * yn, axis=0, keepdims=True)
        dyn = dy * gf
        dh = r * (dyn - yn * jnp.mean(dyn * yn, axis=-1, keepdims=True))
        dh_ref[...] = dh
        dhb_ref[...] = dh.astype(BF16)

    rows = lambda w, c=0: pl.BlockSpec((tm, w), lambda i: (i, c))
    halo = pl.BlockSpec((8, D_FF), lambda i: (jnp.maximum(i * (tm // 8) - 1, 0), 0))
    return pl.pallas_call(
        body, name="f_ffn_down_loss", grid=(L // tm,),
        in_specs=[rows(D_FF), halo, rows(D_FF, 1), _full((1, D_FF)), _full((1, D_FF)), _full((1, D_FF)),
                  _full((1, D_FF)), _full((D_FF, D_MODEL)), rows(D_MODEL), _full((1, D_MODEL)), rows(D_MODEL)],
        out_specs=[rows(D_FF), rows(D_MODEL), rows(D_MODEL), _full((1, D_MODEL)), _full((1, BLK))],
        out_shape=[jax.ShapeDtypeStruct((L, D_FF), BF16), jax.ShapeDtypeStruct((L, D_MODEL), F32),
                   jax.ShapeDtypeStruct((L, D_MODEL), BF16), jax.ShapeDtypeStruct((1, D_MODEL), F32),
                   jax.ShapeDtypeStruct((1, BLK), F32)],
        compiler_params=_params(("arbitrary",)),
    )(up, up, up, cw[0], cw[1], cw[2], conv_b, w_down, h1, final_g, target)


def _ffn_bwd_gate(dh2b, w_down, up, conv_w, conv_b):
    L = dh2b.shape[0]
    tm = _row_tile(L)
    cw = [conv_w[j:j + 1] for j in range(3)]

    def body(dh_ref, wd_ref, a_ref, halo_ref, b_ref, cw0, cw1, cw2, cb_ref, dacc_ref, db_ref, dcw_ref):
        i = pl.program_id(0)

        @pl.when(i == 0)
        def _():
            dcw_ref[...] = jnp.zeros_like(dcw_ref)

        a, a1, a2, acc = _conv_acc(a_ref, halo_ref, (cw0, cw1, cw2), cb_ref, i, tm)
        dg = _dot_nt(dh_ref[...], wd_ref[...])
        sg = jax.nn.sigmoid(acc)
        db_ref[...] = (dg * acc * sg).astype(BF16)
        dacc = dg * b_ref[...].astype(F32) * (sg * (1.0 + acc * (1.0 - sg)))
        dacc_ref[...] = dacc.astype(BF16)
        sub8 = _iota((8, 1), 0)
        rows = [jnp.sum(dacc * t, axis=0, keepdims=True) for t in (a2, a1, a)] + [jnp.sum(dacc, axis=0, keepdims=True)]
        upd = jnp.zeros((8, D_FF), F32)
        for j, rj in enumerate(rows):
            upd = upd + jnp.where(sub8 == j, rj, 0.0)
        dcw_ref[...] = dcw_ref[...] + upd

    rows = lambda w, c=0: pl.BlockSpec((tm, w), lambda i: (i, c))
    halo = pl.BlockSpec((8, D_FF), lambda i: (jnp.maximum(i * (tm // 8) - 1, 0), 0))
    return pl.pallas_call(
        body, name="b_ffn_gate", grid=(L // tm,),
        in_specs=[rows(D_MODEL), _full((D_FF, D_MODEL)), rows(D_FF), halo, rows(D_FF, 1),
                  _full((1, D_FF)), _full((1, D_FF)), _full((1, D_FF)), _full((1, D_FF))],
        out_specs=[rows(D_FF), rows(D_FF), _full((8, D_FF))],
        out_shape=[jax.ShapeDtypeStruct((L, D_FF), BF16), jax.ShapeDtypeStruct((L, D_FF), BF16),
                   jax.ShapeDtypeStruct((8, D_FF), F32)],
        compiler_params=_params(("arbitrary",)),
    )(dh2b, w_down, up, up, up, cw[0], cw[1], cw[2], conv_b)


def _ffn_bwd_up(dacc, db, conv_w, w_up, h1, ffn_g, dh2, w_out):
    L = h1.shape[0]
    tm = _row_tile(L)
    nt = L // tm
    shard = w_up.shape[2]
    cw = [conv_w[j:j + 1] for j in range(3)]

    def body(da_ref, halo_ref, db_ref, cw0, cw1, cw2, wu_ref, h1_ref, g_ref, dh2_ref, wo_ref,
             dup_ref, dh1_ref, dh1b_ref, dmix_ref, dg_ref):
        i = pl.program_id(0)

        @pl.when(i == 0)
        def _():
            dg_ref[...] = jnp.zeros_like(dg_ref)

        sub = _iota((tm, 1), 0)
        d0 = da_ref[...].astype(F32)
        halo = jnp.where(i < nt - 1, halo_ref[...].astype(F32), 0.0)
        d1 = jnp.where(sub == tm - 1, _pick_row(halo, 0), pltpu.roll(d0, tm - 1, 0))
        d2 = jnp.where(sub == tm - 2, _pick_row(halo, 0),
                       jnp.where(sub == tm - 1, _pick_row(halo, 1), pltpu.roll(d0, tm - 2, 0)))
        da = d0 * cw2[...] + d1 * cw1[...] + d2 * cw0[...]
        da = jnp.where(i * tm + sub >= N_PAD, da, 0.0).astype(BF16)
        dup_ref[:, 0:D_FF] = da
        dbv = db_ref[...]
        dup_ref[:, D_FF:2 * D_FF] = dbv
        dn = jnp.zeros((tm, D_MODEL), F32)
        for j in range(N_CHIPS):
            src = da if j < 2 else dbv
            lo = (j % 2) * shard
            dn = dn + _dot_nt(src[:, lo:lo + shard], wu_ref[j])
        h1 = h1_ref[...]
        r = lax.rsqrt(jnp.mean(h1 * h1, axis=-1, keepdims=True) + EPS)
        yn = h1 * r
        dg_ref[...] = dg_ref[...] + jnp.sum(dn * yn, axis=0, keepdims=True)
        dyn = dn * g_ref[...]
        dh1 = dh2_ref[...] + r * (dyn - yn * jnp.mean(dyn * yn, axis=-1, keepdims=True))
        dh1_ref[...] = dh1
        dh1b = dh1.astype(BF16)
        dh1b_ref[...] = dh1b
        dmix_ref[...] = _dot_nt(dh1b, wo_ref[...]).astype(BF16)

    rows = lambda w: pl.BlockSpec((tm, w), lambda i: (i, 0))
    halo = pl.BlockSpec((8, D_FF), lambda i: (jnp.minimum((i + 1) * (tm // 8), L // 8 - 1), 0))
    return pl.pallas_call(
        body, name="b_ffn_up", grid=(nt,),
        in_specs=[rows(D_FF), halo, rows(D_FF), _full((1, D_FF)), _full((1, D_FF)), _full((1, D_FF)),
                  _full((N_CHIPS, D_MODEL, shard)), rows(D_MODEL), _full((1, D_MODEL)), rows(D_MODEL),
                  _full((D_MODEL, D_MODEL))],
        out_specs=[rows(2 * D_FF), rows(D_MODEL), rows(D_MODEL), rows(D_MODEL), _full((1, D_MODEL))],
        out_shape=[jax.ShapeDtypeStruct((L, 2 * D_FF), BF16), jax.ShapeDtypeStruct((L, D_MODEL), F32),
                   jax.ShapeDtypeStruct((L, D_MODEL), BF16), jax.ShapeDtypeStruct((L, D_MODEL), BF16),
                   jax.ShapeDtypeStruct((1, D_MODEL), F32)],
        compiler_params=_params(("arbitrary",)),
    )(dacc, dacc, db, cw[0], cw[1], cw[2], w_up, h1, ffn_g, dh2, w_out)


def _wgrad(a, b, name, tn=None, tk=None):
    L, K = a.shape
    N = b.shape[1]
    tn = N if tn is None else tn
    tk = K if tk is None else tk
    tl = _row_tile(L, (1408, 768, 512, 256, 128))

    def body(a_ref, b_ref, o_ref):
        @pl.when(pl.program_id(2) == 0)
        def _():
            o_ref[...] = jnp.zeros_like(o_ref)

        o_ref[0] = o_ref[0] + _dot_tn(a_ref[...], b_ref[...])

    return pl.pallas_call(
        body, name=name, grid=(N // tn, K // tk, L // tl),
        in_specs=[pl.BlockSpec((tl, tk), lambda n, k, l: (l, k)), pl.BlockSpec((tl, tn), lambda n, k, l: (l, n))],
        out_specs=pl.BlockSpec((1, tk, tn), lambda n, k, l: (n, k, 0)),
        out_shape=jax.ShapeDtypeStruct((N // tn, K, tn), F32),
        compiler_params=_params(("parallel", "parallel", "arbitrary")),
    )(a, b)


def _retention_bwd(dmix, o, proj, cos_t, sin_t, ret_g, states):
    L = proj.shape[0]
    nblk = L // BLK
    dmat, wq_t, wk_t, g_blk = _decay_tables()

    def body(dm_ref, o_ref, q_ref, k_ref, v_ref, gate_ref, cos_ref, sin_ref, d_ref, wq_ref, wk_ref, rg_ref, rs_ref,
             dp_ref, drg_ref, gstate):
        i = pl.program_id(0)

        @pl.when(i == 0)
        def _():
            gstate[...] = jnp.zeros_like(gstate)
            drg_ref[...] = jnp.zeros_like(drg_ref)

        rot, rot_t = _rot_fns(cos_ref[...], sin_ref[...])
        lane = _iota((BLK, BLK), 1)
        sub = _iota((BLK, BLK), 0)
        scale = HEAD_LANES ** -0.5
        for p in range(2):
            qr = rot(q_ref[:, p * BLK:(p + 1) * BLK].astype(F32))
            kr = rot(k_ref[:, p * BLK:(p + 1) * BLK].astype(F32)) * scale
            kr_b = kr.astype(BF16)
            qw = (qr * wq_ref[p]).astype(BF16)
            kw = (kr * wk_ref[p]).astype(BF16)
            dqr = jnp.zeros((BLK, BLK), F32)
            dkr = jnp.zeros((BLK, BLK), F32)
            for e in range(2):
                h = 2 * p + e
                cols = slice(h * BLK, (h + 1) * BLK)
                head_lanes = (lane >> 6) == e
                o = o_ref[:, cols]
                rn = lax.rsqrt(jnp.mean(o * o, axis=-1, keepdims=True) + EPS)
                y = o * rn
                gate = gate_ref[:, cols].astype(F32)
                sg = jax.nn.sigmoid(gate)
                dm = dm_ref[:, cols].astype(F32)
                rgain = rg_ref[:, cols]
                drg_ref[:, cols] = drg_ref[:, cols] + jnp.sum(dm * y * (gate * sg), axis=0, keepdims=True)
                dp_ref[:, 1024 + h * BLK:1024 + (h + 1) * BLK] = (
                    dm * y * rgain * (sg * (1.0 + gate * (1.0 - sg)))).astype(BF16)
                dy = dm * rgain * (gate * sg)
                do = (rn * (dy - y * jnp.mean(dy * y, axis=-1, keepdims=True))).astype(BF16)
                vh = v_ref[:, cols]
                qm = jnp.where(head_lanes, qr, 0.0).astype(BF16)
                dmh = d_ref[h]
                s = (_dot_nt(qm, kr_b) * dmh).astype(BF16)
                ds = (_dot_nt(do, vh) * dmh).astype(BF16)
                st = rs_ref[0, h].astype(BF16)
                gs = gstate[h]
                gs_b = gs.astype(BF16)
                dqr = dqr + jnp.where(head_lanes, _dot(ds, kr_b), 0.0) + _dot_nt(do, st) * wq_ref[p]
                dkr = dkr + _dot_tn(ds, qm) + _dot_nt(vh, gs_b) * wk_ref[p]
                dp_ref[:, 512 + h * BLK:512 + (h + 1) * BLK] = (_dot_tn(s, do) + _dot(kw, gs_b)).astype(BF16)
                dr = jnp.where((sub >> 6) == e, _dot_tn(qw, do), 0.0)
                gstate[h] = dr + g_blk[h] * gs
            dp_ref[:, p * BLK:(p + 1) * BLK] = rot_t(dqr).astype(BF16)
            dp_ref[:, 256 + p * BLK:256 + (p + 1) * BLK] = (rot_t(dkr) * scale).astype(BF16)

    row = lambda c: (lambda i: (nblk - 1 - i, c))
    return pl.pallas_call(
        body, name="b_retention", grid=(nblk,),
        in_specs=[pl.BlockSpec((BLK, 512), row(0)), pl.BlockSpec((BLK, 512), row(0)),
                  pl.BlockSpec((BLK, 256), row(0)), pl.BlockSpec((BLK, 256), row(1)),
                  pl.BlockSpec((BLK, 512), row(1)), pl.BlockSpec((BLK, 512), row(2)),
                  pl.BlockSpec((BLK, BLK), row(0)), pl.BlockSpec((BLK, BLK), row(0)),
                  _full((RET_HEADS, BLK, BLK)), _full((2, BLK, BLK)), _full((2, BLK, BLK)), _full((1, 512)),
                  pl.BlockSpec((1, RET_HEADS, BLK, BLK), lambda i: (nblk - 1 - i, 0, 0, 0))],
        out_specs=[pl.BlockSpec((BLK, RET_W), row(0)), _full((1, 512))],
        out_shape=[jax.ShapeDtypeStruct((L, RET_W), BF16), jax.ShapeDtypeStruct((1, 512), F32)],
        scratch_shapes=[pltpu.VMEM((RET_HEADS, BLK, BLK), F32)],
        compiler_params=_params(("arbitrary",)),
    )(dmix, o, proj, proj, proj, proj, cos_t, sin_t, dmat, wq_t, wk_t, ret_g, states)


def _fox_delta(dmix, o_f):
    L = o_f.shape[0]
    nblk = L // BLK
    G = _block_group(nblk)

    def body(do_ref, o_ref, d_ref):
        sel = ((_iota((8, 512), 1) >> 6) == _iota((8, 512), 0)).astype(BF16)
        for b in range(G):
            rows = slice(b * BLK, (b + 1) * BLK)
            prod = do_ref[rows, :].astype(F32) * o_ref[rows, :].astype(F32)
            hi = prod.astype(BF16)
            lo = (prod - hi.astype(F32)).astype(BF16)
            d_ref[b] = _dot_nt(sel, hi) + _dot_nt(sel, lo)

    return pl.pallas_call(
        body, name="b_foxdelta", grid=(nblk // G,),
        in_specs=[pl.BlockSpec((G * BLK, 512), lambda i: (i, 1)), pl.BlockSpec((G * BLK, 512), lambda i: (i, 0))],
        out_specs=pl.BlockSpec((G, 8, BLK), lambda i: (i, 0, 0)),
        out_shape=jax.ShapeDtypeStruct((nblk, 8, BLK), F32),
        compiler_params=_params(("parallel",)),
    )(dmix, o_f)


def _fox_bwd(proj, dmix, c, ctb, lse, delta, scatter=()):
    L = proj.shape[0]
    nblk, nu = _fox_units(L)
    scale = HEAD_LANES ** -0.5
    ns = len(scatter)

    def body(qkv_ref, do_ref, c_ref, ct_ref, lse_ref, dl_ref, *rest):
        s_in, (dp_ref, dc_ref, dcq_ref), s_out = rest[:ns], rest[ns:ns + 3], rest[ns + 3:2 * ns + 3]
        ktt, dqt, dk_acc, dv_acc, dcs_acc = rest[2 * ns + 3:2 * ns + 8]
        p = pl.program_id(0)

        @pl.when(p == 0)
        def _():
            dc_ref[...] = jnp.zeros_like(dc_ref)
            dcq_ref[...] = jnp.zeros_like(dcq_ref)
            if ns:
                for cp in _scatter_copies(s_in, s_out, *rest[2 * ns + 8:]):
                    cp.start()

        lane = _iota((BLK, BLK), 1)
        sub8 = _iota((8, BLK), 0)
        masks = _fox_tile_masks()

        def pre(j, carry):
            off = pl.multiple_of(j * BLK, BLK)
            ktt[j] = qkv_ref[pl.ds(off, BLK), BLK:2 * BLK].astype(F32).T.astype(BF16)
            dqt[j] = jnp.zeros((BLK, BLK), F32)
            return carry

        lax.fori_loop(0, nblk, pre, 0)

        def kv_pass(kblk, nk, n_later):
            klen = nk * BLK
            koff = pl.multiple_of(kblk * BLK, BLK)
            kt = qkv_ref[pl.ds(koff, klen), BLK:2 * BLK]
            vtile = qkv_ref[pl.ds(koff, klen), 2 * BLK:3 * BLK]
            ct = c_ref[pl.ds(koff, klen), :]
            klane = _iota((klen, BLK), 1)
            cs = [jnp.broadcast_to(jnp.sum(jnp.where(klane == 2 * p + e, ct, 0.0), axis=1, keepdims=True),
                                   (klen, WIDE * UNIT)) for e in range(2)]
            dk_acc[0:klen] = jnp.zeros((klen, BLK), F32)
            dv_acc[0:klen] = jnp.zeros((klen, BLK), F32)
            for e in range(2):
                dcs_acc[e, 0:klen] = jnp.zeros((klen, BLK), F32)

            def tile(qblk, nq, mask):
                qlen = nq * BLK
                if mask == "valid":
                    mask = _iota((klen, qlen), 0) >= N_PAD
                qoff = pl.multiple_of(qblk * BLK, BLK)
                qs = qkv_ref[pl.ds(qoff, qlen), 0:BLK].astype(F32) * scale
                dot_ = do_ref[pl.ds(qoff, qlen), :]
                qlane = _iota((qlen, BLK), 1)
                stats = [[ref[qblk + a] for a in range(nq)] for ref in (ct_ref, lse_ref, dl_ref)]
                for e in range(2):
                    h = 2 * p + e
                    head = (qlane >> 6) == e
                    ct_row, lse_row, dl_row = [jnp.concatenate([_pick_row(t, h) for t in ts], axis=1) for ts in stats]
                    qm = jnp.where(head, qs, 0.0).astype(BF16)
                    dom = jnp.where(head, dot_, jnp.zeros_like(dot_))
                    t = _dot_nt(kt, qm) - cs[e][:, 0:qlen]
                    if mask is not None:
                        t = jnp.where(mask, t, NEG)
                    pr = jnp.exp(t + (ct_row - lse_row))
                    dv_acc[0:klen] = dv_acc[0:klen] + _dot(pr.astype(BF16), dom)
                    dsv = pr * (_dot_nt(vtile, dom) - dl_row)
                    ds_b = dsv.astype(BF16)
                    dk_acc[0:klen] = dk_acc[0:klen] + _dot(ds_b, qm)
                    rows = slice(e * HEAD_LANES, (e + 1) * HEAD_LANES)
                    dq_t = _dot(ktt[kblk, rows, :], ds_b[0:BLK])
                    for b in range(1, nk):
                        dq_t = dq_t + _dot(ktt[kblk + b, rows, :], ds_b[b * BLK:(b + 1) * BLK])
                    key_side = dsv[:, 0:BLK]
                    for a in range(1, nq):
                        key_side = key_side + dsv[:, a * BLK:(a + 1) * BLK]
                    dcs_acc[e, 0:klen] = dcs_acc[e, 0:klen] + key_side
                    query_side = jnp.sum(dsv, axis=0, keepdims=True)
                    for a in range(nq):
                        cols = slice(a * BLK, (a + 1) * BLK)
                        dqt[qblk + a, rows, :] = dqt[qblk + a, rows, :] + dq_t[:, cols]
                        dcq_ref[qblk + a] = dcq_ref[qblk + a] + jnp.where(sub8 == h, query_side[:, cols], 0.0)

            later_mask = "valid" if nk == 1 else None
            n_later = jnp.asarray(n_later, jnp.int32)
            n_wide = n_later // WIDE

            def later_wide(i, carry):
                tile(kblk + nk + 2 * WIDE * i, 2 * WIDE, later_mask)
                return carry

            tile(kblk, nk, masks["first"] if nk == 1 else masks["diag"])
            lax.fori_loop(0, n_wide, later_wide, 0)
            rest = kblk + nk + 2 * WIDE * n_wide

            @pl.when((n_later & 2) != 0)
            def _():
                tile(rest, 4, later_mask)

            @pl.when((n_later & 1) != 0)
            def _():
                tile(rest + 2 * (n_later & 2), 2, later_mask)
            dp_ref[pl.ds(koff, klen), BLK:2 * BLK] = dk_acc[0:klen].astype(BF16)
            dp_ref[pl.ds(koff, klen), 2 * BLK:3 * BLK] = dv_acc[0:klen].astype(BF16)
            upd = jnp.zeros((klen, BLK), F32)
            for e in range(2):
                upd = upd + jnp.where(klane == 2 * p + e, -jnp.sum(dcs_acc[e, 0:klen], axis=1, keepdims=True), 0.0)
            dc_ref[pl.ds(koff, klen), :] = dc_ref[pl.ds(koff, klen), :] + upd

        kv_pass(0, 1, nu)

        def k_loop(u, carry):
            kv_pass(1 + 2 * u, 2, nu - 1 - u)
            return carry

        lax.fori_loop(0, nu, k_loop, 0)

        def flush(j, carry):
            off = pl.multiple_of(j * BLK, BLK)
            dp_ref[pl.ds(off, BLK), 0:BLK] = (dqt[j].T * scale).astype(BF16)
            return carry

        lax.fori_loop(0, nblk, flush, 0)

        if ns:
            @pl.when(p == FOX_HEADS // 2 - 1)
            def _():
                copies = _scatter_copies(s_in, s_out, *rest[2 * ns + 8:])
                for cp in copies:
                    cp.wait_recv()
                for cp in copies:
                    cp.wait_send()

    stat = _full((nblk, 8, BLK))
    return pl.pallas_call(
        body, name="b_fox", grid=(FOX_HEADS // 2,),
        in_specs=[pl.BlockSpec((L, 384), lambda p: (0, RET_W // 384 + p)), pl.BlockSpec((L, BLK), lambda p: (0, 4 + p)),
                  _full((L, BLK)), stat, stat, stat] + [_ANY] * ns,
        out_specs=[pl.BlockSpec((L, 384), lambda p: (0, p)), _full((L, BLK)), stat] + [_ANY] * ns,
        out_shape=[jax.ShapeDtypeStruct((L, FOX_W), BF16), jax.ShapeDtypeStruct((L, BLK), F32),
                   jax.ShapeDtypeStruct((nblk, 8, BLK), F32)] + _scatter_shapes(scatter),
        scratch_shapes=[pltpu.VMEM((nblk, BLK, BLK), BF16), pltpu.VMEM((nblk, BLK, BLK), F32),
                        pltpu.VMEM((UNIT, BLK), F32), pltpu.VMEM((UNIT, BLK), F32), pltpu.VMEM((2, UNIT, BLK), F32)]
        + _scatter_semaphores(ns),
        compiler_params=_params(("arbitrary",)),
    )(proj, dmix, c, ctb, lse, delta, *scatter)


def _fox_post(dc, dcq, ff, fb):
    L = dc.shape[0]
    nblk = L // BLK
    G = _block_group(nblk)
    steps = nblk // G

    def body(dc_ref, dcq_ref, ff_ref, b_ref, dff_ref, dffb_ref, dfb_ref, carry):
        @pl.when(pl.program_id(0) == 0)
        def _():
            carry[...] = jnp.zeros_like(carry)
            dfb_ref[...] = jnp.zeros_like(dfb_ref)

        tri = (_iota((BLK, BLK), 0) <= _iota((BLK, BLK), 1)).astype(BF16)
        live = _iota((BLK, BLK), 1) < FOX_HEADS
        run, dfb = carry[...], dfb_ref[...]
        for b in reversed(range(G)):
            rows = slice(b * BLK, (b + 1) * BLK)
            d = dc_ref[rows, :] + jnp.concatenate([dcq_ref[b], jnp.zeros((BLK - 8, BLK), F32)], axis=0).T
            hi, mid, lo = _split3(d)
            dlf = _dot(tri, hi) + _dot(tri, mid) + _dot(tri, lo) + run
            run = run + jnp.sum(d, axis=0, keepdims=True)
            z = ff_ref[rows, :] + b_ref[...]
            dff = jnp.where(live, dlf * jax.nn.sigmoid(-z), 0.0)
            dff_ref[rows, :] = dff
            dffb_ref[rows, :] = dff.astype(BF16)
            dfb = dfb + jnp.sum(dff, axis=0, keepdims=True)
        carry[...] = run
        dfb_ref[...] = dfb

    rev = lambda i: (steps - 1 - i, 0)
    return pl.pallas_call(
        body, name="b_foxpost", grid=(steps,),
        in_specs=[pl.BlockSpec((G * BLK, BLK), rev), pl.BlockSpec((G, 8, BLK), lambda i: (steps - 1 - i, 0, 0)),
                  pl.BlockSpec((G * BLK, BLK), rev), _full((1, BLK))],
        out_specs=[pl.BlockSpec((G * BLK, BLK), rev), pl.BlockSpec((G * BLK, BLK), rev), _full((1, BLK))],
        out_shape=[jax.ShapeDtypeStruct((L, BLK), F32), jax.ShapeDtypeStruct((L, BLK), BF16),
                   jax.ShapeDtypeStruct((1, BLK), F32)],
        scratch_shapes=[pltpu.VMEM((1, BLK), F32)],
        compiler_params=_params(("arbitrary",)),
    )(dc, dcq, ff, fb)


def _inproj_bwd(dpr, dpf, dffb, w_main, w_ff, h0, g, dh1):
    L = h0.shape[0]
    tm = _row_tile(L)

    def body(dpr_ref, dpf_ref, dff_ref, wm_ref, wf_ref, h_ref, g_ref, dh1_ref, dh0_ref, dg_ref):
        @pl.when(pl.program_id(0) == 0)
        def _():
            dg_ref[...] = jnp.zeros_like(dg_ref)

        dn = (_dot_nt(dpr_ref[...], wm_ref[:, 0:RET_W]) + _dot_nt(dpf_ref[...], wm_ref[:, RET_W:MAIN_W])
              + _dot_nt(dff_ref[...], wf_ref[...]))
        h = h_ref[...]
        r = lax.rsqrt(jnp.mean(h * h, axis=-1, keepdims=True) + EPS)
        yn = h * r
        dg_ref[...] = dg_ref[...] + jnp.sum(dn * yn, axis=0, keepdims=True)
        dyn = dn * g_ref[...]
        dh0_ref[...] = dh1_ref[...] + r * (dyn - yn * jnp.mean(dyn * yn, axis=-1, keepdims=True))

    rows = lambda w: pl.BlockSpec((tm, w), lambda i: (i, 0))
    return pl.pallas_call(
        body, name="b_inproj", grid=(L // tm,),
        in_specs=[rows(RET_W), rows(FOX_W), rows(BLK), _full((D_MODEL, MAIN_W)), _full((D_MODEL, BLK)),
                  rows(D_MODEL), _full((1, D_MODEL)), rows(D_MODEL)],
        out_specs=[rows(D_MODEL), _full((1, D_MODEL))],
        out_shape=[jax.ShapeDtypeStruct((L, D_MODEL), F32), jax.ShapeDtypeStruct((1, D_MODEL), F32)],
        compiler_params=_params(("arbitrary",)),
    )(dpr, dpf, dffb, w_main, w_ff, h0, g, dh1)


def _local_step(x, target, meta, attn_g, w_main, w_ff, fox_b, ret_g, w_out, ffn_g, w_up, conv_w, conv_b, w_down, final_g,
                late=None, mid=None):
    S = x.shape[0]
    L = S + PREFIX
    h0 = jnp.concatenate([jnp.zeros((N_PAD, D_MODEL), F32), meta, x], axis=0)
    tgt = jnp.concatenate([jnp.zeros((PREFIX, D_MODEL), F32), target], axis=0)
    fb = jnp.pad(fox_b, ((0, 0), (0, BLK - FOX_HEADS)))
    cos_t, sin_t = _rotary_tables(L)

    n1, proj, ff = _rms_inproj(h0, attn_g, w_main, w_ff)
    c, ctb = _fox_prep(ff, fb)
    mix_r, o_ret, states = _retention_fwd(proj, cos_t, sin_t, ret_g)
    if late is None:
        o_f, lse = _fox_fwd(proj, c, ctb)
    else:
        o_f, lse, *gathered = _fox_fwd(proj, c, ctb, gather=late[0])
        w_out, w_up, w_down = late[1](gathered)
    h1, n2, up = _outproj_up(mix_r, o_f, h0, w_out, ffn_g, w_up)
    g_act, dh2, dh2b, d_final_g, loss = _ffn_down_loss(up, conv_w, conv_b, w_down, h1, final_g, tgt)

    dacc, db, dconv = _ffn_bwd_gate(dh2b, w_down, up, conv_w, conv_b)
    dup, dh1, dh1b, dmix, d_ffn_g = _ffn_bwd_up(dacc, db, conv_w, w_up, h1, ffn_g, dh2, w_out)
    d_w_down = _wgrad(g_act, dh2b, "wgrad_down", tk=D_FF // 2)[0]
    d_w_up = _wgrad(n2, dup, "wgrad_up", tn=w_up.shape[2])
    d_w_out = jnp.concatenate([_wgrad(mix_r, dh1b, "wgrad_out_r")[0], _wgrad(o_f, dh1b, "wgrad_out_f")[0]], axis=0)

    dpr, d_ret_g = _retention_bwd(dmix, o_ret, proj, cos_t, sin_t, ret_g, states)
    delta = _fox_delta(dmix, o_f)
    scatter = () if mid is None else mid(d_w_out, d_w_up, d_w_down)
    dpf, dc, dcq, *received = _fox_bwd(proj, dmix, c, ctb, lse, delta, scatter=scatter)
    dff, dffb, d_fox_b = _fox_post(dc, dcq, ff, fb)
    dh0, d_attn_g = _inproj_bwd(dpr, dpf, dffb, w_main, w_ff, h0, attn_g, dh1)
    d_w_main = jnp.concatenate([_wgrad(n1, dpr, "wgrad_in_r")[0], _wgrad(n1, dpf, "wgrad_in_f")[0]], axis=1)
    d_w_ff = _wgrad(n1, dffb, "wgrad_in_ff")[0]

    return dict(
        loss=loss[0, 0], dx=dh0[PREFIX:], dmeta=dh0[N_PAD:PREFIX], attn_g=d_attn_g, w_main=d_w_main,
        w_ff=d_w_ff[:, :FOX_HEADS], fox_b=d_fox_b[:, :FOX_HEADS], ret_g=d_ret_g, w_out=d_w_out, ffn_g=d_ffn_g,
        w_up=d_w_up, conv_w=dconv[0:3], conv_b=dconv[3:4], w_down=d_w_down, final_g=d_final_g,
        scatter=scatter, received=received)


_ANY = pl.BlockSpec(memory_space=pl.ANY)


def _place():
    return lax.axis_index("x"), lax.axis_index("y"), lax.axis_index("c")


def _other_chips(x, y):
    return [(1 - x, y), (x, 1 - y), (1 - x, 1 - y)]


def _allgather_semaphores(n):
    if n == 0:
        return []
    return [pltpu.SemaphoreType.DMA((3 * n,)), pltpu.SemaphoreType.DMA((3 * n,)), pltpu.SemaphoreType.DMA((n,))]


def _allgather_copies(ins, outs, send, recv, loc):
    n = len(ins)
    x, y, c = _place()
    mine = 2 * x + y
    peers = _other_chips(x, y)

    def remote(a, k, slot):
        return pltpu.make_async_remote_copy(
            src_ref=ins[a], dst_ref=outs[a].at[slot], send_sem=send.at[3 * a + k], recv_sem=recv.at[3 * a + k],
            device_id=(peers[k][0], peers[k][1], c), device_id_type=MESH)

    local = [pltpu.make_async_copy(ins[a], outs[a].at[mine], loc.at[a]) for a in range(n)]
    sends = [remote(a, k, mine) for a in range(n) for k in range(3)]
    recvs = [remote(a, k, 2 * peers[k][0] + peers[k][1]) for a in range(n) for k in range(3)]
    return local, sends, recvs


def _chip_allgather(arrays):
    n = len(arrays)

    def body(*refs):
        local, sends, recvs = _allgather_copies(refs[:n], refs[n:2 * n], *refs[2 * n:])
        for cp in local + sends:
            cp.start()
        for cp in recvs:
            cp.wait_recv()
        for cp in sends:
            cp.wait_send()
        for cp in local:
            cp.wait()

    return pl.pallas_call(
        body, name="ag_weights", in_specs=[_ANY] * n, out_specs=[_ANY] * n,
        out_shape=[jax.ShapeDtypeStruct((N_CHIPS,) + a.shape, a.dtype) for a in arrays],
        scratch_shapes=_allgather_semaphores(n),
    )(*arrays)


def _sibling_exchange(grads, small):
    n = len(grads)

    def body(*refs):
        ins, small_in = refs[:n], refs[n]
        outs, small_out = refs[n + 1:2 * n + 1], refs[2 * n + 1]
        send, recv, s_send, s_recv, loc = refs[2 * n + 2:]
        x, y, c = _place()
        me = 4 * x + 2 * y + c

        def half_copy(a, which):
            half = ins[a].shape[1] // 2
            return pltpu.make_async_remote_copy(
                src_ref=ins[a].at[pl.ds(0, N_CHIPS), pl.ds(which * half, half)], dst_ref=outs[a],
                send_sem=send.at[a], recv_sem=recv.at[a], device_id=(x, y, 1 - c), device_id_type=MESH)

        def peer_of(r):
            return tuple(1 - v if (r >> b) & 1 else v for v, b in ((x, 2), (y, 1), (c, 0)))

        def small_copy(r, slot):
            return pltpu.make_async_remote_copy(
                src_ref=small_in, dst_ref=small_out.at[slot], send_sem=s_send.at[r - 1], recv_sem=s_recv.at[r - 1],
                device_id=peer_of(r), device_id_type=MESH)

        local = pltpu.make_async_copy(small_in, small_out.at[me], loc.at[0])
        sends = [half_copy(a, 1 - c) for a in range(n)] + [small_copy(r, me) for r in range(1, N_DEV)]
        local.start()
        for cp in sends:
            cp.start()
        for r in range(1, N_DEV):
            px, py, pc = peer_of(r)
            small_copy(r, 4 * px + 2 * py + pc).wait_recv()
        for a in range(n):
            half_copy(a, c).wait_recv()
        for cp in sends:
            cp.wait_send()
        local.wait()

    rows = small.shape[0]
    return pl.pallas_call(
        body, name="rs_sibling", in_specs=[_ANY] * (n + 1), out_specs=[_ANY] * (n + 1),
        out_shape=[jax.ShapeDtypeStruct((N_CHIPS, g.shape[1] // 2, g.shape[2]), g.dtype) for g in grads]
        + [jax.ShapeDtypeStruct((N_DEV, rows, small.shape[1]), small.dtype)],
        scratch_shapes=[pltpu.SemaphoreType.DMA((n,)), pltpu.SemaphoreType.DMA((n,)),
                        pltpu.SemaphoreType.DMA((N_DEV - 1,)), pltpu.SemaphoreType.DMA((N_DEV - 1,)),
                        pltpu.SemaphoreType.DMA((1,))],
    )(*grads, small)


def _sibling_halves(grads):
    n = len(grads)

    def body(*refs):
        ins, outs = refs[:n], refs[n:2 * n]
        send, recv = refs[2 * n:]
        x, y, c = _place()

        def half_copy(a, which):
            half = ins[a].shape[1] // 2
            return pltpu.make_async_remote_copy(
                src_ref=ins[a].at[pl.ds(0, N_CHIPS), pl.ds(which * half, half)], dst_ref=outs[a],
                send_sem=send.at[a], recv_sem=recv.at[a], device_id=(x, y, 1 - c), device_id_type=MESH)

        sends = [half_copy(a, 1 - c) for a in range(n)]
        for cp in sends:
            cp.start()
        for a in range(n):
            half_copy(a, c).wait_recv()
        for cp in sends:
            cp.wait_send()

    return pl.pallas_call(
        body, name="rs_sibling_early", in_specs=[_ANY] * n, out_specs=[_ANY] * n,
        out_shape=[jax.ShapeDtypeStruct((N_CHIPS, g.shape[1] // 2, g.shape[2]), g.dtype) for g in grads],
        scratch_shapes=[pltpu.SemaphoreType.DMA((n,)), pltpu.SemaphoreType.DMA((n,))],
    )(*grads)


def _chip_reduce_scatter(parts):
    n = len(parts)

    def body(*refs):
        copies = _scatter_copies(refs[:n], refs[n:2 * n], *refs[2 * n:])
        for cp in copies:
            cp.start()
        for cp in copies:
            cp.wait_recv()
        for cp in copies:
            cp.wait_send()

    return pl.pallas_call(
        body, name="rs_chips", in_specs=[_ANY] * n, out_specs=[_ANY] * n,
        out_shape=_scatter_shapes(parts), scratch_shapes=_scatter_semaphores(n),
    )(*parts)


def _scatter_shapes(parts):
    return [jax.ShapeDtypeStruct((3,) + p.shape[1:], p.dtype) for p in parts]


def _scatter_semaphores(n):
    return [pltpu.SemaphoreType.DMA((3 * n,)), pltpu.SemaphoreType.DMA((3 * n,))] if n else []


def _scatter_copies(ins, outs, send, recv):
    x, y, c = _place()
    peers = _other_chips(x, y)
    return [pltpu.make_async_remote_copy(
        src_ref=ins[a].at[2 * peers[k][0] + peers[k][1]], dst_ref=outs[a].at[k], send_sem=send.at[3 * a + k],
        recv_sem=recv.at[3 * a + k], device_id=(peers[k][0], peers[k][1], c), device_id_type=MESH)
        for a in range(len(ins)) for k in range(3)]


def _sibling_allgather(bufs):
    n = len(bufs)

    def body(*refs):
        outs = refs[n:2 * n]
        send, recv = refs[2 * n:]
        x, y, c = _place()

        def remote(a, which):
            return pltpu.make_async_remote_copy(
                src_ref=outs[a].at[which], dst_ref=outs[a].at[which], send_sem=send.at[a], recv_sem=recv.at[a],
                device_id=(x, y, 1 - c), device_id_type=MESH)

        sends = [remote(a, c) for a in range(n)]
        for cp in sends:
            cp.start()
        for a in range(n):
            remote(a, 1 - c).wait_recv()
        for cp in sends:
            cp.wait_send()

    outs = pl.pallas_call(
        body, name="ag_sibling", in_specs=[_ANY] * n, out_specs=[_ANY] * n,
        out_shape=[jax.ShapeDtypeStruct(b.shape, b.dtype) for b in bufs],
        input_output_aliases={a: a for a in range(n)},
        scratch_shapes=[pltpu.SemaphoreType.DMA((n,)), pltpu.SemaphoreType.DMA((n,))],
    )(*bufs)
    return [o.reshape(2 * o.shape[1], o.shape[2]) for o in outs]


def _pair_add(full, recv, core, name):
    _, R, C = full.shape
    half = R // 2

    def body(core_ref, a_ref, b_ref, o_ref):
        o_ref[...] = (a_ref[...] + b_ref[...]).astype(BF16)

    return pl.pallas_call(
        body, name=name,
        grid_spec=pltpu.PrefetchScalarGridSpec(
            num_scalar_prefetch=1, grid=(N_CHIPS,),
            in_specs=[pl.BlockSpec((1, half, C), lambda j, core_ref: (j, core_ref[0], 0)),
                      pl.BlockSpec((1, half, C), lambda j, core_ref: (j, 0, 0))],
            out_specs=pl.BlockSpec((1, half, C), lambda j, core_ref: (j, 0, 0))),
        out_shape=jax.ShapeDtypeStruct((N_CHIPS, half, C), BF16),
        compiler_params=_params(("parallel",)),
    )(core, full, recv)


def _sum_slots(q, name, tiles=2):
    n, R, C = q.shape
    tr = R // tiles

    def body(q_ref, o_ref):
        acc = q_ref[0].astype(F32)
        for j in range(1, n):
            acc = acc + q_ref[j].astype(F32)
        o_ref[...] = acc

    return pl.pallas_call(
        body, name=name, grid=(tiles,),
        in_specs=[pl.BlockSpec((n, tr, C), lambda i: (0, i, 0))],
        out_specs=pl.BlockSpec((tr, C), lambda i: (i, 0)),
        out_shape=jax.ShapeDtypeStruct((R, C), F32),
        compiler_params=_params(("parallel",)),
    )(q)


def _sum_partials(own_all, recv, place, name, tiles=2):
    _, R, C = own_all.shape
    tr = R // tiles

    def body(place_ref, own_ref, r_ref, o_ref):
        acc = own_ref[0].astype(F32)
        for k in range(3):
            acc = acc + r_ref[k].astype(F32)
        o_ref[0] = acc

    return pl.pallas_call(
        body, name=name,
        grid_spec=pltpu.PrefetchScalarGridSpec(
            num_scalar_prefetch=1, grid=(tiles,),
            in_specs=[pl.BlockSpec((1, tr, C), lambda i, place_ref: (place_ref[0], i, 0)),
                      pl.BlockSpec((3, tr, C), lambda i, place_ref: (0, i, 0))],
            out_specs=pl.BlockSpec((1, tr, C), lambda i, place_ref: (place_ref[1], i, 0))),
        out_shape=jax.ShapeDtypeStruct((2, R, C), F32),
        compiler_params=_params(("parallel",)),
    )(place, own_all, recv)


def _adamw(w, g, m, v, name, tiles=4):
    R, C = w.shape
    tr = R // tiles

    def body(w_ref, g_ref, m_ref, v_ref, go_ref, d_ref, m2_ref, v2_ref):
        g_ = g_ref[...]
        go_ref[...] = g_
        m2 = ADAM_B1 * m_ref[...] + (1.0 - ADAM_B1) * g_
        v2 = ADAM_B2 * v_ref[...] + (1.0 - ADAM_B2) * (g_ * g_)
        m_hat = m2 / (1.0 - ADAM_B1 ** ADAM_STEP)
        v_hat = v2 / (1.0 - ADAM_B2 ** ADAM_STEP)
        d_ref[...] = -ADAM_LR * (m_hat / (jnp.sqrt(v_hat) + ADAM_EPS) + ADAM_WD * w_ref[...])
        m2_ref[...] = m2
        v2_ref[...] = v2

    spec = pl.BlockSpec((tr, C), lambda i: (i, 0))
    return pl.pallas_call(
        body, name=name, grid=(tiles,), in_specs=[spec] * 4, out_specs=[spec] * 4,
        out_shape=[jax.ShapeDtypeStruct((R, C), F32)] * 4,
        compiler_params=_params(("parallel",)),
    )(w, g, m, v)


def _pack_rows(pieces, rows):
    flat = jnp.concatenate([jnp.pad(p.reshape(-1).astype(F32), (0, (-p.size) % D_MODEL)) for p in pieces])
    return jnp.pad(flat, (0, rows * D_MODEL - flat.size)).reshape(rows, D_MODEL)


def _unpack_rows(pack, shapes):
    flat = pack.reshape(-1)
    out, off = [], 0
    for shp in shapes:
        size = int(np.prod(shp))
        out.append(flat[off:off + size].reshape(shp))
        off += size + (-size) % D_MODEL
    return out


def _kernel_order(w):
    parts = [w[:, 0:RET_W]]
    for p in range(FOX_HEADS // 2):
        parts += [w[:, RET_W + part * 512 + p * BLK:RET_W + part * 512 + (p + 1) * BLK] for part in range(3)]
    return jnp.concatenate(parts, axis=1)


def _reference_order(g_main, g_ff):
    parts = [g_main[:, 0:RET_W]]
    for part in range(3):
        parts += [g_main[:, RET_W + 384 * p + part * BLK:RET_W + 384 * p + (part + 1) * BLK] for p in range(FOX_HEADS // 2)]
    return jnp.concatenate(parts + [g_ff], axis=1)


def kernel(x, meta_tokens, attn_norm_g, w_in, fox_forget_b, ret_norm_g, w_out, ffn_norm_g, w_up, conv_w, conv_b, w_down, final_norm_g, loss_target, m_meta_tokens, m_attn_norm_g, m_w_in, m_fox_forget_b, m_ret_norm_g, m_w_out, m_ffn_norm_g, m_w_up, m_conv_w, m_conv_b, m_w_down, m_final_norm_g, v_meta_tokens, v_attn_norm_g, v_w_in, v_fox_forget_b, v_ret_norm_g, v_w_out, v_ffn_norm_g, v_w_up, v_conv_w, v_conv_b, v_w_down, v_final_norm_g):
    chip = 2 * lax.axis_index("x") + lax.axis_index("y")
    core = lax.axis_index("c")
    meta_w, conv_sw = meta_tokens.shape[1], conv_w.shape[2]

    small_w = _pack_rows([meta_tokens, conv_w[0]], 8)
    g_in, g_small = _chip_allgather([w_in[0].astype(BF16), small_w])
    w_in_full = g_in.transpose(1, 0, 2).reshape(D_MODEL, IN_WIDTH)
    w_main = _kernel_order(w_in_full)
    w_ff = jnp.pad(w_in_full[:, MAIN_W:], ((0, 0), (0, BLK - FOX_HEADS)))
    small_parts = [_unpack_rows(g_small[j], [meta_tokens.shape, conv_w.shape[1:]]) for j in range(N_CHIPS)]
    meta_full = jnp.concatenate([sp[0] for sp in small_parts], axis=1)
    conv_w_full = jnp.concatenate([sp[1] for sp in small_parts], axis=1)

    core_idx = core.reshape(1).astype(jnp.int32)
    place = jnp.stack([chip, core]).astype(jnp.int32)

    def assemble(gathered):
        g_out, g_up, g_down = gathered
        return g_out.reshape(D_MODEL, D_MODEL), g_up, g_down.reshape(D_FF, D_MODEL)

    def early_reduce(d_w_out, d_w_up, d_w_down):
        early = [d_w_out.reshape(N_CHIPS, -1, D_MODEL), d_w_up, d_w_down.reshape(N_CHIPS, -1, D_MODEL)]
        from_sib = _sibling_halves(early)
        return [_pair_add(g, r, core_idx, "pair_add_" + nm) for g, r, nm in zip(early, from_sib, ("out", "up", "down"))]

    out = _local_step(x[0], loss_target[0], meta_full, attn_norm_g, w_main, w_ff, fox_forget_b, ret_norm_g,
                      None, ffn_norm_g, None, conv_w_full, conv_b, None, final_norm_g[None],
                      late=([w_out[0].astype(BF16), w_up[0].astype(BF16), w_down[0].astype(BF16)], assemble),
                      mid=early_reduce)

    g_in_full = _reference_order(out["w_main"], out["w_ff"]).reshape(D_MODEL, N_CHIPS, -1).transpose(1, 0, 2)
    small_shapes = [(1, D_MODEL), (1, D_MODEL), (1, D_MODEL), (1, 512 + FOX_HEADS + 1), (1, D_FF), (N_META, D_MODEL), (3, D_FF)]
    small = _pack_rows([out["attn_g"], out["ffn_g"], out["final_g"],
                        jnp.concatenate([out["ret_g"], out["fox_b"], out["loss"].reshape(1, 1)], axis=1),
                        out["conv_b"], out["dmeta"], out["conv_w"]], 32)
    from_sibling_in, small_all = _sibling_exchange([g_in_full], small)
    sum_in = _pair_add(g_in_full, from_sibling_in, core_idx, "pair_add_in")
    (from_chips_in,) = _chip_reduce_scatter([sum_in])
    chip_sums = [sum_in] + list(out["scatter"])
    from_chips = [from_chips_in] + list(out["received"])
    names = ("in", "out", "up", "down")
    totals = [_sum_partials(s, q, place, "sum_chips_" + nm) for s, q, nm in zip(chip_sums, from_chips, names)]
    grad_in, grad_out, grad_up, grad_down = _sibling_allgather(totals)
    s_attn, s_ffn, s_final, s_misc, s_conv_b, s_meta, s_conv_w = _unpack_rows(
        _sum_slots(small_all, "sum_small", tiles=1), small_shapes)
    loss = s_misc[0, 512 + FOX_HEADS]
    small_grads = [lax.dynamic_slice_in_dim(s_meta, chip * meta_w, meta_w, axis=1), s_attn, s_misc[:, 512:512 + FOX_HEADS],
                   s_misc[:, :512], s_ffn, lax.dynamic_slice_in_dim(s_conv_w, chip * conv_sw, conv_sw, axis=1)[None],
                   s_conv_b, s_final[0]]

    big_w = [(w_in, m_w_in, v_w_in, grad_in, "adamw_in"), (w_out, m_w_out, v_w_out, grad_out, "adamw_out"),
             (w_up, m_w_up, v_w_up, grad_up, "adamw_up"), (w_down, m_w_down, v_w_down, grad_down, "adamw_down")]
    big_res = [[r[None] for r in _adamw(w[0], g, m[0], v[0], nm)] for w, m, v, g, nm in big_w]
    small_w_list = [meta_tokens, attn_norm_g, fox_forget_b, ret_norm_g, ffn_norm_g, conv_w, conv_b, final_norm_g]
    small_m = [m_meta_tokens, m_attn_norm_g, m_fox_forget_b, m_ret_norm_g, m_ffn_norm_g, m_conv_w, m_conv_b, m_final_norm_g]
    small_v = [v_meta_tokens, v_attn_norm_g, v_fox_forget_b, v_ret_norm_g, v_ffn_norm_g, v_conv_w, v_conv_b, v_final_norm_g]
    shapes = [a.shape for a in small_w_list]
    packs = [_pack_rows(lst, 16) for lst in (small_w_list, small_grads, small_m, small_v)]
    small_res = [_unpack_rows(r, shapes) for r in _adamw(*packs, "adamw_small", tiles=1)[1:]]
    small_grads = [g.reshape(s) for g, s in zip(small_grads, shapes)]

    def ordered(kind):
        sm = small_grads if kind == 0 else small_res[kind - 1]
        bg = [r[kind] for r in big_res]
        return [sm[0], sm[1], bg[0], sm[2], sm[3], bg[1], sm[4], bg[2], sm[5], sm[6], bg[3], sm[7]]

    return (loss, out["dx"][None], *ordered(0), *ordered(1), *ordered(2), *ordered(3))
```

```python
import functools

import numpy as np
import jax
import jax.numpy as jnp
from jax import lax
from jax.experimental import pallas as pl
from jax.experimental.pallas import tpu as pltpu

F32 = jnp.float32
BF16 = jnp.bfloat16

D_MODEL = 1024
N_META = 16
BLK = 128
UNIT = 2 * BLK
WIDE = 4
CHUNK = 64
N_PAD = BLK - N_META
PREFIX = BLK
RET_HEADS = 4
FOX_HEADS = 8
HEAD_LANES = 64
D_FF = 2816
ROPE_BASE = 10000.0
EPS = 1e-6
NEG = -1e30
RET_W = 1536
FOX_W = 1536
MAIN_W = RET_W + FOX_W
IN_WIDTH = MAIN_W + FOX_HEADS
N_CHIPS = 4
N_DEV = 8

ADAM_LR = 0.001
ADAM_B1 = 0.9
ADAM_B2 = 0.999
ADAM_EPS = 1e-08
ADAM_WD = 0.01
ADAM_STEP = 10

MESH = pl.DeviceIdType.MESH
VMEM_LIMIT_MB = 56

_NT = (((1,), (1,)), ((), ()))
_TN = (((0,), (0,)), ((), ()))


def _dot(a, b):
    return jnp.dot(a, b, preferred_element_type=F32)


def _dot_nt(a, b):
    return lax.dot_general(a, b, _NT, preferred_element_type=F32)


def _dot_tn(a, b):
    return lax.dot_general(a, b, _TN, preferred_element_type=F32)


def _params(dims=None, vmem_mb=VMEM_LIMIT_MB):
    kw = dict(vmem_limit_bytes=vmem_mb << 20)
    if dims is not None:
        kw["dimension_semantics"] = dims
    return pltpu.CompilerParams(**kw)


def _row_tile(n, prefs=(384, 256, 128)):
    for t in prefs:
        if n % t == 0:
            return t
    raise ValueError(f"no row tile for {n}")


def _iota(shape, dim):
    return lax.broadcasted_iota(jnp.int32, shape, dim)


def _pick_row(tile, row):
    sub = _iota(tile.shape, 0)
    return jnp.sum(jnp.where(sub == row, tile, 0.0), axis=0, keepdims=True)


def _split3(x):
    hi = x.astype(BF16)
    r1 = x - hi.astype(F32)
    mid = r1.astype(BF16)
    lo = (r1 - mid.astype(F32)).astype(BF16)
    return hi, mid, lo


def _full(shape):
    nd = len(shape)
    return pl.BlockSpec(shape, lambda *_: (0,) * nd)


def _in_perm():
    cols = list(range(RET_W))
    for p in range(FOX_HEADS // 2):
        for part in range(3):
            start = RET_W + part * 512 + p * BLK
            cols += list(range(start, start + BLK))
    return np.asarray(cols, np.int32)


def _rotary_tables(L):
    half = HEAD_LANES // 2
    inv = 1.0 / (ROPE_BASE ** (jnp.arange(half, dtype=F32) / half))
    ang = jnp.arange(L).astype(F32)[:, None] * inv[None, :]
    cos, sin = jnp.cos(ang), jnp.sin(ang)
    cos_t = jnp.tile(cos, (1, 4))
    sin_t = jnp.tile(jnp.concatenate([-sin, sin], axis=1), (1, 2))
    return cos_t, sin_t


def _decay_tables():
    gam = 1.0 - 2.0 ** (-5.0 - np.arange(RET_HEADS, dtype=np.float64))
    n = np.arange(BLK)
    same_or_past = (n[:, None] // CHUNK) >= (n[None, :] // CHUNK)
    dist = np.abs(n[:, None] - n[None, :])
    dmat = np.stack([np.where(same_or_past, g ** dist, 0.0) for g in gam]).astype(np.float32)
    lane_head = np.arange(BLK) // HEAD_LANES
    wq = np.stack([gam[2 * p + lane_head][None, :] ** (n[:, None] + 1.0) for p in range(2)]).astype(np.float32)
    wk = np.stack([gam[2 * p + lane_head][None, :] ** (BLK - 1.0 - n[:, None]) for p in range(2)]).astype(np.float32)
    g_blk = tuple(float(g ** BLK) for g in gam)
    return jnp.asarray(dmat), jnp.asarray(wq), jnp.asarray(wk), g_blk


def _rms_inproj(h0, g, w_main, w_ff):
    L = h0.shape[0]
    tm = _row_tile(L)

    def body(h_ref, g_ref, wm_ref, wf_ref, n_ref, p_ref, ff_ref):
        h = h_ref[...]
        r = lax.rsqrt(jnp.mean(h * h, axis=-1, keepdims=True) + EPS)
        n = (h * r * g_ref[...]).astype(BF16)
        n_ref[...] = n
        p_ref[...] = _dot(n, wm_ref[...]).astype(BF16)
        ff_ref[...] = _dot(n, wf_ref[...])

    return pl.pallas_call(
        body, name="f_inproj", grid=(L // tm,),
        in_specs=[pl.BlockSpec((tm, D_MODEL), lambda i: (i, 0)), _full((1, D_MODEL)),
                  _full((D_MODEL, MAIN_W)), _full((D_MODEL, BLK))],
        out_specs=[pl.BlockSpec((tm, D_MODEL), lambda i: (i, 0)), pl.BlockSpec((tm, MAIN_W), lambda i: (i, 0)),
                   pl.BlockSpec((tm, BLK), lambda i: (i, 0))],
        out_shape=[jax.ShapeDtypeStruct((L, D_MODEL), BF16), jax.ShapeDtypeStruct((L, MAIN_W), BF16),
                   jax.ShapeDtypeStruct((L, BLK), F32)],
        compiler_params=_params(("parallel",)),
    )(h0, g, w_main, w_ff)


def _block_group(nblk):
    return 3 if nblk % 3 == 0 else 1


def _fox_prep(ff, fb):
    L = ff.shape[0]
    nblk = L // BLK
    G = _block_group(nblk)

    def body(ff_ref, b_ref, c_ref, ct_ref, carry):
        @pl.when(pl.program_id(0) == 0)
        def _():
            carry[...] = jnp.zeros_like(carry)

        tri = (_iota((BLK, BLK), 0) >= _iota((BLK, BLK), 1)).astype(BF16)
        live = _iota((BLK, BLK), 1) < FOX_HEADS
        run = carry[...]
        for b in range(G):
            z = ff_ref[b * BLK:(b + 1) * BLK, :] + b_ref[...]
            lf = jnp.where(live, jnp.minimum(z, 0.0) - jnp.log1p(jnp.exp(-jnp.abs(z))), 0.0)
            hi, mid, lo = _split3(lf)
            cs = _dot(tri, hi) + _dot(tri, mid) + _dot(tri, lo) + run
            c_ref[b * BLK:(b + 1) * BLK, :] = cs
            ct_ref[b] = cs.T[0:8, :]
            run = run + jnp.sum(lf, axis=0, keepdims=True)
        carry[...] = run

    return pl.pallas_call(
        body, name="f_foxprep", grid=(nblk // G,),
        in_specs=[pl.BlockSpec((G * BLK, BLK), lambda i: (i, 0)), _full((1, BLK))],
        out_specs=[pl.BlockSpec((G * BLK, BLK), lambda i: (i, 0)), pl.BlockSpec((G, 8, BLK), lambda i: (i, 0, 0))],
        out_shape=[jax.ShapeDtypeStruct((L, BLK), F32), jax.ShapeDtypeStruct((nblk, 8, BLK), F32)],
        scratch_shapes=[pltpu.VMEM((1, BLK), F32)],
        compiler_params=_params(("arbitrary",)),
    )(ff, fb)


def _rot_fns(cos, sin):
    lane = _iota((BLK, BLK), 1)
    first = (lane & (HEAD_LANES - 1)) < HEAD_LANES // 2

    def swap(x):
        return jnp.where(first, pltpu.roll(x, BLK - 32, 1), pltpu.roll(x, 32, 1))

    def rot(x):
        return x * cos + swap(x) * sin

    def rot_t(dy):
        return dy * cos + swap(dy * sin)

    return rot, rot_t


def _retention_fwd(proj, cos_t, sin_t, ret_g):
    L = proj.shape[0]
    nblk = L // BLK
    G = _block_group(nblk)
    dmat, wq_t, wk_t, g_blk = _decay_tables()

    def body(q_ref, k_ref, v_ref, gate_ref, cos_ref, sin_ref, d_ref, wq_ref, wk_ref, rg_ref,
             mix_ref, o_ref, rs_ref, state):
        @pl.when(pl.program_id(0) == 0)
        def _():
            state[...] = jnp.zeros_like(state)

        lane = _iota((BLK, BLK), 1)
        sub = _iota((BLK, BLK), 0)
        for b in range(G):
            rows = slice(b * BLK, (b + 1) * BLK)
            rot, _ = _rot_fns(cos_ref[rows, :], sin_ref[rows, :])
            for p in range(2):
                qr = rot(q_ref[rows, p * BLK:(p + 1) * BLK].astype(F32))
                kr = rot(k_ref[rows, p * BLK:(p + 1) * BLK].astype(F32)) * (HEAD_LANES ** -0.5)
                kr_b = kr.astype(BF16)
                qw = (qr * wq_ref[p]).astype(BF16)
                kw = (kr * wk_ref[p]).astype(BF16)
                for e in range(2):
                    h = 2 * p + e
                    cols = slice(h * BLK, (h + 1) * BLK)
                    qm = jnp.where((lane >> 6) == e, qr, 0.0).astype(BF16)
                    s = _dot_nt(qm, kr_b) * d_ref[h]
                    vh = v_ref[rows, cols]
                    st = state[h]
                    rs_ref[b, h] = st
                    o = _dot(s.astype(BF16), vh) + _dot(qw, st.astype(BF16))
                    u = jnp.where((sub >> 6) == e, _dot_tn(kw, vh), 0.0)
                    state[h] = g_blk[h] * st + u
                    rn = lax.rsqrt(jnp.mean(o * o, axis=-1, keepdims=True) + EPS)
                    gate = gate_ref[rows, cols].astype(F32)
                    o_ref[rows, cols] = o
                    mix_ref[rows, cols] = (o * rn * rg_ref[:, cols] * (gate * jax.nn.sigmoid(gate))).astype(BF16)

    row = lambda c: (lambda i: (i, c))
    return pl.pallas_call(
        body, name="f_retention", grid=(nblk // G,),
        in_specs=[pl.BlockSpec((G * BLK, 256), row(0)), pl.BlockSpec((G * BLK, 256), row(1)),
                  pl.BlockSpec((G * BLK, 512), row(1)), pl.BlockSpec((G * BLK, 512), row(2)),
                  pl.BlockSpec((G * BLK, BLK), row(0)), pl.BlockSpec((G * BLK, BLK), row(0)),
                  _full((RET_HEADS, BLK, BLK)), _full((2, BLK, BLK)), _full((2, BLK, BLK)), _full((1, 512))],
        out_specs=[pl.BlockSpec((G * BLK, 512), row(0)), pl.BlockSpec((G * BLK, 512), row(0)),
                   pl.BlockSpec((G, RET_HEADS, BLK, BLK), lambda i: (i, 0, 0, 0))],
        out_shape=[jax.ShapeDtypeStruct((L, 512), BF16), jax.ShapeDtypeStruct((L, 512), F32),
                   jax.ShapeDtypeStruct((nblk, RET_HEADS, BLK, BLK), F32)],
        scratch_shapes=[pltpu.VMEM((RET_HEADS, BLK, BLK), F32)],
        compiler_params=_params(("arbitrary",)),
    )(proj, proj, proj, proj, cos_t, sin_t, dmat, wq_t, wk_t, ret_g)


def _fox_units(L):
    nblk = L // BLK
    assert L % BLK == 0 and nblk % 2 == 1, "sequence must be one 128-row block plus whole 256-row tiles"
    return nblk, (nblk - 1) // 2


def _fox_tile_masks():
    sub, lane = _iota((BLK, BLK), 0), _iota((BLK, BLK), 1)
    valid = _iota((BLK, UNIT), 0) >= N_PAD
    diag = _iota((UNIT, UNIT), 0) <= _iota((UNIT, UNIT), 1)
    r, q = _iota((BLK + UNIT, UNIT), 0), _iota((BLK + UNIT, UNIT), 1)
    first_and_diag = ((r < BLK) & (r >= N_PAD)) | ((r >= BLK) & (r - BLK <= q))
    return dict(first=(sub <= lane) & (sub >= N_PAD), valid=valid, diag=diag, first_and_diag=first_and_diag)


def _fox_fwd(proj, c, ctb, gather=()):
    L = proj.shape[0]
    nblk, nu = _fox_units(L)
    scale = HEAD_LANES ** -0.5
    ng = len(gather)

    def body(qkv_ref, c_ref, ct_ref, *rest):
        g_in, (of_ref, lse_ref), g_out = rest[:ng], rest[ng:ng + 2], rest[ng + 2:2 * ng + 2]
        vt, csb = rest[2 * ng + 2:2 * ng + 4]
        p = pl.program_id(0)

        @pl.when(p == 0)
        def _():
            lse_ref[...] = jnp.zeros_like(lse_ref)
            if ng:
                local, sends, _ = _allgather_copies(g_in, g_out, *rest[2 * ng + 4:])
                for cp in local + sends:
                    cp.start()

        lane = _iota((BLK, BLK), 1)
        sub8 = _iota((8, BLK), 0)
        masks = _fox_tile_masks()

        def pre(j, carry):
            off = pl.multiple_of(j * BLK, BLK)
            vt[j] = qkv_ref[pl.ds(off, BLK), 2 * BLK:3 * BLK].astype(F32).T.astype(BF16)
            ct = c_ref[pl.ds(off, BLK), :]
            for e in range(2):
                col = jnp.sum(jnp.where(lane == 2 * p + e, ct, 0.0), axis=1, keepdims=True)
                csb[e, j] = jnp.broadcast_to(col, (BLK, UNIT))
            return carry

        lax.fori_loop(0, nblk, pre, 0)

        def attend(qblk, nq, n_whole):
            qlen = nq * BLK
            qoff = pl.multiple_of(qblk * BLK, BLK)
            qs = qkv_ref[pl.ds(qoff, qlen), 0:BLK].astype(F32) * scale
            qlane = _iota((qlen, BLK), 1)
            qm = [jnp.where((qlane >> 6) == e, qs, 0.0).astype(BF16) for e in range(2)]
            ct_row = [jnp.concatenate([_pick_row(ct_ref[qblk + a], 2 * p + e) for a in range(nq)], axis=1)
                      for e in range(2)]

            def step(segs, mask, st):
                blocks = [kblk + b for kblk, nk in segs for b in range(nk)]
                kt = [qkv_ref[pl.ds(pl.multiple_of(kblk * BLK, BLK), nk * BLK), BLK:2 * BLK] for kblk, nk in segs]
                kt = kt[0] if len(kt) == 1 else jnp.concatenate(kt, axis=0)
                out = []
                for e in range(2):
                    m, l, acc = st[3 * e:3 * e + 3]
                    s = _dot_nt(kt, qm[e])
                    t = jnp.concatenate([s[b * BLK:(b + 1) * BLK] - csb[e, blk, :, 0:qlen]
                                         for b, blk in enumerate(blocks)], axis=0)
                    if mask is not None:
                        t = jnp.where(mask, t, NEG)
                    m_new = jnp.maximum(m, jnp.max(t, axis=0, keepdims=True) + ct_row[e])
                    alpha = jnp.exp(m - m_new)
                    pr = jnp.exp(t - (m_new - ct_row[e]))
                    l = alpha * l + jnp.sum(pr, axis=0, keepdims=True)
                    pr_b = pr.astype(BF16)
                    pv = None
                    for b, blk in enumerate(blocks):
                        part = _dot(vt[blk, e * HEAD_LANES:(e + 1) * HEAD_LANES, :], pr_b[b * BLK:(b + 1) * BLK])
                        pv = part if pv is None else pv + part
                    out += [m_new, l, alpha * acc + pv]
                return tuple(out)

            st = (jnp.full((1, qlen), NEG, F32), jnp.zeros((1, qlen), F32), jnp.zeros((HEAD_LANES, qlen), F32)) * 2
            if nq == 1:
                st = step([(0, 1)], masks["first"], st)
            else:
                st = step([(0, 1), (qblk, 2)], masks["first_and_diag"], st)
                n_wide = n_whole // WIDE
                st = lax.fori_loop(0, n_wide, lambda j, s_: step([(1 + 2 * WIDE * j, 2 * WIDE)], None, s_), st)
                rest = 1 + 2 * WIDE * n_wide
                st = lax.cond((n_whole & 2) != 0, lambda s_: step([(rest, 4)], None, s_), lambda s_: s_, st)
                st = lax.cond((n_whole & 1) != 0, lambda s_: step([(rest + 2 * (n_whole & 2), 2)], None, s_),
                              lambda s_: s_, st)
            o_t = jnp.concatenate([st[2] * (1.0 / st[1]), st[5] * (1.0 / st[4])], axis=0)
            of_ref[pl.ds(qoff, qlen), :] = o_t.T.astype(BF16)
            lse = [st[3 * e] + jnp.log(st[3 * e + 1]) for e in range(2)]
            for a in range(nq):
                rows = [lse[e][:, a * BLK:(a + 1) * BLK] for e in range(2)]
                lse_ref[qblk + a] = lse_ref[qblk + a] + (
                    jnp.where(sub8 == 2 * p, rows[0], 0.0) + jnp.where(sub8 == 2 * p + 1, rows[1], 0.0))

        attend(0, 1, 0)

        def q_loop(u, carry):
            attend(1 + 2 * u, 2, u)
            return carry

        lax.fori_loop(0, nu, q_loop, 0)

        if ng:
            @pl.when(p == FOX_HEADS // 2 - 1)
            def _():
                local, sends, recvs = _allgather_copies(g_in, g_out, *rest[2 * ng + 4:])
                for cp in recvs:
                    cp.wait_recv()
                for cp in sends:
                    cp.wait_send()
                for cp in local:
                    cp.wait()

    return pl.pallas_call(
        body, name="f_fox", grid=(FOX_HEADS // 2,),
        in_specs=[pl.BlockSpec((L, 384), lambda p: (0, RET_W // 384 + p)), _full((L, BLK)), _full((nblk, 8, BLK))]
        + [_ANY] * ng,
        out_specs=[pl.BlockSpec((L, BLK), lambda p: (0, p)), _full((nblk, 8, BLK))] + [_ANY] * ng,
        out_shape=[jax.ShapeDtypeStruct((L, 512), BF16), jax.ShapeDtypeStruct((nblk, 8, BLK), F32)]
        + [jax.ShapeDtypeStruct((N_CHIPS,) + a.shape, a.dtype) for a in gather],
        scratch_shapes=[pltpu.VMEM((nblk, BLK, BLK), BF16), pltpu.VMEM((2, nblk, BLK, UNIT), F32)]
        + _allgather_semaphores(ng),
        compiler_params=_params(("arbitrary",)),
    )(proj, c, ctb, *gather)


def _outproj_up(mix_r, o_f, h0, w_out, ffn_g, w_up, conv_w, conv_b):
    L = h0.shape[0]
    tm = _row_tile(L)
    shard = w_up.shape[2]
    assert 2 * shard == D_FF
    cw = [conv_w[j:j + 1] for j in range(3)]
    resident = lambda shape: pl.BlockSpec(shape, lambda i: (0,) * len(shape), pipeline_mode=pl.Buffered(1))

    def body(mr_ref, of_ref, h0_ref, wo_ref, g_ref, wu_ref, cw0, cw1, cw2, cb_ref,
             h1_ref, n2_ref, up_ref, act_ref, halo):
        i = pl.program_id(0)

        @pl.when(i == 0)
        def _():
            halo[...] = jnp.zeros_like(halo)

        h1 = h0_ref[...] + _dot(mr_ref[...], wo_ref[0:512, :]) + _dot(of_ref[...], wo_ref[512:1024, :])
        h1_ref[...] = h1
        r = lax.rsqrt(jnp.mean(h1 * h1, axis=-1, keepdims=True) + EPS)
        n2 = (h1 * r * g_ref[...]).astype(BF16)
        n2_ref[...] = n2
        live = i * tm + _iota((tm, 1), 0) >= N_PAD
        for half in range(2):
            cols = slice(half * shard, (half + 1) * shard)
            a_b = _dot(n2, wu_ref[half]).astype(BF16)
            b_b = _dot(n2, wu_ref[2 + half]).astype(BF16)
            up_ref[:, cols] = a_b
            up_ref[:, D_FF + half * shard:D_FF + (half + 1) * shard] = b_b
            a = jnp.where(live, a_b.astype(F32), 0.0)
            _, _, acc = _conv_taps(a, halo[:, cols], [cw0[:, cols], cw1[:, cols], cw2[:, cols]], cb_ref[:, cols])
            act_ref[:, cols] = (acc * jax.nn.sigmoid(acc) * b_b.astype(F32)).astype(BF16)
            halo[:, cols] = a[tm - 8:tm, :]

    rows = lambda w: pl.BlockSpec((tm, w), lambda i: (i, 0))
    return pl.pallas_call(
        body, name="f_outproj_up", grid=(L // tm,),
        in_specs=[rows(512), rows(512), rows(D_MODEL), resident((D_MODEL, D_MODEL)), _full((1, D_MODEL)),
                  resident((N_CHIPS, D_MODEL, shard)), _full((1, D_FF)), _full((1, D_FF)), _full((1, D_FF)),
                  _full((1, D_FF))],
        out_specs=[rows(D_MODEL), rows(D_MODEL), rows(2 * D_FF), rows(D_FF)],
        out_shape=[jax.ShapeDtypeStruct((L, D_MODEL), F32), jax.ShapeDtypeStruct((L, D_MODEL), BF16),
                   jax.ShapeDtypeStruct((L, 2 * D_FF), BF16), jax.ShapeDtypeStruct((L, D_FF), BF16)],
        scratch_shapes=[pltpu.VMEM((8, D_FF), F32)],
        compiler_params=_params(("arbitrary",)),
    )(mix_r, o_f, h0, w_out, ffn_g, w_up, cw[0], cw[1], cw[2], conv_b)


def _conv_taps(a, halo, cw, cb):
    sub = _iota((a.shape[0], 1), 0)
    a1 = jnp.where(sub == 0, _pick_row(halo, 7), pltpu.roll(a, 1, 0))
    a2 = jnp.where(sub == 0, _pick_row(halo, 6), jnp.where(sub == 1, _pick_row(halo, 7), pltpu.roll(a, 2, 0)))
    acc = cb + a2 * cw[0]
    acc = acc + a1 * cw[1]
    acc = acc + a * cw[2]
    return a1, a2, acc


def _conv_acc(a_ref, halo_ref, cw_refs, cb_ref, i, tm):
    sub = _iota((tm, 1), 0)
    a = jnp.where(i * tm + sub >= N_PAD, a_ref[...].astype(F32), 0.0)
    hrow = i * tm - 8 + _iota((8, 1), 0)
    halo = jnp.where((hrow >= N_PAD) & (i > 0), halo_ref[...].astype(F32), 0.0)
    a1, a2, acc = _conv_taps(a, halo, [r[...] for r in cw_refs], cb_ref[...])
    return a, a1, a2, acc


def _ffn_down_loss(g_act, w_down, h1, final_g, target):
    L = h1.shape[0]
    tm = _row_tile(L)

    def body(g_ref, wd_ref, h1_ref, gf_ref, t_ref, dh_ref, dhb_ref, dgf_ref, loss_ref):
        i = pl.program_id(0)

        @pl.when(i == 0)
        def _():
            dgf_ref[...] = jnp.zeros_like(dgf_ref)
            loss_ref[...] = jnp.zeros_like(loss_ref)

        h2 = h1_ref[...] + _dot(g_ref[...], wd_ref[...])
        r = lax.rsqrt(jnp.mean(h2 * h2, axis=-1, keepdims=True) + EPS)
        yn = h2 * r
        gf = gf_ref[...]
        live = i * tm + _iota((tm, 1), 0) >= PREFIX
        err = jnp.where(live, yn * gf - t_ref[...], 0.0)
        loss_ref[...] = loss_ref[...] + 0.5 * jnp.sum(jnp.mean(err * err, axis=-1, keepdims=True))
        dy = err * (1.0 / D_MODEL)
        dgf_ref[...] = dgf_ref[...] + jnp.sum(dy * yn, axis=0, keepdims=True)
        dyn = dy * gf
        dh = r * (dyn - yn * jnp.mean(dyn * yn, axis=-1, keepdims=True))
        dh_ref[...] = dh
        dhb_ref[...] = dh.astype(BF16)

    rows = lambda w: pl.BlockSpec((tm, w), lambda i: (i, 0))
    return pl.pallas_call(
        body, name="f_ffn_down_loss", grid=(L // tm,),
        in_specs=[rows(D_FF), _full((D_FF, D_MODEL)), rows(D_MODEL), _full((1, D_MODEL)), rows(D_MODEL)],
        out_specs=[rows(D_MODEL), rows(D_MODEL), _full((1, D_MODEL)), _full((1, BLK))],
        out_shape=[jax.ShapeDtypeStruct((L, D_MODEL), F32), jax.ShapeDtypeStruct((L, D_MODEL), BF16),
                   jax.ShapeDtypeStruct((1, D_MODEL), F32), jax.ShapeDtypeStruct((1, BLK), F32)],
        compiler_params=_params(("arbitrary",)),
    )(g_act, w_down, h1, final_g, target)


def _ffn_bwd_gate(dh2b, w_down, up, conv_w, conv_b):
    L = dh2b.shape[0]
    tm = _row_tile(L)
    cw = [conv_w[j:j + 1] for j in range(3)]

    def body(dh_ref, wd_ref, a_ref, halo_ref, b_ref, cw0, cw1, cw2, cb_ref, dacc_ref, db_ref, dcw_ref):
        i = pl.program_id(0)

        @pl.when(i == 0)
        def _():
            dcw_ref[...] = jnp.zeros_like(dcw_ref)

        a, a1, a2, acc = _conv_acc(a_ref, halo_ref, (cw0, cw1, cw2), cb_ref, i, tm)
        dg = _dot_nt(dh_ref[...], wd_ref[...])
        sg = jax.nn.sigmoid(acc)
        db_ref[...] = (dg * acc * sg).astype(BF16)
        dacc = dg * b_ref[...].astype(F32) * (sg * (1.0 + acc * (1.0 - sg)))
        dacc_ref[...] = dacc.astype(BF16)
        sub8 = _iota((8, 1), 0)
        rows = [jnp.sum(dacc * t, axis=0, keepdims=True) for t in (a2, a1, a)] + [jnp.sum(dacc, axis=0, keepdims=True)]
        upd = jnp.zeros((8, D_FF), F32)
        for j, rj in enumerate(rows):
            upd = upd + jnp.where(sub8 == j, rj, 0.0)
        dcw_ref[...] = dcw_ref[...] + upd

    rows = lambda w, c=0: pl.BlockSpec((tm, w), lambda i: (i, c))
    halo = pl.BlockSpec((8, D_FF), lambda i: (jnp.maximum(i * (tm // 8) - 1, 0), 0))
    return pl.pallas_call(
        body, name="b_ffn_gate", grid=(L // tm,),
        in_specs=[rows(D_MODEL), _full((D_FF, D_MODEL)), rows(D_FF), halo, rows(D_FF, 1),
                  _full((1, D_FF)), _full((1, D_FF)), _full((1, D_FF)), _full((1, D_FF))],
        out_specs=[rows(D_FF), rows(D_FF), _full((8, D_FF))],
        out_shape=[jax.ShapeDtypeStruct((L, D_FF), BF16), jax.ShapeDtypeStruct((L, D_FF), BF16),
                   jax.ShapeDtypeStruct((8, D_FF), F32)],
        compiler_params=_params(("arbitrary",)),
    )(dh2b, w_down, up, up, up, cw[0], cw[1], cw[2], conv_b)


def _ffn_bwd_up(dacc, db, conv_w, w_up, h1, ffn_g, dh2, w_out):
    L = h1.shape[0]
    tm = _row_tile(L)
    nt = L // tm
    shard = w_up.shape[2]
    cw = [conv_w[j:j + 1] for j in range(3)]

    def body(da_ref, halo_ref, db_ref, cw0, cw1, cw2, wu_ref, h1_ref, g_ref, dh2_ref, wo_ref,
             dup_ref, dh1_ref, dh1b_ref, dmix_ref, dg_ref):
        i = pl.program_id(0)

        @pl.when(i == 0)
        def _():
            dg_ref[...] = jnp.zeros_like(dg_ref)

        sub = _iota((tm, 1), 0)
        d0 = da_ref[...].astype(F32)
        halo = jnp.where(i < nt - 1, halo_ref[...].astype(F32), 0.0)
        d1 = jnp.where(sub == tm - 1, _pick_row(halo, 0), pltpu.roll(d0, tm - 1, 0))
        d2 = jnp.where(sub == tm - 2, _pick_row(halo, 0),
                       jnp.where(sub == tm - 1, _pick_row(halo, 1), pltpu.roll(d0, tm - 2, 0)))
        da = d0 * cw2[...] + d1 * cw1[...] + d2 * cw0[...]
        da = jnp.where(i * tm + sub >= N_PAD, da, 0.0).astype(BF16)
        dup_ref[:, 0:D_FF] = da
        dbv = db_ref[...]
        dup_ref[:, D_FF:2 * D_FF] = dbv
        dn = jnp.zeros((tm, D_MODEL), F32)
        for j in range(N_CHIPS):
            src = da if j < 2 else dbv
            lo = (j % 2) * shard
            dn = dn + _dot_nt(src[:, lo:lo + shard], wu_ref[j])
        h1 = h1_ref[...]
        r = lax.rsqrt(jnp.mean(h1 * h1, axis=-1, keepdims=True) + EPS)
        yn = h1 * r
        dg_ref[...] = dg_ref[...] + jnp.sum(dn * yn, axis=0, keepdims=True)
        dyn = dn * g_ref[...]
        dh1 = dh2_ref[...] + r * (dyn - yn * jnp.mean(dyn * yn, axis=-1, keepdims=True))
        dh1_ref[...] = dh1
        dh1b = dh1.astype(BF16)
        dh1b_ref[...] = dh1b
        dmix_ref[...] = _dot_nt(dh1b, wo_ref[...]).astype(BF16)

    rows = lambda w: pl.BlockSpec((tm, w), lambda i: (i, 0))
    halo = pl.BlockSpec((8, D_FF), lambda i: (jnp.minimum((i + 1) * (tm // 8), L // 8 - 1), 0))
    return pl.pallas_call(
        body, name="b_ffn_up", grid=(nt,),
        in_specs=[rows(D_FF), halo, rows(D_FF), _full((1, D_FF)), _full((1, D_FF)), _full((1, D_FF)),
                  _full((N_CHIPS, D_MODEL, shard)), rows(D_MODEL), _full((1, D_MODEL)), rows(D_MODEL),
                  _full((D_MODEL, D_MODEL))],
        out_specs=[rows(2 * D_FF), rows(D_MODEL), rows(D_MODEL), rows(D_MODEL), _full((1, D_MODEL))],
        out_shape=[jax.ShapeDtypeStruct((L, 2 * D_FF), BF16), jax.ShapeDtypeStruct((L, D_MODEL), F32),
                   jax.ShapeDtypeStruct((L, D_MODEL), BF16), jax.ShapeDtypeStruct((L, D_MODEL), BF16),
                   jax.ShapeDtypeStruct((1, D_MODEL), F32)],
        compiler_params=_params(("arbitrary",)),
    )(dacc, dacc, db, cw[0], cw[1], cw[2], w_up, h1, ffn_g, dh2, w_out)


def _wgrad(a, b, name, tn=None, tk=None):
    L, K = a.shape
    N = b.shape[1]
    tn = N if tn is None else tn
    tk = K if tk is None else tk
    tl = _row_tile(L, (1408, 768, 512, 256, 128))

    def body(a_ref, b_ref, o_ref):
        @pl.when(pl.program_id(2) == 0)
        def _():
            o_ref[...] = jnp.zeros_like(o_ref)

        o_ref[0] = o_ref[0] + _dot_tn(a_ref[...], b_ref[...])

    return pl.pallas_call(
        body, name=name, grid=(N // tn, K // tk, L // tl),
        in_specs=[pl.BlockSpec((tl, tk), lambda n, k, l: (l, k)), pl.BlockSpec((tl, tn), lambda n, k, l: (l, n))],
        out_specs=pl.BlockSpec((1, tk, tn), lambda n, k, l: (n, k, 0)),
        out_shape=jax.ShapeDtypeStruct((N // tn, K, tn), F32),
        compiler_params=_params(("parallel", "parallel", "arbitrary")),
    )(a, b)


def _retention_bwd(dmix, o, proj, cos_t, sin_t, ret_g, states):
    L = proj.shape[0]
    nblk = L // BLK
    G = _block_group(nblk)
    steps = nblk // G
    dmat, wq_t, wk_t, g_blk = _decay_tables()

    def body(dm_ref, o_ref, q_ref, k_ref, v_ref, gate_ref, cos_ref, sin_ref, d_ref, wq_ref, wk_ref, rg_ref, rs_ref,
             dp_ref, drg_ref, gstate):
        @pl.when(pl.program_id(0) == 0)
        def _():
            gstate[...] = jnp.zeros_like(gstate)
            drg_ref[...] = jnp.zeros_like(drg_ref)

        lane = _iota((BLK, BLK), 1)
        sub = _iota((BLK, BLK), 0)
        scale = HEAD_LANES ** -0.5
        for b in reversed(range(G)):
            rows = slice(b * BLK, (b + 1) * BLK)
            rot, rot_t = _rot_fns(cos_ref[rows, :], sin_ref[rows, :])
            for p in range(2):
                qr = rot(q_ref[rows, p * BLK:(p + 1) * BLK].astype(F32))
                kr = rot(k_ref[rows, p * BLK:(p + 1) * BLK].astype(F32)) * scale
                kr_b = kr.astype(BF16)
                qw = (qr * wq_ref[p]).astype(BF16)
                kw = (kr * wk_ref[p]).astype(BF16)
                dqr = jnp.zeros((BLK, BLK), F32)
                dkr = jnp.zeros((BLK, BLK), F32)
                for e in range(2):
                    h = 2 * p + e
                    cols = slice(h * BLK, (h + 1) * BLK)
                    head_lanes = (lane >> 6) == e
                    o = o_ref[rows, cols]
                    rn = lax.rsqrt(jnp.mean(o * o, axis=-1, keepdims=True) + EPS)
                    y = o * rn
                    gate = gate_ref[rows, cols].astype(F32)
                    sg = jax.nn.sigmoid(gate)
                    dm = dm_ref[rows, cols].astype(F32)
                    rgain = rg_ref[:, cols]
                    drg_ref[:, cols] = drg_ref[:, cols] + jnp.sum(dm * y * (gate * sg), axis=0, keepdims=True)
                    dp_ref[rows, 1024 + h * BLK:1024 + (h + 1) * BLK] = (
                        dm * y * rgain * (sg * (1.0 + gate * (1.0 - sg)))).astype(BF16)
                    dy = dm * rgain * (gate * sg)
                    do = (rn * (dy - y * jnp.mean(dy * y, axis=-1, keepdims=True))).astype(BF16)
                    vh = v_ref[rows, cols]
                    qm = jnp.where(head_lanes, qr, 0.0).astype(BF16)
                    dmh = d_ref[h]
                    s = (_dot_nt(qm, kr_b) * dmh).astype(BF16)
                    ds = (_dot_nt(do, vh) * dmh).astype(BF16)
                    st = rs_ref[b, h].astype(BF16)
                    gs = gstate[h]
                    gs_b = gs.astype(BF16)
                    dqr = dqr + jnp.where(head_lanes, _dot(ds, kr_b), 0.0) + _dot_nt(do, st) * wq_ref[p]
                    dkr = dkr + _dot_tn(ds, qm) + _dot_nt(vh, gs_b) * wk_ref[p]
                    dp_ref[rows, 512 + h * BLK:512 + (h + 1) * BLK] = (_dot_tn(s, do) + _dot(kw, gs_b)).astype(BF16)
                    dr = jnp.where((sub >> 6) == e, _dot_tn(qw, do), 0.0)
                    gstate[h] = dr + g_blk[h] * gs
                dp_ref[rows, p * BLK:(p + 1) * BLK] = rot_t(dqr).astype(BF16)
                dp_ref[rows, 256 + p * BLK:256 + (p + 1) * BLK] = (rot_t(dkr) * scale).astype(BF16)

    row = lambda c: (lambda i: (steps - 1 - i, c))
    return pl.pallas_call(
        body, name="b_retention", grid=(steps,),
        in_specs=[pl.BlockSpec((G * BLK, 512), row(0)), pl.BlockSpec((G * BLK, 512), row(0)),
                  pl.BlockSpec((G * BLK, 256), row(0)), pl.BlockSpec((G * BLK, 256), row(1)),
                  pl.BlockSpec((G * BLK, 512), row(1)), pl.BlockSpec((G * BLK, 512), row(2)),
                  pl.BlockSpec((G * BLK, BLK), row(0)), pl.BlockSpec((G * BLK, BLK), row(0)),
                  _full((RET_HEADS, BLK, BLK)), _full((2, BLK, BLK)), _full((2, BLK, BLK)), _full((1, 512)),
                  pl.BlockSpec((G, RET_HEADS, BLK, BLK), lambda i: (steps - 1 - i, 0, 0, 0))],
        out_specs=[pl.BlockSpec((G * BLK, RET_W), row(0)), _full((1, 512))],
        out_shape=[jax.ShapeDtypeStruct((L, RET_W), BF16), jax.ShapeDtypeStruct((1, 512), F32)],
        scratch_shapes=[pltpu.VMEM((RET_HEADS, BLK, BLK), F32)],
        compiler_params=_params(("arbitrary",)),
    )(dmix, o, proj, proj, proj, proj, cos_t, sin_t, dmat, wq_t, wk_t, ret_g, states)


def _fox_delta(dmix, o_f):
    L = o_f.shape[0]
    nblk = L // BLK
    G = _block_group(nblk)

    def body(do_ref, o_ref, d_ref):
        sel = ((_iota((8, 512), 1) >> 6) == _iota((8, 512), 0)).astype(BF16)
        for b in range(G):
            rows = slice(b * BLK, (b + 1) * BLK)
            prod = do_ref[rows, :].astype(F32) * o_ref[rows, :].astype(F32)
            hi = prod.astype(BF16)
            lo = (prod - hi.astype(F32)).astype(BF16)
            d_ref[b] = _dot_nt(sel, hi) + _dot_nt(sel, lo)

    return pl.pallas_call(
        body, name="b_foxdelta", grid=(nblk // G,),
        in_specs=[pl.BlockSpec((G * BLK, 512), lambda i: (i, 1)), pl.BlockSpec((G * BLK, 512), lambda i: (i, 0))],
        out_specs=pl.BlockSpec((G, 8, BLK), lambda i: (i, 0, 0)),
        out_shape=jax.ShapeDtypeStruct((nblk, 8, BLK), F32),
        compiler_params=_params(("parallel",)),
    )(dmix, o_f)


def _fox_bwd(proj, dmix, c, ctb, lse, delta, scatter=()):
    L = proj.shape[0]
    nblk, nu = _fox_units(L)
    scale = HEAD_LANES ** -0.5
    ns = len(scatter)

    def body(qkv_ref, do_ref, c_ref, ct_ref, lse_ref, dl_ref, *rest):
        s_in, (dp_ref, dc_ref, dcq_ref), s_out = rest[:ns], rest[ns:ns + 3], rest[ns + 3:2 * ns + 3]
        ktt, dqt, dk_acc, dv_acc, dcs_acc = rest[2 * ns + 3:2 * ns + 8]
        p = pl.program_id(0)

        @pl.when(p == 0)
        def _():
            dc_ref[...] = jnp.zeros_like(dc_ref)
            dcq_ref[...] = jnp.zeros_like(dcq_ref)
            if ns:
                for cp in _scatter_copies(s_in, s_out, *rest[2 * ns + 8:]):
                    cp.start()

        lane = _iota((BLK, BLK), 1)
        sub8 = _iota((8, BLK), 0)
        masks = _fox_tile_masks()

        def pre(j, carry):
            off = pl.multiple_of(j * BLK, BLK)
            ktt[j] = qkv_ref[pl.ds(off, BLK), BLK:2 * BLK].astype(F32).T.astype(BF16)
            dqt[j] = jnp.zeros((BLK, BLK), F32)
            return carry

        lax.fori_loop(0, nblk, pre, 0)

        def kv_pass(kblk, nk, n_later):
            klen = nk * BLK
            koff = pl.multiple_of(kblk * BLK, BLK)
            kt = qkv_ref[pl.ds(koff, klen), BLK:2 * BLK]
            vtile = qkv_ref[pl.ds(koff, klen), 2 * BLK:3 * BLK]
            ct = c_ref[pl.ds(koff, klen), :]
            klane = _iota((klen, BLK), 1)
            cs = [jnp.broadcast_to(jnp.sum(jnp.where(klane == 2 * p + e, ct, 0.0), axis=1, keepdims=True),
                                   (klen, WIDE * UNIT)) for e in range(2)]
            dk_acc[0:klen] = jnp.zeros((klen, BLK), F32)
            dv_acc[0:klen] = jnp.zeros((klen, BLK), F32)
            for e in range(2):
                dcs_acc[e, 0:klen] = jnp.zeros((klen, BLK), F32)

            def tile(qblk, nq, mask):
                qlen = nq * BLK
                if mask == "valid":
                    mask = _iota((klen, qlen), 0) >= N_PAD
                qoff = pl.multiple_of(qblk * BLK, BLK)
                qs = qkv_ref[pl.ds(qoff, qlen), 0:BLK].astype(F32) * scale
                dot_ = do_ref[pl.ds(qoff, qlen), :]
                qlane = _iota((qlen, BLK), 1)
                stats = [[ref[qblk + a] for a in range(nq)] for ref in (ct_ref, lse_ref, dl_ref)]
                for e in range(2):
                    h = 2 * p + e
                    head = (qlane >> 6) == e
                    ct_row, lse_row, dl_row = [jnp.concatenate([_pick_row(t, h) for t in ts], axis=1) for ts in stats]
                    qm = jnp.where(head, qs, 0.0).astype(BF16)
                    dom = jnp.where(head, dot_, jnp.zeros_like(dot_))
                    t = _dot_nt(kt, qm) - cs[e][:, 0:qlen]
                    if mask is not None:
                        t = jnp.where(mask, t, NEG)
                    pr = jnp.exp(t + (ct_row - lse_row))
                    dv_acc[0:klen] = dv_acc[0:klen] + _dot(pr.astype(BF16), dom)
                    dsv = pr * (_dot_nt(vtile, dom) - dl_row)
                    ds_b = dsv.astype(BF16)
                    dk_acc[0:klen] = dk_acc[0:klen] + _dot(ds_b, qm)
                    rows = slice(e * HEAD_LANES, (e + 1) * HEAD_LANES)
                    dq_t = _dot(ktt[kblk, rows, :], ds_b[0:BLK])
                    for b in range(1, nk):
                        dq_t = dq_t + _dot(ktt[kblk + b, rows, :], ds_b[b * BLK:(b + 1) * BLK])
                    key_side = dsv[:, 0:BLK]
                    for a in range(1, nq):
                        key_side = key_side + dsv[:, a * BLK:(a + 1) * BLK]
                    dcs_acc[e, 0:klen] = dcs_acc[e, 0:klen] + key_side
                    query_side = jnp.sum(dsv, axis=0, keepdims=True)
                    for a in range(nq):
                        cols = slice(a * BLK, (a + 1) * BLK)
                        dqt[qblk + a, rows, :] = dqt[qblk + a, rows, :] + dq_t[:, cols]
                        dcq_ref[qblk + a] = dcq_ref[qblk + a] + jnp.where(sub8 == h, query_side[:, cols], 0.0)

            later_mask = "valid" if nk == 1 else None
            n_later = jnp.asarray(n_later, jnp.int32)
            n_wide = n_later // WIDE

            def later_wide(i, carry):
                tile(kblk + nk + 2 * WIDE * i, 2 * WIDE, later_mask)
                return carry

            tile(kblk, nk, masks["first"] if nk == 1 else masks["diag"])
            lax.fori_loop(0, n_wide, later_wide, 0)
            rest = kblk + nk + 2 * WIDE * n_wide

            @pl.when((n_later & 2) != 0)
            def _():
                tile(rest, 4, later_mask)

            @pl.when((n_later & 1) != 0)
            def _():
                tile(rest + 2 * (n_later & 2), 2, later_mask)
            dp_ref[pl.ds(koff, klen), BLK:2 * BLK] = dk_acc[0:klen].astype(BF16)
            dp_ref[pl.ds(koff, klen), 2 * BLK:3 * BLK] = dv_acc[0:klen].astype(BF16)
            upd = jnp.zeros((klen, BLK), F32)
            for e in range(2):
                upd = upd + jnp.where(klane == 2 * p + e, -jnp.sum(dcs_acc[e, 0:klen], axis=1, keepdims=True), 0.0)
            dc_ref[pl.ds(koff, klen), :] = dc_ref[pl.ds(koff, klen), :] + upd

        kv_pass(0, 1, nu)

        def k_loop(u, carry):
            kv_pass(1 + 2 * u, 2, nu - 1 - u)
            return carry

        lax.fori_loop(0, nu, k_loop, 0)

        def flush(j, carry):
            off = pl.multiple_of(j * BLK, BLK)
            dp_ref[pl.ds(off, BLK), 0:BLK] = (dqt[j].T * scale).astype(BF16)
            return carry

        lax.fori_loop(0, nblk, flush, 0)

        if ns:
            @pl.when(p == FOX_HEADS // 2 - 1)
            def _():
                copies = _scatter_copies(s_in, s_out, *rest[2 * ns + 8:])
                for cp in copies:
                    cp.wait_recv()
                for cp in copies:
                    cp.wait_send()

    stat = _full((nblk, 8, BLK))
    return pl.pallas_call(
        body, name="b_fox", grid=(FOX_HEADS // 2,),
        in_specs=[pl.BlockSpec((L, 384), lambda p: (0, RET_W // 384 + p)), pl.BlockSpec((L, BLK), lambda p: (0, 4 + p)),
                  _full((L, BLK)), stat, stat, stat] + [_ANY] * ns,
        out_specs=[pl.BlockSpec((L, 384), lambda p: (0, p)), _full((L, BLK)), stat] + [_ANY] * ns,
        out_shape=[jax.ShapeDtypeStruct((L, FOX_W), BF16), jax.ShapeDtypeStruct((L, BLK), F32),
                   jax.ShapeDtypeStruct((nblk, 8, BLK), F32)] + _scatter_shapes(scatter),
        scratch_shapes=[pltpu.VMEM((nblk, BLK, BLK), BF16), pltpu.VMEM((nblk, BLK, BLK), F32),
                        pltpu.VMEM((UNIT, BLK), F32), pltpu.VMEM((UNIT, BLK), F32), pltpu.VMEM((2, UNIT, BLK), F32)]
        + _scatter_semaphores(ns),
        compiler_params=_params(("arbitrary",)),
    )(proj, dmix, c, ctb, lse, delta, *scatter)


def _fox_post(dc, dcq, ff, fb):
    L = dc.shape[0]
    nblk = L // BLK
    G = _block_group(nblk)
    steps = nblk // G

    def body(dc_ref, dcq_ref, ff_ref, b_ref, dff_ref, dffb_ref, dfb_ref, carry):
        @pl.when(pl.program_id(0) == 0)
        def _():
            carry[...] = jnp.zeros_like(carry)
            dfb_ref[...] = jnp.zeros_like(dfb_ref)

        tri = (_iota((BLK, BLK), 0) <= _iota((BLK, BLK), 1)).astype(BF16)
        live = _iota((BLK, BLK), 1) < FOX_HEADS
        run, dfb = carry[...], dfb_ref[...]
        for b in reversed(range(G)):
            rows = slice(b * BLK, (b + 1) * BLK)
            d = dc_ref[rows, :] + jnp.concatenate([dcq_ref[b], jnp.zeros((BLK - 8, BLK), F32)], axis=0).T
            hi, mid, lo = _split3(d)
            dlf = _dot(tri, hi) + _dot(tri, mid) + _dot(tri, lo) + run
            run = run + jnp.sum(d, axis=0, keepdims=True)
            z = ff_ref[rows, :] + b_ref[...]
            dff = jnp.where(live, dlf * jax.nn.sigmoid(-z), 0.0)
            dff_ref[rows, :] = dff
            dffb_ref[rows, :] = dff.astype(BF16)
            dfb = dfb + jnp.sum(dff, axis=0, keepdims=True)
        carry[...] = run
        dfb_ref[...] = dfb

    rev = lambda i: (steps - 1 - i, 0)
    return pl.pallas_call(
        body, name="b_foxpost", grid=(steps,),
        in_specs=[pl.BlockSpec((G * BLK, BLK), rev), pl.BlockSpec((G, 8, BLK), lambda i: (steps - 1 - i, 0, 0)),
                  pl.BlockSpec((G * BLK, BLK), rev), _full((1, BLK))],
        out_specs=[pl.BlockSpec((G * BLK, BLK), rev), pl.BlockSpec((G * BLK, BLK), rev), _full((1, BLK))],
        out_shape=[jax.ShapeDtypeStruct((L, BLK), F32), jax.ShapeDtypeStruct((L, BLK), BF16),
                   jax.ShapeDtypeStruct((1, BLK), F32)],
        scratch_shapes=[pltpu.VMEM((1, BLK), F32)],
        compiler_params=_params(("arbitrary",)),
    )(dc, dcq, ff, fb)


def _inproj_bwd(dpr, dpf, dffb, w_main, w_ff, h0, g, dh1):
    L = h0.shape[0]
    tm = _row_tile(L)

    def body(dpr_ref, dpf_ref, dff_ref, wm_ref, wf_ref, h_ref, g_ref, dh1_ref, dh0_ref, dg_ref):
        @pl.when(pl.program_id(0) == 0)
        def _():
            dg_ref[...] = jnp.zeros_like(dg_ref)

        dn = (_dot_nt(dpr_ref[...], wm_ref[:, 0:RET_W]) + _dot_nt(dpf_ref[...], wm_ref[:, RET_W:MAIN_W])
              + _dot_nt(dff_ref[...], wf_ref[...]))
        h = h_ref[...]
        r = lax.rsqrt(jnp.mean(h * h, axis=-1, keepdims=True) + EPS)
        yn = h * r
        dg_ref[...] = dg_ref[...] + jnp.sum(dn * yn, axis=0, keepdims=True)
        dyn = dn * g_ref[...]
        dh0_ref[...] = dh1_ref[...] + r * (dyn - yn * jnp.mean(dyn * yn, axis=-1, keepdims=True))

    rows = lambda w: pl.BlockSpec((tm, w), lambda i: (i, 0))
    return pl.pallas_call(
        body, name="b_inproj", grid=(L // tm,),
        in_specs=[rows(RET_W), rows(FOX_W), rows(BLK), _full((D_MODEL, MAIN_W)), _full((D_MODEL, BLK)),
                  rows(D_MODEL), _full((1, D_MODEL)), rows(D_MODEL)],
        out_specs=[rows(D_MODEL), _full((1, D_MODEL))],
        out_shape=[jax.ShapeDtypeStruct((L, D_MODEL), F32), jax.ShapeDtypeStruct((1, D_MODEL), F32)],
        compiler_params=_params(("arbitrary",)),
    )(dpr, dpf, dffb, w_main, w_ff, h0, g, dh1)


def _local_step(x, target, meta, attn_g, w_main, w_ff, fox_b, ret_g, w_out, ffn_g, w_up, conv_w, conv_b, w_down, final_g,
                late=None, mid=None):
    S = x.shape[0]
    L = S + PREFIX
    h0 = jnp.concatenate([jnp.zeros((N_PAD, D_MODEL), F32), meta, x], axis=0)
    tgt = jnp.concatenate([jnp.zeros((PREFIX, D_MODEL), F32), target], axis=0)
    fb = jnp.pad(fox_b, ((0, 0), (0, BLK - FOX_HEADS)))
    cos_t, sin_t = _rotary_tables(L)

    n1, proj, ff = _rms_inproj(h0, attn_g, w_main, w_ff)
    c, ctb = _fox_prep(ff, fb)
    mix_r, o_ret, states = _retention_fwd(proj, cos_t, sin_t, ret_g)
    if late is None:
        o_f, lse = _fox_fwd(proj, c, ctb)
    else:
        o_f, lse, *gathered = _fox_fwd(proj, c, ctb, gather=late[0])
        w_out, w_up, w_down = late[1](gathered)
    h1, n2, up, g_act = _outproj_up(mix_r, o_f, h0, w_out, ffn_g, w_up, conv_w, conv_b)
    dh2, dh2b, d_final_g, loss = _ffn_down_loss(g_act, w_down, h1, final_g, tgt)

    dacc, db, dconv = _ffn_bwd_gate(dh2b, w_down, up, conv_w, conv_b)
    dup, dh1, dh1b, dmix, d_ffn_g = _ffn_bwd_up(dacc, db, conv_w, w_up, h1, ffn_g, dh2, w_out)
    d_w_down = _wgrad(g_act, dh2b, "wgrad_down", tk=D_FF // 2)[0]
    d_w_up = _wgrad(n2, dup, "wgrad_up", tn=w_up.shape[2])
    d_w_out = jnp.concatenate([_wgrad(mix_r, dh1b, "wgrad_out_r")[0], _wgrad(o_f, dh1b, "wgrad_out_f")[0]], axis=0)

    dpr, d_ret_g = _retention_bwd(dmix, o_ret, proj, cos_t, sin_t, ret_g, states)
    delta = _fox_delta(dmix, o_f)
    scatter = () if mid is None else mid(d_w_out, d_w_up, d_w_down)
    dpf, dc, dcq, *received = _fox_bwd(proj, dmix, c, ctb, lse, delta, scatter=scatter)
    dff, dffb, d_fox_b = _fox_post(dc, dcq, ff, fb)
    dh0, d_attn_g = _inproj_bwd(dpr, dpf, dffb, w_main, w_ff, h0, attn_g, dh1)
    d_w_main = jnp.concatenate([_wgrad(n1, dpr, "wgrad_in_r")[0], _wgrad(n1, dpf, "wgrad_in_f")[0]], axis=1)
    d_w_ff = _wgrad(n1, dffb, "wgrad_in_ff")[0]

    return dict(
        loss=loss[0, 0], dx=dh0[PREFIX:], dmeta=dh0[N_PAD:PREFIX], attn_g=d_attn_g, w_main=d_w_main,
        w_ff=d_w_ff[:, :FOX_HEADS], fox_b=d_fox_b[:, :FOX_HEADS], ret_g=d_ret_g, w_out=d_w_out, ffn_g=d_ffn_g,
        w_up=d_w_up, conv_w=dconv[0:3], conv_b=dconv[3:4], w_down=d_w_down, final_g=d_final_g,
        scatter=scatter, received=received)


_ANY = pl.BlockSpec(memory_space=pl.ANY)


def _place():
    return lax.axis_index("x"), lax.axis_index("y"), lax.axis_index("c")


def _other_chips(x, y):
    return [(1 - x, y), (x, 1 - y), (1 - x, 1 - y)]


def _allgather_semaphores(n):
    if n == 0:
        return []
    return [pltpu.SemaphoreType.DMA((3 * n,)), pltpu.SemaphoreType.DMA((3 * n,)), pltpu.SemaphoreType.DMA((n,))]


def _allgather_copies(ins, outs, send, recv, loc):
    n = len(ins)
    x, y, c = _place()
    mine = 2 * x + y
    peers = _other_chips(x, y)

    def remote(a, k, slot):
        return pltpu.make_async_remote_copy(
            src_ref=ins[a], dst_ref=outs[a].at[slot], send_sem=send.at[3 * a + k], recv_sem=recv.at[3 * a + k],
            device_id=(peers[k][0], peers[k][1], c), device_id_type=MESH)

    local = [pltpu.make_async_copy(ins[a], outs[a].at[mine], loc.at[a]) for a in range(n)]
    sends = [remote(a, k, mine) for a in range(n) for k in range(3)]
    recvs = [remote(a, k, 2 * peers[k][0] + peers[k][1]) for a in range(n) for k in range(3)]
    return local, sends, recvs


def _chip_allgather(arrays):
    n = len(arrays)

    def body(*refs):
        local, sends, recvs = _allgather_copies(refs[:n], refs[n:2 * n], *refs[2 * n:])
        for cp in local + sends:
            cp.start()
        for cp in recvs:
            cp.wait_recv()
        for cp in sends:
            cp.wait_send()
        for cp in local:
            cp.wait()

    return pl.pallas_call(
        body, name="ag_weights", in_specs=[_ANY] * n, out_specs=[_ANY] * n,
        out_shape=[jax.ShapeDtypeStruct((N_CHIPS,) + a.shape, a.dtype) for a in arrays],
        scratch_shapes=_allgather_semaphores(n),
    )(*arrays)


def _sibling_exchange(grads, small):
    n = len(grads)

    def body(*refs):
        ins, small_in = refs[:n], refs[n]
        outs, small_out = refs[n + 1:2 * n + 1], refs[2 * n + 1]
        send, recv, s_send, s_recv, loc = refs[2 * n + 2:]
        x, y, c = _place()
        me = 4 * x + 2 * y + c

        def half_copy(a, which):
            half = ins[a].shape[1] // 2
            return pltpu.make_async_remote_copy(
                src_ref=ins[a].at[pl.ds(0, N_CHIPS), pl.ds(which * half, half)], dst_ref=outs[a],
                send_sem=send.at[a], recv_sem=recv.at[a], device_id=(x, y, 1 - c), device_id_type=MESH)

        def peer_of(r):
            return tuple(1 - v if (r >> b) & 1 else v for v, b in ((x, 2), (y, 1), (c, 0)))

        def small_copy(r, slot):
            return pltpu.make_async_remote_copy(
                src_ref=small_in, dst_ref=small_out.at[slot], send_sem=s_send.at[r - 1], recv_sem=s_recv.at[r - 1],
                device_id=peer_of(r), device_id_type=MESH)

        local = pltpu.make_async_copy(small_in, small_out.at[me], loc.at[0])
        sends = [half_copy(a, 1 - c) for a in range(n)] + [small_copy(r, me) for r in range(1, N_DEV)]
        local.start()
        for cp in sends:
            cp.start()
        for r in range(1, N_DEV):
            px, py, pc = peer_of(r)
            small_copy(r, 4 * px + 2 * py + pc).wait_recv()
        for a in range(n):
            half_copy(a, c).wait_recv()
        for cp in sends:
            cp.wait_send()
        local.wait()

    rows = small.shape[0]
    return pl.pallas_call(
        body, name="rs_sibling", in_specs=[_ANY] * (n + 1), out_specs=[_ANY] * (n + 1),
        out_shape=[jax.ShapeDtypeStruct((N_CHIPS, g.shape[1] // 2, g.shape[2]), g.dtype) for g in grads]
        + [jax.ShapeDtypeStruct((N_DEV, rows, small.shape[1]), small.dtype)],
        scratch_shapes=[pltpu.SemaphoreType.DMA((n,)), pltpu.SemaphoreType.DMA((n,)),
                        pltpu.SemaphoreType.DMA((N_DEV - 1,)), pltpu.SemaphoreType.DMA((N_DEV - 1,)),
                        pltpu.SemaphoreType.DMA((1,))],
    )(*grads, small)


def _sibling_halves(grads):
    n = len(grads)

    def body(*refs):
        ins, outs = refs[:n], refs[n:2 * n]
        send, recv = refs[2 * n:]
        x, y, c = _place()

        def half_copy(a, which):
            half = ins[a].shape[1] // 2
            return pltpu.make_async_remote_copy(
                src_ref=ins[a].at[pl.ds(0, N_CHIPS), pl.ds(which * half, half)], dst_ref=outs[a],
                send_sem=send.at[a], recv_sem=recv.at[a], device_id=(x, y, 1 - c), device_id_type=MESH)

        sends = [half_copy(a, 1 - c) for a in range(n)]
        for cp in sends:
            cp.start()
        for a in range(n):
            half_copy(a, c).wait_recv()
        for cp in sends:
            cp.wait_send()

    return pl.pallas_call(
        body, name="rs_sibling_early", in_specs=[_ANY] * n, out_specs=[_ANY] * n,
        out_shape=[jax.ShapeDtypeStruct((N_CHIPS, g.shape[1] // 2, g.shape[2]), g.dtype) for g in grads],
        scratch_shapes=[pltpu.SemaphoreType.DMA((n,)), pltpu.SemaphoreType.DMA((n,))],
    )(*grads)


def _chip_reduce_scatter(parts):
    n = len(parts)

    def body(*refs):
        copies = _scatter_copies(refs[:n], refs[n:2 * n], *refs[2 * n:])
        for cp in copies:
            cp.start()
        for cp in copies:
            cp.wait_recv()
        for cp in copies:
            cp.wait_send()

    return pl.pallas_call(
        body, name="rs_chips", in_specs=[_ANY] * n, out_specs=[_ANY] * n,
        out_shape=_scatter_shapes(parts), scratch_shapes=_scatter_semaphores(n),
    )(*parts)


def _scatter_shapes(parts):
    return [jax.ShapeDtypeStruct((3,) + p.shape[1:], p.dtype) for p in parts]


def _scatter_semaphores(n):
    return [pltpu.SemaphoreType.DMA((3 * n,)), pltpu.SemaphoreType.DMA((3 * n,))] if n else []


def _scatter_copies(ins, outs, send, recv):
    x, y, c = _place()
    peers = _other_chips(x, y)
    return [pltpu.make_async_remote_copy(
        src_ref=ins[a].at[2 * peers[k][0] + peers[k][1]], dst_ref=outs[a].at[k], send_sem=send.at[3 * a + k],
        recv_sem=recv.at[3 * a + k], device_id=(peers[k][0], peers[k][1], c), device_id_type=MESH)
        for a in range(len(ins)) for k in range(3)]


def _sibling_allgather(bufs):
    n = len(bufs)

    def body(*refs):
        outs = refs[n:2 * n]
        send, recv = refs[2 * n:]
        x, y, c = _place()

        def remote(a, which):
            return pltpu.make_async_remote_copy(
                src_ref=outs[a].at[which], dst_ref=outs[a].at[which], send_sem=send.at[a], recv_sem=recv.at[a],
                device_id=(x, y, 1 - c), device_id_type=MESH)

        sends = [remote(a, c) for a in range(n)]
        for cp in sends:
            cp.start()
        for a in range(n):
            remote(a, 1 - c).wait_recv()
        for cp in sends:
            cp.wait_send()

    outs = pl.pallas_call(
        body, name="ag_sibling", in_specs=[_ANY] * n, out_specs=[_ANY] * n,
        out_shape=[jax.ShapeDtypeStruct(b.shape, b.dtype) for b in bufs],
        input_output_aliases={a: a for a in range(n)},
        scratch_shapes=[pltpu.SemaphoreType.DMA((n,)), pltpu.SemaphoreType.DMA((n,))],
    )(*bufs)
    return [o.reshape(2 * o.shape[1], o.shape[2]) for o in outs]


def _pair_add(full, recv, core, name):
    _, R, C = full.shape
    half = R // 2

    def body(core_ref, a_ref, b_ref, o_ref):
        o_ref[...] = (a_ref[...] + b_ref[...]).astype(BF16)

    return pl.pallas_call(
        body, name=name,
        grid_spec=pltpu.PrefetchScalarGridSpec(
            num_scalar_prefetch=1, grid=(N_CHIPS,),
            in_specs=[pl.BlockSpec((1, half, C), lambda j, core_ref: (j, core_ref[0], 0)),
                      pl.BlockSpec((1, half, C), lambda j, core_ref: (j, 0, 0))],
            out_specs=pl.BlockSpec((1, half, C), lambda j, core_ref: (j, 0, 0))),
        out_shape=jax.ShapeDtypeStruct((N_CHIPS, half, C), BF16),
        compiler_params=_params(("parallel",)),
    )(core, full, recv)


def _sum_slots(q, name, tiles=2):
    n, R, C = q.shape
    tr = R // tiles

    def body(q_ref, o_ref):
        acc = q_ref[0].astype(F32)
        for j in range(1, n):
            acc = acc + q_ref[j].astype(F32)
        o_ref[...] = acc

    return pl.pallas_call(
        body, name=name, grid=(tiles,),
        in_specs=[pl.BlockSpec((n, tr, C), lambda i: (0, i, 0))],
        out_specs=pl.BlockSpec((tr, C), lambda i: (i, 0)),
        out_shape=jax.ShapeDtypeStruct((R, C), F32),
        compiler_params=_params(("parallel",)),
    )(q)


def _sum_partials(own_all, recv, place, name, tiles=2):
    _, R, C = own_all.shape
    tr = R // tiles

    def body(place_ref, own_ref, r_ref, o_ref):
        acc = own_ref[0].astype(F32)
        for k in range(3):
            acc = acc + r_ref[k].astype(F32)
        o_ref[0] = acc

    return pl.pallas_call(
        body, name=name,
        grid_spec=pltpu.PrefetchScalarGridSpec(
            num_scalar_prefetch=1, grid=(tiles,),
            in_specs=[pl.BlockSpec((1, tr, C), lambda i, place_ref: (place_ref[0], i, 0)),
                      pl.BlockSpec((3, tr, C), lambda i, place_ref: (0, i, 0))],
            out_specs=pl.BlockSpec((1, tr, C), lambda i, place_ref: (place_ref[1], i, 0))),
        out_shape=jax.ShapeDtypeStruct((2, R, C), F32),
        compiler_params=_params(("parallel",)),
    )(place, own_all, recv)


def _adamw(w, g, m, v, name, tiles=4):
    R, C = w.shape
    tr = R // tiles

    def body(w_ref, g_ref, m_ref, v_ref, go_ref, d_ref, m2_ref, v2_ref):
        g_ = g_ref[...]
        go_ref[...] = g_
        m2 = ADAM_B1 * m_ref[...] + (1.0 - ADAM_B1) * g_
        v2 = ADAM_B2 * v_ref[...] + (1.0 - ADAM_B2) * (g_ * g_)
        m_hat = m2 / (1.0 - ADAM_B1 ** ADAM_STEP)
        v_hat = v2 / (1.0 - ADAM_B2 ** ADAM_STEP)
        d_ref[...] = -ADAM_LR * (m_hat / (jnp.sqrt(v_hat) + ADAM_EPS) + ADAM_WD * w_ref[...])
        m2_ref[...] = m2
        v2_ref[...] = v2

    spec = pl.BlockSpec((tr, C), lambda i: (i, 0))
    return pl.pallas_call(
        body, name=name, grid=(tiles,), in_specs=[spec] * 4, out_specs=[spec] * 4,
        out_shape=[jax.ShapeDtypeStruct((R, C), F32)] * 4,
        compiler_params=_params(("parallel",)),
    )(w, g, m, v)


def _pack_rows(pieces, rows):
    flat = jnp.concatenate([jnp.pad(p.reshape(-1).astype(F32), (0, (-p.size) % D_MODEL)) for p in pieces])
    return jnp.pad(flat, (0, rows * D_MODEL - flat.size)).reshape(rows, D_MODEL)


def _unpack_rows(pack, shapes):
    flat = pack.reshape(-1)
    out, off = [], 0
    for shp in shapes:
        size = int(np.prod(shp))
        out.append(flat[off:off + size].reshape(shp))
        off += size + (-size) % D_MODEL
    return out


def _kernel_order(w):
    parts = [w[:, 0:RET_W]]
    for p in range(FOX_HEADS // 2):
        parts += [w[:, RET_W + part * 512 + p * BLK:RET_W + part * 512 + (p + 1) * BLK] for part in range(3)]
    return jnp.concatenate(parts, axis=1)


def _reference_order(g_main, g_ff):
    parts = [g_main[:, 0:RET_W]]
    for part in range(3):
        parts += [g_main[:, RET_W + 384 * p + part * BLK:RET_W + 384 * p + (part + 1) * BLK] for p in range(FOX_HEADS // 2)]
    return jnp.concatenate(parts + [g_ff], axis=1)


def kernel(x, meta_tokens, attn_norm_g, w_in, fox_forget_b, ret_norm_g, w_out, ffn_norm_g, w_up, conv_w, conv_b, w_down, final_norm_g, loss_target, m_meta_tokens, m_attn_norm_g, m_w_in, m_fox_forget_b, m_ret_norm_g, m_w_out, m_ffn_norm_g, m_w_up, m_conv_w, m_conv_b, m_w_down, m_final_norm_g, v_meta_tokens, v_attn_norm_g, v_w_in, v_fox_forget_b, v_ret_norm_g, v_w_out, v_ffn_norm_g, v_w_up, v_conv_w, v_conv_b, v_w_down, v_final_norm_g):
    chip = 2 * lax.axis_index("x") + lax.axis_index("y")
    core = lax.axis_index("c")
    meta_w, conv_sw = meta_tokens.shape[1], conv_w.shape[2]

    small_w = _pack_rows([meta_tokens, conv_w[0]], 8)
    g_in, g_small = _chip_allgather([w_in[0].astype(BF16), small_w])
    w_in_full = g_in.transpose(1, 0, 2).reshape(D_MODEL, IN_WIDTH)
    w_main = _kernel_order(w_in_full)
    w_ff = jnp.pad(w_in_full[:, MAIN_W:], ((0, 0), (0, BLK - FOX_HEADS)))
    small_parts = [_unpack_rows(g_small[j], [meta_tokens.shape, conv_w.shape[1:]]) for j in range(N_CHIPS)]
    meta_full = jnp.concatenate([sp[0] for sp in small_parts], axis=1)
    conv_w_full = jnp.concatenate([sp[1] for sp in small_parts], axis=1)

    core_idx = core.reshape(1).astype(jnp.int32)
    place = jnp.stack([chip, core]).astype(jnp.int32)

    def assemble(gathered):
        g_out, g_up, g_down = gathered
        return g_out.reshape(D_MODEL, D_MODEL), g_up, g_down.reshape(D_FF, D_MODEL)

    def early_reduce(d_w_out, d_w_up, d_w_down):
        early = [d_w_out.reshape(N_CHIPS, -1, D_MODEL), d_w_up, d_w_down.reshape(N_CHIPS, -1, D_MODEL)]
        from_sib = _sibling_halves(early)
        return [_pair_add(g, r, core_idx, "pair_add_" + nm) for g, r, nm in zip(early, from_sib, ("out", "up", "down"))]

    out = _local_step(x[0], loss_target[0], meta_full, attn_norm_g, w_main, w_ff, fox_forget_b, ret_norm_g,
                      None, ffn_norm_g, None, conv_w_full, conv_b, None, final_norm_g[None],
                      late=([w_out[0].astype(BF16), w_up[0].astype(BF16), w_down[0].astype(BF16)], assemble),
                      mid=early_reduce)

    g_in_full = _reference_order(out["w_main"], out["w_ff"]).reshape(D_MODEL, N_CHIPS, -1).transpose(1, 0, 2)
    small_shapes = [(1, D_MODEL), (1, D_MODEL), (1, D_MODEL), (1, 512 + FOX_HEADS + 1), (1, D_FF), (N_META, D_MODEL), (3, D_FF)]
    small = _pack_rows([out["attn_g"], out["ffn_g"], out["final_g"],
                        jnp.concatenate([out["ret_g"], out["fox_b"], out["loss"].reshape(1, 1)], axis=1),
                        out["conv_b"], out["dmeta"], out["conv_w"]], 32)
    from_sibling_in, small_all = _sibling_exchange([g_in_full], small)
    sum_in = _pair_add(g_in_full, from_sibling_in, core_idx, "pair_add_in")
    (from_chips_in,) = _chip_reduce_scatter([sum_in])
    chip_sums = [sum_in] + list(out["scatter"])
    from_chips = [from_chips_in] + list(out["received"])
    names = ("in", "out", "up", "down")
    totals = [_sum_partials(s, q, place, "sum_chips_" + nm) for s, q, nm in zip(chip_sums, from_chips, names)]
    grad_in, grad_out, grad_up, grad_down = _sibling_allgather(totals)
    s_attn, s_ffn, s_final, s_misc, s_conv_b, s_meta, s_conv_w = _unpack_rows(
        _sum_slots(small_all, "sum_small", tiles=1), small_shapes)
    loss = s_misc[0, 512 + FOX_HEADS]
    small_grads = [lax.dynamic_slice_in_dim(s_meta, chip * meta_w, meta_w, axis=1), s_attn, s_misc[:, 512:512 + FOX_HEADS],
                   s_misc[:, :512], s_ffn, lax.dynamic_slice_in_dim(s_conv_w, chip * conv_sw, conv_sw, axis=1)[None],
                   s_conv_b, s_final[0]]

    big_w = [(w_in, m_w_in, v_w_in, grad_in, "adamw_in"), (w_out, m_w_out, v_w_out, grad_out, "adamw_out"),
             (w_up, m_w_up, v_w_up, grad_up, "adamw_up"), (w_down, m_w_down, v_w_down, grad_down, "adamw_down")]
    big_res = [[r[None] for r in _adamw(w[0], g, m[0], v[0], nm)] for w, m, v, g, nm in big_w]
    small_w_list = [meta_tokens, attn_norm_g, fox_forget_b, ret_norm_g, ffn_norm_g, conv_w, conv_b, final_norm_g]
    small_m = [m_meta_tokens, m_attn_norm_g, m_fox_forget_b, m_ret_norm_g, m_ffn_norm_g, m_conv_w, m_conv_b, m_final_norm_g]
    small_v = [v_meta_tokens, v_attn_norm_g, v_fox_forget_b, v_ret_norm_g, v_ffn_norm_g, v_conv_w, v_conv_b, v_final_norm_g]
    shapes = [a.shape for a in small_w_list]
    packs = [_pack_rows(lst, 16) for lst in (small_w_list, small_grads, small_m, small_v)]
    small_res = [_unpack_rows(r, shapes) for r in _adamw(*packs, "adamw_small", tiles=1)[1:]]
    small_grads = [g.reshape(s) for g, s in zip(small_grads, shapes)]

    def ordered(kind):
        sm = small_grads if kind == 0 else small_res[kind - 1]
        bg = [r[kind] for r in big_res]
        return [sm[0], sm[1], bg[0], sm[2], sm[3], bg[1], sm[4], bg[2], sm[5], sm[6], bg[3], sm[7]]

    return (loss, out["dx"][None], *ordered(0), *ordered(1), *ordered(2), *ordered(3))
```

```python
import functools

import numpy as np
import jax
import jax.numpy as jnp
from jax import lax
from jax.experimental import pallas as pl
from jax.experimental.pallas import tpu as pltpu

F32 = jnp.float32
BF16 = jnp.bfloat16

D_MODEL = 1024
N_META = 16
BLK = 128
UNIT = 2 * BLK
FOX_PAIRS = 2
WIDE = 4
CHUNK = 64
N_PAD = BLK - N_META
PREFIX = BLK
RET_HEADS = 4
FOX_HEADS = 8
HEAD_LANES = 64
D_FF = 2816
ROPE_BASE = 10000.0
EPS = 1e-6
NEG = -1e30
RET_W = 1536
FOX_W = 1536
MAIN_W = RET_W + FOX_W
IN_WIDTH = MAIN_W + FOX_HEADS
N_CHIPS = 4
N_DEV = 8

ADAM_LR = 0.001
ADAM_B1 = 0.9
ADAM_B2 = 0.999
ADAM_EPS = 1e-08
ADAM_WD = 0.01
ADAM_STEP = 10

MESH = pl.DeviceIdType.MESH
VMEM_LIMIT_MB = 56

_NT = (((1,), (1,)), ((), ()))
_TN = (((0,), (0,)), ((), ()))


def _dot(a, b):
    return jnp.dot(a, b, preferred_element_type=F32)


def _dot_nt(a, b):
    return lax.dot_general(a, b, _NT, preferred_element_type=F32)


def _dot_tn(a, b):
    return lax.dot_general(a, b, _TN, preferred_element_type=F32)


def _params(dims=None, vmem_mb=VMEM_LIMIT_MB):
    kw = dict(vmem_limit_bytes=vmem_mb << 20)
    if dims is not None:
        kw["dimension_semantics"] = dims
    return pltpu.CompilerParams(**kw)


def _row_tile(n, prefs=(384, 256, 128)):
    for t in prefs:
        if n % t == 0:
            return t
    raise ValueError(f"no row tile for {n}")


def _iota(shape, dim):
    return lax.broadcasted_iota(jnp.int32, shape, dim)


def _pick_row(tile, row):
    sub = _iota(tile.shape, 0)
    return jnp.sum(jnp.where(sub == row, tile, 0.0), axis=0, keepdims=True)


def _split3(x):
    hi = x.astype(BF16)
    r1 = x - hi.astype(F32)
    mid = r1.astype(BF16)
    lo = (r1 - mid.astype(F32)).astype(BF16)
    return hi, mid, lo


def _full(shape):
    nd = len(shape)
    return pl.BlockSpec(shape, lambda *_: (0,) * nd)


def _in_perm():
    cols = list(range(RET_W))
    for p in range(FOX_HEADS // 2):
        for part in range(3):
            start = RET_W + part * 512 + p * BLK
            cols += list(range(start, start + BLK))
    return np.asarray(cols, np.int32)


def _rotary_tables(L):
    half = HEAD_LANES // 2
    inv = 1.0 / (ROPE_BASE ** (jnp.arange(half, dtype=F32) / half))
    ang = jnp.arange(L).astype(F32)[:, None] * inv[None, :]
    cos, sin = jnp.cos(ang), jnp.sin(ang)
    cos_t = jnp.tile(cos, (1, 4))
    sin_t = jnp.tile(jnp.concatenate([-sin, sin], axis=1), (1, 2))
    return cos_t, sin_t


def _decay_tables():
    gam = 1.0 - 2.0 ** (-5.0 - np.arange(RET_HEADS, dtype=np.float64))
    n = np.arange(BLK)
    same_or_past = (n[:, None] // CHUNK) >= (n[None, :] // CHUNK)
    dist = np.abs(n[:, None] - n[None, :])
    dmat = np.stack([np.where(same_or_past, g ** dist, 0.0) for g in gam]).astype(np.float32)
    lane_head = np.arange(BLK) // HEAD_LANES
    wq = np.stack([gam[2 * p + lane_head][None, :] ** (n[:, None] + 1.0) for p in range(2)]).astype(np.float32)
    wk = np.stack([gam[2 * p + lane_head][None, :] ** (BLK - 1.0 - n[:, None]) for p in range(2)]).astype(np.float32)
    g_blk = tuple(float(g ** BLK) for g in gam)
    return jnp.asarray(dmat), jnp.asarray(wq), jnp.asarray(wk), g_blk


def _rms_inproj(h0, g, w_main, w_ff):
    L = h0.shape[0]
    tm = _row_tile(L)

    def body(h_ref, g_ref, wm_ref, wf_ref, n_ref, p_ref, ff_ref):
        h = h_ref[...]
        r = lax.rsqrt(jnp.mean(h * h, axis=-1, keepdims=True) + EPS)
        n = (h * r * g_ref[...]).astype(BF16)
        n_ref[...] = n
        p_ref[...] = _dot(n, wm_ref[...]).astype(BF16)
        ff_ref[...] = _dot(n, wf_ref[...])

    return pl.pallas_call(
        body, name="f_inproj", grid=(L // tm,),
        in_specs=[pl.BlockSpec((tm, D_MODEL), lambda i: (i, 0)), _full((1, D_MODEL)),
                  _full((D_MODEL, MAIN_W)), _full((D_MODEL, BLK))],
        out_specs=[pl.BlockSpec((tm, D_MODEL), lambda i: (i, 0)), pl.BlockSpec((tm, MAIN_W), lambda i: (i, 0)),
                   pl.BlockSpec((tm, BLK), lambda i: (i, 0))],
        out_shape=[jax.ShapeDtypeStruct((L, D_MODEL), BF16), jax.ShapeDtypeStruct((L, MAIN_W), BF16),
                   jax.ShapeDtypeStruct((L, BLK), F32)],
        compiler_params=_params(("parallel",)),
    )(h0, g, w_main, w_ff)


def _block_group(nblk):
    return 3 if nblk % 3 == 0 else 1


def _fox_prep(ff, fb):
    L = ff.shape[0]
    nblk = L // BLK
    G = _block_group(nblk)

    def body(ff_ref, b_ref, c_ref, ct_ref, carry):
        @pl.when(pl.program_id(0) == 0)
        def _():
            carry[...] = jnp.zeros_like(carry)

        tri = (_iota((BLK, BLK), 0) >= _iota((BLK, BLK), 1)).astype(BF16)
        live = _iota((BLK, BLK), 1) < FOX_HEADS
        run = carry[...]
        for b in range(G):
            z = ff_ref[b * BLK:(b + 1) * BLK, :] + b_ref[...]
            lf = jnp.where(live, jnp.minimum(z, 0.0) - jnp.log1p(jnp.exp(-jnp.abs(z))), 0.0)
            hi, mid, lo = _split3(lf)
            cs = _dot(tri, hi) + _dot(tri, mid) + _dot(tri, lo) + run
            c_ref[b * BLK:(b + 1) * BLK, :] = cs
            ct_ref[b] = cs.T[0:8, :]
            run = run + jnp.sum(lf, axis=0, keepdims=True)
        carry[...] = run

    return pl.pallas_call(
        body, name="f_foxprep", grid=(nblk // G,),
        in_specs=[pl.BlockSpec((G * BLK, BLK), lambda i: (i, 0)), _full((1, BLK))],
        out_specs=[pl.BlockSpec((G * BLK, BLK), lambda i: (i, 0)), pl.BlockSpec((G, 8, BLK), lambda i: (i, 0, 0))],
        out_shape=[jax.ShapeDtypeStruct((L, BLK), F32), jax.ShapeDtypeStruct((nblk, 8, BLK), F32)],
        scratch_shapes=[pltpu.VMEM((1, BLK), F32)],
        compiler_params=_params(("arbitrary",)),
    )(ff, fb)


def _rot_fns(cos, sin):
    lane = _iota((BLK, BLK), 1)
    first = (lane & (HEAD_LANES - 1)) < HEAD_LANES // 2

    def swap(x):
        return jnp.where(first, pltpu.roll(x, BLK - 32, 1), pltpu.roll(x, 32, 1))

    def rot(x):
        return x * cos + swap(x) * sin

    def rot_t(dy):
        return dy * cos + swap(dy * sin)

    return rot, rot_t


def _retention_fwd(proj, cos_t, sin_t, ret_g):
    L = proj.shape[0]
    nblk = L // BLK
    G = _block_group(nblk)
    dmat, wq_t, wk_t, g_blk = _decay_tables()

    def body(q_ref, k_ref, v_ref, gate_ref, cos_ref, sin_ref, d_ref, wq_ref, wk_ref, rg_ref,
             mix_ref, o_ref, rs_ref, state):
        @pl.when(pl.program_id(0) == 0)
        def _():
            state[...] = jnp.zeros_like(state)

        lane = _iota((BLK, BLK), 1)
        sub = _iota((BLK, BLK), 0)
        for b in range(G):
            rows = slice(b * BLK, (b + 1) * BLK)
            rot, _ = _rot_fns(cos_ref[rows, :], sin_ref[rows, :])
            for p in range(2):
                qr = rot(q_ref[rows, p * BLK:(p + 1) * BLK].astype(F32))
                kr = rot(k_ref[rows, p * BLK:(p + 1) * BLK].astype(F32)) * (HEAD_LANES ** -0.5)
                kr_b = kr.astype(BF16)
                qw = (qr * wq_ref[p]).astype(BF16)
                kw = (kr * wk_ref[p]).astype(BF16)
                for e in range(2):
                    h = 2 * p + e
                    cols = slice(h * BLK, (h + 1) * BLK)
                    qm = jnp.where((lane >> 6) == e, qr, 0.0).astype(BF16)
                    s = _dot_nt(qm, kr_b) * d_ref[h]
                    vh = v_ref[rows, cols]
                    st = state[h]
                    rs_ref[b, h] = st
                    o = _dot(s.astype(BF16), vh) + _dot(qw, st.astype(BF16))
                    u = jnp.where((sub >> 6) == e, _dot_tn(kw, vh), 0.0)
                    state[h] = g_blk[h] * st + u
                    rn = lax.rsqrt(jnp.mean(o * o, axis=-1, keepdims=True) + EPS)
                    gate = gate_ref[rows, cols].astype(F32)
                    o_ref[rows, cols] = o
                    mix_ref[rows, cols] = (o * rn * rg_ref[:, cols] * (gate * jax.nn.sigmoid(gate))).astype(BF16)

    row = lambda c: (lambda i: (i, c))
    return pl.pallas_call(
        body, name="f_retention", grid=(nblk // G,),
        in_specs=[pl.BlockSpec((G * BLK, 256), row(0)), pl.BlockSpec((G * BLK, 256), row(1)),
                  pl.BlockSpec((G * BLK, 512), row(1)), pl.BlockSpec((G * BLK, 512), row(2)),
                  pl.BlockSpec((G * BLK, BLK), row(0)), pl.BlockSpec((G * BLK, BLK), row(0)),
                  _full((RET_HEADS, BLK, BLK)), _full((2, BLK, BLK)), _full((2, BLK, BLK)), _full((1, 512))],
        out_specs=[pl.BlockSpec((G * BLK, 512), row(0)), pl.BlockSpec((G * BLK, 512), row(0)),
                   pl.BlockSpec((G, RET_HEADS, BLK, BLK), lambda i: (i, 0, 0, 0))],
        out_shape=[jax.ShapeDtypeStruct((L, 512), BF16), jax.ShapeDtypeStruct((L, 512), F32),
                   jax.ShapeDtypeStruct((nblk, RET_HEADS, BLK, BLK), F32)],
        scratch_shapes=[pltpu.VMEM((RET_HEADS, BLK, BLK), F32)],
        compiler_params=_params(("arbitrary",)),
    )(proj, proj, proj, proj, cos_t, sin_t, dmat, wq_t, wk_t, ret_g)


def _fox_units(L):
    nblk = L // BLK
    assert L % BLK == 0 and nblk % 2 == 1, "sequence must be one 128-row block plus whole 256-row tiles"
    return nblk, (nblk - 1) // 2


def _fox_tile_masks():
    sub, lane = _iota((BLK, BLK), 0), _iota((BLK, BLK), 1)
    valid = _iota((BLK, UNIT), 0) >= N_PAD
    diag = _iota((UNIT, UNIT), 0) <= _iota((UNIT, UNIT), 1)
    r, q = _iota((BLK + UNIT, UNIT), 0), _iota((BLK + UNIT, UNIT), 1)
    first_and_diag = ((r < BLK) & (r >= N_PAD)) | ((r >= BLK) & (r - BLK <= q))
    return dict(first=(sub <= lane) & (sub >= N_PAD), valid=valid, diag=diag, first_and_diag=first_and_diag)


def _fox_fwd(proj, c, ctb, gather=()):
    L = proj.shape[0]
    nblk, nu = _fox_units(L)
    scale = HEAD_LANES ** -0.5
    ng = len(gather)
    steps = FOX_HEADS // (2 * FOX_PAIRS)

    def body(qkv_ref, c_ref, ct_ref, *rest):
        g_in, (of_ref, lse_ref), g_out = rest[:ng], rest[ng:ng + 2], rest[ng + 2:2 * ng + 2]
        vt, csb = rest[2 * ng + 2:2 * ng + 4]
        p = pl.program_id(0)
        heads = [(pp, e, 2 * FOX_PAIRS * p + 2 * pp + e) for pp in range(FOX_PAIRS) for e in range(2)]

        @pl.when(p == 0)
        def _():
            lse_ref[...] = jnp.zeros_like(lse_ref)
            if ng:
                local, sends, _ = _allgather_copies(g_in, g_out, *rest[2 * ng + 4:])
                for cp in local + sends:
                    cp.start()

        lane = _iota((BLK, BLK), 1)
        sub8 = _iota((8, BLK), 0)
        masks = _fox_tile_masks()

        def pre(j, carry):
            off = pl.multiple_of(j * BLK, BLK)
            ct = c_ref[pl.ds(off, BLK), :]
            for pp in range(FOX_PAIRS):
                vt[pp, j] = qkv_ref[pl.ds(off, BLK), pp * 384 + 2 * BLK:pp * 384 + 3 * BLK].astype(F32).T.astype(BF16)
            for hh, (_, _, h) in enumerate(heads):
                col = jnp.sum(jnp.where(lane == h, ct, 0.0), axis=1, keepdims=True)
                csb[hh, j] = jnp.broadcast_to(col, (BLK, BLK))
            return carry

        lax.fori_loop(0, nblk, pre, 0)

        def attend(qblk, nq, n_whole):
            qlen = nq * BLK
            qoff = pl.multiple_of(qblk * BLK, BLK)
            qlane = _iota((qlen, BLK), 1)
            qs = [qkv_ref[pl.ds(qoff, qlen), pp * 384:pp * 384 + BLK].astype(F32) * scale for pp in range(FOX_PAIRS)]
            qm = [jnp.where((qlane >> 6) == e, qs[pp], 0.0).astype(BF16) for pp, e, _ in heads]
            ct_row = [jnp.concatenate([_pick_row(ct_ref[qblk + a], h) for a in range(nq)], axis=1) for _, _, h in heads]

            def step(segs, mask, st):
                blocks = [kblk + b for kblk, nk in segs for b in range(nk)]
                kts = []
                for pp in range(FOX_PAIRS):
                    kt = [qkv_ref[pl.ds(pl.multiple_of(kblk * BLK, BLK), nk * BLK), pp * 384 + BLK:pp * 384 + 2 * BLK]
                          for kblk, nk in segs]
                    kts.append(kt[0] if len(kt) == 1 else jnp.concatenate(kt, axis=0))
                out = []
                for hh, (pp, e, _) in enumerate(heads):
                    m, l, acc = st[3 * hh:3 * hh + 3]
                    s = _dot_nt(kts[pp], qm[hh])
                    t = jnp.concatenate([s[b * BLK:(b + 1) * BLK] - jnp.concatenate([csb[hh, blk]] * nq, axis=1)
                                         for b, blk in enumerate(blocks)], axis=0)
                    if mask is not None:
                        t = jnp.where(mask, t, NEG)
                    m_new = jnp.maximum(m, jnp.max(t, axis=0, keepdims=True) + ct_row[hh])
                    alpha = jnp.exp(m - m_new)
                    pr = jnp.exp(t - (m_new - ct_row[hh]))
                    l = alpha * l + jnp.sum(pr, axis=0, keepdims=True)
                    pr_b = pr.astype(BF16)
                    pv = None
                    for b, blk in enumerate(blocks):
                        part = _dot(vt[pp, blk, e * HEAD_LANES:(e + 1) * HEAD_LANES, :], pr_b[b * BLK:(b + 1) * BLK])
                        pv = part if pv is None else pv + part
                    out += [m_new, l, alpha * acc + pv]
                return tuple(out)

            st = (jnp.full((1, qlen), NEG, F32), jnp.zeros((1, qlen), F32),
                  jnp.zeros((HEAD_LANES, qlen), F32)) * len(heads)
            if nq == 1:
                st = step([(0, 1)], masks["first"], st)
            else:
                st = step([(0, 1), (qblk, 2)], masks["first_and_diag"], st)
                n_wide = n_whole // WIDE
                st = lax.fori_loop(0, n_wide, lambda j, s_: step([(1 + 2 * WIDE * j, 2 * WIDE)], None, s_), st)
                rest = 1 + 2 * WIDE * n_wide
                st = lax.cond((n_whole & 2) != 0, lambda s_: step([(rest, 4)], None, s_), lambda s_: s_, st)
                st = lax.cond((n_whole & 1) != 0, lambda s_: step([(rest + 2 * (n_whole & 2), 2)], None, s_),
                              lambda s_: s_, st)
            for pp in range(FOX_PAIRS):
                lo, hi = st[6 * pp:6 * pp + 3], st[6 * pp + 3:6 * pp + 6]
                o_t = jnp.concatenate([lo[2] * (1.0 / lo[1]), hi[2] * (1.0 / hi[1])], axis=0)
                of_ref[pl.ds(qoff, qlen), pp * BLK:(pp + 1) * BLK] = o_t.T.astype(BF16)
            lse = [st[3 * hh] + jnp.log(st[3 * hh + 1]) for hh in range(len(heads))]
            for a in range(nq):
                upd = jnp.zeros((8, BLK), F32)
                for hh, (_, _, h) in enumerate(heads):
                    upd = upd + jnp.where(sub8 == h, lse[hh][:, a * BLK:(a + 1) * BLK], 0.0)
                lse_ref[qblk + a] = lse_ref[qblk + a] + upd

        attend(0, 1, 0)

        def q_loop(u, carry):
            attend(1 + 2 * u, 2, u)
            return carry

        lax.fori_loop(0, nu, q_loop, 0)

        if ng:
            @pl.when(p == steps - 1)
            def _():
                local, sends, recvs = _allgather_copies(g_in, g_out, *rest[2 * ng + 4:])
                for cp in recvs:
                    cp.wait_recv()
                for cp in sends:
                    cp.wait_send()
                for cp in local:
                    cp.wait()

    width = 384 * FOX_PAIRS
    return pl.pallas_call(
        body, name="f_fox", grid=(steps,),
        in_specs=[pl.BlockSpec((L, width), lambda p: (0, RET_W // width + p)), _full((L, BLK)), _full((nblk, 8, BLK))]
        + [_ANY] * ng,
        out_specs=[pl.BlockSpec((L, FOX_PAIRS * BLK), lambda p: (0, p)), _full((nblk, 8, BLK))] + [_ANY] * ng,
        out_shape=[jax.ShapeDtypeStruct((L, 512), BF16), jax.ShapeDtypeStruct((nblk, 8, BLK), F32)]
        + [jax.ShapeDtypeStruct((N_CHIPS,) + a.shape, a.dtype) for a in gather],
        scratch_shapes=[pltpu.VMEM((FOX_PAIRS, nblk, BLK, BLK), BF16), pltpu.VMEM((2 * FOX_PAIRS, nblk, BLK, BLK), F32)]
        + _allgather_semaphores(ng),
        compiler_params=_params(("arbitrary",)),
    )(proj, c, ctb, *gather)


def _outproj_up(mix_r, o_f, h0, w_out, ffn_g, w_up, conv_w, conv_b):
    L = h0.shape[0]
    tm = _row_tile(L)
    shard = w_up.shape[2]
    assert 2 * shard == D_FF
    cw = [conv_w[j:j + 1] for j in range(3)]
    resident = lambda shape: pl.BlockSpec(shape, lambda i: (0,) * len(shape), pipeline_mode=pl.Buffered(1))

    def body(mr_ref, of_ref, h0_ref, wo_ref, g_ref, wu_ref, cw0, cw1, cw2, cb_ref,
             h1_ref, n2_ref, up_ref, act_ref, halo):
        i = pl.program_id(0)

        @pl.when(i == 0)
        def _():
            halo[...] = jnp.zeros_like(halo)

        h1 = h0_ref[...] + _dot(mr_ref[...], wo_ref[0:512, :]) + _dot(of_ref[...], wo_ref[512:1024, :])
        h1_ref[...] = h1
        r = lax.rsqrt(jnp.mean(h1 * h1, axis=-1, keepdims=True) + EPS)
        n2 = (h1 * r * g_ref[...]).astype(BF16)
        n2_ref[...] = n2
        live = i * tm + _iota((tm, 1), 0) >= N_PAD
        for half in range(2):
            cols = slice(half * shard, (half + 1) * shard)
            a_b = _dot(n2, wu_ref[half]).astype(BF16)
            b_b = _dot(n2, wu_ref[2 + half]).astype(BF16)
            up_ref[:, cols] = a_b
            up_ref[:, D_FF + half * shard:D_FF + (half + 1) * shard] = b_b
            a = jnp.where(live, a_b.astype(F32), 0.0)
            _, _, acc = _conv_taps(a, halo[:, cols], [cw0[:, cols], cw1[:, cols], cw2[:, cols]], cb_ref[:, cols])
            act_ref[:, cols] = (acc * jax.nn.sigmoid(acc) * b_b.astype(F32)).astype(BF16)
            halo[:, cols] = a[tm - 8:tm, :]

    rows = lambda w: pl.BlockSpec((tm, w), lambda i: (i, 0))
    return pl.pallas_call(
        body, name="f_outproj_up", grid=(L // tm,),
        in_specs=[rows(512), rows(512), rows(D_MODEL), resident((D_MODEL, D_MODEL)), _full((1, D_MODEL)),
                  resident((N_CHIPS, D_MODEL, shard)), _full((1, D_FF)), _full((1, D_FF)), _full((1, D_FF)),
                  _full((1, D_FF))],
        out_specs=[rows(D_MODEL), rows(D_MODEL), rows(2 * D_FF), rows(D_FF)],
        out_shape=[jax.ShapeDtypeStruct((L, D_MODEL), F32), jax.ShapeDtypeStruct((L, D_MODEL), BF16),
                   jax.ShapeDtypeStruct((L, 2 * D_FF), BF16), jax.ShapeDtypeStruct((L, D_FF), BF16)],
        scratch_shapes=[pltpu.VMEM((8, D_FF), F32)],
        compiler_params=_params(("arbitrary",)),
    )(mix_r, o_f, h0, w_out, ffn_g, w_up, cw[0], cw[1], cw[2], conv_b)


def _conv_taps(a, halo, cw, cb):
    sub = _iota((a.shape[0], 1), 0)
    a1 = jnp.where(sub == 0, _pick_row(halo, 7), pltpu.roll(a, 1, 0))
    a2 = jnp.where(sub == 0, _pick_row(halo, 6), jnp.where(sub == 1, _pick_row(halo, 7), pltpu.roll(a, 2, 0)))
    acc = cb + a2 * cw[0]
    acc = acc + a1 * cw[1]
    acc = acc + a * cw[2]
    return a1, a2, acc


def _conv_acc(a_ref, halo_ref, cw_refs, cb_ref, i, tm):
    sub = _iota((tm, 1), 0)
    a = jnp.where(i * tm + sub >= N_PAD, a_ref[...].astype(F32), 0.0)
    hrow = i * tm - 8 + _iota((8, 1), 0)
    halo = jnp.where((hrow >= N_PAD) & (i > 0), halo_ref[...].astype(F32), 0.0)
    a1, a2, acc = _conv_taps(a, halo, [r[...] for r in cw_refs], cb_ref[...])
    return a, a1, a2, acc


def _ffn_down_loss(g_act, w_down, h1, final_g, target):
    L = h1.shape[0]
    tm = _row_tile(L)

    def body(g_ref, wd_ref, h1_ref, gf_ref, t_ref, dh_ref, dhb_ref, dgf_ref, loss_ref):
        i = pl.program_id(0)

        @pl.when(i == 0)
        def _():
            dgf_ref[...] = jnp.zeros_like(dgf_ref)
            loss_ref[...] = jnp.zeros_like(loss_ref)

        h2 = h1_ref[...] + _dot(g_ref[...], wd_ref[...])
        r = lax.rsqrt(jnp.mean(h2 * h2, axis=-1, keepdims=True) + EPS)
        yn = h2 * r
        gf = gf_ref[...]
        live = i * tm + _iota((tm, 1), 0) >= PREFIX
        err = jnp.where(live, yn * gf - t_ref[...], 0.0)
        loss_ref[...] = loss_ref[...] + 0.5 * jnp.sum(jnp.mean(err * err, axis=-1, keepdims=True))
        dy = err * (1.0 / D_MODEL)
        dgf_ref[...] = dgf_ref[...] + jnp.sum(dy * yn, axis=0, keepdims=True)
        dyn = dy * gf
        dh = r * (dyn - yn * jnp.mean(dyn * yn, axis=-1, keepdims=True))
        dh_ref[...] = dh
        dhb_ref[...] = dh.astype(BF16)

    rows = lambda w: pl.BlockSpec((tm, w), lambda i: (i, 0))
    return pl.pallas_call(
        body, name="f_ffn_down_loss", grid=(L // tm,),
        in_specs=[rows(D_FF), _full((D_FF, D_MODEL)), rows(D_MODEL), _full((1, D_MODEL)), rows(D_MODEL)],
        out_specs=[rows(D_MODEL), rows(D_MODEL), _full((1, D_MODEL)), _full((1, BLK))],
        out_shape=[jax.ShapeDtypeStruct((L, D_MODEL), F32), jax.ShapeDtypeStruct((L, D_MODEL), BF16),
                   jax.ShapeDtypeStruct((1, D_MODEL), F32), jax.ShapeDtypeStruct((1, BLK), F32)],
        compiler_params=_params(("arbitrary",)),
    )(g_act, w_down, h1, final_g, target)


def _ffn_bwd_gate(dh2b, w_down, up, conv_w, conv_b):
    L = dh2b.shape[0]
    tm = _row_tile(L)
    cw = [conv_w[j:j + 1] for j in range(3)]

    def body(dh_ref, wd_ref, a_ref, halo_ref, b_ref, cw0, cw1, cw2, cb_ref, dacc_ref, db_ref, dcw_ref):
        i = pl.program_id(0)

        @pl.when(i == 0)
        def _():
            dcw_ref[...] = jnp.zeros_like(dcw_ref)

        a, a1, a2, acc = _conv_acc(a_ref, halo_ref, (cw0, cw1, cw2), cb_ref, i, tm)
        dg = _dot_nt(dh_ref[...], wd_ref[...])
        sg = jax.nn.sigmoid(acc)
        db_ref[...] = (dg * acc * sg).astype(BF16)
        dacc = dg * b_ref[...].astype(F32) * (sg * (1.0 + acc * (1.0 - sg)))
        dacc_ref[...] = dacc.astype(BF16)
        sub8 = _iota((8, 1), 0)
        rows = [jnp.sum(dacc * t, axis=0, keepdims=True) for t in (a2, a1, a)] + [jnp.sum(dacc, axis=0, keepdims=True)]
        upd = jnp.zeros((8, D_FF), F32)
        for j, rj in enumerate(rows):
            upd = upd + jnp.where(sub8 == j, rj, 0.0)
        dcw_ref[...] = dcw_ref[...] + upd

    rows = lambda w, c=0: pl.BlockSpec((tm, w), lambda i: (i, c))
    halo = pl.BlockSpec((8, D_FF), lambda i: (jnp.maximum(i * (tm // 8) - 1, 0), 0))
    return pl.pallas_call(
        body, name="b_ffn_gate", grid=(L // tm,),
        in_specs=[rows(D_MODEL), _full((D_FF, D_MODEL)), rows(D_FF), halo, rows(D_FF, 1),
                  _full((1, D_FF)), _full((1, D_FF)), _full((1, D_FF)), _full((1, D_FF))],
        out_specs=[rows(D_FF), rows(D_FF), _full((8, D_FF))],
        out_shape=[jax.ShapeDtypeStruct((L, D_FF), BF16), jax.ShapeDtypeStruct((L, D_FF), BF16),
                   jax.ShapeDtypeStruct((8, D_FF), F32)],
        compiler_params=_params(("arbitrary",)),
    )(dh2b, w_down, up, up, up, cw[0], cw[1], cw[2], conv_b)


def _ffn_bwd_up(dacc, db, conv_w, w_up, h1, ffn_g, dh2, w_out):
    L = h1.shape[0]
    tm = _row_tile(L)
    nt = L // tm
    shard = w_up.shape[2]
    cw = [conv_w[j:j + 1] for j in range(3)]

    def body(da_ref, halo_ref, db_ref, cw0, cw1, cw2, wu_ref, h1_ref, g_ref, dh2_ref, wo_ref,
             dup_ref, dh1_ref, dh1b_ref, dmix_ref, dg_ref):
        i = pl.program_id(0)

        @pl.when(i == 0)
        def _():
            dg_ref[...] = jnp.zeros_like(dg_ref)

        sub = _iota((tm, 1), 0)
        d0 = da_ref[...].astype(F32)
        halo = jnp.where(i < nt - 1, halo_ref[...].astype(F32), 0.0)
        d1 = jnp.where(sub == tm - 1, _pick_row(halo, 0), pltpu.roll(d0, tm - 1, 0))
        d2 = jnp.where(sub == tm - 2, _pick_row(halo, 0),
                       jnp.where(sub == tm - 1, _pick_row(halo, 1), pltpu.roll(d0, tm - 2, 0)))
        da = d0 * cw2[...] + d1 * cw1[...] + d2 * cw0[...]
        da = jnp.where(i * tm + sub >= N_PAD, da, 0.0).astype(BF16)
        dup_ref[:, 0:D_FF] = da
        dbv = db_ref[...]
        dup_ref[:, D_FF:2 * D_FF] = dbv
        dn = jnp.zeros((tm, D_MODEL), F32)
        for j in range(N_CHIPS):
            src = da if j < 2 else dbv
            lo = (j % 2) * shard
            dn = dn + _dot_nt(src[:, lo:lo + shard], wu_ref[j])
        h1 = h1_ref[...]
        r = lax.rsqrt(jnp.mean(h1 * h1, axis=-1, keepdims=True) + EPS)
        yn = h1 * r
        dg_ref[...] = dg_ref[...] + jnp.sum(dn * yn, axis=0, keepdims=True)
        dyn = dn * g_ref[...]
        dh1 = dh2_ref[...] + r * (dyn - yn * jnp.mean(dyn * yn, axis=-1, keepdims=True))
        dh1_ref[...] = dh1
        dh1b = dh1.astype(BF16)
        dh1b_ref[...] = dh1b
        dmix_ref[...] = _dot_nt(dh1b, wo_ref[...]).astype(BF16)

    rows = lambda w: pl.BlockSpec((tm, w), lambda i: (i, 0))
    halo = pl.BlockSpec((8, D_FF), lambda i: (jnp.minimum((i + 1) * (tm // 8), L // 8 - 1), 0))
    return pl.pallas_call(
        body, name="b_ffn_up", grid=(nt,),
        in_specs=[rows(D_FF), halo, rows(D_FF), _full((1, D_FF)), _full((1, D_FF)), _full((1, D_FF)),
                  _full((N_CHIPS, D_MODEL, shard)), rows(D_MODEL), _full((1, D_MODEL)), rows(D_MODEL),
                  _full((D_MODEL, D_MODEL))],
        out_specs=[rows(2 * D_FF), rows(D_MODEL), rows(D_MODEL), rows(D_MODEL), _full((1, D_MODEL))],
        out_shape=[jax.ShapeDtypeStruct((L, 2 * D_FF), BF16), jax.ShapeDtypeStruct((L, D_MODEL), F32),
                   jax.ShapeDtypeStruct((L, D_MODEL), BF16), jax.ShapeDtypeStruct((L, D_MODEL), BF16),
                   jax.ShapeDtypeStruct((1, D_MODEL), F32)],
        compiler_params=_params(("arbitrary",)),
    )(dacc, dacc, db, cw[0], cw[1], cw[2], w_up, h1, ffn_g, dh2, w_out)


def _wgrad(a, b, name, tn=None, tk=None):
    L, K = a.shape
    N = b.shape[1]
    tn = N if tn is None else tn
    tk = K if tk is None else tk
    tl = _row_tile(L, (1408, 768, 512, 256, 128))

    def body(a_ref, b_ref, o_ref):
        @pl.when(pl.program_id(2) == 0)
        def _():
            o_ref[...] = jnp.zeros_like(o_ref)

        o_ref[0] = o_ref[0] + _dot_tn(a_ref[...], b_ref[...])

    return pl.pallas_call(
        body, name=name, grid=(N // tn, K // tk, L // tl),
        in_specs=[pl.BlockSpec((tl, tk), lambda n, k, l: (l, k)), pl.BlockSpec((tl, tn), lambda n, k, l: (l, n))],
        out_specs=pl.BlockSpec((1, tk, tn), lambda n, k, l: (n, k, 0)),
        out_shape=jax.ShapeDtypeStruct((N // tn, K, tn), F32),
        compiler_params=_params(("parallel", "parallel", "arbitrary")),
    )(a, b)


def _retention_bwd(dmix, o, proj, cos_t, sin_t, ret_g, states):
    L = proj.shape[0]
    nblk = L // BLK
    G = _block_group(nblk)
    steps = nblk // G
    dmat, wq_t, wk_t, g_blk = _decay_tables()

    def body(dm_ref, o_ref, q_ref, k_ref, v_ref, gate_ref, cos_ref, sin_ref, d_ref, wq_ref, wk_ref, rg_ref, rs_ref,
             dp_ref, drg_ref, gstate):
        @pl.when(pl.program_id(0) == 0)
        def _():
            gstate[...] = jnp.zeros_like(gstate)
            drg_ref[...] = jnp.zeros_like(drg_ref)

        lane = _iota((BLK, BLK), 1)
        sub = _iota((BLK, BLK), 0)
        scale = HEAD_LANES ** -0.5
        for b in reversed(range(G)):
            rows = slice(b * BLK, (b + 1) * BLK)
            rot, rot_t = _rot_fns(cos_ref[rows, :], sin_ref[rows, :])
            for p in range(2):
                qr = rot(q_ref[rows, p * BLK:(p + 1) * BLK].astype(F32))
                kr = rot(k_ref[rows, p * BLK:(p + 1) * BLK].astype(F32)) * scale
                kr_b = kr.astype(BF16)
                qw = (qr * wq_ref[p]).astype(BF16)
                kw = (kr * wk_ref[p]).astype(BF16)
                dqr = jnp.zeros((BLK, BLK), F32)
                dkr = jnp.zeros((BLK, BLK), F32)
                for e in range(2):
                    h = 2 * p + e
                    cols = slice(h * BLK, (h + 1) * BLK)
                    head_lanes = (lane >> 6) == e
                    o = o_ref[rows, cols]
                    rn = lax.rsqrt(jnp.mean(o * o, axis=-1, keepdims=True) + EPS)
                    y = o * rn
                    gate = gate_ref[rows, cols].astype(F32)
                    sg = jax.nn.sigmoid(gate)
                    dm = dm_ref[rows, cols].astype(F32)
                    rgain = rg_ref[:, cols]
                    drg_ref[:, cols] = drg_ref[:, cols] + jnp.sum(dm * y * (gate * sg), axis=0, keepdims=True)
                    dp_ref[rows, 1024 + h * BLK:1024 + (h + 1) * BLK] = (
                        dm * y * rgain * (sg * (1.0 + gate * (1.0 - sg)))).astype(BF16)
                    dy = dm * rgain * (gate * sg)
                    do = (rn * (dy - y * jnp.mean(dy * y, axis=-1, keepdims=True))).astype(BF16)
                    vh = v_ref[rows, cols]
                    qm = jnp.where(head_lanes, qr, 0.0).astype(BF16)
                    dmh = d_ref[h]
                    s = (_dot_nt(qm, kr_b) * dmh).astype(BF16)
                    ds = (_dot_nt(do, vh) * dmh).astype(BF16)
                    st = rs_ref[b, h].astype(BF16)
                    gs = gstate[h]
                    gs_b = gs.astype(BF16)
                    dqr = dqr + jnp.where(head_lanes, _dot(ds, kr_b), 0.0) + _dot_nt(do, st) * wq_ref[p]
                    dkr = dkr + _dot_tn(ds, qm) + _dot_nt(vh, gs_b) * wk_ref[p]
                    dp_ref[rows, 512 + h * BLK:512 + (h + 1) * BLK] = (_dot_tn(s, do) + _dot(kw, gs_b)).astype(BF16)
                    dr = jnp.where((sub >> 6) == e, _dot_tn(qw, do), 0.0)
                    gstate[h] = dr + g_blk[h] * gs
                dp_ref[rows, p * BLK:(p + 1) * BLK] = rot_t(dqr).astype(BF16)
                dp_ref[rows, 256 + p * BLK:256 + (p + 1) * BLK] = (rot_t(dkr) * scale).astype(BF16)

    row = lambda c: (lambda i: (steps - 1 - i, c))
    return pl.pallas_call(
        body, name="b_retention", grid=(steps,),
        in_specs=[pl.BlockSpec((G * BLK, 512), row(0)), pl.BlockSpec((G * BLK, 512), row(0)),
                  pl.BlockSpec((G * BLK, 256), row(0)), pl.BlockSpec((G * BLK, 256), row(1)),
                  pl.BlockSpec((G * BLK, 512), row(1)), pl.BlockSpec((G * BLK, 512), row(2)),
                  pl.BlockSpec((G * BLK, BLK), row(0)), pl.BlockSpec((G * BLK, BLK), row(0)),
                  _full((RET_HEADS, BLK, BLK)), _full((2, BLK, BLK)), _full((2, BLK, BLK)), _full((1, 512)),
                  pl.BlockSpec((G, RET_HEADS, BLK, BLK), lambda i: (steps - 1 - i, 0, 0, 0))],
        out_specs=[pl.BlockSpec((G * BLK, RET_W), row(0)), _full((1, 512))],
        out_shape=[jax.ShapeDtypeStruct((L, RET_W), BF16), jax.ShapeDtypeStruct((1, 512), F32)],
        scratch_shapes=[pltpu.VMEM((RET_HEADS, BLK, BLK), F32)],
        compiler_params=_params(("arbitrary",)),
    )(dmix, o, proj, proj, proj, proj, cos_t, sin_t, dmat, wq_t, wk_t, ret_g, states)


def _fox_delta(dmix, o_f):
    L = o_f.shape[0]
    nblk = L // BLK
    G = _block_group(nblk)

    def body(do_ref, o_ref, d_ref):
        sel = ((_iota((8, 512), 1) >> 6) == _iota((8, 512), 0)).astype(BF16)
        for b in range(G):
            rows = slice(b * BLK, (b + 1) * BLK)
            prod = do_ref[rows, :].astype(F32) * o_ref[rows, :].astype(F32)
            hi = prod.astype(BF16)
            lo = (prod - hi.astype(F32)).astype(BF16)
            d_ref[b] = _dot_nt(sel, hi) + _dot_nt(sel, lo)

    return pl.pallas_call(
        body, name="b_foxdelta", grid=(nblk // G,),
        in_specs=[pl.BlockSpec((G * BLK, 512), lambda i: (i, 1)), pl.BlockSpec((G * BLK, 512), lambda i: (i, 0))],
        out_specs=pl.BlockSpec((G, 8, BLK), lambda i: (i, 0, 0)),
        out_shape=jax.ShapeDtypeStruct((nblk, 8, BLK), F32),
        compiler_params=_params(("parallel",)),
    )(dmix, o_f)


def _fox_bwd(proj, dmix, c, ctb, lse, delta, scatter=()):
    L = proj.shape[0]
    nblk, nu = _fox_units(L)
    scale = HEAD_LANES ** -0.5
    ns = len(scatter)

    def body(qkv_ref, do_ref, c_ref, ct_ref, lse_ref, dl_ref, *rest):
        s_in, (dp_ref, dc_ref, dcq_ref), s_out = rest[:ns], rest[ns:ns + 3], rest[ns + 3:2 * ns + 3]
        ktt, dqt, dk_acc, dv_acc, dcs_acc = rest[2 * ns + 3:2 * ns + 8]
        p = pl.program_id(0)

        @pl.when(p == 0)
        def _():
            dc_ref[...] = jnp.zeros_like(dc_ref)
            dcq_ref[...] = jnp.zeros_like(dcq_ref)
            if ns:
                for cp in _scatter_copies(s_in, s_out, *rest[2 * ns + 8:]):
                    cp.start()

        lane = _iota((BLK, BLK), 1)
        sub8 = _iota((8, BLK), 0)
        masks = _fox_tile_masks()

        def pre(j, carry):
            off = pl.multiple_of(j * BLK, BLK)
            ktt[j] = qkv_ref[pl.ds(off, BLK), BLK:2 * BLK].astype(F32).T.astype(BF16)
            dqt[j] = jnp.zeros((BLK, BLK), F32)
            return carry

        lax.fori_loop(0, nblk, pre, 0)

        def kv_pass(kblk, nk, n_later):
            klen = nk * BLK
            koff = pl.multiple_of(kblk * BLK, BLK)
            kt = qkv_ref[pl.ds(koff, klen), BLK:2 * BLK]
            vtile = qkv_ref[pl.ds(koff, klen), 2 * BLK:3 * BLK]
            ct = c_ref[pl.ds(koff, klen), :]
            klane = _iota((klen, BLK), 1)
            cs = [jnp.broadcast_to(jnp.sum(jnp.where(klane == 2 * p + e, ct, 0.0), axis=1, keepdims=True),
                                   (klen, WIDE * UNIT)) for e in range(2)]
            dk_acc[0:klen] = jnp.zeros((klen, BLK), F32)
            dv_acc[0:klen] = jnp.zeros((klen, BLK), F32)
            for e in range(2):
                dcs_acc[e, 0:klen] = jnp.zeros((klen, BLK), F32)

            def tile(qblk, nq, mask):
                qlen = nq * BLK
                if mask == "valid":
                    mask = _iota((klen, qlen), 0) >= N_PAD
                qoff = pl.multiple_of(qblk * BLK, BLK)
                qs = qkv_ref[pl.ds(qoff, qlen), 0:BLK].astype(F32) * scale
                dot_ = do_ref[pl.ds(qoff, qlen), :]
                qlane = _iota((qlen, BLK), 1)
                stats = [[ref[qblk + a] for a in range(nq)] for ref in (ct_ref, lse_ref, dl_ref)]
                for e in range(2):
                    h = 2 * p + e
                    head = (qlane >> 6) == e
                    ct_row, lse_row, dl_row = [jnp.concatenate([_pick_row(t, h) for t in ts], axis=1) for ts in stats]
                    qm = jnp.where(head, qs, 0.0).astype(BF16)
                    dom = jnp.where(head, dot_, jnp.zeros_like(dot_))
                    t = _dot_nt(kt, qm) - cs[e][:, 0:qlen]
                    if mask is not None:
                        t = jnp.where(mask, t, NEG)
                    pr = jnp.exp(t + (ct_row - lse_row))
                    dv_acc[0:klen] = dv_acc[0:klen] + _dot(pr.astype(BF16), dom)
                    dsv = pr * (_dot_nt(vtile, dom) - dl_row)
                    ds_b = dsv.astype(BF16)
                    dk_acc[0:klen] = dk_acc[0:klen] + _dot(ds_b, qm)
                    rows = slice(e * HEAD_LANES, (e + 1) * HEAD_LANES)
                    dq_t = _dot(ktt[kblk, rows, :], ds_b[0:BLK])
                    for b in range(1, nk):
                        dq_t = dq_t + _dot(ktt[kblk + b, rows, :], ds_b[b * BLK:(b + 1) * BLK])
                    key_side = dsv[:, 0:BLK]
                    for a in range(1, nq):
                        key_side = key_side + dsv[:, a * BLK:(a + 1) * BLK]
                    dcs_acc[e, 0:klen] = dcs_acc[e, 0:klen] + key_side
                    query_side = jnp.sum(dsv, axis=0, keepdims=True)
                    for a in range(nq):
                        cols = slice(a * BLK, (a + 1) * BLK)
                        dqt[qblk + a, rows, :] = dqt[qblk + a, rows, :] + dq_t[:, cols]
                        dcq_ref[qblk + a] = dcq_ref[qblk + a] + jnp.where(sub8 == h, query_side[:, cols], 0.0)

            later_mask = "valid" if nk == 1 else None
            n_later = jnp.asarray(n_later, jnp.int32)
            n_wide = n_later // WIDE

            def later_wide(i, carry):
                tile(kblk + nk + 2 * WIDE * i, 2 * WIDE, later_mask)
                return carry

            tile(kblk, nk, masks["first"] if nk == 1 else masks["diag"])
            lax.fori_loop(0, n_wide, later_wide, 0)
            rest = kblk + nk + 2 * WIDE * n_wide

            @pl.when((n_later & 2) != 0)
            def _():
                tile(rest, 4, later_mask)

            @pl.when((n_later & 1) != 0)
            def _():
                tile(rest + 2 * (n_later & 2), 2, later_mask)
            dp_ref[pl.ds(koff, klen), BLK:2 * BLK] = dk_acc[0:klen].astype(BF16)
            dp_ref[pl.ds(koff, klen), 2 * BLK:3 * BLK] = dv_acc[0:klen].astype(BF16)
            upd = jnp.zeros((klen, BLK), F32)
            for e in range(2):
                upd = upd + jnp.where(klane == 2 * p + e, -jnp.sum(dcs_acc[e, 0:klen], axis=1, keepdims=True), 0.0)
            dc_ref[pl.ds(koff, klen), :] = dc_ref[pl.ds(koff, klen), :] + upd

        kv_pass(0, 1, nu)

        def k_loop(u, carry):
            kv_pass(1 + 2 * u, 2, nu - 1 - u)
            return carry

        lax.fori_loop(0, nu, k_loop, 0)

        def flush(j, carry):
            off = pl.multiple_of(j * BLK, BLK)
            dp_ref[pl.ds(off, BLK), 0:BLK] = (dqt[j].T * scale).astype(BF16)
            return carry

        lax.fori_loop(0, nblk, flush, 0)

        if ns:
            @pl.when(p == FOX_HEADS // 2 - 1)
            def _():
                copies = _scatter_copies(s_in, s_out, *rest[2 * ns + 8:])
                for cp in copies:
                    cp.wait_recv()
                for cp in copies:
                    cp.wait_send()

    stat = _full((nblk, 8, BLK))
    return pl.pallas_call(
        body, name="b_fox", grid=(FOX_HEADS // 2,),
        in_specs=[pl.BlockSpec((L, 384), lambda p: (0, RET_W // 384 + p)), pl.BlockSpec((L, BLK), lambda p: (0, 4 + p)),
                  _full((L, BLK)), stat, stat, stat] + [_ANY] * ns,
        out_specs=[pl.BlockSpec((L, 384), lambda p: (0, p)), _full((L, BLK)), stat] + [_ANY] * ns,
        out_shape=[jax.ShapeDtypeStruct((L, FOX_W), BF16), jax.ShapeDtypeStruct((L, BLK), F32),
                   jax.ShapeDtypeStruct((nblk, 8, BLK), F32)] + _scatter_shapes(scatter),
        scratch_shapes=[pltpu.VMEM((nblk, BLK, BLK), BF16), pltpu.VMEM((nblk, BLK, BLK), F32),
                        pltpu.VMEM((UNIT, BLK), F32), pltpu.VMEM((UNIT, BLK), F32), pltpu.VMEM((2, UNIT, BLK), F32)]
        + _scatter_semaphores(ns),
        compiler_params=_params(("arbitrary",)),
    )(proj, dmix, c, ctb, lse, delta, *scatter)


def _fox_post(dc, dcq, ff, fb):
    L = dc.shape[0]
    nblk = L // BLK
    G = _block_group(nblk)
    steps = nblk // G

    def body(dc_ref, dcq_ref, ff_ref, b_ref, dff_ref, dffb_ref, dfb_ref, carry):
        @pl.when(pl.program_id(0) == 0)
        def _():
            carry[...] = jnp.zeros_like(carry)
            dfb_ref[...] = jnp.zeros_like(dfb_ref)

        tri = (_iota((BLK, BLK), 0) <= _iota((BLK, BLK), 1)).astype(BF16)
        live = _iota((BLK, BLK), 1) < FOX_HEADS
        run, dfb = carry[...], dfb_ref[...]
        for b in reversed(range(G)):
            rows = slice(b * BLK, (b + 1) * BLK)
            d = dc_ref[rows, :] + jnp.concatenate([dcq_ref[b], jnp.zeros((BLK - 8, BLK), F32)], axis=0).T
            hi, mid, lo = _split3(d)
            dlf = _dot(tri, hi) + _dot(tri, mid) + _dot(tri, lo) + run
            run = run + jnp.sum(d, axis=0, keepdims=True)
            z = ff_ref[rows, :] + b_ref[...]
            dff = jnp.where(live, dlf * jax.nn.sigmoid(-z), 0.0)
            dff_ref[rows, :] = dff
            dffb_ref[rows, :] = dff.astype(BF16)
            dfb = dfb + jnp.sum(dff, axis=0, keepdims=True)
        carry[...] = run
        dfb_ref[...] = dfb

    rev = lambda i: (steps - 1 - i, 0)
    return pl.pallas_call(
        body, name="b_foxpost", grid=(steps,),
        in_specs=[pl.BlockSpec((G * BLK, BLK), rev), pl.BlockSpec((G, 8, BLK), lambda i: (steps - 1 - i, 0, 0)),
                  pl.BlockSpec((G * BLK, BLK), rev), _full((1, BLK))],
        out_specs=[pl.BlockSpec((G * BLK, BLK), rev), pl.BlockSpec((G * BLK, BLK), rev), _full((1, BLK))],
        out_shape=[jax.ShapeDtypeStruct((L, BLK), F32), jax.ShapeDtypeStruct((L, BLK), BF16),
                   jax.ShapeDtypeStruct((1, BLK), F32)],
        scratch_shapes=[pltpu.VMEM((1, BLK), F32)],
        compiler_params=_params(("arbitrary",)),
    )(dc, dcq, ff, fb)


def _inproj_bwd(dpr, dpf, dffb, w_main, w_ff, h0, g, dh1):
    L = h0.shape[0]
    tm = _row_tile(L)

    def body(dpr_ref, dpf_ref, dff_ref, wm_ref, wf_ref, h_ref, g_ref, dh1_ref, dh0_ref, dg_ref):
        @pl.when(pl.program_id(0) == 0)
        def _():
            dg_ref[...] = jnp.zeros_like(dg_ref)

        dn = (_dot_nt(dpr_ref[...], wm_ref[:, 0:RET_W]) + _dot_nt(dpf_ref[...], wm_ref[:, RET_W:MAIN_W])
              + _dot_nt(dff_ref[...], wf_ref[...]))
        h = h_ref[...]
        r = lax.rsqrt(jnp.mean(h * h, axis=-1, keepdims=True) + EPS)
        yn = h * r
        dg_ref[...] = dg_ref[...] + jnp.sum(dn * yn, axis=0, keepdims=True)
        dyn = dn * g_ref[...]
        dh0_ref[...] = dh1_ref[...] + r * (dyn - yn * jnp.mean(dyn * yn, axis=-1, keepdims=True))

    rows = lambda w: pl.BlockSpec((tm, w), lambda i: (i, 0))
    return pl.pallas_call(
        body, name="b_inproj", grid=(L // tm,),
        in_specs=[rows(RET_W), rows(FOX_W), rows(BLK), _full((D_MODEL, MAIN_W)), _full((D_MODEL, BLK)),
                  rows(D_MODEL), _full((1, D_MODEL)), rows(D_MODEL)],
        out_specs=[rows(D_MODEL), _full((1, D_MODEL))],
        out_shape=[jax.ShapeDtypeStruct((L, D_MODEL), F32), jax.ShapeDtypeStruct((1, D_MODEL), F32)],
        compiler_params=_params(("arbitrary",)),
    )(dpr, dpf, dffb, w_main, w_ff, h0, g, dh1)


def _local_step(x, target, meta, attn_g, w_main, w_ff, fox_b, ret_g, w_out, ffn_g, w_up, conv_w, conv_b, w_down, final_g,
                late=None, mid=None):
    S = x.shape[0]
    L = S + PREFIX
    h0 = jnp.concatenate([jnp.zeros((N_PAD, D_MODEL), F32), meta, x], axis=0)
    tgt = jnp.concatenate([jnp.zeros((PREFIX, D_MODEL), F32), target], axis=0)
    fb = jnp.pad(fox_b, ((0, 0), (0, BLK - FOX_HEADS)))
    cos_t, sin_t = _rotary_tables(L)

    n1, proj, ff = _rms_inproj(h0, attn_g, w_main, w_ff)
    c, ctb = _fox_prep(ff, fb)
    mix_r, o_ret, states = _retention_fwd(proj, cos_t, sin_t, ret_g)
    if late is None:
        o_f, lse = _fox_fwd(proj, c, ctb)
    else:
        o_f, lse, *gathered = _fox_fwd(proj, c, ctb, gather=late[0])
        w_out, w_up, w_down = late[1](gathered)
    h1, n2, up, g_act = _outproj_up(mix_r, o_f, h0, w_out, ffn_g, w_up, conv_w, conv_b)
    dh2, dh2b, d_final_g, loss = _ffn_down_loss(g_act, w_down, h1, final_g, tgt)

    dacc, db, dconv = _ffn_bwd_gate(dh2b, w_down, up, conv_w, conv_b)
    dup, dh1, dh1b, dmix, d_ffn_g = _ffn_bwd_up(dacc, db, conv_w, w_up, h1, ffn_g, dh2, w_out)
    d_w_down = _wgrad(g_act, dh2b, "wgrad_down", tk=D_FF // 2)[0]
    d_w_up = _wgrad(n2, dup, "wgrad_up", tn=w_up.shape[2])
    d_w_out = jnp.concatenate([_wgrad(mix_r, dh1b, "wgrad_out_r")[0], _wgrad(o_f, dh1b, "wgrad_out_f")[0]], axis=0)

    dpr, d_ret_g = _retention_bwd(dmix, o_ret, proj, cos_t, sin_t, ret_g, states)
    delta = _fox_delta(dmix, o_f)
    scatter = () if mid is None else mid(d_w_out, d_w_up, d_w_down)
    dpf, dc, dcq, *received = _fox_bwd(proj, dmix, c, ctb, lse, delta, scatter=scatter)
    dff, dffb, d_fox_b = _fox_post(dc, dcq, ff, fb)
    dh0, d_attn_g = _inproj_bwd(dpr, dpf, dffb, w_main, w_ff, h0, attn_g, dh1)
    d_w_main = jnp.concatenate([_wgrad(n1, dpr, "wgrad_in_r")[0], _wgrad(n1, dpf, "wgrad_in_f")[0]], axis=1)
    d_w_ff = _wgrad(n1, dffb, "wgrad_in_ff")[0]

    return dict(
        loss=loss[0, 0], dx=dh0[PREFIX:], dmeta=dh0[N_PAD:PREFIX], attn_g=d_attn_g, w_main=d_w_main,
        w_ff=d_w_ff[:, :FOX_HEADS], fox_b=d_fox_b[:, :FOX_HEADS], ret_g=d_ret_g, w_out=d_w_out, ffn_g=d_ffn_g,
        w_up=d_w_up, conv_w=dconv[0:3], conv_b=dconv[3:4], w_down=d_w_down, final_g=d_final_g,
        scatter=scatter, received=received)


_ANY = pl.BlockSpec(memory_space=pl.ANY)


def _place():
    return lax.axis_index("x"), lax.axis_index("y"), lax.axis_index("c")


def _other_chips(x, y):
    return [(1 - x, y), (x, 1 - y), (1 - x, 1 - y)]


def _allgather_semaphores(n):
    if n == 0:
        return []
    return [pltpu.SemaphoreType.DMA((3 * n,)), pltpu.SemaphoreType.DMA((3 * n,)), pltpu.SemaphoreType.DMA((n,))]


def _allgather_copies(ins, outs, send, recv, loc):
    n = len(ins)
    x, y, c = _place()
    mine = 2 * x + y
    peers = _other_chips(x, y)

    def remote(a, k, slot):
        return pltpu.make_async_remote_copy(
            src_ref=ins[a], dst_ref=outs[a].at[slot], send_sem=send.at[3 * a + k], recv_sem=recv.at[3 * a + k],
            device_id=(peers[k][0], peers[k][1], c), device_id_type=MESH)

    local = [pltpu.make_async_copy(ins[a], outs[a].at[mine], loc.at[a]) for a in range(n)]
    sends = [remote(a, k, mine) for a in range(n) for k in range(3)]
    recvs = [remote(a, k, 2 * peers[k][0] + peers[k][1]) for a in range(n) for k in range(3)]
    return local, sends, recvs


def _chip_allgather(arrays):
    n = len(arrays)

    def body(*refs):
        local, sends, recvs = _allgather_copies(refs[:n], refs[n:2 * n], *refs[2 * n:])
        for cp in local + sends:
            cp.start()
        for cp in recvs:
            cp.wait_recv()
        for cp in sends:
            cp.wait_send()
        for cp in local:
            cp.wait()

    return pl.pallas_call(
        body, name="ag_weights", in_specs=[_ANY] * n, out_specs=[_ANY] * n,
        out_shape=[jax.ShapeDtypeStruct((N_CHIPS,) + a.shape, a.dtype) for a in arrays],
        scratch_shapes=_allgather_semaphores(n),
    )(*arrays)


def _sibling_exchange(grads, small):
    n = len(grads)

    def body(*refs):
        ins, small_in = refs[:n], refs[n]
        outs, small_out = refs[n + 1:2 * n + 1], refs[2 * n + 1]
        send, recv, s_send, s_recv, loc = refs[2 * n + 2:]
        x, y, c = _place()
        me = 4 * x + 2 * y + c

        def half_copy(a, which):
            half = ins[a].shape[1] // 2
            return pltpu.make_async_remote_copy(
                src_ref=ins[a].at[pl.ds(0, N_CHIPS), pl.ds(which * half, half)], dst_ref=outs[a],
                send_sem=send.at[a], recv_sem=recv.at[a], device_id=(x, y, 1 - c), device_id_type=MESH)

        def peer_of(r):
            return tuple(1 - v if (r >> b) & 1 else v for v, b in ((x, 2), (y, 1), (c, 0)))

        def small_copy(r, slot):
            return pltpu.make_async_remote_copy(
                src_ref=small_in, dst_ref=small_out.at[slot], send_sem=s_send.at[r - 1], recv_sem=s_recv.at[r - 1],
                device_id=peer_of(r), device_id_type=MESH)

        local = pltpu.make_async_copy(small_in, small_out.at[me], loc.at[0])
        sends = [half_copy(a, 1 - c) for a in range(n)] + [small_copy(r, me) for r in range(1, N_DEV)]
        local.start()
        for cp in sends:
            cp.start()
        for r in range(1, N_DEV):
            px, py, pc = peer_of(r)
            small_copy(r, 4 * px + 2 * py + pc).wait_recv()
        for a in range(n):
            half_copy(a, c).wait_recv()
        for cp in sends:
            cp.wait_send()
        local.wait()

    rows = small.shape[0]
    return pl.pallas_call(
        body, name="rs_sibling", in_specs=[_ANY] * (n + 1), out_specs=[_ANY] * (n + 1),
        out_shape=[jax.ShapeDtypeStruct((N_CHIPS, g.shape[1] // 2, g.shape[2]), g.dtype) for g in grads]
        + [jax.ShapeDtypeStruct((N_DEV, rows, small.shape[1]), small.dtype)],
        scratch_shapes=[pltpu.SemaphoreType.DMA((n,)), pltpu.SemaphoreType.DMA((n,)),
                        pltpu.SemaphoreType.DMA((N_DEV - 1,)), pltpu.SemaphoreType.DMA((N_DEV - 1,)),
                        pltpu.SemaphoreType.DMA((1,))],
    )(*grads, small)


def _sibling_halves(grads):
    n = len(grads)

    def body(*refs):
        ins, outs = refs[:n], refs[n:2 * n]
        send, recv = refs[2 * n:]
        x, y, c = _place()

        def half_copy(a, which):
            half = ins[a].shape[1] // 2
            return pltpu.make_async_remote_copy(
                src_ref=ins[a].at[pl.ds(0, N_CHIPS), pl.ds(which * half, half)], dst_ref=outs[a],
                send_sem=send.at[a], recv_sem=recv.at[a], device_id=(x, y, 1 - c), device_id_type=MESH)

        sends = [half_copy(a, 1 - c) for a in range(n)]
        for cp in sends:
            cp.start()
        for a in range(n):
            half_copy(a, c).wait_recv()
        for cp in sends:
            cp.wait_send()

    return pl.pallas_call(
        body, name="rs_sibling_early", in_specs=[_ANY] * n, out_specs=[_ANY] * n,
        out_shape=[jax.ShapeDtypeStruct((N_CHIPS, g.shape[1] // 2, g.shape[2]), g.dtype) for g in grads],
        scratch_shapes=[pltpu.SemaphoreType.DMA((n,)), pltpu.SemaphoreType.DMA((n,))],
    )(*grads)


def _chip_reduce_scatter(parts):
    n = len(parts)

    def body(*refs):
        copies = _scatter_copies(refs[:n], refs[n:2 * n], *refs[2 * n:])
        for cp in copies:
            cp.start()
        for cp in copies:
            cp.wait_recv()
        for cp in copies:
            cp.wait_send()

    return pl.pallas_call(
        body, name="rs_chips", in_specs=[_ANY] * n, out_specs=[_ANY] * n,
        out_shape=_scatter_shapes(parts), scratch_shapes=_scatter_semaphores(n),
    )(*parts)


def _scatter_shapes(parts):
    return [jax.ShapeDtypeStruct((3,) + p.shape[1:], p.dtype) for p in parts]


def _scatter_semaphores(n):
    return [pltpu.SemaphoreType.DMA((3 * n,)), pltpu.SemaphoreType.DMA((3 * n,))] if n else []


def _scatter_copies(ins, outs, send, recv):
    x, y, c = _place()
    peers = _other_chips(x, y)
    return [pltpu.make_async_remote_copy(
        src_ref=ins[a].at[2 * peers[k][0] + peers[k][1]], dst_ref=outs[a].at[k], send_sem=send.at[3 * a + k],
        recv_sem=recv.at[3 * a + k], device_id=(peers[k][0], peers[k][1], c), device_id_type=MESH)
        for a in range(len(ins)) for k in range(3)]


def _sibling_allgather(bufs):
    n = len(bufs)

    def body(*refs):
        outs = refs[n:2 * n]
        send, recv = refs[2 * n:]
        x, y, c = _place()

        def remote(a, which):
            return pltpu.make_async_remote_copy(
                src_ref=outs[a].at[which], dst_ref=outs[a].at[which], send_sem=send.at[a], recv_sem=recv.at[a],
                device_id=(x, y, 1 - c), device_id_type=MESH)

        sends = [remote(a, c) for a in range(n)]
        for cp in sends:
            cp.start()
        for a in range(n):
            remote(a, 1 - c).wait_recv()
        for cp in sends:
            cp.wait_send()

    outs = pl.pallas_call(
        body, name="ag_sibling", in_specs=[_ANY] * n, out_specs=[_ANY] * n,
        out_shape=[jax.ShapeDtypeStruct(b.shape, b.dtype) for b in bufs],
        input_output_aliases={a: a for a in range(n)},
        scratch_shapes=[pltpu.SemaphoreType.DMA((n,)), pltpu.SemaphoreType.DMA((n,))],
    )(*bufs)
    return [o.reshape(2 * o.shape[1], o.shape[2]) for o in outs]


def _pair_add(full, recv, core, name):
    _, R, C = full.shape
    half = R // 2

    def body(core_ref, a_ref, b_ref, o_ref):
        o_ref[...] = (a_ref[...] + b_ref[...]).astype(BF16)

    return pl.pallas_call(
        body, name=name,
        grid_spec=pltpu.PrefetchScalarGridSpec(
            num_scalar_prefetch=1, grid=(N_CHIPS,),
            in_specs=[pl.BlockSpec((1, half, C), lambda j, core_ref: (j, core_ref[0], 0)),
                      pl.BlockSpec((1, half, C), lambda j, core_ref: (j, 0, 0))],
            out_specs=pl.BlockSpec((1, half, C), lambda j, core_ref: (j, 0, 0))),
        out_shape=jax.ShapeDtypeStruct((N_CHIPS, half, C), BF16),
        compiler_params=_params(("parallel",)),
    )(core, full, recv)


def _sum_slots(q, name, tiles=2):
    n, R, C = q.shape
    tr = R // tiles

    def body(q_ref, o_ref):
        acc = q_ref[0].astype(F32)
        for j in range(1, n):
            acc = acc + q_ref[j].astype(F32)
        o_ref[...] = acc

    return pl.pallas_call(
        body, name=name, grid=(tiles,),
        in_specs=[pl.BlockSpec((n, tr, C), lambda i: (0, i, 0))],
        out_specs=pl.BlockSpec((tr, C), lambda i: (i, 0)),
        out_shape=jax.ShapeDtypeStruct((R, C), F32),
        compiler_params=_params(("parallel",)),
    )(q)


def _sum_partials(own_all, recv, place, name, tiles=2):
    _, R, C = own_all.shape
    tr = R // tiles

    def body(place_ref, own_ref, r_ref, o_ref):
        acc = own_ref[0].astype(F32)
        for k in range(3):
            acc = acc + r_ref[k].astype(F32)
        o_ref[0] = acc

    return pl.pallas_call(
        body, name=name,
        grid_spec=pltpu.PrefetchScalarGridSpec(
            num_scalar_prefetch=1, grid=(tiles,),
            in_specs=[pl.BlockSpec((1, tr, C), lambda i, place_ref: (place_ref[0], i, 0)),
                      pl.BlockSpec((3, tr, C), lambda i, place_ref: (0, i, 0))],
            out_specs=pl.BlockSpec((1, tr, C), lambda i, place_ref: (place_ref[1], i, 0))),
        out_shape=jax.ShapeDtypeStruct((2, R, C), F32),
        compiler_params=_params(("parallel",)),
    )(place, own_all, recv)


def _adamw(w, g, m, v, name, tiles=4):
    R, C = w.shape
    tr = R // tiles

    def body(w_ref, g_ref, m_ref, v_ref, go_ref, d_ref, m2_ref, v2_ref):
        g_ = g_ref[...]
        go_ref[...] = g_
        m2 = ADAM_B1 * m_ref[...] + (1.0 - ADAM_B1) * g_
        v2 = ADAM_B2 * v_ref[...] + (1.0 - ADAM_B2) * (g_ * g_)
        m_hat = m2 / (1.0 - ADAM_B1 ** ADAM_STEP)
        v_hat = v2 / (1.0 - ADAM_B2 ** ADAM_STEP)
        d_ref[...] = -ADAM_LR * (m_hat / (jnp.sqrt(v_hat) + ADAM_EPS) + ADAM_WD * w_ref[...])
        m2_ref[...] = m2
        v2_ref[...] = v2

    spec = pl.BlockSpec((tr, C), lambda i: (i, 0))
    return pl.pallas_call(
        body, name=name, grid=(tiles,), in_specs=[spec] * 4, out_specs=[spec] * 4,
        out_shape=[jax.ShapeDtypeStruct((R, C), F32)] * 4,
        compiler_params=_params(("parallel",)),
    )(w, g, m, v)


def _pack_rows(pieces, rows):
    flat = jnp.concatenate([jnp.pad(p.reshape(-1).astype(F32), (0, (-p.size) % D_MODEL)) for p in pieces])
    return jnp.pad(flat, (0, rows * D_MODEL - flat.size)).reshape(rows, D_MODEL)


def _unpack_rows(pack, shapes):
    flat = pack.reshape(-1)
    out, off = [], 0
    for shp in shapes:
        size = int(np.prod(shp))
        out.append(flat[off:off + size].reshape(shp))
        off += size + (-size) % D_MODEL
    return out


def _kernel_order(w):
    parts = [w[:, 0:RET_W]]
    for p in range(FOX_HEADS // 2):
        parts += [w[:, RET_W + part * 512 + p * BLK:RET_W + part * 512 + (p + 1) * BLK] for part in range(3)]
    return jnp.concatenate(parts, axis=1)


def _reference_order(g_main, g_ff):
    parts = [g_main[:, 0:RET_W]]
    for part in range(3):
        parts += [g_main[:, RET_W + 384 * p + part * BLK:RET_W + 384 * p + (part + 1) * BLK] for p in range(FOX_HEADS // 2)]
    return jnp.concatenate(parts + [g_ff], axis=1)


def kernel(x, meta_tokens, attn_norm_g, w_in, fox_forget_b, ret_norm_g, w_out, ffn_norm_g, w_up, conv_w, conv_b, w_down, final_norm_g, loss_target, m_meta_tokens, m_attn_norm_g, m_w_in, m_fox_forget_b, m_ret_norm_g, m_w_out, m_ffn_norm_g, m_w_up, m_conv_w, m_conv_b, m_w_down, m_final_norm_g, v_meta_tokens, v_attn_norm_g, v_w_in, v_fox_forget_b, v_ret_norm_g, v_w_out, v_ffn_norm_g, v_w_up, v_conv_w, v_conv_b, v_w_down, v_final_norm_g):
    chip = 2 * lax.axis_index("x") + lax.axis_index("y")
    core = lax.axis_index("c")
    meta_w, conv_sw = meta_tokens.shape[1], conv_w.shape[2]

    small_w = _pack_rows([meta_tokens, conv_w[0]], 8)
    g_in, g_small = _chip_allgather([w_in[0].astype(BF16), small_w])
    w_in_full = g_in.transpose(1, 0, 2).reshape(D_MODEL, IN_WIDTH)
    w_main = _kernel_order(w_in_full)
    w_ff = jnp.pad(w_in_full[:, MAIN_W:], ((0, 0), (0, BLK - FOX_HEADS)))
    small_parts = [_unpack_rows(g_small[j], [meta_tokens.shape, conv_w.shape[1:]]) for j in range(N_CHIPS)]
    meta_full = jnp.concatenate([sp[0] for sp in small_parts], axis=1)
    conv_w_full = jnp.concatenate([sp[1] for sp in small_parts], axis=1)

    core_idx = core.reshape(1).astype(jnp.int32)
    place = jnp.stack([chip, core]).astype(jnp.int32)

    def assemble(gathered):
        g_out, g_up, g_down = gathered
        return g_out.reshape(D_MODEL, D_MODEL), g_up, g_down.reshape(D_FF, D_MODEL)

    def early_reduce(d_w_out, d_w_up, d_w_down):
        early = [d_w_out.reshape(N_CHIPS, -1, D_MODEL), d_w_up, d_w_down.reshape(N_CHIPS, -1, D_MODEL)]
        from_sib = _sibling_halves(early)
        return [_pair_add(g, r, core_idx, "pair_add_" + nm) for g, r, nm in zip(early, from_sib, ("out", "up", "down"))]

    out = _local_step(x[0], loss_target[0], meta_full, attn_norm_g, w_main, w_ff, fox_forget_b, ret_norm_g,
                      None, ffn_norm_g, None, conv_w_full, conv_b, None, final_norm_g[None],
                      late=([w_out[0].astype(BF16), w_up[0].astype(BF16), w_down[0].astype(BF16)], assemble),
                      mid=early_reduce)

    g_in_full = _reference_order(out["w_main"], out["w_ff"]).reshape(D_MODEL, N_CHIPS, -1).transpose(1, 0, 2)
    small_shapes = [(1, D_MODEL), (1, D_MODEL), (1, D_MODEL), (1, 512 + FOX_HEADS + 1), (1, D_FF), (N_META, D_MODEL), (3, D_FF)]
    small = _pack_rows([out["attn_g"], out["ffn_g"], out["final_g"],
                        jnp.concatenate([out["ret_g"], out["fox_b"], out["loss"].reshape(1, 1)], axis=1),
                        out["conv_b"], out["dmeta"], out["conv_w"]], 32)
    from_sibling_in, small_all = _sibling_exchange([g_in_full], small)
    sum_in = _pair_add(g_in_full, from_sibling_in, core_idx, "pair_add_in")
    (from_chips_in,) = _chip_reduce_scatter([sum_in])
    chip_sums = [sum_in] + list(out["scatter"])
    from_chips = [from_chips_in] + list(out["received"])
    names = ("in", "out", "up", "down")
    totals = [_sum_partials(s, q, place, "sum_chips_" + nm) for s, q, nm in zip(chip_sums, from_chips, names)]
    grad_in, grad_out, grad_up, grad_down = _sibling_allgather(totals)
    s_attn, s_ffn, s_final, s_misc, s_conv_b, s_meta, s_conv_w = _unpack_rows(
        _sum_slots(small_all, "sum_small", tiles=1), small_shapes)
    loss = s_misc[0, 512 + FOX_HEADS]
    small_grads = [lax.dynamic_slice_in_dim(s_meta, chip * meta_w, meta_w, axis=1), s_attn, s_misc[:, 512:512 + FOX_HEADS],
                   s_misc[:, :512], s_ffn, lax.dynamic_slice_in_dim(s_conv_w, chip * conv_sw, conv_sw, axis=1)[None],
                   s_conv_b, s_final[0]]

    big_w = [(w_in, m_w_in, v_w_in, grad_in, "adamw_in"), (w_out, m_w_out, v_w_out, grad_out, "adamw_out"),
             (w_up, m_w_up, v_w_up, grad_up, "adamw_up"), (w_down, m_w_down, v_w_down, grad_down, "adamw_down")]
    big_res = [[r[None] for r in _adamw(w[0], g, m[0], v[0], nm)] for w, m, v, g, nm in big_w]
    small_w_list = [meta_tokens, attn_norm_g, fox_forget_b, ret_norm_g, ffn_norm_g, conv_w, conv_b, final_norm_g]
    small_m = [m_meta_tokens, m_attn_norm_g, m_fox_forget_b, m_ret_norm_g, m_ffn_norm_g, m_conv_w, m_conv_b, m_final_norm_g]
    small_v = [v_meta_tokens, v_attn_norm_g, v_fox_forget_b, v_ret_norm_g, v_ffn_norm_g, v_conv_w, v_conv_b, v_final_norm_g]
    shapes = [a.shape for a in small_w_list]
    packs = [_pack_rows(lst, 16) for lst in (small_w_list, small_grads, small_m, small_v)]
    small_res = [_unpack_rows(r, shapes) for r in _adamw(*packs, "adamw_small", tiles=1)[1:]]
    small_grads = [g.reshape(s) for g, s in zip(small_grads, shapes)]

    def ordered(kind):
        sm = small_grads if kind == 0 else small_res[kind - 1]
        bg = [r[kind] for r in big_res]
        return [sm[0], sm[1], bg[0], sm[2], sm[3], bg[1], sm[4], bg[2], sm[5], sm[6], bg[3], sm[7]]

    return (loss, out["dx"][None], *ordered(0), *ordered(1), *ordered(2), *ordered(3))
```

```python
import functools

import numpy as np
import jax
import jax.numpy as jnp
from jax import lax
from jax.experimental import pallas as pl
from jax.experimental.pallas import tpu as pltpu

F32 = jnp.float32
BF16 = jnp.bfloat16

D_MODEL = 1024
N_META = 16
BLK = 128
UNIT = 2 * BLK
FOX_PAIRS = 2
WIDE = 4
CHUNK = 64
N_PAD = BLK - N_META
PREFIX = BLK
RET_HEADS = 4
FOX_HEADS = 8
HEAD_LANES = 64
D_FF = 2816
ROPE_BASE = 10000.0
EPS = 1e-6
NEG = -1e30
RET_W = 1536
FOX_W = 1536
MAIN_W = RET_W + FOX_W
IN_WIDTH = MAIN_W + FOX_HEADS
N_CHIPS = 4
N_DEV = 8

ADAM_LR = 0.001
ADAM_B1 = 0.9
ADAM_B2 = 0.999
ADAM_EPS = 1e-08
ADAM_WD = 0.01
ADAM_STEP = 10

MESH = pl.DeviceIdType.MESH
VMEM_LIMIT_MB = 56

_NT = (((1,), (1,)), ((), ()))
_TN = (((0,), (0,)), ((), ()))


def _dot(a, b):
    return jnp.dot(a, b, preferred_element_type=F32)


def _dot_nt(a, b):
    return lax.dot_general(a, b, _NT, preferred_element_type=F32)


def _dot_tn(a, b):
    return lax.dot_general(a, b, _TN, preferred_element_type=F32)


def _params(dims=None, vmem_mb=VMEM_LIMIT_MB):
    kw = dict(vmem_limit_bytes=vmem_mb << 20)
    if dims is not None:
        kw["dimension_semantics"] = dims
    return pltpu.CompilerParams(**kw)


def _row_tile(n, prefs=(384, 256, 128)):
    for t in prefs:
        if n % t == 0:
            return t
    raise ValueError(f"no row tile for {n}")


def _iota(shape, dim):
    return lax.broadcasted_iota(jnp.int32, shape, dim)


def _pick_row(tile, row):
    sub = _iota(tile.shape, 0)
    return jnp.sum(jnp.where(sub == row, tile, 0.0), axis=0, keepdims=True)


def _split3(x):
    hi = x.astype(BF16)
    r1 = x - hi.astype(F32)
    mid = r1.astype(BF16)
    lo = (r1 - mid.astype(F32)).astype(BF16)
    return hi, mid, lo


def _full(shape):
    nd = len(shape)
    return pl.BlockSpec(shape, lambda *_: (0,) * nd)


def _in_perm():
    cols = list(range(RET_W))
    for p in range(FOX_HEADS // 2):
        for part in range(3):
            start = RET_W + part * 512 + p * BLK
            cols += list(range(start, start + BLK))
    return np.asarray(cols, np.int32)


def _rotary_tables(L):
    half = HEAD_LANES // 2
    inv = 1.0 / (ROPE_BASE ** (jnp.arange(half, dtype=F32) / half))
    ang = jnp.arange(L).astype(F32)[:, None] * inv[None, :]
    cos, sin = jnp.cos(ang), jnp.sin(ang)
    cos_t = jnp.tile(cos, (1, 4))
    sin_t = jnp.tile(jnp.concatenate([-sin, sin], axis=1), (1, 2))
    return cos_t, sin_t


def _decay_tables():
    gam = 1.0 - 2.0 ** (-5.0 - np.arange(RET_HEADS, dtype=np.float64))
    n = np.arange(BLK)
    same_or_past = (n[:, None] // CHUNK) >= (n[None, :] // CHUNK)
    dist = np.abs(n[:, None] - n[None, :])
    dmat = np.stack([np.where(same_or_past, g ** dist, 0.0) for g in gam]).astype(np.float32)
    lane_head = np.arange(BLK) // HEAD_LANES
    wq = np.stack([gam[2 * p + lane_head][None, :] ** (n[:, None] + 1.0) for p in range(2)]).astype(np.float32)
    wk = np.stack([gam[2 * p + lane_head][None, :] ** (BLK - 1.0 - n[:, None]) for p in range(2)]).astype(np.float32)
    g_blk = tuple(float(g ** BLK) for g in gam)
    return jnp.asarray(dmat), jnp.asarray(wq), jnp.asarray(wk), g_blk


def _rms_inproj(h0, g, w_main, w_ff):
    L = h0.shape[0]
    tm = _row_tile(L)

    def body(h_ref, g_ref, wm_ref, wf_ref, n_ref, p_ref, ff_ref):
        h = h_ref[...]
        r = lax.rsqrt(jnp.mean(h * h, axis=-1, keepdims=True) + EPS)
        n = (h * r * g_ref[...]).astype(BF16)
        n_ref[...] = n
        p_ref[...] = _dot(n, wm_ref[...]).astype(BF16)
        ff_ref[...] = _dot(n, wf_ref[...])

    return pl.pallas_call(
        body, name="f_inproj", grid=(L // tm,),
        in_specs=[pl.BlockSpec((tm, D_MODEL), lambda i: (i, 0)), _full((1, D_MODEL)),
                  _full((D_MODEL, MAIN_W)), _full((D_MODEL, BLK))],
        out_specs=[pl.BlockSpec((tm, D_MODEL), lambda i: (i, 0)), pl.BlockSpec((tm, MAIN_W), lambda i: (i, 0)),
                   pl.BlockSpec((tm, BLK), lambda i: (i, 0))],
        out_shape=[jax.ShapeDtypeStruct((L, D_MODEL), BF16), jax.ShapeDtypeStruct((L, MAIN_W), BF16),
                   jax.ShapeDtypeStruct((L, BLK), F32)],
        compiler_params=_params(("parallel",)),
    )(h0, g, w_main, w_ff)


def _block_group(nblk):
    return 3 if nblk % 3 == 0 else 1


def _fox_prep(ff, fb):
    L = ff.shape[0]
    nblk = L // BLK
    G = _block_group(nblk)

    def body(ff_ref, b_ref, c_ref, ct_ref, carry):
        @pl.when(pl.program_id(0) == 0)
        def _():
            carry[...] = jnp.zeros_like(carry)

        tri = (_iota((BLK, BLK), 0) >= _iota((BLK, BLK), 1)).astype(BF16)
        live = _iota((BLK, BLK), 1) < FOX_HEADS
        run = carry[...]
        for b in range(G):
            z = ff_ref[b * BLK:(b + 1) * BLK, :] + b_ref[...]
            lf = jnp.where(live, jnp.minimum(z, 0.0) - jnp.log1p(jnp.exp(-jnp.abs(z))), 0.0)
            hi, mid, lo = _split3(lf)
            cs = _dot(tri, hi) + _dot(tri, mid) + _dot(tri, lo) + run
            c_ref[b * BLK:(b + 1) * BLK, :] = cs
            ct_ref[b] = cs.T[0:8, :]
            run = run + jnp.sum(lf, axis=0, keepdims=True)
        carry[...] = run

    return pl.pallas_call(
        body, name="f_foxprep", grid=(nblk // G,),
        in_specs=[pl.BlockSpec((G * BLK, BLK), lambda i: (i, 0)), _full((1, BLK))],
        out_specs=[pl.BlockSpec((G * BLK, BLK), lambda i: (i, 0)), pl.BlockSpec((G, 8, BLK), lambda i: (i, 0, 0))],
        out_shape=[jax.ShapeDtypeStruct((L, BLK), F32), jax.ShapeDtypeStruct((nblk, 8, BLK), F32)],
        scratch_shapes=[pltpu.VMEM((1, BLK), F32)],
        compiler_params=_params(("arbitrary",)),
    )(ff, fb)


def _rot_fns(cos, sin):
    lane = _iota((BLK, BLK), 1)
    first = (lane & (HEAD_LANES - 1)) < HEAD_LANES // 2

    def swap(x):
        return jnp.where(first, pltpu.roll(x, BLK - 32, 1), pltpu.roll(x, 32, 1))

    def rot(x):
        return x * cos + swap(x) * sin

    def rot_t(dy):
        return dy * cos + swap(dy * sin)

    return rot, rot_t


def _retention_fwd(proj, cos_t, sin_t, ret_g):
    L = proj.shape[0]
    nblk = L // BLK
    G = _block_group(nblk)
    dmat, wq_t, wk_t, g_blk = _decay_tables()

    def body(q_ref, k_ref, v_ref, gate_ref, cos_ref, sin_ref, d_ref, wq_ref, wk_ref, rg_ref,
             mix_ref, o_ref, rs_ref, state):
        @pl.when(pl.program_id(0) == 0)
        def _():
            state[...] = jnp.zeros_like(state)

        lane = _iota((BLK, BLK), 1)
        sub = _iota((BLK, BLK), 0)
        for b in range(G):
            rows = slice(b * BLK, (b + 1) * BLK)
            rot, _ = _rot_fns(cos_ref[rows, :], sin_ref[rows, :])
            for p in range(2):
                qr = rot(q_ref[rows, p * BLK:(p + 1) * BLK].astype(F32))
                kr = rot(k_ref[rows, p * BLK:(p + 1) * BLK].astype(F32)) * (HEAD_LANES ** -0.5)
                kr_b = kr.astype(BF16)
                qw = (qr * wq_ref[p]).astype(BF16)
                kw = (kr * wk_ref[p]).astype(BF16)
                for e in range(2):
                    h = 2 * p + e
                    cols = slice(h * BLK, (h + 1) * BLK)
                    qm = jnp.where((lane >> 6) == e, qr, 0.0).astype(BF16)
                    s = _dot_nt(qm, kr_b) * d_ref[h]
                    vh = v_ref[rows, cols]
                    st = state[h]
                    rs_ref[b, h] = st
                    o = _dot(s.astype(BF16), vh) + _dot(qw, st.astype(BF16))
                    u = jnp.where((sub >> 6) == e, _dot_tn(kw, vh), 0.0)
                    state[h] = g_blk[h] * st + u
                    rn = lax.rsqrt(jnp.mean(o * o, axis=-1, keepdims=True) + EPS)
                    gate = gate_ref[rows, cols].astype(F32)
                    o_ref[rows, cols] = o
                    mix_ref[rows, cols] = (o * rn * rg_ref[:, cols] * (gate * jax.nn.sigmoid(gate))).astype(BF16)

    row = lambda c: (lambda i: (i, c))
    return pl.pallas_call(
        body, name="f_retention", grid=(nblk // G,),
        in_specs=[pl.BlockSpec((G * BLK, 256), row(0)), pl.BlockSpec((G * BLK, 256), row(1)),
                  pl.BlockSpec((G * BLK, 512), row(1)), pl.BlockSpec((G * BLK, 512), row(2)),
                  pl.BlockSpec((G * BLK, BLK), row(0)), pl.BlockSpec((G * BLK, BLK), row(0)),
                  _full((RET_HEADS, BLK, BLK)), _full((2, BLK, BLK)), _full((2, BLK, BLK)), _full((1, 512))],
        out_specs=[pl.BlockSpec((G * BLK, 512), row(0)), pl.BlockSpec((G * BLK, 512), row(0)),
                   pl.BlockSpec((G, RET_HEADS, BLK, BLK), lambda i: (i, 0, 0, 0))],
        out_shape=[jax.ShapeDtypeStruct((L, 512), BF16), jax.ShapeDtypeStruct((L, 512), F32),
                   jax.ShapeDtypeStruct((nblk, RET_HEADS, BLK, BLK), F32)],
        scratch_shapes=[pltpu.VMEM((RET_HEADS, BLK, BLK), F32)],
        compiler_params=_params(("arbitrary",)),
    )(proj, proj, proj, proj, cos_t, sin_t, dmat, wq_t, wk_t, ret_g)


def _fox_units(L):
    nblk = L // BLK
    assert L % BLK == 0 and nblk % 2 == 1, "sequence must be one 128-row block plus whole 256-row tiles"
    return nblk, (nblk - 1) // 2


def _fox_tile_masks():
    sub, lane = _iota((BLK, BLK), 0), _iota((BLK, BLK), 1)
    valid = _iota((BLK, UNIT), 0) >= N_PAD
    diag = _iota((UNIT, UNIT), 0) <= _iota((UNIT, UNIT), 1)
    r, q = _iota((BLK + UNIT, UNIT), 0), _iota((BLK + UNIT, UNIT), 1)
    first_and_diag = ((r < BLK) & (r >= N_PAD)) | ((r >= BLK) & (r - BLK <= q))
    return dict(first=(sub <= lane) & (sub >= N_PAD), valid=valid, diag=diag, first_and_diag=first_and_diag)


def _fox_fwd(proj, c, ctb, gather=()):
    L = proj.shape[0]
    nblk, nu = _fox_units(L)
    scale = HEAD_LANES ** -0.5
    ng = len(gather)
    steps = FOX_HEADS // (2 * FOX_PAIRS)

    def body(qkv_ref, c_ref, ct_ref, *rest):
        g_in, (of_ref, lse_ref), g_out = rest[:ng], rest[ng:ng + 2], rest[ng + 2:2 * ng + 2]
        vt, csb = rest[2 * ng + 2:2 * ng + 4]
        p = pl.program_id(0)
        heads = [(pp, e, 2 * FOX_PAIRS * p + 2 * pp + e) for pp in range(FOX_PAIRS) for e in range(2)]

        @pl.when(p == 0)
        def _():
            lse_ref[...] = jnp.zeros_like(lse_ref)
            if ng:
                local, sends, _ = _allgather_copies(g_in, g_out, *rest[2 * ng + 4:])
                for cp in local + sends:
                    cp.start()

        lane = _iota((BLK, BLK), 1)
        sub8 = _iota((8, BLK), 0)
        masks = _fox_tile_masks()

        def pre(j, carry):
            off = pl.multiple_of(j * BLK, BLK)
            ct = c_ref[pl.ds(off, BLK), :]
            for pp in range(FOX_PAIRS):
                vt[pp, j] = qkv_ref[pl.ds(off, BLK), pp * 384 + 2 * BLK:pp * 384 + 3 * BLK].astype(F32).T.astype(BF16)
            for hh, (_, _, h) in enumerate(heads):
                col = jnp.sum(jnp.where(lane == h, ct, 0.0), axis=1, keepdims=True)
                csb[hh, j] = jnp.broadcast_to(col, (BLK, BLK))
            return carry

        lax.fori_loop(0, nblk, pre, 0)

        def attend(qblk, nq, n_whole):
            qlen = nq * BLK
            qoff = pl.multiple_of(qblk * BLK, BLK)
            qlane = _iota((qlen, BLK), 1)
            qs = [qkv_ref[pl.ds(qoff, qlen), pp * 384:pp * 384 + BLK].astype(F32) * scale for pp in range(FOX_PAIRS)]
            qm = [jnp.where((qlane >> 6) == e, qs[pp], 0.0).astype(BF16) for pp, e, _ in heads]
            ct_row = [jnp.concatenate([_pick_row(ct_ref[qblk + a], h) for a in range(nq)], axis=1) for _, _, h in heads]

            def step(segs, mask, st):
                blocks = [kblk + b for kblk, nk in segs for b in range(nk)]
                kts = []
                for pp in range(FOX_PAIRS):
                    kt = [qkv_ref[pl.ds(pl.multiple_of(kblk * BLK, BLK), nk * BLK), pp * 384 + BLK:pp * 384 + 2 * BLK]
                          for kblk, nk in segs]
                    kts.append(kt[0] if len(kt) == 1 else jnp.concatenate(kt, axis=0))
                out = []
                for hh, (pp, e, _) in enumerate(heads):
                    m, l, acc = st[3 * hh:3 * hh + 3]
                    s = _dot_nt(kts[pp], qm[hh])
                    t = jnp.concatenate([s[b * BLK:(b + 1) * BLK] - jnp.concatenate([csb[hh, blk]] * nq, axis=1)
                                         for b, blk in enumerate(blocks)], axis=0)
                    if mask is not None:
                        t = jnp.where(mask, t, NEG)
                    m_new = jnp.maximum(m, jnp.max(t, axis=0, keepdims=True) + ct_row[hh])
                    alpha = jnp.exp(m - m_new)
                    pr = jnp.exp(t - (m_new - ct_row[hh]))
                    l = alpha * l + jnp.sum(pr, axis=0, keepdims=True)
                    pr_b = pr.astype(BF16)
                    pv = None
                    for b, blk in enumerate(blocks):
                        part = _dot(vt[pp, blk, e * HEAD_LANES:(e + 1) * HEAD_LANES, :], pr_b[b * BLK:(b + 1) * BLK])
                        pv = part if pv is None else pv + part
                    out += [m_new, l, alpha * acc + pv]
                return tuple(out)

            st = (jnp.full((1, qlen), NEG, F32), jnp.zeros((1, qlen), F32),
                  jnp.zeros((HEAD_LANES, qlen), F32)) * len(heads)
            if nq == 1:
                st = step([(0, 1)], masks["first"], st)
            else:
                st = step([(0, 1), (qblk, 2)], masks["first_and_diag"], st)
                n_wide = n_whole // WIDE
                st = lax.fori_loop(0, n_wide, lambda j, s_: step([(1 + 2 * WIDE * j, 2 * WIDE)], None, s_), st)
                rest = 1 + 2 * WIDE * n_wide
                st = lax.cond((n_whole & 2) != 0, lambda s_: step([(rest, 4)], None, s_), lambda s_: s_, st)
                st = lax.cond((n_whole & 1) != 0, lambda s_: step([(rest + 2 * (n_whole & 2), 2)], None, s_),
                              lambda s_: s_, st)
            for pp in range(FOX_PAIRS):
                lo, hi = st[6 * pp:6 * pp + 3], st[6 * pp + 3:6 * pp + 6]
                o_t = jnp.concatenate([lo[2] * (1.0 / lo[1]), hi[2] * (1.0 / hi[1])], axis=0)
                of_ref[pl.ds(qoff, qlen), pp * BLK:(pp + 1) * BLK] = o_t.T.astype(BF16)
            lse = [st[3 * hh] + jnp.log(st[3 * hh + 1]) for hh in range(len(heads))]
            for a in range(nq):
                upd = jnp.zeros((8, BLK), F32)
                for hh, (_, _, h) in enumerate(heads):
                    upd = upd + jnp.where(sub8 == h, lse[hh][:, a * BLK:(a + 1) * BLK], 0.0)
                lse_ref[qblk + a] = lse_ref[qblk + a] + upd

        attend(0, 1, 0)

        def q_loop(u, carry):
            attend(1 + 2 * u, 2, u)
            return carry

        lax.fori_loop(0, nu, q_loop, 0)

        if ng:
            @pl.when(p == steps - 1)
            def _():
                local, sends, recvs = _allgather_copies(g_in, g_out, *rest[2 * ng + 4:])
                for cp in recvs:
                    cp.wait_recv()
                for cp in sends:
                    cp.wait_send()
                for cp in local:
                    cp.wait()

    width = 384 * FOX_PAIRS
    return pl.pallas_call(
        body, name="f_fox", grid=(steps,),
        in_specs=[pl.BlockSpec((L, width), lambda p: (0, RET_W // width + p)), _full((L, BLK)), _full((nblk, 8, BLK))]
        + [_ANY] * ng,
        out_specs=[pl.BlockSpec((L, FOX_PAIRS * BLK), lambda p: (0, p)), _full((nblk, 8, BLK))] + [_ANY] * ng,
        out_shape=[jax.ShapeDtypeStruct((L, 512), BF16), jax.ShapeDtypeStruct((nblk, 8, BLK), F32)]
        + [jax.ShapeDtypeStruct((N_CHIPS,) + a.shape, a.dtype) for a in gather],
        scratch_shapes=[pltpu.VMEM((FOX_PAIRS, nblk, BLK, BLK), BF16), pltpu.VMEM((2 * FOX_PAIRS, nblk, BLK, BLK), F32)]
        + _allgather_semaphores(ng),
        compiler_params=_params(("arbitrary",)),
    )(proj, c, ctb, *gather)


def _outproj_up(mix_r, o_f, h0, w_out, ffn_g, w_up, conv_w, conv_b):
    L = h0.shape[0]
    tm = _row_tile(L)
    shard = w_up.shape[2]
    assert 2 * shard == D_FF
    cw = [conv_w[j:j + 1] for j in range(3)]
    resident = lambda shape: pl.BlockSpec(shape, lambda i: (0,) * len(shape), pipeline_mode=pl.Buffered(1))

    def body(mr_ref, of_ref, h0_ref, wo_ref, g_ref, wu_ref, cw0, cw1, cw2, cb_ref,
             h1_ref, n2_ref, up_ref, act_ref, halo):
        i = pl.program_id(0)

        @pl.when(i == 0)
        def _():
            halo[...] = jnp.zeros_like(halo)

        h1 = h0_ref[...] + _dot(mr_ref[...], wo_ref[0:512, :]) + _dot(of_ref[...], wo_ref[512:1024, :])
        h1_ref[...] = h1
        r = lax.rsqrt(jnp.mean(h1 * h1, axis=-1, keepdims=True) + EPS)
        n2 = (h1 * r * g_ref[...]).astype(BF16)
        n2_ref[...] = n2
        live = i * tm + _iota((tm, 1), 0) >= N_PAD
        for half in range(2):
            cols = slice(half * shard, (half + 1) * shard)
            a_b = _dot(n2, wu_ref[half]).astype(BF16)
            b_b = _dot(n2, wu_ref[2 + half]).astype(BF16)
            up_ref[:, cols] = a_b
            up_ref[:, D_FF + half * shard:D_FF + (half + 1) * shard] = b_b
            a = jnp.where(live, a_b.astype(F32), 0.0)
            _, _, acc = _conv_taps(a, halo[:, cols], [cw0[:, cols], cw1[:, cols], cw2[:, cols]], cb_ref[:, cols])
            act_ref[:, cols] = (acc * jax.nn.sigmoid(acc) * b_b.astype(F32)).astype(BF16)
            halo[:, cols] = a[tm - 8:tm, :]

    rows = lambda w: pl.BlockSpec((tm, w), lambda i: (i, 0))
    return pl.pallas_call(
        body, name="f_outproj_up", grid=(L // tm,),
        in_specs=[rows(512), rows(512), rows(D_MODEL), resident((D_MODEL, D_MODEL)), _full((1, D_MODEL)),
                  resident((N_CHIPS, D_MODEL, shard)), _full((1, D_FF)), _full((1, D_FF)), _full((1, D_FF)),
                  _full((1, D_FF))],
        out_specs=[rows(D_MODEL), rows(D_MODEL), rows(2 * D_FF), rows(D_FF)],
        out_shape=[jax.ShapeDtypeStruct((L, D_MODEL), F32), jax.ShapeDtypeStruct((L, D_MODEL), BF16),
                   jax.ShapeDtypeStruct((L, 2 * D_FF), BF16), jax.ShapeDtypeStruct((L, D_FF), BF16)],
        scratch_shapes=[pltpu.VMEM((8, D_FF), F32)],
        compiler_params=_params(("arbitrary",)),
    )(mix_r, o_f, h0, w_out, ffn_g, w_up, cw[0], cw[1], cw[2], conv_b)


def _conv_taps(a, halo, cw, cb):
    sub = _iota((a.shape[0], 1), 0)
    a1 = jnp.where(sub == 0, _pick_row(halo, 7), pltpu.roll(a, 1, 0))
    a2 = jnp.where(sub == 0, _pick_row(halo, 6), jnp.where(sub == 1, _pick_row(halo, 7), pltpu.roll(a, 2, 0)))
    acc = cb + a2 * cw[0]
    acc = acc + a1 * cw[1]
    acc = acc + a * cw[2]
    return a1, a2, acc


def _conv_acc(a_ref, halo_ref, cw_refs, cb_ref, i, tm):
    sub = _iota((tm, 1), 0)
    a = jnp.where(i * tm + sub >= N_PAD, a_ref[...].astype(F32), 0.0)
    hrow = i * tm - 8 + _iota((8, 1), 0)
    halo = jnp.where((hrow >= N_PAD) & (i > 0), halo_ref[...].astype(F32), 0.0)
    a1, a2, acc = _conv_taps(a, halo, [r[...] for r in cw_refs], cb_ref[...])
    return a, a1, a2, acc


def _ffn_down_loss(g_act, w_down, h1, final_g, target):
    L = h1.shape[0]
    tm = _row_tile(L)

    def body(g_ref, wd_ref, h1_ref, gf_ref, t_ref, dh_ref, dhb_ref, dgf_ref, loss_ref):
        i = pl.program_id(0)

        @pl.when(i == 0)
        def _():
            dgf_ref[...] = jnp.zeros_like(dgf_ref)
            loss_ref[...] = jnp.zeros_like(loss_ref)

        h2 = h1_ref[...] + _dot(g_ref[...], wd_ref[...])
        r = lax.rsqrt(jnp.mean(h2 * h2, axis=-1, keepdims=True) + EPS)
        yn = h2 * r
        gf = gf_ref[...]
        live = i * tm + _iota((tm, 1), 0) >= PREFIX
        err = jnp.where(live, yn * gf - t_ref[...], 0.0)
        loss_ref[...] = loss_ref[...] + 0.5 * jnp.sum(jnp.mean(err * err, axis=-1, keepdims=True))
        dy = err * (1.0 / D_MODEL)
        dgf_ref[...] = dgf_ref[...] + jnp.sum(dy * yn, axis=0, keepdims=True)
        dyn = dy * gf
        dh = r * (dyn - yn * jnp.mean(dyn * yn, axis=-1, keepdims=True))
        dh_ref[...] = dh
        dhb_ref[...] = dh.astype(BF16)

    rows = lambda w: pl.BlockSpec((tm, w), lambda i: (i, 0))
    return pl.pallas_call(
        body, name="f_ffn_down_loss", grid=(L // tm,),
        in_specs=[rows(D_FF), _full((D_FF, D_MODEL)), rows(D_MODEL), _full((1, D_MODEL)), rows(D_MODEL)],
        out_specs=[rows(D_MODEL), rows(D_MODEL), _full((1, D_MODEL)), _full((1, BLK))],
        out_shape=[jax.ShapeDtypeStruct((L, D_MODEL), F32), jax.ShapeDtypeStruct((L, D_MODEL), BF16),
                   jax.ShapeDtypeStruct((1, D_MODEL), F32), jax.ShapeDtypeStruct((1, BLK), F32)],
        compiler_params=_params(("arbitrary",)),
    )(g_act, w_down, h1, final_g, target)


def _ffn_bwd_gate(dh2b, w_down, up, conv_w, conv_b):
    L = dh2b.shape[0]
    tm = _row_tile(L)
    cw = [conv_w[j:j + 1] for j in range(3)]

    def body(dh_ref, wd_ref, a_ref, halo_ref, b_ref, cw0, cw1, cw2, cb_ref, dacc_ref, db_ref, dcw_ref):
        i = pl.program_id(0)

        @pl.when(i == 0)
        def _():
            dcw_ref[...] = jnp.zeros_like(dcw_ref)

        a, a1, a2, acc = _conv_acc(a_ref, halo_ref, (cw0, cw1, cw2), cb_ref, i, tm)
        dg = _dot_nt(dh_ref[...], wd_ref[...])
        sg = jax.nn.sigmoid(acc)
        db_ref[...] = (dg * acc * sg).astype(BF16)
        dacc = dg * b_ref[...].astype(F32) * (sg * (1.0 + acc * (1.0 - sg)))
        dacc_ref[...] = dacc.astype(BF16)
        sub8 = _iota((8, 1), 0)
        rows = [jnp.sum(dacc * t, axis=0, keepdims=True) for t in (a2, a1, a)] + [jnp.sum(dacc, axis=0, keepdims=True)]
        upd = jnp.zeros((8, D_FF), F32)
        for j, rj in enumerate(rows):
            upd = upd + jnp.where(sub8 == j, rj, 0.0)
        dcw_ref[...] = dcw_ref[...] + upd

    rows = lambda w, c=0: pl.BlockSpec((tm, w), lambda i: (i, c))
    halo = pl.BlockSpec((8, D_FF), lambda i: (jnp.maximum(i * (tm // 8) - 1, 0), 0))
    return pl.pallas_call(
        body, name="b_ffn_gate", grid=(L // tm,),
        in_specs=[rows(D_MODEL), _full((D_FF, D_MODEL)), rows(D_FF), halo, rows(D_FF, 1),
                  _full((1, D_FF)), _full((1, D_FF)), _full((1, D_FF)), _full((1, D_FF))],
        out_specs=[rows(D_FF), rows(D_FF), _full((8, D_FF))],
        out_shape=[jax.ShapeDtypeStruct((L, D_FF), BF16), jax.ShapeDtypeStruct((L, D_FF), BF16),
                   jax.ShapeDtypeStruct((8, D_FF), F32)],
        compiler_params=_params(("arbitrary",)),
    )(dh2b, w_down, up, up, up, cw[0], cw[1], cw[2], conv_b)


def _ffn_bwd_up(dacc, db, conv_w, w_up, h1, ffn_g, dh2, w_out):
    L = h1.shape[0]
    tm = _row_tile(L)
    nt = L // tm
    shard = w_up.shape[2]
    cw = [conv_w[j:j + 1] for j in range(3)]

    def body(da_ref, halo_ref, db_ref, cw0, cw1, cw2, wu_ref, h1_ref, g_ref, dh2_ref, wo_ref,
             dup_ref, dh1_ref, dh1b_ref, dmix_ref, dg_ref):
        i = pl.program_id(0)

        @pl.when(i == 0)
        def _():
            dg_ref[...] = jnp.zeros_like(dg_ref)

        sub = _iota((tm, 1), 0)
        d0 = da_ref[...].astype(F32)
        halo = jnp.where(i < nt - 1, halo_ref[...].astype(F32), 0.0)
        d1 = jnp.where(sub == tm - 1, _pick_row(halo, 0), pltpu.roll(d0, tm - 1, 0))
        d2 = jnp.where(sub == tm - 2, _pick_row(halo, 0),
                       jnp.where(sub == tm - 1, _pick_row(halo, 1), pltpu.roll(d0, tm - 2, 0)))
        da = d0 * cw2[...] + d1 * cw1[...] + d2 * cw0[...]
        da = jnp.where(i * tm + sub >= N_PAD, da, 0.0).astype(BF16)
        dup_ref[:, 0:D_FF] = da
        dbv = db_ref[...]
        dup_ref[:, D_FF:2 * D_FF] = dbv
        dn = jnp.zeros((tm, D_MODEL), F32)
        for j in range(N_CHIPS):
            src = da if j < 2 else dbv
            lo = (j % 2) * shard
            dn = dn + _dot_nt(src[:, lo:lo + shard], wu_ref[j])
        h1 = h1_ref[...]
        r = lax.rsqrt(jnp.mean(h1 * h1, axis=-1, keepdims=True) + EPS)
        yn = h1 * r
        dg_ref[...] = dg_ref[...] + jnp.sum(dn * yn, axis=0, keepdims=True)
        dyn = dn * g_ref[...]
        dh1 = dh2_ref[...] + r * (dyn - yn * jnp.mean(dyn * yn, axis=-1, keepdims=True))
        dh1_ref[...] = dh1
        dh1b = dh1.astype(BF16)
        dh1b_ref[...] = dh1b
        dmix_ref[...] = _dot_nt(dh1b, wo_ref[...]).astype(BF16)

    rows = lambda w: pl.BlockSpec((tm, w), lambda i: (i, 0))
    halo = pl.BlockSpec((8, D_FF), lambda i: (jnp.minimum((i + 1) * (tm // 8), L // 8 - 1), 0))
    return pl.pallas_call(
        body, name="b_ffn_up", grid=(nt,),
        in_specs=[rows(D_FF), halo, rows(D_FF), _full((1, D_FF)), _full((1, D_FF)), _full((1, D_FF)),
                  _full((N_CHIPS, D_MODEL, shard)), rows(D_MODEL), _full((1, D_MODEL)), rows(D_MODEL),
                  _full((D_MODEL, D_MODEL))],
        out_specs=[rows(2 * D_FF), rows(D_MODEL), rows(D_MODEL), rows(D_MODEL), _full((1, D_MODEL))],
        out_shape=[jax.ShapeDtypeStruct((L, 2 * D_FF), BF16), jax.ShapeDtypeStruct((L, D_MODEL), F32),
                   jax.ShapeDtypeStruct((L, D_MODEL), BF16), jax.ShapeDtypeStruct((L, D_MODEL), BF16),
                   jax.ShapeDtypeStruct((1, D_MODEL), F32)],
        compiler_params=_params(("arbitrary",)),
    )(dacc, dacc, db, cw[0], cw[1], cw[2], w_up, h1, ffn_g, dh2, w_out)


def _wgrad(a, b, name, tn=None, tk=None):
    L, K = a.shape
    N = b.shape[1]
    tn = N if tn is None else tn
    tk = K if tk is None else tk
    tl = _row_tile(L, (1408, 768, 512, 256, 128))

    def body(a_ref, b_ref, o_ref):
        @pl.when(pl.program_id(2) == 0)
        def _():
            o_ref[...] = jnp.zeros_like(o_ref)

        o_ref[0] = o_ref[0] + _dot_tn(a_ref[...], b_ref[...])

    return pl.pallas_call(
        body, name=name, grid=(N // tn, K // tk, L // tl),
        in_specs=[pl.BlockSpec((tl, tk), lambda n, k, l: (l, k)), pl.BlockSpec((tl, tn), lambda n, k, l: (l, n))],
        out_specs=pl.BlockSpec((1, tk, tn), lambda n, k, l: (n, k, 0)),
        out_shape=jax.ShapeDtypeStruct((N // tn, K, tn), F32),
        compiler_params=_params(("parallel", "parallel", "arbitrary")),
    )(a, b)


def _retention_bwd(dmix, o, proj, cos_t, sin_t, ret_g, states):
    L = proj.shape[0]
    nblk = L // BLK
    G = _block_group(nblk)
    steps = nblk // G
    dmat, wq_t, wk_t, g_blk = _decay_tables()

    def body(dm_ref, o_ref, q_ref, k_ref, v_ref, gate_ref, cos_ref, sin_ref, d_ref, wq_ref, wk_ref, rg_ref, rs_ref,
             dp_ref, drg_ref, gstate):
        @pl.when(pl.program_id(0) == 0)
        def _():
            gstate[...] = jnp.zeros_like(gstate)
            drg_ref[...] = jnp.zeros_like(drg_ref)

        lane = _iota((BLK, BLK), 1)
        sub = _iota((BLK, BLK), 0)
        scale = HEAD_LANES ** -0.5
        for b in reversed(range(G)):
            rows = slice(b * BLK, (b + 1) * BLK)
            rot, rot_t = _rot_fns(cos_ref[rows, :], sin_ref[rows, :])
            for p in range(2):
                qr = rot(q_ref[rows, p * BLK:(p + 1) * BLK].astype(F32))
                kr = rot(k_ref[rows, p * BLK:(p + 1) * BLK].astype(F32)) * scale
                kr_b = kr.astype(BF16)
                qw = (qr * wq_ref[p]).astype(BF16)
                kw = (kr * wk_ref[p]).astype(BF16)
                dqr = jnp.zeros((BLK, BLK), F32)
                dkr = jnp.zeros((BLK, BLK), F32)
                for e in range(2):
                    h = 2 * p + e
                    cols = slice(h * BLK, (h + 1) * BLK)
                    head_lanes = (lane >> 6) == e
                    o = o_ref[rows, cols]
                    rn = lax.rsqrt(jnp.mean(o * o, axis=-1, keepdims=True) + EPS)
                    y = o * rn
                    gate = gate_ref[rows, cols].astype(F32)
                    sg = jax.nn.sigmoid(gate)
                    dm = dm_ref[rows, cols].astype(F32)
                    rgain = rg_ref[:, cols]
                    drg_ref[:, cols] = drg_ref[:, cols] + jnp.sum(dm * y * (gate * sg), axis=0, keepdims=True)
                    dp_ref[rows, 1024 + h * BLK:1024 + (h + 1) * BLK] = (
                        dm * y * rgain * (sg * (1.0 + gate * (1.0 - sg)))).astype(BF16)
                    dy = dm * rgain * (gate * sg)
                    do = (rn * (dy - y * jnp.mean(dy * y, axis=-1, keepdims=True))).astype(BF16)
                    vh = v_ref[rows, cols]
                    qm = jnp.where(head_lanes, qr, 0.0).astype(BF16)
                    dmh = d_ref[h]
                    s = (_dot_nt(qm, kr_b) * dmh).astype(BF16)
                    ds = (_dot_nt(do, vh) * dmh).astype(BF16)
                    st = rs_ref[b, h].astype(BF16)
                    gs = gstate[h]
                    gs_b = gs.astype(BF16)
                    dqr = dqr + jnp.where(head_lanes, _dot(ds, kr_b), 0.0) + _dot_nt(do, st) * wq_ref[p]
                    dkr = dkr + _dot_tn(ds, qm) + _dot_nt(vh, gs_b) * wk_ref[p]
                    dp_ref[rows, 512 + h * BLK:512 + (h + 1) * BLK] = (_dot_tn(s, do) + _dot(kw, gs_b)).astype(BF16)
                    dr = jnp.where((sub >> 6) == e, _dot_tn(qw, do), 0.0)
                    gstate[h] = dr + g_blk[h] * gs
                dp_ref[rows, p * BLK:(p + 1) * BLK] = rot_t(dqr).astype(BF16)
                dp_ref[rows, 256 + p * BLK:256 + (p + 1) * BLK] = (rot_t(dkr) * scale).astype(BF16)

    row = lambda c: (lambda i: (steps - 1 - i, c))
    return pl.pallas_call(
        body, name="b_retention", grid=(steps,),
        in_specs=[pl.BlockSpec((G * BLK, 512), row(0)), pl.BlockSpec((G * BLK, 512), row(0)),
                  pl.BlockSpec((G * BLK, 256), row(0)), pl.BlockSpec((G * BLK, 256), row(1)),
                  pl.BlockSpec((G * BLK, 512), row(1)), pl.BlockSpec((G * BLK, 512), row(2)),
                  pl.BlockSpec((G * BLK, BLK), row(0)), pl.BlockSpec((G * BLK, BLK), row(0)),
                  _full((RET_HEADS, BLK, BLK)), _full((2, BLK, BLK)), _full((2, BLK, BLK)), _full((1, 512)),
                  pl.BlockSpec((G, RET_HEADS, BLK, BLK), lambda i: (steps - 1 - i, 0, 0, 0))],
        out_specs=[pl.BlockSpec((G * BLK, RET_W), row(0)), _full((1, 512))],
        out_shape=[jax.ShapeDtypeStruct((L, RET_W), BF16), jax.ShapeDtypeStruct((1, 512), F32)],
        scratch_shapes=[pltpu.VMEM((RET_HEADS, BLK, BLK), F32)],
        compiler_params=_params(("arbitrary",)),
    )(dmix, o, proj, proj, proj, proj, cos_t, sin_t, dmat, wq_t, wk_t, ret_g, states)


def _fox_delta(dmix, o_f):
    L = o_f.shape[0]
    nblk = L // BLK
    G = _block_group(nblk)

    def body(do_ref, o_ref, d_ref):
        sel = ((_iota((8, 512), 1) >> 6) == _iota((8, 512), 0)).astype(BF16)
        for b in range(G):
            rows = slice(b * BLK, (b + 1) * BLK)
            prod = do_ref[rows, :].astype(F32) * o_ref[rows, :].astype(F32)
            hi = prod.astype(BF16)
            lo = (prod - hi.astype(F32)).astype(BF16)
            d_ref[b] = _dot_nt(sel, hi) + _dot_nt(sel, lo)

    return pl.pallas_call(
        body, name="b_foxdelta", grid=(nblk // G,),
        in_specs=[pl.BlockSpec((G * BLK, 512), lambda i: (i, 1)), pl.BlockSpec((G * BLK, 512), lambda i: (i, 0))],
        out_specs=pl.BlockSpec((G, 8, BLK), lambda i: (i, 0, 0)),
        out_shape=jax.ShapeDtypeStruct((nblk, 8, BLK), F32),
        compiler_params=_params(("parallel",)),
    )(dmix, o_f)


def _fox_bwd(proj, dmix, c, ctb, lse, delta, scatter=()):
    L = proj.shape[0]
    nblk, nu = _fox_units(L)
    scale = HEAD_LANES ** -0.5
    ns = len(scatter)

    steps = FOX_HEADS // (2 * FOX_PAIRS)

    def body(qkv_ref, do_ref, c_ref, ct_ref, lse_ref, dl_ref, *rest):
        s_in, (dp_ref, dc_ref, dcq_ref), s_out = rest[:ns], rest[ns:ns + 3], rest[ns + 3:2 * ns + 3]
        ktt, dqt, dk_acc, dv_acc, dcs_acc = rest[2 * ns + 3:2 * ns + 8]
        p = pl.program_id(0)
        heads = [(pp, e, 2 * FOX_PAIRS * p + 2 * pp + e) for pp in range(FOX_PAIRS) for e in range(2)]

        @pl.when(p == 0)
        def _():
            dc_ref[...] = jnp.zeros_like(dc_ref)
            dcq_ref[...] = jnp.zeros_like(dcq_ref)
            if ns:
                for cp in _scatter_copies(s_in, s_out, *rest[2 * ns + 8:]):
                    cp.start()

        sub8 = _iota((8, BLK), 0)
        masks = _fox_tile_masks()

        def pre(j, carry):
            off = pl.multiple_of(j * BLK, BLK)
            for pp in range(FOX_PAIRS):
                ktt[pp, j] = qkv_ref[pl.ds(off, BLK), pp * 384 + BLK:pp * 384 + 2 * BLK].astype(F32).T.astype(BF16)
                dqt[pp, j] = jnp.zeros((BLK, BLK), F32)
            return carry

        lax.fori_loop(0, nblk, pre, 0)

        def kv_pass(kblk, nk, n_later):
            klen = nk * BLK
            koff = pl.multiple_of(kblk * BLK, BLK)
            kt = [qkv_ref[pl.ds(koff, klen), pp * 384 + BLK:pp * 384 + 2 * BLK] for pp in range(FOX_PAIRS)]
            vtile = [qkv_ref[pl.ds(koff, klen), pp * 384 + 2 * BLK:pp * 384 + 3 * BLK] for pp in range(FOX_PAIRS)]
            ct = c_ref[pl.ds(koff, klen), :]
            klane = _iota((klen, BLK), 1)
            cs = [jnp.broadcast_to(jnp.sum(jnp.where(klane == h, ct, 0.0), axis=1, keepdims=True), (klen, WIDE * UNIT))
                  for _, _, h in heads]
            for pp in range(FOX_PAIRS):
                dk_acc[pp, 0:klen] = jnp.zeros((klen, BLK), F32)
                dv_acc[pp, 0:klen] = jnp.zeros((klen, BLK), F32)
            for hh in range(len(heads)):
                dcs_acc[hh, 0:klen] = jnp.zeros((klen, BLK), F32)

            def tile(qblk, nq, mask):
                qlen = nq * BLK
                if mask == "valid":
                    mask = _iota((klen, qlen), 0) >= N_PAD
                qoff = pl.multiple_of(qblk * BLK, BLK)
                qlane = _iota((qlen, BLK), 1)
                qs = [qkv_ref[pl.ds(qoff, qlen), pp * 384:pp * 384 + BLK].astype(F32) * scale for pp in range(FOX_PAIRS)]
                dot_ = [do_ref[pl.ds(qoff, qlen), pp * BLK:(pp + 1) * BLK] for pp in range(FOX_PAIRS)]
                stats = [[ref[qblk + a] for a in range(nq)] for ref in (ct_ref, lse_ref, dl_ref)]
                dcq = [jnp.zeros((8, BLK), F32) for _ in range(nq)]
                for hh, (pp, e, h) in enumerate(heads):
                    head = (qlane >> 6) == e
                    ct_row, lse_row, dl_row = [jnp.concatenate([_pick_row(t, h) for t in ts], axis=1) for ts in stats]
                    qm = jnp.where(head, qs[pp], 0.0).astype(BF16)
                    dom = jnp.where(head, dot_[pp], jnp.zeros_like(dot_[pp]))
                    t = _dot_nt(kt[pp], qm) - cs[hh][:, 0:qlen]
                    if mask is not None:
                        t = jnp.where(mask, t, NEG)
                    pr = jnp.exp(t + (ct_row - lse_row))
                    dv_acc[pp, 0:klen] = dv_acc[pp, 0:klen] + _dot(pr.astype(BF16), dom)
                    dsv = pr * (_dot_nt(vtile[pp], dom) - dl_row)
                    ds_b = dsv.astype(BF16)
                    dk_acc[pp, 0:klen] = dk_acc[pp, 0:klen] + _dot(ds_b, qm)
                    rows = slice(e * HEAD_LANES, (e + 1) * HEAD_LANES)
                    dq_t = _dot(ktt[pp, kblk, rows, :], ds_b[0:BLK])
                    for b in range(1, nk):
                        dq_t = dq_t + _dot(ktt[pp, kblk + b, rows, :], ds_b[b * BLK:(b + 1) * BLK])
                    key_side = dsv[:, 0:BLK]
                    for a in range(1, nq):
                        key_side = key_side + dsv[:, a * BLK:(a + 1) * BLK]
                    dcs_acc[hh, 0:klen] = dcs_acc[hh, 0:klen] + key_side
                    query_side = jnp.sum(dsv, axis=0, keepdims=True)
                    for a in range(nq):
                        cols = slice(a * BLK, (a + 1) * BLK)
                        dqt[pp, qblk + a, rows, :] = dqt[pp, qblk + a, rows, :] + dq_t[:, cols]
                        dcq[a] = dcq[a] + jnp.where(sub8 == h, query_side[:, cols], 0.0)
                for a in range(nq):
                    dcq_ref[qblk + a] = dcq_ref[qblk + a] + dcq[a]

            later_mask = "valid" if nk == 1 else None
            n_later = jnp.asarray(n_later, jnp.int32)
            n_wide = n_later // WIDE

            def later_wide(i, carry):
                tile(kblk + nk + 2 * WIDE * i, 2 * WIDE, later_mask)
                return carry

            tile(kblk, nk, masks["first"] if nk == 1 else masks["diag"])
            lax.fori_loop(0, n_wide, later_wide, 0)
            rest_blk = kblk + nk + 2 * WIDE * n_wide

            @pl.when((n_later & 2) != 0)
            def _():
                tile(rest_blk, 4, later_mask)

            @pl.when((n_later & 1) != 0)
            def _():
                tile(rest_blk + 2 * (n_later & 2), 2, later_mask)

            upd = jnp.zeros((klen, BLK), F32)
            for hh, (_, _, h) in enumerate(heads):
                upd = upd + jnp.where(klane == h, -jnp.sum(dcs_acc[hh, 0:klen], axis=1, keepdims=True), 0.0)
            dc_ref[pl.ds(koff, klen), :] = dc_ref[pl.ds(koff, klen), :] + upd
            for pp in range(FOX_PAIRS):
                dp_ref[pl.ds(koff, klen), pp * 384 + BLK:pp * 384 + 2 * BLK] = dk_acc[pp, 0:klen].astype(BF16)
                dp_ref[pl.ds(koff, klen), pp * 384 + 2 * BLK:pp * 384 + 3 * BLK] = dv_acc[pp, 0:klen].astype(BF16)

        kv_pass(0, 1, nu)

        def k_loop(u, carry):
            kv_pass(1 + 2 * u, 2, nu - 1 - u)
            return carry

        lax.fori_loop(0, nu, k_loop, 0)

        def flush(j, carry):
            off = pl.multiple_of(j * BLK, BLK)
            for pp in range(FOX_PAIRS):
                dp_ref[pl.ds(off, BLK), pp * 384:pp * 384 + BLK] = (dqt[pp, j].T * scale).astype(BF16)
            return carry

        lax.fori_loop(0, nblk, flush, 0)

        if ns:
            @pl.when(p == steps - 1)
            def _():
                copies = _scatter_copies(s_in, s_out, *rest[2 * ns + 8:])
                for cp in copies:
                    cp.wait_recv()
                for cp in copies:
                    cp.wait_send()

    width = 384 * FOX_PAIRS
    once = lambda shape, index: pl.BlockSpec(shape, index, pipeline_mode=pl.Buffered(1))
    stat = once((nblk, 8, BLK), lambda p: (0, 0, 0))
    return pl.pallas_call(
        body, name="b_fox", grid=(steps,),
        in_specs=[once((L, width), lambda p: (0, RET_W // width + p)),
                  once((L, FOX_PAIRS * BLK), lambda p: (0, 4 // FOX_PAIRS + p)),
                  once((L, BLK), lambda p: (0, 0)), stat, stat, stat] + [_ANY] * ns,
        out_specs=[pl.BlockSpec((L, width), lambda p: (0, p)), _full((L, BLK)), _full((nblk, 8, BLK))] + [_ANY] * ns,
        out_shape=[jax.ShapeDtypeStruct((L, FOX_W), BF16), jax.ShapeDtypeStruct((L, BLK), F32),
                   jax.ShapeDtypeStruct((nblk, 8, BLK), F32)] + _scatter_shapes(scatter),
        scratch_shapes=[pltpu.VMEM((FOX_PAIRS, nblk, BLK, BLK), BF16), pltpu.VMEM((FOX_PAIRS, nblk, BLK, BLK), F32),
                        pltpu.VMEM((FOX_PAIRS, UNIT, BLK), F32), pltpu.VMEM((FOX_PAIRS, UNIT, BLK), F32),
                        pltpu.VMEM((2 * FOX_PAIRS, UNIT, BLK), F32)]
        + _scatter_semaphores(ns),
        compiler_params=_params(("arbitrary",)),
    )(proj, dmix, c, ctb, lse, delta, *scatter)


def _fox_post(dc, dcq, ff, fb):
    L = dc.shape[0]
    nblk = L // BLK
    G = _block_group(nblk)
    steps = nblk // G

    def body(dc_ref, dcq_ref, ff_ref, b_ref, dff_ref, dffb_ref, dfb_ref, carry):
        @pl.when(pl.program_id(0) == 0)
        def _():
            carry[...] = jnp.zeros_like(carry)
            dfb_ref[...] = jnp.zeros_like(dfb_ref)

        tri = (_iota((BLK, BLK), 0) <= _iota((BLK, BLK), 1)).astype(BF16)
        live = _iota((BLK, BLK), 1) < FOX_HEADS
        run, dfb = carry[...], dfb_ref[...]
        for b in reversed(range(G)):
            rows = slice(b * BLK, (b + 1) * BLK)
            d = dc_ref[rows, :] + jnp.concatenate([dcq_ref[b], jnp.zeros((BLK - 8, BLK), F32)], axis=0).T
            hi, mid, lo = _split3(d)
            dlf = _dot(tri, hi) + _dot(tri, mid) + _dot(tri, lo) + run
            run = run + jnp.sum(d, axis=0, keepdims=True)
            z = ff_ref[rows, :] + b_ref[...]
            dff = jnp.where(live, dlf * jax.nn.sigmoid(-z), 0.0)
            dff_ref[rows, :] = dff
            dffb_ref[rows, :] = dff.astype(BF16)
            dfb = dfb + jnp.sum(dff, axis=0, keepdims=True)
        carry[...] = run
        dfb_ref[...] = dfb

    rev = lambda i: (steps - 1 - i, 0)
    return pl.pallas_call(
        body, name="b_foxpost", grid=(steps,),
        in_specs=[pl.BlockSpec((G * BLK, BLK), rev), pl.BlockSpec((G, 8, BLK), lambda i: (steps - 1 - i, 0, 0)),
                  pl.BlockSpec((G * BLK, BLK), rev), _full((1, BLK))],
        out_specs=[pl.BlockSpec((G * BLK, BLK), rev), pl.BlockSpec((G * BLK, BLK), rev), _full((1, BLK))],
        out_shape=[jax.ShapeDtypeStruct((L, BLK), F32), jax.ShapeDtypeStruct((L, BLK), BF16),
                   jax.ShapeDtypeStruct((1, BLK), F32)],
        scratch_shapes=[pltpu.VMEM((1, BLK), F32)],
        compiler_params=_params(("arbitrary",)),
    )(dc, dcq, ff, fb)


def _inproj_bwd(dpr, dpf, dffb, w_main, w_ff, h0, g, dh1):
    L = h0.shape[0]
    tm = _row_tile(L)

    def body(dpr_ref, dpf_ref, dff_ref, wm_ref, wf_ref, h_ref, g_ref, dh1_ref, dh0_ref, dg_ref):
        @pl.when(pl.program_id(0) == 0)
        def _():
            dg_ref[...] = jnp.zeros_like(dg_ref)

        dn = (_dot_nt(dpr_ref[...], wm_ref[:, 0:RET_W]) + _dot_nt(dpf_ref[...], wm_ref[:, RET_W:MAIN_W])
              + _dot_nt(dff_ref[...], wf_ref[...]))
        h = h_ref[...]
        r = lax.rsqrt(jnp.mean(h * h, axis=-1, keepdims=True) + EPS)
        yn = h * r
        dg_ref[...] = dg_ref[...] + jnp.sum(dn * yn, axis=0, keepdims=True)
        dyn = dn * g_ref[...]
        dh0_ref[...] = dh1_ref[...] + r * (dyn - yn * jnp.mean(dyn * yn, axis=-1, keepdims=True))

    rows = lambda w: pl.BlockSpec((tm, w), lambda i: (i, 0))
    return pl.pallas_call(
        body, name="b_inproj", grid=(L // tm,),
        in_specs=[rows(RET_W), rows(FOX_W), rows(BLK), _full((D_MODEL, MAIN_W)), _full((D_MODEL, BLK)),
                  rows(D_MODEL), _full((1, D_MODEL)), rows(D_MODEL)],
        out_specs=[rows(D_MODEL), _full((1, D_MODEL))],
        out_shape=[jax.ShapeDtypeStruct((L, D_MODEL), F32), jax.ShapeDtypeStruct((1, D_MODEL), F32)],
        compiler_params=_params(("arbitrary",)),
    )(dpr, dpf, dffb, w_main, w_ff, h0, g, dh1)


def _local_step(x, target, meta, attn_g, w_main, w_ff, fox_b, ret_g, w_out, ffn_g, w_up, conv_w, conv_b, w_down, final_g,
                late=None, mid=None):
    S = x.shape[0]
    L = S + PREFIX
    h0 = jnp.concatenate([jnp.zeros((N_PAD, D_MODEL), F32), meta, x], axis=0)
    tgt = jnp.concatenate([jnp.zeros((PREFIX, D_MODEL), F32), target], axis=0)
    fb = jnp.pad(fox_b, ((0, 0), (0, BLK - FOX_HEADS)))
    cos_t, sin_t = _rotary_tables(L)

    n1, proj, ff = _rms_inproj(h0, attn_g, w_main, w_ff)
    c, ctb = _fox_prep(ff, fb)
    mix_r, o_ret, states = _retention_fwd(proj, cos_t, sin_t, ret_g)
    if late is None:
        o_f, lse = _fox_fwd(proj, c, ctb)
    else:
        o_f, lse, *gathered = _fox_fwd(proj, c, ctb, gather=late[0])
        w_out, w_up, w_down = late[1](gathered)
    h1, n2, up, g_act = _outproj_up(mix_r, o_f, h0, w_out, ffn_g, w_up, conv_w, conv_b)
    dh2, dh2b, d_final_g, loss = _ffn_down_loss(g_act, w_down, h1, final_g, tgt)

    dacc, db, dconv = _ffn_bwd_gate(dh2b, w_down, up, conv_w, conv_b)
    dup, dh1, dh1b, dmix, d_ffn_g = _ffn_bwd_up(dacc, db, conv_w, w_up, h1, ffn_g, dh2, w_out)
    d_w_down = _wgrad(g_act, dh2b, "wgrad_down", tk=D_FF // 2)[0]
    d_w_up = _wgrad(n2, dup, "wgrad_up", tn=w_up.shape[2])
    d_w_out = jnp.concatenate([_wgrad(mix_r, dh1b, "wgrad_out_r")[0], _wgrad(o_f, dh1b, "wgrad_out_f")[0]], axis=0)

    dpr, d_ret_g = _retention_bwd(dmix, o_ret, proj, cos_t, sin_t, ret_g, states)
    delta = _fox_delta(dmix, o_f)
    scatter = () if mid is None else mid(d_w_out, d_w_up, d_w_down)
    dpf, dc, dcq, *received = _fox_bwd(proj, dmix, c, ctb, lse, delta, scatter=scatter)
    dff, dffb, d_fox_b = _fox_post(dc, dcq, ff, fb)
    dh0, d_attn_g = _inproj_bwd(dpr, dpf, dffb, w_main, w_ff, h0, attn_g, dh1)
    d_w_main = jnp.concatenate([_wgrad(n1, dpr, "wgrad_in_r")[0], _wgrad(n1, dpf, "wgrad_in_f")[0]], axis=1)
    d_w_ff = _wgrad(n1, dffb, "wgrad_in_ff")[0]

    return dict(
        loss=loss[0, 0], dx=dh0[PREFIX:], dmeta=dh0[N_PAD:PREFIX], attn_g=d_attn_g, w_main=d_w_main,
        w_ff=d_w_ff[:, :FOX_HEADS], fox_b=d_fox_b[:, :FOX_HEADS], ret_g=d_ret_g, w_out=d_w_out, ffn_g=d_ffn_g,
        w_up=d_w_up, conv_w=dconv[0:3], conv_b=dconv[3:4], w_down=d_w_down, final_g=d_final_g,
        scatter=scatter, received=received)


_ANY = pl.BlockSpec(memory_space=pl.ANY)


def _place():
    return lax.axis_index("x"), lax.axis_index("y"), lax.axis_index("c")


def _other_chips(x, y):
    return [(1 - x, y), (x, 1 - y), (1 - x, 1 - y)]


def _allgather_semaphores(n):
    if n == 0:
        return []
    return [pltpu.SemaphoreType.DMA((3 * n,)), pltpu.SemaphoreType.DMA((3 * n,)), pltpu.SemaphoreType.DMA((n,))]


def _allgather_copies(ins, outs, send, recv, loc):
    n = len(ins)
    x, y, c = _place()
    mine = 2 * x + y
    peers = _other_chips(x, y)

    def remote(a, k, slot):
        return pltpu.make_async_remote_copy(
            src_ref=ins[a], dst_ref=outs[a].at[slot], send_sem=send.at[3 * a + k], recv_sem=recv.at[3 * a + k],
            device_id=(peers[k][0], peers[k][1], c), device_id_type=MESH)

    local = [pltpu.make_async_copy(ins[a], outs[a].at[mine], loc.at[a]) for a in range(n)]
    sends = [remote(a, k, mine) for a in range(n) for k in range(3)]
    recvs = [remote(a, k, 2 * peers[k][0] + peers[k][1]) for a in range(n) for k in range(3)]
    return local, sends, recvs


def _chip_allgather(arrays):
    n = len(arrays)

    def body(*refs):
        local, sends, recvs = _allgather_copies(refs[:n], refs[n:2 * n], *refs[2 * n:])
        for cp in local + sends:
            cp.start()
        for cp in recvs:
            cp.wait_recv()
        for cp in sends:
            cp.wait_send()
        for cp in local:
            cp.wait()

    return pl.pallas_call(
        body, name="ag_weights", in_specs=[_ANY] * n, out_specs=[_ANY] * n,
        out_shape=[jax.ShapeDtypeStruct((N_CHIPS,) + a.shape, a.dtype) for a in arrays],
        scratch_shapes=_allgather_semaphores(n),
    )(*arrays)


def _sibling_exchange(grads, small):
    n = len(grads)

    def body(*refs):
        ins, small_in = refs[:n], refs[n]
        outs, small_out = refs[n + 1:2 * n + 1], refs[2 * n + 1]
        send, recv, s_send, s_recv, loc = refs[2 * n + 2:]
        x, y, c = _place()
        me = 4 * x + 2 * y + c

        def half_copy(a, which):
            half = ins[a].shape[1] // 2
            return pltpu.make_async_remote_copy(
                src_ref=ins[a].at[pl.ds(0, N_CHIPS), pl.ds(which * half, half)], dst_ref=outs[a],
                send_sem=send.at[a], recv_sem=recv.at[a], device_id=(x, y, 1 - c), device_id_type=MESH)

        def peer_of(r):
            return tuple(1 - v if (r >> b) & 1 else v for v, b in ((x, 2), (y, 1), (c, 0)))

        def small_copy(r, slot):
            return pltpu.make_async_remote_copy(
                src_ref=small_in, dst_ref=small_out.at[slot], send_sem=s_send.at[r - 1], recv_sem=s_recv.at[r - 1],
                device_id=peer_of(r), device_id_type=MESH)

        local = pltpu.make_async_copy(small_in, small_out.at[me], loc.at[0])
        sends = [half_copy(a, 1 - c) for a in range(n)] + [small_copy(r, me) for r in range(1, N_DEV)]
        local.start()
        for cp in sends:
            cp.start()
        for r in range(1, N_DEV):
            px, py, pc = peer_of(r)
            small_copy(r, 4 * px + 2 * py + pc).wait_recv()
        for a in range(n):
            half_copy(a, c).wait_recv()
        for cp in sends:
            cp.wait_send()
        local.wait()

    rows = small.shape[0]
    return pl.pallas_call(
        body, name="rs_sibling", in_specs=[_ANY] * (n + 1), out_specs=[_ANY] * (n + 1),
        out_shape=[jax.ShapeDtypeStruct((N_CHIPS, g.shape[1] // 2, g.shape[2]), g.dtype) for g in grads]
        + [jax.ShapeDtypeStruct((N_DEV, rows, small.shape[1]), small.dtype)],
        scratch_shapes=[pltpu.SemaphoreType.DMA((n,)), pltpu.SemaphoreType.DMA((n,)),
                        pltpu.SemaphoreType.DMA((N_DEV - 1,)), pltpu.SemaphoreType.DMA((N_DEV - 1,)),
                        pltpu.SemaphoreType.DMA((1,))],
    )(*grads, small)


def _sibling_halves(grads):
    n = len(grads)

    def body(*refs):
        ins, outs = refs[:n], refs[n:2 * n]
        send, recv = refs[2 * n:]
        x, y, c = _place()

        def half_copy(a, which):
            half = ins[a].shape[1] // 2
            return pltpu.make_async_remote_copy(
                src_ref=ins[a].at[pl.ds(0, N_CHIPS), pl.ds(which * half, half)], dst_ref=outs[a],
                send_sem=send.at[a], recv_sem=recv.at[a], device_id=(x, y, 1 - c), device_id_type=MESH)

        sends = [half_copy(a, 1 - c) for a in range(n)]
        for cp in sends:
            cp.start()
        for a in range(n):
            half_copy(a, c).wait_recv()
        for cp in sends:
            cp.wait_send()

    return pl.pallas_call(
        body, name="rs_sibling_early", in_specs=[_ANY] * n, out_specs=[_ANY] * n,
        out_shape=[jax.ShapeDtypeStruct((N_CHIPS, g.shape[1] // 2, g.shape[2]), g.dtype) for g in grads],
        scratch_shapes=[pltpu.SemaphoreType.DMA((n,)), pltpu.SemaphoreType.DMA((n,))],
    )(*grads)


def _chip_reduce_scatter(parts):
    n = len(parts)

    def body(*refs):
        copies = _scatter_copies(refs[:n], refs[n:2 * n], *refs[2 * n:])
        for cp in copies:
            cp.start()
        for cp in copies:
            cp.wait_recv()
        for cp in copies:
            cp.wait_send()

    return pl.pallas_call(
        body, name="rs_chips", in_specs=[_ANY] * n, out_specs=[_ANY] * n,
        out_shape=_scatter_shapes(parts), scratch_shapes=_scatter_semaphores(n),
    )(*parts)


def _scatter_shapes(parts):
    return [jax.ShapeDtypeStruct((3,) + p.shape[1:], p.dtype) for p in parts]


def _scatter_semaphores(n):
    return [pltpu.SemaphoreType.DMA((3 * n,)), pltpu.SemaphoreType.DMA((3 * n,))] if n else []


def _scatter_copies(ins, outs, send, recv):
    x, y, c = _place()
    peers = _other_chips(x, y)
    return [pltpu.make_async_remote_copy(
        src_ref=ins[a].at[2 * peers[k][0] + peers[k][1]], dst_ref=outs[a].at[k], send_sem=send.at[3 * a + k],
        recv_sem=recv.at[3 * a + k], device_id=(peers[k][0], peers[k][1], c), device_id_type=MESH)
        for a in range(len(ins)) for k in range(3)]


def _sibling_allgather(bufs):
    n = len(bufs)

    def body(*refs):
        outs = refs[n:2 * n]
        send, recv = refs[2 * n:]
        x, y, c = _place()

        def remote(a, which):
            return pltpu.make_async_remote_copy(
                src_ref=outs[a].at[which], dst_ref=outs[a].at[which], send_sem=send.at[a], recv_sem=recv.at[a],
                device_id=(x, y, 1 - c), device_id_type=MESH)

        sends = [remote(a, c) for a in range(n)]
        for cp in sends:
            cp.start()
        for a in range(n):
            remote(a, 1 - c).wait_recv()
        for cp in sends:
            cp.wait_send()

    outs = pl.pallas_call(
        body, name="ag_sibling", in_specs=[_ANY] * n, out_specs=[_ANY] * n,
        out_shape=[jax.ShapeDtypeStruct(b.shape, b.dtype) for b in bufs],
        input_output_aliases={a: a for a in range(n)},
        scratch_shapes=[pltpu.SemaphoreType.DMA((n,)), pltpu.SemaphoreType.DMA((n,))],
    )(*bufs)
    return [o.reshape(2 * o.shape[1], o.shape[2]) for o in outs]


def _pair_add(full, recv, core, name):
    _, R, C = full.shape
    half = R // 2

    def body(core_ref, a_ref, b_ref, o_ref):
        o_ref[...] = (a_ref[...] + b_ref[...]).astype(BF16)

    return pl.pallas_call(
        body, name=name,
        grid_spec=pltpu.PrefetchScalarGridSpec(
            num_scalar_prefetch=1, grid=(N_CHIPS,),
            in_specs=[pl.BlockSpec((1, half, C), lambda j, core_ref: (j, core_ref[0], 0)),
                      pl.BlockSpec((1, half, C), lambda j, core_ref: (j, 0, 0))],
            out_specs=pl.BlockSpec((1, half, C), lambda j, core_ref: (j, 0, 0))),
        out_shape=jax.ShapeDtypeStruct((N_CHIPS, half, C), BF16),
        compiler_params=_params(("parallel",)),
    )(core, full, recv)


def _sum_slots(q, name, tiles=2):
    n, R, C = q.shape
    tr = R // tiles

    def body(q_ref, o_ref):
        acc = q_ref[0].astype(F32)
        for j in range(1, n):
            acc = acc + q_ref[j].astype(F32)
        o_ref[...] = acc

    return pl.pallas_call(
        body, name=name, grid=(tiles,),
        in_specs=[pl.BlockSpec((n, tr, C), lambda i: (0, i, 0))],
        out_specs=pl.BlockSpec((tr, C), lambda i: (i, 0)),
        out_shape=jax.ShapeDtypeStruct((R, C), F32),
        compiler_params=_params(("parallel",)),
    )(q)


def _sum_partials(own_all, recv, place, name, tiles=2):
    _, R, C = own_all.shape
    tr = R // tiles

    def body(place_ref, own_ref, r_ref, o_ref):
        acc = own_ref[0].astype(F32)
        for k in range(3):
            acc = acc + r_ref[k].astype(F32)
        o_ref[0] = acc

    return pl.pallas_call(
        body, name=name,
        grid_spec=pltpu.PrefetchScalarGridSpec(
            num_scalar_prefetch=1, grid=(tiles,),
            in_specs=[pl.BlockSpec((1, tr, C), lambda i, place_ref: (place_ref[0], i, 0)),
                      pl.BlockSpec((3, tr, C), lambda i, place_ref: (0, i, 0))],
            out_specs=pl.BlockSpec((1, tr, C), lambda i, place_ref: (place_ref[1], i, 0))),
        out_shape=jax.ShapeDtypeStruct((2, R, C), F32),
        compiler_params=_params(("parallel",)),
    )(place, own_all, recv)


def _adamw(w, g, m, v, name, tiles=4):
    R, C = w.shape
    tr = R // tiles

    def body(w_ref, g_ref, m_ref, v_ref, go_ref, d_ref, m2_ref, v2_ref):
        g_ = g_ref[...]
        go_ref[...] = g_
        m2 = ADAM_B1 * m_ref[...] + (1.0 - ADAM_B1) * g_
        v2 = ADAM_B2 * v_ref[...] + (1.0 - ADAM_B2) * (g_ * g_)
        m_hat = m2 / (1.0 - ADAM_B1 ** ADAM_STEP)
        v_hat = v2 / (1.0 - ADAM_B2 ** ADAM_STEP)
        d_ref[...] = -ADAM_LR * (m_hat / (jnp.sqrt(v_hat) + ADAM_EPS) + ADAM_WD * w_ref[...])
        m2_ref[...] = m2
        v2_ref[...] = v2

    spec = pl.BlockSpec((tr, C), lambda i: (i, 0))
    return pl.pallas_call(
        body, name=name, grid=(tiles,), in_specs=[spec] * 4, out_specs=[spec] * 4,
        out_shape=[jax.ShapeDtypeStruct((R, C), F32)] * 4,
        compiler_params=_params(("parallel",)),
    )(w, g, m, v)


def _pack_rows(pieces, rows):
    flat = jnp.concatenate([jnp.pad(p.reshape(-1).astype(F32), (0, (-p.size) % D_MODEL)) for p in pieces])
    return jnp.pad(flat, (0, rows * D_MODEL - flat.size)).reshape(rows, D_MODEL)


def _unpack_rows(pack, shapes):
    flat = pack.reshape(-1)
    out, off = [], 0
    for shp in shapes:
        size = int(np.prod(shp))
        out.append(flat[off:off + size].reshape(shp))
        off += size + (-size) % D_MODEL
    return out


def _kernel_order(w):
    parts = [w[:, 0:RET_W]]
    for p in range(FOX_HEADS // 2):
        parts += [w[:, RET_W + part * 512 + p * BLK:RET_W + part * 512 + (p + 1) * BLK] for part in range(3)]
    return jnp.concatenate(parts, axis=1)


def _reference_order(g_main, g_ff):
    parts = [g_main[:, 0:RET_W]]
    for part in range(3):
        parts += [g_main[:, RET_W + 384 * p + part * BLK:RET_W + 384 * p + (part + 1) * BLK] for p in range(FOX_HEADS // 2)]
    return jnp.concatenate(parts + [g_ff], axis=1)


def kernel(x, meta_tokens, attn_norm_g, w_in, fox_forget_b, ret_norm_g, w_out, ffn_norm_g, w_up, conv_w, conv_b, w_down, final_norm_g, loss_target, m_meta_tokens, m_attn_norm_g, m_w_in, m_fox_forget_b, m_ret_norm_g, m_w_out, m_ffn_norm_g, m_w_up, m_conv_w, m_conv_b, m_w_down, m_final_norm_g, v_meta_tokens, v_attn_norm_g, v_w_in, v_fox_forget_b, v_ret_norm_g, v_w_out, v_ffn_norm_g, v_w_up, v_conv_w, v_conv_b, v_w_down, v_final_norm_g):
    chip = 2 * lax.axis_index("x") + lax.axis_index("y")
    core = lax.axis_index("c")
    meta_w, conv_sw = meta_tokens.shape[1], conv_w.shape[2]

    small_w = _pack_rows([meta_tokens, conv_w[0]], 8)
    g_in, g_small = _chip_allgather([w_in[0].astype(BF16), small_w])
    w_in_full = g_in.transpose(1, 0, 2).reshape(D_MODEL, IN_WIDTH)
    w_main = _kernel_order(w_in_full)
    w_ff = jnp.pad(w_in_full[:, MAIN_W:], ((0, 0), (0, BLK - FOX_HEADS)))
    small_parts = [_unpack_rows(g_small[j], [meta_tokens.shape, conv_w.shape[1:]]) for j in range(N_CHIPS)]
    meta_full = jnp.concatenate([sp[0] for sp in small_parts], axis=1)
    conv_w_full = jnp.concatenate([sp[1] for sp in small_parts], axis=1)

    core_idx = core.reshape(1).astype(jnp.int32)
    place = jnp.stack([chip, core]).astype(jnp.int32)

    def assemble(gathered):
        g_out, g_up, g_down = gathered
        return g_out.reshape(D_MODEL, D_MODEL), g_up, g_down.reshape(D_FF, D_MODEL)

    def early_reduce(d_w_out, d_w_up, d_w_down):
        early = [d_w_out.reshape(N_CHIPS, -1, D_MODEL), d_w_up, d_w_down.reshape(N_CHIPS, -1, D_MODEL)]
        from_sib = _sibling_halves(early)
        return [_pair_add(g, r, core_idx, "pair_add_" + nm) for g, r, nm in zip(early, from_sib, ("out", "up", "down"))]

    out = _local_step(x[0], loss_target[0], meta_full, attn_norm_g, w_main, w_ff, fox_forget_b, ret_norm_g,
                      None, ffn_norm_g, None, conv_w_full, conv_b, None, final_norm_g[None],
                      late=([w_out[0].astype(BF16), w_up[0].astype(BF16), w_down[0].astype(BF16)], assemble),
                      mid=early_reduce)

    g_in_full = _reference_order(out["w_main"], out["w_ff"]).reshape(D_MODEL, N_CHIPS, -1).transpose(1, 0, 2)
    small_shapes = [(1, D_MODEL), (1, D_MODEL), (1, D_MODEL), (1, 512 + FOX_HEADS + 1), (1, D_FF), (N_META, D_MODEL), (3, D_FF)]
    small = _pack_rows([out["attn_g"], out["ffn_g"], out["final_g"],
                        jnp.concatenate([out["ret_g"], out["fox_b"], out["loss"].reshape(1, 1)], axis=1),
                        out["conv_b"], out["dmeta"], out["conv_w"]], 32)
    from_sibling_in, small_all = _sibling_exchange([g_in_full], small)
    sum_in = _pair_add(g_in_full, from_sibling_in, core_idx, "pair_add_in")
    (from_chips_in,) = _chip_reduce_scatter([sum_in])
    chip_sums = [sum_in] + list(out["scatter"])
    from_chips = [from_chips_in] + list(out["received"])
    names = ("in", "out", "up", "down")
    totals = [_sum_partials(s, q, place, "sum_chips_" + nm) for s, q, nm in zip(chip_sums, from_chips, names)]
    grad_in, grad_out, grad_up, grad_down = _sibling_allgather(totals)
    s_attn, s_ffn, s_final, s_misc, s_conv_b, s_meta, s_conv_w = _unpack_rows(
        _sum_slots(small_all, "sum_small", tiles=1), small_shapes)
    loss = s_misc[0, 512 + FOX_HEADS]
    small_grads = [lax.dynamic_slice_in_dim(s_meta, chip * meta_w, meta_w, axis=1), s_attn, s_misc[:, 512:512 + FOX_HEADS],
                   s_misc[:, :512], s_ffn, lax.dynamic_slice_in_dim(s_conv_w, chip * conv_sw, conv_sw, axis=1)[None],
                   s_conv_b, s_final[0]]

    big_w = [(w_in, m_w_in, v_w_in, grad_in, "adamw_in"), (w_out, m_w_out, v_w_out, grad_out, "adamw_out"),
             (w_up, m_w_up, v_w_up, grad_up, "adamw_up"), (w_down, m_w_down, v_w_down, grad_down, "adamw_down")]
    big_res = [[r[None] for r in _adamw(w[0], g, m[0], v[0], nm)] for w, m, v, g, nm in big_w]
    small_w_list = [meta_tokens, attn_norm_g, fox_forget_b, ret_norm_g, ffn_norm_g, conv_w, conv_b, final_norm_g]
    small_m = [m_meta_tokens, m_attn_norm_g, m_fox_forget_b, m_ret_norm_g, m_ffn_norm_g, m_conv_w, m_conv_b, m_final_norm_g]
    small_v = [v_meta_tokens, v_attn_norm_g, v_fox_forget_b, v_ret_norm_g, v_ffn_norm_g, v_conv_w, v_conv_b, v_final_norm_g]
    shapes = [a.shape for a in small_w_list]
    packs = [_pack_rows(lst, 16) for lst in (small_w_list, small_grads, small_m, small_v)]
    small_res = [_unpack_rows(r, shapes) for r in _adamw(*packs, "adamw_small", tiles=1)[1:]]
    small_grads = [g.reshape(s) for g, s in zip(small_grads, shapes)]

    def ordered(kind):
        sm = small_grads if kind == 0 else small_res[kind - 1]
        bg = [r[kind] for r in big_res]
        return [sm[0], sm[1], bg[0], sm[2], sm[3], bg[1], sm[4], bg[2], sm[5], sm[6], bg[3], sm[7]]

    return (loss, out["dx"][None], *ordered(0), *ordered(1), *ordered(2), *ordered(3))
```

```python
import functools

import numpy as np
import jax
import jax.numpy as jnp
from jax import lax
from jax.experimental import pallas as pl
from jax.experimental.pallas import tpu as pltpu

F32 = jnp.float32
BF16 = jnp.bfloat16

D_MODEL = 1024
N_META = 16
BLK = 128
UNIT = 2 * BLK
FOX_PAIRS = 2
WIDE = 4
CHUNK = 64
N_PAD = BLK - N_META
PREFIX = BLK
RET_HEADS = 4
FOX_HEADS = 8
HEAD_LANES = 64
D_FF = 2816
ROPE_BASE = 10000.0
EPS = 1e-6
NEG = -1e30
RET_W = 1536
FOX_W = 1536
MAIN_W = RET_W + FOX_W
IN_WIDTH = MAIN_W + FOX_HEADS
N_CHIPS = 4
N_DEV = 8

ADAM_LR = 0.001
ADAM_B1 = 0.9
ADAM_B2 = 0.999
ADAM_EPS = 1e-08
ADAM_WD = 0.01
ADAM_STEP = 10

MESH = pl.DeviceIdType.MESH
VMEM_LIMIT_MB = 56

_NT = (((1,), (1,)), ((), ()))
_TN = (((0,), (0,)), ((), ()))


def _dot(a, b):
    return jnp.dot(a, b, preferred_element_type=F32)


def _dot_nt(a, b):
    return lax.dot_general(a, b, _NT, preferred_element_type=F32)


def _dot_tn(a, b):
    return lax.dot_general(a, b, _TN, preferred_element_type=F32)


def _params(dims=None, vmem_mb=VMEM_LIMIT_MB):
    kw = dict(vmem_limit_bytes=vmem_mb << 20)
    if dims is not None:
        kw["dimension_semantics"] = dims
    return pltpu.CompilerParams(**kw)


def _row_tile(n, prefs=(384, 256, 128)):
    for t in prefs:
        if n % t == 0:
            return t
    raise ValueError(f"no row tile for {n}")


def _iota(shape, dim):
    return lax.broadcasted_iota(jnp.int32, shape, dim)


def _pick_row(tile, row):
    sub = _iota(tile.shape, 0)
    return jnp.sum(jnp.where(sub == row, tile, 0.0), axis=0, keepdims=True)


def _split3(x):
    hi = x.astype(BF16)
    r1 = x - hi.astype(F32)
    mid = r1.astype(BF16)
    lo = (r1 - mid.astype(F32)).astype(BF16)
    return hi, mid, lo


def _full(shape):
    nd = len(shape)
    return pl.BlockSpec(shape, lambda *_: (0,) * nd)


def _in_perm():
    cols = list(range(RET_W))
    for p in range(FOX_HEADS // 2):
        for part in range(3):
            start = RET_W + part * 512 + p * BLK
            cols += list(range(start, start + BLK))
    return np.asarray(cols, np.int32)


def _rotary_tables(L):
    half = HEAD_LANES // 2
    inv = 1.0 / (ROPE_BASE ** (jnp.arange(half, dtype=F32) / half))
    ang = jnp.arange(L).astype(F32)[:, None] * inv[None, :]
    cos, sin = jnp.cos(ang), jnp.sin(ang)
    cos_t = jnp.tile(cos, (1, 4))
    sin_t = jnp.tile(jnp.concatenate([-sin, sin], axis=1), (1, 2))
    return cos_t, sin_t


def _decay_tables():
    gam = 1.0 - 2.0 ** (-5.0 - np.arange(RET_HEADS, dtype=np.float64))
    n = np.arange(BLK)
    same_or_past = (n[:, None] // CHUNK) >= (n[None, :] // CHUNK)
    dist = np.abs(n[:, None] - n[None, :])
    dmat = np.stack([np.where(same_or_past, g ** dist, 0.0) for g in gam]).astype(np.float32)
    lane_head = np.arange(BLK) // HEAD_LANES
    wq = np.stack([gam[2 * p + lane_head][None, :] ** (n[:, None] + 1.0) for p in range(2)]).astype(np.float32)
    wk = np.stack([gam[2 * p + lane_head][None, :] ** (BLK - 1.0 - n[:, None]) for p in range(2)]).astype(np.float32)
    g_blk = tuple(float(g ** BLK) for g in gam)
    return jnp.asarray(dmat), jnp.asarray(wq), jnp.asarray(wk), g_blk


def _rms_inproj(h0, g, w_main, w_ff):
    L = h0.shape[0]
    tm = _row_tile(L)

    def body(h_ref, g_ref, wm_ref, wf_ref, n_ref, p_ref, ff_ref):
        h = h_ref[...]
        r = lax.rsqrt(jnp.mean(h * h, axis=-1, keepdims=True) + EPS)
        n = (h * r * g_ref[...]).astype(BF16)
        n_ref[...] = n
        p_ref[...] = _dot(n, wm_ref[...]).astype(BF16)
        ff_ref[...] = _dot(n, wf_ref[...])

    return pl.pallas_call(
        body, name="f_inproj", grid=(L // tm,),
        in_specs=[pl.BlockSpec((tm, D_MODEL), lambda i: (i, 0)), _full((1, D_MODEL)),
                  _full((D_MODEL, MAIN_W)), _full((D_MODEL, BLK))],
        out_specs=[pl.BlockSpec((tm, D_MODEL), lambda i: (i, 0)), pl.BlockSpec((tm, MAIN_W), lambda i: (i, 0)),
                   pl.BlockSpec((tm, BLK), lambda i: (i, 0))],
        out_shape=[jax.ShapeDtypeStruct((L, D_MODEL), BF16), jax.ShapeDtypeStruct((L, MAIN_W), BF16),
                   jax.ShapeDtypeStruct((L, BLK), F32)],
        compiler_params=_params(("parallel",)),
    )(h0, g, w_main, w_ff)


def _block_group(nblk):
    return 3 if nblk % 3 == 0 else 1


def _fox_prep(ff, fb):
    L = ff.shape[0]
    nblk = L // BLK
    G = _block_group(nblk)

    def body(ff_ref, b_ref, c_ref, ct_ref, carry):
        @pl.when(pl.program_id(0) == 0)
        def _():
            carry[...] = jnp.zeros_like(carry)

        tri = (_iota((BLK, BLK), 0) >= _iota((BLK, BLK), 1)).astype(BF16)
        live = _iota((BLK, BLK), 1) < FOX_HEADS
        run = carry[...]
        for b in range(G):
            z = ff_ref[b * BLK:(b + 1) * BLK, :] + b_ref[...]
            lf = jnp.where(live, jnp.minimum(z, 0.0) - jnp.log1p(jnp.exp(-jnp.abs(z))), 0.0)
            hi, mid, lo = _split3(lf)
            cs = _dot(tri, hi) + _dot(tri, mid) + _dot(tri, lo) + run
            c_ref[b * BLK:(b + 1) * BLK, :] = cs
            ct_ref[b] = cs.T[0:8, :]
            run = run + jnp.sum(lf, axis=0, keepdims=True)
        carry[...] = run

    return pl.pallas_call(
        body, name="f_foxprep", grid=(nblk // G,),
        in_specs=[pl.BlockSpec((G * BLK, BLK), lambda i: (i, 0)), _full((1, BLK))],
        out_specs=[pl.BlockSpec((G * BLK, BLK), lambda i: (i, 0)), pl.BlockSpec((G, 8, BLK), lambda i: (i, 0, 0))],
        out_shape=[jax.ShapeDtypeStruct((L, BLK), F32), jax.ShapeDtypeStruct((nblk, 8, BLK), F32)],
        scratch_shapes=[pltpu.VMEM((1, BLK), F32)],
        compiler_params=_params(("arbitrary",)),
    )(ff, fb)


def _rot_fns(cos, sin):
    lane = _iota((BLK, BLK), 1)
    first = (lane & (HEAD_LANES - 1)) < HEAD_LANES // 2

    def swap(x):
        return jnp.where(first, pltpu.roll(x, BLK - 32, 1), pltpu.roll(x, 32, 1))

    def rot(x):
        return x * cos + swap(x) * sin

    def rot_t(dy):
        return dy * cos + swap(dy * sin)

    return rot, rot_t


def _retention_fwd(proj, cos_t, sin_t, ret_g):
    L = proj.shape[0]
    nblk = L // BLK
    G = _block_group(nblk)
    dmat, wq_t, wk_t, g_blk = _decay_tables()

    def body(q_ref, k_ref, v_ref, gate_ref, cos_ref, sin_ref, d_ref, wq_ref, wk_ref, rg_ref,
             mix_ref, o_ref, rs_ref, state):
        @pl.when(pl.program_id(0) == 0)
        def _():
            state[...] = jnp.zeros_like(state)

        lane = _iota((BLK, BLK), 1)
        sub = _iota((BLK, BLK), 0)
        for b in range(G):
            rows = slice(b * BLK, (b + 1) * BLK)
            rot, _ = _rot_fns(cos_ref[rows, :], sin_ref[rows, :])
            for p in range(2):
                qr = rot(q_ref[rows, p * BLK:(p + 1) * BLK].astype(F32))
                kr = rot(k_ref[rows, p * BLK:(p + 1) * BLK].astype(F32)) * (HEAD_LANES ** -0.5)
                kr_b = kr.astype(BF16)
                qw = (qr * wq_ref[p]).astype(BF16)
                kw = (kr * wk_ref[p]).astype(BF16)
                for e in range(2):
                    h = 2 * p + e
                    cols = slice(h * BLK, (h + 1) * BLK)
                    qm = jnp.where((lane >> 6) == e, qr, 0.0).astype(BF16)
                    s = _dot_nt(qm, kr_b) * d_ref[h]
                    vh = v_ref[rows, cols]
                    st = state[h]
                    rs_ref[b, h] = st
                    o = _dot(s.astype(BF16), vh) + _dot(qw, st.astype(BF16))
                    u = jnp.where((sub >> 6) == e, _dot_tn(kw, vh), 0.0)
                    state[h] = g_blk[h] * st + u
                    rn = lax.rsqrt(jnp.mean(o * o, axis=-1, keepdims=True) + EPS)
                    gate = gate_ref[rows, cols].astype(F32)
                    o_ref[rows, cols] = o
                    mix_ref[rows, cols] = (o * rn * rg_ref[:, cols] * (gate * jax.nn.sigmoid(gate))).astype(BF16)

    row = lambda c: (lambda i: (i, c))
    return pl.pallas_call(
        body, name="f_retention", grid=(nblk // G,),
        in_specs=[pl.BlockSpec((G * BLK, 256), row(0)), pl.BlockSpec((G * BLK, 256), row(1)),
                  pl.BlockSpec((G * BLK, 512), row(1)), pl.BlockSpec((G * BLK, 512), row(2)),
                  pl.BlockSpec((G * BLK, BLK), row(0)), pl.BlockSpec((G * BLK, BLK), row(0)),
                  _full((RET_HEADS, BLK, BLK)), _full((2, BLK, BLK)), _full((2, BLK, BLK)), _full((1, 512))],
        out_specs=[pl.BlockSpec((G * BLK, 512), row(0)), pl.BlockSpec((G * BLK, 512), row(0)),
                   pl.BlockSpec((G, RET_HEADS, BLK, BLK), lambda i: (i, 0, 0, 0))],
        out_shape=[jax.ShapeDtypeStruct((L, 512), BF16), jax.ShapeDtypeStruct((L, 512), F32),
                   jax.ShapeDtypeStruct((nblk, RET_HEADS, BLK, BLK), F32)],
        scratch_shapes=[pltpu.VMEM((RET_HEADS, BLK, BLK), F32)],
        compiler_params=_params(("arbitrary",)),
    )(proj, proj, proj, proj, cos_t, sin_t, dmat, wq_t, wk_t, ret_g)


def _fox_units(L):
    nblk = L // BLK
    assert L % BLK == 0 and nblk % 2 == 1, "sequence must be one 128-row block plus whole 256-row tiles"
    return nblk, (nblk - 1) // 2


def _fox_tile_masks():
    sub, lane = _iota((BLK, BLK), 0), _iota((BLK, BLK), 1)
    valid = _iota((BLK, UNIT), 0) >= N_PAD
    diag = _iota((UNIT, UNIT), 0) <= _iota((UNIT, UNIT), 1)
    r, q = _iota((BLK + UNIT, UNIT), 0), _iota((BLK + UNIT, UNIT), 1)
    first_and_diag = ((r < BLK) & (r >= N_PAD)) | ((r >= BLK) & (r - BLK <= q))
    return dict(first=(sub <= lane) & (sub >= N_PAD), valid=valid, diag=diag, first_and_diag=first_and_diag)


def _fox_fwd(proj, c, ctb, gather=()):
    L = proj.shape[0]
    nblk, nu = _fox_units(L)
    scale = HEAD_LANES ** -0.5
    ng = len(gather)
    steps = FOX_HEADS // (2 * FOX_PAIRS)

    def body(qkv_ref, c_ref, ct_ref, *rest):
        g_in, (of_ref, lse_ref), g_out = rest[:ng], rest[ng:ng + 2], rest[ng + 2:2 * ng + 2]
        vt, csb = rest[2 * ng + 2:2 * ng + 4]
        p = pl.program_id(0)
        heads = [(pp, e, 2 * FOX_PAIRS * p + 2 * pp + e) for pp in range(FOX_PAIRS) for e in range(2)]

        @pl.when(p == 0)
        def _():
            lse_ref[...] = jnp.zeros_like(lse_ref)
            if ng:
                local, sends, _ = _allgather_copies(g_in, g_out, *rest[2 * ng + 4:])
                for cp in local + sends:
                    cp.start()

        lane = _iota((BLK, BLK), 1)
        sub8 = _iota((8, BLK), 0)
        masks = _fox_tile_masks()

        def pre(j, carry):
            off = pl.multiple_of(j * BLK, BLK)
            ct = c_ref[pl.ds(off, BLK), :]
            for pp in range(FOX_PAIRS):
                vt[pp, j] = qkv_ref[pl.ds(off, BLK), pp * 384 + 2 * BLK:pp * 384 + 3 * BLK].astype(F32).T.astype(BF16)
            for hh, (_, _, h) in enumerate(heads):
                col = jnp.sum(jnp.where(lane == h, ct, 0.0), axis=1, keepdims=True)
                csb[hh, j] = jnp.broadcast_to(col, (BLK, BLK))
            return carry

        lax.fori_loop(0, nblk, pre, 0)

        def attend(qblk, nq, n_whole):
            qlen = nq * BLK
            qoff = pl.multiple_of(qblk * BLK, BLK)
            qlane = _iota((qlen, BLK), 1)
            qs = [qkv_ref[pl.ds(qoff, qlen), pp * 384:pp * 384 + BLK].astype(F32) * scale for pp in range(FOX_PAIRS)]
            qm = [jnp.where((qlane >> 6) == e, qs[pp], 0.0).astype(BF16) for pp, e, _ in heads]
            ct_row = [jnp.concatenate([_pick_row(ct_ref[qblk + a], h) for a in range(nq)], axis=1) for _, _, h in heads]

            def step(segs, mask, st):
                blocks = [kblk + b for kblk, nk in segs for b in range(nk)]
                kts = []
                for pp in range(FOX_PAIRS):
                    kt = [qkv_ref[pl.ds(pl.multiple_of(kblk * BLK, BLK), nk * BLK), pp * 384 + BLK:pp * 384 + 2 * BLK]
                          for kblk, nk in segs]
                    kts.append(kt[0] if len(kt) == 1 else jnp.concatenate(kt, axis=0))
                out = []
                for hh, (pp, e, _) in enumerate(heads):
                    m, l, acc = st[3 * hh:3 * hh + 3]
                    s = _dot_nt(kts[pp], qm[hh])
                    t = jnp.concatenate([s[b * BLK:(b + 1) * BLK] - jnp.concatenate([csb[hh, blk]] * nq, axis=1)
                                         for b, blk in enumerate(blocks)], axis=0)
                    if mask is not None:
                        t = jnp.where(mask, t, NEG)
                    m_new = jnp.maximum(m, jnp.max(t, axis=0, keepdims=True) + ct_row[hh])
                    alpha = jnp.exp(m - m_new)
                    pr = jnp.exp(t - (m_new - ct_row[hh]))
                    l = alpha * l + jnp.sum(pr, axis=0, keepdims=True)
                    pr_b = pr.astype(BF16)
                    pv = None
                    for b, blk in enumerate(blocks):
                        part = _dot(vt[pp, blk, e * HEAD_LANES:(e + 1) * HEAD_LANES, :], pr_b[b * BLK:(b + 1) * BLK])
                        pv = part if pv is None else pv + part
                    out += [m_new, l, alpha * acc + pv]
                return tuple(out)

            st = (jnp.full((1, qlen), NEG, F32), jnp.zeros((1, qlen), F32),
                  jnp.zeros((HEAD_LANES, qlen), F32)) * len(heads)
            if nq == 1:
                st = step([(0, 1)], masks["first"], st)
            else:
                st = step([(0, 1), (qblk, 2)], masks["first_and_diag"], st)
                n_wide = n_whole // WIDE
                st = lax.fori_loop(0, n_wide, lambda j, s_: step([(1 + 2 * WIDE * j, 2 * WIDE)], None, s_), st)
                rest = 1 + 2 * WIDE * n_wide
                st = lax.cond((n_whole & 2) != 0, lambda s_: step([(rest, 4)], None, s_), lambda s_: s_, st)
                st = lax.cond((n_whole & 1) != 0, lambda s_: step([(rest + 2 * (n_whole & 2), 2)], None, s_),
                              lambda s_: s_, st)
            for pp in range(FOX_PAIRS):
                lo, hi = st[6 * pp:6 * pp + 3], st[6 * pp + 3:6 * pp + 6]
                o_t = jnp.concatenate([lo[2] * (1.0 / lo[1]), hi[2] * (1.0 / hi[1])], axis=0)
                of_ref[pl.ds(qoff, qlen), pp * BLK:(pp + 1) * BLK] = o_t.T.astype(BF16)
            lse = [st[3 * hh] + jnp.log(st[3 * hh + 1]) for hh in range(len(heads))]
            for a in range(nq):
                upd = jnp.zeros((8, BLK), F32)
                for hh, (_, _, h) in enumerate(heads):
                    upd = upd + jnp.where(sub8 == h, lse[hh][:, a * BLK:(a + 1) * BLK], 0.0)
                lse_ref[qblk + a] = lse_ref[qblk + a] + upd

        attend(0, 1, 0)

        def q_loop(u, carry):
            attend(1 + 2 * u, 2, u)
            return carry

        lax.fori_loop(0, nu, q_loop, 0)

        if ng:
            @pl.when(p == steps - 1)
            def _():
                local, sends, recvs = _allgather_copies(g_in, g_out, *rest[2 * ng + 4:])
                for cp in recvs:
                    cp.wait_recv()
                for cp in sends:
                    cp.wait_send()
                for cp in local:
                    cp.wait()

    width = 384 * FOX_PAIRS
    return pl.pallas_call(
        body, name="f_fox", grid=(steps,),
        in_specs=[pl.BlockSpec((L, width), lambda p: (0, RET_W // width + p)), _full((L, BLK)), _full((nblk, 8, BLK))]
        + [_ANY] * ng,
        out_specs=[pl.BlockSpec((L, FOX_PAIRS * BLK), lambda p: (0, p)), _full((nblk, 8, BLK))] + [_ANY] * ng,
        out_shape=[jax.ShapeDtypeStruct((L, 512), BF16), jax.ShapeDtypeStruct((nblk, 8, BLK), F32)]
        + [jax.ShapeDtypeStruct((N_CHIPS,) + a.shape, a.dtype) for a in gather],
        scratch_shapes=[pltpu.VMEM((FOX_PAIRS, nblk, BLK, BLK), BF16), pltpu.VMEM((2 * FOX_PAIRS, nblk, BLK, BLK), F32)]
        + _allgather_semaphores(ng),
        compiler_params=_params(("arbitrary",)),
    )(proj, c, ctb, *gather)


def _outproj_up(mix_r, o_f, h0, w_out, ffn_g, w_up, conv_w, conv_b):
    L = h0.shape[0]
    tm = _row_tile(L)
    shard = w_up.shape[2]
    assert 2 * shard == D_FF
    cw = [conv_w[j:j + 1] for j in range(3)]
    resident = lambda shape: pl.BlockSpec(shape, lambda i: (0,) * len(shape), pipeline_mode=pl.Buffered(1))

    def body(mr_ref, of_ref, h0_ref, wo_ref, g_ref, wu_ref, cw0, cw1, cw2, cb_ref,
             h1_ref, n2_ref, up_ref, act_ref, halo):
        i = pl.program_id(0)

        @pl.when(i == 0)
        def _():
            halo[...] = jnp.zeros_like(halo)

        h1 = h0_ref[...] + _dot(mr_ref[...], wo_ref[0:512, :]) + _dot(of_ref[...], wo_ref[512:1024, :])
        h1_ref[...] = h1
        r = lax.rsqrt(jnp.mean(h1 * h1, axis=-1, keepdims=True) + EPS)
        n2 = (h1 * r * g_ref[...]).astype(BF16)
        n2_ref[...] = n2
        live = i * tm + _iota((tm, 1), 0) >= N_PAD
        for half in range(2):
            cols = slice(half * shard, (half + 1) * shard)
            a_b = _dot(n2, wu_ref[half]).astype(BF16)
            b_b = _dot(n2, wu_ref[2 + half]).astype(BF16)
            up_ref[:, cols] = a_b
            up_ref[:, D_FF + half * shard:D_FF + (half + 1) * shard] = b_b
            a = jnp.where(live, a_b.astype(F32), 0.0)
            _, _, acc = _conv_taps(a, halo[:, cols], [cw0[:, cols], cw1[:, cols], cw2[:, cols]], cb_ref[:, cols])
            act_ref[:, cols] = (acc * jax.nn.sigmoid(acc) * b_b.astype(F32)).astype(BF16)
            halo[:, cols] = a[tm - 8:tm, :]

    rows = lambda w: pl.BlockSpec((tm, w), lambda i: (i, 0))
    return pl.pallas_call(
        body, name="f_outproj_up", grid=(L // tm,),
        in_specs=[rows(512), rows(512), rows(D_MODEL), resident((D_MODEL, D_MODEL)), _full((1, D_MODEL)),
                  resident((N_CHIPS, D_MODEL, shard)), _full((1, D_FF)), _full((1, D_FF)), _full((1, D_FF)),
                  _full((1, D_FF))],
        out_specs=[rows(D_MODEL), rows(D_MODEL), rows(2 * D_FF), rows(D_FF)],
        out_shape=[jax.ShapeDtypeStruct((L, D_MODEL), F32), jax.ShapeDtypeStruct((L, D_MODEL), BF16),
                   jax.ShapeDtypeStruct((L, 2 * D_FF), BF16), jax.ShapeDtypeStruct((L, D_FF), BF16)],
        scratch_shapes=[pltpu.VMEM((8, D_FF), F32)],
        compiler_params=_params(("arbitrary",)),
    )(mix_r, o_f, h0, w_out, ffn_g, w_up, cw[0], cw[1], cw[2], conv_b)


def _conv_taps(a, halo, cw, cb):
    sub = _iota((a.shape[0], 1), 0)
    a1 = jnp.where(sub == 0, _pick_row(halo, 7), pltpu.roll(a, 1, 0))
    a2 = jnp.where(sub == 0, _pick_row(halo, 6), jnp.where(sub == 1, _pick_row(halo, 7), pltpu.roll(a, 2, 0)))
    acc = cb + a2 * cw[0]
    acc = acc + a1 * cw[1]
    acc = acc + a * cw[2]
    return a1, a2, acc


def _conv_acc(a_ref, halo_ref, cw_refs, cb_ref, i, tm):
    sub = _iota((tm, 1), 0)
    a = jnp.where(i * tm + sub >= N_PAD, a_ref[...].astype(F32), 0.0)
    hrow = i * tm - 8 + _iota((8, 1), 0)
    halo = jnp.where((hrow >= N_PAD) & (i > 0), halo_ref[...].astype(F32), 0.0)
    a1, a2, acc = _conv_taps(a, halo, [r[...] for r in cw_refs], cb_ref[...])
    return a, a1, a2, acc


def _ffn_down_loss(g_act, w_down, h1, final_g, target):
    L = h1.shape[0]
    tm = _row_tile(L)

    def body(g_ref, wd_ref, h1_ref, gf_ref, t_ref, dh_ref, dhb_ref, dgf_ref, loss_ref):
        i = pl.program_id(0)

        @pl.when(i == 0)
        def _():
            dgf_ref[...] = jnp.zeros_like(dgf_ref)
            loss_ref[...] = jnp.zeros_like(loss_ref)

        h2 = h1_ref[...] + _dot(g_ref[...], wd_ref[...])
        r = lax.rsqrt(jnp.mean(h2 * h2, axis=-1, keepdims=True) + EPS)
        yn = h2 * r
        gf = gf_ref[...]
        live = i * tm + _iota((tm, 1), 0) >= PREFIX
        err = jnp.where(live, yn * gf - t_ref[...], 0.0)
        loss_ref[...] = loss_ref[...] + 0.5 * jnp.sum(jnp.mean(err * err, axis=-1, keepdims=True))
        dy = err * (1.0 / D_MODEL)
        dgf_ref[...] = dgf_ref[...] + jnp.sum(dy * yn, axis=0, keepdims=True)
        dyn = dy * gf
        dh = r * (dyn - yn * jnp.mean(dyn * yn, axis=-1, keepdims=True))
        dh_ref[...] = dh
        dhb_ref[...] = dh.astype(BF16)

    rows = lambda w: pl.BlockSpec((tm, w), lambda i: (i, 0))
    return pl.pallas_call(
        body, name="f_ffn_down_loss", grid=(L // tm,),
        in_specs=[rows(D_FF), _full((D_FF, D_MODEL)), rows(D_MODEL), _full((1, D_MODEL)), rows(D_MODEL)],
        out_specs=[rows(D_MODEL), rows(D_MODEL), _full((1, D_MODEL)), _full((1, BLK))],
        out_shape=[jax.ShapeDtypeStruct((L, D_MODEL), F32), jax.ShapeDtypeStruct((L, D_MODEL), BF16),
                   jax.ShapeDtypeStruct((1, D_MODEL), F32), jax.ShapeDtypeStruct((1, BLK), F32)],
        compiler_params=_params(("arbitrary",)),
    )(g_act, w_down, h1, final_g, target)


def _ffn_bwd_gate(dh2b, w_down, up, conv_w, conv_b):
    L = dh2b.shape[0]
    tm = _row_tile(L)
    cw = [conv_w[j:j + 1] for j in range(3)]

    def body(dh_ref, wd_ref, a_ref, halo_ref, b_ref, cw0, cw1, cw2, cb_ref, dacc_ref, db_ref, dcw_ref):
        i = pl.program_id(0)

        @pl.when(i == 0)
        def _():
            dcw_ref[...] = jnp.zeros_like(dcw_ref)

        a, a1, a2, acc = _conv_acc(a_ref, halo_ref, (cw0, cw1, cw2), cb_ref, i, tm)
        dg = _dot_nt(dh_ref[...], wd_ref[...])
        sg = jax.nn.sigmoid(acc)
        db_ref[...] = (dg * acc * sg).astype(BF16)
        dacc = dg * b_ref[...].astype(F32) * (sg * (1.0 + acc * (1.0 - sg)))
        dacc_ref[...] = dacc.astype(BF16)
        sub8 = _iota((8, 1), 0)
        rows = [jnp.sum(dacc * t, axis=0, keepdims=True) for t in (a2, a1, a)] + [jnp.sum(dacc, axis=0, keepdims=True)]
        upd = jnp.zeros((8, D_FF), F32)
        for j, rj in enumerate(rows):
            upd = upd + jnp.where(sub8 == j, rj, 0.0)
        dcw_ref[...] = dcw_ref[...] + upd

    rows = lambda w, c=0: pl.BlockSpec((tm, w), lambda i: (i, c))
    halo = pl.BlockSpec((8, D_FF), lambda i: (jnp.maximum(i * (tm // 8) - 1, 0), 0))
    return pl.pallas_call(
        body, name="b_ffn_gate", grid=(L // tm,),
        in_specs=[rows(D_MODEL), _full((D_FF, D_MODEL)), rows(D_FF), halo, rows(D_FF, 1),
                  _full((1, D_FF)), _full((1, D_FF)), _full((1, D_FF)), _full((1, D_FF))],
        out_specs=[rows(D_FF), rows(D_FF), _full((8, D_FF))],
        out_shape=[jax.ShapeDtypeStruct((L, D_FF), BF16), jax.ShapeDtypeStruct((L, D_FF), BF16),
                   jax.ShapeDtypeStruct((8, D_FF), F32)],
        compiler_params=_params(("arbitrary",)),
    )(dh2b, w_down, up, up, up, cw[0], cw[1], cw[2], conv_b)


def _ffn_bwd_up(dacc, db, conv_w, w_up, h1, ffn_g, dh2, w_out):
    L = h1.shape[0]
    tm = _row_tile(L)
    nt = L // tm
    shard = w_up.shape[2]
    cw = [conv_w[j:j + 1] for j in range(3)]

    def body(da_ref, halo_ref, db_ref, cw0, cw1, cw2, wu_ref, h1_ref, g_ref, dh2_ref, wo_ref,
             dup_ref, dh1_ref, dh1b_ref, dmix_ref, dg_ref):
        i = pl.program_id(0)

        @pl.when(i == 0)
        def _():
            dg_ref[...] = jnp.zeros_like(dg_ref)

        sub = _iota((tm, 1), 0)
        d0 = da_ref[...].astype(F32)
        halo = jnp.where(i < nt - 1, halo_ref[...].astype(F32), 0.0)
        d1 = jnp.where(sub == tm - 1, _pick_row(halo, 0), pltpu.roll(d0, tm - 1, 0))
        d2 = jnp.where(sub == tm - 2, _pick_row(halo, 0),
                       jnp.where(sub == tm - 1, _pick_row(halo, 1), pltpu.roll(d0, tm - 2, 0)))
        da = d0 * cw2[...] + d1 * cw1[...] + d2 * cw0[...]
        da = jnp.where(i * tm + sub >= N_PAD, da, 0.0).astype(BF16)
        dup_ref[:, 0:D_FF] = da
        dbv = db_ref[...]
        dup_ref[:, D_FF:2 * D_FF] = dbv
        dn = jnp.zeros((tm, D_MODEL), F32)
        for j in range(N_CHIPS):
            src = da if j < 2 else dbv
            lo = (j % 2) * shard
            dn = dn + _dot_nt(src[:, lo:lo + shard], wu_ref[j])
        h1 = h1_ref[...]
        r = lax.rsqrt(jnp.mean(h1 * h1, axis=-1, keepdims=True) + EPS)
        yn = h1 * r
        dg_ref[...] = dg_ref[...] + jnp.sum(dn * yn, axis=0, keepdims=True)
        dyn = dn * g_ref[...]
        dh1 = dh2_ref[...] + r * (dyn - yn * jnp.mean(dyn * yn, axis=-1, keepdims=True))
        dh1_ref[...] = dh1
        dh1b = dh1.astype(BF16)
        dh1b_ref[...] = dh1b
        dmix_ref[...] = _dot_nt(dh1b, wo_ref[...]).astype(BF16)

    rows = lambda w: pl.BlockSpec((tm, w), lambda i: (i, 0))
    halo = pl.BlockSpec((8, D_FF), lambda i: (jnp.minimum((i + 1) * (tm // 8), L // 8 - 1), 0))
    return pl.pallas_call(
        body, name="b_ffn_up", grid=(nt,),
        in_specs=[rows(D_FF), halo, rows(D_FF), _full((1, D_FF)), _full((1, D_FF)), _full((1, D_FF)),
                  _full((N_CHIPS, D_MODEL, shard)), rows(D_MODEL), _full((1, D_MODEL)), rows(D_MODEL),
                  _full((D_MODEL, D_MODEL))],
        out_specs=[rows(2 * D_FF), rows(D_MODEL), rows(D_MODEL), rows(D_MODEL), _full((1, D_MODEL))],
        out_shape=[jax.ShapeDtypeStruct((L, 2 * D_FF), BF16), jax.ShapeDtypeStruct((L, D_MODEL), F32),
                   jax.ShapeDtypeStruct((L, D_MODEL), BF16), jax.ShapeDtypeStruct((L, D_MODEL), BF16),
                   jax.ShapeDtypeStruct((1, D_MODEL), F32)],
        compiler_params=_params(("arbitrary",)),
    )(dacc, dacc, db, cw[0], cw[1], cw[2], w_up, h1, ffn_g, dh2, w_out)


def _wgrad(a, b, name, tn=None, tk=None):
    L, K = a.shape
    N = b.shape[1]
    tn = N if tn is None else tn
    tk = K if tk is None else tk
    tl = _row_tile(L, (1408, 768, 512, 256, 128))

    def body(a_ref, b_ref, o_ref):
        @pl.when(pl.program_id(2) == 0)
        def _():
            o_ref[...] = jnp.zeros_like(o_ref)

        o_ref[0] = o_ref[0] + _dot_tn(a_ref[...], b_ref[...])

    return pl.pallas_call(
        body, name=name, grid=(N // tn, K // tk, L // tl),
        in_specs=[pl.BlockSpec((tl, tk), lambda n, k, l: (l, k)), pl.BlockSpec((tl, tn), lambda n, k, l: (l, n))],
        out_specs=pl.BlockSpec((1, tk, tn), lambda n, k, l: (n, k, 0)),
        out_shape=jax.ShapeDtypeStruct((N // tn, K, tn), F32),
        compiler_params=_params(("parallel", "parallel", "arbitrary")),
    )(a, b)


def _retention_bwd(dmix, o, proj, cos_t, sin_t, ret_g, states, exchange=()):
    L = proj.shape[0]
    nblk = L // BLK
    G = _block_group(nblk)
    steps = nblk // G
    nx = len(exchange)
    dmat, wq_t, wk_t, g_blk = _decay_tables()

    def body(dm_ref, o_ref, q_ref, k_ref, v_ref, gate_ref, cos_ref, sin_ref, d_ref, wq_ref, wk_ref, rg_ref, rs_ref,
             *rest):
        x_in, (dp_ref, drg_ref), x_out, gstate = rest[:nx], rest[nx:nx + 2], rest[nx + 2:2 * nx + 2], rest[2 * nx + 2]

        @pl.when(pl.program_id(0) == 0)
        def _():
            if nx:
                for cp in _sibling_half_copies(x_in, x_out, *rest[2 * nx + 3:])[0]:
                    cp.start()
            gstate[...] = jnp.zeros_like(gstate)
            drg_ref[...] = jnp.zeros_like(drg_ref)

        lane = _iota((BLK, BLK), 1)
        sub = _iota((BLK, BLK), 0)
        scale = HEAD_LANES ** -0.5
        for b in reversed(range(G)):
            rows = slice(b * BLK, (b + 1) * BLK)
            rot, rot_t = _rot_fns(cos_ref[rows, :], sin_ref[rows, :])
            for p in range(2):
                qr = rot(q_ref[rows, p * BLK:(p + 1) * BLK].astype(F32))
                kr = rot(k_ref[rows, p * BLK:(p + 1) * BLK].astype(F32)) * scale
                kr_b = kr.astype(BF16)
                qw = (qr * wq_ref[p]).astype(BF16)
                kw = (kr * wk_ref[p]).astype(BF16)
                dqr = jnp.zeros((BLK, BLK), F32)
                dkr = jnp.zeros((BLK, BLK), F32)
                for e in range(2):
                    h = 2 * p + e
                    cols = slice(h * BLK, (h + 1) * BLK)
                    head_lanes = (lane >> 6) == e
                    o = o_ref[rows, cols]
                    rn = lax.rsqrt(jnp.mean(o * o, axis=-1, keepdims=True) + EPS)
                    y = o * rn
                    gate = gate_ref[rows, cols].astype(F32)
                    sg = jax.nn.sigmoid(gate)
                    dm = dm_ref[rows, cols].astype(F32)
                    rgain = rg_ref[:, cols]
                    drg_ref[:, cols] = drg_ref[:, cols] + jnp.sum(dm * y * (gate * sg), axis=0, keepdims=True)
                    dp_ref[rows, 1024 + h * BLK:1024 + (h + 1) * BLK] = (
                        dm * y * rgain * (sg * (1.0 + gate * (1.0 - sg)))).astype(BF16)
                    dy = dm * rgain * (gate * sg)
                    do = (rn * (dy - y * jnp.mean(dy * y, axis=-1, keepdims=True))).astype(BF16)
                    vh = v_ref[rows, cols]
                    qm = jnp.where(head_lanes, qr, 0.0).astype(BF16)
                    dmh = d_ref[h]
                    s = (_dot_nt(qm, kr_b) * dmh).astype(BF16)
                    ds = (_dot_nt(do, vh) * dmh).astype(BF16)
                    st = rs_ref[b, h].astype(BF16)
                    gs = gstate[h]
                    gs_b = gs.astype(BF16)
                    dqr = dqr + jnp.where(head_lanes, _dot(ds, kr_b), 0.0) + _dot_nt(do, st) * wq_ref[p]
                    dkr = dkr + _dot_tn(ds, qm) + _dot_nt(vh, gs_b) * wk_ref[p]
                    dp_ref[rows, 512 + h * BLK:512 + (h + 1) * BLK] = (_dot_tn(s, do) + _dot(kw, gs_b)).astype(BF16)
                    dr = jnp.where((sub >> 6) == e, _dot_tn(qw, do), 0.0)
                    gstate[h] = dr + g_blk[h] * gs
                dp_ref[rows, p * BLK:(p + 1) * BLK] = rot_t(dqr).astype(BF16)
                dp_ref[rows, 256 + p * BLK:256 + (p + 1) * BLK] = (rot_t(dkr) * scale).astype(BF16)

        if nx:
            @pl.when(pl.program_id(0) == steps - 1)
            def _():
                sends, recvs = _sibling_half_copies(x_in, x_out, *rest[2 * nx + 3:])
                for cp in recvs:
                    cp.wait_recv()
                for cp in sends:
                    cp.wait_send()

    row = lambda c: (lambda i: (steps - 1 - i, c))
    return pl.pallas_call(
        body, name="b_retention", grid=(steps,),
        in_specs=[pl.BlockSpec((G * BLK, 512), row(0)), pl.BlockSpec((G * BLK, 512), row(0)),
                  pl.BlockSpec((G * BLK, 256), row(0)), pl.BlockSpec((G * BLK, 256), row(1)),
                  pl.BlockSpec((G * BLK, 512), row(1)), pl.BlockSpec((G * BLK, 512), row(2)),
                  pl.BlockSpec((G * BLK, BLK), row(0)), pl.BlockSpec((G * BLK, BLK), row(0)),
                  _full((RET_HEADS, BLK, BLK)), _full((2, BLK, BLK)), _full((2, BLK, BLK)), _full((1, 512)),
                  pl.BlockSpec((G, RET_HEADS, BLK, BLK), lambda i: (steps - 1 - i, 0, 0, 0))] + [_ANY] * nx,
        out_specs=[pl.BlockSpec((G * BLK, RET_W), row(0)), _full((1, 512))] + [_ANY] * nx,
        out_shape=[jax.ShapeDtypeStruct((L, RET_W), BF16), jax.ShapeDtypeStruct((1, 512), F32)]
        + _sibling_half_shapes(exchange),
        scratch_shapes=[pltpu.VMEM((RET_HEADS, BLK, BLK), F32)] + _sibling_half_semaphores(nx),
        compiler_params=_params(("arbitrary",)),
    )(dmix, o, proj, proj, proj, proj, cos_t, sin_t, dmat, wq_t, wk_t, ret_g, states, *exchange)


def _fox_delta(dmix, o_f):
    L = o_f.shape[0]
    nblk = L // BLK
    G = _block_group(nblk)

    def body(do_ref, o_ref, d_ref):
        sel = ((_iota((8, 512), 1) >> 6) == _iota((8, 512), 0)).astype(BF16)
        for b in range(G):
            rows = slice(b * BLK, (b + 1) * BLK)
            prod = do_ref[rows, :].astype(F32) * o_ref[rows, :].astype(F32)
            hi = prod.astype(BF16)
            lo = (prod - hi.astype(F32)).astype(BF16)
            d_ref[b] = _dot_nt(sel, hi) + _dot_nt(sel, lo)

    return pl.pallas_call(
        body, name="b_foxdelta", grid=(nblk // G,),
        in_specs=[pl.BlockSpec((G * BLK, 512), lambda i: (i, 1)), pl.BlockSpec((G * BLK, 512), lambda i: (i, 0))],
        out_specs=pl.BlockSpec((G, 8, BLK), lambda i: (i, 0, 0)),
        out_shape=jax.ShapeDtypeStruct((nblk, 8, BLK), F32),
        compiler_params=_params(("parallel",)),
    )(dmix, o_f)


def _fox_bwd(proj, dmix, c, ctb, lse, delta, scatter=()):
    L = proj.shape[0]
    nblk, nu = _fox_units(L)
    scale = HEAD_LANES ** -0.5
    ns = len(scatter)

    steps = FOX_HEADS // (2 * FOX_PAIRS)

    def body(qkv_ref, do_ref, c_ref, ct_ref, lse_ref, dl_ref, *rest):
        s_in, (dp_ref, dc_ref, dcq_ref), s_out = rest[:ns], rest[ns:ns + 3], rest[ns + 3:2 * ns + 3]
        ktt, dqt, dk_acc, dv_acc, dcs_acc = rest[2 * ns + 3:2 * ns + 8]
        p = pl.program_id(0)
        heads = [(pp, e, 2 * FOX_PAIRS * p + 2 * pp + e) for pp in range(FOX_PAIRS) for e in range(2)]

        @pl.when(p == 0)
        def _():
            dc_ref[...] = jnp.zeros_like(dc_ref)
            dcq_ref[...] = jnp.zeros_like(dcq_ref)
            if ns:
                for cp in _scatter_copies(s_in, s_out, *rest[2 * ns + 8:]):
                    cp.start()

        sub8 = _iota((8, BLK), 0)
        masks = _fox_tile_masks()

        def pre(j, carry):
            off = pl.multiple_of(j * BLK, BLK)
            for pp in range(FOX_PAIRS):
                ktt[pp, j] = qkv_ref[pl.ds(off, BLK), pp * 384 + BLK:pp * 384 + 2 * BLK].astype(F32).T.astype(BF16)
                dqt[pp, j] = jnp.zeros((BLK, BLK), F32)
            return carry

        lax.fori_loop(0, nblk, pre, 0)

        def kv_pass(kblk, nk, n_later):
            klen = nk * BLK
            koff = pl.multiple_of(kblk * BLK, BLK)
            kt = [qkv_ref[pl.ds(koff, klen), pp * 384 + BLK:pp * 384 + 2 * BLK] for pp in range(FOX_PAIRS)]
            vtile = [qkv_ref[pl.ds(koff, klen), pp * 384 + 2 * BLK:pp * 384 + 3 * BLK] for pp in range(FOX_PAIRS)]
            ct = c_ref[pl.ds(koff, klen), :]
            klane = _iota((klen, BLK), 1)
            cs = [jnp.broadcast_to(jnp.sum(jnp.where(klane == h, ct, 0.0), axis=1, keepdims=True), (klen, WIDE * UNIT))
                  for _, _, h in heads]
            for pp in range(FOX_PAIRS):
                dk_acc[pp, 0:klen] = jnp.zeros((klen, BLK), F32)
                dv_acc[pp, 0:klen] = jnp.zeros((klen, BLK), F32)
            for hh in range(len(heads)):
                dcs_acc[hh, 0:klen] = jnp.zeros((klen, BLK), F32)

            def tile(qblk, nq, mask):
                qlen = nq * BLK
                if mask == "valid":
                    mask = _iota((klen, qlen), 0) >= N_PAD
                qoff = pl.multiple_of(qblk * BLK, BLK)
                qlane = _iota((qlen, BLK), 1)
                qs = [qkv_ref[pl.ds(qoff, qlen), pp * 384:pp * 384 + BLK].astype(F32) * scale for pp in range(FOX_PAIRS)]
                dot_ = [do_ref[pl.ds(qoff, qlen), pp * BLK:(pp + 1) * BLK] for pp in range(FOX_PAIRS)]
                stats = [[ref[qblk + a] for a in range(nq)] for ref in (ct_ref, lse_ref, dl_ref)]
                dcq = [jnp.zeros((8, BLK), F32) for _ in range(nq)]
                for hh, (pp, e, h) in enumerate(heads):
                    head = (qlane >> 6) == e
                    ct_row, lse_row, dl_row = [jnp.concatenate([_pick_row(t, h) for t in ts], axis=1) for ts in stats]
                    qm = jnp.where(head, qs[pp], 0.0).astype(BF16)
                    dom = jnp.where(head, dot_[pp], jnp.zeros_like(dot_[pp]))
                    t = _dot_nt(kt[pp], qm) - cs[hh][:, 0:qlen]
                    if mask is not None:
                        t = jnp.where(mask, t, NEG)
                    pr = jnp.exp(t + (ct_row - lse_row))
                    dv_acc[pp, 0:klen] = dv_acc[pp, 0:klen] + _dot(pr.astype(BF16), dom)
                    dsv = pr * (_dot_nt(vtile[pp], dom) - dl_row)
                    ds_b = dsv.astype(BF16)
                    dk_acc[pp, 0:klen] = dk_acc[pp, 0:klen] + _dot(ds_b, qm)
                    rows = slice(e * HEAD_LANES, (e + 1) * HEAD_LANES)
                    dq_t = _dot(ktt[pp, kblk, rows, :], ds_b[0:BLK])
                    for b in range(1, nk):
                        dq_t = dq_t + _dot(ktt[pp, kblk + b, rows, :], ds_b[b * BLK:(b + 1) * BLK])
                    key_side = dsv[:, 0:BLK]
                    for a in range(1, nq):
                        key_side = key_side + dsv[:, a * BLK:(a + 1) * BLK]
                    dcs_acc[hh, 0:klen] = dcs_acc[hh, 0:klen] + key_side
                    query_side = jnp.sum(dsv, axis=0, keepdims=True)
                    for a in range(nq):
                        cols = slice(a * BLK, (a + 1) * BLK)
                        dqt[pp, qblk + a, rows, :] = dqt[pp, qblk + a, rows, :] + dq_t[:, cols]
                        dcq[a] = dcq[a] + jnp.where(sub8 == h, query_side[:, cols], 0.0)
                for a in range(nq):
                    dcq_ref[qblk + a] = dcq_ref[qblk + a] + dcq[a]

            later_mask = "valid" if nk == 1 else None
            n_later = jnp.asarray(n_later, jnp.int32)
            n_wide = n_later // WIDE

            def later_wide(i, carry):
                tile(kblk + nk + 2 * WIDE * i, 2 * WIDE, later_mask)
                return carry

            tile(kblk, nk, masks["first"] if nk == 1 else masks["diag"])
            lax.fori_loop(0, n_wide, later_wide, 0)
            rest_blk = kblk + nk + 2 * WIDE * n_wide

            @pl.when((n_later & 2) != 0)
            def _():
                tile(rest_blk, 4, later_mask)

            @pl.when((n_later & 1) != 0)
            def _():
                tile(rest_blk + 2 * (n_later & 2), 2, later_mask)

            upd = jnp.zeros((klen, BLK), F32)
            for hh, (_, _, h) in enumerate(heads):
                upd = upd + jnp.where(klane == h, -jnp.sum(dcs_acc[hh, 0:klen], axis=1, keepdims=True), 0.0)
            dc_ref[pl.ds(koff, klen), :] = dc_ref[pl.ds(koff, klen), :] + upd
            for pp in range(FOX_PAIRS):
                dp_ref[pl.ds(koff, klen), pp * 384 + BLK:pp * 384 + 2 * BLK] = dk_acc[pp, 0:klen].astype(BF16)
                dp_ref[pl.ds(koff, klen), pp * 384 + 2 * BLK:pp * 384 + 3 * BLK] = dv_acc[pp, 0:klen].astype(BF16)

        kv_pass(0, 1, nu)

        def k_loop(u, carry):
            kv_pass(1 + 2 * u, 2, nu - 1 - u)
            return carry

        lax.fori_loop(0, nu, k_loop, 0)

        def flush(j, carry):
            off = pl.multiple_of(j * BLK, BLK)
            for pp in range(FOX_PAIRS):
                dp_ref[pl.ds(off, BLK), pp * 384:pp * 384 + BLK] = (dqt[pp, j].T * scale).astype(BF16)
            return carry

        lax.fori_loop(0, nblk, flush, 0)

        if ns:
            @pl.when(p == steps - 1)
            def _():
                copies = _scatter_copies(s_in, s_out, *rest[2 * ns + 8:])
                for cp in copies:
                    cp.wait_recv()
                for cp in copies:
                    cp.wait_send()

    width = 384 * FOX_PAIRS
    once = lambda shape, index: pl.BlockSpec(shape, index, pipeline_mode=pl.Buffered(1))
    stat = once((nblk, 8, BLK), lambda p: (0, 0, 0))
    return pl.pallas_call(
        body, name="b_fox", grid=(steps,),
        in_specs=[once((L, width), lambda p: (0, RET_W // width + p)),
                  once((L, FOX_PAIRS * BLK), lambda p: (0, 4 // FOX_PAIRS + p)),
                  once((L, BLK), lambda p: (0, 0)), stat, stat, stat] + [_ANY] * ns,
        out_specs=[pl.BlockSpec((L, width), lambda p: (0, p)), _full((L, BLK)), _full((nblk, 8, BLK))] + [_ANY] * ns,
        out_shape=[jax.ShapeDtypeStruct((L, FOX_W), BF16), jax.ShapeDtypeStruct((L, BLK), F32),
                   jax.ShapeDtypeStruct((nblk, 8, BLK), F32)] + _scatter_shapes(scatter),
        scratch_shapes=[pltpu.VMEM((FOX_PAIRS, nblk, BLK, BLK), BF16), pltpu.VMEM((FOX_PAIRS, nblk, BLK, BLK), F32),
                        pltpu.VMEM((FOX_PAIRS, UNIT, BLK), F32), pltpu.VMEM((FOX_PAIRS, UNIT, BLK), F32),
                        pltpu.VMEM((2 * FOX_PAIRS, UNIT, BLK), F32)]
        + _scatter_semaphores(ns),
        compiler_params=_params(("arbitrary",)),
    )(proj, dmix, c, ctb, lse, delta, *scatter)


def _fox_post(dc, dcq, ff, fb):
    L = dc.shape[0]
    nblk = L // BLK
    G = _block_group(nblk)
    steps = nblk // G

    def body(dc_ref, dcq_ref, ff_ref, b_ref, dff_ref, dffb_ref, dfb_ref, carry):
        @pl.when(pl.program_id(0) == 0)
        def _():
            carry[...] = jnp.zeros_like(carry)
            dfb_ref[...] = jnp.zeros_like(dfb_ref)

        tri = (_iota((BLK, BLK), 0) <= _iota((BLK, BLK), 1)).astype(BF16)
        live = _iota((BLK, BLK), 1) < FOX_HEADS
        run, dfb = carry[...], dfb_ref[...]
        for b in reversed(range(G)):
            rows = slice(b * BLK, (b + 1) * BLK)
            d = dc_ref[rows, :] + jnp.concatenate([dcq_ref[b], jnp.zeros((BLK - 8, BLK), F32)], axis=0).T
            hi, mid, lo = _split3(d)
            dlf = _dot(tri, hi) + _dot(tri, mid) + _dot(tri, lo) + run
            run = run + jnp.sum(d, axis=0, keepdims=True)
            z = ff_ref[rows, :] + b_ref[...]
            dff = jnp.where(live, dlf * jax.nn.sigmoid(-z), 0.0)
            dff_ref[rows, :] = dff
            dffb_ref[rows, :] = dff.astype(BF16)
            dfb = dfb + jnp.sum(dff, axis=0, keepdims=True)
        carry[...] = run
        dfb_ref[...] = dfb

    rev = lambda i: (steps - 1 - i, 0)
    return pl.pallas_call(
        body, name="b_foxpost", grid=(steps,),
        in_specs=[pl.BlockSpec((G * BLK, BLK), rev), pl.BlockSpec((G, 8, BLK), lambda i: (steps - 1 - i, 0, 0)),
                  pl.BlockSpec((G * BLK, BLK), rev), _full((1, BLK))],
        out_specs=[pl.BlockSpec((G * BLK, BLK), rev), pl.BlockSpec((G * BLK, BLK), rev), _full((1, BLK))],
        out_shape=[jax.ShapeDtypeStruct((L, BLK), F32), jax.ShapeDtypeStruct((L, BLK), BF16),
                   jax.ShapeDtypeStruct((1, BLK), F32)],
        scratch_shapes=[pltpu.VMEM((1, BLK), F32)],
        compiler_params=_params(("arbitrary",)),
    )(dc, dcq, ff, fb)


def _inproj_bwd(dpr, dpf, dffb, w_main, w_ff, h0, g, dh1):
    L = h0.shape[0]
    tm = _row_tile(L)

    def body(dpr_ref, dpf_ref, dff_ref, wm_ref, wf_ref, h_ref, g_ref, dh1_ref, dh0_ref, dg_ref):
        @pl.when(pl.program_id(0) == 0)
        def _():
            dg_ref[...] = jnp.zeros_like(dg_ref)

        dn = (_dot_nt(dpr_ref[...], wm_ref[:, 0:RET_W]) + _dot_nt(dpf_ref[...], wm_ref[:, RET_W:MAIN_W])
              + _dot_nt(dff_ref[...], wf_ref[...]))
        h = h_ref[...]
        r = lax.rsqrt(jnp.mean(h * h, axis=-1, keepdims=True) + EPS)
        yn = h * r
        dg_ref[...] = dg_ref[...] + jnp.sum(dn * yn, axis=0, keepdims=True)
        dyn = dn * g_ref[...]
        dh0_ref[...] = dh1_ref[...] + r * (dyn - yn * jnp.mean(dyn * yn, axis=-1, keepdims=True))

    rows = lambda w: pl.BlockSpec((tm, w), lambda i: (i, 0))
    return pl.pallas_call(
        body, name="b_inproj", grid=(L // tm,),
        in_specs=[rows(RET_W), rows(FOX_W), rows(BLK), _full((D_MODEL, MAIN_W)), _full((D_MODEL, BLK)),
                  rows(D_MODEL), _full((1, D_MODEL)), rows(D_MODEL)],
        out_specs=[rows(D_MODEL), _full((1, D_MODEL))],
        out_shape=[jax.ShapeDtypeStruct((L, D_MODEL), F32), jax.ShapeDtypeStruct((1, D_MODEL), F32)],
        compiler_params=_params(("arbitrary",)),
    )(dpr, dpf, dffb, w_main, w_ff, h0, g, dh1)


def _local_step(x, target, meta, attn_g, w_main, w_ff, fox_b, ret_g, w_out, ffn_g, w_up, conv_w, conv_b, w_down, final_g,
                late=None, mid=None):
    S = x.shape[0]
    L = S + PREFIX
    h0 = jnp.concatenate([jnp.zeros((N_PAD, D_MODEL), F32), meta, x], axis=0)
    tgt = jnp.concatenate([jnp.zeros((PREFIX, D_MODEL), F32), target], axis=0)
    fb = jnp.pad(fox_b, ((0, 0), (0, BLK - FOX_HEADS)))
    cos_t, sin_t = _rotary_tables(L)

    n1, proj, ff = _rms_inproj(h0, attn_g, w_main, w_ff)
    c, ctb = _fox_prep(ff, fb)
    mix_r, o_ret, states = _retention_fwd(proj, cos_t, sin_t, ret_g)
    if late is None:
        o_f, lse = _fox_fwd(proj, c, ctb)
    else:
        o_f, lse, *gathered = _fox_fwd(proj, c, ctb, gather=late[0])
        w_out, w_up, w_down = late[1](gathered)
    h1, n2, up, g_act = _outproj_up(mix_r, o_f, h0, w_out, ffn_g, w_up, conv_w, conv_b)
    dh2, dh2b, d_final_g, loss = _ffn_down_loss(g_act, w_down, h1, final_g, tgt)

    dacc, db, dconv = _ffn_bwd_gate(dh2b, w_down, up, conv_w, conv_b)
    dup, dh1, dh1b, dmix, d_ffn_g = _ffn_bwd_up(dacc, db, conv_w, w_up, h1, ffn_g, dh2, w_out)
    d_w_down = _wgrad(g_act, dh2b, "wgrad_down", tk=D_FF // 2)[0]
    d_w_up = _wgrad(n2, dup, "wgrad_up", tn=w_up.shape[2])
    d_w_out = jnp.concatenate([_wgrad(mix_r, dh1b, "wgrad_out_r")[0], _wgrad(o_f, dh1b, "wgrad_out_f")[0]], axis=0)

    early = () if mid is None else mid[0](d_w_out, d_w_up, d_w_down)
    dpr, d_ret_g, *from_sibling = _retention_bwd(dmix, o_ret, proj, cos_t, sin_t, ret_g, states, exchange=early)
    delta = _fox_delta(dmix, o_f)
    scatter = () if mid is None else mid[1](early, from_sibling)
    dpf, dc, dcq, *received = _fox_bwd(proj, dmix, c, ctb, lse, delta, scatter=scatter)
    dff, dffb, d_fox_b = _fox_post(dc, dcq, ff, fb)
    dh0, d_attn_g = _inproj_bwd(dpr, dpf, dffb, w_main, w_ff, h0, attn_g, dh1)
    d_w_main = jnp.concatenate([_wgrad(n1, dpr, "wgrad_in_r")[0], _wgrad(n1, dpf, "wgrad_in_f")[0]], axis=1)
    d_w_ff = _wgrad(n1, dffb, "wgrad_in_ff")[0]

    return dict(
        loss=loss[0, 0], dx=dh0[PREFIX:], dmeta=dh0[N_PAD:PREFIX], attn_g=d_attn_g, w_main=d_w_main,
        w_ff=d_w_ff[:, :FOX_HEADS], fox_b=d_fox_b[:, :FOX_HEADS], ret_g=d_ret_g, w_out=d_w_out, ffn_g=d_ffn_g,
        w_up=d_w_up, conv_w=dconv[0:3], conv_b=dconv[3:4], w_down=d_w_down, final_g=d_final_g,
        scatter=scatter, received=received)


_ANY = pl.BlockSpec(memory_space=pl.ANY)


def _place():
    return lax.axis_index("x"), lax.axis_index("y"), lax.axis_index("c")


def _other_chips(x, y):
    return [(1 - x, y), (x, 1 - y), (1 - x, 1 - y)]


def _allgather_semaphores(n):
    if n == 0:
        return []
    return [pltpu.SemaphoreType.DMA((3 * n,)), pltpu.SemaphoreType.DMA((3 * n,)), pltpu.SemaphoreType.DMA((n,))]


def _allgather_copies(ins, outs, send, recv, loc):
    n = len(ins)
    x, y, c = _place()
    mine = 2 * x + y
    peers = _other_chips(x, y)

    def remote(a, k, slot):
        return pltpu.make_async_remote_copy(
            src_ref=ins[a], dst_ref=outs[a].at[slot], send_sem=send.at[3 * a + k], recv_sem=recv.at[3 * a + k],
            device_id=(peers[k][0], peers[k][1], c), device_id_type=MESH)

    local = [pltpu.make_async_copy(ins[a], outs[a].at[mine], loc.at[a]) for a in range(n)]
    sends = [remote(a, k, mine) for a in range(n) for k in range(3)]
    recvs = [remote(a, k, 2 * peers[k][0] + peers[k][1]) for a in range(n) for k in range(3)]
    return local, sends, recvs


def _chip_allgather_halves(w, small):
    half = w.shape[0] // 2

    def body(w_ref, s_ref, wo_ref, so_ref, send, recv, fsend, frecv, ssend, srecv, loc):
        x, y, c = _place()
        mine = 2 * x + y
        peers = _other_chips(x, y)

        def fetch(k, slot):
            return pltpu.make_async_remote_copy(
                src_ref=w_ref.at[pl.ds(c * half, half)], dst_ref=wo_ref.at[slot, c], send_sem=send.at[k],
                recv_sem=recv.at[k], device_id=(peers[k][0], peers[k][1], c), device_id_type=MESH)

        def forward(k, which):
            slot = 2 * peers[k][0] + peers[k][1]
            return pltpu.make_async_remote_copy(
                src_ref=wo_ref.at[slot, which], dst_ref=wo_ref.at[slot, which], send_sem=fsend.at[k],
                recv_sem=frecv.at[k], device_id=(x, y, 1 - c), device_id_type=MESH)

        def small_copy(k, slot):
            return pltpu.make_async_remote_copy(
                src_ref=s_ref, dst_ref=so_ref.at[slot], send_sem=ssend.at[k], recv_sem=srecv.at[k],
                device_id=(peers[k][0], peers[k][1], c), device_id_type=MESH)

        local = pltpu.make_async_copy(s_ref, so_ref.at[mine], loc.at[0])
        sends = [fetch(k, mine) for k in range(3)] + [small_copy(k, mine) for k in range(3)]
        local.start()
        for cp in sends:
            cp.start()
        forwards = []
        for k in range(3):
            fetch(k, 2 * peers[k][0] + peers[k][1]).wait_recv()
            forwards.append(forward(k, c))
            forwards[-1].start()
        for k in range(3):
            forward(k, 1 - c).wait_recv()
            small_copy(k, 2 * peers[k][0] + peers[k][1]).wait_recv()
        for cp in sends + forwards:
            cp.wait_send()
        local.wait()

    three = pltpu.SemaphoreType.DMA((3,))
    return pl.pallas_call(
        body, name="ag_weights", in_specs=[_ANY] * 2, out_specs=[_ANY] * 2,
        out_shape=[jax.ShapeDtypeStruct((N_CHIPS, 2, half, w.shape[1]), w.dtype),
                   jax.ShapeDtypeStruct((N_CHIPS,) + small.shape, small.dtype)],
        scratch_shapes=[three, three, three, three, three, three, pltpu.SemaphoreType.DMA((1,))],
    )(w, small)


def _chip_allgather(arrays):
    n = len(arrays)

    def body(*refs):
        local, sends, recvs = _allgather_copies(refs[:n], refs[n:2 * n], *refs[2 * n:])
        for cp in local + sends:
            cp.start()
        for cp in recvs:
            cp.wait_recv()
        for cp in sends:
            cp.wait_send()
        for cp in local:
            cp.wait()

    return pl.pallas_call(
        body, name="ag_weights", in_specs=[_ANY] * n, out_specs=[_ANY] * n,
        out_shape=[jax.ShapeDtypeStruct((N_CHIPS,) + a.shape, a.dtype) for a in arrays],
        scratch_shapes=_allgather_semaphores(n),
    )(*arrays)


def _sibling_exchange(grads, small):
    n = len(grads)

    def body(*refs):
        ins, small_in = refs[:n], refs[n]
        outs, small_out = refs[n + 1:2 * n + 1], refs[2 * n + 1]
        send, recv, s_send, s_recv, loc = refs[2 * n + 2:]
        x, y, c = _place()
        me = 4 * x + 2 * y + c

        def half_copy(a, which):
            half = ins[a].shape[1] // 2
            return pltpu.make_async_remote_copy(
                src_ref=ins[a].at[pl.ds(0, N_CHIPS), pl.ds(which * half, half)], dst_ref=outs[a],
                send_sem=send.at[a], recv_sem=recv.at[a], device_id=(x, y, 1 - c), device_id_type=MESH)

        def peer_of(r):
            return tuple(1 - v if (r >> b) & 1 else v for v, b in ((x, 2), (y, 1), (c, 0)))

        def small_copy(r, slot):
            return pltpu.make_async_remote_copy(
                src_ref=small_in, dst_ref=small_out.at[slot], send_sem=s_send.at[r - 1], recv_sem=s_recv.at[r - 1],
                device_id=peer_of(r), device_id_type=MESH)

        local = pltpu.make_async_copy(small_in, small_out.at[me], loc.at[0])
        sends = [half_copy(a, 1 - c) for a in range(n)] + [small_copy(r, me) for r in range(1, N_DEV)]
        local.start()
        for cp in sends:
            cp.start()
        for r in range(1, N_DEV):
            px, py, pc = peer_of(r)
            small_copy(r, 4 * px + 2 * py + pc).wait_recv()
        for a in range(n):
            half_copy(a, c).wait_recv()
        for cp in sends:
            cp.wait_send()
        local.wait()

    rows = small.shape[0]
    return pl.pallas_call(
        body, name="rs_sibling", in_specs=[_ANY] * (n + 1), out_specs=[_ANY] * (n + 1),
        out_shape=[jax.ShapeDtypeStruct((N_CHIPS, g.shape[1] // 2, g.shape[2]), g.dtype) for g in grads]
        + [jax.ShapeDtypeStruct((N_DEV, rows, small.shape[1]), small.dtype)],
        scratch_shapes=[pltpu.SemaphoreType.DMA((n,)), pltpu.SemaphoreType.DMA((n,)),
                        pltpu.SemaphoreType.DMA((N_DEV - 1,)), pltpu.SemaphoreType.DMA((N_DEV - 1,)),
                        pltpu.SemaphoreType.DMA((1,))],
    )(*grads, small)


def _sibling_half_shapes(grads):
    return [jax.ShapeDtypeStruct((N_CHIPS, g.shape[1] // 2, g.shape[2]), g.dtype) for g in grads]


def _sibling_half_semaphores(n):
    return [pltpu.SemaphoreType.DMA((n,)), pltpu.SemaphoreType.DMA((n,))] if n else []


def _sibling_half_copies(ins, outs, send, recv):
    x, y, c = _place()

    def half_copy(a, which):
        half = ins[a].shape[1] // 2
        return pltpu.make_async_remote_copy(
            src_ref=ins[a].at[pl.ds(0, N_CHIPS), pl.ds(which * half, half)], dst_ref=outs[a],
            send_sem=send.at[a], recv_sem=recv.at[a], device_id=(x, y, 1 - c), device_id_type=MESH)

    return [half_copy(a, 1 - c) for a in range(len(ins))], [half_copy(a, c) for a in range(len(ins))]


def _chip_reduce_scatter(parts):
    n = len(parts)

    def body(*refs):
        copies = _scatter_copies(refs[:n], refs[n:2 * n], *refs[2 * n:])
        for cp in copies:
            cp.start()
        for cp in copies:
            cp.wait_recv()
        for cp in copies:
            cp.wait_send()

    return pl.pallas_call(
        body, name="rs_chips", in_specs=[_ANY] * n, out_specs=[_ANY] * n,
        out_shape=_scatter_shapes(parts), scratch_shapes=_scatter_semaphores(n),
    )(*parts)


def _scatter_shapes(parts):
    return [jax.ShapeDtypeStruct((3,) + p.shape[1:], p.dtype) for p in parts]


def _scatter_semaphores(n):
    return [pltpu.SemaphoreType.DMA((3 * n,)), pltpu.SemaphoreType.DMA((3 * n,))] if n else []


def _scatter_copies(ins, outs, send, recv):
    x, y, c = _place()
    peers = _other_chips(x, y)
    return [pltpu.make_async_remote_copy(
        src_ref=ins[a].at[2 * peers[k][0] + peers[k][1]], dst_ref=outs[a].at[k], send_sem=send.at[3 * a + k],
        recv_sem=recv.at[3 * a + k], device_id=(peers[k][0], peers[k][1], c), device_id_type=MESH)
        for a in range(len(ins)) for k in range(3)]


def _sibling_allgather(bufs):
    n = len(bufs)

    def body(*refs):
        outs = refs[n:2 * n]
        send, recv = refs[2 * n:]
        x, y, c = _place()

        def remote(a, which):
            return pltpu.make_async_remote_copy(
                src_ref=outs[a].at[which], dst_ref=outs[a].at[which], send_sem=send.at[a], recv_sem=recv.at[a],
                device_id=(x, y, 1 - c), device_id_type=MESH)

        sends = [remote(a, c) for a in range(n)]
        for cp in sends:
            cp.start()
        for a in range(n):
            remote(a, 1 - c).wait_recv()
        for cp in sends:
            cp.wait_send()

    outs = pl.pallas_call(
        body, name="ag_sibling", in_specs=[_ANY] * n, out_specs=[_ANY] * n,
        out_shape=[jax.ShapeDtypeStruct(b.shape, b.dtype) for b in bufs],
        input_output_aliases={a: a for a in range(n)},
        scratch_shapes=[pltpu.SemaphoreType.DMA((n,)), pltpu.SemaphoreType.DMA((n,))],
    )(*bufs)
    return [o.reshape(2 * o.shape[1], o.shape[2]) for o in outs]


def _pair_add(full, recv, core, name):
    _, R, C = full.shape
    half = R // 2

    def body(core_ref, a_ref, b_ref, o_ref):
        o_ref[...] = (a_ref[...] + b_ref[...]).astype(BF16)

    return pl.pallas_call(
        body, name=name,
        grid_spec=pltpu.PrefetchScalarGridSpec(
            num_scalar_prefetch=1, grid=(N_CHIPS,),
            in_specs=[pl.BlockSpec((1, half, C), lambda j, core_ref: (j, core_ref[0], 0)),
                      pl.BlockSpec((1, half, C), lambda j, core_ref: (j, 0, 0))],
            out_specs=pl.BlockSpec((1, half, C), lambda j, core_ref: (j, 0, 0))),
        out_shape=jax.ShapeDtypeStruct((N_CHIPS, half, C), BF16),
        compiler_params=_params(("parallel",)),
    )(core, full, recv)


def _sum_slots(q, name, tiles=2):
    n, R, C = q.shape
    tr = R // tiles

    def body(q_ref, o_ref):
        acc = q_ref[0].astype(F32)
        for j in range(1, n):
            acc = acc + q_ref[j].astype(F32)
        o_ref[...] = acc

    return pl.pallas_call(
        body, name=name, grid=(tiles,),
        in_specs=[pl.BlockSpec((n, tr, C), lambda i: (0, i, 0))],
        out_specs=pl.BlockSpec((tr, C), lambda i: (i, 0)),
        out_shape=jax.ShapeDtypeStruct((R, C), F32),
        compiler_params=_params(("parallel",)),
    )(q)


def _sum_partials(own_all, recv, place, name, tiles=2):
    _, R, C = own_all.shape
    tr = R // tiles

    def body(place_ref, own_ref, r_ref, o_ref):
        acc = own_ref[0].astype(F32)
        for k in range(3):
            acc = acc + r_ref[k].astype(F32)
        o_ref[0] = acc

    return pl.pallas_call(
        body, name=name,
        grid_spec=pltpu.PrefetchScalarGridSpec(
            num_scalar_prefetch=1, grid=(tiles,),
            in_specs=[pl.BlockSpec((1, tr, C), lambda i, place_ref: (place_ref[0], i, 0)),
                      pl.BlockSpec((3, tr, C), lambda i, place_ref: (0, i, 0))],
            out_specs=pl.BlockSpec((1, tr, C), lambda i, place_ref: (place_ref[1], i, 0))),
        out_shape=jax.ShapeDtypeStruct((2, R, C), F32),
        compiler_params=_params(("parallel",)),
    )(place, own_all, recv)


def _adamw(w, g, m, v, name, tiles=4):
    R, C = w.shape
    tr = R // tiles

    def body(w_ref, g_ref, m_ref, v_ref, go_ref, d_ref, m2_ref, v2_ref):
        g_ = g_ref[...]
        go_ref[...] = g_
        m2 = ADAM_B1 * m_ref[...] + (1.0 - ADAM_B1) * g_
        v2 = ADAM_B2 * v_ref[...] + (1.0 - ADAM_B2) * (g_ * g_)
        m_hat = m2 / (1.0 - ADAM_B1 ** ADAM_STEP)
        v_hat = v2 / (1.0 - ADAM_B2 ** ADAM_STEP)
        d_ref[...] = -ADAM_LR * (m_hat / (jnp.sqrt(v_hat) + ADAM_EPS) + ADAM_WD * w_ref[...])
        m2_ref[...] = m2
        v2_ref[...] = v2

    spec = pl.BlockSpec((tr, C), lambda i: (i, 0))
    return pl.pallas_call(
        body, name=name, grid=(tiles,), in_specs=[spec] * 4, out_specs=[spec] * 4,
        out_shape=[jax.ShapeDtypeStruct((R, C), F32)] * 4,
        compiler_params=_params(("parallel",)),
    )(w, g, m, v)


def _pack_rows(pieces, rows):
    flat = jnp.concatenate([jnp.pad(p.reshape(-1).astype(F32), (0, (-p.size) % D_MODEL)) for p in pieces])
    return jnp.pad(flat, (0, rows * D_MODEL - flat.size)).reshape(rows, D_MODEL)


def _unpack_rows(pack, shapes):
    flat = pack.reshape(-1)
    out, off = [], 0
    for shp in shapes:
        size = int(np.prod(shp))
        out.append(flat[off:off + size].reshape(shp))
        off += size + (-size) % D_MODEL
    return out


def _kernel_order(w):
    parts = [w[:, 0:RET_W]]
    for p in range(FOX_HEADS // 2):
        parts += [w[:, RET_W + part * 512 + p * BLK:RET_W + part * 512 + (p + 1) * BLK] for part in range(3)]
    return jnp.concatenate(parts, axis=1)


def _reference_order(g_main, g_ff):
    parts = [g_main[:, 0:RET_W]]
    for part in range(3):
        parts += [g_main[:, RET_W + 384 * p + part * BLK:RET_W + 384 * p + (part + 1) * BLK] for p in range(FOX_HEADS // 2)]
    return jnp.concatenate(parts + [g_ff], axis=1)


def kernel(x, meta_tokens, attn_norm_g, w_in, fox_forget_b, ret_norm_g, w_out, ffn_norm_g, w_up, conv_w, conv_b, w_down, final_norm_g, loss_target, m_meta_tokens, m_attn_norm_g, m_w_in, m_fox_forget_b, m_ret_norm_g, m_w_out, m_ffn_norm_g, m_w_up, m_conv_w, m_conv_b, m_w_down, m_final_norm_g, v_meta_tokens, v_attn_norm_g, v_w_in, v_fox_forget_b, v_ret_norm_g, v_w_out, v_ffn_norm_g, v_w_up, v_conv_w, v_conv_b, v_w_down, v_final_norm_g):
    chip = 2 * lax.axis_index("x") + lax.axis_index("y")
    core = lax.axis_index("c")
    meta_w, conv_sw = meta_tokens.shape[1], conv_w.shape[2]

    small_w = _pack_rows([meta_tokens, conv_w[0]], 8)
    w_in_b = w_in[0].astype(BF16)
    g_in, g_small = _chip_allgather_halves(w_in_b, small_w)
    g_in = lax.dynamic_update_slice(g_in.reshape((N_CHIPS,) + w_in_b.shape), w_in_b[None], (chip, 0, 0))
    w_in_full = g_in.transpose(1, 0, 2).reshape(D_MODEL, IN_WIDTH)
    w_main = _kernel_order(w_in_full)
    w_ff = jnp.pad(w_in_full[:, MAIN_W:], ((0, 0), (0, BLK - FOX_HEADS)))
    small_parts = [_unpack_rows(g_small[j], [meta_tokens.shape, conv_w.shape[1:]]) for j in range(N_CHIPS)]
    meta_full = jnp.concatenate([sp[0] for sp in small_parts], axis=1)
    conv_w_full = jnp.concatenate([sp[1] for sp in small_parts], axis=1)

    core_idx = core.reshape(1).astype(jnp.int32)
    place = jnp.stack([chip, core]).astype(jnp.int32)

    def assemble(gathered):
        g_out, g_up, g_down = gathered
        return g_out.reshape(D_MODEL, D_MODEL), g_up, g_down.reshape(D_FF, D_MODEL)

    def early_arrays(d_w_out, d_w_up, d_w_down):
        return [d_w_out.reshape(N_CHIPS, -1, D_MODEL), d_w_up, d_w_down.reshape(N_CHIPS, -1, D_MODEL)]

    def early_sums(early, from_sib):
        return [_pair_add(g, r, core_idx, "pair_add_" + nm) for g, r, nm in zip(early, from_sib, ("out", "up", "down"))]

    out = _local_step(x[0], loss_target[0], meta_full, attn_norm_g, w_main, w_ff, fox_forget_b, ret_norm_g,
                      None, ffn_norm_g, None, conv_w_full, conv_b, None, final_norm_g[None],
                      late=([w_out[0].astype(BF16), w_up[0].astype(BF16), w_down[0].astype(BF16)], assemble),
                      mid=(early_arrays, early_sums))

    g_in_full = _reference_order(out["w_main"], out["w_ff"]).reshape(D_MODEL, N_CHIPS, -1).transpose(1, 0, 2)
    small_shapes = [(1, D_MODEL), (1, D_MODEL), (1, D_MODEL), (1, 512 + FOX_HEADS + 1), (1, D_FF), (N_META, D_MODEL), (3, D_FF)]
    small = _pack_rows([out["attn_g"], out["ffn_g"], out["final_g"],
                        jnp.concatenate([out["ret_g"], out["fox_b"], out["loss"].reshape(1, 1)], axis=1),
                        out["conv_b"], out["dmeta"], out["conv_w"]], 32)
    from_sibling_in, small_all = _sibling_exchange([g_in_full], small)
    sum_in = _pair_add(g_in_full, from_sibling_in, core_idx, "pair_add_in")
    (from_chips_in,) = _chip_reduce_scatter([sum_in])
    chip_sums = [sum_in] + list(out["scatter"])
    from_chips = [from_chips_in] + list(out["received"])
    names = ("in", "out", "up", "down")
    totals = [_sum_partials(s, q, place, "sum_chips_" + nm) for s, q, nm in zip(chip_sums, from_chips, names)]
    grad_in, grad_out, grad_up, grad_down = _sibling_allgather(totals)
    s_attn, s_ffn, s_final, s_misc, s_conv_b, s_meta, s_conv_w = _unpack_rows(
        _sum_slots(small_all, "sum_small", tiles=1), small_shapes)
    loss = s_misc[0, 512 + FOX_HEADS]
    small_grads = [lax.dynamic_slice_in_dim(s_meta, chip * meta_w, meta_w, axis=1), s_attn, s_misc[:, 512:512 + FOX_HEADS],
                   s_misc[:, :512], s_ffn, lax.dynamic_slice_in_dim(s_conv_w, chip * conv_sw, conv_sw, axis=1)[None],
                   s_conv_b, s_final[0]]

    big_w = [(w_in, m_w_in, v_w_in, grad_in, "adamw_in"), (w_out, m_w_out, v_w_out, grad_out, "adamw_out"),
             (w_up, m_w_up, v_w_up, grad_up, "adamw_up"), (w_down, m_w_down, v_w_down, grad_down, "adamw_down")]
    big_res = [[r[None] for r in _adamw(w[0], g, m[0], v[0], nm)] for w, m, v, g, nm in big_w]
    small_w_list = [meta_tokens, attn_norm_g, fox_forget_b, ret_norm_g, ffn_norm_g, conv_w, conv_b, final_norm_g]
    small_m = [m_meta_tokens, m_attn_norm_g, m_fox_forget_b, m_ret_norm_g, m_ffn_norm_g, m_conv_w, m_conv_b, m_final_norm_g]
    small_v = [v_meta_tokens, v_attn_norm_g, v_fox_forget_b, v_ret_norm_g, v_ffn_norm_g, v_conv_w, v_conv_b, v_final_norm_g]
    shapes = [a.shape for a in small_w_list]
    packs = [_pack_rows(lst, 16) for lst in (small_w_list, small_grads, small_m, small_v)]
    small_res = [_unpack_rows(r, shapes) for r in _adamw(*packs, "adamw_small", tiles=1)[1:]]
    small_grads = [g.reshape(s) for g, s in zip(small_grads, shapes)]

    def ordered(kind):
        sm = small_grads if kind == 0 else small_res[kind - 1]
        bg = [r[kind] for r in big_res]
        return [sm[0], sm[1], bg[0], sm[2], sm[3], bg[1], sm[4], bg[2], sm[5], sm[6], bg[3], sm[7]]

    return (loss, out["dx"][None], *ordered(0), *ordered(1), *ordered(2), *ordered(3))
```

```python
import functools

import numpy as np
import jax
import jax.numpy as jnp
from jax import lax
from jax.experimental import pallas as pl
from jax.experimental.pallas import tpu as pltpu

F32 = jnp.float32
BF16 = jnp.bfloat16

D_MODEL = 1024
N_META = 16
BLK = 128
UNIT = 2 * BLK
FOX_PAIRS = 2
WIDE = 4
CHUNK = 64
N_PAD = BLK - N_META
PREFIX = BLK
RET_HEADS = 4
FOX_HEADS = 8
HEAD_LANES = 64
D_FF = 2816
ROPE_BASE = 10000.0
EPS = 1e-6
NEG = -1e30
RET_W = 1536
FOX_W = 1536
MAIN_W = RET_W + FOX_W
IN_WIDTH = MAIN_W + FOX_HEADS
N_CHIPS = 4
N_DEV = 8

ADAM_LR = 0.001
ADAM_B1 = 0.9
ADAM_B2 = 0.999
ADAM_EPS = 1e-08
ADAM_WD = 0.01
ADAM_STEP = 10

MESH = pl.DeviceIdType.MESH
VMEM_LIMIT_MB = 56

_NT = (((1,), (1,)), ((), ()))
_TN = (((0,), (0,)), ((), ()))


def _dot(a, b):
    return jnp.dot(a, b, preferred_element_type=F32)


def _dot_nt(a, b):
    return lax.dot_general(a, b, _NT, preferred_element_type=F32)


def _dot_tn(a, b):
    return lax.dot_general(a, b, _TN, preferred_element_type=F32)


def _params(dims=None, vmem_mb=VMEM_LIMIT_MB):
    kw = dict(vmem_limit_bytes=vmem_mb << 20)
    if dims is not None:
        kw["dimension_semantics"] = dims
    return pltpu.CompilerParams(**kw)


def _row_tile(n, prefs=(384, 256, 128)):
    for t in prefs:
        if n % t == 0:
            return t
    raise ValueError(f"no row tile for {n}")


def _iota(shape, dim):
    return lax.broadcasted_iota(jnp.int32, shape, dim)


def _pick_row(tile, row):
    sub = _iota(tile.shape, 0)
    return jnp.sum(jnp.where(sub == row, tile, 0.0), axis=0, keepdims=True)


def _split3(x):
    hi = x.astype(BF16)
    r1 = x - hi.astype(F32)
    mid = r1.astype(BF16)
    lo = (r1 - mid.astype(F32)).astype(BF16)
    return hi, mid, lo


def _full(shape):
    nd = len(shape)
    return pl.BlockSpec(shape, lambda *_: (0,) * nd)


def _in_perm():
    cols = list(range(RET_W))
    for p in range(FOX_HEADS // 2):
        for part in range(3):
            start = RET_W + part * 512 + p * BLK
            cols += list(range(start, start + BLK))
    return np.asarray(cols, np.int32)


def _rotary_tables(L):
    half = HEAD_LANES // 2
    inv = 1.0 / (ROPE_BASE ** (jnp.arange(half, dtype=F32) / half))
    ang = jnp.arange(L).astype(F32)[:, None] * inv[None, :]
    cos, sin = jnp.cos(ang), jnp.sin(ang)
    cos_t = jnp.tile(cos, (1, 4))
    sin_t = jnp.tile(jnp.concatenate([-sin, sin], axis=1), (1, 2))
    return cos_t, sin_t


def _decay_tables():
    gam = 1.0 - 2.0 ** (-5.0 - np.arange(RET_HEADS, dtype=np.float64))
    n = np.arange(BLK)
    same_or_past = (n[:, None] // CHUNK) >= (n[None, :] // CHUNK)
    dist = np.abs(n[:, None] - n[None, :])
    dmat = np.stack([np.where(same_or_past, g ** dist, 0.0) for g in gam]).astype(np.float32)
    lane_head = np.arange(BLK) // HEAD_LANES
    wq = np.stack([gam[2 * p + lane_head][None, :] ** (n[:, None] + 1.0) for p in range(2)]).astype(np.float32)
    wk = np.stack([gam[2 * p + lane_head][None, :] ** (BLK - 1.0 - n[:, None]) for p in range(2)]).astype(np.float32)
    g_blk = tuple(float(g ** BLK) for g in gam)
    return jnp.asarray(dmat), jnp.asarray(wq), jnp.asarray(wk), g_blk


def _rms_inproj(h0, g, w_main, w_ff):
    L = h0.shape[0]
    tm = _row_tile(L)

    def body(h_ref, g_ref, wm_ref, wf_ref, n_ref, p_ref, ff_ref):
        h = h_ref[...]
        r = lax.rsqrt(jnp.mean(h * h, axis=-1, keepdims=True) + EPS)
        n = (h * r * g_ref[...]).astype(BF16)
        n_ref[...] = n
        p_ref[...] = _dot(n, wm_ref[...]).astype(BF16)
        ff_ref[...] = _dot(n, wf_ref[...])

    return pl.pallas_call(
        body, name="f_inproj", grid=(L // tm,),
        in_specs=[pl.BlockSpec((tm, D_MODEL), lambda i: (i, 0)), _full((1, D_MODEL)),
                  _full((D_MODEL, MAIN_W)), _full((D_MODEL, BLK))],
        out_specs=[pl.BlockSpec((tm, D_MODEL), lambda i: (i, 0)), pl.BlockSpec((tm, MAIN_W), lambda i: (i, 0)),
                   pl.BlockSpec((tm, BLK), lambda i: (i, 0))],
        out_shape=[jax.ShapeDtypeStruct((L, D_MODEL), BF16), jax.ShapeDtypeStruct((L, MAIN_W), BF16),
                   jax.ShapeDtypeStruct((L, BLK), F32)],
        compiler_params=_params(("parallel",)),
    )(h0, g, w_main, w_ff)


def _block_group(nblk):
    return 3 if nblk % 3 == 0 else 1


def _fox_prep(ff, fb):
    L = ff.shape[0]
    nblk = L // BLK
    G = _block_group(nblk)

    def body(ff_ref, b_ref, c_ref, ct_ref, carry):
        @pl.when(pl.program_id(0) == 0)
        def _():
            carry[...] = jnp.zeros_like(carry)

        tri = (_iota((BLK, BLK), 0) >= _iota((BLK, BLK), 1)).astype(BF16)
        live = _iota((BLK, BLK), 1) < FOX_HEADS
        run = carry[...]
        for b in range(G):
            z = ff_ref[b * BLK:(b + 1) * BLK, :] + b_ref[...]
            lf = jnp.where(live, jnp.minimum(z, 0.0) - jnp.log1p(jnp.exp(-jnp.abs(z))), 0.0)
            hi, mid, lo = _split3(lf)
            cs = _dot(tri, hi) + _dot(tri, mid) + _dot(tri, lo) + run
            c_ref[b * BLK:(b + 1) * BLK, :] = cs
            ct_ref[b] = cs.T[0:8, :]
            run = run + jnp.sum(lf, axis=0, keepdims=True)
        carry[...] = run

    return pl.pallas_call(
        body, name="f_foxprep", grid=(nblk // G,),
        in_specs=[pl.BlockSpec((G * BLK, BLK), lambda i: (i, 0)), _full((1, BLK))],
        out_specs=[pl.BlockSpec((G * BLK, BLK), lambda i: (i, 0)), pl.BlockSpec((G, 8, BLK), lambda i: (i, 0, 0))],
        out_shape=[jax.ShapeDtypeStruct((L, BLK), F32), jax.ShapeDtypeStruct((nblk, 8, BLK), F32)],
        scratch_shapes=[pltpu.VMEM((1, BLK), F32)],
        compiler_params=_params(("arbitrary",)),
    )(ff, fb)


def _rot_fns(cos, sin):
    lane = _iota((BLK, BLK), 1)
    first = (lane & (HEAD_LANES - 1)) < HEAD_LANES // 2

    def swap(x):
        return jnp.where(first, pltpu.roll(x, BLK - 32, 1), pltpu.roll(x, 32, 1))

    def rot(x):
        return x * cos + swap(x) * sin

    def rot_t(dy):
        return dy * cos + swap(dy * sin)

    return rot, rot_t


def _retention_fwd(proj, cos_t, sin_t, ret_g):
    L = proj.shape[0]
    nblk = L // BLK
    G = _block_group(nblk)
    dmat, wq_t, wk_t, g_blk = _decay_tables()

    def body(q_ref, k_ref, v_ref, gate_ref, cos_ref, sin_ref, d_ref, wq_ref, wk_ref, rg_ref,
             mix_ref, o_ref, rs_ref, state):
        @pl.when(pl.program_id(0) == 0)
        def _():
            state[...] = jnp.zeros_like(state)

        lane = _iota((BLK, BLK), 1)
        sub = _iota((BLK, BLK), 0)
        for b in range(G):
            rows = slice(b * BLK, (b + 1) * BLK)
            rot, _ = _rot_fns(cos_ref[rows, :], sin_ref[rows, :])
            for p in range(2):
                qr = rot(q_ref[rows, p * BLK:(p + 1) * BLK].astype(F32))
                kr = rot(k_ref[rows, p * BLK:(p + 1) * BLK].astype(F32)) * (HEAD_LANES ** -0.5)
                kr_b = kr.astype(BF16)
                qw = (qr * wq_ref[p]).astype(BF16)
                kw = (kr * wk_ref[p]).astype(BF16)
                for e in range(2):
                    h = 2 * p + e
                    cols = slice(h * BLK, (h + 1) * BLK)
                    qm = jnp.where((lane >> 6) == e, qr, 0.0).astype(BF16)
                    s = _dot_nt(qm, kr_b) * d_ref[h]
                    vh = v_ref[rows, cols]
                    st = state[h]
                    rs_ref[b, h] = st
                    o = _dot(s.astype(BF16), vh) + _dot(qw, st.astype(BF16))
                    u = jnp.where((sub >> 6) == e, _dot_tn(kw, vh), 0.0)
                    state[h] = g_blk[h] * st + u
                    rn = lax.rsqrt(jnp.mean(o * o, axis=-1, keepdims=True) + EPS)
                    gate = gate_ref[rows, cols].astype(F32)
                    o_ref[rows, cols] = o
                    mix_ref[rows, cols] = (o * rn * rg_ref[:, cols] * (gate * jax.nn.sigmoid(gate))).astype(BF16)

    row = lambda c: (lambda i: (i, c))
    return pl.pallas_call(
        body, name="f_retention", grid=(nblk // G,),
        in_specs=[pl.BlockSpec((G * BLK, 256), row(0)), pl.BlockSpec((G * BLK, 256), row(1)),
                  pl.BlockSpec((G * BLK, 512), row(1)), pl.BlockSpec((G * BLK, 512), row(2)),
                  pl.BlockSpec((G * BLK, BLK), row(0)), pl.BlockSpec((G * BLK, BLK), row(0)),
                  _full((RET_HEADS, BLK, BLK)), _full((2, BLK, BLK)), _full((2, BLK, BLK)), _full((1, 512))],
        out_specs=[pl.BlockSpec((G * BLK, 512), row(0)), pl.BlockSpec((G * BLK, 512), row(0)),
                   pl.BlockSpec((G, RET_HEADS, BLK, BLK), lambda i: (i, 0, 0, 0))],
        out_shape=[jax.ShapeDtypeStruct((L, 512), BF16), jax.ShapeDtypeStruct((L, 512), F32),
                   jax.ShapeDtypeStruct((nblk, RET_HEADS, BLK, BLK), F32)],
        scratch_shapes=[pltpu.VMEM((RET_HEADS, BLK, BLK), F32)],
        compiler_params=_params(("arbitrary",)),
    )(proj, proj, proj, proj, cos_t, sin_t, dmat, wq_t, wk_t, ret_g)


def _fox_units(L):
    nblk = L // BLK
    assert L % BLK == 0 and nblk % 2 == 1, "sequence must be one 128-row block plus whole 256-row tiles"
    return nblk, (nblk - 1) // 2


def _fox_tile_masks():
    sub, lane = _iota((BLK, BLK), 0), _iota((BLK, BLK), 1)
    valid = _iota((BLK, UNIT), 0) >= N_PAD
    diag = _iota((UNIT, UNIT), 0) <= _iota((UNIT, UNIT), 1)
    r, q = _iota((BLK + UNIT, UNIT), 0), _iota((BLK + UNIT, UNIT), 1)
    first_and_diag = ((r < BLK) & (r >= N_PAD)) | ((r >= BLK) & (r - BLK <= q))
    return dict(first=(sub <= lane) & (sub >= N_PAD), valid=valid, diag=diag, first_and_diag=first_and_diag)


def _fox_fwd(proj, c, ctb, gather=()):
    L = proj.shape[0]
    nblk, nu = _fox_units(L)
    scale = HEAD_LANES ** -0.5
    ng = len(gather)
    steps = FOX_HEADS // (2 * FOX_PAIRS)

    def body(qkv_ref, c_ref, ct_ref, *rest):
        g_in, (of_ref, lse_ref), g_out = rest[:ng], rest[ng:ng + 2], rest[ng + 2:2 * ng + 2]
        vt, csb = rest[2 * ng + 2:2 * ng + 4]
        p = pl.program_id(0)
        heads = [(pp, e, 2 * FOX_PAIRS * p + 2 * pp + e) for pp in range(FOX_PAIRS) for e in range(2)]

        @pl.when(p == 0)
        def _():
            lse_ref[...] = jnp.zeros_like(lse_ref)
            if ng:
                local, sends, _ = _allgather_copies(g_in, g_out, *rest[2 * ng + 4:])
                for cp in local + sends:
                    cp.start()

        lane = _iota((BLK, BLK), 1)
        sub8 = _iota((8, BLK), 0)
        masks = _fox_tile_masks()

        def pre(j, carry):
            off = pl.multiple_of(j * BLK, BLK)
            ct = c_ref[pl.ds(off, BLK), :]
            for pp in range(FOX_PAIRS):
                vt[pp, j] = qkv_ref[pl.ds(off, BLK), pp * 384 + 2 * BLK:pp * 384 + 3 * BLK].astype(F32).T.astype(BF16)
            for hh, (_, _, h) in enumerate(heads):
                col = jnp.sum(jnp.where(lane == h, ct, 0.0), axis=1, keepdims=True)
                csb[hh, j] = jnp.broadcast_to(col, (BLK, BLK))
            return carry

        lax.fori_loop(0, nblk, pre, 0)

        def attend(qblk, nq, n_whole):
            qlen = nq * BLK
            qoff = pl.multiple_of(qblk * BLK, BLK)
            qlane = _iota((qlen, BLK), 1)
            qs = [qkv_ref[pl.ds(qoff, qlen), pp * 384:pp * 384 + BLK].astype(F32) * scale for pp in range(FOX_PAIRS)]
            qm = [jnp.where((qlane >> 6) == e, qs[pp], 0.0).astype(BF16) for pp, e, _ in heads]
            ct_row = [jnp.concatenate([_pick_row(ct_ref[qblk + a], h) for a in range(nq)], axis=1) for _, _, h in heads]

            def step(segs, mask, st):
                blocks = [kblk + b for kblk, nk in segs for b in range(nk)]
                kts = []
                for pp in range(FOX_PAIRS):
                    kt = [qkv_ref[pl.ds(pl.multiple_of(kblk * BLK, BLK), nk * BLK), pp * 384 + BLK:pp * 384 + 2 * BLK]
                          for kblk, nk in segs]
                    kts.append(kt[0] if len(kt) == 1 else jnp.concatenate(kt, axis=0))
                out = []
                for hh, (pp, e, _) in enumerate(heads):
                    m, l, acc = st[3 * hh:3 * hh + 3]
                    s = _dot_nt(kts[pp], qm[hh])
                    t = jnp.concatenate([s[b * BLK:(b + 1) * BLK] - jnp.concatenate([csb[hh, blk]] * nq, axis=1)
                                         for b, blk in enumerate(blocks)], axis=0)
                    if mask is not None:
                        t = jnp.where(mask, t, NEG)
                    m_new = jnp.maximum(m, jnp.max(t, axis=0, keepdims=True) + ct_row[hh])
                    alpha = jnp.exp(m - m_new)
                    pr = jnp.exp(t - (m_new - ct_row[hh]))
                    l = alpha * l + jnp.sum(pr, axis=0, keepdims=True)
                    pr_b = pr.astype(BF16)
                    pv = None
                    for b, blk in enumerate(blocks):
                        part = _dot(vt[pp, blk, e * HEAD_LANES:(e + 1) * HEAD_LANES, :], pr_b[b * BLK:(b + 1) * BLK])
                        pv = part if pv is None else pv + part
                    out += [m_new, l, alpha * acc + pv]
                return tuple(out)

            st = (jnp.full((1, qlen), NEG, F32), jnp.zeros((1, qlen), F32),
                  jnp.zeros((HEAD_LANES, qlen), F32)) * len(heads)
            if nq == 1:
                st = step([(0, 1)], masks["first"], st)
            else:
                st = step([(0, 1), (qblk, 2)], masks["first_and_diag"], st)
                n_wide = n_whole // WIDE
                st = lax.fori_loop(0, n_wide, lambda j, s_: step([(1 + 2 * WIDE * j, 2 * WIDE)], None, s_), st)
                rest = 1 + 2 * WIDE * n_wide
                st = lax.cond((n_whole & 2) != 0, lambda s_: step([(rest, 4)], None, s_), lambda s_: s_, st)
                st = lax.cond((n_whole & 1) != 0, lambda s_: step([(rest + 2 * (n_whole & 2), 2)], None, s_),
                              lambda s_: s_, st)
            for pp in range(FOX_PAIRS):
                lo, hi = st[6 * pp:6 * pp + 3], st[6 * pp + 3:6 * pp + 6]
                o_t = jnp.concatenate([lo[2] * (1.0 / lo[1]), hi[2] * (1.0 / hi[1])], axis=0)
                of_ref[pl.ds(qoff, qlen), pp * BLK:(pp + 1) * BLK] = o_t.T.astype(BF16)
            lse = [st[3 * hh] + jnp.log(st[3 * hh + 1]) for hh in range(len(heads))]
            for a in range(nq):
                upd = jnp.zeros((8, BLK), F32)
                for hh, (_, _, h) in enumerate(heads):
                    upd = upd + jnp.where(sub8 == h, lse[hh][:, a * BLK:(a + 1) * BLK], 0.0)
                lse_ref[qblk + a] = lse_ref[qblk + a] + upd

        attend(0, 1, 0)

        def q_loop(u, carry):
            attend(1 + 2 * u, 2, u)
            return carry

        lax.fori_loop(0, nu, q_loop, 0)

        if ng:
            @pl.when(p == steps - 1)
            def _():
                local, sends, recvs = _allgather_copies(g_in, g_out, *rest[2 * ng + 4:])
                for cp in recvs:
                    cp.wait_recv()
                for cp in sends:
                    cp.wait_send()
                for cp in local:
                    cp.wait()

    width = 384 * FOX_PAIRS
    return pl.pallas_call(
        body, name="f_fox", grid=(steps,),
        in_specs=[pl.BlockSpec((L, width), lambda p: (0, RET_W // width + p)), _full((L, BLK)), _full((nblk, 8, BLK))]
        + [_ANY] * ng,
        out_specs=[pl.BlockSpec((L, FOX_PAIRS * BLK), lambda p: (0, p)), _full((nblk, 8, BLK))] + [_ANY] * ng,
        out_shape=[jax.ShapeDtypeStruct((L, 512), BF16), jax.ShapeDtypeStruct((nblk, 8, BLK), F32)]
        + [jax.ShapeDtypeStruct((N_CHIPS,) + a.shape, a.dtype) for a in gather],
        scratch_shapes=[pltpu.VMEM((FOX_PAIRS, nblk, BLK, BLK), BF16), pltpu.VMEM((2 * FOX_PAIRS, nblk, BLK, BLK), F32)]
        + _allgather_semaphores(ng),
        compiler_params=_params(("arbitrary",)),
    )(proj, c, ctb, *gather)


def _outproj_up(mix_r, o_f, h0, w_out, ffn_g, w_up, conv_w, conv_b):
    L = h0.shape[0]
    tm = _row_tile(L)
    shard = w_up.shape[2]
    assert 2 * shard == D_FF
    cw = [conv_w[j:j + 1] for j in range(3)]
    resident = lambda shape: pl.BlockSpec(shape, lambda i: (0,) * len(shape), pipeline_mode=pl.Buffered(1))

    def body(mr_ref, of_ref, h0_ref, wo_ref, g_ref, wu_ref, cw0, cw1, cw2, cb_ref,
             h1_ref, n2_ref, up_ref, act_ref, acc_ref, halo):
        i = pl.program_id(0)

        @pl.when(i == 0)
        def _():
            halo[...] = jnp.zeros_like(halo)

        h1 = h0_ref[...] + _dot(mr_ref[...], wo_ref[0:512, :]) + _dot(of_ref[...], wo_ref[512:1024, :])
        h1_ref[...] = h1
        r = lax.rsqrt(jnp.mean(h1 * h1, axis=-1, keepdims=True) + EPS)
        n2 = (h1 * r * g_ref[...]).astype(BF16)
        n2_ref[...] = n2
        live = i * tm + _iota((tm, 1), 0) >= N_PAD
        for half in range(2):
            cols = slice(half * shard, (half + 1) * shard)
            a_b = _dot(n2, wu_ref[half]).astype(BF16)
            b_b = _dot(n2, wu_ref[2 + half]).astype(BF16)
            up_ref[:, cols] = a_b
            up_ref[:, D_FF + half * shard:D_FF + (half + 1) * shard] = b_b
            a = jnp.where(live, a_b.astype(F32), 0.0)
            _, _, acc = _conv_taps(a, halo[:, cols], [cw0[:, cols], cw1[:, cols], cw2[:, cols]], cb_ref[:, cols])
            act_ref[:, cols] = (acc * jax.nn.sigmoid(acc) * b_b.astype(F32)).astype(BF16)
            acc_ref[:, cols] = acc.astype(BF16)
            halo[:, cols] = a[tm - 8:tm, :]

    rows = lambda w: pl.BlockSpec((tm, w), lambda i: (i, 0))
    return pl.pallas_call(
        body, name="f_outproj_up", grid=(L // tm,),
        in_specs=[rows(512), rows(512), rows(D_MODEL), resident((D_MODEL, D_MODEL)), _full((1, D_MODEL)),
                  resident((N_CHIPS, D_MODEL, shard)), _full((1, D_FF)), _full((1, D_FF)), _full((1, D_FF)),
                  _full((1, D_FF))],
        out_specs=[rows(D_MODEL), rows(D_MODEL), rows(2 * D_FF), rows(D_FF), rows(D_FF)],
        out_shape=[jax.ShapeDtypeStruct((L, D_MODEL), F32), jax.ShapeDtypeStruct((L, D_MODEL), BF16),
                   jax.ShapeDtypeStruct((L, 2 * D_FF), BF16), jax.ShapeDtypeStruct((L, D_FF), BF16),
                   jax.ShapeDtypeStruct((L, D_FF), BF16)],
        scratch_shapes=[pltpu.VMEM((8, D_FF), F32)],
        compiler_params=_params(("arbitrary",)),
    )(mix_r, o_f, h0, w_out, ffn_g, w_up, cw[0], cw[1], cw[2], conv_b)


def _conv_taps(a, halo, cw, cb):
    sub = _iota((a.shape[0], 1), 0)
    a1 = jnp.where(sub == 0, _pick_row(halo, 7), pltpu.roll(a, 1, 0))
    a2 = jnp.where(sub == 0, _pick_row(halo, 6), jnp.where(sub == 1, _pick_row(halo, 7), pltpu.roll(a, 2, 0)))
    acc = cb + a2 * cw[0]
    acc = acc + a1 * cw[1]
    acc = acc + a * cw[2]
    return a1, a2, acc


def _ffn_down_loss(g_act, w_down, h1, final_g, target):
    L = h1.shape[0]
    tm = _row_tile(L)

    def body(g_ref, wd_ref, h1_ref, gf_ref, t_ref, dh_ref, dhb_ref, dgf_ref, loss_ref):
        i = pl.program_id(0)

        @pl.when(i == 0)
        def _():
            dgf_ref[...] = jnp.zeros_like(dgf_ref)
            loss_ref[...] = jnp.zeros_like(loss_ref)

        h2 = h1_ref[...] + _dot(g_ref[...], wd_ref[...])
        r = lax.rsqrt(jnp.mean(h2 * h2, axis=-1, keepdims=True) + EPS)
        yn = h2 * r
        gf = gf_ref[...]
        live = i * tm + _iota((tm, 1), 0) >= PREFIX
        err = jnp.where(live, yn * gf - t_ref[...], 0.0)
        loss_ref[...] = loss_ref[...] + 0.5 * jnp.sum(jnp.mean(err * err, axis=-1, keepdims=True))
        dy = err * (1.0 / D_MODEL)
        dgf_ref[...] = dgf_ref[...] + jnp.sum(dy * yn, axis=0, keepdims=True)
        dyn = dy * gf
        dh = r * (dyn - yn * jnp.mean(dyn * yn, axis=-1, keepdims=True))
        dh_ref[...] = dh
        dhb_ref[...] = dh.astype(BF16)

    rows = lambda w: pl.BlockSpec((tm, w), lambda i: (i, 0))
    return pl.pallas_call(
        body, name="f_ffn_down_loss", grid=(L // tm,),
        in_specs=[rows(D_FF), _full((D_FF, D_MODEL)), rows(D_MODEL), _full((1, D_MODEL)), rows(D_MODEL)],
        out_specs=[rows(D_MODEL), rows(D_MODEL), _full((1, D_MODEL)), _full((1, BLK))],
        out_shape=[jax.ShapeDtypeStruct((L, D_MODEL), F32), jax.ShapeDtypeStruct((L, D_MODEL), BF16),
                   jax.ShapeDtypeStruct((1, D_MODEL), F32), jax.ShapeDtypeStruct((1, BLK), F32)],
        compiler_params=_params(("arbitrary",)),
    )(g_act, w_down, h1, final_g, target)


def _ffn_bwd_gate(dh2b, w_down, acc_saved, up):
    L = dh2b.shape[0]
    tm = _row_tile(L)

    def body(dh_ref, wd_ref, acc_ref, b_ref, dacc_ref, db_ref):
        acc = acc_ref[...].astype(F32)
        dg = _dot_nt(dh_ref[...], wd_ref[...])
        sg = jax.nn.sigmoid(acc)
        silu = acc * sg
        db_ref[...] = (dg * silu).astype(BF16)
        dacc_ref[...] = (dg * b_ref[...].astype(F32) * (sg + silu * (1.0 - sg))).astype(BF16)

    rows = lambda w, c=0: pl.BlockSpec((tm, w), lambda i: (i, c))
    return pl.pallas_call(
        body, name="b_ffn_gate", grid=(L // tm,),
        in_specs=[rows(D_MODEL), _full((D_FF, D_MODEL)), rows(D_FF), rows(D_FF, 1)],
        out_specs=[rows(D_FF), rows(D_FF)],
        out_shape=[jax.ShapeDtypeStruct((L, D_FF), BF16), jax.ShapeDtypeStruct((L, D_FF), BF16)],
        compiler_params=_params(("parallel",)),
    )(dh2b, w_down, acc_saved, up)


def _ffn_bwd_up(dacc, db, up, conv_w, w_up, h1, ffn_g, dh2, w_out):
    L = h1.shape[0]
    tm = _row_tile(L)
    nt = L // tm
    shard = w_up.shape[2]
    cw = [conv_w[j:j + 1] for j in range(3)]

    def body(da_ref, halo_ref, db_ref, a_ref, cw0, cw1, cw2, wu_ref, h1_ref, g_ref, dh2_ref, wo_ref,
             dup_ref, dh1_ref, dh1b_ref, dmix_ref, dg_ref, dcw_ref):
        i = pl.program_id(0)

        @pl.when(i == 0)
        def _():
            dg_ref[...] = jnp.zeros_like(dg_ref)
            dcw_ref[...] = jnp.zeros_like(dcw_ref)

        sub = _iota((tm, 1), 0)
        live = i * tm + sub >= N_PAD
        d0 = da_ref[...].astype(F32)
        halo = jnp.where(i < nt - 1, halo_ref[...].astype(F32), 0.0)
        d1 = jnp.where(sub == tm - 1, _pick_row(halo, 0), pltpu.roll(d0, tm - 1, 0))
        d2 = jnp.where(sub == tm - 2, _pick_row(halo, 0),
                       jnp.where(sub == tm - 1, _pick_row(halo, 1), pltpu.roll(d0, tm - 2, 0)))
        a = jnp.where(live, a_ref[...].astype(F32), 0.0)
        sub8 = _iota((8, 1), 0)
        upd = jnp.zeros((8, D_FF), F32)
        for j, t in enumerate((d2 * a, d1 * a, d0 * a, d0)):
            upd = upd + jnp.where(sub8 == j, jnp.sum(t, axis=0, keepdims=True), 0.0)
        dcw_ref[...] = dcw_ref[...] + upd
        da = d0 * cw2[...] + d1 * cw1[...] + d2 * cw0[...]
        da = jnp.where(live, da, 0.0).astype(BF16)
        dup_ref[:, 0:D_FF] = da
        dbv = db_ref[...]
        dup_ref[:, D_FF:2 * D_FF] = dbv
        dn = jnp.zeros((tm, D_MODEL), F32)
        for j in range(N_CHIPS):
            src = da if j < 2 else dbv
            lo = (j % 2) * shard
            dn = dn + _dot_nt(src[:, lo:lo + shard], wu_ref[j])
        h1 = h1_ref[...]
        r = lax.rsqrt(jnp.mean(h1 * h1, axis=-1, keepdims=True) + EPS)
        yn = h1 * r
        dg_ref[...] = dg_ref[...] + jnp.sum(dn * yn, axis=0, keepdims=True)
        dyn = dn * g_ref[...]
        dh1 = dh2_ref[...] + r * (dyn - yn * jnp.mean(dyn * yn, axis=-1, keepdims=True))
        dh1_ref[...] = dh1
        dh1b = dh1.astype(BF16)
        dh1b_ref[...] = dh1b
        dmix_ref[...] = _dot_nt(dh1b, wo_ref[...]).astype(BF16)

    rows = lambda w: pl.BlockSpec((tm, w), lambda i: (i, 0))
    halo = pl.BlockSpec((8, D_FF), lambda i: (jnp.minimum((i + 1) * (tm // 8), L // 8 - 1), 0))
    return pl.pallas_call(
        body, name="b_ffn_up", grid=(nt,),
        in_specs=[rows(D_FF), halo, rows(D_FF), rows(D_FF), _full((1, D_FF)), _full((1, D_FF)), _full((1, D_FF)),
                  _full((N_CHIPS, D_MODEL, shard)), rows(D_MODEL), _full((1, D_MODEL)), rows(D_MODEL),
                  _full((D_MODEL, D_MODEL))],
        out_specs=[rows(2 * D_FF), rows(D_MODEL), rows(D_MODEL), rows(D_MODEL), _full((1, D_MODEL)),
                   _full((8, D_FF))],
        out_shape=[jax.ShapeDtypeStruct((L, 2 * D_FF), BF16), jax.ShapeDtypeStruct((L, D_MODEL), F32),
                   jax.ShapeDtypeStruct((L, D_MODEL), BF16), jax.ShapeDtypeStruct((L, D_MODEL), BF16),
                   jax.ShapeDtypeStruct((1, D_MODEL), F32), jax.ShapeDtypeStruct((8, D_FF), F32)],
        compiler_params=_params(("arbitrary",)),
    )(dacc, dacc, db, up, cw[0], cw[1], cw[2], w_up, h1, ffn_g, dh2, w_out)


def _wgrad(a, b, name, tn=None, tk=None):
    L, K = a.shape
    N = b.shape[1]
    tn = N if tn is None else tn
    tk = K if tk is None else tk
    tl = _row_tile(L, (1408, 768, 512, 256, 128))

    def body(a_ref, b_ref, o_ref):
        @pl.when(pl.program_id(2) == 0)
        def _():
            o_ref[...] = jnp.zeros_like(o_ref)

        o_ref[0] = o_ref[0] + _dot_tn(a_ref[...], b_ref[...])

    return pl.pallas_call(
        body, name=name, grid=(N // tn, K // tk, L // tl),
        in_specs=[pl.BlockSpec((tl, tk), lambda n, k, l: (l, k)), pl.BlockSpec((tl, tn), lambda n, k, l: (l, n))],
        out_specs=pl.BlockSpec((1, tk, tn), lambda n, k, l: (n, k, 0)),
        out_shape=jax.ShapeDtypeStruct((N // tn, K, tn), F32),
        compiler_params=_params(("parallel", "parallel", "arbitrary")),
    )(a, b)


def _retention_bwd(dmix, o, proj, cos_t, sin_t, ret_g, states, exchange=()):
    L = proj.shape[0]
    nblk = L // BLK
    G = _block_group(nblk)
    steps = nblk // G
    nx = len(exchange)
    dmat, wq_t, wk_t, g_blk = _decay_tables()

    def body(dm_ref, o_ref, q_ref, k_ref, v_ref, gate_ref, cos_ref, sin_ref, d_ref, wq_ref, wk_ref, rg_ref, rs_ref,
             *rest):
        x_in, (dp_ref, drg_ref), x_out, gstate = rest[:nx], rest[nx:nx + 2], rest[nx + 2:2 * nx + 2], rest[2 * nx + 2]

        @pl.when(pl.program_id(0) == 0)
        def _():
            if nx:
                for cp in _sibling_half_copies(x_in, x_out, *rest[2 * nx + 3:])[0]:
                    cp.start()
            gstate[...] = jnp.zeros_like(gstate)
            drg_ref[...] = jnp.zeros_like(drg_ref)

        lane = _iota((BLK, BLK), 1)
        sub = _iota((BLK, BLK), 0)
        scale = HEAD_LANES ** -0.5
        for b in reversed(range(G)):
            rows = slice(b * BLK, (b + 1) * BLK)
            rot, rot_t = _rot_fns(cos_ref[rows, :], sin_ref[rows, :])
            for p in range(2):
                qr = rot(q_ref[rows, p * BLK:(p + 1) * BLK].astype(F32))
                kr = rot(k_ref[rows, p * BLK:(p + 1) * BLK].astype(F32)) * scale
                kr_b = kr.astype(BF16)
                qw = (qr * wq_ref[p]).astype(BF16)
                kw = (kr * wk_ref[p]).astype(BF16)
                dqr = jnp.zeros((BLK, BLK), F32)
                dkr = jnp.zeros((BLK, BLK), F32)
                for e in range(2):
                    h = 2 * p + e
                    cols = slice(h * BLK, (h + 1) * BLK)
                    head_lanes = (lane >> 6) == e
                    o = o_ref[rows, cols]
                    rn = lax.rsqrt(jnp.mean(o * o, axis=-1, keepdims=True) + EPS)
                    y = o * rn
                    gate = gate_ref[rows, cols].astype(F32)
                    sg = jax.nn.sigmoid(gate)
                    dm = dm_ref[rows, cols].astype(F32)
                    rgain = rg_ref[:, cols]
                    drg_ref[:, cols] = drg_ref[:, cols] + jnp.sum(dm * y * (gate * sg), axis=0, keepdims=True)
                    dp_ref[rows, 1024 + h * BLK:1024 + (h + 1) * BLK] = (
                        dm * y * rgain * (sg * (1.0 + gate * (1.0 - sg)))).astype(BF16)
                    dy = dm * rgain * (gate * sg)
                    do = (rn * (dy - y * jnp.mean(dy * y, axis=-1, keepdims=True))).astype(BF16)
                    vh = v_ref[rows, cols]
                    qm = jnp.where(head_lanes, qr, 0.0).astype(BF16)
                    dmh = d_ref[h]
                    s = (_dot_nt(qm, kr_b) * dmh).astype(BF16)
                    ds = (_dot_nt(do, vh) * dmh).astype(BF16)
                    st = rs_ref[b, h].astype(BF16)
                    gs = gstate[h]
                    gs_b = gs.astype(BF16)
                    dqr = dqr + jnp.where(head_lanes, _dot(ds, kr_b), 0.0) + _dot_nt(do, st) * wq_ref[p]
                    dkr = dkr + _dot_tn(ds, qm) + _dot_nt(vh, gs_b) * wk_ref[p]
                    dp_ref[rows, 512 + h * BLK:512 + (h + 1) * BLK] = (_dot_tn(s, do) + _dot(kw, gs_b)).astype(BF16)
                    dr = jnp.where((sub >> 6) == e, _dot_tn(qw, do), 0.0)
                    gstate[h] = dr + g_blk[h] * gs
                dp_ref[rows, p * BLK:(p + 1) * BLK] = rot_t(dqr).astype(BF16)
                dp_ref[rows, 256 + p * BLK:256 + (p + 1) * BLK] = (rot_t(dkr) * scale).astype(BF16)

        if nx:
            @pl.when(pl.program_id(0) == steps - 1)
            def _():
                sends, recvs = _sibling_half_copies(x_in, x_out, *rest[2 * nx + 3:])
                for cp in recvs:
                    cp.wait_recv()
                for cp in sends:
                    cp.wait_send()

    row = lambda c: (lambda i: (steps - 1 - i, c))
    return pl.pallas_call(
        body, name="b_retention", grid=(steps,),
        in_specs=[pl.BlockSpec((G * BLK, 512), row(0)), pl.BlockSpec((G * BLK, 512), row(0)),
                  pl.BlockSpec((G * BLK, 256), row(0)), pl.BlockSpec((G * BLK, 256), row(1)),
                  pl.BlockSpec((G * BLK, 512), row(1)), pl.BlockSpec((G * BLK, 512), row(2)),
                  pl.BlockSpec((G * BLK, BLK), row(0)), pl.BlockSpec((G * BLK, BLK), row(0)),
                  _full((RET_HEADS, BLK, BLK)), _full((2, BLK, BLK)), _full((2, BLK, BLK)), _full((1, 512)),
                  pl.BlockSpec((G, RET_HEADS, BLK, BLK), lambda i: (steps - 1 - i, 0, 0, 0))] + [_ANY] * nx,
        out_specs=[pl.BlockSpec((G * BLK, RET_W), row(0)), _full((1, 512))] + [_ANY] * nx,
        out_shape=[jax.ShapeDtypeStruct((L, RET_W), BF16), jax.ShapeDtypeStruct((1, 512), F32)]
        + _sibling_half_shapes(exchange),
        scratch_shapes=[pltpu.VMEM((RET_HEADS, BLK, BLK), F32)] + _sibling_half_semaphores(nx),
        compiler_params=_params(("arbitrary",)),
    )(dmix, o, proj, proj, proj, proj, cos_t, sin_t, dmat, wq_t, wk_t, ret_g, states, *exchange)


def _fox_delta(dmix, o_f):
    L = o_f.shape[0]
    nblk = L // BLK
    G = _block_group(nblk)

    def body(do_ref, o_ref, d_ref):
        sel = ((_iota((8, 512), 1) >> 6) == _iota((8, 512), 0)).astype(BF16)
        for b in range(G):
            rows = slice(b * BLK, (b + 1) * BLK)
            prod = do_ref[rows, :].astype(F32) * o_ref[rows, :].astype(F32)
            hi = prod.astype(BF16)
            lo = (prod - hi.astype(F32)).astype(BF16)
            d_ref[b] = _dot_nt(sel, hi) + _dot_nt(sel, lo)

    return pl.pallas_call(
        body, name="b_foxdelta", grid=(nblk // G,),
        in_specs=[pl.BlockSpec((G * BLK, 512), lambda i: (i, 1)), pl.BlockSpec((G * BLK, 512), lambda i: (i, 0))],
        out_specs=pl.BlockSpec((G, 8, BLK), lambda i: (i, 0, 0)),
        out_shape=jax.ShapeDtypeStruct((nblk, 8, BLK), F32),
        compiler_params=_params(("parallel",)),
    )(dmix, o_f)


def _fox_bwd(proj, dmix, c, ctb, lse, delta, scatter=()):
    L = proj.shape[0]
    nblk, nu = _fox_units(L)
    scale = HEAD_LANES ** -0.5
    ns = len(scatter)

    steps = FOX_HEADS // (2 * FOX_PAIRS)

    def body(qkv_ref, do_ref, c_ref, ct_ref, lse_ref, dl_ref, *rest):
        s_in, (dp_ref, dc_ref, dcq_ref), s_out = rest[:ns], rest[ns:ns + 3], rest[ns + 3:2 * ns + 3]
        ktt, dqt, dk_acc, dv_acc, dcs_acc = rest[2 * ns + 3:2 * ns + 8]
        p = pl.program_id(0)
        heads = [(pp, e, 2 * FOX_PAIRS * p + 2 * pp + e) for pp in range(FOX_PAIRS) for e in range(2)]

        @pl.when(p == 0)
        def _():
            dc_ref[...] = jnp.zeros_like(dc_ref)
            dcq_ref[...] = jnp.zeros_like(dcq_ref)
            if ns:
                for cp in _scatter_copies(s_in, s_out, *rest[2 * ns + 8:]):
                    cp.start()

        sub8 = _iota((8, BLK), 0)
        masks = _fox_tile_masks()

        def pre(j, carry):
            off = pl.multiple_of(j * BLK, BLK)
            for pp in range(FOX_PAIRS):
                ktt[pp, j] = qkv_ref[pl.ds(off, BLK), pp * 384 + BLK:pp * 384 + 2 * BLK].astype(F32).T.astype(BF16)
                dqt[pp, j] = jnp.zeros((BLK, BLK), F32)
            return carry

        lax.fori_loop(0, nblk, pre, 0)

        def kv_pass(kblk, nk, n_later):
            klen = nk * BLK
            koff = pl.multiple_of(kblk * BLK, BLK)
            kt = [qkv_ref[pl.ds(koff, klen), pp * 384 + BLK:pp * 384 + 2 * BLK] for pp in range(FOX_PAIRS)]
            vtile = [qkv_ref[pl.ds(koff, klen), pp * 384 + 2 * BLK:pp * 384 + 3 * BLK] for pp in range(FOX_PAIRS)]
            ct = c_ref[pl.ds(koff, klen), :]
            klane = _iota((klen, BLK), 1)
            cs = [jnp.broadcast_to(jnp.sum(jnp.where(klane == h, ct, 0.0), axis=1, keepdims=True), (klen, WIDE * UNIT))
                  for _, _, h in heads]
            for pp in range(FOX_PAIRS):
                dk_acc[pp, 0:klen] = jnp.zeros((klen, BLK), F32)
                dv_acc[pp, 0:klen] = jnp.zeros((klen, BLK), F32)
            for hh in range(len(heads)):
                dcs_acc[hh, 0:klen] = jnp.zeros((klen, BLK), F32)

            def tile(qblk, nq, mask):
                qlen = nq * BLK
                if mask == "valid":
                    mask = _iota((klen, qlen), 0) >= N_PAD
                qoff = pl.multiple_of(qblk * BLK, BLK)
                qlane = _iota((qlen, BLK), 1)
                qs = [qkv_ref[pl.ds(qoff, qlen), pp * 384:pp * 384 + BLK].astype(F32) * scale for pp in range(FOX_PAIRS)]
                dot_ = [do_ref[pl.ds(qoff, qlen), pp * BLK:(pp + 1) * BLK] for pp in range(FOX_PAIRS)]
                stats = [[ref[qblk + a] for a in range(nq)] for ref in (ct_ref, lse_ref, dl_ref)]
                dcq = [jnp.zeros((8, BLK), F32) for _ in range(nq)]
                for hh, (pp, e, h) in enumerate(heads):
                    head = (qlane >> 6) == e
                    ct_row, lse_row, dl_row = [jnp.concatenate([_pick_row(t, h) for t in ts], axis=1) for ts in stats]
                    qm = jnp.where(head, qs[pp], 0.0).astype(BF16)
                    dom = jnp.where(head, dot_[pp], jnp.zeros_like(dot_[pp]))
                    t = _dot_nt(kt[pp], qm) - cs[hh][:, 0:qlen]
                    if mask is not None:
                        t = jnp.where(mask, t, NEG)
                    pr = jnp.exp(t + (ct_row - lse_row))
                    dv_acc[pp, 0:klen] = dv_acc[pp, 0:klen] + _dot(pr.astype(BF16), dom)
                    dsv = pr * (_dot_nt(vtile[pp], dom) - dl_row)
                    ds_b = dsv.astype(BF16)
                    dk_acc[pp, 0:klen] = dk_acc[pp, 0:klen] + _dot(ds_b, qm)
                    rows = slice(e * HEAD_LANES, (e + 1) * HEAD_LANES)
                    dq_t = _dot(ktt[pp, kblk, rows, :], ds_b[0:BLK])
                    for b in range(1, nk):
                        dq_t = dq_t + _dot(ktt[pp, kblk + b, rows, :], ds_b[b * BLK:(b + 1) * BLK])
                    key_side = dsv[:, 0:BLK]
                    for a in range(1, nq):
                        key_side = key_side + dsv[:, a * BLK:(a + 1) * BLK]
                    dcs_acc[hh, 0:klen] = dcs_acc[hh, 0:klen] + key_side
                    query_side = jnp.sum(dsv, axis=0, keepdims=True)
                    for a in range(nq):
                        cols = slice(a * BLK, (a + 1) * BLK)
                        dqt[pp, qblk + a, rows, :] = dqt[pp, qblk + a, rows, :] + dq_t[:, cols]
                        dcq[a] = dcq[a] + jnp.where(sub8 == h, query_side[:, cols], 0.0)
                for a in range(nq):
                    dcq_ref[qblk + a] = dcq_ref[qblk + a] + dcq[a]

            later_mask = "valid" if nk == 1 else None
            n_later = jnp.asarray(n_later, jnp.int32)
            n_wide = n_later // WIDE

            def later_wide(i, carry):
                tile(kblk + nk + 2 * WIDE * i, 2 * WIDE, later_mask)
                return carry

            tile(kblk, nk, masks["first"] if nk == 1 else masks["diag"])
            lax.fori_loop(0, n_wide, later_wide, 0)
            rest_blk = kblk + nk + 2 * WIDE * n_wide

            @pl.when((n_later & 2) != 0)
            def _():
                tile(rest_blk, 4, later_mask)

            @pl.when((n_later & 1) != 0)
            def _():
                tile(rest_blk + 2 * (n_later & 2), 2, later_mask)

            upd = jnp.zeros((klen, BLK), F32)
            for hh, (_, _, h) in enumerate(heads):
                upd = upd + jnp.where(klane == h, -jnp.sum(dcs_acc[hh, 0:klen], axis=1, keepdims=True), 0.0)
            dc_ref[pl.ds(koff, klen), :] = dc_ref[pl.ds(koff, klen), :] + upd
            for pp in range(FOX_PAIRS):
                dp_ref[pl.ds(koff, klen), pp * 384 + BLK:pp * 384 + 2 * BLK] = dk_acc[pp, 0:klen].astype(BF16)
                dp_ref[pl.ds(koff, klen), pp * 384 + 2 * BLK:pp * 384 + 3 * BLK] = dv_acc[pp, 0:klen].astype(BF16)

        kv_pass(0, 1, nu)

        def k_loop(u, carry):
            kv_pass(1 + 2 * u, 2, nu - 1 - u)
            return carry

        lax.fori_loop(0, nu, k_loop, 0)

        def flush(j, carry):
            off = pl.multiple_of(j * BLK, BLK)
            for pp in range(FOX_PAIRS):
                dp_ref[pl.ds(off, BLK), pp * 384:pp * 384 + BLK] = (dqt[pp, j].T * scale).astype(BF16)
            return carry

        lax.fori_loop(0, nblk, flush, 0)

        if ns:
            @pl.when(p == steps - 1)
            def _():
                copies = _scatter_copies(s_in, s_out, *rest[2 * ns + 8:])
                for cp in copies:
                    cp.wait_recv()
                for cp in copies:
                    cp.wait_send()

    width = 384 * FOX_PAIRS
    once = lambda shape, index: pl.BlockSpec(shape, index, pipeline_mode=pl.Buffered(1))
    stat = once((nblk, 8, BLK), lambda p: (0, 0, 0))
    return pl.pallas_call(
        body, name="b_fox", grid=(steps,),
        in_specs=[once((L, width), lambda p: (0, RET_W // width + p)),
                  once((L, FOX_PAIRS * BLK), lambda p: (0, 4 // FOX_PAIRS + p)),
                  once((L, BLK), lambda p: (0, 0)), stat, stat, stat] + [_ANY] * ns,
        out_specs=[pl.BlockSpec((L, width), lambda p: (0, p)), _full((L, BLK)), _full((nblk, 8, BLK))] + [_ANY] * ns,
        out_shape=[jax.ShapeDtypeStruct((L, FOX_W), BF16), jax.ShapeDtypeStruct((L, BLK), F32),
                   jax.ShapeDtypeStruct((nblk, 8, BLK), F32)] + _scatter_shapes(scatter),
        scratch_shapes=[pltpu.VMEM((FOX_PAIRS, nblk, BLK, BLK), BF16), pltpu.VMEM((FOX_PAIRS, nblk, BLK, BLK), F32),
                        pltpu.VMEM((FOX_PAIRS, UNIT, BLK), F32), pltpu.VMEM((FOX_PAIRS, UNIT, BLK), F32),
                        pltpu.VMEM((2 * FOX_PAIRS, UNIT, BLK), F32)]
        + _scatter_semaphores(ns),
        compiler_params=_params(("arbitrary",)),
    )(proj, dmix, c, ctb, lse, delta, *scatter)


def _fox_post(dc, dcq, ff, fb):
    L = dc.shape[0]
    nblk = L // BLK
    G = _block_group(nblk)
    steps = nblk // G

    def body(dc_ref, dcq_ref, ff_ref, b_ref, dff_ref, dffb_ref, dfb_ref, carry):
        @pl.when(pl.program_id(0) == 0)
        def _():
            carry[...] = jnp.zeros_like(carry)
            dfb_ref[...] = jnp.zeros_like(dfb_ref)

        tri = (_iota((BLK, BLK), 0) <= _iota((BLK, BLK), 1)).astype(BF16)
        live = _iota((BLK, BLK), 1) < FOX_HEADS
        run, dfb = carry[...], dfb_ref[...]
        for b in reversed(range(G)):
            rows = slice(b * BLK, (b + 1) * BLK)
            d = dc_ref[rows, :] + jnp.concatenate([dcq_ref[b], jnp.zeros((BLK - 8, BLK), F32)], axis=0).T
            hi, mid, lo = _split3(d)
            dlf = _dot(tri, hi) + _dot(tri, mid) + _dot(tri, lo) + run
            run = run + jnp.sum(d, axis=0, keepdims=True)
            z = ff_ref[rows, :] + b_ref[...]
            dff = jnp.where(live, dlf * jax.nn.sigmoid(-z), 0.0)
            dff_ref[rows, :] = dff
            dffb_ref[rows, :] = dff.astype(BF16)
            dfb = dfb + jnp.sum(dff, axis=0, keepdims=True)
        carry[...] = run
        dfb_ref[...] = dfb

    rev = lambda i: (steps - 1 - i, 0)
    return pl.pallas_call(
        body, name="b_foxpost", grid=(steps,),
        in_specs=[pl.BlockSpec((G * BLK, BLK), rev), pl.BlockSpec((G, 8, BLK), lambda i: (steps - 1 - i, 0, 0)),
                  pl.BlockSpec((G * BLK, BLK), rev), _full((1, BLK))],
        out_specs=[pl.BlockSpec((G * BLK, BLK), rev), pl.BlockSpec((G * BLK, BLK), rev), _full((1, BLK))],
        out_shape=[jax.ShapeDtypeStruct((L, BLK), F32), jax.ShapeDtypeStruct((L, BLK), BF16),
                   jax.ShapeDtypeStruct((1, BLK), F32)],
        scratch_shapes=[pltpu.VMEM((1, BLK), F32)],
        compiler_params=_params(("arbitrary",)),
    )(dc, dcq, ff, fb)


def _inproj_bwd(dpr, dpf, dffb, w_main, w_ff, h0, g, dh1):
    L = h0.shape[0]
    tm = _row_tile(L)

    def body(dpr_ref, dpf_ref, dff_ref, wm_ref, wf_ref, h_ref, g_ref, dh1_ref, dh0_ref, dg_ref):
        @pl.when(pl.program_id(0) == 0)
        def _():
            dg_ref[...] = jnp.zeros_like(dg_ref)

        dn = (_dot_nt(dpr_ref[...], wm_ref[:, 0:RET_W]) + _dot_nt(dpf_ref[...], wm_ref[:, RET_W:MAIN_W])
              + _dot_nt(dff_ref[...], wf_ref[...]))
        h = h_ref[...]
        r = lax.rsqrt(jnp.mean(h * h, axis=-1, keepdims=True) + EPS)
        yn = h * r
        dg_ref[...] = dg_ref[...] + jnp.sum(dn * yn, axis=0, keepdims=True)
        dyn = dn * g_ref[...]
        dh0_ref[...] = dh1_ref[...] + r * (dyn - yn * jnp.mean(dyn * yn, axis=-1, keepdims=True))

    rows = lambda w: pl.BlockSpec((tm, w), lambda i: (i, 0))
    return pl.pallas_call(
        body, name="b_inproj", grid=(L // tm,),
        in_specs=[rows(RET_W), rows(FOX_W), rows(BLK), _full((D_MODEL, MAIN_W)), _full((D_MODEL, BLK)),
                  rows(D_MODEL), _full((1, D_MODEL)), rows(D_MODEL)],
        out_specs=[rows(D_MODEL), _full((1, D_MODEL))],
        out_shape=[jax.ShapeDtypeStruct((L, D_MODEL), F32), jax.ShapeDtypeStruct((1, D_MODEL), F32)],
        compiler_params=_params(("arbitrary",)),
    )(dpr, dpf, dffb, w_main, w_ff, h0, g, dh1)


def _local_step(x, target, meta, attn_g, w_main, w_ff, fox_b, ret_g, w_out, ffn_g, w_up, conv_w, conv_b, w_down, final_g,
                late=None, mid=None):
    S = x.shape[0]
    L = S + PREFIX
    h0 = jnp.concatenate([jnp.zeros((N_PAD, D_MODEL), F32), meta, x], axis=0)
    tgt = jnp.concatenate([jnp.zeros((PREFIX, D_MODEL), F32), target], axis=0)
    fb = jnp.pad(fox_b, ((0, 0), (0, BLK - FOX_HEADS)))
    cos_t, sin_t = _rotary_tables(L)

    n1, proj, ff = _rms_inproj(h0, attn_g, w_main, w_ff)
    c, ctb = _fox_prep(ff, fb)
    mix_r, o_ret, states = _retention_fwd(proj, cos_t, sin_t, ret_g)
    if late is None:
        o_f, lse = _fox_fwd(proj, c, ctb)
    else:
        o_f, lse, *gathered = _fox_fwd(proj, c, ctb, gather=late[0])
        w_out, w_up, w_down = late[1](gathered)
    h1, n2, up, g_act, acc_saved = _outproj_up(mix_r, o_f, h0, w_out, ffn_g, w_up, conv_w, conv_b)
    dh2, dh2b, d_final_g, loss = _ffn_down_loss(g_act, w_down, h1, final_g, tgt)

    dacc, db = _ffn_bwd_gate(dh2b, w_down, acc_saved, up)
    dup, dh1, dh1b, dmix, d_ffn_g, dconv = _ffn_bwd_up(dacc, db, up, conv_w, w_up, h1, ffn_g, dh2, w_out)
    d_w_down = _wgrad(g_act, dh2b, "wgrad_down", tk=D_FF // 2)[0]
    d_w_up = _wgrad(n2, dup, "wgrad_up", tn=w_up.shape[2])
    d_w_out = jnp.concatenate([_wgrad(mix_r, dh1b, "wgrad_out_r")[0], _wgrad(o_f, dh1b, "wgrad_out_f")[0]], axis=0)

    early = () if mid is None else mid[0](d_w_out, d_w_up, d_w_down)
    dpr, d_ret_g, *from_sibling = _retention_bwd(dmix, o_ret, proj, cos_t, sin_t, ret_g, states, exchange=early)
    delta = _fox_delta(dmix, o_f)
    scatter = () if mid is None else mid[1](early, from_sibling)
    dpf, dc, dcq, *received = _fox_bwd(proj, dmix, c, ctb, lse, delta, scatter=scatter)
    dff, dffb, d_fox_b = _fox_post(dc, dcq, ff, fb)
    dh0, d_attn_g = _inproj_bwd(dpr, dpf, dffb, w_main, w_ff, h0, attn_g, dh1)
    d_w_main = jnp.concatenate([_wgrad(n1, dpr, "wgrad_in_r")[0], _wgrad(n1, dpf, "wgrad_in_f")[0]], axis=1)
    d_w_ff = _wgrad(n1, dffb, "wgrad_in_ff")[0]

    return dict(
        loss=loss[0, 0], dx=dh0[PREFIX:], dmeta=dh0[N_PAD:PREFIX], attn_g=d_attn_g, w_main=d_w_main,
        w_ff=d_w_ff[:, :FOX_HEADS], fox_b=d_fox_b[:, :FOX_HEADS], ret_g=d_ret_g, w_out=d_w_out, ffn_g=d_ffn_g,
        w_up=d_w_up, conv_w=dconv[0:3], conv_b=dconv[3:4], w_down=d_w_down, final_g=d_final_g,
        scatter=scatter, received=received)


_ANY = pl.BlockSpec(memory_space=pl.ANY)


def _place():
    return lax.axis_index("x"), lax.axis_index("y"), lax.axis_index("c")


def _other_chips(x, y):
    return [(1 - x, y), (x, 1 - y), (1 - x, 1 - y)]


def _allgather_semaphores(n):
    if n == 0:
        return []
    return [pltpu.SemaphoreType.DMA((3 * n,)), pltpu.SemaphoreType.DMA((3 * n,)), pltpu.SemaphoreType.DMA((n,))]


def _allgather_copies(ins, outs, send, recv, loc):
    n = len(ins)
    x, y, c = _place()
    mine = 2 * x + y
    peers = _other_chips(x, y)

    def remote(a, k, slot):
        return pltpu.make_async_remote_copy(
            src_ref=ins[a], dst_ref=outs[a].at[slot], send_sem=send.at[3 * a + k], recv_sem=recv.at[3 * a + k],
            device_id=(peers[k][0], peers[k][1], c), device_id_type=MESH)

    local = [pltpu.make_async_copy(ins[a], outs[a].at[mine], loc.at[a]) for a in range(n)]
    sends = [remote(a, k, mine) for a in range(n) for k in range(3)]
    recvs = [remote(a, k, 2 * peers[k][0] + peers[k][1]) for a in range(n) for k in range(3)]
    return local, sends, recvs


def _chip_allgather_halves(w, small):
    half = w.shape[0] // 2

    def body(w_ref, s_ref, wo_ref, so_ref, send, recv, fsend, frecv, ssend, srecv, loc):
        x, y, c = _place()
        mine = 2 * x + y
        peers = _other_chips(x, y)

        def fetch(k, slot):
            return pltpu.make_async_remote_copy(
                src_ref=w_ref.at[pl.ds(c * half, half)], dst_ref=wo_ref.at[slot, c], send_sem=send.at[k],
                recv_sem=recv.at[k], device_id=(peers[k][0], peers[k][1], c), device_id_type=MESH)

        def forward(k, which):
            slot = 2 * peers[k][0] + peers[k][1]
            return pltpu.make_async_remote_copy(
                src_ref=wo_ref.at[slot, which], dst_ref=wo_ref.at[slot, which], send_sem=fsend.at[k],
                recv_sem=frecv.at[k], device_id=(x, y, 1 - c), device_id_type=MESH)

        def small_copy(k, slot):
            return pltpu.make_async_remote_copy(
                src_ref=s_ref, dst_ref=so_ref.at[slot], send_sem=ssend.at[k], recv_sem=srecv.at[k],
                device_id=(peers[k][0], peers[k][1], c), device_id_type=MESH)

        local = pltpu.make_async_copy(s_ref, so_ref.at[mine], loc.at[0])
        sends = [fetch(k, mine) for k in range(3)] + [small_copy(k, mine) for k in range(3)]
        local.start()
        for cp in sends:
            cp.start()
        forwards = []
        for k in range(3):
            fetch(k, 2 * peers[k][0] + peers[k][1]).wait_recv()
            forwards.append(forward(k, c))
            forwards[-1].start()
        for k in range(3):
            forward(k, 1 - c).wait_recv()
            small_copy(k, 2 * peers[k][0] + peers[k][1]).wait_recv()
        for cp in sends + forwards:
            cp.wait_send()
        local.wait()

    three = pltpu.SemaphoreType.DMA((3,))
    return pl.pallas_call(
        body, name="ag_weights", in_specs=[_ANY] * 2, out_specs=[_ANY] * 2,
        out_shape=[jax.ShapeDtypeStruct((N_CHIPS, 2, half, w.shape[1]), w.dtype),
                   jax.ShapeDtypeStruct((N_CHIPS,) + small.shape, small.dtype)],
        scratch_shapes=[three, three, three, three, three, three, pltpu.SemaphoreType.DMA((1,))],
    )(w, small)


def _chip_allgather(arrays):
    n = len(arrays)

    def body(*refs):
        local, sends, recvs = _allgather_copies(refs[:n], refs[n:2 * n], *refs[2 * n:])
        for cp in local + sends:
            cp.start()
        for cp in recvs:
            cp.wait_recv()
        for cp in sends:
            cp.wait_send()
        for cp in local:
            cp.wait()

    return pl.pallas_call(
        body, name="ag_weights", in_specs=[_ANY] * n, out_specs=[_ANY] * n,
        out_shape=[jax.ShapeDtypeStruct((N_CHIPS,) + a.shape, a.dtype) for a in arrays],
        scratch_shapes=_allgather_semaphores(n),
    )(*arrays)


def _sibling_exchange(grads, small):
    n = len(grads)

    def body(*refs):
        ins, small_in = refs[:n], refs[n]
        outs, small_out = refs[n + 1:2 * n + 1], refs[2 * n + 1]
        send, recv, s_send, s_recv, loc = refs[2 * n + 2:]
        x, y, c = _place()
        me = 4 * x + 2 * y + c

        def half_copy(a, which):
            half = ins[a].shape[1] // 2
            return pltpu.make_async_remote_copy(
                src_ref=ins[a].at[pl.ds(0, N_CHIPS), pl.ds(which * half, half)], dst_ref=outs[a],
                send_sem=send.at[a], recv_sem=recv.at[a], device_id=(x, y, 1 - c), device_id_type=MESH)

        def peer_of(r):
            return tuple(1 - v if (r >> b) & 1 else v for v, b in ((x, 2), (y, 1), (c, 0)))

        def small_copy(r, slot):
            return pltpu.make_async_remote_copy(
                src_ref=small_in, dst_ref=small_out.at[slot], send_sem=s_send.at[r - 1], recv_sem=s_recv.at[r - 1],
                device_id=peer_of(r), device_id_type=MESH)

        local = pltpu.make_async_copy(small_in, small_out.at[me], loc.at[0])
        sends = [half_copy(a, 1 - c) for a in range(n)] + [small_copy(r, me) for r in range(1, N_DEV)]
        local.start()
        for cp in sends:
            cp.start()
        for r in range(1, N_DEV):
            px, py, pc = peer_of(r)
            small_copy(r, 4 * px + 2 * py + pc).wait_recv()
        for a in range(n):
            half_copy(a, c).wait_recv()
        for cp in sends:
            cp.wait_send()
        local.wait()

    rows = small.shape[0]
    return pl.pallas_call(
        body, name="rs_sibling", in_specs=[_ANY] * (n + 1), out_specs=[_ANY] * (n + 1),
        out_shape=[jax.ShapeDtypeStruct((N_CHIPS, g.shape[1] // 2, g.shape[2]), g.dtype) for g in grads]
        + [jax.ShapeDtypeStruct((N_DEV, rows, small.shape[1]), small.dtype)],
        scratch_shapes=[pltpu.SemaphoreType.DMA((n,)), pltpu.SemaphoreType.DMA((n,)),
                        pltpu.SemaphoreType.DMA((N_DEV - 1,)), pltpu.SemaphoreType.DMA((N_DEV - 1,)),
                        pltpu.SemaphoreType.DMA((1,))],
    )(*grads, small)


def _sibling_half_shapes(grads):
    return [jax.ShapeDtypeStruct((N_CHIPS, g.shape[1] // 2, g.shape[2]), g.dtype) for g in grads]


def _sibling_half_semaphores(n):
    return [pltpu.SemaphoreType.DMA((n,)), pltpu.SemaphoreType.DMA((n,))] if n else []


def _sibling_half_copies(ins, outs, send, recv):
    x, y, c = _place()

    def half_copy(a, which):
        half = ins[a].shape[1] // 2
        return pltpu.make_async_remote_copy(
            src_ref=ins[a].at[pl.ds(0, N_CHIPS), pl.ds(which * half, half)], dst_ref=outs[a],
            send_sem=send.at[a], recv_sem=recv.at[a], device_id=(x, y, 1 - c), device_id_type=MESH)

    return [half_copy(a, 1 - c) for a in range(len(ins))], [half_copy(a, c) for a in range(len(ins))]


def _chip_reduce_scatter(parts):
    n = len(parts)

    def body(*refs):
        copies = _scatter_copies(refs[:n], refs[n:2 * n], *refs[2 * n:])
        for cp in copies:
            cp.start()
        for cp in copies:
            cp.wait_recv()
        for cp in copies:
            cp.wait_send()

    return pl.pallas_call(
        body, name="rs_chips", in_specs=[_ANY] * n, out_specs=[_ANY] * n,
        out_shape=_scatter_shapes(parts), scratch_shapes=_scatter_semaphores(n),
    )(*parts)


def _scatter_shapes(parts):
    return [jax.ShapeDtypeStruct((3,) + p.shape[1:], p.dtype) for p in parts]


def _scatter_semaphores(n):
    return [pltpu.SemaphoreType.DMA((3 * n,)), pltpu.SemaphoreType.DMA((3 * n,))] if n else []


def _scatter_copies(ins, outs, send, recv):
    x, y, c = _place()
    peers = _other_chips(x, y)
    return [pltpu.make_async_remote_copy(
        src_ref=ins[a].at[2 * peers[k][0] + peers[k][1]], dst_ref=outs[a].at[k], send_sem=send.at[3 * a + k],
        recv_sem=recv.at[3 * a + k], device_id=(peers[k][0], peers[k][1], c), device_id_type=MESH)
        for a in range(len(ins)) for k in range(3)]


def _sibling_allgather(bufs):
    n = len(bufs)

    def body(*refs):
        outs = refs[n:2 * n]
        send, recv = refs[2 * n:]
        x, y, c = _place()

        def remote(a, which):
            return pltpu.make_async_remote_copy(
                src_ref=outs[a].at[which], dst_ref=outs[a].at[which], send_sem=send.at[a], recv_sem=recv.at[a],
                device_id=(x, y, 1 - c), device_id_type=MESH)

        sends = [remote(a, c) for a in range(n)]
        for cp in sends:
            cp.start()
        for a in range(n):
            remote(a, 1 - c).wait_recv()
        for cp in sends:
            cp.wait_send()

    outs = pl.pallas_call(
        body, name="ag_sibling", in_specs=[_ANY] * n, out_specs=[_ANY] * n,
        out_shape=[jax.ShapeDtypeStruct(b.shape, b.dtype) for b in bufs],
        input_output_aliases={a: a for a in range(n)},
        scratch_shapes=[pltpu.SemaphoreType.DMA((n,)), pltpu.SemaphoreType.DMA((n,))],
    )(*bufs)
    return [o.reshape(2 * o.shape[1], o.shape[2]) for o in outs]


def _pair_add(full, recv, core, name):
    _, R, C = full.shape
    half = R // 2

    def body(core_ref, a_ref, b_ref, o_ref):
        o_ref[...] = (a_ref[...] + b_ref[...]).astype(BF16)

    return pl.pallas_call(
        body, name=name,
        grid_spec=pltpu.PrefetchScalarGridSpec(
            num_scalar_prefetch=1, grid=(N_CHIPS,),
            in_specs=[pl.BlockSpec((1, half, C), lambda j, core_ref: (j, core_ref[0], 0)),
                      pl.BlockSpec((1, half, C), lambda j, core_ref: (j, 0, 0))],
            out_specs=pl.BlockSpec((1, half, C), lambda j, core_ref: (j, 0, 0))),
        out_shape=jax.ShapeDtypeStruct((N_CHIPS, half, C), BF16),
        compiler_params=_params(("parallel",)),
    )(core, full, recv)


def _sum_slots(q, name, tiles=2):
    n, R, C = q.shape
    tr = R // tiles

    def body(q_ref, o_ref):
        acc = q_ref[0].astype(F32)
        for j in range(1, n):
            acc = acc + q_ref[j].astype(F32)
        o_ref[...] = acc

    return pl.pallas_call(
        body, name=name, grid=(tiles,),
        in_specs=[pl.BlockSpec((n, tr, C), lambda i: (0, i, 0))],
        out_specs=pl.BlockSpec((tr, C), lambda i: (i, 0)),
        out_shape=jax.ShapeDtypeStruct((R, C), F32),
        compiler_params=_params(("parallel",)),
    )(q)


def _sum_partials(own_all, recv, place, name, tiles=2):
    _, R, C = own_all.shape
    tr = R // tiles

    def body(place_ref, own_ref, r_ref, o_ref):
        acc = own_ref[0].astype(F32)
        for k in range(3):
            acc = acc + r_ref[k].astype(F32)
        o_ref[0] = acc

    return pl.pallas_call(
        body, name=name,
        grid_spec=pltpu.PrefetchScalarGridSpec(
            num_scalar_prefetch=1, grid=(tiles,),
            in_specs=[pl.BlockSpec((1, tr, C), lambda i, place_ref: (place_ref[0], i, 0)),
                      pl.BlockSpec((3, tr, C), lambda i, place_ref: (0, i, 0))],
            out_specs=pl.BlockSpec((1, tr, C), lambda i, place_ref: (place_ref[1], i, 0))),
        out_shape=jax.ShapeDtypeStruct((2, R, C), F32),
        compiler_params=_params(("parallel",)),
    )(place, own_all, recv)


def _adamw(w, g, m, v, name, tiles=4):
    R, C = w.shape
    tr = R // tiles

    def body(w_ref, g_ref, m_ref, v_ref, go_ref, d_ref, m2_ref, v2_ref):
        g_ = g_ref[...]
        go_ref[...] = g_
        m2 = ADAM_B1 * m_ref[...] + (1.0 - ADAM_B1) * g_
        v2 = ADAM_B2 * v_ref[...] + (1.0 - ADAM_B2) * (g_ * g_)
        m_hat = m2 / (1.0 - ADAM_B1 ** ADAM_STEP)
        v_hat = v2 / (1.0 - ADAM_B2 ** ADAM_STEP)
        d_ref[...] = -ADAM_LR * (m_hat / (jnp.sqrt(v_hat) + ADAM_EPS) + ADAM_WD * w_ref[...])
        m2_ref[...] = m2
        v2_ref[...] = v2

    spec = pl.BlockSpec((tr, C), lambda i: (i, 0))
    return pl.pallas_call(
        body, name=name, grid=(tiles,), in_specs=[spec] * 4, out_specs=[spec] * 4,
        out_shape=[jax.ShapeDtypeStruct((R, C), F32)] * 4,
        compiler_params=_params(("parallel",)),
    )(w, g, m, v)


def _pack_rows(pieces, rows):
    flat = jnp.concatenate([jnp.pad(p.reshape(-1).astype(F32), (0, (-p.size) % D_MODEL)) for p in pieces])
    return jnp.pad(flat, (0, rows * D_MODEL - flat.size)).reshape(rows, D_MODEL)


def _unpack_rows(pack, shapes):
    flat = pack.reshape(-1)
    out, off = [], 0
    for shp in shapes:
        size = int(np.prod(shp))
        out.append(flat[off:off + size].reshape(shp))
        off += size + (-size) % D_MODEL
    return out


def _kernel_order(w):
    parts = [w[:, 0:RET_W]]
    for p in range(FOX_HEADS // 2):
        parts += [w[:, RET_W + part * 512 + p * BLK:RET_W + part * 512 + (p + 1) * BLK] for part in range(3)]
    return jnp.concatenate(parts, axis=1)


def _reference_order(g_main, g_ff):
    parts = [g_main[:, 0:RET_W]]
    for part in range(3):
        parts += [g_main[:, RET_W + 384 * p + part * BLK:RET_W + 384 * p + (part + 1) * BLK] for p in range(FOX_HEADS // 2)]
    return jnp.concatenate(parts + [g_ff], axis=1)


def kernel(x, meta_tokens, attn_norm_g, w_in, fox_forget_b, ret_norm_g, w_out, ffn_norm_g, w_up, conv_w, conv_b, w_down, final_norm_g, loss_target, m_meta_tokens, m_attn_norm_g, m_w_in, m_fox_forget_b, m_ret_norm_g, m_w_out, m_ffn_norm_g, m_w_up, m_conv_w, m_conv_b, m_w_down, m_final_norm_g, v_meta_tokens, v_attn_norm_g, v_w_in, v_fox_forget_b, v_ret_norm_g, v_w_out, v_ffn_norm_g, v_w_up, v_conv_w, v_conv_b, v_w_down, v_final_norm_g):
    chip = 2 * lax.axis_index("x") + lax.axis_index("y")
    core = lax.axis_index("c")
    meta_w, conv_sw = meta_tokens.shape[1], conv_w.shape[2]

    small_w = _pack_rows([meta_tokens, conv_w[0]], 8)
    w_in_b = w_in[0].astype(BF16)
    g_in, g_small = _chip_allgather_halves(w_in_b, small_w)
    g_in = lax.dynamic_update_slice(g_in.reshape((N_CHIPS,) + w_in_b.shape), w_in_b[None], (chip, 0, 0))
    w_in_full = g_in.transpose(1, 0, 2).reshape(D_MODEL, IN_WIDTH)
    w_main = _kernel_order(w_in_full)
    w_ff = jnp.pad(w_in_full[:, MAIN_W:], ((0, 0), (0, BLK - FOX_HEADS)))
    small_parts = [_unpack_rows(g_small[j], [meta_tokens.shape, conv_w.shape[1:]]) for j in range(N_CHIPS)]
    meta_full = jnp.concatenate([sp[0] for sp in small_parts], axis=1)
    conv_w_full = jnp.concatenate([sp[1] for sp in small_parts], axis=1)

    core_idx = core.reshape(1).astype(jnp.int32)
    place = jnp.stack([chip, core]).astype(jnp.int32)

    def assemble(gathered):
        g_out, g_up, g_down = gathered
        return g_out.reshape(D_MODEL, D_MODEL), g_up, g_down.reshape(D_FF, D_MODEL)

    def early_arrays(d_w_out, d_w_up, d_w_down):
        return [d_w_out.reshape(N_CHIPS, -1, D_MODEL), d_w_up, d_w_down.reshape(N_CHIPS, -1, D_MODEL)]

    def early_sums(early, from_sib):
        return [_pair_add(g, r, core_idx, "pair_add_" + nm) for g, r, nm in zip(early, from_sib, ("out", "up", "down"))]

    out = _local_step(x[0], loss_target[0], meta_full, attn_norm_g, w_main, w_ff, fox_forget_b, ret_norm_g,
                      None, ffn_norm_g, None, conv_w_full, conv_b, None, final_norm_g[None],
                      late=([w_out[0].astype(BF16), w_up[0].astype(BF16), w_down[0].astype(BF16)], assemble),
                      mid=(early_arrays, early_sums))

    g_in_full = _reference_order(out["w_main"], out["w_ff"]).reshape(D_MODEL, N_CHIPS, -1).transpose(1, 0, 2)
    small_shapes = [(1, D_MODEL), (1, D_MODEL), (1, D_MODEL), (1, 512 + FOX_HEADS + 1), (1, D_FF), (N_META, D_MODEL), (3, D_FF)]
    small = _pack_rows([out["attn_g"], out["ffn_g"], out["final_g"],
                        jnp.concatenate([out["ret_g"], out["fox_b"], out["loss"].reshape(1, 1)], axis=1),
                        out["conv_b"], out["dmeta"], out["conv_w"]], 32)
    from_sibling_in, small_all = _sibling_exchange([g_in_full], small)
    sum_in = _pair_add(g_in_full, from_sibling_in, core_idx, "pair_add_in")
    (from_chips_in,) = _chip_reduce_scatter([sum_in])
    chip_sums = [sum_in] + list(out["scatter"])
    from_chips = [from_chips_in] + list(out["received"])
    names = ("in", "out", "up", "down")
    totals = [_sum_partials(s, q, place, "sum_chips_" + nm) for s, q, nm in zip(chip_sums, from_chips, names)]
    grad_in, grad_out, grad_up, grad_down = _sibling_allgather(totals)
    s_attn, s_ffn, s_final, s_misc, s_conv_b, s_meta, s_conv_w = _unpack_rows(
        _sum_slots(small_all, "sum_small", tiles=1), small_shapes)
    loss = s_misc[0, 512 + FOX_HEADS]
    small_grads = [lax.dynamic_slice_in_dim(s_meta, chip * meta_w, meta_w, axis=1), s_attn, s_misc[:, 512:512 + FOX_HEADS],
                   s_misc[:, :512], s_ffn, lax.dynamic_slice_in_dim(s_conv_w, chip * conv_sw, conv_sw, axis=1)[None],
                   s_conv_b, s_final[0]]

    big_w = [(w_in, m_w_in, v_w_in, grad_in, "adamw_in"), (w_out, m_w_out, v_w_out, grad_out, "adamw_out"),
             (w_up, m_w_up, v_w_up, grad_up, "adamw_up"), (w_down, m_w_down, v_w_down, grad_down, "adamw_down")]
    big_res = [[r[None] for r in _adamw(w[0], g, m[0], v[0], nm)] for w, m, v, g, nm in big_w]
    small_w_list = [meta_tokens, attn_norm_g, fox_forget_b, ret_norm_g, ffn_norm_g, conv_w, conv_b, final_norm_g]
    small_m = [m_meta_tokens, m_attn_norm_g, m_fox_forget_b, m_ret_norm_g, m_ffn_norm_g, m_conv_w, m_conv_b, m_final_norm_g]
    small_v = [v_meta_tokens, v_attn_norm_g, v_fox_forget_b, v_ret_norm_g, v_ffn_norm_g, v_conv_w, v_conv_b, v_final_norm_g]
    shapes = [a.shape for a in small_w_list]
    packs = [_pack_rows(lst, 16) for lst in (small_w_list, small_grads, small_m, small_v)]
    small_res = [_unpack_rows(r, shapes) for r in _adamw(*packs, "adamw_small", tiles=1)[1:]]
    small_grads = [g.reshape(s) for g, s in zip(small_grads, shapes)]

    def ordered(kind):
        sm = small_grads if kind == 0 else small_res[kind - 1]
        bg = [r[kind] for r in big_res]
        return [sm[0], sm[1], bg[0], sm[2], sm[3], bg[1], sm[4], bg[2], sm[5], sm[6], bg[3], sm[7]]

    return (loss, out["dx"][None], *ordered(0), *ordered(1), *ordered(2), *ordered(3))
```

```python
import functools

import numpy as np
import jax
import jax.numpy as jnp
from jax import lax
from jax.experimental import pallas as pl
from jax.experimental.pallas import tpu as pltpu

F32 = jnp.float32
BF16 = jnp.bfloat16

D_MODEL = 1024
N_META = 16
BLK = 128
UNIT = 2 * BLK
FOX_PAIRS = 2
WIDE = 4
CHUNK = 64
N_PAD = BLK - N_META
PREFIX = BLK
RET_HEADS = 4
FOX_HEADS = 8
HEAD_LANES = 64
D_FF = 2816
ROPE_BASE = 10000.0
EPS = 1e-6
NEG = -1e30
LOG2E = 1.4426950408889634
RET_W = 1536
FOX_W = 1536
MAIN_W = RET_W + FOX_W
IN_WIDTH = MAIN_W + FOX_HEADS
N_CHIPS = 4
N_DEV = 8

ADAM_LR = 0.001
ADAM_B1 = 0.9
ADAM_B2 = 0.999
ADAM_EPS = 1e-08
ADAM_WD = 0.01
ADAM_STEP = 10

MESH = pl.DeviceIdType.MESH
VMEM_LIMIT_MB = 56

_NT = (((1,), (1,)), ((), ()))
_TN = (((0,), (0,)), ((), ()))


def _dot(a, b):
    return jnp.dot(a, b, preferred_element_type=F32)


def _dot_nt(a, b):
    return lax.dot_general(a, b, _NT, preferred_element_type=F32)


def _dot_tn(a, b):
    return lax.dot_general(a, b, _TN, preferred_element_type=F32)


def _params(dims=None, vmem_mb=VMEM_LIMIT_MB):
    kw = dict(vmem_limit_bytes=vmem_mb << 20)
    if dims is not None:
        kw["dimension_semantics"] = dims
    return pltpu.CompilerParams(**kw)


def _row_tile(n, prefs=(384, 256, 128)):
    for t in prefs:
        if n % t == 0:
            return t
    raise ValueError(f"no row tile for {n}")


def _iota(shape, dim):
    return lax.broadcasted_iota(jnp.int32, shape, dim)


def _pick_row(tile, row):
    sub = _iota(tile.shape, 0)
    return jnp.sum(jnp.where(sub == row, tile, 0.0), axis=0, keepdims=True)


def _split3(x):
    hi = x.astype(BF16)
    r1 = x - hi.astype(F32)
    mid = r1.astype(BF16)
    lo = (r1 - mid.astype(F32)).astype(BF16)
    return hi, mid, lo


def _full(shape):
    nd = len(shape)
    return pl.BlockSpec(shape, lambda *_: (0,) * nd)


def _in_perm():
    cols = list(range(RET_W))
    for p in range(FOX_HEADS // 2):
        for part in range(3):
            start = RET_W + part * 512 + p * BLK
            cols += list(range(start, start + BLK))
    return np.asarray(cols, np.int32)


def _rotary_tables(L):
    half = HEAD_LANES // 2
    inv = 1.0 / (ROPE_BASE ** (jnp.arange(half, dtype=F32) / half))
    ang = jnp.arange(L).astype(F32)[:, None] * inv[None, :]
    cos, sin = jnp.cos(ang), jnp.sin(ang)
    cos_t = jnp.tile(cos, (1, 4))
    sin_t = jnp.tile(jnp.concatenate([-sin, sin], axis=1), (1, 2))
    return cos_t, sin_t


def _decay_tables():
    gam = 1.0 - 2.0 ** (-5.0 - np.arange(RET_HEADS, dtype=np.float64))
    n = np.arange(BLK)
    same_or_past = (n[:, None] // CHUNK) >= (n[None, :] // CHUNK)
    dist = np.abs(n[:, None] - n[None, :])
    dmat = np.stack([np.where(same_or_past, g ** dist, 0.0) for g in gam]).astype(np.float32)
    lane_head = np.arange(BLK) // HEAD_LANES
    wq = np.stack([gam[2 * p + lane_head][None, :] ** (n[:, None] + 1.0) for p in range(2)]).astype(np.float32)
    wk = np.stack([gam[2 * p + lane_head][None, :] ** (BLK - 1.0 - n[:, None]) for p in range(2)]).astype(np.float32)
    g_blk = tuple(float(g ** BLK) for g in gam)
    return jnp.asarray(dmat), jnp.asarray(wq), jnp.asarray(wk), g_blk


def _shifted_blocks(tm):
    nb = tm // BLK
    return [pl.BlockSpec((BLK, D_MODEL), lambda i, j=j: (jnp.maximum(nb * i + j - 1, 0), 0)) for j in range(nb)]


def _rms_inproj(head, x, g, w_main, w_ff):
    L = x.shape[0] + BLK
    tm = _row_tile(L)
    nb = tm // BLK

    def body(head_ref, *refs):
        x_refs, (g_ref, wm_ref, wf_ref, h_ref, n_ref, p_ref, ff_ref) = refs[:nb], refs[nb:]
        parts = [r[...] for r in x_refs]
        parts[0] = jnp.where(pl.program_id(0) == 0, head_ref[...], parts[0])
        h = jnp.concatenate(parts, axis=0)
        h_ref[...] = h
        r = lax.rsqrt(jnp.mean(h * h, axis=-1, keepdims=True) + EPS)
        n = (h * r * g_ref[...]).astype(BF16)
        n_ref[...] = n
        p_ref[...] = _dot(n, wm_ref[...]).astype(BF16)
        ff_ref[...] = _dot(n, wf_ref[...])

    rows = lambda w: pl.BlockSpec((tm, w), lambda i: (i, 0))
    return pl.pallas_call(
        body, name="f_inproj", grid=(L // tm,),
        in_specs=[_full((BLK, D_MODEL))] + _shifted_blocks(tm)
        + [_full((1, D_MODEL)), _full((D_MODEL, MAIN_W)), _full((D_MODEL, BLK))],
        out_specs=[rows(D_MODEL), rows(D_MODEL), rows(MAIN_W), rows(BLK)],
        out_shape=[jax.ShapeDtypeStruct((L, D_MODEL), F32), jax.ShapeDtypeStruct((L, D_MODEL), BF16),
                   jax.ShapeDtypeStruct((L, MAIN_W), BF16), jax.ShapeDtypeStruct((L, BLK), F32)],
        compiler_params=_params(("parallel",)),
    )(head, *([x] * nb), g, w_main, w_ff)


def _block_group(nblk):
    return 3 if nblk % 3 == 0 else 1


def _fox_prep(ff, fb):
    L = ff.shape[0]
    nblk = L // BLK
    G = _block_group(nblk)

    def body(ff_ref, b_ref, c_ref, ct_ref, carry):
        @pl.when(pl.program_id(0) == 0)
        def _():
            carry[...] = jnp.zeros_like(carry)

        tri = (_iota((BLK, BLK), 0) >= _iota((BLK, BLK), 1)).astype(BF16)
        live = _iota((BLK, BLK), 1) < FOX_HEADS
        run = carry[...]
        for b in range(G):
            z = ff_ref[b * BLK:(b + 1) * BLK, :] + b_ref[...]
            lf = jnp.where(live, jnp.minimum(z, 0.0) - jnp.log1p(jnp.exp(-jnp.abs(z))), 0.0)
            hi, mid, lo = _split3(lf)
            cs = (_dot(tri, hi) + _dot(tri, mid) + _dot(tri, lo) + run) * LOG2E
            c_ref[b * BLK:(b + 1) * BLK, :] = cs
            ct_ref[b] = cs.T[0:8, :]
            run = run + jnp.sum(lf, axis=0, keepdims=True)
        carry[...] = run

    return pl.pallas_call(
        body, name="f_foxprep", grid=(nblk // G,),
        in_specs=[pl.BlockSpec((G * BLK, BLK), lambda i: (i, 0)), _full((1, BLK))],
        out_specs=[pl.BlockSpec((G * BLK, BLK), lambda i: (i, 0)), pl.BlockSpec((G, 8, BLK), lambda i: (i, 0, 0))],
        out_shape=[jax.ShapeDtypeStruct((L, BLK), F32), jax.ShapeDtypeStruct((nblk, 8, BLK), F32)],
        scratch_shapes=[pltpu.VMEM((1, BLK), F32)],
        compiler_params=_params(("arbitrary",)),
    )(ff, fb)


def _rot_fns(cos, sin):
    lane = _iota((BLK, BLK), 1)
    first = (lane & (HEAD_LANES - 1)) < HEAD_LANES // 2

    def swap(x):
        return jnp.where(first, pltpu.roll(x, BLK - 32, 1), pltpu.roll(x, 32, 1))

    def rot(x):
        return x * cos + swap(x) * sin

    def rot_t(dy):
        return dy * cos + swap(dy * sin)

    return rot, rot_t


def _retention_fwd(proj, cos_t, sin_t, ret_g):
    L = proj.shape[0]
    nblk = L // BLK
    G = _block_group(nblk)
    dmat, wq_t, wk_t, g_blk = _decay_tables()

    def body(q_ref, k_ref, v_ref, gate_ref, cos_ref, sin_ref, d_ref, wq_ref, wk_ref, rg_ref,
             mix_ref, o_ref, rs_ref, state):
        @pl.when(pl.program_id(0) == 0)
        def _():
            state[...] = jnp.zeros_like(state)

        lane = _iota((BLK, BLK), 1)
        sub = _iota((BLK, BLK), 0)
        for b in range(G):
            rows = slice(b * BLK, (b + 1) * BLK)
            rot, _ = _rot_fns(cos_ref[rows, :], sin_ref[rows, :])
            for p in range(2):
                qr = rot(q_ref[rows, p * BLK:(p + 1) * BLK].astype(F32))
                kr = rot(k_ref[rows, p * BLK:(p + 1) * BLK].astype(F32)) * (HEAD_LANES ** -0.5)
                kr_b = kr.astype(BF16)
                qw = (qr * wq_ref[p]).astype(BF16)
                kw = (kr * wk_ref[p]).astype(BF16)
                for e in range(2):
                    h = 2 * p + e
                    cols = slice(h * BLK, (h + 1) * BLK)
                    qm = jnp.where((lane >> 6) == e, qr, 0.0).astype(BF16)
                    s = _dot_nt(qm, kr_b) * d_ref[h]
                    vh = v_ref[rows, cols]
                    st = state[h]
                    rs_ref[b, h] = st
                    o = _dot(s.astype(BF16), vh) + _dot(qw, st.astype(BF16))
                    u = jnp.where((sub >> 6) == e, _dot_tn(kw, vh), 0.0)
                    state[h] = g_blk[h] * st + u
                    rn = lax.rsqrt(jnp.mean(o * o, axis=-1, keepdims=True) + EPS)
                    gate = gate_ref[rows, cols].astype(F32)
                    o_ref[rows, cols] = o
                    mix_ref[rows, cols] = (o * rn * rg_ref[:, cols] * (gate * jax.nn.sigmoid(gate))).astype(BF16)

    row = lambda c: (lambda i: (i, c))
    return pl.pallas_call(
        body, name="f_retention", grid=(nblk // G,),
        in_specs=[pl.BlockSpec((G * BLK, 256), row(0)), pl.BlockSpec((G * BLK, 256), row(1)),
                  pl.BlockSpec((G * BLK, 512), row(1)), pl.BlockSpec((G * BLK, 512), row(2)),
                  pl.BlockSpec((G * BLK, BLK), row(0)), pl.BlockSpec((G * BLK, BLK), row(0)),
                  _full((RET_HEADS, BLK, BLK)), _full((2, BLK, BLK)), _full((2, BLK, BLK)), _full((1, 512))],
        out_specs=[pl.BlockSpec((G * BLK, 512), row(0)), pl.BlockSpec((G * BLK, 512), row(0)),
                   pl.BlockSpec((G, RET_HEADS, BLK, BLK), lambda i: (i, 0, 0, 0))],
        out_shape=[jax.ShapeDtypeStruct((L, 512), BF16), jax.ShapeDtypeStruct((L, 512), F32),
                   jax.ShapeDtypeStruct((nblk, RET_HEADS, BLK, BLK), F32)],
        scratch_shapes=[pltpu.VMEM((RET_HEADS, BLK, BLK), F32)],
        compiler_params=_params(("arbitrary",)),
    )(proj, proj, proj, proj, cos_t, sin_t, dmat, wq_t, wk_t, ret_g)


def _fox_units(L):
    nblk = L // BLK
    assert L % BLK == 0 and nblk % 2 == 1, "sequence must be one 128-row block plus whole 256-row tiles"
    return nblk, (nblk - 1) // 2


def _fox_tile_masks():
    sub, lane = _iota((BLK, BLK), 0), _iota((BLK, BLK), 1)
    valid = _iota((BLK, UNIT), 0) >= N_PAD
    diag = _iota((UNIT, UNIT), 0) <= _iota((UNIT, UNIT), 1)
    r, q = _iota((BLK + UNIT, UNIT), 0), _iota((BLK + UNIT, UNIT), 1)
    first_and_diag = ((r < BLK) & (r >= N_PAD)) | ((r >= BLK) & (r - BLK <= q))
    return dict(first=(sub <= lane) & (sub >= N_PAD), valid=valid, diag=diag, first_and_diag=first_and_diag)


def _fox_fwd(proj, c, ctb, gather=()):
    L = proj.shape[0]
    nblk, nu = _fox_units(L)
    scale = HEAD_LANES ** -0.5 * LOG2E
    ng = len(gather)
    steps = FOX_HEADS // (2 * FOX_PAIRS)

    def body(qkv_ref, c_ref, ct_ref, *rest):
        g_in, (of_ref, lse_ref), g_out = rest[:ng], rest[ng:ng + 2], rest[ng + 2:2 * ng + 2]
        vt, csb = rest[2 * ng + 2:2 * ng + 4]
        p = pl.program_id(0)
        heads = [(pp, e, 2 * FOX_PAIRS * p + 2 * pp + e) for pp in range(FOX_PAIRS) for e in range(2)]

        @pl.when(p == 0)
        def _():
            lse_ref[...] = jnp.zeros_like(lse_ref)
            if ng:
                local, sends, _ = _allgather_copies(g_in, g_out, *rest[2 * ng + 4:])
                for cp in local + sends:
                    cp.start()

        lane = _iota((BLK, BLK), 1)
        sub8 = _iota((8, BLK), 0)
        masks = _fox_tile_masks()

        def pre(j, carry):
            off = pl.multiple_of(j * BLK, BLK)
            ct = c_ref[pl.ds(off, BLK), :]
            for pp in range(FOX_PAIRS):
                vt[pp, j] = qkv_ref[pl.ds(off, BLK), pp * 384 + 2 * BLK:pp * 384 + 3 * BLK].astype(F32).T.astype(BF16)
            for hh, (_, _, h) in enumerate(heads):
                col = jnp.sum(jnp.where(lane == h, ct, 0.0), axis=1, keepdims=True)
                csb[hh, j] = jnp.broadcast_to(col, (BLK, BLK))
            return carry

        lax.fori_loop(0, nblk, pre, 0)

        def attend(qblk, nq, n_whole):
            qlen = nq * BLK
            qoff = pl.multiple_of(qblk * BLK, BLK)
            qlane = _iota((qlen, BLK), 1)
            qs = [qkv_ref[pl.ds(qoff, qlen), pp * 384:pp * 384 + BLK].astype(F32) * scale for pp in range(FOX_PAIRS)]
            qm = [jnp.where((qlane >> 6) == e, qs[pp], 0.0).astype(BF16) for pp, e, _ in heads]
            ct_row = [jnp.concatenate([_pick_row(ct_ref[qblk + a], h) for a in range(nq)], axis=1) for _, _, h in heads]

            def step(segs, mask, st):
                blocks = [kblk + b for kblk, nk in segs for b in range(nk)]
                kts = []
                for pp in range(FOX_PAIRS):
                    kt = [qkv_ref[pl.ds(pl.multiple_of(kblk * BLK, BLK), nk * BLK), pp * 384 + BLK:pp * 384 + 2 * BLK]
                          for kblk, nk in segs]
                    kts.append(kt[0] if len(kt) == 1 else jnp.concatenate(kt, axis=0))
                out = []
                for hh, (pp, e, _) in enumerate(heads):
                    m, l, acc = st[3 * hh:3 * hh + 3]
                    s = _dot_nt(kts[pp], qm[hh])
                    t = jnp.concatenate([s[b * BLK:(b + 1) * BLK] - jnp.concatenate([csb[hh, blk]] * nq, axis=1)
                                         for b, blk in enumerate(blocks)], axis=0)
                    if mask is not None:
                        t = jnp.where(mask, t, NEG)
                    m_new = jnp.maximum(m, jnp.max(t, axis=0, keepdims=True) + ct_row[hh])
                    alpha = jnp.exp2(m - m_new)
                    pr = jnp.exp2(t - (m_new - ct_row[hh]))
                    l = alpha * l + jnp.sum(pr, axis=0, keepdims=True)
                    pr_b = pr.astype(BF16)
                    pv = None
                    for b, blk in enumerate(blocks):
                        part = _dot(vt[pp, blk, e * HEAD_LANES:(e + 1) * HEAD_LANES, :], pr_b[b * BLK:(b + 1) * BLK])
                        pv = part if pv is None else pv + part
                    out += [m_new, l, alpha * acc + pv]
                return tuple(out)

            st = (jnp.full((1, qlen), NEG, F32), jnp.zeros((1, qlen), F32),
                  jnp.zeros((HEAD_LANES, qlen), F32)) * len(heads)
            if nq == 1:
                st = step([(0, 1)], masks["first"], st)
            else:
                st = step([(0, 1), (qblk, 2)], masks["first_and_diag"], st)
                n_wide = n_whole // WIDE
                st = lax.fori_loop(0, n_wide, lambda j, s_: step([(1 + 2 * WIDE * j, 2 * WIDE)], None, s_), st)
                rest = 1 + 2 * WIDE * n_wide
                st = lax.cond((n_whole & 2) != 0, lambda s_: step([(rest, 4)], None, s_), lambda s_: s_, st)
                st = lax.cond((n_whole & 1) != 0, lambda s_: step([(rest + 2 * (n_whole & 2), 2)], None, s_),
                              lambda s_: s_, st)
            for pp in range(FOX_PAIRS):
                lo, hi = st[6 * pp:6 * pp + 3], st[6 * pp + 3:6 * pp + 6]
                o_t = jnp.concatenate([lo[2] * (1.0 / lo[1]), hi[2] * (1.0 / hi[1])], axis=0)
                of_ref[pl.ds(qoff, qlen), pp * BLK:(pp + 1) * BLK] = o_t.T.astype(BF16)
            lse = [st[3 * hh] + jnp.log(st[3 * hh + 1]) * LOG2E for hh in range(len(heads))]
            for a in range(nq):
                upd = jnp.zeros((8, BLK), F32)
                for hh, (_, _, h) in enumerate(heads):
                    upd = upd + jnp.where(sub8 == h, lse[hh][:, a * BLK:(a + 1) * BLK], 0.0)
                lse_ref[qblk + a] = lse_ref[qblk + a] + upd

        attend(0, 1, 0)

        def q_loop(u, carry):
            attend(1 + 2 * u, 2, u)
            return carry

        lax.fori_loop(0, nu, q_loop, 0)

        if ng:
            @pl.when(p == steps - 1)
            def _():
                local, sends, recvs = _allgather_copies(g_in, g_out, *rest[2 * ng + 4:])
                for cp in recvs:
                    cp.wait_recv()
                for cp in sends:
                    cp.wait_send()
                for cp in local:
                    cp.wait()

    width = 384 * FOX_PAIRS
    return pl.pallas_call(
        body, name="f_fox", grid=(steps,),
        in_specs=[pl.BlockSpec((L, width), lambda p: (0, RET_W // width + p)), _full((L, BLK)), _full((nblk, 8, BLK))]
        + [_ANY] * ng,
        out_specs=[pl.BlockSpec((L, FOX_PAIRS * BLK), lambda p: (0, p)), _full((nblk, 8, BLK))] + [_ANY] * ng,
        out_shape=[jax.ShapeDtypeStruct((L, 512), BF16), jax.ShapeDtypeStruct((nblk, 8, BLK), F32)]
        + [jax.ShapeDtypeStruct((N_CHIPS,) + a.shape, a.dtype) for a in gather],
        scratch_shapes=[pltpu.VMEM((FOX_PAIRS, nblk, BLK, BLK), BF16), pltpu.VMEM((2 * FOX_PAIRS, nblk, BLK, BLK), F32)]
        + _allgather_semaphores(ng),
        compiler_params=_params(("arbitrary",)),
    )(proj, c, ctb, *gather)


def _outproj_up(mix_r, o_f, h0, w_out, ffn_g, w_up, conv_w, conv_b):
    L = h0.shape[0]
    tm = _row_tile(L)
    shard = w_up.shape[2]
    assert 2 * shard == D_FF
    cw = [conv_w[j:j + 1] for j in range(3)]
    resident = lambda shape: pl.BlockSpec(shape, lambda i: (0,) * len(shape), pipeline_mode=pl.Buffered(1))

    def body(mr_ref, of_ref, h0_ref, wo_ref, g_ref, wu_ref, cw0, cw1, cw2, cb_ref,
             h1_ref, n2_ref, up_ref, act_ref, acc_ref, halo):
        i = pl.program_id(0)

        @pl.when(i == 0)
        def _():
            halo[...] = jnp.zeros_like(halo)

        h1 = h0_ref[...] + _dot(mr_ref[...], wo_ref[0:512, :]) + _dot(of_ref[...], wo_ref[512:1024, :])
        h1_ref[...] = h1
        r = lax.rsqrt(jnp.mean(h1 * h1, axis=-1, keepdims=True) + EPS)
        n2 = (h1 * r * g_ref[...]).astype(BF16)
        n2_ref[...] = n2
        live = i * tm + _iota((tm, 1), 0) >= N_PAD
        for half in range(2):
            cols = slice(half * shard, (half + 1) * shard)
            a_b = _dot(n2, wu_ref[half]).astype(BF16)
            b_b = _dot(n2, wu_ref[2 + half]).astype(BF16)
            up_ref[:, cols] = a_b
            up_ref[:, D_FF + half * shard:D_FF + (half + 1) * shard] = b_b
            a = jnp.where(live, a_b.astype(F32), 0.0)
            _, _, acc = _conv_taps(a, halo[:, cols], [cw0[:, cols], cw1[:, cols], cw2[:, cols]], cb_ref[:, cols])
            act_ref[:, cols] = (acc * jax.nn.sigmoid(acc) * b_b.astype(F32)).astype(BF16)
            acc_ref[:, cols] = acc.astype(BF16)
            halo[:, cols] = a[tm - 8:tm, :]

    rows = lambda w: pl.BlockSpec((tm, w), lambda i: (i, 0))
    return pl.pallas_call(
        body, name="f_outproj_up", grid=(L // tm,),
        in_specs=[rows(512), rows(512), rows(D_MODEL), resident((D_MODEL, D_MODEL)), _full((1, D_MODEL)),
                  resident((N_CHIPS, D_MODEL, shard)), _full((1, D_FF)), _full((1, D_FF)), _full((1, D_FF)),
                  _full((1, D_FF))],
        out_specs=[rows(D_MODEL), rows(D_MODEL), rows(2 * D_FF), rows(D_FF), rows(D_FF)],
        out_shape=[jax.ShapeDtypeStruct((L, D_MODEL), F32), jax.ShapeDtypeStruct((L, D_MODEL), BF16),
                   jax.ShapeDtypeStruct((L, 2 * D_FF), BF16), jax.ShapeDtypeStruct((L, D_FF), BF16),
                   jax.ShapeDtypeStruct((L, D_FF), BF16)],
        scratch_shapes=[pltpu.VMEM((8, D_FF), F32)],
        compiler_params=_params(("arbitrary",)),
    )(mix_r, o_f, h0, w_out, ffn_g, w_up, cw[0], cw[1], cw[2], conv_b)


def _conv_taps(a, halo, cw, cb):
    sub = _iota((a.shape[0], 1), 0)
    a1 = jnp.where(sub == 0, _pick_row(halo, 7), pltpu.roll(a, 1, 0))
    a2 = jnp.where(sub == 0, _pick_row(halo, 6), jnp.where(sub == 1, _pick_row(halo, 7), pltpu.roll(a, 2, 0)))
    acc = cb + a2 * cw[0]
    acc = acc + a1 * cw[1]
    acc = acc + a * cw[2]
    return a1, a2, acc


def _ffn_down_loss(g_act, w_down, h1, final_g, target):
    L = h1.shape[0]
    tm = _row_tile(L)
    nb = tm // BLK

    def body(g_ref, wd_ref, h1_ref, gf_ref, *refs):
        t_refs, (dh_ref, dhb_ref, dgf_ref, loss_ref) = refs[:nb], refs[nb:]
        i = pl.program_id(0)

        @pl.when(i == 0)
        def _():
            dgf_ref[...] = jnp.zeros_like(dgf_ref)
            loss_ref[...] = jnp.zeros_like(loss_ref)

        h2 = h1_ref[...] + _dot(g_ref[...], wd_ref[...])
        r = lax.rsqrt(jnp.mean(h2 * h2, axis=-1, keepdims=True) + EPS)
        yn = h2 * r
        gf = gf_ref[...]
        live = i * tm + _iota((tm, 1), 0) >= PREFIX
        target = jnp.concatenate([t[...] for t in t_refs], axis=0)
        err = jnp.where(live, yn * gf - target, 0.0)
        loss_ref[...] = loss_ref[...] + 0.5 * jnp.sum(jnp.mean(err * err, axis=-1, keepdims=True))
        dy = err * (1.0 / D_MODEL)
        dgf_ref[...] = dgf_ref[...] + jnp.sum(dy * yn, axis=0, keepdims=True)
        dyn = dy * gf
        dh = r * (dyn - yn * jnp.mean(dyn * yn, axis=-1, keepdims=True))
        dh_ref[...] = dh
        dhb_ref[...] = dh.astype(BF16)

    rows = lambda w: pl.BlockSpec((tm, w), lambda i: (i, 0))
    return pl.pallas_call(
        body, name="f_ffn_down_loss", grid=(L // tm,),
        in_specs=[rows(D_FF), _full((D_FF, D_MODEL)), rows(D_MODEL), _full((1, D_MODEL))] + _shifted_blocks(tm),
        out_specs=[rows(D_MODEL), rows(D_MODEL), _full((1, D_MODEL)), _full((1, BLK))],
        out_shape=[jax.ShapeDtypeStruct((L, D_MODEL), F32), jax.ShapeDtypeStruct((L, D_MODEL), BF16),
                   jax.ShapeDtypeStruct((1, D_MODEL), F32), jax.ShapeDtypeStruct((1, BLK), F32)],
        compiler_params=_params(("arbitrary",)),
    )(g_act, w_down, h1, final_g, *([target] * nb))


def _ffn_bwd_gate(dh2b, w_down, acc_saved, up):
    L = dh2b.shape[0]
    tm = _row_tile(L)

    def body(dh_ref, wd_ref, acc_ref, b_ref, dacc_ref, db_ref):
        acc = acc_ref[...].astype(F32)
        dg = _dot_nt(dh_ref[...], wd_ref[...])
        sg = jax.nn.sigmoid(acc)
        silu = acc * sg
        db_ref[...] = (dg * silu).astype(BF16)
        dacc_ref[...] = (dg * b_ref[...].astype(F32) * (sg + silu * (1.0 - sg))).astype(BF16)

    rows = lambda w, c=0: pl.BlockSpec((tm, w), lambda i: (i, c))
    return pl.pallas_call(
        body, name="b_ffn_gate", grid=(L // tm,),
        in_specs=[rows(D_MODEL), _full((D_FF, D_MODEL)), rows(D_FF), rows(D_FF, 1)],
        out_specs=[rows(D_FF), rows(D_FF)],
        out_shape=[jax.ShapeDtypeStruct((L, D_FF), BF16), jax.ShapeDtypeStruct((L, D_FF), BF16)],
        compiler_params=_params(("parallel",)),
    )(dh2b, w_down, acc_saved, up)


def _ffn_bwd_up(dacc, db, up, conv_w, w_up, h1, ffn_g, dh2, w_out):
    L = h1.shape[0]
    tm = _row_tile(L)
    nt = L // tm
    shard = w_up.shape[2]
    cw = [conv_w[j:j + 1] for j in range(3)]

    def body(da_ref, halo_ref, db_ref, a_ref, cw0, cw1, cw2, wu_ref, h1_ref, g_ref, dh2_ref, wo_ref,
             dup_ref, dh1_ref, dh1b_ref, dmix_ref, dg_ref, dcw_ref):
        i = pl.program_id(0)

        @pl.when(i == 0)
        def _():
            dg_ref[...] = jnp.zeros_like(dg_ref)
            dcw_ref[...] = jnp.zeros_like(dcw_ref)

        sub = _iota((tm, 1), 0)
        live = i * tm + sub >= N_PAD
        d0 = da_ref[...].astype(F32)
        halo = jnp.where(i < nt - 1, halo_ref[...].astype(F32), 0.0)
        d1 = jnp.where(sub == tm - 1, _pick_row(halo, 0), pltpu.roll(d0, tm - 1, 0))
        d2 = jnp.where(sub == tm - 2, _pick_row(halo, 0),
                       jnp.where(sub == tm - 1, _pick_row(halo, 1), pltpu.roll(d0, tm - 2, 0)))
        a = jnp.where(live, a_ref[...].astype(F32), 0.0)
        sub8 = _iota((8, 1), 0)
        upd = jnp.zeros((8, D_FF), F32)
        for j, t in enumerate((d2 * a, d1 * a, d0 * a, d0)):
            upd = upd + jnp.where(sub8 == j, jnp.sum(t, axis=0, keepdims=True), 0.0)
        dcw_ref[...] = dcw_ref[...] + upd
        da = d0 * cw2[...] + d1 * cw1[...] + d2 * cw0[...]
        da = jnp.where(live, da, 0.0).astype(BF16)
        dup_ref[:, 0:D_FF] = da
        dbv = db_ref[...]
        dup_ref[:, D_FF:2 * D_FF] = dbv
        dn = jnp.zeros((tm, D_MODEL), F32)
        for j in range(N_CHIPS):
            src = da if j < 2 else dbv
            lo = (j % 2) * shard
            dn = dn + _dot_nt(src[:, lo:lo + shard], wu_ref[j])
        h1 = h1_ref[...]
        r = lax.rsqrt(jnp.mean(h1 * h1, axis=-1, keepdims=True) + EPS)
        yn = h1 * r
        dg_ref[...] = dg_ref[...] + jnp.sum(dn * yn, axis=0, keepdims=True)
        dyn = dn * g_ref[...]
        dh1 = dh2_ref[...] + r * (dyn - yn * jnp.mean(dyn * yn, axis=-1, keepdims=True))
        dh1_ref[...] = dh1
        dh1b = dh1.astype(BF16)
        dh1b_ref[...] = dh1b
        dmix_ref[...] = _dot_nt(dh1b, wo_ref[...]).astype(BF16)

    rows = lambda w: pl.BlockSpec((tm, w), lambda i: (i, 0))
    halo = pl.BlockSpec((8, D_FF), lambda i: (jnp.minimum((i + 1) * (tm // 8), L // 8 - 1), 0))
    return pl.pallas_call(
        body, name="b_ffn_up", grid=(nt,),
        in_specs=[rows(D_FF), halo, rows(D_FF), rows(D_FF), _full((1, D_FF)), _full((1, D_FF)), _full((1, D_FF)),
                  _full((N_CHIPS, D_MODEL, shard)), rows(D_MODEL), _full((1, D_MODEL)), rows(D_MODEL),
                  _full((D_MODEL, D_MODEL))],
        out_specs=[rows(2 * D_FF), rows(D_MODEL), rows(D_MODEL), rows(D_MODEL), _full((1, D_MODEL)),
                   _full((8, D_FF))],
        out_shape=[jax.ShapeDtypeStruct((L, 2 * D_FF), BF16), jax.ShapeDtypeStruct((L, D_MODEL), F32),
                   jax.ShapeDtypeStruct((L, D_MODEL), BF16), jax.ShapeDtypeStruct((L, D_MODEL), BF16),
                   jax.ShapeDtypeStruct((1, D_MODEL), F32), jax.ShapeDtypeStruct((8, D_FF), F32)],
        compiler_params=_params(("arbitrary",)),
    )(dacc, dacc, db, up, cw[0], cw[1], cw[2], w_up, h1, ffn_g, dh2, w_out)


def _wgrad(a, b, name, tn=None, tk=None):
    L, K = a.shape
    N = b.shape[1]
    tn = N if tn is None else tn
    tk = K if tk is None else tk
    tl = _row_tile(L, (1408, 768, 512, 256, 128))

    def body(a_ref, b_ref, o_ref):
        @pl.when(pl.program_id(2) == 0)
        def _():
            o_ref[...] = jnp.zeros_like(o_ref)

        o_ref[0] = o_ref[0] + _dot_tn(a_ref[...], b_ref[...])

    return pl.pallas_call(
        body, name=name, grid=(N // tn, K // tk, L // tl),
        in_specs=[pl.BlockSpec((tl, tk), lambda n, k, l: (l, k)), pl.BlockSpec((tl, tn), lambda n, k, l: (l, n))],
        out_specs=pl.BlockSpec((1, tk, tn), lambda n, k, l: (n, k, 0)),
        out_shape=jax.ShapeDtypeStruct((N // tn, K, tn), F32),
        compiler_params=_params(("parallel", "parallel", "arbitrary")),
    )(a, b)


def _retention_bwd(dmix, o, proj, cos_t, sin_t, ret_g, states, exchange=()):
    L = proj.shape[0]
    nblk = L // BLK
    G = _block_group(nblk)
    steps = nblk // G
    nx = len(exchange)
    dmat, wq_t, wk_t, g_blk = _decay_tables()

    def body(dm_ref, o_ref, q_ref, k_ref, v_ref, gate_ref, cos_ref, sin_ref, d_ref, wq_ref, wk_ref, rg_ref, rs_ref,
             *rest):
        x_in, (dp_ref, drg_ref), x_out, gstate = rest[:nx], rest[nx:nx + 2], rest[nx + 2:2 * nx + 2], rest[2 * nx + 2]

        @pl.when(pl.program_id(0) == 0)
        def _():
            if nx:
                for cp in _sibling_half_copies(x_in, x_out, *rest[2 * nx + 3:])[0]:
                    cp.start()
            gstate[...] = jnp.zeros_like(gstate)
            drg_ref[...] = jnp.zeros_like(drg_ref)

        lane = _iota((BLK, BLK), 1)
        sub = _iota((BLK, BLK), 0)
        scale = HEAD_LANES ** -0.5
        for b in reversed(range(G)):
            rows = slice(b * BLK, (b + 1) * BLK)
            rot, rot_t = _rot_fns(cos_ref[rows, :], sin_ref[rows, :])
            for p in range(2):
                qr = rot(q_ref[rows, p * BLK:(p + 1) * BLK].astype(F32))
                kr = rot(k_ref[rows, p * BLK:(p + 1) * BLK].astype(F32)) * scale
                kr_b = kr.astype(BF16)
                qw = (qr * wq_ref[p]).astype(BF16)
                kw = (kr * wk_ref[p]).astype(BF16)
                dqr = jnp.zeros((BLK, BLK), F32)
                dkr = jnp.zeros((BLK, BLK), F32)
                for e in range(2):
                    h = 2 * p + e
                    cols = slice(h * BLK, (h + 1) * BLK)
                    head_lanes = (lane >> 6) == e
                    o = o_ref[rows, cols]
                    rn = lax.rsqrt(jnp.mean(o * o, axis=-1, keepdims=True) + EPS)
                    y = o * rn
                    gate = gate_ref[rows, cols].astype(F32)
                    sg = jax.nn.sigmoid(gate)
                    dm = dm_ref[rows, cols].astype(F32)
                    rgain = rg_ref[:, cols]
                    drg_ref[:, cols] = drg_ref[:, cols] + jnp.sum(dm * y * (gate * sg), axis=0, keepdims=True)
                    dp_ref[rows, 1024 + h * BLK:1024 + (h + 1) * BLK] = (
                        dm * y * rgain * (sg * (1.0 + gate * (1.0 - sg)))).astype(BF16)
                    dy = dm * rgain * (gate * sg)
                    do = (rn * (dy - y * jnp.mean(dy * y, axis=-1, keepdims=True))).astype(BF16)
                    vh = v_ref[rows, cols]
                    qm = jnp.where(head_lanes, qr, 0.0).astype(BF16)
                    dmh = d_ref[h]
                    s = (_dot_nt(qm, kr_b) * dmh).astype(BF16)
                    ds = (_dot_nt(do, vh) * dmh).astype(BF16)
                    st = rs_ref[b, h].astype(BF16)
                    gs = gstate[h]
                    gs_b = gs.astype(BF16)
                    dqr = dqr + jnp.where(head_lanes, _dot(ds, kr_b), 0.0) + _dot_nt(do, st) * wq_ref[p]
                    dkr = dkr + _dot_tn(ds, qm) + _dot_nt(vh, gs_b) * wk_ref[p]
                    dp_ref[rows, 512 + h * BLK:512 + (h + 1) * BLK] = (_dot_tn(s, do) + _dot(kw, gs_b)).astype(BF16)
                    dr = jnp.where((sub >> 6) == e, _dot_tn(qw, do), 0.0)
                    gstate[h] = dr + g_blk[h] * gs
                dp_ref[rows, p * BLK:(p + 1) * BLK] = rot_t(dqr).astype(BF16)
                dp_ref[rows, 256 + p * BLK:256 + (p + 1) * BLK] = (rot_t(dkr) * scale).astype(BF16)

        if nx:
            @pl.when(pl.program_id(0) == steps - 1)
            def _():
                sends, recvs = _sibling_half_copies(x_in, x_out, *rest[2 * nx + 3:])
                for cp in recvs:
                    cp.wait_recv()
                for cp in sends:
                    cp.wait_send()

    row = lambda c: (lambda i: (steps - 1 - i, c))
    return pl.pallas_call(
        body, name="b_retention", grid=(steps,),
        in_specs=[pl.BlockSpec((G * BLK, 512), row(0)), pl.BlockSpec((G * BLK, 512), row(0)),
                  pl.BlockSpec((G * BLK, 256), row(0)), pl.BlockSpec((G * BLK, 256), row(1)),
                  pl.BlockSpec((G * BLK, 512), row(1)), pl.BlockSpec((G * BLK, 512), row(2)),
                  pl.BlockSpec((G * BLK, BLK), row(0)), pl.BlockSpec((G * BLK, BLK), row(0)),
                  _full((RET_HEADS, BLK, BLK)), _full((2, BLK, BLK)), _full((2, BLK, BLK)), _full((1, 512)),
                  pl.BlockSpec((G, RET_HEADS, BLK, BLK), lambda i: (steps - 1 - i, 0, 0, 0))] + [_ANY] * nx,
        out_specs=[pl.BlockSpec((G * BLK, RET_W), row(0)), _full((1, 512))] + [_ANY] * nx,
        out_shape=[jax.ShapeDtypeStruct((L, RET_W), BF16), jax.ShapeDtypeStruct((1, 512), F32)]
        + _sibling_half_shapes(exchange),
        scratch_shapes=[pltpu.VMEM((RET_HEADS, BLK, BLK), F32)] + _sibling_half_semaphores(nx),
        compiler_params=_params(("arbitrary",)),
    )(dmix, o, proj, proj, proj, proj, cos_t, sin_t, dmat, wq_t, wk_t, ret_g, states, *exchange)


def _fox_delta(dmix, o_f):
    L = o_f.shape[0]
    nblk = L // BLK
    G = _block_group(nblk)

    def body(do_ref, o_ref, d_ref):
        sel = ((_iota((8, 512), 1) >> 6) == _iota((8, 512), 0)).astype(BF16)
        for b in range(G):
            rows = slice(b * BLK, (b + 1) * BLK)
            prod = do_ref[rows, :].astype(F32) * o_ref[rows, :].astype(F32)
            hi = prod.astype(BF16)
            lo = (prod - hi.astype(F32)).astype(BF16)
            d_ref[b] = _dot_nt(sel, hi) + _dot_nt(sel, lo)

    return pl.pallas_call(
        body, name="b_foxdelta", grid=(nblk // G,),
        in_specs=[pl.BlockSpec((G * BLK, 512), lambda i: (i, 1)), pl.BlockSpec((G * BLK, 512), lambda i: (i, 0))],
        out_specs=pl.BlockSpec((G, 8, BLK), lambda i: (i, 0, 0)),
        out_shape=jax.ShapeDtypeStruct((nblk, 8, BLK), F32),
        compiler_params=_params(("parallel",)),
    )(dmix, o_f)


def _fox_bwd(proj, dmix, c, ctb, lse, delta, scatter=()):
    L = proj.shape[0]
    nblk, nu = _fox_units(L)
    scale = HEAD_LANES ** -0.5
    ns = len(scatter)

    steps = FOX_HEADS // (2 * FOX_PAIRS)

    def body(qkv_ref, do_ref, c_ref, ct_ref, lse_ref, dl_ref, *rest):
        s_in, (dp_ref, dc_ref, dcq_ref), s_out = rest[:ns], rest[ns:ns + 3], rest[ns + 3:2 * ns + 3]
        ktt, dqt, dk_acc, dv_acc, dcs_acc = rest[2 * ns + 3:2 * ns + 8]
        p = pl.program_id(0)
        heads = [(pp, e, 2 * FOX_PAIRS * p + 2 * pp + e) for pp in range(FOX_PAIRS) for e in range(2)]

        @pl.when(p == 0)
        def _():
            dc_ref[...] = jnp.zeros_like(dc_ref)
            dcq_ref[...] = jnp.zeros_like(dcq_ref)
            if ns:
                for cp in _scatter_copies(s_in, s_out, *rest[2 * ns + 8:]):
                    cp.start()

        sub8 = _iota((8, BLK), 0)
        masks = _fox_tile_masks()

        def pre(j, carry):
            off = pl.multiple_of(j * BLK, BLK)
            for pp in range(FOX_PAIRS):
                ktt[pp, j] = qkv_ref[pl.ds(off, BLK), pp * 384 + BLK:pp * 384 + 2 * BLK].astype(F32).T.astype(BF16)
                dqt[pp, j] = jnp.zeros((BLK, BLK), F32)
            return carry

        lax.fori_loop(0, nblk, pre, 0)

        def kv_pass(kblk, nk, n_later):
            klen = nk * BLK
            koff = pl.multiple_of(kblk * BLK, BLK)
            kt = [qkv_ref[pl.ds(koff, klen), pp * 384 + BLK:pp * 384 + 2 * BLK] for pp in range(FOX_PAIRS)]
            vtile = [qkv_ref[pl.ds(koff, klen), pp * 384 + 2 * BLK:pp * 384 + 3 * BLK] for pp in range(FOX_PAIRS)]
            ct = c_ref[pl.ds(koff, klen), :]
            klane = _iota((klen, BLK), 1)
            cs = [jnp.broadcast_to(jnp.sum(jnp.where(klane == h, ct, 0.0), axis=1, keepdims=True), (klen, WIDE * UNIT))
                  for _, _, h in heads]
            for pp in range(FOX_PAIRS):
                dk_acc[pp, 0:klen] = jnp.zeros((klen, BLK), F32)
                dv_acc[pp, 0:klen] = jnp.zeros((klen, BLK), F32)
            for hh in range(len(heads)):
                dcs_acc[hh, 0:klen] = jnp.zeros((klen, BLK), F32)

            def tile(qblk, nq, mask):
                qlen = nq * BLK
                if mask == "valid":
                    mask = _iota((klen, qlen), 0) >= N_PAD
                qoff = pl.multiple_of(qblk * BLK, BLK)
                qlane = _iota((qlen, BLK), 1)
                qs = [qkv_ref[pl.ds(qoff, qlen), pp * 384:pp * 384 + BLK].astype(F32) * (scale * LOG2E)
                      for pp in range(FOX_PAIRS)]
                dot_ = [do_ref[pl.ds(qoff, qlen), pp * BLK:(pp + 1) * BLK] for pp in range(FOX_PAIRS)]
                stats = [[ref[qblk + a] for a in range(nq)] for ref in (ct_ref, lse_ref, dl_ref)]
                dcq = [jnp.zeros((8, BLK), F32) for _ in range(nq)]
                for hh, (pp, e, h) in enumerate(heads):
                    head = (qlane >> 6) == e
                    ct_row, lse_row, dl_row = [jnp.concatenate([_pick_row(t, h) for t in ts], axis=1) for ts in stats]
                    qm = jnp.where(head, qs[pp], 0.0).astype(BF16)
                    dom = jnp.where(head, dot_[pp], jnp.zeros_like(dot_[pp]))
                    t = _dot_nt(kt[pp], qm) - cs[hh][:, 0:qlen]
                    if mask is not None:
                        t = jnp.where(mask, t, NEG)
                    pr = jnp.exp2(t + (ct_row - lse_row))
                    dv_acc[pp, 0:klen] = dv_acc[pp, 0:klen] + _dot(pr.astype(BF16), dom)
                    dsv = pr * (_dot_nt(vtile[pp], dom) - dl_row)
                    ds_b = dsv.astype(BF16)
                    dk_acc[pp, 0:klen] = dk_acc[pp, 0:klen] + _dot(ds_b, qm)
                    rows = slice(e * HEAD_LANES, (e + 1) * HEAD_LANES)
                    dq_t = _dot(ktt[pp, kblk, rows, :], ds_b[0:BLK])
                    for b in range(1, nk):
                        dq_t = dq_t + _dot(ktt[pp, kblk + b, rows, :], ds_b[b * BLK:(b + 1) * BLK])
                    key_side = dsv[:, 0:BLK]
                    for a in range(1, nq):
                        key_side = key_side + dsv[:, a * BLK:(a + 1) * BLK]
                    dcs_acc[hh, 0:klen] = dcs_acc[hh, 0:klen] + key_side
                    query_side = jnp.sum(dsv, axis=0, keepdims=True)
                    for a in range(nq):
                        cols = slice(a * BLK, (a + 1) * BLK)
                        dqt[pp, qblk + a, rows, :] = dqt[pp, qblk + a, rows, :] + dq_t[:, cols]
                        dcq[a] = dcq[a] + jnp.where(sub8 == h, query_side[:, cols], 0.0)
                for a in range(nq):
                    dcq_ref[qblk + a] = dcq_ref[qblk + a] + dcq[a]

            later_mask = "valid" if nk == 1 else None
            n_later = jnp.asarray(n_later, jnp.int32)
            n_wide = n_later // WIDE

            def later_wide(i, carry):
                tile(kblk + nk + 2 * WIDE * i, 2 * WIDE, later_mask)
                return carry

            tile(kblk, nk, masks["first"] if nk == 1 else masks["diag"])
            lax.fori_loop(0, n_wide, later_wide, 0)
            rest_blk = kblk + nk + 2 * WIDE * n_wide

            @pl.when((n_later & 2) != 0)
            def _():
                tile(rest_blk, 4, later_mask)

            @pl.when((n_later & 1) != 0)
            def _():
                tile(rest_blk + 2 * (n_later & 2), 2, later_mask)

            upd = jnp.zeros((klen, BLK), F32)
            for hh, (_, _, h) in enumerate(heads):
                upd = upd + jnp.where(klane == h, -jnp.sum(dcs_acc[hh, 0:klen], axis=1, keepdims=True), 0.0)
            dc_ref[pl.ds(koff, klen), :] = dc_ref[pl.ds(koff, klen), :] + upd
            for pp in range(FOX_PAIRS):
                dp_ref[pl.ds(koff, klen), pp * 384 + BLK:pp * 384 + 2 * BLK] = (
                    dk_acc[pp, 0:klen] * (1.0 / LOG2E)).astype(BF16)
                dp_ref[pl.ds(koff, klen), pp * 384 + 2 * BLK:pp * 384 + 3 * BLK] = dv_acc[pp, 0:klen].astype(BF16)

        kv_pass(0, 1, nu)

        def k_loop(u, carry):
            kv_pass(1 + 2 * u, 2, nu - 1 - u)
            return carry

        lax.fori_loop(0, nu, k_loop, 0)

        def flush(j, carry):
            off = pl.multiple_of(j * BLK, BLK)
            for pp in range(FOX_PAIRS):
                dp_ref[pl.ds(off, BLK), pp * 384:pp * 384 + BLK] = (dqt[pp, j].T * scale).astype(BF16)
            return carry

        lax.fori_loop(0, nblk, flush, 0)

        if ns:
            @pl.when(p == steps - 1)
            def _():
                copies = _scatter_copies(s_in, s_out, *rest[2 * ns + 8:])
                for cp in copies:
                    cp.wait_recv()
                for cp in copies:
                    cp.wait_send()

    width = 384 * FOX_PAIRS
    once = lambda shape, index: pl.BlockSpec(shape, index, pipeline_mode=pl.Buffered(1))
    stat = once((nblk, 8, BLK), lambda p: (0, 0, 0))
    return pl.pallas_call(
        body, name="b_fox", grid=(steps,),
        in_specs=[once((L, width), lambda p: (0, RET_W // width + p)),
                  once((L, FOX_PAIRS * BLK), lambda p: (0, 4 // FOX_PAIRS + p)),
                  once((L, BLK), lambda p: (0, 0)), stat, stat, stat] + [_ANY] * ns,
        out_specs=[pl.BlockSpec((L, width), lambda p: (0, p)), _full((L, BLK)), _full((nblk, 8, BLK))] + [_ANY] * ns,
        out_shape=[jax.ShapeDtypeStruct((L, FOX_W), BF16), jax.ShapeDtypeStruct((L, BLK), F32),
                   jax.ShapeDtypeStruct((nblk, 8, BLK), F32)] + _scatter_shapes(scatter),
        scratch_shapes=[pltpu.VMEM((FOX_PAIRS, nblk, BLK, BLK), BF16), pltpu.VMEM((FOX_PAIRS, nblk, BLK, BLK), F32),
                        pltpu.VMEM((FOX_PAIRS, UNIT, BLK), F32), pltpu.VMEM((FOX_PAIRS, UNIT, BLK), F32),
                        pltpu.VMEM((2 * FOX_PAIRS, UNIT, BLK), F32)]
        + _scatter_semaphores(ns),
        compiler_params=_params(("arbitrary",)),
    )(proj, dmix, c, ctb, lse, delta, *scatter)


def _fox_post(dc, dcq, ff, fb):
    L = dc.shape[0]
    nblk = L // BLK
    G = _block_group(nblk)
    steps = nblk // G

    def body(dc_ref, dcq_ref, ff_ref, b_ref, dff_ref, dffb_ref, dfb_ref, carry):
        @pl.when(pl.program_id(0) == 0)
        def _():
            carry[...] = jnp.zeros_like(carry)
            dfb_ref[...] = jnp.zeros_like(dfb_ref)

        tri = (_iota((BLK, BLK), 0) <= _iota((BLK, BLK), 1)).astype(BF16)
        live = _iota((BLK, BLK), 1) < FOX_HEADS
        run, dfb = carry[...], dfb_ref[...]
        for b in reversed(range(G)):
            rows = slice(b * BLK, (b + 1) * BLK)
            d = dc_ref[rows, :] + jnp.concatenate([dcq_ref[b], jnp.zeros((BLK - 8, BLK), F32)], axis=0).T
            hi, mid, lo = _split3(d)
            dlf = _dot(tri, hi) + _dot(tri, mid) + _dot(tri, lo) + run
            run = run + jnp.sum(d, axis=0, keepdims=True)
            z = ff_ref[rows, :] + b_ref[...]
            dff = jnp.where(live, dlf * jax.nn.sigmoid(-z), 0.0)
            dff_ref[rows, :] = dff
            dffb_ref[rows, :] = dff.astype(BF16)
            dfb = dfb + jnp.sum(dff, axis=0, keepdims=True)
        carry[...] = run
        dfb_ref[...] = dfb

    rev = lambda i: (steps - 1 - i, 0)
    return pl.pallas_call(
        body, name="b_foxpost", grid=(steps,),
        in_specs=[pl.BlockSpec((G * BLK, BLK), rev), pl.BlockSpec((G, 8, BLK), lambda i: (steps - 1 - i, 0, 0)),
                  pl.BlockSpec((G * BLK, BLK), rev), _full((1, BLK))],
        out_specs=[pl.BlockSpec((G * BLK, BLK), rev), pl.BlockSpec((G * BLK, BLK), rev), _full((1, BLK))],
        out_shape=[jax.ShapeDtypeStruct((L, BLK), F32), jax.ShapeDtypeStruct((L, BLK), BF16),
                   jax.ShapeDtypeStruct((1, BLK), F32)],
        scratch_shapes=[pltpu.VMEM((1, BLK), F32)],
        compiler_params=_params(("arbitrary",)),
    )(dc, dcq, ff, fb)


def _inproj_bwd(dpr, dpf, dffb, w_main, w_ff, h0, g, dh1):
    L = h0.shape[0]
    tm = _row_tile(L)

    def body(dpr_ref, dpf_ref, dff_ref, wm_ref, wf_ref, h_ref, g_ref, dh1_ref, dh0_ref, dg_ref):
        @pl.when(pl.program_id(0) == 0)
        def _():
            dg_ref[...] = jnp.zeros_like(dg_ref)

        dn = (_dot_nt(dpr_ref[...], wm_ref[:, 0:RET_W]) + _dot_nt(dpf_ref[...], wm_ref[:, RET_W:MAIN_W])
              + _dot_nt(dff_ref[...], wf_ref[...]))
        h = h_ref[...]
        r = lax.rsqrt(jnp.mean(h * h, axis=-1, keepdims=True) + EPS)
        yn = h * r
        dg_ref[...] = dg_ref[...] + jnp.sum(dn * yn, axis=0, keepdims=True)
        dyn = dn * g_ref[...]
        dh0_ref[...] = dh1_ref[...] + r * (dyn - yn * jnp.mean(dyn * yn, axis=-1, keepdims=True))

    rows = lambda w: pl.BlockSpec((tm, w), lambda i: (i, 0))
    return pl.pallas_call(
        body, name="b_inproj", grid=(L // tm,),
        in_specs=[rows(RET_W), rows(FOX_W), rows(BLK), _full((D_MODEL, MAIN_W)), _full((D_MODEL, BLK)),
                  rows(D_MODEL), _full((1, D_MODEL)), rows(D_MODEL)],
        out_specs=[rows(D_MODEL), _full((1, D_MODEL))],
        out_shape=[jax.ShapeDtypeStruct((L, D_MODEL), F32), jax.ShapeDtypeStruct((1, D_MODEL), F32)],
        compiler_params=_params(("arbitrary",)),
    )(dpr, dpf, dffb, w_main, w_ff, h0, g, dh1)


def _local_step(x, target, meta, attn_g, w_main, w_ff, fox_b, ret_g, w_out, ffn_g, w_up, conv_w, conv_b, w_down, final_g,
                late=None, mid=None):
    S = x.shape[0]
    L = S + PREFIX
    head = jnp.concatenate([jnp.zeros((N_PAD, D_MODEL), F32), meta], axis=0)
    fb = jnp.pad(fox_b, ((0, 0), (0, BLK - FOX_HEADS)))
    cos_t, sin_t = _rotary_tables(L)

    h0, n1, proj, ff = _rms_inproj(head, x, attn_g, w_main, w_ff)
    c, ctb = _fox_prep(ff, fb)
    mix_r, o_ret, states = _retention_fwd(proj, cos_t, sin_t, ret_g)
    if late is None:
        o_f, lse = _fox_fwd(proj, c, ctb)
    else:
        o_f, lse, *gathered = _fox_fwd(proj, c, ctb, gather=late[0])
        w_out, w_up, w_down = late[1](gathered)
    h1, n2, up, g_act, acc_saved = _outproj_up(mix_r, o_f, h0, w_out, ffn_g, w_up, conv_w, conv_b)
    dh2, dh2b, d_final_g, loss = _ffn_down_loss(g_act, w_down, h1, final_g, target)

    dacc, db = _ffn_bwd_gate(dh2b, w_down, acc_saved, up)
    dup, dh1, dh1b, dmix, d_ffn_g, dconv = _ffn_bwd_up(dacc, db, up, conv_w, w_up, h1, ffn_g, dh2, w_out)
    d_w_down = _wgrad(g_act, dh2b, "wgrad_down", tk=D_FF // 2)[0]
    d_w_up = _wgrad(n2, dup, "wgrad_up", tn=w_up.shape[2])
    d_w_out = jnp.concatenate([_wgrad(mix_r, dh1b, "wgrad_out_r")[0], _wgrad(o_f, dh1b, "wgrad_out_f")[0]], axis=0)

    early = () if mid is None else mid[0](d_w_out, d_w_up, d_w_down)
    dpr, d_ret_g, *from_sibling = _retention_bwd(dmix, o_ret, proj, cos_t, sin_t, ret_g, states, exchange=early)
    delta = _fox_delta(dmix, o_f)
    scatter = () if mid is None else mid[1](early, from_sibling)
    dpf, dc, dcq, *received = _fox_bwd(proj, dmix, c, ctb, lse, delta, scatter=scatter)
    dff, dffb, d_fox_b = _fox_post(dc, dcq, ff, fb)
    dh0, d_attn_g = _inproj_bwd(dpr, dpf, dffb, w_main, w_ff, h0, attn_g, dh1)
    d_w_main = jnp.concatenate([_wgrad(n1, dpr, "wgrad_in_r")[0], _wgrad(n1, dpf, "wgrad_in_f")[0]], axis=1)
    d_w_ff = _wgrad(n1, dffb, "wgrad_in_ff")[0]

    return dict(
        loss=loss[0, 0], dx=dh0[PREFIX:], dmeta=dh0[N_PAD:PREFIX], attn_g=d_attn_g, w_main=d_w_main,
        w_ff=d_w_ff[:, :FOX_HEADS], fox_b=d_fox_b[:, :FOX_HEADS], ret_g=d_ret_g, w_out=d_w_out, ffn_g=d_ffn_g,
        w_up=d_w_up, conv_w=dconv[0:3], conv_b=dconv[3:4], w_down=d_w_down, final_g=d_final_g,
        scatter=scatter, received=received)


_ANY = pl.BlockSpec(memory_space=pl.ANY)


def _place():
    return lax.axis_index("x"), lax.axis_index("y"), lax.axis_index("c")


def _other_chips(x, y):
    return [(1 - x, y), (x, 1 - y), (1 - x, 1 - y)]


def _allgather_semaphores(n):
    if n == 0:
        return []
    return [pltpu.SemaphoreType.DMA((3 * n,)), pltpu.SemaphoreType.DMA((3 * n,)), pltpu.SemaphoreType.DMA((n,))]


def _allgather_copies(ins, outs, send, recv, loc):
    n = len(ins)
    x, y, c = _place()
    mine = 2 * x + y
    peers = _other_chips(x, y)

    def remote(a, k, slot):
        return pltpu.make_async_remote_copy(
            src_ref=ins[a], dst_ref=outs[a].at[slot], send_sem=send.at[3 * a + k], recv_sem=recv.at[3 * a + k],
            device_id=(peers[k][0], peers[k][1], c), device_id_type=MESH)

    local = [pltpu.make_async_copy(ins[a], outs[a].at[mine], loc.at[a]) for a in range(n)]
    sends = [remote(a, k, mine) for a in range(n) for k in range(3)]
    recvs = [remote(a, k, 2 * peers[k][0] + peers[k][1]) for a in range(n) for k in range(3)]
    return local, sends, recvs


def _chip_allgather_halves(w, small):
    half = w.shape[0] // 2

    def body(w_ref, s_ref, wo_ref, so_ref, send, recv, fsend, frecv, ssend, srecv, loc):
        x, y, c = _place()
        mine = 2 * x + y
        peers = _other_chips(x, y)

        def fetch(k, slot):
            return pltpu.make_async_remote_copy(
                src_ref=w_ref.at[pl.ds(c * half, half)], dst_ref=wo_ref.at[slot, c], send_sem=send.at[k],
                recv_sem=recv.at[k], device_id=(peers[k][0], peers[k][1], c), device_id_type=MESH)

        def forward(k, which):
            slot = 2 * peers[k][0] + peers[k][1]
            return pltpu.make_async_remote_copy(
                src_ref=wo_ref.at[slot, which], dst_ref=wo_ref.at[slot, which], send_sem=fsend.at[k],
                recv_sem=frecv.at[k], device_id=(x, y, 1 - c), device_id_type=MESH)

        def small_copy(k, slot):
            return pltpu.make_async_remote_copy(
                src_ref=s_ref, dst_ref=so_ref.at[slot], send_sem=ssend.at[k], recv_sem=srecv.at[k],
                device_id=(peers[k][0], peers[k][1], c), device_id_type=MESH)

        local = pltpu.make_async_copy(s_ref, so_ref.at[mine], loc.at[0])
        sends = [fetch(k, mine) for k in range(3)] + [small_copy(k, mine) for k in range(3)]
        local.start()
        for cp in sends:
            cp.start()
        forwards = []
        for k in range(3):
            fetch(k, 2 * peers[k][0] + peers[k][1]).wait_recv()
            forwards.append(forward(k, c))
            forwards[-1].start()
        for k in range(3):
            forward(k, 1 - c).wait_recv()
            small_copy(k, 2 * peers[k][0] + peers[k][1]).wait_recv()
        for cp in sends + forwards:
            cp.wait_send()
        local.wait()

    three = pltpu.SemaphoreType.DMA((3,))
    return pl.pallas_call(
        body, name="ag_weights", in_specs=[_ANY] * 2, out_specs=[_ANY] * 2,
        out_shape=[jax.ShapeDtypeStruct((N_CHIPS, 2, half, w.shape[1]), w.dtype),
                   jax.ShapeDtypeStruct((N_CHIPS,) + small.shape, small.dtype)],
        scratch_shapes=[three, three, three, three, three, three, pltpu.SemaphoreType.DMA((1,))],
    )(w, small)


def _chip_allgather(arrays):
    n = len(arrays)

    def body(*refs):
        local, sends, recvs = _allgather_copies(refs[:n], refs[n:2 * n], *refs[2 * n:])
        for cp in local + sends:
            cp.start()
        for cp in recvs:
            cp.wait_recv()
        for cp in sends:
            cp.wait_send()
        for cp in local:
            cp.wait()

    return pl.pallas_call(
        body, name="ag_weights", in_specs=[_ANY] * n, out_specs=[_ANY] * n,
        out_shape=[jax.ShapeDtypeStruct((N_CHIPS,) + a.shape, a.dtype) for a in arrays],
        scratch_shapes=_allgather_semaphores(n),
    )(*arrays)


def _sibling_exchange(grads, small):
    n = len(grads)

    def body(*refs):
        ins, small_in = refs[:n], refs[n]
        outs, small_out = refs[n + 1:2 * n + 1], refs[2 * n + 1]
        send, recv, s_send, s_recv, loc = refs[2 * n + 2:]
        x, y, c = _place()
        me = 4 * x + 2 * y + c

        def half_copy(a, which):
            half = ins[a].shape[1] // 2
            return pltpu.make_async_remote_copy(
                src_ref=ins[a].at[pl.ds(0, N_CHIPS), pl.ds(which * half, half)], dst_ref=outs[a],
                send_sem=send.at[a], recv_sem=recv.at[a], device_id=(x, y, 1 - c), device_id_type=MESH)

        def peer_of(r):
            return tuple(1 - v if (r >> b) & 1 else v for v, b in ((x, 2), (y, 1), (c, 0)))

        def small_copy(r, slot):
            return pltpu.make_async_remote_copy(
                src_ref=small_in, dst_ref=small_out.at[slot], send_sem=s_send.at[r - 1], recv_sem=s_recv.at[r - 1],
                device_id=peer_of(r), device_id_type=MESH)

        local = pltpu.make_async_copy(small_in, small_out.at[me], loc.at[0])
        sends = [half_copy(a, 1 - c) for a in range(n)] + [small_copy(r, me) for r in range(1, N_DEV)]
        local.start()
        for cp in sends:
            cp.start()
        for r in range(1, N_DEV):
            px, py, pc = peer_of(r)
            small_copy(r, 4 * px + 2 * py + pc).wait_recv()
        for a in range(n):
            half_copy(a, c).wait_recv()
        for cp in sends:
            cp.wait_send()
        local.wait()

    rows = small.shape[0]
    return pl.pallas_call(
        body, name="rs_sibling", in_specs=[_ANY] * (n + 1), out_specs=[_ANY] * (n + 1),
        out_shape=[jax.ShapeDtypeStruct((N_CHIPS, g.shape[1] // 2, g.shape[2]), g.dtype) for g in grads]
        + [jax.ShapeDtypeStruct((N_DEV, rows, small.shape[1]), small.dtype)],
        scratch_shapes=[pltpu.SemaphoreType.DMA((n,)), pltpu.SemaphoreType.DMA((n,)),
                        pltpu.SemaphoreType.DMA((N_DEV - 1,)), pltpu.SemaphoreType.DMA((N_DEV - 1,)),
                        pltpu.SemaphoreType.DMA((1,))],
    )(*grads, small)


def _sibling_half_shapes(grads):
    return [jax.ShapeDtypeStruct((N_CHIPS, g.shape[1] // 2, g.shape[2]), g.dtype) for g in grads]


def _sibling_half_semaphores(n):
    return [pltpu.SemaphoreType.DMA((n,)), pltpu.SemaphoreType.DMA((n,))] if n else []


def _sibling_half_copies(ins, outs, send, recv):
    x, y, c = _place()

    def half_copy(a, which):
        half = ins[a].shape[1] // 2
        return pltpu.make_async_remote_copy(
            src_ref=ins[a].at[pl.ds(0, N_CHIPS), pl.ds(which * half, half)], dst_ref=outs[a],
            send_sem=send.at[a], recv_sem=recv.at[a], device_id=(x, y, 1 - c), device_id_type=MESH)

    return [half_copy(a, 1 - c) for a in range(len(ins))], [half_copy(a, c) for a in range(len(ins))]


def _chip_reduce_scatter(parts):
    n = len(parts)

    def body(*refs):
        copies = _scatter_copies(refs[:n], refs[n:2 * n], *refs[2 * n:])
        for cp in copies:
            cp.start()
        for cp in copies:
            cp.wait_recv()
        for cp in copies:
            cp.wait_send()

    return pl.pallas_call(
        body, name="rs_chips", in_specs=[_ANY] * n, out_specs=[_ANY] * n,
        out_shape=_scatter_shapes(parts), scratch_shapes=_scatter_semaphores(n),
    )(*parts)


def _scatter_shapes(parts):
    return [jax.ShapeDtypeStruct((3,) + p.shape[1:], p.dtype) for p in parts]


def _scatter_semaphores(n):
    return [pltpu.SemaphoreType.DMA((3 * n,)), pltpu.SemaphoreType.DMA((3 * n,))] if n else []


def _scatter_copies(ins, outs, send, recv):
    x, y, c = _place()
    peers = _other_chips(x, y)
    return [pltpu.make_async_remote_copy(
        src_ref=ins[a].at[2 * peers[k][0] + peers[k][1]], dst_ref=outs[a].at[k], send_sem=send.at[3 * a + k],
        recv_sem=recv.at[3 * a + k], device_id=(peers[k][0], peers[k][1], c), device_id_type=MESH)
        for a in range(len(ins)) for k in range(3)]


def _sibling_allgather(bufs):
    n = len(bufs)

    def body(*refs):
        outs = refs[n:2 * n]
        send, recv = refs[2 * n:]
        x, y, c = _place()

        def remote(a, which):
            return pltpu.make_async_remote_copy(
                src_ref=outs[a].at[which], dst_ref=outs[a].at[which], send_sem=send.at[a], recv_sem=recv.at[a],
                device_id=(x, y, 1 - c), device_id_type=MESH)

        sends = [remote(a, c) for a in range(n)]
        for cp in sends:
            cp.start()
        for a in range(n):
            remote(a, 1 - c).wait_recv()
        for cp in sends:
            cp.wait_send()

    outs = pl.pallas_call(
        body, name="ag_sibling", in_specs=[_ANY] * n, out_specs=[_ANY] * n,
        out_shape=[jax.ShapeDtypeStruct(b.shape, b.dtype) for b in bufs],
        input_output_aliases={a: a for a in range(n)},
        scratch_shapes=[pltpu.SemaphoreType.DMA((n,)), pltpu.SemaphoreType.DMA((n,))],
    )(*bufs)
    return [o.reshape(2 * o.shape[1], o.shape[2]) for o in outs]


def _pair_add(full, recv, core, name):
    _, R, C = full.shape
    half = R // 2

    def body(core_ref, a_ref, b_ref, o_ref):
        o_ref[...] = (a_ref[...] + b_ref[...]).astype(BF16)

    return pl.pallas_call(
        body, name=name,
        grid_spec=pltpu.PrefetchScalarGridSpec(
            num_scalar_prefetch=1, grid=(N_CHIPS,),
            in_specs=[pl.BlockSpec((1, half, C), lambda j, core_ref: (j, core_ref[0], 0)),
                      pl.BlockSpec((1, half, C), lambda j, core_ref: (j, 0, 0))],
            out_specs=pl.BlockSpec((1, half, C), lambda j, core_ref: (j, 0, 0))),
        out_shape=jax.ShapeDtypeStruct((N_CHIPS, half, C), BF16),
        compiler_params=_params(("parallel",)),
    )(core, full, recv)


def _sum_slots(q, name, tiles=2):
    n, R, C = q.shape
    tr = R // tiles

    def body(q_ref, o_ref):
        acc = q_ref[0].astype(F32)
        for j in range(1, n):
            acc = acc + q_ref[j].astype(F32)
        o_ref[...] = acc

    return pl.pallas_call(
        body, name=name, grid=(tiles,),
        in_specs=[pl.BlockSpec((n, tr, C), lambda i: (0, i, 0))],
        out_specs=pl.BlockSpec((tr, C), lambda i: (i, 0)),
        out_shape=jax.ShapeDtypeStruct((R, C), F32),
        compiler_params=_params(("parallel",)),
    )(q)


def _sum_partials(own_all, recv, place, name, tiles=2):
    _, R, C = own_all.shape
    tr = R // tiles

    def body(place_ref, own_ref, r_ref, o_ref):
        acc = own_ref[0].astype(F32)
        for k in range(3):
            acc = acc + r_ref[k].astype(F32)
        o_ref[0] = acc

    return pl.pallas_call(
        body, name=name,
        grid_spec=pltpu.PrefetchScalarGridSpec(
            num_scalar_prefetch=1, grid=(tiles,),
            in_specs=[pl.BlockSpec((1, tr, C), lambda i, place_ref: (place_ref[0], i, 0)),
                      pl.BlockSpec((3, tr, C), lambda i, place_ref: (0, i, 0))],
            out_specs=pl.BlockSpec((1, tr, C), lambda i, place_ref: (place_ref[1], i, 0))),
        out_shape=jax.ShapeDtypeStruct((2, R, C), F32),
        compiler_params=_params(("parallel",)),
    )(place, own_all, recv)


def _adamw(w, g, m, v, name, tiles=4):
    R, C = w.shape
    tr = R // tiles

    def body(w_ref, g_ref, m_ref, v_ref, go_ref, d_ref, m2_ref, v2_ref):
        g_ = g_ref[...]
        go_ref[...] = g_
        m2 = ADAM_B1 * m_ref[...] + (1.0 - ADAM_B1) * g_
        v2 = ADAM_B2 * v_ref[...] + (1.0 - ADAM_B2) * (g_ * g_)
        m_hat = m2 / (1.0 - ADAM_B1 ** ADAM_STEP)
        v_hat = v2 / (1.0 - ADAM_B2 ** ADAM_STEP)
        d_ref[...] = -ADAM_LR * (m_hat / (jnp.sqrt(v_hat) + ADAM_EPS) + ADAM_WD * w_ref[...])
        m2_ref[...] = m2
        v2_ref[...] = v2

    spec = pl.BlockSpec((tr, C), lambda i: (i, 0))
    return pl.pallas_call(
        body, name=name, grid=(tiles,), in_specs=[spec] * 4, out_specs=[spec] * 4,
        out_shape=[jax.ShapeDtypeStruct((R, C), F32)] * 4,
        compiler_params=_params(("parallel",)),
    )(w, g, m, v)


def _pack_rows(pieces, rows):
    flat = jnp.concatenate([jnp.pad(p.reshape(-1).astype(F32), (0, (-p.size) % D_MODEL)) for p in pieces])
    return jnp.pad(flat, (0, rows * D_MODEL - flat.size)).reshape(rows, D_MODEL)


def _unpack_rows(pack, shapes):
    flat = pack.reshape(-1)
    out, off = [], 0
    for shp in shapes:
        size = int(np.prod(shp))
        out.append(flat[off:off + size].reshape(shp))
        off += size + (-size) % D_MODEL
    return out


def _kernel_order(w):
    parts = [w[:, 0:RET_W]]
    for p in range(FOX_HEADS // 2):
        parts += [w[:, RET_W + part * 512 + p * BLK:RET_W + part * 512 + (p + 1) * BLK] for part in range(3)]
    return jnp.concatenate(parts, axis=1)


def _reference_order(g_main, g_ff):
    parts = [g_main[:, 0:RET_W]]
    for part in range(3):
        parts += [g_main[:, RET_W + 384 * p + part * BLK:RET_W + 384 * p + (part + 1) * BLK] for p in range(FOX_HEADS // 2)]
    return jnp.concatenate(parts + [g_ff], axis=1)


def kernel(x, meta_tokens, attn_norm_g, w_in, fox_forget_b, ret_norm_g, w_out, ffn_norm_g, w_up, conv_w, conv_b, w_down, final_norm_g, loss_target, m_meta_tokens, m_attn_norm_g, m_w_in, m_fox_forget_b, m_ret_norm_g, m_w_out, m_ffn_norm_g, m_w_up, m_conv_w, m_conv_b, m_w_down, m_final_norm_g, v_meta_tokens, v_attn_norm_g, v_w_in, v_fox_forget_b, v_ret_norm_g, v_w_out, v_ffn_norm_g, v_w_up, v_conv_w, v_conv_b, v_w_down, v_final_norm_g):
    chip = 2 * lax.axis_index("x") + lax.axis_index("y")
    core = lax.axis_index("c")
    meta_w, conv_sw = meta_tokens.shape[1], conv_w.shape[2]

    small_w = _pack_rows([meta_tokens, conv_w[0]], 8)
    w_in_b = w_in[0].astype(BF16)
    g_in, g_small = _chip_allgather_halves(w_in_b, small_w)
    g_in = lax.dynamic_update_slice(g_in.reshape((N_CHIPS,) + w_in_b.shape), w_in_b[None], (chip, 0, 0))
    w_in_full = g_in.transpose(1, 0, 2).reshape(D_MODEL, IN_WIDTH)
    w_main = _kernel_order(w_in_full)
    w_ff = jnp.pad(w_in_full[:, MAIN_W:], ((0, 0), (0, BLK - FOX_HEADS)))
    small_parts = [_unpack_rows(g_small[j], [meta_tokens.shape, conv_w.shape[1:]]) for j in range(N_CHIPS)]
    meta_full = jnp.concatenate([sp[0] for sp in small_parts], axis=1)
    conv_w_full = jnp.concatenate([sp[1] for sp in small_parts], axis=1)

    core_idx = core.reshape(1).astype(jnp.int32)
    place = jnp.stack([chip, core]).astype(jnp.int32)

    def assemble(gathered):
        g_out, g_up, g_down = gathered
        return g_out.reshape(D_MODEL, D_MODEL), g_up, g_down.reshape(D_FF, D_MODEL)

    def early_arrays(d_w_out, d_w_up, d_w_down):
        return [d_w_out.reshape(N_CHIPS, -1, D_MODEL), d_w_up, d_w_down.reshape(N_CHIPS, -1, D_MODEL)]

    def early_sums(early, from_sib):
        return [_pair_add(g, r, core_idx, "pair_add_" + nm) for g, r, nm in zip(early, from_sib, ("out", "up", "down"))]

    out = _local_step(x[0], loss_target[0], meta_full, attn_norm_g, w_main, w_ff, fox_forget_b, ret_norm_g,
                      None, ffn_norm_g, None, conv_w_full, conv_b, None, final_norm_g[None],
                      late=([w_out[0].astype(BF16), w_up[0].astype(BF16), w_down[0].astype(BF16)], assemble),
                      mid=(early_arrays, early_sums))

    g_in_full = _reference_order(out["w_main"], out["w_ff"]).reshape(D_MODEL, N_CHIPS, -1).transpose(1, 0, 2)
    small_shapes = [(1, D_MODEL), (1, D_MODEL), (1, D_MODEL), (1, 512 + FOX_HEADS + 1), (1, D_FF), (N_META, D_MODEL), (3, D_FF)]
    small = _pack_rows([out["attn_g"], out["ffn_g"], out["final_g"],
                        jnp.concatenate([out["ret_g"], out["fox_b"], out["loss"].reshape(1, 1)], axis=1),
                        out["conv_b"], out["dmeta"], out["conv_w"]], 32)
    from_sibling_in, small_all = _sibling_exchange([g_in_full], small)
    sum_in = _pair_add(g_in_full, from_sibling_in, core_idx, "pair_add_in")
    (from_chips_in,) = _chip_reduce_scatter([sum_in])
    chip_sums = [sum_in] + list(out["scatter"])
    from_chips = [from_chips_in] + list(out["received"])
    names = ("in", "out", "up", "down")
    totals = [_sum_partials(s, q, place, "sum_chips_" + nm) for s, q, nm in zip(chip_sums, from_chips, names)]
    grad_in, grad_out, grad_up, grad_down = _sibling_allgather(totals)
    s_attn, s_ffn, s_final, s_misc, s_conv_b, s_meta, s_conv_w = _unpack_rows(
        _sum_slots(small_all, "sum_small", tiles=1), small_shapes)
    loss = s_misc[0, 512 + FOX_HEADS]
    small_grads = [lax.dynamic_slice_in_dim(s_meta, chip * meta_w, meta_w, axis=1), s_attn, s_misc[:, 512:512 + FOX_HEADS],
                   s_misc[:, :512], s_ffn, lax.dynamic_slice_in_dim(s_conv_w, chip * conv_sw, conv_sw, axis=1)[None],
                   s_conv_b, s_final[0]]

    big_w = [(w_in, m_w_in, v_w_in, grad_in, "adamw_in"), (w_out, m_w_out, v_w_out, grad_out, "adamw_out"),
             (w_up, m_w_up, v_w_up, grad_up, "adamw_up"), (w_down, m_w_down, v_w_down, grad_down, "adamw_down")]
    big_res = [[r[None] for r in _adamw(w[0], g, m[0], v[0], nm)] for w, m, v, g, nm in big_w]
    small_w_list = [meta_tokens, attn_norm_g, fox_forget_b, ret_norm_g, ffn_norm_g, conv_w, conv_b, final_norm_g]
    small_m = [m_meta_tokens, m_attn_norm_g, m_fox_forget_b, m_ret_norm_g, m_ffn_norm_g, m_conv_w, m_conv_b, m_final_norm_g]
    small_v = [v_meta_tokens, v_attn_norm_g, v_fox_forget_b, v_ret_norm_g, v_ffn_norm_g, v_conv_w, v_conv_b, v_final_norm_g]
    shapes = [a.shape for a in small_w_list]
    packs = [_pack_rows(lst, 16) for lst in (small_w_list, small_grads, small_m, small_v)]
    small_res = [_unpack_rows(r, shapes) for r in _adamw(*packs, "adamw_small", tiles=1)[1:]]
    small_grads = [g.reshape(s) for g, s in zip(small_grads, shapes)]

    def ordered(kind):
        sm = small_grads if kind == 0 else small_res[kind - 1]
        bg = [r[kind] for r in big_res]
        return [sm[0], sm[1], bg[0], sm[2], sm[3], bg[1], sm[4], bg[2], sm[5], sm[6], bg[3], sm[7]]

    return (loss, out["dx"][None], *ordered(0), *ordered(1), *ordered(2), *ordered(3))
```

```python
import functools

import numpy as np
import jax
import jax.numpy as jnp
from jax import lax
from jax.experimental import pallas as pl
from jax.experimental.pallas import tpu as pltpu

F32 = jnp.float32
BF16 = jnp.bfloat16

D_MODEL = 1024
N_META = 16
BLK = 128
UNIT = 2 * BLK
FOX_PAIRS = 2
WIDE = 4
CHUNK = 64
N_PAD = BLK - N_META
PREFIX = BLK
RET_HEADS = 4
FOX_HEADS = 8
HEAD_LANES = 64
D_FF = 2816
ROPE_BASE = 10000.0
EPS = 1e-6
NEG = -1e30
LOG2E = 1.4426950408889634
RET_W = 1536
FOX_W = 1536
MAIN_W = RET_W + FOX_W
IN_WIDTH = MAIN_W + FOX_HEADS
N_CHIPS = 4
N_DEV = 8

ADAM_LR = 0.001
ADAM_B1 = 0.9
ADAM_B2 = 0.999
ADAM_EPS = 1e-08
ADAM_WD = 0.01
ADAM_STEP = 10

MESH = pl.DeviceIdType.MESH
VMEM_LIMIT_MB = 56

_NT = (((1,), (1,)), ((), ()))
_TN = (((0,), (0,)), ((), ()))


def _dot(a, b):
    return jnp.dot(a, b, preferred_element_type=F32)


def _dot_nt(a, b):
    return lax.dot_general(a, b, _NT, preferred_element_type=F32)


def _dot_tn(a, b):
    return lax.dot_general(a, b, _TN, preferred_element_type=F32)


def _params(dims=None, vmem_mb=VMEM_LIMIT_MB):
    kw = dict(vmem_limit_bytes=vmem_mb << 20)
    if dims is not None:
        kw["dimension_semantics"] = dims
    return pltpu.CompilerParams(**kw)


def _row_tile(n, prefs=(384, 256, 128)):
    for t in prefs:
        if n % t == 0:
            return t
    raise ValueError(f"no row tile for {n}")


def _iota(shape, dim):
    return lax.broadcasted_iota(jnp.int32, shape, dim)


def _pick_row(tile, row):
    sub = _iota(tile.shape, 0)
    return jnp.sum(jnp.where(sub == row, tile, 0.0), axis=0, keepdims=True)


def _split3(x):
    hi = x.astype(BF16)
    r1 = x - hi.astype(F32)
    mid = r1.astype(BF16)
    lo = (r1 - mid.astype(F32)).astype(BF16)
    return hi, mid, lo


def _full(shape):
    nd = len(shape)
    return pl.BlockSpec(shape, lambda *_: (0,) * nd)


def _in_perm():
    cols = list(range(RET_W))
    for p in range(FOX_HEADS // 2):
        for part in range(3):
            start = RET_W + part * 512 + p * BLK
            cols += list(range(start, start + BLK))
    return np.asarray(cols, np.int32)


def _rotary_tables(L):
    half = HEAD_LANES // 2
    inv = 1.0 / (ROPE_BASE ** (jnp.arange(half, dtype=F32) / half))
    ang = jnp.arange(L).astype(F32)[:, None] * inv[None, :]
    cos, sin = jnp.cos(ang), jnp.sin(ang)
    cos_t = jnp.tile(cos, (1, 4))
    sin_t = jnp.tile(jnp.concatenate([-sin, sin], axis=1), (1, 2))
    return cos_t, sin_t


def _decay_tables():
    gam = 1.0 - 2.0 ** (-5.0 - np.arange(RET_HEADS, dtype=np.float64))
    n = np.arange(BLK)
    same_or_past = (n[:, None] // CHUNK) >= (n[None, :] // CHUNK)
    dist = np.abs(n[:, None] - n[None, :])
    dmat = np.stack([np.where(same_or_past, g ** dist, 0.0) for g in gam]).astype(np.float32)
    lane_head = np.arange(BLK) // HEAD_LANES
    wq = np.stack([gam[2 * p + lane_head][None, :] ** (n[:, None] + 1.0) for p in range(2)]).astype(np.float32)
    wk = np.stack([gam[2 * p + lane_head][None, :] ** (BLK - 1.0 - n[:, None]) for p in range(2)]).astype(np.float32)
    g_blk = tuple(float(g ** BLK) for g in gam)
    return jnp.asarray(dmat), jnp.asarray(wq), jnp.asarray(wk), g_blk


def _shifted_blocks(tm):
    nb = tm // BLK
    return [pl.BlockSpec((BLK, D_MODEL), lambda i, j=j: (jnp.maximum(nb * i + j - 1, 0), 0)) for j in range(nb)]


def _rms_inproj(head, x, g, w_main, w_ff):
    L = x.shape[0] + BLK
    tm = _row_tile(L)
    nb = tm // BLK

    def body(head_ref, *refs):
        x_refs, (g_ref, wm_ref, wf_ref, h_ref, n_ref, p_ref, ff_ref) = refs[:nb], refs[nb:]
        parts = [r[...] for r in x_refs]
        parts[0] = jnp.where(pl.program_id(0) == 0, head_ref[...], parts[0])
        h = jnp.concatenate(parts, axis=0)
        h_ref[...] = h
        r = lax.rsqrt(jnp.mean(h * h, axis=-1, keepdims=True) + EPS)
        n = (h * r * g_ref[...]).astype(BF16)
        n_ref[...] = n
        p_ref[...] = _dot(n, wm_ref[...]).astype(BF16)
        ff_ref[...] = _dot(n, wf_ref[...])

    rows = lambda w: pl.BlockSpec((tm, w), lambda i: (i, 0))
    return pl.pallas_call(
        body, name="f_inproj", grid=(L // tm,),
        in_specs=[_full((BLK, D_MODEL))] + _shifted_blocks(tm)
        + [_full((1, D_MODEL)), _full((D_MODEL, MAIN_W)), _full((D_MODEL, BLK))],
        out_specs=[rows(D_MODEL), rows(D_MODEL), rows(MAIN_W), rows(BLK)],
        out_shape=[jax.ShapeDtypeStruct((L, D_MODEL), F32), jax.ShapeDtypeStruct((L, D_MODEL), BF16),
                   jax.ShapeDtypeStruct((L, MAIN_W), BF16), jax.ShapeDtypeStruct((L, BLK), F32)],
        compiler_params=_params(("parallel",)),
    )(head, *([x] * nb), g, w_main, w_ff)


def _block_group(nblk):
    return 3 if nblk % 3 == 0 else 1


def _fox_prep(ff, fb):
    L = ff.shape[0]
    nblk = L // BLK
    G = _block_group(nblk)

    def body(ff_ref, b_ref, c_ref, ct_ref, carry):
        @pl.when(pl.program_id(0) == 0)
        def _():
            carry[...] = jnp.zeros_like(carry)

        tri = (_iota((BLK, BLK), 0) >= _iota((BLK, BLK), 1)).astype(BF16)
        live = _iota((BLK, BLK), 1) < FOX_HEADS
        run = carry[...]
        for b in range(G):
            z = ff_ref[b * BLK:(b + 1) * BLK, :] + b_ref[...]
            lf = jnp.where(live, jnp.minimum(z, 0.0) - jnp.log1p(jnp.exp(-jnp.abs(z))), 0.0)
            hi, mid, lo = _split3(lf)
            cs = (_dot(tri, hi) + _dot(tri, mid) + _dot(tri, lo) + run) * LOG2E
            c_ref[b * BLK:(b + 1) * BLK, :] = cs
            ct_ref[b] = cs.T[0:8, :]
            run = run + jnp.sum(lf, axis=0, keepdims=True)
        carry[...] = run

    return pl.pallas_call(
        body, name="f_foxprep", grid=(nblk // G,),
        in_specs=[pl.BlockSpec((G * BLK, BLK), lambda i: (i, 0)), _full((1, BLK))],
        out_specs=[pl.BlockSpec((G * BLK, BLK), lambda i: (i, 0)), pl.BlockSpec((G, 8, BLK), lambda i: (i, 0, 0))],
        out_shape=[jax.ShapeDtypeStruct((L, BLK), F32), jax.ShapeDtypeStruct((nblk, 8, BLK), F32)],
        scratch_shapes=[pltpu.VMEM((1, BLK), F32)],
        compiler_params=_params(("arbitrary",)),
    )(ff, fb)


def _rot_fns(cos, sin):
    lane = _iota((BLK, BLK), 1)
    first = (lane & (HEAD_LANES - 1)) < HEAD_LANES // 2

    def swap(x):
        return jnp.where(first, pltpu.roll(x, BLK - 32, 1), pltpu.roll(x, 32, 1))

    def rot(x):
        return x * cos + swap(x) * sin

    def rot_t(dy):
        return dy * cos + swap(dy * sin)

    return rot, rot_t


def _retention_fwd(proj, cos_t, sin_t, ret_g):
    L = proj.shape[0]
    nblk = L // BLK
    G = _block_group(nblk)
    dmat, wq_t, wk_t, g_blk = _decay_tables()

    def body(q_ref, k_ref, v_ref, gate_ref, cos_ref, sin_ref, d_ref, wq_ref, wk_ref, rg_ref,
             mix_ref, o_ref, rs_ref, state):
        @pl.when(pl.program_id(0) == 0)
        def _():
            state[...] = jnp.zeros_like(state)

        lane = _iota((BLK, BLK), 1)
        sub = _iota((BLK, BLK), 0)
        for b in range(G):
            rows = slice(b * BLK, (b + 1) * BLK)
            rot, _ = _rot_fns(cos_ref[rows, :], sin_ref[rows, :])
            for p in range(2):
                qr = rot(q_ref[rows, p * BLK:(p + 1) * BLK].astype(F32))
                kr = rot(k_ref[rows, p * BLK:(p + 1) * BLK].astype(F32)) * (HEAD_LANES ** -0.5)
                kr_b = kr.astype(BF16)
                qw = (qr * wq_ref[p]).astype(BF16)
                kw = (kr * wk_ref[p]).astype(BF16)
                for e in range(2):
                    h = 2 * p + e
                    cols = slice(h * BLK, (h + 1) * BLK)
                    qm = jnp.where((lane >> 6) == e, qr, 0.0).astype(BF16)
                    s = _dot_nt(qm, kr_b) * d_ref[h]
                    vh = v_ref[rows, cols]
                    st = state[h]
                    rs_ref[b, h] = st
                    o = _dot(s.astype(BF16), vh) + _dot(qw, st.astype(BF16))
                    u = jnp.where((sub >> 6) == e, _dot_tn(kw, vh), 0.0)
                    state[h] = g_blk[h] * st + u
                    rn = lax.rsqrt(jnp.mean(o * o, axis=-1, keepdims=True) + EPS)
                    gate = gate_ref[rows, cols].astype(F32)
                    o_ref[rows, cols] = o
                    mix_ref[rows, cols] = (o * rn * rg_ref[:, cols] * (gate * jax.nn.sigmoid(gate))).astype(BF16)

    row = lambda c: (lambda i: (i, c))
    return pl.pallas_call(
        body, name="f_retention", grid=(nblk // G,),
        in_specs=[pl.BlockSpec((G * BLK, 256), row(0)), pl.BlockSpec((G * BLK, 256), row(1)),
                  pl.BlockSpec((G * BLK, 512), row(1)), pl.BlockSpec((G * BLK, 512), row(2)),
                  pl.BlockSpec((G * BLK, BLK), row(0)), pl.BlockSpec((G * BLK, BLK), row(0)),
                  _full((RET_HEADS, BLK, BLK)), _full((2, BLK, BLK)), _full((2, BLK, BLK)), _full((1, 512))],
        out_specs=[pl.BlockSpec((G * BLK, 512), row(0)), pl.BlockSpec((G * BLK, 512), row(0)),
                   pl.BlockSpec((G, RET_HEADS, BLK, BLK), lambda i: (i, 0, 0, 0))],
        out_shape=[jax.ShapeDtypeStruct((L, 512), BF16), jax.ShapeDtypeStruct((L, 512), F32),
                   jax.ShapeDtypeStruct((nblk, RET_HEADS, BLK, BLK), F32)],
        scratch_shapes=[pltpu.VMEM((RET_HEADS, BLK, BLK), F32)],
        compiler_params=_params(("arbitrary",)),
    )(proj, proj, proj, proj, cos_t, sin_t, dmat, wq_t, wk_t, ret_g)


def _fox_units(L):
    nblk = L // BLK
    assert L % BLK == 0 and nblk % 2 == 1, "sequence must be one 128-row block plus whole 256-row tiles"
    return nblk, (nblk - 1) // 2


def _fox_tile_masks():
    sub, lane = _iota((BLK, BLK), 0), _iota((BLK, BLK), 1)
    valid = _iota((BLK, UNIT), 0) >= N_PAD
    diag = _iota((UNIT, UNIT), 0) <= _iota((UNIT, UNIT), 1)
    r, q = _iota((BLK + UNIT, UNIT), 0), _iota((BLK + UNIT, UNIT), 1)
    first_and_diag = ((r < BLK) & (r >= N_PAD)) | ((r >= BLK) & (r - BLK <= q))
    return dict(first=(sub <= lane) & (sub >= N_PAD), valid=valid, diag=diag, first_and_diag=first_and_diag)


def _fox_fwd(proj, c, ctb, gather=()):
    L = proj.shape[0]
    nblk, nu = _fox_units(L)
    scale = HEAD_LANES ** -0.5 * LOG2E
    ng = len(gather)
    steps = FOX_HEADS // (2 * FOX_PAIRS)

    def body(qkv_ref, c_ref, ct_ref, *rest):
        g_in, (of_ref, lse_ref), g_out = rest[:ng], rest[ng:ng + 2], rest[ng + 2:2 * ng + 2]
        vt, csb = rest[2 * ng + 2:2 * ng + 4]
        p = pl.program_id(0)
        heads = [(pp, e, 2 * FOX_PAIRS * p + 2 * pp + e) for pp in range(FOX_PAIRS) for e in range(2)]

        @pl.when(p == 0)
        def _():
            lse_ref[...] = jnp.zeros_like(lse_ref)
            if ng:
                local, sends, _ = _allgather_copies(g_in, g_out, *rest[2 * ng + 4:])
                for cp in local + sends:
                    cp.start()

        lane = _iota((BLK, BLK), 1)
        sub8 = _iota((8, BLK), 0)
        masks = _fox_tile_masks()

        def pre(j, carry):
            off = pl.multiple_of(j * BLK, BLK)
            ct = c_ref[pl.ds(off, BLK), :]
            for pp in range(FOX_PAIRS):
                vt[pp, j] = qkv_ref[pl.ds(off, BLK), pp * 384 + 2 * BLK:pp * 384 + 3 * BLK].astype(F32).T.astype(BF16)
            for hh, (_, _, h) in enumerate(heads):
                col = jnp.sum(jnp.where(lane == h, ct, 0.0), axis=1, keepdims=True)
                csb[hh, j] = jnp.broadcast_to(col, (BLK, BLK))
            return carry

        lax.fori_loop(0, nblk, pre, 0)

        def attend(qblk, nq, n_whole):
            qlen = nq * BLK
            qoff = pl.multiple_of(qblk * BLK, BLK)
            qlane = _iota((qlen, BLK), 1)
            qs = [qkv_ref[pl.ds(qoff, qlen), pp * 384:pp * 384 + BLK].astype(F32) * scale for pp in range(FOX_PAIRS)]
            qm = [jnp.where((qlane >> 6) == e, qs[pp], 0.0).astype(BF16) for pp, e, _ in heads]
            ct_row = [jnp.concatenate([_pick_row(ct_ref[qblk + a], h) for a in range(nq)], axis=1) for _, _, h in heads]

            def step(segs, mask, st):
                blocks = [kblk + b for kblk, nk in segs for b in range(nk)]
                kts = []
                for pp in range(FOX_PAIRS):
                    kt = [qkv_ref[pl.ds(pl.multiple_of(kblk * BLK, BLK), nk * BLK), pp * 384 + BLK:pp * 384 + 2 * BLK]
                          for kblk, nk in segs]
                    kts.append(kt[0] if len(kt) == 1 else jnp.concatenate(kt, axis=0))
                out = []
                for hh, (pp, e, _) in enumerate(heads):
                    m, l, acc = st[3 * hh:3 * hh + 3]
                    s = _dot_nt(kts[pp], qm[hh])
                    t = jnp.concatenate([s[b * BLK:(b + 1) * BLK] - jnp.concatenate([csb[hh, blk]] * nq, axis=1)
                                         for b, blk in enumerate(blocks)], axis=0)
                    if mask is not None:
                        t = jnp.where(mask, t, NEG)
                    m_new = jnp.maximum(m, jnp.max(t, axis=0, keepdims=True) + ct_row[hh])
                    alpha = jnp.exp2(m - m_new)
                    pr = jnp.exp2(t - (m_new - ct_row[hh]))
                    l = alpha * l + jnp.sum(pr, axis=0, keepdims=True)
                    pr_b = pr.astype(BF16)
                    pv = None
                    for b, blk in enumerate(blocks):
                        part = _dot(vt[pp, blk, e * HEAD_LANES:(e + 1) * HEAD_LANES, :], pr_b[b * BLK:(b + 1) * BLK])
                        pv = part if pv is None else pv + part
                    out += [m_new, l, alpha * acc + pv]
                return tuple(out)

            st = (jnp.full((1, qlen), NEG, F32), jnp.zeros((1, qlen), F32),
                  jnp.zeros((HEAD_LANES, qlen), F32)) * len(heads)
            if nq == 1:
                st = step([(0, 1)], masks["first"], st)
            else:
                st = step([(0, 1), (qblk, 2)], masks["first_and_diag"], st)
                n_wide = n_whole // WIDE
                st = lax.fori_loop(0, n_wide, lambda j, s_: step([(1 + 2 * WIDE * j, 2 * WIDE)], None, s_), st)
                rest = 1 + 2 * WIDE * n_wide
                st = lax.cond((n_whole & 2) != 0, lambda s_: step([(rest, 4)], None, s_), lambda s_: s_, st)
                st = lax.cond((n_whole & 1) != 0, lambda s_: step([(rest + 2 * (n_whole & 2), 2)], None, s_),
                              lambda s_: s_, st)
            for pp in range(FOX_PAIRS):
                lo, hi = st[6 * pp:6 * pp + 3], st[6 * pp + 3:6 * pp + 6]
                o_t = jnp.concatenate([lo[2] * (1.0 / lo[1]), hi[2] * (1.0 / hi[1])], axis=0)
                of_ref[pl.ds(qoff, qlen), pp * BLK:(pp + 1) * BLK] = o_t.T.astype(BF16)
            lse = [st[3 * hh] + jnp.log(st[3 * hh + 1]) * LOG2E for hh in range(len(heads))]
            for a in range(nq):
                upd = jnp.zeros((8, BLK), F32)
                for hh, (_, _, h) in enumerate(heads):
                    upd = upd + jnp.where(sub8 == h, lse[hh][:, a * BLK:(a + 1) * BLK], 0.0)
                lse_ref[qblk + a] = lse_ref[qblk + a] + upd

        attend(0, 1, 0)

        def q_loop(u, carry):
            attend(1 + 2 * u, 2, u)
            return carry

        lax.fori_loop(0, nu, q_loop, 0)

        if ng:
            @pl.when(p == steps - 1)
            def _():
                local, sends, recvs = _allgather_copies(g_in, g_out, *rest[2 * ng + 4:])
                for cp in recvs:
                    cp.wait_recv()
                for cp in sends:
                    cp.wait_send()
                for cp in local:
                    cp.wait()

    width = 384 * FOX_PAIRS
    return pl.pallas_call(
        body, name="f_fox", grid=(steps,),
        in_specs=[pl.BlockSpec((L, width), lambda p: (0, RET_W // width + p)), _full((L, BLK)), _full((nblk, 8, BLK))]
        + [_ANY] * ng,
        out_specs=[pl.BlockSpec((L, FOX_PAIRS * BLK), lambda p: (0, p)), _full((nblk, 8, BLK))] + [_ANY] * ng,
        out_shape=[jax.ShapeDtypeStruct((L, 512), BF16), jax.ShapeDtypeStruct((nblk, 8, BLK), F32)]
        + [jax.ShapeDtypeStruct((N_CHIPS,) + a.shape, a.dtype) for a in gather],
        scratch_shapes=[pltpu.VMEM((FOX_PAIRS, nblk, BLK, BLK), BF16), pltpu.VMEM((2 * FOX_PAIRS, nblk, BLK, BLK), F32)]
        + _allgather_semaphores(ng),
        compiler_params=_params(("arbitrary",)),
    )(proj, c, ctb, *gather)


def _outproj_up(mix_r, o_f, h0, w_out, ffn_g, w_up, conv_w, conv_b):
    L = h0.shape[0]
    tm = _row_tile(L)
    shard = w_up.shape[2]
    assert 2 * shard == D_FF
    cw = [conv_w[j:j + 1] for j in range(3)]
    resident = lambda shape: pl.BlockSpec(shape, lambda i: (0,) * len(shape), pipeline_mode=pl.Buffered(1))

    def body(mr_ref, of_ref, h0_ref, wo_ref, g_ref, wu_ref, cw0, cw1, cw2, cb_ref,
             h1_ref, n2_ref, up_ref, act_ref, acc_ref, halo):
        i = pl.program_id(0)

        @pl.when(i == 0)
        def _():
            halo[...] = jnp.zeros_like(halo)

        h1 = h0_ref[...] + _dot(mr_ref[...], wo_ref[0:512, :]) + _dot(of_ref[...], wo_ref[512:1024, :])
        h1_ref[...] = h1
        r = lax.rsqrt(jnp.mean(h1 * h1, axis=-1, keepdims=True) + EPS)
        n2 = (h1 * r * g_ref[...]).astype(BF16)
        n2_ref[...] = n2
        live = i * tm + _iota((tm, 1), 0) >= N_PAD
        for half in range(2):
            cols = slice(half * shard, (half + 1) * shard)
            a_b = _dot(n2, wu_ref[half]).astype(BF16)
            b_b = _dot(n2, wu_ref[2 + half]).astype(BF16)
            up_ref[:, cols] = a_b
            up_ref[:, D_FF + half * shard:D_FF + (half + 1) * shard] = b_b
            a = jnp.where(live, a_b.astype(F32), 0.0)
            _, _, acc = _conv_taps(a, halo[:, cols], [cw0[:, cols], cw1[:, cols], cw2[:, cols]], cb_ref[:, cols])
            act_ref[:, cols] = (acc * jax.nn.sigmoid(acc) * b_b.astype(F32)).astype(BF16)
            acc_ref[:, cols] = acc.astype(BF16)
            halo[:, cols] = a[tm - 8:tm, :]

    rows = lambda w: pl.BlockSpec((tm, w), lambda i: (i, 0))
    return pl.pallas_call(
        body, name="f_outproj_up", grid=(L // tm,),
        in_specs=[rows(512), rows(512), rows(D_MODEL), resident((D_MODEL, D_MODEL)), _full((1, D_MODEL)),
                  resident((N_CHIPS, D_MODEL, shard)), _full((1, D_FF)), _full((1, D_FF)), _full((1, D_FF)),
                  _full((1, D_FF))],
        out_specs=[rows(D_MODEL), rows(D_MODEL), rows(2 * D_FF), rows(D_FF), rows(D_FF)],
        out_shape=[jax.ShapeDtypeStruct((L, D_MODEL), F32), jax.ShapeDtypeStruct((L, D_MODEL), BF16),
                   jax.ShapeDtypeStruct((L, 2 * D_FF), BF16), jax.ShapeDtypeStruct((L, D_FF), BF16),
                   jax.ShapeDtypeStruct((L, D_FF), BF16)],
        scratch_shapes=[pltpu.VMEM((8, D_FF), F32)],
        compiler_params=_params(("arbitrary",)),
    )(mix_r, o_f, h0, w_out, ffn_g, w_up, cw[0], cw[1], cw[2], conv_b)


def _conv_taps(a, halo, cw, cb):
    sub = _iota((a.shape[0], 1), 0)
    a1 = jnp.where(sub == 0, _pick_row(halo, 7), pltpu.roll(a, 1, 0))
    a2 = jnp.where(sub == 0, _pick_row(halo, 6), jnp.where(sub == 1, _pick_row(halo, 7), pltpu.roll(a, 2, 0)))
    acc = cb + a2 * cw[0]
    acc = acc + a1 * cw[1]
    acc = acc + a * cw[2]
    return a1, a2, acc


def _ffn_down_loss(g_act, w_down, h1, final_g, target):
    L = h1.shape[0]
    tm = _row_tile(L)
    nb = tm // BLK

    def body(g_ref, wd_ref, h1_ref, gf_ref, *refs):
        t_refs, (dh_ref, dhb_ref, dgf_ref, loss_ref) = refs[:nb], refs[nb:]
        i = pl.program_id(0)

        @pl.when(i == 0)
        def _():
            dgf_ref[...] = jnp.zeros_like(dgf_ref)
            loss_ref[...] = jnp.zeros_like(loss_ref)

        h2 = h1_ref[...] + _dot(g_ref[...], wd_ref[...])
        r = lax.rsqrt(jnp.mean(h2 * h2, axis=-1, keepdims=True) + EPS)
        yn = h2 * r
        gf = gf_ref[...]
        live = i * tm + _iota((tm, 1), 0) >= PREFIX
        target = jnp.concatenate([t[...] for t in t_refs], axis=0)
        err = jnp.where(live, yn * gf - target, 0.0)
        loss_ref[...] = loss_ref[...] + 0.5 * jnp.sum(jnp.mean(err * err, axis=-1, keepdims=True))
        dy = err * (1.0 / D_MODEL)
        dgf_ref[...] = dgf_ref[...] + jnp.sum(dy * yn, axis=0, keepdims=True)
        dyn = dy * gf
        dh = r * (dyn - yn * jnp.mean(dyn * yn, axis=-1, keepdims=True))
        dh_ref[...] = dh
        dhb_ref[...] = dh.astype(BF16)

    rows = lambda w: pl.BlockSpec((tm, w), lambda i: (i, 0))
    return pl.pallas_call(
        body, name="f_ffn_down_loss", grid=(L // tm,),
        in_specs=[rows(D_FF), _full((D_FF, D_MODEL)), rows(D_MODEL), _full((1, D_MODEL))] + _shifted_blocks(tm),
        out_specs=[rows(D_MODEL), rows(D_MODEL), _full((1, D_MODEL)), _full((1, BLK))],
        out_shape=[jax.ShapeDtypeStruct((L, D_MODEL), F32), jax.ShapeDtypeStruct((L, D_MODEL), BF16),
                   jax.ShapeDtypeStruct((1, D_MODEL), F32), jax.ShapeDtypeStruct((1, BLK), F32)],
        compiler_params=_params(("arbitrary",)),
    )(g_act, w_down, h1, final_g, *([target] * nb))


def _ffn_bwd_gate(dh2b, w_down, acc_saved, up):
    L = dh2b.shape[0]
    tm = _row_tile(L)

    def body(dh_ref, wd_ref, acc_ref, b_ref, dacc_ref, db_ref):
        acc = acc_ref[...].astype(F32)
        dg = _dot_nt(dh_ref[...], wd_ref[...])
        sg = jax.nn.sigmoid(acc)
        silu = acc * sg
        db_ref[...] = (dg * silu).astype(BF16)
        dacc_ref[...] = (dg * b_ref[...].astype(F32) * (sg + silu * (1.0 - sg))).astype(BF16)

    rows = lambda w, c=0: pl.BlockSpec((tm, w), lambda i: (i, c))
    return pl.pallas_call(
        body, name="b_ffn_gate", grid=(L // tm,),
        in_specs=[rows(D_MODEL), _full((D_FF, D_MODEL)), rows(D_FF), rows(D_FF, 1)],
        out_specs=[rows(D_FF), rows(D_FF)],
        out_shape=[jax.ShapeDtypeStruct((L, D_FF), BF16), jax.ShapeDtypeStruct((L, D_FF), BF16)],
        compiler_params=_params(("parallel",)),
    )(dh2b, w_down, acc_saved, up)


def _ffn_bwd_up(dacc, db, up, conv_w, w_up, h1, ffn_g, dh2, w_out):
    L = h1.shape[0]
    tm = _row_tile(L)
    nt = L // tm
    shard = w_up.shape[2]
    cw = [conv_w[j:j + 1] for j in range(3)]

    def body(da_ref, halo_ref, db_ref, a_ref, cw0, cw1, cw2, wu_ref, h1_ref, g_ref, dh2_ref, wo_ref,
             dup_ref, dh1_ref, dh1b_ref, dmix_ref, dg_ref, dcw_ref):
        i = pl.program_id(0)

        @pl.when(i == 0)
        def _():
            dg_ref[...] = jnp.zeros_like(dg_ref)
            dcw_ref[...] = jnp.zeros_like(dcw_ref)

        sub = _iota((tm, 1), 0)
        live = i * tm + sub >= N_PAD
        d0 = da_ref[...].astype(F32)
        halo = jnp.where(i < nt - 1, halo_ref[...].astype(F32), 0.0)
        d1 = jnp.where(sub == tm - 1, _pick_row(halo, 0), pltpu.roll(d0, tm - 1, 0))
        d2 = jnp.where(sub == tm - 2, _pick_row(halo, 0),
                       jnp.where(sub == tm - 1, _pick_row(halo, 1), pltpu.roll(d0, tm - 2, 0)))
        a = jnp.where(live, a_ref[...].astype(F32), 0.0)
        sub8 = _iota((8, 1), 0)
        upd = jnp.zeros((8, D_FF), F32)
        for j, t in enumerate((d2 * a, d1 * a, d0 * a, d0)):
            upd = upd + jnp.where(sub8 == j, jnp.sum(t, axis=0, keepdims=True), 0.0)
        dcw_ref[...] = dcw_ref[...] + upd
        da = d0 * cw2[...] + d1 * cw1[...] + d2 * cw0[...]
        da = jnp.where(live, da, 0.0).astype(BF16)
        dup_ref[:, 0:D_FF] = da
        dbv = db_ref[...]
        dup_ref[:, D_FF:2 * D_FF] = dbv
        dn = jnp.zeros((tm, D_MODEL), F32)
        for j in range(N_CHIPS):
            src = da if j < 2 else dbv
            lo = (j % 2) * shard
            dn = dn + _dot_nt(src[:, lo:lo + shard], wu_ref[j])
        h1 = h1_ref[...]
        r = lax.rsqrt(jnp.mean(h1 * h1, axis=-1, keepdims=True) + EPS)
        yn = h1 * r
        dg_ref[...] = dg_ref[...] + jnp.sum(dn * yn, axis=0, keepdims=True)
        dyn = dn * g_ref[...]
        dh1 = dh2_ref[...] + r * (dyn - yn * jnp.mean(dyn * yn, axis=-1, keepdims=True))
        dh1_ref[...] = dh1
        dh1b = dh1.astype(BF16)
        dh1b_ref[...] = dh1b
        dmix_ref[...] = _dot_nt(dh1b, wo_ref[...]).astype(BF16)

    rows = lambda w: pl.BlockSpec((tm, w), lambda i: (i, 0))
    halo = pl.BlockSpec((8, D_FF), lambda i: (jnp.minimum((i + 1) * (tm // 8), L // 8 - 1), 0))
    return pl.pallas_call(
        body, name="b_ffn_up", grid=(nt,),
        in_specs=[rows(D_FF), halo, rows(D_FF), rows(D_FF), _full((1, D_FF)), _full((1, D_FF)), _full((1, D_FF)),
                  _full((N_CHIPS, D_MODEL, shard)), rows(D_MODEL), _full((1, D_MODEL)), rows(D_MODEL),
                  _full((D_MODEL, D_MODEL))],
        out_specs=[rows(2 * D_FF), rows(D_MODEL), rows(D_MODEL), rows(D_MODEL), _full((1, D_MODEL)),
                   _full((8, D_FF))],
        out_shape=[jax.ShapeDtypeStruct((L, 2 * D_FF), BF16), jax.ShapeDtypeStruct((L, D_MODEL), F32),
                   jax.ShapeDtypeStruct((L, D_MODEL), BF16), jax.ShapeDtypeStruct((L, D_MODEL), BF16),
                   jax.ShapeDtypeStruct((1, D_MODEL), F32), jax.ShapeDtypeStruct((8, D_FF), F32)],
        compiler_params=_params(("arbitrary",)),
    )(dacc, dacc, db, up, cw[0], cw[1], cw[2], w_up, h1, ffn_g, dh2, w_out)


def _wgrad(a, b, name, tn=None, tk=None):
    L, K = a.shape
    N = b.shape[1]
    tn = N if tn is None else tn
    tk = K if tk is None else tk
    tl = _row_tile(L, (1408, 768, 512, 256, 128))

    def body(a_ref, b_ref, o_ref):
        @pl.when(pl.program_id(2) == 0)
        def _():
            o_ref[...] = jnp.zeros_like(o_ref)

        o_ref[0] = o_ref[0] + _dot_tn(a_ref[...], b_ref[...])

    return pl.pallas_call(
        body, name=name, grid=(N // tn, K // tk, L // tl),
        in_specs=[pl.BlockSpec((tl, tk), lambda n, k, l: (l, k)), pl.BlockSpec((tl, tn), lambda n, k, l: (l, n))],
        out_specs=pl.BlockSpec((1, tk, tn), lambda n, k, l: (n, k, 0)),
        out_shape=jax.ShapeDtypeStruct((N // tn, K, tn), F32),
        compiler_params=_params(("parallel", "parallel", "arbitrary")),
    )(a, b)


def _retention_bwd(dmix, o, proj, cos_t, sin_t, ret_g, states, exchange=()):
    L = proj.shape[0]
    nblk = L // BLK
    G = _block_group(nblk)
    steps = nblk // G
    nx = len(exchange)
    dmat, wq_t, wk_t, g_blk = _decay_tables()

    def body(dm_ref, o_ref, q_ref, k_ref, v_ref, gate_ref, cos_ref, sin_ref, d_ref, wq_ref, wk_ref, rg_ref, rs_ref,
             *rest):
        x_in, (dp_ref, drg_ref), x_out, gstate = rest[:nx], rest[nx:nx + 2], rest[nx + 2:2 * nx + 2], rest[2 * nx + 2]

        @pl.when(pl.program_id(0) == 0)
        def _():
            if nx:
                for cp in _sibling_half_copies(x_in, x_out, *rest[2 * nx + 3:])[0]:
                    cp.start()
            gstate[...] = jnp.zeros_like(gstate)
            drg_ref[...] = jnp.zeros_like(drg_ref)

        lane = _iota((BLK, BLK), 1)
        sub = _iota((BLK, BLK), 0)
        scale = HEAD_LANES ** -0.5
        for b in reversed(range(G)):
            rows = slice(b * BLK, (b + 1) * BLK)
            rot, rot_t = _rot_fns(cos_ref[rows, :], sin_ref[rows, :])
            for p in range(2):
                qr = rot(q_ref[rows, p * BLK:(p + 1) * BLK].astype(F32))
                kr = rot(k_ref[rows, p * BLK:(p + 1) * BLK].astype(F32)) * scale
                kr_b = kr.astype(BF16)
                qw = (qr * wq_ref[p]).astype(BF16)
                kw = (kr * wk_ref[p]).astype(BF16)
                dqr = jnp.zeros((BLK, BLK), F32)
                dkr = jnp.zeros((BLK, BLK), F32)
                for e in range(2):
                    h = 2 * p + e
                    cols = slice(h * BLK, (h + 1) * BLK)
                    head_lanes = (lane >> 6) == e
                    o = o_ref[rows, cols]
                    rn = lax.rsqrt(jnp.mean(o * o, axis=-1, keepdims=True) + EPS)
                    y = o * rn
                    gate = gate_ref[rows, cols].astype(F32)
                    sg = jax.nn.sigmoid(gate)
                    dm = dm_ref[rows, cols].astype(F32)
                    rgain = rg_ref[:, cols]
                    drg_ref[:, cols] = drg_ref[:, cols] + jnp.sum(dm * y * (gate * sg), axis=0, keepdims=True)
                    dp_ref[rows, 1024 + h * BLK:1024 + (h + 1) * BLK] = (
                        dm * y * rgain * (sg * (1.0 + gate * (1.0 - sg)))).astype(BF16)
                    dy = dm * rgain * (gate * sg)
                    do = (rn * (dy - y * jnp.mean(dy * y, axis=-1, keepdims=True))).astype(BF16)
                    vh = v_ref[rows, cols]
                    qm = jnp.where(head_lanes, qr, 0.0).astype(BF16)
                    dmh = d_ref[h]
                    s = (_dot_nt(qm, kr_b) * dmh).astype(BF16)
                    ds = (_dot_nt(do, vh) * dmh).astype(BF16)
                    st = rs_ref[b, h].astype(BF16)
                    gs = gstate[h]
                    gs_b = gs.astype(BF16)
                    dqr = dqr + jnp.where(head_lanes, _dot(ds, kr_b), 0.0) + _dot_nt(do, st) * wq_ref[p]
                    dkr = dkr + _dot_tn(ds, qm) + _dot_nt(vh, gs_b) * wk_ref[p]
                    dp_ref[rows, 512 + h * BLK:512 + (h + 1) * BLK] = (_dot_tn(s, do) + _dot(kw, gs_b)).astype(BF16)
                    dr = jnp.where((sub >> 6) == e, _dot_tn(qw, do), 0.0)
                    gstate[h] = dr + g_blk[h] * gs
                dp_ref[rows, p * BLK:(p + 1) * BLK] = rot_t(dqr).astype(BF16)
                dp_ref[rows, 256 + p * BLK:256 + (p + 1) * BLK] = (rot_t(dkr) * scale).astype(BF16)

        if nx:
            @pl.when(pl.program_id(0) == steps - 1)
            def _():
                sends, recvs = _sibling_half_copies(x_in, x_out, *rest[2 * nx + 3:])
                for cp in recvs:
                    cp.wait_recv()
                for cp in sends:
                    cp.wait_send()

    row = lambda c: (lambda i: (steps - 1 - i, c))
    return pl.pallas_call(
        body, name="b_retention", grid=(steps,),
        in_specs=[pl.BlockSpec((G * BLK, 512), row(0)), pl.BlockSpec((G * BLK, 512), row(0)),
                  pl.BlockSpec((G * BLK, 256), row(0)), pl.BlockSpec((G * BLK, 256), row(1)),
                  pl.BlockSpec((G * BLK, 512), row(1)), pl.BlockSpec((G * BLK, 512), row(2)),
                  pl.BlockSpec((G * BLK, BLK), row(0)), pl.BlockSpec((G * BLK, BLK), row(0)),
                  _full((RET_HEADS, BLK, BLK)), _full((2, BLK, BLK)), _full((2, BLK, BLK)), _full((1, 512)),
                  pl.BlockSpec((G, RET_HEADS, BLK, BLK), lambda i: (steps - 1 - i, 0, 0, 0))] + [_ANY] * nx,
        out_specs=[pl.BlockSpec((G * BLK, RET_W), row(0)), _full((1, 512))] + [_ANY] * nx,
        out_shape=[jax.ShapeDtypeStruct((L, RET_W), BF16), jax.ShapeDtypeStruct((1, 512), F32)]
        + _sibling_half_shapes(exchange),
        scratch_shapes=[pltpu.VMEM((RET_HEADS, BLK, BLK), F32)] + _sibling_half_semaphores(nx),
        compiler_params=_params(("arbitrary",)),
    )(dmix, o, proj, proj, proj, proj, cos_t, sin_t, dmat, wq_t, wk_t, ret_g, states, *exchange)


def _fox_delta(dmix, o_f):
    L = o_f.shape[0]
    nblk = L // BLK
    G = _block_group(nblk)

    def body(do_ref, o_ref, d_ref):
        sel = ((_iota((8, 512), 1) >> 6) == _iota((8, 512), 0)).astype(BF16)
        for b in range(G):
            rows = slice(b * BLK, (b + 1) * BLK)
            prod = do_ref[rows, :].astype(F32) * o_ref[rows, :].astype(F32)
            hi = prod.astype(BF16)
            lo = (prod - hi.astype(F32)).astype(BF16)
            d_ref[b] = _dot_nt(sel, hi) + _dot_nt(sel, lo)

    return pl.pallas_call(
        body, name="b_foxdelta", grid=(nblk // G,),
        in_specs=[pl.BlockSpec((G * BLK, 512), lambda i: (i, 1)), pl.BlockSpec((G * BLK, 512), lambda i: (i, 0))],
        out_specs=pl.BlockSpec((G, 8, BLK), lambda i: (i, 0, 0)),
        out_shape=jax.ShapeDtypeStruct((nblk, 8, BLK), F32),
        compiler_params=_params(("parallel",)),
    )(dmix, o_f)


def _fox_bwd(proj, dmix, c, ctb, lse, delta, scatter=()):
    L = proj.shape[0]
    nblk, nu = _fox_units(L)
    scale = HEAD_LANES ** -0.5
    ns = len(scatter)

    steps = FOX_HEADS // (2 * FOX_PAIRS)

    def body(qkv_ref, do_ref, c_ref, ct_ref, lse_ref, dl_ref, *rest):
        s_in, (dp_ref, dc_ref, dcq_ref), s_out = rest[:ns], rest[ns:ns + 3], rest[ns + 3:2 * ns + 3]
        ktt, dqt, dk_acc, dv_acc, dcs_acc = rest[2 * ns + 3:2 * ns + 8]
        p = pl.program_id(0)
        heads = [(pp, e, 2 * FOX_PAIRS * p + 2 * pp + e) for pp in range(FOX_PAIRS) for e in range(2)]

        @pl.when(p == 0)
        def _():
            dc_ref[...] = jnp.zeros_like(dc_ref)
            dcq_ref[...] = jnp.zeros_like(dcq_ref)
            if ns:
                for cp in _scatter_copies(s_in, s_out, *rest[2 * ns + 8:]):
                    cp.start()

        sub8 = _iota((8, BLK), 0)
        masks = _fox_tile_masks()

        def pre(j, carry):
            off = pl.multiple_of(j * BLK, BLK)
            for pp in range(FOX_PAIRS):
                ktt[pp, j] = qkv_ref[pl.ds(off, BLK), pp * 384 + BLK:pp * 384 + 2 * BLK].astype(F32).T.astype(BF16)
                dqt[pp, j] = jnp.zeros((BLK, BLK), F32)
            return carry

        lax.fori_loop(0, nblk, pre, 0)

        def kv_pass(kblk, nk, n_later):
            klen = nk * BLK
            koff = pl.multiple_of(kblk * BLK, BLK)
            kt = [qkv_ref[pl.ds(koff, klen), pp * 384 + BLK:pp * 384 + 2 * BLK] for pp in range(FOX_PAIRS)]
            vtile = [qkv_ref[pl.ds(koff, klen), pp * 384 + 2 * BLK:pp * 384 + 3 * BLK] for pp in range(FOX_PAIRS)]
            ct = c_ref[pl.ds(koff, klen), :]
            klane = _iota((klen, BLK), 1)
            cs = [jnp.broadcast_to(jnp.sum(jnp.where(klane == h, ct, 0.0), axis=1, keepdims=True), (klen, WIDE * UNIT))
                  for _, _, h in heads]
            for pp in range(FOX_PAIRS):
                dk_acc[pp, 0:klen] = jnp.zeros((klen, BLK), F32)
                dv_acc[pp, 0:klen] = jnp.zeros((klen, BLK), F32)
            for hh in range(len(heads)):
                dcs_acc[hh, 0:klen] = jnp.zeros((klen, BLK), F32)

            def tile(qblk, nq, mask):
                qlen = nq * BLK
                if mask == "valid":
                    mask = _iota((klen, qlen), 0) >= N_PAD
                qoff = pl.multiple_of(qblk * BLK, BLK)
                qlane = _iota((qlen, BLK), 1)
                qs = [qkv_ref[pl.ds(qoff, qlen), pp * 384:pp * 384 + BLK].astype(F32) * (scale * LOG2E)
                      for pp in range(FOX_PAIRS)]
                dot_ = [do_ref[pl.ds(qoff, qlen), pp * BLK:(pp + 1) * BLK] for pp in range(FOX_PAIRS)]
                stats = [[ref[qblk + a] for a in range(nq)] for ref in (ct_ref, lse_ref, dl_ref)]
                dcq = [jnp.zeros((8, BLK), F32) for _ in range(nq)]
                for hh, (pp, e, h) in enumerate(heads):
                    head = (qlane >> 6) == e
                    ct_row, lse_row, dl_row = [jnp.concatenate([_pick_row(t, h) for t in ts], axis=1) for ts in stats]
                    qm = jnp.where(head, qs[pp], 0.0).astype(BF16)
                    dom = jnp.where(head, dot_[pp], jnp.zeros_like(dot_[pp]))
                    t = _dot_nt(kt[pp], qm) - cs[hh][:, 0:qlen]
                    if mask is not None:
                        t = jnp.where(mask, t, NEG)
                    pr = jnp.exp2(t + (ct_row - lse_row))
                    dv_acc[pp, 0:klen] = dv_acc[pp, 0:klen] + _dot(pr.astype(BF16), dom)
                    dsv = pr * (_dot_nt(vtile[pp], dom) - dl_row)
                    ds_b = dsv.astype(BF16)
                    dk_acc[pp, 0:klen] = dk_acc[pp, 0:klen] + _dot(ds_b, qm)
                    rows = slice(e * HEAD_LANES, (e + 1) * HEAD_LANES)
                    dq_t = _dot(ktt[pp, kblk, rows, :], ds_b[0:BLK])
                    for b in range(1, nk):
                        dq_t = dq_t + _dot(ktt[pp, kblk + b, rows, :], ds_b[b * BLK:(b + 1) * BLK])
                    key_side = dsv[:, 0:BLK]
                    for a in range(1, nq):
                        key_side = key_side + dsv[:, a * BLK:(a + 1) * BLK]
                    dcs_acc[hh, 0:klen] = dcs_acc[hh, 0:klen] + key_side
                    query_side = jnp.sum(dsv, axis=0, keepdims=True)
                    for a in range(nq):
                        cols = slice(a * BLK, (a + 1) * BLK)
                        dqt[pp, qblk + a, rows, :] = dqt[pp, qblk + a, rows, :] + dq_t[:, cols]
                        dcq[a] = dcq[a] + jnp.where(sub8 == h, query_side[:, cols], 0.0)
                for a in range(nq):
                    dcq_ref[qblk + a] = dcq_ref[qblk + a] + dcq[a]

            later_mask = "valid" if nk == 1 else None
            n_later = jnp.asarray(n_later, jnp.int32)
            n_wide = n_later // WIDE

            def later_wide(i, carry):
                tile(kblk + nk + 2 * WIDE * i, 2 * WIDE, later_mask)
                return carry

            tile(kblk, nk, masks["first"] if nk == 1 else masks["diag"])
            lax.fori_loop(0, n_wide, later_wide, 0)
            rest_blk = kblk + nk + 2 * WIDE * n_wide

            @pl.when((n_later & 2) != 0)
            def _():
                tile(rest_blk, 4, later_mask)

            @pl.when((n_later & 1) != 0)
            def _():
                tile(rest_blk + 2 * (n_later & 2), 2, later_mask)

            upd = jnp.zeros((klen, BLK), F32)
            for hh, (_, _, h) in enumerate(heads):
                upd = upd + jnp.where(klane == h, -jnp.sum(dcs_acc[hh, 0:klen], axis=1, keepdims=True), 0.0)
            dc_ref[pl.ds(koff, klen), :] = dc_ref[pl.ds(koff, klen), :] + upd
            for pp in range(FOX_PAIRS):
                dp_ref[pl.ds(koff, klen), pp * 384 + BLK:pp * 384 + 2 * BLK] = (
                    dk_acc[pp, 0:klen] * (1.0 / LOG2E)).astype(BF16)
                dp_ref[pl.ds(koff, klen), pp * 384 + 2 * BLK:pp * 384 + 3 * BLK] = dv_acc[pp, 0:klen].astype(BF16)

        kv_pass(0, 1, nu)

        def k_loop(u, carry):
            kv_pass(1 + 2 * u, 2, nu - 1 - u)
            return carry

        lax.fori_loop(0, nu, k_loop, 0)

        def flush(j, carry):
            off = pl.multiple_of(j * BLK, BLK)
            for pp in range(FOX_PAIRS):
                dp_ref[pl.ds(off, BLK), pp * 384:pp * 384 + BLK] = (dqt[pp, j].T * scale).astype(BF16)
            return carry

        lax.fori_loop(0, nblk, flush, 0)

        if ns:
            @pl.when(p == steps - 1)
            def _():
                copies = _scatter_copies(s_in, s_out, *rest[2 * ns + 8:])
                for cp in copies:
                    cp.wait_recv()
                for cp in copies:
                    cp.wait_send()

    width = 384 * FOX_PAIRS
    once = lambda shape, index: pl.BlockSpec(shape, index, pipeline_mode=pl.Buffered(1))
    stat = once((nblk, 8, BLK), lambda p: (0, 0, 0))
    return pl.pallas_call(
        body, name="b_fox", grid=(steps,),
        in_specs=[once((L, width), lambda p: (0, RET_W // width + p)),
                  once((L, FOX_PAIRS * BLK), lambda p: (0, 4 // FOX_PAIRS + p)),
                  once((L, BLK), lambda p: (0, 0)), stat, stat, stat] + [_ANY] * ns,
        out_specs=[pl.BlockSpec((L, width), lambda p: (0, p)), _full((L, BLK)), _full((nblk, 8, BLK))] + [_ANY] * ns,
        out_shape=[jax.ShapeDtypeStruct((L, FOX_W), BF16), jax.ShapeDtypeStruct((L, BLK), F32),
                   jax.ShapeDtypeStruct((nblk, 8, BLK), F32)] + _scatter_shapes(scatter),
        scratch_shapes=[pltpu.VMEM((FOX_PAIRS, nblk, BLK, BLK), BF16), pltpu.VMEM((FOX_PAIRS, nblk, BLK, BLK), F32),
                        pltpu.VMEM((FOX_PAIRS, UNIT, BLK), F32), pltpu.VMEM((FOX_PAIRS, UNIT, BLK), F32),
                        pltpu.VMEM((2 * FOX_PAIRS, UNIT, BLK), F32)]
        + _scatter_semaphores(ns),
        compiler_params=_params(("arbitrary",)),
    )(proj, dmix, c, ctb, lse, delta, *scatter)


def _fox_post(dc, dcq, ff, fb):
    L = dc.shape[0]
    nblk = L // BLK
    G = _block_group(nblk)
    steps = nblk // G

    def body(dc_ref, dcq_ref, ff_ref, b_ref, dff_ref, dffb_ref, dfb_ref, carry):
        @pl.when(pl.program_id(0) == 0)
        def _():
            carry[...] = jnp.zeros_like(carry)
            dfb_ref[...] = jnp.zeros_like(dfb_ref)

        tri = (_iota((BLK, BLK), 0) <= _iota((BLK, BLK), 1)).astype(BF16)
        live = _iota((BLK, BLK), 1) < FOX_HEADS
        run, dfb = carry[...], dfb_ref[...]
        for b in reversed(range(G)):
            rows = slice(b * BLK, (b + 1) * BLK)
            d = dc_ref[rows, :] + jnp.concatenate([dcq_ref[b], jnp.zeros((BLK - 8, BLK), F32)], axis=0).T
            hi, mid, lo = _split3(d)
            dlf = _dot(tri, hi) + _dot(tri, mid) + _dot(tri, lo) + run
            run = run + jnp.sum(d, axis=0, keepdims=True)
            z = ff_ref[rows, :] + b_ref[...]
            dff = jnp.where(live, dlf * jax.nn.sigmoid(-z), 0.0)
            dff_ref[rows, :] = dff
            dffb_ref[rows, :] = dff.astype(BF16)
            dfb = dfb + jnp.sum(dff, axis=0, keepdims=True)
        carry[...] = run
        dfb_ref[...] = dfb

    rev = lambda i: (steps - 1 - i, 0)
    return pl.pallas_call(
        body, name="b_foxpost", grid=(steps,),
        in_specs=[pl.BlockSpec((G * BLK, BLK), rev), pl.BlockSpec((G, 8, BLK), lambda i: (steps - 1 - i, 0, 0)),
                  pl.BlockSpec((G * BLK, BLK), rev), _full((1, BLK))],
        out_specs=[pl.BlockSpec((G * BLK, BLK), rev), pl.BlockSpec((G * BLK, BLK), rev), _full((1, BLK))],
        out_shape=[jax.ShapeDtypeStruct((L, BLK), F32), jax.ShapeDtypeStruct((L, BLK), BF16),
                   jax.ShapeDtypeStruct((1, BLK), F32)],
        scratch_shapes=[pltpu.VMEM((1, BLK), F32)],
        compiler_params=_params(("arbitrary",)),
    )(dc, dcq, ff, fb)


def _inproj_bwd(dpr, dpf, dffb, w_main, w_ff, h0, g, dh1, scatter=()):
    L = h0.shape[0]
    tm = _row_tile(L)
    ns = len(scatter)

    def body(dpr_ref, dpf_ref, dff_ref, wm_ref, wf_ref, h_ref, g_ref, dh1_ref, *rest):
        s_in, (dh0_ref, dg_ref), s_out = rest[:ns], rest[ns:ns + 2], rest[ns + 2:2 * ns + 2]

        @pl.when(pl.program_id(0) == 0)
        def _():
            dg_ref[...] = jnp.zeros_like(dg_ref)
            if ns:
                for cp in _scatter_copies(s_in, s_out, *rest[2 * ns + 2:]):
                    cp.start()

        dn = (_dot_nt(dpr_ref[...], wm_ref[:, 0:RET_W]) + _dot_nt(dpf_ref[...], wm_ref[:, RET_W:MAIN_W])
              + _dot_nt(dff_ref[...], wf_ref[...]))
        h = h_ref[...]
        r = lax.rsqrt(jnp.mean(h * h, axis=-1, keepdims=True) + EPS)
        yn = h * r
        dg_ref[...] = dg_ref[...] + jnp.sum(dn * yn, axis=0, keepdims=True)
        dyn = dn * g_ref[...]
        dh0_ref[...] = dh1_ref[...] + r * (dyn - yn * jnp.mean(dyn * yn, axis=-1, keepdims=True))

        if ns:
            @pl.when(pl.program_id(0) == L // tm - 1)
            def _():
                copies = _scatter_copies(s_in, s_out, *rest[2 * ns + 2:])
                for cp in copies:
                    cp.wait_recv()
                for cp in copies:
                    cp.wait_send()

    rows = lambda w: pl.BlockSpec((tm, w), lambda i: (i, 0))
    return pl.pallas_call(
        body, name="b_inproj", grid=(L // tm,),
        in_specs=[rows(RET_W), rows(FOX_W), rows(BLK), _full((D_MODEL, MAIN_W)), _full((D_MODEL, BLK)),
                  rows(D_MODEL), _full((1, D_MODEL)), rows(D_MODEL)] + [_ANY] * ns,
        out_specs=[rows(D_MODEL), _full((1, D_MODEL))] + [_ANY] * ns,
        out_shape=[jax.ShapeDtypeStruct((L, D_MODEL), F32), jax.ShapeDtypeStruct((1, D_MODEL), F32)]
        + _scatter_shapes(scatter),
        scratch_shapes=_scatter_semaphores(ns),
        compiler_params=_params(("arbitrary",)),
    )(dpr, dpf, dffb, w_main, w_ff, h0, g, dh1, *scatter)


def _local_step(x, target, meta, attn_g, w_main, w_ff, fox_b, ret_g, w_out, ffn_g, w_up, conv_w, conv_b, w_down, final_g,
                late=None, mid=None, last=None):
    S = x.shape[0]
    L = S + PREFIX
    head = jnp.concatenate([jnp.zeros((N_PAD, D_MODEL), F32), meta], axis=0)
    fb = jnp.pad(fox_b, ((0, 0), (0, BLK - FOX_HEADS)))
    cos_t, sin_t = _rotary_tables(L)

    h0, n1, proj, ff = _rms_inproj(head, x, attn_g, w_main, w_ff)
    c, ctb = _fox_prep(ff, fb)
    mix_r, o_ret, states = _retention_fwd(proj, cos_t, sin_t, ret_g)
    if late is None:
        o_f, lse = _fox_fwd(proj, c, ctb)
    else:
        o_f, lse, *gathered = _fox_fwd(proj, c, ctb, gather=late[0])
        w_out, w_up, w_down = late[1](gathered)
    h1, n2, up, g_act, acc_saved = _outproj_up(mix_r, o_f, h0, w_out, ffn_g, w_up, conv_w, conv_b)
    dh2, dh2b, d_final_g, loss = _ffn_down_loss(g_act, w_down, h1, final_g, target)

    dacc, db = _ffn_bwd_gate(dh2b, w_down, acc_saved, up)
    dup, dh1, dh1b, dmix, d_ffn_g, dconv = _ffn_bwd_up(dacc, db, up, conv_w, w_up, h1, ffn_g, dh2, w_out)
    d_w_down = _wgrad(g_act, dh2b, "wgrad_down", tk=D_FF // 2)[0]
    d_w_up = _wgrad(n2, dup, "wgrad_up", tn=w_up.shape[2])
    d_w_out = jnp.concatenate([_wgrad(mix_r, dh1b, "wgrad_out_r")[0], _wgrad(o_f, dh1b, "wgrad_out_f")[0]], axis=0)

    early = () if mid is None else mid[0](d_w_out, d_w_up, d_w_down)
    dpr, d_ret_g, *from_sibling = _retention_bwd(dmix, o_ret, proj, cos_t, sin_t, ret_g, states, exchange=early)
    delta = _fox_delta(dmix, o_f)
    scatter = () if mid is None else mid[1](early, from_sibling)
    dpf, dc, dcq, *received = _fox_bwd(proj, dmix, c, ctb, lse, delta, scatter=scatter)
    dff, dffb, d_fox_b = _fox_post(dc, dcq, ff, fb)
    d_w_main = jnp.concatenate([_wgrad(n1, dpr, "wgrad_in_r")[0], _wgrad(n1, dpf, "wgrad_in_f")[0]], axis=1)
    d_w_ff = _wgrad(n1, dffb, "wgrad_in_ff")[0][:, :FOX_HEADS]
    scatter_in = () if last is None else last(d_w_main, d_w_ff)
    dh0, d_attn_g, *received_in = _inproj_bwd(dpr, dpf, dffb, w_main, w_ff, h0, attn_g, dh1, scatter=scatter_in)

    return dict(
        loss=loss[0, 0], dx=dh0[PREFIX:], dmeta=dh0[N_PAD:PREFIX], attn_g=d_attn_g, w_main=d_w_main,
        w_ff=d_w_ff, fox_b=d_fox_b[:, :FOX_HEADS], ret_g=d_ret_g, w_out=d_w_out, ffn_g=d_ffn_g,
        w_up=d_w_up, conv_w=dconv[0:3], conv_b=dconv[3:4], w_down=d_w_down, final_g=d_final_g,
        scatter=list(scatter_in) + list(scatter), received=list(received_in) + list(received))


_ANY = pl.BlockSpec(memory_space=pl.ANY)


def _place():
    return lax.axis_index("x"), lax.axis_index("y"), lax.axis_index("c")


def _other_chips(x, y):
    return [(1 - x, y), (x, 1 - y), (1 - x, 1 - y)]


def _allgather_semaphores(n):
    if n == 0:
        return []
    return [pltpu.SemaphoreType.DMA((3 * n,)), pltpu.SemaphoreType.DMA((3 * n,)), pltpu.SemaphoreType.DMA((n,))]


def _allgather_copies(ins, outs, send, recv, loc):
    n = len(ins)
    x, y, c = _place()
    mine = 2 * x + y
    peers = _other_chips(x, y)

    def remote(a, k, slot):
        return pltpu.make_async_remote_copy(
            src_ref=ins[a], dst_ref=outs[a].at[slot], send_sem=send.at[3 * a + k], recv_sem=recv.at[3 * a + k],
            device_id=(peers[k][0], peers[k][1], c), device_id_type=MESH)

    local = [pltpu.make_async_copy(ins[a], outs[a].at[mine], loc.at[a]) for a in range(n)]
    sends = [remote(a, k, mine) for a in range(n) for k in range(3)]
    recvs = [remote(a, k, 2 * peers[k][0] + peers[k][1]) for a in range(n) for k in range(3)]
    return local, sends, recvs


def _chip_allgather_halves(w, small):
    half = w.shape[0] // 2

    def body(w_ref, s_ref, wo_ref, so_ref, send, recv, fsend, frecv, ssend, srecv, loc):
        x, y, c = _place()
        mine = 2 * x + y
        peers = _other_chips(x, y)

        def fetch(k, slot):
            return pltpu.make_async_remote_copy(
                src_ref=w_ref.at[pl.ds(c * half, half)], dst_ref=wo_ref.at[slot, c], send_sem=send.at[k],
                recv_sem=recv.at[k], device_id=(peers[k][0], peers[k][1], c), device_id_type=MESH)

        def forward(k, which):
            slot = 2 * peers[k][0] + peers[k][1]
            return pltpu.make_async_remote_copy(
                src_ref=wo_ref.at[slot, which], dst_ref=wo_ref.at[slot, which], send_sem=fsend.at[k],
                recv_sem=frecv.at[k], device_id=(x, y, 1 - c), device_id_type=MESH)

        def small_copy(k, slot):
            return pltpu.make_async_remote_copy(
                src_ref=s_ref, dst_ref=so_ref.at[slot], send_sem=ssend.at[k], recv_sem=srecv.at[k],
                device_id=(peers[k][0], peers[k][1], c), device_id_type=MESH)

        local = pltpu.make_async_copy(s_ref, so_ref.at[mine], loc.at[0])
        sends = [fetch(k, mine) for k in range(3)] + [small_copy(k, mine) for k in range(3)]
        local.start()
        for cp in sends:
            cp.start()
        forwards = []
        for k in range(3):
            fetch(k, 2 * peers[k][0] + peers[k][1]).wait_recv()
            forwards.append(forward(k, c))
            forwards[-1].start()
        for k in range(3):
            forward(k, 1 - c).wait_recv()
            small_copy(k, 2 * peers[k][0] + peers[k][1]).wait_recv()
        for cp in sends + forwards:
            cp.wait_send()
        local.wait()

    three = pltpu.SemaphoreType.DMA((3,))
    return pl.pallas_call(
        body, name="ag_weights", in_specs=[_ANY] * 2, out_specs=[_ANY] * 2,
        out_shape=[jax.ShapeDtypeStruct((N_CHIPS, 2, half, w.shape[1]), w.dtype),
                   jax.ShapeDtypeStruct((N_CHIPS,) + small.shape, small.dtype)],
        scratch_shapes=[three, three, three, three, three, three, pltpu.SemaphoreType.DMA((1,))],
    )(w, small)


def _chip_allgather(arrays):
    n = len(arrays)

    def body(*refs):
        local, sends, recvs = _allgather_copies(refs[:n], refs[n:2 * n], *refs[2 * n:])
        for cp in local + sends:
            cp.start()
        for cp in recvs:
            cp.wait_recv()
        for cp in sends:
            cp.wait_send()
        for cp in local:
            cp.wait()

    return pl.pallas_call(
        body, name="ag_weights", in_specs=[_ANY] * n, out_specs=[_ANY] * n,
        out_shape=[jax.ShapeDtypeStruct((N_CHIPS,) + a.shape, a.dtype) for a in arrays],
        scratch_shapes=_allgather_semaphores(n),
    )(*arrays)


def _sibling_halves(grads):
    n = len(grads)

    def body(*refs):
        sends, recvs = _sibling_half_copies(refs[:n], refs[n:2 * n], *refs[2 * n:])
        for cp in sends:
            cp.start()
        for cp in recvs:
            cp.wait_recv()
        for cp in sends:
            cp.wait_send()

    return pl.pallas_call(
        body, name="rs_sibling", in_specs=[_ANY] * n, out_specs=[_ANY] * n,
        out_shape=_sibling_half_shapes(grads), scratch_shapes=_sibling_half_semaphores(n),
    )(*grads)


def _sibling_half_shapes(grads):
    return [jax.ShapeDtypeStruct((N_CHIPS, g.shape[1] // 2, g.shape[2]), g.dtype) for g in grads]


def _sibling_half_semaphores(n):
    return [pltpu.SemaphoreType.DMA((n,)), pltpu.SemaphoreType.DMA((n,))] if n else []


def _sibling_half_copies(ins, outs, send, recv):
    x, y, c = _place()

    def half_copy(a, which):
        half = ins[a].shape[1] // 2
        return pltpu.make_async_remote_copy(
            src_ref=ins[a].at[pl.ds(0, N_CHIPS), pl.ds(which * half, half)], dst_ref=outs[a],
            send_sem=send.at[a], recv_sem=recv.at[a], device_id=(x, y, 1 - c), device_id_type=MESH)

    return [half_copy(a, 1 - c) for a in range(len(ins))], [half_copy(a, c) for a in range(len(ins))]


def _scatter_shapes(parts):
    return [jax.ShapeDtypeStruct((3,) + p.shape[1:], p.dtype) for p in parts]


def _scatter_semaphores(n):
    return [pltpu.SemaphoreType.DMA((3 * n,)), pltpu.SemaphoreType.DMA((3 * n,))] if n else []


def _scatter_copies(ins, outs, send, recv):
    x, y, c = _place()
    peers = _other_chips(x, y)
    return [pltpu.make_async_remote_copy(
        src_ref=ins[a].at[2 * peers[k][0] + peers[k][1]], dst_ref=outs[a].at[k], send_sem=send.at[3 * a + k],
        recv_sem=recv.at[3 * a + k], device_id=(peers[k][0], peers[k][1], c), device_id_type=MESH)
        for a in range(len(ins)) for k in range(3)]


def _sibling_allgather(bufs, small):
    n = len(bufs)

    def body(*refs):
        small_in, outs, small_out = refs[n], refs[n + 1:2 * n + 1], refs[2 * n + 1]
        send, recv, s_send, s_recv, loc = refs[2 * n + 2:]
        x, y, c = _place()
        me = 4 * x + 2 * y + c

        def remote(a, which):
            return pltpu.make_async_remote_copy(
                src_ref=outs[a].at[which], dst_ref=outs[a].at[which], send_sem=send.at[a], recv_sem=recv.at[a],
                device_id=(x, y, 1 - c), device_id_type=MESH)

        def peer_of(r):
            return tuple(1 - v if (r >> b) & 1 else v for v, b in ((x, 2), (y, 1), (c, 0)))

        def small_copy(r, slot):
            return pltpu.make_async_remote_copy(
                src_ref=small_in, dst_ref=small_out.at[slot], send_sem=s_send.at[r - 1], recv_sem=s_recv.at[r - 1],
                device_id=peer_of(r), device_id_type=MESH)

        local = pltpu.make_async_copy(small_in, small_out.at[me], loc.at[0])
        sends = [remote(a, c) for a in range(n)] + [small_copy(r, me) for r in range(1, N_DEV)]
        local.start()
        for cp in sends:
            cp.start()
        for r in range(1, N_DEV):
            px, py, pc = peer_of(r)
            small_copy(r, 4 * px + 2 * py + pc).wait_recv()
        for a in range(n):
            remote(a, 1 - c).wait_recv()
        for cp in sends:
            cp.wait_send()
        local.wait()

    outs = pl.pallas_call(
        body, name="ag_sibling", in_specs=[_ANY] * (n + 1), out_specs=[_ANY] * (n + 1),
        out_shape=[jax.ShapeDtypeStruct(b.shape, b.dtype) for b in bufs]
        + [jax.ShapeDtypeStruct((N_DEV,) + small.shape, small.dtype)],
        input_output_aliases={a: a for a in range(n)},
        scratch_shapes=[pltpu.SemaphoreType.DMA((n,)), pltpu.SemaphoreType.DMA((n,)),
                        pltpu.SemaphoreType.DMA((N_DEV - 1,)), pltpu.SemaphoreType.DMA((N_DEV - 1,)),
                        pltpu.SemaphoreType.DMA((1,))],
    )(*bufs, small)
    return [o.reshape(2 * o.shape[1], o.shape[2]) for o in outs[:n]], outs[n]


def _pair_add(full, recv, core, name):
    _, R, C = full.shape
    half = R // 2

    def body(core_ref, a_ref, b_ref, o_ref):
        o_ref[...] = (a_ref[...] + b_ref[...]).astype(BF16)

    return pl.pallas_call(
        body, name=name,
        grid_spec=pltpu.PrefetchScalarGridSpec(
            num_scalar_prefetch=1, grid=(N_CHIPS,),
            in_specs=[pl.BlockSpec((1, half, C), lambda j, core_ref: (j, core_ref[0], 0)),
                      pl.BlockSpec((1, half, C), lambda j, core_ref: (j, 0, 0))],
            out_specs=pl.BlockSpec((1, half, C), lambda j, core_ref: (j, 0, 0))),
        out_shape=jax.ShapeDtypeStruct((N_CHIPS, half, C), BF16),
        compiler_params=_params(("parallel",)),
    )(core, full, recv)


def _sum_slots(q, name, tiles=2):
    n, R, C = q.shape
    tr = R // tiles

    def body(q_ref, o_ref):
        acc = q_ref[0].astype(F32)
        for j in range(1, n):
            acc = acc + q_ref[j].astype(F32)
        o_ref[...] = acc

    return pl.pallas_call(
        body, name=name, grid=(tiles,),
        in_specs=[pl.BlockSpec((n, tr, C), lambda i: (0, i, 0))],
        out_specs=pl.BlockSpec((tr, C), lambda i: (i, 0)),
        out_shape=jax.ShapeDtypeStruct((R, C), F32),
        compiler_params=_params(("parallel",)),
    )(q)


def _sum_partials(own_all, recv, place, name, tiles=2):
    _, R, C = own_all.shape
    tr = R // tiles

    def body(place_ref, own_ref, r_ref, o_ref):
        acc = own_ref[0].astype(F32)
        for k in range(3):
            acc = acc + r_ref[k].astype(F32)
        o_ref[0] = acc

    return pl.pallas_call(
        body, name=name,
        grid_spec=pltpu.PrefetchScalarGridSpec(
            num_scalar_prefetch=1, grid=(tiles,),
            in_specs=[pl.BlockSpec((1, tr, C), lambda i, place_ref: (place_ref[0], i, 0)),
                      pl.BlockSpec((3, tr, C), lambda i, place_ref: (0, i, 0))],
            out_specs=pl.BlockSpec((1, tr, C), lambda i, place_ref: (place_ref[1], i, 0))),
        out_shape=jax.ShapeDtypeStruct((2, R, C), F32),
        compiler_params=_params(("parallel",)),
    )(place, own_all, recv)


def _adamw(w, g, m, v, name, tiles=4):
    R, C = w.shape
    tr = R // tiles

    def body(w_ref, g_ref, m_ref, v_ref, go_ref, d_ref, m2_ref, v2_ref):
        g_ = g_ref[...]
        go_ref[...] = g_
        m2 = ADAM_B1 * m_ref[...] + (1.0 - ADAM_B1) * g_
        v2 = ADAM_B2 * v_ref[...] + (1.0 - ADAM_B2) * (g_ * g_)
        m_hat = m2 / (1.0 - ADAM_B1 ** ADAM_STEP)
        v_hat = v2 / (1.0 - ADAM_B2 ** ADAM_STEP)
        d_ref[...] = -ADAM_LR * (m_hat / (jnp.sqrt(v_hat) + ADAM_EPS) + ADAM_WD * w_ref[...])
        m2_ref[...] = m2
        v2_ref[...] = v2

    spec = pl.BlockSpec((tr, C), lambda i: (i, 0))
    return pl.pallas_call(
        body, name=name, grid=(tiles,), in_specs=[spec] * 4, out_specs=[spec] * 4,
        out_shape=[jax.ShapeDtypeStruct((R, C), F32)] * 4,
        compiler_params=_params(("parallel",)),
    )(w, g, m, v)


def _pack_rows(pieces, rows):
    flat = jnp.concatenate([jnp.pad(p.reshape(-1).astype(F32), (0, (-p.size) % D_MODEL)) for p in pieces])
    return jnp.pad(flat, (0, rows * D_MODEL - flat.size)).reshape(rows, D_MODEL)


def _unpack_rows(pack, shapes):
    flat = pack.reshape(-1)
    out, off = [], 0
    for shp in shapes:
        size = int(np.prod(shp))
        out.append(flat[off:off + size].reshape(shp))
        off += size + (-size) % D_MODEL
    return out


def _kernel_order(w):
    parts = [w[:, 0:RET_W]]
    for p in range(FOX_HEADS // 2):
        parts += [w[:, RET_W + part * 512 + p * BLK:RET_W + part * 512 + (p + 1) * BLK] for part in range(3)]
    return jnp.concatenate(parts, axis=1)


def _reference_order(g_main, g_ff):
    parts = [g_main[:, 0:RET_W]]
    for part in range(3):
        parts += [g_main[:, RET_W + 384 * p + part * BLK:RET_W + 384 * p + (part + 1) * BLK] for p in range(FOX_HEADS // 2)]
    return jnp.concatenate(parts + [g_ff], axis=1)


def kernel(x, meta_tokens, attn_norm_g, w_in, fox_forget_b, ret_norm_g, w_out, ffn_norm_g, w_up, conv_w, conv_b, w_down, final_norm_g, loss_target, m_meta_tokens, m_attn_norm_g, m_w_in, m_fox_forget_b, m_ret_norm_g, m_w_out, m_ffn_norm_g, m_w_up, m_conv_w, m_conv_b, m_w_down, m_final_norm_g, v_meta_tokens, v_attn_norm_g, v_w_in, v_fox_forget_b, v_ret_norm_g, v_w_out, v_ffn_norm_g, v_w_up, v_conv_w, v_conv_b, v_w_down, v_final_norm_g):
    chip = 2 * lax.axis_index("x") + lax.axis_index("y")
    core = lax.axis_index("c")
    meta_w, conv_sw = meta_tokens.shape[1], conv_w.shape[2]

    small_w = _pack_rows([meta_tokens, conv_w[0]], 8)
    w_in_b = w_in[0].astype(BF16)
    g_in, g_small = _chip_allgather_halves(w_in_b, small_w)
    g_in = lax.dynamic_update_slice(g_in.reshape((N_CHIPS,) + w_in_b.shape), w_in_b[None], (chip, 0, 0))
    w_in_full = g_in.transpose(1, 0, 2).reshape(D_MODEL, IN_WIDTH)
    w_main = _kernel_order(w_in_full)
    w_ff = jnp.pad(w_in_full[:, MAIN_W:], ((0, 0), (0, BLK - FOX_HEADS)))
    small_parts = [_unpack_rows(g_small[j], [meta_tokens.shape, conv_w.shape[1:]]) for j in range(N_CHIPS)]
    meta_full = jnp.concatenate([sp[0] for sp in small_parts], axis=1)
    conv_w_full = jnp.concatenate([sp[1] for sp in small_parts], axis=1)

    core_idx = core.reshape(1).astype(jnp.int32)
    place = jnp.stack([chip, core]).astype(jnp.int32)

    def assemble(gathered):
        g_out, g_up, g_down = gathered
        return g_out.reshape(D_MODEL, D_MODEL), g_up, g_down.reshape(D_FF, D_MODEL)

    def early_arrays(d_w_out, d_w_up, d_w_down):
        return [d_w_out.reshape(N_CHIPS, -1, D_MODEL), d_w_up, d_w_down.reshape(N_CHIPS, -1, D_MODEL)]

    def in_sums(d_w_main, d_w_ff):
        g_in_full = _reference_order(d_w_main, d_w_ff).reshape(D_MODEL, N_CHIPS, -1).transpose(1, 0, 2)
        (from_sib,) = _sibling_halves([g_in_full])
        return [_pair_add(g_in_full, from_sib, core_idx, "pair_add_in")]

    def early_sums(early, from_sib):
        return [_pair_add(g, r, core_idx, "pair_add_" + nm) for g, r, nm in zip(early, from_sib, ("out", "up", "down"))]

    out = _local_step(x[0], loss_target[0], meta_full, attn_norm_g, w_main, w_ff, fox_forget_b, ret_norm_g,
                      None, ffn_norm_g, None, conv_w_full, conv_b, None, final_norm_g[None],
                      late=([w_out[0].astype(BF16), w_up[0].astype(BF16), w_down[0].astype(BF16)], assemble),
                      mid=(early_arrays, early_sums), last=in_sums)

    small_shapes = [(1, D_MODEL), (1, D_MODEL), (1, D_MODEL), (1, 512 + FOX_HEADS + 1), (1, D_FF), (N_META, D_MODEL), (3, D_FF)]
    small = _pack_rows([out["attn_g"], out["ffn_g"], out["final_g"],
                        jnp.concatenate([out["ret_g"], out["fox_b"], out["loss"].reshape(1, 1)], axis=1),
                        out["conv_b"], out["dmeta"], out["conv_w"]], 32)
    names = ("in", "out", "up", "down")
    totals = [_sum_partials(s, q, place, "sum_chips_" + nm) for s, q, nm in zip(out["scatter"], out["received"], names)]
    (grad_in, grad_out, grad_up, grad_down), small_all = _sibling_allgather(totals, small)
    s_attn, s_ffn, s_final, s_misc, s_conv_b, s_meta, s_conv_w = _unpack_rows(
        _sum_slots(small_all, "sum_small", tiles=1), small_shapes)
    loss = s_misc[0, 512 + FOX_HEADS]
    small_grads = [lax.dynamic_slice_in_dim(s_meta, chip * meta_w, meta_w, axis=1), s_attn, s_misc[:, 512:512 + FOX_HEADS],
                   s_misc[:, :512], s_ffn, lax.dynamic_slice_in_dim(s_conv_w, chip * conv_sw, conv_sw, axis=1)[None],
                   s_conv_b, s_final[0]]

    big_w = [(w_in, m_w_in, v_w_in, grad_in, "adamw_in"), (w_out, m_w_out, v_w_out, grad_out, "adamw_out"),
             (w_up, m_w_up, v_w_up, grad_up, "adamw_up"), (w_down, m_w_down, v_w_down, grad_down, "adamw_down")]
    big_res = [[r[None] for r in _adamw(w[0], g, m[0], v[0], nm)] for w, m, v, g, nm in big_w]
    small_w_list = [meta_tokens, attn_norm_g, fox_forget_b, ret_norm_g, ffn_norm_g, conv_w, conv_b, final_norm_g]
    small_m = [m_meta_tokens, m_attn_norm_g, m_fox_forget_b, m_ret_norm_g, m_ffn_norm_g, m_conv_w, m_conv_b, m_final_norm_g]
    small_v = [v_meta_tokens, v_attn_norm_g, v_fox_forget_b, v_ret_norm_g, v_ffn_norm_g, v_conv_w, v_conv_b, v_final_norm_g]
    shapes = [a.shape for a in small_w_list]
    packs = [_pack_rows(lst, 16) for lst in (small_w_list, small_grads, small_m, small_v)]
    small_res = [_unpack_rows(r, shapes) for r in _adamw(*packs, "adamw_small", tiles=1)[1:]]
    small_grads = [g.reshape(s) for g, s in zip(small_grads, shapes)]

    def ordered(kind):
        sm = small_grads if kind == 0 else small_res[kind - 1]
        bg = [r[kind] for r in big_res]
        return [sm[0], sm[1], bg[0], sm[2], sm[3], bg[1], sm[4], bg[2], sm[5], sm[6], bg[3], sm[7]]

    return (loss, out["dx"][None], *ordered(0), *ordered(1), *ordered(2), *ordered(3))
```

```python
import functools

import numpy as np
import jax
import jax.numpy as jnp
from jax import lax
from jax.experimental import pallas as pl
from jax.experimental.pallas import tpu as pltpu

F32 = jnp.float32
BF16 = jnp.bfloat16

D_MODEL = 1024
N_META = 16
BLK = 128
UNIT = 2 * BLK
FOX_PAIRS = 2
WIDE = 4
CHUNK = 64
N_PAD = BLK - N_META
PREFIX = BLK
RET_HEADS = 4
FOX_HEADS = 8
HEAD_LANES = 64
D_FF = 2816
ROPE_BASE = 10000.0
EPS = 1e-6
NEG = -1e30
LOG2E = 1.4426950408889634
RET_W = 1536
FOX_W = 1536
MAIN_W = RET_W + FOX_W
IN_WIDTH = MAIN_W + FOX_HEADS
N_CHIPS = 4
N_DEV = 8

ADAM_LR = 0.001
ADAM_B1 = 0.9
ADAM_B2 = 0.999
ADAM_EPS = 1e-08
ADAM_WD = 0.01
ADAM_STEP = 10

MESH = pl.DeviceIdType.MESH
VMEM_LIMIT_MB = 56

_NT = (((1,), (1,)), ((), ()))
_TN = (((0,), (0,)), ((), ()))


def _dot(a, b):
    return jnp.dot(a, b, preferred_element_type=F32)


def _dot_nt(a, b):
    return lax.dot_general(a, b, _NT, preferred_element_type=F32)


def _dot_tn(a, b):
    return lax.dot_general(a, b, _TN, preferred_element_type=F32)


def _params(dims=None, vmem_mb=VMEM_LIMIT_MB):
    kw = dict(vmem_limit_bytes=vmem_mb << 20)
    if dims is not None:
        kw["dimension_semantics"] = dims
    return pltpu.CompilerParams(**kw)


def _row_tile(n, prefs=(384, 256, 128)):
    for t in prefs:
        if n % t == 0:
            return t
    raise ValueError(f"no row tile for {n}")


def _iota(shape, dim):
    return lax.broadcasted_iota(jnp.int32, shape, dim)


def _pick_row(tile, row):
    sub = _iota(tile.shape, 0)
    return jnp.sum(jnp.where(sub == row, tile, 0.0), axis=0, keepdims=True)


def _split3(x):
    hi = x.astype(BF16)
    r1 = x - hi.astype(F32)
    mid = r1.astype(BF16)
    lo = (r1 - mid.astype(F32)).astype(BF16)
    return hi, mid, lo


def _full(shape):
    nd = len(shape)
    return pl.BlockSpec(shape, lambda *_: (0,) * nd)


def _in_perm():
    cols = list(range(RET_W))
    for p in range(FOX_HEADS // 2):
        for part in range(3):
            start = RET_W + part * 512 + p * BLK
            cols += list(range(start, start + BLK))
    return np.asarray(cols, np.int32)


def _rotary_tables(L):
    half = HEAD_LANES // 2
    inv = 1.0 / (ROPE_BASE ** (jnp.arange(half, dtype=F32) / half))
    ang = jnp.arange(L).astype(F32)[:, None] * inv[None, :]
    cos, sin = jnp.cos(ang), jnp.sin(ang)
    cos_t = jnp.tile(cos, (1, 4))
    sin_t = jnp.tile(jnp.concatenate([-sin, sin], axis=1), (1, 2))
    return cos_t, sin_t


def _decay_tables():
    gam = 1.0 - 2.0 ** (-5.0 - np.arange(RET_HEADS, dtype=np.float64))
    n = np.arange(BLK)
    same_or_past = (n[:, None] // CHUNK) >= (n[None, :] // CHUNK)
    dist = np.abs(n[:, None] - n[None, :])
    dmat = np.stack([np.where(same_or_past, g ** dist, 0.0) for g in gam]).astype(np.float32)
    lane_head = np.arange(BLK) // HEAD_LANES
    wq = np.stack([gam[2 * p + lane_head][None, :] ** (n[:, None] + 1.0) for p in range(2)]).astype(np.float32)
    wk = np.stack([gam[2 * p + lane_head][None, :] ** (BLK - 1.0 - n[:, None]) for p in range(2)]).astype(np.float32)
    g_blk = tuple(float(g ** BLK) for g in gam)
    return jnp.asarray(dmat), jnp.asarray(wq), jnp.asarray(wk), g_blk


def _shifted_blocks(tm):
    nb = tm // BLK
    return [pl.BlockSpec((BLK, D_MODEL), lambda i, j=j: (jnp.maximum(nb * i + j - 1, 0), 0)) for j in range(nb)]


def _rms_inproj(head, x, g, w_main, w_ff):
    L = x.shape[0] + BLK
    tm = _row_tile(L)
    nb = tm // BLK

    def body(head_ref, *refs):
        x_refs, (g_ref, wm_ref, wf_ref, h_ref, n_ref, p_ref, ff_ref) = refs[:nb], refs[nb:]
        parts = [r[...] for r in x_refs]
        parts[0] = jnp.where(pl.program_id(0) == 0, head_ref[...], parts[0])
        h = jnp.concatenate(parts, axis=0)
        h_ref[...] = h
        r = lax.rsqrt(jnp.mean(h * h, axis=-1, keepdims=True) + EPS)
        n = (h * r * g_ref[...]).astype(BF16)
        n_ref[...] = n
        p_ref[...] = _dot(n, wm_ref[...]).astype(BF16)
        ff_ref[...] = _dot(n, wf_ref[...])

    rows = lambda w: pl.BlockSpec((tm, w), lambda i: (i, 0))
    return pl.pallas_call(
        body, name="f_inproj", grid=(L // tm,),
        in_specs=[_full((BLK, D_MODEL))] + _shifted_blocks(tm)
        + [_full((1, D_MODEL)), _full((D_MODEL, MAIN_W)), _full((D_MODEL, BLK))],
        out_specs=[rows(D_MODEL), rows(D_MODEL), rows(MAIN_W), rows(BLK)],
        out_shape=[jax.ShapeDtypeStruct((L, D_MODEL), F32), jax.ShapeDtypeStruct((L, D_MODEL), BF16),
                   jax.ShapeDtypeStruct((L, MAIN_W), BF16), jax.ShapeDtypeStruct((L, BLK), F32)],
        compiler_params=_params(("parallel",)),
    )(head, *([x] * nb), g, w_main, w_ff)


def _block_group(nblk):
    return 3 if nblk % 3 == 0 else 1


def _fox_prep(ff, fb):
    L = ff.shape[0]
    nblk = L // BLK
    G = _block_group(nblk)

    def body(ff_ref, b_ref, c_ref, ct_ref, carry):
        @pl.when(pl.program_id(0) == 0)
        def _():
            carry[...] = jnp.zeros_like(carry)

        tri = (_iota((BLK, BLK), 0) >= _iota((BLK, BLK), 1)).astype(BF16)
        live = _iota((BLK, BLK), 1) < FOX_HEADS
        run = carry[...]
        for b in range(G):
            z = ff_ref[b * BLK:(b + 1) * BLK, :] + b_ref[...]
            lf = jnp.where(live, jnp.minimum(z, 0.0) - jnp.log1p(jnp.exp(-jnp.abs(z))), 0.0)
            hi, mid, lo = _split3(lf)
            cs = (_dot(tri, hi) + _dot(tri, mid) + _dot(tri, lo) + run) * LOG2E
            c_ref[b * BLK:(b + 1) * BLK, :] = cs
            ct_ref[b] = cs.T[0:8, :]
            run = run + jnp.sum(lf, axis=0, keepdims=True)
        carry[...] = run

    return pl.pallas_call(
        body, name="f_foxprep", grid=(nblk // G,),
        in_specs=[pl.BlockSpec((G * BLK, BLK), lambda i: (i, 0)), _full((1, BLK))],
        out_specs=[pl.BlockSpec((G * BLK, BLK), lambda i: (i, 0)), pl.BlockSpec((G, 8, BLK), lambda i: (i, 0, 0))],
        out_shape=[jax.ShapeDtypeStruct((L, BLK), F32), jax.ShapeDtypeStruct((nblk, 8, BLK), F32)],
        scratch_shapes=[pltpu.VMEM((1, BLK), F32)],
        compiler_params=_params(("arbitrary",)),
    )(ff, fb)


def _rot_fns(cos, sin):
    lane = _iota((BLK, BLK), 1)
    first = (lane & (HEAD_LANES - 1)) < HEAD_LANES // 2

    def swap(x):
        return jnp.where(first, pltpu.roll(x, BLK - 32, 1), pltpu.roll(x, 32, 1))

    def rot(x):
        return x * cos + swap(x) * sin

    def rot_t(dy):
        return dy * cos + swap(dy * sin)

    return rot, rot_t


def _retention_fwd(proj, cos_t, sin_t, ret_g):
    L = proj.shape[0]
    nblk = L // BLK
    G = _block_group(nblk)
    dmat, wq_t, wk_t, g_blk = _decay_tables()

    def body(q_ref, k_ref, v_ref, gate_ref, cos_ref, sin_ref, d_ref, wq_ref, wk_ref, rg_ref,
             mix_ref, o_ref, rs_ref, state):
        @pl.when(pl.program_id(0) == 0)
        def _():
            state[...] = jnp.zeros_like(state)

        lane = _iota((BLK, BLK), 1)
        sub = _iota((BLK, BLK), 0)
        for b in range(G):
            rows = slice(b * BLK, (b + 1) * BLK)
            rot, _ = _rot_fns(cos_ref[rows, :], sin_ref[rows, :])
            for p in range(2):
                qr = rot(q_ref[rows, p * BLK:(p + 1) * BLK].astype(F32))
                kr = rot(k_ref[rows, p * BLK:(p + 1) * BLK].astype(F32)) * (HEAD_LANES ** -0.5)
                kr_b = kr.astype(BF16)
                qw = (qr * wq_ref[p]).astype(BF16)
                kw = (kr * wk_ref[p]).astype(BF16)
                for e in range(2):
                    h = 2 * p + e
                    cols = slice(h * BLK, (h + 1) * BLK)
                    qm = jnp.where((lane >> 6) == e, qr, 0.0).astype(BF16)
                    s = _dot_nt(qm, kr_b) * d_ref[h]
                    vh = v_ref[rows, cols]
                    st = state[h]
                    rs_ref[b, h] = st
                    o = _dot(s.astype(BF16), vh) + _dot(qw, st.astype(BF16))
                    u = jnp.where((sub >> 6) == e, _dot_tn(kw, vh), 0.0)
                    state[h] = g_blk[h] * st + u
                    rn = lax.rsqrt(jnp.mean(o * o, axis=-1, keepdims=True) + EPS)
                    gate = gate_ref[rows, cols].astype(F32)
                    o_ref[rows, cols] = o
                    mix_ref[rows, cols] = (o * rn * rg_ref[:, cols] * (gate * jax.nn.sigmoid(gate))).astype(BF16)

    row = lambda c: (lambda i: (i, c))
    return pl.pallas_call(
        body, name="f_retention", grid=(nblk // G,),
        in_specs=[pl.BlockSpec((G * BLK, 256), row(0)), pl.BlockSpec((G * BLK, 256), row(1)),
                  pl.BlockSpec((G * BLK, 512), row(1)), pl.BlockSpec((G * BLK, 512), row(2)),
                  pl.BlockSpec((G * BLK, BLK), row(0)), pl.BlockSpec((G * BLK, BLK), row(0)),
                  _full((RET_HEADS, BLK, BLK)), _full((2, BLK, BLK)), _full((2, BLK, BLK)), _full((1, 512))],
        out_specs=[pl.BlockSpec((G * BLK, 512), row(0)), pl.BlockSpec((G * BLK, 512), row(0)),
                   pl.BlockSpec((G, RET_HEADS, BLK, BLK), lambda i: (i, 0, 0, 0))],
        out_shape=[jax.ShapeDtypeStruct((L, 512), BF16), jax.ShapeDtypeStruct((L, 512), F32),
                   jax.ShapeDtypeStruct((nblk, RET_HEADS, BLK, BLK), F32)],
        scratch_shapes=[pltpu.VMEM((RET_HEADS, BLK, BLK), F32)],
        compiler_params=_params(("arbitrary",)),
    )(proj, proj, proj, proj, cos_t, sin_t, dmat, wq_t, wk_t, ret_g)


def _fox_units(L):
    nblk = L // BLK
    assert L % BLK == 0 and nblk % 2 == 1, "sequence must be one 128-row block plus whole 256-row tiles"
    return nblk, (nblk - 1) // 2


def _fox_tile_masks():
    sub, lane = _iota((BLK, BLK), 0), _iota((BLK, BLK), 1)
    valid = _iota((BLK, UNIT), 0) >= N_PAD
    diag = _iota((UNIT, UNIT), 0) <= _iota((UNIT, UNIT), 1)
    r, q = _iota((BLK + UNIT, UNIT), 0), _iota((BLK + UNIT, UNIT), 1)
    first_and_diag = ((r < BLK) & (r >= N_PAD)) | ((r >= BLK) & (r - BLK <= q))
    return dict(first=(sub <= lane) & (sub >= N_PAD), valid=valid, diag=diag, first_and_diag=first_and_diag)


def _fox_fwd(proj, c, ctb, gather=()):
    L = proj.shape[0]
    nblk, nu = _fox_units(L)
    scale = HEAD_LANES ** -0.5 * LOG2E
    ng = len(gather)
    steps = FOX_HEADS // (2 * FOX_PAIRS)

    def body(qkv_ref, c_ref, ct_ref, *rest):
        g_in, (of_ref, lse_ref), g_out = rest[:ng], rest[ng:ng + 2], rest[ng + 2:2 * ng + 2]
        vt, csb = rest[2 * ng + 2:2 * ng + 4]
        p = pl.program_id(0)
        heads = [(pp, e, 2 * FOX_PAIRS * p + 2 * pp + e) for pp in range(FOX_PAIRS) for e in range(2)]

        @pl.when(p == 0)
        def _():
            lse_ref[...] = jnp.zeros_like(lse_ref)
            if ng:
                local, sends, _ = _allgather_copies(g_in, g_out, *rest[2 * ng + 4:])
                for cp in local + sends:
                    cp.start()

        lane = _iota((BLK, BLK), 1)
        sub8 = _iota((8, BLK), 0)
        masks = _fox_tile_masks()

        def pre(j, carry):
            off = pl.multiple_of(j * BLK, BLK)
            ct = c_ref[pl.ds(off, BLK), :]
            for pp in range(FOX_PAIRS):
                vt[pp, j] = qkv_ref[pl.ds(off, BLK), pp * 384 + 2 * BLK:pp * 384 + 3 * BLK].astype(F32).T.astype(BF16)
            for hh, (_, _, h) in enumerate(heads):
                col = jnp.sum(jnp.where(lane == h, ct, 0.0), axis=1, keepdims=True)
                csb[hh, j] = jnp.broadcast_to(col, (BLK, BLK))
            return carry

        lax.fori_loop(0, nblk, pre, 0)

        def attend(qblk, nq, n_whole):
            qlen = nq * BLK
            qoff = pl.multiple_of(qblk * BLK, BLK)
            qlane = _iota((qlen, BLK), 1)
            qs = [qkv_ref[pl.ds(qoff, qlen), pp * 384:pp * 384 + BLK].astype(F32) * scale for pp in range(FOX_PAIRS)]
            qm = [jnp.where((qlane >> 6) == e, qs[pp], 0.0).astype(BF16) for pp, e, _ in heads]
            ct_row = [jnp.concatenate([_pick_row(ct_ref[qblk + a], h) for a in range(nq)], axis=1) for _, _, h in heads]

            def step(segs, mask, st):
                blocks = [kblk + b for kblk, nk in segs for b in range(nk)]
                kts = []
                for pp in range(FOX_PAIRS):
                    kt = [qkv_ref[pl.ds(pl.multiple_of(kblk * BLK, BLK), nk * BLK), pp * 384 + BLK:pp * 384 + 2 * BLK]
                          for kblk, nk in segs]
                    kts.append(kt[0] if len(kt) == 1 else jnp.concatenate(kt, axis=0))
                out = []
                for hh, (pp, e, _) in enumerate(heads):
                    m, l, acc = st[3 * hh:3 * hh + 3]
                    s = _dot_nt(kts[pp], qm[hh])
                    t = jnp.concatenate([s[b * BLK:(b + 1) * BLK] - jnp.concatenate([csb[hh, blk]] * nq, axis=1)
                                         for b, blk in enumerate(blocks)], axis=0)
                    if mask is not None:
                        t = jnp.where(mask, t, NEG)
                    m_new = jnp.maximum(m, jnp.max(t, axis=0, keepdims=True) + ct_row[hh])
                    alpha = jnp.exp2(m - m_new)
                    pr = jnp.exp2(t - (m_new - ct_row[hh]))
                    l = alpha * l + jnp.sum(pr, axis=0, keepdims=True)
                    pr_b = pr.astype(BF16)
                    pv = None
                    for b, blk in enumerate(blocks):
                        part = _dot(vt[pp, blk, e * HEAD_LANES:(e + 1) * HEAD_LANES, :], pr_b[b * BLK:(b + 1) * BLK])
                        pv = part if pv is None else pv + part
                    out += [m_new, l, alpha * acc + pv]
                return tuple(out)

            st = (jnp.full((1, qlen), NEG, F32), jnp.zeros((1, qlen), F32),
                  jnp.zeros((HEAD_LANES, qlen), F32)) * len(heads)
            if nq == 1:
                st = step([(0, 1)], masks["first"], st)
            else:
                st = step([(0, 1), (qblk, 2)], masks["first_and_diag"], st)
                n_wide = n_whole // WIDE
                st = lax.fori_loop(0, n_wide, lambda j, s_: step([(1 + 2 * WIDE * j, 2 * WIDE)], None, s_), st)
                rest = 1 + 2 * WIDE * n_wide
                st = lax.cond((n_whole & 2) != 0, lambda s_: step([(rest, 4)], None, s_), lambda s_: s_, st)
                st = lax.cond((n_whole & 1) != 0, lambda s_: step([(rest + 2 * (n_whole & 2), 2)], None, s_),
                              lambda s_: s_, st)
            for pp in range(FOX_PAIRS):
                lo, hi = st[6 * pp:6 * pp + 3], st[6 * pp + 3:6 * pp + 6]
                o_t = jnp.concatenate([lo[2] * (1.0 / lo[1]), hi[2] * (1.0 / hi[1])], axis=0)
                of_ref[pl.ds(qoff, qlen), pp * BLK:(pp + 1) * BLK] = o_t.T.astype(BF16)
            lse = [st[3 * hh] + jnp.log(st[3 * hh + 1]) * LOG2E for hh in range(len(heads))]
            for a in range(nq):
                upd = jnp.zeros((8, BLK), F32)
                for hh, (_, _, h) in enumerate(heads):
                    upd = upd + jnp.where(sub8 == h, lse[hh][:, a * BLK:(a + 1) * BLK], 0.0)
                lse_ref[qblk + a] = lse_ref[qblk + a] + upd

        attend(0, 1, 0)

        def q_loop(u, carry):
            attend(1 + 2 * u, 2, u)
            return carry

        lax.fori_loop(0, nu, q_loop, 0)

        if ng:
            @pl.when(p == steps - 1)
            def _():
                local, sends, recvs = _allgather_copies(g_in, g_out, *rest[2 * ng + 4:])
                for cp in recvs:
                    cp.wait_recv()
                for cp in sends:
                    cp.wait_send()
                for cp in local:
                    cp.wait()

    width = 384 * FOX_PAIRS
    return pl.pallas_call(
        body, name="f_fox", grid=(steps,),
        in_specs=[pl.BlockSpec((L, width), lambda p: (0, RET_W // width + p)), _full((L, BLK)), _full((nblk, 8, BLK))]
        + [_ANY] * ng,
        out_specs=[pl.BlockSpec((L, FOX_PAIRS * BLK), lambda p: (0, p)), _full((nblk, 8, BLK))] + [_ANY] * ng,
        out_shape=[jax.ShapeDtypeStruct((L, 512), BF16), jax.ShapeDtypeStruct((nblk, 8, BLK), F32)]
        + [jax.ShapeDtypeStruct((N_CHIPS,) + a.shape, a.dtype) for a in gather],
        scratch_shapes=[pltpu.VMEM((FOX_PAIRS, nblk, BLK, BLK), BF16), pltpu.VMEM((2 * FOX_PAIRS, nblk, BLK, BLK), F32)]
        + _allgather_semaphores(ng),
        compiler_params=_params(("arbitrary",)),
    )(proj, c, ctb, *gather)


def _outproj_up(mix_r, o_f, h0, w_out, ffn_g, w_up, conv_w, conv_b):
    L = h0.shape[0]
    tm = _row_tile(L)
    shard = w_up.shape[2]
    assert 2 * shard == D_FF
    cw = [conv_w[j:j + 1] for j in range(3)]
    resident = lambda shape: pl.BlockSpec(shape, lambda i: (0,) * len(shape), pipeline_mode=pl.Buffered(1))

    def body(mr_ref, of_ref, h0_ref, wo_ref, g_ref, wu_ref, cw0, cw1, cw2, cb_ref,
             h1_ref, n2_ref, up_ref, act_ref, acc_ref, halo):
        i = pl.program_id(0)

        @pl.when(i == 0)
        def _():
            halo[...] = jnp.zeros_like(halo)

        h1 = h0_ref[...] + _dot(mr_ref[...], wo_ref[0:512, :]) + _dot(of_ref[...], wo_ref[512:1024, :])
        h1_ref[...] = h1
        r = lax.rsqrt(jnp.mean(h1 * h1, axis=-1, keepdims=True) + EPS)
        n2 = (h1 * r * g_ref[...]).astype(BF16)
        n2_ref[...] = n2
        live = i * tm + _iota((tm, 1), 0) >= N_PAD
        for half in range(2):
            for lo, hi in _col_chunks(shard):
                cols = slice(half * shard + lo, half * shard + hi)
                a_b = _dot(n2, wu_ref[half, :, lo:hi]).astype(BF16)
                b_b = _dot(n2, wu_ref[2 + half, :, lo:hi]).astype(BF16)
                up_ref[:, cols] = a_b
                up_ref[:, D_FF + half * shard + lo:D_FF + half * shard + hi] = b_b
                a = jnp.where(live, a_b.astype(F32), 0.0)
                _, _, acc = _conv_taps(a, halo[:, cols], [cw0[:, cols], cw1[:, cols], cw2[:, cols]], cb_ref[:, cols])
                act_ref[:, cols] = (acc * jax.nn.sigmoid(acc) * b_b.astype(F32)).astype(BF16)
                acc_ref[:, cols] = acc.astype(BF16)
                halo[:, cols] = a[tm - 8:tm, :]

    rows = lambda w: pl.BlockSpec((tm, w), lambda i: (i, 0))
    return pl.pallas_call(
        body, name="f_outproj_up", grid=(L // tm,),
        in_specs=[rows(512), rows(512), rows(D_MODEL), resident((D_MODEL, D_MODEL)), _full((1, D_MODEL)),
                  resident((N_CHIPS, D_MODEL, shard)), _full((1, D_FF)), _full((1, D_FF)), _full((1, D_FF)),
                  _full((1, D_FF))],
        out_specs=[rows(D_MODEL), rows(D_MODEL), rows(2 * D_FF), rows(D_FF), rows(D_FF)],
        out_shape=[jax.ShapeDtypeStruct((L, D_MODEL), F32), jax.ShapeDtypeStruct((L, D_MODEL), BF16),
                   jax.ShapeDtypeStruct((L, 2 * D_FF), BF16), jax.ShapeDtypeStruct((L, D_FF), BF16),
                   jax.ShapeDtypeStruct((L, D_FF), BF16)],
        scratch_shapes=[pltpu.VMEM((8, D_FF), F32)],
        compiler_params=_params(("arbitrary",)),
    )(mix_r, o_f, h0, w_out, ffn_g, w_up, cw[0], cw[1], cw[2], conv_b)


def _conv_taps(a, halo, cw, cb):
    sub = _iota((a.shape[0], 1), 0)
    a1 = jnp.where(sub == 0, _pick_row(halo, 7), pltpu.roll(a, 1, 0))
    a2 = jnp.where(sub == 0, _pick_row(halo, 6), jnp.where(sub == 1, _pick_row(halo, 7), pltpu.roll(a, 2, 0)))
    acc = cb + a2 * cw[0]
    acc = acc + a1 * cw[1]
    acc = acc + a * cw[2]
    return a1, a2, acc


def _ffn_down_loss(g_act, w_down, h1, final_g, target):
    L = h1.shape[0]
    tm = _row_tile(L)
    nb = tm // BLK

    def body(g_ref, wd_ref, h1_ref, gf_ref, *refs):
        t_refs, (dh_ref, dhb_ref, dgf_ref, loss_ref) = refs[:nb], refs[nb:]
        i = pl.program_id(0)

        @pl.when(i == 0)
        def _():
            dgf_ref[...] = jnp.zeros_like(dgf_ref)
            loss_ref[...] = jnp.zeros_like(loss_ref)

        h2 = h1_ref[...] + _dot(g_ref[...], wd_ref[...])
        r = lax.rsqrt(jnp.mean(h2 * h2, axis=-1, keepdims=True) + EPS)
        yn = h2 * r
        gf = gf_ref[...]
        live = i * tm + _iota((tm, 1), 0) >= PREFIX
        target = jnp.concatenate([t[...] for t in t_refs], axis=0)
        err = jnp.where(live, yn * gf - target, 0.0)
        loss_ref[...] = loss_ref[...] + 0.5 * jnp.sum(jnp.mean(err * err, axis=-1, keepdims=True))
        dy = err * (1.0 / D_MODEL)
        dgf_ref[...] = dgf_ref[...] + jnp.sum(dy * yn, axis=0, keepdims=True)
        dyn = dy * gf
        dh = r * (dyn - yn * jnp.mean(dyn * yn, axis=-1, keepdims=True))
        dh_ref[...] = dh
        dhb_ref[...] = dh.astype(BF16)

    rows = lambda w: pl.BlockSpec((tm, w), lambda i: (i, 0))
    return pl.pallas_call(
        body, name="f_ffn_down_loss", grid=(L // tm,),
        in_specs=[rows(D_FF), _full((D_FF, D_MODEL)), rows(D_MODEL), _full((1, D_MODEL))] + _shifted_blocks(tm),
        out_specs=[rows(D_MODEL), rows(D_MODEL), _full((1, D_MODEL)), _full((1, BLK))],
        out_shape=[jax.ShapeDtypeStruct((L, D_MODEL), F32), jax.ShapeDtypeStruct((L, D_MODEL), BF16),
                   jax.ShapeDtypeStruct((1, D_MODEL), F32), jax.ShapeDtypeStruct((1, BLK), F32)],
        compiler_params=_params(("arbitrary",)),
    )(g_act, w_down, h1, final_g, *([target] * nb))


def _col_chunks(width, chunk=768):
    return [(lo, min(lo + chunk, width)) for lo in range(0, width, chunk)]


def _ffn_bwd_gate(dh2b, w_down, acc_saved, up):
    L = dh2b.shape[0]
    tm = _row_tile(L)

    def body(dh_ref, wd_ref, acc_ref, b_ref, dacc_ref, db_ref):
        dh = dh_ref[...]
        for lo, hi in _col_chunks(D_FF):
            acc = acc_ref[:, lo:hi].astype(F32)
            dg = _dot_nt(dh, wd_ref[lo:hi, :])
            sg = jax.nn.sigmoid(acc)
            silu = acc * sg
            db_ref[:, lo:hi] = (dg * silu).astype(BF16)
            dacc_ref[:, lo:hi] = (dg * b_ref[:, lo:hi].astype(F32) * (sg + silu * (1.0 - sg))).astype(BF16)

    rows = lambda w, c=0: pl.BlockSpec((tm, w), lambda i: (i, c))
    return pl.pallas_call(
        body, name="b_ffn_gate", grid=(L // tm,),
        in_specs=[rows(D_MODEL), _full((D_FF, D_MODEL)), rows(D_FF), rows(D_FF, 1)],
        out_specs=[rows(D_FF), rows(D_FF)],
        out_shape=[jax.ShapeDtypeStruct((L, D_FF), BF16), jax.ShapeDtypeStruct((L, D_FF), BF16)],
        compiler_params=_params(("parallel",)),
    )(dh2b, w_down, acc_saved, up)


def _ffn_bwd_up(dacc, db, up, conv_w, w_up, h1, ffn_g, dh2, w_out):
    L = h1.shape[0]
    tm = _row_tile(L)
    nt = L // tm
    shard = w_up.shape[2]
    cw = [conv_w[j:j + 1] for j in range(3)]

    def body(da_ref, halo_ref, db_ref, a_ref, cw0, cw1, cw2, wu_ref, h1_ref, g_ref, dh2_ref, wo_ref,
             dup_ref, dh1_ref, dh1b_ref, dmix_ref, dg_ref, dcw_ref):
        i = pl.program_id(0)

        @pl.when(i == 0)
        def _():
            dg_ref[...] = jnp.zeros_like(dg_ref)
            dcw_ref[...] = jnp.zeros_like(dcw_ref)

        sub = _iota((tm, 1), 0)
        sub8 = _iota((8, 1), 0)
        last_tile = i == nt - 1
        dbv = db_ref[...]
        dup_ref[:, D_FF:2 * D_FF] = dbv
        dn = _dot_nt(dbv[:, 0:shard], wu_ref[2]) + _dot_nt(dbv[:, shard:2 * shard], wu_ref[3])
        for half in range(2):
            cols = slice(half * shard, (half + 1) * shard)
            d0 = da_ref[:, cols].astype(F32)
            halo = jnp.where(last_tile, 0.0, halo_ref[:, cols].astype(F32))
            d1 = jnp.where(sub == tm - 1, _pick_row(halo, 0), pltpu.roll(d0, tm - 1, 0))
            d2 = jnp.where(sub == tm - 2, _pick_row(halo, 0),
                           jnp.where(sub == tm - 1, _pick_row(halo, 1), pltpu.roll(d0, tm - 2, 0)))
            a = a_ref[:, cols].astype(F32)
            upd = jnp.zeros((8, shard), F32)
            for j, t in enumerate((d2 * a, d1 * a, d0 * a, d0)):
                upd = upd + jnp.where(sub8 == j, jnp.sum(t, axis=0, keepdims=True), 0.0)
            dcw_ref[:, cols] = dcw_ref[:, cols] + upd
            da = (d0 * cw2[:, cols] + d1 * cw1[:, cols] + d2 * cw0[:, cols]).astype(BF16)
            dup_ref[:, cols] = da
            dn = dn + _dot_nt(da, wu_ref[half])
        h1 = h1_ref[...]
        r = lax.rsqrt(jnp.mean(h1 * h1, axis=-1, keepdims=True) + EPS)
        yn = h1 * r
        dg_ref[...] = dg_ref[...] + jnp.sum(dn * yn, axis=0, keepdims=True)
        dyn = dn * g_ref[...]
        dh1 = dh2_ref[...] + r * (dyn - yn * jnp.mean(dyn * yn, axis=-1, keepdims=True))
        dh1_ref[...] = dh1
        dh1b = dh1.astype(BF16)
        dh1b_ref[...] = dh1b
        dmix_ref[...] = _dot_nt(dh1b, wo_ref[...]).astype(BF16)

    rows = lambda w: pl.BlockSpec((tm, w), lambda i: (i, 0))
    halo = pl.BlockSpec((8, D_FF), lambda i: (jnp.minimum((i + 1) * (tm // 8), L // 8 - 1), 0))
    return pl.pallas_call(
        body, name="b_ffn_up", grid=(nt,),
        in_specs=[rows(D_FF), halo, rows(D_FF), rows(D_FF), _full((1, D_FF)), _full((1, D_FF)), _full((1, D_FF)),
                  _full((N_CHIPS, D_MODEL, shard)), rows(D_MODEL), _full((1, D_MODEL)), rows(D_MODEL),
                  _full((D_MODEL, D_MODEL))],
        out_specs=[rows(2 * D_FF), rows(D_MODEL), rows(D_MODEL), rows(D_MODEL), _full((1, D_MODEL)),
                   _full((8, D_FF))],
        out_shape=[jax.ShapeDtypeStruct((L, 2 * D_FF), BF16), jax.ShapeDtypeStruct((L, D_MODEL), F32),
                   jax.ShapeDtypeStruct((L, D_MODEL), BF16), jax.ShapeDtypeStruct((L, D_MODEL), BF16),
                   jax.ShapeDtypeStruct((1, D_MODEL), F32), jax.ShapeDtypeStruct((8, D_FF), F32)],
        compiler_params=_params(("arbitrary",)),
    )(dacc, dacc, db, up, cw[0], cw[1], cw[2], w_up, h1, ffn_g, dh2, w_out)


def _wgrad(a, b, name, tn=None, tk=None):
    L, K = a.shape
    N = b.shape[1]
    tn = N if tn is None else tn
    tk = K if tk is None else tk
    tl = _row_tile(L, (1408, 768, 512, 256, 128))

    def body(a_ref, b_ref, o_ref):
        @pl.when(pl.program_id(2) == 0)
        def _():
            o_ref[...] = jnp.zeros_like(o_ref)

        o_ref[0] = o_ref[0] + _dot_tn(a_ref[...], b_ref[...])

    return pl.pallas_call(
        body, name=name, grid=(N // tn, K // tk, L // tl),
        in_specs=[pl.BlockSpec((tl, tk), lambda n, k, l: (l, k)), pl.BlockSpec((tl, tn), lambda n, k, l: (l, n))],
        out_specs=pl.BlockSpec((1, tk, tn), lambda n, k, l: (n, k, 0)),
        out_shape=jax.ShapeDtypeStruct((N // tn, K, tn), F32),
        compiler_params=_params(("parallel", "parallel", "arbitrary")),
    )(a, b)


def _retention_bwd(dmix, o, proj, cos_t, sin_t, ret_g, states, exchange=()):
    L = proj.shape[0]
    nblk = L // BLK
    G = _block_group(nblk)
    steps = nblk // G
    nx = len(exchange)
    dmat, wq_t, wk_t, g_blk = _decay_tables()

    def body(dm_ref, o_ref, q_ref, k_ref, v_ref, gate_ref, cos_ref, sin_ref, d_ref, wq_ref, wk_ref, rg_ref, rs_ref,
             *rest):
        x_in, (dp_ref, drg_ref), x_out, gstate = rest[:nx], rest[nx:nx + 2], rest[nx + 2:2 * nx + 2], rest[2 * nx + 2]

        @pl.when(pl.program_id(0) == 0)
        def _():
            if nx:
                for cp in _sibling_half_copies(x_in, x_out, *rest[2 * nx + 3:])[0]:
                    cp.start()
            gstate[...] = jnp.zeros_like(gstate)
            drg_ref[...] = jnp.zeros_like(drg_ref)

        lane = _iota((BLK, BLK), 1)
        sub = _iota((BLK, BLK), 0)
        scale = HEAD_LANES ** -0.5
        for b in reversed(range(G)):
            rows = slice(b * BLK, (b + 1) * BLK)
            rot, rot_t = _rot_fns(cos_ref[rows, :], sin_ref[rows, :])
            for p in range(2):
                qr = rot(q_ref[rows, p * BLK:(p + 1) * BLK].astype(F32))
                kr = rot(k_ref[rows, p * BLK:(p + 1) * BLK].astype(F32)) * scale
                kr_b = kr.astype(BF16)
                qw = (qr * wq_ref[p]).astype(BF16)
                kw = (kr * wk_ref[p]).astype(BF16)
                dqr = jnp.zeros((BLK, BLK), F32)
                dkr = jnp.zeros((BLK, BLK), F32)
                for e in range(2):
                    h = 2 * p + e
                    cols = slice(h * BLK, (h + 1) * BLK)
                    head_lanes = (lane >> 6) == e
                    o = o_ref[rows, cols]
                    rn = lax.rsqrt(jnp.mean(o * o, axis=-1, keepdims=True) + EPS)
                    y = o * rn
                    gate = gate_ref[rows, cols].astype(F32)
                    sg = jax.nn.sigmoid(gate)
                    dm = dm_ref[rows, cols].astype(F32)
                    rgain = rg_ref[:, cols]
                    drg_ref[:, cols] = drg_ref[:, cols] + jnp.sum(dm * y * (gate * sg), axis=0, keepdims=True)
                    dp_ref[rows, 1024 + h * BLK:1024 + (h + 1) * BLK] = (
                        dm * y * rgain * (sg * (1.0 + gate * (1.0 - sg)))).astype(BF16)
                    dy = dm * rgain * (gate * sg)
                    do = (rn * (dy - y * jnp.mean(dy * y, axis=-1, keepdims=True))).astype(BF16)
                    vh = v_ref[rows, cols]
                    qm = jnp.where(head_lanes, qr, 0.0).astype(BF16)
                    dmh = d_ref[h]
                    s = (_dot_nt(qm, kr_b) * dmh).astype(BF16)
                    ds = (_dot_nt(do, vh) * dmh).astype(BF16)
                    st = rs_ref[b, h].astype(BF16)
                    gs = gstate[h]
                    gs_b = gs.astype(BF16)
                    dqr = dqr + jnp.where(head_lanes, _dot(ds, kr_b), 0.0) + _dot_nt(do, st) * wq_ref[p]
                    dkr = dkr + _dot_tn(ds, qm) + _dot_nt(vh, gs_b) * wk_ref[p]
                    dp_ref[rows, 512 + h * BLK:512 + (h + 1) * BLK] = (_dot_tn(s, do) + _dot(kw, gs_b)).astype(BF16)
                    dr = jnp.where((sub >> 6) == e, _dot_tn(qw, do), 0.0)
                    gstate[h] = dr + g_blk[h] * gs
                dp_ref[rows, p * BLK:(p + 1) * BLK] = rot_t(dqr).astype(BF16)
                dp_ref[rows, 256 + p * BLK:256 + (p + 1) * BLK] = (rot_t(dkr) * scale).astype(BF16)

        if nx:
            @pl.when(pl.program_id(0) == steps - 1)
            def _():
                sends, recvs = _sibling_half_copies(x_in, x_out, *rest[2 * nx + 3:])
                for cp in recvs:
                    cp.wait_recv()
                for cp in sends:
                    cp.wait_send()

    row = lambda c: (lambda i: (steps - 1 - i, c))
    return pl.pallas_call(
        body, name="b_retention", grid=(steps,),
        in_specs=[pl.BlockSpec((G * BLK, 512), row(0)), pl.BlockSpec((G * BLK, 512), row(0)),
                  pl.BlockSpec((G * BLK, 256), row(0)), pl.BlockSpec((G * BLK, 256), row(1)),
                  pl.BlockSpec((G * BLK, 512), row(1)), pl.BlockSpec((G * BLK, 512), row(2)),
                  pl.BlockSpec((G * BLK, BLK), row(0)), pl.BlockSpec((G * BLK, BLK), row(0)),
                  _full((RET_HEADS, BLK, BLK)), _full((2, BLK, BLK)), _full((2, BLK, BLK)), _full((1, 512)),
                  pl.BlockSpec((G, RET_HEADS, BLK, BLK), lambda i: (steps - 1 - i, 0, 0, 0))] + [_ANY] * nx,
        out_specs=[pl.BlockSpec((G * BLK, RET_W), row(0)), _full((1, 512))] + [_ANY] * nx,
        out_shape=[jax.ShapeDtypeStruct((L, RET_W), BF16), jax.ShapeDtypeStruct((1, 512), F32)]
        + _sibling_half_shapes(exchange),
        scratch_shapes=[pltpu.VMEM((RET_HEADS, BLK, BLK), F32)] + _sibling_half_semaphores(nx),
        compiler_params=_params(("arbitrary",)),
    )(dmix, o, proj, proj, proj, proj, cos_t, sin_t, dmat, wq_t, wk_t, ret_g, states, *exchange)


def _fox_delta(dmix, o_f):
    L = o_f.shape[0]
    nblk = L // BLK
    G = _block_group(nblk)

    def body(do_ref, o_ref, d_ref):
        sel = ((_iota((8, 512), 1) >> 6) == _iota((8, 512), 0)).astype(BF16)
        for b in range(G):
            rows = slice(b * BLK, (b + 1) * BLK)
            prod = do_ref[rows, :].astype(F32) * o_ref[rows, :].astype(F32)
            hi = prod.astype(BF16)
            lo = (prod - hi.astype(F32)).astype(BF16)
            d_ref[b] = _dot_nt(sel, hi) + _dot_nt(sel, lo)

    return pl.pallas_call(
        body, name="b_foxdelta", grid=(nblk // G,),
        in_specs=[pl.BlockSpec((G * BLK, 512), lambda i: (i, 1)), pl.BlockSpec((G * BLK, 512), lambda i: (i, 0))],
        out_specs=pl.BlockSpec((G, 8, BLK), lambda i: (i, 0, 0)),
        out_shape=jax.ShapeDtypeStruct((nblk, 8, BLK), F32),
        compiler_params=_params(("parallel",)),
    )(dmix, o_f)


def _fox_bwd(proj, dmix, c, ctb, lse, delta, scatter=()):
    L = proj.shape[0]
    nblk, nu = _fox_units(L)
    scale = HEAD_LANES ** -0.5
    ns = len(scatter)

    steps = FOX_HEADS // (2 * FOX_PAIRS)

    def body(qkv_ref, do_ref, c_ref, ct_ref, lse_ref, dl_ref, *rest):
        s_in, (dp_ref, dc_ref, dcq_ref), s_out = rest[:ns], rest[ns:ns + 3], rest[ns + 3:2 * ns + 3]
        ktt, dqt, dk_acc, dv_acc, dcs_acc = rest[2 * ns + 3:2 * ns + 8]
        p = pl.program_id(0)
        heads = [(pp, e, 2 * FOX_PAIRS * p + 2 * pp + e) for pp in range(FOX_PAIRS) for e in range(2)]

        @pl.when(p == 0)
        def _():
            dc_ref[...] = jnp.zeros_like(dc_ref)
            dcq_ref[...] = jnp.zeros_like(dcq_ref)
            if ns:
                for cp in _scatter_copies(s_in, s_out, *rest[2 * ns + 8:]):
                    cp.start()

        sub8 = _iota((8, BLK), 0)
        masks = _fox_tile_masks()

        def pre(j, carry):
            off = pl.multiple_of(j * BLK, BLK)
            for pp in range(FOX_PAIRS):
                ktt[pp, j] = qkv_ref[pl.ds(off, BLK), pp * 384 + BLK:pp * 384 + 2 * BLK].astype(F32).T.astype(BF16)
                dqt[pp, j] = jnp.zeros((BLK, BLK), F32)
            return carry

        lax.fori_loop(0, nblk, pre, 0)

        def kv_pass(kblk, nk, n_later):
            klen = nk * BLK
            koff = pl.multiple_of(kblk * BLK, BLK)
            kt = [qkv_ref[pl.ds(koff, klen), pp * 384 + BLK:pp * 384 + 2 * BLK] for pp in range(FOX_PAIRS)]
            vtile = [qkv_ref[pl.ds(koff, klen), pp * 384 + 2 * BLK:pp * 384 + 3 * BLK] for pp in range(FOX_PAIRS)]
            ct = c_ref[pl.ds(koff, klen), :]
            klane = _iota((klen, BLK), 1)
            cs = [jnp.broadcast_to(jnp.sum(jnp.where(klane == h, ct, 0.0), axis=1, keepdims=True), (klen, WIDE * UNIT))
                  for _, _, h in heads]
            for pp in range(FOX_PAIRS):
                dk_acc[pp, 0:klen] = jnp.zeros((klen, BLK), F32)
                dv_acc[pp, 0:klen] = jnp.zeros((klen, BLK), F32)
            for hh in range(len(heads)):
                dcs_acc[hh, 0:klen] = jnp.zeros((klen, BLK), F32)

            def tile(qblk, nq, mask):
                qlen = nq * BLK
                if mask == "valid":
                    mask = _iota((klen, qlen), 0) >= N_PAD
                qoff = pl.multiple_of(qblk * BLK, BLK)
                qlane = _iota((qlen, BLK), 1)
                qs = [qkv_ref[pl.ds(qoff, qlen), pp * 384:pp * 384 + BLK].astype(F32) * (scale * LOG2E)
                      for pp in range(FOX_PAIRS)]
                dot_ = [do_ref[pl.ds(qoff, qlen), pp * BLK:(pp + 1) * BLK] for pp in range(FOX_PAIRS)]
                stats = [[ref[qblk + a] for a in range(nq)] for ref in (ct_ref, lse_ref, dl_ref)]
                dcq = [jnp.zeros((8, BLK), F32) for _ in range(nq)]
                for hh, (pp, e, h) in enumerate(heads):
                    head = (qlane >> 6) == e
                    ct_row, lse_row, dl_row = [jnp.concatenate([_pick_row(t, h) for t in ts], axis=1) for ts in stats]
                    qm = jnp.where(head, qs[pp], 0.0).astype(BF16)
                    dom = jnp.where(head, dot_[pp], jnp.zeros_like(dot_[pp]))
                    t = _dot_nt(kt[pp], qm) - cs[hh][:, 0:qlen]
                    if mask is not None:
                        t = jnp.where(mask, t, NEG)
                    pr = jnp.exp2(t + (ct_row - lse_row))
                    dv_acc[pp, 0:klen] = dv_acc[pp, 0:klen] + _dot(pr.astype(BF16), dom)
                    dsv = pr * (_dot_nt(vtile[pp], dom) - dl_row)
                    ds_b = dsv.astype(BF16)
                    dk_acc[pp, 0:klen] = dk_acc[pp, 0:klen] + _dot(ds_b, qm)
                    rows = slice(e * HEAD_LANES, (e + 1) * HEAD_LANES)
                    dq_t = _dot(ktt[pp, kblk, rows, :], ds_b[0:BLK])
                    for b in range(1, nk):
                        dq_t = dq_t + _dot(ktt[pp, kblk + b, rows, :], ds_b[b * BLK:(b + 1) * BLK])
                    key_side = dsv[:, 0:BLK]
                    for a in range(1, nq):
                        key_side = key_side + dsv[:, a * BLK:(a + 1) * BLK]
                    dcs_acc[hh, 0:klen] = dcs_acc[hh, 0:klen] + key_side
                    query_side = jnp.sum(dsv, axis=0, keepdims=True)
                    for a in range(nq):
                        cols = slice(a * BLK, (a + 1) * BLK)
                        dqt[pp, qblk + a, rows, :] = dqt[pp, qblk + a, rows, :] + dq_t[:, cols]
                        dcq[a] = dcq[a] + jnp.where(sub8 == h, query_side[:, cols], 0.0)
                for a in range(nq):
                    dcq_ref[qblk + a] = dcq_ref[qblk + a] + dcq[a]

            later_mask = "valid" if nk == 1 else None
            n_later = jnp.asarray(n_later, jnp.int32)
            n_wide = n_later // WIDE

            def later_wide(i, carry):
                tile(kblk + nk + 2 * WIDE * i, 2 * WIDE, later_mask)
                return carry

            tile(kblk, nk, masks["first"] if nk == 1 else masks["diag"])
            lax.fori_loop(0, n_wide, later_wide, 0)
            rest_blk = kblk + nk + 2 * WIDE * n_wide

            @pl.when((n_later & 2) != 0)
            def _():
                tile(rest_blk, 4, later_mask)

            @pl.when((n_later & 1) != 0)
            def _():
                tile(rest_blk + 2 * (n_later & 2), 2, later_mask)

            upd = jnp.zeros((klen, BLK), F32)
            for hh, (_, _, h) in enumerate(heads):
                upd = upd + jnp.where(klane == h, -jnp.sum(dcs_acc[hh, 0:klen], axis=1, keepdims=True), 0.0)
            dc_ref[pl.ds(koff, klen), :] = dc_ref[pl.ds(koff, klen), :] + upd
            for pp in range(FOX_PAIRS):
                dp_ref[pl.ds(koff, klen), pp * 384 + BLK:pp * 384 + 2 * BLK] = (
                    dk_acc[pp, 0:klen] * (1.0 / LOG2E)).astype(BF16)
                dp_ref[pl.ds(koff, klen), pp * 384 + 2 * BLK:pp * 384 + 3 * BLK] = dv_acc[pp, 0:klen].astype(BF16)

        kv_pass(0, 1, nu)

        def k_loop(u, carry):
            kv_pass(1 + 2 * u, 2, nu - 1 - u)
            return carry

        lax.fori_loop(0, nu, k_loop, 0)

        def flush(j, carry):
            off = pl.multiple_of(j * BLK, BLK)
            for pp in range(FOX_PAIRS):
                dp_ref[pl.ds(off, BLK), pp * 384:pp * 384 + BLK] = (dqt[pp, j].T * scale).astype(BF16)
            return carry

        lax.fori_loop(0, nblk, flush, 0)

        if ns:
            @pl.when(p == steps - 1)
            def _():
                copies = _scatter_copies(s_in, s_out, *rest[2 * ns + 8:])
                for cp in copies:
                    cp.wait_recv()
                for cp in copies:
                    cp.wait_send()

    width = 384 * FOX_PAIRS
    once = lambda shape, index: pl.BlockSpec(shape, index, pipeline_mode=pl.Buffered(1))
    stat = once((nblk, 8, BLK), lambda p: (0, 0, 0))
    return pl.pallas_call(
        body, name="b_fox", grid=(steps,),
        in_specs=[once((L, width), lambda p: (0, RET_W // width + p)),
                  once((L, FOX_PAIRS * BLK), lambda p: (0, 4 // FOX_PAIRS + p)),
                  once((L, BLK), lambda p: (0, 0)), stat, stat, stat] + [_ANY] * ns,
        out_specs=[pl.BlockSpec((L, width), lambda p: (0, p)), _full((L, BLK)), _full((nblk, 8, BLK))] + [_ANY] * ns,
        out_shape=[jax.ShapeDtypeStruct((L, FOX_W), BF16), jax.ShapeDtypeStruct((L, BLK), F32),
                   jax.ShapeDtypeStruct((nblk, 8, BLK), F32)] + _scatter_shapes(scatter),
        scratch_shapes=[pltpu.VMEM((FOX_PAIRS, nblk, BLK, BLK), BF16), pltpu.VMEM((FOX_PAIRS, nblk, BLK, BLK), F32),
                        pltpu.VMEM((FOX_PAIRS, UNIT, BLK), F32), pltpu.VMEM((FOX_PAIRS, UNIT, BLK), F32),
                        pltpu.VMEM((2 * FOX_PAIRS, UNIT, BLK), F32)]
        + _scatter_semaphores(ns),
        compiler_params=_params(("arbitrary",)),
    )(proj, dmix, c, ctb, lse, delta, *scatter)


def _fox_post(dc, dcq, ff, fb):
    L = dc.shape[0]
    nblk = L // BLK
    G = _block_group(nblk)
    steps = nblk // G

    def body(dc_ref, dcq_ref, ff_ref, b_ref, dff_ref, dffb_ref, dfb_ref, carry):
        @pl.when(pl.program_id(0) == 0)
        def _():
            carry[...] = jnp.zeros_like(carry)
            dfb_ref[...] = jnp.zeros_like(dfb_ref)

        tri = (_iota((BLK, BLK), 0) <= _iota((BLK, BLK), 1)).astype(BF16)
        live = _iota((BLK, BLK), 1) < FOX_HEADS
        run, dfb = carry[...], dfb_ref[...]
        for b in reversed(range(G)):
            rows = slice(b * BLK, (b + 1) * BLK)
            d = dc_ref[rows, :] + jnp.concatenate([dcq_ref[b], jnp.zeros((BLK - 8, BLK), F32)], axis=0).T
            hi, mid, lo = _split3(d)
            dlf = _dot(tri, hi) + _dot(tri, mid) + _dot(tri, lo) + run
            run = run + jnp.sum(d, axis=0, keepdims=True)
            z = ff_ref[rows, :] + b_ref[...]
            dff = jnp.where(live, dlf * jax.nn.sigmoid(-z), 0.0)
            dff_ref[rows, :] = dff
            dffb_ref[rows, :] = dff.astype(BF16)
            dfb = dfb + jnp.sum(dff, axis=0, keepdims=True)
        carry[...] = run
        dfb_ref[...] = dfb

    rev = lambda i: (steps - 1 - i, 0)
    return pl.pallas_call(
        body, name="b_foxpost", grid=(steps,),
        in_specs=[pl.BlockSpec((G * BLK, BLK), rev), pl.BlockSpec((G, 8, BLK), lambda i: (steps - 1 - i, 0, 0)),
                  pl.BlockSpec((G * BLK, BLK), rev), _full((1, BLK))],
        out_specs=[pl.BlockSpec((G * BLK, BLK), rev), pl.BlockSpec((G * BLK, BLK), rev), _full((1, BLK))],
        out_shape=[jax.ShapeDtypeStruct((L, BLK), F32), jax.ShapeDtypeStruct((L, BLK), BF16),
                   jax.ShapeDtypeStruct((1, BLK), F32)],
        scratch_shapes=[pltpu.VMEM((1, BLK), F32)],
        compiler_params=_params(("arbitrary",)),
    )(dc, dcq, ff, fb)


def _inproj_bwd(dpr, dpf, dffb, w_main, w_ff, h0, g, dh1, scatter=()):
    L = h0.shape[0]
    tm = _row_tile(L)
    ns = len(scatter)

    def body(dpr_ref, dpf_ref, dff_ref, wm_ref, wf_ref, h_ref, g_ref, dh1_ref, *rest):
        s_in, (dh0_ref, dg_ref), s_out = rest[:ns], rest[ns:ns + 2], rest[ns + 2:2 * ns + 2]

        @pl.when(pl.program_id(0) == 0)
        def _():
            dg_ref[...] = jnp.zeros_like(dg_ref)
            if ns:
                for cp in _scatter_copies(s_in, s_out, *rest[2 * ns + 2:]):
                    cp.start()

        dn = (_dot_nt(dpr_ref[...], wm_ref[:, 0:RET_W]) + _dot_nt(dpf_ref[...], wm_ref[:, RET_W:MAIN_W])
              + _dot_nt(dff_ref[...], wf_ref[...]))
        h = h_ref[...]
        r = lax.rsqrt(jnp.mean(h * h, axis=-1, keepdims=True) + EPS)
        yn = h * r
        dg_ref[...] = dg_ref[...] + jnp.sum(dn * yn, axis=0, keepdims=True)
        dyn = dn * g_ref[...]
        dh0_ref[...] = dh1_ref[...] + r * (dyn - yn * jnp.mean(dyn * yn, axis=-1, keepdims=True))

        if ns:
            @pl.when(pl.program_id(0) == L // tm - 1)
            def _():
                copies = _scatter_copies(s_in, s_out, *rest[2 * ns + 2:])
                for cp in copies:
                    cp.wait_recv()
                for cp in copies:
                    cp.wait_send()

    rows = lambda w: pl.BlockSpec((tm, w), lambda i: (i, 0))
    return pl.pallas_call(
        body, name="b_inproj", grid=(L // tm,),
        in_specs=[rows(RET_W), rows(FOX_W), rows(BLK), _full((D_MODEL, MAIN_W)), _full((D_MODEL, BLK)),
                  rows(D_MODEL), _full((1, D_MODEL)), rows(D_MODEL)] + [_ANY] * ns,
        out_specs=[rows(D_MODEL), _full((1, D_MODEL))] + [_ANY] * ns,
        out_shape=[jax.ShapeDtypeStruct((L, D_MODEL), F32), jax.ShapeDtypeStruct((1, D_MODEL), F32)]
        + _scatter_shapes(scatter),
        scratch_shapes=_scatter_semaphores(ns),
        compiler_params=_params(("arbitrary",)),
    )(dpr, dpf, dffb, w_main, w_ff, h0, g, dh1, *scatter)


def _local_step(x, target, meta, attn_g, w_main, w_ff, fox_b, ret_g, w_out, ffn_g, w_up, conv_w, conv_b, w_down, final_g,
                late=None, mid=None, last=None):
    S = x.shape[0]
    L = S + PREFIX
    head = jnp.concatenate([jnp.zeros((N_PAD, D_MODEL), F32), meta], axis=0)
    fb = jnp.pad(fox_b, ((0, 0), (0, BLK - FOX_HEADS)))
    cos_t, sin_t = _rotary_tables(L)

    h0, n1, proj, ff = _rms_inproj(head, x, attn_g, w_main, w_ff)
    c, ctb = _fox_prep(ff, fb)
    mix_r, o_ret, states = _retention_fwd(proj, cos_t, sin_t, ret_g)
    if late is None:
        o_f, lse = _fox_fwd(proj, c, ctb)
    else:
        o_f, lse, *gathered = _fox_fwd(proj, c, ctb, gather=late[0])
        w_out, w_up, w_down = late[1](gathered)
    h1, n2, up, g_act, acc_saved = _outproj_up(mix_r, o_f, h0, w_out, ffn_g, w_up, conv_w, conv_b)
    dh2, dh2b, d_final_g, loss = _ffn_down_loss(g_act, w_down, h1, final_g, target)

    dacc, db = _ffn_bwd_gate(dh2b, w_down, acc_saved, up)
    dup, dh1, dh1b, dmix, d_ffn_g, dconv = _ffn_bwd_up(dacc, db, up, conv_w, w_up, h1, ffn_g, dh2, w_out)
    d_w_down = _wgrad(g_act, dh2b, "wgrad_down", tk=D_FF // 2)[0]
    d_w_up = _wgrad(n2, dup, "wgrad_up", tn=w_up.shape[2])
    d_w_out = jnp.concatenate([_wgrad(mix_r, dh1b, "wgrad_out_r")[0], _wgrad(o_f, dh1b, "wgrad_out_f")[0]], axis=0)

    early = () if mid is None else mid[0](d_w_out, d_w_up, d_w_down)
    dpr, d_ret_g, *from_sibling = _retention_bwd(dmix, o_ret, proj, cos_t, sin_t, ret_g, states, exchange=early)
    delta = _fox_delta(dmix, o_f)
    scatter = () if mid is None else mid[1](early, from_sibling)
    dpf, dc, dcq, *received = _fox_bwd(proj, dmix, c, ctb, lse, delta, scatter=scatter)
    dff, dffb, d_fox_b = _fox_post(dc, dcq, ff, fb)
    d_w_main = jnp.concatenate([_wgrad(n1, dpr, "wgrad_in_r")[0], _wgrad(n1, dpf, "wgrad_in_f")[0]], axis=1)
    d_w_ff = _wgrad(n1, dffb, "wgrad_in_ff")[0][:, :FOX_HEADS]
    scatter_in = () if last is None else last(d_w_main, d_w_ff)
    dh0, d_attn_g, *received_in = _inproj_bwd(dpr, dpf, dffb, w_main, w_ff, h0, attn_g, dh1, scatter=scatter_in)

    return dict(
        loss=loss[0, 0], dx=dh0[PREFIX:], dmeta=dh0[N_PAD:PREFIX], attn_g=d_attn_g, w_main=d_w_main,
        w_ff=d_w_ff, fox_b=d_fox_b[:, :FOX_HEADS], ret_g=d_ret_g, w_out=d_w_out, ffn_g=d_ffn_g,
        w_up=d_w_up, conv_w=dconv[0:3], conv_b=dconv[3:4], w_down=d_w_down, final_g=d_final_g,
        scatter=list(scatter_in) + list(scatter), received=list(received_in) + list(received))


_ANY = pl.BlockSpec(memory_space=pl.ANY)


def _place():
    return lax.axis_index("x"), lax.axis_index("y"), lax.axis_index("c")


def _other_chips(x, y):
    return [(1 - x, y), (x, 1 - y), (1 - x, 1 - y)]


def _allgather_semaphores(n):
    if n == 0:
        return []
    return [pltpu.SemaphoreType.DMA((3 * n,)), pltpu.SemaphoreType.DMA((3 * n,)), pltpu.SemaphoreType.DMA((n,))]


def _allgather_copies(ins, outs, send, recv, loc):
    n = len(ins)
    x, y, c = _place()
    mine = 2 * x + y
    peers = _other_chips(x, y)

    def remote(a, k, slot):
        return pltpu.make_async_remote_copy(
            src_ref=ins[a], dst_ref=outs[a].at[slot], send_sem=send.at[3 * a + k], recv_sem=recv.at[3 * a + k],
            device_id=(peers[k][0], peers[k][1], c), device_id_type=MESH)

    local = [pltpu.make_async_copy(ins[a], outs[a].at[mine], loc.at[a]) for a in range(n)]
    sends = [remote(a, k, mine) for a in range(n) for k in range(3)]
    recvs = [remote(a, k, 2 * peers[k][0] + peers[k][1]) for a in range(n) for k in range(3)]
    return local, sends, recvs


def _chip_allgather_halves(w, small):
    half = w.shape[0] // 2

    def body(w_ref, s_ref, wo_ref, so_ref, send, recv, fsend, frecv, ssend, srecv, loc):
        x, y, c = _place()
        mine = 2 * x + y
        peers = _other_chips(x, y)

        def fetch(k, slot):
            return pltpu.make_async_remote_copy(
                src_ref=w_ref.at[pl.ds(c * half, half)], dst_ref=wo_ref.at[slot, c], send_sem=send.at[k],
                recv_sem=recv.at[k], device_id=(peers[k][0], peers[k][1], c), device_id_type=MESH)

        def forward(k, which):
            slot = 2 * peers[k][0] + peers[k][1]
            return pltpu.make_async_remote_copy(
                src_ref=wo_ref.at[slot, which], dst_ref=wo_ref.at[slot, which], send_sem=fsend.at[k],
                recv_sem=frecv.at[k], device_id=(x, y, 1 - c), device_id_type=MESH)

        def small_copy(k, slot):
            return pltpu.make_async_remote_copy(
                src_ref=s_ref, dst_ref=so_ref.at[slot], send_sem=ssend.at[k], recv_sem=srecv.at[k],
                device_id=(peers[k][0], peers[k][1], c), device_id_type=MESH)

        local = pltpu.make_async_copy(s_ref, so_ref.at[mine], loc.at[0])
        sends = [fetch(k, mine) for k in range(3)] + [small_copy(k, mine) for k in range(3)]
        local.start()
        for cp in sends:
            cp.start()
        forwards = []
        for k in range(3):
            fetch(k, 2 * peers[k][0] + peers[k][1]).wait_recv()
            forwards.append(forward(k, c))
            forwards[-1].start()
        for k in range(3):
            forward(k, 1 - c).wait_recv()
            small_copy(k, 2 * peers[k][0] + peers[k][1]).wait_recv()
        for cp in sends + forwards:
            cp.wait_send()
        local.wait()

    three = pltpu.SemaphoreType.DMA((3,))
    return pl.pallas_call(
        body, name="ag_weights", in_specs=[_ANY] * 2, out_specs=[_ANY] * 2,
        out_shape=[jax.ShapeDtypeStruct((N_CHIPS, 2, half, w.shape[1]), w.dtype),
                   jax.ShapeDtypeStruct((N_CHIPS,) + small.shape, small.dtype)],
        scratch_shapes=[three, three, three, three, three, three, pltpu.SemaphoreType.DMA((1,))],
    )(w, small)


def _chip_allgather(arrays):
    n = len(arrays)

    def body(*refs):
        local, sends, recvs = _allgather_copies(refs[:n], refs[n:2 * n], *refs[2 * n:])
        for cp in local + sends:
            cp.start()
        for cp in recvs:
            cp.wait_recv()
        for cp in sends:
            cp.wait_send()
        for cp in local:
            cp.wait()

    return pl.pallas_call(
        body, name="ag_weights", in_specs=[_ANY] * n, out_specs=[_ANY] * n,
        out_shape=[jax.ShapeDtypeStruct((N_CHIPS,) + a.shape, a.dtype) for a in arrays],
        scratch_shapes=_allgather_semaphores(n),
    )(*arrays)


def _sibling_halves(grads):
    n = len(grads)

    def body(*refs):
        sends, recvs = _sibling_half_copies(refs[:n], refs[n:2 * n], *refs[2 * n:])
        for cp in sends:
            cp.start()
        for cp in recvs:
            cp.wait_recv()
        for cp in sends:
            cp.wait_send()

    return pl.pallas_call(
        body, name="rs_sibling", in_specs=[_ANY] * n, out_specs=[_ANY] * n,
        out_shape=_sibling_half_shapes(grads), scratch_shapes=_sibling_half_semaphores(n),
    )(*grads)


def _sibling_half_shapes(grads):
    return [jax.ShapeDtypeStruct((N_CHIPS, g.shape[1] // 2, g.shape[2]), g.dtype) for g in grads]


def _sibling_half_semaphores(n):
    return [pltpu.SemaphoreType.DMA((n,)), pltpu.SemaphoreType.DMA((n,))] if n else []


def _sibling_half_copies(ins, outs, send, recv):
    x, y, c = _place()

    def half_copy(a, which):
        half = ins[a].shape[1] // 2
        return pltpu.make_async_remote_copy(
            src_ref=ins[a].at[pl.ds(0, N_CHIPS), pl.ds(which * half, half)], dst_ref=outs[a],
            send_sem=send.at[a], recv_sem=recv.at[a], device_id=(x, y, 1 - c), device_id_type=MESH)

    return [half_copy(a, 1 - c) for a in range(len(ins))], [half_copy(a, c) for a in range(len(ins))]


def _scatter_shapes(parts):
    return [jax.ShapeDtypeStruct((3,) + p.shape[1:], p.dtype) for p in parts]


def _scatter_semaphores(n):
    return [pltpu.SemaphoreType.DMA((3 * n,)), pltpu.SemaphoreType.DMA((3 * n,))] if n else []


def _scatter_copies(ins, outs, send, recv):
    x, y, c = _place()
    peers = _other_chips(x, y)
    return [pltpu.make_async_remote_copy(
        src_ref=ins[a].at[2 * peers[k][0] + peers[k][1]], dst_ref=outs[a].at[k], send_sem=send.at[3 * a + k],
        recv_sem=recv.at[3 * a + k], device_id=(peers[k][0], peers[k][1], c), device_id_type=MESH)
        for a in range(len(ins)) for k in range(3)]


def _sibling_allgather(bufs, small):
    n = len(bufs)

    def body(*refs):
        small_in, outs, small_out = refs[n], refs[n + 1:2 * n + 1], refs[2 * n + 1]
        send, recv, s_send, s_recv, loc = refs[2 * n + 2:]
        x, y, c = _place()
        me = 4 * x + 2 * y + c

        def remote(a, which):
            return pltpu.make_async_remote_copy(
                src_ref=outs[a].at[which], dst_ref=outs[a].at[which], send_sem=send.at[a], recv_sem=recv.at[a],
                device_id=(x, y, 1 - c), device_id_type=MESH)

        def peer_of(r):
            return tuple(1 - v if (r >> b) & 1 else v for v, b in ((x, 2), (y, 1), (c, 0)))

        def small_copy(r, slot):
            return pltpu.make_async_remote_copy(
                src_ref=small_in, dst_ref=small_out.at[slot], send_sem=s_send.at[r - 1], recv_sem=s_recv.at[r - 1],
                device_id=peer_of(r), device_id_type=MESH)

        local = pltpu.make_async_copy(small_in, small_out.at[me], loc.at[0])
        sends = [remote(a, c) for a in range(n)] + [small_copy(r, me) for r in range(1, N_DEV)]
        local.start()
        for cp in sends:
            cp.start()
        for r in range(1, N_DEV):
            px, py, pc = peer_of(r)
            small_copy(r, 4 * px + 2 * py + pc).wait_recv()
        for a in range(n):
            remote(a, 1 - c).wait_recv()
        for cp in sends:
            cp.wait_send()
        local.wait()

    outs = pl.pallas_call(
        body, name="ag_sibling", in_specs=[_ANY] * (n + 1), out_specs=[_ANY] * (n + 1),
        out_shape=[jax.ShapeDtypeStruct(b.shape, b.dtype) for b in bufs]
        + [jax.ShapeDtypeStruct((N_DEV,) + small.shape, small.dtype)],
        input_output_aliases={a: a for a in range(n)},
        scratch_shapes=[pltpu.SemaphoreType.DMA((n,)), pltpu.SemaphoreType.DMA((n,)),
                        pltpu.SemaphoreType.DMA((N_DEV - 1,)), pltpu.SemaphoreType.DMA((N_DEV - 1,)),
                        pltpu.SemaphoreType.DMA((1,))],
    )(*bufs, small)
    return [o.reshape(2 * o.shape[1], o.shape[2]) for o in outs[:n]], outs[n]


def _pair_add(full, recv, core, name):
    _, R, C = full.shape
    half = R // 2

    def body(core_ref, a_ref, b_ref, o_ref):
        o_ref[...] = (a_ref[...] + b_ref[...]).astype(BF16)

    return pl.pallas_call(
        body, name=name,
        grid_spec=pltpu.PrefetchScalarGridSpec(
            num_scalar_prefetch=1, grid=(N_CHIPS,),
            in_specs=[pl.BlockSpec((1, half, C), lambda j, core_ref: (j, core_ref[0], 0)),
                      pl.BlockSpec((1, half, C), lambda j, core_ref: (j, 0, 0))],
            out_specs=pl.BlockSpec((1, half, C), lambda j, core_ref: (j, 0, 0))),
        out_shape=jax.ShapeDtypeStruct((N_CHIPS, half, C), BF16),
        compiler_params=_params(("parallel",)),
    )(core, full, recv)


def _sum_slots(q, name, tiles=2):
    n, R, C = q.shape
    tr = R // tiles

    def body(q_ref, o_ref):
        acc = q_ref[0].astype(F32)
        for j in range(1, n):
            acc = acc + q_ref[j].astype(F32)
        o_ref[...] = acc

    return pl.pallas_call(
        body, name=name, grid=(tiles,),
        in_specs=[pl.BlockSpec((n, tr, C), lambda i: (0, i, 0))],
        out_specs=pl.BlockSpec((tr, C), lambda i: (i, 0)),
        out_shape=jax.ShapeDtypeStruct((R, C), F32),
        compiler_params=_params(("parallel",)),
    )(q)


def _sum_partials(own_all, recv, place, name, tiles=2):
    _, R, C = own_all.shape
    tr = R // tiles

    def body(place_ref, own_ref, r_ref, o_ref):
        acc = own_ref[0].astype(F32)
        for k in range(3):
            acc = acc + r_ref[k].astype(F32)
        o_ref[0] = acc

    return pl.pallas_call(
        body, name=name,
        grid_spec=pltpu.PrefetchScalarGridSpec(
            num_scalar_prefetch=1, grid=(tiles,),
            in_specs=[pl.BlockSpec((1, tr, C), lambda i, place_ref: (place_ref[0], i, 0)),
                      pl.BlockSpec((3, tr, C), lambda i, place_ref: (0, i, 0))],
            out_specs=pl.BlockSpec((1, tr, C), lambda i, place_ref: (place_ref[1], i, 0))),
        out_shape=jax.ShapeDtypeStruct((2, R, C), F32),
        compiler_params=_params(("parallel",)),
    )(place, own_all, recv)


def _adamw(w, g, m, v, name, tiles=4):
    R, C = w.shape
    tr = R // tiles

    def body(w_ref, g_ref, m_ref, v_ref, go_ref, d_ref, m2_ref, v2_ref):
        g_ = g_ref[...]
        go_ref[...] = g_
        m2 = ADAM_B1 * m_ref[...] + (1.0 - ADAM_B1) * g_
        v2 = ADAM_B2 * v_ref[...] + (1.0 - ADAM_B2) * (g_ * g_)
        m_hat = m2 / (1.0 - ADAM_B1 ** ADAM_STEP)
        v_hat = v2 / (1.0 - ADAM_B2 ** ADAM_STEP)
        d_ref[...] = -ADAM_LR * (m_hat / (jnp.sqrt(v_hat) + ADAM_EPS) + ADAM_WD * w_ref[...])
        m2_ref[...] = m2
        v2_ref[...] = v2

    spec = pl.BlockSpec((tr, C), lambda i: (i, 0))
    return pl.pallas_call(
        body, name=name, grid=(tiles,), in_specs=[spec] * 4, out_specs=[spec] * 4,
        out_shape=[jax.ShapeDtypeStruct((R, C), F32)] * 4,
        compiler_params=_params(("parallel",)),
    )(w, g, m, v)


def _pack_rows(pieces, rows):
    flat = jnp.concatenate([jnp.pad(p.reshape(-1).astype(F32), (0, (-p.size) % D_MODEL)) for p in pieces])
    return jnp.pad(flat, (0, rows * D_MODEL - flat.size)).reshape(rows, D_MODEL)


def _unpack_rows(pack, shapes):
    flat = pack.reshape(-1)
    out, off = [], 0
    for shp in shapes:
        size = int(np.prod(shp))
        out.append(flat[off:off + size].reshape(shp))
        off += size + (-size) % D_MODEL
    return out


def _kernel_order(w):
    parts = [w[:, 0:RET_W]]
    for p in range(FOX_HEADS // 2):
        parts += [w[:, RET_W + part * 512 + p * BLK:RET_W + part * 512 + (p + 1) * BLK] for part in range(3)]
    return jnp.concatenate(parts, axis=1)


def _reference_order(g_main, g_ff):
    parts = [g_main[:, 0:RET_W]]
    for part in range(3):
        parts += [g_main[:, RET_W + 384 * p + part * BLK:RET_W + 384 * p + (part + 1) * BLK] for p in range(FOX_HEADS // 2)]
    return jnp.concatenate(parts + [g_ff], axis=1)


def kernel(x, meta_tokens, attn_norm_g, w_in, fox_forget_b, ret_norm_g, w_out, ffn_norm_g, w_up, conv_w, conv_b, w_down, final_norm_g, loss_target, m_meta_tokens, m_attn_norm_g, m_w_in, m_fox_forget_b, m_ret_norm_g, m_w_out, m_ffn_norm_g, m_w_up, m_conv_w, m_conv_b, m_w_down, m_final_norm_g, v_meta_tokens, v_attn_norm_g, v_w_in, v_fox_forget_b, v_ret_norm_g, v_w_out, v_ffn_norm_g, v_w_up, v_conv_w, v_conv_b, v_w_down, v_final_norm_g):
    chip = 2 * lax.axis_index("x") + lax.axis_index("y")
    core = lax.axis_index("c")
    meta_w, conv_sw = meta_tokens.shape[1], conv_w.shape[2]

    small_w = _pack_rows([meta_tokens, conv_w[0]], 8)
    w_in_b = w_in[0].astype(BF16)
    g_in, g_small = _chip_allgather_halves(w_in_b, small_w)
    g_in = lax.dynamic_update_slice(g_in.reshape((N_CHIPS,) + w_in_b.shape), w_in_b[None], (chip, 0, 0))
    w_in_full = g_in.transpose(1, 0, 2).reshape(D_MODEL, IN_WIDTH)
    w_main = _kernel_order(w_in_full)
    w_ff = jnp.pad(w_in_full[:, MAIN_W:], ((0, 0), (0, BLK - FOX_HEADS)))
    small_parts = [_unpack_rows(g_small[j], [meta_tokens.shape, conv_w.shape[1:]]) for j in range(N_CHIPS)]
    meta_full = jnp.concatenate([sp[0] for sp in small_parts], axis=1)
    conv_w_full = jnp.concatenate([sp[1] for sp in small_parts], axis=1)

    core_idx = core.reshape(1).astype(jnp.int32)
    place = jnp.stack([chip, core]).astype(jnp.int32)

    def assemble(gathered):
        g_out, g_up, g_down = gathered
        return g_out.reshape(D_MODEL, D_MODEL), g_up, g_down.reshape(D_FF, D_MODEL)

    def early_arrays(d_w_out, d_w_up, d_w_down):
        return [d_w_out.reshape(N_CHIPS, -1, D_MODEL), d_w_up, d_w_down.reshape(N_CHIPS, -1, D_MODEL)]

    def in_sums(d_w_main, d_w_ff):
        g_in_full = _reference_order(d_w_main, d_w_ff).reshape(D_MODEL, N_CHIPS, -1).transpose(1, 0, 2)
        (from_sib,) = _sibling_halves([g_in_full])
        return [_pair_add(g_in_full, from_sib, core_idx, "pair_add_in")]

    def early_sums(early, from_sib):
        return [_pair_add(g, r, core_idx, "pair_add_" + nm) for g, r, nm in zip(early, from_sib, ("out", "up", "down"))]

    out = _local_step(x[0], loss_target[0], meta_full, attn_norm_g, w_main, w_ff, fox_forget_b, ret_norm_g,
                      None, ffn_norm_g, None, conv_w_full, conv_b, None, final_norm_g[None],
                      late=([w_out[0].astype(BF16), w_up[0].astype(BF16), w_down[0].astype(BF16)], assemble),
                      mid=(early_arrays, early_sums), last=in_sums)

    small_shapes = [(1, D_MODEL), (1, D_MODEL), (1, D_MODEL), (1, 512 + FOX_HEADS + 1), (1, D_FF), (N_META, D_MODEL), (3, D_FF)]
    small = _pack_rows([out["attn_g"], out["ffn_g"], out["final_g"],
                        jnp.concatenate([out["ret_g"], out["fox_b"], out["loss"].reshape(1, 1)], axis=1),
                        out["conv_b"], out["dmeta"], out["conv_w"]], 32)
    names = ("in", "out", "up", "down")
    totals = [_sum_partials(s, q, place, "sum_chips_" + nm) for s, q, nm in zip(out["scatter"], out["received"], names)]
    (grad_in, grad_out, grad_up, grad_down), small_all = _sibling_allgather(totals, small)
    s_attn, s_ffn, s_final, s_misc, s_conv_b, s_meta, s_conv_w = _unpack_rows(
        _sum_slots(small_all, "sum_small", tiles=1), small_shapes)
    loss = s_misc[0, 512 + FOX_HEADS]
    small_grads = [lax.dynamic_slice_in_dim(s_meta, chip * meta_w, meta_w, axis=1), s_attn, s_misc[:, 512:512 + FOX_HEADS],
                   s_misc[:, :512], s_ffn, lax.dynamic_slice_in_dim(s_conv_w, chip * conv_sw, conv_sw, axis=1)[None],
                   s_conv_b, s_final[0]]

    big_w = [(w_in, m_w_in, v_w_in, grad_in, "adamw_in"), (w_out, m_w_out, v_w_out, grad_out, "adamw_out"),
             (w_up, m_w_up, v_w_up, grad_up, "adamw_up"), (w_down, m_w_down, v_w_down, grad_down, "adamw_down")]
    big_res = [[r[None] for r in _adamw(w[0], g, m[0], v[0], nm)] for w, m, v, g, nm in big_w]
    small_w_list = [meta_tokens, attn_norm_g, fox_forget_b, ret_norm_g, ffn_norm_g, conv_w, conv_b, final_norm_g]
    small_m = [m_meta_tokens, m_attn_norm_g, m_fox_forget_b, m_ret_norm_g, m_ffn_norm_g, m_conv_w, m_conv_b, m_final_norm_g]
    small_v = [v_meta_tokens, v_attn_norm_g, v_fox_forget_b, v_ret_norm_g, v_ffn_norm_g, v_conv_w, v_conv_b, v_final_norm_g]
    shapes = [a.shape for a in small_w_list]
    packs = [_pack_rows(lst, 16) for lst in (small_w_list, small_grads, small_m, small_v)]
    small_res = [_unpack_rows(r, shapes) for r in _adamw(*packs, "adamw_small", tiles=1)[1:]]
    small_grads = [g.reshape(s) for g, s in zip(small_grads, shapes)]

    def ordered(kind):
        sm = small_grads if kind == 0 else small_res[kind - 1]
        bg = [r[kind] for r in big_res]
        return [sm[0], sm[1], bg[0], sm[2], sm[3], bg[1], sm[4], bg[2], sm[5], sm[6], bg[3], sm[7]]

    return (loss, out["dx"][None], *ordered(0), *ordered(1), *ordered(2), *ordered(3))
```

```python
import functools

import numpy as np
import jax
import jax.numpy as jnp
from jax import lax
from jax.experimental import pallas as pl
from jax.experimental.pallas import tpu as pltpu

F32 = jnp.float32
BF16 = jnp.bfloat16

D_MODEL = 1024
N_META = 16
BLK = 128
UNIT = 2 * BLK
FOX_PAIRS = 2
WIDE = 4
CHUNK = 64
N_PAD = BLK - N_META
PREFIX = BLK
RET_HEADS = 4
FOX_HEADS = 8
HEAD_LANES = 64
D_FF = 2816
ROPE_BASE = 10000.0
EPS = 1e-6
NEG = -1e30
LOG2E = 1.4426950408889634
RET_W = 1536
FOX_W = 1536
MAIN_W = RET_W + FOX_W
IN_WIDTH = MAIN_W + FOX_HEADS
N_CHIPS = 4
N_DEV = 8

ADAM_LR = 0.001
ADAM_B1 = 0.9
ADAM_B2 = 0.999
ADAM_EPS = 1e-08
ADAM_WD = 0.01
ADAM_STEP = 10

MESH = pl.DeviceIdType.MESH
VMEM_LIMIT_MB = 56

_NT = (((1,), (1,)), ((), ()))
_TN = (((0,), (0,)), ((), ()))


def _dot(a, b):
    return jnp.dot(a, b, preferred_element_type=F32)


def _dot_nt(a, b):
    return lax.dot_general(a, b, _NT, preferred_element_type=F32)


def _dot_tn(a, b):
    return lax.dot_general(a, b, _TN, preferred_element_type=F32)


def _params(dims=None, vmem_mb=VMEM_LIMIT_MB):
    kw = dict(vmem_limit_bytes=vmem_mb << 20)
    if dims is not None:
        kw["dimension_semantics"] = dims
    return pltpu.CompilerParams(**kw)


def _row_tile(n, prefs=(384, 256, 128)):
    for t in prefs:
        if n % t == 0:
            return t
    raise ValueError(f"no row tile for {n}")


def _iota(shape, dim):
    return lax.broadcasted_iota(jnp.int32, shape, dim)


def _pick_row(tile, row):
    sub = _iota(tile.shape, 0)
    return jnp.sum(jnp.where(sub == row, tile, 0.0), axis=0, keepdims=True)


def _split3(x):
    hi = x.astype(BF16)
    r1 = x - hi.astype(F32)
    mid = r1.astype(BF16)
    lo = (r1 - mid.astype(F32)).astype(BF16)
    return hi, mid, lo


def _full(shape):
    nd = len(shape)
    return pl.BlockSpec(shape, lambda *_: (0,) * nd)


def _in_perm():
    cols = list(range(RET_W))
    for p in range(FOX_HEADS // 2):
        for part in range(3):
            start = RET_W + part * 512 + p * BLK
            cols += list(range(start, start + BLK))
    return np.asarray(cols, np.int32)


def _rotary_tables(L):
    half = HEAD_LANES // 2
    inv = 1.0 / (ROPE_BASE ** (jnp.arange(half, dtype=F32) / half))
    ang = jnp.arange(L).astype(F32)[:, None] * inv[None, :]
    cos, sin = jnp.cos(ang), jnp.sin(ang)
    cos_t = jnp.tile(cos, (1, 4))
    sin_t = jnp.tile(jnp.concatenate([-sin, sin], axis=1), (1, 2))
    return cos_t, sin_t


def _decay_tables():
    gam = 1.0 - 2.0 ** (-5.0 - np.arange(RET_HEADS, dtype=np.float64))
    n = np.arange(BLK)
    same_or_past = (n[:, None] // CHUNK) >= (n[None, :] // CHUNK)
    dist = np.abs(n[:, None] - n[None, :])
    dmat = np.stack([np.where(same_or_past, g ** dist, 0.0) for g in gam]).astype(np.float32)
    lane_head = np.arange(BLK) // HEAD_LANES
    wq = np.stack([gam[2 * p + lane_head][None, :] ** (n[:, None] + 1.0) for p in range(2)]).astype(np.float32)
    wk = np.stack([gam[2 * p + lane_head][None, :] ** (BLK - 1.0 - n[:, None]) for p in range(2)]).astype(np.float32)
    g_blk = tuple(float(g ** BLK) for g in gam)
    return jnp.asarray(dmat), jnp.asarray(wq), jnp.asarray(wk), g_blk


def _shifted_blocks(tm):
    nb = tm // BLK
    return [pl.BlockSpec((BLK, D_MODEL), lambda i, j=j: (jnp.maximum(nb * i + j - 1, 0), 0)) for j in range(nb)]


def _rms_inproj(head, x, g, w_main, w_ff):
    L = x.shape[0] + BLK
    tm = _row_tile(L)
    nb = tm // BLK

    def body(head_ref, *refs):
        x_refs, (g_ref, wm_ref, wf_ref, h_ref, n_ref, p_ref, ff_ref) = refs[:nb], refs[nb:]
        parts = [r[...] for r in x_refs]
        parts[0] = jnp.where(pl.program_id(0) == 0, head_ref[...], parts[0])
        h = jnp.concatenate(parts, axis=0)
        h_ref[...] = h
        r = lax.rsqrt(jnp.mean(h * h, axis=-1, keepdims=True) + EPS)
        n = (h * r * g_ref[...]).astype(BF16)
        n_ref[...] = n
        p_ref[...] = _dot(n, wm_ref[...]).astype(BF16)
        ff_ref[...] = _dot(n, wf_ref[...])

    rows = lambda w: pl.BlockSpec((tm, w), lambda i: (i, 0))
    return pl.pallas_call(
        body, name="f_inproj", grid=(L // tm,),
        in_specs=[_full((BLK, D_MODEL))] + _shifted_blocks(tm)
        + [_full((1, D_MODEL)), _full((D_MODEL, MAIN_W)), _full((D_MODEL, BLK))],
        out_specs=[rows(D_MODEL), rows(D_MODEL), rows(MAIN_W), rows(BLK)],
        out_shape=[jax.ShapeDtypeStruct((L, D_MODEL), F32), jax.ShapeDtypeStruct((L, D_MODEL), BF16),
                   jax.ShapeDtypeStruct((L, MAIN_W), BF16), jax.ShapeDtypeStruct((L, BLK), F32)],
        compiler_params=_params(("parallel",)),
    )(head, *([x] * nb), g, w_main, w_ff)


def _block_group(nblk):
    return 3 if nblk % 3 == 0 else 1


def _fox_prep(ff, fb):
    L = ff.shape[0]
    nblk = L // BLK
    G = _block_group(nblk)

    def body(ff_ref, b_ref, c_ref, ct_ref, carry):
        @pl.when(pl.program_id(0) == 0)
        def _():
            carry[...] = jnp.zeros_like(carry)

        tri = (_iota((BLK, BLK), 0) >= _iota((BLK, BLK), 1)).astype(BF16)
        live = _iota((BLK, BLK), 1) < FOX_HEADS
        run = carry[...]
        for b in range(G):
            z = ff_ref[b * BLK:(b + 1) * BLK, :] + b_ref[...]
            lf = jnp.where(live, jnp.minimum(z, 0.0) - jnp.log1p(jnp.exp(-jnp.abs(z))), 0.0)
            hi, mid, lo = _split3(lf)
            cs = (_dot(tri, hi) + _dot(tri, mid) + _dot(tri, lo) + run) * LOG2E
            c_ref[b * BLK:(b + 1) * BLK, :] = cs
            ct_ref[b] = cs.T[0:8, :]
            run = run + jnp.sum(lf, axis=0, keepdims=True)
        carry[...] = run

    return pl.pallas_call(
        body, name="f_foxprep", grid=(nblk // G,),
        in_specs=[pl.BlockSpec((G * BLK, BLK), lambda i: (i, 0)), _full((1, BLK))],
        out_specs=[pl.BlockSpec((G * BLK, BLK), lambda i: (i, 0)), pl.BlockSpec((G, 8, BLK), lambda i: (i, 0, 0))],
        out_shape=[jax.ShapeDtypeStruct((L, BLK), F32), jax.ShapeDtypeStruct((nblk, 8, BLK), F32)],
        scratch_shapes=[pltpu.VMEM((1, BLK), F32)],
        compiler_params=_params(("arbitrary",)),
    )(ff, fb)


def _rot_fns(cos, sin):
    lane = _iota((BLK, BLK), 1)
    first = (lane & (HEAD_LANES - 1)) < HEAD_LANES // 2

    def swap(x):
        return jnp.where(first, pltpu.roll(x, BLK - 32, 1), pltpu.roll(x, 32, 1))

    def rot(x):
        return x * cos + swap(x) * sin

    def rot_t(dy):
        return dy * cos + swap(dy * sin)

    return rot, rot_t


def _retention_fwd(proj, cos_t, sin_t, ret_g):
    L = proj.shape[0]
    nblk = L // BLK
    G = _block_group(nblk)
    dmat, wq_t, wk_t, g_blk = _decay_tables()

    def body(q_ref, k_ref, v_ref, gate_ref, cos_ref, sin_ref, d_ref, wq_ref, wk_ref, rg_ref,
             mix_ref, o_ref, rs_ref, state):
        @pl.when(pl.program_id(0) == 0)
        def _():
            state[...] = jnp.zeros_like(state)

        lane = _iota((BLK, BLK), 1)
        sub = _iota((BLK, BLK), 0)
        for b in range(G):
            rows = slice(b * BLK, (b + 1) * BLK)
            rot, _ = _rot_fns(cos_ref[rows, :], sin_ref[rows, :])
            for p in range(2):
                qr = rot(q_ref[rows, p * BLK:(p + 1) * BLK].astype(F32))
                kr = rot(k_ref[rows, p * BLK:(p + 1) * BLK].astype(F32)) * (HEAD_LANES ** -0.5)
                kr_b = kr.astype(BF16)
                qw = (qr * wq_ref[p]).astype(BF16)
                kw = (kr * wk_ref[p]).astype(BF16)
                for e in range(2):
                    h = 2 * p + e
                    cols = slice(h * BLK, (h + 1) * BLK)
                    qm = jnp.where((lane >> 6) == e, qr, 0.0).astype(BF16)
                    s = _dot_nt(qm, kr_b) * d_ref[h]
                    vh = v_ref[rows, cols]
                    st = state[h]
                    rs_ref[b, h] = st
                    o = _dot(s.astype(BF16), vh) + _dot(qw, st.astype(BF16))
                    u = jnp.where((sub >> 6) == e, _dot_tn(kw, vh), 0.0)
                    state[h] = g_blk[h] * st + u
                    rn = lax.rsqrt(jnp.mean(o * o, axis=-1, keepdims=True) + EPS)
                    gate = gate_ref[rows, cols].astype(F32)
                    o_ref[rows, cols] = o
                    mix_ref[rows, cols] = (o * rn * rg_ref[:, cols] * (gate * jax.nn.sigmoid(gate))).astype(BF16)

    row = lambda c: (lambda i: (i, c))
    return pl.pallas_call(
        body, name="f_retention", grid=(nblk // G,),
        in_specs=[pl.BlockSpec((G * BLK, 256), row(0)), pl.BlockSpec((G * BLK, 256), row(1)),
                  pl.BlockSpec((G * BLK, 512), row(1)), pl.BlockSpec((G * BLK, 512), row(2)),
                  pl.BlockSpec((G * BLK, BLK), row(0)), pl.BlockSpec((G * BLK, BLK), row(0)),
                  _full((RET_HEADS, BLK, BLK)), _full((2, BLK, BLK)), _full((2, BLK, BLK)), _full((1, 512))],
        out_specs=[pl.BlockSpec((G * BLK, 512), row(0)), pl.BlockSpec((G * BLK, 512), row(0)),
                   pl.BlockSpec((G, RET_HEADS, BLK, BLK), lambda i: (i, 0, 0, 0))],
        out_shape=[jax.ShapeDtypeStruct((L, 512), BF16), jax.ShapeDtypeStruct((L, 512), F32),
                   jax.ShapeDtypeStruct((nblk, RET_HEADS, BLK, BLK), F32)],
        scratch_shapes=[pltpu.VMEM((RET_HEADS, BLK, BLK), F32)],
        compiler_params=_params(("arbitrary",)),
    )(proj, proj, proj, proj, cos_t, sin_t, dmat, wq_t, wk_t, ret_g)


def _fox_units(L):
    nblk = L // BLK
    assert L % BLK == 0 and nblk % 2 == 1, "sequence must be one 128-row block plus whole 256-row tiles"
    return nblk, (nblk - 1) // 2


def _fox_tile_masks():
    sub, lane = _iota((BLK, BLK), 0), _iota((BLK, BLK), 1)
    valid = _iota((BLK, UNIT), 0) >= N_PAD
    diag = _iota((UNIT, UNIT), 0) <= _iota((UNIT, UNIT), 1)
    r, q = _iota((BLK + UNIT, UNIT), 0), _iota((BLK + UNIT, UNIT), 1)
    first_and_diag = ((r < BLK) & (r >= N_PAD)) | ((r >= BLK) & (r - BLK <= q))
    return dict(first=(sub <= lane) & (sub >= N_PAD), valid=valid, diag=diag, first_and_diag=first_and_diag)


def _fox_fwd(proj, c, ctb, gather=()):
    L = proj.shape[0]
    nblk, nu = _fox_units(L)
    scale = HEAD_LANES ** -0.5 * LOG2E
    ng = len(gather)
    steps = FOX_HEADS // (2 * FOX_PAIRS)

    def body(qkv_ref, c_ref, ct_ref, *rest):
        g_in, (of_ref, lse_ref), g_out = rest[:ng], rest[ng:ng + 2], rest[ng + 2:2 * ng + 2]
        vt, csb = rest[2 * ng + 2:2 * ng + 4]
        p = pl.program_id(0)
        heads = [(pp, e, 2 * FOX_PAIRS * p + 2 * pp + e) for pp in range(FOX_PAIRS) for e in range(2)]

        @pl.when(p == 0)
        def _():
            lse_ref[...] = jnp.zeros_like(lse_ref)
            if ng:
                local, sends, _ = _allgather_copies(g_in, g_out, *rest[2 * ng + 4:])
                for cp in local + sends:
                    cp.start()

        lane = _iota((BLK, BLK), 1)
        sub8 = _iota((8, BLK), 0)
        masks = _fox_tile_masks()

        def pre(j, carry):
            off = pl.multiple_of(j * BLK, BLK)
            ct = c_ref[pl.ds(off, BLK), :]
            for pp in range(FOX_PAIRS):
                vt[pp, j] = qkv_ref[pl.ds(off, BLK), pp * 384 + 2 * BLK:pp * 384 + 3 * BLK].astype(F32).T.astype(BF16)
            for hh, (_, _, h) in enumerate(heads):
                col = jnp.sum(jnp.where(lane == h, ct, 0.0), axis=1, keepdims=True)
                csb[hh, j] = jnp.broadcast_to(col, (BLK, BLK))
            return carry

        lax.fori_loop(0, nblk, pre, 0)

        def attend(qblk, nq, n_whole):
            qlen = nq * BLK
            qoff = pl.multiple_of(qblk * BLK, BLK)
            qlane = _iota((qlen, BLK), 1)
            qs = [qkv_ref[pl.ds(qoff, qlen), pp * 384:pp * 384 + BLK].astype(F32) * scale for pp in range(FOX_PAIRS)]
            qm = [jnp.where((qlane >> 6) == e, qs[pp], 0.0).astype(BF16) for pp, e, _ in heads]
            ct_row = [jnp.concatenate([_pick_row(ct_ref[qblk + a], h) for a in range(nq)], axis=1) for _, _, h in heads]

            def step(segs, mask, st):
                blocks = [kblk + b for kblk, nk in segs for b in range(nk)]
                kts = []
                for pp in range(FOX_PAIRS):
                    kt = [qkv_ref[pl.ds(pl.multiple_of(kblk * BLK, BLK), nk * BLK), pp * 384 + BLK:pp * 384 + 2 * BLK]
                          for kblk, nk in segs]
                    kts.append(kt[0] if len(kt) == 1 else jnp.concatenate(kt, axis=0))
                out = []
                for hh, (pp, e, _) in enumerate(heads):
                    m, l, acc = st[3 * hh:3 * hh + 3]
                    s = _dot_nt(kts[pp], qm[hh])
                    t = jnp.concatenate([s[b * BLK:(b + 1) * BLK] - jnp.concatenate([csb[hh, blk]] * nq, axis=1)
                                         for b, blk in enumerate(blocks)], axis=0)
                    if mask is not None:
                        t = jnp.where(mask, t, NEG)
                    m_new = jnp.maximum(m, jnp.max(t, axis=0, keepdims=True) + ct_row[hh])
                    alpha = jnp.exp2(m - m_new)
                    pr = jnp.exp2(t - (m_new - ct_row[hh]))
                    l = alpha * l + jnp.sum(pr, axis=0, keepdims=True)
                    pr_b = pr.astype(BF16)
                    pv = None
                    for b, blk in enumerate(blocks):
                        part = _dot(vt[pp, blk, e * HEAD_LANES:(e + 1) * HEAD_LANES, :], pr_b[b * BLK:(b + 1) * BLK])
                        pv = part if pv is None else pv + part
                    out += [m_new, l, alpha * acc + pv]
                return tuple(out)

            st = (jnp.full((1, qlen), NEG, F32), jnp.zeros((1, qlen), F32),
                  jnp.zeros((HEAD_LANES, qlen), F32)) * len(heads)
            if nq == 1:
                st = step([(0, 1)], masks["first"], st)
            else:
                st = step([(0, 1), (qblk, 2)], masks["first_and_diag"], st)
                n_wide = n_whole // WIDE
                st = lax.fori_loop(0, n_wide, lambda j, s_: step([(1 + 2 * WIDE * j, 2 * WIDE)], None, s_), st)
                rest = 1 + 2 * WIDE * n_wide
                st = lax.cond((n_whole & 2) != 0, lambda s_: step([(rest, 4)], None, s_), lambda s_: s_, st)
                st = lax.cond((n_whole & 1) != 0, lambda s_: step([(rest + 2 * (n_whole & 2), 2)], None, s_),
                              lambda s_: s_, st)
            for pp in range(FOX_PAIRS):
                lo, hi = st[6 * pp:6 * pp + 3], st[6 * pp + 3:6 * pp + 6]
                o_t = jnp.concatenate([lo[2] * (1.0 / lo[1]), hi[2] * (1.0 / hi[1])], axis=0)
                of_ref[pl.ds(qoff, qlen), pp * BLK:(pp + 1) * BLK] = o_t.T.astype(BF16)
            lse = [st[3 * hh] + jnp.log(st[3 * hh + 1]) * LOG2E for hh in range(len(heads))]
            for a in range(nq):
                upd = jnp.zeros((8, BLK), F32)
                for hh, (_, _, h) in enumerate(heads):
                    upd = upd + jnp.where(sub8 == h, lse[hh][:, a * BLK:(a + 1) * BLK], 0.0)
                lse_ref[qblk + a] = lse_ref[qblk + a] + upd

        attend(0, 1, 0)

        def q_loop(u, carry):
            attend(1 + 2 * u, 2, u)
            return carry

        lax.fori_loop(0, nu, q_loop, 0)

        if ng:
            @pl.when(p == steps - 1)
            def _():
                local, sends, recvs = _allgather_copies(g_in, g_out, *rest[2 * ng + 4:])
                for cp in recvs:
                    cp.wait_recv()
                for cp in sends:
                    cp.wait_send()
                for cp in local:
                    cp.wait()

    width = 384 * FOX_PAIRS
    return pl.pallas_call(
        body, name="f_fox", grid=(steps,),
        in_specs=[pl.BlockSpec((L, width), lambda p: (0, RET_W // width + p)), _full((L, BLK)), _full((nblk, 8, BLK))]
        + [_ANY] * ng,
        out_specs=[pl.BlockSpec((L, FOX_PAIRS * BLK), lambda p: (0, p)), _full((nblk, 8, BLK))] + [_ANY] * ng,
        out_shape=[jax.ShapeDtypeStruct((L, 512), BF16), jax.ShapeDtypeStruct((nblk, 8, BLK), F32)]
        + [jax.ShapeDtypeStruct((N_CHIPS,) + a.shape, a.dtype) for a in gather],
        scratch_shapes=[pltpu.VMEM((FOX_PAIRS, nblk, BLK, BLK), BF16), pltpu.VMEM((2 * FOX_PAIRS, nblk, BLK, BLK), F32)]
        + _allgather_semaphores(ng),
        compiler_params=_params(("arbitrary",)),
    )(proj, c, ctb, *gather)


def _outproj_up(mix_r, o_f, h0, w_out, ffn_g, w_up, conv_w, conv_b):
    L = h0.shape[0]
    tm = _row_tile(L)
    shard = w_up.shape[2]
    assert 2 * shard == D_FF
    cw = [conv_w[j:j + 1] for j in range(3)]
    resident = lambda shape: pl.BlockSpec(shape, lambda i: (0,) * len(shape), pipeline_mode=pl.Buffered(1))

    def body(mr_ref, of_ref, h0_ref, wo_ref, g_ref, wu_ref, cw0, cw1, cw2, cb_ref,
             h1_ref, n2_ref, up_ref, act_ref, acc_ref, halo):
        i = pl.program_id(0)

        @pl.when(i == 0)
        def _():
            halo[...] = jnp.zeros_like(halo)

        h1 = h0_ref[...] + _dot(mr_ref[...], wo_ref[0:512, :]) + _dot(of_ref[...], wo_ref[512:1024, :])
        h1_ref[...] = h1
        r = lax.rsqrt(jnp.mean(h1 * h1, axis=-1, keepdims=True) + EPS)
        n2 = (h1 * r * g_ref[...]).astype(BF16)
        n2_ref[...] = n2
        live = i * tm + _iota((tm, 1), 0) >= N_PAD
        for half in range(2):
            cols = slice(half * shard, (half + 1) * shard)
            a_b = _dot(n2, wu_ref[half]).astype(BF16)
            b_b = _dot(n2, wu_ref[2 + half]).astype(BF16)
            up_ref[:, cols] = a_b
            up_ref[:, D_FF + half * shard:D_FF + (half + 1) * shard] = b_b
            a = jnp.where(live, a_b.astype(F32), 0.0)
            _, _, acc = _conv_taps(a, halo[:, cols], [cw0[:, cols], cw1[:, cols], cw2[:, cols]], cb_ref[:, cols])
            act_ref[:, cols] = (acc * jax.nn.sigmoid(acc) * b_b.astype(F32)).astype(BF16)
            acc_ref[:, cols] = acc.astype(BF16)
            halo[:, cols] = a[tm - 8:tm, :]

    rows = lambda w: pl.BlockSpec((tm, w), lambda i: (i, 0))
    return pl.pallas_call(
        body, name="f_outproj_up", grid=(L // tm,),
        in_specs=[rows(512), rows(512), rows(D_MODEL), resident((D_MODEL, D_MODEL)), _full((1, D_MODEL)),
                  resident((N_CHIPS, D_MODEL, shard)), _full((1, D_FF)), _full((1, D_FF)), _full((1, D_FF)),
                  _full((1, D_FF))],
        out_specs=[rows(D_MODEL), rows(D_MODEL), rows(2 * D_FF), rows(D_FF), rows(D_FF)],
        out_shape=[jax.ShapeDtypeStruct((L, D_MODEL), F32), jax.ShapeDtypeStruct((L, D_MODEL), BF16),
                   jax.ShapeDtypeStruct((L, 2 * D_FF), BF16), jax.ShapeDtypeStruct((L, D_FF), BF16),
                   jax.ShapeDtypeStruct((L, D_FF), BF16)],
        scratch_shapes=[pltpu.VMEM((8, D_FF), F32)],
        compiler_params=_params(("arbitrary",)),
    )(mix_r, o_f, h0, w_out, ffn_g, w_up, cw[0], cw[1], cw[2], conv_b)


def _conv_taps(a, halo, cw, cb):
    sub = _iota((a.shape[0], 1), 0)
    a1 = jnp.where(sub == 0, _pick_row(halo, 7), pltpu.roll(a, 1, 0))
    a2 = jnp.where(sub == 0, _pick_row(halo, 6), jnp.where(sub == 1, _pick_row(halo, 7), pltpu.roll(a, 2, 0)))
    acc = cb + a2 * cw[0]
    acc = acc + a1 * cw[1]
    acc = acc + a * cw[2]
    return a1, a2, acc


def _ffn_down_loss(g_act, w_down, h1, final_g, target):
    L = h1.shape[0]
    tm = _row_tile(L)
    nb = tm // BLK

    def body(g_ref, wd_ref, h1_ref, gf_ref, *refs):
        t_refs, (dh_ref, dhb_ref, dgf_ref, loss_ref) = refs[:nb], refs[nb:]
        i = pl.program_id(0)

        @pl.when(i == 0)
        def _():
            dgf_ref[...] = jnp.zeros_like(dgf_ref)
            loss_ref[...] = jnp.zeros_like(loss_ref)

        h2 = h1_ref[...] + _dot(g_ref[...], wd_ref[...])
        r = lax.rsqrt(jnp.mean(h2 * h2, axis=-1, keepdims=True) + EPS)
        yn = h2 * r
        gf = gf_ref[...]
        live = i * tm + _iota((tm, 1), 0) >= PREFIX
        target = jnp.concatenate([t[...] for t in t_refs], axis=0)
        err = jnp.where(live, yn * gf - target, 0.0)
        loss_ref[...] = loss_ref[...] + 0.5 * jnp.sum(jnp.mean(err * err, axis=-1, keepdims=True))
        dy = err * (1.0 / D_MODEL)
        dgf_ref[...] = dgf_ref[...] + jnp.sum(dy * yn, axis=0, keepdims=True)
        dyn = dy * gf
        dh = r * (dyn - yn * jnp.mean(dyn * yn, axis=-1, keepdims=True))
        dh_ref[...] = dh
        dhb_ref[...] = dh.astype(BF16)

    rows = lambda w: pl.BlockSpec((tm, w), lambda i: (i, 0))
    return pl.pallas_call(
        body, name="f_ffn_down_loss", grid=(L // tm,),
        in_specs=[rows(D_FF), _full((D_FF, D_MODEL)), rows(D_MODEL), _full((1, D_MODEL))] + _shifted_blocks(tm),
        out_specs=[rows(D_MODEL), rows(D_MODEL), _full((1, D_MODEL)), _full((1, BLK))],
        out_shape=[jax.ShapeDtypeStruct((L, D_MODEL), F32), jax.ShapeDtypeStruct((L, D_MODEL), BF16),
                   jax.ShapeDtypeStruct((1, D_MODEL), F32), jax.ShapeDtypeStruct((1, BLK), F32)],
        compiler_params=_params(("arbitrary",)),
    )(g_act, w_down, h1, final_g, *([target] * nb))


def _ffn_bwd_gate(dh2b, w_down, acc_saved, up):
    L = dh2b.shape[0]
    tm = _row_tile(L)

    def body(dh_ref, wd_ref, acc_ref, b_ref, dacc_ref, db_ref):
        acc = acc_ref[...].astype(F32)
        dg = _dot_nt(dh_ref[...], wd_ref[...])
        sg = jax.nn.sigmoid(acc)
        silu = acc * sg
        db_ref[...] = (dg * silu).astype(BF16)
        dacc_ref[...] = (dg * b_ref[...].astype(F32) * (sg + silu * (1.0 - sg))).astype(BF16)

    rows = lambda w, c=0: pl.BlockSpec((tm, w), lambda i: (i, c))
    return pl.pallas_call(
        body, name="b_ffn_gate", grid=(L // tm,),
        in_specs=[rows(D_MODEL), _full((D_FF, D_MODEL)), rows(D_FF), rows(D_FF, 1)],
        out_specs=[rows(D_FF), rows(D_FF)],
        out_shape=[jax.ShapeDtypeStruct((L, D_FF), BF16), jax.ShapeDtypeStruct((L, D_FF), BF16)],
        compiler_params=_params(("parallel",)),
    )(dh2b, w_down, acc_saved, up)


def _ffn_bwd_up(dacc, db, up, conv_w, w_up, h1, ffn_g, dh2, w_out):
    L = h1.shape[0]
    tm = _row_tile(L)
    nt = L // tm
    shard = w_up.shape[2]
    cw = [conv_w[j:j + 1] for j in range(3)]

    def body(da_ref, halo_ref, db_ref, a_ref, cw0, cw1, cw2, wu_ref, h1_ref, g_ref, dh2_ref, wo_ref,
             dup_ref, dh1_ref, dh1b_ref, dmix_ref, dg_ref, dcw_ref):
        i = pl.program_id(0)

        @pl.when(i == 0)
        def _():
            dg_ref[...] = jnp.zeros_like(dg_ref)
            dcw_ref[...] = jnp.zeros_like(dcw_ref)

        sub = _iota((tm, 1), 0)
        sub8 = _iota((8, 1), 0)
        last_tile = i == nt - 1
        dbv = db_ref[...]
        dup_ref[:, D_FF:2 * D_FF] = dbv
        dn = _dot_nt(dbv[:, 0:shard], wu_ref[2]) + _dot_nt(dbv[:, shard:2 * shard], wu_ref[3])
        for half in range(2):
            cols = slice(half * shard, (half + 1) * shard)
            d0 = da_ref[:, cols].astype(F32)
            halo = jnp.where(last_tile, 0.0, halo_ref[:, cols].astype(F32))
            d1 = jnp.where(sub == tm - 1, _pick_row(halo, 0), pltpu.roll(d0, tm - 1, 0))
            d2 = jnp.where(sub == tm - 2, _pick_row(halo, 0),
                           jnp.where(sub == tm - 1, _pick_row(halo, 1), pltpu.roll(d0, tm - 2, 0)))
            a = a_ref[:, cols].astype(F32)
            upd = jnp.zeros((8, shard), F32)
            for j, t in enumerate((d2 * a, d1 * a, d0 * a, d0)):
                upd = upd + jnp.where(sub8 == j, jnp.sum(t, axis=0, keepdims=True), 0.0)
            dcw_ref[:, cols] = dcw_ref[:, cols] + upd
            da = (d0 * cw2[:, cols] + d1 * cw1[:, cols] + d2 * cw0[:, cols]).astype(BF16)
            dup_ref[:, cols] = da
            dn = dn + _dot_nt(da, wu_ref[half])
        h1 = h1_ref[...]
        r = lax.rsqrt(jnp.mean(h1 * h1, axis=-1, keepdims=True) + EPS)
        yn = h1 * r
        dg_ref[...] = dg_ref[...] + jnp.sum(dn * yn, axis=0, keepdims=True)
        dyn = dn * g_ref[...]
        dh1 = dh2_ref[...] + r * (dyn - yn * jnp.mean(dyn * yn, axis=-1, keepdims=True))
        dh1_ref[...] = dh1
        dh1b = dh1.astype(BF16)
        dh1b_ref[...] = dh1b
        dmix_ref[...] = _dot_nt(dh1b, wo_ref[...]).astype(BF16)

    rows = lambda w: pl.BlockSpec((tm, w), lambda i: (i, 0))
    halo = pl.BlockSpec((8, D_FF), lambda i: (jnp.minimum((i + 1) * (tm // 8), L // 8 - 1), 0))
    return pl.pallas_call(
        body, name="b_ffn_up", grid=(nt,),
        in_specs=[rows(D_FF), halo, rows(D_FF), rows(D_FF), _full((1, D_FF)), _full((1, D_FF)), _full((1, D_FF)),
                  _full((N_CHIPS, D_MODEL, shard)), rows(D_MODEL), _full((1, D_MODEL)), rows(D_MODEL),
                  _full((D_MODEL, D_MODEL))],
        out_specs=[rows(2 * D_FF), rows(D_MODEL), rows(D_MODEL), rows(D_MODEL), _full((1, D_MODEL)),
                   _full((8, D_FF))],
        out_shape=[jax.ShapeDtypeStruct((L, 2 * D_FF), BF16), jax.ShapeDtypeStruct((L, D_MODEL), F32),
                   jax.ShapeDtypeStruct((L, D_MODEL), BF16), jax.ShapeDtypeStruct((L, D_MODEL), BF16),
                   jax.ShapeDtypeStruct((1, D_MODEL), F32), jax.ShapeDtypeStruct((8, D_FF), F32)],
        compiler_params=_params(("arbitrary",)),
    )(dacc, dacc, db, up, cw[0], cw[1], cw[2], w_up, h1, ffn_g, dh2, w_out)


def _wgrad(a, b, name, tn=None, tk=None):
    L, K = a.shape
    N = b.shape[1]
    tn = N if tn is None else tn
    tk = K if tk is None else tk
    tl = _row_tile(L, (1408, 768, 512, 256, 128))

    def body(a_ref, b_ref, o_ref):
        @pl.when(pl.program_id(2) == 0)
        def _():
            o_ref[...] = jnp.zeros_like(o_ref)

        o_ref[0] = o_ref[0] + _dot_tn(a_ref[...], b_ref[...])

    return pl.pallas_call(
        body, name=name, grid=(N // tn, K // tk, L // tl),
        in_specs=[pl.BlockSpec((tl, tk), lambda n, k, l: (l, k)), pl.BlockSpec((tl, tn), lambda n, k, l: (l, n))],
        out_specs=pl.BlockSpec((1, tk, tn), lambda n, k, l: (n, k, 0)),
        out_shape=jax.ShapeDtypeStruct((N // tn, K, tn), F32),
        compiler_params=_params(("parallel", "parallel", "arbitrary")),
    )(a, b)


def _retention_bwd(dmix, o, proj, cos_t, sin_t, ret_g, states, exchange=()):
    L = proj.shape[0]
    nblk = L // BLK
    G = _block_group(nblk)
    steps = nblk // G
    nx = len(exchange)
    dmat, wq_t, wk_t, g_blk = _decay_tables()

    def body(dm_ref, o_ref, q_ref, k_ref, v_ref, gate_ref, cos_ref, sin_ref, d_ref, wq_ref, wk_ref, rg_ref, rs_ref,
             *rest):
        x_in, (dp_ref, drg_ref), x_out, gstate = rest[:nx], rest[nx:nx + 2], rest[nx + 2:2 * nx + 2], rest[2 * nx + 2]

        @pl.when(pl.program_id(0) == 0)
        def _():
            if nx:
                for cp in _sibling_half_copies(x_in, x_out, *rest[2 * nx + 3:])[0]:
                    cp.start()
            gstate[...] = jnp.zeros_like(gstate)
            drg_ref[...] = jnp.zeros_like(drg_ref)

        lane = _iota((BLK, BLK), 1)
        sub = _iota((BLK, BLK), 0)
        scale = HEAD_LANES ** -0.5
        for b in reversed(range(G)):
            rows = slice(b * BLK, (b + 1) * BLK)
            rot, rot_t = _rot_fns(cos_ref[rows, :], sin_ref[rows, :])
            for p in range(2):
                qr = rot(q_ref[rows, p * BLK:(p + 1) * BLK].astype(F32))
                kr = rot(k_ref[rows, p * BLK:(p + 1) * BLK].astype(F32)) * scale
                kr_b = kr.astype(BF16)
                qw = (qr * wq_ref[p]).astype(BF16)
                kw = (kr * wk_ref[p]).astype(BF16)
                dqr = jnp.zeros((BLK, BLK), F32)
                dkr = jnp.zeros((BLK, BLK), F32)
                for e in range(2):
                    h = 2 * p + e
                    cols = slice(h * BLK, (h + 1) * BLK)
                    head_lanes = (lane >> 6) == e
                    o = o_ref[rows, cols]
                    rn = lax.rsqrt(jnp.mean(o * o, axis=-1, keepdims=True) + EPS)
                    y = o * rn
                    gate = gate_ref[rows, cols].astype(F32)
                    sg = jax.nn.sigmoid(gate)
                    dm = dm_ref[rows, cols].astype(F32)
                    rgain = rg_ref[:, cols]
                    drg_ref[:, cols] = drg_ref[:, cols] + jnp.sum(dm * y * (gate * sg), axis=0, keepdims=True)
                    dp_ref[rows, 1024 + h * BLK:1024 + (h + 1) * BLK] = (
                        dm * y * rgain * (sg * (1.0 + gate * (1.0 - sg)))).astype(BF16)
                    dy = dm * rgain * (gate * sg)
                    do = (rn * (dy - y * jnp.mean(dy * y, axis=-1, keepdims=True))).astype(BF16)
                    vh = v_ref[rows, cols]
                    qm = jnp.where(head_lanes, qr, 0.0).astype(BF16)
                    dmh = d_ref[h]
                    s = (_dot_nt(qm, kr_b) * dmh).astype(BF16)
                    ds = (_dot_nt(do, vh) * dmh).astype(BF16)
                    st = rs_ref[b, h].astype(BF16)
                    gs = gstate[h]
                    gs_b = gs.astype(BF16)
                    dqr = dqr + jnp.where(head_lanes, _dot(ds, kr_b), 0.0) + _dot_nt(do, st) * wq_ref[p]
                    dkr = dkr + _dot_tn(ds, qm) + _dot_nt(vh, gs_b) * wk_ref[p]
                    dp_ref[rows, 512 + h * BLK:512 + (h + 1) * BLK] = (_dot_tn(s, do) + _dot(kw, gs_b)).astype(BF16)
                    dr = jnp.where((sub >> 6) == e, _dot_tn(qw, do), 0.0)
                    gstate[h] = dr + g_blk[h] * gs
                dp_ref[rows, p * BLK:(p + 1) * BLK] = rot_t(dqr).astype(BF16)
                dp_ref[rows, 256 + p * BLK:256 + (p + 1) * BLK] = (rot_t(dkr) * scale).astype(BF16)

        if nx:
            @pl.when(pl.program_id(0) == steps - 1)
            def _():
                sends, recvs = _sibling_half_copies(x_in, x_out, *rest[2 * nx + 3:])
                for cp in recvs:
                    cp.wait_recv()
                for cp in sends:
                    cp.wait_send()

    row = lambda c: (lambda i: (steps - 1 - i, c))
    return pl.pallas_call(
        body, name="b_retention", grid=(steps,),
        in_specs=[pl.BlockSpec((G * BLK, 512), row(0)), pl.BlockSpec((G * BLK, 512), row(0)),
                  pl.BlockSpec((G * BLK, 256), row(0)), pl.BlockSpec((G * BLK, 256), row(1)),
                  pl.BlockSpec((G * BLK, 512), row(1)), pl.BlockSpec((G * BLK, 512), row(2)),
                  pl.BlockSpec((G * BLK, BLK), row(0)), pl.BlockSpec((G * BLK, BLK), row(0)),
                  _full((RET_HEADS, BLK, BLK)), _full((2, BLK, BLK)), _full((2, BLK, BLK)), _full((1, 512)),
                  pl.BlockSpec((G, RET_HEADS, BLK, BLK), lambda i: (steps - 1 - i, 0, 0, 0))] + [_ANY] * nx,
        out_specs=[pl.BlockSpec((G * BLK, RET_W), row(0)), _full((1, 512))] + [_ANY] * nx,
        out_shape=[jax.ShapeDtypeStruct((L, RET_W), BF16), jax.ShapeDtypeStruct((1, 512), F32)]
        + _sibling_half_shapes(exchange),
        scratch_shapes=[pltpu.VMEM((RET_HEADS, BLK, BLK), F32)] + _sibling_half_semaphores(nx),
        compiler_params=_params(("arbitrary",)),
    )(dmix, o, proj, proj, proj, proj, cos_t, sin_t, dmat, wq_t, wk_t, ret_g, states, *exchange)


def _fox_delta(dmix, o_f):
    L = o_f.shape[0]
    nblk = L // BLK
    G = _block_group(nblk)

    def body(do_ref, o_ref, d_ref):
        sel = ((_iota((8, 512), 1) >> 6) == _iota((8, 512), 0)).astype(BF16)
        for b in range(G):
            rows = slice(b * BLK, (b + 1) * BLK)
            prod = do_ref[rows, :].astype(F32) * o_ref[rows, :].astype(F32)
            hi = prod.astype(BF16)
            lo = (prod - hi.astype(F32)).astype(BF16)
            d_ref[b] = _dot_nt(sel, hi) + _dot_nt(sel, lo)

    return pl.pallas_call(
        body, name="b_foxdelta", grid=(nblk // G,),
        in_specs=[pl.BlockSpec((G * BLK, 512), lambda i: (i, 1)), pl.BlockSpec((G * BLK, 512), lambda i: (i, 0))],
        out_specs=pl.BlockSpec((G, 8, BLK), lambda i: (i, 0, 0)),
        out_shape=jax.ShapeDtypeStruct((nblk, 8, BLK), F32),
        compiler_params=_params(("parallel",)),
    )(dmix, o_f)


def _fox_bwd(proj, dmix, c, ctb, lse, delta, scatter=()):
    L = proj.shape[0]
    nblk, nu = _fox_units(L)
    scale = HEAD_LANES ** -0.5
    ns = len(scatter)

    steps = FOX_HEADS // (2 * FOX_PAIRS)

    def body(qkv_ref, do_ref, c_ref, ct_ref, lse_ref, dl_ref, *rest):
        s_in, (dp_ref, dc_ref, dcq_ref), s_out = rest[:ns], rest[ns:ns + 3], rest[ns + 3:2 * ns + 3]
        ktt, dqt, dk_acc, dv_acc, dcs_acc = rest[2 * ns + 3:2 * ns + 8]
        p = pl.program_id(0)
        heads = [(pp, e, 2 * FOX_PAIRS * p + 2 * pp + e) for pp in range(FOX_PAIRS) for e in range(2)]

        @pl.when(p == 0)
        def _():
            dc_ref[...] = jnp.zeros_like(dc_ref)
            dcq_ref[...] = jnp.zeros_like(dcq_ref)
            if ns:
                for cp in _scatter_copies(s_in, s_out, *rest[2 * ns + 8:]):
                    cp.start()

        sub8 = _iota((8, BLK), 0)
        masks = _fox_tile_masks()

        def pre(j, carry):
            off = pl.multiple_of(j * BLK, BLK)
            for pp in range(FOX_PAIRS):
                ktt[pp, j] = qkv_ref[pl.ds(off, BLK), pp * 384 + BLK:pp * 384 + 2 * BLK].astype(F32).T.astype(BF16)
                dqt[pp, j] = jnp.zeros((BLK, BLK), F32)
            return carry

        lax.fori_loop(0, nblk, pre, 0)

        def kv_pass(kblk, nk, n_later):
            klen = nk * BLK
            koff = pl.multiple_of(kblk * BLK, BLK)
            kt = [qkv_ref[pl.ds(koff, klen), pp * 384 + BLK:pp * 384 + 2 * BLK] for pp in range(FOX_PAIRS)]
            vtile = [qkv_ref[pl.ds(koff, klen), pp * 384 + 2 * BLK:pp * 384 + 3 * BLK] for pp in range(FOX_PAIRS)]
            ct = c_ref[pl.ds(koff, klen), :]
            klane = _iota((klen, BLK), 1)
            cs = [jnp.broadcast_to(jnp.sum(jnp.where(klane == h, ct, 0.0), axis=1, keepdims=True), (klen, WIDE * UNIT))
                  for _, _, h in heads]
            for pp in range(FOX_PAIRS):
                dk_acc[pp, 0:klen] = jnp.zeros((klen, BLK), F32)
                dv_acc[pp, 0:klen] = jnp.zeros((klen, BLK), F32)
            for hh in range(len(heads)):
                dcs_acc[hh, 0:klen] = jnp.zeros((klen, BLK), F32)

            def tile(qblk, nq, mask):
                qlen = nq * BLK
                if mask == "valid":
                    mask = _iota((klen, qlen), 0) >= N_PAD
                qoff = pl.multiple_of(qblk * BLK, BLK)
                qlane = _iota((qlen, BLK), 1)
                qs = [qkv_ref[pl.ds(qoff, qlen), pp * 384:pp * 384 + BLK].astype(F32) * (scale * LOG2E)
                      for pp in range(FOX_PAIRS)]
                dot_ = [do_ref[pl.ds(qoff, qlen), pp * BLK:(pp + 1) * BLK] for pp in range(FOX_PAIRS)]
                stats = [[ref[qblk + a] for a in range(nq)] for ref in (ct_ref, lse_ref, dl_ref)]
                dcq = [jnp.zeros((8, BLK), F32) for _ in range(nq)]
                for hh, (pp, e, h) in enumerate(heads):
                    head = (qlane >> 6) == e
                    ct_row, lse_row, dl_row = [jnp.concatenate([_pick_row(t, h) for t in ts], axis=1) for ts in stats]
                    qm = jnp.where(head, qs[pp], 0.0).astype(BF16)
                    dom = jnp.where(head, dot_[pp], jnp.zeros_like(dot_[pp]))
                    t = _dot_nt(kt[pp], qm) - cs[hh][:, 0:qlen]
                    if mask is not None:
                        t = jnp.where(mask, t, NEG)
                    pr = jnp.exp2(t + (ct_row - lse_row))
                    dv_acc[pp, 0:klen] = dv_acc[pp, 0:klen] + _dot(pr.astype(BF16), dom)
                    dsv = pr * (_dot_nt(vtile[pp], dom) - dl_row)
                    ds_b = dsv.astype(BF16)
                    dk_acc[pp, 0:klen] = dk_acc[pp, 0:klen] + _dot(ds_b, qm)
                    rows = slice(e * HEAD_LANES, (e + 1) * HEAD_LANES)
                    dq_t = _dot(ktt[pp, kblk, rows, :], ds_b[0:BLK])
                    for b in range(1, nk):
                        dq_t = dq_t + _dot(ktt[pp, kblk + b, rows, :], ds_b[b * BLK:(b + 1) * BLK])
                    key_side = dsv[:, 0:BLK]
                    for a in range(1, nq):
                        key_side = key_side + dsv[:, a * BLK:(a + 1) * BLK]
                    dcs_acc[hh, 0:klen] = dcs_acc[hh, 0:klen] + key_side
                    query_side = jnp.sum(dsv, axis=0, keepdims=True)
                    for a in range(nq):
                        cols = slice(a * BLK, (a + 1) * BLK)
                        dqt[pp, qblk + a, rows, :] = dqt[pp, qblk + a, rows, :] + dq_t[:, cols]
                        dcq[a] = dcq[a] + jnp.where(sub8 == h, query_side[:, cols], 0.0)
                for a in range(nq):
                    dcq_ref[qblk + a] = dcq_ref[qblk + a] + dcq[a]

            later_mask = "valid" if nk == 1 else None
            n_later = jnp.asarray(n_later, jnp.int32)
            n_wide = n_later // WIDE

            def later_wide(i, carry):
                tile(kblk + nk + 2 * WIDE * i, 2 * WIDE, later_mask)
                return carry

            tile(kblk, nk, masks["first"] if nk == 1 else masks["diag"])
            lax.fori_loop(0, n_wide, later_wide, 0)
            rest_blk = kblk + nk + 2 * WIDE * n_wide

            @pl.when((n_later & 2) != 0)
            def _():
                tile(rest_blk, 4, later_mask)

            @pl.when((n_later & 1) != 0)
            def _():
                tile(rest_blk + 2 * (n_later & 2), 2, later_mask)

            upd = jnp.zeros((klen, BLK), F32)
            for hh, (_, _, h) in enumerate(heads):
                upd = upd + jnp.where(klane == h, -jnp.sum(dcs_acc[hh, 0:klen], axis=1, keepdims=True), 0.0)
            dc_ref[pl.ds(koff, klen), :] = dc_ref[pl.ds(koff, klen), :] + upd
            for pp in range(FOX_PAIRS):
                dp_ref[pl.ds(koff, klen), pp * 384 + BLK:pp * 384 + 2 * BLK] = (
                    dk_acc[pp, 0:klen] * (1.0 / LOG2E)).astype(BF16)
                dp_ref[pl.ds(koff, klen), pp * 384 + 2 * BLK:pp * 384 + 3 * BLK] = dv_acc[pp, 0:klen].astype(BF16)

        kv_pass(0, 1, nu)

        def k_loop(u, carry):
            kv_pass(1 + 2 * u, 2, nu - 1 - u)
            return carry

        lax.fori_loop(0, nu, k_loop, 0)

        def flush(j, carry):
            off = pl.multiple_of(j * BLK, BLK)
            for pp in range(FOX_PAIRS):
                dp_ref[pl.ds(off, BLK), pp * 384:pp * 384 + BLK] = (dqt[pp, j].T * scale).astype(BF16)
            return carry

        lax.fori_loop(0, nblk, flush, 0)

        if ns:
            @pl.when(p == steps - 1)
            def _():
                copies = _scatter_copies(s_in, s_out, *rest[2 * ns + 8:])
                for cp in copies:
                    cp.wait_recv()
                for cp in copies:
                    cp.wait_send()

    width = 384 * FOX_PAIRS
    once = lambda shape, index: pl.BlockSpec(shape, index, pipeline_mode=pl.Buffered(1))
    stat = once((nblk, 8, BLK), lambda p: (0, 0, 0))
    return pl.pallas_call(
        body, name="b_fox", grid=(steps,),
        in_specs=[once((L, width), lambda p: (0, RET_W // width + p)),
                  once((L, FOX_PAIRS * BLK), lambda p: (0, 4 // FOX_PAIRS + p)),
                  once((L, BLK), lambda p: (0, 0)), stat, stat, stat] + [_ANY] * ns,
        out_specs=[pl.BlockSpec((L, width), lambda p: (0, p)), _full((L, BLK)), _full((nblk, 8, BLK))] + [_ANY] * ns,
        out_shape=[jax.ShapeDtypeStruct((L, FOX_W), BF16), jax.ShapeDtypeStruct((L, BLK), F32),
                   jax.ShapeDtypeStruct((nblk, 8, BLK), F32)] + _scatter_shapes(scatter),
        scratch_shapes=[pltpu.VMEM((FOX_PAIRS, nblk, BLK, BLK), BF16), pltpu.VMEM((FOX_PAIRS, nblk, BLK, BLK), F32),
                        pltpu.VMEM((FOX_PAIRS, UNIT, BLK), F32), pltpu.VMEM((FOX_PAIRS, UNIT, BLK), F32),
                        pltpu.VMEM((2 * FOX_PAIRS, UNIT, BLK), F32)]
        + _scatter_semaphores(ns),
        compiler_params=_params(("arbitrary",)),
    )(proj, dmix, c, ctb, lse, delta, *scatter)


def _fox_post(dc, dcq, ff, fb):
    L = dc.shape[0]
    nblk = L // BLK
    G = _block_group(nblk)
    steps = nblk // G

    def body(dc_ref, dcq_ref, ff_ref, b_ref, dff_ref, dffb_ref, dfb_ref, carry):
        @pl.when(pl.program_id(0) == 0)
        def _():
            carry[...] = jnp.zeros_like(carry)
            dfb_ref[...] = jnp.zeros_like(dfb_ref)

        tri = (_iota((BLK, BLK), 0) <= _iota((BLK, BLK), 1)).astype(BF16)
        live = _iota((BLK, BLK), 1) < FOX_HEADS
        run, dfb = carry[...], dfb_ref[...]
        for b in reversed(range(G)):
            rows = slice(b * BLK, (b + 1) * BLK)
            d = dc_ref[rows, :] + jnp.concatenate([dcq_ref[b], jnp.zeros((BLK - 8, BLK), F32)], axis=0).T
            hi, mid, lo = _split3(d)
            dlf = _dot(tri, hi) + _dot(tri, mid) + _dot(tri, lo) + run
            run = run + jnp.sum(d, axis=0, keepdims=True)
            z = ff_ref[rows, :] + b_ref[...]
            dff = jnp.where(live, dlf * jax.nn.sigmoid(-z), 0.0)
            dff_ref[rows, :] = dff
            dffb_ref[rows, :] = dff.astype(BF16)
            dfb = dfb + jnp.sum(dff, axis=0, keepdims=True)
        carry[...] = run
        dfb_ref[...] = dfb

    rev = lambda i: (steps - 1 - i, 0)
    return pl.pallas_call(
        body, name="b_foxpost", grid=(steps,),
        in_specs=[pl.BlockSpec((G * BLK, BLK), rev), pl.BlockSpec((G, 8, BLK), lambda i: (steps - 1 - i, 0, 0)),
                  pl.BlockSpec((G * BLK, BLK), rev), _full((1, BLK))],
        out_specs=[pl.BlockSpec((G * BLK, BLK), rev), pl.BlockSpec((G * BLK, BLK), rev), _full((1, BLK))],
        out_shape=[jax.ShapeDtypeStruct((L, BLK), F32), jax.ShapeDtypeStruct((L, BLK), BF16),
                   jax.ShapeDtypeStruct((1, BLK), F32)],
        scratch_shapes=[pltpu.VMEM((1, BLK), F32)],
        compiler_params=_params(("arbitrary",)),
    )(dc, dcq, ff, fb)


def _inproj_bwd(dpr, dpf, dffb, w_main, w_ff, h0, g, dh1, scatter=()):
    L = h0.shape[0]
    S = L - BLK
    tm = _row_tile(S, (512, 256, 128))
    nt = S // tm
    ns = len(scatter)
    operands = (dpr, dpf, dffb, h0, dh1)

    def body(*refs):
        lead, tile = refs[0:5], refs[5:10]
        wm_ref, wf_ref, g_ref = refs[10:13]
        rest = refs[13:]
        s_in, (dlead_ref, dx_ref, dg_ref), s_out = rest[:ns], rest[ns:ns + 3], rest[ns + 3:2 * ns + 3]
        i = pl.program_id(0)

        def rows_bwd(dpr_ref, dpf_ref, dff_ref, h_ref, dh1_ref):
            dn = (_dot_nt(dpr_ref[...], wm_ref[:, 0:RET_W]) + _dot_nt(dpf_ref[...], wm_ref[:, RET_W:MAIN_W])
                  + _dot_nt(dff_ref[...], wf_ref[...]))
            h = h_ref[...]
            r = lax.rsqrt(jnp.mean(h * h, axis=-1, keepdims=True) + EPS)
            yn = h * r
            dyn = dn * g_ref[...]
            dh0 = dh1_ref[...] + r * (dyn - yn * jnp.mean(dyn * yn, axis=-1, keepdims=True))
            return dh0, jnp.sum(dn * yn, axis=0, keepdims=True)

        @pl.when(i == 0)
        def _():
            if ns:
                for cp in _scatter_copies(s_in, s_out, *rest[2 * ns + 3:]):
                    cp.start()
            dlead_ref[...], dg_ref[...] = rows_bwd(*lead)

        dx_ref[...], dg_tile = rows_bwd(*tile)
        dg_ref[...] = dg_ref[...] + dg_tile

        if ns:
            @pl.when(i == nt - 1)
            def _():
                copies = _scatter_copies(s_in, s_out, *rest[2 * ns + 3:])
                for cp in copies:
                    cp.wait_recv()
                for cp in copies:
                    cp.wait_send()

    lead_spec = lambda a: pl.BlockSpec((BLK, a.shape[1]), lambda i: (0, 0))
    tile_spec = lambda a: pl.BlockSpec((pl.Element(tm), pl.Element(a.shape[1])),
                                       lambda i: (pl.multiple_of(BLK + i * tm, BLK), 0))
    return pl.pallas_call(
        body, name="b_inproj", grid=(nt,),
        in_specs=[lead_spec(a) for a in operands] + [tile_spec(a) for a in operands]
        + [_full((D_MODEL, MAIN_W)), _full((D_MODEL, BLK)), _full((1, D_MODEL))] + [_ANY] * ns,
        out_specs=[_full((BLK, D_MODEL)), pl.BlockSpec((tm, D_MODEL), lambda i: (i, 0)), _full((1, D_MODEL))]
        + [_ANY] * ns,
        out_shape=[jax.ShapeDtypeStruct((BLK, D_MODEL), F32), jax.ShapeDtypeStruct((S, D_MODEL), F32),
                   jax.ShapeDtypeStruct((1, D_MODEL), F32)] + _scatter_shapes(scatter),
        scratch_shapes=_scatter_semaphores(ns),
        compiler_params=_params(("arbitrary",)),
    )(*operands, *operands, w_main, w_ff, g, *scatter)


def _local_step(x, target, meta, attn_g, w_main, w_ff, fox_b, ret_g, w_out, ffn_g, w_up, conv_w, conv_b, w_down, final_g,
                late=None, mid=None, last=None):
    S = x.shape[0]
    L = S + PREFIX
    head = jnp.concatenate([jnp.zeros((N_PAD, D_MODEL), F32), meta], axis=0)
    fb = jnp.pad(fox_b, ((0, 0), (0, BLK - FOX_HEADS)))
    cos_t, sin_t = _rotary_tables(L)

    h0, n1, proj, ff = _rms_inproj(head, x, attn_g, w_main, w_ff)
    c, ctb = _fox_prep(ff, fb)
    mix_r, o_ret, states = _retention_fwd(proj, cos_t, sin_t, ret_g)
    if late is None:
        o_f, lse = _fox_fwd(proj, c, ctb)
    else:
        o_f, lse, *gathered = _fox_fwd(proj, c, ctb, gather=late[0])
        w_out, w_up, w_down = late[1](gathered)
    h1, n2, up, g_act, acc_saved = _outproj_up(mix_r, o_f, h0, w_out, ffn_g, w_up, conv_w, conv_b)
    dh2, dh2b, d_final_g, loss = _ffn_down_loss(g_act, w_down, h1, final_g, target)

    dacc, db = _ffn_bwd_gate(dh2b, w_down, acc_saved, up)
    dup, dh1, dh1b, dmix, d_ffn_g, dconv = _ffn_bwd_up(dacc, db, up, conv_w, w_up, h1, ffn_g, dh2, w_out)
    d_w_down = _wgrad(g_act, dh2b, "wgrad_down", tk=D_FF // 2)[0]
    d_w_up = _wgrad(n2, dup, "wgrad_up", tn=w_up.shape[2])
    d_w_out = jnp.concatenate([_wgrad(mix_r, dh1b, "wgrad_out_r")[0], _wgrad(o_f, dh1b, "wgrad_out_f")[0]], axis=0)

    early = () if mid is None else mid[0](d_w_out, d_w_up, d_w_down)
    dpr, d_ret_g, *from_sibling = _retention_bwd(dmix, o_ret, proj, cos_t, sin_t, ret_g, states, exchange=early)
    delta = _fox_delta(dmix, o_f)
    scatter = () if mid is None else mid[1](early, from_sibling)
    dpf, dc, dcq, *received = _fox_bwd(proj, dmix, c, ctb, lse, delta, scatter=scatter)
    dff, dffb, d_fox_b = _fox_post(dc, dcq, ff, fb)
    d_w_main = jnp.concatenate([_wgrad(n1, dpr, "wgrad_in_r")[0], _wgrad(n1, dpf, "wgrad_in_f")[0]], axis=1)
    d_w_ff = _wgrad(n1, dffb, "wgrad_in_ff")[0][:, :FOX_HEADS]
    scatter_in = () if last is None else last(d_w_main, d_w_ff)
    dlead, dx, d_attn_g, *received_in = _inproj_bwd(dpr, dpf, dffb, w_main, w_ff, h0, attn_g, dh1, scatter=scatter_in)

    return dict(
        loss=loss[0, 0], dx=dx, dmeta=dlead[N_PAD:], attn_g=d_attn_g, w_main=d_w_main,
        w_ff=d_w_ff, fox_b=d_fox_b[:, :FOX_HEADS], ret_g=d_ret_g, w_out=d_w_out, ffn_g=d_ffn_g,
        w_up=d_w_up, conv_w=dconv[0:3], conv_b=dconv[3:4], w_down=d_w_down, final_g=d_final_g,
        scatter=list(scatter_in) + list(scatter), received=list(received_in) + list(received))


_ANY = pl.BlockSpec(memory_space=pl.ANY)


def _place():
    return lax.axis_index("x"), lax.axis_index("y"), lax.axis_index("c")


def _other_chips(x, y):
    return [(1 - x, y), (x, 1 - y), (1 - x, 1 - y)]


def _allgather_semaphores(n):
    if n == 0:
        return []
    return [pltpu.SemaphoreType.DMA((3 * n,)), pltpu.SemaphoreType.DMA((3 * n,)), pltpu.SemaphoreType.DMA((n,))]


def _allgather_copies(ins, outs, send, recv, loc):
    n = len(ins)
    x, y, c = _place()
    mine = 2 * x + y
    peers = _other_chips(x, y)

    def remote(a, k, slot):
        return pltpu.make_async_remote_copy(
            src_ref=ins[a], dst_ref=outs[a].at[slot], send_sem=send.at[3 * a + k], recv_sem=recv.at[3 * a + k],
            device_id=(peers[k][0], peers[k][1], c), device_id_type=MESH)

    local = [pltpu.make_async_copy(ins[a], outs[a].at[mine], loc.at[a]) for a in range(n)]
    sends = [remote(a, k, mine) for a in range(n) for k in range(3)]
    recvs = [remote(a, k, 2 * peers[k][0] + peers[k][1]) for a in range(n) for k in range(3)]
    return local, sends, recvs


def _chip_allgather_halves(w, small):
    half = w.shape[0] // 2

    def body(w_ref, s_ref, wo_ref, so_ref, send, recv, fsend, frecv, ssend, srecv, loc):
        x, y, c = _place()
        mine = 2 * x + y
        peers = _other_chips(x, y)

        def fetch(k, slot):
            return pltpu.make_async_remote_copy(
                src_ref=w_ref.at[pl.ds(c * half, half)], dst_ref=wo_ref.at[slot, c], send_sem=send.at[k],
                recv_sem=recv.at[k], device_id=(peers[k][0], peers[k][1], c), device_id_type=MESH)

        def forward(k, which):
            slot = 2 * peers[k][0] + peers[k][1]
            return pltpu.make_async_remote_copy(
                src_ref=wo_ref.at[slot, which], dst_ref=wo_ref.at[slot, which], send_sem=fsend.at[k],
                recv_sem=frecv.at[k], device_id=(x, y, 1 - c), device_id_type=MESH)

        def small_copy(k, slot):
            return pltpu.make_async_remote_copy(
                src_ref=s_ref, dst_ref=so_ref.at[slot], send_sem=ssend.at[k], recv_sem=srecv.at[k],
                device_id=(peers[k][0], peers[k][1], c), device_id_type=MESH)

        local = pltpu.make_async_copy(s_ref, so_ref.at[mine], loc.at[0])
        sends = [fetch(k, mine) for k in range(3)] + [small_copy(k, mine) for k in range(3)]
        local.start()
        for cp in sends:
            cp.start()
        forwards = []
        for k in range(3):
            fetch(k, 2 * peers[k][0] + peers[k][1]).wait_recv()
            forwards.append(forward(k, c))
            forwards[-1].start()
        for k in range(3):
            forward(k, 1 - c).wait_recv()
            small_copy(k, 2 * peers[k][0] + peers[k][1]).wait_recv()
        for cp in sends + forwards:
            cp.wait_send()
        local.wait()

    three = pltpu.SemaphoreType.DMA((3,))
    return pl.pallas_call(
        body, name="ag_weights", in_specs=[_ANY] * 2, out_specs=[_ANY] * 2,
        out_shape=[jax.ShapeDtypeStruct((N_CHIPS, 2, half, w.shape[1]), w.dtype),
                   jax.ShapeDtypeStruct((N_CHIPS,) + small.shape, small.dtype)],
        scratch_shapes=[three, three, three, three, three, three, pltpu.SemaphoreType.DMA((1,))],
    )(w, small)


def _chip_allgather(arrays):
    n = len(arrays)

    def body(*refs):
        local, sends, recvs = _allgather_copies(refs[:n], refs[n:2 * n], *refs[2 * n:])
        for cp in local + sends:
            cp.start()
        for cp in recvs:
            cp.wait_recv()
        for cp in sends:
            cp.wait_send()
        for cp in local:
            cp.wait()

    return pl.pallas_call(
        body, name="ag_weights", in_specs=[_ANY] * n, out_specs=[_ANY] * n,
        out_shape=[jax.ShapeDtypeStruct((N_CHIPS,) + a.shape, a.dtype) for a in arrays],
        scratch_shapes=_allgather_semaphores(n),
    )(*arrays)


def _sibling_halves(grads):
    n = len(grads)

    def body(*refs):
        sends, recvs = _sibling_half_copies(refs[:n], refs[n:2 * n], *refs[2 * n:])
        for cp in sends:
            cp.start()
        for cp in recvs:
            cp.wait_recv()
        for cp in sends:
            cp.wait_send()

    return pl.pallas_call(
        body, name="rs_sibling", in_specs=[_ANY] * n, out_specs=[_ANY] * n,
        out_shape=_sibling_half_shapes(grads), scratch_shapes=_sibling_half_semaphores(n),
    )(*grads)


def _sibling_half_shapes(grads):
    return [jax.ShapeDtypeStruct((N_CHIPS, g.shape[1] // 2, g.shape[2]), g.dtype) for g in grads]


def _sibling_half_semaphores(n):
    return [pltpu.SemaphoreType.DMA((n,)), pltpu.SemaphoreType.DMA((n,))] if n else []


def _sibling_half_copies(ins, outs, send, recv):
    x, y, c = _place()

    def half_copy(a, which):
        half = ins[a].shape[1] // 2
        return pltpu.make_async_remote_copy(
            src_ref=ins[a].at[pl.ds(0, N_CHIPS), pl.ds(which * half, half)], dst_ref=outs[a],
            send_sem=send.at[a], recv_sem=recv.at[a], device_id=(x, y, 1 - c), device_id_type=MESH)

    return [half_copy(a, 1 - c) for a in range(len(ins))], [half_copy(a, c) for a in range(len(ins))]


def _scatter_shapes(parts):
    return [jax.ShapeDtypeStruct((3,) + p.shape[1:], p.dtype) for p in parts]


def _scatter_semaphores(n):
    return [pltpu.SemaphoreType.DMA((3 * n,)), pltpu.SemaphoreType.DMA((3 * n,))] if n else []


def _scatter_copies(ins, outs, send, recv):
    x, y, c = _place()
    peers = _other_chips(x, y)
    return [pltpu.make_async_remote_copy(
        src_ref=ins[a].at[2 * peers[k][0] + peers[k][1]], dst_ref=outs[a].at[k], send_sem=send.at[3 * a + k],
        recv_sem=recv.at[3 * a + k], device_id=(peers[k][0], peers[k][1], c), device_id_type=MESH)
        for a in range(len(ins)) for k in range(3)]


def _sibling_allgather(bufs, small):
    n = len(bufs)

    def body(*refs):
        small_in, outs, small_out = refs[n], refs[n + 1:2 * n + 1], refs[2 * n + 1]
        send, recv, s_send, s_recv, loc = refs[2 * n + 2:]
        x, y, c = _place()
        me = 4 * x + 2 * y + c

        def remote(a, which):
            return pltpu.make_async_remote_copy(
                src_ref=outs[a].at[which], dst_ref=outs[a].at[which], send_sem=send.at[a], recv_sem=recv.at[a],
                device_id=(x, y, 1 - c), device_id_type=MESH)

        def peer_of(r):
            return tuple(1 - v if (r >> b) & 1 else v for v, b in ((x, 2), (y, 1), (c, 0)))

        def small_copy(r, slot):
            return pltpu.make_async_remote_copy(
                src_ref=small_in, dst_ref=small_out.at[slot], send_sem=s_send.at[r - 1], recv_sem=s_recv.at[r - 1],
                device_id=peer_of(r), device_id_type=MESH)

        local = pltpu.make_async_copy(small_in, small_out.at[me], loc.at[0])
        sends = [remote(a, c) for a in range(n)] + [small_copy(r, me) for r in range(1, N_DEV)]
        local.start()
        for cp in sends:
            cp.start()
        for r in range(1, N_DEV):
            px, py, pc = peer_of(r)
            small_copy(r, 4 * px + 2 * py + pc).wait_recv()
        for a in range(n):
            remote(a, 1 - c).wait_recv()
        for cp in sends:
            cp.wait_send()
        local.wait()

    outs = pl.pallas_call(
        body, name="ag_sibling", in_specs=[_ANY] * (n + 1), out_specs=[_ANY] * (n + 1),
        out_shape=[jax.ShapeDtypeStruct(b.shape, b.dtype) for b in bufs]
        + [jax.ShapeDtypeStruct((N_DEV,) + small.shape, small.dtype)],
        input_output_aliases={a: a for a in range(n)},
        scratch_shapes=[pltpu.SemaphoreType.DMA((n,)), pltpu.SemaphoreType.DMA((n,)),
                        pltpu.SemaphoreType.DMA((N_DEV - 1,)), pltpu.SemaphoreType.DMA((N_DEV - 1,)),
                        pltpu.SemaphoreType.DMA((1,))],
    )(*bufs, small)
    return [o.reshape(2 * o.shape[1], o.shape[2]) for o in outs[:n]], outs[n]


def _pair_add(full, recv, core, name):
    _, R, C = full.shape
    half = R // 2

    def body(core_ref, a_ref, b_ref, o_ref):
        o_ref[...] = (a_ref[...] + b_ref[...]).astype(BF16)

    return pl.pallas_call(
        body, name=name,
        grid_spec=pltpu.PrefetchScalarGridSpec(
            num_scalar_prefetch=1, grid=(N_CHIPS,),
            in_specs=[pl.BlockSpec((1, half, C), lambda j, core_ref: (j, core_ref[0], 0)),
                      pl.BlockSpec((1, half, C), lambda j, core_ref: (j, 0, 0))],
            out_specs=pl.BlockSpec((1, half, C), lambda j, core_ref: (j, 0, 0))),
        out_shape=jax.ShapeDtypeStruct((N_CHIPS, half, C), BF16),
        compiler_params=_params(("parallel",)),
    )(core, full, recv)


def _sum_slots(q, name, tiles=2):
    n, R, C = q.shape
    tr = R // tiles

    def body(q_ref, o_ref):
        acc = q_ref[0].astype(F32)
        for j in range(1, n):
            acc = acc + q_ref[j].astype(F32)
        o_ref[...] = acc

    return pl.pallas_call(
        body, name=name, grid=(tiles,),
        in_specs=[pl.BlockSpec((n, tr, C), lambda i: (0, i, 0))],
        out_specs=pl.BlockSpec((tr, C), lambda i: (i, 0)),
        out_shape=jax.ShapeDtypeStruct((R, C), F32),
        compiler_params=_params(("parallel",)),
    )(q)


def _sum_partials(own_all, recv, place, name, tiles=2):
    _, R, C = own_all.shape
    tr = R // tiles

    def body(place_ref, own_ref, r_ref, o_ref):
        acc = own_ref[0].astype(F32)
        for k in range(3):
            acc = acc + r_ref[k].astype(F32)
        o_ref[0] = acc

    return pl.pallas_call(
        body, name=name,
        grid_spec=pltpu.PrefetchScalarGridSpec(
            num_scalar_prefetch=1, grid=(tiles,),
            in_specs=[pl.BlockSpec((1, tr, C), lambda i, place_ref: (place_ref[0], i, 0)),
                      pl.BlockSpec((3, tr, C), lambda i, place_ref: (0, i, 0))],
            out_specs=pl.BlockSpec((1, tr, C), lambda i, place_ref: (place_ref[1], i, 0))),
        out_shape=jax.ShapeDtypeStruct((2, R, C), F32),
        compiler_params=_params(("parallel",)),
    )(place, own_all, recv)


def _adamw(w, g, m, v, name, tiles=4):
    R, C = w.shape
    tr = R // tiles

    def body(w_ref, g_ref, m_ref, v_ref, go_ref, d_ref, m2_ref, v2_ref):
        g_ = g_ref[...]
        go_ref[...] = g_
        m2 = ADAM_B1 * m_ref[...] + (1.0 - ADAM_B1) * g_
        v2 = ADAM_B2 * v_ref[...] + (1.0 - ADAM_B2) * (g_ * g_)
        m_hat = m2 / (1.0 - ADAM_B1 ** ADAM_STEP)
        v_hat = v2 / (1.0 - ADAM_B2 ** ADAM_STEP)
        d_ref[...] = -ADAM_LR * (m_hat / (jnp.sqrt(v_hat) + ADAM_EPS) + ADAM_WD * w_ref[...])
        m2_ref[...] = m2
        v2_ref[...] = v2

    spec = pl.BlockSpec((tr, C), lambda i: (i, 0))
    return pl.pallas_call(
        body, name=name, grid=(tiles,), in_specs=[spec] * 4, out_specs=[spec] * 4,
        out_shape=[jax.ShapeDtypeStruct((R, C), F32)] * 4,
        compiler_params=_params(("parallel",)),
    )(w, g, m, v)


def _pack_rows(pieces, rows):
    flat = jnp.concatenate([jnp.pad(p.reshape(-1).astype(F32), (0, (-p.size) % D_MODEL)) for p in pieces])
    return jnp.pad(flat, (0, rows * D_MODEL - flat.size)).reshape(rows, D_MODEL)


def _unpack_rows(pack, shapes):
    flat = pack.reshape(-1)
    out, off = [], 0
    for shp in shapes:
        size = int(np.prod(shp))
        out.append(flat[off:off + size].reshape(shp))
        off += size + (-size) % D_MODEL
    return out


def _kernel_order(w):
    parts = [w[:, 0:RET_W]]
    for p in range(FOX_HEADS // 2):
        parts += [w[:, RET_W + part * 512 + p * BLK:RET_W + part * 512 + (p + 1) * BLK] for part in range(3)]
    return jnp.concatenate(parts, axis=1)


def _reference_order(g_main, g_ff):
    parts = [g_main[:, 0:RET_W]]
    for part in range(3):
        parts += [g_main[:, RET_W + 384 * p + part * BLK:RET_W + 384 * p + (part + 1) * BLK] for p in range(FOX_HEADS // 2)]
    return jnp.concatenate(parts + [g_ff], axis=1)


def kernel(x, meta_tokens, attn_norm_g, w_in, fox_forget_b, ret_norm_g, w_out, ffn_norm_g, w_up, conv_w, conv_b, w_down, final_norm_g, loss_target, m_meta_tokens, m_attn_norm_g, m_w_in, m_fox_forget_b, m_ret_norm_g, m_w_out, m_ffn_norm_g, m_w_up, m_conv_w, m_conv_b, m_w_down, m_final_norm_g, v_meta_tokens, v_attn_norm_g, v_w_in, v_fox_forget_b, v_ret_norm_g, v_w_out, v_ffn_norm_g, v_w_up, v_conv_w, v_conv_b, v_w_down, v_final_norm_g):
    chip = 2 * lax.axis_index("x") + lax.axis_index("y")
    core = lax.axis_index("c")
    meta_w, conv_sw = meta_tokens.shape[1], conv_w.shape[2]

    small_w = _pack_rows([meta_tokens, conv_w[0]], 8)
    w_in_b = w_in[0].astype(BF16)
    g_in, g_small = _chip_allgather_halves(w_in_b, small_w)
    g_in = lax.dynamic_update_slice(g_in.reshape((N_CHIPS,) + w_in_b.shape), w_in_b[None], (chip, 0, 0))
    w_in_full = g_in.transpose(1, 0, 2).reshape(D_MODEL, IN_WIDTH)
    w_main = _kernel_order(w_in_full)
    w_ff = jnp.pad(w_in_full[:, MAIN_W:], ((0, 0), (0, BLK - FOX_HEADS)))
    small_parts = [_unpack_rows(g_small[j], [meta_tokens.shape, conv_w.shape[1:]]) for j in range(N_CHIPS)]
    meta_full = jnp.concatenate([sp[0] for sp in small_parts], axis=1)
    conv_w_full = jnp.concatenate([sp[1] for sp in small_parts], axis=1)

    core_idx = core.reshape(1).astype(jnp.int32)
    place = jnp.stack([chip, core]).astype(jnp.int32)

    def assemble(gathered):
        g_out, g_up, g_down = gathered
        return g_out.reshape(D_MODEL, D_MODEL), g_up, g_down.reshape(D_FF, D_MODEL)

    def early_arrays(d_w_out, d_w_up, d_w_down):
        return [d_w_out.reshape(N_CHIPS, -1, D_MODEL), d_w_up, d_w_down.reshape(N_CHIPS, -1, D_MODEL)]

    def in_sums(d_w_main, d_w_ff):
        g_in_full = _reference_order(d_w_main, d_w_ff).reshape(D_MODEL, N_CHIPS, -1).transpose(1, 0, 2)
        (from_sib,) = _sibling_halves([g_in_full])
        return [_pair_add(g_in_full, from_sib, core_idx, "pair_add_in")]

    def early_sums(early, from_sib):
        return [_pair_add(g, r, core_idx, "pair_add_" + nm) for g, r, nm in zip(early, from_sib, ("out", "up", "down"))]

    out = _local_step(x[0], loss_target[0], meta_full, attn_norm_g, w_main, w_ff, fox_forget_b, ret_norm_g,
                      None, ffn_norm_g, None, conv_w_full, conv_b, None, final_norm_g[None],
                      late=([w_out[0].astype(BF16), w_up[0].astype(BF16), w_down[0].astype(BF16)], assemble),
                      mid=(early_arrays, early_sums), last=in_sums)

    small_shapes = [(1, D_MODEL), (1, D_MODEL), (1, D_MODEL), (1, 512 + FOX_HEADS + 1), (1, D_FF), (N_META, D_MODEL), (3, D_FF)]
    small = _pack_rows([out["attn_g"], out["ffn_g"], out["final_g"],
                        jnp.concatenate([out["ret_g"], out["fox_b"], out["loss"].reshape(1, 1)], axis=1),
                        out["conv_b"], out["dmeta"], out["conv_w"]], 32)
    names = ("in", "out", "up", "down")
    totals = [_sum_partials(s, q, place, "sum_chips_" + nm) for s, q, nm in zip(out["scatter"], out["received"], names)]
    (grad_in, grad_out, grad_up, grad_down), small_all = _sibling_allgather(totals, small)
    s_attn, s_ffn, s_final, s_misc, s_conv_b, s_meta, s_conv_w = _unpack_rows(
        _sum_slots(small_all, "sum_small", tiles=1), small_shapes)
    loss = s_misc[0, 512 + FOX_HEADS]
    small_grads = [lax.dynamic_slice_in_dim(s_meta, chip * meta_w, meta_w, axis=1), s_attn, s_misc[:, 512:512 + FOX_HEADS],
                   s_misc[:, :512], s_ffn, lax.dynamic_slice_in_dim(s_conv_w, chip * conv_sw, conv_sw, axis=1)[None],
                   s_conv_b, s_final[0]]

    big_w = [(w_in, m_w_in, v_w_in, grad_in, "adamw_in"), (w_out, m_w_out, v_w_out, grad_out, "adamw_out"),
             (w_up, m_w_up, v_w_up, grad_up, "adamw_up"), (w_down, m_w_down, v_w_down, grad_down, "adamw_down")]
    big_res = [[r[None] for r in _adamw(w[0], g, m[0], v[0], nm)] for w, m, v, g, nm in big_w]
    small_w_list = [meta_tokens, attn_norm_g, fox_forget_b, ret_norm_g, ffn_norm_g, conv_w, conv_b, final_norm_g]
    small_m = [m_meta_tokens, m_attn_norm_g, m_fox_forget_b, m_ret_norm_g, m_ffn_norm_g, m_conv_w, m_conv_b, m_final_norm_g]
    small_v = [v_meta_tokens, v_attn_norm_g, v_fox_forget_b, v_ret_norm_g, v_ffn_norm_g, v_conv_w, v_conv_b, v_final_norm_g]
    shapes = [a.shape for a in small_w_list]
    packs = [_pack_rows(lst, 16) for lst in (small_w_list, small_grads, small_m, small_v)]
    small_res = [_unpack_rows(r, shapes) for r in _adamw(*packs, "adamw_small", tiles=1)[1:]]
    small_grads = [g.reshape(s) for g, s in zip(small_grads, shapes)]

    def ordered(kind):
        sm = small_grads if kind == 0 else small_res[kind - 1]
        bg = [r[kind] for r in big_res]
        return [sm[0], sm[1], bg[0], sm[2], sm[3], bg[1], sm[4], bg[2], sm[5], sm[6], bg[3], sm[7]]

    return (loss, out["dx"][None], *ordered(0), *ordered(1), *ordered(2), *ordered(3))
```

```python
import functools

import numpy as np
import jax
import jax.numpy as jnp
from jax import lax
from jax.experimental import pallas as pl
from jax.experimental.pallas import tpu as pltpu

F32 = jnp.float32
BF16 = jnp.bfloat16

D_MODEL = 1024
N_META = 16
BLK = 128
UNIT = 2 * BLK
FOX_PAIRS = 2
WIDE = 4
CHUNK = 64
N_PAD = BLK - N_META
PREFIX = BLK
RET_HEADS = 4
FOX_HEADS = 8
HEAD_LANES = 64
D_FF = 2816
ROPE_BASE = 10000.0
EPS = 1e-6
NEG = -1e30
LOG2E = 1.4426950408889634
RET_W = 1536
FOX_W = 1536
MAIN_W = RET_W + FOX_W
IN_WIDTH = MAIN_W + FOX_HEADS
N_CHIPS = 4
N_DEV = 8

ADAM_LR = 0.001
ADAM_B1 = 0.9
ADAM_B2 = 0.999
ADAM_EPS = 1e-08
ADAM_WD = 0.01
ADAM_STEP = 10

MESH = pl.DeviceIdType.MESH
VMEM_LIMIT_MB = 56

_NT = (((1,), (1,)), ((), ()))
_TN = (((0,), (0,)), ((), ()))


def _dot(a, b):
    return jnp.dot(a, b, preferred_element_type=F32)


def _dot_nt(a, b):
    return lax.dot_general(a, b, _NT, preferred_element_type=F32)


def _dot_tn(a, b):
    return lax.dot_general(a, b, _TN, preferred_element_type=F32)


def _params(dims=None, vmem_mb=VMEM_LIMIT_MB):
    kw = dict(vmem_limit_bytes=vmem_mb << 20)
    if dims is not None:
        kw["dimension_semantics"] = dims
    return pltpu.CompilerParams(**kw)


def _row_tile(n, prefs=(384, 256, 128)):
    for t in prefs:
        if n % t == 0:
            return t
    raise ValueError(f"no row tile for {n}")


def _iota(shape, dim):
    return lax.broadcasted_iota(jnp.int32, shape, dim)


def _pick_row(tile, row):
    sub = _iota(tile.shape, 0)
    return jnp.sum(jnp.where(sub == row, tile, 0.0), axis=0, keepdims=True)


def _split3(x):
    hi = x.astype(BF16)
    r1 = x - hi.astype(F32)
    mid = r1.astype(BF16)
    lo = (r1 - mid.astype(F32)).astype(BF16)
    return hi, mid, lo


def _full(shape):
    nd = len(shape)
    return pl.BlockSpec(shape, lambda *_: (0,) * nd)


def _in_perm():
    cols = list(range(RET_W))
    for p in range(FOX_HEADS // 2):
        for part in range(3):
            start = RET_W + part * 512 + p * BLK
            cols += list(range(start, start + BLK))
    return np.asarray(cols, np.int32)


def _rotary_tables(L):
    half = HEAD_LANES // 2
    inv = 1.0 / (ROPE_BASE ** (jnp.arange(half, dtype=F32) / half))
    ang = jnp.arange(L).astype(F32)[:, None] * inv[None, :]
    cos, sin = jnp.cos(ang), jnp.sin(ang)
    cos_t = jnp.tile(cos, (1, 4))
    sin_t = jnp.tile(jnp.concatenate([-sin, sin], axis=1), (1, 2))
    return cos_t, sin_t


def _decay_tables():
    gam = 1.0 - 2.0 ** (-5.0 - np.arange(RET_HEADS, dtype=np.float64))
    n = np.arange(BLK)
    same_or_past = (n[:, None] // CHUNK) >= (n[None, :] // CHUNK)
    dist = np.abs(n[:, None] - n[None, :])
    dmat = np.stack([np.where(same_or_past, g ** dist, 0.0) for g in gam]).astype(np.float32)
    lane_head = np.arange(BLK) // HEAD_LANES
    wq = np.stack([gam[2 * p + lane_head][None, :] ** (n[:, None] + 1.0) for p in range(2)]).astype(np.float32)
    wk = np.stack([gam[2 * p + lane_head][None, :] ** (BLK - 1.0 - n[:, None]) for p in range(2)]).astype(np.float32)
    g_blk = tuple(float(g ** BLK) for g in gam)
    return jnp.asarray(dmat), jnp.asarray(wq), jnp.asarray(wk), g_blk


def _shifted_blocks(tm):
    nb = tm // BLK
    return [pl.BlockSpec((BLK, D_MODEL), lambda i, j=j: (jnp.maximum(nb * i + j - 1, 0), 0)) for j in range(nb)]


def _rms_inproj(head, x, g, w_main, w_ff):
    L = x.shape[0] + BLK
    tm = _row_tile(L)
    nb = tm // BLK

    def body(head_ref, *refs):
        x_refs, (g_ref, wm_ref, wf_ref, h_ref, n_ref, p_ref, ff_ref) = refs[:nb], refs[nb:]
        parts = [r[...] for r in x_refs]
        parts[0] = jnp.where(pl.program_id(0) == 0, head_ref[...], parts[0])
        h = jnp.concatenate(parts, axis=0)
        h_ref[...] = h
        r = lax.rsqrt(jnp.mean(h * h, axis=-1, keepdims=True) + EPS)
        n = (h * r * g_ref[...]).astype(BF16)
        n_ref[...] = n
        p_ref[...] = _dot(n, wm_ref[...]).astype(BF16)
        ff_ref[...] = _dot(n, wf_ref[...])

    rows = lambda w: pl.BlockSpec((tm, w), lambda i: (i, 0))
    return pl.pallas_call(
        body, name="f_inproj", grid=(L // tm,),
        in_specs=[_full((BLK, D_MODEL))] + _shifted_blocks(tm)
        + [_full((1, D_MODEL)), _full((D_MODEL, MAIN_W)), _full((D_MODEL, BLK))],
        out_specs=[rows(D_MODEL), rows(D_MODEL), rows(MAIN_W), rows(BLK)],
        out_shape=[jax.ShapeDtypeStruct((L, D_MODEL), F32), jax.ShapeDtypeStruct((L, D_MODEL), BF16),
                   jax.ShapeDtypeStruct((L, MAIN_W), BF16), jax.ShapeDtypeStruct((L, BLK), F32)],
        compiler_params=_params(("parallel",)),
    )(head, *([x] * nb), g, w_main, w_ff)


def _block_group(nblk):
    return 3 if nblk % 3 == 0 else 1


def _fox_prep(ff, fb):
    L = ff.shape[0]
    nblk = L // BLK
    G = _block_group(nblk)

    def body(ff_ref, b_ref, c_ref, ct_ref, carry):
        @pl.when(pl.program_id(0) == 0)
        def _():
            carry[...] = jnp.zeros_like(carry)

        tri = (_iota((BLK, BLK), 0) >= _iota((BLK, BLK), 1)).astype(BF16)
        live = _iota((BLK, BLK), 1) < FOX_HEADS
        run = carry[...]
        for b in range(G):
            z = ff_ref[b * BLK:(b + 1) * BLK, :] + b_ref[...]
            lf = jnp.where(live, jnp.minimum(z, 0.0) - jnp.log1p(jnp.exp(-jnp.abs(z))), 0.0)
            hi, mid, lo = _split3(lf)
            cs = (_dot(tri, hi) + _dot(tri, mid) + _dot(tri, lo) + run) * LOG2E
            c_ref[b * BLK:(b + 1) * BLK, :] = cs
            ct_ref[b] = cs.T[0:8, :]
            run = run + jnp.sum(lf, axis=0, keepdims=True)
        carry[...] = run

    return pl.pallas_call(
        body, name="f_foxprep", grid=(nblk // G,),
        in_specs=[pl.BlockSpec((G * BLK, BLK), lambda i: (i, 0)), _full((1, BLK))],
        out_specs=[pl.BlockSpec((G * BLK, BLK), lambda i: (i, 0)), pl.BlockSpec((G, 8, BLK), lambda i: (i, 0, 0))],
        out_shape=[jax.ShapeDtypeStruct((L, BLK), F32), jax.ShapeDtypeStruct((nblk, 8, BLK), F32)],
        scratch_shapes=[pltpu.VMEM((1, BLK), F32)],
        compiler_params=_params(("arbitrary",)),
    )(ff, fb)


def _rot_fns(cos, sin):
    lane = _iota((BLK, BLK), 1)
    first = (lane & (HEAD_LANES - 1)) < HEAD_LANES // 2

    def swap(x):
        return jnp.where(first, pltpu.roll(x, BLK - 32, 1), pltpu.roll(x, 32, 1))

    def rot(x):
        return x * cos + swap(x) * sin

    def rot_t(dy):
        return dy * cos + swap(dy * sin)

    return rot, rot_t


def _retention_fwd(proj, cos_t, sin_t, ret_g):
    L = proj.shape[0]
    nblk = L // BLK
    G = _block_group(nblk)
    dmat, wq_t, wk_t, g_blk = _decay_tables()

    def body(q_ref, k_ref, v_ref, gate_ref, cos_ref, sin_ref, d_ref, wq_ref, wk_ref, rg_ref,
             mix_ref, o_ref, rs_ref, state):
        @pl.when(pl.program_id(0) == 0)
        def _():
            state[...] = jnp.zeros_like(state)

        lane = _iota((BLK, BLK), 1)
        sub = _iota((BLK, BLK), 0)
        for b in range(G):
            rows = slice(b * BLK, (b + 1) * BLK)
            rot, _ = _rot_fns(cos_ref[rows, :], sin_ref[rows, :])
            for p in range(2):
                qr = rot(q_ref[rows, p * BLK:(p + 1) * BLK].astype(F32))
                kr = rot(k_ref[rows, p * BLK:(p + 1) * BLK].astype(F32)) * (HEAD_LANES ** -0.5)
                kr_b = kr.astype(BF16)
                qw = (qr * wq_ref[p]).astype(BF16)
                kw = (kr * wk_ref[p]).astype(BF16)
                for e in range(2):
                    h = 2 * p + e
                    cols = slice(h * BLK, (h + 1) * BLK)
                    qm = jnp.where((lane >> 6) == e, qr, 0.0).astype(BF16)
                    s = _dot_nt(qm, kr_b) * d_ref[h]
                    vh = v_ref[rows, cols]
                    st = state[h]
                    rs_ref[b, h] = st
                    o = _dot(s.astype(BF16), vh) + _dot(qw, st.astype(BF16))
                    u = jnp.where((sub >> 6) == e, _dot_tn(kw, vh), 0.0)
                    state[h] = g_blk[h] * st + u
                    rn = lax.rsqrt(jnp.mean(o * o, axis=-1, keepdims=True) + EPS)
                    gate = gate_ref[rows, cols].astype(F32)
                    o_ref[rows, cols] = o
                    mix_ref[rows, cols] = (o * rn * rg_ref[:, cols] * (gate * jax.nn.sigmoid(gate))).astype(BF16)

    row = lambda c: (lambda i: (i, c))
    return pl.pallas_call(
        body, name="f_retention", grid=(nblk // G,),
        in_specs=[pl.BlockSpec((G * BLK, 256), row(0)), pl.BlockSpec((G * BLK, 256), row(1)),
                  pl.BlockSpec((G * BLK, 512), row(1)), pl.BlockSpec((G * BLK, 512), row(2)),
                  pl.BlockSpec((G * BLK, BLK), row(0)), pl.BlockSpec((G * BLK, BLK), row(0)),
                  _full((RET_HEADS, BLK, BLK)), _full((2, BLK, BLK)), _full((2, BLK, BLK)), _full((1, 512))],
        out_specs=[pl.BlockSpec((G * BLK, 512), row(0)), pl.BlockSpec((G * BLK, 512), row(0)),
                   pl.BlockSpec((G, RET_HEADS, BLK, BLK), lambda i: (i, 0, 0, 0))],
        out_shape=[jax.ShapeDtypeStruct((L, 512), BF16), jax.ShapeDtypeStruct((L, 512), F32),
                   jax.ShapeDtypeStruct((nblk, RET_HEADS, BLK, BLK), F32)],
        scratch_shapes=[pltpu.VMEM((RET_HEADS, BLK, BLK), F32)],
        compiler_params=_params(("arbitrary",)),
    )(proj, proj, proj, proj, cos_t, sin_t, dmat, wq_t, wk_t, ret_g)


def _fox_units(L):
    nblk = L // BLK
    assert L % BLK == 0 and nblk % 2 == 1, "sequence must be one 128-row block plus whole 256-row tiles"
    return nblk, (nblk - 1) // 2


def _fox_tile_masks():
    sub, lane = _iota((BLK, BLK), 0), _iota((BLK, BLK), 1)
    valid = _iota((BLK, UNIT), 0) >= N_PAD
    diag = _iota((UNIT, UNIT), 0) <= _iota((UNIT, UNIT), 1)
    r, q = _iota((BLK + UNIT, UNIT), 0), _iota((BLK + UNIT, UNIT), 1)
    first_and_diag = ((r < BLK) & (r >= N_PAD)) | ((r >= BLK) & (r - BLK <= q))
    return dict(first=(sub <= lane) & (sub >= N_PAD), valid=valid, diag=diag, first_and_diag=first_and_diag)


def _fox_fwd(proj, c, ctb, gather=()):
    L = proj.shape[0]
    nblk, nu = _fox_units(L)
    scale = HEAD_LANES ** -0.5 * LOG2E
    ng = len(gather)
    steps = FOX_HEADS // (2 * FOX_PAIRS)

    def body(qkv_ref, c_ref, ct_ref, *rest):
        g_in, (of_ref, lse_ref), g_out = rest[:ng], rest[ng:ng + 2], rest[ng + 2:2 * ng + 2]
        vt, csb = rest[2 * ng + 2:2 * ng + 4]
        p = pl.program_id(0)
        heads = [(pp, e, 2 * FOX_PAIRS * p + 2 * pp + e) for pp in range(FOX_PAIRS) for e in range(2)]

        @pl.when(p == 0)
        def _():
            lse_ref[...] = jnp.zeros_like(lse_ref)
            if ng:
                local, sends, _ = _allgather_copies(g_in, g_out, *rest[2 * ng + 4:])
                for cp in local + sends:
                    cp.start()

        lane = _iota((BLK, BLK), 1)
        sub8 = _iota((8, BLK), 0)
        masks = _fox_tile_masks()

        def pre(j, carry):
            off = pl.multiple_of(j * BLK, BLK)
            ct = c_ref[pl.ds(off, BLK), :]
            for pp in range(FOX_PAIRS):
                vt[pp, j] = qkv_ref[pl.ds(off, BLK), pp * 384 + 2 * BLK:pp * 384 + 3 * BLK].astype(F32).T.astype(BF16)
            for hh, (_, _, h) in enumerate(heads):
                col = jnp.sum(jnp.where(lane == h, ct, 0.0), axis=1, keepdims=True)
                csb[hh, j] = jnp.broadcast_to(col, (BLK, BLK))
            return carry

        lax.fori_loop(0, nblk, pre, 0)

        def attend(qblk, nq, n_whole):
            qlen = nq * BLK
            qoff = pl.multiple_of(qblk * BLK, BLK)
            qlane = _iota((qlen, BLK), 1)
            qs = [qkv_ref[pl.ds(qoff, qlen), pp * 384:pp * 384 + BLK].astype(F32) * scale for pp in range(FOX_PAIRS)]
            qm = [jnp.where((qlane >> 6) == e, qs[pp], 0.0).astype(BF16) for pp, e, _ in heads]
            ct_row = [jnp.concatenate([_pick_row(ct_ref[qblk + a], h) for a in range(nq)], axis=1) for _, _, h in heads]

            def step(segs, mask, st):
                blocks = [kblk + b for kblk, nk in segs for b in range(nk)]
                kts = []
                for pp in range(FOX_PAIRS):
                    kt = [qkv_ref[pl.ds(pl.multiple_of(kblk * BLK, BLK), nk * BLK), pp * 384 + BLK:pp * 384 + 2 * BLK]
                          for kblk, nk in segs]
                    kts.append(kt[0] if len(kt) == 1 else jnp.concatenate(kt, axis=0))
                out = []
                for hh, (pp, e, _) in enumerate(heads):
                    m, l, acc = st[3 * hh:3 * hh + 3]
                    s = _dot_nt(kts[pp], qm[hh])
                    t = jnp.concatenate([s[b * BLK:(b + 1) * BLK] - jnp.concatenate([csb[hh, blk]] * nq, axis=1)
                                         for b, blk in enumerate(blocks)], axis=0)
                    if mask is not None:
                        t = jnp.where(mask, t, NEG)
                    m_new = jnp.maximum(m, jnp.max(t, axis=0, keepdims=True) + ct_row[hh])
                    alpha = jnp.exp2(m - m_new)
                    pr = jnp.exp2(t - (m_new - ct_row[hh]))
                    l = alpha * l + jnp.sum(pr, axis=0, keepdims=True)
                    pr_b = pr.astype(BF16)
                    pv = None
                    for b, blk in enumerate(blocks):
                        part = _dot(vt[pp, blk, e * HEAD_LANES:(e + 1) * HEAD_LANES, :], pr_b[b * BLK:(b + 1) * BLK])
                        pv = part if pv is None else pv + part
                    out += [m_new, l, alpha * acc + pv]
                return tuple(out)

            st = (jnp.full((1, qlen), NEG, F32), jnp.zeros((1, qlen), F32),
                  jnp.zeros((HEAD_LANES, qlen), F32)) * len(heads)
            if nq == 1:
                st = step([(0, 1)], masks["first"], st)
            else:
                st = step([(0, 1), (qblk, 2)], masks["first_and_diag"], st)
                n_wide = n_whole // WIDE
                st = lax.fori_loop(0, n_wide, lambda j, s_: step([(1 + 2 * WIDE * j, 2 * WIDE)], None, s_), st)
                rest = 1 + 2 * WIDE * n_wide
                st = lax.cond((n_whole & 2) != 0, lambda s_: step([(rest, 4)], None, s_), lambda s_: s_, st)
                st = lax.cond((n_whole & 1) != 0, lambda s_: step([(rest + 2 * (n_whole & 2), 2)], None, s_),
                              lambda s_: s_, st)
            for pp in range(FOX_PAIRS):
                lo, hi = st[6 * pp:6 * pp + 3], st[6 * pp + 3:6 * pp + 6]
                o_t = jnp.concatenate([lo[2] * (1.0 / lo[1]), hi[2] * (1.0 / hi[1])], axis=0)
                of_ref[pl.ds(qoff, qlen), pp * BLK:(pp + 1) * BLK] = o_t.T.astype(BF16)
            lse = [st[3 * hh] + jnp.log(st[3 * hh + 1]) * LOG2E for hh in range(len(heads))]
            for a in range(nq):
                upd = jnp.zeros((8, BLK), F32)
                for hh, (_, _, h) in enumerate(heads):
                    upd = upd + jnp.where(sub8 == h, lse[hh][:, a * BLK:(a + 1) * BLK], 0.0)
                lse_ref[qblk + a] = lse_ref[qblk + a] + upd

        attend(0, 1, 0)

        def q_loop(u, carry):
            attend(1 + 2 * u, 2, u)
            return carry

        lax.fori_loop(0, nu, q_loop, 0)

        if ng:
            @pl.when(p == steps - 1)
            def _():
                local, sends, recvs = _allgather_copies(g_in, g_out, *rest[2 * ng + 4:])
                for cp in recvs:
                    cp.wait_recv()
                for cp in sends:
                    cp.wait_send()
                for cp in local:
                    cp.wait()

    width = 384 * FOX_PAIRS
    return pl.pallas_call(
        body, name="f_fox", grid=(steps,),
        in_specs=[pl.BlockSpec((L, width), lambda p: (0, RET_W // width + p)), _full((L, BLK)), _full((nblk, 8, BLK))]
        + [_ANY] * ng,
        out_specs=[pl.BlockSpec((L, FOX_PAIRS * BLK), lambda p: (0, p)), _full((nblk, 8, BLK))] + [_ANY] * ng,
        out_shape=[jax.ShapeDtypeStruct((L, 512), BF16), jax.ShapeDtypeStruct((nblk, 8, BLK), F32)]
        + [jax.ShapeDtypeStruct((N_CHIPS,) + a.shape, a.dtype) for a in gather],
        scratch_shapes=[pltpu.VMEM((FOX_PAIRS, nblk, BLK, BLK), BF16), pltpu.VMEM((2 * FOX_PAIRS, nblk, BLK, BLK), F32)]
        + _allgather_semaphores(ng),
        compiler_params=_params(("arbitrary",)),
    )(proj, c, ctb, *gather)


def _outproj_up(mix_r, o_f, h0, w_out, ffn_g, w_up, conv_w, conv_b):
    L = h0.shape[0]
    tm = _row_tile(L)
    shard = w_up.shape[2]
    assert 2 * shard == D_FF
    cw = [conv_w[j:j + 1] for j in range(3)]
    resident = lambda shape: pl.BlockSpec(shape, lambda i: (0,) * len(shape), pipeline_mode=pl.Buffered(1))

    def body(mr_ref, of_ref, h0_ref, wo_ref, g_ref, wu_ref, cw0, cw1, cw2, cb_ref,
             h1_ref, n2_ref, up_ref, act_ref, acc_ref, halo):
        i = pl.program_id(0)

        @pl.when(i == 0)
        def _():
            halo[...] = jnp.zeros_like(halo)

        h1 = h0_ref[...] + _dot(mr_ref[...], wo_ref[0:512, :]) + _dot(of_ref[...], wo_ref[512:1024, :])
        h1_ref[...] = h1
        r = lax.rsqrt(jnp.mean(h1 * h1, axis=-1, keepdims=True) + EPS)
        n2 = (h1 * r * g_ref[...]).astype(BF16)
        n2_ref[...] = n2
        live = i * tm + _iota((tm, 1), 0) >= N_PAD
        for half in range(2):
            cols = slice(half * shard, (half + 1) * shard)
            a_b = _dot(n2, wu_ref[half]).astype(BF16)
            b_b = _dot(n2, wu_ref[2 + half]).astype(BF16)
            up_ref[:, cols] = a_b
            up_ref[:, D_FF + half * shard:D_FF + (half + 1) * shard] = b_b
            a = jnp.where(live, a_b.astype(F32), 0.0)
            _, _, acc = _conv_taps(a, halo[:, cols], [cw0[:, cols], cw1[:, cols], cw2[:, cols]], cb_ref[:, cols])
            act_ref[:, cols] = (acc * jax.nn.sigmoid(acc) * b_b.astype(F32)).astype(BF16)
            acc_ref[:, cols] = acc.astype(BF16)
            halo[:, cols] = a[tm - 8:tm, :]

    rows = lambda w: pl.BlockSpec((tm, w), lambda i: (i, 0))
    return pl.pallas_call(
        body, name="f_outproj_up", grid=(L // tm,),
        in_specs=[rows(512), rows(512), rows(D_MODEL), resident((D_MODEL, D_MODEL)), _full((1, D_MODEL)),
                  resident((N_CHIPS, D_MODEL, shard)), _full((1, D_FF)), _full((1, D_FF)), _full((1, D_FF)),
                  _full((1, D_FF))],
        out_specs=[rows(D_MODEL), rows(D_MODEL), rows(2 * D_FF), rows(D_FF), rows(D_FF)],
        out_shape=[jax.ShapeDtypeStruct((L, D_MODEL), F32), jax.ShapeDtypeStruct((L, D_MODEL), BF16),
                   jax.ShapeDtypeStruct((L, 2 * D_FF), BF16), jax.ShapeDtypeStruct((L, D_FF), BF16),
                   jax.ShapeDtypeStruct((L, D_FF), BF16)],
        scratch_shapes=[pltpu.VMEM((8, D_FF), F32)],
        compiler_params=_params(("arbitrary",)),
    )(mix_r, o_f, h0, w_out, ffn_g, w_up, cw[0], cw[1], cw[2], conv_b)


def _conv_taps(a, halo, cw, cb):
    sub = _iota((a.shape[0], 1), 0)
    a1 = jnp.where(sub == 0, _pick_row(halo, 7), pltpu.roll(a, 1, 0))
    a2 = jnp.where(sub == 0, _pick_row(halo, 6), jnp.where(sub == 1, _pick_row(halo, 7), pltpu.roll(a, 2, 0)))
    acc = cb + a2 * cw[0]
    acc = acc + a1 * cw[1]
    acc = acc + a * cw[2]
    return a1, a2, acc


def _ffn_down_loss(g_act, w_down, h1, final_g, target):
    L = h1.shape[0]
    tm = _row_tile(L)
    nb = tm // BLK

    def body(g_ref, wd_ref, h1_ref, gf_ref, *refs):
        t_refs, (dh_ref, dhb_ref, dgf_ref, loss_ref) = refs[:nb], refs[nb:]
        i = pl.program_id(0)

        @pl.when(i == 0)
        def _():
            dgf_ref[...] = jnp.zeros_like(dgf_ref)
            loss_ref[...] = jnp.zeros_like(loss_ref)

        h2 = h1_ref[...] + _dot(g_ref[...], wd_ref[...])
        r = lax.rsqrt(jnp.mean(h2 * h2, axis=-1, keepdims=True) + EPS)
        yn = h2 * r
        gf = gf_ref[...]
        live = i * tm + _iota((tm, 1), 0) >= PREFIX
        target = jnp.concatenate([t[...] for t in t_refs], axis=0)
        err = jnp.where(live, yn * gf - target, 0.0)
        loss_ref[...] = loss_ref[...] + 0.5 * jnp.sum(jnp.mean(err * err, axis=-1, keepdims=True))
        dy = err * (1.0 / D_MODEL)
        dgf_ref[...] = dgf_ref[...] + jnp.sum(dy * yn, axis=0, keepdims=True)
        dyn = dy * gf
        dh = r * (dyn - yn * jnp.mean(dyn * yn, axis=-1, keepdims=True))
        dh_ref[...] = dh
        dhb_ref[...] = dh.astype(BF16)

    rows = lambda w: pl.BlockSpec((tm, w), lambda i: (i, 0))
    return pl.pallas_call(
        body, name="f_ffn_down_loss", grid=(L // tm,),
        in_specs=[rows(D_FF), _full((D_FF, D_MODEL)), rows(D_MODEL), _full((1, D_MODEL))] + _shifted_blocks(tm),
        out_specs=[rows(D_MODEL), rows(D_MODEL), _full((1, D_MODEL)), _full((1, BLK))],
        out_shape=[jax.ShapeDtypeStruct((L, D_MODEL), F32), jax.ShapeDtypeStruct((L, D_MODEL), BF16),
                   jax.ShapeDtypeStruct((1, D_MODEL), F32), jax.ShapeDtypeStruct((1, BLK), F32)],
        compiler_params=_params(("arbitrary",)),
    )(g_act, w_down, h1, final_g, *([target] * nb))


def _ffn_bwd_gate(dh2b, w_down, acc_saved, up):
    L = dh2b.shape[0]
    tm = _row_tile(L)

    def body(dh_ref, wd_ref, acc_ref, b_ref, dacc_ref, db_ref):
        acc = acc_ref[...].astype(F32)
        dg = _dot_nt(dh_ref[...], wd_ref[...])
        sg = jax.nn.sigmoid(acc)
        silu = acc * sg
        db_ref[...] = (dg * silu).astype(BF16)
        dacc_ref[...] = (dg * b_ref[...].astype(F32) * (sg + silu * (1.0 - sg))).astype(BF16)

    rows = lambda w, c=0: pl.BlockSpec((tm, w), lambda i: (i, c))
    return pl.pallas_call(
        body, name="b_ffn_gate", grid=(L // tm,),
        in_specs=[rows(D_MODEL), _full((D_FF, D_MODEL)), rows(D_FF), rows(D_FF, 1)],
        out_specs=[rows(D_FF), rows(D_FF)],
        out_shape=[jax.ShapeDtypeStruct((L, D_FF), BF16), jax.ShapeDtypeStruct((L, D_FF), BF16)],
        compiler_params=_params(("parallel",)),
    )(dh2b, w_down, acc_saved, up)


def _ffn_bwd_up(dacc, db, up, conv_w, w_up, h1, ffn_g, dh2, w_out):
    L = h1.shape[0]
    tm = _row_tile(L)
    nt = L // tm
    shard = w_up.shape[2]
    cw = [conv_w[j:j + 1] for j in range(3)]

    def body(da_ref, halo_ref, db_ref, a_ref, cw0, cw1, cw2, wu_ref, h1_ref, g_ref, dh2_ref, wo_ref,
             dup_ref, dh1_ref, dh1b_ref, dmix_ref, dg_ref, dcw_ref):
        i = pl.program_id(0)

        @pl.when(i == 0)
        def _():
            dg_ref[...] = jnp.zeros_like(dg_ref)
            dcw_ref[...] = jnp.zeros_like(dcw_ref)

        sub = _iota((tm, 1), 0)
        sub8 = _iota((8, 1), 0)
        last_tile = i == nt - 1
        dbv = db_ref[...]
        dup_ref[:, D_FF:2 * D_FF] = dbv
        dn = _dot_nt(dbv[:, 0:shard], wu_ref[2]) + _dot_nt(dbv[:, shard:2 * shard], wu_ref[3])
        for half in range(2):
            cols = slice(half * shard, (half + 1) * shard)
            d0 = da_ref[:, cols].astype(F32)
            halo = jnp.where(last_tile, 0.0, halo_ref[:, cols].astype(F32))
            d1 = jnp.where(sub == tm - 1, _pick_row(halo, 0), pltpu.roll(d0, tm - 1, 0))
            d2 = jnp.where(sub == tm - 2, _pick_row(halo, 0),
                           jnp.where(sub == tm - 1, _pick_row(halo, 1), pltpu.roll(d0, tm - 2, 0)))
            a = a_ref[:, cols].astype(F32)
            upd = jnp.zeros((8, shard), F32)
            for j, t in enumerate((d2 * a, d1 * a, d0 * a, d0)):
                upd = upd + jnp.where(sub8 == j, jnp.sum(t, axis=0, keepdims=True), 0.0)
            dcw_ref[:, cols] = dcw_ref[:, cols] + upd
            da = (d0 * cw2[:, cols] + d1 * cw1[:, cols] + d2 * cw0[:, cols]).astype(BF16)
            dup_ref[:, cols] = da
            dn = dn + _dot_nt(da, wu_ref[half])
        h1 = h1_ref[...]
        r = lax.rsqrt(jnp.mean(h1 * h1, axis=-1, keepdims=True) + EPS)
        yn = h1 * r
        dg_ref[...] = dg_ref[...] + jnp.sum(dn * yn, axis=0, keepdims=True)
        dyn = dn * g_ref[...]
        dh1 = dh2_ref[...] + r * (dyn - yn * jnp.mean(dyn * yn, axis=-1, keepdims=True))
        dh1_ref[...] = dh1
        dh1b = dh1.astype(BF16)
        dh1b_ref[...] = dh1b
        dmix_ref[...] = _dot_nt(dh1b, wo_ref[...]).astype(BF16)

    rows = lambda w: pl.BlockSpec((tm, w), lambda i: (i, 0))
    halo = pl.BlockSpec((8, D_FF), lambda i: (jnp.minimum((i + 1) * (tm // 8), L // 8 - 1), 0))
    return pl.pallas_call(
        body, name="b_ffn_up", grid=(nt,),
        in_specs=[rows(D_FF), halo, rows(D_FF), rows(D_FF), _full((1, D_FF)), _full((1, D_FF)), _full((1, D_FF)),
                  _full((N_CHIPS, D_MODEL, shard)), rows(D_MODEL), _full((1, D_MODEL)), rows(D_MODEL),
                  _full((D_MODEL, D_MODEL))],
        out_specs=[rows(2 * D_FF), rows(D_MODEL), rows(D_MODEL), rows(D_MODEL), _full((1, D_MODEL)),
                   _full((8, D_FF))],
        out_shape=[jax.ShapeDtypeStruct((L, 2 * D_FF), BF16), jax.ShapeDtypeStruct((L, D_MODEL), F32),
                   jax.ShapeDtypeStruct((L, D_MODEL), BF16), jax.ShapeDtypeStruct((L, D_MODEL), BF16),
                   jax.ShapeDtypeStruct((1, D_MODEL), F32), jax.ShapeDtypeStruct((8, D_FF), F32)],
        compiler_params=_params(("arbitrary",)),
    )(dacc, dacc, db, up, cw[0], cw[1], cw[2], w_up, h1, ffn_g, dh2, w_out)


def _wgrad(a, b, name, tn=None, tk=None):
    L, K = a.shape
    N = b.shape[1]
    tn = N if tn is None else tn
    tk = K if tk is None else tk
    tl = _row_tile(L, (1408, 768, 512, 256, 128))

    def body(a_ref, b_ref, o_ref):
        @pl.when(pl.program_id(2) == 0)
        def _():
            o_ref[...] = jnp.zeros_like(o_ref)

        o_ref[0] = o_ref[0] + _dot_tn(a_ref[...], b_ref[...])

    return pl.pallas_call(
        body, name=name, grid=(N // tn, K // tk, L // tl),
        in_specs=[pl.BlockSpec((tl, tk), lambda n, k, l: (l, k)), pl.BlockSpec((tl, tn), lambda n, k, l: (l, n))],
        out_specs=pl.BlockSpec((1, tk, tn), lambda n, k, l: (n, k, 0)),
        out_shape=jax.ShapeDtypeStruct((N // tn, K, tn), F32),
        compiler_params=_params(("parallel", "parallel", "arbitrary")),
    )(a, b)


def _retention_bwd(dmix, o, proj, cos_t, sin_t, ret_g, states, exchange=()):
    L = proj.shape[0]
    nblk = L // BLK
    G = _block_group(nblk)
    steps = nblk // G
    nx = len(exchange)
    dmat, wq_t, wk_t, g_blk = _decay_tables()

    def body(dm_ref, o_ref, q_ref, k_ref, v_ref, gate_ref, cos_ref, sin_ref, d_ref, wq_ref, wk_ref, rg_ref, rs_ref,
             *rest):
        x_in, (dp_ref, drg_ref), x_out, gstate = rest[:nx], rest[nx:nx + 2], rest[nx + 2:2 * nx + 2], rest[2 * nx + 2]

        @pl.when(pl.program_id(0) == 0)
        def _():
            if nx:
                for cp in _sibling_half_copies(x_in, x_out, *rest[2 * nx + 3:])[0]:
                    cp.start()
            gstate[...] = jnp.zeros_like(gstate)
            drg_ref[...] = jnp.zeros_like(drg_ref)

        lane = _iota((BLK, BLK), 1)
        sub = _iota((BLK, BLK), 0)
        scale = HEAD_LANES ** -0.5
        for b in reversed(range(G)):
            rows = slice(b * BLK, (b + 1) * BLK)
            rot, rot_t = _rot_fns(cos_ref[rows, :], sin_ref[rows, :])
            for p in range(2):
                qr = rot(q_ref[rows, p * BLK:(p + 1) * BLK].astype(F32))
                kr = rot(k_ref[rows, p * BLK:(p + 1) * BLK].astype(F32)) * scale
                kr_b = kr.astype(BF16)
                qw = (qr * wq_ref[p]).astype(BF16)
                kw = (kr * wk_ref[p]).astype(BF16)
                dqr = jnp.zeros((BLK, BLK), F32)
                dkr = jnp.zeros((BLK, BLK), F32)
                for e in range(2):
                    h = 2 * p + e
                    cols = slice(h * BLK, (h + 1) * BLK)
                    head_lanes = (lane >> 6) == e
                    o = o_ref[rows, cols]
                    rn = lax.rsqrt(jnp.mean(o * o, axis=-1, keepdims=True) + EPS)
                    y = o * rn
                    gate = gate_ref[rows, cols].astype(F32)
                    sg = jax.nn.sigmoid(gate)
                    dm = dm_ref[rows, cols].astype(F32)
                    rgain = rg_ref[:, cols]
                    drg_ref[:, cols] = drg_ref[:, cols] + jnp.sum(dm * y * (gate * sg), axis=0, keepdims=True)
                    dp_ref[rows, 1024 + h * BLK:1024 + (h + 1) * BLK] = (
                        dm * y * rgain * (sg * (1.0 + gate * (1.0 - sg)))).astype(BF16)
                    dy = dm * rgain * (gate * sg)
                    do = (rn * (dy - y * jnp.mean(dy * y, axis=-1, keepdims=True))).astype(BF16)
                    vh = v_ref[rows, cols]
                    qm = jnp.where(head_lanes, qr, 0.0).astype(BF16)
                    dmh = d_ref[h]
                    s = (_dot_nt(qm, kr_b) * dmh).astype(BF16)
                    ds = (_dot_nt(do, vh) * dmh).astype(BF16)
                    st = rs_ref[b, h].astype(BF16)
                    gs = gstate[h]
                    gs_b = gs.astype(BF16)
                    dqr = dqr + jnp.where(head_lanes, _dot(ds, kr_b), 0.0) + _dot_nt(do, st) * wq_ref[p]
                    dkr = dkr + _dot_tn(ds, qm) + _dot_nt(vh, gs_b) * wk_ref[p]
                    dp_ref[rows, 512 + h * BLK:512 + (h + 1) * BLK] = (_dot_tn(s, do) + _dot(kw, gs_b)).astype(BF16)
                    dr = jnp.where((sub >> 6) == e, _dot_tn(qw, do), 0.0)
                    gstate[h] = dr + g_blk[h] * gs
                dp_ref[rows, p * BLK:(p + 1) * BLK] = rot_t(dqr).astype(BF16)
                dp_ref[rows, 256 + p * BLK:256 + (p + 1) * BLK] = (rot_t(dkr) * scale).astype(BF16)

        if nx:
            @pl.when(pl.program_id(0) == steps - 1)
            def _():
                sends, recvs = _sibling_half_copies(x_in, x_out, *rest[2 * nx + 3:])
                for cp in recvs:
                    cp.wait_recv()
                for cp in sends:
                    cp.wait_send()

    row = lambda c: (lambda i: (steps - 1 - i, c))
    return pl.pallas_call(
        body, name="b_retention", grid=(steps,),
        in_specs=[pl.BlockSpec((G * BLK, 512), row(0)), pl.BlockSpec((G * BLK, 512), row(0)),
                  pl.BlockSpec((G * BLK, 256), row(0)), pl.BlockSpec((G * BLK, 256), row(1)),
                  pl.BlockSpec((G * BLK, 512), row(1)), pl.BlockSpec((G * BLK, 512), row(2)),
                  pl.BlockSpec((G * BLK, BLK), row(0)), pl.BlockSpec((G * BLK, BLK), row(0)),
                  _full((RET_HEADS, BLK, BLK)), _full((2, BLK, BLK)), _full((2, BLK, BLK)), _full((1, 512)),
                  pl.BlockSpec((G, RET_HEADS, BLK, BLK), lambda i: (steps - 1 - i, 0, 0, 0))] + [_ANY] * nx,
        out_specs=[pl.BlockSpec((G * BLK, RET_W), row(0)), _full((1, 512))] + [_ANY] * nx,
        out_shape=[jax.ShapeDtypeStruct((L, RET_W), BF16), jax.ShapeDtypeStruct((1, 512), F32)]
        + _sibling_half_shapes(exchange),
        scratch_shapes=[pltpu.VMEM((RET_HEADS, BLK, BLK), F32)] + _sibling_half_semaphores(nx),
        compiler_params=_params(("arbitrary",)),
    )(dmix, o, proj, proj, proj, proj, cos_t, sin_t, dmat, wq_t, wk_t, ret_g, states, *exchange)


def _fox_delta(dmix, o_f):
    L = o_f.shape[0]
    nblk = L // BLK
    G = _block_group(nblk)

    def body(do_ref, o_ref, d_ref):
        sel = ((_iota((8, 512), 1) >> 6) == _iota((8, 512), 0)).astype(BF16)
        for b in range(G):
            rows = slice(b * BLK, (b + 1) * BLK)
            prod = do_ref[rows, :].astype(F32) * o_ref[rows, :].astype(F32)
            hi = prod.astype(BF16)
            lo = (prod - hi.astype(F32)).astype(BF16)
            d_ref[b] = _dot_nt(sel, hi) + _dot_nt(sel, lo)

    return pl.pallas_call(
        body, name="b_foxdelta", grid=(nblk // G,),
        in_specs=[pl.BlockSpec((G * BLK, 512), lambda i: (i, 1)), pl.BlockSpec((G * BLK, 512), lambda i: (i, 0))],
        out_specs=pl.BlockSpec((G, 8, BLK), lambda i: (i, 0, 0)),
        out_shape=jax.ShapeDtypeStruct((nblk, 8, BLK), F32),
        compiler_params=_params(("parallel",)),
    )(dmix, o_f)


def _fox_bwd(proj, dmix, c, ctb, lse, delta, scatter=()):
    L = proj.shape[0]
    nblk, nu = _fox_units(L)
    scale = HEAD_LANES ** -0.5
    ns = len(scatter)

    steps = FOX_HEADS // (2 * FOX_PAIRS)

    def body(qkv_ref, do_ref, c_ref, ct_ref, lse_ref, dl_ref, *rest):
        s_in, (dp_ref, dc_ref, dcq_ref), s_out = rest[:ns], rest[ns:ns + 3], rest[ns + 3:2 * ns + 3]
        ktt, dqt, dk_acc, dv_acc, dcs_acc = rest[2 * ns + 3:2 * ns + 8]
        p = pl.program_id(0)
        heads = [(pp, e, 2 * FOX_PAIRS * p + 2 * pp + e) for pp in range(FOX_PAIRS) for e in range(2)]

        @pl.when(p == 0)
        def _():
            dc_ref[...] = jnp.zeros_like(dc_ref)
            dcq_ref[...] = jnp.zeros_like(dcq_ref)
            if ns:
                for cp in _scatter_copies(s_in, s_out, *rest[2 * ns + 8:]):
                    cp.start()

        sub8 = _iota((8, BLK), 0)
        masks = _fox_tile_masks()

        def pre(j, carry):
            off = pl.multiple_of(j * BLK, BLK)
            for pp in range(FOX_PAIRS):
                ktt[pp, j] = qkv_ref[pl.ds(off, BLK), pp * 384 + BLK:pp * 384 + 2 * BLK].astype(F32).T.astype(BF16)
                dqt[pp, j] = jnp.zeros((BLK, BLK), F32)
            return carry

        lax.fori_loop(0, nblk, pre, 0)

        def kv_pass(kblk, nk, n_later):
            klen = nk * BLK
            koff = pl.multiple_of(kblk * BLK, BLK)
            kt = [qkv_ref[pl.ds(koff, klen), pp * 384 + BLK:pp * 384 + 2 * BLK] for pp in range(FOX_PAIRS)]
            vtile = [qkv_ref[pl.ds(koff, klen), pp * 384 + 2 * BLK:pp * 384 + 3 * BLK] for pp in range(FOX_PAIRS)]
            ct = c_ref[pl.ds(koff, klen), :]
            klane = _iota((klen, BLK), 1)
            cs = [jnp.broadcast_to(jnp.sum(jnp.where(klane == h, ct, 0.0), axis=1, keepdims=True), (klen, WIDE * UNIT))
                  for _, _, h in heads]
            k_t = [jnp.concatenate([ktt[pp, kblk + b, e * HEAD_LANES:(e + 1) * HEAD_LANES, :] for b in range(nk)], axis=1)
                   for pp, e, _ in heads]
            for pp in range(FOX_PAIRS):
                dk_acc[pp, 0:klen] = jnp.zeros((klen, BLK), F32)
                dv_acc[pp, 0:klen] = jnp.zeros((klen, BLK), F32)
            for hh in range(len(heads)):
                dcs_acc[hh, 0:klen] = jnp.zeros((klen, BLK), F32)

            def tile(qblk, nq, mask):
                qlen = nq * BLK
                if mask == "valid":
                    mask = _iota((klen, qlen), 0) >= N_PAD
                qoff = pl.multiple_of(qblk * BLK, BLK)
                qlane = _iota((qlen, BLK), 1)
                qs = [qkv_ref[pl.ds(qoff, qlen), pp * 384:pp * 384 + BLK].astype(F32) * (scale * LOG2E)
                      for pp in range(FOX_PAIRS)]
                dot_ = [do_ref[pl.ds(qoff, qlen), pp * BLK:(pp + 1) * BLK] for pp in range(FOX_PAIRS)]
                stats = [[ref[qblk + a] for a in range(nq)] for ref in (ct_ref, lse_ref, dl_ref)]
                dcq = [jnp.zeros((8, BLK), F32) for _ in range(nq)]
                for hh, (pp, e, h) in enumerate(heads):
                    head = (qlane >> 6) == e
                    ct_row, lse_row, dl_row = [jnp.concatenate([_pick_row(t, h) for t in ts], axis=1) for ts in stats]
                    qm = jnp.where(head, qs[pp], 0.0).astype(BF16)
                    dom = jnp.where(head, dot_[pp], jnp.zeros_like(dot_[pp]))
                    t = _dot_nt(kt[pp], qm) - cs[hh][:, 0:qlen]
                    if mask is not None:
                        t = jnp.where(mask, t, NEG)
                    pr = jnp.exp2(t + (ct_row - lse_row))
                    dv_acc[pp, 0:klen] = dv_acc[pp, 0:klen] + _dot(pr.astype(BF16), dom)
                    dsv = pr * (_dot_nt(vtile[pp], dom) - dl_row)
                    ds_b = dsv.astype(BF16)
                    dk_acc[pp, 0:klen] = dk_acc[pp, 0:klen] + _dot(ds_b, qm)
                    rows = slice(e * HEAD_LANES, (e + 1) * HEAD_LANES)
                    dq_t = _dot(k_t[hh], ds_b)
                    key_side = dsv[:, 0:BLK]
                    for a in range(1, nq):
                        key_side = key_side + dsv[:, a * BLK:(a + 1) * BLK]
                    dcs_acc[hh, 0:klen] = dcs_acc[hh, 0:klen] + key_side
                    query_side = jnp.sum(dsv, axis=0, keepdims=True)
                    for a in range(nq):
                        cols = slice(a * BLK, (a + 1) * BLK)
                        dqt[pp, qblk + a, rows, :] = dqt[pp, qblk + a, rows, :] + dq_t[:, cols]
                        dcq[a] = dcq[a] + jnp.where(sub8 == h, query_side[:, cols], 0.0)
                for a in range(nq):
                    dcq_ref[qblk + a] = dcq_ref[qblk + a] + dcq[a]

            later_mask = "valid" if nk == 1 else None
            n_later = jnp.asarray(n_later, jnp.int32)
            n_wide = n_later // WIDE

            def later_wide(i, carry):
                tile(kblk + nk + 2 * WIDE * i, 2 * WIDE, later_mask)
                return carry

            tile(kblk, nk, masks["first"] if nk == 1 else masks["diag"])
            lax.fori_loop(0, n_wide, later_wide, 0)
            rest_blk = kblk + nk + 2 * WIDE * n_wide

            @pl.when((n_later & 2) != 0)
            def _():
                tile(rest_blk, 4, later_mask)

            @pl.when((n_later & 1) != 0)
            def _():
                tile(rest_blk + 2 * (n_later & 2), 2, later_mask)

            upd = jnp.zeros((klen, BLK), F32)
            for hh, (_, _, h) in enumerate(heads):
                upd = upd + jnp.where(klane == h, -jnp.sum(dcs_acc[hh, 0:klen], axis=1, keepdims=True), 0.0)
            dc_ref[pl.ds(koff, klen), :] = dc_ref[pl.ds(koff, klen), :] + upd
            for pp in range(FOX_PAIRS):
                dp_ref[pl.ds(koff, klen), pp * 384 + BLK:pp * 384 + 2 * BLK] = (
                    dk_acc[pp, 0:klen] * (1.0 / LOG2E)).astype(BF16)
                dp_ref[pl.ds(koff, klen), pp * 384 + 2 * BLK:pp * 384 + 3 * BLK] = dv_acc[pp, 0:klen].astype(BF16)

        kv_pass(0, 1, nu)

        def k_loop(u, carry):
            kv_pass(1 + 2 * u, 2, nu - 1 - u)
            return carry

        lax.fori_loop(0, nu, k_loop, 0)

        def flush(j, carry):
            off = pl.multiple_of(j * BLK, BLK)
            for pp in range(FOX_PAIRS):
                dp_ref[pl.ds(off, BLK), pp * 384:pp * 384 + BLK] = (dqt[pp, j].T * scale).astype(BF16)
            return carry

        lax.fori_loop(0, nblk, flush, 0)

        if ns:
            @pl.when(p == steps - 1)
            def _():
                copies = _scatter_copies(s_in, s_out, *rest[2 * ns + 8:])
                for cp in copies:
                    cp.wait_recv()
                for cp in copies:
                    cp.wait_send()

    width = 384 * FOX_PAIRS
    once = lambda shape, index: pl.BlockSpec(shape, index, pipeline_mode=pl.Buffered(1))
    stat = once((nblk, 8, BLK), lambda p: (0, 0, 0))
    return pl.pallas_call(
        body, name="b_fox", grid=(steps,),
        in_specs=[once((L, width), lambda p: (0, RET_W // width + p)),
                  once((L, FOX_PAIRS * BLK), lambda p: (0, 4 // FOX_PAIRS + p)),
                  once((L, BLK), lambda p: (0, 0)), stat, stat, stat] + [_ANY] * ns,
        out_specs=[pl.BlockSpec((L, width), lambda p: (0, p)), _full((L, BLK)), _full((nblk, 8, BLK))] + [_ANY] * ns,
        out_shape=[jax.ShapeDtypeStruct((L, FOX_W), BF16), jax.ShapeDtypeStruct((L, BLK), F32),
                   jax.ShapeDtypeStruct((nblk, 8, BLK), F32)] + _scatter_shapes(scatter),
        scratch_shapes=[pltpu.VMEM((FOX_PAIRS, nblk, BLK, BLK), BF16), pltpu.VMEM((FOX_PAIRS, nblk, BLK, BLK), F32),
                        pltpu.VMEM((FOX_PAIRS, UNIT, BLK), F32), pltpu.VMEM((FOX_PAIRS, UNIT, BLK), F32),
                        pltpu.VMEM((2 * FOX_PAIRS, UNIT, BLK), F32)]
        + _scatter_semaphores(ns),
        compiler_params=_params(("arbitrary",)),
    )(proj, dmix, c, ctb, lse, delta, *scatter)


def _fox_post(dc, dcq, ff, fb):
    L = dc.shape[0]
    nblk = L // BLK
    G = _block_group(nblk)
    steps = nblk // G

    def body(dc_ref, dcq_ref, ff_ref, b_ref, dff_ref, dffb_ref, dfb_ref, carry):
        @pl.when(pl.program_id(0) == 0)
        def _():
            carry[...] = jnp.zeros_like(carry)
            dfb_ref[...] = jnp.zeros_like(dfb_ref)

        tri = (_iota((BLK, BLK), 0) <= _iota((BLK, BLK), 1)).astype(BF16)
        live = _iota((BLK, BLK), 1) < FOX_HEADS
        run, dfb = carry[...], dfb_ref[...]
        for b in reversed(range(G)):
            rows = slice(b * BLK, (b + 1) * BLK)
            d = dc_ref[rows, :] + jnp.concatenate([dcq_ref[b], jnp.zeros((BLK - 8, BLK), F32)], axis=0).T
            hi, mid, lo = _split3(d)
            dlf = _dot(tri, hi) + _dot(tri, mid) + _dot(tri, lo) + run
            run = run + jnp.sum(d, axis=0, keepdims=True)
            z = ff_ref[rows, :] + b_ref[...]
            dff = jnp.where(live, dlf * jax.nn.sigmoid(-z), 0.0)
            dff_ref[rows, :] = dff
            dffb_ref[rows, :] = dff.astype(BF16)
            dfb = dfb + jnp.sum(dff, axis=0, keepdims=True)
        carry[...] = run
        dfb_ref[...] = dfb

    rev = lambda i: (steps - 1 - i, 0)
    return pl.pallas_call(
        body, name="b_foxpost", grid=(steps,),
        in_specs=[pl.BlockSpec((G * BLK, BLK), rev), pl.BlockSpec((G, 8, BLK), lambda i: (steps - 1 - i, 0, 0)),
                  pl.BlockSpec((G * BLK, BLK), rev), _full((1, BLK))],
        out_specs=[pl.BlockSpec((G * BLK, BLK), rev), pl.BlockSpec((G * BLK, BLK), rev), _full((1, BLK))],
        out_shape=[jax.ShapeDtypeStruct((L, BLK), F32), jax.ShapeDtypeStruct((L, BLK), BF16),
                   jax.ShapeDtypeStruct((1, BLK), F32)],
        scratch_shapes=[pltpu.VMEM((1, BLK), F32)],
        compiler_params=_params(("arbitrary",)),
    )(dc, dcq, ff, fb)


def _inproj_bwd(dpr, dpf, dffb, w_main, w_ff, h0, g, dh1, scatter=()):
    L = h0.shape[0]
    S = L - BLK
    tm = _row_tile(S, (512, 256, 128))
    nt = S // tm
    ns = len(scatter)
    operands = (dpr, dpf, dffb, h0, dh1)

    def body(*refs):
        lead, tile = refs[0:5], refs[5:10]
        wm_ref, wf_ref, g_ref = refs[10:13]
        rest = refs[13:]
        s_in, (dlead_ref, dx_ref, dg_ref), s_out = rest[:ns], rest[ns:ns + 3], rest[ns + 3:2 * ns + 3]
        i = pl.program_id(0)

        def rows_bwd(dpr_ref, dpf_ref, dff_ref, h_ref, dh1_ref):
            dn = (_dot_nt(dpr_ref[...], wm_ref[:, 0:RET_W]) + _dot_nt(dpf_ref[...], wm_ref[:, RET_W:MAIN_W])
                  + _dot_nt(dff_ref[...], wf_ref[...]))
            h = h_ref[...]
            r = lax.rsqrt(jnp.mean(h * h, axis=-1, keepdims=True) + EPS)
            yn = h * r
            dyn = dn * g_ref[...]
            dh0 = dh1_ref[...] + r * (dyn - yn * jnp.mean(dyn * yn, axis=-1, keepdims=True))
            return dh0, jnp.sum(dn * yn, axis=0, keepdims=True)

        @pl.when(i == 0)
        def _():
            if ns:
                for cp in _scatter_copies(s_in, s_out, *rest[2 * ns + 3:]):
                    cp.start()
            dlead_ref[...], dg_ref[...] = rows_bwd(*lead)

        dx_ref[...], dg_tile = rows_bwd(*tile)
        dg_ref[...] = dg_ref[...] + dg_tile

        if ns:
            @pl.when(i == nt - 1)
            def _():
                copies = _scatter_copies(s_in, s_out, *rest[2 * ns + 3:])
                for cp in copies:
                    cp.wait_recv()
                for cp in copies:
                    cp.wait_send()

    lead_spec = lambda a: pl.BlockSpec((BLK, a.shape[1]), lambda i: (0, 0))
    tile_spec = lambda a: pl.BlockSpec((pl.Element(tm), pl.Element(a.shape[1])),
                                       lambda i: (pl.multiple_of(BLK + i * tm, BLK), 0))
    return pl.pallas_call(
        body, name="b_inproj", grid=(nt,),
        in_specs=[lead_spec(a) for a in operands] + [tile_spec(a) for a in operands]
        + [_full((D_MODEL, MAIN_W)), _full((D_MODEL, BLK)), _full((1, D_MODEL))] + [_ANY] * ns,
        out_specs=[_full((BLK, D_MODEL)), pl.BlockSpec((tm, D_MODEL), lambda i: (i, 0)), _full((1, D_MODEL))]
        + [_ANY] * ns,
        out_shape=[jax.ShapeDtypeStruct((BLK, D_MODEL), F32), jax.ShapeDtypeStruct((S, D_MODEL), F32),
                   jax.ShapeDtypeStruct((1, D_MODEL), F32)] + _scatter_shapes(scatter),
        scratch_shapes=_scatter_semaphores(ns),
        compiler_params=_params(("arbitrary",)),
    )(*operands, *operands, w_main, w_ff, g, *scatter)


def _local_step(x, target, meta, attn_g, w_main, w_ff, fox_b, ret_g, w_out, ffn_g, w_up, conv_w, conv_b, w_down, final_g,
                late=None, mid=None, last=None):
    S = x.shape[0]
    L = S + PREFIX
    head = jnp.concatenate([jnp.zeros((N_PAD, D_MODEL), F32), meta], axis=0)
    fb = jnp.pad(fox_b, ((0, 0), (0, BLK - FOX_HEADS)))
    cos_t, sin_t = _rotary_tables(L)

    h0, n1, proj, ff = _rms_inproj(head, x, attn_g, w_main, w_ff)
    c, ctb = _fox_prep(ff, fb)
    mix_r, o_ret, states = _retention_fwd(proj, cos_t, sin_t, ret_g)
    if late is None:
        o_f, lse = _fox_fwd(proj, c, ctb)
    else:
        o_f, lse, *gathered = _fox_fwd(proj, c, ctb, gather=late[0])
        w_out, w_up, w_down = late[1](gathered)
    h1, n2, up, g_act, acc_saved = _outproj_up(mix_r, o_f, h0, w_out, ffn_g, w_up, conv_w, conv_b)
    dh2, dh2b, d_final_g, loss = _ffn_down_loss(g_act, w_down, h1, final_g, target)

    dacc, db = _ffn_bwd_gate(dh2b, w_down, acc_saved, up)
    dup, dh1, dh1b, dmix, d_ffn_g, dconv = _ffn_bwd_up(dacc, db, up, conv_w, w_up, h1, ffn_g, dh2, w_out)
    d_w_down = _wgrad(g_act, dh2b, "wgrad_down", tk=D_FF // 2)[0]
    d_w_up = _wgrad(n2, dup, "wgrad_up", tn=w_up.shape[2])
    d_w_out = jnp.concatenate([_wgrad(mix_r, dh1b, "wgrad_out_r")[0], _wgrad(o_f, dh1b, "wgrad_out_f")[0]], axis=0)

    early = () if mid is None else mid[0](d_w_out, d_w_up, d_w_down)
    dpr, d_ret_g, *from_sibling = _retention_bwd(dmix, o_ret, proj, cos_t, sin_t, ret_g, states, exchange=early)
    delta = _fox_delta(dmix, o_f)
    scatter = () if mid is None else mid[1](early, from_sibling)
    dpf, dc, dcq, *received = _fox_bwd(proj, dmix, c, ctb, lse, delta, scatter=scatter)
    dff, dffb, d_fox_b = _fox_post(dc, dcq, ff, fb)
    d_w_main = jnp.concatenate([_wgrad(n1, dpr, "wgrad_in_r")[0], _wgrad(n1, dpf, "wgrad_in_f")[0]], axis=1)
    d_w_ff = _wgrad(n1, dffb, "wgrad_in_ff")[0][:, :FOX_HEADS]
    scatter_in = () if last is None else last(d_w_main, d_w_ff)
    dlead, dx, d_attn_g, *received_in = _inproj_bwd(dpr, dpf, dffb, w_main, w_ff, h0, attn_g, dh1, scatter=scatter_in)

    return dict(
        loss=loss[0, 0], dx=dx, dmeta=dlead[N_PAD:], attn_g=d_attn_g, w_main=d_w_main,
        w_ff=d_w_ff, fox_b=d_fox_b[:, :FOX_HEADS], ret_g=d_ret_g, w_out=d_w_out, ffn_g=d_ffn_g,
        w_up=d_w_up, conv_w=dconv[0:3], conv_b=dconv[3:4], w_down=d_w_down, final_g=d_final_g,
        scatter=list(scatter_in) + list(scatter), received=list(received_in) + list(received))


_ANY = pl.BlockSpec(memory_space=pl.ANY)


def _place():
    return lax.axis_index("x"), lax.axis_index("y"), lax.axis_index("c")


def _other_chips(x, y):
    return [(1 - x, y), (x, 1 - y), (1 - x, 1 - y)]


def _allgather_semaphores(n):
    if n == 0:
        return []
    return [pltpu.SemaphoreType.DMA((3 * n,)), pltpu.SemaphoreType.DMA((3 * n,)), pltpu.SemaphoreType.DMA((n,))]


def _allgather_copies(ins, outs, send, recv, loc):
    n = len(ins)
    x, y, c = _place()
    mine = 2 * x + y
    peers = _other_chips(x, y)

    def remote(a, k, slot):
        return pltpu.make_async_remote_copy(
            src_ref=ins[a], dst_ref=outs[a].at[slot], send_sem=send.at[3 * a + k], recv_sem=recv.at[3 * a + k],
            device_id=(peers[k][0], peers[k][1], c), device_id_type=MESH)

    local = [pltpu.make_async_copy(ins[a], outs[a].at[mine], loc.at[a]) for a in range(n)]
    sends = [remote(a, k, mine) for a in range(n) for k in range(3)]
    recvs = [remote(a, k, 2 * peers[k][0] + peers[k][1]) for a in range(n) for k in range(3)]
    return local, sends, recvs


def _chip_allgather_halves(w, small):
    half = w.shape[0] // 2

    def body(w_ref, s_ref, wo_ref, so_ref, send, recv, fsend, frecv, ssend, srecv, loc):
        x, y, c = _place()
        mine = 2 * x + y
        peers = _other_chips(x, y)

        def fetch(k, slot):
            return pltpu.make_async_remote_copy(
                src_ref=w_ref.at[pl.ds(c * half, half)], dst_ref=wo_ref.at[slot, c], send_sem=send.at[k],
                recv_sem=recv.at[k], device_id=(peers[k][0], peers[k][1], c), device_id_type=MESH)

        def forward(k, which):
            slot = 2 * peers[k][0] + peers[k][1]
            return pltpu.make_async_remote_copy(
                src_ref=wo_ref.at[slot, which], dst_ref=wo_ref.at[slot, which], send_sem=fsend.at[k],
                recv_sem=frecv.at[k], device_id=(x, y, 1 - c), device_id_type=MESH)

        def small_copy(k, slot):
            return pltpu.make_async_remote_copy(
                src_ref=s_ref, dst_ref=so_ref.at[slot], send_sem=ssend.at[k], recv_sem=srecv.at[k],
                device_id=(peers[k][0], peers[k][1], c), device_id_type=MESH)

        local = pltpu.make_async_copy(s_ref, so_ref.at[mine], loc.at[0])
        sends = [fetch(k, mine) for k in range(3)] + [small_copy(k, mine) for k in range(3)]
        local.start()
        for cp in sends:
            cp.start()
        forwards = []
        for k in range(3):
            fetch(k, 2 * peers[k][0] + peers[k][1]).wait_recv()
            forwards.append(forward(k, c))
            forwards[-1].start()
        for k in range(3):
            forward(k, 1 - c).wait_recv()
            small_copy(k, 2 * peers[k][0] + peers[k][1]).wait_recv()
        for cp in sends + forwards:
            cp.wait_send()
        local.wait()

    three = pltpu.SemaphoreType.DMA((3,))
    return pl.pallas_call(
        body, name="ag_weights", in_specs=[_ANY] * 2, out_specs=[_ANY] * 2,
        out_shape=[jax.ShapeDtypeStruct((N_CHIPS, 2, half, w.shape[1]), w.dtype),
                   jax.ShapeDtypeStruct((N_CHIPS,) + small.shape, small.dtype)],
        scratch_shapes=[three, three, three, three, three, three, pltpu.SemaphoreType.DMA((1,))],
    )(w, small)


def _chip_allgather(arrays):
    n = len(arrays)

    def body(*refs):
        local, sends, recvs = _allgather_copies(refs[:n], refs[n:2 * n], *refs[2 * n:])
        for cp in local + sends:
            cp.start()
        for cp in recvs:
            cp.wait_recv()
        for cp in sends:
            cp.wait_send()
        for cp in local:
            cp.wait()

    return pl.pallas_call(
        body, name="ag_weights", in_specs=[_ANY] * n, out_specs=[_ANY] * n,
        out_shape=[jax.ShapeDtypeStruct((N_CHIPS,) + a.shape, a.dtype) for a in arrays],
        scratch_shapes=_allgather_semaphores(n),
    )(*arrays)


def _sibling_halves(grads):
    n = len(grads)

    def body(*refs):
        sends, recvs = _sibling_half_copies(refs[:n], refs[n:2 * n], *refs[2 * n:])
        for cp in sends:
            cp.start()
        for cp in recvs:
            cp.wait_recv()
        for cp in sends:
            cp.wait_send()

    return pl.pallas_call(
        body, name="rs_sibling", in_specs=[_ANY] * n, out_specs=[_ANY] * n,
        out_shape=_sibling_half_shapes(grads), scratch_shapes=_sibling_half_semaphores(n),
    )(*grads)


def _sibling_half_shapes(grads):
    return [jax.ShapeDtypeStruct((N_CHIPS, g.shape[1] // 2, g.shape[2]), g.dtype) for g in grads]


def _sibling_half_semaphores(n):
    return [pltpu.SemaphoreType.DMA((n,)), pltpu.SemaphoreType.DMA((n,))] if n else []


def _sibling_half_copies(ins, outs, send, recv):
    x, y, c = _place()

    def half_copy(a, which):
        half = ins[a].shape[1] // 2
        return pltpu.make_async_remote_copy(
            src_ref=ins[a].at[pl.ds(0, N_CHIPS), pl.ds(which * half, half)], dst_ref=outs[a],
            send_sem=send.at[a], recv_sem=recv.at[a], device_id=(x, y, 1 - c), device_id_type=MESH)

    return [half_copy(a, 1 - c) for a in range(len(ins))], [half_copy(a, c) for a in range(len(ins))]


def _scatter_shapes(parts):
    return [jax.ShapeDtypeStruct((3,) + p.shape[1:], p.dtype) for p in parts]


def _scatter_semaphores(n):
    return [pltpu.SemaphoreType.DMA((3 * n,)), pltpu.SemaphoreType.DMA((3 * n,))] if n else []


def _scatter_copies(ins, outs, send, recv):
    x, y, c = _place()
    peers = _other_chips(x, y)
    return [pltpu.make_async_remote_copy(
        src_ref=ins[a].at[2 * peers[k][0] + peers[k][1]], dst_ref=outs[a].at[k], send_sem=send.at[3 * a + k],
        recv_sem=recv.at[3 * a + k], device_id=(peers[k][0], peers[k][1], c), device_id_type=MESH)
        for a in range(len(ins)) for k in range(3)]


def _sibling_allgather(bufs, small):
    n = len(bufs)

    def body(*refs):
        small_in, outs, small_out = refs[n], refs[n + 1:2 * n + 1], refs[2 * n + 1]
        send, recv, s_send, s_recv, loc = refs[2 * n + 2:]
        x, y, c = _place()
        me = 4 * x + 2 * y + c

        def remote(a, which):
            return pltpu.make_async_remote_copy(
                src_ref=outs[a].at[which], dst_ref=outs[a].at[which], send_sem=send.at[a], recv_sem=recv.at[a],
                device_id=(x, y, 1 - c), device_id_type=MESH)

        def peer_of(r):
            return tuple(1 - v if (r >> b) & 1 else v for v, b in ((x, 2), (y, 1), (c, 0)))

        def small_copy(r, slot):
            return pltpu.make_async_remote_copy(
                src_ref=small_in, dst_ref=small_out.at[slot], send_sem=s_send.at[r - 1], recv_sem=s_recv.at[r - 1],
                device_id=peer_of(r), device_id_type=MESH)

        local = pltpu.make_async_copy(small_in, small_out.at[me], loc.at[0])
        sends = [remote(a, c) for a in range(n)] + [small_copy(r, me) for r in range(1, N_DEV)]
        local.start()
        for cp in sends:
            cp.start()
        for r in range(1, N_DEV):
            px, py, pc = peer_of(r)
            small_copy(r, 4 * px + 2 * py + pc).wait_recv()
        for a in range(n):
            remote(a, 1 - c).wait_recv()
        for cp in sends:
            cp.wait_send()
        local.wait()

    outs = pl.pallas_call(
        body, name="ag_sibling", in_specs=[_ANY] * (n + 1), out_specs=[_ANY] * (n + 1),
        out_shape=[jax.ShapeDtypeStruct(b.shape, b.dtype) for b in bufs]
        + [jax.ShapeDtypeStruct((N_DEV,) + small.shape, small.dtype)],
        input_output_aliases={a: a for a in range(n)},
        scratch_shapes=[pltpu.SemaphoreType.DMA((n,)), pltpu.SemaphoreType.DMA((n,)),
                        pltpu.SemaphoreType.DMA((N_DEV - 1,)), pltpu.SemaphoreType.DMA((N_DEV - 1,)),
                        pltpu.SemaphoreType.DMA((1,))],
    )(*bufs, small)
    return [o.reshape(2 * o.shape[1], o.shape[2]) for o in outs[:n]], outs[n]


def _pair_add(full, recv, core, name):
    _, R, C = full.shape
    half = R // 2

    def body(core_ref, a_ref, b_ref, o_ref):
        o_ref[...] = (a_ref[...] + b_ref[...]).astype(BF16)

    return pl.pallas_call(
        body, name=name,
        grid_spec=pltpu.PrefetchScalarGridSpec(
            num_scalar_prefetch=1, grid=(N_CHIPS,),
            in_specs=[pl.BlockSpec((1, half, C), lambda j, core_ref: (j, core_ref[0], 0)),
                      pl.BlockSpec((1, half, C), lambda j, core_ref: (j, 0, 0))],
            out_specs=pl.BlockSpec((1, half, C), lambda j, core_ref: (j, 0, 0))),
        out_shape=jax.ShapeDtypeStruct((N_CHIPS, half, C), BF16),
        compiler_params=_params(("parallel",)),
    )(core, full, recv)


def _sum_slots(q, name, tiles=2):
    n, R, C = q.shape
    tr = R // tiles

    def body(q_ref, o_ref):
        acc = q_ref[0].astype(F32)
        for j in range(1, n):
            acc = acc + q_ref[j].astype(F32)
        o_ref[...] = acc

    return pl.pallas_call(
        body, name=name, grid=(tiles,),
        in_specs=[pl.BlockSpec((n, tr, C), lambda i: (0, i, 0))],
        out_specs=pl.BlockSpec((tr, C), lambda i: (i, 0)),
        out_shape=jax.ShapeDtypeStruct((R, C), F32),
        compiler_params=_params(("parallel",)),
    )(q)


def _sum_partials(own_all, recv, place, name, tiles=2):
    _, R, C = own_all.shape
    tr = R // tiles

    def body(place_ref, own_ref, r_ref, o_ref):
        acc = own_ref[0].astype(F32)
        for k in range(3):
            acc = acc + r_ref[k].astype(F32)
        o_ref[0] = acc

    return pl.pallas_call(
        body, name=name,
        grid_spec=pltpu.PrefetchScalarGridSpec(
            num_scalar_prefetch=1, grid=(tiles,),
            in_specs=[pl.BlockSpec((1, tr, C), lambda i, place_ref: (place_ref[0], i, 0)),
                      pl.BlockSpec((3, tr, C), lambda i, place_ref: (0, i, 0))],
            out_specs=pl.BlockSpec((1, tr, C), lambda i, place_ref: (place_ref[1], i, 0))),
        out_shape=jax.ShapeDtypeStruct((2, R, C), F32),
        compiler_params=_params(("parallel",)),
    )(place, own_all, recv)


def _adamw(w, g, m, v, name, tiles=4):
    R, C = w.shape
    tr = R // tiles

    def body(w_ref, g_ref, m_ref, v_ref, go_ref, d_ref, m2_ref, v2_ref):
        g_ = g_ref[...]
        go_ref[...] = g_
        m2 = ADAM_B1 * m_ref[...] + (1.0 - ADAM_B1) * g_
        v2 = ADAM_B2 * v_ref[...] + (1.0 - ADAM_B2) * (g_ * g_)
        m_hat = m2 / (1.0 - ADAM_B1 ** ADAM_STEP)
        v_hat = v2 / (1.0 - ADAM_B2 ** ADAM_STEP)
        d_ref[...] = -ADAM_LR * (m_hat / (jnp.sqrt(v_hat) + ADAM_EPS) + ADAM_WD * w_ref[...])
        m2_ref[...] = m2
        v2_ref[...] = v2

    spec = pl.BlockSpec((tr, C), lambda i: (i, 0))
    return pl.pallas_call(
        body, name=name, grid=(tiles,), in_specs=[spec] * 4, out_specs=[spec] * 4,
        out_shape=[jax.ShapeDtypeStruct((R, C), F32)] * 4,
        compiler_params=_params(("parallel",)),
    )(w, g, m, v)


def _pack_rows(pieces, rows):
    flat = jnp.concatenate([jnp.pad(p.reshape(-1).astype(F32), (0, (-p.size) % D_MODEL)) for p in pieces])
    return jnp.pad(flat, (0, rows * D_MODEL - flat.size)).reshape(rows, D_MODEL)


def _unpack_rows(pack, shapes):
    flat = pack.reshape(-1)
    out, off = [], 0
    for shp in shapes:
        size = int(np.prod(shp))
        out.append(flat[off:off + size].reshape(shp))
        off += size + (-size) % D_MODEL
    return out


def _kernel_order(w):
    parts = [w[:, 0:RET_W]]
    for p in range(FOX_HEADS // 2):
        parts += [w[:, RET_W + part * 512 + p * BLK:RET_W + part * 512 + (p + 1) * BLK] for part in range(3)]
    return jnp.concatenate(parts, axis=1)


def _reference_order(g_main, g_ff):
    parts = [g_main[:, 0:RET_W]]
    for part in range(3):
        parts += [g_main[:, RET_W + 384 * p + part * BLK:RET_W + 384 * p + (part + 1) * BLK] for p in range(FOX_HEADS // 2)]
    return jnp.concatenate(parts + [g_ff], axis=1)


def kernel(x, meta_tokens, attn_norm_g, w_in, fox_forget_b, ret_norm_g, w_out, ffn_norm_g, w_up, conv_w, conv_b, w_down, final_norm_g, loss_target, m_meta_tokens, m_attn_norm_g, m_w_in, m_fox_forget_b, m_ret_norm_g, m_w_out, m_ffn_norm_g, m_w_up, m_conv_w, m_conv_b, m_w_down, m_final_norm_g, v_meta_tokens, v_attn_norm_g, v_w_in, v_fox_forget_b, v_ret_norm_g, v_w_out, v_ffn_norm_g, v_w_up, v_conv_w, v_conv_b, v_w_down, v_final_norm_g):
    chip = 2 * lax.axis_index("x") + lax.axis_index("y")
    core = lax.axis_index("c")
    meta_w, conv_sw = meta_tokens.shape[1], conv_w.shape[2]

    small_w = _pack_rows([meta_tokens, conv_w[0]], 8)
    w_in_b = w_in[0].astype(BF16)
    g_in, g_small = _chip_allgather_halves(w_in_b, small_w)
    g_in = lax.dynamic_update_slice(g_in.reshape((N_CHIPS,) + w_in_b.shape), w_in_b[None], (chip, 0, 0))
    w_in_full = g_in.transpose(1, 0, 2).reshape(D_MODEL, IN_WIDTH)
    w_main = _kernel_order(w_in_full)
    w_ff = jnp.pad(w_in_full[:, MAIN_W:], ((0, 0), (0, BLK - FOX_HEADS)))
    small_parts = [_unpack_rows(g_small[j], [meta_tokens.shape, conv_w.shape[1:]]) for j in range(N_CHIPS)]
    meta_full = jnp.concatenate([sp[0] for sp in small_parts], axis=1)
    conv_w_full = jnp.concatenate([sp[1] for sp in small_parts], axis=1)

    core_idx = core.reshape(1).astype(jnp.int32)
    place = jnp.stack([chip, core]).astype(jnp.int32)

    def assemble(gathered):
        g_out, g_up, g_down = gathered
        return g_out.reshape(D_MODEL, D_MODEL), g_up, g_down.reshape(D_FF, D_MODEL)

    def early_arrays(d_w_out, d_w_up, d_w_down):
        return [d_w_out.reshape(N_CHIPS, -1, D_MODEL), d_w_up, d_w_down.reshape(N_CHIPS, -1, D_MODEL)]

    def in_sums(d_w_main, d_w_ff):
        g_in_full = _reference_order(d_w_main, d_w_ff).reshape(D_MODEL, N_CHIPS, -1).transpose(1, 0, 2)
        (from_sib,) = _sibling_halves([g_in_full])
        return [_pair_add(g_in_full, from_sib, core_idx, "pair_add_in")]

    def early_sums(early, from_sib):
        return [_pair_add(g, r, core_idx, "pair_add_" + nm) for g, r, nm in zip(early, from_sib, ("out", "up", "down"))]

    out = _local_step(x[0], loss_target[0], meta_full, attn_norm_g, w_main, w_ff, fox_forget_b, ret_norm_g,
                      None, ffn_norm_g, None, conv_w_full, conv_b, None, final_norm_g[None],
                      late=([w_out[0].astype(BF16), w_up[0].astype(BF16), w_down[0].astype(BF16)], assemble),
                      mid=(early_arrays, early_sums), last=in_sums)

    small_shapes = [(1, D_MODEL), (1, D_MODEL), (1, D_MODEL), (1, 512 + FOX_HEADS + 1), (1, D_FF), (N_META, D_MODEL), (3, D_FF)]
    small = _pack_rows([out["attn_g"], out["ffn_g"], out["final_g"],
                        jnp.concatenate([out["ret_g"], out["fox_b"], out["loss"].reshape(1, 1)], axis=1),
                        out["conv_b"], out["dmeta"], out["conv_w"]], 32)
    names = ("in", "out", "up", "down")
    totals = [_sum_partials(s, q, place, "sum_chips_" + nm) for s, q, nm in zip(out["scatter"], out["received"], names)]
    (grad_in, grad_out, grad_up, grad_down), small_all = _sibling_allgather(totals, small)
    s_attn, s_ffn, s_final, s_misc, s_conv_b, s_meta, s_conv_w = _unpack_rows(
        _sum_slots(small_all, "sum_small", tiles=1), small_shapes)
    loss = s_misc[0, 512 + FOX_HEADS]
    small_grads = [lax.dynamic_slice_in_dim(s_meta, chip * meta_w, meta_w, axis=1), s_attn, s_misc[:, 512:512 + FOX_HEADS],
                   s_misc[:, :512], s_ffn, lax.dynamic_slice_in_dim(s_conv_w, chip * conv_sw, conv_sw, axis=1)[None],
                   s_conv_b, s_final[0]]

    big_w = [(w_in, m_w_in, v_w_in, grad_in, "adamw_in"), (w_out, m_w_out, v_w_out, grad_out, "adamw_out"),
             (w_up, m_w_up, v_w_up, grad_up, "adamw_up"), (w_down, m_w_down, v_w_down, grad_down, "adamw_down")]
    big_res = [[r[None] for r in _adamw(w[0], g, m[0], v[0], nm)] for w, m, v, g, nm in big_w]
    small_w_list = [meta_tokens, attn_norm_g, fox_forget_b, ret_norm_g, ffn_norm_g, conv_w, conv_b, final_norm_g]
    small_m = [m_meta_tokens, m_attn_norm_g, m_fox_forget_b, m_ret_norm_g, m_ffn_norm_g, m_conv_w, m_conv_b, m_final_norm_g]
    small_v = [v_meta_tokens, v_attn_norm_g, v_fox_forget_b, v_ret_norm_g, v_ffn_norm_g, v_conv_w, v_conv_b, v_final_norm_g]
    shapes = [a.shape for a in small_w_list]
    packs = [_pack_rows(lst, 16) for lst in (small_w_list, small_grads, small_m, small_v)]
    small_res = [_unpack_rows(r, shapes) for r in _adamw(*packs, "adamw_small", tiles=1)[1:]]
    small_grads = [g.reshape(s) for g, s in zip(small_grads, shapes)]

    def ordered(kind):
        sm = small_grads if kind == 0 else small_res[kind - 1]
        bg = [r[kind] for r in big_res]
        return [sm[0], sm[1], bg[0], sm[2], sm[3], bg[1], sm[4], bg[2], sm[5], sm[6], bg[3], sm[7]]

    return (loss, out["dx"][None], *ordered(0), *ordered(1), *ordered(2), *ordered(3))
```

```python
import functools

import numpy as np
import jax
import jax.numpy as jnp
from jax import lax
from jax.experimental import pallas as pl
from jax.experimental.pallas import tpu as pltpu

F32 = jnp.float32
BF16 = jnp.bfloat16

D_MODEL = 1024
N_META = 16
BLK = 128
UNIT = 2 * BLK
FOX_PAIRS = 2
WIDE = 4
CHUNK = 64
N_PAD = BLK - N_META
PREFIX = BLK
RET_HEADS = 4
FOX_HEADS = 8
HEAD_LANES = 64
D_FF = 2816
ROPE_BASE = 10000.0
EPS = 1e-6
NEG = -1e30
LOG2E = 1.4426950408889634
RET_W = 1536
FOX_W = 1536
MAIN_W = RET_W + FOX_W
IN_WIDTH = MAIN_W + FOX_HEADS
N_CHIPS = 4
N_DEV = 8

ADAM_LR = 0.001
ADAM_B1 = 0.9
ADAM_B2 = 0.999
ADAM_EPS = 1e-08
ADAM_WD = 0.01
ADAM_STEP = 10

MESH = pl.DeviceIdType.MESH
VMEM_LIMIT_MB = 56

_NT = (((1,), (1,)), ((), ()))
_TN = (((0,), (0,)), ((), ()))


def _dot(a, b):
    return jnp.dot(a, b, preferred_element_type=F32)


def _dot_nt(a, b):
    return lax.dot_general(a, b, _NT, preferred_element_type=F32)


def _dot_tn(a, b):
    return lax.dot_general(a, b, _TN, preferred_element_type=F32)


def _params(dims=None, vmem_mb=VMEM_LIMIT_MB):
    kw = dict(vmem_limit_bytes=vmem_mb << 20)
    if dims is not None:
        kw["dimension_semantics"] = dims
    return pltpu.CompilerParams(**kw)


def _row_tile(n, prefs=(384, 256, 128)):
    for t in prefs:
        if n % t == 0:
            return t
    raise ValueError(f"no row tile for {n}")


def _iota(shape, dim):
    return lax.broadcasted_iota(jnp.int32, shape, dim)


def _pick_row(tile, row):
    sub = _iota(tile.shape, 0)
    return jnp.sum(jnp.where(sub == row, tile, 0.0), axis=0, keepdims=True)


def _split3(x):
    hi = x.astype(BF16)
    r1 = x - hi.astype(F32)
    mid = r1.astype(BF16)
    lo = (r1 - mid.astype(F32)).astype(BF16)
    return hi, mid, lo


def _full(shape):
    nd = len(shape)
    return pl.BlockSpec(shape, lambda *_: (0,) * nd)


def _in_perm():
    cols = list(range(RET_W))
    for p in range(FOX_HEADS // 2):
        for part in range(3):
            start = RET_W + part * 512 + p * BLK
            cols += list(range(start, start + BLK))
    return np.asarray(cols, np.int32)


def _rotary_tables(L):
    half = HEAD_LANES // 2
    inv = 1.0 / (ROPE_BASE ** (jnp.arange(half, dtype=F32) / half))
    ang = jnp.arange(L).astype(F32)[:, None] * inv[None, :]
    cos, sin = jnp.cos(ang), jnp.sin(ang)
    cos_t = jnp.tile(cos, (1, 4))
    sin_t = jnp.tile(jnp.concatenate([-sin, sin], axis=1), (1, 2))
    return cos_t, sin_t


def _decay_tables():
    gam = 1.0 - 2.0 ** (-5.0 - np.arange(RET_HEADS, dtype=np.float64))
    n = np.arange(BLK)
    same_or_past = (n[:, None] // CHUNK) >= (n[None, :] // CHUNK)
    dist = np.abs(n[:, None] - n[None, :])
    dmat = np.stack([np.where(same_or_past, g ** dist, 0.0) for g in gam]).astype(np.float32)
    lane_head = np.arange(BLK) // HEAD_LANES
    wq = np.stack([gam[2 * p + lane_head][None, :] ** (n[:, None] + 1.0) for p in range(2)]).astype(np.float32)
    wk = np.stack([gam[2 * p + lane_head][None, :] ** (BLK - 1.0 - n[:, None]) for p in range(2)]).astype(np.float32)
    g_blk = tuple(float(g ** BLK) for g in gam)
    return jnp.asarray(dmat), jnp.asarray(wq), jnp.asarray(wk), g_blk


def _shifted_blocks(tm):
    nb = tm // BLK
    return [pl.BlockSpec((BLK, D_MODEL), lambda i, j=j: (jnp.maximum(nb * i + j - 1, 0), 0)) for j in range(nb)]


def _rms_inproj(head, x, g, w_main, w_ff):
    L = x.shape[0] + BLK
    tm = _row_tile(L)
    nb = tm // BLK

    def body(head_ref, *refs):
        x_refs, (g_ref, wm_ref, wf_ref, h_ref, n_ref, p_ref, ff_ref) = refs[:nb], refs[nb:]
        parts = [r[...] for r in x_refs]
        parts[0] = jnp.where(pl.program_id(0) == 0, head_ref[...], parts[0])
        h = jnp.concatenate(parts, axis=0)
        h_ref[...] = h
        r = lax.rsqrt(jnp.mean(h * h, axis=-1, keepdims=True) + EPS)
        n = (h * r * g_ref[...]).astype(BF16)
        n_ref[...] = n
        p_ref[...] = _dot(n, wm_ref[...]).astype(BF16)
        ff_ref[...] = _dot(n, wf_ref[...])

    rows = lambda w: pl.BlockSpec((tm, w), lambda i: (i, 0))
    return pl.pallas_call(
        body, name="f_inproj", grid=(L // tm,),
        in_specs=[_full((BLK, D_MODEL))] + _shifted_blocks(tm)
        + [_full((1, D_MODEL)), _full((D_MODEL, MAIN_W)), _full((D_MODEL, BLK))],
        out_specs=[rows(D_MODEL), rows(D_MODEL), rows(MAIN_W), rows(BLK)],
        out_shape=[jax.ShapeDtypeStruct((L, D_MODEL), F32), jax.ShapeDtypeStruct((L, D_MODEL), BF16),
                   jax.ShapeDtypeStruct((L, MAIN_W), BF16), jax.ShapeDtypeStruct((L, BLK), F32)],
        compiler_params=_params(("parallel",)),
    )(head, *([x] * nb), g, w_main, w_ff)


def _block_group(nblk):
    return 3 if nblk % 3 == 0 else 1


def _fox_prep(ff, fb):
    L = ff.shape[0]
    nblk = L // BLK
    G = _block_group(nblk)

    def body(ff_ref, b_ref, c_ref, ct_ref, carry):
        @pl.when(pl.program_id(0) == 0)
        def _():
            carry[...] = jnp.zeros_like(carry)

        tri = (_iota((BLK, BLK), 0) >= _iota((BLK, BLK), 1)).astype(BF16)
        live = _iota((BLK, BLK), 1) < FOX_HEADS
        run = carry[...]
        for b in range(G):
            z = ff_ref[b * BLK:(b + 1) * BLK, :] + b_ref[...]
            lf = jnp.where(live, jnp.minimum(z, 0.0) - jnp.log1p(jnp.exp(-jnp.abs(z))), 0.0)
            hi, mid, lo = _split3(lf)
            cs = (_dot(tri, hi) + _dot(tri, mid) + _dot(tri, lo) + run) * LOG2E
            c_ref[b * BLK:(b + 1) * BLK, :] = cs
            ct_ref[b] = cs.T[0:8, :]
            run = run + jnp.sum(lf, axis=0, keepdims=True)
        carry[...] = run

    return pl.pallas_call(
        body, name="f_foxprep", grid=(nblk // G,),
        in_specs=[pl.BlockSpec((G * BLK, BLK), lambda i: (i, 0)), _full((1, BLK))],
        out_specs=[pl.BlockSpec((G * BLK, BLK), lambda i: (i, 0)), pl.BlockSpec((G, 8, BLK), lambda i: (i, 0, 0))],
        out_shape=[jax.ShapeDtypeStruct((L, BLK), F32), jax.ShapeDtypeStruct((nblk, 8, BLK), F32)],
        scratch_shapes=[pltpu.VMEM((1, BLK), F32)],
        compiler_params=_params(("arbitrary",)),
    )(ff, fb)


def _rot_fns(cos, sin):
    lane = _iota((BLK, BLK), 1)
    first = (lane & (HEAD_LANES - 1)) < HEAD_LANES // 2

    def swap(x):
        return jnp.where(first, pltpu.roll(x, BLK - 32, 1), pltpu.roll(x, 32, 1))

    def rot(x):
        return x * cos + swap(x) * sin

    def rot_t(dy):
        return dy * cos + swap(dy * sin)

    return rot, rot_t


def _retention_fwd(proj, cos_t, sin_t, ret_g):
    L = proj.shape[0]
    nblk = L // BLK
    G = _block_group(nblk)
    dmat, wq_t, wk_t, g_blk = _decay_tables()

    def body(q_ref, k_ref, v_ref, gate_ref, cos_ref, sin_ref, d_ref, wq_ref, wk_ref, rg_ref,
             mix_ref, o_ref, rs_ref, state):
        @pl.when(pl.program_id(0) == 0)
        def _():
            state[...] = jnp.zeros_like(state)

        lane = _iota((BLK, BLK), 1)
        sub = _iota((BLK, BLK), 0)
        for b in range(G):
            rows = slice(b * BLK, (b + 1) * BLK)
            rot, _ = _rot_fns(cos_ref[rows, :], sin_ref[rows, :])
            for p in range(2):
                qr = rot(q_ref[rows, p * BLK:(p + 1) * BLK].astype(F32))
                kr = rot(k_ref[rows, p * BLK:(p + 1) * BLK].astype(F32)) * (HEAD_LANES ** -0.5)
                kr_b = kr.astype(BF16)
                qw = (qr * wq_ref[p]).astype(BF16)
                kw = (kr * wk_ref[p]).astype(BF16)
                for e in range(2):
                    h = 2 * p + e
                    cols = slice(h * BLK, (h + 1) * BLK)
                    qm = jnp.where((lane >> 6) == e, qr, 0.0).astype(BF16)
                    s = _dot_nt(qm, kr_b) * d_ref[h]
                    vh = v_ref[rows, cols]
                    st = state[h]
                    rs_ref[b, h] = st
                    o = _dot(s.astype(BF16), vh) + _dot(qw, st.astype(BF16))
                    u = jnp.where((sub >> 6) == e, _dot_tn(kw, vh), 0.0)
                    state[h] = g_blk[h] * st + u
                    rn = lax.rsqrt(jnp.mean(o * o, axis=-1, keepdims=True) + EPS)
                    gate = gate_ref[rows, cols].astype(F32)
                    o_ref[rows, cols] = o
                    mix_ref[rows, cols] = (o * rn * rg_ref[:, cols] * (gate * jax.nn.sigmoid(gate))).astype(BF16)

    row = lambda c: (lambda i: (i, c))
    return pl.pallas_call(
        body, name="f_retention", grid=(nblk // G,),
        in_specs=[pl.BlockSpec((G * BLK, 256), row(0)), pl.BlockSpec((G * BLK, 256), row(1)),
                  pl.BlockSpec((G * BLK, 512), row(1)), pl.BlockSpec((G * BLK, 512), row(2)),
                  pl.BlockSpec((G * BLK, BLK), row(0)), pl.BlockSpec((G * BLK, BLK), row(0)),
                  _full((RET_HEADS, BLK, BLK)), _full((2, BLK, BLK)), _full((2, BLK, BLK)), _full((1, 512))],
        out_specs=[pl.BlockSpec((G * BLK, 512), row(0)), pl.BlockSpec((G * BLK, 512), row(0)),
                   pl.BlockSpec((G, RET_HEADS, BLK, BLK), lambda i: (i, 0, 0, 0))],
        out_shape=[jax.ShapeDtypeStruct((L, 512), BF16), jax.ShapeDtypeStruct((L, 512), F32),
                   jax.ShapeDtypeStruct((nblk, RET_HEADS, BLK, BLK), F32)],
        scratch_shapes=[pltpu.VMEM((RET_HEADS, BLK, BLK), F32)],
        compiler_params=_params(("arbitrary",)),
    )(proj, proj, proj, proj, cos_t, sin_t, dmat, wq_t, wk_t, ret_g)


def _fox_units(L):
    nblk = L // BLK
    assert L % BLK == 0 and nblk % 2 == 1, "sequence must be one 128-row block plus whole 256-row tiles"
    return nblk, (nblk - 1) // 2


def _fox_tile_masks():
    sub, lane = _iota((BLK, BLK), 0), _iota((BLK, BLK), 1)
    valid = _iota((BLK, UNIT), 0) >= N_PAD
    diag = _iota((UNIT, UNIT), 0) <= _iota((UNIT, UNIT), 1)
    r, q = _iota((BLK + UNIT, UNIT), 0), _iota((BLK + UNIT, UNIT), 1)
    first_and_diag = ((r < BLK) & (r >= N_PAD)) | ((r >= BLK) & (r - BLK <= q))
    return dict(first=(sub <= lane) & (sub >= N_PAD), valid=valid, diag=diag, first_and_diag=first_and_diag)


def _fox_fwd(proj, c, ctb, gather=()):
    L = proj.shape[0]
    nblk, nu = _fox_units(L)
    scale = HEAD_LANES ** -0.5 * LOG2E
    ng = len(gather)
    steps = FOX_HEADS // (2 * FOX_PAIRS)

    def body(qkv_ref, c_ref, ct_ref, *rest):
        g_in, (of_ref, lse_ref), g_out = rest[:ng], rest[ng:ng + 2], rest[ng + 2:2 * ng + 2]
        vt, csb = rest[2 * ng + 2:2 * ng + 4]
        p = pl.program_id(0)
        heads = [(pp, e, 2 * FOX_PAIRS * p + 2 * pp + e) for pp in range(FOX_PAIRS) for e in range(2)]

        @pl.when(p == 0)
        def _():
            lse_ref[...] = jnp.zeros_like(lse_ref)
            if ng:
                local, sends, _ = _allgather_copies(g_in, g_out, *rest[2 * ng + 4:])
                for cp in local + sends:
                    cp.start()

        lane = _iota((BLK, BLK), 1)
        sub8 = _iota((8, BLK), 0)
        masks = _fox_tile_masks()

        def pre(j, carry):
            off = pl.multiple_of(j * BLK, BLK)
            ct = c_ref[pl.ds(off, BLK), :]
            for pp in range(FOX_PAIRS):
                vt[pp, j] = qkv_ref[pl.ds(off, BLK), pp * 384 + 2 * BLK:pp * 384 + 3 * BLK].astype(F32).T.astype(BF16)
            for hh, (_, _, h) in enumerate(heads):
                col = jnp.sum(jnp.where(lane == h, ct, 0.0), axis=1, keepdims=True)
                csb[hh, j] = jnp.broadcast_to(col, (BLK, BLK))
            return carry

        lax.fori_loop(0, nblk, pre, 0)

        def attend(qblk, nq, n_whole):
            qlen = nq * BLK
            qoff = pl.multiple_of(qblk * BLK, BLK)
            qlane = _iota((qlen, BLK), 1)
            qs = [qkv_ref[pl.ds(qoff, qlen), pp * 384:pp * 384 + BLK].astype(F32) * scale for pp in range(FOX_PAIRS)]
            qm = [jnp.where((qlane >> 6) == e, qs[pp], 0.0).astype(BF16) for pp, e, _ in heads]
            ct_row = [jnp.concatenate([_pick_row(ct_ref[qblk + a], h) for a in range(nq)], axis=1) for _, _, h in heads]

            def step(segs, mask, st):
                blocks = [kblk + b for kblk, nk in segs for b in range(nk)]
                kts = []
                for pp in range(FOX_PAIRS):
                    kt = [qkv_ref[pl.ds(pl.multiple_of(kblk * BLK, BLK), nk * BLK), pp * 384 + BLK:pp * 384 + 2 * BLK]
                          for kblk, nk in segs]
                    kts.append(kt[0] if len(kt) == 1 else jnp.concatenate(kt, axis=0))
                out = []
                for hh, (pp, e, _) in enumerate(heads):
                    m, l, acc = st[3 * hh:3 * hh + 3]
                    s = _dot_nt(kts[pp], qm[hh])
                    t = jnp.concatenate([s[b * BLK:(b + 1) * BLK] - jnp.concatenate([csb[hh, blk]] * nq, axis=1)
                                         for b, blk in enumerate(blocks)], axis=0)
                    if mask is not None:
                        t = jnp.where(mask, t, NEG)
                    m_new = jnp.maximum(m, jnp.max(t, axis=0, keepdims=True) + ct_row[hh])
                    alpha = jnp.exp2(m - m_new)
                    pr = jnp.exp2(t - (m_new - ct_row[hh]))
                    l = alpha * l + jnp.sum(pr, axis=0, keepdims=True)
                    pr_b = pr.astype(BF16)
                    pv = None
                    for b, blk in enumerate(blocks):
                        part = _dot(vt[pp, blk, e * HEAD_LANES:(e + 1) * HEAD_LANES, :], pr_b[b * BLK:(b + 1) * BLK])
                        pv = part if pv is None else pv + part
                    out += [m_new, l, alpha * acc + pv]
                return tuple(out)

            st = (jnp.full((1, qlen), NEG, F32), jnp.zeros((1, qlen), F32),
                  jnp.zeros((HEAD_LANES, qlen), F32)) * len(heads)
            if nq == 1:
                st = step([(0, 1)], masks["first"], st)
            else:
                st = step([(0, 1), (qblk, 2)], masks["first_and_diag"], st)
                n_wide = n_whole // WIDE
                st = lax.fori_loop(0, n_wide, lambda j, s_: step([(1 + 2 * WIDE * j, 2 * WIDE)], None, s_), st)
                rest = 1 + 2 * WIDE * n_wide
                st = lax.cond((n_whole & 2) != 0, lambda s_: step([(rest, 4)], None, s_), lambda s_: s_, st)
                st = lax.cond((n_whole & 1) != 0, lambda s_: step([(rest + 2 * (n_whole & 2), 2)], None, s_),
                              lambda s_: s_, st)
            for pp in range(FOX_PAIRS):
                lo, hi = st[6 * pp:6 * pp + 3], st[6 * pp + 3:6 * pp + 6]
                o_t = jnp.concatenate([lo[2] * (1.0 / lo[1]), hi[2] * (1.0 / hi[1])], axis=0)
                of_ref[pl.ds(qoff, qlen), pp * BLK:(pp + 1) * BLK] = o_t.T.astype(BF16)
            lse = [st[3 * hh] + jnp.log(st[3 * hh + 1]) * LOG2E for hh in range(len(heads))]
            for a in range(nq):
                upd = jnp.zeros((8, BLK), F32)
                for hh, (_, _, h) in enumerate(heads):
                    upd = upd + jnp.where(sub8 == h, lse[hh][:, a * BLK:(a + 1) * BLK], 0.0)
                lse_ref[qblk + a] = lse_ref[qblk + a] + upd

        attend(0, 1, 0)

        def q_loop(u, carry):
            attend(1 + 2 * u, 2, u)
            return carry

        lax.fori_loop(0, nu, q_loop, 0)

        if ng:
            @pl.when(p == steps - 1)
            def _():
                local, sends, recvs = _allgather_copies(g_in, g_out, *rest[2 * ng + 4:])
                for cp in recvs:
                    cp.wait_recv()
                for cp in sends:
                    cp.wait_send()
                for cp in local:
                    cp.wait()

    width = 384 * FOX_PAIRS
    return pl.pallas_call(
        body, name="f_fox", grid=(steps,),
        in_specs=[pl.BlockSpec((L, width), lambda p: (0, RET_W // width + p)), _full((L, BLK)), _full((nblk, 8, BLK))]
        + [_ANY] * ng,
        out_specs=[pl.BlockSpec((L, FOX_PAIRS * BLK), lambda p: (0, p)), _full((nblk, 8, BLK))] + [_ANY] * ng,
        out_shape=[jax.ShapeDtypeStruct((L, 512), BF16), jax.ShapeDtypeStruct((nblk, 8, BLK), F32)]
        + [jax.ShapeDtypeStruct((N_CHIPS,) + a.shape, a.dtype) for a in gather],
        scratch_shapes=[pltpu.VMEM((FOX_PAIRS, nblk, BLK, BLK), BF16), pltpu.VMEM((2 * FOX_PAIRS, nblk, BLK, BLK), F32)]
        + _allgather_semaphores(ng),
        compiler_params=_params(("arbitrary",)),
    )(proj, c, ctb, *gather)


def _outproj_up(mix_r, o_f, h0, w_out, ffn_g, w_up, conv_w, conv_b):
    L = h0.shape[0]
    tm = _row_tile(L)
    shard = w_up.shape[2]
    assert 2 * shard == D_FF
    cw = [conv_w[j:j + 1] for j in range(3)]
    resident = lambda shape: pl.BlockSpec(shape, lambda i: (0,) * len(shape), pipeline_mode=pl.Buffered(1))

    def body(mr_ref, of_ref, h0_ref, wo_ref, g_ref, wu_ref, cw0, cw1, cw2, cb_ref,
             h1_ref, n2_ref, up_ref, act_ref, acc_ref, halo):
        i = pl.program_id(0)

        @pl.when(i == 0)
        def _():
            halo[...] = jnp.zeros_like(halo)

        h1 = h0_ref[...] + _dot(mr_ref[...], wo_ref[0:512, :]) + _dot(of_ref[...], wo_ref[512:1024, :])
        h1_ref[...] = h1
        r = lax.rsqrt(jnp.mean(h1 * h1, axis=-1, keepdims=True) + EPS)
        n2 = (h1 * r * g_ref[...]).astype(BF16)
        n2_ref[...] = n2
        live = i * tm + _iota((tm, 1), 0) >= N_PAD
        for half in range(2):
            cols = slice(half * shard, (half + 1) * shard)
            a_b = _dot(n2, wu_ref[half]).astype(BF16)
            b_b = _dot(n2, wu_ref[2 + half]).astype(BF16)
            up_ref[:, cols] = a_b
            up_ref[:, D_FF + half * shard:D_FF + (half + 1) * shard] = b_b
            a = jnp.where(live, a_b.astype(F32), 0.0)
            _, _, acc = _conv_taps(a, halo[:, cols], [cw0[:, cols], cw1[:, cols], cw2[:, cols]], cb_ref[:, cols])
            act_ref[:, cols] = (acc * jax.nn.sigmoid(acc) * b_b.astype(F32)).astype(BF16)
            acc_ref[:, cols] = acc.astype(BF16)
            halo[:, cols] = a[tm - 8:tm, :]

    rows = lambda w: pl.BlockSpec((tm, w), lambda i: (i, 0))
    return pl.pallas_call(
        body, name="f_outproj_up", grid=(L // tm,),
        in_specs=[rows(512), rows(512), rows(D_MODEL), resident((D_MODEL, D_MODEL)), _full((1, D_MODEL)),
                  resident((N_CHIPS, D_MODEL, shard)), _full((1, D_FF)), _full((1, D_FF)), _full((1, D_FF)),
                  _full((1, D_FF))],
        out_specs=[rows(D_MODEL), rows(D_MODEL), rows(2 * D_FF), rows(D_FF), rows(D_FF)],
        out_shape=[jax.ShapeDtypeStruct((L, D_MODEL), F32), jax.ShapeDtypeStruct((L, D_MODEL), BF16),
                   jax.ShapeDtypeStruct((L, 2 * D_FF), BF16), jax.ShapeDtypeStruct((L, D_FF), BF16),
                   jax.ShapeDtypeStruct((L, D_FF), BF16)],
        scratch_shapes=[pltpu.VMEM((8, D_FF), F32)],
        compiler_params=_params(("arbitrary",)),
    )(mix_r, o_f, h0, w_out, ffn_g, w_up, cw[0], cw[1], cw[2], conv_b)


def _conv_taps(a, halo, cw, cb):
    sub = _iota((a.shape[0], 1), 0)
    a1 = jnp.where(sub == 0, _pick_row(halo, 7), pltpu.roll(a, 1, 0))
    a2 = jnp.where(sub == 0, _pick_row(halo, 6), jnp.where(sub == 1, _pick_row(halo, 7), pltpu.roll(a, 2, 0)))
    acc = cb + a2 * cw[0]
    acc = acc + a1 * cw[1]
    acc = acc + a * cw[2]
    return a1, a2, acc


def _ffn_down_loss(g_act, w_down, h1, final_g, target):
    L = h1.shape[0]
    tm = _row_tile(L)
    nb = tm // BLK

    def body(g_ref, wd_ref, h1_ref, gf_ref, *refs):
        t_refs, (dh_ref, dhb_ref, dgf_ref, loss_ref) = refs[:nb], refs[nb:]
        i = pl.program_id(0)

        @pl.when(i == 0)
        def _():
            dgf_ref[...] = jnp.zeros_like(dgf_ref)
            loss_ref[...] = jnp.zeros_like(loss_ref)

        h2 = h1_ref[...] + _dot(g_ref[...], wd_ref[...])
        r = lax.rsqrt(jnp.mean(h2 * h2, axis=-1, keepdims=True) + EPS)
        yn = h2 * r
        gf = gf_ref[...]
        live = i * tm + _iota((tm, 1), 0) >= PREFIX
        target = jnp.concatenate([t[...] for t in t_refs], axis=0)
        err = jnp.where(live, yn * gf - target, 0.0)
        loss_ref[...] = loss_ref[...] + 0.5 * jnp.sum(jnp.mean(err * err, axis=-1, keepdims=True))
        dy = err * (1.0 / D_MODEL)
        dgf_ref[...] = dgf_ref[...] + jnp.sum(dy * yn, axis=0, keepdims=True)
        dyn = dy * gf
        dh = r * (dyn - yn * jnp.mean(dyn * yn, axis=-1, keepdims=True))
        dh_ref[...] = dh
        dhb_ref[...] = dh.astype(BF16)

    rows = lambda w: pl.BlockSpec((tm, w), lambda i: (i, 0))
    return pl.pallas_call(
        body, name="f_ffn_down_loss", grid=(L // tm,),
        in_specs=[rows(D_FF), _full((D_FF, D_MODEL)), rows(D_MODEL), _full((1, D_MODEL))] + _shifted_blocks(tm),
        out_specs=[rows(D_MODEL), rows(D_MODEL), _full((1, D_MODEL)), _full((1, BLK))],
        out_shape=[jax.ShapeDtypeStruct((L, D_MODEL), F32), jax.ShapeDtypeStruct((L, D_MODEL), BF16),
                   jax.ShapeDtypeStruct((1, D_MODEL), F32), jax.ShapeDtypeStruct((1, BLK), F32)],
        compiler_params=_params(("arbitrary",)),
    )(g_act, w_down, h1, final_g, *([target] * nb))


def _ffn_bwd_gate(dh2b, w_down, acc_saved, up):
    L = dh2b.shape[0]
    tm = _row_tile(L)

    def body(dh_ref, wd_ref, acc_ref, b_ref, dacc_ref, db_ref):
        acc = acc_ref[...].astype(F32)
        dg = _dot_nt(dh_ref[...], wd_ref[...])
        sg = jax.nn.sigmoid(acc)
        silu = acc * sg
        db_ref[...] = (dg * silu).astype(BF16)
        dacc_ref[...] = (dg * b_ref[...].astype(F32) * (sg + silu * (1.0 - sg))).astype(BF16)

    rows = lambda w, c=0: pl.BlockSpec((tm, w), lambda i: (i, c))
    return pl.pallas_call(
        body, name="b_ffn_gate", grid=(L // tm,),
        in_specs=[rows(D_MODEL), _full((D_FF, D_MODEL)), rows(D_FF), rows(D_FF, 1)],
        out_specs=[rows(D_FF), rows(D_FF)],
        out_shape=[jax.ShapeDtypeStruct((L, D_FF), BF16), jax.ShapeDtypeStruct((L, D_FF), BF16)],
        compiler_params=_params(("parallel",)),
    )(dh2b, w_down, acc_saved, up)


def _ffn_bwd_up(dacc, db, up, conv_w, w_up, h1, ffn_g, dh2, w_out):
    L = h1.shape[0]
    tm = _row_tile(L)
    nt = L // tm
    shard = w_up.shape[2]
    cw = [conv_w[j:j + 1] for j in range(3)]

    def body(da_ref, halo_ref, db_ref, a_ref, cw0, cw1, cw2, wu_ref, h1_ref, g_ref, dh2_ref, wo_ref,
             dup_ref, dh1_ref, dh1b_ref, dmix_ref, dg_ref, dcw_ref):
        i = pl.program_id(0)

        @pl.when(i == 0)
        def _():
            dg_ref[...] = jnp.zeros_like(dg_ref)
            dcw_ref[...] = jnp.zeros_like(dcw_ref)

        sub = _iota((tm, 1), 0)
        sub8 = _iota((8, 1), 0)
        last_tile = i == nt - 1
        dbv = db_ref[...]
        dup_ref[:, D_FF:2 * D_FF] = dbv
        dn = _dot_nt(dbv[:, 0:shard], wu_ref[2]) + _dot_nt(dbv[:, shard:2 * shard], wu_ref[3])
        for half in range(2):
            cols = slice(half * shard, (half + 1) * shard)
            d0 = da_ref[:, cols].astype(F32)
            halo = jnp.where(last_tile, 0.0, halo_ref[:, cols].astype(F32))
            d1 = jnp.where(sub == tm - 1, _pick_row(halo, 0), pltpu.roll(d0, tm - 1, 0))
            d2 = jnp.where(sub == tm - 2, _pick_row(halo, 0),
                           jnp.where(sub == tm - 1, _pick_row(halo, 1), pltpu.roll(d0, tm - 2, 0)))
            a = a_ref[:, cols].astype(F32)
            upd = jnp.zeros((8, shard), F32)
            for j, t in enumerate((d2 * a, d1 * a, d0 * a, d0)):
                upd = upd + jnp.where(sub8 == j, jnp.sum(t, axis=0, keepdims=True), 0.0)
            dcw_ref[:, cols] = dcw_ref[:, cols] + upd
            da = (d0 * cw2[:, cols] + d1 * cw1[:, cols] + d2 * cw0[:, cols]).astype(BF16)
            dup_ref[:, cols] = da
            dn = dn + _dot_nt(da, wu_ref[half])
        h1 = h1_ref[...]
        r = lax.rsqrt(jnp.mean(h1 * h1, axis=-1, keepdims=True) + EPS)
        yn = h1 * r
        dg_ref[...] = dg_ref[...] + jnp.sum(dn * yn, axis=0, keepdims=True)
        dyn = dn * g_ref[...]
        dh1 = dh2_ref[...] + r * (dyn - yn * jnp.mean(dyn * yn, axis=-1, keepdims=True))
        dh1_ref[...] = dh1
        dh1b = dh1.astype(BF16)
        dh1b_ref[...] = dh1b
        dmix_ref[...] = _dot_nt(dh1b, wo_ref[...]).astype(BF16)

    rows = lambda w: pl.BlockSpec((tm, w), lambda i: (i, 0))
    halo = pl.BlockSpec((8, D_FF), lambda i: (jnp.minimum((i + 1) * (tm // 8), L // 8 - 1), 0))
    return pl.pallas_call(
        body, name="b_ffn_up", grid=(nt,),
        in_specs=[rows(D_FF), halo, rows(D_FF), rows(D_FF), _full((1, D_FF)), _full((1, D_FF)), _full((1, D_FF)),
                  _full((N_CHIPS, D_MODEL, shard)), rows(D_MODEL), _full((1, D_MODEL)), rows(D_MODEL),
                  _full((D_MODEL, D_MODEL))],
        out_specs=[rows(2 * D_FF), rows(D_MODEL), rows(D_MODEL), rows(D_MODEL), _full((1, D_MODEL)),
                   _full((8, D_FF))],
        out_shape=[jax.ShapeDtypeStruct((L, 2 * D_FF), BF16), jax.ShapeDtypeStruct((L, D_MODEL), F32),
                   jax.ShapeDtypeStruct((L, D_MODEL), BF16), jax.ShapeDtypeStruct((L, D_MODEL), BF16),
                   jax.ShapeDtypeStruct((1, D_MODEL), F32), jax.ShapeDtypeStruct((8, D_FF), F32)],
        compiler_params=_params(("arbitrary",)),
    )(dacc, dacc, db, up, cw[0], cw[1], cw[2], w_up, h1, ffn_g, dh2, w_out)


def _wgrad(a, b, name, tn=None, tk=None):
    L, K = a.shape
    N = b.shape[1]
    tn = N if tn is None else tn
    tk = K if tk is None else tk
    tl = _row_tile(L, (1408, 768, 512, 256, 128))

    def body(a_ref, b_ref, o_ref):
        @pl.when(pl.program_id(2) == 0)
        def _():
            o_ref[...] = jnp.zeros_like(o_ref)

        o_ref[0] = o_ref[0] + _dot_tn(a_ref[...], b_ref[...])

    return pl.pallas_call(
        body, name=name, grid=(N // tn, K // tk, L // tl),
        in_specs=[pl.BlockSpec((tl, tk), lambda n, k, l: (l, k)), pl.BlockSpec((tl, tn), lambda n, k, l: (l, n))],
        out_specs=pl.BlockSpec((1, tk, tn), lambda n, k, l: (n, k, 0)),
        out_shape=jax.ShapeDtypeStruct((N // tn, K, tn), F32),
        compiler_params=_params(("parallel", "parallel", "arbitrary")),
    )(a, b)


def _retention_bwd(dmix, o, proj, cos_t, sin_t, ret_g, states, exchange=()):
    L = proj.shape[0]
    nblk = L // BLK
    G = _block_group(nblk)
    steps = nblk // G
    nx = len(exchange)
    dmat, wq_t, wk_t, g_blk = _decay_tables()

    def body(dm_ref, o_ref, q_ref, k_ref, v_ref, gate_ref, cos_ref, sin_ref, d_ref, wq_ref, wk_ref, rg_ref, rs_ref,
             *rest):
        x_in, (dp_ref, drg_ref), x_out, gstate = rest[:nx], rest[nx:nx + 2], rest[nx + 2:2 * nx + 2], rest[2 * nx + 2]

        @pl.when(pl.program_id(0) == 0)
        def _():
            if nx:
                for cp in _sibling_half_copies(x_in, x_out, *rest[2 * nx + 3:])[0]:
                    cp.start()
            gstate[...] = jnp.zeros_like(gstate)
            drg_ref[...] = jnp.zeros_like(drg_ref)

        lane = _iota((BLK, BLK), 1)
        sub = _iota((BLK, BLK), 0)
        scale = HEAD_LANES ** -0.5
        for b in reversed(range(G)):
            rows = slice(b * BLK, (b + 1) * BLK)
            rot, rot_t = _rot_fns(cos_ref[rows, :], sin_ref[rows, :])
            for p in range(2):
                qr = rot(q_ref[rows, p * BLK:(p + 1) * BLK].astype(F32))
                kr = rot(k_ref[rows, p * BLK:(p + 1) * BLK].astype(F32)) * scale
                kr_b = kr.astype(BF16)
                qw = (qr * wq_ref[p]).astype(BF16)
                kw = (kr * wk_ref[p]).astype(BF16)
                dqr = jnp.zeros((BLK, BLK), F32)
                dkr = jnp.zeros((BLK, BLK), F32)
                for e in range(2):
                    h = 2 * p + e
                    cols = slice(h * BLK, (h + 1) * BLK)
                    head_lanes = (lane >> 6) == e
                    o = o_ref[rows, cols]
                    rn = lax.rsqrt(jnp.mean(o * o, axis=-1, keepdims=True) + EPS)
                    y = o * rn
                    gate = gate_ref[rows, cols].astype(F32)
                    sg = jax.nn.sigmoid(gate)
                    dm = dm_ref[rows, cols].astype(F32)
                    rgain = rg_ref[:, cols]
                    drg_ref[:, cols] = drg_ref[:, cols] + jnp.sum(dm * y * (gate * sg), axis=0, keepdims=True)
                    dp_ref[rows, 1024 + h * BLK:1024 + (h + 1) * BLK] = (
                        dm * y * rgain * (sg * (1.0 + gate * (1.0 - sg)))).astype(BF16)
                    dy = dm * rgain * (gate * sg)
                    do = (rn * (dy - y * jnp.mean(dy * y, axis=-1, keepdims=True))).astype(BF16)
                    vh = v_ref[rows, cols]
                    qm = jnp.where(head_lanes, qr, 0.0).astype(BF16)
                    dmh = d_ref[h]
                    s = (_dot_nt(qm, kr_b) * dmh).astype(BF16)
                    ds = (_dot_nt(do, vh) * dmh).astype(BF16)
                    st = rs_ref[b, h].astype(BF16)
                    gs = gstate[h]
                    gs_b = gs.astype(BF16)
                    dqr = dqr + jnp.where(head_lanes, _dot(ds, kr_b), 0.0) + _dot_nt(do, st) * wq_ref[p]
                    dkr = dkr + _dot_tn(ds, qm) + _dot_nt(vh, gs_b) * wk_ref[p]
                    dp_ref[rows, 512 + h * BLK:512 + (h + 1) * BLK] = (_dot_tn(s, do) + _dot(kw, gs_b)).astype(BF16)
                    dr = jnp.where((sub >> 6) == e, _dot_tn(qw, do), 0.0)
                    gstate[h] = dr + g_blk[h] * gs
                dp_ref[rows, p * BLK:(p + 1) * BLK] = rot_t(dqr).astype(BF16)
                dp_ref[rows, 256 + p * BLK:256 + (p + 1) * BLK] = (rot_t(dkr) * scale).astype(BF16)

        if nx:
            @pl.when(pl.program_id(0) == steps - 1)
            def _():
                sends, recvs = _sibling_half_copies(x_in, x_out, *rest[2 * nx + 3:])
                for cp in recvs:
                    cp.wait_recv()
                for cp in sends:
                    cp.wait_send()

    row = lambda c: (lambda i: (steps - 1 - i, c))
    return pl.pallas_call(
        body, name="b_retention", grid=(steps,),
        in_specs=[pl.BlockSpec((G * BLK, 512), row(0)), pl.BlockSpec((G * BLK, 512), row(0)),
                  pl.BlockSpec((G * BLK, 256), row(0)), pl.BlockSpec((G * BLK, 256), row(1)),
                  pl.BlockSpec((G * BLK, 512), row(1)), pl.BlockSpec((G * BLK, 512), row(2)),
                  pl.BlockSpec((G * BLK, BLK), row(0)), pl.BlockSpec((G * BLK, BLK), row(0)),
                  _full((RET_HEADS, BLK, BLK)), _full((2, BLK, BLK)), _full((2, BLK, BLK)), _full((1, 512)),
                  pl.BlockSpec((G, RET_HEADS, BLK, BLK), lambda i: (steps - 1 - i, 0, 0, 0))] + [_ANY] * nx,
        out_specs=[pl.BlockSpec((G * BLK, RET_W), row(0)), _full((1, 512))] + [_ANY] * nx,
        out_shape=[jax.ShapeDtypeStruct((L, RET_W), BF16), jax.ShapeDtypeStruct((1, 512), F32)]
        + _sibling_half_shapes(exchange),
        scratch_shapes=[pltpu.VMEM((RET_HEADS, BLK, BLK), F32)] + _sibling_half_semaphores(nx),
        compiler_params=_params(("arbitrary",)),
    )(dmix, o, proj, proj, proj, proj, cos_t, sin_t, dmat, wq_t, wk_t, ret_g, states, *exchange)


def _fox_delta(dmix, o_f):
    L = o_f.shape[0]
    nblk = L // BLK
    G = _block_group(nblk)

    def body(do_ref, o_ref, d_ref):
        sel = ((_iota((8, 512), 1) >> 6) == _iota((8, 512), 0)).astype(BF16)
        for b in range(G):
            rows = slice(b * BLK, (b + 1) * BLK)
            prod = do_ref[rows, :].astype(F32) * o_ref[rows, :].astype(F32)
            hi = prod.astype(BF16)
            lo = (prod - hi.astype(F32)).astype(BF16)
            d_ref[b] = _dot_nt(sel, hi) + _dot_nt(sel, lo)

    return pl.pallas_call(
        body, name="b_foxdelta", grid=(nblk // G,),
        in_specs=[pl.BlockSpec((G * BLK, 512), lambda i: (i, 1)), pl.BlockSpec((G * BLK, 512), lambda i: (i, 0))],
        out_specs=pl.BlockSpec((G, 8, BLK), lambda i: (i, 0, 0)),
        out_shape=jax.ShapeDtypeStruct((nblk, 8, BLK), F32),
        compiler_params=_params(("parallel",)),
    )(dmix, o_f)


def _fox_bwd(proj, dmix, c, ctb, lse, delta, scatter=()):
    L = proj.shape[0]
    nblk, nu = _fox_units(L)
    scale = HEAD_LANES ** -0.5
    ns = len(scatter)

    steps = FOX_HEADS // (2 * FOX_PAIRS)

    def body(qkv_ref, do_ref, c_ref, ct_ref, lse_ref, dl_ref, *rest):
        s_in, (dp_ref, dc_ref, dcq_ref), s_out = rest[:ns], rest[ns:ns + 3], rest[ns + 3:2 * ns + 3]
        ktt, dqt, dk_acc, dv_acc, dcs_acc = rest[2 * ns + 3:2 * ns + 8]
        p = pl.program_id(0)
        heads = [(pp, e, 2 * FOX_PAIRS * p + 2 * pp + e) for pp in range(FOX_PAIRS) for e in range(2)]

        @pl.when(p == 0)
        def _():
            dc_ref[...] = jnp.zeros_like(dc_ref)
            dcq_ref[...] = jnp.zeros_like(dcq_ref)
            if ns:
                for cp in _scatter_copies(s_in, s_out, *rest[2 * ns + 8:]):
                    cp.start()

        sub8 = _iota((8, BLK), 0)
        masks = _fox_tile_masks()

        def pre(j, carry):
            off = pl.multiple_of(j * BLK, BLK)
            for pp in range(FOX_PAIRS):
                ktt[pp, j] = qkv_ref[pl.ds(off, BLK), pp * 384 + BLK:pp * 384 + 2 * BLK].astype(F32).T.astype(BF16)
                dqt[pp, j] = jnp.zeros((BLK, BLK), F32)
            return carry

        lax.fori_loop(0, nblk, pre, 0)

        def kv_pass(kblk, nk, n_later):
            klen = nk * BLK
            koff = pl.multiple_of(kblk * BLK, BLK)
            kt = [qkv_ref[pl.ds(koff, klen), pp * 384 + BLK:pp * 384 + 2 * BLK] for pp in range(FOX_PAIRS)]
            vtile = [qkv_ref[pl.ds(koff, klen), pp * 384 + 2 * BLK:pp * 384 + 3 * BLK] for pp in range(FOX_PAIRS)]
            ct = c_ref[pl.ds(koff, klen), :]
            klane = _iota((klen, BLK), 1)
            cs = [jnp.broadcast_to(jnp.sum(jnp.where(klane == h, ct, 0.0), axis=1, keepdims=True), (klen, WIDE * UNIT))
                  for _, _, h in heads]
            k_t = [jnp.concatenate([ktt[pp, kblk + b, e * HEAD_LANES:(e + 1) * HEAD_LANES, :] for b in range(nk)], axis=1)
                   for pp, e, _ in heads]
            for pp in range(FOX_PAIRS):
                dk_acc[pp, 0:klen] = jnp.zeros((klen, BLK), F32)
                dv_acc[pp, 0:klen] = jnp.zeros((klen, BLK), F32)
            for hh in range(len(heads)):
                dcs_acc[hh, 0:klen] = jnp.zeros((klen, BLK), F32)

            def tile(qblk, nq, mask):
                qlen = nq * BLK
                if mask == "valid":
                    mask = _iota((klen, qlen), 0) >= N_PAD
                qoff = pl.multiple_of(qblk * BLK, BLK)
                qlane = _iota((qlen, BLK), 1)
                qs = [qkv_ref[pl.ds(qoff, qlen), pp * 384:pp * 384 + BLK].astype(F32) * (scale * LOG2E)
                      for pp in range(FOX_PAIRS)]
                dot_ = [do_ref[pl.ds(qoff, qlen), pp * BLK:(pp + 1) * BLK] for pp in range(FOX_PAIRS)]
                stats = [[ref[qblk + a] for a in range(nq)] for ref in (ct_ref, lse_ref, dl_ref)]
                dcq = [jnp.zeros((8, BLK), F32) for _ in range(nq)]
                for hh, (pp, e, h) in enumerate(heads):
                    head = (qlane >> 6) == e
                    ct_row, lse_row, dl_row = [jnp.concatenate([_pick_row(t, h) for t in ts], axis=1) for ts in stats]
                    qm = jnp.where(head, qs[pp], 0.0).astype(BF16)
                    dom = jnp.where(head, dot_[pp], jnp.zeros_like(dot_[pp]))
                    t = _dot_nt(kt[pp], qm) - cs[hh][:, 0:qlen]
                    if mask is not None:
                        t = jnp.where(mask, t, NEG)
                    pr = jnp.exp2(t + (ct_row - lse_row))
                    dv_acc[pp, 0:klen] = dv_acc[pp, 0:klen] + _dot(pr.astype(BF16), dom)
                    dsv = pr * (_dot_nt(vtile[pp], dom) - dl_row)
                    ds_b = dsv.astype(BF16)
                    dk_acc[pp, 0:klen] = dk_acc[pp, 0:klen] + _dot(ds_b, qm)
                    rows = slice(e * HEAD_LANES, (e + 1) * HEAD_LANES)
                    dq_t = _dot(k_t[hh], ds_b)
                    key_side = dsv[:, 0:BLK]
                    for a in range(1, nq):
                        key_side = key_side + dsv[:, a * BLK:(a + 1) * BLK]
                    dcs_acc[hh, 0:klen] = dcs_acc[hh, 0:klen] + key_side
                    query_side = jnp.sum(dsv, axis=0, keepdims=True)
                    for a in range(nq):
                        cols = slice(a * BLK, (a + 1) * BLK)
                        dqt[pp, qblk + a, rows, :] = dqt[pp, qblk + a, rows, :] + dq_t[:, cols]
                        dcq[a] = dcq[a] + jnp.where(sub8 == h, query_side[:, cols], 0.0)
                for a in range(nq):
                    dcq_ref[qblk + a] = dcq_ref[qblk + a] + dcq[a]

            later_mask = "valid" if nk == 1 else None
            n_later = jnp.asarray(n_later, jnp.int32)
            n_wide = n_later // WIDE

            def later_wide(i, carry):
                tile(kblk + nk + 2 * WIDE * i, 2 * WIDE, later_mask)
                return carry

            tile(kblk, nk, masks["first"] if nk == 1 else masks["diag"])
            lax.fori_loop(0, n_wide, later_wide, 0)
            rest_blk = kblk + nk + 2 * WIDE * n_wide

            @pl.when((n_later & 2) != 0)
            def _():
                tile(rest_blk, 4, later_mask)

            @pl.when((n_later & 1) != 0)
            def _():
                tile(rest_blk + 2 * (n_later & 2), 2, later_mask)

            upd = jnp.zeros((klen, BLK), F32)
            for hh, (_, _, h) in enumerate(heads):
                upd = upd + jnp.where(klane == h, -jnp.sum(dcs_acc[hh, 0:klen], axis=1, keepdims=True), 0.0)
            dc_ref[pl.ds(koff, klen), :] = dc_ref[pl.ds(koff, klen), :] + upd
            for pp in range(FOX_PAIRS):
                dp_ref[pl.ds(koff, klen), pp * 384 + BLK:pp * 384 + 2 * BLK] = (
                    dk_acc[pp, 0:klen] * (1.0 / LOG2E)).astype(BF16)
                dp_ref[pl.ds(koff, klen), pp * 384 + 2 * BLK:pp * 384 + 3 * BLK] = dv_acc[pp, 0:klen].astype(BF16)

        kv_pass(0, 1, nu)

        def k_loop(u, carry):
            kv_pass(1 + 2 * u, 2, nu - 1 - u)
            return carry

        lax.fori_loop(0, nu, k_loop, 0)

        def flush(j, carry):
            off = pl.multiple_of(j * BLK, BLK)
            for pp in range(FOX_PAIRS):
                dp_ref[pl.ds(off, BLK), pp * 384:pp * 384 + BLK] = (dqt[pp, j].T * scale).astype(BF16)
            return carry

        lax.fori_loop(0, nblk, flush, 0)

        if ns:
            @pl.when(p == steps - 1)
            def _():
                copies = _scatter_copies(s_in, s_out, *rest[2 * ns + 8:])
                for cp in copies:
                    cp.wait_recv()
                for cp in copies:
                    cp.wait_send()

    width = 384 * FOX_PAIRS
    once = lambda shape, index: pl.BlockSpec(shape, index, pipeline_mode=pl.Buffered(1))
    stat = once((nblk, 8, BLK), lambda p: (0, 0, 0))
    return pl.pallas_call(
        body, name="b_fox", grid=(steps,),
        in_specs=[once((L, width), lambda p: (0, RET_W // width + p)),
                  once((L, FOX_PAIRS * BLK), lambda p: (0, 4 // FOX_PAIRS + p)),
                  once((L, BLK), lambda p: (0, 0)), stat, stat, stat] + [_ANY] * ns,
        out_specs=[pl.BlockSpec((L, width), lambda p: (0, p)), _full((L, BLK)), _full((nblk, 8, BLK))] + [_ANY] * ns,
        out_shape=[jax.ShapeDtypeStruct((L, FOX_W), BF16), jax.ShapeDtypeStruct((L, BLK), F32),
                   jax.ShapeDtypeStruct((nblk, 8, BLK), F32)] + _scatter_shapes(scatter),
        scratch_shapes=[pltpu.VMEM((FOX_PAIRS, nblk, BLK, BLK), BF16), pltpu.VMEM((FOX_PAIRS, nblk, BLK, BLK), F32),
                        pltpu.VMEM((FOX_PAIRS, UNIT, BLK), F32), pltpu.VMEM((FOX_PAIRS, UNIT, BLK), F32),
                        pltpu.VMEM((2 * FOX_PAIRS, UNIT, BLK), F32)]
        + _scatter_semaphores(ns),
        compiler_params=_params(("arbitrary",)),
    )(proj, dmix, c, ctb, lse, delta, *scatter)


def _fox_post(dc, dcq, ff, fb):
    L = dc.shape[0]
    nblk = L // BLK
    G = _block_group(nblk)
    steps = nblk // G

    def body(dc_ref, dcq_ref, ff_ref, b_ref, dff_ref, dffb_ref, dfb_ref, carry):
        @pl.when(pl.program_id(0) == 0)
        def _():
            carry[...] = jnp.zeros_like(carry)
            dfb_ref[...] = jnp.zeros_like(dfb_ref)

        tri = (_iota((BLK, BLK), 0) <= _iota((BLK, BLK), 1)).astype(BF16)
        live = _iota((BLK, BLK), 1) < FOX_HEADS
        run, dfb = carry[...], dfb_ref[...]
        for b in reversed(range(G)):
            rows = slice(b * BLK, (b + 1) * BLK)
            d = dc_ref[rows, :] + jnp.concatenate([dcq_ref[b], jnp.zeros((BLK - 8, BLK), F32)], axis=0).T
            hi, mid, lo = _split3(d)
            dlf = _dot(tri, hi) + _dot(tri, mid) + _dot(tri, lo) + run
            run = run + jnp.sum(d, axis=0, keepdims=True)
            z = ff_ref[rows, :] + b_ref[...]
            dff = jnp.where(live, dlf * jax.nn.sigmoid(-z), 0.0)
            dff_ref[rows, :] = dff
            dffb_ref[rows, :] = dff.astype(BF16)
            dfb = dfb + jnp.sum(dff, axis=0, keepdims=True)
        carry[...] = run
        dfb_ref[...] = dfb

    rev = lambda i: (steps - 1 - i, 0)
    return pl.pallas_call(
        body, name="b_foxpost", grid=(steps,),
        in_specs=[pl.BlockSpec((G * BLK, BLK), rev), pl.BlockSpec((G, 8, BLK), lambda i: (steps - 1 - i, 0, 0)),
                  pl.BlockSpec((G * BLK, BLK), rev), _full((1, BLK))],
        out_specs=[pl.BlockSpec((G * BLK, BLK), rev), pl.BlockSpec((G * BLK, BLK), rev), _full((1, BLK))],
        out_shape=[jax.ShapeDtypeStruct((L, BLK), F32), jax.ShapeDtypeStruct((L, BLK), BF16),
                   jax.ShapeDtypeStruct((1, BLK), F32)],
        scratch_shapes=[pltpu.VMEM((1, BLK), F32)],
        compiler_params=_params(("arbitrary",)),
    )(dc, dcq, ff, fb)


def _inproj_bwd(dpr, dpf, dffb, w_main, w_ff, h0, g, dh1, scatter=()):
    L = h0.shape[0]
    S = L - BLK
    tm = _row_tile(S, (512, 256, 128))
    nt = S // tm
    ns = len(scatter)
    operands = (dpr, dpf, dffb, h0, dh1)

    def body(*refs):
        lead, tile = refs[0:5], refs[5:10]
        wm_ref, wf_ref, g_ref = refs[10:13]
        rest = refs[13:]
        s_in, (dlead_ref, dx_ref, dg_ref), s_out = rest[:ns], rest[ns:ns + 3], rest[ns + 3:2 * ns + 3]
        i = pl.program_id(0)

        def rows_bwd(dpr_ref, dpf_ref, dff_ref, h_ref, dh1_ref):
            dn = (_dot_nt(dpr_ref[...], wm_ref[:, 0:RET_W]) + _dot_nt(dpf_ref[...], wm_ref[:, RET_W:MAIN_W])
                  + _dot_nt(dff_ref[...], wf_ref[...]))
            h = h_ref[...]
            r = lax.rsqrt(jnp.mean(h * h, axis=-1, keepdims=True) + EPS)
            yn = h * r
            dyn = dn * g_ref[...]
            dh0 = dh1_ref[...] + r * (dyn - yn * jnp.mean(dyn * yn, axis=-1, keepdims=True))
            return dh0, jnp.sum(dn * yn, axis=0, keepdims=True)

        @pl.when(i == 0)
        def _():
            if ns:
                for cp in _scatter_copies(s_in, s_out, *rest[2 * ns + 3:]):
                    cp.start()
            dlead_ref[...], dg_ref[...] = rows_bwd(*lead)

        dx_ref[...], dg_tile = rows_bwd(*tile)
        dg_ref[...] = dg_ref[...] + dg_tile

        if ns:
            @pl.when(i == nt - 1)
            def _():
                copies = _scatter_copies(s_in, s_out, *rest[2 * ns + 3:])
                for cp in copies:
                    cp.wait_recv()
                for cp in copies:
                    cp.wait_send()

    lead_spec = lambda a: pl.BlockSpec((BLK, a.shape[1]), lambda i: (0, 0))
    tile_spec = lambda a: pl.BlockSpec((pl.Element(tm), pl.Element(a.shape[1])),
                                       lambda i: (pl.multiple_of(BLK + i * tm, BLK), 0))
    return pl.pallas_call(
        body, name="b_inproj", grid=(nt,),
        in_specs=[lead_spec(a) for a in operands] + [tile_spec(a) for a in operands]
        + [_full((D_MODEL, MAIN_W)), _full((D_MODEL, BLK)), _full((1, D_MODEL))] + [_ANY] * ns,
        out_specs=[_full((BLK, D_MODEL)), pl.BlockSpec((tm, D_MODEL), lambda i: (i, 0)), _full((1, D_MODEL))]
        + [_ANY] * ns,
        out_shape=[jax.ShapeDtypeStruct((BLK, D_MODEL), F32), jax.ShapeDtypeStruct((S, D_MODEL), F32),
                   jax.ShapeDtypeStruct((1, D_MODEL), F32)] + _scatter_shapes(scatter),
        scratch_shapes=_scatter_semaphores(ns),
        compiler_params=_params(("arbitrary",)),
    )(*operands, *operands, w_main, w_ff, g, *scatter)


def _local_step(x, target, meta, attn_g, w_main, w_ff, fox_b, ret_g, w_out, ffn_g, w_up, conv_w, conv_b, w_down, final_g,
                late=None, mid=None, last=None):
    S = x.shape[0]
    L = S + PREFIX
    head = jnp.concatenate([jnp.zeros((N_PAD, D_MODEL), F32), meta], axis=0)
    fb = jnp.pad(fox_b, ((0, 0), (0, BLK - FOX_HEADS)))
    cos_t, sin_t = _rotary_tables(L)

    h0, n1, proj, ff = _rms_inproj(head, x, attn_g, w_main, w_ff)
    c, ctb = _fox_prep(ff, fb)
    mix_r, o_ret, states = _retention_fwd(proj, cos_t, sin_t, ret_g)
    if late is None:
        o_f, lse = _fox_fwd(proj, c, ctb)
    else:
        o_f, lse, *gathered = _fox_fwd(proj, c, ctb, gather=late[0])
        w_out, w_up, w_down = late[1](gathered)
    h1, n2, up, g_act, acc_saved = _outproj_up(mix_r, o_f, h0, w_out, ffn_g, w_up, conv_w, conv_b)
    dh2, dh2b, d_final_g, loss = _ffn_down_loss(g_act, w_down, h1, final_g, target)

    dacc, db = _ffn_bwd_gate(dh2b, w_down, acc_saved, up)
    dup, dh1, dh1b, dmix, d_ffn_g, dconv = _ffn_bwd_up(dacc, db, up, conv_w, w_up, h1, ffn_g, dh2, w_out)
    d_w_down = _wgrad(g_act, dh2b, "wgrad_down", tk=D_FF // 2)[0]
    d_w_up = _wgrad(n2, dup, "wgrad_up", tn=w_up.shape[2])
    d_w_out = jnp.concatenate([_wgrad(mix_r, dh1b, "wgrad_out_r")[0], _wgrad(o_f, dh1b, "wgrad_out_f")[0]], axis=0)

    early = () if mid is None else mid[0](d_w_out, d_w_up, d_w_down)
    dpr, d_ret_g, *from_sibling = _retention_bwd(dmix, o_ret, proj, cos_t, sin_t, ret_g, states, exchange=early)
    delta = _fox_delta(dmix, o_f)
    scatter = () if mid is None else mid[1](early, from_sibling)
    dpf, dc, dcq, *received = _fox_bwd(proj, dmix, c, ctb, lse, delta, scatter=scatter)
    dff, dffb, d_fox_b = _fox_post(dc, dcq, ff, fb)
    d_w_main = jnp.concatenate([_wgrad(n1, dpr, "wgrad_in_r")[0], _wgrad(n1, dpf, "wgrad_in_f")[0]], axis=1)
    d_w_ff = _wgrad(n1, dffb, "wgrad_in_ff")[0][:, :FOX_HEADS]
    scatter_in = () if last is None else last(d_w_main, d_w_ff)
    dlead, dx, d_attn_g, *received_in = _inproj_bwd(dpr, dpf, dffb, w_main, w_ff, h0, attn_g, dh1, scatter=scatter_in)

    return dict(
        loss=loss[0, 0], dx=dx, dmeta=dlead[N_PAD:], attn_g=d_attn_g, w_main=d_w_main,
        w_ff=d_w_ff, fox_b=d_fox_b[:, :FOX_HEADS], ret_g=d_ret_g, w_out=d_w_out, ffn_g=d_ffn_g,
        w_up=d_w_up, conv_w=dconv[0:3], conv_b=dconv[3:4], w_down=d_w_down, final_g=d_final_g,
        scatter=list(scatter_in) + list(scatter), received=list(received_in) + list(received))


_ANY = pl.BlockSpec(memory_space=pl.ANY)


def _place():
    return lax.axis_index("x"), lax.axis_index("y"), lax.axis_index("c")


def _other_chips(x, y):
    return [(1 - x, y), (x, 1 - y), (1 - x, 1 - y)]


def _allgather_semaphores(n):
    if n == 0:
        return []
    return [pltpu.SemaphoreType.DMA((3 * n,)), pltpu.SemaphoreType.DMA((3 * n,)), pltpu.SemaphoreType.DMA((n,))]


def _allgather_copies(ins, outs, send, recv, loc):
    n = len(ins)
    x, y, c = _place()
    mine = 2 * x + y
    peers = _other_chips(x, y)

    def remote(a, k, slot):
        return pltpu.make_async_remote_copy(
            src_ref=ins[a], dst_ref=outs[a].at[slot], send_sem=send.at[3 * a + k], recv_sem=recv.at[3 * a + k],
            device_id=(peers[k][0], peers[k][1], c), device_id_type=MESH)

    local = [pltpu.make_async_copy(ins[a], outs[a].at[mine], loc.at[a]) for a in range(n)]
    sends = [remote(a, k, mine) for a in range(n) for k in range(3)]
    recvs = [remote(a, k, 2 * peers[k][0] + peers[k][1]) for a in range(n) for k in range(3)]
    return local, sends, recvs


def _chip_allgather_halves(w, small):
    half = w.shape[0] // 2

    def body(w_ref, s_ref, wo_ref, so_ref, send, recv, fsend, frecv, ssend, srecv, loc):
        x, y, c = _place()
        mine = 2 * x + y
        peers = _other_chips(x, y)

        def fetch(k, slot):
            return pltpu.make_async_remote_copy(
                src_ref=w_ref.at[pl.ds(c * half, half)], dst_ref=wo_ref.at[slot, c], send_sem=send.at[k],
                recv_sem=recv.at[k], device_id=(peers[k][0], peers[k][1], c), device_id_type=MESH)

        def forward(k, which):
            slot = 2 * peers[k][0] + peers[k][1]
            return pltpu.make_async_remote_copy(
                src_ref=wo_ref.at[slot, which], dst_ref=wo_ref.at[slot, which], send_sem=fsend.at[k],
                recv_sem=frecv.at[k], device_id=(x, y, 1 - c), device_id_type=MESH)

        def small_copy(k, slot):
            return pltpu.make_async_remote_copy(
                src_ref=s_ref, dst_ref=so_ref.at[slot], send_sem=ssend.at[k], recv_sem=srecv.at[k],
                device_id=(peers[k][0], peers[k][1], c), device_id_type=MESH)

        local = pltpu.make_async_copy(s_ref, so_ref.at[mine], loc.at[0])
        sends = [fetch(k, mine) for k in range(3)] + [small_copy(k, mine) for k in range(3)]
        local.start()
        for cp in sends:
            cp.start()
        forwards = []
        for k in range(3):
            fetch(k, 2 * peers[k][0] + peers[k][1]).wait_recv()
            forwards.append(forward(k, c))
            forwards[-1].start()
        for k in range(3):
            forward(k, 1 - c).wait_recv()
            small_copy(k, 2 * peers[k][0] + peers[k][1]).wait_recv()
        for cp in sends + forwards:
            cp.wait_send()
        local.wait()

    three = pltpu.SemaphoreType.DMA((3,))
    return pl.pallas_call(
        body, name="ag_weights", in_specs=[_ANY] * 2, out_specs=[_ANY] * 2,
        out_shape=[jax.ShapeDtypeStruct((N_CHIPS, 2, half, w.shape[1]), w.dtype),
                   jax.ShapeDtypeStruct((N_CHIPS,) + small.shape, small.dtype)],
        scratch_shapes=[three, three, three, three, three, three, pltpu.SemaphoreType.DMA((1,))],
    )(w, small)


def _chip_allgather(arrays):
    n = len(arrays)

    def body(*refs):
        local, sends, recvs = _allgather_copies(refs[:n], refs[n:2 * n], *refs[2 * n:])
        for cp in local + sends:
            cp.start()
        for cp in recvs:
            cp.wait_recv()
        for cp in sends:
            cp.wait_send()
        for cp in local:
            cp.wait()

    return pl.pallas_call(
        body, name="ag_weights", in_specs=[_ANY] * n, out_specs=[_ANY] * n,
        out_shape=[jax.ShapeDtypeStruct((N_CHIPS,) + a.shape, a.dtype) for a in arrays],
        scratch_shapes=_allgather_semaphores(n),
    )(*arrays)


def _sibling_halves(grads):
    n = len(grads)

    def body(*refs):
        sends, recvs = _sibling_half_copies(refs[:n], refs[n:2 * n], *refs[2 * n:])
        for cp in sends:
            cp.start()
        for cp in recvs:
            cp.wait_recv()
        for cp in sends:
            cp.wait_send()

    return pl.pallas_call(
        body, name="rs_sibling", in_specs=[_ANY] * n, out_specs=[_ANY] * n,
        out_shape=_sibling_half_shapes(grads), scratch_shapes=_sibling_half_semaphores(n),
    )(*grads)


def _sibling_half_shapes(grads):
    return [jax.ShapeDtypeStruct((N_CHIPS, g.shape[1] // 2, g.shape[2]), g.dtype) for g in grads]


def _sibling_half_semaphores(n):
    return [pltpu.SemaphoreType.DMA((n,)), pltpu.SemaphoreType.DMA((n,))] if n else []


def _sibling_half_copies(ins, outs, send, recv):
    x, y, c = _place()

    def half_copy(a, which):
        half = ins[a].shape[1] // 2
        return pltpu.make_async_remote_copy(
            src_ref=ins[a].at[pl.ds(0, N_CHIPS), pl.ds(which * half, half)], dst_ref=outs[a],
            send_sem=send.at[a], recv_sem=recv.at[a], device_id=(x, y, 1 - c), device_id_type=MESH)

    return [half_copy(a, 1 - c) for a in range(len(ins))], [half_copy(a, c) for a in range(len(ins))]


def _scatter_shapes(parts):
    return [jax.ShapeDtypeStruct((3,) + p.shape[1:], p.dtype) for p in parts]


def _scatter_semaphores(n):
    return [pltpu.SemaphoreType.DMA((3 * n,)), pltpu.SemaphoreType.DMA((3 * n,))] if n else []


def _scatter_copies(ins, outs, send, recv):
    x, y, c = _place()
    peers = _other_chips(x, y)
    return [pltpu.make_async_remote_copy(
        src_ref=ins[a].at[2 * peers[k][0] + peers[k][1]], dst_ref=outs[a].at[k], send_sem=send.at[3 * a + k],
        recv_sem=recv.at[3 * a + k], device_id=(peers[k][0], peers[k][1], c), device_id_type=MESH)
        for a in range(len(ins)) for k in range(3)]


def _sibling_allgather(bufs, small):
    n = len(bufs)

    def body(*refs):
        small_in, outs, small_out = refs[n], refs[n + 1:2 * n + 1], refs[2 * n + 1]
        send, recv, s_send, s_recv, loc = refs[2 * n + 2:]
        x, y, c = _place()
        me = 4 * x + 2 * y + c

        def remote(a, which):
            return pltpu.make_async_remote_copy(
                src_ref=outs[a].at[which], dst_ref=outs[a].at[which], send_sem=send.at[a], recv_sem=recv.at[a],
                device_id=(x, y, 1 - c), device_id_type=MESH)

        def peer_of(r):
            return tuple(1 - v if (r >> b) & 1 else v for v, b in ((x, 2), (y, 1), (c, 0)))

        def small_copy(r, slot):
            return pltpu.make_async_remote_copy(
                src_ref=small_in, dst_ref=small_out.at[slot], send_sem=s_send.at[r - 1], recv_sem=s_recv.at[r - 1],
                device_id=peer_of(r), device_id_type=MESH)

        local = pltpu.make_async_copy(small_in, small_out.at[me], loc.at[0])
        sends = [remote(a, c) for a in range(n)] + [small_copy(r, me) for r in range(1, N_DEV)]
        local.start()
        for cp in sends:
            cp.start()
        for r in range(1, N_DEV):
            px, py, pc = peer_of(r)
            small_copy(r, 4 * px + 2 * py + pc).wait_recv()
        for a in range(n):
            remote(a, 1 - c).wait_recv()
        for cp in sends:
            cp.wait_send()
        local.wait()

    outs = pl.pallas_call(
        body, name="ag_sibling", in_specs=[_ANY] * (n + 1), out_specs=[_ANY] * (n + 1),
        out_shape=[jax.ShapeDtypeStruct(b.shape, b.dtype) for b in bufs]
        + [jax.ShapeDtypeStruct((N_DEV,) + small.shape, small.dtype)],
        input_output_aliases={a: a for a in range(n)},
        scratch_shapes=[pltpu.SemaphoreType.DMA((n,)), pltpu.SemaphoreType.DMA((n,)),
                        pltpu.SemaphoreType.DMA((N_DEV - 1,)), pltpu.SemaphoreType.DMA((N_DEV - 1,)),
                        pltpu.SemaphoreType.DMA((1,))],
    )(*bufs, small)
    return [o.reshape(2 * o.shape[1], o.shape[2]) for o in outs[:n]], outs[n]


def _pair_add(full, recv, core, name):
    _, R, C = full.shape
    half = R // 2

    def body(core_ref, a_ref, b_ref, o_ref):
        o_ref[...] = (a_ref[...] + b_ref[...]).astype(BF16)

    return pl.pallas_call(
        body, name=name,
        grid_spec=pltpu.PrefetchScalarGridSpec(
            num_scalar_prefetch=1, grid=(N_CHIPS,),
            in_specs=[pl.BlockSpec((1, half, C), lambda j, core_ref: (j, core_ref[0], 0)),
                      pl.BlockSpec((1, half, C), lambda j, core_ref: (j, 0, 0))],
            out_specs=pl.BlockSpec((1, half, C), lambda j, core_ref: (j, 0, 0))),
        out_shape=jax.ShapeDtypeStruct((N_CHIPS, half, C), BF16),
        compiler_params=_params(("parallel",)),
    )(core, full, recv)


def _sum_slots(q, name, tiles=2):
    n, R, C = q.shape
    tr = R // tiles

    def body(q_ref, o_ref):
        acc = q_ref[0].astype(F32)
        for j in range(1, n):
            acc = acc + q_ref[j].astype(F32)
        o_ref[...] = acc

    return pl.pallas_call(
        body, name=name, grid=(tiles,),
        in_specs=[pl.BlockSpec((n, tr, C), lambda i: (0, i, 0))],
        out_specs=pl.BlockSpec((tr, C), lambda i: (i, 0)),
        out_shape=jax.ShapeDtypeStruct((R, C), F32),
        compiler_params=_params(("parallel",)),
    )(q)


def _sum_partials(own_all, recv, place, name, tiles=2):
    _, R, C = own_all.shape
    tr = R // tiles

    def body(place_ref, own_ref, r_ref, o_ref):
        acc = own_ref[0].astype(F32)
        for k in range(3):
            acc = acc + r_ref[k].astype(F32)
        o_ref[0] = acc

    return pl.pallas_call(
        body, name=name,
        grid_spec=pltpu.PrefetchScalarGridSpec(
            num_scalar_prefetch=1, grid=(tiles,),
            in_specs=[pl.BlockSpec((1, tr, C), lambda i, place_ref: (place_ref[0], i, 0)),
                      pl.BlockSpec((3, tr, C), lambda i, place_ref: (0, i, 0))],
            out_specs=pl.BlockSpec((1, tr, C), lambda i, place_ref: (place_ref[1], i, 0))),
        out_shape=jax.ShapeDtypeStruct((2, R, C), F32),
        compiler_params=_params(("parallel",)),
    )(place, own_all, recv)


def _adamw(w, g, m, v, name, tiles=4):
    R, tail = w.shape[0], w.shape[1:]
    assert R % tiles == 0
    tr = R // tiles

    def body(w_ref, g_ref, m_ref, v_ref, go_ref, d_ref, m2_ref, v2_ref):
        g_ = g_ref[...]
        go_ref[...] = g_
        m2 = ADAM_B1 * m_ref[...] + (1.0 - ADAM_B1) * g_
        v2 = ADAM_B2 * v_ref[...] + (1.0 - ADAM_B2) * (g_ * g_)
        m_hat = m2 / (1.0 - ADAM_B1 ** ADAM_STEP)
        v_hat = v2 / (1.0 - ADAM_B2 ** ADAM_STEP)
        d_ref[...] = -ADAM_LR * (m_hat / (jnp.sqrt(v_hat) + ADAM_EPS) + ADAM_WD * w_ref[...])
        m2_ref[...] = m2
        v2_ref[...] = v2

    spec = pl.BlockSpec((tr,) + tail, lambda i: (i,) + (0,) * len(tail))
    return pl.pallas_call(
        body, name=name, grid=(tiles,), in_specs=[spec] * 4, out_specs=[spec] * 4,
        out_shape=[jax.ShapeDtypeStruct(w.shape, F32)] * 4,
        compiler_params=_params(("parallel",)),
    )(w, g, m, v)


def _row_vector_tiles(n, most=80):
    return next(t for t in range(1, n + 1) if n % t == 0 and n // t <= most)


def _pack_rows(pieces, rows):
    flat = jnp.concatenate([jnp.pad(p.reshape(-1).astype(F32), (0, (-p.size) % D_MODEL)) for p in pieces])
    return jnp.pad(flat, (0, rows * D_MODEL - flat.size)).reshape(rows, D_MODEL)


def _unpack_rows(pack, shapes):
    flat = pack.reshape(-1)
    out, off = [], 0
    for shp in shapes:
        size = int(np.prod(shp))
        out.append(flat[off:off + size].reshape(shp))
        off += size + (-size) % D_MODEL
    return out


def _kernel_order(w):
    parts = [w[:, 0:RET_W]]
    for p in range(FOX_HEADS // 2):
        parts += [w[:, RET_W + part * 512 + p * BLK:RET_W + part * 512 + (p + 1) * BLK] for part in range(3)]
    return jnp.concatenate(parts, axis=1)


def _reference_order(g_main, g_ff):
    parts = [g_main[:, 0:RET_W]]
    for part in range(3):
        parts += [g_main[:, RET_W + 384 * p + part * BLK:RET_W + 384 * p + (part + 1) * BLK] for p in range(FOX_HEADS // 2)]
    return jnp.concatenate(parts + [g_ff], axis=1)


def kernel(x, meta_tokens, attn_norm_g, w_in, fox_forget_b, ret_norm_g, w_out, ffn_norm_g, w_up, conv_w, conv_b, w_down, final_norm_g, loss_target, m_meta_tokens, m_attn_norm_g, m_w_in, m_fox_forget_b, m_ret_norm_g, m_w_out, m_ffn_norm_g, m_w_up, m_conv_w, m_conv_b, m_w_down, m_final_norm_g, v_meta_tokens, v_attn_norm_g, v_w_in, v_fox_forget_b, v_ret_norm_g, v_w_out, v_ffn_norm_g, v_w_up, v_conv_w, v_conv_b, v_w_down, v_final_norm_g):
    chip = 2 * lax.axis_index("x") + lax.axis_index("y")
    core = lax.axis_index("c")
    meta_w, conv_sw = meta_tokens.shape[1], conv_w.shape[2]

    small_w = _pack_rows([meta_tokens, conv_w[0]], 8)
    w_in_b = w_in[0].astype(BF16)
    g_in, g_small = _chip_allgather_halves(w_in_b, small_w)
    g_in = lax.dynamic_update_slice(g_in.reshape((N_CHIPS,) + w_in_b.shape), w_in_b[None], (chip, 0, 0))
    w_in_full = g_in.transpose(1, 0, 2).reshape(D_MODEL, IN_WIDTH)
    w_main = _kernel_order(w_in_full)
    w_ff = jnp.pad(w_in_full[:, MAIN_W:], ((0, 0), (0, BLK - FOX_HEADS)))
    small_parts = [_unpack_rows(g_small[j], [meta_tokens.shape, conv_w.shape[1:]]) for j in range(N_CHIPS)]
    meta_full = jnp.concatenate([sp[0] for sp in small_parts], axis=1)
    conv_w_full = jnp.concatenate([sp[1] for sp in small_parts], axis=1)

    core_idx = core.reshape(1).astype(jnp.int32)
    place = jnp.stack([chip, core]).astype(jnp.int32)

    def assemble(gathered):
        g_out, g_up, g_down = gathered
        return g_out.reshape(D_MODEL, D_MODEL), g_up, g_down.reshape(D_FF, D_MODEL)

    def early_arrays(d_w_out, d_w_up, d_w_down):
        return [d_w_out.reshape(N_CHIPS, -1, D_MODEL), d_w_up, d_w_down.reshape(N_CHIPS, -1, D_MODEL)]

    def in_sums(d_w_main, d_w_ff):
        g_in_full = _reference_order(d_w_main, d_w_ff).reshape(D_MODEL, N_CHIPS, -1).transpose(1, 0, 2)
        (from_sib,) = _sibling_halves([g_in_full])
        return [_pair_add(g_in_full, from_sib, core_idx, "pair_add_in")]

    def early_sums(early, from_sib):
        return [_pair_add(g, r, core_idx, "pair_add_" + nm) for g, r, nm in zip(early, from_sib, ("out", "up", "down"))]

    out = _local_step(x[0], loss_target[0], meta_full, attn_norm_g, w_main, w_ff, fox_forget_b, ret_norm_g,
                      None, ffn_norm_g, None, conv_w_full, conv_b, None, final_norm_g[None],
                      late=([w_out[0].astype(BF16), w_up[0].astype(BF16), w_down[0].astype(BF16)], assemble),
                      mid=(early_arrays, early_sums), last=in_sums)

    small_shapes = [(1, D_MODEL), (1, D_MODEL), (1, D_MODEL), (1, 512 + FOX_HEADS + 1), (1, D_FF), (N_META, D_MODEL), (3, D_FF)]
    small = _pack_rows([out["attn_g"], out["ffn_g"], out["final_g"],
                        jnp.concatenate([out["ret_g"], out["fox_b"], out["loss"].reshape(1, 1)], axis=1),
                        out["conv_b"], out["dmeta"], out["conv_w"]], 32)
    names = ("in", "out", "up", "down")
    totals = [_sum_partials(s, q, place, "sum_chips_" + nm) for s, q, nm in zip(out["scatter"], out["received"], names)]
    (grad_in, grad_out, grad_up, grad_down), small_all = _sibling_allgather(totals, small)
    s_attn, s_ffn, s_final, s_misc, s_conv_b, s_meta, s_conv_w = _unpack_rows(
        _sum_slots(small_all, "sum_small", tiles=1), small_shapes)
    loss = s_misc[0, 512 + FOX_HEADS]
    small_grads = [lax.dynamic_slice_in_dim(s_meta, chip * meta_w, meta_w, axis=1), s_attn, s_misc[:, 512:512 + FOX_HEADS],
                   s_misc[:, :512], s_ffn, lax.dynamic_slice_in_dim(s_conv_w, chip * conv_sw, conv_sw, axis=1)[None],
                   s_conv_b, s_final[0]]

    big_w = [(w_out, m_w_out, v_w_out, grad_out, "adamw_out"), (w_up, m_w_up, v_w_up, grad_up, "adamw_up"),
             (w_down, m_w_down, v_w_down, grad_down, "adamw_down")]
    big_res = [[r[None] for r in _adamw(w[0], g, m[0], v[0], nm)] for w, m, v, g, nm in big_w]
    as_rows = lambda a: jnp.transpose(a, (2, 0, 1))
    in_rows = _adamw(as_rows(w_in), grad_in.T[:, None, :], as_rows(m_w_in), as_rows(v_w_in), "adamw_in",
                     tiles=_row_vector_tiles(w_in.shape[2]))
    big_res.insert(0, [jnp.transpose(r, (1, 2, 0)) for r in in_rows])
    small_w_list = [meta_tokens, attn_norm_g, fox_forget_b, ret_norm_g, ffn_norm_g, conv_w, conv_b, final_norm_g]
    small_m = [m_meta_tokens, m_attn_norm_g, m_fox_forget_b, m_ret_norm_g, m_ffn_norm_g, m_conv_w, m_conv_b, m_final_norm_g]
    small_v = [v_meta_tokens, v_attn_norm_g, v_fox_forget_b, v_ret_norm_g, v_ffn_norm_g, v_conv_w, v_conv_b, v_final_norm_g]
    shapes = [a.shape for a in small_w_list]
    packs = [_pack_rows(lst, 16) for lst in (small_w_list, small_grads, small_m, small_v)]
    small_res = [_unpack_rows(r, shapes) for r in _adamw(*packs, "adamw_small", tiles=1)[1:]]
    small_grads = [g.reshape(s) for g, s in zip(small_grads, shapes)]

    def ordered(kind):
        sm = small_grads if kind == 0 else small_res[kind - 1]
        bg = [r[kind] for r in big_res]
        return [sm[0], sm[1], bg[0], sm[2], sm[3], bg[1], sm[4], bg[2], sm[5], sm[6], bg[3], sm[7]]

    return (loss, out["dx"][None], *ordered(0), *ordered(1), *ordered(2), *ordered(3))
```

```python
import functools

import numpy as np
import jax
import jax.numpy as jnp
from jax import lax
from jax.experimental import pallas as pl
from jax.experimental.pallas import tpu as pltpu

F32 = jnp.float32
BF16 = jnp.bfloat16

D_MODEL = 1024
N_META = 16
BLK = 128
UNIT = 2 * BLK
FOX_PAIRS = 2
WIDE = 4
CHUNK = 64
N_PAD = BLK - N_META
PREFIX = BLK
RET_HEADS = 4
FOX_HEADS = 8
HEAD_LANES = 64
D_FF = 2816
ROPE_BASE = 10000.0
EPS = 1e-6
NEG = -1e30
LOG2E = 1.4426950408889634
RET_W = 1536
FOX_W = 1536
MAIN_W = RET_W + FOX_W
IN_WIDTH = MAIN_W + FOX_HEADS
N_CHIPS = 4
N_DEV = 8

ADAM_LR = 0.001
ADAM_B1 = 0.9
ADAM_B2 = 0.999
ADAM_EPS = 1e-08
ADAM_WD = 0.01
ADAM_STEP = 10

MESH = pl.DeviceIdType.MESH
VMEM_LIMIT_MB = 56

_NT = (((1,), (1,)), ((), ()))
_TN = (((0,), (0,)), ((), ()))


def _dot(a, b):
    return jnp.dot(a, b, preferred_element_type=F32)


def _dot_nt(a, b):
    return lax.dot_general(a, b, _NT, preferred_element_type=F32)


def _dot_tn(a, b):
    return lax.dot_general(a, b, _TN, preferred_element_type=F32)


def _params(dims=None, vmem_mb=VMEM_LIMIT_MB):
    kw = dict(vmem_limit_bytes=vmem_mb << 20)
    if dims is not None:
        kw["dimension_semantics"] = dims
    return pltpu.CompilerParams(**kw)


def _row_tile(n, prefs=(384, 256, 128)):
    for t in prefs:
        if n % t == 0:
            return t
    raise ValueError(f"no row tile for {n}")


def _iota(shape, dim):
    return lax.broadcasted_iota(jnp.int32, shape, dim)


def _pick_row(tile, row):
    sub = _iota(tile.shape, 0)
    return jnp.sum(jnp.where(sub == row, tile, 0.0), axis=0, keepdims=True)


def _split3(x):
    hi = x.astype(BF16)
    r1 = x - hi.astype(F32)
    mid = r1.astype(BF16)
    lo = (r1 - mid.astype(F32)).astype(BF16)
    return hi, mid, lo


def _full(shape):
    nd = len(shape)
    return pl.BlockSpec(shape, lambda *_: (0,) * nd)


def _in_perm():
    cols = list(range(RET_W))
    for p in range(FOX_HEADS // 2):
        for part in range(3):
            start = RET_W + part * 512 + p * BLK
            cols += list(range(start, start + BLK))
    return np.asarray(cols, np.int32)


def _rotary_tables(L):
    half = HEAD_LANES // 2
    inv = 1.0 / (ROPE_BASE ** (jnp.arange(half, dtype=F32) / half))
    ang = jnp.arange(L).astype(F32)[:, None] * inv[None, :]
    cos, sin = jnp.cos(ang), jnp.sin(ang)
    cos_t = jnp.tile(cos, (1, 4))
    sin_t = jnp.tile(jnp.concatenate([-sin, sin], axis=1), (1, 2))
    return cos_t, sin_t


def _decay_tables():
    gam = 1.0 - 2.0 ** (-5.0 - np.arange(RET_HEADS, dtype=np.float64))
    n = np.arange(BLK)
    same_or_past = (n[:, None] // CHUNK) >= (n[None, :] // CHUNK)
    dist = np.abs(n[:, None] - n[None, :])
    dmat = np.stack([np.where(same_or_past, g ** dist, 0.0) for g in gam]).astype(np.float32)
    lane_head = np.arange(BLK) // HEAD_LANES
    wq = np.stack([gam[2 * p + lane_head][None, :] ** (n[:, None] + 1.0) for p in range(2)]).astype(np.float32)
    wk = np.stack([gam[2 * p + lane_head][None, :] ** (BLK - 1.0 - n[:, None]) for p in range(2)]).astype(np.float32)
    g_blk = tuple(float(g ** BLK) for g in gam)
    return jnp.asarray(dmat), jnp.asarray(wq), jnp.asarray(wk), g_blk


def _shifted_blocks(tm):
    nb = tm // BLK
    return [pl.BlockSpec((BLK, D_MODEL), lambda i, j=j: (jnp.maximum(nb * i + j - 1, 0), 0)) for j in range(nb)]


def _rms_inproj(head, x, g, w_main, w_ff):
    L = x.shape[0] + BLK
    tm = _row_tile(L)
    nb = tm // BLK

    def body(head_ref, *refs):
        x_refs, (g_ref, wm_ref, wf_ref, h_ref, n_ref, p_ref, ff_ref) = refs[:nb], refs[nb:]
        parts = [r[...] for r in x_refs]
        parts[0] = jnp.where(pl.program_id(0) == 0, head_ref[...], parts[0])
        h = jnp.concatenate(parts, axis=0)
        h_ref[...] = h
        r = lax.rsqrt(jnp.mean(h * h, axis=-1, keepdims=True) + EPS)
        n = (h * r * g_ref[...]).astype(BF16)
        n_ref[...] = n
        p_ref[...] = _dot(n, wm_ref[...]).astype(BF16)
        ff_ref[...] = _dot(n, wf_ref[...])

    rows = lambda w: pl.BlockSpec((tm, w), lambda i: (i, 0))
    return pl.pallas_call(
        body, name="f_inproj", grid=(L // tm,),
        in_specs=[_full((BLK, D_MODEL))] + _shifted_blocks(tm)
        + [_full((1, D_MODEL)), _full((D_MODEL, MAIN_W)), _full((D_MODEL, BLK))],
        out_specs=[rows(D_MODEL), rows(D_MODEL), rows(MAIN_W), rows(BLK)],
        out_shape=[jax.ShapeDtypeStruct((L, D_MODEL), F32), jax.ShapeDtypeStruct((L, D_MODEL), BF16),
                   jax.ShapeDtypeStruct((L, MAIN_W), BF16), jax.ShapeDtypeStruct((L, BLK), F32)],
        compiler_params=_params(("parallel",)),
    )(head, *([x] * nb), g, w_main, w_ff)


def _block_group(nblk):
    return 3 if nblk % 3 == 0 else 1


def _fox_prep(ff, fb):
    L = ff.shape[0]
    nblk = L // BLK
    G = _block_group(nblk)

    def body(ff_ref, b_ref, c_ref, ct_ref, carry):
        @pl.when(pl.program_id(0) == 0)
        def _():
            carry[...] = jnp.zeros_like(carry)

        tri = (_iota((BLK, BLK), 0) >= _iota((BLK, BLK), 1)).astype(BF16)
        live = _iota((BLK, BLK), 1) < FOX_HEADS
        run = carry[...]
        for b in range(G):
            z = ff_ref[b * BLK:(b + 1) * BLK, :] + b_ref[...]
            lf = jnp.where(live, jnp.minimum(z, 0.0) - jnp.log1p(jnp.exp(-jnp.abs(z))), 0.0)
            hi, mid, lo = _split3(lf)
            cs = (_dot(tri, hi) + _dot(tri, mid) + _dot(tri, lo) + run) * LOG2E
            c_ref[b * BLK:(b + 1) * BLK, :] = cs
            ct_ref[b] = cs.T[0:8, :]
            run = run + jnp.sum(lf, axis=0, keepdims=True)
        carry[...] = run

    return pl.pallas_call(
        body, name="f_foxprep", grid=(nblk // G,),
        in_specs=[pl.BlockSpec((G * BLK, BLK), lambda i: (i, 0)), _full((1, BLK))],
        out_specs=[pl.BlockSpec((G * BLK, BLK), lambda i: (i, 0)), pl.BlockSpec((G, 8, BLK), lambda i: (i, 0, 0))],
        out_shape=[jax.ShapeDtypeStruct((L, BLK), F32), jax.ShapeDtypeStruct((nblk, 8, BLK), F32)],
        scratch_shapes=[pltpu.VMEM((1, BLK), F32)],
        compiler_params=_params(("arbitrary",)),
    )(ff, fb)


def _rot_fns(cos, sin):
    lane = _iota((BLK, BLK), 1)
    first = (lane & (HEAD_LANES - 1)) < HEAD_LANES // 2

    def swap(x):
        return jnp.where(first, pltpu.roll(x, BLK - 32, 1), pltpu.roll(x, 32, 1))

    def rot(x):
        return x * cos + swap(x) * sin

    def rot_t(dy):
        return dy * cos + swap(dy * sin)

    return rot, rot_t


def _retention_fwd(proj, cos_t, sin_t, ret_g):
    L = proj.shape[0]
    nblk = L // BLK
    G = _block_group(nblk)
    dmat, wq_t, wk_t, g_blk = _decay_tables()

    def body(q_ref, k_ref, v_ref, gate_ref, cos_ref, sin_ref, d_ref, wq_ref, wk_ref, rg_ref,
             mix_ref, o_ref, rs_ref, state):
        @pl.when(pl.program_id(0) == 0)
        def _():
            state[...] = jnp.zeros_like(state)

        lane = _iota((BLK, BLK), 1)
        sub = _iota((BLK, BLK), 0)
        for b in range(G):
            rows = slice(b * BLK, (b + 1) * BLK)
            rot, _ = _rot_fns(cos_ref[rows, :], sin_ref[rows, :])
            for p in range(2):
                qr = rot(q_ref[rows, p * BLK:(p + 1) * BLK].astype(F32))
                kr = rot(k_ref[rows, p * BLK:(p + 1) * BLK].astype(F32)) * (HEAD_LANES ** -0.5)
                kr_b = kr.astype(BF16)
                qw = (qr * wq_ref[p]).astype(BF16)
                kw = (kr * wk_ref[p]).astype(BF16)
                for e in range(2):
                    h = 2 * p + e
                    cols = slice(h * BLK, (h + 1) * BLK)
                    qm = jnp.where((lane >> 6) == e, qr, 0.0).astype(BF16)
                    s = _dot_nt(qm, kr_b) * d_ref[h]
                    vh = v_ref[rows, cols]
                    st = state[h]
                    rs_ref[b, h] = st
                    o = _dot(s.astype(BF16), vh) + _dot(qw, st.astype(BF16))
                    u = jnp.where((sub >> 6) == e, _dot_tn(kw, vh), 0.0)
                    state[h] = g_blk[h] * st + u
                    rn = lax.rsqrt(jnp.mean(o * o, axis=-1, keepdims=True) + EPS)
                    gate = gate_ref[rows, cols].astype(F32)
                    o_ref[rows, cols] = o
                    mix_ref[rows, cols] = (o * rn * rg_ref[:, cols] * (gate * jax.nn.sigmoid(gate))).astype(BF16)

    row = lambda c: (lambda i: (i, c))
    return pl.pallas_call(
        body, name="f_retention", grid=(nblk // G,),
        in_specs=[pl.BlockSpec((G * BLK, 256), row(0)), pl.BlockSpec((G * BLK, 256), row(1)),
                  pl.BlockSpec((G * BLK, 512), row(1)), pl.BlockSpec((G * BLK, 512), row(2)),
                  pl.BlockSpec((G * BLK, BLK), row(0)), pl.BlockSpec((G * BLK, BLK), row(0)),
                  _full((RET_HEADS, BLK, BLK)), _full((2, BLK, BLK)), _full((2, BLK, BLK)), _full((1, 512))],
        out_specs=[pl.BlockSpec((G * BLK, 512), row(0)), pl.BlockSpec((G * BLK, 512), row(0)),
                   pl.BlockSpec((G, RET_HEADS, BLK, BLK), lambda i: (i, 0, 0, 0))],
        out_shape=[jax.ShapeDtypeStruct((L, 512), BF16), jax.ShapeDtypeStruct((L, 512), F32),
                   jax.ShapeDtypeStruct((nblk, RET_HEADS, BLK, BLK), F32)],
        scratch_shapes=[pltpu.VMEM((RET_HEADS, BLK, BLK), F32)],
        compiler_params=_params(("arbitrary",)),
    )(proj, proj, proj, proj, cos_t, sin_t, dmat, wq_t, wk_t, ret_g)


def _fox_units(L):
    nblk = L // BLK
    assert L % BLK == 0 and nblk % 2 == 1, "sequence must be one 128-row block plus whole 256-row tiles"
    return nblk, (nblk - 1) // 2


def _fox_tile_masks():
    sub, lane = _iota((BLK, BLK), 0), _iota((BLK, BLK), 1)
    valid = _iota((BLK, UNIT), 0) >= N_PAD
    diag = _iota((UNIT, UNIT), 0) <= _iota((UNIT, UNIT), 1)
    r, q = _iota((BLK + UNIT, UNIT), 0), _iota((BLK + UNIT, UNIT), 1)
    first_and_diag = ((r < BLK) & (r >= N_PAD)) | ((r >= BLK) & (r - BLK <= q))
    return dict(first=(sub <= lane) & (sub >= N_PAD), valid=valid, diag=diag, first_and_diag=first_and_diag)


def _fox_fwd(proj, c, ctb, gather=()):
    L = proj.shape[0]
    nblk, nu = _fox_units(L)
    scale = HEAD_LANES ** -0.5 * LOG2E
    ng = len(gather)
    steps = FOX_HEADS // (2 * FOX_PAIRS)

    def body(qkv_ref, c_ref, ct_ref, *rest):
        g_in, (of_ref, lse_ref), g_out = rest[:ng], rest[ng:ng + 2], rest[ng + 2:2 * ng + 2]
        vt, csb = rest[2 * ng + 2:2 * ng + 4]
        p = pl.program_id(0)
        heads = [(pp, e, 2 * FOX_PAIRS * p + 2 * pp + e) for pp in range(FOX_PAIRS) for e in range(2)]

        @pl.when(p == 0)
        def _():
            lse_ref[...] = jnp.zeros_like(lse_ref)
            if ng:
                local, sends, _ = _allgather_copies(g_in, g_out, *rest[2 * ng + 4:])
                for cp in local + sends:
                    cp.start()

        lane = _iota((BLK, BLK), 1)
        sub8 = _iota((8, BLK), 0)
        masks = _fox_tile_masks()

        def pre(j, carry):
            off = pl.multiple_of(j * BLK, BLK)
            ct = c_ref[pl.ds(off, BLK), :]
            for pp in range(FOX_PAIRS):
                vt[pp, j] = qkv_ref[pl.ds(off, BLK), pp * 384 + 2 * BLK:pp * 384 + 3 * BLK].astype(F32).T.astype(BF16)
            for hh, (_, _, h) in enumerate(heads):
                col = jnp.sum(jnp.where(lane == h, ct, 0.0), axis=1, keepdims=True)
                csb[hh, j] = jnp.broadcast_to(col, (BLK, BLK))
            return carry

        lax.fori_loop(0, nblk, pre, 0)

        def attend(qblk, nq, n_whole):
            qlen = nq * BLK
            qoff = pl.multiple_of(qblk * BLK, BLK)
            qlane = _iota((qlen, BLK), 1)
            qs = [qkv_ref[pl.ds(qoff, qlen), pp * 384:pp * 384 + BLK].astype(F32) * scale for pp in range(FOX_PAIRS)]
            qm = [jnp.where((qlane >> 6) == e, qs[pp], 0.0).astype(BF16) for pp, e, _ in heads]
            ct_row = [jnp.concatenate([_pick_row(ct_ref[qblk + a], h) for a in range(nq)], axis=1) for _, _, h in heads]

            def step(segs, mask, st):
                blocks = [kblk + b for kblk, nk in segs for b in range(nk)]
                kts = []
                for pp in range(FOX_PAIRS):
                    kt = [qkv_ref[pl.ds(pl.multiple_of(kblk * BLK, BLK), nk * BLK), pp * 384 + BLK:pp * 384 + 2 * BLK]
                          for kblk, nk in segs]
                    kts.append(kt[0] if len(kt) == 1 else jnp.concatenate(kt, axis=0))
                out = []
                for hh, (pp, e, _) in enumerate(heads):
                    m, l, acc = st[3 * hh:3 * hh + 3]
                    s = _dot_nt(kts[pp], qm[hh])
                    t = jnp.concatenate([s[b * BLK:(b + 1) * BLK] - jnp.concatenate([csb[hh, blk]] * nq, axis=1)
                                         for b, blk in enumerate(blocks)], axis=0)
                    if mask is not None:
                        t = jnp.where(mask, t, NEG)
                    m_new = jnp.maximum(m, jnp.max(t, axis=0, keepdims=True) + ct_row[hh])
                    alpha = jnp.exp2(m - m_new)
                    pr = jnp.exp2(t - (m_new - ct_row[hh]))
                    l = alpha * l + jnp.sum(pr, axis=0, keepdims=True)
                    pr_b = pr.astype(BF16)
                    pv = None
                    for b, blk in enumerate(blocks):
                        part = _dot(vt[pp, blk, e * HEAD_LANES:(e + 1) * HEAD_LANES, :], pr_b[b * BLK:(b + 1) * BLK])
                        pv = part if pv is None else pv + part
                    out += [m_new, l, alpha * acc + pv]
                return tuple(out)

            st = (jnp.full((1, qlen), NEG, F32), jnp.zeros((1, qlen), F32),
                  jnp.zeros((HEAD_LANES, qlen), F32)) * len(heads)
            if nq == 1:
                st = step([(0, 1)], masks["first"], st)
            else:
                st = step([(0, 1), (qblk, 2)], masks["first_and_diag"], st)
                n_wide = n_whole // WIDE
                st = lax.fori_loop(0, n_wide, lambda j, s_: step([(1 + 2 * WIDE * j, 2 * WIDE)], None, s_), st)
                rest = 1 + 2 * WIDE * n_wide
                st = lax.cond((n_whole & 2) != 0, lambda s_: step([(rest, 4)], None, s_), lambda s_: s_, st)
                st = lax.cond((n_whole & 1) != 0, lambda s_: step([(rest + 2 * (n_whole & 2), 2)], None, s_),
                              lambda s_: s_, st)
            for pp in range(FOX_PAIRS):
                lo, hi = st[6 * pp:6 * pp + 3], st[6 * pp + 3:6 * pp + 6]
                o_t = jnp.concatenate([lo[2] * (1.0 / lo[1]), hi[2] * (1.0 / hi[1])], axis=0)
                of_ref[pl.ds(qoff, qlen), pp * BLK:(pp + 1) * BLK] = o_t.T.astype(BF16)
            lse = [st[3 * hh] + jnp.log(st[3 * hh + 1]) * LOG2E for hh in range(len(heads))]
            for a in range(nq):
                upd = jnp.zeros((8, BLK), F32)
                for hh, (_, _, h) in enumerate(heads):
                    upd = upd + jnp.where(sub8 == h, lse[hh][:, a * BLK:(a + 1) * BLK], 0.0)
                lse_ref[qblk + a] = lse_ref[qblk + a] + upd

        attend(0, 1, 0)

        def q_loop(u, carry):
            attend(1 + 2 * u, 2, u)
            return carry

        lax.fori_loop(0, nu, q_loop, 0)

        if ng:
            @pl.when(p == steps - 1)
            def _():
                local, sends, recvs = _allgather_copies(g_in, g_out, *rest[2 * ng + 4:])
                for cp in recvs:
                    cp.wait_recv()
                for cp in sends:
                    cp.wait_send()
                for cp in local:
                    cp.wait()

    width = 384 * FOX_PAIRS
    return pl.pallas_call(
        body, name="f_fox", grid=(steps,),
        in_specs=[pl.BlockSpec((L, width), lambda p: (0, RET_W // width + p)), _full((L, BLK)), _full((nblk, 8, BLK))]
        + [_ANY] * ng,
        out_specs=[pl.BlockSpec((L, FOX_PAIRS * BLK), lambda p: (0, p)), _full((nblk, 8, BLK))] + [_ANY] * ng,
        out_shape=[jax.ShapeDtypeStruct((L, 512), BF16), jax.ShapeDtypeStruct((nblk, 8, BLK), F32)]
        + [jax.ShapeDtypeStruct((N_CHIPS,) + a.shape, a.dtype) for a in gather],
        scratch_shapes=[pltpu.VMEM((FOX_PAIRS, nblk, BLK, BLK), BF16), pltpu.VMEM((2 * FOX_PAIRS, nblk, BLK, BLK), F32)]
        + _allgather_semaphores(ng),
        compiler_params=_params(("arbitrary",)),
    )(proj, c, ctb, *gather)


def _outproj_up(mix_r, o_f, h0, w_out, ffn_g, w_up, conv_w, conv_b):
    L = h0.shape[0]
    tm = _row_tile(L)
    shard = w_up.shape[2]
    assert 2 * shard == D_FF
    cw = [conv_w[j:j + 1] for j in range(3)]
    resident = lambda shape: pl.BlockSpec(shape, lambda i: (0,) * len(shape), pipeline_mode=pl.Buffered(1))

    def body(mr_ref, of_ref, h0_ref, wo_ref, g_ref, wu_ref, cw0, cw1, cw2, cb_ref,
             h1_ref, n2_ref, up_ref, act_ref, acc_ref, halo):
        i = pl.program_id(0)

        @pl.when(i == 0)
        def _():
            halo[...] = jnp.zeros_like(halo)

        h1 = h0_ref[...] + _dot(mr_ref[...], wo_ref[0:512, :]) + _dot(of_ref[...], wo_ref[512:1024, :])
        h1_ref[...] = h1
        r = lax.rsqrt(jnp.mean(h1 * h1, axis=-1, keepdims=True) + EPS)
        n2 = (h1 * r * g_ref[...]).astype(BF16)
        n2_ref[...] = n2
        live = i * tm + _iota((tm, 1), 0) >= N_PAD
        for half in range(2):
            cols = slice(half * shard, (half + 1) * shard)
            a_b = _dot(n2, wu_ref[half]).astype(BF16)
            b_b = _dot(n2, wu_ref[2 + half]).astype(BF16)
            up_ref[:, cols] = a_b
            up_ref[:, D_FF + half * shard:D_FF + (half + 1) * shard] = b_b
            a = jnp.where(live, a_b.astype(F32), 0.0)
            _, _, acc = _conv_taps(a, halo[:, cols], [cw0[:, cols], cw1[:, cols], cw2[:, cols]], cb_ref[:, cols])
            act_ref[:, cols] = (acc * jax.nn.sigmoid(acc) * b_b.astype(F32)).astype(BF16)
            acc_ref[:, cols] = acc.astype(BF16)
            halo[:, cols] = a[tm - 8:tm, :]

    rows = lambda w: pl.BlockSpec((tm, w), lambda i: (i, 0))
    return pl.pallas_call(
        body, name="f_outproj_up", grid=(L // tm,),
        in_specs=[rows(512), rows(512), rows(D_MODEL), resident((D_MODEL, D_MODEL)), _full((1, D_MODEL)),
                  resident((N_CHIPS, D_MODEL, shard)), _full((1, D_FF)), _full((1, D_FF)), _full((1, D_FF)),
                  _full((1, D_FF))],
        out_specs=[rows(D_MODEL), rows(D_MODEL), rows(2 * D_FF), rows(D_FF), rows(D_FF)],
        out_shape=[jax.ShapeDtypeStruct((L, D_MODEL), F32), jax.ShapeDtypeStruct((L, D_MODEL), BF16),
                   jax.ShapeDtypeStruct((L, 2 * D_FF), BF16), jax.ShapeDtypeStruct((L, D_FF), BF16),
                   jax.ShapeDtypeStruct((L, D_FF), BF16)],
        scratch_shapes=[pltpu.VMEM((8, D_FF), F32)],
        compiler_params=_params(("arbitrary",)),
    )(mix_r, o_f, h0, w_out, ffn_g, w_up, cw[0], cw[1], cw[2], conv_b)


def _conv_taps(a, halo, cw, cb):
    sub = _iota((a.shape[0], 1), 0)
    a1 = jnp.where(sub == 0, _pick_row(halo, 7), pltpu.roll(a, 1, 0))
    a2 = jnp.where(sub == 0, _pick_row(halo, 6), jnp.where(sub == 1, _pick_row(halo, 7), pltpu.roll(a, 2, 0)))
    acc = cb + a2 * cw[0]
    acc = acc + a1 * cw[1]
    acc = acc + a * cw[2]
    return a1, a2, acc


def _ffn_down_loss(g_act, w_down, h1, final_g, target):
    L = h1.shape[0]
    tm = _row_tile(L)
    nb = tm // BLK

    def body(g_ref, wd_ref, h1_ref, gf_ref, *refs):
        t_refs, (dh_ref, dhb_ref, dgf_ref, loss_ref) = refs[:nb], refs[nb:]
        i = pl.program_id(0)

        @pl.when(i == 0)
        def _():
            dgf_ref[...] = jnp.zeros_like(dgf_ref)
            loss_ref[...] = jnp.zeros_like(loss_ref)

        h2 = h1_ref[...] + _dot(g_ref[...], wd_ref[...])
        r = lax.rsqrt(jnp.mean(h2 * h2, axis=-1, keepdims=True) + EPS)
        yn = h2 * r
        gf = gf_ref[...]
        live = i * tm + _iota((tm, 1), 0) >= PREFIX
        target = jnp.concatenate([t[...] for t in t_refs], axis=0)
        err = jnp.where(live, yn * gf - target, 0.0)
        loss_ref[...] = loss_ref[...] + 0.5 * jnp.sum(jnp.mean(err * err, axis=-1, keepdims=True))
        dy = err * (1.0 / D_MODEL)
        dgf_ref[...] = dgf_ref[...] + jnp.sum(dy * yn, axis=0, keepdims=True)
        dyn = dy * gf
        dh = r * (dyn - yn * jnp.mean(dyn * yn, axis=-1, keepdims=True))
        dh_ref[...] = dh
        dhb_ref[...] = dh.astype(BF16)

    rows = lambda w: pl.BlockSpec((tm, w), lambda i: (i, 0))
    return pl.pallas_call(
        body, name="f_ffn_down_loss", grid=(L // tm,),
        in_specs=[rows(D_FF), _full((D_FF, D_MODEL)), rows(D_MODEL), _full((1, D_MODEL))] + _shifted_blocks(tm),
        out_specs=[rows(D_MODEL), rows(D_MODEL), _full((1, D_MODEL)), _full((1, BLK))],
        out_shape=[jax.ShapeDtypeStruct((L, D_MODEL), F32), jax.ShapeDtypeStruct((L, D_MODEL), BF16),
                   jax.ShapeDtypeStruct((1, D_MODEL), F32), jax.ShapeDtypeStruct((1, BLK), F32)],
        compiler_params=_params(("arbitrary",)),
    )(g_act, w_down, h1, final_g, *([target] * nb))


def _ffn_bwd_gate(dh2b, w_down, acc_saved, up):
    L = dh2b.shape[0]
    tm = _row_tile(L)

    def body(dh_ref, wd_ref, acc_ref, b_ref, dacc_ref, db_ref):
        acc = acc_ref[...].astype(F32)
        dg = _dot_nt(dh_ref[...], wd_ref[...])
        sg = jax.nn.sigmoid(acc)
        silu = acc * sg
        db_ref[...] = (dg * silu).astype(BF16)
        dacc_ref[...] = (dg * b_ref[...].astype(F32) * (sg + silu * (1.0 - sg))).astype(BF16)

    rows = lambda w, c=0: pl.BlockSpec((tm, w), lambda i: (i, c))
    return pl.pallas_call(
        body, name="b_ffn_gate", grid=(L // tm,),
        in_specs=[rows(D_MODEL), _full((D_FF, D_MODEL)), rows(D_FF), rows(D_FF, 1)],
        out_specs=[rows(D_FF), rows(D_FF)],
        out_shape=[jax.ShapeDtypeStruct((L, D_FF), BF16), jax.ShapeDtypeStruct((L, D_FF), BF16)],
        compiler_params=_params(("parallel",)),
    )(dh2b, w_down, acc_saved, up)


def _ffn_bwd_up(dacc, db, up, conv_w, w_up, h1, ffn_g, dh2, w_out):
    L = h1.shape[0]
    tm = _row_tile(L)
    nt = L // tm
    shard = w_up.shape[2]
    cw = [conv_w[j:j + 1] for j in range(3)]

    def body(da_ref, halo_ref, db_ref, a_ref, cw0, cw1, cw2, wu_ref, h1_ref, g_ref, dh2_ref, wo_ref,
             dup_ref, dh1_ref, dh1b_ref, dmix_ref, dg_ref, dcw_ref):
        i = pl.program_id(0)

        @pl.when(i == 0)
        def _():
            dg_ref[...] = jnp.zeros_like(dg_ref)
            dcw_ref[...] = jnp.zeros_like(dcw_ref)

        sub = _iota((tm, 1), 0)
        sub8 = _iota((8, 1), 0)
        last_tile = i == nt - 1
        dbv = db_ref[...]
        dup_ref[:, D_FF:2 * D_FF] = dbv
        dn = _dot_nt(dbv[:, 0:shard], wu_ref[2]) + _dot_nt(dbv[:, shard:2 * shard], wu_ref[3])
        for half in range(2):
            cols = slice(half * shard, (half + 1) * shard)
            d0 = da_ref[:, cols].astype(F32)
            halo = jnp.where(last_tile, 0.0, halo_ref[:, cols].astype(F32))
            d1 = jnp.where(sub == tm - 1, _pick_row(halo, 0), pltpu.roll(d0, tm - 1, 0))
            d2 = jnp.where(sub == tm - 2, _pick_row(halo, 0),
                           jnp.where(sub == tm - 1, _pick_row(halo, 1), pltpu.roll(d0, tm - 2, 0)))
            a = a_ref[:, cols].astype(F32)
            upd = jnp.zeros((8, shard), F32)
            for j, t in enumerate((d2 * a, d1 * a, d0 * a, d0)):
                upd = upd + jnp.where(sub8 == j, jnp.sum(t, axis=0, keepdims=True), 0.0)
            dcw_ref[:, cols] = dcw_ref[:, cols] + upd
            da = (d0 * cw2[:, cols] + d1 * cw1[:, cols] + d2 * cw0[:, cols]).astype(BF16)
            dup_ref[:, cols] = da
            dn = dn + _dot_nt(da, wu_ref[half])
        h1 = h1_ref[...]
        r = lax.rsqrt(jnp.mean(h1 * h1, axis=-1, keepdims=True) + EPS)
        yn = h1 * r
        dg_ref[...] = dg_ref[...] + jnp.sum(dn * yn, axis=0, keepdims=True)
        dyn = dn * g_ref[...]
        dh1 = dh2_ref[...] + r * (dyn - yn * jnp.mean(dyn * yn, axis=-1, keepdims=True))
        dh1_ref[...] = dh1
        dh1b = dh1.astype(BF16)
        dh1b_ref[...] = dh1b
        dmix_ref[...] = _dot_nt(dh1b, wo_ref[...]).astype(BF16)

    rows = lambda w: pl.BlockSpec((tm, w), lambda i: (i, 0))
    halo = pl.BlockSpec((8, D_FF), lambda i: (jnp.minimum((i + 1) * (tm // 8), L // 8 - 1), 0))
    return pl.pallas_call(
        body, name="b_ffn_up", grid=(nt,),
        in_specs=[rows(D_FF), halo, rows(D_FF), rows(D_FF), _full((1, D_FF)), _full((1, D_FF)), _full((1, D_FF)),
                  _full((N_CHIPS, D_MODEL, shard)), rows(D_MODEL), _full((1, D_MODEL)), rows(D_MODEL),
                  _full((D_MODEL, D_MODEL))],
        out_specs=[rows(2 * D_FF), rows(D_MODEL), rows(D_MODEL), rows(D_MODEL), _full((1, D_MODEL)),
                   _full((8, D_FF))],
        out_shape=[jax.ShapeDtypeStruct((L, 2 * D_FF), BF16), jax.ShapeDtypeStruct((L, D_MODEL), F32),
                   jax.ShapeDtypeStruct((L, D_MODEL), BF16), jax.ShapeDtypeStruct((L, D_MODEL), BF16),
                   jax.ShapeDtypeStruct((1, D_MODEL), F32), jax.ShapeDtypeStruct((8, D_FF), F32)],
        compiler_params=_params(("arbitrary",)),
    )(dacc, dacc, db, up, cw[0], cw[1], cw[2], w_up, h1, ffn_g, dh2, w_out)


def _wgrad(a, b, name, tn=None, tk=None):
    L, K = a.shape
    N = b.shape[1]
    tn = N if tn is None else tn
    tk = K if tk is None else tk
    tl = _row_tile(L, (1408, 768, 512, 256, 128))

    def body(a_ref, b_ref, o_ref):
        @pl.when(pl.program_id(2) == 0)
        def _():
            o_ref[...] = jnp.zeros_like(o_ref)

        o_ref[0] = o_ref[0] + _dot_tn(a_ref[...], b_ref[...])

    return pl.pallas_call(
        body, name=name, grid=(N // tn, K // tk, L // tl),
        in_specs=[pl.BlockSpec((tl, tk), lambda n, k, l: (l, k)), pl.BlockSpec((tl, tn), lambda n, k, l: (l, n))],
        out_specs=pl.BlockSpec((1, tk, tn), lambda n, k, l: (n, k, 0)),
        out_shape=jax.ShapeDtypeStruct((N // tn, K, tn), F32),
        compiler_params=_params(("parallel", "parallel", "arbitrary")),
    )(a, b)


def _retention_bwd(dmix, o, proj, cos_t, sin_t, ret_g, states, exchange=()):
    L = proj.shape[0]
    nblk = L // BLK
    G = _block_group(nblk)
    steps = nblk // G
    nx = len(exchange)
    dmat, wq_t, wk_t, g_blk = _decay_tables()

    def body(dm_ref, o_ref, q_ref, k_ref, v_ref, gate_ref, cos_ref, sin_ref, d_ref, wq_ref, wk_ref, rg_ref, rs_ref,
             *rest):
        x_in, (dp_ref, drg_ref), x_out, gstate = rest[:nx], rest[nx:nx + 2], rest[nx + 2:2 * nx + 2], rest[2 * nx + 2]

        @pl.when(pl.program_id(0) == 0)
        def _():
            if nx:
                for cp in _sibling_half_copies(x_in, x_out, *rest[2 * nx + 3:])[0]:
                    cp.start()
            gstate[...] = jnp.zeros_like(gstate)
            drg_ref[...] = jnp.zeros_like(drg_ref)

        lane = _iota((BLK, BLK), 1)
        sub = _iota((BLK, BLK), 0)
        scale = HEAD_LANES ** -0.5
        for b in reversed(range(G)):
            rows = slice(b * BLK, (b + 1) * BLK)
            rot, rot_t = _rot_fns(cos_ref[rows, :], sin_ref[rows, :])
            for p in range(2):
                qr = rot(q_ref[rows, p * BLK:(p + 1) * BLK].astype(F32))
                kr = rot(k_ref[rows, p * BLK:(p + 1) * BLK].astype(F32)) * scale
                kr_b = kr.astype(BF16)
                qw = (qr * wq_ref[p]).astype(BF16)
                kw = (kr * wk_ref[p]).astype(BF16)
                dqr = jnp.zeros((BLK, BLK), F32)
                dkr = jnp.zeros((BLK, BLK), F32)
                for e in range(2):
                    h = 2 * p + e
                    cols = slice(h * BLK, (h + 1) * BLK)
                    head_lanes = (lane >> 6) == e
                    o = o_ref[rows, cols]
                    rn = lax.rsqrt(jnp.mean(o * o, axis=-1, keepdims=True) + EPS)
                    y = o * rn
                    gate = gate_ref[rows, cols].astype(F32)
                    sg = jax.nn.sigmoid(gate)
                    dm = dm_ref[rows, cols].astype(F32)
                    rgain = rg_ref[:, cols]
                    drg_ref[:, cols] = drg_ref[:, cols] + jnp.sum(dm * y * (gate * sg), axis=0, keepdims=True)
                    dp_ref[rows, 1024 + h * BLK:1024 + (h + 1) * BLK] = (
                        dm * y * rgain * (sg * (1.0 + gate * (1.0 - sg)))).astype(BF16)
                    dy = dm * rgain * (gate * sg)
                    do = (rn * (dy - y * jnp.mean(dy * y, axis=-1, keepdims=True))).astype(BF16)
                    vh = v_ref[rows, cols]
                    qm = jnp.where(head_lanes, qr, 0.0).astype(BF16)
                    dmh = d_ref[h]
                    s = (_dot_nt(qm, kr_b) * dmh).astype(BF16)
                    ds = (_dot_nt(do, vh) * dmh).astype(BF16)
                    st = rs_ref[b, h].astype(BF16)
                    gs = gstate[h]
                    gs_b = gs.astype(BF16)
                    dqr = dqr + jnp.where(head_lanes, _dot(ds, kr_b), 0.0) + _dot_nt(do, st) * wq_ref[p]
                    dkr = dkr + _dot_tn(ds, qm) + _dot_nt(vh, gs_b) * wk_ref[p]
                    dp_ref[rows, 512 + h * BLK:512 + (h + 1) * BLK] = (_dot_tn(s, do) + _dot(kw, gs_b)).astype(BF16)
                    dr = jnp.where((sub >> 6) == e, _dot_tn(qw, do), 0.0)
                    gstate[h] = dr + g_blk[h] * gs
                dp_ref[rows, p * BLK:(p + 1) * BLK] = rot_t(dqr).astype(BF16)
                dp_ref[rows, 256 + p * BLK:256 + (p + 1) * BLK] = (rot_t(dkr) * scale).astype(BF16)

        if nx:
            @pl.when(pl.program_id(0) == steps - 1)
            def _():
                sends, recvs = _sibling_half_copies(x_in, x_out, *rest[2 * nx + 3:])
                for cp in recvs:
                    cp.wait_recv()
                for cp in sends:
                    cp.wait_send()

    row = lambda c: (lambda i: (steps - 1 - i, c))
    return pl.pallas_call(
        body, name="b_retention", grid=(steps,),
        in_specs=[pl.BlockSpec((G * BLK, 512), row(0)), pl.BlockSpec((G * BLK, 512), row(0)),
                  pl.BlockSpec((G * BLK, 256), row(0)), pl.BlockSpec((G * BLK, 256), row(1)),
                  pl.BlockSpec((G * BLK, 512), row(1)), pl.BlockSpec((G * BLK, 512), row(2)),
                  pl.BlockSpec((G * BLK, BLK), row(0)), pl.BlockSpec((G * BLK, BLK), row(0)),
                  _full((RET_HEADS, BLK, BLK)), _full((2, BLK, BLK)), _full((2, BLK, BLK)), _full((1, 512)),
                  pl.BlockSpec((G, RET_HEADS, BLK, BLK), lambda i: (steps - 1 - i, 0, 0, 0))] + [_ANY] * nx,
        out_specs=[pl.BlockSpec((G * BLK, RET_W), row(0)), _full((1, 512))] + [_ANY] * nx,
        out_shape=[jax.ShapeDtypeStruct((L, RET_W), BF16), jax.ShapeDtypeStruct((1, 512), F32)]
        + _sibling_half_shapes(exchange),
        scratch_shapes=[pltpu.VMEM((RET_HEADS, BLK, BLK), F32)] + _sibling_half_semaphores(nx),
        compiler_params=_params(("arbitrary",)),
    )(dmix, o, proj, proj, proj, proj, cos_t, sin_t, dmat, wq_t, wk_t, ret_g, states, *exchange)


def _fox_delta(dmix, o_f):
    L = o_f.shape[0]
    nblk = L // BLK
    G = _block_group(nblk)

    def body(do_ref, o_ref, d_ref):
        sel = ((_iota((8, 512), 1) >> 6) == _iota((8, 512), 0)).astype(BF16)
        for b in range(G):
            rows = slice(b * BLK, (b + 1) * BLK)
            prod = do_ref[rows, :].astype(F32) * o_ref[rows, :].astype(F32)
            hi = prod.astype(BF16)
            lo = (prod - hi.astype(F32)).astype(BF16)
            d_ref[b] = _dot_nt(sel, hi) + _dot_nt(sel, lo)

    return pl.pallas_call(
        body, name="b_foxdelta", grid=(nblk // G,),
        in_specs=[pl.BlockSpec((G * BLK, 512), lambda i: (i, 1)), pl.BlockSpec((G * BLK, 512), lambda i: (i, 0))],
        out_specs=pl.BlockSpec((G, 8, BLK), lambda i: (i, 0, 0)),
        out_shape=jax.ShapeDtypeStruct((nblk, 8, BLK), F32),
        compiler_params=_params(("parallel",)),
    )(dmix, o_f)


def _fox_bwd(proj, dmix, c, ctb, lse, delta, scatter=()):
    L = proj.shape[0]
    nblk, nu = _fox_units(L)
    scale = HEAD_LANES ** -0.5
    ns = len(scatter)

    steps = FOX_HEADS // (2 * FOX_PAIRS)

    def body(qkv_ref, do_ref, c_ref, ct_ref, lse_ref, dl_ref, *rest):
        s_in, (dp_ref, dc_ref, dcq_ref), s_out = rest[:ns], rest[ns:ns + 3], rest[ns + 3:2 * ns + 3]
        ktt, dqt, dk_acc, dv_acc, dcs_acc = rest[2 * ns + 3:2 * ns + 8]
        p = pl.program_id(0)
        heads = [(pp, e, 2 * FOX_PAIRS * p + 2 * pp + e) for pp in range(FOX_PAIRS) for e in range(2)]

        @pl.when(p == 0)
        def _():
            dc_ref[...] = jnp.zeros_like(dc_ref)
            dcq_ref[...] = jnp.zeros_like(dcq_ref)
            if ns:
                for cp in _scatter_copies(s_in, s_out, *rest[2 * ns + 8:]):
                    cp.start()

        sub8 = _iota((8, BLK), 0)
        masks = _fox_tile_masks()

        def pre(j, carry):
            off = pl.multiple_of(j * BLK, BLK)
            for pp in range(FOX_PAIRS):
                ktt[pp, j] = qkv_ref[pl.ds(off, BLK), pp * 384 + BLK:pp * 384 + 2 * BLK].astype(F32).T.astype(BF16)
                dqt[pp, j] = jnp.zeros((BLK, BLK), F32)
            return carry

        lax.fori_loop(0, nblk, pre, 0)

        def kv_pass(kblk, nk, n_later):
            klen = nk * BLK
            koff = pl.multiple_of(kblk * BLK, BLK)
            kt = [qkv_ref[pl.ds(koff, klen), pp * 384 + BLK:pp * 384 + 2 * BLK] for pp in range(FOX_PAIRS)]
            vtile = [qkv_ref[pl.ds(koff, klen), pp * 384 + 2 * BLK:pp * 384 + 3 * BLK] for pp in range(FOX_PAIRS)]
            ct = c_ref[pl.ds(koff, klen), :]
            klane = _iota((klen, BLK), 1)
            cs = [jnp.broadcast_to(jnp.sum(jnp.where(klane == h, ct, 0.0), axis=1, keepdims=True), (klen, WIDE * UNIT))
                  for _, _, h in heads]
            k_t = [jnp.concatenate([ktt[pp, kblk + b, e * HEAD_LANES:(e + 1) * HEAD_LANES, :] for b in range(nk)], axis=1)
                   for pp, e, _ in heads]
            for pp in range(FOX_PAIRS):
                dk_acc[pp, 0:klen] = jnp.zeros((klen, BLK), F32)
                dv_acc[pp, 0:klen] = jnp.zeros((klen, BLK), F32)
            for hh in range(len(heads)):
                dcs_acc[hh, 0:klen] = jnp.zeros((klen, BLK), F32)

            def tile(qblk, nq, mask):
                qlen = nq * BLK
                if mask == "valid":
                    mask = _iota((klen, qlen), 0) >= N_PAD
                qoff = pl.multiple_of(qblk * BLK, BLK)
                qlane = _iota((qlen, BLK), 1)
                qs = [qkv_ref[pl.ds(qoff, qlen), pp * 384:pp * 384 + BLK].astype(F32) * (scale * LOG2E)
                      for pp in range(FOX_PAIRS)]
                dot_ = [do_ref[pl.ds(qoff, qlen), pp * BLK:(pp + 1) * BLK] for pp in range(FOX_PAIRS)]
                stats = [[ref[qblk + a] for a in range(nq)] for ref in (ct_ref, lse_ref, dl_ref)]
                dcq = [jnp.zeros((8, BLK), F32) for _ in range(nq)]
                for hh, (pp, e, h) in enumerate(heads):
                    head = (qlane >> 6) == e
                    ct_row, lse_row, dl_row = [jnp.concatenate([_pick_row(t, h) for t in ts], axis=1) for ts in stats]
                    qm = jnp.where(head, qs[pp], 0.0).astype(BF16)
                    dom = jnp.where(head, dot_[pp], jnp.zeros_like(dot_[pp]))
                    t = _dot_nt(kt[pp], qm) - cs[hh][:, 0:qlen]
                    if mask is not None:
                        t = jnp.where(mask, t, NEG)
                    pr = jnp.exp2(t + (ct_row - lse_row))
                    dv_acc[pp, 0:klen] = dv_acc[pp, 0:klen] + _dot(pr.astype(BF16), dom)
                    dsv = pr * (_dot_nt(vtile[pp], dom) - dl_row)
                    ds_b = dsv.astype(BF16)
                    dk_acc[pp, 0:klen] = dk_acc[pp, 0:klen] + _dot(ds_b, qm)
                    rows = slice(e * HEAD_LANES, (e + 1) * HEAD_LANES)
                    dq_t = _dot(k_t[hh], ds_b)
                    key_side = dsv[:, 0:BLK]
                    for a in range(1, nq):
                        key_side = key_side + dsv[:, a * BLK:(a + 1) * BLK]
                    dcs_acc[hh, 0:klen] = dcs_acc[hh, 0:klen] + key_side
                    query_side = jnp.sum(dsv, axis=0, keepdims=True)
                    for a in range(nq):
                        cols = slice(a * BLK, (a + 1) * BLK)
                        dqt[pp, qblk + a, rows, :] = dqt[pp, qblk + a, rows, :] + dq_t[:, cols]
                        dcq[a] = dcq[a] + jnp.where(sub8 == h, query_side[:, cols], 0.0)
                for a in range(nq):
                    dcq_ref[qblk + a] = dcq_ref[qblk + a] + dcq[a]

            later_mask = "valid" if nk == 1 else None
            n_later = jnp.asarray(n_later, jnp.int32)
            n_wide = n_later // WIDE

            def later_wide(i, carry):
                tile(kblk + nk + 2 * WIDE * i, 2 * WIDE, later_mask)
                return carry

            tile(kblk, nk, masks["first"] if nk == 1 else masks["diag"])
            lax.fori_loop(0, n_wide, later_wide, 0)
            rest_blk = kblk + nk + 2 * WIDE * n_wide

            @pl.when((n_later & 2) != 0)
            def _():
                tile(rest_blk, 4, later_mask)

            @pl.when((n_later & 1) != 0)
            def _():
                tile(rest_blk + 2 * (n_later & 2), 2, later_mask)

            upd = jnp.zeros((klen, BLK), F32)
            for hh, (_, _, h) in enumerate(heads):
                upd = upd + jnp.where(klane == h, -jnp.sum(dcs_acc[hh, 0:klen], axis=1, keepdims=True), 0.0)
            dc_ref[pl.ds(koff, klen), :] = dc_ref[pl.ds(koff, klen), :] + upd
            for pp in range(FOX_PAIRS):
                dp_ref[pl.ds(koff, klen), pp * 384 + BLK:pp * 384 + 2 * BLK] = (
                    dk_acc[pp, 0:klen] * (1.0 / LOG2E)).astype(BF16)
                dp_ref[pl.ds(koff, klen), pp * 384 + 2 * BLK:pp * 384 + 3 * BLK] = dv_acc[pp, 0:klen].astype(BF16)

        kv_pass(0, 1, nu)

        def k_loop(u, carry):
            kv_pass(1 + 2 * u, 2, nu - 1 - u)
            return carry

        lax.fori_loop(0, nu, k_loop, 0)

        def flush(j, carry):
            off = pl.multiple_of(j * BLK, BLK)
            for pp in range(FOX_PAIRS):
                dp_ref[pl.ds(off, BLK), pp * 384:pp * 384 + BLK] = (dqt[pp, j].T * scale).astype(BF16)
            return carry

        lax.fori_loop(0, nblk, flush, 0)

        if ns:
            @pl.when(p == steps - 1)
            def _():
                copies = _scatter_copies(s_in, s_out, *rest[2 * ns + 8:])
                for cp in copies:
                    cp.wait_recv()
                for cp in copies:
                    cp.wait_send()

    width = 384 * FOX_PAIRS
    once = lambda shape, index: pl.BlockSpec(shape, index, pipeline_mode=pl.Buffered(1))
    stat = once((nblk, 8, BLK), lambda p: (0, 0, 0))
    return pl.pallas_call(
        body, name="b_fox", grid=(steps,),
        in_specs=[once((L, width), lambda p: (0, RET_W // width + p)),
                  once((L, FOX_PAIRS * BLK), lambda p: (0, 4 // FOX_PAIRS + p)),
                  once((L, BLK), lambda p: (0, 0)), stat, stat, stat] + [_ANY] * ns,
        out_specs=[pl.BlockSpec((L, width), lambda p: (0, p)), _full((L, BLK)), _full((nblk, 8, BLK))] + [_ANY] * ns,
        out_shape=[jax.ShapeDtypeStruct((L, FOX_W), BF16), jax.ShapeDtypeStruct((L, BLK), F32),
                   jax.ShapeDtypeStruct((nblk, 8, BLK), F32)] + _scatter_shapes(scatter),
        scratch_shapes=[pltpu.VMEM((FOX_PAIRS, nblk, BLK, BLK), BF16), pltpu.VMEM((FOX_PAIRS, nblk, BLK, BLK), F32),
                        pltpu.VMEM((FOX_PAIRS, UNIT, BLK), F32), pltpu.VMEM((FOX_PAIRS, UNIT, BLK), F32),
                        pltpu.VMEM((2 * FOX_PAIRS, UNIT, BLK), F32)]
        + _scatter_semaphores(ns),
        compiler_params=_params(("arbitrary",)),
    )(proj, dmix, c, ctb, lse, delta, *scatter)


def _fox_post(dc, dcq, ff, fb):
    L = dc.shape[0]
    nblk = L // BLK
    G = _block_group(nblk)
    steps = nblk // G

    def body(dc_ref, dcq_ref, ff_ref, b_ref, dff_ref, dffb_ref, dfb_ref, carry):
        @pl.when(pl.program_id(0) == 0)
        def _():
            carry[...] = jnp.zeros_like(carry)
            dfb_ref[...] = jnp.zeros_like(dfb_ref)

        tri = (_iota((BLK, BLK), 0) <= _iota((BLK, BLK), 1)).astype(BF16)
        live = _iota((BLK, BLK), 1) < FOX_HEADS
        run, dfb = carry[...], dfb_ref[...]
        for b in reversed(range(G)):
            rows = slice(b * BLK, (b + 1) * BLK)
            d = dc_ref[rows, :] + jnp.concatenate([dcq_ref[b], jnp.zeros((BLK - 8, BLK), F32)], axis=0).T
            hi, mid, lo = _split3(d)
            dlf = _dot(tri, hi) + _dot(tri, mid) + _dot(tri, lo) + run
            run = run + jnp.sum(d, axis=0, keepdims=True)
            z = ff_ref[rows, :] + b_ref[...]
            dff = jnp.where(live, dlf * jax.nn.sigmoid(-z), 0.0)
            dff_ref[rows, :] = dff
            dffb_ref[rows, :] = dff.astype(BF16)
            dfb = dfb + jnp.sum(dff, axis=0, keepdims=True)
        carry[...] = run
        dfb_ref[...] = dfb

    rev = lambda i: (steps - 1 - i, 0)
    return pl.pallas_call(
        body, name="b_foxpost", grid=(steps,),
        in_specs=[pl.BlockSpec((G * BLK, BLK), rev), pl.BlockSpec((G, 8, BLK), lambda i: (steps - 1 - i, 0, 0)),
                  pl.BlockSpec((G * BLK, BLK), rev), _full((1, BLK))],
        out_specs=[pl.BlockSpec((G * BLK, BLK), rev), pl.BlockSpec((G * BLK, BLK), rev), _full((1, BLK))],
        out_shape=[jax.ShapeDtypeStruct((L, BLK), F32), jax.ShapeDtypeStruct((L, BLK), BF16),
                   jax.ShapeDtypeStruct((1, BLK), F32)],
        scratch_shapes=[pltpu.VMEM((1, BLK), F32)],
        compiler_params=_params(("arbitrary",)),
    )(dc, dcq, ff, fb)


def _inproj_bwd(dpr, dpf, dffb, w_main, w_ff, h0, g, dh1, scatter=()):
    L = h0.shape[0]
    S = L - BLK
    tm = _row_tile(S, (512, 256, 128))
    nt = S // tm
    ns = len(scatter)
    operands = (dpr, dpf, dffb, h0, dh1)

    def body(*refs):
        lead, tile = refs[0:5], refs[5:10]
        wm_ref, wf_ref, g_ref = refs[10:13]
        rest = refs[13:]
        s_in, (dlead_ref, dx_ref, dg_ref), s_out = rest[:ns], rest[ns:ns + 3], rest[ns + 3:2 * ns + 3]
        i = pl.program_id(0)

        def rows_bwd(dpr_ref, dpf_ref, dff_ref, h_ref, dh1_ref):
            dn = (_dot_nt(dpr_ref[...], wm_ref[:, 0:RET_W]) + _dot_nt(dpf_ref[...], wm_ref[:, RET_W:MAIN_W])
                  + _dot_nt(dff_ref[...], wf_ref[...]))
            h = h_ref[...]
            r = lax.rsqrt(jnp.mean(h * h, axis=-1, keepdims=True) + EPS)
            yn = h * r
            dyn = dn * g_ref[...]
            dh0 = dh1_ref[...] + r * (dyn - yn * jnp.mean(dyn * yn, axis=-1, keepdims=True))
            return dh0, jnp.sum(dn * yn, axis=0, keepdims=True)

        @pl.when(i == 0)
        def _():
            if ns:
                for cp in _scatter_copies(s_in, s_out, *rest[2 * ns + 3:]):
                    cp.start()
            dlead_ref[...], dg_ref[...] = rows_bwd(*lead)

        dx_ref[...], dg_tile = rows_bwd(*tile)
        dg_ref[...] = dg_ref[...] + dg_tile

        if ns:
            @pl.when(i == nt - 1)
            def _():
                copies = _scatter_copies(s_in, s_out, *rest[2 * ns + 3:])
                for cp in copies:
                    cp.wait_recv()
                for cp in copies:
                    cp.wait_send()

    lead_spec = lambda a: pl.BlockSpec((BLK, a.shape[1]), lambda i: (0, 0))
    tile_spec = lambda a: pl.BlockSpec((pl.Element(tm), pl.Element(a.shape[1])),
                                       lambda i: (pl.multiple_of(BLK + i * tm, BLK), 0))
    return pl.pallas_call(
        body, name="b_inproj", grid=(nt,),
        in_specs=[lead_spec(a) for a in operands] + [tile_spec(a) for a in operands]
        + [_full((D_MODEL, MAIN_W)), _full((D_MODEL, BLK)), _full((1, D_MODEL))] + [_ANY] * ns,
        out_specs=[_full((BLK, D_MODEL)), pl.BlockSpec((tm, D_MODEL), lambda i: (i, 0)), _full((1, D_MODEL))]
        + [_ANY] * ns,
        out_shape=[jax.ShapeDtypeStruct((BLK, D_MODEL), F32), jax.ShapeDtypeStruct((S, D_MODEL), F32),
                   jax.ShapeDtypeStruct((1, D_MODEL), F32)] + _scatter_shapes(scatter),
        scratch_shapes=_scatter_semaphores(ns),
        compiler_params=_params(("arbitrary",)),
    )(*operands, *operands, w_main, w_ff, g, *scatter)


def _local_step(x, target, meta, attn_g, w_main, w_ff, fox_b, ret_g, w_out, ffn_g, w_up, conv_w, conv_b, w_down, final_g,
                late=None, mid=None, last=None):
    S = x.shape[0]
    L = S + PREFIX
    head = jnp.concatenate([jnp.zeros((N_PAD, D_MODEL), F32), meta], axis=0)
    fb = jnp.pad(fox_b, ((0, 0), (0, BLK - FOX_HEADS)))
    cos_t, sin_t = _rotary_tables(L)

    h0, n1, proj, ff = _rms_inproj(head, x, attn_g, w_main, w_ff)
    c, ctb = _fox_prep(ff, fb)
    mix_r, o_ret, states = _retention_fwd(proj, cos_t, sin_t, ret_g)
    if late is None:
        o_f, lse = _fox_fwd(proj, c, ctb)
    else:
        o_f, lse, *gathered = _fox_fwd(proj, c, ctb, gather=late[0])
        w_out, w_up, w_down = late[1](gathered)
    h1, n2, up, g_act, acc_saved = _outproj_up(mix_r, o_f, h0, w_out, ffn_g, w_up, conv_w, conv_b)
    dh2, dh2b, d_final_g, loss = _ffn_down_loss(g_act, w_down, h1, final_g, target)

    dacc, db = _ffn_bwd_gate(dh2b, w_down, acc_saved, up)
    dup, dh1, dh1b, dmix, d_ffn_g, dconv = _ffn_bwd_up(dacc, db, up, conv_w, w_up, h1, ffn_g, dh2, w_out)
    d_w_down = _wgrad(g_act, dh2b, "wgrad_down", tk=D_FF // 2)[0]
    d_w_up = _wgrad(n2, dup, "wgrad_up", tn=w_up.shape[2])
    d_w_out = jnp.concatenate([_wgrad(mix_r, dh1b, "wgrad_out_r")[0], _wgrad(o_f, dh1b, "wgrad_out_f")[0]], axis=0)

    early = () if mid is None else mid[0](d_w_out, d_w_up, d_w_down)
    dpr, d_ret_g, *from_sibling = _retention_bwd(dmix, o_ret, proj, cos_t, sin_t, ret_g, states, exchange=early)
    delta = _fox_delta(dmix, o_f)
    scatter = () if mid is None else mid[1](early, from_sibling)
    dpf, dc, dcq, *received = _fox_bwd(proj, dmix, c, ctb, lse, delta, scatter=scatter)
    dff, dffb, d_fox_b = _fox_post(dc, dcq, ff, fb)
    d_w_main = jnp.concatenate([_wgrad(n1, dpr, "wgrad_in_r")[0], _wgrad(n1, dpf, "wgrad_in_f")[0]], axis=1)
    d_w_ff = _wgrad(n1, dffb, "wgrad_in_ff")[0][:, :FOX_HEADS]
    scatter_in = () if last is None else last(d_w_main, d_w_ff)
    dlead, dx, d_attn_g, *received_in = _inproj_bwd(dpr, dpf, dffb, w_main, w_ff, h0, attn_g, dh1, scatter=scatter_in)

    return dict(
        loss=loss[0, 0], dx=dx, dmeta=dlead[N_PAD:], attn_g=d_attn_g, w_main=d_w_main,
        w_ff=d_w_ff, fox_b=d_fox_b[:, :FOX_HEADS], ret_g=d_ret_g, w_out=d_w_out, ffn_g=d_ffn_g,
        w_up=d_w_up, conv_w=dconv[0:3], conv_b=dconv[3:4], w_down=d_w_down, final_g=d_final_g,
        scatter=list(scatter_in) + list(scatter), received=list(received_in) + list(received))


_ANY = pl.BlockSpec(memory_space=pl.ANY)


def _place():
    return lax.axis_index("x"), lax.axis_index("y"), lax.axis_index("c")


def _other_chips(x, y):
    return [(1 - x, y), (x, 1 - y), (1 - x, 1 - y)]


def _allgather_semaphores(n):
    if n == 0:
        return []
    return [pltpu.SemaphoreType.DMA((3 * n,)), pltpu.SemaphoreType.DMA((3 * n,)), pltpu.SemaphoreType.DMA((n,))]


def _allgather_copies(ins, outs, send, recv, loc):
    n = len(ins)
    x, y, c = _place()
    mine = 2 * x + y
    peers = _other_chips(x, y)

    def remote(a, k, slot):
        return pltpu.make_async_remote_copy(
            src_ref=ins[a], dst_ref=outs[a].at[slot], send_sem=send.at[3 * a + k], recv_sem=recv.at[3 * a + k],
            device_id=(peers[k][0], peers[k][1], c), device_id_type=MESH)

    local = [pltpu.make_async_copy(ins[a], outs[a].at[mine], loc.at[a]) for a in range(n)]
    sends = [remote(a, k, mine) for a in range(n) for k in range(3)]
    recvs = [remote(a, k, 2 * peers[k][0] + peers[k][1]) for a in range(n) for k in range(3)]
    return local, sends, recvs


def _chip_allgather_halves(w, small):
    half = w.shape[0] // 2

    def body(w_ref, s_ref, wo_ref, so_ref, send, recv, fsend, frecv, ssend, srecv, loc):
        x, y, c = _place()
        mine = 2 * x + y
        peers = _other_chips(x, y)

        def fetch(k, slot):
            return pltpu.make_async_remote_copy(
                src_ref=w_ref.at[pl.ds(c * half, half)], dst_ref=wo_ref.at[slot, c], send_sem=send.at[k],
                recv_sem=recv.at[k], device_id=(peers[k][0], peers[k][1], c), device_id_type=MESH)

        def forward(k, which):
            slot = 2 * peers[k][0] + peers[k][1]
            return pltpu.make_async_remote_copy(
                src_ref=wo_ref.at[slot, which], dst_ref=wo_ref.at[slot, which], send_sem=fsend.at[k],
                recv_sem=frecv.at[k], device_id=(x, y, 1 - c), device_id_type=MESH)

        def small_copy(k, slot):
            return pltpu.make_async_remote_copy(
                src_ref=s_ref, dst_ref=so_ref.at[slot], send_sem=ssend.at[k], recv_sem=srecv.at[k],
                device_id=(peers[k][0], peers[k][1], c), device_id_type=MESH)

        local = pltpu.make_async_copy(s_ref, so_ref.at[mine], loc.at[0])
        sends = [fetch(k, mine) for k in range(3)] + [small_copy(k, mine) for k in range(3)]
        local.start()
        for cp in sends:
            cp.start()
        forwards = []
        for k in range(3):
            fetch(k, 2 * peers[k][0] + peers[k][1]).wait_recv()
            forwards.append(forward(k, c))
            forwards[-1].start()
        for k in range(3):
            forward(k, 1 - c).wait_recv()
            small_copy(k, 2 * peers[k][0] + peers[k][1]).wait_recv()
        for cp in sends + forwards:
            cp.wait_send()
        local.wait()

    three = pltpu.SemaphoreType.DMA((3,))
    return pl.pallas_call(
        body, name="ag_weights", in_specs=[_ANY] * 2, out_specs=[_ANY] * 2,
        out_shape=[jax.ShapeDtypeStruct((N_CHIPS, 2, half, w.shape[1]), w.dtype),
                   jax.ShapeDtypeStruct((N_CHIPS,) + small.shape, small.dtype)],
        scratch_shapes=[three, three, three, three, three, three, pltpu.SemaphoreType.DMA((1,))],
    )(w, small)


def _chip_allgather(arrays):
    n = len(arrays)

    def body(*refs):
        local, sends, recvs = _allgather_copies(refs[:n], refs[n:2 * n], *refs[2 * n:])
        for cp in local + sends:
            cp.start()
        for cp in recvs:
            cp.wait_recv()
        for cp in sends:
            cp.wait_send()
        for cp in local:
            cp.wait()

    return pl.pallas_call(
        body, name="ag_weights", in_specs=[_ANY] * n, out_specs=[_ANY] * n,
        out_shape=[jax.ShapeDtypeStruct((N_CHIPS,) + a.shape, a.dtype) for a in arrays],
        scratch_shapes=_allgather_semaphores(n),
    )(*arrays)


def _sibling_halves(grads):
    n = len(grads)

    def body(*refs):
        sends, recvs = _sibling_half_copies(refs[:n], refs[n:2 * n], *refs[2 * n:])
        for cp in sends:
            cp.start()
        for cp in recvs:
            cp.wait_recv()
        for cp in sends:
            cp.wait_send()

    return pl.pallas_call(
        body, name="rs_sibling", in_specs=[_ANY] * n, out_specs=[_ANY] * n,
        out_shape=_sibling_half_shapes(grads), scratch_shapes=_sibling_half_semaphores(n),
    )(*grads)


def _sibling_half_shapes(grads):
    return [jax.ShapeDtypeStruct((N_CHIPS, g.shape[1] // 2, g.shape[2]), g.dtype) for g in grads]


def _sibling_half_semaphores(n):
    return [pltpu.SemaphoreType.DMA((n,)), pltpu.SemaphoreType.DMA((n,))] if n else []


def _sibling_half_copies(ins, outs, send, recv):
    x, y, c = _place()

    def half_copy(a, which):
        half = ins[a].shape[1] // 2
        return pltpu.make_async_remote_copy(
            src_ref=ins[a].at[pl.ds(0, N_CHIPS), pl.ds(which * half, half)], dst_ref=outs[a],
            send_sem=send.at[a], recv_sem=recv.at[a], device_id=(x, y, 1 - c), device_id_type=MESH)

    return [half_copy(a, 1 - c) for a in range(len(ins))], [half_copy(a, c) for a in range(len(ins))]


def _scatter_shapes(parts):
    return [jax.ShapeDtypeStruct((3,) + p.shape[1:], p.dtype) for p in parts]


def _scatter_semaphores(n):
    return [pltpu.SemaphoreType.DMA((3 * n,)), pltpu.SemaphoreType.DMA((3 * n,))] if n else []


def _scatter_copies(ins, outs, send, recv):
    x, y, c = _place()
    peers = _other_chips(x, y)
    return [pltpu.make_async_remote_copy(
        src_ref=ins[a].at[2 * peers[k][0] + peers[k][1]], dst_ref=outs[a].at[k], send_sem=send.at[3 * a + k],
        recv_sem=recv.at[3 * a + k], device_id=(peers[k][0], peers[k][1], c), device_id_type=MESH)
        for a in range(len(ins)) for k in range(3)]


def _sibling_allgather(bufs, small):
    n = len(bufs)

    def body(*refs):
        small_in, outs, small_out = refs[n], refs[n + 1:2 * n + 1], refs[2 * n + 1]
        send, recv, s_send, s_recv, loc = refs[2 * n + 2:]
        x, y, c = _place()
        me = 4 * x + 2 * y + c

        def remote(a, which):
            return pltpu.make_async_remote_copy(
                src_ref=outs[a].at[which], dst_ref=outs[a].at[which], send_sem=send.at[a], recv_sem=recv.at[a],
                device_id=(x, y, 1 - c), device_id_type=MESH)

        def peer_of(r):
            return tuple(1 - v if (r >> b) & 1 else v for v, b in ((x, 2), (y, 1), (c, 0)))

        def small_copy(r, slot):
            return pltpu.make_async_remote_copy(
                src_ref=small_in, dst_ref=small_out.at[slot], send_sem=s_send.at[r - 1], recv_sem=s_recv.at[r - 1],
                device_id=peer_of(r), device_id_type=MESH)

        local = pltpu.make_async_copy(small_in, small_out.at[me], loc.at[0])
        sends = [remote(a, c) for a in range(n)] + [small_copy(r, me) for r in range(1, N_DEV)]
        local.start()
        for cp in sends:
            cp.start()
        for r in range(1, N_DEV):
            px, py, pc = peer_of(r)
            small_copy(r, 4 * px + 2 * py + pc).wait_recv()
        for a in range(n):
            remote(a, 1 - c).wait_recv()
        for cp in sends:
            cp.wait_send()
        local.wait()

    outs = pl.pallas_call(
        body, name="ag_sibling", in_specs=[_ANY] * (n + 1), out_specs=[_ANY] * (n + 1),
        out_shape=[jax.ShapeDtypeStruct(b.shape, b.dtype) for b in bufs]
        + [jax.ShapeDtypeStruct((N_DEV,) + small.shape, small.dtype)],
        input_output_aliases={a: a for a in range(n)},
        scratch_shapes=[pltpu.SemaphoreType.DMA((n,)), pltpu.SemaphoreType.DMA((n,)),
                        pltpu.SemaphoreType.DMA((N_DEV - 1,)), pltpu.SemaphoreType.DMA((N_DEV - 1,)),
                        pltpu.SemaphoreType.DMA((1,))],
    )(*bufs, small)
    return [o.reshape(2 * o.shape[1], o.shape[2]) for o in outs[:n]], outs[n]


def _pair_add(full, recv, core, name):
    _, R, C = full.shape
    half = R // 2

    def body(core_ref, a_ref, b_ref, o_ref):
        o_ref[...] = (a_ref[...] + b_ref[...]).astype(BF16)

    return pl.pallas_call(
        body, name=name,
        grid_spec=pltpu.PrefetchScalarGridSpec(
            num_scalar_prefetch=1, grid=(N_CHIPS,),
            in_specs=[pl.BlockSpec((1, half, C), lambda j, core_ref: (j, core_ref[0], 0)),
                      pl.BlockSpec((1, half, C), lambda j, core_ref: (j, 0, 0))],
            out_specs=pl.BlockSpec((1, half, C), lambda j, core_ref: (j, 0, 0))),
        out_shape=jax.ShapeDtypeStruct((N_CHIPS, half, C), BF16),
        compiler_params=_params(("parallel",)),
    )(core, full, recv)


def _sum_partials(own_all, recv, place, name, tiles=2):
    _, R, C = own_all.shape
    tr = R // tiles

    def body(place_ref, own_ref, r_ref, o_ref):
        acc = own_ref[0].astype(F32)
        for k in range(3):
            acc = acc + r_ref[k].astype(F32)
        o_ref[0] = acc

    return pl.pallas_call(
        body, name=name,
        grid_spec=pltpu.PrefetchScalarGridSpec(
            num_scalar_prefetch=1, grid=(tiles,),
            in_specs=[pl.BlockSpec((1, tr, C), lambda i, place_ref: (place_ref[0], i, 0)),
                      pl.BlockSpec((3, tr, C), lambda i, place_ref: (0, i, 0))],
            out_specs=pl.BlockSpec((1, tr, C), lambda i, place_ref: (place_ref[1], i, 0))),
        out_shape=jax.ShapeDtypeStruct((2, R, C), F32),
        compiler_params=_params(("parallel",)),
    )(place, own_all, recv)


def _adamw_math(w, g, m, v):
    m2 = ADAM_B1 * m + (1.0 - ADAM_B1) * g
    v2 = ADAM_B2 * v + (1.0 - ADAM_B2) * (g * g)
    m_hat = m2 / (1.0 - ADAM_B1 ** ADAM_STEP)
    v_hat = v2 / (1.0 - ADAM_B2 ** ADAM_STEP)
    return -ADAM_LR * (m_hat / (jnp.sqrt(v_hat) + ADAM_EPS) + ADAM_WD * w), m2, v2


ROW_ATTN_G, ROW_FFN_G, ROW_FINAL_G, ROW_MISC, ROW_CONV_B, ROW_CONV_W, ROW_META, SMALL_ROWS = 0, 1, 2, 3, 4, 8, 24, 40
MISC_FOX_B, MISC_LOSS = 512, 640


def _small_pack(out):
    def rows(a, n):
        a = a.astype(F32)
        return jnp.pad(a, ((0, n - a.shape[0]), (0, D_MODEL - a.shape[1])))

    misc = jnp.concatenate([out["ret_g"], out["fox_b"], jnp.zeros((1, MISC_LOSS - MISC_FOX_B - FOX_HEADS), F32),
                            out["loss"].reshape(1, 1)], axis=1)
    conv_b = jnp.pad(out["conv_b"], ((0, 0), (0, (-D_FF) % D_MODEL))).reshape(-1, D_MODEL)
    conv_w = out["conv_w"].reshape(3, N_CHIPS, -1).transpose(1, 0, 2).reshape(3 * N_CHIPS, -1)
    return jnp.concatenate([
        rows(out["attn_g"], 1), rows(out["ffn_g"], 1), rows(out["final_g"], 1), rows(misc, 1),
        rows(conv_b, ROW_CONV_W - ROW_CONV_B), rows(conv_w, ROW_META - ROW_CONV_W), rows(out["dmeta"], N_META)], axis=0)


def _small_update(packs, chip, ws, ms, vs):
    n = len(ws)
    meta_w, conv_sw = ws[0].shape[1], ws[5].shape[2]
    assert packs.shape == (N_DEV, SMALL_ROWS, D_MODEL) and ws[0].shape[0] == N_META and ws[5].shape[:2] == (1, 3)

    def body(chip_ref, p_ref, *refs):
        w_refs, m_refs, v_refs = refs[:n], refs[n:2 * n], refs[2 * n:3 * n]
        loss_ref, out_refs, tot = refs[3 * n], refs[3 * n + 1:7 * n + 1], refs[7 * n + 1]
        acc = p_ref[0]
        for d in range(1, N_DEV):
            acc = acc + p_ref[d]
        tot[...] = acc

        def of_chip(pieces):
            val = pieces[-1]
            for j in range(N_CHIPS - 2, -1, -1):
                val = jnp.where(chip_ref[0] == j, pieces[j], val)
            return val

        row = lambda r, lo=0, hi=D_MODEL: tot[r:r + 1, lo:hi]
        grads = [
            of_chip([tot[ROW_META:ROW_META + N_META, j * meta_w:(j + 1) * meta_w] for j in range(N_CHIPS)]),
            row(ROW_ATTN_G), row(ROW_MISC, MISC_FOX_B, MISC_FOX_B + FOX_HEADS), row(ROW_MISC, 0, MISC_FOX_B),
            row(ROW_FFN_G),
            of_chip([tot[ROW_CONV_W + 3 * j:ROW_CONV_W + 3 * j + 3, 0:conv_sw] for j in range(N_CHIPS)]),
            jnp.concatenate([row(ROW_CONV_B), row(ROW_CONV_B + 1), row(ROW_CONV_B + 2, 0, D_FF - 2 * D_MODEL)], axis=1),
            row(ROW_FINAL_G)]
        loss_ref[...] = row(ROW_MISC, MISC_LOSS, MISC_LOSS + BLK)
        for k in range(n):
            at = (0,) if len(ws[k].shape) == 3 else (Ellipsis,)
            res = (grads[k],) + _adamw_math(w_refs[k][at], grads[k], m_refs[k][at], v_refs[k][at])
            for kind in range(4):
                out_refs[kind * n + k][at] = res[kind]

    res = pl.pallas_call(
        body, name="small_update",
        grid_spec=pltpu.PrefetchScalarGridSpec(
            num_scalar_prefetch=1, grid=(1,),
            in_specs=[_full(packs.shape)] + [_full(a.shape) for a in list(ws) * 3],
            out_specs=[_full((1, BLK))] + [_full(a.shape) for a in list(ws) * 4],
            scratch_shapes=[pltpu.VMEM((SMALL_ROWS, D_MODEL), F32)]),
        out_shape=[jax.ShapeDtypeStruct((1, BLK), F32)] + [jax.ShapeDtypeStruct(a.shape, F32) for a in list(ws) * 4],
        compiler_params=_params(("arbitrary",)),
    )(chip, packs, *ws, *ms, *vs)
    return res[0], res[1:n + 1], res[n + 1:2 * n + 1], res[2 * n + 1:3 * n + 1], res[3 * n + 1:]


def _adamw(w, g, m, v, name, tiles=4):
    R, tail = w.shape[0], w.shape[1:]
    assert R % tiles == 0
    tr = R // tiles

    def body(w_ref, g_ref, m_ref, v_ref, go_ref, d_ref, m2_ref, v2_ref):
        g_ = g_ref[...]
        go_ref[...] = g_
        d_ref[...], m2_ref[...], v2_ref[...] = _adamw_math(w_ref[...], g_, m_ref[...], v_ref[...])

    spec = pl.BlockSpec((tr,) + tail, lambda i: (i,) + (0,) * len(tail))
    return pl.pallas_call(
        body, name=name, grid=(tiles,), in_specs=[spec] * 4, out_specs=[spec] * 4,
        out_shape=[jax.ShapeDtypeStruct(w.shape, F32)] * 4,
        compiler_params=_params(("parallel",)),
    )(w, g, m, v)


def _row_vector_tiles(n, most=80):
    return next(t for t in range(1, n + 1) if n % t == 0 and n // t <= most)


def _pack_rows(pieces, rows):
    flat = jnp.concatenate([jnp.pad(p.reshape(-1).astype(F32), (0, (-p.size) % D_MODEL)) for p in pieces])
    return jnp.pad(flat, (0, rows * D_MODEL - flat.size)).reshape(rows, D_MODEL)


def _unpack_rows(pack, shapes):
    flat = pack.reshape(-1)
    out, off = [], 0
    for shp in shapes:
        size = int(np.prod(shp))
        out.append(flat[off:off + size].reshape(shp))
        off += size + (-size) % D_MODEL
    return out


def _kernel_order(w):
    parts = [w[:, 0:RET_W]]
    for p in range(FOX_HEADS // 2):
        parts += [w[:, RET_W + part * 512 + p * BLK:RET_W + part * 512 + (p + 1) * BLK] for part in range(3)]
    return jnp.concatenate(parts, axis=1)


def _reference_order(g_main, g_ff):
    parts = [g_main[:, 0:RET_W]]
    for part in range(3):
        parts += [g_main[:, RET_W + 384 * p + part * BLK:RET_W + 384 * p + (part + 1) * BLK] for p in range(FOX_HEADS // 2)]
    return jnp.concatenate(parts + [g_ff], axis=1)


def kernel(x, meta_tokens, attn_norm_g, w_in, fox_forget_b, ret_norm_g, w_out, ffn_norm_g, w_up, conv_w, conv_b, w_down, final_norm_g, loss_target, m_meta_tokens, m_attn_norm_g, m_w_in, m_fox_forget_b, m_ret_norm_g, m_w_out, m_ffn_norm_g, m_w_up, m_conv_w, m_conv_b, m_w_down, m_final_norm_g, v_meta_tokens, v_attn_norm_g, v_w_in, v_fox_forget_b, v_ret_norm_g, v_w_out, v_ffn_norm_g, v_w_up, v_conv_w, v_conv_b, v_w_down, v_final_norm_g):
    chip = 2 * lax.axis_index("x") + lax.axis_index("y")
    core = lax.axis_index("c")

    small_w = _pack_rows([meta_tokens, conv_w[0]], 8)
    w_in_b = w_in[0].astype(BF16)
    g_in, g_small = _chip_allgather_halves(w_in_b, small_w)
    g_in = lax.dynamic_update_slice(g_in.reshape((N_CHIPS,) + w_in_b.shape), w_in_b[None], (chip, 0, 0))
    w_in_full = g_in.transpose(1, 0, 2).reshape(D_MODEL, IN_WIDTH)
    w_main = _kernel_order(w_in_full)
    w_ff = jnp.pad(w_in_full[:, MAIN_W:], ((0, 0), (0, BLK - FOX_HEADS)))
    small_parts = [_unpack_rows(g_small[j], [meta_tokens.shape, conv_w.shape[1:]]) for j in range(N_CHIPS)]
    meta_full = jnp.concatenate([sp[0] for sp in small_parts], axis=1)
    conv_w_full = jnp.concatenate([sp[1] for sp in small_parts], axis=1)

    core_idx = core.reshape(1).astype(jnp.int32)
    place = jnp.stack([chip, core]).astype(jnp.int32)

    def assemble(gathered):
        g_out, g_up, g_down = gathered
        return g_out.reshape(D_MODEL, D_MODEL), g_up, g_down.reshape(D_FF, D_MODEL)

    def early_arrays(d_w_out, d_w_up, d_w_down):
        return [d_w_out.reshape(N_CHIPS, -1, D_MODEL), d_w_up, d_w_down.reshape(N_CHIPS, -1, D_MODEL)]

    def in_sums(d_w_main, d_w_ff):
        g_in_full = _reference_order(d_w_main, d_w_ff).reshape(D_MODEL, N_CHIPS, -1).transpose(1, 0, 2)
        (from_sib,) = _sibling_halves([g_in_full])
        return [_pair_add(g_in_full, from_sib, core_idx, "pair_add_in")]

    def early_sums(early, from_sib):
        return [_pair_add(g, r, core_idx, "pair_add_" + nm) for g, r, nm in zip(early, from_sib, ("out", "up", "down"))]

    out = _local_step(x[0], loss_target[0], meta_full, attn_norm_g, w_main, w_ff, fox_forget_b, ret_norm_g,
                      None, ffn_norm_g, None, conv_w_full, conv_b, None, final_norm_g[None],
                      late=([w_out[0].astype(BF16), w_up[0].astype(BF16), w_down[0].astype(BF16)], assemble),
                      mid=(early_arrays, early_sums), last=in_sums)

    names = ("in", "out", "up", "down")
    totals = [_sum_partials(s, q, place, "sum_chips_" + nm) for s, q, nm in zip(out["scatter"], out["received"], names)]
    (grad_in, grad_out, grad_up, grad_down), small_all = _sibling_allgather(totals, _small_pack(out))

    big_w = [(w_out, m_w_out, v_w_out, grad_out, "adamw_out"), (w_up, m_w_up, v_w_up, grad_up, "adamw_up"),
             (w_down, m_w_down, v_w_down, grad_down, "adamw_down")]
    big_res = [[r[None] for r in _adamw(w[0], g, m[0], v[0], nm)] for w, m, v, g, nm in big_w]
    as_rows = lambda a: jnp.transpose(a, (2, 0, 1))
    in_rows = _adamw(as_rows(w_in), grad_in.T[:, None, :], as_rows(m_w_in), as_rows(v_w_in), "adamw_in",
                     tiles=_row_vector_tiles(w_in.shape[2]))
    big_res.insert(0, [jnp.transpose(r, (1, 2, 0)) for r in in_rows])
    small_p = [meta_tokens, attn_norm_g, fox_forget_b, ret_norm_g, ffn_norm_g, conv_w, conv_b, final_norm_g[None]]
    small_m = [m_meta_tokens, m_attn_norm_g, m_fox_forget_b, m_ret_norm_g, m_ffn_norm_g, m_conv_w, m_conv_b, m_final_norm_g[None]]
    small_v = [v_meta_tokens, v_attn_norm_g, v_fox_forget_b, v_ret_norm_g, v_ffn_norm_g, v_conv_w, v_conv_b, v_final_norm_g[None]]
    loss_row, *small_res = _small_update(small_all, chip.reshape(1).astype(jnp.int32), small_p, small_m, small_v)
    loss = loss_row[0, 0]

    def ordered(kind):
        sm = list(small_res[kind][:-1]) + [small_res[kind][-1][0]]
        bg = [r[kind] for r in big_res]
        return [sm[0], sm[1], bg[0], sm[2], sm[3], bg[1], sm[4], bg[2], sm[5], sm[6], bg[3], sm[7]]

    return (loss, out["dx"][None], *ordered(0), *ordered(1), *ordered(2), *ordered(3))
```

```python
import functools

import numpy as np
import jax
import jax.numpy as jnp
from jax import lax
from jax.experimental import pallas as pl
from jax.experimental.pallas import tpu as pltpu

F32 = jnp.float32
BF16 = jnp.bfloat16

D_MODEL = 1024
N_META = 16
BLK = 128
UNIT = 2 * BLK
FOX_PAIRS = 2
WIDE = 4
CHUNK = 64
N_PAD = BLK - N_META
PREFIX = BLK
RET_HEADS = 4
FOX_HEADS = 8
HEAD_LANES = 64
D_FF = 2816
ROPE_BASE = 10000.0
EPS = 1e-6
NEG = -1e30
LOG2E = 1.4426950408889634
RET_W = 1536
FOX_W = 1536
MAIN_W = RET_W + FOX_W
IN_WIDTH = MAIN_W + FOX_HEADS
N_CHIPS = 4
N_DEV = 8

ADAM_LR = 0.001
ADAM_B1 = 0.9
ADAM_B2 = 0.999
ADAM_EPS = 1e-08
ADAM_WD = 0.01
ADAM_STEP = 10

MESH = pl.DeviceIdType.MESH
VMEM_LIMIT_MB = 56

_NT = (((1,), (1,)), ((), ()))
_TN = (((0,), (0,)), ((), ()))


def _dot(a, b):
    return jnp.dot(a, b, preferred_element_type=F32)


def _dot_nt(a, b):
    return lax.dot_general(a, b, _NT, preferred_element_type=F32)


def _dot_tn(a, b):
    return lax.dot_general(a, b, _TN, preferred_element_type=F32)


def _params(dims=None, vmem_mb=VMEM_LIMIT_MB):
    kw = dict(vmem_limit_bytes=vmem_mb << 20)
    if dims is not None:
        kw["dimension_semantics"] = dims
    return pltpu.CompilerParams(**kw)


def _row_tile(n, prefs=(384, 256, 128)):
    for t in prefs:
        if n % t == 0:
            return t
    raise ValueError(f"no row tile for {n}")


def _iota(shape, dim):
    return lax.broadcasted_iota(jnp.int32, shape, dim)


def _pick_row(tile, row):
    sub = _iota(tile.shape, 0)
    return jnp.sum(jnp.where(sub == row, tile, 0.0), axis=0, keepdims=True)


def _split3(x):
    hi = x.astype(BF16)
    r1 = x - hi.astype(F32)
    mid = r1.astype(BF16)
    lo = (r1 - mid.astype(F32)).astype(BF16)
    return hi, mid, lo


def _full(shape):
    nd = len(shape)
    return pl.BlockSpec(shape, lambda *_: (0,) * nd)


def _in_perm():
    cols = list(range(RET_W))
    for p in range(FOX_HEADS // 2):
        for part in range(3):
            start = RET_W + part * 512 + p * BLK
            cols += list(range(start, start + BLK))
    return np.asarray(cols, np.int32)


def _rotary_tables(L):
    half = HEAD_LANES // 2
    inv = 1.0 / (ROPE_BASE ** (jnp.arange(half, dtype=F32) / half))
    ang = jnp.arange(L).astype(F32)[:, None] * inv[None, :]
    cos, sin = jnp.cos(ang), jnp.sin(ang)
    cos_t = jnp.tile(cos, (1, 4))
    sin_t = jnp.tile(jnp.concatenate([-sin, sin], axis=1), (1, 2))
    return cos_t, sin_t


def _decay_tables():
    gam = 1.0 - 2.0 ** (-5.0 - np.arange(RET_HEADS, dtype=np.float64))
    n = np.arange(BLK)
    same_or_past = (n[:, None] // CHUNK) >= (n[None, :] // CHUNK)
    dist = np.abs(n[:, None] - n[None, :])
    dmat = np.stack([np.where(same_or_past, g ** dist, 0.0) for g in gam]).astype(np.float32)
    lane_head = np.arange(BLK) // HEAD_LANES
    wq = np.stack([gam[2 * p + lane_head][None, :] ** (n[:, None] + 1.0) for p in range(2)]).astype(np.float32)
    wk = np.stack([gam[2 * p + lane_head][None, :] ** (BLK - 1.0 - n[:, None]) for p in range(2)]).astype(np.float32)
    g_blk = tuple(float(g ** BLK) for g in gam)
    return jnp.asarray(dmat), jnp.asarray(wq), jnp.asarray(wk), g_blk


def _shifted_blocks(tm):
    nb = tm // BLK
    return [pl.BlockSpec((BLK, D_MODEL), lambda i, j=j: (jnp.maximum(nb * i + j - 1, 0), 0)) for j in range(nb)]


def _rms_inproj(head, x, g, w_main, w_ff):
    L = x.shape[0] + BLK
    tm = _row_tile(L)
    nb = tm // BLK

    def body(head_ref, *refs):
        x_refs, (g_ref, wm_ref, wf_ref, h_ref, n_ref, p_ref, ff_ref) = refs[:nb], refs[nb:]
        parts = [r[...] for r in x_refs]
        parts[0] = jnp.where(pl.program_id(0) == 0, head_ref[...], parts[0])
        h = jnp.concatenate(parts, axis=0)
        h_ref[...] = h
        r = lax.rsqrt(jnp.mean(h * h, axis=-1, keepdims=True) + EPS)
        n = (h * r * g_ref[...]).astype(BF16)
        n_ref[...] = n
        p_ref[...] = _dot(n, wm_ref[...]).astype(BF16)
        ff_ref[...] = _dot(n, wf_ref[...])

    rows = lambda w: pl.BlockSpec((tm, w), lambda i: (i, 0))
    return pl.pallas_call(
        body, name="f_inproj", grid=(L // tm,),
        in_specs=[_full((BLK, D_MODEL))] + _shifted_blocks(tm)
        + [_full((1, D_MODEL)), _full((D_MODEL, MAIN_W)), _full((D_MODEL, BLK))],
        out_specs=[rows(D_MODEL), rows(D_MODEL), rows(MAIN_W), rows(BLK)],
        out_shape=[jax.ShapeDtypeStruct((L, D_MODEL), F32), jax.ShapeDtypeStruct((L, D_MODEL), BF16),
                   jax.ShapeDtypeStruct((L, MAIN_W), BF16), jax.ShapeDtypeStruct((L, BLK), F32)],
        compiler_params=_params(("parallel",)),
    )(head, *([x] * nb), g, w_main, w_ff)


def _block_group(nblk):
    return 3 if nblk % 3 == 0 else 1


def _fox_prep(ff, fb):
    L = ff.shape[0]
    nblk = L // BLK
    G = _block_group(nblk)

    def body(ff_ref, b_ref, c_ref, ct_ref, carry):
        @pl.when(pl.program_id(0) == 0)
        def _():
            carry[...] = jnp.zeros_like(carry)

        tri = (_iota((BLK, BLK), 0) >= _iota((BLK, BLK), 1)).astype(BF16)
        live = _iota((BLK, BLK), 1) < FOX_HEADS
        run = carry[...]
        for b in range(G):
            z = ff_ref[b * BLK:(b + 1) * BLK, :] + b_ref[...]
            lf = jnp.where(live, jnp.minimum(z, 0.0) - jnp.log1p(jnp.exp(-jnp.abs(z))), 0.0)
            hi, mid, lo = _split3(lf)
            cs = (_dot(tri, hi) + _dot(tri, mid) + _dot(tri, lo) + run) * LOG2E
            c_ref[b * BLK:(b + 1) * BLK, :] = cs
            ct_ref[b] = cs.T[0:8, :]
            run = run + jnp.sum(lf, axis=0, keepdims=True)
        carry[...] = run

    return pl.pallas_call(
        body, name="f_foxprep", grid=(nblk // G,),
        in_specs=[pl.BlockSpec((G * BLK, BLK), lambda i: (i, 0)), _full((1, BLK))],
        out_specs=[pl.BlockSpec((G * BLK, BLK), lambda i: (i, 0)), pl.BlockSpec((G, 8, BLK), lambda i: (i, 0, 0))],
        out_shape=[jax.ShapeDtypeStruct((L, BLK), F32), jax.ShapeDtypeStruct((nblk, 8, BLK), F32)],
        scratch_shapes=[pltpu.VMEM((1, BLK), F32)],
        compiler_params=_params(("arbitrary",)),
    )(ff, fb)


def _rot_fns(cos, sin):
    lane = _iota((BLK, BLK), 1)
    first = (lane & (HEAD_LANES - 1)) < HEAD_LANES // 2

    def swap(x):
        return jnp.where(first, pltpu.roll(x, BLK - 32, 1), pltpu.roll(x, 32, 1))

    def rot(x):
        return x * cos + swap(x) * sin

    def rot_t(dy):
        return dy * cos + swap(dy * sin)

    return rot, rot_t


def _retention_fwd(proj, cos_t, sin_t, ret_g):
    L = proj.shape[0]
    nblk = L // BLK
    G = _block_group(nblk)
    dmat, wq_t, wk_t, g_blk = _decay_tables()

    def body(q_ref, k_ref, v_ref, gate_ref, cos_ref, sin_ref, d_ref, wq_ref, wk_ref, rg_ref,
             mix_ref, o_ref, rs_ref, state):
        @pl.when(pl.program_id(0) == 0)
        def _():
            state[...] = jnp.zeros_like(state)

        lane = _iota((BLK, BLK), 1)
        sub = _iota((BLK, BLK), 0)
        for b in range(G):
            rows = slice(b * BLK, (b + 1) * BLK)
            rot, _ = _rot_fns(cos_ref[rows, :], sin_ref[rows, :])
            for p in range(2):
                qr = rot(q_ref[rows, p * BLK:(p + 1) * BLK].astype(F32))
                kr = rot(k_ref[rows, p * BLK:(p + 1) * BLK].astype(F32)) * (HEAD_LANES ** -0.5)
                kr_b = kr.astype(BF16)
                qw = (qr * wq_ref[p]).astype(BF16)
                kw = (kr * wk_ref[p]).astype(BF16)
                for e in range(2):
                    h = 2 * p + e
                    cols = slice(h * BLK, (h + 1) * BLK)
                    qm = jnp.where((lane >> 6) == e, qr, 0.0).astype(BF16)
                    s = _dot_nt(qm, kr_b) * d_ref[h]
                    vh = v_ref[rows, cols]
                    st = state[h]
                    rs_ref[b, h] = st
                    o = _dot(s.astype(BF16), vh) + _dot(qw, st.astype(BF16))
                    u = jnp.where((sub >> 6) == e, _dot_tn(kw, vh), 0.0)
                    state[h] = g_blk[h] * st + u
                    rn = lax.rsqrt(jnp.mean(o * o, axis=-1, keepdims=True) + EPS)
                    gate = gate_ref[rows, cols].astype(F32)
                    o_ref[rows, cols] = o
                    mix_ref[rows, cols] = (o * rn * rg_ref[:, cols] * (gate * jax.nn.sigmoid(gate))).astype(BF16)

    row = lambda c: (lambda i: (i, c))
    return pl.pallas_call(
        body, name="f_retention", grid=(nblk // G,),
        in_specs=[pl.BlockSpec((G * BLK, 256), row(0)), pl.BlockSpec((G * BLK, 256), row(1)),
                  pl.BlockSpec((G * BLK, 512), row(1)), pl.BlockSpec((G * BLK, 512), row(2)),
                  pl.BlockSpec((G * BLK, BLK), row(0)), pl.BlockSpec((G * BLK, BLK), row(0)),
                  _full((RET_HEADS, BLK, BLK)), _full((2, BLK, BLK)), _full((2, BLK, BLK)), _full((1, 512))],
        out_specs=[pl.BlockSpec((G * BLK, 512), row(0)), pl.BlockSpec((G * BLK, 512), row(0)),
                   pl.BlockSpec((G, RET_HEADS, BLK, BLK), lambda i: (i, 0, 0, 0))],
        out_shape=[jax.ShapeDtypeStruct((L, 512), BF16), jax.ShapeDtypeStruct((L, 512), F32),
                   jax.ShapeDtypeStruct((nblk, RET_HEADS, BLK, BLK), F32)],
        scratch_shapes=[pltpu.VMEM((RET_HEADS, BLK, BLK), F32)],
        compiler_params=_params(("arbitrary",)),
    )(proj, proj, proj, proj, cos_t, sin_t, dmat, wq_t, wk_t, ret_g)


def _fox_units(L):
    nblk = L // BLK
    assert L % BLK == 0 and nblk % 2 == 1, "sequence must be one 128-row block plus whole 256-row tiles"
    return nblk, (nblk - 1) // 2


def _fox_tile_masks():
    sub, lane = _iota((BLK, BLK), 0), _iota((BLK, BLK), 1)
    valid = _iota((BLK, UNIT), 0) >= N_PAD
    diag = _iota((UNIT, UNIT), 0) <= _iota((UNIT, UNIT), 1)
    r, q = _iota((BLK + UNIT, UNIT), 0), _iota((BLK + UNIT, UNIT), 1)
    first_and_diag = ((r < BLK) & (r >= N_PAD)) | ((r >= BLK) & (r - BLK <= q))
    return dict(first=(sub <= lane) & (sub >= N_PAD), valid=valid, diag=diag, first_and_diag=first_and_diag)


def _fox_fwd(proj, c, ctb, gather=()):
    L = proj.shape[0]
    nblk, nu = _fox_units(L)
    scale = HEAD_LANES ** -0.5 * LOG2E
    ng = len(gather)
    steps = FOX_HEADS // (2 * FOX_PAIRS)

    def body(qkv_ref, c_ref, ct_ref, *rest):
        g_in, (of_ref, lse_ref), g_out = rest[:ng], rest[ng:ng + 2], rest[ng + 2:2 * ng + 2]
        vt, csb = rest[2 * ng + 2:2 * ng + 4]
        p = pl.program_id(0)
        heads = [(pp, e, 2 * FOX_PAIRS * p + 2 * pp + e) for pp in range(FOX_PAIRS) for e in range(2)]

        @pl.when(p == 0)
        def _():
            lse_ref[...] = jnp.zeros_like(lse_ref)
            if ng:
                local, sends, _ = _allgather_copies(g_in, g_out, *rest[2 * ng + 4:])
                for cp in local + sends:
                    cp.start()

        lane = _iota((BLK, BLK), 1)
        sub8 = _iota((8, BLK), 0)
        masks = _fox_tile_masks()

        def pre(j, carry):
            off = pl.multiple_of(j * BLK, BLK)
            ct = c_ref[pl.ds(off, BLK), :]
            for pp in range(FOX_PAIRS):
                vt[pp, j] = qkv_ref[pl.ds(off, BLK), pp * 384 + 2 * BLK:pp * 384 + 3 * BLK].astype(F32).T.astype(BF16)
            for hh, (_, _, h) in enumerate(heads):
                col = jnp.sum(jnp.where(lane == h, ct, 0.0), axis=1, keepdims=True)
                csb[hh, j] = jnp.broadcast_to(col, (BLK, BLK))
            return carry

        lax.fori_loop(0, nblk, pre, 0)

        def attend(qblk, nq, n_whole):
            qlen = nq * BLK
            qoff = pl.multiple_of(qblk * BLK, BLK)
            qlane = _iota((qlen, BLK), 1)
            qs = [qkv_ref[pl.ds(qoff, qlen), pp * 384:pp * 384 + BLK].astype(F32) * scale for pp in range(FOX_PAIRS)]
            qm = [jnp.where((qlane >> 6) == e, qs[pp], 0.0).astype(BF16) for pp, e, _ in heads]
            ct_row = [jnp.concatenate([_pick_row(ct_ref[qblk + a], h) for a in range(nq)], axis=1) for _, _, h in heads]

            def step(segs, mask, st):
                blocks = [kblk + b for kblk, nk in segs for b in range(nk)]
                kts = []
                for pp in range(FOX_PAIRS):
                    kt = [qkv_ref[pl.ds(pl.multiple_of(kblk * BLK, BLK), nk * BLK), pp * 384 + BLK:pp * 384 + 2 * BLK]
                          for kblk, nk in segs]
                    kts.append(kt[0] if len(kt) == 1 else jnp.concatenate(kt, axis=0))
                out = []
                for hh, (pp, e, _) in enumerate(heads):
                    m, l, acc = st[3 * hh:3 * hh + 3]
                    s = _dot_nt(kts[pp], qm[hh])
                    t = jnp.concatenate([s[b * BLK:(b + 1) * BLK] - jnp.concatenate([csb[hh, blk]] * nq, axis=1)
                                         for b, blk in enumerate(blocks)], axis=0)
                    if mask is not None:
                        t = jnp.where(mask, t, NEG)
                    m_new = jnp.maximum(m, jnp.max(t, axis=0, keepdims=True) + ct_row[hh])
                    alpha = jnp.exp2(m - m_new)
                    pr = jnp.exp2(t - (m_new - ct_row[hh]))
                    l = alpha * l + jnp.sum(pr, axis=0, keepdims=True)
                    pr_b = pr.astype(BF16)
                    pv = None
                    for b, blk in enumerate(blocks):
                        part = _dot(vt[pp, blk, e * HEAD_LANES:(e + 1) * HEAD_LANES, :], pr_b[b * BLK:(b + 1) * BLK])
                        pv = part if pv is None else pv + part
                    out += [m_new, l, alpha * acc + pv]
                return tuple(out)

            st = (jnp.full((1, qlen), NEG, F32), jnp.zeros((1, qlen), F32),
                  jnp.zeros((HEAD_LANES, qlen), F32)) * len(heads)
            if nq == 1:
                st = step([(0, 1)], masks["first"], st)
            else:
                st = step([(0, 1), (qblk, 2)], masks["first_and_diag"], st)
                n_wide = n_whole // WIDE
                st = lax.fori_loop(0, n_wide, lambda j, s_: step([(1 + 2 * WIDE * j, 2 * WIDE)], None, s_), st)
                rest = 1 + 2 * WIDE * n_wide
                st = lax.cond((n_whole & 2) != 0, lambda s_: step([(rest, 4)], None, s_), lambda s_: s_, st)
                st = lax.cond((n_whole & 1) != 0, lambda s_: step([(rest + 2 * (n_whole & 2), 2)], None, s_),
                              lambda s_: s_, st)
            for pp in range(FOX_PAIRS):
                lo, hi = st[6 * pp:6 * pp + 3], st[6 * pp + 3:6 * pp + 6]
                o_t = jnp.concatenate([lo[2] * (1.0 / lo[1]), hi[2] * (1.0 / hi[1])], axis=0)
                of_ref[pl.ds(qoff, qlen), pp * BLK:(pp + 1) * BLK] = o_t.T.astype(BF16)
            lse = [st[3 * hh] + jnp.log(st[3 * hh + 1]) * LOG2E for hh in range(len(heads))]
            for a in range(nq):
                upd = jnp.zeros((8, BLK), F32)
                for hh, (_, _, h) in enumerate(heads):
                    upd = upd + jnp.where(sub8 == h, lse[hh][:, a * BLK:(a + 1) * BLK], 0.0)
                lse_ref[qblk + a] = lse_ref[qblk + a] + upd

        attend(0, 1, 0)

        def q_loop(u, carry):
            attend(1 + 2 * u, 2, u)
            return carry

        lax.fori_loop(0, nu, q_loop, 0)

        if ng:
            @pl.when(p == steps - 1)
            def _():
                local, sends, recvs = _allgather_copies(g_in, g_out, *rest[2 * ng + 4:])
                for cp in recvs:
                    cp.wait_recv()
                for cp in sends:
                    cp.wait_send()
                for cp in local:
                    cp.wait()

    width = 384 * FOX_PAIRS
    return pl.pallas_call(
        body, name="f_fox", grid=(steps,),
        in_specs=[pl.BlockSpec((L, width), lambda p: (0, RET_W // width + p)), _full((L, BLK)), _full((nblk, 8, BLK))]
        + [_ANY] * ng,
        out_specs=[pl.BlockSpec((L, FOX_PAIRS * BLK), lambda p: (0, p)), _full((nblk, 8, BLK))] + [_ANY] * ng,
        out_shape=[jax.ShapeDtypeStruct((L, 512), BF16), jax.ShapeDtypeStruct((nblk, 8, BLK), F32)]
        + [jax.ShapeDtypeStruct((N_CHIPS,) + a.shape, a.dtype) for a in gather],
        scratch_shapes=[pltpu.VMEM((FOX_PAIRS, nblk, BLK, BLK), BF16), pltpu.VMEM((2 * FOX_PAIRS, nblk, BLK, BLK), F32)]
        + _allgather_semaphores(ng),
        compiler_params=_params(("arbitrary",)),
    )(proj, c, ctb, *gather)


def _outproj_up(mix_r, o_f, h0, w_out, ffn_g, w_up, conv_w, conv_b):
    L = h0.shape[0]
    tm = _row_tile(L)
    shard = w_up.shape[2]
    assert 2 * shard == D_FF
    cw = [conv_w[j:j + 1] for j in range(3)]
    resident = lambda shape: pl.BlockSpec(shape, lambda i: (0,) * len(shape), pipeline_mode=pl.Buffered(1))

    def body(mr_ref, of_ref, h0_ref, wo_ref, g_ref, wu_ref, cw0, cw1, cw2, cb_ref,
             h1_ref, n2_ref, up_ref, act_ref, acc_ref, halo):
        i = pl.program_id(0)

        @pl.when(i == 0)
        def _():
            halo[...] = jnp.zeros_like(halo)

        h1 = h0_ref[...] + _dot(mr_ref[...], wo_ref[0:512, :]) + _dot(of_ref[...], wo_ref[512:1024, :])
        h1_ref[...] = h1
        r = lax.rsqrt(jnp.mean(h1 * h1, axis=-1, keepdims=True) + EPS)
        n2 = (h1 * r * g_ref[...]).astype(BF16)
        n2_ref[...] = n2
        live = i * tm + _iota((tm, 1), 0) >= N_PAD
        for half in range(2):
            cols = slice(half * shard, (half + 1) * shard)
            a_b = _dot(n2, wu_ref[half]).astype(BF16)
            b_b = _dot(n2, wu_ref[2 + half]).astype(BF16)
            up_ref[:, cols] = a_b
            up_ref[:, D_FF + half * shard:D_FF + (half + 1) * shard] = b_b
            a = jnp.where(live, a_b.astype(F32), 0.0)
            _, _, acc = _conv_taps(a, halo[:, cols], [cw0[:, cols], cw1[:, cols], cw2[:, cols]], cb_ref[:, cols])
            act_ref[:, cols] = (acc * jax.nn.sigmoid(acc) * b_b.astype(F32)).astype(BF16)
            acc_ref[:, cols] = acc.astype(BF16)
            halo[:, cols] = a[tm - 8:tm, :]

    rows = lambda w: pl.BlockSpec((tm, w), lambda i: (i, 0))
    return pl.pallas_call(
        body, name="f_outproj_up", grid=(L // tm,),
        in_specs=[rows(512), rows(512), rows(D_MODEL), resident((D_MODEL, D_MODEL)), _full((1, D_MODEL)),
                  resident((N_CHIPS, D_MODEL, shard)), _full((1, D_FF)), _full((1, D_FF)), _full((1, D_FF)),
                  _full((1, D_FF))],
        out_specs=[rows(D_MODEL), rows(D_MODEL), rows(2 * D_FF), rows(D_FF), rows(D_FF)],
        out_shape=[jax.ShapeDtypeStruct((L, D_MODEL), F32), jax.ShapeDtypeStruct((L, D_MODEL), BF16),
                   jax.ShapeDtypeStruct((L, 2 * D_FF), BF16), jax.ShapeDtypeStruct((L, D_FF), BF16),
                   jax.ShapeDtypeStruct((L, D_FF), BF16)],
        scratch_shapes=[pltpu.VMEM((8, D_FF), F32)],
        compiler_params=_params(("arbitrary",)),
    )(mix_r, o_f, h0, w_out, ffn_g, w_up, cw[0], cw[1], cw[2], conv_b)


def _conv_taps(a, halo, cw, cb):
    sub = _iota((a.shape[0], 1), 0)
    a1 = jnp.where(sub == 0, _pick_row(halo, 7), pltpu.roll(a, 1, 0))
    a2 = jnp.where(sub == 0, _pick_row(halo, 6), jnp.where(sub == 1, _pick_row(halo, 7), pltpu.roll(a, 2, 0)))
    acc = cb + a2 * cw[0]
    acc = acc + a1 * cw[1]
    acc = acc + a * cw[2]
    return a1, a2, acc


def _ffn_down_loss(g_act, w_down, h1, final_g, target, acc_saved, up):
    L = h1.shape[0]
    tm = _row_tile(L)
    nb = tm // BLK
    half_w = D_FF // 2

    def body(g_ref, wd_ref, h1_ref, gf_ref, acc_ref, b_ref, *refs):
        t_refs, (dh_ref, dhb_ref, dgf_ref, loss_ref, dacc_ref, db_ref) = refs[:nb], refs[nb:]
        i = pl.program_id(0)

        @pl.when(i == 0)
        def _():
            dgf_ref[...] = jnp.zeros_like(dgf_ref)
            loss_ref[...] = jnp.zeros_like(loss_ref)

        h2 = h1_ref[...] + _dot(g_ref[...], wd_ref[...])
        r = lax.rsqrt(jnp.mean(h2 * h2, axis=-1, keepdims=True) + EPS)
        yn = h2 * r
        gf = gf_ref[...]
        live = i * tm + _iota((tm, 1), 0) >= PREFIX
        target = jnp.concatenate([t[...] for t in t_refs], axis=0)
        err = jnp.where(live, yn * gf - target, 0.0)
        loss_ref[...] = loss_ref[...] + 0.5 * jnp.sum(jnp.mean(err * err, axis=-1, keepdims=True))
        dy = err * (1.0 / D_MODEL)
        dgf_ref[...] = dgf_ref[...] + jnp.sum(dy * yn, axis=0, keepdims=True)
        dyn = dy * gf
        dh = r * (dyn - yn * jnp.mean(dyn * yn, axis=-1, keepdims=True))
        dh_ref[...] = dh
        dhb = dh.astype(BF16)
        dhb_ref[...] = dhb
        for half in range(2):
            cols = slice(half * half_w, (half + 1) * half_w)
            acc = acc_ref[:, cols].astype(F32)
            dg = _dot_nt(dhb, wd_ref[cols, :])
            sg = jax.nn.sigmoid(acc)
            silu = acc * sg
            db_ref[:, cols] = (dg * silu).astype(BF16)
            dacc_ref[:, cols] = (dg * b_ref[:, cols].astype(F32) * (sg + silu * (1.0 - sg))).astype(BF16)

    rows = lambda w, c=0: pl.BlockSpec((tm, w), lambda i: (i, c))
    return pl.pallas_call(
        body, name="f_ffn_down_loss", grid=(L // tm,),
        in_specs=[rows(D_FF), pl.BlockSpec((D_FF, D_MODEL), lambda i: (0, 0), pipeline_mode=pl.Buffered(1)),
                  rows(D_MODEL), _full((1, D_MODEL)), rows(D_FF), rows(D_FF, 1)] + _shifted_blocks(tm),
        out_specs=[rows(D_MODEL), rows(D_MODEL), _full((1, D_MODEL)), _full((1, BLK)), rows(D_FF), rows(D_FF)],
        out_shape=[jax.ShapeDtypeStruct((L, D_MODEL), F32), jax.ShapeDtypeStruct((L, D_MODEL), BF16),
                   jax.ShapeDtypeStruct((1, D_MODEL), F32), jax.ShapeDtypeStruct((1, BLK), F32),
                   jax.ShapeDtypeStruct((L, D_FF), BF16), jax.ShapeDtypeStruct((L, D_FF), BF16)],
        compiler_params=_params(("arbitrary",)),
    )(g_act, w_down, h1, final_g, acc_saved, up, *([target] * nb))


def _ffn_bwd_up(dacc, db, up, conv_w, w_up, h1, ffn_g, dh2, w_out):
    L = h1.shape[0]
    tm = _row_tile(L)
    nt = L // tm
    shard = w_up.shape[2]
    cw = [conv_w[j:j + 1] for j in range(3)]

    def body(da_ref, halo_ref, db_ref, a_ref, cw0, cw1, cw2, wu_ref, h1_ref, g_ref, dh2_ref, wo_ref,
             dup_ref, dh1_ref, dh1b_ref, dmix_ref, dg_ref, dcw_ref):
        i = pl.program_id(0)

        @pl.when(i == 0)
        def _():
            dg_ref[...] = jnp.zeros_like(dg_ref)
            dcw_ref[...] = jnp.zeros_like(dcw_ref)

        sub = _iota((tm, 1), 0)
        sub8 = _iota((8, 1), 0)
        last_tile = i == nt - 1
        dbv = db_ref[...]
        dup_ref[:, D_FF:2 * D_FF] = dbv
        dn = _dot_nt(dbv[:, 0:shard], wu_ref[2]) + _dot_nt(dbv[:, shard:2 * shard], wu_ref[3])
        for half in range(2):
            cols = slice(half * shard, (half + 1) * shard)
            d0 = da_ref[:, cols].astype(F32)
            halo = jnp.where(last_tile, 0.0, halo_ref[:, cols].astype(F32))
            d1 = jnp.where(sub == tm - 1, _pick_row(halo, 0), pltpu.roll(d0, tm - 1, 0))
            d2 = jnp.where(sub == tm - 2, _pick_row(halo, 0),
                           jnp.where(sub == tm - 1, _pick_row(halo, 1), pltpu.roll(d0, tm - 2, 0)))
            a = a_ref[:, cols].astype(F32)
            upd = jnp.zeros((8, shard), F32)
            for j, t in enumerate((d2 * a, d1 * a, d0 * a, d0)):
                upd = upd + jnp.where(sub8 == j, jnp.sum(t, axis=0, keepdims=True), 0.0)
            dcw_ref[:, cols] = dcw_ref[:, cols] + upd
            da = (d0 * cw2[:, cols] + d1 * cw1[:, cols] + d2 * cw0[:, cols]).astype(BF16)
            dup_ref[:, cols] = da
            dn = dn + _dot_nt(da, wu_ref[half])
        h1 = h1_ref[...]
        r = lax.rsqrt(jnp.mean(h1 * h1, axis=-1, keepdims=True) + EPS)
        yn = h1 * r
        dg_ref[...] = dg_ref[...] + jnp.sum(dn * yn, axis=0, keepdims=True)
        dyn = dn * g_ref[...]
        dh1 = dh2_ref[...] + r * (dyn - yn * jnp.mean(dyn * yn, axis=-1, keepdims=True))
        dh1_ref[...] = dh1
        dh1b = dh1.astype(BF16)
        dh1b_ref[...] = dh1b
        dmix_ref[...] = _dot_nt(dh1b, wo_ref[...]).astype(BF16)

    rows = lambda w: pl.BlockSpec((tm, w), lambda i: (i, 0))
    halo = pl.BlockSpec((8, D_FF), lambda i: (jnp.minimum((i + 1) * (tm // 8), L // 8 - 1), 0))
    return pl.pallas_call(
        body, name="b_ffn_up", grid=(nt,),
        in_specs=[rows(D_FF), halo, rows(D_FF), rows(D_FF), _full((1, D_FF)), _full((1, D_FF)), _full((1, D_FF)),
                  _full((N_CHIPS, D_MODEL, shard)), rows(D_MODEL), _full((1, D_MODEL)), rows(D_MODEL),
                  _full((D_MODEL, D_MODEL))],
        out_specs=[rows(2 * D_FF), rows(D_MODEL), rows(D_MODEL), rows(D_MODEL), _full((1, D_MODEL)),
                   _full((8, D_FF))],
        out_shape=[jax.ShapeDtypeStruct((L, 2 * D_FF), BF16), jax.ShapeDtypeStruct((L, D_MODEL), F32),
                   jax.ShapeDtypeStruct((L, D_MODEL), BF16), jax.ShapeDtypeStruct((L, D_MODEL), BF16),
                   jax.ShapeDtypeStruct((1, D_MODEL), F32), jax.ShapeDtypeStruct((8, D_FF), F32)],
        compiler_params=_params(("arbitrary",)),
    )(dacc, dacc, db, up, cw[0], cw[1], cw[2], w_up, h1, ffn_g, dh2, w_out)


def _wgrad(a, b, name, tn=None, tk=None):
    L, K = a.shape
    N = b.shape[1]
    tn = N if tn is None else tn
    tk = K if tk is None else tk
    tl = _row_tile(L, (1408, 768, 512, 256, 128))

    def body(a_ref, b_ref, o_ref):
        @pl.when(pl.program_id(2) == 0)
        def _():
            o_ref[...] = jnp.zeros_like(o_ref)

        o_ref[0] = o_ref[0] + _dot_tn(a_ref[...], b_ref[...])

    return pl.pallas_call(
        body, name=name, grid=(N // tn, K // tk, L // tl),
        in_specs=[pl.BlockSpec((tl, tk), lambda n, k, l: (l, k)), pl.BlockSpec((tl, tn), lambda n, k, l: (l, n))],
        out_specs=pl.BlockSpec((1, tk, tn), lambda n, k, l: (n, k, 0)),
        out_shape=jax.ShapeDtypeStruct((N // tn, K, tn), F32),
        compiler_params=_params(("parallel", "parallel", "arbitrary")),
    )(a, b)


def _retention_bwd(dmix, o, proj, cos_t, sin_t, ret_g, states, exchange=()):
    L = proj.shape[0]
    nblk = L // BLK
    G = _block_group(nblk)
    steps = nblk // G
    nx = len(exchange)
    dmat, wq_t, wk_t, g_blk = _decay_tables()

    def body(dm_ref, o_ref, q_ref, k_ref, v_ref, gate_ref, cos_ref, sin_ref, d_ref, wq_ref, wk_ref, rg_ref, rs_ref,
             *rest):
        x_in, (dp_ref, drg_ref), x_out, gstate = rest[:nx], rest[nx:nx + 2], rest[nx + 2:2 * nx + 2], rest[2 * nx + 2]

        @pl.when(pl.program_id(0) == 0)
        def _():
            if nx:
                for cp in _sibling_half_copies(x_in, x_out, *rest[2 * nx + 3:])[0]:
                    cp.start()
            gstate[...] = jnp.zeros_like(gstate)
            drg_ref[...] = jnp.zeros_like(drg_ref)

        lane = _iota((BLK, BLK), 1)
        sub = _iota((BLK, BLK), 0)
        scale = HEAD_LANES ** -0.5
        for b in reversed(range(G)):
            rows = slice(b * BLK, (b + 1) * BLK)
            rot, rot_t = _rot_fns(cos_ref[rows, :], sin_ref[rows, :])
            for p in range(2):
                qr = rot(q_ref[rows, p * BLK:(p + 1) * BLK].astype(F32))
                kr = rot(k_ref[rows, p * BLK:(p + 1) * BLK].astype(F32)) * scale
                kr_b = kr.astype(BF16)
                qw = (qr * wq_ref[p]).astype(BF16)
                kw = (kr * wk_ref[p]).astype(BF16)
                dqr = jnp.zeros((BLK, BLK), F32)
                dkr = jnp.zeros((BLK, BLK), F32)
                for e in range(2):
                    h = 2 * p + e
                    cols = slice(h * BLK, (h + 1) * BLK)
                    head_lanes = (lane >> 6) == e
                    o = o_ref[rows, cols]
                    rn = lax.rsqrt(jnp.mean(o * o, axis=-1, keepdims=True) + EPS)
                    y = o * rn
                    gate = gate_ref[rows, cols].astype(F32)
                    sg = jax.nn.sigmoid(gate)
                    dm = dm_ref[rows, cols].astype(F32)
                    rgain = rg_ref[:, cols]
                    drg_ref[:, cols] = drg_ref[:, cols] + jnp.sum(dm * y * (gate * sg), axis=0, keepdims=True)
                    dp_ref[rows, 1024 + h * BLK:1024 + (h + 1) * BLK] = (
                        dm * y * rgain * (sg * (1.0 + gate * (1.0 - sg)))).astype(BF16)
                    dy = dm * rgain * (gate * sg)
                    do = (rn * (dy - y * jnp.mean(dy * y, axis=-1, keepdims=True))).astype(BF16)
                    vh = v_ref[rows, cols]
                    qm = jnp.where(head_lanes, qr, 0.0).astype(BF16)
                    dmh = d_ref[h]
                    s = (_dot_nt(qm, kr_b) * dmh).astype(BF16)
                    ds = (_dot_nt(do, vh) * dmh).astype(BF16)
                    st = rs_ref[b, h].astype(BF16)
                    gs = gstate[h]
                    gs_b = gs.astype(BF16)
                    dqr = dqr + jnp.where(head_lanes, _dot(ds, kr_b), 0.0) + _dot_nt(do, st) * wq_ref[p]
                    dkr = dkr + _dot_tn(ds, qm) + _dot_nt(vh, gs_b) * wk_ref[p]
                    dp_ref[rows, 512 + h * BLK:512 + (h + 1) * BLK] = (_dot_tn(s, do) + _dot(kw, gs_b)).astype(BF16)
                    dr = jnp.where((sub >> 6) == e, _dot_tn(qw, do), 0.0)
                    gstate[h] = dr + g_blk[h] * gs
                dp_ref[rows, p * BLK:(p + 1) * BLK] = rot_t(dqr).astype(BF16)
                dp_ref[rows, 256 + p * BLK:256 + (p + 1) * BLK] = (rot_t(dkr) * scale).astype(BF16)

        if nx:
            @pl.when(pl.program_id(0) == steps - 1)
            def _():
                sends, recvs = _sibling_half_copies(x_in, x_out, *rest[2 * nx + 3:])
                for cp in recvs:
                    cp.wait_recv()
                for cp in sends:
                    cp.wait_send()

    row = lambda c: (lambda i: (steps - 1 - i, c))
    return pl.pallas_call(
        body, name="b_retention", grid=(steps,),
        in_specs=[pl.BlockSpec((G * BLK, 512), row(0)), pl.BlockSpec((G * BLK, 512), row(0)),
                  pl.BlockSpec((G * BLK, 256), row(0)), pl.BlockSpec((G * BLK, 256), row(1)),
                  pl.BlockSpec((G * BLK, 512), row(1)), pl.BlockSpec((G * BLK, 512), row(2)),
                  pl.BlockSpec((G * BLK, BLK), row(0)), pl.BlockSpec((G * BLK, BLK), row(0)),
                  _full((RET_HEADS, BLK, BLK)), _full((2, BLK, BLK)), _full((2, BLK, BLK)), _full((1, 512)),
                  pl.BlockSpec((G, RET_HEADS, BLK, BLK), lambda i: (steps - 1 - i, 0, 0, 0))] + [_ANY] * nx,
        out_specs=[pl.BlockSpec((G * BLK, RET_W), row(0)), _full((1, 512))] + [_ANY] * nx,
        out_shape=[jax.ShapeDtypeStruct((L, RET_W), BF16), jax.ShapeDtypeStruct((1, 512), F32)]
        + _sibling_half_shapes(exchange),
        scratch_shapes=[pltpu.VMEM((RET_HEADS, BLK, BLK), F32)] + _sibling_half_semaphores(nx),
        compiler_params=_params(("arbitrary",)),
    )(dmix, o, proj, proj, proj, proj, cos_t, sin_t, dmat, wq_t, wk_t, ret_g, states, *exchange)


def _fox_delta(dmix, o_f):
    L = o_f.shape[0]
    nblk = L // BLK
    G = _block_group(nblk)

    def body(do_ref, o_ref, d_ref):
        sel = ((_iota((8, 512), 1) >> 6) == _iota((8, 512), 0)).astype(BF16)
        for b in range(G):
            rows = slice(b * BLK, (b + 1) * BLK)
            prod = do_ref[rows, :].astype(F32) * o_ref[rows, :].astype(F32)
            hi = prod.astype(BF16)
            lo = (prod - hi.astype(F32)).astype(BF16)
            d_ref[b] = _dot_nt(sel, hi) + _dot_nt(sel, lo)

    return pl.pallas_call(
        body, name="b_foxdelta", grid=(nblk // G,),
        in_specs=[pl.BlockSpec((G * BLK, 512), lambda i: (i, 1)), pl.BlockSpec((G * BLK, 512), lambda i: (i, 0))],
        out_specs=pl.BlockSpec((G, 8, BLK), lambda i: (i, 0, 0)),
        out_shape=jax.ShapeDtypeStruct((nblk, 8, BLK), F32),
        compiler_params=_params(("parallel",)),
    )(dmix, o_f)


def _fox_bwd(proj, dmix, c, ctb, lse, delta, scatter=()):
    L = proj.shape[0]
    nblk, nu = _fox_units(L)
    scale = HEAD_LANES ** -0.5
    ns = len(scatter)

    steps = FOX_HEADS // (2 * FOX_PAIRS)

    def body(qkv_ref, do_ref, c_ref, ct_ref, lse_ref, dl_ref, *rest):
        s_in, (dp_ref, dc_ref, dcq_ref), s_out = rest[:ns], rest[ns:ns + 3], rest[ns + 3:2 * ns + 3]
        ktt, dqt, dk_acc, dv_acc, dcs_acc = rest[2 * ns + 3:2 * ns + 8]
        p = pl.program_id(0)
        heads = [(pp, e, 2 * FOX_PAIRS * p + 2 * pp + e) for pp in range(FOX_PAIRS) for e in range(2)]

        @pl.when(p == 0)
        def _():
            dc_ref[...] = jnp.zeros_like(dc_ref)
            dcq_ref[...] = jnp.zeros_like(dcq_ref)
            if ns:
                for cp in _scatter_copies(s_in, s_out, *rest[2 * ns + 8:]):
                    cp.start()

        sub8 = _iota((8, BLK), 0)
        masks = _fox_tile_masks()

        def pre(j, carry):
            off = pl.multiple_of(j * BLK, BLK)
            for pp in range(FOX_PAIRS):
                ktt[pp, j] = qkv_ref[pl.ds(off, BLK), pp * 384 + BLK:pp * 384 + 2 * BLK].astype(F32).T.astype(BF16)
                dqt[pp, j] = jnp.zeros((BLK, BLK), F32)
            return carry

        lax.fori_loop(0, nblk, pre, 0)

        def kv_pass(kblk, nk, n_later):
            klen = nk * BLK
            koff = pl.multiple_of(kblk * BLK, BLK)
            kt = [qkv_ref[pl.ds(koff, klen), pp * 384 + BLK:pp * 384 + 2 * BLK] for pp in range(FOX_PAIRS)]
            vtile = [qkv_ref[pl.ds(koff, klen), pp * 384 + 2 * BLK:pp * 384 + 3 * BLK] for pp in range(FOX_PAIRS)]
            ct = c_ref[pl.ds(koff, klen), :]
            klane = _iota((klen, BLK), 1)
            cs = [jnp.broadcast_to(jnp.sum(jnp.where(klane == h, ct, 0.0), axis=1, keepdims=True), (klen, WIDE * UNIT))
                  for _, _, h in heads]
            k_t = [jnp.concatenate([ktt[pp, kblk + b, e * HEAD_LANES:(e + 1) * HEAD_LANES, :] for b in range(nk)], axis=1)
                   for pp, e, _ in heads]
            for pp in range(FOX_PAIRS):
                dk_acc[pp, 0:klen] = jnp.zeros((klen, BLK), F32)
                dv_acc[pp, 0:klen] = jnp.zeros((klen, BLK), F32)
            for hh in range(len(heads)):
                dcs_acc[hh, 0:klen] = jnp.zeros((klen, BLK), F32)

            def tile(qblk, nq, mask):
                qlen = nq * BLK
                if mask == "valid":
                    mask = _iota((klen, qlen), 0) >= N_PAD
                qoff = pl.multiple_of(qblk * BLK, BLK)
                qlane = _iota((qlen, BLK), 1)
                qs = [qkv_ref[pl.ds(qoff, qlen), pp * 384:pp * 384 + BLK].astype(F32) * (scale * LOG2E)
                      for pp in range(FOX_PAIRS)]
                dot_ = [do_ref[pl.ds(qoff, qlen), pp * BLK:(pp + 1) * BLK] for pp in range(FOX_PAIRS)]
                stats = [[ref[qblk + a] for a in range(nq)] for ref in (ct_ref, lse_ref, dl_ref)]
                dcq = [jnp.zeros((8, BLK), F32) for _ in range(nq)]
                for hh, (pp, e, h) in enumerate(heads):
                    head = (qlane >> 6) == e
                    ct_row, lse_row, dl_row = [jnp.concatenate([_pick_row(t, h) for t in ts], axis=1) for ts in stats]
                    qm = jnp.where(head, qs[pp], 0.0).astype(BF16)
                    dom = jnp.where(head, dot_[pp], jnp.zeros_like(dot_[pp]))
                    t = _dot_nt(kt[pp], qm) - cs[hh][:, 0:qlen]
                    if mask is not None:
                        t = jnp.where(mask, t, NEG)
                    pr = jnp.exp2(t + (ct_row - lse_row))
                    dv_acc[pp, 0:klen] = dv_acc[pp, 0:klen] + _dot(pr.astype(BF16), dom)
                    dsv = pr * (_dot_nt(vtile[pp], dom) - dl_row)
                    ds_b = dsv.astype(BF16)
                    dk_acc[pp, 0:klen] = dk_acc[pp, 0:klen] + _dot(ds_b, qm)
                    rows = slice(e * HEAD_LANES, (e + 1) * HEAD_LANES)
                    dq_t = _dot(k_t[hh], ds_b)
                    key_side = dsv[:, 0:BLK]
                    for a in range(1, nq):
                        key_side = key_side + dsv[:, a * BLK:(a + 1) * BLK]
                    dcs_acc[hh, 0:klen] = dcs_acc[hh, 0:klen] + key_side
                    query_side = jnp.sum(dsv, axis=0, keepdims=True)
                    for a in range(nq):
                        cols = slice(a * BLK, (a + 1) * BLK)
                        dqt[pp, qblk + a, rows, :] = dqt[pp, qblk + a, rows, :] + dq_t[:, cols]
                        dcq[a] = dcq[a] + jnp.where(sub8 == h, query_side[:, cols], 0.0)
                for a in range(nq):
                    dcq_ref[qblk + a] = dcq_ref[qblk + a] + dcq[a]

            later_mask = "valid" if nk == 1 else None
            n_later = jnp.asarray(n_later, jnp.int32)
            n_wide = n_later // WIDE

            def later_wide(i, carry):
                tile(kblk + nk + 2 * WIDE * i, 2 * WIDE, later_mask)
                return carry

            tile(kblk, nk, masks["first"] if nk == 1 else masks["diag"])
            lax.fori_loop(0, n_wide, later_wide, 0)
            rest_blk = kblk + nk + 2 * WIDE * n_wide

            @pl.when((n_later & 2) != 0)
            def _():
                tile(rest_blk, 4, later_mask)

            @pl.when((n_later & 1) != 0)
            def _():
                tile(rest_blk + 2 * (n_later & 2), 2, later_mask)

            upd = jnp.zeros((klen, BLK), F32)
            for hh, (_, _, h) in enumerate(heads):
                upd = upd + jnp.where(klane == h, -jnp.sum(dcs_acc[hh, 0:klen], axis=1, keepdims=True), 0.0)
            dc_ref[pl.ds(koff, klen), :] = dc_ref[pl.ds(koff, klen), :] + upd
            for pp in range(FOX_PAIRS):
                dp_ref[pl.ds(koff, klen), pp * 384 + BLK:pp * 384 + 2 * BLK] = (
                    dk_acc[pp, 0:klen] * (1.0 / LOG2E)).astype(BF16)
                dp_ref[pl.ds(koff, klen), pp * 384 + 2 * BLK:pp * 384 + 3 * BLK] = dv_acc[pp, 0:klen].astype(BF16)

        kv_pass(0, 1, nu)

        def k_loop(u, carry):
            kv_pass(1 + 2 * u, 2, nu - 1 - u)
            return carry

        lax.fori_loop(0, nu, k_loop, 0)

        def flush(j, carry):
            off = pl.multiple_of(j * BLK, BLK)
            for pp in range(FOX_PAIRS):
                dp_ref[pl.ds(off, BLK), pp * 384:pp * 384 + BLK] = (dqt[pp, j].T * scale).astype(BF16)
            return carry

        lax.fori_loop(0, nblk, flush, 0)

        if ns:
            @pl.when(p == steps - 1)
            def _():
                copies = _scatter_copies(s_in, s_out, *rest[2 * ns + 8:])
                for cp in copies:
                    cp.wait_recv()
                for cp in copies:
                    cp.wait_send()

    width = 384 * FOX_PAIRS
    once = lambda shape, index: pl.BlockSpec(shape, index, pipeline_mode=pl.Buffered(1))
    stat = once((nblk, 8, BLK), lambda p: (0, 0, 0))
    return pl.pallas_call(
        body, name="b_fox", grid=(steps,),
        in_specs=[once((L, width), lambda p: (0, RET_W // width + p)),
                  once((L, FOX_PAIRS * BLK), lambda p: (0, 4 // FOX_PAIRS + p)),
                  once((L, BLK), lambda p: (0, 0)), stat, stat, stat] + [_ANY] * ns,
        out_specs=[pl.BlockSpec((L, width), lambda p: (0, p)), _full((L, BLK)), _full((nblk, 8, BLK))] + [_ANY] * ns,
        out_shape=[jax.ShapeDtypeStruct((L, FOX_W), BF16), jax.ShapeDtypeStruct((L, BLK), F32),
                   jax.ShapeDtypeStruct((nblk, 8, BLK), F32)] + _scatter_shapes(scatter),
        scratch_shapes=[pltpu.VMEM((FOX_PAIRS, nblk, BLK, BLK), BF16), pltpu.VMEM((FOX_PAIRS, nblk, BLK, BLK), F32),
                        pltpu.VMEM((FOX_PAIRS, UNIT, BLK), F32), pltpu.VMEM((FOX_PAIRS, UNIT, BLK), F32),
                        pltpu.VMEM((2 * FOX_PAIRS, UNIT, BLK), F32)]
        + _scatter_semaphores(ns),
        compiler_params=_params(("arbitrary",)),
    )(proj, dmix, c, ctb, lse, delta, *scatter)


def _fox_post(dc, dcq, ff, fb):
    L = dc.shape[0]
    nblk = L // BLK
    G = _block_group(nblk)
    steps = nblk // G

    def body(dc_ref, dcq_ref, ff_ref, b_ref, dff_ref, dffb_ref, dfb_ref, carry):
        @pl.when(pl.program_id(0) == 0)
        def _():
            carry[...] = jnp.zeros_like(carry)
            dfb_ref[...] = jnp.zeros_like(dfb_ref)

        tri = (_iota((BLK, BLK), 0) <= _iota((BLK, BLK), 1)).astype(BF16)
        live = _iota((BLK, BLK), 1) < FOX_HEADS
        run, dfb = carry[...], dfb_ref[...]
        for b in reversed(range(G)):
            rows = slice(b * BLK, (b + 1) * BLK)
            d = dc_ref[rows, :] + jnp.concatenate([dcq_ref[b], jnp.zeros((BLK - 8, BLK), F32)], axis=0).T
            hi, mid, lo = _split3(d)
            dlf = _dot(tri, hi) + _dot(tri, mid) + _dot(tri, lo) + run
            run = run + jnp.sum(d, axis=0, keepdims=True)
            z = ff_ref[rows, :] + b_ref[...]
            dff = jnp.where(live, dlf * jax.nn.sigmoid(-z), 0.0)
            dff_ref[rows, :] = dff
            dffb_ref[rows, :] = dff.astype(BF16)
            dfb = dfb + jnp.sum(dff, axis=0, keepdims=True)
        carry[...] = run
        dfb_ref[...] = dfb

    rev = lambda i: (steps - 1 - i, 0)
    return pl.pallas_call(
        body, name="b_foxpost", grid=(steps,),
        in_specs=[pl.BlockSpec((G * BLK, BLK), rev), pl.BlockSpec((G, 8, BLK), lambda i: (steps - 1 - i, 0, 0)),
                  pl.BlockSpec((G * BLK, BLK), rev), _full((1, BLK))],
        out_specs=[pl.BlockSpec((G * BLK, BLK), rev), pl.BlockSpec((G * BLK, BLK), rev), _full((1, BLK))],
        out_shape=[jax.ShapeDtypeStruct((L, BLK), F32), jax.ShapeDtypeStruct((L, BLK), BF16),
                   jax.ShapeDtypeStruct((1, BLK), F32)],
        scratch_shapes=[pltpu.VMEM((1, BLK), F32)],
        compiler_params=_params(("arbitrary",)),
    )(dc, dcq, ff, fb)


def _inproj_bwd(dpr, dpf, dffb, w_main, w_ff, h0, g, dh1, scatter=()):
    L = h0.shape[0]
    S = L - BLK
    tm = _row_tile(S, (512, 256, 128))
    nt = S // tm
    ns = len(scatter)
    operands = (dpr, dpf, dffb, h0, dh1)

    def body(*refs):
        lead, tile = refs[0:5], refs[5:10]
        wm_ref, wf_ref, g_ref = refs[10:13]
        rest = refs[13:]
        s_in, (dlead_ref, dx_ref, dg_ref), s_out = rest[:ns], rest[ns:ns + 3], rest[ns + 3:2 * ns + 3]
        i = pl.program_id(0)

        def rows_bwd(dpr_ref, dpf_ref, dff_ref, h_ref, dh1_ref):
            dn = (_dot_nt(dpr_ref[...], wm_ref[:, 0:RET_W]) + _dot_nt(dpf_ref[...], wm_ref[:, RET_W:MAIN_W])
                  + _dot_nt(dff_ref[...], wf_ref[...]))
            h = h_ref[...]
            r = lax.rsqrt(jnp.mean(h * h, axis=-1, keepdims=True) + EPS)
            yn = h * r
            dyn = dn * g_ref[...]
            dh0 = dh1_ref[...] + r * (dyn - yn * jnp.mean(dyn * yn, axis=-1, keepdims=True))
            return dh0, jnp.sum(dn * yn, axis=0, keepdims=True)

        @pl.when(i == 0)
        def _():
            if ns:
                for cp in _scatter_copies(s_in, s_out, *rest[2 * ns + 3:]):
                    cp.start()
            dlead_ref[...], dg_ref[...] = rows_bwd(*lead)

        dx_ref[...], dg_tile = rows_bwd(*tile)
        dg_ref[...] = dg_ref[...] + dg_tile

        if ns:
            @pl.when(i == nt - 1)
            def _():
                copies = _scatter_copies(s_in, s_out, *rest[2 * ns + 3:])
                for cp in copies:
                    cp.wait_recv()
                for cp in copies:
                    cp.wait_send()

    lead_spec = lambda a: pl.BlockSpec((BLK, a.shape[1]), lambda i: (0, 0))
    tile_spec = lambda a: pl.BlockSpec((pl.Element(tm), pl.Element(a.shape[1])),
                                       lambda i: (pl.multiple_of(BLK + i * tm, BLK), 0))
    return pl.pallas_call(
        body, name="b_inproj", grid=(nt,),
        in_specs=[lead_spec(a) for a in operands] + [tile_spec(a) for a in operands]
        + [_full((D_MODEL, MAIN_W)), _full((D_MODEL, BLK)), _full((1, D_MODEL))] + [_ANY] * ns,
        out_specs=[_full((BLK, D_MODEL)), pl.BlockSpec((tm, D_MODEL), lambda i: (i, 0)), _full((1, D_MODEL))]
        + [_ANY] * ns,
        out_shape=[jax.ShapeDtypeStruct((BLK, D_MODEL), F32), jax.ShapeDtypeStruct((S, D_MODEL), F32),
                   jax.ShapeDtypeStruct((1, D_MODEL), F32)] + _scatter_shapes(scatter),
        scratch_shapes=_scatter_semaphores(ns),
        compiler_params=_params(("arbitrary",)),
    )(*operands, *operands, w_main, w_ff, g, *scatter)


def _local_step(x, target, meta, attn_g, w_main, w_ff, fox_b, ret_g, w_out, ffn_g, w_up, conv_w, conv_b, w_down, final_g,
                late=None, mid=None, last=None):
    S = x.shape[0]
    L = S + PREFIX
    head = jnp.concatenate([jnp.zeros((N_PAD, D_MODEL), F32), meta], axis=0)
    fb = jnp.pad(fox_b, ((0, 0), (0, BLK - FOX_HEADS)))
    cos_t, sin_t = _rotary_tables(L)

    h0, n1, proj, ff = _rms_inproj(head, x, attn_g, w_main, w_ff)
    c, ctb = _fox_prep(ff, fb)
    mix_r, o_ret, states = _retention_fwd(proj, cos_t, sin_t, ret_g)
    if late is None:
        o_f, lse = _fox_fwd(proj, c, ctb)
    else:
        o_f, lse, *gathered = _fox_fwd(proj, c, ctb, gather=late[0])
        w_out, w_up, w_down = late[1](gathered)
    h1, n2, up, g_act, acc_saved = _outproj_up(mix_r, o_f, h0, w_out, ffn_g, w_up, conv_w, conv_b)
    dh2, dh2b, d_final_g, loss, dacc, db = _ffn_down_loss(g_act, w_down, h1, final_g, target, acc_saved, up)

    dup, dh1, dh1b, dmix, d_ffn_g, dconv = _ffn_bwd_up(dacc, db, up, conv_w, w_up, h1, ffn_g, dh2, w_out)
    d_w_down = _wgrad(g_act, dh2b, "wgrad_down", tk=D_FF // 2)[0]
    d_w_up = _wgrad(n2, dup, "wgrad_up", tn=w_up.shape[2])
    d_w_out = jnp.concatenate([_wgrad(mix_r, dh1b, "wgrad_out_r")[0], _wgrad(o_f, dh1b, "wgrad_out_f")[0]], axis=0)

    early = () if mid is None else mid[0](d_w_out, d_w_up, d_w_down)
    dpr, d_ret_g, *from_sibling = _retention_bwd(dmix, o_ret, proj, cos_t, sin_t, ret_g, states, exchange=early)
    delta = _fox_delta(dmix, o_f)
    scatter = () if mid is None else mid[1](early, from_sibling)
    dpf, dc, dcq, *received = _fox_bwd(proj, dmix, c, ctb, lse, delta, scatter=scatter)
    dff, dffb, d_fox_b = _fox_post(dc, dcq, ff, fb)
    d_w_main = jnp.concatenate([_wgrad(n1, dpr, "wgrad_in_r")[0], _wgrad(n1, dpf, "wgrad_in_f")[0]], axis=1)
    d_w_ff = _wgrad(n1, dffb, "wgrad_in_ff")[0][:, :FOX_HEADS]
    scatter_in = () if last is None else last(d_w_main, d_w_ff)
    dlead, dx, d_attn_g, *received_in = _inproj_bwd(dpr, dpf, dffb, w_main, w_ff, h0, attn_g, dh1, scatter=scatter_in)

    return dict(
        loss=loss[0, 0], dx=dx, dmeta=dlead[N_PAD:], attn_g=d_attn_g, w_main=d_w_main,
        w_ff=d_w_ff, fox_b=d_fox_b[:, :FOX_HEADS], ret_g=d_ret_g, w_out=d_w_out, ffn_g=d_ffn_g,
        w_up=d_w_up, conv_w=dconv[0:3], conv_b=dconv[3:4], w_down=d_w_down, final_g=d_final_g,
        scatter=list(scatter_in) + list(scatter), received=list(received_in) + list(received))


_ANY = pl.BlockSpec(memory_space=pl.ANY)


def _place():
    return lax.axis_index("x"), lax.axis_index("y"), lax.axis_index("c")


def _other_chips(x, y):
    return [(1 - x, y), (x, 1 - y), (1 - x, 1 - y)]


def _allgather_semaphores(n):
    if n == 0:
        return []
    return [pltpu.SemaphoreType.DMA((3 * n,)), pltpu.SemaphoreType.DMA((3 * n,)), pltpu.SemaphoreType.DMA((n,))]


def _allgather_copies(ins, outs, send, recv, loc):
    n = len(ins)
    x, y, c = _place()
    mine = 2 * x + y
    peers = _other_chips(x, y)

    def remote(a, k, slot):
        return pltpu.make_async_remote_copy(
            src_ref=ins[a], dst_ref=outs[a].at[slot], send_sem=send.at[3 * a + k], recv_sem=recv.at[3 * a + k],
            device_id=(peers[k][0], peers[k][1], c), device_id_type=MESH)

    local = [pltpu.make_async_copy(ins[a], outs[a].at[mine], loc.at[a]) for a in range(n)]
    sends = [remote(a, k, mine) for a in range(n) for k in range(3)]
    recvs = [remote(a, k, 2 * peers[k][0] + peers[k][1]) for a in range(n) for k in range(3)]
    return local, sends, recvs


def _chip_allgather_halves(w, small):
    half = w.shape[0] // 2

    def body(w_ref, s_ref, wo_ref, so_ref, send, recv, fsend, frecv, ssend, srecv, loc):
        x, y, c = _place()
        mine = 2 * x + y
        peers = _other_chips(x, y)

        def fetch(k, slot):
            return pltpu.make_async_remote_copy(
                src_ref=w_ref.at[pl.ds(c * half, half)], dst_ref=wo_ref.at[slot, c], send_sem=send.at[k],
                recv_sem=recv.at[k], device_id=(peers[k][0], peers[k][1], c), device_id_type=MESH)

        def forward(k, which):
            slot = 2 * peers[k][0] + peers[k][1]
            return pltpu.make_async_remote_copy(
                src_ref=wo_ref.at[slot, which], dst_ref=wo_ref.at[slot, which], send_sem=fsend.at[k],
                recv_sem=frecv.at[k], device_id=(x, y, 1 - c), device_id_type=MESH)

        def small_copy(k, slot):
            return pltpu.make_async_remote_copy(
                src_ref=s_ref, dst_ref=so_ref.at[slot], send_sem=ssend.at[k], recv_sem=srecv.at[k],
                device_id=(peers[k][0], peers[k][1], c), device_id_type=MESH)

        local = pltpu.make_async_copy(s_ref, so_ref.at[mine], loc.at[0])
        sends = [fetch(k, mine) for k in range(3)] + [small_copy(k, mine) for k in range(3)]
        local.start()
        for cp in sends:
            cp.start()
        forwards = []
        for k in range(3):
            fetch(k, 2 * peers[k][0] + peers[k][1]).wait_recv()
            forwards.append(forward(k, c))
            forwards[-1].start()
        for k in range(3):
            forward(k, 1 - c).wait_recv()
            small_copy(k, 2 * peers[k][0] + peers[k][1]).wait_recv()
        for cp in sends + forwards:
            cp.wait_send()
        local.wait()

    three = pltpu.SemaphoreType.DMA((3,))
    return pl.pallas_call(
        body, name="ag_weights", in_specs=[_ANY] * 2, out_specs=[_ANY] * 2,
        out_shape=[jax.ShapeDtypeStruct((N_CHIPS, 2, half, w.shape[1]), w.dtype),
                   jax.ShapeDtypeStruct((N_CHIPS,) + small.shape, small.dtype)],
        scratch_shapes=[three, three, three, three, three, three, pltpu.SemaphoreType.DMA((1,))],
    )(w, small)


def _chip_allgather(arrays):
    n = len(arrays)

    def body(*refs):
        local, sends, recvs = _allgather_copies(refs[:n], refs[n:2 * n], *refs[2 * n:])
        for cp in local + sends:
            cp.start()
        for cp in recvs:
            cp.wait_recv()
        for cp in sends:
            cp.wait_send()
        for cp in local:
            cp.wait()

    return pl.pallas_call(
        body, name="ag_weights", in_specs=[_ANY] * n, out_specs=[_ANY] * n,
        out_shape=[jax.ShapeDtypeStruct((N_CHIPS,) + a.shape, a.dtype) for a in arrays],
        scratch_shapes=_allgather_semaphores(n),
    )(*arrays)


def _sibling_halves(grads):
    n = len(grads)

    def body(*refs):
        sends, recvs = _sibling_half_copies(refs[:n], refs[n:2 * n], *refs[2 * n:])
        for cp in sends:
            cp.start()
        for cp in recvs:
            cp.wait_recv()
        for cp in sends:
            cp.wait_send()

    return pl.pallas_call(
        body, name="rs_sibling", in_specs=[_ANY] * n, out_specs=[_ANY] * n,
        out_shape=_sibling_half_shapes(grads), scratch_shapes=_sibling_half_semaphores(n),
    )(*grads)


def _sibling_half_shapes(grads):
    return [jax.ShapeDtypeStruct((N_CHIPS, g.shape[1] // 2, g.shape[2]), g.dtype) for g in grads]


def _sibling_half_semaphores(n):
    return [pltpu.SemaphoreType.DMA((n,)), pltpu.SemaphoreType.DMA((n,))] if n else []


def _sibling_half_copies(ins, outs, send, recv):
    x, y, c = _place()

    def half_copy(a, which):
        half = ins[a].shape[1] // 2
        return pltpu.make_async_remote_copy(
            src_ref=ins[a].at[pl.ds(0, N_CHIPS), pl.ds(which * half, half)], dst_ref=outs[a],
            send_sem=send.at[a], recv_sem=recv.at[a], device_id=(x, y, 1 - c), device_id_type=MESH)

    return [half_copy(a, 1 - c) for a in range(len(ins))], [half_copy(a, c) for a in range(len(ins))]


def _scatter_shapes(parts):
    return [jax.ShapeDtypeStruct((3,) + p.shape[1:], p.dtype) for p in parts]


def _scatter_semaphores(n):
    return [pltpu.SemaphoreType.DMA((3 * n,)), pltpu.SemaphoreType.DMA((3 * n,))] if n else []


def _scatter_copies(ins, outs, send, recv):
    x, y, c = _place()
    peers = _other_chips(x, y)
    return [pltpu.make_async_remote_copy(
        src_ref=ins[a].at[2 * peers[k][0] + peers[k][1]], dst_ref=outs[a].at[k], send_sem=send.at[3 * a + k],
        recv_sem=recv.at[3 * a + k], device_id=(peers[k][0], peers[k][1], c), device_id_type=MESH)
        for a in range(len(ins)) for k in range(3)]


def _sibling_allgather(bufs, small):
    n = len(bufs)

    def body(*refs):
        small_in, outs, small_out = refs[n], refs[n + 1:2 * n + 1], refs[2 * n + 1]
        send, recv, s_send, s_recv, loc = refs[2 * n + 2:]
        x, y, c = _place()
        me = 4 * x + 2 * y + c

        def remote(a, which):
            return pltpu.make_async_remote_copy(
                src_ref=outs[a].at[which], dst_ref=outs[a].at[which], send_sem=send.at[a], recv_sem=recv.at[a],
                device_id=(x, y, 1 - c), device_id_type=MESH)

        def peer_of(r):
            return tuple(1 - v if (r >> b) & 1 else v for v, b in ((x, 2), (y, 1), (c, 0)))

        def small_copy(r, slot):
            return pltpu.make_async_remote_copy(
                src_ref=small_in, dst_ref=small_out.at[slot], send_sem=s_send.at[r - 1], recv_sem=s_recv.at[r - 1],
                device_id=peer_of(r), device_id_type=MESH)

        local = pltpu.make_async_copy(small_in, small_out.at[me], loc.at[0])
        sends = [remote(a, c) for a in range(n)] + [small_copy(r, me) for r in range(1, N_DEV)]
        local.start()
        for cp in sends:
            cp.start()
        for r in range(1, N_DEV):
            px, py, pc = peer_of(r)
            small_copy(r, 4 * px + 2 * py + pc).wait_recv()
        for a in range(n):
            remote(a, 1 - c).wait_recv()
        for cp in sends:
            cp.wait_send()
        local.wait()

    outs = pl.pallas_call(
        body, name="ag_sibling", in_specs=[_ANY] * (n + 1), out_specs=[_ANY] * (n + 1),
        out_shape=[jax.ShapeDtypeStruct(b.shape, b.dtype) for b in bufs]
        + [jax.ShapeDtypeStruct((N_DEV,) + small.shape, small.dtype)],
        input_output_aliases={a: a for a in range(n)},
        scratch_shapes=[pltpu.SemaphoreType.DMA((n,)), pltpu.SemaphoreType.DMA((n,)),
                        pltpu.SemaphoreType.DMA((N_DEV - 1,)), pltpu.SemaphoreType.DMA((N_DEV - 1,)),
                        pltpu.SemaphoreType.DMA((1,))],
    )(*bufs, small)
    return [o.reshape(2 * o.shape[1], o.shape[2]) for o in outs[:n]], outs[n]


def _pair_add(full, recv, core, name):
    _, R, C = full.shape
    half = R // 2

    def body(core_ref, a_ref, b_ref, o_ref):
        o_ref[...] = (a_ref[...] + b_ref[...]).astype(BF16)

    return pl.pallas_call(
        body, name=name,
        grid_spec=pltpu.PrefetchScalarGridSpec(
            num_scalar_prefetch=1, grid=(N_CHIPS,),
            in_specs=[pl.BlockSpec((1, half, C), lambda j, core_ref: (j, core_ref[0], 0)),
                      pl.BlockSpec((1, half, C), lambda j, core_ref: (j, 0, 0))],
            out_specs=pl.BlockSpec((1, half, C), lambda j, core_ref: (j, 0, 0))),
        out_shape=jax.ShapeDtypeStruct((N_CHIPS, half, C), BF16),
        compiler_params=_params(("parallel",)),
    )(core, full, recv)


def _sum_partials(own_all, recv, place, name, tiles=2):
    _, R, C = own_all.shape
    tr = R // tiles

    def body(place_ref, own_ref, r_ref, o_ref):
        acc = own_ref[0].astype(F32)
        for k in range(3):
            acc = acc + r_ref[k].astype(F32)
        o_ref[0] = acc

    return pl.pallas_call(
        body, name=name,
        grid_spec=pltpu.PrefetchScalarGridSpec(
            num_scalar_prefetch=1, grid=(tiles,),
            in_specs=[pl.BlockSpec((1, tr, C), lambda i, place_ref: (place_ref[0], i, 0)),
                      pl.BlockSpec((3, tr, C), lambda i, place_ref: (0, i, 0))],
            out_specs=pl.BlockSpec((1, tr, C), lambda i, place_ref: (place_ref[1], i, 0))),
        out_shape=jax.ShapeDtypeStruct((2, R, C), F32),
        compiler_params=_params(("parallel",)),
    )(place, own_all, recv)


def _adamw_math(w, g, m, v):
    m2 = ADAM_B1 * m + (1.0 - ADAM_B1) * g
    v2 = ADAM_B2 * v + (1.0 - ADAM_B2) * (g * g)
    m_hat = m2 / (1.0 - ADAM_B1 ** ADAM_STEP)
    v_hat = v2 / (1.0 - ADAM_B2 ** ADAM_STEP)
    return -ADAM_LR * (m_hat / (jnp.sqrt(v_hat) + ADAM_EPS) + ADAM_WD * w), m2, v2


ROW_ATTN_G, ROW_FFN_G, ROW_FINAL_G, ROW_MISC, ROW_CONV_B, ROW_CONV_W, ROW_META, SMALL_ROWS = 0, 1, 2, 3, 4, 8, 24, 40
MISC_FOX_B, MISC_LOSS = 512, 640


def _small_pack(out):
    def rows(a, n):
        a = a.astype(F32)
        return jnp.pad(a, ((0, n - a.shape[0]), (0, D_MODEL - a.shape[1])))

    misc = jnp.concatenate([out["ret_g"], out["fox_b"], jnp.zeros((1, MISC_LOSS - MISC_FOX_B - FOX_HEADS), F32),
                            out["loss"].reshape(1, 1)], axis=1)
    conv_b = jnp.pad(out["conv_b"], ((0, 0), (0, (-D_FF) % D_MODEL))).reshape(-1, D_MODEL)
    conv_w = out["conv_w"].reshape(3, N_CHIPS, -1).transpose(1, 0, 2).reshape(3 * N_CHIPS, -1)
    return jnp.concatenate([
        rows(out["attn_g"], 1), rows(out["ffn_g"], 1), rows(out["final_g"], 1), rows(misc, 1),
        rows(conv_b, ROW_CONV_W - ROW_CONV_B), rows(conv_w, ROW_META - ROW_CONV_W), rows(out["dmeta"], N_META)], axis=0)


def _small_update(packs, chip, ws, ms, vs):
    n = len(ws)
    meta_w, conv_sw = ws[0].shape[1], ws[5].shape[2]
    assert packs.shape == (N_DEV, SMALL_ROWS, D_MODEL) and ws[0].shape[0] == N_META and ws[5].shape[:2] == (1, 3)

    def body(chip_ref, p_ref, *refs):
        w_refs, m_refs, v_refs = refs[:n], refs[n:2 * n], refs[2 * n:3 * n]
        loss_ref, out_refs, tot = refs[3 * n], refs[3 * n + 1:7 * n + 1], refs[7 * n + 1]
        acc = p_ref[0]
        for d in range(1, N_DEV):
            acc = acc + p_ref[d]
        tot[...] = acc

        def of_chip(pieces):
            val = pieces[-1]
            for j in range(N_CHIPS - 2, -1, -1):
                val = jnp.where(chip_ref[0] == j, pieces[j], val)
            return val

        row = lambda r, lo=0, hi=D_MODEL: tot[r:r + 1, lo:hi]
        grads = [
            of_chip([tot[ROW_META:ROW_META + N_META, j * meta_w:(j + 1) * meta_w] for j in range(N_CHIPS)]),
            row(ROW_ATTN_G), row(ROW_MISC, MISC_FOX_B, MISC_FOX_B + FOX_HEADS), row(ROW_MISC, 0, MISC_FOX_B),
            row(ROW_FFN_G),
            of_chip([tot[ROW_CONV_W + 3 * j:ROW_CONV_W + 3 * j + 3, 0:conv_sw] for j in range(N_CHIPS)]),
            jnp.concatenate([row(ROW_CONV_B), row(ROW_CONV_B + 1), row(ROW_CONV_B + 2, 0, D_FF - 2 * D_MODEL)], axis=1),
            row(ROW_FINAL_G)]
        loss_ref[...] = row(ROW_MISC, MISC_LOSS, MISC_LOSS + BLK)
        for k in range(n):
            at = (0,) if len(ws[k].shape) == 3 else (Ellipsis,)
            res = (grads[k],) + _adamw_math(w_refs[k][at], grads[k], m_refs[k][at], v_refs[k][at])
            for kind in range(4):
                out_refs[kind * n + k][at] = res[kind]

    res = pl.pallas_call(
        body, name="small_update",
        grid_spec=pltpu.PrefetchScalarGridSpec(
            num_scalar_prefetch=1, grid=(1,),
            in_specs=[_full(packs.shape)] + [_full(a.shape) for a in list(ws) * 3],
            out_specs=[_full((1, BLK))] + [_full(a.shape) for a in list(ws) * 4],
            scratch_shapes=[pltpu.VMEM((SMALL_ROWS, D_MODEL), F32)]),
        out_shape=[jax.ShapeDtypeStruct((1, BLK), F32)] + [jax.ShapeDtypeStruct(a.shape, F32) for a in list(ws) * 4],
        compiler_params=_params(("arbitrary",)),
    )(chip, packs, *ws, *ms, *vs)
    return res[0], res[1:n + 1], res[n + 1:2 * n + 1], res[2 * n + 1:3 * n + 1], res[3 * n + 1:]


def _adamw(w, g, m, v, name, tiles=4):
    R, tail = w.shape[0], w.shape[1:]
    assert R % tiles == 0
    tr = R // tiles

    def body(w_ref, g_ref, m_ref, v_ref, go_ref, d_ref, m2_ref, v2_ref):
        g_ = g_ref[...]
        go_ref[...] = g_
        d_ref[...], m2_ref[...], v2_ref[...] = _adamw_math(w_ref[...], g_, m_ref[...], v_ref[...])

    spec = pl.BlockSpec((tr,) + tail, lambda i: (i,) + (0,) * len(tail))
    return pl.pallas_call(
        body, name=name, grid=(tiles,), in_specs=[spec] * 4, out_specs=[spec] * 4,
        out_shape=[jax.ShapeDtypeStruct(w.shape, F32)] * 4,
        compiler_params=_params(("parallel",)),
    )(w, g, m, v)


def _row_vector_tiles(n, most=80):
    return next(t for t in range(1, n + 1) if n % t == 0 and n // t <= most)


def _pack_rows(pieces, rows):
    flat = jnp.concatenate([jnp.pad(p.reshape(-1).astype(F32), (0, (-p.size) % D_MODEL)) for p in pieces])
    return jnp.pad(flat, (0, rows * D_MODEL - flat.size)).reshape(rows, D_MODEL)


def _unpack_rows(pack, shapes):
    flat = pack.reshape(-1)
    out, off = [], 0
    for shp in shapes:
        size = int(np.prod(shp))
        out.append(flat[off:off + size].reshape(shp))
        off += size + (-size) % D_MODEL
    return out


def _kernel_order(w):
    parts = [w[:, 0:RET_W]]
    for p in range(FOX_HEADS // 2):
        parts += [w[:, RET_W + part * 512 + p * BLK:RET_W + part * 512 + (p + 1) * BLK] for part in range(3)]
    return jnp.concatenate(parts, axis=1)


def _reference_order(g_main, g_ff):
    parts = [g_main[:, 0:RET_W]]
    for part in range(3):
        parts += [g_main[:, RET_W + 384 * p + part * BLK:RET_W + 384 * p + (part + 1) * BLK] for p in range(FOX_HEADS // 2)]
    return jnp.concatenate(parts + [g_ff], axis=1)


def kernel(x, meta_tokens, attn_norm_g, w_in, fox_forget_b, ret_norm_g, w_out, ffn_norm_g, w_up, conv_w, conv_b, w_down, final_norm_g, loss_target, m_meta_tokens, m_attn_norm_g, m_w_in, m_fox_forget_b, m_ret_norm_g, m_w_out, m_ffn_norm_g, m_w_up, m_conv_w, m_conv_b, m_w_down, m_final_norm_g, v_meta_tokens, v_attn_norm_g, v_w_in, v_fox_forget_b, v_ret_norm_g, v_w_out, v_ffn_norm_g, v_w_up, v_conv_w, v_conv_b, v_w_down, v_final_norm_g):
    chip = 2 * lax.axis_index("x") + lax.axis_index("y")
    core = lax.axis_index("c")

    small_w = _pack_rows([meta_tokens, conv_w[0]], 8)
    w_in_b = w_in[0].astype(BF16)
    g_in, g_small = _chip_allgather_halves(w_in_b, small_w)
    g_in = lax.dynamic_update_slice(g_in.reshape((N_CHIPS,) + w_in_b.shape), w_in_b[None], (chip, 0, 0))
    w_in_full = g_in.transpose(1, 0, 2).reshape(D_MODEL, IN_WIDTH)
    w_main = _kernel_order(w_in_full)
    w_ff = jnp.pad(w_in_full[:, MAIN_W:], ((0, 0), (0, BLK - FOX_HEADS)))
    small_parts = [_unpack_rows(g_small[j], [meta_tokens.shape, conv_w.shape[1:]]) for j in range(N_CHIPS)]
    meta_full = jnp.concatenate([sp[0] for sp in small_parts], axis=1)
    conv_w_full = jnp.concatenate([sp[1] for sp in small_parts], axis=1)

    core_idx = core.reshape(1).astype(jnp.int32)
    place = jnp.stack([chip, core]).astype(jnp.int32)

    def assemble(gathered):
        g_out, g_up, g_down = gathered
        return g_out.reshape(D_MODEL, D_MODEL), g_up, g_down.reshape(D_FF, D_MODEL)

    def early_arrays(d_w_out, d_w_up, d_w_down):
        return [d_w_out.reshape(N_CHIPS, -1, D_MODEL), d_w_up, d_w_down.reshape(N_CHIPS, -1, D_MODEL)]

    def in_sums(d_w_main, d_w_ff):
        g_in_full = _reference_order(d_w_main, d_w_ff).reshape(D_MODEL, N_CHIPS, -1).transpose(1, 0, 2)
        (from_sib,) = _sibling_halves([g_in_full])
        return [_pair_add(g_in_full, from_sib, core_idx, "pair_add_in")]

    def early_sums(early, from_sib):
        return [_pair_add(g, r, core_idx, "pair_add_" + nm) for g, r, nm in zip(early, from_sib, ("out", "up", "down"))]

    out = _local_step(x[0], loss_target[0], meta_full, attn_norm_g, w_main, w_ff, fox_forget_b, ret_norm_g,
                      None, ffn_norm_g, None, conv_w_full, conv_b, None, final_norm_g[None],
                      late=([w_out[0].astype(BF16), w_up[0].astype(BF16), w_down[0].astype(BF16)], assemble),
                      mid=(early_arrays, early_sums), last=in_sums)

    names = ("in", "out", "up", "down")
    totals = [_sum_partials(s, q, place, "sum_chips_" + nm) for s, q, nm in zip(out["scatter"], out["received"], names)]
    (grad_in, grad_out, grad_up, grad_down), small_all = _sibling_allgather(totals, _small_pack(out))

    big_w = [(w_out, m_w_out, v_w_out, grad_out, "adamw_out"), (w_up, m_w_up, v_w_up, grad_up, "adamw_up"),
             (w_down, m_w_down, v_w_down, grad_down, "adamw_down")]
    big_res = [[r[None] for r in _adamw(w[0], g, m[0], v[0], nm)] for w, m, v, g, nm in big_w]
    as_rows = lambda a: jnp.transpose(a, (2, 0, 1))
    in_rows = _adamw(as_rows(w_in), grad_in.T[:, None, :], as_rows(m_w_in), as_rows(v_w_in), "adamw_in",
                     tiles=_row_vector_tiles(w_in.shape[2]))
    big_res.insert(0, [jnp.transpose(r, (1, 2, 0)) for r in in_rows])
    small_p = [meta_tokens, attn_norm_g, fox_forget_b, ret_norm_g, ffn_norm_g, conv_w, conv_b, final_norm_g[None]]
    small_m = [m_meta_tokens, m_attn_norm_g, m_fox_forget_b, m_ret_norm_g, m_ffn_norm_g, m_conv_w, m_conv_b, m_final_norm_g[None]]
    small_v = [v_meta_tokens, v_attn_norm_g, v_fox_forget_b, v_ret_norm_g, v_ffn_norm_g, v_conv_w, v_conv_b, v_final_norm_g[None]]
    loss_row, *small_res = _small_update(small_all, chip.reshape(1).astype(jnp.int32), small_p, small_m, small_v)
    loss = loss_row[0, 0]

    def ordered(kind):
        sm = list(small_res[kind][:-1]) + [small_res[kind][-1][0]]
        bg = [r[kind] for r in big_res]
        return [sm[0], sm[1], bg[0], sm[2], sm[3], bg[1], sm[4], bg[2], sm[5], sm[6], bg[3], sm[7]]

    return (loss, out["dx"][None], *ordered(0), *ordered(1), *ordered(2), *ordered(3))
```

```python
import functools

import numpy as np
import jax
import jax.numpy as jnp
from jax import lax
from jax.experimental import pallas as pl
from jax.experimental.pallas import tpu as pltpu

F32 = jnp.float32
BF16 = jnp.bfloat16

D_MODEL = 1024
N_META = 16
BLK = 128
UNIT = 2 * BLK
FOX_PAIRS = 2
WIDE = 4
CHUNK = 64
N_PAD = BLK - N_META
PREFIX = BLK
RET_HEADS = 4
FOX_HEADS = 8
HEAD_LANES = 64
D_FF = 2816
ROPE_BASE = 10000.0
EPS = 1e-6
NEG = -1e30
LOG2E = 1.4426950408889634
RET_W = 1536
FOX_W = 1536
MAIN_W = RET_W + FOX_W
IN_WIDTH = MAIN_W + FOX_HEADS
N_CHIPS = 4
N_DEV = 8

ADAM_LR = 0.001
ADAM_B1 = 0.9
ADAM_B2 = 0.999
ADAM_EPS = 1e-08
ADAM_WD = 0.01
ADAM_STEP = 10

MESH = pl.DeviceIdType.MESH
VMEM_LIMIT_MB = 56

_NT = (((1,), (1,)), ((), ()))
_TN = (((0,), (0,)), ((), ()))


def _dot(a, b):
    return jnp.dot(a, b, preferred_element_type=F32)


def _dot_nt(a, b):
    return lax.dot_general(a, b, _NT, preferred_element_type=F32)


def _dot_tn(a, b):
    return lax.dot_general(a, b, _TN, preferred_element_type=F32)


def _params(dims=None, vmem_mb=VMEM_LIMIT_MB):
    kw = dict(vmem_limit_bytes=vmem_mb << 20)
    if dims is not None:
        kw["dimension_semantics"] = dims
    return pltpu.CompilerParams(**kw)


def _row_tile(n, prefs=(384, 256, 128)):
    for t in prefs:
        if n % t == 0:
            return t
    raise ValueError(f"no row tile for {n}")


def _iota(shape, dim):
    return lax.broadcasted_iota(jnp.int32, shape, dim)


def _pick_row(tile, row):
    sub = _iota(tile.shape, 0)
    return jnp.sum(jnp.where(sub == row, tile, 0.0), axis=0, keepdims=True)


def _split3(x):
    hi = x.astype(BF16)
    r1 = x - hi.astype(F32)
    mid = r1.astype(BF16)
    lo = (r1 - mid.astype(F32)).astype(BF16)
    return hi, mid, lo


def _full(shape):
    nd = len(shape)
    return pl.BlockSpec(shape, lambda *_: (0,) * nd)


def _in_perm():
    cols = list(range(RET_W))
    for p in range(FOX_HEADS // 2):
        for part in range(3):
            start = RET_W + part * 512 + p * BLK
            cols += list(range(start, start + BLK))
    return np.asarray(cols, np.int32)


def _rotary_tables(L):
    half = HEAD_LANES // 2
    inv = 1.0 / (ROPE_BASE ** (jnp.arange(half, dtype=F32) / half))
    ang = jnp.arange(L).astype(F32)[:, None] * inv[None, :]
    cos, sin = jnp.cos(ang), jnp.sin(ang)
    cos_t = jnp.tile(cos, (1, 4))
    sin_t = jnp.tile(jnp.concatenate([-sin, sin], axis=1), (1, 2))
    return cos_t, sin_t


def _decay_tables():
    gam = 1.0 - 2.0 ** (-5.0 - np.arange(RET_HEADS, dtype=np.float64))
    n = np.arange(BLK)
    same_or_past = (n[:, None] // CHUNK) >= (n[None, :] // CHUNK)
    dist = np.abs(n[:, None] - n[None, :])
    dmat = np.stack([np.where(same_or_past, g ** dist, 0.0) for g in gam]).astype(np.float32)
    lane_head = np.arange(BLK) // HEAD_LANES
    wq = np.stack([gam[2 * p + lane_head][None, :] ** (n[:, None] + 1.0) for p in range(2)]).astype(np.float32)
    wk = np.stack([gam[2 * p + lane_head][None, :] ** (BLK - 1.0 - n[:, None]) for p in range(2)]).astype(np.float32)
    g_blk = tuple(float(g ** BLK) for g in gam)
    return jnp.asarray(dmat), jnp.asarray(wq), jnp.asarray(wk), g_blk


def _shifted_blocks(tm):
    nb = tm // BLK
    return [pl.BlockSpec((BLK, D_MODEL), lambda i, j=j: (jnp.maximum(nb * i + j - 1, 0), 0)) for j in range(nb)]


def _rms_inproj(head, x, g, w_main, w_ff):
    L = x.shape[0] + BLK
    tm = _row_tile(L)
    nb = tm // BLK

    def body(head_ref, *refs):
        x_refs, (g_ref, wm_ref, wf_ref, h_ref, n_ref, p_ref, ff_ref) = refs[:nb], refs[nb:]
        parts = [r[...] for r in x_refs]
        parts[0] = jnp.where(pl.program_id(0) == 0, head_ref[...], parts[0])
        h = jnp.concatenate(parts, axis=0)
        h_ref[...] = h
        r = lax.rsqrt(jnp.mean(h * h, axis=-1, keepdims=True) + EPS)
        n = (h * r * g_ref[...]).astype(BF16)
        n_ref[...] = n
        p_ref[...] = _dot(n, wm_ref[...]).astype(BF16)
        ff_ref[...] = _dot(n, wf_ref[...])

    rows = lambda w: pl.BlockSpec((tm, w), lambda i: (i, 0))
    return pl.pallas_call(
        body, name="f_inproj", grid=(L // tm,),
        in_specs=[_full((BLK, D_MODEL))] + _shifted_blocks(tm)
        + [_full((1, D_MODEL)), _full((D_MODEL, MAIN_W)), _full((D_MODEL, BLK))],
        out_specs=[rows(D_MODEL), rows(D_MODEL), rows(MAIN_W), rows(BLK)],
        out_shape=[jax.ShapeDtypeStruct((L, D_MODEL), F32), jax.ShapeDtypeStruct((L, D_MODEL), BF16),
                   jax.ShapeDtypeStruct((L, MAIN_W), BF16), jax.ShapeDtypeStruct((L, BLK), F32)],
        compiler_params=_params(("parallel",)),
    )(head, *([x] * nb), g, w_main, w_ff)


def _block_group(nblk):
    return 3 if nblk % 3 == 0 else 1


def _fox_prep(ff, fb):
    L = ff.shape[0]
    nblk = L // BLK
    G = _block_group(nblk)

    def body(ff_ref, b_ref, c_ref, ct_ref, carry):
        @pl.when(pl.program_id(0) == 0)
        def _():
            carry[...] = jnp.zeros_like(carry)

        tri = (_iota((BLK, BLK), 0) >= _iota((BLK, BLK), 1)).astype(BF16)
        live = _iota((BLK, BLK), 1) < FOX_HEADS
        run = carry[...]
        for b in range(G):
            z = ff_ref[b * BLK:(b + 1) * BLK, :] + b_ref[...]
            lf = jnp.where(live, jnp.minimum(z, 0.0) - jnp.log1p(jnp.exp(-jnp.abs(z))), 0.0)
            hi, mid, lo = _split3(lf)
            cs = (_dot(tri, hi) + _dot(tri, mid) + _dot(tri, lo) + run) * LOG2E
            c_ref[b * BLK:(b + 1) * BLK, :] = cs
            ct_ref[b] = cs.T[0:8, :]
            run = run + jnp.sum(lf, axis=0, keepdims=True)
        carry[...] = run

    return pl.pallas_call(
        body, name="f_foxprep", grid=(nblk // G,),
        in_specs=[pl.BlockSpec((G * BLK, BLK), lambda i: (i, 0)), _full((1, BLK))],
        out_specs=[pl.BlockSpec((G * BLK, BLK), lambda i: (i, 0)), pl.BlockSpec((G, 8, BLK), lambda i: (i, 0, 0))],
        out_shape=[jax.ShapeDtypeStruct((L, BLK), F32), jax.ShapeDtypeStruct((nblk, 8, BLK), F32)],
        scratch_shapes=[pltpu.VMEM((1, BLK), F32)],
        compiler_params=_params(("arbitrary",)),
    )(ff, fb)


def _rot_fns(cos, sin):
    lane = _iota((BLK, BLK), 1)
    first = (lane & (HEAD_LANES - 1)) < HEAD_LANES // 2

    def swap(x):
        return jnp.where(first, pltpu.roll(x, BLK - 32, 1), pltpu.roll(x, 32, 1))

    def rot(x):
        return x * cos + swap(x) * sin

    def rot_t(dy):
        return dy * cos + swap(dy * sin)

    return rot, rot_t


def _retention_fwd(proj, cos_t, sin_t, ret_g):
    L = proj.shape[0]
    nblk = L // BLK
    G = _block_group(nblk)
    dmat, wq_t, wk_t, g_blk = _decay_tables()

    def body(q_ref, k_ref, v_ref, gate_ref, cos_ref, sin_ref, d_ref, wq_ref, wk_ref, rg_ref,
             mix_ref, o_ref, rs_ref, state):
        @pl.when(pl.program_id(0) == 0)
        def _():
            state[...] = jnp.zeros_like(state)

        lane = _iota((BLK, BLK), 1)
        sub = _iota((BLK, BLK), 0)
        for b in range(G):
            rows = slice(b * BLK, (b + 1) * BLK)
            rot, _ = _rot_fns(cos_ref[rows, :], sin_ref[rows, :])
            for p in range(2):
                qr = rot(q_ref[rows, p * BLK:(p + 1) * BLK].astype(F32))
                kr = rot(k_ref[rows, p * BLK:(p + 1) * BLK].astype(F32)) * (HEAD_LANES ** -0.5)
                kr_b = kr.astype(BF16)
                qw = (qr * wq_ref[p]).astype(BF16)
                kw = (kr * wk_ref[p]).astype(BF16)
                for e in range(2):
                    h = 2 * p + e
                    cols = slice(h * BLK, (h + 1) * BLK)
                    qm = jnp.where((lane >> 6) == e, qr, 0.0).astype(BF16)
                    s = _dot_nt(qm, kr_b) * d_ref[h]
                    vh = v_ref[rows, cols]
                    st = state[h]
                    rs_ref[b, h] = st
                    o = _dot(s.astype(BF16), vh) + _dot(qw, st.astype(BF16))
                    u = jnp.where((sub >> 6) == e, _dot_tn(kw, vh), 0.0)
                    state[h] = g_blk[h] * st + u
                    rn = lax.rsqrt(jnp.mean(o * o, axis=-1, keepdims=True) + EPS)
                    gate = gate_ref[rows, cols].astype(F32)
                    o_ref[rows, cols] = o
                    mix_ref[rows, cols] = (o * rn * rg_ref[:, cols] * (gate * jax.nn.sigmoid(gate))).astype(BF16)

    row = lambda c: (lambda i: (i, c))
    return pl.pallas_call(
        body, name="f_retention", grid=(nblk // G,),
        in_specs=[pl.BlockSpec((G * BLK, 256), row(0)), pl.BlockSpec((G * BLK, 256), row(1)),
                  pl.BlockSpec((G * BLK, 512), row(1)), pl.BlockSpec((G * BLK, 512), row(2)),
                  pl.BlockSpec((G * BLK, BLK), row(0)), pl.BlockSpec((G * BLK, BLK), row(0)),
                  _full((RET_HEADS, BLK, BLK)), _full((2, BLK, BLK)), _full((2, BLK, BLK)), _full((1, 512))],
        out_specs=[pl.BlockSpec((G * BLK, 512), row(0)), pl.BlockSpec((G * BLK, 512), row(0)),
                   pl.BlockSpec((G, RET_HEADS, BLK, BLK), lambda i: (i, 0, 0, 0))],
        out_shape=[jax.ShapeDtypeStruct((L, 512), BF16), jax.ShapeDtypeStruct((L, 512), F32),
                   jax.ShapeDtypeStruct((nblk, RET_HEADS, BLK, BLK), F32)],
        scratch_shapes=[pltpu.VMEM((RET_HEADS, BLK, BLK), F32)],
        compiler_params=_params(("arbitrary",)),
    )(proj, proj, proj, proj, cos_t, sin_t, dmat, wq_t, wk_t, ret_g)


def _fox_units(L):
    nblk = L // BLK
    assert L % BLK == 0 and nblk % 2 == 1, "sequence must be one 128-row block plus whole 256-row tiles"
    return nblk, (nblk - 1) // 2


def _fox_tile_masks():
    sub, lane = _iota((BLK, BLK), 0), _iota((BLK, BLK), 1)
    valid = _iota((BLK, UNIT), 0) >= N_PAD
    diag = _iota((UNIT, UNIT), 0) <= _iota((UNIT, UNIT), 1)
    r, q = _iota((BLK + UNIT, UNIT), 0), _iota((BLK + UNIT, UNIT), 1)
    first_and_diag = ((r < BLK) & (r >= N_PAD)) | ((r >= BLK) & (r - BLK <= q))
    return dict(first=(sub <= lane) & (sub >= N_PAD), valid=valid, diag=diag, first_and_diag=first_and_diag)


def _fox_fwd(proj, c, ctb, gather=()):
    L = proj.shape[0]
    nblk, nu = _fox_units(L)
    scale = HEAD_LANES ** -0.5 * LOG2E
    ng = len(gather)
    steps = FOX_HEADS // (2 * FOX_PAIRS)

    def body(qkv_ref, c_ref, ct_ref, *rest):
        g_in, (of_ref, lse_ref), g_out = rest[:ng], rest[ng:ng + 2], rest[ng + 2:2 * ng + 2]
        vt, csb = rest[2 * ng + 2:2 * ng + 4]
        p = pl.program_id(0)
        heads = [(pp, e, 2 * FOX_PAIRS * p + 2 * pp + e) for pp in range(FOX_PAIRS) for e in range(2)]

        @pl.when(p == 0)
        def _():
            lse_ref[...] = jnp.zeros_like(lse_ref)
            if ng:
                local, sends, _ = _allgather_copies(g_in, g_out, *rest[2 * ng + 4:])
                for cp in local + sends:
                    cp.start()

        lane = _iota((BLK, BLK), 1)
        sub8 = _iota((8, BLK), 0)
        masks = _fox_tile_masks()

        def pre(j, carry):
            off = pl.multiple_of(j * BLK, BLK)
            ct = c_ref[pl.ds(off, BLK), :]
            for pp in range(FOX_PAIRS):
                vt[pp, j] = qkv_ref[pl.ds(off, BLK), pp * 384 + 2 * BLK:pp * 384 + 3 * BLK].astype(F32).T.astype(BF16)
            for hh, (_, _, h) in enumerate(heads):
                col = jnp.sum(jnp.where(lane == h, ct, 0.0), axis=1, keepdims=True)
                csb[hh, j] = jnp.broadcast_to(col, (BLK, BLK))
            return carry

        lax.fori_loop(0, nblk, pre, 0)

        def attend(qblk, nq, n_whole):
            qlen = nq * BLK
            qoff = pl.multiple_of(qblk * BLK, BLK)
            qlane = _iota((qlen, BLK), 1)
            qs = [qkv_ref[pl.ds(qoff, qlen), pp * 384:pp * 384 + BLK].astype(F32) * scale for pp in range(FOX_PAIRS)]
            qm = [jnp.where((qlane >> 6) == e, qs[pp], 0.0).astype(BF16) for pp, e, _ in heads]
            ct_row = [jnp.concatenate([_pick_row(ct_ref[qblk + a], h) for a in range(nq)], axis=1) for _, _, h in heads]

            def step(segs, mask, st):
                blocks = [kblk + b for kblk, nk in segs for b in range(nk)]
                kts = []
                for pp in range(FOX_PAIRS):
                    kt = [qkv_ref[pl.ds(pl.multiple_of(kblk * BLK, BLK), nk * BLK), pp * 384 + BLK:pp * 384 + 2 * BLK]
                          for kblk, nk in segs]
                    kts.append(kt[0] if len(kt) == 1 else jnp.concatenate(kt, axis=0))
                out = []
                for hh, (pp, e, _) in enumerate(heads):
                    m, l, acc = st[3 * hh:3 * hh + 3]
                    s = _dot_nt(kts[pp], qm[hh])
                    t = jnp.concatenate([s[b * BLK:(b + 1) * BLK] - jnp.concatenate([csb[hh, blk]] * nq, axis=1)
                                         for b, blk in enumerate(blocks)], axis=0)
                    if mask is not None:
                        t = jnp.where(mask, t, NEG)
                    m_new = jnp.maximum(m, jnp.max(t, axis=0, keepdims=True) + ct_row[hh])
                    alpha = jnp.exp2(m - m_new)
                    pr = jnp.exp2(t - (m_new - ct_row[hh]))
                    l = alpha * l + jnp.sum(pr, axis=0, keepdims=True)
                    pr_b = pr.astype(BF16)
                    pv = None
                    for b, blk in enumerate(blocks):
                        part = _dot(vt[pp, blk, e * HEAD_LANES:(e + 1) * HEAD_LANES, :], pr_b[b * BLK:(b + 1) * BLK])
                        pv = part if pv is None else pv + part
                    out += [m_new, l, alpha * acc + pv]
                return tuple(out)

            st = (jnp.full((1, qlen), NEG, F32), jnp.zeros((1, qlen), F32),
                  jnp.zeros((HEAD_LANES, qlen), F32)) * len(heads)
            if nq == 1:
                st = step([(0, 1)], masks["first"], st)
            else:
                st = step([(0, 1), (qblk, 2)], masks["first_and_diag"], st)
                n_wide = n_whole // WIDE
                st = lax.fori_loop(0, n_wide, lambda j, s_: step([(1 + 2 * WIDE * j, 2 * WIDE)], None, s_), st)
                rest = 1 + 2 * WIDE * n_wide
                st = lax.cond((n_whole & 2) != 0, lambda s_: step([(rest, 4)], None, s_), lambda s_: s_, st)
                st = lax.cond((n_whole & 1) != 0, lambda s_: step([(rest + 2 * (n_whole & 2), 2)], None, s_),
                              lambda s_: s_, st)
            for pp in range(FOX_PAIRS):
                lo, hi = st[6 * pp:6 * pp + 3], st[6 * pp + 3:6 * pp + 6]
                o_t = jnp.concatenate([lo[2] * (1.0 / lo[1]), hi[2] * (1.0 / hi[1])], axis=0)
                of_ref[pl.ds(qoff, qlen), pp * BLK:(pp + 1) * BLK] = o_t.T.astype(BF16)
            lse = [st[3 * hh] + jnp.log(st[3 * hh + 1]) * LOG2E for hh in range(len(heads))]
            for a in range(nq):
                upd = jnp.zeros((8, BLK), F32)
                for hh, (_, _, h) in enumerate(heads):
                    upd = upd + jnp.where(sub8 == h, lse[hh][:, a * BLK:(a + 1) * BLK], 0.0)
                lse_ref[qblk + a] = lse_ref[qblk + a] + upd

        attend(0, 1, 0)

        def q_loop(u, carry):
            attend(1 + 2 * u, 2, u)
            return carry

        lax.fori_loop(0, nu, q_loop, 0)

        if ng:
            @pl.when(p == steps - 1)
            def _():
                local, sends, recvs = _allgather_copies(g_in, g_out, *rest[2 * ng + 4:])
                for cp in recvs:
                    cp.wait_recv()
                for cp in sends:
                    cp.wait_send()
                for cp in local:
                    cp.wait()

    width = 384 * FOX_PAIRS
    return pl.pallas_call(
        body, name="f_fox", grid=(steps,),
        in_specs=[pl.BlockSpec((L, width), lambda p: (0, RET_W // width + p)), _full((L, BLK)), _full((nblk, 8, BLK))]
        + [_ANY] * ng,
        out_specs=[pl.BlockSpec((L, FOX_PAIRS * BLK), lambda p: (0, p)), _full((nblk, 8, BLK))] + [_ANY] * ng,
        out_shape=[jax.ShapeDtypeStruct((L, 512), BF16), jax.ShapeDtypeStruct((nblk, 8, BLK), F32)]
        + [jax.ShapeDtypeStruct((N_CHIPS,) + a.shape, a.dtype) for a in gather],
        scratch_shapes=[pltpu.VMEM((FOX_PAIRS, nblk, BLK, BLK), BF16), pltpu.VMEM((2 * FOX_PAIRS, nblk, BLK, BLK), F32)]
        + _allgather_semaphores(ng),
        compiler_params=_params(("arbitrary",)),
    )(proj, c, ctb, *gather)


def _outproj_up(mix_r, o_f, h0, w_out, ffn_g, w_up, conv_w, conv_b):
    L = h0.shape[0]
    tm = _row_tile(L)
    shard = w_up.shape[2]
    assert 2 * shard == D_FF
    cw = [conv_w[j:j + 1] for j in range(3)]
    resident = lambda shape: pl.BlockSpec(shape, lambda i: (0,) * len(shape), pipeline_mode=pl.Buffered(1))

    def body(mr_ref, of_ref, h0_ref, wo_ref, g_ref, wu_ref, cw0, cw1, cw2, cb_ref,
             h1_ref, n2_ref, up_ref, act_ref, acc_ref, halo):
        i = pl.program_id(0)

        @pl.when(i == 0)
        def _():
            halo[...] = jnp.zeros_like(halo)

        h1 = h0_ref[...] + _dot(mr_ref[...], wo_ref[0:512, :]) + _dot(of_ref[...], wo_ref[512:1024, :])
        h1_ref[...] = h1
        r = lax.rsqrt(jnp.mean(h1 * h1, axis=-1, keepdims=True) + EPS)
        n2 = (h1 * r * g_ref[...]).astype(BF16)
        n2_ref[...] = n2
        live = i * tm + _iota((tm, 1), 0) >= N_PAD
        for half in range(2):
            cols = slice(half * shard, (half + 1) * shard)
            a_b = _dot(n2, wu_ref[half]).astype(BF16)
            b_b = _dot(n2, wu_ref[2 + half]).astype(BF16)
            up_ref[:, cols] = a_b
            up_ref[:, D_FF + half * shard:D_FF + (half + 1) * shard] = b_b
            a = jnp.where(live, a_b.astype(F32), 0.0)
            _, _, acc = _conv_taps(a, halo[:, cols], [cw0[:, cols], cw1[:, cols], cw2[:, cols]], cb_ref[:, cols])
            act_ref[:, cols] = (acc * jax.nn.sigmoid(acc) * b_b.astype(F32)).astype(BF16)
            acc_ref[:, cols] = acc.astype(BF16)
            halo[:, cols] = a[tm - 8:tm, :]

    rows = lambda w: pl.BlockSpec((tm, w), lambda i: (i, 0))
    return pl.pallas_call(
        body, name="f_outproj_up", grid=(L // tm,),
        in_specs=[rows(512), rows(512), rows(D_MODEL), resident((D_MODEL, D_MODEL)), _full((1, D_MODEL)),
                  resident((N_CHIPS, D_MODEL, shard)), _full((1, D_FF)), _full((1, D_FF)), _full((1, D_FF)),
                  _full((1, D_FF))],
        out_specs=[rows(D_MODEL), rows(D_MODEL), rows(2 * D_FF), rows(D_FF), rows(D_FF)],
        out_shape=[jax.ShapeDtypeStruct((L, D_MODEL), F32), jax.ShapeDtypeStruct((L, D_MODEL), BF16),
                   jax.ShapeDtypeStruct((L, 2 * D_FF), BF16), jax.ShapeDtypeStruct((L, D_FF), BF16),
                   jax.ShapeDtypeStruct((L, D_FF), BF16)],
        scratch_shapes=[pltpu.VMEM((8, D_FF), F32)],
        compiler_params=_params(("arbitrary",)),
    )(mix_r, o_f, h0, w_out, ffn_g, w_up, cw[0], cw[1], cw[2], conv_b)


def _conv_taps(a, halo, cw, cb):
    sub = _iota((a.shape[0], 1), 0)
    a1 = jnp.where(sub == 0, _pick_row(halo, 7), pltpu.roll(a, 1, 0))
    a2 = jnp.where(sub == 0, _pick_row(halo, 6), jnp.where(sub == 1, _pick_row(halo, 7), pltpu.roll(a, 2, 0)))
    acc = cb + a2 * cw[0]
    acc = acc + a1 * cw[1]
    acc = acc + a * cw[2]
    return a1, a2, acc


def _ffn_down_loss(g_act, w_down, h1, final_g, target, acc_saved, up):
    L = h1.shape[0]
    tm = _row_tile(L)
    nb = tm // BLK
    half_w = D_FF // 2

    def body(g_ref, wd_ref, h1_ref, gf_ref, acc_ref, b_ref, *refs):
        t_refs, (dh_ref, dhb_ref, dgf_ref, loss_ref, dacc_ref, db_ref) = refs[:nb], refs[nb:]
        i = pl.program_id(0)

        @pl.when(i == 0)
        def _():
            dgf_ref[...] = jnp.zeros_like(dgf_ref)
            loss_ref[...] = jnp.zeros_like(loss_ref)

        h2 = h1_ref[...] + _dot(g_ref[...], wd_ref[...])
        r = lax.rsqrt(jnp.mean(h2 * h2, axis=-1, keepdims=True) + EPS)
        yn = h2 * r
        gf = gf_ref[...]
        live = i * tm + _iota((tm, 1), 0) >= PREFIX
        target = jnp.concatenate([t[...] for t in t_refs], axis=0)
        err = jnp.where(live, yn * gf - target, 0.0)
        loss_ref[...] = loss_ref[...] + 0.5 * jnp.sum(jnp.mean(err * err, axis=-1, keepdims=True))
        dy = err * (1.0 / D_MODEL)
        dgf_ref[...] = dgf_ref[...] + jnp.sum(dy * yn, axis=0, keepdims=True)
        dyn = dy * gf
        dh = r * (dyn - yn * jnp.mean(dyn * yn, axis=-1, keepdims=True))
        dh_ref[...] = dh
        dhb = dh.astype(BF16)
        dhb_ref[...] = dhb
        for half in range(2):
            cols = slice(half * half_w, (half + 1) * half_w)
            acc = acc_ref[:, cols].astype(F32)
            dg = _dot_nt(dhb, wd_ref[cols, :])
            sg = jax.nn.sigmoid(acc)
            silu = acc * sg
            db_ref[:, cols] = (dg * silu).astype(BF16)
            dacc_ref[:, cols] = (dg * b_ref[:, cols].astype(F32) * (sg + silu * (1.0 - sg))).astype(BF16)

    rows = lambda w, c=0: pl.BlockSpec((tm, w), lambda i: (i, c))
    return pl.pallas_call(
        body, name="f_ffn_down_loss", grid=(L // tm,),
        in_specs=[rows(D_FF), pl.BlockSpec((D_FF, D_MODEL), lambda i: (0, 0), pipeline_mode=pl.Buffered(1)),
                  rows(D_MODEL), _full((1, D_MODEL)), rows(D_FF), rows(D_FF, 1)] + _shifted_blocks(tm),
        out_specs=[rows(D_MODEL), rows(D_MODEL), _full((1, D_MODEL)), _full((1, BLK)), rows(D_FF), rows(D_FF)],
        out_shape=[jax.ShapeDtypeStruct((L, D_MODEL), F32), jax.ShapeDtypeStruct((L, D_MODEL), BF16),
                   jax.ShapeDtypeStruct((1, D_MODEL), F32), jax.ShapeDtypeStruct((1, BLK), F32),
                   jax.ShapeDtypeStruct((L, D_FF), BF16), jax.ShapeDtypeStruct((L, D_FF), BF16)],
        compiler_params=_params(("arbitrary",)),
    )(g_act, w_down, h1, final_g, acc_saved, up, *([target] * nb))


def _ffn_bwd_up(dacc, db, up, conv_w, w_up, h1, ffn_g, dh2, w_out):
    L = h1.shape[0]
    tm = _row_tile(L)
    nt = L // tm
    shard = w_up.shape[2]
    cw = [conv_w[j:j + 1] for j in range(3)]

    def body(da_ref, halo_ref, db_ref, a_ref, cw0, cw1, cw2, wu_ref, h1_ref, g_ref, dh2_ref, wo_ref,
             dup_ref, dh1_ref, dh1b_ref, dmix_ref, dg_ref, dcw_ref):
        i = pl.program_id(0)

        @pl.when(i == 0)
        def _():
            dg_ref[...] = jnp.zeros_like(dg_ref)
            dcw_ref[...] = jnp.zeros_like(dcw_ref)

        sub = _iota((tm, 1), 0)
        sub8 = _iota((8, 1), 0)
        last_tile = i == nt - 1
        dbv = db_ref[...]
        dup_ref[:, D_FF:2 * D_FF] = dbv
        dn = _dot_nt(dbv[:, 0:shard], wu_ref[2]) + _dot_nt(dbv[:, shard:2 * shard], wu_ref[3])
        for half in range(2):
            cols = slice(half * shard, (half + 1) * shard)
            d0 = da_ref[:, cols].astype(F32)
            halo = jnp.where(last_tile, 0.0, halo_ref[:, cols].astype(F32))
            d1 = jnp.where(sub == tm - 1, _pick_row(halo, 0), pltpu.roll(d0, tm - 1, 0))
            d2 = jnp.where(sub == tm - 2, _pick_row(halo, 0),
                           jnp.where(sub == tm - 1, _pick_row(halo, 1), pltpu.roll(d0, tm - 2, 0)))
            a = a_ref[:, cols].astype(F32)
            upd = jnp.zeros((8, shard), F32)
            for j, t in enumerate((d2 * a, d1 * a, d0 * a, d0)):
                upd = upd + jnp.where(sub8 == j, jnp.sum(t, axis=0, keepdims=True), 0.0)
            dcw_ref[:, cols] = dcw_ref[:, cols] + upd
            da = (d0 * cw2[:, cols] + d1 * cw1[:, cols] + d2 * cw0[:, cols]).astype(BF16)
            dup_ref[:, cols] = da
            dn = dn + _dot_nt(da, wu_ref[half])
        h1 = h1_ref[...]
        r = lax.rsqrt(jnp.mean(h1 * h1, axis=-1, keepdims=True) + EPS)
        yn = h1 * r
        dg_ref[...] = dg_ref[...] + jnp.sum(dn * yn, axis=0, keepdims=True)
        dyn = dn * g_ref[...]
        dh1 = dh2_ref[...] + r * (dyn - yn * jnp.mean(dyn * yn, axis=-1, keepdims=True))
        dh1_ref[...] = dh1
        dh1b = dh1.astype(BF16)
        dh1b_ref[...] = dh1b
        dmix_ref[...] = _dot_nt(dh1b, wo_ref[...]).astype(BF16)

    rows = lambda w: pl.BlockSpec((tm, w), lambda i: (i, 0))
    halo = pl.BlockSpec((8, D_FF), lambda i: (jnp.minimum((i + 1) * (tm // 8), L // 8 - 1), 0))
    return pl.pallas_call(
        body, name="b_ffn_up", grid=(nt,),
        in_specs=[rows(D_FF), halo, rows(D_FF), rows(D_FF), _full((1, D_FF)), _full((1, D_FF)), _full((1, D_FF)),
                  _full((N_CHIPS, D_MODEL, shard)), rows(D_MODEL), _full((1, D_MODEL)), rows(D_MODEL),
                  _full((D_MODEL, D_MODEL))],
        out_specs=[rows(2 * D_FF), rows(D_MODEL), rows(D_MODEL), rows(D_MODEL), _full((1, D_MODEL)),
                   _full((8, D_FF))],
        out_shape=[jax.ShapeDtypeStruct((L, 2 * D_FF), BF16), jax.ShapeDtypeStruct((L, D_MODEL), F32),
                   jax.ShapeDtypeStruct((L, D_MODEL), BF16), jax.ShapeDtypeStruct((L, D_MODEL), BF16),
                   jax.ShapeDtypeStruct((1, D_MODEL), F32), jax.ShapeDtypeStruct((8, D_FF), F32)],
        compiler_params=_params(("arbitrary",)),
    )(dacc, dacc, db, up, cw[0], cw[1], cw[2], w_up, h1, ffn_g, dh2, w_out)


def _wgrad(a, b, name, tn=None, tk=None):
    L, K = a.shape
    N = b.shape[1]
    tn = N if tn is None else tn
    tk = K if tk is None else tk
    tl = _row_tile(L, (1408, 768, 512, 256, 128))

    def body(a_ref, b_ref, o_ref):
        @pl.when(pl.program_id(2) == 0)
        def _():
            o_ref[...] = jnp.zeros_like(o_ref)

        o_ref[0] = o_ref[0] + _dot_tn(a_ref[...], b_ref[...])

    return pl.pallas_call(
        body, name=name, grid=(N // tn, K // tk, L // tl),
        in_specs=[pl.BlockSpec((tl, tk), lambda n, k, l: (l, k)), pl.BlockSpec((tl, tn), lambda n, k, l: (l, n))],
        out_specs=pl.BlockSpec((1, tk, tn), lambda n, k, l: (n, k, 0)),
        out_shape=jax.ShapeDtypeStruct((N // tn, K, tn), F32),
        compiler_params=_params(("parallel", "parallel", "arbitrary")),
    )(a, b)


def _retention_bwd(dmix, o, proj, cos_t, sin_t, ret_g, states, exchange=()):
    L = proj.shape[0]
    nblk = L // BLK
    G = _block_group(nblk)
    steps = nblk // G
    nx = len(exchange)
    dmat, wq_t, wk_t, g_blk = _decay_tables()

    def body(dm_ref, o_ref, q_ref, k_ref, v_ref, gate_ref, cos_ref, sin_ref, d_ref, wq_ref, wk_ref, rg_ref, rs_ref,
             *rest):
        x_in, (dp_ref, drg_ref), x_out, gstate = rest[:nx], rest[nx:nx + 2], rest[nx + 2:2 * nx + 2], rest[2 * nx + 2]

        @pl.when(pl.program_id(0) == 0)
        def _():
            if nx:
                for cp in _sibling_half_copies(x_in, x_out, *rest[2 * nx + 3:])[0]:
                    cp.start()
            gstate[...] = jnp.zeros_like(gstate)
            drg_ref[...] = jnp.zeros_like(drg_ref)

        lane = _iota((BLK, BLK), 1)
        sub = _iota((BLK, BLK), 0)
        scale = HEAD_LANES ** -0.5
        for b in reversed(range(G)):
            rows = slice(b * BLK, (b + 1) * BLK)
            rot, rot_t = _rot_fns(cos_ref[rows, :], sin_ref[rows, :])
            for p in range(2):
                qr = rot(q_ref[rows, p * BLK:(p + 1) * BLK].astype(F32))
                kr = rot(k_ref[rows, p * BLK:(p + 1) * BLK].astype(F32)) * scale
                kr_b = kr.astype(BF16)
                qw = (qr * wq_ref[p]).astype(BF16)
                kw = (kr * wk_ref[p]).astype(BF16)
                dqr = jnp.zeros((BLK, BLK), F32)
                dkr = jnp.zeros((BLK, BLK), F32)
                for e in range(2):
                    h = 2 * p + e
                    cols = slice(h * BLK, (h + 1) * BLK)
                    head_lanes = (lane >> 6) == e
                    o = o_ref[rows, cols]
                    rn = lax.rsqrt(jnp.mean(o * o, axis=-1, keepdims=True) + EPS)
                    y = o * rn
                    gate = gate_ref[rows, cols].astype(F32)
                    sg = jax.nn.sigmoid(gate)
                    dm = dm_ref[rows, cols].astype(F32)
                    rgain = rg_ref[:, cols]
                    drg_ref[:, cols] = drg_ref[:, cols] + jnp.sum(dm * y * (gate * sg), axis=0, keepdims=True)
                    dp_ref[rows, 1024 + h * BLK:1024 + (h + 1) * BLK] = (
                        dm * y * rgain * (sg * (1.0 + gate * (1.0 - sg)))).astype(BF16)
                    dy = dm * rgain * (gate * sg)
                    do = (rn * (dy - y * jnp.mean(dy * y, axis=-1, keepdims=True))).astype(BF16)
                    vh = v_ref[rows, cols]
                    qm = jnp.where(head_lanes, qr, 0.0).astype(BF16)
                    dmh = d_ref[h]
                    s = (_dot_nt(qm, kr_b) * dmh).astype(BF16)
                    ds = (_dot_nt(do, vh) * dmh).astype(BF16)
                    st = rs_ref[b, h].astype(BF16)
                    gs = gstate[h]
                    gs_b = gs.astype(BF16)
                    dqr = dqr + jnp.where(head_lanes, _dot(ds, kr_b), 0.0) + _dot_nt(do, st) * wq_ref[p]
                    dkr = dkr + _dot_tn(ds, qm) + _dot_nt(vh, gs_b) * wk_ref[p]
                    dp_ref[rows, 512 + h * BLK:512 + (h + 1) * BLK] = (_dot_tn(s, do) + _dot(kw, gs_b)).astype(BF16)
                    dr = jnp.where((sub >> 6) == e, _dot_tn(qw, do), 0.0)
                    gstate[h] = dr + g_blk[h] * gs
                dp_ref[rows, p * BLK:(p + 1) * BLK] = rot_t(dqr).astype(BF16)
                dp_ref[rows, 256 + p * BLK:256 + (p + 1) * BLK] = (rot_t(dkr) * scale).astype(BF16)

        if nx:
            @pl.when(pl.program_id(0) == steps - 1)
            def _():
                sends, recvs = _sibling_half_copies(x_in, x_out, *rest[2 * nx + 3:])
                for cp in recvs:
                    cp.wait_recv()
                for cp in sends:
                    cp.wait_send()

    row = lambda c: (lambda i: (steps - 1 - i, c))
    return pl.pallas_call(
        body, name="b_retention", grid=(steps,),
        in_specs=[pl.BlockSpec((G * BLK, 512), row(0)), pl.BlockSpec((G * BLK, 512), row(0)),
                  pl.BlockSpec((G * BLK, 256), row(0)), pl.BlockSpec((G * BLK, 256), row(1)),
                  pl.BlockSpec((G * BLK, 512), row(1)), pl.BlockSpec((G * BLK, 512), row(2)),
                  pl.BlockSpec((G * BLK, BLK), row(0)), pl.BlockSpec((G * BLK, BLK), row(0)),
                  _full((RET_HEADS, BLK, BLK)), _full((2, BLK, BLK)), _full((2, BLK, BLK)), _full((1, 512)),
                  pl.BlockSpec((G, RET_HEADS, BLK, BLK), lambda i: (steps - 1 - i, 0, 0, 0))] + [_ANY] * nx,
        out_specs=[pl.BlockSpec((G * BLK, RET_W), row(0)), _full((1, 512))] + [_ANY] * nx,
        out_shape=[jax.ShapeDtypeStruct((L, RET_W), BF16), jax.ShapeDtypeStruct((1, 512), F32)]
        + _sibling_half_shapes(exchange),
        scratch_shapes=[pltpu.VMEM((RET_HEADS, BLK, BLK), F32)] + _sibling_half_semaphores(nx),
        compiler_params=_params(("arbitrary",)),
    )(dmix, o, proj, proj, proj, proj, cos_t, sin_t, dmat, wq_t, wk_t, ret_g, states, *exchange)


def _fox_delta(dmix, o_f):
    L = o_f.shape[0]
    nblk = L // BLK
    G = _block_group(nblk)

    def body(do_ref, o_ref, d_ref):
        sel = ((_iota((8, 512), 1) >> 6) == _iota((8, 512), 0)).astype(BF16)
        for b in range(G):
            rows = slice(b * BLK, (b + 1) * BLK)
            prod = do_ref[rows, :].astype(F32) * o_ref[rows, :].astype(F32)
            hi = prod.astype(BF16)
            lo = (prod - hi.astype(F32)).astype(BF16)
            d_ref[b] = _dot_nt(sel, hi) + _dot_nt(sel, lo)

    return pl.pallas_call(
        body, name="b_foxdelta", grid=(nblk // G,),
        in_specs=[pl.BlockSpec((G * BLK, 512), lambda i: (i, 1)), pl.BlockSpec((G * BLK, 512), lambda i: (i, 0))],
        out_specs=pl.BlockSpec((G, 8, BLK), lambda i: (i, 0, 0)),
        out_shape=jax.ShapeDtypeStruct((nblk, 8, BLK), F32),
        compiler_params=_params(("parallel",)),
    )(dmix, o_f)


def _fox_bwd(proj, dmix, c, ctb, lse, delta, scatter=()):
    L = proj.shape[0]
    nblk, nu = _fox_units(L)
    scale = HEAD_LANES ** -0.5
    ns = len(scatter)

    steps = FOX_HEADS // (2 * FOX_PAIRS)

    def body(qkv_ref, do_ref, c_ref, ct_ref, lse_ref, dl_ref, *rest):
        s_in, (dp_ref, dc_ref, dcq_ref), s_out = rest[:ns], rest[ns:ns + 3], rest[ns + 3:2 * ns + 3]
        ktt, dqt, dk_acc, dv_acc, dcs_acc = rest[2 * ns + 3:2 * ns + 8]
        p = pl.program_id(0)
        heads = [(pp, e, 2 * FOX_PAIRS * p + 2 * pp + e) for pp in range(FOX_PAIRS) for e in range(2)]

        @pl.when(p == 0)
        def _():
            dc_ref[...] = jnp.zeros_like(dc_ref)
            dcq_ref[...] = jnp.zeros_like(dcq_ref)
            if ns:
                for cp in _scatter_copies(s_in, s_out, *rest[2 * ns + 8:]):
                    cp.start()

        sub8 = _iota((8, BLK), 0)
        masks = _fox_tile_masks()

        def pre(j, carry):
            off = pl.multiple_of(j * BLK, BLK)
            for pp in range(FOX_PAIRS):
                ktt[pp, j] = qkv_ref[pl.ds(off, BLK), pp * 384 + BLK:pp * 384 + 2 * BLK].astype(F32).T.astype(BF16)
                dqt[pp, j] = jnp.zeros((BLK, BLK), F32)
            return carry

        lax.fori_loop(0, nblk, pre, 0)

        def kv_pass(kblk, nk, n_later):
            klen = nk * BLK
            koff = pl.multiple_of(kblk * BLK, BLK)
            kt = [qkv_ref[pl.ds(koff, klen), pp * 384 + BLK:pp * 384 + 2 * BLK] for pp in range(FOX_PAIRS)]
            vtile = [qkv_ref[pl.ds(koff, klen), pp * 384 + 2 * BLK:pp * 384 + 3 * BLK] for pp in range(FOX_PAIRS)]
            ct = c_ref[pl.ds(koff, klen), :]
            klane = _iota((klen, BLK), 1)
            cs = [jnp.broadcast_to(jnp.sum(jnp.where(klane == h, ct, 0.0), axis=1, keepdims=True), (klen, WIDE * UNIT))
                  for _, _, h in heads]
            k_t = [jnp.concatenate([ktt[pp, kblk + b, e * HEAD_LANES:(e + 1) * HEAD_LANES, :] for b in range(nk)], axis=1)
                   for pp, e, _ in heads]
            for pp in range(FOX_PAIRS):
                dk_acc[pp, 0:klen] = jnp.zeros((klen, BLK), F32)
                dv_acc[pp, 0:klen] = jnp.zeros((klen, BLK), F32)
            for hh in range(len(heads)):
                dcs_acc[hh, 0:klen] = jnp.zeros((klen, BLK), F32)

            def tile(qblk, nq, mask):
                qlen = nq * BLK
                if mask == "valid":
                    mask = _iota((klen, qlen), 0) >= N_PAD
                qoff = pl.multiple_of(qblk * BLK, BLK)
                qlane = _iota((qlen, BLK), 1)
                qs = [qkv_ref[pl.ds(qoff, qlen), pp * 384:pp * 384 + BLK].astype(F32) * (scale * LOG2E)
                      for pp in range(FOX_PAIRS)]
                dot_ = [do_ref[pl.ds(qoff, qlen), pp * BLK:(pp + 1) * BLK] for pp in range(FOX_PAIRS)]
                stats = [[ref[qblk + a] for a in range(nq)] for ref in (ct_ref, lse_ref, dl_ref)]
                dcq = [jnp.zeros((8, BLK), F32) for _ in range(nq)]
                for hh, (pp, e, h) in enumerate(heads):
                    head = (qlane >> 6) == e
                    ct_row, lse_row, dl_row = [jnp.concatenate([_pick_row(t, h) for t in ts], axis=1) for ts in stats]
                    qm = jnp.where(head, qs[pp], 0.0).astype(BF16)
                    dom = jnp.where(head, dot_[pp], jnp.zeros_like(dot_[pp]))
                    t = _dot_nt(kt[pp], qm) - cs[hh][:, 0:qlen]
                    if mask is not None:
                        t = jnp.where(mask, t, NEG)
                    pr = jnp.exp2(t + (ct_row - lse_row))
                    dv_acc[pp, 0:klen] = dv_acc[pp, 0:klen] + _dot(pr.astype(BF16), dom)
                    dsv = pr * (_dot_nt(vtile[pp], dom) - dl_row)
                    ds_b = dsv.astype(BF16)
                    dk_acc[pp, 0:klen] = dk_acc[pp, 0:klen] + _dot(ds_b, qm)
                    rows = slice(e * HEAD_LANES, (e + 1) * HEAD_LANES)
                    dq_t = _dot(k_t[hh], ds_b)
                    key_side = dsv[:, 0:BLK]
                    for a in range(1, nq):
                        key_side = key_side + dsv[:, a * BLK:(a + 1) * BLK]
                    dcs_acc[hh, 0:klen] = dcs_acc[hh, 0:klen] + key_side
                    query_side = jnp.sum(dsv, axis=0, keepdims=True)
                    for a in range(nq):
                        cols = slice(a * BLK, (a + 1) * BLK)
                        dqt[pp, qblk + a, rows, :] = dqt[pp, qblk + a, rows, :] + dq_t[:, cols]
                        dcq[a] = dcq[a] + jnp.where(sub8 == h, query_side[:, cols], 0.0)
                for a in range(nq):
                    dcq_ref[qblk + a] = dcq_ref[qblk + a] + dcq[a]

            later_mask = "valid" if nk == 1 else None
            n_later = jnp.asarray(n_later, jnp.int32)
            n_wide = n_later // WIDE

            def later_wide(i, carry):
                tile(kblk + nk + 2 * WIDE * i, 2 * WIDE, later_mask)
                return carry

            tile(kblk, nk, masks["first"] if nk == 1 else masks["diag"])
            lax.fori_loop(0, n_wide, later_wide, 0)
            rest_blk = kblk + nk + 2 * WIDE * n_wide

            @pl.when((n_later & 2) != 0)
            def _():
                tile(rest_blk, 4, later_mask)

            @pl.when((n_later & 1) != 0)
            def _():
                tile(rest_blk + 2 * (n_later & 2), 2, later_mask)

            upd = jnp.zeros((klen, BLK), F32)
            for hh, (_, _, h) in enumerate(heads):
                upd = upd + jnp.where(klane == h, -jnp.sum(dcs_acc[hh, 0:klen], axis=1, keepdims=True), 0.0)
            dc_ref[pl.ds(koff, klen), :] = dc_ref[pl.ds(koff, klen), :] + upd
            for pp in range(FOX_PAIRS):
                dp_ref[pl.ds(koff, klen), pp * 384 + BLK:pp * 384 + 2 * BLK] = (
                    dk_acc[pp, 0:klen] * (1.0 / LOG2E)).astype(BF16)
                dp_ref[pl.ds(koff, klen), pp * 384 + 2 * BLK:pp * 384 + 3 * BLK] = dv_acc[pp, 0:klen].astype(BF16)

        kv_pass(0, 1, nu)

        def k_loop(u, carry):
            kv_pass(1 + 2 * u, 2, nu - 1 - u)
            return carry

        lax.fori_loop(0, nu, k_loop, 0)

        def flush(j, carry):
            off = pl.multiple_of(j * BLK, BLK)
            for pp in range(FOX_PAIRS):
                dp_ref[pl.ds(off, BLK), pp * 384:pp * 384 + BLK] = (dqt[pp, j].T * scale).astype(BF16)
            return carry

        lax.fori_loop(0, nblk, flush, 0)

        if ns:
            @pl.when(p == steps - 1)
            def _():
                copies = _scatter_copies(s_in, s_out, *rest[2 * ns + 8:])
                for cp in copies:
                    cp.wait_recv()
                for cp in copies:
                    cp.wait_send()

    width = 384 * FOX_PAIRS
    once = lambda shape, index: pl.BlockSpec(shape, index, pipeline_mode=pl.Buffered(1))
    stat = once((nblk, 8, BLK), lambda p: (0, 0, 0))
    return pl.pallas_call(
        body, name="b_fox", grid=(steps,),
        in_specs=[once((L, width), lambda p: (0, RET_W // width + p)),
                  once((L, FOX_PAIRS * BLK), lambda p: (0, 4 // FOX_PAIRS + p)),
                  once((L, BLK), lambda p: (0, 0)), stat, stat, stat] + [_ANY] * ns,
        out_specs=[pl.BlockSpec((L, width), lambda p: (0, p)), _full((L, BLK)), _full((nblk, 8, BLK))] + [_ANY] * ns,
        out_shape=[jax.ShapeDtypeStruct((L, FOX_W), BF16), jax.ShapeDtypeStruct((L, BLK), F32),
                   jax.ShapeDtypeStruct((nblk, 8, BLK), F32)] + _scatter_shapes(scatter),
        scratch_shapes=[pltpu.VMEM((FOX_PAIRS, nblk, BLK, BLK), BF16), pltpu.VMEM((FOX_PAIRS, nblk, BLK, BLK), F32),
                        pltpu.VMEM((FOX_PAIRS, UNIT, BLK), F32), pltpu.VMEM((FOX_PAIRS, UNIT, BLK), F32),
                        pltpu.VMEM((2 * FOX_PAIRS, UNIT, BLK), F32)]
        + _scatter_semaphores(ns),
        compiler_params=_params(("arbitrary",)),
    )(proj, dmix, c, ctb, lse, delta, *scatter)


def _fox_post(dc, dcq, ff, fb):
    L = dc.shape[0]
    nblk = L // BLK
    G = _block_group(nblk)
    steps = nblk // G

    def body(dc_ref, dcq_ref, ff_ref, b_ref, dff_ref, dffb_ref, dfb_ref, carry):
        @pl.when(pl.program_id(0) == 0)
        def _():
            carry[...] = jnp.zeros_like(carry)
            dfb_ref[...] = jnp.zeros_like(dfb_ref)

        tri = (_iota((BLK, BLK), 0) <= _iota((BLK, BLK), 1)).astype(BF16)
        live = _iota((BLK, BLK), 1) < FOX_HEADS
        run, dfb = carry[...], dfb_ref[...]
        for b in reversed(range(G)):
            rows = slice(b * BLK, (b + 1) * BLK)
            d = dc_ref[rows, :] + jnp.concatenate([dcq_ref[b], jnp.zeros((BLK - 8, BLK), F32)], axis=0).T
            hi, mid, lo = _split3(d)
            dlf = _dot(tri, hi) + _dot(tri, mid) + _dot(tri, lo) + run
            run = run + jnp.sum(d, axis=0, keepdims=True)
            z = ff_ref[rows, :] + b_ref[...]
            dff = jnp.where(live, dlf * jax.nn.sigmoid(-z), 0.0)
            dff_ref[rows, :] = dff
            dffb_ref[rows, :] = dff.astype(BF16)
            dfb = dfb + jnp.sum(dff, axis=0, keepdims=True)
        carry[...] = run
        dfb_ref[...] = dfb

    rev = lambda i: (steps - 1 - i, 0)
    return pl.pallas_call(
        body, name="b_foxpost", grid=(steps,),
        in_specs=[pl.BlockSpec((G * BLK, BLK), rev), pl.BlockSpec((G, 8, BLK), lambda i: (steps - 1 - i, 0, 0)),
                  pl.BlockSpec((G * BLK, BLK), rev), _full((1, BLK))],
        out_specs=[pl.BlockSpec((G * BLK, BLK), rev), pl.BlockSpec((G * BLK, BLK), rev), _full((1, BLK))],
        out_shape=[jax.ShapeDtypeStruct((L, BLK), F32), jax.ShapeDtypeStruct((L, BLK), BF16),
                   jax.ShapeDtypeStruct((1, BLK), F32)],
        scratch_shapes=[pltpu.VMEM((1, BLK), F32)],
        compiler_params=_params(("arbitrary",)),
    )(dc, dcq, ff, fb)


def _inproj_bwd(dpr, dpf, dffb, w_main, w_ff, h0, g, dh1, scatter=()):
    L = h0.shape[0]
    S = L - BLK
    tm = _row_tile(S, (512, 256, 128))
    nt = S // tm
    ns = len(scatter)
    operands = (dpr, dpf, dffb, h0, dh1)

    def body(*refs):
        lead, tile = refs[0:5], refs[5:10]
        wm_ref, wf_ref, g_ref = refs[10:13]
        rest = refs[13:]
        s_in, (dlead_ref, dx_ref, dg_ref), s_out = rest[:ns], rest[ns:ns + 3], rest[ns + 3:2 * ns + 3]
        i = pl.program_id(0)

        def rows_bwd(dpr_ref, dpf_ref, dff_ref, h_ref, dh1_ref):
            dn = (_dot_nt(dpr_ref[...], wm_ref[:, 0:RET_W]) + _dot_nt(dpf_ref[...], wm_ref[:, RET_W:MAIN_W])
                  + _dot_nt(dff_ref[...], wf_ref[...]))
            h = h_ref[...]
            r = lax.rsqrt(jnp.mean(h * h, axis=-1, keepdims=True) + EPS)
            yn = h * r
            dyn = dn * g_ref[...]
            dh0 = dh1_ref[...] + r * (dyn - yn * jnp.mean(dyn * yn, axis=-1, keepdims=True))
            return dh0, jnp.sum(dn * yn, axis=0, keepdims=True)

        @pl.when(i == 0)
        def _():
            if ns:
                for cp in _scatter_copies(s_in, s_out, *rest[2 * ns + 3:]):
                    cp.start()
            dlead_ref[...], dg_ref[...] = rows_bwd(*lead)

        dx_ref[...], dg_tile = rows_bwd(*tile)
        dg_ref[...] = dg_ref[...] + dg_tile

        if ns:
            @pl.when(i == nt - 1)
            def _():
                copies = _scatter_copies(s_in, s_out, *rest[2 * ns + 3:])
                for cp in copies:
                    cp.wait_recv()
                for cp in copies:
                    cp.wait_send()

    lead_spec = lambda a: pl.BlockSpec((BLK, a.shape[1]), lambda i: (0, 0))
    tile_spec = lambda a: pl.BlockSpec((pl.Element(tm), pl.Element(a.shape[1])),
                                       lambda i: (pl.multiple_of(BLK + i * tm, BLK), 0))
    return pl.pallas_call(
        body, name="b_inproj", grid=(nt,),
        in_specs=[lead_spec(a) for a in operands] + [tile_spec(a) for a in operands]
        + [_full((D_MODEL, MAIN_W)), _full((D_MODEL, BLK)), _full((1, D_MODEL))] + [_ANY] * ns,
        out_specs=[_full((BLK, D_MODEL)), pl.BlockSpec((tm, D_MODEL), lambda i: (i, 0)), _full((1, D_MODEL))]
        + [_ANY] * ns,
        out_shape=[jax.ShapeDtypeStruct((BLK, D_MODEL), F32), jax.ShapeDtypeStruct((S, D_MODEL), F32),
                   jax.ShapeDtypeStruct((1, D_MODEL), F32)] + _scatter_shapes(scatter),
        scratch_shapes=_scatter_semaphores(ns),
        compiler_params=_params(("arbitrary",)),
    )(*operands, *operands, w_main, w_ff, g, *scatter)


def _local_step(x, target, meta, attn_g, w_main, w_ff, fox_b, ret_g, w_out, ffn_g, w_up, conv_w, conv_b, w_down, final_g,
                late=None, mid=None, last=None):
    S = x.shape[0]
    L = S + PREFIX
    head = jnp.concatenate([jnp.zeros((N_PAD, D_MODEL), F32), meta], axis=0)
    fb = jnp.pad(fox_b, ((0, 0), (0, BLK - FOX_HEADS)))
    cos_t, sin_t = _rotary_tables(L)

    h0, n1, proj, ff = _rms_inproj(head, x, attn_g, w_main, w_ff)
    c, ctb = _fox_prep(ff, fb)
    mix_r, o_ret, states = _retention_fwd(proj, cos_t, sin_t, ret_g)
    if late is None:
        o_f, lse = _fox_fwd(proj, c, ctb)
    else:
        o_f, lse, *gathered = _fox_fwd(proj, c, ctb, gather=late[0])
        w_out, w_up, w_down = late[1](gathered)
    h1, n2, up, g_act, acc_saved = _outproj_up(mix_r, o_f, h0, w_out, ffn_g, w_up, conv_w, conv_b)
    dh2, dh2b, d_final_g, loss, dacc, db = _ffn_down_loss(g_act, w_down, h1, final_g, target, acc_saved, up)

    dup, dh1, dh1b, dmix, d_ffn_g, dconv = _ffn_bwd_up(dacc, db, up, conv_w, w_up, h1, ffn_g, dh2, w_out)
    d_w_down = _wgrad(g_act, dh2b, "wgrad_down", tk=D_FF // 2)[0]
    d_w_up = _wgrad(n2, dup, "wgrad_up", tn=w_up.shape[2])
    d_w_out = jnp.concatenate([_wgrad(mix_r, dh1b, "wgrad_out_r")[0], _wgrad(o_f, dh1b, "wgrad_out_f")[0]], axis=0)

    early = () if mid is None else mid[0](d_w_out, d_w_up, d_w_down)
    dpr, d_ret_g, *from_sibling = _retention_bwd(dmix, o_ret, proj, cos_t, sin_t, ret_g, states, exchange=early)
    delta = _fox_delta(dmix, o_f)
    scatter = () if mid is None else mid[1](early, from_sibling)
    dpf, dc, dcq, *received = _fox_bwd(proj, dmix, c, ctb, lse, delta, scatter=scatter)
    dff, dffb, d_fox_b = _fox_post(dc, dcq, ff, fb)
    d_w_ret, d_w_fox = _wgrad(n1, dpr, "wgrad_in_r")[0], _wgrad(n1, dpf, "wgrad_in_f")[0]
    d_w_ff = _wgrad(n1, dffb, "wgrad_in_ff")[0][:, :FOX_HEADS]
    scatter_in = () if last is None else last(d_w_ret, d_w_fox, d_w_ff)
    dlead, dx, d_attn_g, *received_in = _inproj_bwd(dpr, dpf, dffb, w_main, w_ff, h0, attn_g, dh1, scatter=scatter_in)

    return dict(
        loss=loss[0, 0], dx=dx, dmeta=dlead[N_PAD:], attn_g=d_attn_g, w_main=jnp.concatenate([d_w_ret, d_w_fox], axis=1),
        w_ff=d_w_ff, fox_b=d_fox_b[:, :FOX_HEADS], ret_g=d_ret_g, w_out=d_w_out, ffn_g=d_ffn_g,
        w_up=d_w_up, conv_w=dconv[0:3], conv_b=dconv[3:4], w_down=d_w_down, final_g=d_final_g,
        scatter=list(scatter_in) + list(scatter), received=list(received_in) + list(received))


_ANY = pl.BlockSpec(memory_space=pl.ANY)


def _place():
    return lax.axis_index("x"), lax.axis_index("y"), lax.axis_index("c")


def _other_chips(x, y):
    return [(1 - x, y), (x, 1 - y), (1 - x, 1 - y)]


def _allgather_semaphores(n):
    if n == 0:
        return []
    return [pltpu.SemaphoreType.DMA((3 * n,)), pltpu.SemaphoreType.DMA((3 * n,)), pltpu.SemaphoreType.DMA((n,))]


def _allgather_copies(ins, outs, send, recv, loc):
    n = len(ins)
    x, y, c = _place()
    mine = 2 * x + y
    peers = _other_chips(x, y)

    def remote(a, k, slot):
        return pltpu.make_async_remote_copy(
            src_ref=ins[a], dst_ref=outs[a].at[slot], send_sem=send.at[3 * a + k], recv_sem=recv.at[3 * a + k],
            device_id=(peers[k][0], peers[k][1], c), device_id_type=MESH)

    local = [pltpu.make_async_copy(ins[a], outs[a].at[mine], loc.at[a]) for a in range(n)]
    sends = [remote(a, k, mine) for a in range(n) for k in range(3)]
    recvs = [remote(a, k, 2 * peers[k][0] + peers[k][1]) for a in range(n) for k in range(3)]
    return local, sends, recvs


def _chip_allgather_halves(w, small):
    half = w.shape[0] // 2

    def body(w_ref, s_ref, wo_ref, so_ref, send, recv, fsend, frecv, ssend, srecv, loc):
        x, y, c = _place()
        mine = 2 * x + y
        peers = _other_chips(x, y)

        def fetch(k, slot):
            return pltpu.make_async_remote_copy(
                src_ref=w_ref.at[pl.ds(c * half, half)], dst_ref=wo_ref.at[slot, c], send_sem=send.at[k],
                recv_sem=recv.at[k], device_id=(peers[k][0], peers[k][1], c), device_id_type=MESH)

        def forward(k, which):
            slot = 2 * peers[k][0] + peers[k][1]
            return pltpu.make_async_remote_copy(
                src_ref=wo_ref.at[slot, which], dst_ref=wo_ref.at[slot, which], send_sem=fsend.at[k],
                recv_sem=frecv.at[k], device_id=(x, y, 1 - c), device_id_type=MESH)

        def small_copy(k, slot):
            return pltpu.make_async_remote_copy(
                src_ref=s_ref, dst_ref=so_ref.at[slot], send_sem=ssend.at[k], recv_sem=srecv.at[k],
                device_id=(peers[k][0], peers[k][1], c), device_id_type=MESH)

        local = pltpu.make_async_copy(s_ref, so_ref.at[mine], loc.at[0])
        sends = [fetch(k, mine) for k in range(3)] + [small_copy(k, mine) for k in range(3)]
        local.start()
        for cp in sends:
            cp.start()
        forwards = []
        for k in range(3):
            fetch(k, 2 * peers[k][0] + peers[k][1]).wait_recv()
            forwards.append(forward(k, c))
            forwards[-1].start()
        for k in range(3):
            forward(k, 1 - c).wait_recv()
            small_copy(k, 2 * peers[k][0] + peers[k][1]).wait_recv()
        for cp in sends + forwards:
            cp.wait_send()
        local.wait()

    three = pltpu.SemaphoreType.DMA((3,))
    return pl.pallas_call(
        body, name="ag_weights", in_specs=[_ANY] * 2, out_specs=[_ANY] * 2,
        out_shape=[jax.ShapeDtypeStruct((N_CHIPS, 2, half, w.shape[1]), w.dtype),
                   jax.ShapeDtypeStruct((N_CHIPS,) + small.shape, small.dtype)],
        scratch_shapes=[three, three, three, three, three, three, pltpu.SemaphoreType.DMA((1,))],
    )(w, small)


def _chip_allgather(arrays):
    n = len(arrays)

    def body(*refs):
        local, sends, recvs = _allgather_copies(refs[:n], refs[n:2 * n], *refs[2 * n:])
        for cp in local + sends:
            cp.start()
        for cp in recvs:
            cp.wait_recv()
        for cp in sends:
            cp.wait_send()
        for cp in local:
            cp.wait()

    return pl.pallas_call(
        body, name="ag_weights", in_specs=[_ANY] * n, out_specs=[_ANY] * n,
        out_shape=[jax.ShapeDtypeStruct((N_CHIPS,) + a.shape, a.dtype) for a in arrays],
        scratch_shapes=_allgather_semaphores(n),
    )(*arrays)


def _sibling_halves(grads):
    n = len(grads)

    def body(*refs):
        sends, recvs = _sibling_half_copies(refs[:n], refs[n:2 * n], *refs[2 * n:])
        for cp in sends:
            cp.start()
        for cp in recvs:
            cp.wait_recv()
        for cp in sends:
            cp.wait_send()

    return pl.pallas_call(
        body, name="rs_sibling", in_specs=[_ANY] * n, out_specs=[_ANY] * n,
        out_shape=_sibling_half_shapes(grads), scratch_shapes=_sibling_half_semaphores(n),
    )(*grads)


def _sibling_half_shapes(grads):
    return [jax.ShapeDtypeStruct((N_CHIPS, g.shape[1] // 2, g.shape[2]), g.dtype) for g in grads]


def _sibling_half_semaphores(n):
    return [pltpu.SemaphoreType.DMA((n,)), pltpu.SemaphoreType.DMA((n,))] if n else []


def _sibling_half_copies(ins, outs, send, recv):
    x, y, c = _place()

    def half_copy(a, which):
        half = ins[a].shape[1] // 2
        return pltpu.make_async_remote_copy(
            src_ref=ins[a].at[pl.ds(0, N_CHIPS), pl.ds(which * half, half)], dst_ref=outs[a],
            send_sem=send.at[a], recv_sem=recv.at[a], device_id=(x, y, 1 - c), device_id_type=MESH)

    return [half_copy(a, 1 - c) for a in range(len(ins))], [half_copy(a, c) for a in range(len(ins))]


def _scatter_shapes(parts):
    return [jax.ShapeDtypeStruct((3,) + p.shape[1:], p.dtype) for p in parts]


def _scatter_semaphores(n):
    return [pltpu.SemaphoreType.DMA((3 * n,)), pltpu.SemaphoreType.DMA((3 * n,))] if n else []


def _scatter_copies(ins, outs, send, recv):
    x, y, c = _place()
    peers = _other_chips(x, y)
    return [pltpu.make_async_remote_copy(
        src_ref=ins[a].at[2 * peers[k][0] + peers[k][1]], dst_ref=outs[a].at[k], send_sem=send.at[3 * a + k],
        recv_sem=recv.at[3 * a + k], device_id=(peers[k][0], peers[k][1], c), device_id_type=MESH)
        for a in range(len(ins)) for k in range(3)]


def _sibling_allgather(bufs, small):
    n = len(bufs)

    def body(*refs):
        small_in, outs, small_out = refs[n], refs[n + 1:2 * n + 1], refs[2 * n + 1]
        send, recv, s_send, s_recv, loc = refs[2 * n + 2:]
        x, y, c = _place()
        me = 4 * x + 2 * y + c

        def remote(a, which):
            return pltpu.make_async_remote_copy(
                src_ref=outs[a].at[which], dst_ref=outs[a].at[which], send_sem=send.at[a], recv_sem=recv.at[a],
                device_id=(x, y, 1 - c), device_id_type=MESH)

        def peer_of(r):
            return tuple(1 - v if (r >> b) & 1 else v for v, b in ((x, 2), (y, 1), (c, 0)))

        def small_copy(r, slot):
            return pltpu.make_async_remote_copy(
                src_ref=small_in, dst_ref=small_out.at[slot], send_sem=s_send.at[r - 1], recv_sem=s_recv.at[r - 1],
                device_id=peer_of(r), device_id_type=MESH)

        local = pltpu.make_async_copy(small_in, small_out.at[me], loc.at[0])
        sends = [remote(a, c) for a in range(n)] + [small_copy(r, me) for r in range(1, N_DEV)]
        local.start()
        for cp in sends:
            cp.start()
        for r in range(1, N_DEV):
            px, py, pc = peer_of(r)
            small_copy(r, 4 * px + 2 * py + pc).wait_recv()
        for a in range(n):
            remote(a, 1 - c).wait_recv()
        for cp in sends:
            cp.wait_send()
        local.wait()

    outs = pl.pallas_call(
        body, name="ag_sibling", in_specs=[_ANY] * (n + 1), out_specs=[_ANY] * (n + 1),
        out_shape=[jax.ShapeDtypeStruct(b.shape, b.dtype) for b in bufs]
        + [jax.ShapeDtypeStruct((N_DEV,) + small.shape, small.dtype)],
        input_output_aliases={a: a for a in range(n)},
        scratch_shapes=[pltpu.SemaphoreType.DMA((n,)), pltpu.SemaphoreType.DMA((n,)),
                        pltpu.SemaphoreType.DMA((N_DEV - 1,)), pltpu.SemaphoreType.DMA((N_DEV - 1,)),
                        pltpu.SemaphoreType.DMA((1,))],
    )(*bufs, small)
    return [o.reshape(2 * o.shape[1], o.shape[2]) for o in outs[:n]], outs[n]


def _pair_add(full, recv, core, name):
    _, R, C = full.shape
    half = R // 2

    def body(core_ref, a_ref, b_ref, o_ref):
        o_ref[...] = (a_ref[...] + b_ref[...]).astype(BF16)

    return pl.pallas_call(
        body, name=name,
        grid_spec=pltpu.PrefetchScalarGridSpec(
            num_scalar_prefetch=1, grid=(N_CHIPS,),
            in_specs=[pl.BlockSpec((1, half, C), lambda j, core_ref: (j, core_ref[0], 0)),
                      pl.BlockSpec((1, half, C), lambda j, core_ref: (j, 0, 0))],
            out_specs=pl.BlockSpec((1, half, C), lambda j, core_ref: (j, 0, 0))),
        out_shape=jax.ShapeDtypeStruct((N_CHIPS, half, C), BF16),
        compiler_params=_params(("parallel",)),
    )(core, full, recv)


def _sum_partials(own_all, recv, place, name, tiles=2):
    _, R, C = own_all.shape
    tr = R // tiles

    def body(place_ref, own_ref, r_ref, o_ref):
        acc = own_ref[0].astype(F32)
        for k in range(3):
            acc = acc + r_ref[k].astype(F32)
        o_ref[0] = acc

    return pl.pallas_call(
        body, name=name,
        grid_spec=pltpu.PrefetchScalarGridSpec(
            num_scalar_prefetch=1, grid=(tiles,),
            in_specs=[pl.BlockSpec((1, tr, C), lambda i, place_ref: (place_ref[0], i, 0)),
                      pl.BlockSpec((3, tr, C), lambda i, place_ref: (0, i, 0))],
            out_specs=pl.BlockSpec((1, tr, C), lambda i, place_ref: (place_ref[1], i, 0))),
        out_shape=jax.ShapeDtypeStruct((2, R, C), F32),
        compiler_params=_params(("parallel",)),
    )(place, own_all, recv)


def _adamw_math(w, g, m, v):
    m2 = ADAM_B1 * m + (1.0 - ADAM_B1) * g
    v2 = ADAM_B2 * v + (1.0 - ADAM_B2) * (g * g)
    m_hat = m2 / (1.0 - ADAM_B1 ** ADAM_STEP)
    v_hat = v2 / (1.0 - ADAM_B2 ** ADAM_STEP)
    return -ADAM_LR * (m_hat / (jnp.sqrt(v_hat) + ADAM_EPS) + ADAM_WD * w), m2, v2


ROW_ATTN_G, ROW_FFN_G, ROW_FINAL_G, ROW_MISC, ROW_CONV_B, ROW_CONV_W, ROW_META, SMALL_ROWS = 0, 1, 2, 3, 4, 8, 24, 40
MISC_FOX_B, MISC_LOSS = 512, 640


def _small_pack(out):
    def rows(a, n):
        a = a.astype(F32)
        return jnp.pad(a, ((0, n - a.shape[0]), (0, D_MODEL - a.shape[1])))

    misc = jnp.concatenate([out["ret_g"], out["fox_b"], jnp.zeros((1, MISC_LOSS - MISC_FOX_B - FOX_HEADS), F32),
                            out["loss"].reshape(1, 1)], axis=1)
    conv_b = jnp.pad(out["conv_b"], ((0, 0), (0, (-D_FF) % D_MODEL))).reshape(-1, D_MODEL)
    conv_w = out["conv_w"].reshape(3, N_CHIPS, -1).transpose(1, 0, 2).reshape(3 * N_CHIPS, -1)
    return jnp.concatenate([
        rows(out["attn_g"], 1), rows(out["ffn_g"], 1), rows(out["final_g"], 1), rows(misc, 1),
        rows(conv_b, ROW_CONV_W - ROW_CONV_B), rows(conv_w, ROW_META - ROW_CONV_W), rows(out["dmeta"], N_META)], axis=0)


def _small_update(packs, chip, ws, ms, vs):
    n = len(ws)
    meta_w, conv_sw = ws[0].shape[1], ws[5].shape[2]
    assert packs.shape == (N_DEV, SMALL_ROWS, D_MODEL) and ws[0].shape[0] == N_META and ws[5].shape[:2] == (1, 3)

    def body(chip_ref, p_ref, *refs):
        w_refs, m_refs, v_refs = refs[:n], refs[n:2 * n], refs[2 * n:3 * n]
        loss_ref, out_refs, tot = refs[3 * n], refs[3 * n + 1:7 * n + 1], refs[7 * n + 1]
        acc = p_ref[0]
        for d in range(1, N_DEV):
            acc = acc + p_ref[d]
        tot[...] = acc

        def of_chip(pieces):
            val = pieces[-1]
            for j in range(N_CHIPS - 2, -1, -1):
                val = jnp.where(chip_ref[0] == j, pieces[j], val)
            return val

        row = lambda r, lo=0, hi=D_MODEL: tot[r:r + 1, lo:hi]
        grads = [
            of_chip([tot[ROW_META:ROW_META + N_META, j * meta_w:(j + 1) * meta_w] for j in range(N_CHIPS)]),
            row(ROW_ATTN_G), row(ROW_MISC, MISC_FOX_B, MISC_FOX_B + FOX_HEADS), row(ROW_MISC, 0, MISC_FOX_B),
            row(ROW_FFN_G),
            of_chip([tot[ROW_CONV_W + 3 * j:ROW_CONV_W + 3 * j + 3, 0:conv_sw] for j in range(N_CHIPS)]),
            jnp.concatenate([row(ROW_CONV_B), row(ROW_CONV_B + 1), row(ROW_CONV_B + 2, 0, D_FF - 2 * D_MODEL)], axis=1),
            row(ROW_FINAL_G)]
        loss_ref[...] = row(ROW_MISC, MISC_LOSS, MISC_LOSS + BLK)
        for k in range(n):
            at = (0,) if len(ws[k].shape) == 3 else (Ellipsis,)
            res = (grads[k],) + _adamw_math(w_refs[k][at], grads[k], m_refs[k][at], v_refs[k][at])
            for kind in range(4):
                out_refs[kind * n + k][at] = res[kind]

    res = pl.pallas_call(
        body, name="small_update",
        grid_spec=pltpu.PrefetchScalarGridSpec(
            num_scalar_prefetch=1, grid=(1,),
            in_specs=[_full(packs.shape)] + [_full(a.shape) for a in list(ws) * 3],
            out_specs=[_full((1, BLK))] + [_full(a.shape) for a in list(ws) * 4],
            scratch_shapes=[pltpu.VMEM((SMALL_ROWS, D_MODEL), F32)]),
        out_shape=[jax.ShapeDtypeStruct((1, BLK), F32)] + [jax.ShapeDtypeStruct(a.shape, F32) for a in list(ws) * 4],
        compiler_params=_params(("arbitrary",)),
    )(chip, packs, *ws, *ms, *vs)
    return res[0], res[1:n + 1], res[n + 1:2 * n + 1], res[2 * n + 1:3 * n + 1], res[3 * n + 1:]


def _adamw(w, g, m, v, name, tiles=4):
    R, tail = w.shape[0], w.shape[1:]
    assert R % tiles == 0
    tr = R // tiles

    def body(w_ref, g_ref, m_ref, v_ref, go_ref, d_ref, m2_ref, v2_ref):
        g_ = g_ref[...]
        go_ref[...] = g_
        d_ref[...], m2_ref[...], v2_ref[...] = _adamw_math(w_ref[...], g_, m_ref[...], v_ref[...])

    spec = pl.BlockSpec((tr,) + tail, lambda i: (i,) + (0,) * len(tail))
    return pl.pallas_call(
        body, name=name, grid=(tiles,), in_specs=[spec] * 4, out_specs=[spec] * 4,
        out_shape=[jax.ShapeDtypeStruct(w.shape, F32)] * 4,
        compiler_params=_params(("parallel",)),
    )(w, g, m, v)


def _row_vector_tiles(n, most=80):
    return next(t for t in range(1, n + 1) if n % t == 0 and n // t <= most)


def _pack_rows(pieces, rows):
    flat = jnp.concatenate([jnp.pad(p.reshape(-1).astype(F32), (0, (-p.size) % D_MODEL)) for p in pieces])
    return jnp.pad(flat, (0, rows * D_MODEL - flat.size)).reshape(rows, D_MODEL)


def _unpack_rows(pack, shapes):
    flat = pack.reshape(-1)
    out, off = [], 0
    for shp in shapes:
        size = int(np.prod(shp))
        out.append(flat[off:off + size].reshape(shp))
        off += size + (-size) % D_MODEL
    return out


IN_PADDED = IN_WIDTH + (-IN_WIDTH) % BLK


def _fox_column_blocks():
    return [(RET_W + part * 512 + p * BLK, RET_W + 384 * p + part * BLK)
            for part in range(3) for p in range(FOX_HEADS // 2)]


def _w_in_kernel_order(gathered, own, chip):
    n, R, C = gathered.shape
    tr = R // 4

    def body(chip_ref, g_ref, own_ref, wm_ref, wf_ref, full):
        for j in range(n):
            @pl.when(chip_ref[0] == j)
            def _(j=j):
                full[:, j * C:(j + 1) * C] = own_ref[...]

            @pl.when(chip_ref[0] != j)
            def _(j=j):
                full[:, j * C:(j + 1) * C] = g_ref[j]

        full[:, n * C:] = jnp.zeros((tr, IN_PADDED - n * C), BF16)
        wm_ref[:, 0:RET_W] = full[:, 0:RET_W]
        for src, dst in _fox_column_blocks():
            wm_ref[:, dst:dst + BLK] = full[:, src:src + BLK]
        wf_ref[...] = full[:, MAIN_W:MAIN_W + BLK]

    return pl.pallas_call(
        body, name="w_in_kernel_order",
        grid_spec=pltpu.PrefetchScalarGridSpec(
            num_scalar_prefetch=1, grid=(R // tr,),
            in_specs=[pl.BlockSpec((n, tr, C), lambda i, c: (0, i, 0)), pl.BlockSpec((tr, C), lambda i, c: (i, 0))],
            out_specs=[pl.BlockSpec((tr, MAIN_W), lambda i, c: (i, 0)), pl.BlockSpec((tr, BLK), lambda i, c: (i, 0))],
            scratch_shapes=[pltpu.VMEM((tr, IN_PADDED), BF16)]),
        out_shape=[jax.ShapeDtypeStruct((R, MAIN_W), BF16), jax.ShapeDtypeStruct((R, BLK), BF16)],
        compiler_params=_params(("arbitrary",)),
    )(chip, gathered, own)


def _w_in_grad_shards(g_ret, g_fox, g_ff):
    R = g_ret.shape[0]
    C = IN_WIDTH // N_CHIPS
    tr = R // 4

    def body(gr_ref, gx_ref, gf_ref, o_ref, full):
        full[:, 0:RET_W] = gr_ref[...]
        for src, dst in _fox_column_blocks():
            full[:, src:src + BLK] = gx_ref[:, dst - RET_W:dst - RET_W + BLK]
        full[:, MAIN_W:MAIN_W + FOX_HEADS] = gf_ref[...]
        for j in range(N_CHIPS):
            o_ref[j] = full[:, j * C:(j + 1) * C]

    rows = lambda w: pl.BlockSpec((tr, w), lambda i: (i, 0))
    return pl.pallas_call(
        body, name="w_in_grad_shards", grid=(R // tr,),
        in_specs=[rows(RET_W), rows(FOX_W), rows(FOX_HEADS)],
        out_specs=pl.BlockSpec((N_CHIPS, tr, C), lambda i: (0, i, 0)),
        out_shape=jax.ShapeDtypeStruct((N_CHIPS, R, C), F32),
        scratch_shapes=[pltpu.VMEM((tr, IN_PADDED), F32)],
        compiler_params=_params(("parallel",)),
    )(g_ret, g_fox, g_ff)


def kernel(x, meta_tokens, attn_norm_g, w_in, fox_forget_b, ret_norm_g, w_out, ffn_norm_g, w_up, conv_w, conv_b, w_down, final_norm_g, loss_target, m_meta_tokens, m_attn_norm_g, m_w_in, m_fox_forget_b, m_ret_norm_g, m_w_out, m_ffn_norm_g, m_w_up, m_conv_w, m_conv_b, m_w_down, m_final_norm_g, v_meta_tokens, v_attn_norm_g, v_w_in, v_fox_forget_b, v_ret_norm_g, v_w_out, v_ffn_norm_g, v_w_up, v_conv_w, v_conv_b, v_w_down, v_final_norm_g):
    chip = 2 * lax.axis_index("x") + lax.axis_index("y")
    core = lax.axis_index("c")

    small_w = _pack_rows([meta_tokens, conv_w[0]], 8)
    w_in_b = w_in[0].astype(BF16)
    g_in, g_small = _chip_allgather_halves(w_in_b, small_w)
    chip_idx = chip.reshape(1).astype(jnp.int32)
    w_main, w_ff = _w_in_kernel_order(g_in.reshape((N_CHIPS,) + w_in_b.shape), w_in_b, chip_idx)
    small_parts = [_unpack_rows(g_small[j], [meta_tokens.shape, conv_w.shape[1:]]) for j in range(N_CHIPS)]
    meta_full = jnp.concatenate([sp[0] for sp in small_parts], axis=1)
    conv_w_full = jnp.concatenate([sp[1] for sp in small_parts], axis=1)

    core_idx = core.reshape(1).astype(jnp.int32)
    place = jnp.stack([chip, core]).astype(jnp.int32)

    def assemble(gathered):
        g_out, g_up, g_down = gathered
        return g_out.reshape(D_MODEL, D_MODEL), g_up, g_down.reshape(D_FF, D_MODEL)

    def early_arrays(d_w_out, d_w_up, d_w_down):
        return [d_w_out.reshape(N_CHIPS, -1, D_MODEL), d_w_up, d_w_down.reshape(N_CHIPS, -1, D_MODEL)]

    def in_sums(d_w_ret, d_w_fox, d_w_ff):
        g_in_full = _w_in_grad_shards(d_w_ret, d_w_fox, d_w_ff)
        (from_sib,) = _sibling_halves([g_in_full])
        return [_pair_add(g_in_full, from_sib, core_idx, "pair_add_in")]

    def early_sums(early, from_sib):
        return [_pair_add(g, r, core_idx, "pair_add_" + nm) for g, r, nm in zip(early, from_sib, ("out", "up", "down"))]

    out = _local_step(x[0], loss_target[0], meta_full, attn_norm_g, w_main, w_ff, fox_forget_b, ret_norm_g,
                      None, ffn_norm_g, None, conv_w_full, conv_b, None, final_norm_g[None],
                      late=([w_out[0].astype(BF16), w_up[0].astype(BF16), w_down[0].astype(BF16)], assemble),
                      mid=(early_arrays, early_sums), last=in_sums)

    names = ("in", "out", "up", "down")
    totals = [_sum_partials(s, q, place, "sum_chips_" + nm) for s, q, nm in zip(out["scatter"], out["received"], names)]
    (grad_in, grad_out, grad_up, grad_down), small_all = _sibling_allgather(totals, _small_pack(out))

    big_w = [(w_out, m_w_out, v_w_out, grad_out, "adamw_out"), (w_up, m_w_up, v_w_up, grad_up, "adamw_up"),
             (w_down, m_w_down, v_w_down, grad_down, "adamw_down")]
    big_res = [[r[None] for r in _adamw(w[0], g, m[0], v[0], nm)] for w, m, v, g, nm in big_w]
    as_rows = lambda a: jnp.transpose(a, (2, 0, 1))
    in_rows = _adamw(as_rows(w_in), grad_in.T[:, None, :], as_rows(m_w_in), as_rows(v_w_in), "adamw_in",
                     tiles=_row_vector_tiles(w_in.shape[2]))
    big_res.insert(0, [jnp.transpose(r, (1, 2, 0)) for r in in_rows])
    small_p = [meta_tokens, attn_norm_g, fox_forget_b, ret_norm_g, ffn_norm_g, conv_w, conv_b, final_norm_g[None]]
    small_m = [m_meta_tokens, m_attn_norm_g, m_fox_forget_b, m_ret_norm_g, m_ffn_norm_g, m_conv_w, m_conv_b, m_final_norm_g[None]]
    small_v = [v_meta_tokens, v_attn_norm_g, v_fox_forget_b, v_ret_norm_g, v_ffn_norm_g, v_conv_w, v_conv_b, v_final_norm_g[None]]
    loss_row, *small_res = _small_update(small_all, chip_idx, small_p, small_m, small_v)
    loss = loss_row[0, 0]

    def ordered(kind):
        sm = list(small_res[kind][:-1]) + [small_res[kind][-1][0]]
        bg = [r[kind] for r in big_res]
        return [sm[0], sm[1], bg[0], sm[2], sm[3], bg[1], sm[4], bg[2], sm[5], sm[6], bg[3], sm[7]]

    return (loss, out["dx"][None], *ordered(0), *ordered(1), *ordered(2), *ordered(3))
```

```python
import functools

import numpy as np
import jax
import jax.numpy as jnp
from jax import lax
from jax.experimental import pallas as pl
from jax.experimental.pallas import tpu as pltpu

F32 = jnp.float32
BF16 = jnp.bfloat16

D_MODEL = 1024
N_META = 16
BLK = 128
UNIT = 2 * BLK
FOX_PAIRS = 2
WIDE = 4
CHUNK = 64
N_PAD = BLK - N_META
PREFIX = BLK
RET_HEADS = 4
FOX_HEADS = 8
HEAD_LANES = 64
D_FF = 2816
ROPE_BASE = 10000.0
EPS = 1e-6
NEG = -1e30
LOG2E = 1.4426950408889634
RET_W = 1536
FOX_W = 1536
MAIN_W = RET_W + FOX_W
IN_WIDTH = MAIN_W + FOX_HEADS
N_CHIPS = 4
N_DEV = 8

ADAM_LR = 0.001
ADAM_B1 = 0.9
ADAM_B2 = 0.999
ADAM_EPS = 1e-08
ADAM_WD = 0.01
ADAM_STEP = 10

MESH = pl.DeviceIdType.MESH
VMEM_LIMIT_MB = 56

_NT = (((1,), (1,)), ((), ()))
_TN = (((0,), (0,)), ((), ()))


def _dot(a, b):
    return jnp.dot(a, b, preferred_element_type=F32)


def _dot_nt(a, b):
    return lax.dot_general(a, b, _NT, preferred_element_type=F32)


def _dot_tn(a, b):
    return lax.dot_general(a, b, _TN, preferred_element_type=F32)


def _params(dims=None, vmem_mb=VMEM_LIMIT_MB):
    kw = dict(vmem_limit_bytes=vmem_mb << 20)
    if dims is not None:
        kw["dimension_semantics"] = dims
    return pltpu.CompilerParams(**kw)


def _row_tile(n, prefs=(384, 256, 128)):
    for t in prefs:
        if n % t == 0:
            return t
    raise ValueError(f"no row tile for {n}")


def _iota(shape, dim):
    return lax.broadcasted_iota(jnp.int32, shape, dim)


def _pick_row(tile, row):
    sub = _iota(tile.shape, 0)
    return jnp.sum(jnp.where(sub == row, tile, 0.0), axis=0, keepdims=True)


def _split3(x):
    hi = x.astype(BF16)
    r1 = x - hi.astype(F32)
    mid = r1.astype(BF16)
    lo = (r1 - mid.astype(F32)).astype(BF16)
    return hi, mid, lo


def _full(shape):
    nd = len(shape)
    return pl.BlockSpec(shape, lambda *_: (0,) * nd)


def _in_perm():
    cols = list(range(RET_W))
    for p in range(FOX_HEADS // 2):
        for part in range(3):
            start = RET_W + part * 512 + p * BLK
            cols += list(range(start, start + BLK))
    return np.asarray(cols, np.int32)


def _rotary_tables(L):
    half = HEAD_LANES // 2
    inv = 1.0 / (ROPE_BASE ** (jnp.arange(half, dtype=F32) / half))
    ang = jnp.arange(L).astype(F32)[:, None] * inv[None, :]
    cos, sin = jnp.cos(ang), jnp.sin(ang)
    cos_t = jnp.tile(cos, (1, 4))
    sin_t = jnp.tile(jnp.concatenate([-sin, sin], axis=1), (1, 2))
    return cos_t, sin_t


def _decay_tables():
    gam = 1.0 - 2.0 ** (-5.0 - np.arange(RET_HEADS, dtype=np.float64))
    n = np.arange(BLK)
    same_or_past = (n[:, None] // CHUNK) >= (n[None, :] // CHUNK)
    dist = np.abs(n[:, None] - n[None, :])
    dmat = np.stack([np.where(same_or_past, g ** dist, 0.0) for g in gam]).astype(np.float32)
    lane_head = np.arange(BLK) // HEAD_LANES
    wq = np.stack([gam[2 * p + lane_head][None, :] ** (n[:, None] + 1.0) for p in range(2)]).astype(np.float32)
    wk = np.stack([gam[2 * p + lane_head][None, :] ** (BLK - 1.0 - n[:, None]) for p in range(2)]).astype(np.float32)
    g_blk = tuple(float(g ** BLK) for g in gam)
    return jnp.asarray(dmat), jnp.asarray(wq), jnp.asarray(wk), g_blk


def _shifted_blocks(tm):
    nb = tm // BLK
    return [pl.BlockSpec((BLK, D_MODEL), lambda i, j=j: (jnp.maximum(nb * i + j - 1, 0), 0)) for j in range(nb)]


def _rms_inproj(head, x, g, w_main, w_ff):
    L = x.shape[0] + BLK
    tm = _row_tile(L)
    nb = tm // BLK

    def body(head_ref, *refs):
        x_refs, (g_ref, wm_ref, wf_ref, h_ref, n_ref, p_ref, ff_ref) = refs[:nb], refs[nb:]
        parts = [r[...] for r in x_refs]
        parts[0] = jnp.where(pl.program_id(0) == 0, head_ref[...], parts[0])
        h = jnp.concatenate(parts, axis=0)
        h_ref[...] = h
        r = lax.rsqrt(jnp.mean(h * h, axis=-1, keepdims=True) + EPS)
        n = (h * r * g_ref[...]).astype(BF16)
        n_ref[...] = n
        p_ref[...] = _dot(n, wm_ref[...]).astype(BF16)
        ff_ref[...] = _dot(n, wf_ref[...])

    rows = lambda w: pl.BlockSpec((tm, w), lambda i: (i, 0))
    return pl.pallas_call(
        body, name="f_inproj", grid=(L // tm,),
        in_specs=[_full((BLK, D_MODEL))] + _shifted_blocks(tm)
        + [_full((1, D_MODEL)), _full((D_MODEL, MAIN_W)), _full((D_MODEL, BLK))],
        out_specs=[rows(D_MODEL), rows(D_MODEL), rows(MAIN_W), rows(BLK)],
        out_shape=[jax.ShapeDtypeStruct((L, D_MODEL), F32), jax.ShapeDtypeStruct((L, D_MODEL), BF16),
                   jax.ShapeDtypeStruct((L, MAIN_W), BF16), jax.ShapeDtypeStruct((L, BLK), F32)],
        compiler_params=_params(("parallel",)),
    )(head, *([x] * nb), g, w_main, w_ff)


def _block_group(nblk):
    return 3 if nblk % 3 == 0 else 1


def _fox_prep(ff, fb):
    L = ff.shape[0]
    nblk = L // BLK
    G = _block_group(nblk)

    def body(ff_ref, b_ref, c_ref, ct_ref, carry):
        @pl.when(pl.program_id(0) == 0)
        def _():
            carry[...] = jnp.zeros_like(carry)

        tri = (_iota((BLK, BLK), 0) >= _iota((BLK, BLK), 1)).astype(BF16)
        live = _iota((BLK, BLK), 1) < FOX_HEADS
        run = carry[...]
        for b in range(G):
            z = ff_ref[b * BLK:(b + 1) * BLK, :] + b_ref[...]
            lf = jnp.where(live, jnp.minimum(z, 0.0) - jnp.log1p(jnp.exp(-jnp.abs(z))), 0.0)
            hi, mid, lo = _split3(lf)
            cs = (_dot(tri, hi) + _dot(tri, mid) + _dot(tri, lo) + run) * LOG2E
            c_ref[b * BLK:(b + 1) * BLK, :] = cs
            ct_ref[b] = cs.T[0:8, :]
            run = run + jnp.sum(lf, axis=0, keepdims=True)
        carry[...] = run

    return pl.pallas_call(
        body, name="f_foxprep", grid=(nblk // G,),
        in_specs=[pl.BlockSpec((G * BLK, BLK), lambda i: (i, 0)), _full((1, BLK))],
        out_specs=[pl.BlockSpec((G * BLK, BLK), lambda i: (i, 0)), pl.BlockSpec((G, 8, BLK), lambda i: (i, 0, 0))],
        out_shape=[jax.ShapeDtypeStruct((L, BLK), F32), jax.ShapeDtypeStruct((nblk, 8, BLK), F32)],
        scratch_shapes=[pltpu.VMEM((1, BLK), F32)],
        compiler_params=_params(("arbitrary",)),
    )(ff, fb)


def _rot_fns(cos, sin):
    lane = _iota((BLK, BLK), 1)
    first = (lane & (HEAD_LANES - 1)) < HEAD_LANES // 2

    def swap(x):
        return jnp.where(first, pltpu.roll(x, BLK - 32, 1), pltpu.roll(x, 32, 1))

    def rot(x):
        return x * cos + swap(x) * sin

    def rot_t(dy):
        return dy * cos + swap(dy * sin)

    return rot, rot_t


def _retention_fwd(proj, cos_t, sin_t, ret_g):
    L = proj.shape[0]
    nblk = L // BLK
    G = _block_group(nblk)
    dmat, wq_t, wk_t, g_blk = _decay_tables()

    def body(q_ref, k_ref, v_ref, gate_ref, cos_ref, sin_ref, d_ref, wq_ref, wk_ref, rg_ref,
             mix_ref, o_ref, rs_ref, state):
        @pl.when(pl.program_id(0) == 0)
        def _():
            state[...] = jnp.zeros_like(state)

        lane = _iota((BLK, BLK), 1)
        sub = _iota((BLK, BLK), 0)
        for b in range(G):
            rows = slice(b * BLK, (b + 1) * BLK)
            rot, _ = _rot_fns(cos_ref[rows, :], sin_ref[rows, :])
            for p in range(2):
                qr = rot(q_ref[rows, p * BLK:(p + 1) * BLK].astype(F32))
                kr = rot(k_ref[rows, p * BLK:(p + 1) * BLK].astype(F32)) * (HEAD_LANES ** -0.5)
                kr_b = kr.astype(BF16)
                qw = (qr * wq_ref[p]).astype(BF16)
                kw = (kr * wk_ref[p]).astype(BF16)
                for e in range(2):
                    h = 2 * p + e
                    cols = slice(h * BLK, (h + 1) * BLK)
                    qm = jnp.where((lane >> 6) == e, qr, 0.0).astype(BF16)
                    s = _dot_nt(qm, kr_b) * d_ref[h]
                    vh = v_ref[rows, cols]
                    st = state[h]
                    rs_ref[b, h] = st
                    o = _dot(s.astype(BF16), vh) + _dot(qw, st.astype(BF16))
                    u = jnp.where((sub >> 6) == e, _dot_tn(kw, vh), 0.0)
                    state[h] = g_blk[h] * st + u
                    rn = lax.rsqrt(jnp.mean(o * o, axis=-1, keepdims=True) + EPS)
                    gate = gate_ref[rows, cols].astype(F32)
                    o_ref[rows, cols] = o
                    mix_ref[rows, cols] = (o * rn * rg_ref[:, cols] * (gate * jax.nn.sigmoid(gate))).astype(BF16)

    row = lambda c: (lambda i: (i, c))
    return pl.pallas_call(
        body, name="f_retention", grid=(nblk // G,),
        in_specs=[pl.BlockSpec((G * BLK, 256), row(0)), pl.BlockSpec((G * BLK, 256), row(1)),
                  pl.BlockSpec((G * BLK, 512), row(1)), pl.BlockSpec((G * BLK, 512), row(2)),
                  pl.BlockSpec((G * BLK, BLK), row(0)), pl.BlockSpec((G * BLK, BLK), row(0)),
                  _full((RET_HEADS, BLK, BLK)), _full((2, BLK, BLK)), _full((2, BLK, BLK)), _full((1, 512))],
        out_specs=[pl.BlockSpec((G * BLK, 512), row(0)), pl.BlockSpec((G * BLK, 512), row(0)),
                   pl.BlockSpec((G, RET_HEADS, BLK, BLK), lambda i: (i, 0, 0, 0))],
        out_shape=[jax.ShapeDtypeStruct((L, 512), BF16), jax.ShapeDtypeStruct((L, 512), F32),
                   jax.ShapeDtypeStruct((nblk, RET_HEADS, BLK, BLK), F32)],
        scratch_shapes=[pltpu.VMEM((RET_HEADS, BLK, BLK), F32)],
        compiler_params=_params(("arbitrary",)),
    )(proj, proj, proj, proj, cos_t, sin_t, dmat, wq_t, wk_t, ret_g)


def _fox_units(L):
    nblk = L // BLK
    assert L % BLK == 0 and nblk % 2 == 1, "sequence must be one 128-row block plus whole 256-row tiles"
    return nblk, (nblk - 1) // 2


def _fox_tile_masks():
    sub, lane = _iota((BLK, BLK), 0), _iota((BLK, BLK), 1)
    valid = _iota((BLK, UNIT), 0) >= N_PAD
    diag = _iota((UNIT, UNIT), 0) <= _iota((UNIT, UNIT), 1)
    r, q = _iota((BLK + UNIT, UNIT), 0), _iota((BLK + UNIT, UNIT), 1)
    first_and_diag = ((r < BLK) & (r >= N_PAD)) | ((r >= BLK) & (r - BLK <= q))
    return dict(first=(sub <= lane) & (sub >= N_PAD), valid=valid, diag=diag, first_and_diag=first_and_diag)


def _fox_fwd(proj, c, ctb, gather=()):
    L = proj.shape[0]
    nblk, nu = _fox_units(L)
    scale = HEAD_LANES ** -0.5 * LOG2E
    ng = len(gather)
    steps = FOX_HEADS // (2 * FOX_PAIRS)

    def body(qkv_ref, c_ref, ct_ref, *rest):
        g_in, (of_ref, lse_ref), g_out = rest[:ng], rest[ng:ng + 2], rest[ng + 2:2 * ng + 2]
        vt, csb = rest[2 * ng + 2:2 * ng + 4]
        p = pl.program_id(0)
        heads = [(pp, e, 2 * FOX_PAIRS * p + 2 * pp + e) for pp in range(FOX_PAIRS) for e in range(2)]

        @pl.when(p == 0)
        def _():
            lse_ref[...] = jnp.zeros_like(lse_ref)
            if ng:
                local, sends, _ = _allgather_copies(g_in, g_out, *rest[2 * ng + 4:])
                for cp in local + sends:
                    cp.start()

        lane = _iota((BLK, BLK), 1)
        sub8 = _iota((8, BLK), 0)
        masks = _fox_tile_masks()

        def pre(j, carry):
            off = pl.multiple_of(j * BLK, BLK)
            ct = c_ref[pl.ds(off, BLK), :]
            for pp in range(FOX_PAIRS):
                vt[pp, j] = qkv_ref[pl.ds(off, BLK), pp * 384 + 2 * BLK:pp * 384 + 3 * BLK].astype(F32).T.astype(BF16)
            for hh, (_, _, h) in enumerate(heads):
                col = jnp.sum(jnp.where(lane == h, ct, 0.0), axis=1, keepdims=True)
                csb[hh, j] = jnp.broadcast_to(col, (BLK, BLK))
            return carry

        lax.fori_loop(0, nblk, pre, 0)

        def attend(qblk, nq, n_whole):
            qlen = nq * BLK
            qoff = pl.multiple_of(qblk * BLK, BLK)
            qlane = _iota((qlen, BLK), 1)
            qs = [qkv_ref[pl.ds(qoff, qlen), pp * 384:pp * 384 + BLK].astype(F32) * scale for pp in range(FOX_PAIRS)]
            qm = [jnp.where((qlane >> 6) == e, qs[pp], 0.0).astype(BF16) for pp, e, _ in heads]
            ct_row = [jnp.concatenate([_pick_row(ct_ref[qblk + a], h) for a in range(nq)], axis=1) for _, _, h in heads]

            def step(segs, mask, st):
                blocks = [kblk + b for kblk, nk in segs for b in range(nk)]
                kts = []
                for pp in range(FOX_PAIRS):
                    kt = [qkv_ref[pl.ds(pl.multiple_of(kblk * BLK, BLK), nk * BLK), pp * 384 + BLK:pp * 384 + 2 * BLK]
                          for kblk, nk in segs]
                    kts.append(kt[0] if len(kt) == 1 else jnp.concatenate(kt, axis=0))
                out = []
                for hh, (pp, e, _) in enumerate(heads):
                    m, l, acc = st[3 * hh:3 * hh + 3]
                    s = _dot_nt(kts[pp], qm[hh])
                    t = jnp.concatenate([s[b * BLK:(b + 1) * BLK] - jnp.concatenate([csb[hh, blk]] * nq, axis=1)
                                         for b, blk in enumerate(blocks)], axis=0)
                    if mask is not None:
                        t = jnp.where(mask, t, NEG)
                    m_new = jnp.maximum(m, jnp.max(t, axis=0, keepdims=True) + ct_row[hh])
                    alpha = jnp.exp2(m - m_new)
                    pr = jnp.exp2(t - (m_new - ct_row[hh]))
                    l = alpha * l + jnp.sum(pr, axis=0, keepdims=True)
                    pr_b = pr.astype(BF16)
                    pv = None
                    for b, blk in enumerate(blocks):
                        part = _dot(vt[pp, blk, e * HEAD_LANES:(e + 1) * HEAD_LANES, :], pr_b[b * BLK:(b + 1) * BLK])
                        pv = part if pv is None else pv + part
                    out += [m_new, l, alpha * acc + pv]
                return tuple(out)

            st = (jnp.full((1, qlen), NEG, F32), jnp.zeros((1, qlen), F32),
                  jnp.zeros((HEAD_LANES, qlen), F32)) * len(heads)
            if nq == 1:
                st = step([(0, 1)], masks["first"], st)
            else:
                st = step([(0, 1), (qblk, 2)], masks["first_and_diag"], st)
                n_wide = n_whole // WIDE
                st = lax.fori_loop(0, n_wide, lambda j, s_: step([(1 + 2 * WIDE * j, 2 * WIDE)], None, s_), st)
                rest = 1 + 2 * WIDE * n_wide
                st = lax.cond((n_whole & 2) != 0, lambda s_: step([(rest, 4)], None, s_), lambda s_: s_, st)
                st = lax.cond((n_whole & 1) != 0, lambda s_: step([(rest + 2 * (n_whole & 2), 2)], None, s_),
                              lambda s_: s_, st)
            for pp in range(FOX_PAIRS):
                lo, hi = st[6 * pp:6 * pp + 3], st[6 * pp + 3:6 * pp + 6]
                o_t = jnp.concatenate([lo[2] * (1.0 / lo[1]), hi[2] * (1.0 / hi[1])], axis=0)
                of_ref[pl.ds(qoff, qlen), pp * BLK:(pp + 1) * BLK] = o_t.T.astype(BF16)
            lse = [st[3 * hh] + jnp.log(st[3 * hh + 1]) * LOG2E for hh in range(len(heads))]
            for a in range(nq):
                upd = jnp.zeros((8, BLK), F32)
                for hh, (_, _, h) in enumerate(heads):
                    upd = upd + jnp.where(sub8 == h, lse[hh][:, a * BLK:(a + 1) * BLK], 0.0)
                lse_ref[qblk + a] = lse_ref[qblk + a] + upd

        attend(0, 1, 0)

        def q_loop(u, carry):
            attend(1 + 2 * u, 2, u)
            return carry

        lax.fori_loop(0, nu, q_loop, 0)

        if ng:
            @pl.when(p == steps - 1)
            def _():
                local, sends, recvs = _allgather_copies(g_in, g_out, *rest[2 * ng + 4:])
                for cp in recvs:
                    cp.wait_recv()
                for cp in sends:
                    cp.wait_send()
                for cp in local:
                    cp.wait()

    width = 384 * FOX_PAIRS
    return pl.pallas_call(
        body, name="f_fox", grid=(steps,),
        in_specs=[pl.BlockSpec((L, width), lambda p: (0, RET_W // width + p)), _full((L, BLK)), _full((nblk, 8, BLK))]
        + [_ANY] * ng,
        out_specs=[pl.BlockSpec((L, FOX_PAIRS * BLK), lambda p: (0, p)), _full((nblk, 8, BLK))] + [_ANY] * ng,
        out_shape=[jax.ShapeDtypeStruct((L, 512), BF16), jax.ShapeDtypeStruct((nblk, 8, BLK), F32)]
        + [jax.ShapeDtypeStruct((N_CHIPS,) + a.shape, a.dtype) for a in gather],
        scratch_shapes=[pltpu.VMEM((FOX_PAIRS, nblk, BLK, BLK), BF16), pltpu.VMEM((2 * FOX_PAIRS, nblk, BLK, BLK), F32)]
        + _allgather_semaphores(ng),
        compiler_params=_params(("arbitrary",)),
    )(proj, c, ctb, *gather)


def _outproj_up(mix_r, o_f, h0, w_out, ffn_g, w_up, conv_w, conv_b):
    L = h0.shape[0]
    tm = _row_tile(L)
    shard = w_up.shape[2]
    assert 2 * shard == D_FF
    cw = [conv_w[j:j + 1] for j in range(3)]
    resident = lambda shape: pl.BlockSpec(shape, lambda i: (0,) * len(shape), pipeline_mode=pl.Buffered(1))

    def body(mr_ref, of_ref, h0_ref, wo_ref, g_ref, wu_ref, cw0, cw1, cw2, cb_ref,
             h1_ref, n2_ref, up_ref, act_ref, acc_ref, halo):
        i = pl.program_id(0)

        @pl.when(i == 0)
        def _():
            halo[...] = jnp.zeros_like(halo)

        h1 = h0_ref[...] + _dot(mr_ref[...], wo_ref[0:512, :]) + _dot(of_ref[...], wo_ref[512:1024, :])
        h1_ref[...] = h1
        r = lax.rsqrt(jnp.mean(h1 * h1, axis=-1, keepdims=True) + EPS)
        n2 = (h1 * r * g_ref[...]).astype(BF16)
        n2_ref[...] = n2
        live = i * tm + _iota((tm, 1), 0) >= N_PAD
        for half in range(2):
            cols = slice(half * shard, (half + 1) * shard)
            a_b = _dot(n2, wu_ref[half]).astype(BF16)
            b_b = _dot(n2, wu_ref[2 + half]).astype(BF16)
            up_ref[:, cols] = a_b
            up_ref[:, D_FF + half * shard:D_FF + (half + 1) * shard] = b_b
            a = jnp.where(live, a_b.astype(F32), 0.0)
            _, _, acc = _conv_taps(a, halo[:, cols], [cw0[:, cols], cw1[:, cols], cw2[:, cols]], cb_ref[:, cols])
            act_ref[:, cols] = (acc * jax.nn.sigmoid(acc) * b_b.astype(F32)).astype(BF16)
            acc_ref[:, cols] = acc.astype(BF16)
            halo[:, cols] = a[tm - 8:tm, :]

    rows = lambda w: pl.BlockSpec((tm, w), lambda i: (i, 0))
    return pl.pallas_call(
        body, name="f_outproj_up", grid=(L // tm,),
        in_specs=[rows(512), rows(512), rows(D_MODEL), resident((D_MODEL, D_MODEL)), _full((1, D_MODEL)),
                  resident((N_CHIPS, D_MODEL, shard)), _full((1, D_FF)), _full((1, D_FF)), _full((1, D_FF)),
                  _full((1, D_FF))],
        out_specs=[rows(D_MODEL), rows(D_MODEL), rows(2 * D_FF), rows(D_FF), rows(D_FF)],
        out_shape=[jax.ShapeDtypeStruct((L, D_MODEL), F32), jax.ShapeDtypeStruct((L, D_MODEL), BF16),
                   jax.ShapeDtypeStruct((L, 2 * D_FF), BF16), jax.ShapeDtypeStruct((L, D_FF), BF16),
                   jax.ShapeDtypeStruct((L, D_FF), BF16)],
        scratch_shapes=[pltpu.VMEM((8, D_FF), F32)],
        compiler_params=_params(("arbitrary",)),
    )(mix_r, o_f, h0, w_out, ffn_g, w_up, cw[0], cw[1], cw[2], conv_b)


def _conv_taps(a, halo, cw, cb):
    sub = _iota((a.shape[0], 1), 0)
    a1 = jnp.where(sub == 0, _pick_row(halo, 7), pltpu.roll(a, 1, 0))
    a2 = jnp.where(sub == 0, _pick_row(halo, 6), jnp.where(sub == 1, _pick_row(halo, 7), pltpu.roll(a, 2, 0)))
    acc = cb + a2 * cw[0]
    acc = acc + a1 * cw[1]
    acc = acc + a * cw[2]
    return a1, a2, acc


def _ffn_down_loss(g_act, w_down, h1, final_g, target, acc_saved, up):
    L = h1.shape[0]
    tm = _row_tile(L)
    nb = tm // BLK
    half_w = D_FF // 2

    def body(g_ref, wd_ref, h1_ref, gf_ref, acc_ref, b_ref, *refs):
        t_refs, (dh_ref, dhb_ref, dgf_ref, loss_ref, dacc_ref, db_ref) = refs[:nb], refs[nb:]
        i = pl.program_id(0)

        @pl.when(i == 0)
        def _():
            dgf_ref[...] = jnp.zeros_like(dgf_ref)
            loss_ref[...] = jnp.zeros_like(loss_ref)

        h2 = h1_ref[...] + _dot(g_ref[...], wd_ref[...])
        r = lax.rsqrt(jnp.mean(h2 * h2, axis=-1, keepdims=True) + EPS)
        yn = h2 * r
        gf = gf_ref[...]
        live = i * tm + _iota((tm, 1), 0) >= PREFIX
        target = jnp.concatenate([t[...] for t in t_refs], axis=0)
        err = jnp.where(live, yn * gf - target, 0.0)
        loss_ref[...] = loss_ref[...] + 0.5 * jnp.sum(jnp.mean(err * err, axis=-1, keepdims=True))
        dy = err * (1.0 / D_MODEL)
        dgf_ref[...] = dgf_ref[...] + jnp.sum(dy * yn, axis=0, keepdims=True)
        dyn = dy * gf
        dh = r * (dyn - yn * jnp.mean(dyn * yn, axis=-1, keepdims=True))
        dh_ref[...] = dh
        dhb = dh.astype(BF16)
        dhb_ref[...] = dhb
        for half in range(2):
            cols = slice(half * half_w, (half + 1) * half_w)
            acc = acc_ref[:, cols].astype(F32)
            dg = _dot_nt(dhb, wd_ref[cols, :])
            sg = jax.nn.sigmoid(acc)
            silu = acc * sg
            db_ref[:, cols] = (dg * silu).astype(BF16)
            dacc_ref[:, cols] = (dg * b_ref[:, cols].astype(F32) * (sg + silu * (1.0 - sg))).astype(BF16)

    rows = lambda w, c=0: pl.BlockSpec((tm, w), lambda i: (i, c))
    return pl.pallas_call(
        body, name="f_ffn_down_loss", grid=(L // tm,),
        in_specs=[rows(D_FF), pl.BlockSpec((D_FF, D_MODEL), lambda i: (0, 0), pipeline_mode=pl.Buffered(1)),
                  rows(D_MODEL), _full((1, D_MODEL)), rows(D_FF), rows(D_FF, 1)] + _shifted_blocks(tm),
        out_specs=[rows(D_MODEL), rows(D_MODEL), _full((1, D_MODEL)), _full((1, BLK)), rows(D_FF), rows(D_FF)],
        out_shape=[jax.ShapeDtypeStruct((L, D_MODEL), F32), jax.ShapeDtypeStruct((L, D_MODEL), BF16),
                   jax.ShapeDtypeStruct((1, D_MODEL), F32), jax.ShapeDtypeStruct((1, BLK), F32),
                   jax.ShapeDtypeStruct((L, D_FF), BF16), jax.ShapeDtypeStruct((L, D_FF), BF16)],
        compiler_params=_params(("arbitrary",)),
    )(g_act, w_down, h1, final_g, acc_saved, up, *([target] * nb))


def _ffn_bwd_up(dacc, db, up, conv_w, w_up, h1, ffn_g, dh2, w_out):
    L = h1.shape[0]
    tm = _row_tile(L)
    nt = L // tm
    shard = w_up.shape[2]
    cw = [conv_w[j:j + 1] for j in range(3)]

    def body(da_ref, halo_ref, db_ref, a_ref, cw0, cw1, cw2, wu_ref, h1_ref, g_ref, dh2_ref, wo_ref,
             dup_ref, dh1_ref, dh1b_ref, dmix_ref, dg_ref, dcw_ref):
        i = pl.program_id(0)

        @pl.when(i == 0)
        def _():
            dg_ref[...] = jnp.zeros_like(dg_ref)
            dcw_ref[...] = jnp.zeros_like(dcw_ref)

        sub = _iota((tm, 1), 0)
        sub8 = _iota((8, 1), 0)
        last_tile = i == nt - 1
        dbv = db_ref[...]
        dup_ref[:, D_FF:2 * D_FF] = dbv
        dn = _dot_nt(dbv[:, 0:shard], wu_ref[2]) + _dot_nt(dbv[:, shard:2 * shard], wu_ref[3])
        for half in range(2):
            cols = slice(half * shard, (half + 1) * shard)
            d0 = da_ref[:, cols].astype(F32)
            halo = jnp.where(last_tile, 0.0, halo_ref[:, cols].astype(F32))
            d1 = jnp.where(sub == tm - 1, _pick_row(halo, 0), pltpu.roll(d0, tm - 1, 0))
            d2 = jnp.where(sub == tm - 2, _pick_row(halo, 0),
                           jnp.where(sub == tm - 1, _pick_row(halo, 1), pltpu.roll(d0, tm - 2, 0)))
            a = a_ref[:, cols].astype(F32)
            upd = jnp.zeros((8, shard), F32)
            for j, t in enumerate((d2 * a, d1 * a, d0 * a, d0)):
                upd = upd + jnp.where(sub8 == j, jnp.sum(t, axis=0, keepdims=True), 0.0)
            dcw_ref[:, cols] = dcw_ref[:, cols] + upd
            da = (d0 * cw2[:, cols] + d1 * cw1[:, cols] + d2 * cw0[:, cols]).astype(BF16)
            dup_ref[:, cols] = da
            dn = dn + _dot_nt(da, wu_ref[half])
        h1 = h1_ref[...]
        r = lax.rsqrt(jnp.mean(h1 * h1, axis=-1, keepdims=True) + EPS)
        yn = h1 * r
        dg_ref[...] = dg_ref[...] + jnp.sum(dn * yn, axis=0, keepdims=True)
        dyn = dn * g_ref[...]
        dh1 = dh2_ref[...] + r * (dyn - yn * jnp.mean(dyn * yn, axis=-1, keepdims=True))
        dh1_ref[...] = dh1
        dh1b = dh1.astype(BF16)
        dh1b_ref[...] = dh1b
        dmix_ref[...] = _dot_nt(dh1b, wo_ref[...]).astype(BF16)

    rows = lambda w: pl.BlockSpec((tm, w), lambda i: (i, 0))
    halo = pl.BlockSpec((8, D_FF), lambda i: (jnp.minimum((i + 1) * (tm // 8), L // 8 - 1), 0))
    return pl.pallas_call(
        body, name="b_ffn_up", grid=(nt,),
        in_specs=[rows(D_FF), halo, rows(D_FF), rows(D_FF), _full((1, D_FF)), _full((1, D_FF)), _full((1, D_FF)),
                  _full((N_CHIPS, D_MODEL, shard)), rows(D_MODEL), _full((1, D_MODEL)), rows(D_MODEL),
                  _full((D_MODEL, D_MODEL))],
        out_specs=[rows(2 * D_FF), rows(D_MODEL), rows(D_MODEL), rows(D_MODEL), _full((1, D_MODEL)),
                   _full((8, D_FF))],
        out_shape=[jax.ShapeDtypeStruct((L, 2 * D_FF), BF16), jax.ShapeDtypeStruct((L, D_MODEL), F32),
                   jax.ShapeDtypeStruct((L, D_MODEL), BF16), jax.ShapeDtypeStruct((L, D_MODEL), BF16),
                   jax.ShapeDtypeStruct((1, D_MODEL), F32), jax.ShapeDtypeStruct((8, D_FF), F32)],
        compiler_params=_params(("arbitrary",)),
    )(dacc, dacc, db, up, cw[0], cw[1], cw[2], w_up, h1, ffn_g, dh2, w_out)


def _wgrad(a, b, name, tn=None, tk=None, out_dtype=F32):
    L, K = a.shape
    N = b.shape[1]
    tn = N if tn is None else tn
    tk = K if tk is None else tk
    tl = _row_tile(L, (704, 768, 512, 256, 128))
    nl = L // tl

    def body(a_ref, b_ref, o_ref, acc):
        step = pl.program_id(2)

        @pl.when(step == 0)
        def _():
            acc[...] = jnp.zeros_like(acc)

        acc[...] = acc[...] + _dot_tn(a_ref[...], b_ref[...])

        @pl.when(step == nl - 1)
        def _():
            o_ref[0] = acc[...].astype(out_dtype)

    return pl.pallas_call(
        body, name=name, grid=(N // tn, K // tk, L // tl),
        in_specs=[pl.BlockSpec((tl, tk), lambda n, k, l: (l, k)), pl.BlockSpec((tl, tn), lambda n, k, l: (l, n))],
        out_specs=pl.BlockSpec((1, tk, tn), lambda n, k, l: (n, k, 0)),
        out_shape=jax.ShapeDtypeStruct((N // tn, K, tn), out_dtype),
        scratch_shapes=[pltpu.VMEM((tk, tn), F32)],
        compiler_params=_params(("parallel", "parallel", "arbitrary")),
    )(a, b)


def _retention_bwd(dmix, o, proj, cos_t, sin_t, ret_g, states, exchange=()):
    L = proj.shape[0]
    nblk = L // BLK
    G = _block_group(nblk)
    steps = nblk // G
    nx = len(exchange)
    dmat, wq_t, wk_t, g_blk = _decay_tables()

    def body(dm_ref, o_ref, q_ref, k_ref, v_ref, gate_ref, cos_ref, sin_ref, d_ref, wq_ref, wk_ref, rg_ref, rs_ref,
             *rest):
        x_in, (dp_ref, drg_ref), x_out, gstate = rest[:nx], rest[nx:nx + 2], rest[nx + 2:2 * nx + 2], rest[2 * nx + 2]

        @pl.when(pl.program_id(0) == 0)
        def _():
            if nx:
                for cp in _sibling_half_copies(x_in, x_out, *rest[2 * nx + 3:])[0]:
                    cp.start()
            gstate[...] = jnp.zeros_like(gstate)
            drg_ref[...] = jnp.zeros_like(drg_ref)

        lane = _iota((BLK, BLK), 1)
        sub = _iota((BLK, BLK), 0)
        scale = HEAD_LANES ** -0.5
        for b in reversed(range(G)):
            rows = slice(b * BLK, (b + 1) * BLK)
            rot, rot_t = _rot_fns(cos_ref[rows, :], sin_ref[rows, :])
            for p in range(2):
                qr = rot(q_ref[rows, p * BLK:(p + 1) * BLK].astype(F32))
                kr = rot(k_ref[rows, p * BLK:(p + 1) * BLK].astype(F32)) * scale
                kr_b = kr.astype(BF16)
                qw = (qr * wq_ref[p]).astype(BF16)
                kw = (kr * wk_ref[p]).astype(BF16)
                dqr = jnp.zeros((BLK, BLK), F32)
                dkr = jnp.zeros((BLK, BLK), F32)
                for e in range(2):
                    h = 2 * p + e
                    cols = slice(h * BLK, (h + 1) * BLK)
                    head_lanes = (lane >> 6) == e
                    o = o_ref[rows, cols]
                    rn = lax.rsqrt(jnp.mean(o * o, axis=-1, keepdims=True) + EPS)
                    y = o * rn
                    gate = gate_ref[rows, cols].astype(F32)
                    sg = jax.nn.sigmoid(gate)
                    dm = dm_ref[rows, cols].astype(F32)
                    rgain = rg_ref[:, cols]
                    drg_ref[:, cols] = drg_ref[:, cols] + jnp.sum(dm * y * (gate * sg), axis=0, keepdims=True)
                    dp_ref[rows, 1024 + h * BLK:1024 + (h + 1) * BLK] = (
                        dm * y * rgain * (sg * (1.0 + gate * (1.0 - sg)))).astype(BF16)
                    dy = dm * rgain * (gate * sg)
                    do = (rn * (dy - y * jnp.mean(dy * y, axis=-1, keepdims=True))).astype(BF16)
                    vh = v_ref[rows, cols]
                    qm = jnp.where(head_lanes, qr, 0.0).astype(BF16)
                    dmh = d_ref[h]
                    s = (_dot_nt(qm, kr_b) * dmh).astype(BF16)
                    ds = (_dot_nt(do, vh) * dmh).astype(BF16)
                    st = rs_ref[b, h].astype(BF16)
                    gs = gstate[h]
                    gs_b = gs.astype(BF16)
                    dqr = dqr + jnp.where(head_lanes, _dot(ds, kr_b), 0.0) + _dot_nt(do, st) * wq_ref[p]
                    dkr = dkr + _dot_tn(ds, qm) + _dot_nt(vh, gs_b) * wk_ref[p]
                    dp_ref[rows, 512 + h * BLK:512 + (h + 1) * BLK] = (_dot_tn(s, do) + _dot(kw, gs_b)).astype(BF16)
                    dr = jnp.where((sub >> 6) == e, _dot_tn(qw, do), 0.0)
                    gstate[h] = dr + g_blk[h] * gs
                dp_ref[rows, p * BLK:(p + 1) * BLK] = rot_t(dqr).astype(BF16)
                dp_ref[rows, 256 + p * BLK:256 + (p + 1) * BLK] = (rot_t(dkr) * scale).astype(BF16)

        if nx:
            @pl.when(pl.program_id(0) == steps - 1)
            def _():
                sends, recvs = _sibling_half_copies(x_in, x_out, *rest[2 * nx + 3:])
                for cp in recvs:
                    cp.wait_recv()
                for cp in sends:
                    cp.wait_send()

    row = lambda c: (lambda i: (steps - 1 - i, c))
    return pl.pallas_call(
        body, name="b_retention", grid=(steps,),
        in_specs=[pl.BlockSpec((G * BLK, 512), row(0)), pl.BlockSpec((G * BLK, 512), row(0)),
                  pl.BlockSpec((G * BLK, 256), row(0)), pl.BlockSpec((G * BLK, 256), row(1)),
                  pl.BlockSpec((G * BLK, 512), row(1)), pl.BlockSpec((G * BLK, 512), row(2)),
                  pl.BlockSpec((G * BLK, BLK), row(0)), pl.BlockSpec((G * BLK, BLK), row(0)),
                  _full((RET_HEADS, BLK, BLK)), _full((2, BLK, BLK)), _full((2, BLK, BLK)), _full((1, 512)),
                  pl.BlockSpec((G, RET_HEADS, BLK, BLK), lambda i: (steps - 1 - i, 0, 0, 0))] + [_ANY] * nx,
        out_specs=[pl.BlockSpec((G * BLK, RET_W), row(0)), _full((1, 512))] + [_ANY] * nx,
        out_shape=[jax.ShapeDtypeStruct((L, RET_W), BF16), jax.ShapeDtypeStruct((1, 512), F32)]
        + _sibling_half_shapes(exchange),
        scratch_shapes=[pltpu.VMEM((RET_HEADS, BLK, BLK), F32)] + _sibling_half_semaphores(nx),
        compiler_params=_params(("arbitrary",)),
    )(dmix, o, proj, proj, proj, proj, cos_t, sin_t, dmat, wq_t, wk_t, ret_g, states, *exchange)


def _fox_delta(dmix, o_f):
    L = o_f.shape[0]
    nblk = L // BLK
    G = _block_group(nblk)

    def body(do_ref, o_ref, d_ref):
        sel = ((_iota((8, 512), 1) >> 6) == _iota((8, 512), 0)).astype(BF16)
        for b in range(G):
            rows = slice(b * BLK, (b + 1) * BLK)
            prod = do_ref[rows, :].astype(F32) * o_ref[rows, :].astype(F32)
            hi = prod.astype(BF16)
            lo = (prod - hi.astype(F32)).astype(BF16)
            d_ref[b] = _dot_nt(sel, hi) + _dot_nt(sel, lo)

    return pl.pallas_call(
        body, name="b_foxdelta", grid=(nblk // G,),
        in_specs=[pl.BlockSpec((G * BLK, 512), lambda i: (i, 1)), pl.BlockSpec((G * BLK, 512), lambda i: (i, 0))],
        out_specs=pl.BlockSpec((G, 8, BLK), lambda i: (i, 0, 0)),
        out_shape=jax.ShapeDtypeStruct((nblk, 8, BLK), F32),
        compiler_params=_params(("parallel",)),
    )(dmix, o_f)


def _fox_bwd(proj, dmix, c, ctb, lse, delta, scatter=()):
    L = proj.shape[0]
    nblk, nu = _fox_units(L)
    scale = HEAD_LANES ** -0.5
    ns = len(scatter)

    steps = FOX_HEADS // (2 * FOX_PAIRS)

    def body(qkv_ref, do_ref, c_ref, ct_ref, lse_ref, dl_ref, *rest):
        s_in, (dp_ref, dc_ref, dcq_ref), s_out = rest[:ns], rest[ns:ns + 3], rest[ns + 3:2 * ns + 3]
        ktt, dqt, dk_acc, dv_acc, dcs_acc = rest[2 * ns + 3:2 * ns + 8]
        p = pl.program_id(0)
        heads = [(pp, e, 2 * FOX_PAIRS * p + 2 * pp + e) for pp in range(FOX_PAIRS) for e in range(2)]

        @pl.when(p == 0)
        def _():
            dc_ref[...] = jnp.zeros_like(dc_ref)
            dcq_ref[...] = jnp.zeros_like(dcq_ref)
            if ns:
                for cp in _scatter_copies(s_in, s_out, *rest[2 * ns + 8:]):
                    cp.start()

        sub8 = _iota((8, BLK), 0)
        masks = _fox_tile_masks()

        def pre(j, carry):
            off = pl.multiple_of(j * BLK, BLK)
            for pp in range(FOX_PAIRS):
                ktt[pp, j] = qkv_ref[pl.ds(off, BLK), pp * 384 + BLK:pp * 384 + 2 * BLK].astype(F32).T.astype(BF16)
                dqt[pp, j] = jnp.zeros((BLK, BLK), F32)
            return carry

        lax.fori_loop(0, nblk, pre, 0)

        def kv_pass(kblk, nk, n_later):
            klen = nk * BLK
            koff = pl.multiple_of(kblk * BLK, BLK)
            kt = [qkv_ref[pl.ds(koff, klen), pp * 384 + BLK:pp * 384 + 2 * BLK] for pp in range(FOX_PAIRS)]
            vtile = [qkv_ref[pl.ds(koff, klen), pp * 384 + 2 * BLK:pp * 384 + 3 * BLK] for pp in range(FOX_PAIRS)]
            ct = c_ref[pl.ds(koff, klen), :]
            klane = _iota((klen, BLK), 1)
            cs = [jnp.broadcast_to(jnp.sum(jnp.where(klane == h, ct, 0.0), axis=1, keepdims=True), (klen, WIDE * UNIT))
                  for _, _, h in heads]
            k_t = [jnp.concatenate([ktt[pp, kblk + b, e * HEAD_LANES:(e + 1) * HEAD_LANES, :] for b in range(nk)], axis=1)
                   for pp, e, _ in heads]
            for pp in range(FOX_PAIRS):
                dk_acc[pp, 0:klen] = jnp.zeros((klen, BLK), F32)
                dv_acc[pp, 0:klen] = jnp.zeros((klen, BLK), F32)
            for hh in range(len(heads)):
                dcs_acc[hh, 0:klen] = jnp.zeros((klen, BLK), F32)

            def tile(qblk, nq, mask):
                qlen = nq * BLK
                if mask == "valid":
                    mask = _iota((klen, qlen), 0) >= N_PAD
                qoff = pl.multiple_of(qblk * BLK, BLK)
                qlane = _iota((qlen, BLK), 1)
                qs = [qkv_ref[pl.ds(qoff, qlen), pp * 384:pp * 384 + BLK].astype(F32) * (scale * LOG2E)
                      for pp in range(FOX_PAIRS)]
                dot_ = [do_ref[pl.ds(qoff, qlen), pp * BLK:(pp + 1) * BLK] for pp in range(FOX_PAIRS)]
                stats = [[ref[qblk + a] for a in range(nq)] for ref in (ct_ref, lse_ref, dl_ref)]
                dcq = [jnp.zeros((8, BLK), F32) for _ in range(nq)]
                for hh, (pp, e, h) in enumerate(heads):
                    head = (qlane >> 6) == e
                    ct_row, lse_row, dl_row = [jnp.concatenate([_pick_row(t, h) for t in ts], axis=1) for ts in stats]
                    qm = jnp.where(head, qs[pp], 0.0).astype(BF16)
                    dom = jnp.where(head, dot_[pp], jnp.zeros_like(dot_[pp]))
                    t = _dot_nt(kt[pp], qm) - cs[hh][:, 0:qlen]
                    if mask is not None:
                        t = jnp.where(mask, t, NEG)
                    pr = jnp.exp2(t + (ct_row - lse_row))
                    dv_acc[pp, 0:klen] = dv_acc[pp, 0:klen] + _dot(pr.astype(BF16), dom)
                    dsv = pr * (_dot_nt(vtile[pp], dom) - dl_row)
                    ds_b = dsv.astype(BF16)
                    dk_acc[pp, 0:klen] = dk_acc[pp, 0:klen] + _dot(ds_b, qm)
                    rows = slice(e * HEAD_LANES, (e + 1) * HEAD_LANES)
                    dq_t = _dot(k_t[hh], ds_b)
                    key_side = dsv[:, 0:BLK]
                    for a in range(1, nq):
                        key_side = key_side + dsv[:, a * BLK:(a + 1) * BLK]
                    dcs_acc[hh, 0:klen] = dcs_acc[hh, 0:klen] + key_side
                    query_side = jnp.sum(dsv, axis=0, keepdims=True)
                    for a in range(nq):
                        cols = slice(a * BLK, (a + 1) * BLK)
                        dqt[pp, qblk + a, rows, :] = dqt[pp, qblk + a, rows, :] + dq_t[:, cols]
                        dcq[a] = dcq[a] + jnp.where(sub8 == h, query_side[:, cols], 0.0)
                for a in range(nq):
                    dcq_ref[qblk + a] = dcq_ref[qblk + a] + dcq[a]

            later_mask = "valid" if nk == 1 else None
            n_later = jnp.asarray(n_later, jnp.int32)
            n_wide = n_later // WIDE

            def later_wide(i, carry):
                tile(kblk + nk + 2 * WIDE * i, 2 * WIDE, later_mask)
                return carry

            tile(kblk, nk, masks["first"] if nk == 1 else masks["diag"])
            lax.fori_loop(0, n_wide, later_wide, 0)
            rest_blk = kblk + nk + 2 * WIDE * n_wide

            @pl.when((n_later & 2) != 0)
            def _():
                tile(rest_blk, 4, later_mask)

            @pl.when((n_later & 1) != 0)
            def _():
                tile(rest_blk + 2 * (n_later & 2), 2, later_mask)

            upd = jnp.zeros((klen, BLK), F32)
            for hh, (_, _, h) in enumerate(heads):
                upd = upd + jnp.where(klane == h, -jnp.sum(dcs_acc[hh, 0:klen], axis=1, keepdims=True), 0.0)
            dc_ref[pl.ds(koff, klen), :] = dc_ref[pl.ds(koff, klen), :] + upd
            for pp in range(FOX_PAIRS):
                dp_ref[pl.ds(koff, klen), pp * 384 + BLK:pp * 384 + 2 * BLK] = (
                    dk_acc[pp, 0:klen] * (1.0 / LOG2E)).astype(BF16)
                dp_ref[pl.ds(koff, klen), pp * 384 + 2 * BLK:pp * 384 + 3 * BLK] = dv_acc[pp, 0:klen].astype(BF16)

        kv_pass(0, 1, nu)

        def k_loop(u, carry):
            kv_pass(1 + 2 * u, 2, nu - 1 - u)
            return carry

        lax.fori_loop(0, nu, k_loop, 0)

        def flush(j, carry):
            off = pl.multiple_of(j * BLK, BLK)
            for pp in range(FOX_PAIRS):
                dp_ref[pl.ds(off, BLK), pp * 384:pp * 384 + BLK] = (dqt[pp, j].T * scale).astype(BF16)
            return carry

        lax.fori_loop(0, nblk, flush, 0)

        if ns:
            @pl.when(p == steps - 1)
            def _():
                copies = _scatter_copies(s_in, s_out, *rest[2 * ns + 8:])
                for cp in copies:
                    cp.wait_recv()
                for cp in copies:
                    cp.wait_send()

    width = 384 * FOX_PAIRS
    once = lambda shape, index: pl.BlockSpec(shape, index, pipeline_mode=pl.Buffered(1))
    stat = once((nblk, 8, BLK), lambda p: (0, 0, 0))
    return pl.pallas_call(
        body, name="b_fox", grid=(steps,),
        in_specs=[once((L, width), lambda p: (0, RET_W // width + p)),
                  once((L, FOX_PAIRS * BLK), lambda p: (0, 4 // FOX_PAIRS + p)),
                  once((L, BLK), lambda p: (0, 0)), stat, stat, stat] + [_ANY] * ns,
        out_specs=[pl.BlockSpec((L, width), lambda p: (0, p)), _full((L, BLK)), _full((nblk, 8, BLK))] + [_ANY] * ns,
        out_shape=[jax.ShapeDtypeStruct((L, FOX_W), BF16), jax.ShapeDtypeStruct((L, BLK), F32),
                   jax.ShapeDtypeStruct((nblk, 8, BLK), F32)] + _scatter_shapes(scatter),
        scratch_shapes=[pltpu.VMEM((FOX_PAIRS, nblk, BLK, BLK), BF16), pltpu.VMEM((FOX_PAIRS, nblk, BLK, BLK), F32),
                        pltpu.VMEM((FOX_PAIRS, UNIT, BLK), F32), pltpu.VMEM((FOX_PAIRS, UNIT, BLK), F32),
                        pltpu.VMEM((2 * FOX_PAIRS, UNIT, BLK), F32)]
        + _scatter_semaphores(ns),
        compiler_params=_params(("arbitrary",)),
    )(proj, dmix, c, ctb, lse, delta, *scatter)


def _fox_post(dc, dcq, ff, fb):
    L = dc.shape[0]
    nblk = L // BLK
    G = _block_group(nblk)
    steps = nblk // G

    def body(dc_ref, dcq_ref, ff_ref, b_ref, dff_ref, dffb_ref, dfb_ref, carry):
        @pl.when(pl.program_id(0) == 0)
        def _():
            carry[...] = jnp.zeros_like(carry)
            dfb_ref[...] = jnp.zeros_like(dfb_ref)

        tri = (_iota((BLK, BLK), 0) <= _iota((BLK, BLK), 1)).astype(BF16)
        live = _iota((BLK, BLK), 1) < FOX_HEADS
        run, dfb = carry[...], dfb_ref[...]
        for b in reversed(range(G)):
            rows = slice(b * BLK, (b + 1) * BLK)
            d = dc_ref[rows, :] + jnp.concatenate([dcq_ref[b], jnp.zeros((BLK - 8, BLK), F32)], axis=0).T
            hi, mid, lo = _split3(d)
            dlf = _dot(tri, hi) + _dot(tri, mid) + _dot(tri, lo) + run
            run = run + jnp.sum(d, axis=0, keepdims=True)
            z = ff_ref[rows, :] + b_ref[...]
            dff = jnp.where(live, dlf * jax.nn.sigmoid(-z), 0.0)
            dff_ref[rows, :] = dff
            dffb_ref[rows, :] = dff.astype(BF16)
            dfb = dfb + jnp.sum(dff, axis=0, keepdims=True)
        carry[...] = run
        dfb_ref[...] = dfb

    rev = lambda i: (steps - 1 - i, 0)
    return pl.pallas_call(
        body, name="b_foxpost", grid=(steps,),
        in_specs=[pl.BlockSpec((G * BLK, BLK), rev), pl.BlockSpec((G, 8, BLK), lambda i: (steps - 1 - i, 0, 0)),
                  pl.BlockSpec((G * BLK, BLK), rev), _full((1, BLK))],
        out_specs=[pl.BlockSpec((G * BLK, BLK), rev), pl.BlockSpec((G * BLK, BLK), rev), _full((1, BLK))],
        out_shape=[jax.ShapeDtypeStruct((L, BLK), F32), jax.ShapeDtypeStruct((L, BLK), BF16),
                   jax.ShapeDtypeStruct((1, BLK), F32)],
        scratch_shapes=[pltpu.VMEM((1, BLK), F32)],
        compiler_params=_params(("arbitrary",)),
    )(dc, dcq, ff, fb)


def _inproj_bwd(dpr, dpf, dffb, w_main, w_ff, h0, g, dh1, scatter=()):
    L = h0.shape[0]
    S = L - BLK
    tm = _row_tile(S, (512, 256, 128))
    nt = S // tm
    ns = len(scatter)
    operands = (dpr, dpf, dffb, h0, dh1)

    def body(*refs):
        lead, tile = refs[0:5], refs[5:10]
        wm_ref, wf_ref, g_ref = refs[10:13]
        rest = refs[13:]
        s_in, (dlead_ref, dx_ref, dg_ref), s_out = rest[:ns], rest[ns:ns + 3], rest[ns + 3:2 * ns + 3]
        i = pl.program_id(0)

        def rows_bwd(dpr_ref, dpf_ref, dff_ref, h_ref, dh1_ref):
            dn = (_dot_nt(dpr_ref[...], wm_ref[:, 0:RET_W]) + _dot_nt(dpf_ref[...], wm_ref[:, RET_W:MAIN_W])
                  + _dot_nt(dff_ref[...], wf_ref[...]))
            h = h_ref[...]
            r = lax.rsqrt(jnp.mean(h * h, axis=-1, keepdims=True) + EPS)
            yn = h * r
            dyn = dn * g_ref[...]
            dh0 = dh1_ref[...] + r * (dyn - yn * jnp.mean(dyn * yn, axis=-1, keepdims=True))
            return dh0, jnp.sum(dn * yn, axis=0, keepdims=True)

        @pl.when(i == 0)
        def _():
            if ns:
                for cp in _scatter_copies(s_in, s_out, *rest[2 * ns + 3:]):
                    cp.start()
            dlead_ref[...], dg_ref[...] = rows_bwd(*lead)

        dx_ref[...], dg_tile = rows_bwd(*tile)
        dg_ref[...] = dg_ref[...] + dg_tile

        if ns:
            @pl.when(i == nt - 1)
            def _():
                copies = _scatter_copies(s_in, s_out, *rest[2 * ns + 3:])
                for cp in copies:
                    cp.wait_recv()
                for cp in copies:
                    cp.wait_send()

    lead_spec = lambda a: pl.BlockSpec((BLK, a.shape[1]), lambda i: (0, 0))
    tile_spec = lambda a: pl.BlockSpec((pl.Element(tm), pl.Element(a.shape[1])),
                                       lambda i: (pl.multiple_of(BLK + i * tm, BLK), 0))
    return pl.pallas_call(
        body, name="b_inproj", grid=(nt,),
        in_specs=[lead_spec(a) for a in operands] + [tile_spec(a) for a in operands]
        + [_full((D_MODEL, MAIN_W)), _full((D_MODEL, BLK)), _full((1, D_MODEL))] + [_ANY] * ns,
        out_specs=[_full((BLK, D_MODEL)), pl.BlockSpec((tm, D_MODEL), lambda i: (i, 0)), _full((1, D_MODEL))]
        + [_ANY] * ns,
        out_shape=[jax.ShapeDtypeStruct((BLK, D_MODEL), F32), jax.ShapeDtypeStruct((S, D_MODEL), F32),
                   jax.ShapeDtypeStruct((1, D_MODEL), F32)] + _scatter_shapes(scatter),
        scratch_shapes=_scatter_semaphores(ns),
        compiler_params=_params(("arbitrary",)),
    )(*operands, *operands, w_main, w_ff, g, *scatter)


def _local_step(x, target, meta, attn_g, w_main, w_ff, fox_b, ret_g, w_out, ffn_g, w_up, conv_w, conv_b, w_down, final_g,
                late=None, mid=None, last=None, wire=F32):
    S = x.shape[0]
    L = S + PREFIX
    head = jnp.concatenate([jnp.zeros((N_PAD, D_MODEL), F32), meta], axis=0)
    fb = jnp.pad(fox_b, ((0, 0), (0, BLK - FOX_HEADS)))
    cos_t, sin_t = _rotary_tables(L)

    h0, n1, proj, ff = _rms_inproj(head, x, attn_g, w_main, w_ff)
    c, ctb = _fox_prep(ff, fb)
    mix_r, o_ret, states = _retention_fwd(proj, cos_t, sin_t, ret_g)
    if late is None:
        o_f, lse = _fox_fwd(proj, c, ctb)
    else:
        o_f, lse, *gathered = _fox_fwd(proj, c, ctb, gather=late[0])
        w_out, w_up, w_down = late[1](gathered)
    h1, n2, up, g_act, acc_saved = _outproj_up(mix_r, o_f, h0, w_out, ffn_g, w_up, conv_w, conv_b)
    dh2, dh2b, d_final_g, loss, dacc, db = _ffn_down_loss(g_act, w_down, h1, final_g, target, acc_saved, up)

    dup, dh1, dh1b, dmix, d_ffn_g, dconv = _ffn_bwd_up(dacc, db, up, conv_w, w_up, h1, ffn_g, dh2, w_out)
    d_w_down = _wgrad(g_act, dh2b, "wgrad_down", tk=D_FF // 2, out_dtype=wire)[0]
    d_w_up = _wgrad(n2, dup, "wgrad_up", tn=w_up.shape[2], out_dtype=wire)
    d_w_out = jnp.concatenate([_wgrad(mix_r, dh1b, "wgrad_out_r", out_dtype=wire)[0],
                               _wgrad(o_f, dh1b, "wgrad_out_f", out_dtype=wire)[0]], axis=0)

    early = () if mid is None else mid[0](d_w_out, d_w_up, d_w_down)
    dpr, d_ret_g, *from_sibling = _retention_bwd(dmix, o_ret, proj, cos_t, sin_t, ret_g, states, exchange=early)
    delta = _fox_delta(dmix, o_f)
    scatter = () if mid is None else mid[1](early, from_sibling)
    dpf, dc, dcq, *received = _fox_bwd(proj, dmix, c, ctb, lse, delta, scatter=scatter)
    dff, dffb, d_fox_b = _fox_post(dc, dcq, ff, fb)
    d_w_ret, d_w_fox = _wgrad(n1, dpr, "wgrad_in_r")[0], _wgrad(n1, dpf, "wgrad_in_f")[0]
    d_w_ff = _wgrad(n1, dffb, "wgrad_in_ff")[0][:, :FOX_HEADS]
    scatter_in = () if last is None else last(d_w_ret, d_w_fox, d_w_ff)
    dlead, dx, d_attn_g, *received_in = _inproj_bwd(dpr, dpf, dffb, w_main, w_ff, h0, attn_g, dh1, scatter=scatter_in)

    return dict(
        loss=loss[0, 0], dx=dx, dmeta=dlead[N_PAD:], attn_g=d_attn_g, w_main=jnp.concatenate([d_w_ret, d_w_fox], axis=1),
        w_ff=d_w_ff, fox_b=d_fox_b[:, :FOX_HEADS], ret_g=d_ret_g, w_out=d_w_out, ffn_g=d_ffn_g,
        w_up=d_w_up, conv_w=dconv[0:3], conv_b=dconv[3:4], w_down=d_w_down, final_g=d_final_g,
        scatter=list(scatter_in) + list(scatter), received=list(received_in) + list(received))


_ANY = pl.BlockSpec(memory_space=pl.ANY)


def _place():
    return lax.axis_index("x"), lax.axis_index("y"), lax.axis_index("c")


def _other_chips(x, y):
    return [(1 - x, y), (x, 1 - y), (1 - x, 1 - y)]


def _allgather_semaphores(n):
    if n == 0:
        return []
    return [pltpu.SemaphoreType.DMA((3 * n,)), pltpu.SemaphoreType.DMA((3 * n,)), pltpu.SemaphoreType.DMA((n,))]


def _allgather_copies(ins, outs, send, recv, loc):
    n = len(ins)
    x, y, c = _place()
    mine = 2 * x + y
    peers = _other_chips(x, y)

    def remote(a, k, slot):
        return pltpu.make_async_remote_copy(
            src_ref=ins[a], dst_ref=outs[a].at[slot], send_sem=send.at[3 * a + k], recv_sem=recv.at[3 * a + k],
            device_id=(peers[k][0], peers[k][1], c), device_id_type=MESH)

    local = [pltpu.make_async_copy(ins[a], outs[a].at[mine], loc.at[a]) for a in range(n)]
    sends = [remote(a, k, mine) for a in range(n) for k in range(3)]
    recvs = [remote(a, k, 2 * peers[k][0] + peers[k][1]) for a in range(n) for k in range(3)]
    return local, sends, recvs


def _chip_allgather_halves(w, small):
    half = w.shape[0] // 2

    def body(w_ref, s_ref, wo_ref, so_ref, send, recv, fsend, frecv, ssend, srecv, loc):
        x, y, c = _place()
        mine = 2 * x + y
        peers = _other_chips(x, y)

        def fetch(k, slot):
            return pltpu.make_async_remote_copy(
                src_ref=w_ref.at[pl.ds(c * half, half)], dst_ref=wo_ref.at[slot, c], send_sem=send.at[k],
                recv_sem=recv.at[k], device_id=(peers[k][0], peers[k][1], c), device_id_type=MESH)

        def forward(k, which):
            slot = 2 * peers[k][0] + peers[k][1]
            return pltpu.make_async_remote_copy(
                src_ref=wo_ref.at[slot, which], dst_ref=wo_ref.at[slot, which], send_sem=fsend.at[k],
                recv_sem=frecv.at[k], device_id=(x, y, 1 - c), device_id_type=MESH)

        def small_copy(k, slot):
            return pltpu.make_async_remote_copy(
                src_ref=s_ref, dst_ref=so_ref.at[slot], send_sem=ssend.at[k], recv_sem=srecv.at[k],
                device_id=(peers[k][0], peers[k][1], c), device_id_type=MESH)

        local = pltpu.make_async_copy(s_ref, so_ref.at[mine], loc.at[0])
        sends = [fetch(k, mine) for k in range(3)] + [small_copy(k, mine) for k in range(3)]
        local.start()
        for cp in sends:
            cp.start()
        forwards = []
        for k in range(3):
            fetch(k, 2 * peers[k][0] + peers[k][1]).wait_recv()
            forwards.append(forward(k, c))
            forwards[-1].start()
        for k in range(3):
            forward(k, 1 - c).wait_recv()
            small_copy(k, 2 * peers[k][0] + peers[k][1]).wait_recv()
        for cp in sends + forwards:
            cp.wait_send()
        local.wait()

    three = pltpu.SemaphoreType.DMA((3,))
    return pl.pallas_call(
        body, name="ag_weights", in_specs=[_ANY] * 2, out_specs=[_ANY] * 2,
        out_shape=[jax.ShapeDtypeStruct((N_CHIPS, 2, half, w.shape[1]), w.dtype),
                   jax.ShapeDtypeStruct((N_CHIPS,) + small.shape, small.dtype)],
        scratch_shapes=[three, three, three, three, three, three, pltpu.SemaphoreType.DMA((1,))],
    )(w, small)


def _chip_allgather(arrays):
    n = len(arrays)

    def body(*refs):
        local, sends, recvs = _allgather_copies(refs[:n], refs[n:2 * n], *refs[2 * n:])
        for cp in local + sends:
            cp.start()
        for cp in recvs:
            cp.wait_recv()
        for cp in sends:
            cp.wait_send()
        for cp in local:
            cp.wait()

    return pl.pallas_call(
        body, name="ag_weights", in_specs=[_ANY] * n, out_specs=[_ANY] * n,
        out_shape=[jax.ShapeDtypeStruct((N_CHIPS,) + a.shape, a.dtype) for a in arrays],
        scratch_shapes=_allgather_semaphores(n),
    )(*arrays)


def _sibling_halves(grads):
    n = len(grads)

    def body(*refs):
        sends, recvs = _sibling_half_copies(refs[:n], refs[n:2 * n], *refs[2 * n:])
        for cp in sends:
            cp.start()
        for cp in recvs:
            cp.wait_recv()
        for cp in sends:
            cp.wait_send()

    return pl.pallas_call(
        body, name="rs_sibling", in_specs=[_ANY] * n, out_specs=[_ANY] * n,
        out_shape=_sibling_half_shapes(grads), scratch_shapes=_sibling_half_semaphores(n),
    )(*grads)


def _sibling_half_shapes(grads):
    return [jax.ShapeDtypeStruct((N_CHIPS, g.shape[1] // 2, g.shape[2]), g.dtype) for g in grads]


def _sibling_half_semaphores(n):
    return [pltpu.SemaphoreType.DMA((n,)), pltpu.SemaphoreType.DMA((n,))] if n else []


def _sibling_half_copies(ins, outs, send, recv):
    x, y, c = _place()

    def half_copy(a, which):
        half = ins[a].shape[1] // 2
        return pltpu.make_async_remote_copy(
            src_ref=ins[a].at[pl.ds(0, N_CHIPS), pl.ds(which * half, half)], dst_ref=outs[a],
            send_sem=send.at[a], recv_sem=recv.at[a], device_id=(x, y, 1 - c), device_id_type=MESH)

    return [half_copy(a, 1 - c) for a in range(len(ins))], [half_copy(a, c) for a in range(len(ins))]


def _scatter_shapes(parts):
    return [jax.ShapeDtypeStruct((3,) + p.shape[1:], p.dtype) for p in parts]


def _scatter_semaphores(n):
    return [pltpu.SemaphoreType.DMA((3 * n,)), pltpu.SemaphoreType.DMA((3 * n,))] if n else []


def _scatter_copies(ins, outs, send, recv):
    x, y, c = _place()
    peers = _other_chips(x, y)
    return [pltpu.make_async_remote_copy(
        src_ref=ins[a].at[2 * peers[k][0] + peers[k][1]], dst_ref=outs[a].at[k], send_sem=send.at[3 * a + k],
        recv_sem=recv.at[3 * a + k], device_id=(peers[k][0], peers[k][1], c), device_id_type=MESH)
        for a in range(len(ins)) for k in range(3)]


def _sibling_allgather(bufs, small):
    n = len(bufs)

    def body(*refs):
        small_in, outs, small_out = refs[n], refs[n + 1:2 * n + 1], refs[2 * n + 1]
        send, recv, s_send, s_recv, loc = refs[2 * n + 2:]
        x, y, c = _place()
        me = 4 * x + 2 * y + c

        def remote(a, which):
            return pltpu.make_async_remote_copy(
                src_ref=outs[a].at[which], dst_ref=outs[a].at[which], send_sem=send.at[a], recv_sem=recv.at[a],
                device_id=(x, y, 1 - c), device_id_type=MESH)

        def peer_of(r):
            return tuple(1 - v if (r >> b) & 1 else v for v, b in ((x, 2), (y, 1), (c, 0)))

        def small_copy(r, slot):
            return pltpu.make_async_remote_copy(
                src_ref=small_in, dst_ref=small_out.at[slot], send_sem=s_send.at[r - 1], recv_sem=s_recv.at[r - 1],
                device_id=peer_of(r), device_id_type=MESH)

        local = pltpu.make_async_copy(small_in, small_out.at[me], loc.at[0])
        sends = [remote(a, c) for a in range(n)] + [small_copy(r, me) for r in range(1, N_DEV)]
        local.start()
        for cp in sends:
            cp.start()
        for r in range(1, N_DEV):
            px, py, pc = peer_of(r)
            small_copy(r, 4 * px + 2 * py + pc).wait_recv()
        for a in range(n):
            remote(a, 1 - c).wait_recv()
        for cp in sends:
            cp.wait_send()
        local.wait()

    outs = pl.pallas_call(
        body, name="ag_sibling", in_specs=[_ANY] * (n + 1), out_specs=[_ANY] * (n + 1),
        out_shape=[jax.ShapeDtypeStruct(b.shape, b.dtype) for b in bufs]
        + [jax.ShapeDtypeStruct((N_DEV,) + small.shape, small.dtype)],
        input_output_aliases={a: a for a in range(n)},
        scratch_shapes=[pltpu.SemaphoreType.DMA((n,)), pltpu.SemaphoreType.DMA((n,)),
                        pltpu.SemaphoreType.DMA((N_DEV - 1,)), pltpu.SemaphoreType.DMA((N_DEV - 1,)),
                        pltpu.SemaphoreType.DMA((1,))],
    )(*bufs, small)
    return [o.reshape(2 * o.shape[1], o.shape[2]) for o in outs[:n]], outs[n]


def _pair_add(full, recv, core, name):
    _, R, C = full.shape
    half = R // 2

    def body(core_ref, a_ref, b_ref, o_ref):
        o_ref[...] = (a_ref[...].astype(F32) + b_ref[...].astype(F32)).astype(BF16)

    return pl.pallas_call(
        body, name=name,
        grid_spec=pltpu.PrefetchScalarGridSpec(
            num_scalar_prefetch=1, grid=(N_CHIPS,),
            in_specs=[pl.BlockSpec((1, half, C), lambda j, core_ref: (j, core_ref[0], 0)),
                      pl.BlockSpec((1, half, C), lambda j, core_ref: (j, 0, 0))],
            out_specs=pl.BlockSpec((1, half, C), lambda j, core_ref: (j, 0, 0))),
        out_shape=jax.ShapeDtypeStruct((N_CHIPS, half, C), BF16),
        compiler_params=_params(("parallel",)),
    )(core, full, recv)


def _sum_partials(own_all, recv, place, name, tiles=2):
    _, R, C = own_all.shape
    tr = R // tiles

    def body(place_ref, own_ref, r_ref, o_ref):
        acc = own_ref[0].astype(F32)
        for k in range(3):
            acc = acc + r_ref[k].astype(F32)
        o_ref[0] = acc

    return pl.pallas_call(
        body, name=name,
        grid_spec=pltpu.PrefetchScalarGridSpec(
            num_scalar_prefetch=1, grid=(tiles,),
            in_specs=[pl.BlockSpec((1, tr, C), lambda i, place_ref: (place_ref[0], i, 0)),
                      pl.BlockSpec((3, tr, C), lambda i, place_ref: (0, i, 0))],
            out_specs=pl.BlockSpec((1, tr, C), lambda i, place_ref: (place_ref[1], i, 0))),
        out_shape=jax.ShapeDtypeStruct((2, R, C), F32),
        compiler_params=_params(("parallel",)),
    )(place, own_all, recv)


def _adamw_math(w, g, m, v):
    m2 = ADAM_B1 * m + (1.0 - ADAM_B1) * g
    v2 = ADAM_B2 * v + (1.0 - ADAM_B2) * (g * g)
    m_hat = m2 / (1.0 - ADAM_B1 ** ADAM_STEP)
    v_hat = v2 / (1.0 - ADAM_B2 ** ADAM_STEP)
    return -ADAM_LR * (m_hat / (jnp.sqrt(v_hat) + ADAM_EPS) + ADAM_WD * w), m2, v2


ROW_ATTN_G, ROW_FFN_G, ROW_FINAL_G, ROW_MISC, ROW_CONV_B, ROW_CONV_W, ROW_META, SMALL_ROWS = 0, 1, 2, 3, 4, 8, 24, 40
MISC_FOX_B, MISC_LOSS = 512, 640


def _small_pack(out):
    def rows(a, n):
        a = a.astype(F32)
        return jnp.pad(a, ((0, n - a.shape[0]), (0, D_MODEL - a.shape[1])))

    misc = jnp.concatenate([out["ret_g"], out["fox_b"], jnp.zeros((1, MISC_LOSS - MISC_FOX_B - FOX_HEADS), F32),
                            out["loss"].reshape(1, 1)], axis=1)
    conv_b = jnp.pad(out["conv_b"], ((0, 0), (0, (-D_FF) % D_MODEL))).reshape(-1, D_MODEL)
    conv_w = out["conv_w"].reshape(3, N_CHIPS, -1).transpose(1, 0, 2).reshape(3 * N_CHIPS, -1)
    return jnp.concatenate([
        rows(out["attn_g"], 1), rows(out["ffn_g"], 1), rows(out["final_g"], 1), rows(misc, 1),
        rows(conv_b, ROW_CONV_W - ROW_CONV_B), rows(conv_w, ROW_META - ROW_CONV_W), rows(out["dmeta"], N_META)], axis=0)


def _small_update(packs, chip, ws, ms, vs):
    n = len(ws)
    meta_w, conv_sw = ws[0].shape[1], ws[5].shape[2]
    assert packs.shape == (N_DEV, SMALL_ROWS, D_MODEL) and ws[0].shape[0] == N_META and ws[5].shape[:2] == (3, 1)

    def body(chip_ref, p_ref, *refs):
        w_refs, m_refs, v_refs = refs[:n], refs[n:2 * n], refs[2 * n:3 * n]
        loss_ref, out_refs, tot = refs[3 * n], refs[3 * n + 1:7 * n + 1], refs[7 * n + 1]
        acc = p_ref[0]
        for d in range(1, N_DEV):
            acc = acc + p_ref[d]
        tot[...] = acc

        def of_chip(pieces):
            val = pieces[-1]
            for j in range(N_CHIPS - 2, -1, -1):
                val = jnp.where(chip_ref[0] == j, pieces[j], val)
            return val

        row = lambda r, lo=0, hi=D_MODEL: tot[r:r + 1, lo:hi]
        grads = [
            of_chip([tot[ROW_META:ROW_META + N_META, j * meta_w:(j + 1) * meta_w] for j in range(N_CHIPS)]),
            row(ROW_ATTN_G), row(ROW_MISC, MISC_FOX_B, MISC_FOX_B + FOX_HEADS), row(ROW_MISC, 0, MISC_FOX_B),
            row(ROW_FFN_G),
            of_chip([tot[ROW_CONV_W + 3 * j:ROW_CONV_W + 3 * j + 3, 0:conv_sw] for j in range(N_CHIPS)]),
            jnp.concatenate([row(ROW_CONV_B), row(ROW_CONV_B + 1), row(ROW_CONV_B + 2, 0, D_FF - 2 * D_MODEL)], axis=1),
            row(ROW_FINAL_G)]
        loss_ref[...] = row(ROW_MISC, MISC_LOSS, MISC_LOSS + BLK)
        for k in range(n):
            parts = [((Ellipsis,), grads[k])]
            if len(ws[k].shape) == 3:
                parts = [((t,), grads[k][t:t + 1]) for t in range(ws[k].shape[0])]
            for at, g in parts:
                res = (g,) + _adamw_math(w_refs[k][at], g, m_refs[k][at], v_refs[k][at])
                for kind in range(4):
                    out_refs[kind * n + k][at] = res[kind]

    res = pl.pallas_call(
        body, name="small_update",
        grid_spec=pltpu.PrefetchScalarGridSpec(
            num_scalar_prefetch=1, grid=(1,),
            in_specs=[_full(packs.shape)] + [_full(a.shape) for a in list(ws) * 3],
            out_specs=[_full((1, BLK))] + [_full(a.shape) for a in list(ws) * 4],
            scratch_shapes=[pltpu.VMEM((SMALL_ROWS, D_MODEL), F32)]),
        out_shape=[jax.ShapeDtypeStruct((1, BLK), F32)] + [jax.ShapeDtypeStruct(a.shape, F32) for a in list(ws) * 4],
        compiler_params=_params(("arbitrary",)),
    )(chip, packs, *ws, *ms, *vs)
    return res[0], res[1:n + 1], res[n + 1:2 * n + 1], res[2 * n + 1:3 * n + 1], res[3 * n + 1:]


def _adamw(w, g, m, v, name, tiles=4):
    R, tail = w.shape[0], w.shape[1:]
    assert R % tiles == 0
    tr = R // tiles

    def body(w_ref, g_ref, m_ref, v_ref, go_ref, d_ref, m2_ref, v2_ref):
        g_ = g_ref[...]
        go_ref[...] = g_
        d_ref[...], m2_ref[...], v2_ref[...] = _adamw_math(w_ref[...], g_, m_ref[...], v_ref[...])

    spec = pl.BlockSpec((tr,) + tail, lambda i: (i,) + (0,) * len(tail))
    return pl.pallas_call(
        body, name=name, grid=(tiles,), in_specs=[spec] * 4, out_specs=[spec] * 4,
        out_shape=[jax.ShapeDtypeStruct(w.shape, F32)] * 4,
        compiler_params=_params(("parallel",)),
    )(w, g, m, v)


def _row_vector_tiles(n, most=80):
    return next(t for t in range(1, n + 1) if n % t == 0 and n // t <= most)


def _pack_rows(pieces, rows):
    flat = jnp.concatenate([jnp.pad(p.reshape(-1).astype(F32), (0, (-p.size) % D_MODEL)) for p in pieces])
    return jnp.pad(flat, (0, rows * D_MODEL - flat.size)).reshape(rows, D_MODEL)


def _unpack_rows(pack, shapes):
    flat = pack.reshape(-1)
    out, off = [], 0
    for shp in shapes:
        size = int(np.prod(shp))
        out.append(flat[off:off + size].reshape(shp))
        off += size + (-size) % D_MODEL
    return out


IN_PADDED = IN_WIDTH + (-IN_WIDTH) % BLK


def _fox_column_blocks():
    return [(RET_W + part * 512 + p * BLK, RET_W + 384 * p + part * BLK)
            for part in range(3) for p in range(FOX_HEADS // 2)]


def _w_in_kernel_order(gathered, own, chip):
    n, R, C = gathered.shape
    tr = R // 4

    def body(chip_ref, g_ref, own_ref, wm_ref, wf_ref, full):
        for j in range(n):
            @pl.when(chip_ref[0] == j)
            def _(j=j):
                full[:, j * C:(j + 1) * C] = own_ref[...]

            @pl.when(chip_ref[0] != j)
            def _(j=j):
                full[:, j * C:(j + 1) * C] = g_ref[j]

        full[:, n * C:] = jnp.zeros((tr, IN_PADDED - n * C), BF16)
        wm_ref[:, 0:RET_W] = full[:, 0:RET_W]
        for src, dst in _fox_column_blocks():
            wm_ref[:, dst:dst + BLK] = full[:, src:src + BLK]
        wf_ref[...] = full[:, MAIN_W:MAIN_W + BLK]

    return pl.pallas_call(
        body, name="w_in_kernel_order",
        grid_spec=pltpu.PrefetchScalarGridSpec(
            num_scalar_prefetch=1, grid=(R // tr,),
            in_specs=[pl.BlockSpec((n, tr, C), lambda i, c: (0, i, 0)), pl.BlockSpec((tr, C), lambda i, c: (i, 0))],
            out_specs=[pl.BlockSpec((tr, MAIN_W), lambda i, c: (i, 0)), pl.BlockSpec((tr, BLK), lambda i, c: (i, 0))],
            scratch_shapes=[pltpu.VMEM((tr, IN_PADDED), BF16)]),
        out_shape=[jax.ShapeDtypeStruct((R, MAIN_W), BF16), jax.ShapeDtypeStruct((R, BLK), BF16)],
        compiler_params=_params(("arbitrary",)),
    )(chip, gathered, own)


def _w_in_grad_shards(g_ret, g_fox, g_ff):
    R = g_ret.shape[0]
    C = IN_WIDTH // N_CHIPS
    tr = R // 4

    def body(gr_ref, gx_ref, gf_ref, o_ref, full):
        full[:, 0:RET_W] = gr_ref[...]
        for src, dst in _fox_column_blocks():
            full[:, src:src + BLK] = gx_ref[:, dst - RET_W:dst - RET_W + BLK]
        full[:, MAIN_W:MAIN_W + FOX_HEADS] = gf_ref[...]
        for j in range(N_CHIPS):
            o_ref[j] = full[:, j * C:(j + 1) * C].astype(BF16)

    rows = lambda w: pl.BlockSpec((tr, w), lambda i: (i, 0))
    return pl.pallas_call(
        body, name="w_in_grad_shards", grid=(R // tr,),
        in_specs=[rows(RET_W), rows(FOX_W), rows(FOX_HEADS)],
        out_specs=pl.BlockSpec((N_CHIPS, tr, C), lambda i: (0, i, 0)),
        out_shape=jax.ShapeDtypeStruct((N_CHIPS, R, C), BF16),
        scratch_shapes=[pltpu.VMEM((tr, IN_PADDED), F32)],
        compiler_params=_params(("parallel",)),
    )(g_ret, g_fox, g_ff)


def kernel(x, meta_tokens, attn_norm_g, w_in, fox_forget_b, ret_norm_g, w_out, ffn_norm_g, w_up, conv_w, conv_b, w_down, final_norm_g, loss_target, m_meta_tokens, m_attn_norm_g, m_w_in, m_fox_forget_b, m_ret_norm_g, m_w_out, m_ffn_norm_g, m_w_up, m_conv_w, m_conv_b, m_w_down, m_final_norm_g, v_meta_tokens, v_attn_norm_g, v_w_in, v_fox_forget_b, v_ret_norm_g, v_w_out, v_ffn_norm_g, v_w_up, v_conv_w, v_conv_b, v_w_down, v_final_norm_g):
    chip = 2 * lax.axis_index("x") + lax.axis_index("y")
    core = lax.axis_index("c")

    small_w = _pack_rows([meta_tokens, conv_w[0]], 8)
    w_in_b = w_in[0].astype(BF16)
    g_in, g_small = _chip_allgather_halves(w_in_b, small_w)
    chip_idx = chip.reshape(1).astype(jnp.int32)
    w_main, w_ff = _w_in_kernel_order(g_in.reshape((N_CHIPS,) + w_in_b.shape), w_in_b, chip_idx)
    small_parts = [_unpack_rows(g_small[j], [meta_tokens.shape, conv_w.shape[1:]]) for j in range(N_CHIPS)]
    meta_full = jnp.concatenate([sp[0] for sp in small_parts], axis=1)
    conv_w_full = jnp.concatenate([sp[1] for sp in small_parts], axis=1)

    core_idx = core.reshape(1).astype(jnp.int32)
    place = jnp.stack([chip, core]).astype(jnp.int32)

    def assemble(gathered):
        g_out, g_up, g_down = gathered
        return g_out.reshape(D_MODEL, D_MODEL), g_up, g_down.reshape(D_FF, D_MODEL)

    def early_arrays(d_w_out, d_w_up, d_w_down):
        return [d_w_out.reshape(N_CHIPS, -1, D_MODEL), d_w_up, d_w_down.reshape(N_CHIPS, -1, D_MODEL)]

    def in_sums(d_w_ret, d_w_fox, d_w_ff):
        g_in_full = _w_in_grad_shards(d_w_ret, d_w_fox, d_w_ff)
        (from_sib,) = _sibling_halves([g_in_full])
        return [_pair_add(g_in_full, from_sib, core_idx, "pair_add_in")]

    def early_sums(early, from_sib):
        return [_pair_add(g, r, core_idx, "pair_add_" + nm) for g, r, nm in zip(early, from_sib, ("out", "up", "down"))]

    out = _local_step(x[0], loss_target[0], meta_full, attn_norm_g, w_main, w_ff, fox_forget_b, ret_norm_g,
                      None, ffn_norm_g, None, conv_w_full, conv_b, None, final_norm_g[None],
                      late=([w_out[0].astype(BF16), w_up[0].astype(BF16), w_down[0].astype(BF16)], assemble),
                      mid=(early_arrays, early_sums), last=in_sums, wire=BF16)

    names = ("in", "out", "up", "down")
    totals = [_sum_partials(s, q, place, "sum_chips_" + nm) for s, q, nm in zip(out["scatter"], out["received"], names)]
    (grad_in, grad_out, grad_up, grad_down), small_all = _sibling_allgather(totals, _small_pack(out))

    big_w = [(w_out, m_w_out, v_w_out, grad_out, "adamw_out"), (w_up, m_w_up, v_w_up, grad_up, "adamw_up"),
             (w_down, m_w_down, v_w_down, grad_down, "adamw_down")]
    big_res = [[r[None] for r in _adamw(w[0], g, m[0], v[0], nm)] for w, m, v, g, nm in big_w]
    as_rows = lambda a: jnp.transpose(a, (2, 0, 1))
    in_rows = _adamw(as_rows(w_in), grad_in.T[:, None, :], as_rows(m_w_in), as_rows(v_w_in), "adamw_in",
                     tiles=_row_vector_tiles(w_in.shape[2]))
    big_res.insert(0, [jnp.transpose(r, (1, 2, 0)) for r in in_rows])
    tap_rows = lambda a: jnp.transpose(a, (1, 0, 2))
    small_p = [meta_tokens, attn_norm_g, fox_forget_b, ret_norm_g, ffn_norm_g, tap_rows(conv_w), conv_b, final_norm_g[None]]
    small_m = [m_meta_tokens, m_attn_norm_g, m_fox_forget_b, m_ret_norm_g, m_ffn_norm_g, tap_rows(m_conv_w), m_conv_b,
               m_final_norm_g[None]]
    small_v = [v_meta_tokens, v_attn_norm_g, v_fox_forget_b, v_ret_norm_g, v_ffn_norm_g, tap_rows(v_conv_w), v_conv_b,
               v_final_norm_g[None]]
    loss_row, *small_res = _small_update(small_all, chip_idx, small_p, small_m, small_v)
    loss = loss_row[0, 0]

    def ordered(kind):
        sm = list(small_res[kind][:-1]) + [small_res[kind][-1][0]]
        sm[5] = tap_rows(sm[5])
        bg = [r[kind] for r in big_res]
        return [sm[0], sm[1], bg[0], sm[2], sm[3], bg[1], sm[4], bg[2], sm[5], sm[6], bg[3], sm[7]]

    return (loss, out["dx"][None], *ordered(0), *ordered(1), *ordered(2), *ordered(3))
```

```python
import functools

import numpy as np
import jax
import jax.numpy as jnp
from jax import lax
from jax.experimental import pallas as pl
from jax.experimental.pallas import tpu as pltpu

F32 = jnp.float32
BF16 = jnp.bfloat16

D_MODEL = 1024
N_META = 16
BLK = 128
UNIT = 2 * BLK
FOX_PAIRS = 2
WIDE = 4
CHUNK = 64
N_PAD = BLK - N_META
PREFIX = BLK
RET_HEADS = 4
FOX_HEADS = 8
HEAD_LANES = 64
D_FF = 2816
ROPE_BASE = 10000.0
EPS = 1e-6
NEG = -1e30
LOG2E = 1.4426950408889634
RET_W = 1536
FOX_W = 1536
MAIN_W = RET_W + FOX_W
IN_WIDTH = MAIN_W + FOX_HEADS
N_CHIPS = 4
N_DEV = 8

ADAM_LR = 0.001
ADAM_B1 = 0.9
ADAM_B2 = 0.999
ADAM_EPS = 1e-08
ADAM_WD = 0.01
ADAM_STEP = 10

MESH = pl.DeviceIdType.MESH
VMEM_LIMIT_MB = 56

_NT = (((1,), (1,)), ((), ()))
_TN = (((0,), (0,)), ((), ()))


def _dot(a, b):
    return jnp.dot(a, b, preferred_element_type=F32)


def _dot_nt(a, b):
    return lax.dot_general(a, b, _NT, preferred_element_type=F32)


def _dot_tn(a, b):
    return lax.dot_general(a, b, _TN, preferred_element_type=F32)


def _params(dims=None, vmem_mb=VMEM_LIMIT_MB):
    kw = dict(vmem_limit_bytes=vmem_mb << 20)
    if dims is not None:
        kw["dimension_semantics"] = dims
    return pltpu.CompilerParams(**kw)


def _row_tile(n, prefs=(384, 256, 128)):
    for t in prefs:
        if n % t == 0:
            return t
    raise ValueError(f"no row tile for {n}")


def _iota(shape, dim):
    return lax.broadcasted_iota(jnp.int32, shape, dim)


def _pick_row(tile, row):
    sub = _iota(tile.shape, 0)
    return jnp.sum(jnp.where(sub == row, tile, 0.0), axis=0, keepdims=True)


def _split3(x):
    hi = x.astype(BF16)
    r1 = x - hi.astype(F32)
    mid = r1.astype(BF16)
    lo = (r1 - mid.astype(F32)).astype(BF16)
    return hi, mid, lo


def _full(shape):
    nd = len(shape)
    return pl.BlockSpec(shape, lambda *_: (0,) * nd)


def _in_perm():
    cols = list(range(RET_W))
    for p in range(FOX_HEADS // 2):
        for part in range(3):
            start = RET_W + part * 512 + p * BLK
            cols += list(range(start, start + BLK))
    return np.asarray(cols, np.int32)


def _rotary_tables(L):
    half = HEAD_LANES // 2
    inv = 1.0 / (ROPE_BASE ** (jnp.arange(half, dtype=F32) / half))
    ang = jnp.arange(L).astype(F32)[:, None] * inv[None, :]
    cos, sin = jnp.cos(ang), jnp.sin(ang)
    cos_t = jnp.tile(cos, (1, 4))
    sin_t = jnp.tile(jnp.concatenate([-sin, sin], axis=1), (1, 2))
    return cos_t, sin_t


def _decay_tables():
    gam = 1.0 - 2.0 ** (-5.0 - np.arange(RET_HEADS, dtype=np.float64))
    n = np.arange(BLK)
    same_or_past = (n[:, None] // CHUNK) >= (n[None, :] // CHUNK)
    dist = np.abs(n[:, None] - n[None, :])
    dmat = np.stack([np.where(same_or_past, g ** dist, 0.0) for g in gam]).astype(np.float32)
    lane_head = np.arange(BLK) // HEAD_LANES
    wq = np.stack([gam[2 * p + lane_head][None, :] ** (n[:, None] + 1.0) for p in range(2)]).astype(np.float32)
    wk = np.stack([gam[2 * p + lane_head][None, :] ** (BLK - 1.0 - n[:, None]) for p in range(2)]).astype(np.float32)
    g_blk = tuple(float(g ** BLK) for g in gam)
    return jnp.asarray(dmat), jnp.asarray(wq), jnp.asarray(wk), g_blk


def _shifted_blocks(tm):
    nb = tm // BLK
    return [pl.BlockSpec((BLK, D_MODEL), lambda i, j=j: (jnp.maximum(nb * i + j - 1, 0), 0)) for j in range(nb)]


def _rms_inproj(head, x, g, w_main, w_ff):
    L = x.shape[0] + BLK
    tm = _row_tile(L)
    nb = tm // BLK

    def body(head_ref, *refs):
        x_refs, (g_ref, wm_ref, wf_ref, h_ref, n_ref, p_ref, ff_ref) = refs[:nb], refs[nb:]
        parts = [r[...] for r in x_refs]
        parts[0] = jnp.where(pl.program_id(0) == 0, head_ref[...], parts[0])
        h = jnp.concatenate(parts, axis=0)
        h_ref[...] = h
        r = lax.rsqrt(jnp.mean(h * h, axis=-1, keepdims=True) + EPS)
        n = (h * r * g_ref[...]).astype(BF16)
        n_ref[...] = n
        p_ref[...] = _dot(n, wm_ref[...]).astype(BF16)
        ff_ref[...] = _dot(n, wf_ref[...])

    rows = lambda w: pl.BlockSpec((tm, w), lambda i: (i, 0))
    return pl.pallas_call(
        body, name="f_inproj", grid=(L // tm,),
        in_specs=[_full((BLK, D_MODEL))] + _shifted_blocks(tm)
        + [_full((1, D_MODEL)), _full((D_MODEL, MAIN_W)), _full((D_MODEL, BLK))],
        out_specs=[rows(D_MODEL), rows(D_MODEL), rows(MAIN_W), rows(BLK)],
        out_shape=[jax.ShapeDtypeStruct((L, D_MODEL), F32), jax.ShapeDtypeStruct((L, D_MODEL), BF16),
                   jax.ShapeDtypeStruct((L, MAIN_W), BF16), jax.ShapeDtypeStruct((L, BLK), F32)],
        compiler_params=_params(("parallel",)),
    )(head, *([x] * nb), g, w_main, w_ff)


SMALL_GROUP = 11


def _block_group(nblk, most=3):
    return next(g for g in range(most, 0, -1) if nblk % g == 0)


def _fox_prep(ff, fb):
    L = ff.shape[0]
    nblk = L // BLK
    G = _block_group(nblk, SMALL_GROUP)

    def body(ff_ref, b_ref, c_ref, ct_ref, carry):
        @pl.when(pl.program_id(0) == 0)
        def _():
            carry[...] = jnp.zeros_like(carry)

        tri = (_iota((BLK, BLK), 0) >= _iota((BLK, BLK), 1)).astype(BF16)
        live = _iota((BLK, BLK), 1) < FOX_HEADS
        run = carry[...]
        for b in range(G):
            z = ff_ref[b * BLK:(b + 1) * BLK, :] + b_ref[...]
            lf = jnp.where(live, jnp.minimum(z, 0.0) - jnp.log1p(jnp.exp(-jnp.abs(z))), 0.0)
            hi, mid, lo = _split3(lf)
            cs = (_dot(tri, hi) + _dot(tri, mid) + _dot(tri, lo) + run) * LOG2E
            c_ref[b * BLK:(b + 1) * BLK, :] = cs
            ct_ref[b] = cs.T[0:8, :]
            run = run + jnp.sum(lf, axis=0, keepdims=True)
        carry[...] = run

    return pl.pallas_call(
        body, name="f_foxprep", grid=(nblk // G,),
        in_specs=[pl.BlockSpec((G * BLK, BLK), lambda i: (i, 0)), _full((1, BLK))],
        out_specs=[pl.BlockSpec((G * BLK, BLK), lambda i: (i, 0)), pl.BlockSpec((G, 8, BLK), lambda i: (i, 0, 0))],
        out_shape=[jax.ShapeDtypeStruct((L, BLK), F32), jax.ShapeDtypeStruct((nblk, 8, BLK), F32)],
        scratch_shapes=[pltpu.VMEM((1, BLK), F32)],
        compiler_params=_params(("arbitrary",)),
    )(ff, fb)


def _rot_fns(cos, sin):
    lane = _iota((BLK, BLK), 1)
    first = (lane & (HEAD_LANES - 1)) < HEAD_LANES // 2

    def swap(x):
        return jnp.where(first, pltpu.roll(x, BLK - 32, 1), pltpu.roll(x, 32, 1))

    def rot(x):
        return x * cos + swap(x) * sin

    def rot_t(dy):
        return dy * cos + swap(dy * sin)

    return rot, rot_t


def _retention_fwd(proj, cos_t, sin_t, ret_g):
    L = proj.shape[0]
    nblk = L // BLK
    G = _block_group(nblk)
    dmat, wq_t, wk_t, g_blk = _decay_tables()

    def body(q_ref, k_ref, v_ref, gate_ref, cos_ref, sin_ref, d_ref, wq_ref, wk_ref, rg_ref,
             mix_ref, o_ref, rs_ref, state):
        @pl.when(pl.program_id(0) == 0)
        def _():
            state[...] = jnp.zeros_like(state)

        lane = _iota((BLK, BLK), 1)
        sub = _iota((BLK, BLK), 0)
        for b in range(G):
            rows = slice(b * BLK, (b + 1) * BLK)
            rot, _ = _rot_fns(cos_ref[rows, :], sin_ref[rows, :])
            for p in range(2):
                qr = rot(q_ref[rows, p * BLK:(p + 1) * BLK].astype(F32))
                kr = rot(k_ref[rows, p * BLK:(p + 1) * BLK].astype(F32)) * (HEAD_LANES ** -0.5)
                kr_b = kr.astype(BF16)
                qw = (qr * wq_ref[p]).astype(BF16)
                kw = (kr * wk_ref[p]).astype(BF16)
                for e in range(2):
                    h = 2 * p + e
                    cols = slice(h * BLK, (h + 1) * BLK)
                    qm = jnp.where((lane >> 6) == e, qr, 0.0).astype(BF16)
                    s = _dot_nt(qm, kr_b) * d_ref[h]
                    vh = v_ref[rows, cols]
                    st = state[h]
                    rs_ref[b, h] = st
                    o = _dot(s.astype(BF16), vh) + _dot(qw, st.astype(BF16))
                    u = jnp.where((sub >> 6) == e, _dot_tn(kw, vh), 0.0)
                    state[h] = g_blk[h] * st + u
                    rn = lax.rsqrt(jnp.mean(o * o, axis=-1, keepdims=True) + EPS)
                    gate = gate_ref[rows, cols].astype(F32)
                    o_ref[rows, cols] = o
                    mix_ref[rows, cols] = (o * rn * rg_ref[:, cols] * (gate * jax.nn.sigmoid(gate))).astype(BF16)

    row = lambda c: (lambda i: (i, c))
    return pl.pallas_call(
        body, name="f_retention", grid=(nblk // G,),
        in_specs=[pl.BlockSpec((G * BLK, 256), row(0)), pl.BlockSpec((G * BLK, 256), row(1)),
                  pl.BlockSpec((G * BLK, 512), row(1)), pl.BlockSpec((G * BLK, 512), row(2)),
                  pl.BlockSpec((G * BLK, BLK), row(0)), pl.BlockSpec((G * BLK, BLK), row(0)),
                  _full((RET_HEADS, BLK, BLK)), _full((2, BLK, BLK)), _full((2, BLK, BLK)), _full((1, 512))],
        out_specs=[pl.BlockSpec((G * BLK, 512), row(0)), pl.BlockSpec((G * BLK, 512), row(0)),
                   pl.BlockSpec((G, RET_HEADS, BLK, BLK), lambda i: (i, 0, 0, 0))],
        out_shape=[jax.ShapeDtypeStruct((L, 512), BF16), jax.ShapeDtypeStruct((L, 512), F32),
                   jax.ShapeDtypeStruct((nblk, RET_HEADS, BLK, BLK), F32)],
        scratch_shapes=[pltpu.VMEM((RET_HEADS, BLK, BLK), F32)],
        compiler_params=_params(("arbitrary",)),
    )(proj, proj, proj, proj, cos_t, sin_t, dmat, wq_t, wk_t, ret_g)


def _fox_units(L):
    nblk = L // BLK
    assert L % BLK == 0 and nblk % 2 == 1, "sequence must be one 128-row block plus whole 256-row tiles"
    return nblk, (nblk - 1) // 2


def _fox_tile_masks():
    sub, lane = _iota((BLK, BLK), 0), _iota((BLK, BLK), 1)
    valid = _iota((BLK, UNIT), 0) >= N_PAD
    diag = _iota((UNIT, UNIT), 0) <= _iota((UNIT, UNIT), 1)
    r, q = _iota((BLK + UNIT, UNIT), 0), _iota((BLK + UNIT, UNIT), 1)
    first_and_diag = ((r < BLK) & (r >= N_PAD)) | ((r >= BLK) & (r - BLK <= q))
    return dict(first=(sub <= lane) & (sub >= N_PAD), valid=valid, diag=diag, first_and_diag=first_and_diag)


def _fox_fwd(proj, c, ctb, gather=()):
    L = proj.shape[0]
    nblk, nu = _fox_units(L)
    scale = HEAD_LANES ** -0.5 * LOG2E
    ng = len(gather)
    steps = FOX_HEADS // (2 * FOX_PAIRS)

    def body(qkv_ref, c_ref, ct_ref, *rest):
        g_in, (of_ref, lse_ref), g_out = rest[:ng], rest[ng:ng + 2], rest[ng + 2:2 * ng + 2]
        vt, csb = rest[2 * ng + 2:2 * ng + 4]
        p = pl.program_id(0)
        heads = [(pp, e, 2 * FOX_PAIRS * p + 2 * pp + e) for pp in range(FOX_PAIRS) for e in range(2)]

        @pl.when(p == 0)
        def _():
            lse_ref[...] = jnp.zeros_like(lse_ref)
            if ng:
                local, sends, _ = _allgather_copies(g_in, g_out, *rest[2 * ng + 4:])
                for cp in local + sends:
                    cp.start()

        lane = _iota((BLK, BLK), 1)
        sub8 = _iota((8, BLK), 0)
        masks = _fox_tile_masks()

        def pre(j, carry):
            off = pl.multiple_of(j * BLK, BLK)
            ct = c_ref[pl.ds(off, BLK), :]
            for pp in range(FOX_PAIRS):
                vt[pp, j] = qkv_ref[pl.ds(off, BLK), pp * 384 + 2 * BLK:pp * 384 + 3 * BLK].astype(F32).T.astype(BF16)
            for hh, (_, _, h) in enumerate(heads):
                col = jnp.sum(jnp.where(lane == h, ct, 0.0), axis=1, keepdims=True)
                csb[hh, j] = jnp.broadcast_to(col, (BLK, BLK))
            return carry

        lax.fori_loop(0, nblk, pre, 0)

        def attend(qblk, nq, n_whole):
            qlen = nq * BLK
            qoff = pl.multiple_of(qblk * BLK, BLK)
            qlane = _iota((qlen, BLK), 1)
            qs = [qkv_ref[pl.ds(qoff, qlen), pp * 384:pp * 384 + BLK].astype(F32) * scale for pp in range(FOX_PAIRS)]
            qm = [jnp.where((qlane >> 6) == e, qs[pp], 0.0).astype(BF16) for pp, e, _ in heads]
            ct_row = [jnp.concatenate([_pick_row(ct_ref[qblk + a], h) for a in range(nq)], axis=1) for _, _, h in heads]

            def step(segs, mask, st):
                blocks = [kblk + b for kblk, nk in segs for b in range(nk)]
                kts = []
                for pp in range(FOX_PAIRS):
                    kt = [qkv_ref[pl.ds(pl.multiple_of(kblk * BLK, BLK), nk * BLK), pp * 384 + BLK:pp * 384 + 2 * BLK]
                          for kblk, nk in segs]
                    kts.append(kt[0] if len(kt) == 1 else jnp.concatenate(kt, axis=0))
                out = []
                for hh, (pp, e, _) in enumerate(heads):
                    m, l, acc = st[3 * hh:3 * hh + 3]
                    s = _dot_nt(kts[pp], qm[hh])
                    t = jnp.concatenate([s[b * BLK:(b + 1) * BLK] - jnp.concatenate([csb[hh, blk]] * nq, axis=1)
                                         for b, blk in enumerate(blocks)], axis=0)
                    if mask is not None:
                        t = jnp.where(mask, t, NEG)
                    m_new = jnp.maximum(m, jnp.max(t, axis=0, keepdims=True) + ct_row[hh])
                    alpha = jnp.exp2(m - m_new)
                    pr = jnp.exp2(t - (m_new - ct_row[hh]))
                    l = alpha * l + jnp.sum(pr, axis=0, keepdims=True)
                    pr_b = pr.astype(BF16)
                    pv = None
                    for b, blk in enumerate(blocks):
                        part = _dot(vt[pp, blk, e * HEAD_LANES:(e + 1) * HEAD_LANES, :], pr_b[b * BLK:(b + 1) * BLK])
                        pv = part if pv is None else pv + part
                    out += [m_new, l, alpha * acc + pv]
                return tuple(out)

            st = (jnp.full((1, qlen), NEG, F32), jnp.zeros((1, qlen), F32),
                  jnp.zeros((HEAD_LANES, qlen), F32)) * len(heads)
            if nq == 1:
                st = step([(0, 1)], masks["first"], st)
            else:
                st = step([(0, 1), (qblk, 2)], masks["first_and_diag"], st)
                n_wide = n_whole // WIDE
                st = lax.fori_loop(0, n_wide, lambda j, s_: step([(1 + 2 * WIDE * j, 2 * WIDE)], None, s_), st)
                rest = 1 + 2 * WIDE * n_wide
                st = lax.cond((n_whole & 2) != 0, lambda s_: step([(rest, 4)], None, s_), lambda s_: s_, st)
                st = lax.cond((n_whole & 1) != 0, lambda s_: step([(rest + 2 * (n_whole & 2), 2)], None, s_),
                              lambda s_: s_, st)
            for pp in range(FOX_PAIRS):
                lo, hi = st[6 * pp:6 * pp + 3], st[6 * pp + 3:6 * pp + 6]
                o_t = jnp.concatenate([lo[2] * (1.0 / lo[1]), hi[2] * (1.0 / hi[1])], axis=0)
                of_ref[pl.ds(qoff, qlen), pp * BLK:(pp + 1) * BLK] = o_t.T.astype(BF16)
            lse = [st[3 * hh] + jnp.log(st[3 * hh + 1]) * LOG2E for hh in range(len(heads))]
            for a in range(nq):
                upd = jnp.zeros((8, BLK), F32)
                for hh, (_, _, h) in enumerate(heads):
                    upd = upd + jnp.where(sub8 == h, lse[hh][:, a * BLK:(a + 1) * BLK], 0.0)
                lse_ref[qblk + a] = lse_ref[qblk + a] + upd

        attend(0, 1, 0)

        def q_loop(u, carry):
            attend(1 + 2 * u, 2, u)
            return carry

        lax.fori_loop(0, nu, q_loop, 0)

        if ng:
            @pl.when(p == steps - 1)
            def _():
                local, sends, recvs = _allgather_copies(g_in, g_out, *rest[2 * ng + 4:])
                for cp in recvs:
                    cp.wait_recv()
                for cp in sends:
                    cp.wait_send()
                for cp in local:
                    cp.wait()

    width = 384 * FOX_PAIRS
    return pl.pallas_call(
        body, name="f_fox", grid=(steps,),
        in_specs=[pl.BlockSpec((L, width), lambda p: (0, RET_W // width + p)), _full((L, BLK)), _full((nblk, 8, BLK))]
        + [_ANY] * ng,
        out_specs=[pl.BlockSpec((L, FOX_PAIRS * BLK), lambda p: (0, p)), _full((nblk, 8, BLK))] + [_ANY] * ng,
        out_shape=[jax.ShapeDtypeStruct((L, 512), BF16), jax.ShapeDtypeStruct((nblk, 8, BLK), F32)]
        + [jax.ShapeDtypeStruct((N_CHIPS,) + a.shape, a.dtype) for a in gather],
        scratch_shapes=[pltpu.VMEM((FOX_PAIRS, nblk, BLK, BLK), BF16), pltpu.VMEM((2 * FOX_PAIRS, nblk, BLK, BLK), F32)]
        + _allgather_semaphores(ng),
        compiler_params=_params(("arbitrary",)),
    )(proj, c, ctb, *gather)


def _outproj_up(mix_r, o_f, h0, w_out, ffn_g, w_up, conv_w, conv_b):
    L = h0.shape[0]
    tm = _row_tile(L)
    shard = w_up.shape[2]
    assert 2 * shard == D_FF
    cw = [conv_w[j:j + 1] for j in range(3)]
    resident = lambda shape: pl.BlockSpec(shape, lambda i: (0,) * len(shape), pipeline_mode=pl.Buffered(1))

    def body(mr_ref, of_ref, h0_ref, wo_ref, g_ref, wu_ref, cw0, cw1, cw2, cb_ref,
             h1_ref, n2_ref, up_ref, act_ref, acc_ref, halo):
        i = pl.program_id(0)

        @pl.when(i == 0)
        def _():
            halo[...] = jnp.zeros_like(halo)

        h1 = h0_ref[...] + _dot(mr_ref[...], wo_ref[0:512, :]) + _dot(of_ref[...], wo_ref[512:1024, :])
        h1_ref[...] = h1
        r = lax.rsqrt(jnp.mean(h1 * h1, axis=-1, keepdims=True) + EPS)
        n2 = (h1 * r * g_ref[...]).astype(BF16)
        n2_ref[...] = n2
        live = i * tm + _iota((tm, 1), 0) >= N_PAD
        for half in range(2):
            cols = slice(half * shard, (half + 1) * shard)
            a_b = _dot(n2, wu_ref[half]).astype(BF16)
            b_b = _dot(n2, wu_ref[2 + half]).astype(BF16)
            up_ref[:, cols] = a_b
            up_ref[:, D_FF + half * shard:D_FF + (half + 1) * shard] = b_b
            a = jnp.where(live, a_b.astype(F32), 0.0)
            _, _, acc = _conv_taps(a, halo[:, cols], [cw0[:, cols], cw1[:, cols], cw2[:, cols]], cb_ref[:, cols])
            act_ref[:, cols] = (acc * jax.nn.sigmoid(acc) * b_b.astype(F32)).astype(BF16)
            acc_ref[:, cols] = acc.astype(BF16)
            halo[:, cols] = a[tm - 8:tm, :]

    rows = lambda w: pl.BlockSpec((tm, w), lambda i: (i, 0))
    return pl.pallas_call(
        body, name="f_outproj_up", grid=(L // tm,),
        in_specs=[rows(512), rows(512), rows(D_MODEL), resident((D_MODEL, D_MODEL)), _full((1, D_MODEL)),
                  resident((N_CHIPS, D_MODEL, shard)), _full((1, D_FF)), _full((1, D_FF)), _full((1, D_FF)),
                  _full((1, D_FF))],
        out_specs=[rows(D_MODEL), rows(D_MODEL), rows(2 * D_FF), rows(D_FF), rows(D_FF)],
        out_shape=[jax.ShapeDtypeStruct((L, D_MODEL), F32), jax.ShapeDtypeStruct((L, D_MODEL), BF16),
                   jax.ShapeDtypeStruct((L, 2 * D_FF), BF16), jax.ShapeDtypeStruct((L, D_FF), BF16),
                   jax.ShapeDtypeStruct((L, D_FF), BF16)],
        scratch_shapes=[pltpu.VMEM((8, D_FF), F32)],
        compiler_params=_params(("arbitrary",)),
    )(mix_r, o_f, h0, w_out, ffn_g, w_up, cw[0], cw[1], cw[2], conv_b)


def _conv_taps(a, halo, cw, cb):
    sub = _iota((a.shape[0], 1), 0)
    a1 = jnp.where(sub == 0, _pick_row(halo, 7), pltpu.roll(a, 1, 0))
    a2 = jnp.where(sub == 0, _pick_row(halo, 6), jnp.where(sub == 1, _pick_row(halo, 7), pltpu.roll(a, 2, 0)))
    acc = cb + a2 * cw[0]
    acc = acc + a1 * cw[1]
    acc = acc + a * cw[2]
    return a1, a2, acc


def _ffn_down_loss(g_act, w_down, h1, final_g, target, acc_saved, up):
    L = h1.shape[0]
    tm = _row_tile(L)
    nb = tm // BLK
    half_w = D_FF // 2

    def body(g_ref, wd_ref, h1_ref, gf_ref, acc_ref, b_ref, *refs):
        t_refs, (dh_ref, dhb_ref, dgf_ref, loss_ref, dacc_ref, db_ref) = refs[:nb], refs[nb:]
        i = pl.program_id(0)

        @pl.when(i == 0)
        def _():
            dgf_ref[...] = jnp.zeros_like(dgf_ref)
            loss_ref[...] = jnp.zeros_like(loss_ref)

        h2 = h1_ref[...] + _dot(g_ref[...], wd_ref[...])
        r = lax.rsqrt(jnp.mean(h2 * h2, axis=-1, keepdims=True) + EPS)
        yn = h2 * r
        gf = gf_ref[...]
        live = i * tm + _iota((tm, 1), 0) >= PREFIX
        target = jnp.concatenate([t[...] for t in t_refs], axis=0)
        err = jnp.where(live, yn * gf - target, 0.0)
        loss_ref[...] = loss_ref[...] + 0.5 * jnp.sum(jnp.mean(err * err, axis=-1, keepdims=True))
        dy = err * (1.0 / D_MODEL)
        dgf_ref[...] = dgf_ref[...] + jnp.sum(dy * yn, axis=0, keepdims=True)
        dyn = dy * gf
        dh = r * (dyn - yn * jnp.mean(dyn * yn, axis=-1, keepdims=True))
        dh_ref[...] = dh
        dhb = dh.astype(BF16)
        dhb_ref[...] = dhb
        for half in range(2):
            cols = slice(half * half_w, (half + 1) * half_w)
            acc = acc_ref[:, cols].astype(F32)
            dg = _dot_nt(dhb, wd_ref[cols, :])
            sg = jax.nn.sigmoid(acc)
            silu = acc * sg
            db_ref[:, cols] = (dg * silu).astype(BF16)
            dacc_ref[:, cols] = (dg * b_ref[:, cols].astype(F32) * (sg + silu * (1.0 - sg))).astype(BF16)

    rows = lambda w, c=0: pl.BlockSpec((tm, w), lambda i: (i, c))
    return pl.pallas_call(
        body, name="f_ffn_down_loss", grid=(L // tm,),
        in_specs=[rows(D_FF), pl.BlockSpec((D_FF, D_MODEL), lambda i: (0, 0), pipeline_mode=pl.Buffered(1)),
                  rows(D_MODEL), _full((1, D_MODEL)), rows(D_FF), rows(D_FF, 1)] + _shifted_blocks(tm),
        out_specs=[rows(D_MODEL), rows(D_MODEL), _full((1, D_MODEL)), _full((1, BLK)), rows(D_FF), rows(D_FF)],
        out_shape=[jax.ShapeDtypeStruct((L, D_MODEL), F32), jax.ShapeDtypeStruct((L, D_MODEL), BF16),
                   jax.ShapeDtypeStruct((1, D_MODEL), F32), jax.ShapeDtypeStruct((1, BLK), F32),
                   jax.ShapeDtypeStruct((L, D_FF), BF16), jax.ShapeDtypeStruct((L, D_FF), BF16)],
        compiler_params=_params(("arbitrary",)),
    )(g_act, w_down, h1, final_g, acc_saved, up, *([target] * nb))


def _ffn_bwd_up(dacc, db, up, conv_w, w_up, h1, ffn_g, dh2, w_out):
    L = h1.shape[0]
    tm = _row_tile(L)
    nt = L // tm
    shard = w_up.shape[2]
    cw = [conv_w[j:j + 1] for j in range(3)]

    def body(da_ref, halo_ref, db_ref, a_ref, cw0, cw1, cw2, wu_ref, h1_ref, g_ref, dh2_ref, wo_ref,
             dup_ref, dh1_ref, dh1b_ref, dmix_ref, dg_ref, dcw_ref):
        i = pl.program_id(0)

        @pl.when(i == 0)
        def _():
            dg_ref[...] = jnp.zeros_like(dg_ref)
            dcw_ref[...] = jnp.zeros_like(dcw_ref)

        sub = _iota((tm, 1), 0)
        sub8 = _iota((8, 1), 0)
        last_tile = i == nt - 1
        dbv = db_ref[...]
        dup_ref[:, D_FF:2 * D_FF] = dbv
        dn = _dot_nt(dbv[:, 0:shard], wu_ref[2]) + _dot_nt(dbv[:, shard:2 * shard], wu_ref[3])
        for half in range(2):
            cols = slice(half * shard, (half + 1) * shard)
            d0 = da_ref[:, cols].astype(F32)
            halo = jnp.where(last_tile, 0.0, halo_ref[:, cols].astype(F32))
            d1 = jnp.where(sub == tm - 1, _pick_row(halo, 0), pltpu.roll(d0, tm - 1, 0))
            d2 = jnp.where(sub == tm - 2, _pick_row(halo, 0),
                           jnp.where(sub == tm - 1, _pick_row(halo, 1), pltpu.roll(d0, tm - 2, 0)))
            a = a_ref[:, cols].astype(F32)
            upd = jnp.zeros((8, shard), F32)
            for j, t in enumerate((d2 * a, d1 * a, d0 * a, d0)):
                upd = upd + jnp.where(sub8 == j, jnp.sum(t, axis=0, keepdims=True), 0.0)
            dcw_ref[:, cols] = dcw_ref[:, cols] + upd
            da = (d0 * cw2[:, cols] + d1 * cw1[:, cols] + d2 * cw0[:, cols]).astype(BF16)
            dup_ref[:, cols] = da
            dn = dn + _dot_nt(da, wu_ref[half])
        h1 = h1_ref[...]
        r = lax.rsqrt(jnp.mean(h1 * h1, axis=-1, keepdims=True) + EPS)
        yn = h1 * r
        dg_ref[...] = dg_ref[...] + jnp.sum(dn * yn, axis=0, keepdims=True)
        dyn = dn * g_ref[...]
        dh1 = dh2_ref[...] + r * (dyn - yn * jnp.mean(dyn * yn, axis=-1, keepdims=True))
        dh1_ref[...] = dh1
        dh1b = dh1.astype(BF16)
        dh1b_ref[...] = dh1b
        dmix_ref[...] = _dot_nt(dh1b, wo_ref[...]).astype(BF16)

    rows = lambda w: pl.BlockSpec((tm, w), lambda i: (i, 0))
    halo = pl.BlockSpec((8, D_FF), lambda i: (jnp.minimum((i + 1) * (tm // 8), L // 8 - 1), 0))
    return pl.pallas_call(
        body, name="b_ffn_up", grid=(nt,),
        in_specs=[rows(D_FF), halo, rows(D_FF), rows(D_FF), _full((1, D_FF)), _full((1, D_FF)), _full((1, D_FF)),
                  _full((N_CHIPS, D_MODEL, shard)), rows(D_MODEL), _full((1, D_MODEL)), rows(D_MODEL),
                  _full((D_MODEL, D_MODEL))],
        out_specs=[rows(2 * D_FF), rows(D_MODEL), rows(D_MODEL), rows(D_MODEL), _full((1, D_MODEL)),
                   _full((8, D_FF))],
        out_shape=[jax.ShapeDtypeStruct((L, 2 * D_FF), BF16), jax.ShapeDtypeStruct((L, D_MODEL), F32),
                   jax.ShapeDtypeStruct((L, D_MODEL), BF16), jax.ShapeDtypeStruct((L, D_MODEL), BF16),
                   jax.ShapeDtypeStruct((1, D_MODEL), F32), jax.ShapeDtypeStruct((8, D_FF), F32)],
        compiler_params=_params(("arbitrary",)),
    )(dacc, dacc, db, up, cw[0], cw[1], cw[2], w_up, h1, ffn_g, dh2, w_out)


def _wgrad(a, b, name, tn=None, tk=None, out_dtype=F32):
    L, K = a.shape
    N = b.shape[1]
    tn = N if tn is None else tn
    tk = K if tk is None else tk
    tl = _row_tile(L, (1408, 768, 512, 256, 128))
    nl = L // tl

    def body(a_ref, b_ref, o_ref, acc):
        step = pl.program_id(2)

        @pl.when(step == 0)
        def _():
            acc[...] = jnp.zeros_like(acc)

        acc[...] = acc[...] + _dot_tn(a_ref[...], b_ref[...])

        @pl.when(step == nl - 1)
        def _():
            o_ref[0] = acc[...].astype(out_dtype)

    return pl.pallas_call(
        body, name=name, grid=(N // tn, K // tk, L // tl),
        in_specs=[pl.BlockSpec((tl, tk), lambda n, k, l: (l, k)), pl.BlockSpec((tl, tn), lambda n, k, l: (l, n))],
        out_specs=pl.BlockSpec((1, tk, tn), lambda n, k, l: (n, k, 0)),
        out_shape=jax.ShapeDtypeStruct((N // tn, K, tn), out_dtype),
        scratch_shapes=[pltpu.VMEM((tk, tn), F32)],
        compiler_params=_params(("parallel", "parallel", "arbitrary")),
    )(a, b)


def _retention_bwd(dmix, o, proj, cos_t, sin_t, ret_g, states, exchange=()):
    L = proj.shape[0]
    nblk = L // BLK
    G = _block_group(nblk)
    steps = nblk // G
    nx = len(exchange)
    dmat, wq_t, wk_t, g_blk = _decay_tables()

    def body(dm_ref, o_ref, q_ref, k_ref, v_ref, gate_ref, cos_ref, sin_ref, d_ref, wq_ref, wk_ref, rg_ref, rs_ref,
             *rest):
        x_in, (dp_ref, drg_ref), x_out, gstate = rest[:nx], rest[nx:nx + 2], rest[nx + 2:2 * nx + 2], rest[2 * nx + 2]

        @pl.when(pl.program_id(0) == 0)
        def _():
            if nx:
                for cp in _sibling_half_copies(x_in, x_out, *rest[2 * nx + 3:])[0]:
                    cp.start()
            gstate[...] = jnp.zeros_like(gstate)
            drg_ref[...] = jnp.zeros_like(drg_ref)

        lane = _iota((BLK, BLK), 1)
        sub = _iota((BLK, BLK), 0)
        scale = HEAD_LANES ** -0.5
        for b in reversed(range(G)):
            rows = slice(b * BLK, (b + 1) * BLK)
            rot, rot_t = _rot_fns(cos_ref[rows, :], sin_ref[rows, :])
            for p in range(2):
                qr = rot(q_ref[rows, p * BLK:(p + 1) * BLK].astype(F32))
                kr = rot(k_ref[rows, p * BLK:(p + 1) * BLK].astype(F32)) * scale
                kr_b = kr.astype(BF16)
                qw = (qr * wq_ref[p]).astype(BF16)
                kw = (kr * wk_ref[p]).astype(BF16)
                dqr = jnp.zeros((BLK, BLK), F32)
                dkr = jnp.zeros((BLK, BLK), F32)
                for e in range(2):
                    h = 2 * p + e
                    cols = slice(h * BLK, (h + 1) * BLK)
                    head_lanes = (lane >> 6) == e
                    o = o_ref[rows, cols]
                    rn = lax.rsqrt(jnp.mean(o * o, axis=-1, keepdims=True) + EPS)
                    y = o * rn
                    gate = gate_ref[rows, cols].astype(F32)
                    sg = jax.nn.sigmoid(gate)
                    dm = dm_ref[rows, cols].astype(F32)
                    rgain = rg_ref[:, cols]
                    drg_ref[:, cols] = drg_ref[:, cols] + jnp.sum(dm * y * (gate * sg), axis=0, keepdims=True)
                    dp_ref[rows, 1024 + h * BLK:1024 + (h + 1) * BLK] = (
                        dm * y * rgain * (sg * (1.0 + gate * (1.0 - sg)))).astype(BF16)
                    dy = dm * rgain * (gate * sg)
                    do = (rn * (dy - y * jnp.mean(dy * y, axis=-1, keepdims=True))).astype(BF16)
                    vh = v_ref[rows, cols]
                    qm = jnp.where(head_lanes, qr, 0.0).astype(BF16)
                    dmh = d_ref[h]
                    s = (_dot_nt(qm, kr_b) * dmh).astype(BF16)
                    ds = (_dot_nt(do, vh) * dmh).astype(BF16)
                    st = rs_ref[b, h].astype(BF16)
                    gs = gstate[h]
                    gs_b = gs.astype(BF16)
                    dqr = dqr + jnp.where(head_lanes, _dot(ds, kr_b), 0.0) + _dot_nt(do, st) * wq_ref[p]
                    dkr = dkr + _dot_tn(ds, qm) + _dot_nt(vh, gs_b) * wk_ref[p]
                    dp_ref[rows, 512 + h * BLK:512 + (h + 1) * BLK] = (_dot_tn(s, do) + _dot(kw, gs_b)).astype(BF16)
                    dr = jnp.where((sub >> 6) == e, _dot_tn(qw, do), 0.0)
                    gstate[h] = dr + g_blk[h] * gs
                dp_ref[rows, p * BLK:(p + 1) * BLK] = rot_t(dqr).astype(BF16)
                dp_ref[rows, 256 + p * BLK:256 + (p + 1) * BLK] = (rot_t(dkr) * scale).astype(BF16)

        if nx:
            @pl.when(pl.program_id(0) == steps - 1)
            def _():
                sends, recvs = _sibling_half_copies(x_in, x_out, *rest[2 * nx + 3:])
                for cp in recvs:
                    cp.wait_recv()
                for cp in sends:
                    cp.wait_send()

    row = lambda c: (lambda i: (steps - 1 - i, c))
    return pl.pallas_call(
        body, name="b_retention", grid=(steps,),
        in_specs=[pl.BlockSpec((G * BLK, 512), row(0)), pl.BlockSpec((G * BLK, 512), row(0)),
                  pl.BlockSpec((G * BLK, 256), row(0)), pl.BlockSpec((G * BLK, 256), row(1)),
                  pl.BlockSpec((G * BLK, 512), row(1)), pl.BlockSpec((G * BLK, 512), row(2)),
                  pl.BlockSpec((G * BLK, BLK), row(0)), pl.BlockSpec((G * BLK, BLK), row(0)),
                  _full((RET_HEADS, BLK, BLK)), _full((2, BLK, BLK)), _full((2, BLK, BLK)), _full((1, 512)),
                  pl.BlockSpec((G, RET_HEADS, BLK, BLK), lambda i: (steps - 1 - i, 0, 0, 0))] + [_ANY] * nx,
        out_specs=[pl.BlockSpec((G * BLK, RET_W), row(0)), _full((1, 512))] + [_ANY] * nx,
        out_shape=[jax.ShapeDtypeStruct((L, RET_W), BF16), jax.ShapeDtypeStruct((1, 512), F32)]
        + _sibling_half_shapes(exchange),
        scratch_shapes=[pltpu.VMEM((RET_HEADS, BLK, BLK), F32)] + _sibling_half_semaphores(nx),
        compiler_params=_params(("arbitrary",)),
    )(dmix, o, proj, proj, proj, proj, cos_t, sin_t, dmat, wq_t, wk_t, ret_g, states, *exchange)


def _fox_delta(dmix, o_f):
    L = o_f.shape[0]
    nblk = L // BLK
    G = _block_group(nblk, SMALL_GROUP)

    def body(do_ref, o_ref, d_ref):
        sel = ((_iota((8, 512), 1) >> 6) == _iota((8, 512), 0)).astype(BF16)
        for b in range(G):
            rows = slice(b * BLK, (b + 1) * BLK)
            prod = do_ref[rows, :].astype(F32) * o_ref[rows, :].astype(F32)
            hi = prod.astype(BF16)
            lo = (prod - hi.astype(F32)).astype(BF16)
            d_ref[b] = _dot_nt(sel, hi) + _dot_nt(sel, lo)

    return pl.pallas_call(
        body, name="b_foxdelta", grid=(nblk // G,),
        in_specs=[pl.BlockSpec((G * BLK, 512), lambda i: (i, 1)), pl.BlockSpec((G * BLK, 512), lambda i: (i, 0))],
        out_specs=pl.BlockSpec((G, 8, BLK), lambda i: (i, 0, 0)),
        out_shape=jax.ShapeDtypeStruct((nblk, 8, BLK), F32),
        compiler_params=_params(("parallel",)),
    )(dmix, o_f)


def _fox_bwd(proj, dmix, c, ctb, lse, delta, scatter=()):
    L = proj.shape[0]
    nblk, nu = _fox_units(L)
    scale = HEAD_LANES ** -0.5
    ns = len(scatter)

    steps = FOX_HEADS // (2 * FOX_PAIRS)

    def body(qkv_ref, do_ref, c_ref, ct_ref, lse_ref, dl_ref, *rest):
        s_in, (dp_ref, dc_ref, dcq_ref), s_out = rest[:ns], rest[ns:ns + 3], rest[ns + 3:2 * ns + 3]
        ktt, dqt, dk_acc, dv_acc, dcs_acc = rest[2 * ns + 3:2 * ns + 8]
        p = pl.program_id(0)
        heads = [(pp, e, 2 * FOX_PAIRS * p + 2 * pp + e) for pp in range(FOX_PAIRS) for e in range(2)]

        @pl.when(p == 0)
        def _():
            dc_ref[...] = jnp.zeros_like(dc_ref)
            dcq_ref[...] = jnp.zeros_like(dcq_ref)
            if ns:
                for cp in _scatter_copies(s_in, s_out, *rest[2 * ns + 8:]):
                    cp.start()

        sub8 = _iota((8, BLK), 0)
        masks = _fox_tile_masks()

        def pre(j, carry):
            off = pl.multiple_of(j * BLK, BLK)
            for pp in range(FOX_PAIRS):
                ktt[pp, j] = qkv_ref[pl.ds(off, BLK), pp * 384 + BLK:pp * 384 + 2 * BLK].astype(F32).T.astype(BF16)
                dqt[pp, j] = jnp.zeros((BLK, BLK), F32)
            return carry

        lax.fori_loop(0, nblk, pre, 0)

        def kv_pass(kblk, nk, n_later):
            klen = nk * BLK
            koff = pl.multiple_of(kblk * BLK, BLK)
            kt = [qkv_ref[pl.ds(koff, klen), pp * 384 + BLK:pp * 384 + 2 * BLK] for pp in range(FOX_PAIRS)]
            vtile = [qkv_ref[pl.ds(koff, klen), pp * 384 + 2 * BLK:pp * 384 + 3 * BLK] for pp in range(FOX_PAIRS)]
            ct = c_ref[pl.ds(koff, klen), :]
            klane = _iota((klen, BLK), 1)
            cs = [jnp.broadcast_to(jnp.sum(jnp.where(klane == h, ct, 0.0), axis=1, keepdims=True), (klen, WIDE * UNIT))
                  for _, _, h in heads]
            k_t = [jnp.concatenate([ktt[pp, kblk + b, e * HEAD_LANES:(e + 1) * HEAD_LANES, :] for b in range(nk)], axis=1)
                   for pp, e, _ in heads]
            for pp in range(FOX_PAIRS):
                dk_acc[pp, 0:klen] = jnp.zeros((klen, BLK), F32)
                dv_acc[pp, 0:klen] = jnp.zeros((klen, BLK), F32)
            for hh in range(len(heads)):
                dcs_acc[hh, 0:klen] = jnp.zeros((klen, BLK), F32)

            def tile(qblk, nq, mask):
                qlen = nq * BLK
                if mask == "valid":
                    mask = _iota((klen, qlen), 0) >= N_PAD
                qoff = pl.multiple_of(qblk * BLK, BLK)
                qlane = _iota((qlen, BLK), 1)
                qs = [qkv_ref[pl.ds(qoff, qlen), pp * 384:pp * 384 + BLK].astype(F32) * (scale * LOG2E)
                      for pp in range(FOX_PAIRS)]
                dot_ = [do_ref[pl.ds(qoff, qlen), pp * BLK:(pp + 1) * BLK] for pp in range(FOX_PAIRS)]
                stats = [[ref[qblk + a] for a in range(nq)] for ref in (ct_ref, lse_ref, dl_ref)]
                dcq = [jnp.zeros((8, BLK), F32) for _ in range(nq)]
                for hh, (pp, e, h) in enumerate(heads):
                    head = (qlane >> 6) == e
                    ct_row, lse_row, dl_row = [jnp.concatenate([_pick_row(t, h) for t in ts], axis=1) for ts in stats]
                    qm = jnp.where(head, qs[pp], 0.0).astype(BF16)
                    dom = jnp.where(head, dot_[pp], jnp.zeros_like(dot_[pp]))
                    t = _dot_nt(kt[pp], qm) - cs[hh][:, 0:qlen]
                    if mask is not None:
                        t = jnp.where(mask, t, NEG)
                    pr = jnp.exp2(t + (ct_row - lse_row))
                    dv_acc[pp, 0:klen] = dv_acc[pp, 0:klen] + _dot(pr.astype(BF16), dom)
                    dsv = pr * (_dot_nt(vtile[pp], dom) - dl_row)
                    ds_b = dsv.astype(BF16)
                    dk_acc[pp, 0:klen] = dk_acc[pp, 0:klen] + _dot(ds_b, qm)
                    rows = slice(e * HEAD_LANES, (e + 1) * HEAD_LANES)
                    dq_t = _dot(k_t[hh], ds_b)
                    key_side = dsv[:, 0:BLK]
                    for a in range(1, nq):
                        key_side = key_side + dsv[:, a * BLK:(a + 1) * BLK]
                    dcs_acc[hh, 0:klen] = dcs_acc[hh, 0:klen] + key_side
                    query_side = jnp.sum(dsv, axis=0, keepdims=True)
                    for a in range(nq):
                        cols = slice(a * BLK, (a + 1) * BLK)
                        dqt[pp, qblk + a, rows, :] = dqt[pp, qblk + a, rows, :] + dq_t[:, cols]
                        dcq[a] = dcq[a] + jnp.where(sub8 == h, query_side[:, cols], 0.0)
                for a in range(nq):
                    dcq_ref[qblk + a] = dcq_ref[qblk + a] + dcq[a]

            later_mask = "valid" if nk == 1 else None
            n_later = jnp.asarray(n_later, jnp.int32)
            n_wide = n_later // WIDE

            def later_wide(i, carry):
                tile(kblk + nk + 2 * WIDE * i, 2 * WIDE, later_mask)
                return carry

            tile(kblk, nk, masks["first"] if nk == 1 else masks["diag"])
            lax.fori_loop(0, n_wide, later_wide, 0)
            rest_blk = kblk + nk + 2 * WIDE * n_wide

            @pl.when((n_later & 2) != 0)
            def _():
                tile(rest_blk, 4, later_mask)

            @pl.when((n_later & 1) != 0)
            def _():
                tile(rest_blk + 2 * (n_later & 2), 2, later_mask)

            upd = jnp.zeros((klen, BLK), F32)
            for hh, (_, _, h) in enumerate(heads):
                upd = upd + jnp.where(klane == h, -jnp.sum(dcs_acc[hh, 0:klen], axis=1, keepdims=True), 0.0)
            dc_ref[pl.ds(koff, klen), :] = dc_ref[pl.ds(koff, klen), :] + upd
            for pp in range(FOX_PAIRS):
                dp_ref[pl.ds(koff, klen), pp * 384 + BLK:pp * 384 + 2 * BLK] = (
                    dk_acc[pp, 0:klen] * (1.0 / LOG2E)).astype(BF16)
                dp_ref[pl.ds(koff, klen), pp * 384 + 2 * BLK:pp * 384 + 3 * BLK] = dv_acc[pp, 0:klen].astype(BF16)

        kv_pass(0, 1, nu)

        def k_loop(u, carry):
            kv_pass(1 + 2 * u, 2, nu - 1 - u)
            return carry

        lax.fori_loop(0, nu, k_loop, 0)

        def flush(j, carry):
            off = pl.multiple_of(j * BLK, BLK)
            for pp in range(FOX_PAIRS):
                dp_ref[pl.ds(off, BLK), pp * 384:pp * 384 + BLK] = (dqt[pp, j].T * scale).astype(BF16)
            return carry

        lax.fori_loop(0, nblk, flush, 0)

        if ns:
            @pl.when(p == steps - 1)
            def _():
                copies = _scatter_copies(s_in, s_out, *rest[2 * ns + 8:])
                for cp in copies:
                    cp.wait_recv()
                for cp in copies:
                    cp.wait_send()

    width = 384 * FOX_PAIRS
    once = lambda shape, index: pl.BlockSpec(shape, index, pipeline_mode=pl.Buffered(1))
    stat = once((nblk, 8, BLK), lambda p: (0, 0, 0))
    return pl.pallas_call(
        body, name="b_fox", grid=(steps,),
        in_specs=[once((L, width), lambda p: (0, RET_W // width + p)),
                  once((L, FOX_PAIRS * BLK), lambda p: (0, 4 // FOX_PAIRS + p)),
                  once((L, BLK), lambda p: (0, 0)), stat, stat, stat] + [_ANY] * ns,
        out_specs=[pl.BlockSpec((L, width), lambda p: (0, p)), _full((L, BLK)), _full((nblk, 8, BLK))] + [_ANY] * ns,
        out_shape=[jax.ShapeDtypeStruct((L, FOX_W), BF16), jax.ShapeDtypeStruct((L, BLK), F32),
                   jax.ShapeDtypeStruct((nblk, 8, BLK), F32)] + _scatter_shapes(scatter),
        scratch_shapes=[pltpu.VMEM((FOX_PAIRS, nblk, BLK, BLK), BF16), pltpu.VMEM((FOX_PAIRS, nblk, BLK, BLK), F32),
                        pltpu.VMEM((FOX_PAIRS, UNIT, BLK), F32), pltpu.VMEM((FOX_PAIRS, UNIT, BLK), F32),
                        pltpu.VMEM((2 * FOX_PAIRS, UNIT, BLK), F32)]
        + _scatter_semaphores(ns),
        compiler_params=_params(("arbitrary",)),
    )(proj, dmix, c, ctb, lse, delta, *scatter)


def _fox_post(dc, dcq, ff, fb):
    L = dc.shape[0]
    nblk = L // BLK
    G = _block_group(nblk, SMALL_GROUP)
    steps = nblk // G

    def body(dc_ref, dcq_ref, ff_ref, b_ref, dff_ref, dffb_ref, dfb_ref, carry):
        @pl.when(pl.program_id(0) == 0)
        def _():
            carry[...] = jnp.zeros_like(carry)
            dfb_ref[...] = jnp.zeros_like(dfb_ref)

        tri = (_iota((BLK, BLK), 0) <= _iota((BLK, BLK), 1)).astype(BF16)
        live = _iota((BLK, BLK), 1) < FOX_HEADS
        run, dfb = carry[...], dfb_ref[...]
        for b in reversed(range(G)):
            rows = slice(b * BLK, (b + 1) * BLK)
            d = dc_ref[rows, :] + jnp.concatenate([dcq_ref[b], jnp.zeros((BLK - 8, BLK), F32)], axis=0).T
            hi, mid, lo = _split3(d)
            dlf = _dot(tri, hi) + _dot(tri, mid) + _dot(tri, lo) + run
            run = run + jnp.sum(d, axis=0, keepdims=True)
            z = ff_ref[rows, :] + b_ref[...]
            dff = jnp.where(live, dlf * jax.nn.sigmoid(-z), 0.0)
            dff_ref[rows, :] = dff
            dffb_ref[rows, :] = dff.astype(BF16)
            dfb = dfb + jnp.sum(dff, axis=0, keepdims=True)
        carry[...] = run
        dfb_ref[...] = dfb

    rev = lambda i: (steps - 1 - i, 0)
    return pl.pallas_call(
        body, name="b_foxpost", grid=(steps,),
        in_specs=[pl.BlockSpec((G * BLK, BLK), rev), pl.BlockSpec((G, 8, BLK), lambda i: (steps - 1 - i, 0, 0)),
                  pl.BlockSpec((G * BLK, BLK), rev), _full((1, BLK))],
        out_specs=[pl.BlockSpec((G * BLK, BLK), rev), pl.BlockSpec((G * BLK, BLK), rev), _full((1, BLK))],
        out_shape=[jax.ShapeDtypeStruct((L, BLK), F32), jax.ShapeDtypeStruct((L, BLK), BF16),
                   jax.ShapeDtypeStruct((1, BLK), F32)],
        scratch_shapes=[pltpu.VMEM((1, BLK), F32)],
        compiler_params=_params(("arbitrary",)),
    )(dc, dcq, ff, fb)


def _inproj_bwd(dpr, dpf, dffb, w_main, w_ff, h0, g, dh1, scatter=()):
    L = h0.shape[0]
    S = L - BLK
    tm = _row_tile(S, (512, 256, 128))
    nt = S // tm
    ns = len(scatter)
    operands = (dpr, dpf, dffb, h0, dh1)

    def body(*refs):
        lead, tile = refs[0:5], refs[5:10]
        wm_ref, wf_ref, g_ref = refs[10:13]
        rest = refs[13:]
        s_in, (dlead_ref, dx_ref, dg_ref), s_out = rest[:ns], rest[ns:ns + 3], rest[ns + 3:2 * ns + 3]
        i = pl.program_id(0)

        def rows_bwd(dpr_ref, dpf_ref, dff_ref, h_ref, dh1_ref):
            dn = (_dot_nt(dpr_ref[...], wm_ref[:, 0:RET_W]) + _dot_nt(dpf_ref[...], wm_ref[:, RET_W:MAIN_W])
                  + _dot_nt(dff_ref[...], wf_ref[...]))
            h = h_ref[...]
            r = lax.rsqrt(jnp.mean(h * h, axis=-1, keepdims=True) + EPS)
            yn = h * r
            dyn = dn * g_ref[...]
            dh0 = dh1_ref[...] + r * (dyn - yn * jnp.mean(dyn * yn, axis=-1, keepdims=True))
            return dh0, jnp.sum(dn * yn, axis=0, keepdims=True)

        @pl.when(i == 0)
        def _():
            if ns:
                for cp in _scatter_copies(s_in, s_out, *rest[2 * ns + 3:]):
                    cp.start()
            dlead_ref[...], dg_ref[...] = rows_bwd(*lead)

        dx_ref[...], dg_tile = rows_bwd(*tile)
        dg_ref[...] = dg_ref[...] + dg_tile

        if ns:
            @pl.when(i == nt - 1)
            def _():
                copies = _scatter_copies(s_in, s_out, *rest[2 * ns + 3:])
                for cp in copies:
                    cp.wait_recv()
                for cp in copies:
                    cp.wait_send()

    lead_spec = lambda a: pl.BlockSpec((BLK, a.shape[1]), lambda i: (0, 0))
    tile_spec = lambda a: pl.BlockSpec((pl.Element(tm), pl.Element(a.shape[1])),
                                       lambda i: (pl.multiple_of(BLK + i * tm, BLK), 0))
    return pl.pallas_call(
        body, name="b_inproj", grid=(nt,),
        in_specs=[lead_spec(a) for a in operands] + [tile_spec(a) for a in operands]
        + [_full((D_MODEL, MAIN_W)), _full((D_MODEL, BLK)), _full((1, D_MODEL))] + [_ANY] * ns,
        out_specs=[_full((BLK, D_MODEL)), pl.BlockSpec((tm, D_MODEL), lambda i: (i, 0)), _full((1, D_MODEL))]
        + [_ANY] * ns,
        out_shape=[jax.ShapeDtypeStruct((BLK, D_MODEL), F32), jax.ShapeDtypeStruct((S, D_MODEL), F32),
                   jax.ShapeDtypeStruct((1, D_MODEL), F32)] + _scatter_shapes(scatter),
        scratch_shapes=_scatter_semaphores(ns),
        compiler_params=_params(("arbitrary",)),
    )(*operands, *operands, w_main, w_ff, g, *scatter)


def _local_step(x, target, meta, attn_g, w_main, w_ff, fox_b, ret_g, w_out, ffn_g, w_up, conv_w, conv_b, w_down, final_g,
                late=None, mid=None, last=None, wire=F32):
    S = x.shape[0]
    L = S + PREFIX
    head = jnp.concatenate([jnp.zeros((N_PAD, D_MODEL), F32), meta], axis=0)
    fb = jnp.pad(fox_b, ((0, 0), (0, BLK - FOX_HEADS)))
    cos_t, sin_t = _rotary_tables(L)

    h0, n1, proj, ff = _rms_inproj(head, x, attn_g, w_main, w_ff)
    c, ctb = _fox_prep(ff, fb)
    mix_r, o_ret, states = _retention_fwd(proj, cos_t, sin_t, ret_g)
    if late is None:
        o_f, lse = _fox_fwd(proj, c, ctb)
    else:
        o_f, lse, *gathered = _fox_fwd(proj, c, ctb, gather=late[0])
        w_out, w_up, w_down = late[1](gathered)
    h1, n2, up, g_act, acc_saved = _outproj_up(mix_r, o_f, h0, w_out, ffn_g, w_up, conv_w, conv_b)
    dh2, dh2b, d_final_g, loss, dacc, db = _ffn_down_loss(g_act, w_down, h1, final_g, target, acc_saved, up)

    dup, dh1, dh1b, dmix, d_ffn_g, dconv = _ffn_bwd_up(dacc, db, up, conv_w, w_up, h1, ffn_g, dh2, w_out)
    d_w_down = _wgrad(g_act, dh2b, "wgrad_down", tk=D_FF // 2, out_dtype=wire)[0]
    d_w_up = _wgrad(n2, dup, "wgrad_up", tn=w_up.shape[2], out_dtype=wire)
    d_w_out = jnp.concatenate([_wgrad(mix_r, dh1b, "wgrad_out_r", out_dtype=wire)[0],
                               _wgrad(o_f, dh1b, "wgrad_out_f", out_dtype=wire)[0]], axis=0)

    early = () if mid is None else mid[0](d_w_out, d_w_up, d_w_down)
    dpr, d_ret_g, *from_sibling = _retention_bwd(dmix, o_ret, proj, cos_t, sin_t, ret_g, states, exchange=early)
    delta = _fox_delta(dmix, o_f)
    scatter = () if mid is None else mid[1](early, from_sibling)
    dpf, dc, dcq, *received = _fox_bwd(proj, dmix, c, ctb, lse, delta, scatter=scatter)
    dff, dffb, d_fox_b = _fox_post(dc, dcq, ff, fb)
    d_w_ret, d_w_fox = _wgrad(n1, dpr, "wgrad_in_r")[0], _wgrad(n1, dpf, "wgrad_in_f")[0]
    d_w_ff = _wgrad(n1, dffb, "wgrad_in_ff")[0][:, :FOX_HEADS]
    scatter_in = () if last is None else last(d_w_ret, d_w_fox, d_w_ff)
    dlead, dx, d_attn_g, *received_in = _inproj_bwd(dpr, dpf, dffb, w_main, w_ff, h0, attn_g, dh1, scatter=scatter_in)

    return dict(
        loss=loss[0, 0], dx=dx, dmeta=dlead[N_PAD:], attn_g=d_attn_g, w_main=jnp.concatenate([d_w_ret, d_w_fox], axis=1),
        w_ff=d_w_ff, fox_b=d_fox_b[:, :FOX_HEADS], ret_g=d_ret_g, w_out=d_w_out, ffn_g=d_ffn_g,
        w_up=d_w_up, conv_w=dconv[0:3], conv_b=dconv[3:4], w_down=d_w_down, final_g=d_final_g,
        scatter=list(scatter_in) + list(scatter), received=list(received_in) + list(received))


_ANY = pl.BlockSpec(memory_space=pl.ANY)


def _place():
    return lax.axis_index("x"), lax.axis_index("y"), lax.axis_index("c")


def _other_chips(x, y):
    return [(1 - x, y), (x, 1 - y), (1 - x, 1 - y)]


def _allgather_semaphores(n):
    if n == 0:
        return []
    return [pltpu.SemaphoreType.DMA((3 * n,)), pltpu.SemaphoreType.DMA((3 * n,)), pltpu.SemaphoreType.DMA((n,))]


def _allgather_copies(ins, outs, send, recv, loc):
    n = len(ins)
    x, y, c = _place()
    mine = 2 * x + y
    peers = _other_chips(x, y)

    def remote(a, k, slot):
        return pltpu.make_async_remote_copy(
            src_ref=ins[a], dst_ref=outs[a].at[slot], send_sem=send.at[3 * a + k], recv_sem=recv.at[3 * a + k],
            device_id=(peers[k][0], peers[k][1], c), device_id_type=MESH)

    local = [pltpu.make_async_copy(ins[a], outs[a].at[mine], loc.at[a]) for a in range(n)]
    sends = [remote(a, k, mine) for a in range(n) for k in range(3)]
    recvs = [remote(a, k, 2 * peers[k][0] + peers[k][1]) for a in range(n) for k in range(3)]
    return local, sends, recvs


def _chip_allgather_halves(w, small):
    half = w.shape[0] // 2

    def body(w_ref, s_ref, wo_ref, so_ref, send, recv, fsend, frecv, ssend, srecv, loc):
        x, y, c = _place()
        mine = 2 * x + y
        peers = _other_chips(x, y)

        def fetch(k, slot):
            return pltpu.make_async_remote_copy(
                src_ref=w_ref.at[pl.ds(c * half, half)], dst_ref=wo_ref.at[slot, c], send_sem=send.at[k],
                recv_sem=recv.at[k], device_id=(peers[k][0], peers[k][1], c), device_id_type=MESH)

        def forward(k, which):
            slot = 2 * peers[k][0] + peers[k][1]
            return pltpu.make_async_remote_copy(
                src_ref=wo_ref.at[slot, which], dst_ref=wo_ref.at[slot, which], send_sem=fsend.at[k],
                recv_sem=frecv.at[k], device_id=(x, y, 1 - c), device_id_type=MESH)

        def small_copy(k, slot):
            return pltpu.make_async_remote_copy(
                src_ref=s_ref, dst_ref=so_ref.at[slot], send_sem=ssend.at[k], recv_sem=srecv.at[k],
                device_id=(peers[k][0], peers[k][1], c), device_id_type=MESH)

        local = pltpu.make_async_copy(s_ref, so_ref.at[mine], loc.at[0])
        sends = [fetch(k, mine) for k in range(3)] + [small_copy(k, mine) for k in range(3)]
        local.start()
        for cp in sends:
            cp.start()
        forwards = []
        for k in range(3):
            fetch(k, 2 * peers[k][0] + peers[k][1]).wait_recv()
            forwards.append(forward(k, c))
            forwards[-1].start()
        for k in range(3):
            forward(k, 1 - c).wait_recv()
            small_copy(k, 2 * peers[k][0] + peers[k][1]).wait_recv()
        for cp in sends + forwards:
            cp.wait_send()
        local.wait()

    three = pltpu.SemaphoreType.DMA((3,))
    return pl.pallas_call(
        body, name="ag_weights", in_specs=[_ANY] * 2, out_specs=[_ANY] * 2,
        out_shape=[jax.ShapeDtypeStruct((N_CHIPS, 2, half, w.shape[1]), w.dtype),
                   jax.ShapeDtypeStruct((N_CHIPS,) + small.shape, small.dtype)],
        scratch_shapes=[three, three, three, three, three, three, pltpu.SemaphoreType.DMA((1,))],
    )(w, small)


def _chip_allgather(arrays):
    n = len(arrays)

    def body(*refs):
        local, sends, recvs = _allgather_copies(refs[:n], refs[n:2 * n], *refs[2 * n:])
        for cp in local + sends:
            cp.start()
        for cp in recvs:
            cp.wait_recv()
        for cp in sends:
            cp.wait_send()
        for cp in local:
            cp.wait()

    return pl.pallas_call(
        body, name="ag_weights", in_specs=[_ANY] * n, out_specs=[_ANY] * n,
        out_shape=[jax.ShapeDtypeStruct((N_CHIPS,) + a.shape, a.dtype) for a in arrays],
        scratch_shapes=_allgather_semaphores(n),
    )(*arrays)


def _sibling_halves(grads):
    n = len(grads)

    def body(*refs):
        sends, recvs = _sibling_half_copies(refs[:n], refs[n:2 * n], *refs[2 * n:])
        for cp in sends:
            cp.start()
        for cp in recvs:
            cp.wait_recv()
        for cp in sends:
            cp.wait_send()

    return pl.pallas_call(
        body, name="rs_sibling", in_specs=[_ANY] * n, out_specs=[_ANY] * n,
        out_shape=_sibling_half_shapes(grads), scratch_shapes=_sibling_half_semaphores(n),
    )(*grads)


def _sibling_half_shapes(grads):
    return [jax.ShapeDtypeStruct((N_CHIPS, g.shape[1] // 2, g.shape[2]), g.dtype) for g in grads]


def _sibling_half_semaphores(n):
    return [pltpu.SemaphoreType.DMA((n,)), pltpu.SemaphoreType.DMA((n,))] if n else []


def _sibling_half_copies(ins, outs, send, recv):
    x, y, c = _place()

    def half_copy(a, which):
        half = ins[a].shape[1] // 2
        return pltpu.make_async_remote_copy(
            src_ref=ins[a].at[pl.ds(0, N_CHIPS), pl.ds(which * half, half)], dst_ref=outs[a],
            send_sem=send.at[a], recv_sem=recv.at[a], device_id=(x, y, 1 - c), device_id_type=MESH)

    return [half_copy(a, 1 - c) for a in range(len(ins))], [half_copy(a, c) for a in range(len(ins))]


def _scatter_shapes(parts):
    return [jax.ShapeDtypeStruct((3,) + p.shape[1:], p.dtype) for p in parts]


def _scatter_semaphores(n):
    return [pltpu.SemaphoreType.DMA((3 * n,)), pltpu.SemaphoreType.DMA((3 * n,))] if n else []


def _scatter_copies(ins, outs, send, recv):
    x, y, c = _place()
    peers = _other_chips(x, y)
    return [pltpu.make_async_remote_copy(
        src_ref=ins[a].at[2 * peers[k][0] + peers[k][1]], dst_ref=outs[a].at[k], send_sem=send.at[3 * a + k],
        recv_sem=recv.at[3 * a + k], device_id=(peers[k][0], peers[k][1], c), device_id_type=MESH)
        for a in range(len(ins)) for k in range(3)]


def _sibling_allgather(bufs, small):
    n = len(bufs)

    def body(*refs):
        small_in, outs, small_out = refs[n], refs[n + 1:2 * n + 1], refs[2 * n + 1]
        send, recv, s_send, s_recv, loc = refs[2 * n + 2:]
        x, y, c = _place()
        me = 4 * x + 2 * y + c

        def remote(a, which):
            return pltpu.make_async_remote_copy(
                src_ref=outs[a].at[which], dst_ref=outs[a].at[which], send_sem=send.at[a], recv_sem=recv.at[a],
                device_id=(x, y, 1 - c), device_id_type=MESH)

        def peer_of(r):
            return tuple(1 - v if (r >> b) & 1 else v for v, b in ((x, 2), (y, 1), (c, 0)))

        def small_copy(r, slot):
            return pltpu.make_async_remote_copy(
                src_ref=small_in, dst_ref=small_out.at[slot], send_sem=s_send.at[r - 1], recv_sem=s_recv.at[r - 1],
                device_id=peer_of(r), device_id_type=MESH)

        local = pltpu.make_async_copy(small_in, small_out.at[me], loc.at[0])
        sends = [remote(a, c) for a in range(n)] + [small_copy(r, me) for r in range(1, N_DEV)]
        local.start()
        for cp in sends:
            cp.start()
        for r in range(1, N_DEV):
            px, py, pc = peer_of(r)
            small_copy(r, 4 * px + 2 * py + pc).wait_recv()
        for a in range(n):
            remote(a, 1 - c).wait_recv()
        for cp in sends:
            cp.wait_send()
        local.wait()

    outs = pl.pallas_call(
        body, name="ag_sibling", in_specs=[_ANY] * (n + 1), out_specs=[_ANY] * (n + 1),
        out_shape=[jax.ShapeDtypeStruct(b.shape, b.dtype) for b in bufs]
        + [jax.ShapeDtypeStruct((N_DEV,) + small.shape, small.dtype)],
        input_output_aliases={a: a for a in range(n)},
        scratch_shapes=[pltpu.SemaphoreType.DMA((n,)), pltpu.SemaphoreType.DMA((n,)),
                        pltpu.SemaphoreType.DMA((N_DEV - 1,)), pltpu.SemaphoreType.DMA((N_DEV - 1,)),
                        pltpu.SemaphoreType.DMA((1,))],
    )(*bufs, small)
    return [o.reshape(2 * o.shape[1], o.shape[2]) for o in outs[:n]], outs[n]


def _pair_add(full, recv, core, name):
    _, R, C = full.shape
    half = R // 2

    def body(core_ref, a_ref, b_ref, o_ref):
        o_ref[...] = (a_ref[...].astype(F32) + b_ref[...].astype(F32)).astype(BF16)

    return pl.pallas_call(
        body, name=name,
        grid_spec=pltpu.PrefetchScalarGridSpec(
            num_scalar_prefetch=1, grid=(N_CHIPS,),
            in_specs=[pl.BlockSpec((1, half, C), lambda j, core_ref: (j, core_ref[0], 0)),
                      pl.BlockSpec((1, half, C), lambda j, core_ref: (j, 0, 0))],
            out_specs=pl.BlockSpec((1, half, C), lambda j, core_ref: (j, 0, 0))),
        out_shape=jax.ShapeDtypeStruct((N_CHIPS, half, C), BF16),
        compiler_params=_params(("parallel",)),
    )(core, full, recv)


def _sum_partials(own_all, recv, place, name, tiles=2):
    _, R, C = own_all.shape
    tr = R // tiles

    def body(place_ref, own_ref, r_ref, o_ref):
        acc = own_ref[0].astype(F32)
        for k in range(3):
            acc = acc + r_ref[k].astype(F32)
        o_ref[0] = acc

    return pl.pallas_call(
        body, name=name,
        grid_spec=pltpu.PrefetchScalarGridSpec(
            num_scalar_prefetch=1, grid=(tiles,),
            in_specs=[pl.BlockSpec((1, tr, C), lambda i, place_ref: (place_ref[0], i, 0)),
                      pl.BlockSpec((3, tr, C), lambda i, place_ref: (0, i, 0))],
            out_specs=pl.BlockSpec((1, tr, C), lambda i, place_ref: (place_ref[1], i, 0))),
        out_shape=jax.ShapeDtypeStruct((2, R, C), F32),
        compiler_params=_params(("parallel",)),
    )(place, own_all, recv)


def _adamw_math(w, g, m, v):
    m2 = ADAM_B1 * m + (1.0 - ADAM_B1) * g
    v2 = ADAM_B2 * v + (1.0 - ADAM_B2) * (g * g)
    m_hat = m2 / (1.0 - ADAM_B1 ** ADAM_STEP)
    v_hat = v2 / (1.0 - ADAM_B2 ** ADAM_STEP)
    return -ADAM_LR * (m_hat / (jnp.sqrt(v_hat) + ADAM_EPS) + ADAM_WD * w), m2, v2


ROW_ATTN_G, ROW_FFN_G, ROW_FINAL_G, ROW_MISC, ROW_CONV_B, ROW_CONV_W, ROW_META, SMALL_ROWS = 0, 1, 2, 3, 4, 8, 24, 40
MISC_FOX_B, MISC_LOSS = 512, 640


def _small_pack(out):
    def rows(a, n):
        a = a.astype(F32)
        return jnp.pad(a, ((0, n - a.shape[0]), (0, D_MODEL - a.shape[1])))

    misc = jnp.concatenate([out["ret_g"], out["fox_b"], jnp.zeros((1, MISC_LOSS - MISC_FOX_B - FOX_HEADS), F32),
                            out["loss"].reshape(1, 1)], axis=1)
    conv_b = jnp.pad(out["conv_b"], ((0, 0), (0, (-D_FF) % D_MODEL))).reshape(-1, D_MODEL)
    conv_w = out["conv_w"].reshape(3, N_CHIPS, -1).transpose(1, 0, 2).reshape(3 * N_CHIPS, -1)
    return jnp.concatenate([
        rows(out["attn_g"], 1), rows(out["ffn_g"], 1), rows(out["final_g"], 1), rows(misc, 1),
        rows(conv_b, ROW_CONV_W - ROW_CONV_B), rows(conv_w, ROW_META - ROW_CONV_W), rows(out["dmeta"], N_META)], axis=0)


def _small_update(packs, chip, ws, ms, vs):
    n = len(ws)
    meta_w, conv_sw = ws[0].shape[1], ws[5].shape[2]
    assert packs.shape == (N_DEV, SMALL_ROWS, D_MODEL) and ws[0].shape[0] == N_META and ws[5].shape[:2] == (3, 1)

    def body(chip_ref, p_ref, *refs):
        w_refs, m_refs, v_refs = refs[:n], refs[n:2 * n], refs[2 * n:3 * n]
        loss_ref, out_refs, tot = refs[3 * n], refs[3 * n + 1:7 * n + 1], refs[7 * n + 1]
        acc = p_ref[0]
        for d in range(1, N_DEV):
            acc = acc + p_ref[d]
        tot[...] = acc

        def of_chip(pieces):
            val = pieces[-1]
            for j in range(N_CHIPS - 2, -1, -1):
                val = jnp.where(chip_ref[0] == j, pieces[j], val)
            return val

        row = lambda r, lo=0, hi=D_MODEL: tot[r:r + 1, lo:hi]
        grads = [
            of_chip([tot[ROW_META:ROW_META + N_META, j * meta_w:(j + 1) * meta_w] for j in range(N_CHIPS)]),
            row(ROW_ATTN_G), row(ROW_MISC, MISC_FOX_B, MISC_FOX_B + FOX_HEADS), row(ROW_MISC, 0, MISC_FOX_B),
            row(ROW_FFN_G),
            of_chip([tot[ROW_CONV_W + 3 * j:ROW_CONV_W + 3 * j + 3, 0:conv_sw] for j in range(N_CHIPS)]),
            jnp.concatenate([row(ROW_CONV_B), row(ROW_CONV_B + 1), row(ROW_CONV_B + 2, 0, D_FF - 2 * D_MODEL)], axis=1),
            row(ROW_FINAL_G)]
        loss_ref[...] = row(ROW_MISC, MISC_LOSS, MISC_LOSS + BLK)
        for k in range(n):
            parts = [((Ellipsis,), grads[k])]
            if len(ws[k].shape) == 3:
                parts = [((t,), grads[k][t:t + 1]) for t in range(ws[k].shape[0])]
            for at, g in parts:
                res = (g,) + _adamw_math(w_refs[k][at], g, m_refs[k][at], v_refs[k][at])
                for kind in range(4):
                    out_refs[kind * n + k][at] = res[kind]

    res = pl.pallas_call(
        body, name="small_update",
        grid_spec=pltpu.PrefetchScalarGridSpec(
            num_scalar_prefetch=1, grid=(1,),
            in_specs=[_full(packs.shape)] + [_full(a.shape) for a in list(ws) * 3],
            out_specs=[_full((1, BLK))] + [_full(a.shape) for a in list(ws) * 4],
            scratch_shapes=[pltpu.VMEM((SMALL_ROWS, D_MODEL), F32)]),
        out_shape=[jax.ShapeDtypeStruct((1, BLK), F32)] + [jax.ShapeDtypeStruct(a.shape, F32) for a in list(ws) * 4],
        compiler_params=_params(("arbitrary",)),
    )(chip, packs, *ws, *ms, *vs)
    return res[0], res[1:n + 1], res[n + 1:2 * n + 1], res[2 * n + 1:3 * n + 1], res[3 * n + 1:]


def _adamw(w, g, m, v, name, tiles=8):
    R, tail = w.shape[0], w.shape[1:]
    assert R % tiles == 0
    tr = R // tiles

    def body(w_ref, g_ref, m_ref, v_ref, go_ref, d_ref, m2_ref, v2_ref):
        g_ = g_ref[...]
        go_ref[...] = g_
        d_ref[...], m2_ref[...], v2_ref[...] = _adamw_math(w_ref[...], g_, m_ref[...], v_ref[...])

    spec = pl.BlockSpec((tr,) + tail, lambda i: (i,) + (0,) * len(tail))
    return pl.pallas_call(
        body, name=name, grid=(tiles,), in_specs=[spec] * 4, out_specs=[spec] * 4,
        out_shape=[jax.ShapeDtypeStruct(w.shape, F32)] * 4,
        compiler_params=_params(("parallel",)),
    )(w, g, m, v)


def _row_vector_tiles(n, most=80):
    return next(t for t in range(1, n + 1) if n % t == 0 and n // t <= most)


def _pack_rows(pieces, rows):
    flat = jnp.concatenate([jnp.pad(p.reshape(-1).astype(F32), (0, (-p.size) % D_MODEL)) for p in pieces])
    return jnp.pad(flat, (0, rows * D_MODEL - flat.size)).reshape(rows, D_MODEL)


def _unpack_rows(pack, shapes):
    flat = pack.reshape(-1)
    out, off = [], 0
    for shp in shapes:
        size = int(np.prod(shp))
        out.append(flat[off:off + size].reshape(shp))
        off += size + (-size) % D_MODEL
    return out


IN_PADDED = IN_WIDTH + (-IN_WIDTH) % BLK


def _fox_column_blocks():
    return [(RET_W + part * 512 + p * BLK, RET_W + 384 * p + part * BLK)
            for part in range(3) for p in range(FOX_HEADS // 2)]


def _w_in_kernel_order(gathered, own, chip):
    n, R, C = gathered.shape
    tr = R // 4

    def body(chip_ref, g_ref, own_ref, wm_ref, wf_ref, full):
        for j in range(n):
            @pl.when(chip_ref[0] == j)
            def _(j=j):
                full[:, j * C:(j + 1) * C] = own_ref[...]

            @pl.when(chip_ref[0] != j)
            def _(j=j):
                full[:, j * C:(j + 1) * C] = g_ref[j]

        full[:, n * C:] = jnp.zeros((tr, IN_PADDED - n * C), BF16)
        wm_ref[:, 0:RET_W] = full[:, 0:RET_W]
        for src, dst in _fox_column_blocks():
            wm_ref[:, dst:dst + BLK] = full[:, src:src + BLK]
        wf_ref[...] = full[:, MAIN_W:MAIN_W + BLK]

    return pl.pallas_call(
        body, name="w_in_kernel_order",
        grid_spec=pltpu.PrefetchScalarGridSpec(
            num_scalar_prefetch=1, grid=(R // tr,),
            in_specs=[pl.BlockSpec((n, tr, C), lambda i, c: (0, i, 0)), pl.BlockSpec((tr, C), lambda i, c: (i, 0))],
            out_specs=[pl.BlockSpec((tr, MAIN_W), lambda i, c: (i, 0)), pl.BlockSpec((tr, BLK), lambda i, c: (i, 0))],
            scratch_shapes=[pltpu.VMEM((tr, IN_PADDED), BF16)]),
        out_shape=[jax.ShapeDtypeStruct((R, MAIN_W), BF16), jax.ShapeDtypeStruct((R, BLK), BF16)],
        compiler_params=_params(("arbitrary",)),
    )(chip, gathered, own)


def _w_in_grad_shards(g_ret, g_fox, g_ff):
    R = g_ret.shape[0]
    C = IN_WIDTH // N_CHIPS
    tr = R // 4

    def body(gr_ref, gx_ref, gf_ref, o_ref, full):
        full[:, 0:RET_W] = gr_ref[...]
        for src, dst in _fox_column_blocks():
            full[:, src:src + BLK] = gx_ref[:, dst - RET_W:dst - RET_W + BLK]
        full[:, MAIN_W:MAIN_W + FOX_HEADS] = gf_ref[...]
        for j in range(N_CHIPS):
            o_ref[j] = full[:, j * C:(j + 1) * C].astype(BF16)

    rows = lambda w: pl.BlockSpec((tr, w), lambda i: (i, 0))
    return pl.pallas_call(
        body, name="w_in_grad_shards", grid=(R // tr,),
        in_specs=[rows(RET_W), rows(FOX_W), rows(FOX_HEADS)],
        out_specs=pl.BlockSpec((N_CHIPS, tr, C), lambda i: (0, i, 0)),
        out_shape=jax.ShapeDtypeStruct((N_CHIPS, R, C), BF16),
        scratch_shapes=[pltpu.VMEM((tr, IN_PADDED), F32)],
        compiler_params=_params(("parallel",)),
    )(g_ret, g_fox, g_ff)


def kernel(x, meta_tokens, attn_norm_g, w_in, fox_forget_b, ret_norm_g, w_out, ffn_norm_g, w_up, conv_w, conv_b, w_down, final_norm_g, loss_target, m_meta_tokens, m_attn_norm_g, m_w_in, m_fox_forget_b, m_ret_norm_g, m_w_out, m_ffn_norm_g, m_w_up, m_conv_w, m_conv_b, m_w_down, m_final_norm_g, v_meta_tokens, v_attn_norm_g, v_w_in, v_fox_forget_b, v_ret_norm_g, v_w_out, v_ffn_norm_g, v_w_up, v_conv_w, v_conv_b, v_w_down, v_final_norm_g):
    chip = 2 * lax.axis_index("x") + lax.axis_index("y")
    core = lax.axis_index("c")

    small_w = _pack_rows([meta_tokens, conv_w[0]], 8)
    w_in_b = w_in[0].astype(BF16)
    g_in, g_small = _chip_allgather_halves(w_in_b, small_w)
    chip_idx = chip.reshape(1).astype(jnp.int32)
    w_main, w_ff = _w_in_kernel_order(g_in.reshape((N_CHIPS,) + w_in_b.shape), w_in_b, chip_idx)
    small_parts = [_unpack_rows(g_small[j], [meta_tokens.shape, conv_w.shape[1:]]) for j in range(N_CHIPS)]
    meta_full = jnp.concatenate([sp[0] for sp in small_parts], axis=1)
    conv_w_full = jnp.concatenate([sp[1] for sp in small_parts], axis=1)

    core_idx = core.reshape(1).astype(jnp.int32)
    place = jnp.stack([chip, core]).astype(jnp.int32)

    def assemble(gathered):
        g_out, g_up, g_down = gathered
        return g_out.reshape(D_MODEL, D_MODEL), g_up, g_down.reshape(D_FF, D_MODEL)

    def early_arrays(d_w_out, d_w_up, d_w_down):
        return [d_w_out.reshape(N_CHIPS, -1, D_MODEL), d_w_up, d_w_down.reshape(N_CHIPS, -1, D_MODEL)]

    def in_sums(d_w_ret, d_w_fox, d_w_ff):
        g_in_full = _w_in_grad_shards(d_w_ret, d_w_fox, d_w_ff)
        (from_sib,) = _sibling_halves([g_in_full])
        return [_pair_add(g_in_full, from_sib, core_idx, "pair_add_in")]

    def early_sums(early, from_sib):
        return [_pair_add(g, r, core_idx, "pair_add_" + nm) for g, r, nm in zip(early, from_sib, ("out", "up", "down"))]

    out = _local_step(x[0], loss_target[0], meta_full, attn_norm_g, w_main, w_ff, fox_forget_b, ret_norm_g,
                      None, ffn_norm_g, None, conv_w_full, conv_b, None, final_norm_g[None],
                      late=([w_out[0].astype(BF16), w_up[0].astype(BF16), w_down[0].astype(BF16)], assemble),
                      mid=(early_arrays, early_sums), last=in_sums, wire=BF16)

    names = ("in", "out", "up", "down")
    totals = [_sum_partials(s, q, place, "sum_chips_" + nm) for s, q, nm in zip(out["scatter"], out["received"], names)]
    (grad_in, grad_out, grad_up, grad_down), small_all = _sibling_allgather(totals, _small_pack(out))

    big_w = [(w_out, m_w_out, v_w_out, grad_out, "adamw_out"), (w_up, m_w_up, v_w_up, grad_up, "adamw_up"),
             (w_down, m_w_down, v_w_down, grad_down, "adamw_down")]
    big_res = [[r[None] for r in _adamw(w[0], g, m[0], v[0], nm)] for w, m, v, g, nm in big_w]
    as_rows = lambda a: jnp.transpose(a, (2, 0, 1))
    in_rows = _adamw(as_rows(w_in), grad_in.T[:, None, :], as_rows(m_w_in), as_rows(v_w_in), "adamw_in",
                     tiles=_row_vector_tiles(w_in.shape[2]))
    big_res.insert(0, [jnp.transpose(r, (1, 2, 0)) for r in in_rows])
    tap_rows = lambda a: jnp.transpose(a, (1, 0, 2))
    small_p = [meta_tokens, attn_norm_g, fox_forget_b, ret_norm_g, ffn_norm_g, tap_rows(conv_w), conv_b, final_norm_g[None]]
    small_m = [m_meta_tokens, m_attn_norm_g, m_fox_forget_b, m_ret_norm_g, m_ffn_norm_g, tap_rows(m_conv_w), m_conv_b,
               m_final_norm_g[None]]
    small_v = [v_meta_tokens, v_attn_norm_g, v_fox_forget_b, v_ret_norm_g, v_ffn_norm_g, tap_rows(v_conv_w), v_conv_b,
               v_final_norm_g[None]]
    loss_row, *small_res = _small_update(small_all, chip_idx, small_p, small_m, small_v)
    loss = loss_row[0, 0]

    def ordered(kind):
        sm = list(small_res[kind][:-1]) + [small_res[kind][-1][0]]
        sm[5] = tap_rows(sm[5])
        bg = [r[kind] for r in big_res]
        return [sm[0], sm[1], bg[0], sm[2], sm[3], bg[1], sm[4], bg[2], sm[5], sm[6], bg[3], sm[7]]

    return (loss, out["dx"][None], *ordered(0), *ordered(1), *ordered(2), *ordered(3))
```

```python
import functools

import numpy as np
import jax
import jax.numpy as jnp
from jax import lax
from jax.experimental import pallas as pl
from jax.experimental.pallas import tpu as pltpu

F32 = jnp.float32
BF16 = jnp.bfloat16

D_MODEL = 1024
N_META = 16
BLK = 128
UNIT = 2 * BLK
FOX_PAIRS = 2
WIDE = 4
CHUNK = 64
N_PAD = BLK - N_META
PREFIX = BLK
RET_HEADS = 4
FOX_HEADS = 8
HEAD_LANES = 64
D_FF = 2816
ROPE_BASE = 10000.0
EPS = 1e-6
NEG = -1e30
LOG2E = 1.4426950408889634
RET_W = 1536
FOX_W = 1536
MAIN_W = RET_W + FOX_W
IN_WIDTH = MAIN_W + FOX_HEADS
N_CHIPS = 4
N_DEV = 8

ADAM_LR = 0.001
ADAM_B1 = 0.9
ADAM_B2 = 0.999
ADAM_EPS = 1e-08
ADAM_WD = 0.01
ADAM_STEP = 10

MESH = pl.DeviceIdType.MESH
VMEM_LIMIT_MB = 56

_NT = (((1,), (1,)), ((), ()))
_TN = (((0,), (0,)), ((), ()))


def _dot(a, b):
    return jnp.dot(a, b, preferred_element_type=F32)


def _dot_nt(a, b):
    return lax.dot_general(a, b, _NT, preferred_element_type=F32)


def _dot_tn(a, b):
    return lax.dot_general(a, b, _TN, preferred_element_type=F32)


def _params(dims=None, vmem_mb=VMEM_LIMIT_MB):
    kw = dict(vmem_limit_bytes=vmem_mb << 20)
    if dims is not None:
        kw["dimension_semantics"] = dims
    return pltpu.CompilerParams(**kw)


def _row_tile(n, prefs=(384, 256, 128)):
    for t in prefs:
        if n % t == 0:
            return t
    raise ValueError(f"no row tile for {n}")


def _iota(shape, dim):
    return lax.broadcasted_iota(jnp.int32, shape, dim)


def _pick_row(tile, row):
    sub = _iota(tile.shape, 0)
    return jnp.sum(jnp.where(sub == row, tile, 0.0), axis=0, keepdims=True)


def _split3(x):
    hi = x.astype(BF16)
    r1 = x - hi.astype(F32)
    mid = r1.astype(BF16)
    lo = (r1 - mid.astype(F32)).astype(BF16)
    return hi, mid, lo


def _full(shape):
    nd = len(shape)
    return pl.BlockSpec(shape, lambda *_: (0,) * nd)


def _in_perm():
    cols = list(range(RET_W))
    for p in range(FOX_HEADS // 2):
        for part in range(3):
            start = RET_W + part * 512 + p * BLK
            cols += list(range(start, start + BLK))
    return np.asarray(cols, np.int32)


def _rotary_tables(L):
    half = HEAD_LANES // 2
    inv = 1.0 / (ROPE_BASE ** (jnp.arange(half, dtype=F32) / half))
    ang = jnp.arange(L).astype(F32)[:, None] * inv[None, :]
    cos, sin = jnp.cos(ang), jnp.sin(ang)
    cos_t = jnp.tile(cos, (1, 4))
    sin_t = jnp.tile(jnp.concatenate([-sin, sin], axis=1), (1, 2))
    return cos_t, sin_t


def _decay_tables():
    gam = 1.0 - 2.0 ** (-5.0 - np.arange(RET_HEADS, dtype=np.float64))
    n = np.arange(BLK)
    same_or_past = (n[:, None] // CHUNK) >= (n[None, :] // CHUNK)
    dist = np.abs(n[:, None] - n[None, :])
    dmat = np.stack([np.where(same_or_past, g ** dist, 0.0) for g in gam]).astype(np.float32)
    lane_head = np.arange(BLK) // HEAD_LANES
    wq = np.stack([gam[2 * p + lane_head][None, :] ** (n[:, None] + 1.0) for p in range(2)]).astype(np.float32)
    wk = np.stack([gam[2 * p + lane_head][None, :] ** (BLK - 1.0 - n[:, None]) for p in range(2)]).astype(np.float32)
    g_blk = tuple(float(g ** BLK) for g in gam)
    return jnp.asarray(dmat), jnp.asarray(wq), jnp.asarray(wk), g_blk


def _shifted_blocks(tm):
    nb = tm // BLK
    return [pl.BlockSpec((BLK, D_MODEL), lambda i, j=j: (jnp.maximum(nb * i + j - 1, 0), 0)) for j in range(nb)]


def _rms_inproj(head, x, g, w_main, w_ff):
    L = x.shape[0] + BLK
    tm = _row_tile(L)
    nb = tm // BLK

    def body(head_ref, *refs):
        x_refs, (g_ref, wm_ref, wf_ref, h_ref, n_ref, p_ref, ff_ref) = refs[:nb], refs[nb:]
        parts = [r[...] for r in x_refs]
        parts[0] = jnp.where(pl.program_id(0) == 0, head_ref[...], parts[0])
        h = jnp.concatenate(parts, axis=0)
        h_ref[...] = h
        r = lax.rsqrt(jnp.mean(h * h, axis=-1, keepdims=True) + EPS)
        n = (h * r * g_ref[...]).astype(BF16)
        n_ref[...] = n
        p_ref[...] = _dot(n, wm_ref[...]).astype(BF16)
        ff_ref[...] = _dot(n, wf_ref[...])

    rows = lambda w: pl.BlockSpec((tm, w), lambda i: (i, 0))
    return pl.pallas_call(
        body, name="f_inproj", grid=(L // tm,),
        in_specs=[_full((BLK, D_MODEL))] + _shifted_blocks(tm)
        + [_full((1, D_MODEL)), _full((D_MODEL, MAIN_W)), _full((D_MODEL, BLK))],
        out_specs=[rows(D_MODEL), rows(D_MODEL), rows(MAIN_W), rows(BLK)],
        out_shape=[jax.ShapeDtypeStruct((L, D_MODEL), F32), jax.ShapeDtypeStruct((L, D_MODEL), BF16),
                   jax.ShapeDtypeStruct((L, MAIN_W), BF16), jax.ShapeDtypeStruct((L, BLK), F32)],
        compiler_params=_params(("parallel",)),
    )(head, *([x] * nb), g, w_main, w_ff)


SMALL_GROUP = 11


def _block_group(nblk, most=3):
    return next(g for g in range(most, 0, -1) if nblk % g == 0)


def _fox_prep(ff, fb):
    L = ff.shape[0]
    nblk = L // BLK
    G = _block_group(nblk, SMALL_GROUP)

    def body(ff_ref, b_ref, c_ref, ct_ref, carry):
        @pl.when(pl.program_id(0) == 0)
        def _():
            carry[...] = jnp.zeros_like(carry)

        tri = (_iota((BLK, BLK), 0) >= _iota((BLK, BLK), 1)).astype(BF16)
        live = _iota((BLK, BLK), 1) < FOX_HEADS
        run = carry[...]
        for b in range(G):
            z = ff_ref[b * BLK:(b + 1) * BLK, :] + b_ref[...]
            lf = jnp.where(live, jnp.minimum(z, 0.0) - jnp.log1p(jnp.exp(-jnp.abs(z))), 0.0)
            hi, mid, lo = _split3(lf)
            cs = (_dot(tri, hi) + _dot(tri, mid) + _dot(tri, lo) + run) * LOG2E
            c_ref[b * BLK:(b + 1) * BLK, :] = cs
            ct_ref[b] = cs.T[0:8, :]
            run = run + jnp.sum(lf, axis=0, keepdims=True)
        carry[...] = run

    return pl.pallas_call(
        body, name="f_foxprep", grid=(nblk // G,),
        in_specs=[pl.BlockSpec((G * BLK, BLK), lambda i: (i, 0)), _full((1, BLK))],
        out_specs=[pl.BlockSpec((G * BLK, BLK), lambda i: (i, 0)), pl.BlockSpec((G, 8, BLK), lambda i: (i, 0, 0))],
        out_shape=[jax.ShapeDtypeStruct((L, BLK), F32), jax.ShapeDtypeStruct((nblk, 8, BLK), F32)],
        scratch_shapes=[pltpu.VMEM((1, BLK), F32)],
        compiler_params=_params(("arbitrary",)),
    )(ff, fb)


def _rot_fns(cos, sin):
    lane = _iota((BLK, BLK), 1)
    first = (lane & (HEAD_LANES - 1)) < HEAD_LANES // 2

    def swap(x):
        return jnp.where(first, pltpu.roll(x, BLK - 32, 1), pltpu.roll(x, 32, 1))

    def rot(x):
        return x * cos + swap(x) * sin

    def rot_t(dy):
        return dy * cos + swap(dy * sin)

    return rot, rot_t


def _retention_fwd(proj, cos_t, sin_t, ret_g):
    L = proj.shape[0]
    nblk = L // BLK
    G = _block_group(nblk)
    dmat, wq_t, wk_t, g_blk = _decay_tables()

    def body(q_ref, k_ref, v_ref, gate_ref, cos_ref, sin_ref, d_ref, wq_ref, wk_ref, rg_ref,
             mix_ref, o_ref, rs_ref, state):
        @pl.when(pl.program_id(0) == 0)
        def _():
            state[...] = jnp.zeros_like(state)

        lane = _iota((BLK, BLK), 1)
        sub = _iota((BLK, BLK), 0)
        for b in range(G):
            rows = slice(b * BLK, (b + 1) * BLK)
            rot, _ = _rot_fns(cos_ref[rows, :], sin_ref[rows, :])
            for p in range(2):
                qr = rot(q_ref[rows, p * BLK:(p + 1) * BLK].astype(F32))
                kr = rot(k_ref[rows, p * BLK:(p + 1) * BLK].astype(F32)) * (HEAD_LANES ** -0.5)
                kr_b = kr.astype(BF16)
                qw = (qr * wq_ref[p]).astype(BF16)
                kw = (kr * wk_ref[p]).astype(BF16)
                for e in range(2):
                    h = 2 * p + e
                    cols = slice(h * BLK, (h + 1) * BLK)
                    qm = jnp.where((lane >> 6) == e, qr, 0.0).astype(BF16)
                    s = _dot_nt(qm, kr_b) * d_ref[h]
                    vh = v_ref[rows, cols]
                    st = state[h]
                    rs_ref[b, h] = st
                    o = _dot(s.astype(BF16), vh) + _dot(qw, st.astype(BF16))
                    u = jnp.where((sub >> 6) == e, _dot_tn(kw, vh), 0.0)
                    state[h] = g_blk[h] * st + u
                    rn = lax.rsqrt(jnp.mean(o * o, axis=-1, keepdims=True) + EPS)
                    gate = gate_ref[rows, cols].astype(F32)
                    o_ref[rows, cols] = o
                    mix_ref[rows, cols] = (o * rn * rg_ref[:, cols] * (gate * jax.nn.sigmoid(gate))).astype(BF16)

    row = lambda c: (lambda i: (i, c))
    return pl.pallas_call(
        body, name="f_retention", grid=(nblk // G,),
        in_specs=[pl.BlockSpec((G * BLK, 256), row(0)), pl.BlockSpec((G * BLK, 256), row(1)),
                  pl.BlockSpec((G * BLK, 512), row(1)), pl.BlockSpec((G * BLK, 512), row(2)),
                  pl.BlockSpec((G * BLK, BLK), row(0)), pl.BlockSpec((G * BLK, BLK), row(0)),
                  _full((RET_HEADS, BLK, BLK)), _full((2, BLK, BLK)), _full((2, BLK, BLK)), _full((1, 512))],
        out_specs=[pl.BlockSpec((G * BLK, 512), row(0)), pl.BlockSpec((G * BLK, 512), row(0)),
                   pl.BlockSpec((G, RET_HEADS, BLK, BLK), lambda i: (i, 0, 0, 0))],
        out_shape=[jax.ShapeDtypeStruct((L, 512), BF16), jax.ShapeDtypeStruct((L, 512), F32),
                   jax.ShapeDtypeStruct((nblk, RET_HEADS, BLK, BLK), F32)],
        scratch_shapes=[pltpu.VMEM((RET_HEADS, BLK, BLK), F32)],
        compiler_params=_params(("arbitrary",)),
    )(proj, proj, proj, proj, cos_t, sin_t, dmat, wq_t, wk_t, ret_g)


def _fox_units(L):
    nblk = L // BLK
    assert L % BLK == 0 and nblk % 2 == 1, "sequence must be one 128-row block plus whole 256-row tiles"
    return nblk, (nblk - 1) // 2


def _fox_tile_masks():
    sub, lane = _iota((BLK, BLK), 0), _iota((BLK, BLK), 1)
    valid = _iota((BLK, UNIT), 0) >= N_PAD
    diag = _iota((UNIT, UNIT), 0) <= _iota((UNIT, UNIT), 1)
    r, q = _iota((BLK + UNIT, UNIT), 0), _iota((BLK + UNIT, UNIT), 1)
    first_and_diag = ((r < BLK) & (r >= N_PAD)) | ((r >= BLK) & (r - BLK <= q))
    return dict(first=(sub <= lane) & (sub >= N_PAD), valid=valid, diag=diag, first_and_diag=first_and_diag)


def _fox_fwd(proj, c, ctb, gather=()):
    L = proj.shape[0]
    nblk, nu = _fox_units(L)
    scale = HEAD_LANES ** -0.5 * LOG2E
    ng = len(gather)
    FOX_PAIRS = 4
    steps = FOX_HEADS // (2 * FOX_PAIRS)

    def body(qkv_ref, c_ref, ct_ref, *rest):
        g_in, (of_ref, lse_ref), g_out = rest[:ng], rest[ng:ng + 2], rest[ng + 2:2 * ng + 2]
        vt, csb = rest[2 * ng + 2:2 * ng + 4]
        p = pl.program_id(0)
        heads = [(pp, e, 2 * FOX_PAIRS * p + 2 * pp + e) for pp in range(FOX_PAIRS) for e in range(2)]

        @pl.when(p == 0)
        def _():
            lse_ref[...] = jnp.zeros_like(lse_ref)
            if ng:
                local, sends, _ = _allgather_copies(g_in, g_out, *rest[2 * ng + 4:])
                for cp in local + sends:
                    cp.start()

        lane = _iota((BLK, BLK), 1)
        sub8 = _iota((8, BLK), 0)
        masks = _fox_tile_masks()

        def pre(j, carry):
            off = pl.multiple_of(j * BLK, BLK)
            ct = c_ref[pl.ds(off, BLK), :]
            for pp in range(FOX_PAIRS):
                vt[pp, j] = qkv_ref[pl.ds(off, BLK), pp * 384 + 2 * BLK:pp * 384 + 3 * BLK].astype(F32).T.astype(BF16)
            for hh, (_, _, h) in enumerate(heads):
                col = jnp.sum(jnp.where(lane == h, ct, 0.0), axis=1, keepdims=True)
                csb[hh, j] = jnp.broadcast_to(col, (BLK, BLK))
            return carry

        lax.fori_loop(0, nblk, pre, 0)

        def attend(qblk, nq, n_whole):
            qlen = nq * BLK
            qoff = pl.multiple_of(qblk * BLK, BLK)
            qlane = _iota((qlen, BLK), 1)
            qs = [qkv_ref[pl.ds(qoff, qlen), pp * 384:pp * 384 + BLK].astype(F32) * scale for pp in range(FOX_PAIRS)]
            qm = [jnp.where((qlane >> 6) == e, qs[pp], 0.0).astype(BF16) for pp, e, _ in heads]
            ct_row = [jnp.concatenate([_pick_row(ct_ref[qblk + a], h) for a in range(nq)], axis=1) for _, _, h in heads]

            def step(segs, mask, st):
                blocks = [kblk + b for kblk, nk in segs for b in range(nk)]
                kts = []
                for pp in range(FOX_PAIRS):
                    kt = [qkv_ref[pl.ds(pl.multiple_of(kblk * BLK, BLK), nk * BLK), pp * 384 + BLK:pp * 384 + 2 * BLK]
                          for kblk, nk in segs]
                    kts.append(kt[0] if len(kt) == 1 else jnp.concatenate(kt, axis=0))
                out = []
                for hh, (pp, e, _) in enumerate(heads):
                    m, l, acc = st[3 * hh:3 * hh + 3]
                    s = _dot_nt(kts[pp], qm[hh])
                    t = jnp.concatenate([s[b * BLK:(b + 1) * BLK] - jnp.concatenate([csb[hh, blk]] * nq, axis=1)
                                         for b, blk in enumerate(blocks)], axis=0)
                    if mask is not None:
                        t = jnp.where(mask, t, NEG)
                    m_new = jnp.maximum(m, jnp.max(t, axis=0, keepdims=True) + ct_row[hh])
                    alpha = jnp.exp2(m - m_new)
                    pr = jnp.exp2(t - (m_new - ct_row[hh]))
                    l = alpha * l + jnp.sum(pr, axis=0, keepdims=True)
                    pr_b = pr.astype(BF16)
                    pv = None
                    for b, blk in enumerate(blocks):
                        part = _dot(vt[pp, blk, e * HEAD_LANES:(e + 1) * HEAD_LANES, :], pr_b[b * BLK:(b + 1) * BLK])
                        pv = part if pv is None else pv + part
                    out += [m_new, l, alpha * acc + pv]
                return tuple(out)

            st = (jnp.full((1, qlen), NEG, F32), jnp.zeros((1, qlen), F32),
                  jnp.zeros((HEAD_LANES, qlen), F32)) * len(heads)
            if nq == 1:
                st = step([(0, 1)], masks["first"], st)
            else:
                st = step([(0, 1), (qblk, 2)], masks["first_and_diag"], st)
                n_wide = n_whole // WIDE
                st = lax.fori_loop(0, n_wide, lambda j, s_: step([(1 + 2 * WIDE * j, 2 * WIDE)], None, s_), st)
                rest = 1 + 2 * WIDE * n_wide
                st = lax.cond((n_whole & 2) != 0, lambda s_: step([(rest, 4)], None, s_), lambda s_: s_, st)
                st = lax.cond((n_whole & 1) != 0, lambda s_: step([(rest + 2 * (n_whole & 2), 2)], None, s_),
                              lambda s_: s_, st)
            for pp in range(FOX_PAIRS):
                lo, hi = st[6 * pp:6 * pp + 3], st[6 * pp + 3:6 * pp + 6]
                o_t = jnp.concatenate([lo[2] * (1.0 / lo[1]), hi[2] * (1.0 / hi[1])], axis=0)
                of_ref[pl.ds(qoff, qlen), pp * BLK:(pp + 1) * BLK] = o_t.T.astype(BF16)
            lse = [st[3 * hh] + jnp.log(st[3 * hh + 1]) * LOG2E for hh in range(len(heads))]
            for a in range(nq):
                upd = jnp.zeros((8, BLK), F32)
                for hh, (_, _, h) in enumerate(heads):
                    upd = upd + jnp.where(sub8 == h, lse[hh][:, a * BLK:(a + 1) * BLK], 0.0)
                lse_ref[qblk + a] = lse_ref[qblk + a] + upd

        attend(0, 1, 0)

        def q_loop(u, carry):
            attend(1 + 2 * u, 2, u)
            return carry

        lax.fori_loop(0, nu, q_loop, 0)

        if ng:
            @pl.when(p == steps - 1)
            def _():
                local, sends, recvs = _allgather_copies(g_in, g_out, *rest[2 * ng + 4:])
                for cp in recvs:
                    cp.wait_recv()
                for cp in sends:
                    cp.wait_send()
                for cp in local:
                    cp.wait()

    width = 384 * FOX_PAIRS
    return pl.pallas_call(
        body, name="f_fox", grid=(steps,),
        in_specs=[pl.BlockSpec((L, width), lambda p: (0, RET_W // width + p), pipeline_mode=pl.Buffered(1)),
                  _full((L, BLK)), _full((nblk, 8, BLK))]
        + [_ANY] * ng,
        out_specs=[pl.BlockSpec((L, FOX_PAIRS * BLK), lambda p: (0, p)), _full((nblk, 8, BLK))] + [_ANY] * ng,
        out_shape=[jax.ShapeDtypeStruct((L, 512), BF16), jax.ShapeDtypeStruct((nblk, 8, BLK), F32)]
        + [jax.ShapeDtypeStruct((N_CHIPS,) + a.shape, a.dtype) for a in gather],
        scratch_shapes=[pltpu.VMEM((FOX_PAIRS, nblk, BLK, BLK), BF16), pltpu.VMEM((2 * FOX_PAIRS, nblk, BLK, BLK), F32)]
        + _allgather_semaphores(ng),
        compiler_params=_params(("arbitrary",)),
    )(proj, c, ctb, *gather)


def _outproj_up(mix_r, o_f, h0, w_out, ffn_g, w_up, conv_w, conv_b):
    L = h0.shape[0]
    tm = _row_tile(L)
    shard = w_up.shape[2]
    assert 2 * shard == D_FF
    cw = [conv_w[j:j + 1] for j in range(3)]
    resident = lambda shape: pl.BlockSpec(shape, lambda i: (0,) * len(shape), pipeline_mode=pl.Buffered(1))

    def body(mr_ref, of_ref, h0_ref, wo_ref, g_ref, wu_ref, cw0, cw1, cw2, cb_ref,
             h1_ref, n2_ref, up_ref, act_ref, acc_ref, halo):
        i = pl.program_id(0)

        @pl.when(i == 0)
        def _():
            halo[...] = jnp.zeros_like(halo)

        h1 = h0_ref[...] + _dot(mr_ref[...], wo_ref[0:512, :]) + _dot(of_ref[...], wo_ref[512:1024, :])
        h1_ref[...] = h1
        r = lax.rsqrt(jnp.mean(h1 * h1, axis=-1, keepdims=True) + EPS)
        n2 = (h1 * r * g_ref[...]).astype(BF16)
        n2_ref[...] = n2
        live = i * tm + _iota((tm, 1), 0) >= N_PAD
        for half in range(2):
            cols = slice(half * shard, (half + 1) * shard)
            a_b = _dot(n2, wu_ref[half]).astype(BF16)
            b_b = _dot(n2, wu_ref[2 + half]).astype(BF16)
            up_ref[:, cols] = a_b
            up_ref[:, D_FF + half * shard:D_FF + (half + 1) * shard] = b_b
            a = jnp.where(live, a_b.astype(F32), 0.0)
            _, _, acc = _conv_taps(a, halo[:, cols], [cw0[:, cols], cw1[:, cols], cw2[:, cols]], cb_ref[:, cols])
            act_ref[:, cols] = (acc * jax.nn.sigmoid(acc) * b_b.astype(F32)).astype(BF16)
            acc_ref[:, cols] = acc.astype(BF16)
            halo[:, cols] = a[tm - 8:tm, :]

    rows = lambda w: pl.BlockSpec((tm, w), lambda i: (i, 0))
    return pl.pallas_call(
        body, name="f_outproj_up", grid=(L // tm,),
        in_specs=[rows(512), rows(512), rows(D_MODEL), resident((D_MODEL, D_MODEL)), _full((1, D_MODEL)),
                  resident((N_CHIPS, D_MODEL, shard)), _full((1, D_FF)), _full((1, D_FF)), _full((1, D_FF)),
                  _full((1, D_FF))],
        out_specs=[rows(D_MODEL), rows(D_MODEL), rows(2 * D_FF), rows(D_FF), rows(D_FF)],
        out_shape=[jax.ShapeDtypeStruct((L, D_MODEL), F32), jax.ShapeDtypeStruct((L, D_MODEL), BF16),
                   jax.ShapeDtypeStruct((L, 2 * D_FF), BF16), jax.ShapeDtypeStruct((L, D_FF), BF16),
                   jax.ShapeDtypeStruct((L, D_FF), BF16)],
        scratch_shapes=[pltpu.VMEM((8, D_FF), F32)],
        compiler_params=_params(("arbitrary",)),
    )(mix_r, o_f, h0, w_out, ffn_g, w_up, cw[0], cw[1], cw[2], conv_b)


def _conv_taps(a, halo, cw, cb):
    sub = _iota((a.shape[0], 1), 0)
    a1 = jnp.where(sub == 0, _pick_row(halo, 7), pltpu.roll(a, 1, 0))
    a2 = jnp.where(sub == 0, _pick_row(halo, 6), jnp.where(sub == 1, _pick_row(halo, 7), pltpu.roll(a, 2, 0)))
    acc = cb + a2 * cw[0]
    acc = acc + a1 * cw[1]
    acc = acc + a * cw[2]
    return a1, a2, acc


def _ffn_down_loss(g_act, w_down, h1, final_g, target, acc_saved, up):
    L = h1.shape[0]
    tm = _row_tile(L)
    nb = tm // BLK
    half_w = D_FF // 2

    def body(g_ref, wd_ref, h1_ref, gf_ref, acc_ref, b_ref, *refs):
        t_refs, (dh_ref, dhb_ref, dgf_ref, loss_ref, dacc_ref, db_ref) = refs[:nb], refs[nb:]
        i = pl.program_id(0)

        @pl.when(i == 0)
        def _():
            dgf_ref[...] = jnp.zeros_like(dgf_ref)
            loss_ref[...] = jnp.zeros_like(loss_ref)

        h2 = h1_ref[...] + _dot(g_ref[...], wd_ref[...])
        r = lax.rsqrt(jnp.mean(h2 * h2, axis=-1, keepdims=True) + EPS)
        yn = h2 * r
        gf = gf_ref[...]
        live = i * tm + _iota((tm, 1), 0) >= PREFIX
        target = jnp.concatenate([t[...] for t in t_refs], axis=0)
        err = jnp.where(live, yn * gf - target, 0.0)
        loss_ref[...] = loss_ref[...] + 0.5 * jnp.sum(jnp.mean(err * err, axis=-1, keepdims=True))
        dy = err * (1.0 / D_MODEL)
        dgf_ref[...] = dgf_ref[...] + jnp.sum(dy * yn, axis=0, keepdims=True)
        dyn = dy * gf
        dh = r * (dyn - yn * jnp.mean(dyn * yn, axis=-1, keepdims=True))
        dh_ref[...] = dh
        dhb = dh.astype(BF16)
        dhb_ref[...] = dhb
        for half in range(2):
            cols = slice(half * half_w, (half + 1) * half_w)
            acc = acc_ref[:, cols].astype(F32)
            dg = _dot_nt(dhb, wd_ref[cols, :])
            sg = jax.nn.sigmoid(acc)
            silu = acc * sg
            db_ref[:, cols] = (dg * silu).astype(BF16)
            dacc_ref[:, cols] = (dg * b_ref[:, cols].astype(F32) * (sg + silu * (1.0 - sg))).astype(BF16)

    rows = lambda w, c=0: pl.BlockSpec((tm, w), lambda i: (i, c))
    return pl.pallas_call(
        body, name="f_ffn_down_loss", grid=(L // tm,),
        in_specs=[rows(D_FF), pl.BlockSpec((D_FF, D_MODEL), lambda i: (0, 0), pipeline_mode=pl.Buffered(1)),
                  rows(D_MODEL), _full((1, D_MODEL)), rows(D_FF), rows(D_FF, 1)] + _shifted_blocks(tm),
        out_specs=[rows(D_MODEL), rows(D_MODEL), _full((1, D_MODEL)), _full((1, BLK)), rows(D_FF), rows(D_FF)],
        out_shape=[jax.ShapeDtypeStruct((L, D_MODEL), F32), jax.ShapeDtypeStruct((L, D_MODEL), BF16),
                   jax.ShapeDtypeStruct((1, D_MODEL), F32), jax.ShapeDtypeStruct((1, BLK), F32),
                   jax.ShapeDtypeStruct((L, D_FF), BF16), jax.ShapeDtypeStruct((L, D_FF), BF16)],
        compiler_params=_params(("arbitrary",)),
    )(g_act, w_down, h1, final_g, acc_saved, up, *([target] * nb))


def _ffn_bwd_up(dacc, db, up, conv_w, w_up, h1, ffn_g, dh2, w_out):
    L = h1.shape[0]
    tm = _row_tile(L)
    nt = L // tm
    shard = w_up.shape[2]
    cw = [conv_w[j:j + 1] for j in range(3)]

    def body(da_ref, halo_ref, db_ref, a_ref, cw0, cw1, cw2, wu_ref, h1_ref, g_ref, dh2_ref, wo_ref,
             dup_ref, dh1_ref, dh1b_ref, dmix_ref, dg_ref, dcw_ref):
        i = pl.program_id(0)

        @pl.when(i == 0)
        def _():
            dg_ref[...] = jnp.zeros_like(dg_ref)
            dcw_ref[...] = jnp.zeros_like(dcw_ref)

        sub = _iota((tm, 1), 0)
        sub8 = _iota((8, 1), 0)
        last_tile = i == nt - 1
        dbv = db_ref[...]
        dup_ref[:, D_FF:2 * D_FF] = dbv
        dn = _dot_nt(dbv[:, 0:shard], wu_ref[2]) + _dot_nt(dbv[:, shard:2 * shard], wu_ref[3])
        for half in range(2):
            cols = slice(half * shard, (half + 1) * shard)
            d0 = da_ref[:, cols].astype(F32)
            halo = jnp.where(last_tile, 0.0, halo_ref[:, cols].astype(F32))
            d1 = jnp.where(sub == tm - 1, _pick_row(halo, 0), pltpu.roll(d0, tm - 1, 0))
            d2 = jnp.where(sub == tm - 2, _pick_row(halo, 0),
                           jnp.where(sub == tm - 1, _pick_row(halo, 1), pltpu.roll(d0, tm - 2, 0)))
            a = a_ref[:, cols].astype(F32)
            upd = jnp.zeros((8, shard), F32)
            for j, t in enumerate((d2 * a, d1 * a, d0 * a, d0)):
                upd = upd + jnp.where(sub8 == j, jnp.sum(t, axis=0, keepdims=True), 0.0)
            dcw_ref[:, cols] = dcw_ref[:, cols] + upd
            da = (d0 * cw2[:, cols] + d1 * cw1[:, cols] + d2 * cw0[:, cols]).astype(BF16)
            dup_ref[:, cols] = da
            dn = dn + _dot_nt(da, wu_ref[half])
        h1 = h1_ref[...]
        r = lax.rsqrt(jnp.mean(h1 * h1, axis=-1, keepdims=True) + EPS)
        yn = h1 * r
        dg_ref[...] = dg_ref[...] + jnp.sum(dn * yn, axis=0, keepdims=True)
        dyn = dn * g_ref[...]
        dh1 = dh2_ref[...] + r * (dyn - yn * jnp.mean(dyn * yn, axis=-1, keepdims=True))
        dh1_ref[...] = dh1
        dh1b = dh1.astype(BF16)
        dh1b_ref[...] = dh1b
        dmix_ref[...] = _dot_nt(dh1b, wo_ref[...]).astype(BF16)

    rows = lambda w: pl.BlockSpec((tm, w), lambda i: (i, 0))
    halo = pl.BlockSpec((8, D_FF), lambda i: (jnp.minimum((i + 1) * (tm // 8), L // 8 - 1), 0))
    return pl.pallas_call(
        body, name="b_ffn_up", grid=(nt,),
        in_specs=[rows(D_FF), halo, rows(D_FF), rows(D_FF), _full((1, D_FF)), _full((1, D_FF)), _full((1, D_FF)),
                  _full((N_CHIPS, D_MODEL, shard)), rows(D_MODEL), _full((1, D_MODEL)), rows(D_MODEL),
                  _full((D_MODEL, D_MODEL))],
        out_specs=[rows(2 * D_FF), rows(D_MODEL), rows(D_MODEL), rows(D_MODEL), _full((1, D_MODEL)),
                   _full((8, D_FF))],
        out_shape=[jax.ShapeDtypeStruct((L, 2 * D_FF), BF16), jax.ShapeDtypeStruct((L, D_MODEL), F32),
                   jax.ShapeDtypeStruct((L, D_MODEL), BF16), jax.ShapeDtypeStruct((L, D_MODEL), BF16),
                   jax.ShapeDtypeStruct((1, D_MODEL), F32), jax.ShapeDtypeStruct((8, D_FF), F32)],
        compiler_params=_params(("arbitrary",)),
    )(dacc, dacc, db, up, cw[0], cw[1], cw[2], w_up, h1, ffn_g, dh2, w_out)


def _wgrad(a, b, name, tn=None, tk=None, out_dtype=F32):
    L, K = a.shape
    N = b.shape[1]
    tn = N if tn is None else tn
    tk = K if tk is None else tk
    tl = _row_tile(L, (1408, 768, 512, 256, 128))
    nl = L // tl

    def body(a_ref, b_ref, o_ref, acc):
        step = pl.program_id(2)

        @pl.when(step == 0)
        def _():
            acc[...] = jnp.zeros_like(acc)

        acc[...] = acc[...] + _dot_tn(a_ref[...], b_ref[...])

        @pl.when(step == nl - 1)
        def _():
            o_ref[0] = acc[...].astype(out_dtype)

    return pl.pallas_call(
        body, name=name, grid=(N // tn, K // tk, L // tl),
        in_specs=[pl.BlockSpec((tl, tk), lambda n, k, l: (l, k)), pl.BlockSpec((tl, tn), lambda n, k, l: (l, n))],
        out_specs=pl.BlockSpec((1, tk, tn), lambda n, k, l: (n, k, 0)),
        out_shape=jax.ShapeDtypeStruct((N // tn, K, tn), out_dtype),
        scratch_shapes=[pltpu.VMEM((tk, tn), F32)],
        compiler_params=_params(("parallel", "parallel", "arbitrary")),
    )(a, b)


def _retention_bwd(dmix, o, proj, cos_t, sin_t, ret_g, states, exchange=()):
    L = proj.shape[0]
    nblk = L // BLK
    G = _block_group(nblk)
    steps = nblk // G
    nx = len(exchange)
    dmat, wq_t, wk_t, g_blk = _decay_tables()

    def body(dm_ref, o_ref, q_ref, k_ref, v_ref, gate_ref, cos_ref, sin_ref, d_ref, wq_ref, wk_ref, rg_ref, rs_ref,
             *rest):
        x_in, (dp_ref, drg_ref), x_out, gstate = rest[:nx], rest[nx:nx + 2], rest[nx + 2:2 * nx + 2], rest[2 * nx + 2]

        @pl.when(pl.program_id(0) == 0)
        def _():
            if nx:
                for cp in _sibling_half_copies(x_in, x_out, *rest[2 * nx + 3:])[0]:
                    cp.start()
            gstate[...] = jnp.zeros_like(gstate)
            drg_ref[...] = jnp.zeros_like(drg_ref)

        lane = _iota((BLK, BLK), 1)
        sub = _iota((BLK, BLK), 0)
        scale = HEAD_LANES ** -0.5
        for b in reversed(range(G)):
            rows = slice(b * BLK, (b + 1) * BLK)
            rot, rot_t = _rot_fns(cos_ref[rows, :], sin_ref[rows, :])
            for p in range(2):
                qr = rot(q_ref[rows, p * BLK:(p + 1) * BLK].astype(F32))
                kr = rot(k_ref[rows, p * BLK:(p + 1) * BLK].astype(F32)) * scale
                kr_b = kr.astype(BF16)
                qw = (qr * wq_ref[p]).astype(BF16)
                kw = (kr * wk_ref[p]).astype(BF16)
                dqr = jnp.zeros((BLK, BLK), F32)
                dkr = jnp.zeros((BLK, BLK), F32)
                for e in range(2):
                    h = 2 * p + e
                    cols = slice(h * BLK, (h + 1) * BLK)
                    head_lanes = (lane >> 6) == e
                    o = o_ref[rows, cols]
                    rn = lax.rsqrt(jnp.mean(o * o, axis=-1, keepdims=True) + EPS)
                    y = o * rn
                    gate = gate_ref[rows, cols].astype(F32)
                    sg = jax.nn.sigmoid(gate)
                    dm = dm_ref[rows, cols].astype(F32)
                    rgain = rg_ref[:, cols]
                    drg_ref[:, cols] = drg_ref[:, cols] + jnp.sum(dm * y * (gate * sg), axis=0, keepdims=True)
                    dp_ref[rows, 1024 + h * BLK:1024 + (h + 1) * BLK] = (
                        dm * y * rgain * (sg * (1.0 + gate * (1.0 - sg)))).astype(BF16)
                    dy = dm * rgain * (gate * sg)
                    do = (rn * (dy - y * jnp.mean(dy * y, axis=-1, keepdims=True))).astype(BF16)
                    vh = v_ref[rows, cols]
                    qm = jnp.where(head_lanes, qr, 0.0).astype(BF16)
                    dmh = d_ref[h]
                    s = (_dot_nt(qm, kr_b) * dmh).astype(BF16)
                    ds = (_dot_nt(do, vh) * dmh).astype(BF16)
                    st = rs_ref[b, h].astype(BF16)
                    gs = gstate[h]
                    gs_b = gs.astype(BF16)
                    dqr = dqr + jnp.where(head_lanes, _dot(ds, kr_b), 0.0) + _dot_nt(do, st) * wq_ref[p]
                    dkr = dkr + _dot_tn(ds, qm) + _dot_nt(vh, gs_b) * wk_ref[p]
                    dp_ref[rows, 512 + h * BLK:512 + (h + 1) * BLK] = (_dot_tn(s, do) + _dot(kw, gs_b)).astype(BF16)
                    dr = jnp.where((sub >> 6) == e, _dot_tn(qw, do), 0.0)
                    gstate[h] = dr + g_blk[h] * gs
                dp_ref[rows, p * BLK:(p + 1) * BLK] = rot_t(dqr).astype(BF16)
                dp_ref[rows, 256 + p * BLK:256 + (p + 1) * BLK] = (rot_t(dkr) * scale).astype(BF16)

        if nx:
            @pl.when(pl.program_id(0) == steps - 1)
            def _():
                sends, recvs = _sibling_half_copies(x_in, x_out, *rest[2 * nx + 3:])
                for cp in recvs:
                    cp.wait_recv()
                for cp in sends:
                    cp.wait_send()

    row = lambda c: (lambda i: (steps - 1 - i, c))
    return pl.pallas_call(
        body, name="b_retention", grid=(steps,),
        in_specs=[pl.BlockSpec((G * BLK, 512), row(0)), pl.BlockSpec((G * BLK, 512), row(0)),
                  pl.BlockSpec((G * BLK, 256), row(0)), pl.BlockSpec((G * BLK, 256), row(1)),
                  pl.BlockSpec((G * BLK, 512), row(1)), pl.BlockSpec((G * BLK, 512), row(2)),
                  pl.BlockSpec((G * BLK, BLK), row(0)), pl.BlockSpec((G * BLK, BLK), row(0)),
                  _full((RET_HEADS, BLK, BLK)), _full((2, BLK, BLK)), _full((2, BLK, BLK)), _full((1, 512)),
                  pl.BlockSpec((G, RET_HEADS, BLK, BLK), lambda i: (steps - 1 - i, 0, 0, 0))] + [_ANY] * nx,
        out_specs=[pl.BlockSpec((G * BLK, RET_W), row(0)), _full((1, 512))] + [_ANY] * nx,
        out_shape=[jax.ShapeDtypeStruct((L, RET_W), BF16), jax.ShapeDtypeStruct((1, 512), F32)]
        + _sibling_half_shapes(exchange),
        scratch_shapes=[pltpu.VMEM((RET_HEADS, BLK, BLK), F32)] + _sibling_half_semaphores(nx),
        compiler_params=_params(("arbitrary",)),
    )(dmix, o, proj, proj, proj, proj, cos_t, sin_t, dmat, wq_t, wk_t, ret_g, states, *exchange)


def _fox_delta(dmix, o_f):
    L = o_f.shape[0]
    nblk = L // BLK
    G = _block_group(nblk, SMALL_GROUP)

    def body(do_ref, o_ref, d_ref):
        sel = ((_iota((8, 512), 1) >> 6) == _iota((8, 512), 0)).astype(BF16)
        for b in range(G):
            rows = slice(b * BLK, (b + 1) * BLK)
            prod = do_ref[rows, :].astype(F32) * o_ref[rows, :].astype(F32)
            hi = prod.astype(BF16)
            lo = (prod - hi.astype(F32)).astype(BF16)
            d_ref[b] = _dot_nt(sel, hi) + _dot_nt(sel, lo)

    return pl.pallas_call(
        body, name="b_foxdelta", grid=(nblk // G,),
        in_specs=[pl.BlockSpec((G * BLK, 512), lambda i: (i, 1)), pl.BlockSpec((G * BLK, 512), lambda i: (i, 0))],
        out_specs=pl.BlockSpec((G, 8, BLK), lambda i: (i, 0, 0)),
        out_shape=jax.ShapeDtypeStruct((nblk, 8, BLK), F32),
        compiler_params=_params(("parallel",)),
    )(dmix, o_f)


def _fox_bwd(proj, dmix, c, ctb, lse, delta, scatter=()):
    L = proj.shape[0]
    nblk, nu = _fox_units(L)
    scale = HEAD_LANES ** -0.5
    ns = len(scatter)

    steps = FOX_HEADS // (2 * FOX_PAIRS)

    def body(qkv_ref, do_ref, c_ref, ct_ref, lse_ref, dl_ref, *rest):
        s_in, (dp_ref, dc_ref, dcq_ref), s_out = rest[:ns], rest[ns:ns + 3], rest[ns + 3:2 * ns + 3]
        ktt, dqt, dk_acc, dv_acc, dcs_acc = rest[2 * ns + 3:2 * ns + 8]
        p = pl.program_id(0)
        heads = [(pp, e, 2 * FOX_PAIRS * p + 2 * pp + e) for pp in range(FOX_PAIRS) for e in range(2)]

        @pl.when(p == 0)
        def _():
            dc_ref[...] = jnp.zeros_like(dc_ref)
            dcq_ref[...] = jnp.zeros_like(dcq_ref)
            if ns:
                for cp in _scatter_copies(s_in, s_out, *rest[2 * ns + 8:]):
                    cp.start()

        sub8 = _iota((8, BLK), 0)
        masks = _fox_tile_masks()

        def pre(j, carry):
            off = pl.multiple_of(j * BLK, BLK)
            for pp in range(FOX_PAIRS):
                ktt[pp, j] = qkv_ref[pl.ds(off, BLK), pp * 384 + BLK:pp * 384 + 2 * BLK].astype(F32).T.astype(BF16)
                dqt[pp, j] = jnp.zeros((BLK, BLK), F32)
            return carry

        lax.fori_loop(0, nblk, pre, 0)

        def kv_pass(kblk, nk, n_later):
            klen = nk * BLK
            koff = pl.multiple_of(kblk * BLK, BLK)
            kt = [qkv_ref[pl.ds(koff, klen), pp * 384 + BLK:pp * 384 + 2 * BLK] for pp in range(FOX_PAIRS)]
            vtile = [qkv_ref[pl.ds(koff, klen), pp * 384 + 2 * BLK:pp * 384 + 3 * BLK] for pp in range(FOX_PAIRS)]
            ct = c_ref[pl.ds(koff, klen), :]
            klane = _iota((klen, BLK), 1)
            cs = [jnp.broadcast_to(jnp.sum(jnp.where(klane == h, ct, 0.0), axis=1, keepdims=True), (klen, WIDE * UNIT))
                  for _, _, h in heads]
            k_t = [jnp.concatenate([ktt[pp, kblk + b, e * HEAD_LANES:(e + 1) * HEAD_LANES, :] for b in range(nk)], axis=1)
                   for pp, e, _ in heads]
            for pp in range(FOX_PAIRS):
                dk_acc[pp, 0:klen] = jnp.zeros((klen, BLK), F32)
                dv_acc[pp, 0:klen] = jnp.zeros((klen, BLK), F32)
            for hh in range(len(heads)):
                dcs_acc[hh, 0:klen] = jnp.zeros((klen, BLK), F32)

            def tile(qblk, nq, mask):
                qlen = nq * BLK
                if mask == "valid":
                    mask = _iota((klen, qlen), 0) >= N_PAD
                qoff = pl.multiple_of(qblk * BLK, BLK)
                qlane = _iota((qlen, BLK), 1)
                qs = [qkv_ref[pl.ds(qoff, qlen), pp * 384:pp * 384 + BLK].astype(F32) * (scale * LOG2E)
                      for pp in range(FOX_PAIRS)]
                dot_ = [do_ref[pl.ds(qoff, qlen), pp * BLK:(pp + 1) * BLK] for pp in range(FOX_PAIRS)]
                stats = [[ref[qblk + a] for a in range(nq)] for ref in (ct_ref, lse_ref, dl_ref)]
                dcq = [jnp.zeros((8, BLK), F32) for _ in range(nq)]
                for hh, (pp, e, h) in enumerate(heads):
                    head = (qlane >> 6) == e
                    ct_row, lse_row, dl_row = [jnp.concatenate([_pick_row(t, h) for t in ts], axis=1) for ts in stats]
                    qm = jnp.where(head, qs[pp], 0.0).astype(BF16)
                    dom = jnp.where(head, dot_[pp], jnp.zeros_like(dot_[pp]))
                    t = _dot_nt(kt[pp], qm) - cs[hh][:, 0:qlen]
                    if mask is not None:
                        t = jnp.where(mask, t, NEG)
                    pr = jnp.exp2(t + (ct_row - lse_row))
                    dv_acc[pp, 0:klen] = dv_acc[pp, 0:klen] + _dot(pr.astype(BF16), dom)
                    dsv = pr * (_dot_nt(vtile[pp], dom) - dl_row)
                    ds_b = dsv.astype(BF16)
                    dk_acc[pp, 0:klen] = dk_acc[pp, 0:klen] + _dot(ds_b, qm)
                    rows = slice(e * HEAD_LANES, (e + 1) * HEAD_LANES)
                    dq_t = _dot(k_t[hh], ds_b)
                    key_side = dsv[:, 0:BLK]
                    for a in range(1, nq):
                        key_side = key_side + dsv[:, a * BLK:(a + 1) * BLK]
                    dcs_acc[hh, 0:klen] = dcs_acc[hh, 0:klen] + key_side
                    query_side = jnp.sum(dsv, axis=0, keepdims=True)
                    for a in range(nq):
                        cols = slice(a * BLK, (a + 1) * BLK)
                        dqt[pp, qblk + a, rows, :] = dqt[pp, qblk + a, rows, :] + dq_t[:, cols]
                        dcq[a] = dcq[a] + jnp.where(sub8 == h, query_side[:, cols], 0.0)
                for a in range(nq):
                    dcq_ref[qblk + a] = dcq_ref[qblk + a] + dcq[a]

            later_mask = "valid" if nk == 1 else None
            n_later = jnp.asarray(n_later, jnp.int32)
            n_wide = n_later // WIDE

            def later_wide(i, carry):
                tile(kblk + nk + 2 * WIDE * i, 2 * WIDE, later_mask)
                return carry

            tile(kblk, nk, masks["first"] if nk == 1 else masks["diag"])
            lax.fori_loop(0, n_wide, later_wide, 0)
            rest_blk = kblk + nk + 2 * WIDE * n_wide

            @pl.when((n_later & 2) != 0)
            def _():
                tile(rest_blk, 4, later_mask)

            @pl.when((n_later & 1) != 0)
            def _():
                tile(rest_blk + 2 * (n_later & 2), 2, later_mask)

            upd = jnp.zeros((klen, BLK), F32)
            for hh, (_, _, h) in enumerate(heads):
                upd = upd + jnp.where(klane == h, -jnp.sum(dcs_acc[hh, 0:klen], axis=1, keepdims=True), 0.0)
            dc_ref[pl.ds(koff, klen), :] = dc_ref[pl.ds(koff, klen), :] + upd
            for pp in range(FOX_PAIRS):
                dp_ref[pl.ds(koff, klen), pp * 384 + BLK:pp * 384 + 2 * BLK] = (
                    dk_acc[pp, 0:klen] * (1.0 / LOG2E)).astype(BF16)
                dp_ref[pl.ds(koff, klen), pp * 384 + 2 * BLK:pp * 384 + 3 * BLK] = dv_acc[pp, 0:klen].astype(BF16)

        kv_pass(0, 1, nu)

        def k_loop(u, carry):
            kv_pass(1 + 2 * u, 2, nu - 1 - u)
            return carry

        lax.fori_loop(0, nu, k_loop, 0)

        def flush(j, carry):
            off = pl.multiple_of(j * BLK, BLK)
            for pp in range(FOX_PAIRS):
                dp_ref[pl.ds(off, BLK), pp * 384:pp * 384 + BLK] = (dqt[pp, j].T * scale).astype(BF16)
            return carry

        lax.fori_loop(0, nblk, flush, 0)

        if ns:
            @pl.when(p == steps - 1)
            def _():
                copies = _scatter_copies(s_in, s_out, *rest[2 * ns + 8:])
                for cp in copies:
                    cp.wait_recv()
                for cp in copies:
                    cp.wait_send()

    width = 384 * FOX_PAIRS
    once = lambda shape, index: pl.BlockSpec(shape, index, pipeline_mode=pl.Buffered(1))
    stat = once((nblk, 8, BLK), lambda p: (0, 0, 0))
    return pl.pallas_call(
        body, name="b_fox", grid=(steps,),
        in_specs=[once((L, width), lambda p: (0, RET_W // width + p)),
                  once((L, FOX_PAIRS * BLK), lambda p: (0, 4 // FOX_PAIRS + p)),
                  once((L, BLK), lambda p: (0, 0)), stat, stat, stat] + [_ANY] * ns,
        out_specs=[pl.BlockSpec((L, width), lambda p: (0, p)), _full((L, BLK)), _full((nblk, 8, BLK))] + [_ANY] * ns,
        out_shape=[jax.ShapeDtypeStruct((L, FOX_W), BF16), jax.ShapeDtypeStruct((L, BLK), F32),
                   jax.ShapeDtypeStruct((nblk, 8, BLK), F32)] + _scatter_shapes(scatter),
        scratch_shapes=[pltpu.VMEM((FOX_PAIRS, nblk, BLK, BLK), BF16), pltpu.VMEM((FOX_PAIRS, nblk, BLK, BLK), F32),
                        pltpu.VMEM((FOX_PAIRS, UNIT, BLK), F32), pltpu.VMEM((FOX_PAIRS, UNIT, BLK), F32),
                        pltpu.VMEM((2 * FOX_PAIRS, UNIT, BLK), F32)]
        + _scatter_semaphores(ns),
        compiler_params=_params(("arbitrary",)),
    )(proj, dmix, c, ctb, lse, delta, *scatter)


def _fox_post(dc, dcq, ff, fb):
    L = dc.shape[0]
    nblk = L // BLK
    G = _block_group(nblk, SMALL_GROUP)
    steps = nblk // G

    def body(dc_ref, dcq_ref, ff_ref, b_ref, dff_ref, dffb_ref, dfb_ref, carry):
        @pl.when(pl.program_id(0) == 0)
        def _():
            carry[...] = jnp.zeros_like(carry)
            dfb_ref[...] = jnp.zeros_like(dfb_ref)

        tri = (_iota((BLK, BLK), 0) <= _iota((BLK, BLK), 1)).astype(BF16)
        live = _iota((BLK, BLK), 1) < FOX_HEADS
        run, dfb = carry[...], dfb_ref[...]
        for b in reversed(range(G)):
            rows = slice(b * BLK, (b + 1) * BLK)
            d = dc_ref[rows, :] + jnp.concatenate([dcq_ref[b], jnp.zeros((BLK - 8, BLK), F32)], axis=0).T
            hi, mid, lo = _split3(d)
            dlf = _dot(tri, hi) + _dot(tri, mid) + _dot(tri, lo) + run
            run = run + jnp.sum(d, axis=0, keepdims=True)
            z = ff_ref[rows, :] + b_ref[...]
            dff = jnp.where(live, dlf * jax.nn.sigmoid(-z), 0.0)
            dff_ref[rows, :] = dff
            dffb_ref[rows, :] = dff.astype(BF16)
            dfb = dfb + jnp.sum(dff, axis=0, keepdims=True)
        carry[...] = run
        dfb_ref[...] = dfb

    rev = lambda i: (steps - 1 - i, 0)
    return pl.pallas_call(
        body, name="b_foxpost", grid=(steps,),
        in_specs=[pl.BlockSpec((G * BLK, BLK), rev), pl.BlockSpec((G, 8, BLK), lambda i: (steps - 1 - i, 0, 0)),
                  pl.BlockSpec((G * BLK, BLK), rev), _full((1, BLK))],
        out_specs=[pl.BlockSpec((G * BLK, BLK), rev), pl.BlockSpec((G * BLK, BLK), rev), _full((1, BLK))],
        out_shape=[jax.ShapeDtypeStruct((L, BLK), F32), jax.ShapeDtypeStruct((L, BLK), BF16),
                   jax.ShapeDtypeStruct((1, BLK), F32)],
        scratch_shapes=[pltpu.VMEM((1, BLK), F32)],
        compiler_params=_params(("arbitrary",)),
    )(dc, dcq, ff, fb)


def _inproj_bwd(dpr, dpf, dffb, w_main, w_ff, h0, g, dh1, scatter=()):
    L = h0.shape[0]
    S = L - BLK
    tm = _row_tile(S, (512, 256, 128))
    nt = S // tm
    ns = len(scatter)
    operands = (dpr, dpf, dffb, h0, dh1)

    def body(*refs):
        lead, tile = refs[0:5], refs[5:10]
        wm_ref, wf_ref, g_ref = refs[10:13]
        rest = refs[13:]
        s_in, (dlead_ref, dx_ref, dg_ref), s_out = rest[:ns], rest[ns:ns + 3], rest[ns + 3:2 * ns + 3]
        i = pl.program_id(0)

        def rows_bwd(dpr_ref, dpf_ref, dff_ref, h_ref, dh1_ref):
            dn = (_dot_nt(dpr_ref[...], wm_ref[:, 0:RET_W]) + _dot_nt(dpf_ref[...], wm_ref[:, RET_W:MAIN_W])
                  + _dot_nt(dff_ref[...], wf_ref[...]))
            h = h_ref[...]
            r = lax.rsqrt(jnp.mean(h * h, axis=-1, keepdims=True) + EPS)
            yn = h * r
            dyn = dn * g_ref[...]
            dh0 = dh1_ref[...] + r * (dyn - yn * jnp.mean(dyn * yn, axis=-1, keepdims=True))
            return dh0, jnp.sum(dn * yn, axis=0, keepdims=True)

        @pl.when(i == 0)
        def _():
            if ns:
                for cp in _scatter_copies(s_in, s_out, *rest[2 * ns + 3:]):
                    cp.start()
            dlead_ref[...], dg_ref[...] = rows_bwd(*lead)

        dx_ref[...], dg_tile = rows_bwd(*tile)
        dg_ref[...] = dg_ref[...] + dg_tile

        if ns:
            @pl.when(i == nt - 1)
            def _():
                copies = _scatter_copies(s_in, s_out, *rest[2 * ns + 3:])
                for cp in copies:
                    cp.wait_recv()
                for cp in copies:
                    cp.wait_send()

    lead_spec = lambda a: pl.BlockSpec((BLK, a.shape[1]), lambda i: (0, 0))
    tile_spec = lambda a: pl.BlockSpec((pl.Element(tm), pl.Element(a.shape[1])),
                                       lambda i: (pl.multiple_of(BLK + i * tm, BLK), 0))
    return pl.pallas_call(
        body, name="b_inproj", grid=(nt,),
        in_specs=[lead_spec(a) for a in operands] + [tile_spec(a) for a in operands]
        + [_full((D_MODEL, MAIN_W)), _full((D_MODEL, BLK)), _full((1, D_MODEL))] + [_ANY] * ns,
        out_specs=[_full((BLK, D_MODEL)), pl.BlockSpec((tm, D_MODEL), lambda i: (i, 0)), _full((1, D_MODEL))]
        + [_ANY] * ns,
        out_shape=[jax.ShapeDtypeStruct((BLK, D_MODEL), F32), jax.ShapeDtypeStruct((S, D_MODEL), F32),
                   jax.ShapeDtypeStruct((1, D_MODEL), F32)] + _scatter_shapes(scatter),
        scratch_shapes=_scatter_semaphores(ns),
        compiler_params=_params(("arbitrary",)),
    )(*operands, *operands, w_main, w_ff, g, *scatter)


def _local_step(x, target, meta, attn_g, w_main, w_ff, fox_b, ret_g, w_out, ffn_g, w_up, conv_w, conv_b, w_down, final_g,
                late=None, mid=None, last=None, wire=F32):
    S = x.shape[0]
    L = S + PREFIX
    head = jnp.concatenate([jnp.zeros((N_PAD, D_MODEL), F32), meta], axis=0)
    fb = jnp.pad(fox_b, ((0, 0), (0, BLK - FOX_HEADS)))
    cos_t, sin_t = _rotary_tables(L)

    h0, n1, proj, ff = _rms_inproj(head, x, attn_g, w_main, w_ff)
    c, ctb = _fox_prep(ff, fb)
    mix_r, o_ret, states = _retention_fwd(proj, cos_t, sin_t, ret_g)
    if late is None:
        o_f, lse = _fox_fwd(proj, c, ctb)
    else:
        o_f, lse, *gathered = _fox_fwd(proj, c, ctb, gather=late[0])
        w_out, w_up, w_down = late[1](gathered)
    h1, n2, up, g_act, acc_saved = _outproj_up(mix_r, o_f, h0, w_out, ffn_g, w_up, conv_w, conv_b)
    dh2, dh2b, d_final_g, loss, dacc, db = _ffn_down_loss(g_act, w_down, h1, final_g, target, acc_saved, up)

    dup, dh1, dh1b, dmix, d_ffn_g, dconv = _ffn_bwd_up(dacc, db, up, conv_w, w_up, h1, ffn_g, dh2, w_out)
    d_w_down = _wgrad(g_act, dh2b, "wgrad_down", tk=D_FF // 2, out_dtype=wire)[0]
    d_w_up = _wgrad(n2, dup, "wgrad_up", tn=w_up.shape[2], out_dtype=wire)
    d_w_out = jnp.concatenate([_wgrad(mix_r, dh1b, "wgrad_out_r", out_dtype=wire)[0],
                               _wgrad(o_f, dh1b, "wgrad_out_f", out_dtype=wire)[0]], axis=0)

    early = () if mid is None else mid[0](d_w_out, d_w_up, d_w_down)
    dpr, d_ret_g, *from_sibling = _retention_bwd(dmix, o_ret, proj, cos_t, sin_t, ret_g, states, exchange=early)
    delta = _fox_delta(dmix, o_f)
    scatter = () if mid is None else mid[1](early, from_sibling)
    dpf, dc, dcq, *received = _fox_bwd(proj, dmix, c, ctb, lse, delta, scatter=scatter)
    dff, dffb, d_fox_b = _fox_post(dc, dcq, ff, fb)
    d_w_ret, d_w_fox = _wgrad(n1, dpr, "wgrad_in_r")[0], _wgrad(n1, dpf, "wgrad_in_f")[0]
    d_w_ff = _wgrad(n1, dffb, "wgrad_in_ff")[0][:, :FOX_HEADS]
    scatter_in = () if last is None else last(d_w_ret, d_w_fox, d_w_ff)
    dlead, dx, d_attn_g, *received_in = _inproj_bwd(dpr, dpf, dffb, w_main, w_ff, h0, attn_g, dh1, scatter=scatter_in)

    return dict(
        loss=loss[0, 0], dx=dx, dmeta=dlead[N_PAD:], attn_g=d_attn_g, w_main=jnp.concatenate([d_w_ret, d_w_fox], axis=1),
        w_ff=d_w_ff, fox_b=d_fox_b[:, :FOX_HEADS], ret_g=d_ret_g, w_out=d_w_out, ffn_g=d_ffn_g,
        w_up=d_w_up, conv_w=dconv[0:3], conv_b=dconv[3:4], w_down=d_w_down, final_g=d_final_g,
        scatter=list(scatter_in) + list(scatter), received=list(received_in) + list(received))


_ANY = pl.BlockSpec(memory_space=pl.ANY)


def _place():
    return lax.axis_index("x"), lax.axis_index("y"), lax.axis_index("c")


def _other_chips(x, y):
    return [(1 - x, y), (x, 1 - y), (1 - x, 1 - y)]


def _allgather_semaphores(n):
    if n == 0:
        return []
    return [pltpu.SemaphoreType.DMA((3 * n,)), pltpu.SemaphoreType.DMA((3 * n,)), pltpu.SemaphoreType.DMA((n,))]


def _allgather_copies(ins, outs, send, recv, loc):
    n = len(ins)
    x, y, c = _place()
    mine = 2 * x + y
    peers = _other_chips(x, y)

    def remote(a, k, slot):
        return pltpu.make_async_remote_copy(
            src_ref=ins[a], dst_ref=outs[a].at[slot], send_sem=send.at[3 * a + k], recv_sem=recv.at[3 * a + k],
            device_id=(peers[k][0], peers[k][1], c), device_id_type=MESH)

    local = [pltpu.make_async_copy(ins[a], outs[a].at[mine], loc.at[a]) for a in range(n)]
    sends = [remote(a, k, mine) for a in range(n) for k in range(3)]
    recvs = [remote(a, k, 2 * peers[k][0] + peers[k][1]) for a in range(n) for k in range(3)]
    return local, sends, recvs


def _chip_allgather_halves(w, small):
    half = w.shape[0] // 2

    def body(w_ref, s_ref, wo_ref, so_ref, send, recv, fsend, frecv, ssend, srecv, loc):
        x, y, c = _place()
        mine = 2 * x + y
        peers = _other_chips(x, y)

        def fetch(k, slot):
            return pltpu.make_async_remote_copy(
                src_ref=w_ref.at[pl.ds(c * half, half)], dst_ref=wo_ref.at[slot, c], send_sem=send.at[k],
                recv_sem=recv.at[k], device_id=(peers[k][0], peers[k][1], c), device_id_type=MESH)

        def forward(k, which):
            slot = 2 * peers[k][0] + peers[k][1]
            return pltpu.make_async_remote_copy(
                src_ref=wo_ref.at[slot, which], dst_ref=wo_ref.at[slot, which], send_sem=fsend.at[k],
                recv_sem=frecv.at[k], device_id=(x, y, 1 - c), device_id_type=MESH)

        def small_copy(k, slot):
            return pltpu.make_async_remote_copy(
                src_ref=s_ref, dst_ref=so_ref.at[slot], send_sem=ssend.at[k], recv_sem=srecv.at[k],
                device_id=(peers[k][0], peers[k][1], c), device_id_type=MESH)

        local = pltpu.make_async_copy(s_ref, so_ref.at[mine], loc.at[0])
        sends = [fetch(k, mine) for k in range(3)] + [small_copy(k, mine) for k in range(3)]
        local.start()
        for cp in sends:
            cp.start()
        forwards = []
        for k in range(3):
            fetch(k, 2 * peers[k][0] + peers[k][1]).wait_recv()
            forwards.append(forward(k, c))
            forwards[-1].start()
        for k in range(3):
            forward(k, 1 - c).wait_recv()
            small_copy(k, 2 * peers[k][0] + peers[k][1]).wait_recv()
        for cp in sends + forwards:
            cp.wait_send()
        local.wait()

    three = pltpu.SemaphoreType.DMA((3,))
    return pl.pallas_call(
        body, name="ag_weights", in_specs=[_ANY] * 2, out_specs=[_ANY] * 2,
        out_shape=[jax.ShapeDtypeStruct((N_CHIPS, 2, half, w.shape[1]), w.dtype),
                   jax.ShapeDtypeStruct((N_CHIPS,) + small.shape, small.dtype)],
        scratch_shapes=[three, three, three, three, three, three, pltpu.SemaphoreType.DMA((1,))],
    )(w, small)


def _chip_allgather(arrays):
    n = len(arrays)

    def body(*refs):
        local, sends, recvs = _allgather_copies(refs[:n], refs[n:2 * n], *refs[2 * n:])
        for cp in local + sends:
            cp.start()
        for cp in recvs:
            cp.wait_recv()
        for cp in sends:
            cp.wait_send()
        for cp in local:
            cp.wait()

    return pl.pallas_call(
        body, name="ag_weights", in_specs=[_ANY] * n, out_specs=[_ANY] * n,
        out_shape=[jax.ShapeDtypeStruct((N_CHIPS,) + a.shape, a.dtype) for a in arrays],
        scratch_shapes=_allgather_semaphores(n),
    )(*arrays)


def _sibling_halves(grads):
    n = len(grads)

    def body(*refs):
        sends, recvs = _sibling_half_copies(refs[:n], refs[n:2 * n], *refs[2 * n:])
        for cp in sends:
            cp.start()
        for cp in recvs:
            cp.wait_recv()
        for cp in sends:
            cp.wait_send()

    return pl.pallas_call(
        body, name="rs_sibling", in_specs=[_ANY] * n, out_specs=[_ANY] * n,
        out_shape=_sibling_half_shapes(grads), scratch_shapes=_sibling_half_semaphores(n),
    )(*grads)


def _sibling_half_shapes(grads):
    return [jax.ShapeDtypeStruct((N_CHIPS, g.shape[1] // 2, g.shape[2]), g.dtype) for g in grads]


def _sibling_half_semaphores(n):
    return [pltpu.SemaphoreType.DMA((n,)), pltpu.SemaphoreType.DMA((n,))] if n else []


def _sibling_half_copies(ins, outs, send, recv):
    x, y, c = _place()

    def half_copy(a, which):
        half = ins[a].shape[1] // 2
        return pltpu.make_async_remote_copy(
            src_ref=ins[a].at[pl.ds(0, N_CHIPS), pl.ds(which * half, half)], dst_ref=outs[a],
            send_sem=send.at[a], recv_sem=recv.at[a], device_id=(x, y, 1 - c), device_id_type=MESH)

    return [half_copy(a, 1 - c) for a in range(len(ins))], [half_copy(a, c) for a in range(len(ins))]


def _scatter_shapes(parts):
    return [jax.ShapeDtypeStruct((3,) + p.shape[1:], p.dtype) for p in parts]


def _scatter_semaphores(n):
    return [pltpu.SemaphoreType.DMA((3 * n,)), pltpu.SemaphoreType.DMA((3 * n,))] if n else []


def _scatter_copies(ins, outs, send, recv):
    x, y, c = _place()
    peers = _other_chips(x, y)
    return [pltpu.make_async_remote_copy(
        src_ref=ins[a].at[2 * peers[k][0] + peers[k][1]], dst_ref=outs[a].at[k], send_sem=send.at[3 * a + k],
        recv_sem=recv.at[3 * a + k], device_id=(peers[k][0], peers[k][1], c), device_id_type=MESH)
        for a in range(len(ins)) for k in range(3)]


def _sibling_allgather(bufs, small):
    n = len(bufs)

    def body(*refs):
        small_in, outs, small_out = refs[n], refs[n + 1:2 * n + 1], refs[2 * n + 1]
        send, recv, s_send, s_recv, loc = refs[2 * n + 2:]
        x, y, c = _place()
        me = 4 * x + 2 * y + c

        def remote(a, which):
            return pltpu.make_async_remote_copy(
                src_ref=outs[a].at[which], dst_ref=outs[a].at[which], send_sem=send.at[a], recv_sem=recv.at[a],
                device_id=(x, y, 1 - c), device_id_type=MESH)

        def peer_of(r):
            return tuple(1 - v if (r >> b) & 1 else v for v, b in ((x, 2), (y, 1), (c, 0)))

        def small_copy(r, slot):
            return pltpu.make_async_remote_copy(
                src_ref=small_in, dst_ref=small_out.at[slot], send_sem=s_send.at[r - 1], recv_sem=s_recv.at[r - 1],
                device_id=peer_of(r), device_id_type=MESH)

        local = pltpu.make_async_copy(small_in, small_out.at[me], loc.at[0])
        sends = [remote(a, c) for a in range(n)] + [small_copy(r, me) for r in range(1, N_DEV)]
        local.start()
        for cp in sends:
            cp.start()
        for r in range(1, N_DEV):
            px, py, pc = peer_of(r)
            small_copy(r, 4 * px + 2 * py + pc).wait_recv()
        for a in range(n):
            remote(a, 1 - c).wait_recv()
        for cp in sends:
            cp.wait_send()
        local.wait()

    outs = pl.pallas_call(
        body, name="ag_sibling", in_specs=[_ANY] * (n + 1), out_specs=[_ANY] * (n + 1),
        out_shape=[jax.ShapeDtypeStruct(b.shape, b.dtype) for b in bufs]
        + [jax.ShapeDtypeStruct((N_DEV,) + small.shape, small.dtype)],
        input_output_aliases={a: a for a in range(n)},
        scratch_shapes=[pltpu.SemaphoreType.DMA((n,)), pltpu.SemaphoreType.DMA((n,)),
                        pltpu.SemaphoreType.DMA((N_DEV - 1,)), pltpu.SemaphoreType.DMA((N_DEV - 1,)),
                        pltpu.SemaphoreType.DMA((1,))],
    )(*bufs, small)
    return [o.reshape(2 * o.shape[1], o.shape[2]) for o in outs[:n]], outs[n]


def _pair_add(full, recv, core, name):
    _, R, C = full.shape
    half = R // 2

    def body(core_ref, a_ref, b_ref, o_ref):
        o_ref[...] = (a_ref[...].astype(F32) + b_ref[...].astype(F32)).astype(BF16)

    return pl.pallas_call(
        body, name=name,
        grid_spec=pltpu.PrefetchScalarGridSpec(
            num_scalar_prefetch=1, grid=(N_CHIPS,),
            in_specs=[pl.BlockSpec((1, half, C), lambda j, core_ref: (j, core_ref[0], 0)),
                      pl.BlockSpec((1, half, C), lambda j, core_ref: (j, 0, 0))],
            out_specs=pl.BlockSpec((1, half, C), lambda j, core_ref: (j, 0, 0))),
        out_shape=jax.ShapeDtypeStruct((N_CHIPS, half, C), BF16),
        compiler_params=_params(("parallel",)),
    )(core, full, recv)


def _sum_partials(own_all, recv, place, name, tiles=2):
    _, R, C = own_all.shape
    tr = R // tiles

    def body(place_ref, own_ref, r_ref, o_ref):
        acc = own_ref[0].astype(F32)
        for k in range(3):
            acc = acc + r_ref[k].astype(F32)
        o_ref[0] = acc

    return pl.pallas_call(
        body, name=name,
        grid_spec=pltpu.PrefetchScalarGridSpec(
            num_scalar_prefetch=1, grid=(tiles,),
            in_specs=[pl.BlockSpec((1, tr, C), lambda i, place_ref: (place_ref[0], i, 0)),
                      pl.BlockSpec((3, tr, C), lambda i, place_ref: (0, i, 0))],
            out_specs=pl.BlockSpec((1, tr, C), lambda i, place_ref: (place_ref[1], i, 0))),
        out_shape=jax.ShapeDtypeStruct((2, R, C), F32),
        compiler_params=_params(("parallel",)),
    )(place, own_all, recv)


def _adamw_math(w, g, m, v):
    m2 = ADAM_B1 * m + (1.0 - ADAM_B1) * g
    v2 = ADAM_B2 * v + (1.0 - ADAM_B2) * (g * g)
    m_hat = m2 / (1.0 - ADAM_B1 ** ADAM_STEP)
    v_hat = v2 / (1.0 - ADAM_B2 ** ADAM_STEP)
    return -ADAM_LR * (m_hat / (jnp.sqrt(v_hat) + ADAM_EPS) + ADAM_WD * w), m2, v2


ROW_ATTN_G, ROW_FFN_G, ROW_FINAL_G, ROW_MISC, ROW_CONV_B, ROW_CONV_W, ROW_META, SMALL_ROWS = 0, 1, 2, 3, 4, 8, 24, 40
MISC_FOX_B, MISC_LOSS = 512, 640


def _small_pack(out):
    def rows(a, n):
        a = a.astype(F32)
        return jnp.pad(a, ((0, n - a.shape[0]), (0, D_MODEL - a.shape[1])))

    misc = jnp.concatenate([out["ret_g"], out["fox_b"], jnp.zeros((1, MISC_LOSS - MISC_FOX_B - FOX_HEADS), F32),
                            out["loss"].reshape(1, 1)], axis=1)
    conv_b = jnp.pad(out["conv_b"], ((0, 0), (0, (-D_FF) % D_MODEL))).reshape(-1, D_MODEL)
    conv_w = out["conv_w"].reshape(3, N_CHIPS, -1).transpose(1, 0, 2).reshape(3 * N_CHIPS, -1)
    return jnp.concatenate([
        rows(out["attn_g"], 1), rows(out["ffn_g"], 1), rows(out["final_g"], 1), rows(misc, 1),
        rows(conv_b, ROW_CONV_W - ROW_CONV_B), rows(conv_w, ROW_META - ROW_CONV_W), rows(out["dmeta"], N_META)], axis=0)


def _small_update(packs, chip, ws, ms, vs):
    n = len(ws)
    meta_w, conv_sw = ws[0].shape[1], ws[5].shape[2]
    assert packs.shape == (N_DEV, SMALL_ROWS, D_MODEL) and ws[0].shape[0] == N_META and ws[5].shape[:2] == (3, 1)

    def body(chip_ref, p_ref, *refs):
        w_refs, m_refs, v_refs = refs[:n], refs[n:2 * n], refs[2 * n:3 * n]
        loss_ref, out_refs, tot = refs[3 * n], refs[3 * n + 1:7 * n + 1], refs[7 * n + 1]
        acc = p_ref[0]
        for d in range(1, N_DEV):
            acc = acc + p_ref[d]
        tot[...] = acc

        def of_chip(pieces):
            val = pieces[-1]
            for j in range(N_CHIPS - 2, -1, -1):
                val = jnp.where(chip_ref[0] == j, pieces[j], val)
            return val

        row = lambda r, lo=0, hi=D_MODEL: tot[r:r + 1, lo:hi]
        grads = [
            of_chip([tot[ROW_META:ROW_META + N_META, j * meta_w:(j + 1) * meta_w] for j in range(N_CHIPS)]),
            row(ROW_ATTN_G), row(ROW_MISC, MISC_FOX_B, MISC_FOX_B + FOX_HEADS), row(ROW_MISC, 0, MISC_FOX_B),
            row(ROW_FFN_G),
            of_chip([tot[ROW_CONV_W + 3 * j:ROW_CONV_W + 3 * j + 3, 0:conv_sw] for j in range(N_CHIPS)]),
            jnp.concatenate([row(ROW_CONV_B), row(ROW_CONV_B + 1), row(ROW_CONV_B + 2, 0, D_FF - 2 * D_MODEL)], axis=1),
            row(ROW_FINAL_G)]
        loss_ref[...] = row(ROW_MISC, MISC_LOSS, MISC_LOSS + BLK)
        for k in range(n):
            parts = [((Ellipsis,), grads[k])]
            if len(ws[k].shape) == 3:
                parts = [((t,), grads[k][t:t + 1]) for t in range(ws[k].shape[0])]
            for at, g in parts:
                res = (g,) + _adamw_math(w_refs[k][at], g, m_refs[k][at], v_refs[k][at])
                for kind in range(4):
                    out_refs[kind * n + k][at] = res[kind]

    res = pl.pallas_call(
        body, name="small_update",
        grid_spec=pltpu.PrefetchScalarGridSpec(
            num_scalar_prefetch=1, grid=(1,),
            in_specs=[_full(packs.shape)] + [_full(a.shape) for a in list(ws) * 3],
            out_specs=[_full((1, BLK))] + [_full(a.shape) for a in list(ws) * 4],
            scratch_shapes=[pltpu.VMEM((SMALL_ROWS, D_MODEL), F32)]),
        out_shape=[jax.ShapeDtypeStruct((1, BLK), F32)] + [jax.ShapeDtypeStruct(a.shape, F32) for a in list(ws) * 4],
        compiler_params=_params(("arbitrary",)),
    )(chip, packs, *ws, *ms, *vs)
    return res[0], res[1:n + 1], res[n + 1:2 * n + 1], res[2 * n + 1:3 * n + 1], res[3 * n + 1:]


def _adamw(w, g, m, v, name, tiles=4):
    R, tail = w.shape[0], w.shape[1:]
    assert R % tiles == 0
    tr = R // tiles

    def body(w_ref, g_ref, m_ref, v_ref, go_ref, d_ref, m2_ref, v2_ref):
        g_ = g_ref[...]
        go_ref[...] = g_
        d_ref[...], m2_ref[...], v2_ref[...] = _adamw_math(w_ref[...], g_, m_ref[...], v_ref[...])

    spec = pl.BlockSpec((tr,) + tail, lambda i: (i,) + (0,) * len(tail))
    return pl.pallas_call(
        body, name=name, grid=(tiles,), in_specs=[spec] * 4, out_specs=[spec] * 4,
        out_shape=[jax.ShapeDtypeStruct(w.shape, F32)] * 4,
        compiler_params=_params(("parallel",)),
    )(w, g, m, v)


def _row_vector_tiles(n, most=80):
    return next(t for t in range(1, n + 1) if n % t == 0 and n // t <= most)


def _pack_rows(pieces, rows):
    flat = jnp.concatenate([jnp.pad(p.reshape(-1).astype(F32), (0, (-p.size) % D_MODEL)) for p in pieces])
    return jnp.pad(flat, (0, rows * D_MODEL - flat.size)).reshape(rows, D_MODEL)


def _unpack_rows(pack, shapes):
    flat = pack.reshape(-1)
    out, off = [], 0
    for shp in shapes:
        size = int(np.prod(shp))
        out.append(flat[off:off + size].reshape(shp))
        off += size + (-size) % D_MODEL
    return out


IN_PADDED = IN_WIDTH + (-IN_WIDTH) % BLK


def _fox_column_blocks():
    return [(RET_W + part * 512 + p * BLK, RET_W + 384 * p + part * BLK)
            for part in range(3) for p in range(FOX_HEADS // 2)]


def _w_in_kernel_order(gathered, own, chip):
    n, R, C = gathered.shape
    tr = R // 4

    def body(chip_ref, g_ref, own_ref, wm_ref, wf_ref, full):
        for j in range(n):
            @pl.when(chip_ref[0] == j)
            def _(j=j):
                full[:, j * C:(j + 1) * C] = own_ref[...]

            @pl.when(chip_ref[0] != j)
            def _(j=j):
                full[:, j * C:(j + 1) * C] = g_ref[j]

        full[:, n * C:] = jnp.zeros((tr, IN_PADDED - n * C), BF16)
        wm_ref[:, 0:RET_W] = full[:, 0:RET_W]
        for src, dst in _fox_column_blocks():
            wm_ref[:, dst:dst + BLK] = full[:, src:src + BLK]
        wf_ref[...] = full[:, MAIN_W:MAIN_W + BLK]

    return pl.pallas_call(
        body, name="w_in_kernel_order",
        grid_spec=pltpu.PrefetchScalarGridSpec(
            num_scalar_prefetch=1, grid=(R // tr,),
            in_specs=[pl.BlockSpec((n, tr, C), lambda i, c: (0, i, 0)), pl.BlockSpec((tr, C), lambda i, c: (i, 0))],
            out_specs=[pl.BlockSpec((tr, MAIN_W), lambda i, c: (i, 0)), pl.BlockSpec((tr, BLK), lambda i, c: (i, 0))],
            scratch_shapes=[pltpu.VMEM((tr, IN_PADDED), BF16)]),
        out_shape=[jax.ShapeDtypeStruct((R, MAIN_W), BF16), jax.ShapeDtypeStruct((R, BLK), BF16)],
        compiler_params=_params(("arbitrary",)),
    )(chip, gathered, own)


def _w_in_grad_shards(g_ret, g_fox, g_ff):
    R = g_ret.shape[0]
    C = IN_WIDTH // N_CHIPS
    tr = R // 4

    def body(gr_ref, gx_ref, gf_ref, o_ref, full):
        full[:, 0:RET_W] = gr_ref[...]
        for src, dst in _fox_column_blocks():
            full[:, src:src + BLK] = gx_ref[:, dst - RET_W:dst - RET_W + BLK]
        full[:, MAIN_W:MAIN_W + FOX_HEADS] = gf_ref[...]
        for j in range(N_CHIPS):
            o_ref[j] = full[:, j * C:(j + 1) * C].astype(BF16)

    rows = lambda w: pl.BlockSpec((tr, w), lambda i: (i, 0))
    return pl.pallas_call(
        body, name="w_in_grad_shards", grid=(R // tr,),
        in_specs=[rows(RET_W), rows(FOX_W), rows(FOX_HEADS)],
        out_specs=pl.BlockSpec((N_CHIPS, tr, C), lambda i: (0, i, 0)),
        out_shape=jax.ShapeDtypeStruct((N_CHIPS, R, C), BF16),
        scratch_shapes=[pltpu.VMEM((tr, IN_PADDED), F32)],
        compiler_params=_params(("parallel",)),
    )(g_ret, g_fox, g_ff)


def kernel(x, meta_tokens, attn_norm_g, w_in, fox_forget_b, ret_norm_g, w_out, ffn_norm_g, w_up, conv_w, conv_b, w_down, final_norm_g, loss_target, m_meta_tokens, m_attn_norm_g, m_w_in, m_fox_forget_b, m_ret_norm_g, m_w_out, m_ffn_norm_g, m_w_up, m_conv_w, m_conv_b, m_w_down, m_final_norm_g, v_meta_tokens, v_attn_norm_g, v_w_in, v_fox_forget_b, v_ret_norm_g, v_w_out, v_ffn_norm_g, v_w_up, v_conv_w, v_conv_b, v_w_down, v_final_norm_g):
    chip = 2 * lax.axis_index("x") + lax.axis_index("y")
    core = lax.axis_index("c")

    small_w = _pack_rows([meta_tokens, conv_w[0]], 8)
    w_in_b = w_in[0].astype(BF16)
    g_in, g_small = _chip_allgather_halves(w_in_b, small_w)
    chip_idx = chip.reshape(1).astype(jnp.int32)
    w_main, w_ff = _w_in_kernel_order(g_in.reshape((N_CHIPS,) + w_in_b.shape), w_in_b, chip_idx)
    small_parts = [_unpack_rows(g_small[j], [meta_tokens.shape, conv_w.shape[1:]]) for j in range(N_CHIPS)]
    meta_full = jnp.concatenate([sp[0] for sp in small_parts], axis=1)
    conv_w_full = jnp.concatenate([sp[1] for sp in small_parts], axis=1)

    core_idx = core.reshape(1).astype(jnp.int32)
    place = jnp.stack([chip, core]).astype(jnp.int32)

    def assemble(gathered):
        g_out, g_up, g_down = gathered
        return g_out.reshape(D_MODEL, D_MODEL), g_up, g_down.reshape(D_FF, D_MODEL)

    def early_arrays(d_w_out, d_w_up, d_w_down):
        return [d_w_out.reshape(N_CHIPS, -1, D_MODEL), d_w_up, d_w_down.reshape(N_CHIPS, -1, D_MODEL)]

    def in_sums(d_w_ret, d_w_fox, d_w_ff):
        g_in_full = _w_in_grad_shards(d_w_ret, d_w_fox, d_w_ff)
        (from_sib,) = _sibling_halves([g_in_full])
        return [_pair_add(g_in_full, from_sib, core_idx, "pair_add_in")]

    def early_sums(early, from_sib):
        return [_pair_add(g, r, core_idx, "pair_add_" + nm) for g, r, nm in zip(early, from_sib, ("out", "up", "down"))]

    out = _local_step(x[0], loss_target[0], meta_full, attn_norm_g, w_main, w_ff, fox_forget_b, ret_norm_g,
                      None, ffn_norm_g, None, conv_w_full, conv_b, None, final_norm_g[None],
                      late=([w_out[0].astype(BF16), w_up[0].astype(BF16), w_down[0].astype(BF16)], assemble),
                      mid=(early_arrays, early_sums), last=in_sums, wire=BF16)

    names = ("in", "out", "up", "down")
    totals = [_sum_partials(s, q, place, "sum_chips_" + nm) for s, q, nm in zip(out["scatter"], out["received"], names)]
    (grad_in, grad_out, grad_up, grad_down), small_all = _sibling_allgather(totals, _small_pack(out))

    big_w = [(w_out, m_w_out, v_w_out, grad_out, "adamw_out"), (w_up, m_w_up, v_w_up, grad_up, "adamw_up"),
             (w_down, m_w_down, v_w_down, grad_down, "adamw_down")]
    big_res = [[r[None] for r in _adamw(w[0], g, m[0], v[0], nm)] for w, m, v, g, nm in big_w]
    as_rows = lambda a: jnp.transpose(a, (2, 0, 1))
    in_rows = _adamw(as_rows(w_in), grad_in.T[:, None, :], as_rows(m_w_in), as_rows(v_w_in), "adamw_in",
                     tiles=_row_vector_tiles(w_in.shape[2]))
    big_res.insert(0, [jnp.transpose(r, (1, 2, 0)) for r in in_rows])
    tap_rows = lambda a: jnp.transpose(a, (1, 0, 2))
    small_p = [meta_tokens, attn_norm_g, fox_forget_b, ret_norm_g, ffn_norm_g, tap_rows(conv_w), conv_b, final_norm_g[None]]
    small_m = [m_meta_tokens, m_attn_norm_g, m_fox_forget_b, m_ret_norm_g, m_ffn_norm_g, tap_rows(m_conv_w), m_conv_b,
               m_final_norm_g[None]]
    small_v = [v_meta_tokens, v_attn_norm_g, v_fox_forget_b, v_ret_norm_g, v_ffn_norm_g, tap_rows(v_conv_w), v_conv_b,
               v_final_norm_g[None]]
    loss_row, *small_res = _small_update(small_all, chip_idx, small_p, small_m, small_v)
    loss = loss_row[0, 0]

    def ordered(kind):
        sm = list(small_res[kind][:-1]) + [small_res[kind][-1][0]]
        sm[5] = tap_rows(sm[5])
        bg = [r[kind] for r in big_res]
        return [sm[0], sm[1], bg[0], sm[2], sm[3], bg[1], sm[4], bg[2], sm[5], sm[6], bg[3], sm[7]]

    return (loss, out["dx"][None], *ordered(0), *ordered(1), *ordered(2), *ordered(3))
```

```python
import functools

import numpy as np
import jax
import jax.numpy as jnp
from jax import lax
from jax.experimental import pallas as pl
from jax.experimental.pallas import tpu as pltpu

F32 = jnp.float32
BF16 = jnp.bfloat16

D_MODEL = 1024
N_META = 16
BLK = 128
UNIT = 2 * BLK
FOX_PAIRS = 2
WIDE = 4
CHUNK = 64
N_PAD = BLK - N_META
PREFIX = BLK
RET_HEADS = 4
FOX_HEADS = 8
HEAD_LANES = 64
D_FF = 2816
ROPE_BASE = 10000.0
EPS = 1e-6
NEG = -1e30
LOG2E = 1.4426950408889634
RET_W = 1536
FOX_W = 1536
MAIN_W = RET_W + FOX_W
IN_WIDTH = MAIN_W + FOX_HEADS
N_CHIPS = 4
N_DEV = 8

ADAM_LR = 0.001
ADAM_B1 = 0.9
ADAM_B2 = 0.999
ADAM_EPS = 1e-08
ADAM_WD = 0.01
ADAM_STEP = 10

MESH = pl.DeviceIdType.MESH
VMEM_LIMIT_MB = 56

_NT = (((1,), (1,)), ((), ()))
_TN = (((0,), (0,)), ((), ()))


def _dot(a, b):
    return jnp.dot(a, b, preferred_element_type=F32)


def _dot_nt(a, b):
    return lax.dot_general(a, b, _NT, preferred_element_type=F32)


def _dot_tn(a, b):
    return lax.dot_general(a, b, _TN, preferred_element_type=F32)


def _params(dims=None, vmem_mb=VMEM_LIMIT_MB):
    kw = dict(vmem_limit_bytes=vmem_mb << 20)
    if dims is not None:
        kw["dimension_semantics"] = dims
    return pltpu.CompilerParams(**kw)


def _row_tile(n, prefs=(384, 256, 128)):
    for t in prefs:
        if n % t == 0:
            return t
    raise ValueError(f"no row tile for {n}")


def _iota(shape, dim):
    return lax.broadcasted_iota(jnp.int32, shape, dim)


def _pick_row(tile, row):
    sub = _iota(tile.shape, 0)
    return jnp.sum(jnp.where(sub == row, tile, 0.0), axis=0, keepdims=True)


def _split3(x):
    hi = x.astype(BF16)
    r1 = x - hi.astype(F32)
    mid = r1.astype(BF16)
    lo = (r1 - mid.astype(F32)).astype(BF16)
    return hi, mid, lo


def _full(shape):
    nd = len(shape)
    return pl.BlockSpec(shape, lambda *_: (0,) * nd)


def _in_perm():
    cols = list(range(RET_W))
    for p in range(FOX_HEADS // 2):
        for part in range(3):
            start = RET_W + part * 512 + p * BLK
            cols += list(range(start, start + BLK))
    return np.asarray(cols, np.int32)


def _rotary_tables(L):
    half = HEAD_LANES // 2
    inv = 1.0 / (ROPE_BASE ** (jnp.arange(half, dtype=F32) / half))
    ang = jnp.arange(L).astype(F32)[:, None] * inv[None, :]
    cos, sin = jnp.cos(ang), jnp.sin(ang)
    cos_t = jnp.tile(cos, (1, 4))
    sin_t = jnp.tile(jnp.concatenate([-sin, sin], axis=1), (1, 2))
    return cos_t, sin_t


def _decay_tables():
    gam = 1.0 - 2.0 ** (-5.0 - np.arange(RET_HEADS, dtype=np.float64))
    n = np.arange(BLK)
    same_or_past = (n[:, None] // CHUNK) >= (n[None, :] // CHUNK)
    dist = np.abs(n[:, None] - n[None, :])
    dmat = np.stack([np.where(same_or_past, g ** dist, 0.0) for g in gam]).astype(np.float32)
    lane_head = np.arange(BLK) // HEAD_LANES
    wq = np.stack([gam[2 * p + lane_head][None, :] ** (n[:, None] + 1.0) for p in range(2)]).astype(np.float32)
    wk = np.stack([gam[2 * p + lane_head][None, :] ** (BLK - 1.0 - n[:, None]) for p in range(2)]).astype(np.float32)
    g_blk = tuple(float(g ** BLK) for g in gam)
    return jnp.asarray(dmat), jnp.asarray(wq), jnp.asarray(wk), g_blk


def _shifted_blocks(tm):
    nb = tm // BLK
    return [pl.BlockSpec((BLK, D_MODEL), lambda i, j=j: (jnp.maximum(nb * i + j - 1, 0), 0)) for j in range(nb)]


def _rms_inproj(head, x, g, w_main, w_ff):
    L = x.shape[0] + BLK
    tm = _row_tile(L)
    nb = tm // BLK

    def body(head_ref, *refs):
        x_refs, (g_ref, wm_ref, wf_ref, h_ref, n_ref, p_ref, ff_ref) = refs[:nb], refs[nb:]
        parts = [r[...] for r in x_refs]
        parts[0] = jnp.where(pl.program_id(0) == 0, head_ref[...], parts[0])
        h = jnp.concatenate(parts, axis=0)
        h_ref[...] = h
        r = lax.rsqrt(jnp.mean(h * h, axis=-1, keepdims=True) + EPS)
        n = (h * r * g_ref[...]).astype(BF16)
        n_ref[...] = n
        p_ref[...] = _dot(n, wm_ref[...]).astype(BF16)
        ff_ref[...] = _dot(n, wf_ref[...])

    rows = lambda w: pl.BlockSpec((tm, w), lambda i: (i, 0))
    return pl.pallas_call(
        body, name="f_inproj", grid=(L // tm,),
        in_specs=[_full((BLK, D_MODEL))] + _shifted_blocks(tm)
        + [_full((1, D_MODEL)), _full((D_MODEL, MAIN_W)), _full((D_MODEL, BLK))],
        out_specs=[rows(D_MODEL), rows(D_MODEL), rows(MAIN_W), rows(BLK)],
        out_shape=[jax.ShapeDtypeStruct((L, D_MODEL), F32), jax.ShapeDtypeStruct((L, D_MODEL), BF16),
                   jax.ShapeDtypeStruct((L, MAIN_W), BF16), jax.ShapeDtypeStruct((L, BLK), F32)],
        compiler_params=_params(("parallel",)),
    )(head, *([x] * nb), g, w_main, w_ff)


SMALL_GROUP = 11


def _block_group(nblk, most=3):
    return next(g for g in range(most, 0, -1) if nblk % g == 0)


def _fox_prep(ff, fb):
    L = ff.shape[0]
    nblk = L // BLK
    G = _block_group(nblk, SMALL_GROUP)

    def body(ff_ref, b_ref, c_ref, ct_ref, carry):
        @pl.when(pl.program_id(0) == 0)
        def _():
            carry[...] = jnp.zeros_like(carry)

        tri = (_iota((BLK, BLK), 0) >= _iota((BLK, BLK), 1)).astype(BF16)
        live = _iota((BLK, BLK), 1) < FOX_HEADS
        run = carry[...]
        for b in range(G):
            z = ff_ref[b * BLK:(b + 1) * BLK, :] + b_ref[...]
            lf = jnp.where(live, jnp.minimum(z, 0.0) - jnp.log1p(jnp.exp(-jnp.abs(z))), 0.0)
            hi, mid, lo = _split3(lf)
            cs = (_dot(tri, hi) + _dot(tri, mid) + _dot(tri, lo) + run) * LOG2E
            c_ref[b * BLK:(b + 1) * BLK, :] = cs
            ct_ref[b] = cs.T[0:8, :]
            run = run + jnp.sum(lf, axis=0, keepdims=True)
        carry[...] = run

    return pl.pallas_call(
        body, name="f_foxprep", grid=(nblk // G,),
        in_specs=[pl.BlockSpec((G * BLK, BLK), lambda i: (i, 0)), _full((1, BLK))],
        out_specs=[pl.BlockSpec((G * BLK, BLK), lambda i: (i, 0)), pl.BlockSpec((G, 8, BLK), lambda i: (i, 0, 0))],
        out_shape=[jax.ShapeDtypeStruct((L, BLK), F32), jax.ShapeDtypeStruct((nblk, 8, BLK), F32)],
        scratch_shapes=[pltpu.VMEM((1, BLK), F32)],
        compiler_params=_params(("arbitrary",)),
    )(ff, fb)


def _rot_fns(cos, sin):
    lane = _iota((BLK, BLK), 1)
    first = (lane & (HEAD_LANES - 1)) < HEAD_LANES // 2

    def swap(x):
        return jnp.where(first, pltpu.roll(x, BLK - 32, 1), pltpu.roll(x, 32, 1))

    def rot(x):
        return x * cos + swap(x) * sin

    def rot_t(dy):
        return dy * cos + swap(dy * sin)

    return rot, rot_t


def _retention_fwd(proj, cos_t, sin_t, ret_g):
    L = proj.shape[0]
    nblk = L // BLK
    G = _block_group(nblk)
    dmat, wq_t, wk_t, g_blk = _decay_tables()

    def body(q_ref, k_ref, v_ref, gate_ref, cos_ref, sin_ref, d_ref, wq_ref, wk_ref, rg_ref,
             mix_ref, o_ref, rs_ref, state):
        @pl.when(pl.program_id(0) == 0)
        def _():
            state[...] = jnp.zeros_like(state)

        lane = _iota((BLK, BLK), 1)
        sub = _iota((BLK, BLK), 0)
        for b in range(G):
            rows = slice(b * BLK, (b + 1) * BLK)
            rot, _ = _rot_fns(cos_ref[rows, :], sin_ref[rows, :])
            for p in range(2):
                qr = rot(q_ref[rows, p * BLK:(p + 1) * BLK].astype(F32))
                kr = rot(k_ref[rows, p * BLK:(p + 1) * BLK].astype(F32)) * (HEAD_LANES ** -0.5)
                kr_b = kr.astype(BF16)
                qw = (qr * wq_ref[p]).astype(BF16)
                kw = (kr * wk_ref[p]).astype(BF16)
                for e in range(2):
                    h = 2 * p + e
                    cols = slice(h * BLK, (h + 1) * BLK)
                    qm = jnp.where((lane >> 6) == e, qr, 0.0).astype(BF16)
                    s = _dot_nt(qm, kr_b) * d_ref[h]
                    vh = v_ref[rows, cols]
                    st = state[h]
                    rs_ref[b, h] = st
                    o = _dot(s.astype(BF16), vh) + _dot(qw, st.astype(BF16))
                    u = jnp.where((sub >> 6) == e, _dot_tn(kw, vh), 0.0)
                    state[h] = g_blk[h] * st + u
                    rn = lax.rsqrt(jnp.mean(o * o, axis=-1, keepdims=True) + EPS)
                    gate = gate_ref[rows, cols].astype(F32)
                    o_ref[rows, cols] = o
                    mix_ref[rows, cols] = (o * rn * rg_ref[:, cols] * (gate * jax.nn.sigmoid(gate))).astype(BF16)

    row = lambda c: (lambda i: (i, c))
    return pl.pallas_call(
        body, name="f_retention", grid=(nblk // G,),
        in_specs=[pl.BlockSpec((G * BLK, 256), row(0)), pl.BlockSpec((G * BLK, 256), row(1)),
                  pl.BlockSpec((G * BLK, 512), row(1)), pl.BlockSpec((G * BLK, 512), row(2)),
                  pl.BlockSpec((G * BLK, BLK), row(0)), pl.BlockSpec((G * BLK, BLK), row(0)),
                  _full((RET_HEADS, BLK, BLK)), _full((2, BLK, BLK)), _full((2, BLK, BLK)), _full((1, 512))],
        out_specs=[pl.BlockSpec((G * BLK, 512), row(0)), pl.BlockSpec((G * BLK, 512), row(0)),
                   pl.BlockSpec((G, RET_HEADS, BLK, BLK), lambda i: (i, 0, 0, 0))],
        out_shape=[jax.ShapeDtypeStruct((L, 512), BF16), jax.ShapeDtypeStruct((L, 512), F32),
                   jax.ShapeDtypeStruct((nblk, RET_HEADS, BLK, BLK), F32)],
        scratch_shapes=[pltpu.VMEM((RET_HEADS, BLK, BLK), F32)],
        compiler_params=_params(("arbitrary",)),
    )(proj, proj, proj, proj, cos_t, sin_t, dmat, wq_t, wk_t, ret_g)


def _fox_units(L):
    nblk = L // BLK
    assert L % BLK == 0 and nblk % 2 == 1, "sequence must be one 128-row block plus whole 256-row tiles"
    return nblk, (nblk - 1) // 2


def _fox_tile_masks():
    sub, lane = _iota((BLK, BLK), 0), _iota((BLK, BLK), 1)
    valid = _iota((BLK, UNIT), 0) >= N_PAD
    diag = _iota((UNIT, UNIT), 0) <= _iota((UNIT, UNIT), 1)
    r, q = _iota((BLK + UNIT, UNIT), 0), _iota((BLK + UNIT, UNIT), 1)
    first_and_diag = ((r < BLK) & (r >= N_PAD)) | ((r >= BLK) & (r - BLK <= q))
    return dict(first=(sub <= lane) & (sub >= N_PAD), valid=valid, diag=diag, first_and_diag=first_and_diag)


def _fox_fwd(proj, c, ctb, gather=()):
    L = proj.shape[0]
    nblk, nu = _fox_units(L)
    scale = HEAD_LANES ** -0.5 * LOG2E
    ng = len(gather)
    FOX_PAIRS = 4
    steps = FOX_HEADS // (2 * FOX_PAIRS)

    def body(qkv_ref, c_ref, ct_ref, *rest):
        g_in, (of_ref, lse_ref), g_out = rest[:ng], rest[ng:ng + 2], rest[ng + 2:2 * ng + 2]
        vt, csb = rest[2 * ng + 2:2 * ng + 4]
        p = pl.program_id(0)
        heads = [(pp, e, 2 * FOX_PAIRS * p + 2 * pp + e) for pp in range(FOX_PAIRS) for e in range(2)]

        @pl.when(p == 0)
        def _():
            lse_ref[...] = jnp.zeros_like(lse_ref)
            if ng:
                local, sends, _ = _allgather_copies(g_in, g_out, *rest[2 * ng + 4:])
                for cp in local + sends:
                    cp.start()

        lane = _iota((BLK, BLK), 1)
        sub8 = _iota((8, BLK), 0)
        masks = _fox_tile_masks()

        def pre(j, carry):
            off = pl.multiple_of(j * BLK, BLK)
            ct = c_ref[pl.ds(off, BLK), :]
            for pp in range(FOX_PAIRS):
                vt[pp, j] = qkv_ref[pl.ds(off, BLK), pp * 384 + 2 * BLK:pp * 384 + 3 * BLK].astype(F32).T.astype(BF16)
            for hh, (_, _, h) in enumerate(heads):
                col = jnp.sum(jnp.where(lane == h, ct, 0.0), axis=1, keepdims=True)
                csb[hh, j] = jnp.broadcast_to(col, (BLK, BLK))
            return carry

        lax.fori_loop(0, nblk, pre, 0)

        def attend(qblk, nq, n_whole):
            qlen = nq * BLK
            qoff = pl.multiple_of(qblk * BLK, BLK)
            qlane = _iota((qlen, BLK), 1)
            qs = [qkv_ref[pl.ds(qoff, qlen), pp * 384:pp * 384 + BLK].astype(F32) * scale for pp in range(FOX_PAIRS)]
            qm = [jnp.where((qlane >> 6) == e, qs[pp], 0.0).astype(BF16) for pp, e, _ in heads]
            ct_row = [jnp.concatenate([_pick_row(ct_ref[qblk + a], h) for a in range(nq)], axis=1) for _, _, h in heads]

            def step(segs, mask, st):
                blocks = [kblk + b for kblk, nk in segs for b in range(nk)]
                kts = []
                for pp in range(FOX_PAIRS):
                    kt = [qkv_ref[pl.ds(pl.multiple_of(kblk * BLK, BLK), nk * BLK), pp * 384 + BLK:pp * 384 + 2 * BLK]
                          for kblk, nk in segs]
                    kts.append(kt[0] if len(kt) == 1 else jnp.concatenate(kt, axis=0))
                out = []
                for hh, (pp, e, _) in enumerate(heads):
                    m, l, acc = st[3 * hh:3 * hh + 3]
                    s = _dot_nt(kts[pp], qm[hh])
                    t = jnp.concatenate([s[b * BLK:(b + 1) * BLK] - jnp.concatenate([csb[hh, blk]] * nq, axis=1)
                                         for b, blk in enumerate(blocks)], axis=0)
                    if mask is not None:
                        t = jnp.where(mask, t, NEG)
                    m_new = jnp.maximum(m, jnp.max(t, axis=0, keepdims=True) + ct_row[hh])
                    alpha = jnp.exp2(m - m_new)
                    pr = jnp.exp2(t - (m_new - ct_row[hh]))
                    l = alpha * l + jnp.sum(pr, axis=0, keepdims=True)
                    pr_b = pr.astype(BF16)
                    pv = None
                    for b, blk in enumerate(blocks):
                        part = _dot(vt[pp, blk, e * HEAD_LANES:(e + 1) * HEAD_LANES, :], pr_b[b * BLK:(b + 1) * BLK])
                        pv = part if pv is None else pv + part
                    out += [m_new, l, alpha * acc + pv]
                return tuple(out)

            st = (jnp.full((1, qlen), NEG, F32), jnp.zeros((1, qlen), F32),
                  jnp.zeros((HEAD_LANES, qlen), F32)) * len(heads)
            if nq == 1:
                st = step([(0, 1)], masks["first"], st)
            else:
                st = step([(0, 1), (qblk, 2)], masks["first_and_diag"], st)
                n_wide = n_whole // WIDE
                st = lax.fori_loop(0, n_wide, lambda j, s_: step([(1 + 2 * WIDE * j, 2 * WIDE)], None, s_), st)
                rest = 1 + 2 * WIDE * n_wide
                st = lax.cond((n_whole & 2) != 0, lambda s_: step([(rest, 4)], None, s_), lambda s_: s_, st)
                st = lax.cond((n_whole & 1) != 0, lambda s_: step([(rest + 2 * (n_whole & 2), 2)], None, s_),
                              lambda s_: s_, st)
            for pp in range(FOX_PAIRS):
                lo, hi = st[6 * pp:6 * pp + 3], st[6 * pp + 3:6 * pp + 6]
                o_t = jnp.concatenate([lo[2] * (1.0 / lo[1]), hi[2] * (1.0 / hi[1])], axis=0)
                of_ref[pl.ds(qoff, qlen), pp * BLK:(pp + 1) * BLK] = o_t.T.astype(BF16)
            lse = [st[3 * hh] + jnp.log(st[3 * hh + 1]) * LOG2E for hh in range(len(heads))]
            for a in range(nq):
                upd = jnp.zeros((8, BLK), F32)
                for hh, (_, _, h) in enumerate(heads):
                    upd = upd + jnp.where(sub8 == h, lse[hh][:, a * BLK:(a + 1) * BLK], 0.0)
                lse_ref[qblk + a] = lse_ref[qblk + a] + upd

        attend(0, 1, 0)

        def q_loop(u, carry):
            attend(1 + 2 * u, 2, u)
            return carry

        lax.fori_loop(0, nu, q_loop, 0)

        if ng:
            @pl.when(p == steps - 1)
            def _():
                local, sends, recvs = _allgather_copies(g_in, g_out, *rest[2 * ng + 4:])
                for cp in recvs:
                    cp.wait_recv()
                for cp in sends:
                    cp.wait_send()
                for cp in local:
                    cp.wait()

    width = 384 * FOX_PAIRS
    return pl.pallas_call(
        body, name="f_fox", grid=(steps,),
        in_specs=[pl.BlockSpec((L, width), lambda p: (0, RET_W // width + p), pipeline_mode=pl.Buffered(1)),
                  _full((L, BLK)), _full((nblk, 8, BLK))]
        + [_ANY] * ng,
        out_specs=[pl.BlockSpec((L, FOX_PAIRS * BLK), lambda p: (0, p)), _full((nblk, 8, BLK))] + [_ANY] * ng,
        out_shape=[jax.ShapeDtypeStruct((L, 512), BF16), jax.ShapeDtypeStruct((nblk, 8, BLK), F32)]
        + [jax.ShapeDtypeStruct((N_CHIPS,) + a.shape, a.dtype) for a in gather],
        scratch_shapes=[pltpu.VMEM((FOX_PAIRS, nblk, BLK, BLK), BF16), pltpu.VMEM((2 * FOX_PAIRS, nblk, BLK, BLK), F32)]
        + _allgather_semaphores(ng),
        compiler_params=_params(("arbitrary",)),
    )(proj, c, ctb, *gather)


def _outproj_up(mix_r, o_f, h0, w_out, ffn_g, w_up, conv_w, conv_b):
    L = h0.shape[0]
    tm = _row_tile(L)
    shard = w_up.shape[2]
    assert 2 * shard == D_FF
    cw = [conv_w[j:j + 1] for j in range(3)]
    resident = lambda shape: pl.BlockSpec(shape, lambda i: (0,) * len(shape), pipeline_mode=pl.Buffered(1))

    def body(mr_ref, of_ref, h0_ref, wo_ref, g_ref, wu_ref, cw0, cw1, cw2, cb_ref,
             h1_ref, n2_ref, up_ref, act_ref, acc_ref, halo):
        i = pl.program_id(0)

        @pl.when(i == 0)
        def _():
            halo[...] = jnp.zeros_like(halo)

        h1 = h0_ref[...] + _dot(mr_ref[...], wo_ref[0:512, :]) + _dot(of_ref[...], wo_ref[512:1024, :])
        h1_ref[...] = h1
        r = lax.rsqrt(jnp.mean(h1 * h1, axis=-1, keepdims=True) + EPS)
        n2 = (h1 * r * g_ref[...]).astype(BF16)
        n2_ref[...] = n2
        live = i * tm + _iota((tm, 1), 0) >= N_PAD
        for half in range(2):
            cols = slice(half * shard, (half + 1) * shard)
            a_b = _dot(n2, wu_ref[half]).astype(BF16)
            b_b = _dot(n2, wu_ref[2 + half]).astype(BF16)
            up_ref[:, cols] = a_b
            up_ref[:, D_FF + half * shard:D_FF + (half + 1) * shard] = b_b
            a = jnp.where(live, a_b.astype(F32), 0.0)
            _, _, acc = _conv_taps(a, halo[:, cols], [cw0[:, cols], cw1[:, cols], cw2[:, cols]], cb_ref[:, cols])
            act_ref[:, cols] = (acc * jax.nn.sigmoid(acc) * b_b.astype(F32)).astype(BF16)
            acc_ref[:, cols] = acc.astype(BF16)
            halo[:, cols] = a[tm - 8:tm, :]

    rows = lambda w: pl.BlockSpec((tm, w), lambda i: (i, 0))
    return pl.pallas_call(
        body, name="f_outproj_up", grid=(L // tm,),
        in_specs=[rows(512), rows(512), rows(D_MODEL), resident((D_MODEL, D_MODEL)), _full((1, D_MODEL)),
                  resident((N_CHIPS, D_MODEL, shard)), _full((1, D_FF)), _full((1, D_FF)), _full((1, D_FF)),
                  _full((1, D_FF))],
        out_specs=[rows(D_MODEL), rows(D_MODEL), rows(2 * D_FF), rows(D_FF), rows(D_FF)],
        out_shape=[jax.ShapeDtypeStruct((L, D_MODEL), F32), jax.ShapeDtypeStruct((L, D_MODEL), BF16),
                   jax.ShapeDtypeStruct((L, 2 * D_FF), BF16), jax.ShapeDtypeStruct((L, D_FF), BF16),
                   jax.ShapeDtypeStruct((L, D_FF), BF16)],
        scratch_shapes=[pltpu.VMEM((8, D_FF), F32)],
        compiler_params=_params(("arbitrary",)),
    )(mix_r, o_f, h0, w_out, ffn_g, w_up, cw[0], cw[1], cw[2], conv_b)


def _conv_taps(a, halo, cw, cb):
    sub = _iota((a.shape[0], 1), 0)
    a1 = jnp.where(sub == 0, _pick_row(halo, 7), pltpu.roll(a, 1, 0))
    a2 = jnp.where(sub == 0, _pick_row(halo, 6), jnp.where(sub == 1, _pick_row(halo, 7), pltpu.roll(a, 2, 0)))
    acc = cb + a2 * cw[0]
    acc = acc + a1 * cw[1]
    acc = acc + a * cw[2]
    return a1, a2, acc


def _ffn_down_loss(g_act, w_down, h1, final_g, target, acc_saved, up):
    L = h1.shape[0]
    tm = _row_tile(L)
    nb = tm // BLK
    half_w = D_FF // 2

    def body(g_ref, wd_ref, h1_ref, gf_ref, acc_ref, b_ref, *refs):
        t_refs, (dh_ref, dhb_ref, dgf_ref, loss_ref, dacc_ref, db_ref) = refs[:nb], refs[nb:]
        i = pl.program_id(0)

        @pl.when(i == 0)
        def _():
            dgf_ref[...] = jnp.zeros_like(dgf_ref)
            loss_ref[...] = jnp.zeros_like(loss_ref)

        h2 = h1_ref[...] + _dot(g_ref[...], wd_ref[...])
        r = lax.rsqrt(jnp.mean(h2 * h2, axis=-1, keepdims=True) + EPS)
        yn = h2 * r
        gf = gf_ref[...]
        live = i * tm + _iota((tm, 1), 0) >= PREFIX
        target = jnp.concatenate([t[...] for t in t_refs], axis=0)
        err = jnp.where(live, yn * gf - target, 0.0)
        loss_ref[...] = loss_ref[...] + 0.5 * jnp.sum(jnp.mean(err * err, axis=-1, keepdims=True))
        dy = err * (1.0 / D_MODEL)
        dgf_ref[...] = dgf_ref[...] + jnp.sum(dy * yn, axis=0, keepdims=True)
        dyn = dy * gf
        dh = r * (dyn - yn * jnp.mean(dyn * yn, axis=-1, keepdims=True))
        dh_ref[...] = dh
        dhb = dh.astype(BF16)
        dhb_ref[...] = dhb
        for half in range(2):
            cols = slice(half * half_w, (half + 1) * half_w)
            acc = acc_ref[:, cols].astype(F32)
            dg = _dot_nt(dhb, wd_ref[cols, :])
            sg = jax.nn.sigmoid(acc)
            silu = acc * sg
            db_ref[:, cols] = (dg * silu).astype(BF16)
            dacc_ref[:, cols] = (dg * b_ref[:, cols].astype(F32) * (sg + silu * (1.0 - sg))).astype(BF16)

    rows = lambda w, c=0: pl.BlockSpec((tm, w), lambda i: (i, c))
    return pl.pallas_call(
        body, name="f_ffn_down_loss", grid=(L // tm,),
        in_specs=[rows(D_FF), pl.BlockSpec((D_FF, D_MODEL), lambda i: (0, 0), pipeline_mode=pl.Buffered(1)),
                  rows(D_MODEL), _full((1, D_MODEL)), rows(D_FF), rows(D_FF, 1)] + _shifted_blocks(tm),
        out_specs=[rows(D_MODEL), rows(D_MODEL), _full((1, D_MODEL)), _full((1, BLK)), rows(D_FF), rows(D_FF)],
        out_shape=[jax.ShapeDtypeStruct((L, D_MODEL), F32), jax.ShapeDtypeStruct((L, D_MODEL), BF16),
                   jax.ShapeDtypeStruct((1, D_MODEL), F32), jax.ShapeDtypeStruct((1, BLK), F32),
                   jax.ShapeDtypeStruct((L, D_FF), BF16), jax.ShapeDtypeStruct((L, D_FF), BF16)],
        compiler_params=_params(("arbitrary",)),
    )(g_act, w_down, h1, final_g, acc_saved, up, *([target] * nb))


def _ffn_bwd_up(dacc, db, up, conv_w, w_up, h1, ffn_g, dh2, w_out):
    L = h1.shape[0]
    tm = _row_tile(L)
    nt = L // tm
    shard = w_up.shape[2]
    cw = [conv_w[j:j + 1] for j in range(3)]

    def body(da_ref, halo_ref, db_ref, a_ref, cw0, cw1, cw2, wu_ref, h1_ref, g_ref, dh2_ref, wo_ref,
             dup_ref, dh1_ref, dh1b_ref, dmix_ref, dg_ref, dcw_ref):
        i = pl.program_id(0)

        @pl.when(i == 0)
        def _():
            dg_ref[...] = jnp.zeros_like(dg_ref)
            dcw_ref[...] = jnp.zeros_like(dcw_ref)

        sub = _iota((tm, 1), 0)
        sub8 = _iota((8, 1), 0)
        last_tile = i == nt - 1
        dbv = db_ref[...]
        dup_ref[:, D_FF:2 * D_FF] = dbv
        dn = _dot_nt(dbv[:, 0:shard], wu_ref[2]) + _dot_nt(dbv[:, shard:2 * shard], wu_ref[3])
        for half in range(2):
            cols = slice(half * shard, (half + 1) * shard)
            d0 = da_ref[:, cols].astype(F32)
            halo = jnp.where(last_tile, 0.0, halo_ref[:, cols].astype(F32))
            d1 = jnp.where(sub == tm - 1, _pick_row(halo, 0), pltpu.roll(d0, tm - 1, 0))
            d2 = jnp.where(sub == tm - 2, _pick_row(halo, 0),
                           jnp.where(sub == tm - 1, _pick_row(halo, 1), pltpu.roll(d0, tm - 2, 0)))
            a = a_ref[:, cols].astype(F32)
            upd = jnp.zeros((8, shard), F32)
            for j, t in enumerate((d2 * a, d1 * a, d0 * a, d0)):
                upd = upd + jnp.where(sub8 == j, jnp.sum(t, axis=0, keepdims=True), 0.0)
            dcw_ref[:, cols] = dcw_ref[:, cols] + upd
            da = (d0 * cw2[:, cols] + d1 * cw1[:, cols] + d2 * cw0[:, cols]).astype(BF16)
            dup_ref[:, cols] = da
            dn = dn + _dot_nt(da, wu_ref[half])
        h1 = h1_ref[...]
        r = lax.rsqrt(jnp.mean(h1 * h1, axis=-1, keepdims=True) + EPS)
        yn = h1 * r
        dg_ref[...] = dg_ref[...] + jnp.sum(dn * yn, axis=0, keepdims=True)
        dyn = dn * g_ref[...]
        dh1 = dh2_ref[...] + r * (dyn - yn * jnp.mean(dyn * yn, axis=-1, keepdims=True))
        dh1_ref[...] = dh1
        dh1b = dh1.astype(BF16)
        dh1b_ref[...] = dh1b
        dmix_ref[...] = _dot_nt(dh1b, wo_ref[...]).astype(BF16)

    rows = lambda w: pl.BlockSpec((tm, w), lambda i: (i, 0))
    halo = pl.BlockSpec((8, D_FF), lambda i: (jnp.minimum((i + 1) * (tm // 8), L // 8 - 1), 0))
    return pl.pallas_call(
        body, name="b_ffn_up", grid=(nt,),
        in_specs=[rows(D_FF), halo, rows(D_FF), rows(D_FF), _full((1, D_FF)), _full((1, D_FF)), _full((1, D_FF)),
                  _full((N_CHIPS, D_MODEL, shard)), rows(D_MODEL), _full((1, D_MODEL)), rows(D_MODEL),
                  _full((D_MODEL, D_MODEL))],
        out_specs=[rows(2 * D_FF), rows(D_MODEL), rows(D_MODEL), rows(D_MODEL), _full((1, D_MODEL)),
                   _full((8, D_FF))],
        out_shape=[jax.ShapeDtypeStruct((L, 2 * D_FF), BF16), jax.ShapeDtypeStruct((L, D_MODEL), F32),
                   jax.ShapeDtypeStruct((L, D_MODEL), BF16), jax.ShapeDtypeStruct((L, D_MODEL), BF16),
                   jax.ShapeDtypeStruct((1, D_MODEL), F32), jax.ShapeDtypeStruct((8, D_FF), F32)],
        compiler_params=_params(("arbitrary",)),
    )(dacc, dacc, db, up, cw[0], cw[1], cw[2], w_up, h1, ffn_g, dh2, w_out)


def _wgrad(a, b, name, tn=None, tk=None, out_dtype=F32):
    L, K = a.shape
    N = b.shape[1]
    tn = N if tn is None else tn
    tk = K if tk is None else tk
    tl = _row_tile(L, (1408, 768, 512, 256, 128))
    nl = L // tl

    def body(a_ref, b_ref, o_ref, acc):
        step = pl.program_id(2)

        @pl.when(step == 0)
        def _():
            acc[...] = jnp.zeros_like(acc)

        acc[...] = acc[...] + _dot_tn(a_ref[...], b_ref[...])

        @pl.when(step == nl - 1)
        def _():
            o_ref[0] = acc[...].astype(out_dtype)

    return pl.pallas_call(
        body, name=name, grid=(N // tn, K // tk, L // tl),
        in_specs=[pl.BlockSpec((tl, tk), lambda n, k, l: (l, k)), pl.BlockSpec((tl, tn), lambda n, k, l: (l, n))],
        out_specs=pl.BlockSpec((1, tk, tn), lambda n, k, l: (n, k, 0)),
        out_shape=jax.ShapeDtypeStruct((N // tn, K, tn), out_dtype),
        scratch_shapes=[pltpu.VMEM((tk, tn), F32)],
        compiler_params=_params(("parallel", "parallel", "arbitrary")),
    )(a, b)


def _retention_bwd(dmix, o, proj, cos_t, sin_t, ret_g, states, exchange=()):
    L = proj.shape[0]
    nblk = L // BLK
    G = _block_group(nblk)
    steps = nblk // G
    nx = len(exchange)
    dmat, wq_t, wk_t, g_blk = _decay_tables()

    def body(dm_ref, o_ref, q_ref, k_ref, v_ref, gate_ref, cos_ref, sin_ref, d_ref, wq_ref, wk_ref, rg_ref, rs_ref,
             *rest):
        x_in, (dp_ref, drg_ref), x_out, gstate = rest[:nx], rest[nx:nx + 2], rest[nx + 2:2 * nx + 2], rest[2 * nx + 2]

        @pl.when(pl.program_id(0) == 0)
        def _():
            if nx:
                for cp in _sibling_half_copies(x_in, x_out, *rest[2 * nx + 3:])[0]:
                    cp.start()
            gstate[...] = jnp.zeros_like(gstate)
            drg_ref[...] = jnp.zeros_like(drg_ref)

        lane = _iota((BLK, BLK), 1)
        sub = _iota((BLK, BLK), 0)
        scale = HEAD_LANES ** -0.5
        for b in reversed(range(G)):
            rows = slice(b * BLK, (b + 1) * BLK)
            rot, rot_t = _rot_fns(cos_ref[rows, :], sin_ref[rows, :])
            for p in range(2):
                qr = rot(q_ref[rows, p * BLK:(p + 1) * BLK].astype(F32))
                kr = rot(k_ref[rows, p * BLK:(p + 1) * BLK].astype(F32)) * scale
                kr_b = kr.astype(BF16)
                qw = (qr * wq_ref[p]).astype(BF16)
                kw = (kr * wk_ref[p]).astype(BF16)
                dqr = jnp.zeros((BLK, BLK), F32)
                dkr = jnp.zeros((BLK, BLK), F32)
                for e in range(2):
                    h = 2 * p + e
                    cols = slice(h * BLK, (h + 1) * BLK)
                    head_lanes = (lane >> 6) == e
                    o = o_ref[rows, cols]
                    rn = lax.rsqrt(jnp.mean(o * o, axis=-1, keepdims=True) + EPS)
                    y = o * rn
                    gate = gate_ref[rows, cols].astype(F32)
                    sg = jax.nn.sigmoid(gate)
                    dm = dm_ref[rows, cols].astype(F32)
                    rgain = rg_ref[:, cols]
                    drg_ref[:, cols] = drg_ref[:, cols] + jnp.sum(dm * y * (gate * sg), axis=0, keepdims=True)
                    dp_ref[rows, 1024 + h * BLK:1024 + (h + 1) * BLK] = (
                        dm * y * rgain * (sg * (1.0 + gate * (1.0 - sg)))).astype(BF16)
                    dy = dm * rgain * (gate * sg)
                    do = (rn * (dy - y * jnp.mean(dy * y, axis=-1, keepdims=True))).astype(BF16)
                    vh = v_ref[rows, cols]
                    qm = jnp.where(head_lanes, qr, 0.0).astype(BF16)
                    dmh = d_ref[h]
                    s = (_dot_nt(qm, kr_b) * dmh).astype(BF16)
                    ds = (_dot_nt(do, vh) * dmh).astype(BF16)
                    st = rs_ref[b, h].astype(BF16)
                    gs = gstate[h]
                    gs_b = gs.astype(BF16)
                    dqr = dqr + jnp.where(head_lanes, _dot(ds, kr_b), 0.0) + _dot_nt(do, st) * wq_ref[p]
                    dkr = dkr + _dot_tn(ds, qm) + _dot_nt(vh, gs_b) * wk_ref[p]
                    dp_ref[rows, 512 + h * BLK:512 + (h + 1) * BLK] = (_dot_tn(s, do) + _dot(kw, gs_b)).astype(BF16)
                    dr = jnp.where((sub >> 6) == e, _dot_tn(qw, do), 0.0)
                    gstate[h] = dr + g_blk[h] * gs
                dp_ref[rows, p * BLK:(p + 1) * BLK] = rot_t(dqr).astype(BF16)
                dp_ref[rows, 256 + p * BLK:256 + (p + 1) * BLK] = (rot_t(dkr) * scale).astype(BF16)

        if nx:
            @pl.when(pl.program_id(0) == steps - 1)
            def _():
                sends, recvs = _sibling_half_copies(x_in, x_out, *rest[2 * nx + 3:])
                for cp in recvs:
                    cp.wait_recv()
                for cp in sends:
                    cp.wait_send()

    row = lambda c: (lambda i: (steps - 1 - i, c))
    return pl.pallas_call(
        body, name="b_retention", grid=(steps,),
        in_specs=[pl.BlockSpec((G * BLK, 512), row(0)), pl.BlockSpec((G * BLK, 512), row(0)),
                  pl.BlockSpec((G * BLK, 256), row(0)), pl.BlockSpec((G * BLK, 256), row(1)),
                  pl.BlockSpec((G * BLK, 512), row(1)), pl.BlockSpec((G * BLK, 512), row(2)),
                  pl.BlockSpec((G * BLK, BLK), row(0)), pl.BlockSpec((G * BLK, BLK), row(0)),
                  _full((RET_HEADS, BLK, BLK)), _full((2, BLK, BLK)), _full((2, BLK, BLK)), _full((1, 512)),
                  pl.BlockSpec((G, RET_HEADS, BLK, BLK), lambda i: (steps - 1 - i, 0, 0, 0))] + [_ANY] * nx,
        out_specs=[pl.BlockSpec((G * BLK, RET_W), row(0)), _full((1, 512))] + [_ANY] * nx,
        out_shape=[jax.ShapeDtypeStruct((L, RET_W), BF16), jax.ShapeDtypeStruct((1, 512), F32)]
        + _sibling_half_shapes(exchange),
        scratch_shapes=[pltpu.VMEM((RET_HEADS, BLK, BLK), F32)] + _sibling_half_semaphores(nx),
        compiler_params=_params(("arbitrary",)),
    )(dmix, o, proj, proj, proj, proj, cos_t, sin_t, dmat, wq_t, wk_t, ret_g, states, *exchange)


def _fox_delta(dmix, o_f):
    L = o_f.shape[0]
    nblk = L // BLK
    G = _block_group(nblk, SMALL_GROUP)

    def body(do_ref, o_ref, d_ref):
        sel = ((_iota((8, 512), 1) >> 6) == _iota((8, 512), 0)).astype(BF16)
        for b in range(G):
            rows = slice(b * BLK, (b + 1) * BLK)
            prod = do_ref[rows, :].astype(F32) * o_ref[rows, :].astype(F32)
            hi = prod.astype(BF16)
            lo = (prod - hi.astype(F32)).astype(BF16)
            d_ref[b] = _dot_nt(sel, hi) + _dot_nt(sel, lo)

    return pl.pallas_call(
        body, name="b_foxdelta", grid=(nblk // G,),
        in_specs=[pl.BlockSpec((G * BLK, 512), lambda i: (i, 1)), pl.BlockSpec((G * BLK, 512), lambda i: (i, 0))],
        out_specs=pl.BlockSpec((G, 8, BLK), lambda i: (i, 0, 0)),
        out_shape=jax.ShapeDtypeStruct((nblk, 8, BLK), F32),
        compiler_params=_params(("parallel",)),
    )(dmix, o_f)


def _fox_bwd(proj, dmix, c, ctb, lse, delta, scatter=()):
    L = proj.shape[0]
    nblk, nu = _fox_units(L)
    scale = HEAD_LANES ** -0.5
    ns = len(scatter)

    FOX_PAIRS = 4
    steps = FOX_HEADS // (2 * FOX_PAIRS)

    def body(qkv_ref, do_ref, c_ref, ct_ref, lse_ref, dl_ref, *rest):
        s_in, (dp_ref, dc_ref, dcq_ref), s_out = rest[:ns], rest[ns:ns + 3], rest[ns + 3:2 * ns + 3]
        ktt, dqt, dk_acc, dv_acc, dcs_acc = rest[2 * ns + 3:2 * ns + 8]
        p = pl.program_id(0)
        heads = [(pp, e, 2 * FOX_PAIRS * p + 2 * pp + e) for pp in range(FOX_PAIRS) for e in range(2)]

        @pl.when(p == 0)
        def _():
            dc_ref[...] = jnp.zeros_like(dc_ref)
            dcq_ref[...] = jnp.zeros_like(dcq_ref)
            if ns:
                for cp in _scatter_copies(s_in, s_out, *rest[2 * ns + 8:]):
                    cp.start()

        sub8 = _iota((8, BLK), 0)
        masks = _fox_tile_masks()

        def pre(j, carry):
            off = pl.multiple_of(j * BLK, BLK)
            for pp in range(FOX_PAIRS):
                ktt[pp, j] = qkv_ref[pl.ds(off, BLK), pp * 384 + BLK:pp * 384 + 2 * BLK].astype(F32).T.astype(BF16)
                dqt[pp, j] = jnp.zeros((BLK, BLK), F32)
            return carry

        lax.fori_loop(0, nblk, pre, 0)

        def kv_pass(kblk, nk, n_later):
            klen = nk * BLK
            koff = pl.multiple_of(kblk * BLK, BLK)
            kt = [qkv_ref[pl.ds(koff, klen), pp * 384 + BLK:pp * 384 + 2 * BLK] for pp in range(FOX_PAIRS)]
            vtile = [qkv_ref[pl.ds(koff, klen), pp * 384 + 2 * BLK:pp * 384 + 3 * BLK] for pp in range(FOX_PAIRS)]
            ct = c_ref[pl.ds(koff, klen), :]
            klane = _iota((klen, BLK), 1)
            cs = [jnp.broadcast_to(jnp.sum(jnp.where(klane == h, ct, 0.0), axis=1, keepdims=True), (klen, WIDE * UNIT))
                  for _, _, h in heads]
            k_t = [jnp.concatenate([ktt[pp, kblk + b, e * HEAD_LANES:(e + 1) * HEAD_LANES, :] for b in range(nk)], axis=1)
                   for pp, e, _ in heads]
            for pp in range(FOX_PAIRS):
                dk_acc[pp, 0:klen] = jnp.zeros((klen, BLK), F32)
                dv_acc[pp, 0:klen] = jnp.zeros((klen, BLK), F32)
            for hh in range(len(heads)):
                dcs_acc[hh, 0:klen] = jnp.zeros((klen, BLK), F32)

            def tile(qblk, nq, mask):
                qlen = nq * BLK
                if mask == "valid":
                    mask = _iota((klen, qlen), 0) >= N_PAD
                qoff = pl.multiple_of(qblk * BLK, BLK)
                qlane = _iota((qlen, BLK), 1)
                qs = [qkv_ref[pl.ds(qoff, qlen), pp * 384:pp * 384 + BLK].astype(F32) * (scale * LOG2E)
                      for pp in range(FOX_PAIRS)]
                dot_ = [do_ref[pl.ds(qoff, qlen), pp * BLK:(pp + 1) * BLK] for pp in range(FOX_PAIRS)]
                stats = [[ref[qblk + a] for a in range(nq)] for ref in (ct_ref, lse_ref, dl_ref)]
                dcq = [jnp.zeros((8, BLK), F32) for _ in range(nq)]
                for hh, (pp, e, h) in enumerate(heads):
                    head = (qlane >> 6) == e
                    ct_row, lse_row, dl_row = [jnp.concatenate([_pick_row(t, h) for t in ts], axis=1) for ts in stats]
                    qm = jnp.where(head, qs[pp], 0.0).astype(BF16)
                    dom = jnp.where(head, dot_[pp], jnp.zeros_like(dot_[pp]))
                    t = _dot_nt(kt[pp], qm) - cs[hh][:, 0:qlen]
                    if mask is not None:
                        t = jnp.where(mask, t, NEG)
                    pr = jnp.exp2(t + (ct_row - lse_row))
                    dv_acc[pp, 0:klen] = dv_acc[pp, 0:klen] + _dot(pr.astype(BF16), dom)
                    dsv = pr * (_dot_nt(vtile[pp], dom) - dl_row)
                    ds_b = dsv.astype(BF16)
                    dk_acc[pp, 0:klen] = dk_acc[pp, 0:klen] + _dot(ds_b, qm)
                    rows = slice(e * HEAD_LANES, (e + 1) * HEAD_LANES)
                    dq_t = _dot(k_t[hh], ds_b)
                    key_side = dsv[:, 0:BLK]
                    for a in range(1, nq):
                        key_side = key_side + dsv[:, a * BLK:(a + 1) * BLK]
                    dcs_acc[hh, 0:klen] = dcs_acc[hh, 0:klen] + key_side
                    query_side = jnp.sum(dsv, axis=0, keepdims=True)
                    for a in range(nq):
                        cols = slice(a * BLK, (a + 1) * BLK)
                        dqt[pp, qblk + a, rows, :] = dqt[pp, qblk + a, rows, :] + dq_t[:, cols]
                        dcq[a] = dcq[a] + jnp.where(sub8 == h, query_side[:, cols], 0.0)
                for a in range(nq):
                    dcq_ref[qblk + a] = dcq_ref[qblk + a] + dcq[a]

            later_mask = "valid" if nk == 1 else None
            n_later = jnp.asarray(n_later, jnp.int32)
            n_wide = n_later // WIDE

            def later_wide(i, carry):
                tile(kblk + nk + 2 * WIDE * i, 2 * WIDE, later_mask)
                return carry

            tile(kblk, nk, masks["first"] if nk == 1 else masks["diag"])
            lax.fori_loop(0, n_wide, later_wide, 0)
            rest_blk = kblk + nk + 2 * WIDE * n_wide

            @pl.when((n_later & 2) != 0)
            def _():
                tile(rest_blk, 4, later_mask)

            @pl.when((n_later & 1) != 0)
            def _():
                tile(rest_blk + 2 * (n_later & 2), 2, later_mask)

            upd = jnp.zeros((klen, BLK), F32)
            for hh, (_, _, h) in enumerate(heads):
                upd = upd + jnp.where(klane == h, -jnp.sum(dcs_acc[hh, 0:klen], axis=1, keepdims=True), 0.0)
            dc_ref[pl.ds(koff, klen), :] = dc_ref[pl.ds(koff, klen), :] + upd
            for pp in range(FOX_PAIRS):
                dp_ref[pl.ds(koff, klen), pp * 384 + BLK:pp * 384 + 2 * BLK] = (
                    dk_acc[pp, 0:klen] * (1.0 / LOG2E)).astype(BF16)
                dp_ref[pl.ds(koff, klen), pp * 384 + 2 * BLK:pp * 384 + 3 * BLK] = dv_acc[pp, 0:klen].astype(BF16)

        kv_pass(0, 1, nu)

        def k_loop(u, carry):
            kv_pass(1 + 2 * u, 2, nu - 1 - u)
            return carry

        lax.fori_loop(0, nu, k_loop, 0)

        def flush(j, carry):
            off = pl.multiple_of(j * BLK, BLK)
            for pp in range(FOX_PAIRS):
                dp_ref[pl.ds(off, BLK), pp * 384:pp * 384 + BLK] = (dqt[pp, j].T * scale).astype(BF16)
            return carry

        lax.fori_loop(0, nblk, flush, 0)

        if ns:
            @pl.when(p == steps - 1)
            def _():
                copies = _scatter_copies(s_in, s_out, *rest[2 * ns + 8:])
                for cp in copies:
                    cp.wait_recv()
                for cp in copies:
                    cp.wait_send()

    width = 384 * FOX_PAIRS
    once = lambda shape, index: pl.BlockSpec(shape, index, pipeline_mode=pl.Buffered(1))
    stat = once((nblk, 8, BLK), lambda p: (0, 0, 0))
    return pl.pallas_call(
        body, name="b_fox", grid=(steps,),
        in_specs=[once((L, width), lambda p: (0, RET_W // width + p)),
                  once((L, FOX_PAIRS * BLK), lambda p: (0, 4 // FOX_PAIRS + p)),
                  once((L, BLK), lambda p: (0, 0)), stat, stat, stat] + [_ANY] * ns,
        out_specs=[once((L, width), lambda p: (0, p)), _full((L, BLK)), _full((nblk, 8, BLK))] + [_ANY] * ns,
        out_shape=[jax.ShapeDtypeStruct((L, FOX_W), BF16), jax.ShapeDtypeStruct((L, BLK), F32),
                   jax.ShapeDtypeStruct((nblk, 8, BLK), F32)] + _scatter_shapes(scatter),
        scratch_shapes=[pltpu.VMEM((FOX_PAIRS, nblk, BLK, BLK), BF16), pltpu.VMEM((FOX_PAIRS, nblk, BLK, BLK), F32),
                        pltpu.VMEM((FOX_PAIRS, UNIT, BLK), F32), pltpu.VMEM((FOX_PAIRS, UNIT, BLK), F32),
                        pltpu.VMEM((2 * FOX_PAIRS, UNIT, BLK), F32)]
        + _scatter_semaphores(ns),
        compiler_params=_params(("arbitrary",)),
    )(proj, dmix, c, ctb, lse, delta, *scatter)


def _fox_post(dc, dcq, ff, fb):
    L = dc.shape[0]
    nblk = L // BLK
    G = _block_group(nblk, SMALL_GROUP)
    steps = nblk // G

    def body(dc_ref, dcq_ref, ff_ref, b_ref, dff_ref, dffb_ref, dfb_ref, carry):
        @pl.when(pl.program_id(0) == 0)
        def _():
            carry[...] = jnp.zeros_like(carry)
            dfb_ref[...] = jnp.zeros_like(dfb_ref)

        tri = (_iota((BLK, BLK), 0) <= _iota((BLK, BLK), 1)).astype(BF16)
        live = _iota((BLK, BLK), 1) < FOX_HEADS
        run, dfb = carry[...], dfb_ref[...]
        for b in reversed(range(G)):
            rows = slice(b * BLK, (b + 1) * BLK)
            d = dc_ref[rows, :] + jnp.concatenate([dcq_ref[b], jnp.zeros((BLK - 8, BLK), F32)], axis=0).T
            hi, mid, lo = _split3(d)
            dlf = _dot(tri, hi) + _dot(tri, mid) + _dot(tri, lo) + run
            run = run + jnp.sum(d, axis=0, keepdims=True)
            z = ff_ref[rows, :] + b_ref[...]
            dff = jnp.where(live, dlf * jax.nn.sigmoid(-z), 0.0)
            dff_ref[rows, :] = dff
            dffb_ref[rows, :] = dff.astype(BF16)
            dfb = dfb + jnp.sum(dff, axis=0, keepdims=True)
        carry[...] = run
        dfb_ref[...] = dfb

    rev = lambda i: (steps - 1 - i, 0)
    return pl.pallas_call(
        body, name="b_foxpost", grid=(steps,),
        in_specs=[pl.BlockSpec((G * BLK, BLK), rev), pl.BlockSpec((G, 8, BLK), lambda i: (steps - 1 - i, 0, 0)),
                  pl.BlockSpec((G * BLK, BLK), rev), _full((1, BLK))],
        out_specs=[pl.BlockSpec((G * BLK, BLK), rev), pl.BlockSpec((G * BLK, BLK), rev), _full((1, BLK))],
        out_shape=[jax.ShapeDtypeStruct((L, BLK), F32), jax.ShapeDtypeStruct((L, BLK), BF16),
                   jax.ShapeDtypeStruct((1, BLK), F32)],
        scratch_shapes=[pltpu.VMEM((1, BLK), F32)],
        compiler_params=_params(("arbitrary",)),
    )(dc, dcq, ff, fb)


def _inproj_bwd(dpr, dpf, dffb, w_main, w_ff, h0, g, dh1, scatter=()):
    L = h0.shape[0]
    S = L - BLK
    tm = _row_tile(S, (512, 256, 128))
    nt = S // tm
    ns = len(scatter)
    operands = (dpr, dpf, dffb, h0, dh1)

    def body(*refs):
        lead, tile = refs[0:5], refs[5:10]
        wm_ref, wf_ref, g_ref = refs[10:13]
        rest = refs[13:]
        s_in, (dlead_ref, dx_ref, dg_ref), s_out = rest[:ns], rest[ns:ns + 3], rest[ns + 3:2 * ns + 3]
        i = pl.program_id(0)

        def rows_bwd(dpr_ref, dpf_ref, dff_ref, h_ref, dh1_ref):
            dn = (_dot_nt(dpr_ref[...], wm_ref[:, 0:RET_W]) + _dot_nt(dpf_ref[...], wm_ref[:, RET_W:MAIN_W])
                  + _dot_nt(dff_ref[...], wf_ref[...]))
            h = h_ref[...]
            r = lax.rsqrt(jnp.mean(h * h, axis=-1, keepdims=True) + EPS)
            yn = h * r
            dyn = dn * g_ref[...]
            dh0 = dh1_ref[...] + r * (dyn - yn * jnp.mean(dyn * yn, axis=-1, keepdims=True))
            return dh0, jnp.sum(dn * yn, axis=0, keepdims=True)

        @pl.when(i == 0)
        def _():
            if ns:
                for cp in _scatter_copies(s_in, s_out, *rest[2 * ns + 3:]):
                    cp.start()
            dlead_ref[...], dg_ref[...] = rows_bwd(*lead)

        dx_ref[...], dg_tile = rows_bwd(*tile)
        dg_ref[...] = dg_ref[...] + dg_tile

        if ns:
            @pl.when(i == nt - 1)
            def _():
                copies = _scatter_copies(s_in, s_out, *rest[2 * ns + 3:])
                for cp in copies:
                    cp.wait_recv()
                for cp in copies:
                    cp.wait_send()

    lead_spec = lambda a: pl.BlockSpec((BLK, a.shape[1]), lambda i: (0, 0))
    tile_spec = lambda a: pl.BlockSpec((pl.Element(tm), pl.Element(a.shape[1])),
                                       lambda i: (pl.multiple_of(BLK + i * tm, BLK), 0))
    return pl.pallas_call(
        body, name="b_inproj", grid=(nt,),
        in_specs=[lead_spec(a) for a in operands] + [tile_spec(a) for a in operands]
        + [_full((D_MODEL, MAIN_W)), _full((D_MODEL, BLK)), _full((1, D_MODEL))] + [_ANY] * ns,
        out_specs=[_full((BLK, D_MODEL)), pl.BlockSpec((tm, D_MODEL), lambda i: (i, 0)), _full((1, D_MODEL))]
        + [_ANY] * ns,
        out_shape=[jax.ShapeDtypeStruct((BLK, D_MODEL), F32), jax.ShapeDtypeStruct((S, D_MODEL), F32),
                   jax.ShapeDtypeStruct((1, D_MODEL), F32)] + _scatter_shapes(scatter),
        scratch_shapes=_scatter_semaphores(ns),
        compiler_params=_params(("arbitrary",)),
    )(*operands, *operands, w_main, w_ff, g, *scatter)


def _local_step(x, target, meta, attn_g, w_main, w_ff, fox_b, ret_g, w_out, ffn_g, w_up, conv_w, conv_b, w_down, final_g,
                late=None, mid=None, last=None, wire=F32):
    S = x.shape[0]
    L = S + PREFIX
    head = jnp.concatenate([jnp.zeros((N_PAD, D_MODEL), F32), meta], axis=0)
    fb = jnp.pad(fox_b, ((0, 0), (0, BLK - FOX_HEADS)))
    cos_t, sin_t = _rotary_tables(L)

    h0, n1, proj, ff = _rms_inproj(head, x, attn_g, w_main, w_ff)
    c, ctb = _fox_prep(ff, fb)
    mix_r, o_ret, states = _retention_fwd(proj, cos_t, sin_t, ret_g)
    if late is None:
        o_f, lse = _fox_fwd(proj, c, ctb)
    else:
        o_f, lse, *gathered = _fox_fwd(proj, c, ctb, gather=late[0])
        w_out, w_up, w_down = late[1](gathered)
    h1, n2, up, g_act, acc_saved = _outproj_up(mix_r, o_f, h0, w_out, ffn_g, w_up, conv_w, conv_b)
    dh2, dh2b, d_final_g, loss, dacc, db = _ffn_down_loss(g_act, w_down, h1, final_g, target, acc_saved, up)

    dup, dh1, dh1b, dmix, d_ffn_g, dconv = _ffn_bwd_up(dacc, db, up, conv_w, w_up, h1, ffn_g, dh2, w_out)
    d_w_down = _wgrad(g_act, dh2b, "wgrad_down", tk=D_FF // 2, out_dtype=wire)[0]
    d_w_up = _wgrad(n2, dup, "wgrad_up", tn=w_up.shape[2], out_dtype=wire)
    d_w_out = jnp.concatenate([_wgrad(mix_r, dh1b, "wgrad_out_r", out_dtype=wire)[0],
                               _wgrad(o_f, dh1b, "wgrad_out_f", out_dtype=wire)[0]], axis=0)

    early = () if mid is None else mid[0](d_w_out, d_w_up, d_w_down)
    dpr, d_ret_g, *from_sibling = _retention_bwd(dmix, o_ret, proj, cos_t, sin_t, ret_g, states, exchange=early)
    delta = _fox_delta(dmix, o_f)
    scatter = () if mid is None else mid[1](early, from_sibling)
    dpf, dc, dcq, *received = _fox_bwd(proj, dmix, c, ctb, lse, delta, scatter=scatter)
    dff, dffb, d_fox_b = _fox_post(dc, dcq, ff, fb)
    d_w_ret, d_w_fox = _wgrad(n1, dpr, "wgrad_in_r")[0], _wgrad(n1, dpf, "wgrad_in_f")[0]
    d_w_ff = _wgrad(n1, dffb, "wgrad_in_ff")[0][:, :FOX_HEADS]
    scatter_in = () if last is None else last(d_w_ret, d_w_fox, d_w_ff)
    dlead, dx, d_attn_g, *received_in = _inproj_bwd(dpr, dpf, dffb, w_main, w_ff, h0, attn_g, dh1, scatter=scatter_in)

    return dict(
        loss=loss[0, 0], dx=dx, dmeta=dlead[N_PAD:], attn_g=d_attn_g, w_main=jnp.concatenate([d_w_ret, d_w_fox], axis=1),
        w_ff=d_w_ff, fox_b=d_fox_b[:, :FOX_HEADS], ret_g=d_ret_g, w_out=d_w_out, ffn_g=d_ffn_g,
        w_up=d_w_up, conv_w=dconv[0:3], conv_b=dconv[3:4], w_down=d_w_down, final_g=d_final_g,
        scatter=list(scatter_in) + list(scatter), received=list(received_in) + list(received))


_ANY = pl.BlockSpec(memory_space=pl.ANY)


def _place():
    return lax.axis_index("x"), lax.axis_index("y"), lax.axis_index("c")


def _other_chips(x, y):
    return [(1 - x, y), (x, 1 - y), (1 - x, 1 - y)]


def _allgather_semaphores(n):
    if n == 0:
        return []
    return [pltpu.SemaphoreType.DMA((3 * n,)), pltpu.SemaphoreType.DMA((3 * n,)), pltpu.SemaphoreType.DMA((n,))]


def _allgather_copies(ins, outs, send, recv, loc):
    n = len(ins)
    x, y, c = _place()
    mine = 2 * x + y
    peers = _other_chips(x, y)

    def remote(a, k, slot):
        return pltpu.make_async_remote_copy(
            src_ref=ins[a], dst_ref=outs[a].at[slot], send_sem=send.at[3 * a + k], recv_sem=recv.at[3 * a + k],
            device_id=(peers[k][0], peers[k][1], c), device_id_type=MESH)

    local = [pltpu.make_async_copy(ins[a], outs[a].at[mine], loc.at[a]) for a in range(n)]
    sends = [remote(a, k, mine) for a in range(n) for k in range(3)]
    recvs = [remote(a, k, 2 * peers[k][0] + peers[k][1]) for a in range(n) for k in range(3)]
    return local, sends, recvs


def _chip_allgather_halves(w, small):
    half = w.shape[0] // 2

    def body(w_ref, s_ref, wo_ref, so_ref, send, recv, fsend, frecv, ssend, srecv, loc):
        x, y, c = _place()
        mine = 2 * x + y
        peers = _other_chips(x, y)

        def fetch(k, slot):
            return pltpu.make_async_remote_copy(
                src_ref=w_ref.at[pl.ds(c * half, half)], dst_ref=wo_ref.at[slot, c], send_sem=send.at[k],
                recv_sem=recv.at[k], device_id=(peers[k][0], peers[k][1], c), device_id_type=MESH)

        def forward(k, which):
            slot = 2 * peers[k][0] + peers[k][1]
            return pltpu.make_async_remote_copy(
                src_ref=wo_ref.at[slot, which], dst_ref=wo_ref.at[slot, which], send_sem=fsend.at[k],
                recv_sem=frecv.at[k], device_id=(x, y, 1 - c), device_id_type=MESH)

        def small_copy(k, slot):
            return pltpu.make_async_remote_copy(
                src_ref=s_ref, dst_ref=so_ref.at[slot], send_sem=ssend.at[k], recv_sem=srecv.at[k],
                device_id=(peers[k][0], peers[k][1], c), device_id_type=MESH)

        local = pltpu.make_async_copy(s_ref, so_ref.at[mine], loc.at[0])
        sends = [fetch(k, mine) for k in range(3)] + [small_copy(k, mine) for k in range(3)]
        local.start()
        for cp in sends:
            cp.start()
        forwards = []
        for k in range(3):
            fetch(k, 2 * peers[k][0] + peers[k][1]).wait_recv()
            forwards.append(forward(k, c))
            forwards[-1].start()
        for k in range(3):
            forward(k, 1 - c).wait_recv()
            small_copy(k, 2 * peers[k][0] + peers[k][1]).wait_recv()
        for cp in sends + forwards:
            cp.wait_send()
        local.wait()

    three = pltpu.SemaphoreType.DMA((3,))
    return pl.pallas_call(
        body, name="ag_weights", in_specs=[_ANY] * 2, out_specs=[_ANY] * 2,
        out_shape=[jax.ShapeDtypeStruct((N_CHIPS, 2, half, w.shape[1]), w.dtype),
                   jax.ShapeDtypeStruct((N_CHIPS,) + small.shape, small.dtype)],
        scratch_shapes=[three, three, three, three, three, three, pltpu.SemaphoreType.DMA((1,))],
    )(w, small)


def _chip_allgather(arrays):
    n = len(arrays)

    def body(*refs):
        local, sends, recvs = _allgather_copies(refs[:n], refs[n:2 * n], *refs[2 * n:])
        for cp in local + sends:
            cp.start()
        for cp in recvs:
            cp.wait_recv()
        for cp in sends:
            cp.wait_send()
        for cp in local:
            cp.wait()

    return pl.pallas_call(
        body, name="ag_weights", in_specs=[_ANY] * n, out_specs=[_ANY] * n,
        out_shape=[jax.ShapeDtypeStruct((N_CHIPS,) + a.shape, a.dtype) for a in arrays],
        scratch_shapes=_allgather_semaphores(n),
    )(*arrays)


def _sibling_halves(grads):
    n = len(grads)

    def body(*refs):
        sends, recvs = _sibling_half_copies(refs[:n], refs[n:2 * n], *refs[2 * n:])
        for cp in sends:
            cp.start()
        for cp in recvs:
            cp.wait_recv()
        for cp in sends:
            cp.wait_send()

    return pl.pallas_call(
        body, name="rs_sibling", in_specs=[_ANY] * n, out_specs=[_ANY] * n,
        out_shape=_sibling_half_shapes(grads), scratch_shapes=_sibling_half_semaphores(n),
    )(*grads)


def _sibling_half_shapes(grads):
    return [jax.ShapeDtypeStruct((N_CHIPS, g.shape[1] // 2, g.shape[2]), g.dtype) for g in grads]


def _sibling_half_semaphores(n):
    return [pltpu.SemaphoreType.DMA((n,)), pltpu.SemaphoreType.DMA((n,))] if n else []


def _sibling_half_copies(ins, outs, send, recv):
    x, y, c = _place()

    def half_copy(a, which):
        half = ins[a].shape[1] // 2
        return pltpu.make_async_remote_copy(
            src_ref=ins[a].at[pl.ds(0, N_CHIPS), pl.ds(which * half, half)], dst_ref=outs[a],
            send_sem=send.at[a], recv_sem=recv.at[a], device_id=(x, y, 1 - c), device_id_type=MESH)

    return [half_copy(a, 1 - c) for a in range(len(ins))], [half_copy(a, c) for a in range(len(ins))]


def _scatter_shapes(parts):
    return [jax.ShapeDtypeStruct((3,) + p.shape[1:], p.dtype) for p in parts]


def _scatter_semaphores(n):
    return [pltpu.SemaphoreType.DMA((3 * n,)), pltpu.SemaphoreType.DMA((3 * n,))] if n else []


def _scatter_copies(ins, outs, send, recv):
    x, y, c = _place()
    peers = _other_chips(x, y)
    return [pltpu.make_async_remote_copy(
        src_ref=ins[a].at[2 * peers[k][0] + peers[k][1]], dst_ref=outs[a].at[k], send_sem=send.at[3 * a + k],
        recv_sem=recv.at[3 * a + k], device_id=(peers[k][0], peers[k][1], c), device_id_type=MESH)
        for a in range(len(ins)) for k in range(3)]


def _sibling_allgather(bufs, small):
    n = len(bufs)

    def body(*refs):
        small_in, outs, small_out = refs[n], refs[n + 1:2 * n + 1], refs[2 * n + 1]
        send, recv, s_send, s_recv, loc = refs[2 * n + 2:]
        x, y, c = _place()
        me = 4 * x + 2 * y + c

        def remote(a, which):
            return pltpu.make_async_remote_copy(
                src_ref=outs[a].at[which], dst_ref=outs[a].at[which], send_sem=send.at[a], recv_sem=recv.at[a],
                device_id=(x, y, 1 - c), device_id_type=MESH)

        def peer_of(r):
            return tuple(1 - v if (r >> b) & 1 else v for v, b in ((x, 2), (y, 1), (c, 0)))

        def small_copy(r, slot):
            return pltpu.make_async_remote_copy(
                src_ref=small_in, dst_ref=small_out.at[slot], send_sem=s_send.at[r - 1], recv_sem=s_recv.at[r - 1],
                device_id=peer_of(r), device_id_type=MESH)

        local = pltpu.make_async_copy(small_in, small_out.at[me], loc.at[0])
        sends = [remote(a, c) for a in range(n)] + [small_copy(r, me) for r in range(1, N_DEV)]
        local.start()
        for cp in sends:
            cp.start()
        for r in range(1, N_DEV):
            px, py, pc = peer_of(r)
            small_copy(r, 4 * px + 2 * py + pc).wait_recv()
        for a in range(n):
            remote(a, 1 - c).wait_recv()
        for cp in sends:
            cp.wait_send()
        local.wait()

    outs = pl.pallas_call(
        body, name="ag_sibling", in_specs=[_ANY] * (n + 1), out_specs=[_ANY] * (n + 1),
        out_shape=[jax.ShapeDtypeStruct(b.shape, b.dtype) for b in bufs]
        + [jax.ShapeDtypeStruct((N_DEV,) + small.shape, small.dtype)],
        input_output_aliases={a: a for a in range(n)},
        scratch_shapes=[pltpu.SemaphoreType.DMA((n,)), pltpu.SemaphoreType.DMA((n,)),
                        pltpu.SemaphoreType.DMA((N_DEV - 1,)), pltpu.SemaphoreType.DMA((N_DEV - 1,)),
                        pltpu.SemaphoreType.DMA((1,))],
    )(*bufs, small)
    return [o.reshape(2 * o.shape[1], o.shape[2]) for o in outs[:n]], outs[n]


def _pair_add(full, recv, core, name):
    _, R, C = full.shape
    half = R // 2

    def body(core_ref, a_ref, b_ref, o_ref):
        o_ref[...] = (a_ref[...].astype(F32) + b_ref[...].astype(F32)).astype(BF16)

    return pl.pallas_call(
        body, name=name,
        grid_spec=pltpu.PrefetchScalarGridSpec(
            num_scalar_prefetch=1, grid=(N_CHIPS,),
            in_specs=[pl.BlockSpec((1, half, C), lambda j, core_ref: (j, core_ref[0], 0)),
                      pl.BlockSpec((1, half, C), lambda j, core_ref: (j, 0, 0))],
            out_specs=pl.BlockSpec((1, half, C), lambda j, core_ref: (j, 0, 0))),
        out_shape=jax.ShapeDtypeStruct((N_CHIPS, half, C), BF16),
        compiler_params=_params(("parallel",)),
    )(core, full, recv)


def _sum_partials(own_all, recv, place, name, tiles=2):
    _, R, C = own_all.shape
    tr = R // tiles

    def body(place_ref, own_ref, r_ref, o_ref):
        acc = own_ref[0].astype(F32)
        for k in range(3):
            acc = acc + r_ref[k].astype(F32)
        o_ref[0] = acc

    return pl.pallas_call(
        body, name=name,
        grid_spec=pltpu.PrefetchScalarGridSpec(
            num_scalar_prefetch=1, grid=(tiles,),
            in_specs=[pl.BlockSpec((1, tr, C), lambda i, place_ref: (place_ref[0], i, 0)),
                      pl.BlockSpec((3, tr, C), lambda i, place_ref: (0, i, 0))],
            out_specs=pl.BlockSpec((1, tr, C), lambda i, place_ref: (place_ref[1], i, 0))),
        out_shape=jax.ShapeDtypeStruct((2, R, C), F32),
        compiler_params=_params(("parallel",)),
    )(place, own_all, recv)


def _adamw_math(w, g, m, v):
    m2 = ADAM_B1 * m + (1.0 - ADAM_B1) * g
    v2 = ADAM_B2 * v + (1.0 - ADAM_B2) * (g * g)
    m_hat = m2 / (1.0 - ADAM_B1 ** ADAM_STEP)
    v_hat = v2 / (1.0 - ADAM_B2 ** ADAM_STEP)
    return -ADAM_LR * (m_hat / (jnp.sqrt(v_hat) + ADAM_EPS) + ADAM_WD * w), m2, v2


ROW_ATTN_G, ROW_FFN_G, ROW_FINAL_G, ROW_MISC, ROW_CONV_B, ROW_CONV_W, ROW_META, SMALL_ROWS = 0, 1, 2, 3, 4, 8, 24, 40
MISC_FOX_B, MISC_LOSS = 512, 640


def _small_pack(out):
    def rows(a, n):
        a = a.astype(F32)
        return jnp.pad(a, ((0, n - a.shape[0]), (0, D_MODEL - a.shape[1])))

    misc = jnp.concatenate([out["ret_g"], out["fox_b"], jnp.zeros((1, MISC_LOSS - MISC_FOX_B - FOX_HEADS), F32),
                            out["loss"].reshape(1, 1)], axis=1)
    conv_b = jnp.pad(out["conv_b"], ((0, 0), (0, (-D_FF) % D_MODEL))).reshape(-1, D_MODEL)
    conv_w = out["conv_w"].reshape(3, N_CHIPS, -1).transpose(1, 0, 2).reshape(3 * N_CHIPS, -1)
    return jnp.concatenate([
        rows(out["attn_g"], 1), rows(out["ffn_g"], 1), rows(out["final_g"], 1), rows(misc, 1),
        rows(conv_b, ROW_CONV_W - ROW_CONV_B), rows(conv_w, ROW_META - ROW_CONV_W), rows(out["dmeta"], N_META)], axis=0)


def _small_update(packs, chip, ws, ms, vs):
    n = len(ws)
    meta_w, conv_sw = ws[0].shape[1], ws[5].shape[2]
    assert packs.shape == (N_DEV, SMALL_ROWS, D_MODEL) and ws[0].shape[0] == N_META and ws[5].shape[:2] == (3, 1)

    def body(chip_ref, p_ref, *refs):
        w_refs, m_refs, v_refs = refs[:n], refs[n:2 * n], refs[2 * n:3 * n]
        loss_ref, out_refs, tot = refs[3 * n], refs[3 * n + 1:7 * n + 1], refs[7 * n + 1]
        acc = p_ref[0]
        for d in range(1, N_DEV):
            acc = acc + p_ref[d]
        tot[...] = acc

        def of_chip(pieces):
            val = pieces[-1]
            for j in range(N_CHIPS - 2, -1, -1):
                val = jnp.where(chip_ref[0] == j, pieces[j], val)
            return val

        row = lambda r, lo=0, hi=D_MODEL: tot[r:r + 1, lo:hi]
        grads = [
            of_chip([tot[ROW_META:ROW_META + N_META, j * meta_w:(j + 1) * meta_w] for j in range(N_CHIPS)]),
            row(ROW_ATTN_G), row(ROW_MISC, MISC_FOX_B, MISC_FOX_B + FOX_HEADS), row(ROW_MISC, 0, MISC_FOX_B),
            row(ROW_FFN_G),
            of_chip([tot[ROW_CONV_W + 3 * j:ROW_CONV_W + 3 * j + 3, 0:conv_sw] for j in range(N_CHIPS)]),
            jnp.concatenate([row(ROW_CONV_B), row(ROW_CONV_B + 1), row(ROW_CONV_B + 2, 0, D_FF - 2 * D_MODEL)], axis=1),
            row(ROW_FINAL_G)]
        loss_ref[...] = row(ROW_MISC, MISC_LOSS, MISC_LOSS + BLK)
        for k in range(n):
            parts = [((Ellipsis,), grads[k])]
            if len(ws[k].shape) == 3:
                parts = [((t,), grads[k][t:t + 1]) for t in range(ws[k].shape[0])]
            for at, g in parts:
                res = (g,) + _adamw_math(w_refs[k][at], g, m_refs[k][at], v_refs[k][at])
                for kind in range(4):
                    out_refs[kind * n + k][at] = res[kind]

    res = pl.pallas_call(
        body, name="small_update",
        grid_spec=pltpu.PrefetchScalarGridSpec(
            num_scalar_prefetch=1, grid=(1,),
            in_specs=[_full(packs.shape)] + [_full(a.shape) for a in list(ws) * 3],
            out_specs=[_full((1, BLK))] + [_full(a.shape) for a in list(ws) * 4],
            scratch_shapes=[pltpu.VMEM((SMALL_ROWS, D_MODEL), F32)]),
        out_shape=[jax.ShapeDtypeStruct((1, BLK), F32)] + [jax.ShapeDtypeStruct(a.shape, F32) for a in list(ws) * 4],
        compiler_params=_params(("arbitrary",)),
    )(chip, packs, *ws, *ms, *vs)
    return res[0], res[1:n + 1], res[n + 1:2 * n + 1], res[2 * n + 1:3 * n + 1], res[3 * n + 1:]


def _adamw(w, g, m, v, name, tiles=4):
    R, tail = w.shape[0], w.shape[1:]
    assert R % tiles == 0
    tr = R // tiles

    def body(w_ref, g_ref, m_ref, v_ref, go_ref, d_ref, m2_ref, v2_ref):
        g_ = g_ref[...]
        go_ref[...] = g_
        d_ref[...], m2_ref[...], v2_ref[...] = _adamw_math(w_ref[...], g_, m_ref[...], v_ref[...])

    spec = pl.BlockSpec((tr,) + tail, lambda i: (i,) + (0,) * len(tail))
    return pl.pallas_call(
        body, name=name, grid=(tiles,), in_specs=[spec] * 4, out_specs=[spec] * 4,
        out_shape=[jax.ShapeDtypeStruct(w.shape, F32)] * 4,
        compiler_params=_params(("parallel",)),
    )(w, g, m, v)


def _row_vector_tiles(n, most=80):
    return next(t for t in range(1, n + 1) if n % t == 0 and n // t <= most)


def _pack_rows(pieces, rows):
    flat = jnp.concatenate([jnp.pad(p.reshape(-1).astype(F32), (0, (-p.size) % D_MODEL)) for p in pieces])
    return jnp.pad(flat, (0, rows * D_MODEL - flat.size)).reshape(rows, D_MODEL)


def _unpack_rows(pack, shapes):
    flat = pack.reshape(-1)
    out, off = [], 0
    for shp in shapes:
        size = int(np.prod(shp))
        out.append(flat[off:off + size].reshape(shp))
        off += size + (-size) % D_MODEL
    return out


IN_PADDED = IN_WIDTH + (-IN_WIDTH) % BLK


def _fox_column_blocks():
    return [(RET_W + part * 512 + p * BLK, RET_W + 384 * p + part * BLK)
            for part in range(3) for p in range(FOX_HEADS // 2)]


def _w_in_kernel_order(gathered, own, chip):
    n, R, C = gathered.shape
    tr = R // 4

    def body(chip_ref, g_ref, own_ref, wm_ref, wf_ref, full):
        for j in range(n):
            @pl.when(chip_ref[0] == j)
            def _(j=j):
                full[:, j * C:(j + 1) * C] = own_ref[...]

            @pl.when(chip_ref[0] != j)
            def _(j=j):
                full[:, j * C:(j + 1) * C] = g_ref[j]

        full[:, n * C:] = jnp.zeros((tr, IN_PADDED - n * C), BF16)
        wm_ref[:, 0:RET_W] = full[:, 0:RET_W]
        for src, dst in _fox_column_blocks():
            wm_ref[:, dst:dst + BLK] = full[:, src:src + BLK]
        wf_ref[...] = full[:, MAIN_W:MAIN_W + BLK]

    return pl.pallas_call(
        body, name="w_in_kernel_order",
        grid_spec=pltpu.PrefetchScalarGridSpec(
            num_scalar_prefetch=1, grid=(R // tr,),
            in_specs=[pl.BlockSpec((n, tr, C), lambda i, c: (0, i, 0)), pl.BlockSpec((tr, C), lambda i, c: (i, 0))],
            out_specs=[pl.BlockSpec((tr, MAIN_W), lambda i, c: (i, 0)), pl.BlockSpec((tr, BLK), lambda i, c: (i, 0))],
            scratch_shapes=[pltpu.VMEM((tr, IN_PADDED), BF16)]),
        out_shape=[jax.ShapeDtypeStruct((R, MAIN_W), BF16), jax.ShapeDtypeStruct((R, BLK), BF16)],
        compiler_params=_params(("arbitrary",)),
    )(chip, gathered, own)


def _w_in_grad_shards(g_ret, g_fox, g_ff):
    R = g_ret.shape[0]
    C = IN_WIDTH // N_CHIPS
    tr = R // 4

    def body(gr_ref, gx_ref, gf_ref, o_ref, full):
        full[:, 0:RET_W] = gr_ref[...]
        for src, dst in _fox_column_blocks():
            full[:, src:src + BLK] = gx_ref[:, dst - RET_W:dst - RET_W + BLK]
        full[:, MAIN_W:MAIN_W + FOX_HEADS] = gf_ref[...]
        for j in range(N_CHIPS):
            o_ref[j] = full[:, j * C:(j + 1) * C].astype(BF16)

    rows = lambda w: pl.BlockSpec((tr, w), lambda i: (i, 0))
    return pl.pallas_call(
        body, name="w_in_grad_shards", grid=(R // tr,),
        in_specs=[rows(RET_W), rows(FOX_W), rows(FOX_HEADS)],
        out_specs=pl.BlockSpec((N_CHIPS, tr, C), lambda i: (0, i, 0)),
        out_shape=jax.ShapeDtypeStruct((N_CHIPS, R, C), BF16),
        scratch_shapes=[pltpu.VMEM((tr, IN_PADDED), F32)],
        compiler_params=_params(("parallel",)),
    )(g_ret, g_fox, g_ff)


def kernel(x, meta_tokens, attn_norm_g, w_in, fox_forget_b, ret_norm_g, w_out, ffn_norm_g, w_up, conv_w, conv_b, w_down, final_norm_g, loss_target, m_meta_tokens, m_attn_norm_g, m_w_in, m_fox_forget_b, m_ret_norm_g, m_w_out, m_ffn_norm_g, m_w_up, m_conv_w, m_conv_b, m_w_down, m_final_norm_g, v_meta_tokens, v_attn_norm_g, v_w_in, v_fox_forget_b, v_ret_norm_g, v_w_out, v_ffn_norm_g, v_w_up, v_conv_w, v_conv_b, v_w_down, v_final_norm_g):
    chip = 2 * lax.axis_index("x") + lax.axis_index("y")
    core = lax.axis_index("c")

    small_w = _pack_rows([meta_tokens, conv_w[0]], 8)
    w_in_b = w_in[0].astype(BF16)
    g_in, g_small = _chip_allgather_halves(w_in_b, small_w)
    chip_idx = chip.reshape(1).astype(jnp.int32)
    w_main, w_ff = _w_in_kernel_order(g_in.reshape((N_CHIPS,) + w_in_b.shape), w_in_b, chip_idx)
    small_parts = [_unpack_rows(g_small[j], [meta_tokens.shape, conv_w.shape[1:]]) for j in range(N_CHIPS)]
    meta_full = jnp.concatenate([sp[0] for sp in small_parts], axis=1)
    conv_w_full = jnp.concatenate([sp[1] for sp in small_parts], axis=1)

    core_idx = core.reshape(1).astype(jnp.int32)
    place = jnp.stack([chip, core]).astype(jnp.int32)

    def assemble(gathered):
        g_out, g_up, g_down = gathered
        return g_out.reshape(D_MODEL, D_MODEL), g_up, g_down.reshape(D_FF, D_MODEL)

    def early_arrays(d_w_out, d_w_up, d_w_down):
        return [d_w_out.reshape(N_CHIPS, -1, D_MODEL), d_w_up, d_w_down.reshape(N_CHIPS, -1, D_MODEL)]

    def in_sums(d_w_ret, d_w_fox, d_w_ff):
        g_in_full = _w_in_grad_shards(d_w_ret, d_w_fox, d_w_ff)
        (from_sib,) = _sibling_halves([g_in_full])
        return [_pair_add(g_in_full, from_sib, core_idx, "pair_add_in")]

    def early_sums(early, from_sib):
        return [_pair_add(g, r, core_idx, "pair_add_" + nm) for g, r, nm in zip(early, from_sib, ("out", "up", "down"))]

    out = _local_step(x[0], loss_target[0], meta_full, attn_norm_g, w_main, w_ff, fox_forget_b, ret_norm_g,
                      None, ffn_norm_g, None, conv_w_full, conv_b, None, final_norm_g[None],
                      late=([w_out[0].astype(BF16), w_up[0].astype(BF16), w_down[0].astype(BF16)], assemble),
                      mid=(early_arrays, early_sums), last=in_sums, wire=BF16)

    names = ("in", "out", "up", "down")
    totals = [_sum_partials(s, q, place, "sum_chips_" + nm) for s, q, nm in zip(out["scatter"], out["received"], names)]
    (grad_in, grad_out, grad_up, grad_down), small_all = _sibling_allgather(totals, _small_pack(out))

    big_w = [(w_out, m_w_out, v_w_out, grad_out, "adamw_out"), (w_up, m_w_up, v_w_up, grad_up, "adamw_up"),
             (w_down, m_w_down, v_w_down, grad_down, "adamw_down")]
    big_res = [[r[None] for r in _adamw(w[0], g, m[0], v[0], nm)] for w, m, v, g, nm in big_w]
    as_rows = lambda a: jnp.transpose(a, (2, 0, 1))
    in_rows = _adamw(as_rows(w_in), grad_in.T[:, None, :], as_rows(m_w_in), as_rows(v_w_in), "adamw_in",
                     tiles=_row_vector_tiles(w_in.shape[2]))
    big_res.insert(0, [jnp.transpose(r, (1, 2, 0)) for r in in_rows])
    tap_rows = lambda a: jnp.transpose(a, (1, 0, 2))
    small_p = [meta_tokens, attn_norm_g, fox_forget_b, ret_norm_g, ffn_norm_g, tap_rows(conv_w), conv_b, final_norm_g[None]]
    small_m = [m_meta_tokens, m_attn_norm_g, m_fox_forget_b, m_ret_norm_g, m_ffn_norm_g, tap_rows(m_conv_w), m_conv_b,
               m_final_norm_g[None]]
    small_v = [v_meta_tokens, v_attn_norm_g, v_fox_forget_b, v_ret_norm_g, v_ffn_norm_g, tap_rows(v_conv_w), v_conv_b,
               v_final_norm_g[None]]
    loss_row, *small_res = _small_update(small_all, chip_idx, small_p, small_m, small_v)
    loss = loss_row[0, 0]

    def ordered(kind):
        sm = list(small_res[kind][:-1]) + [small_res[kind][-1][0]]
        sm[5] = tap_rows(sm[5])
        bg = [r[kind] for r in big_res]
        return [sm[0], sm[1], bg[0], sm[2], sm[3], bg[1], sm[4], bg[2], sm[5], sm[6], bg[3], sm[7]]

    return (loss, out["dx"][None], *ordered(0), *ordered(1), *ordered(2), *ordered(3))
```

```python
import functools

import numpy as np
import jax
import jax.numpy as jnp
from jax import lax
from jax.experimental import pallas as pl
from jax.experimental.pallas import tpu as pltpu

F32 = jnp.float32
BF16 = jnp.bfloat16

D_MODEL = 1024
N_META = 16
BLK = 128
UNIT = 2 * BLK
FOX_PAIRS = 4
RET_GROUP = 11
WIDE = 4
CHUNK = 64
N_PAD = BLK - N_META
PREFIX = BLK
RET_HEADS = 4
FOX_HEADS = 8
HEAD_LANES = 64
D_FF = 2816
ROPE_BASE = 10000.0
EPS = 1e-6
NEG = -1e30
LOG2E = 1.4426950408889634
RET_W = 1536
FOX_W = 1536
MAIN_W = RET_W + FOX_W
IN_WIDTH = MAIN_W + FOX_HEADS
N_CHIPS = 4
N_DEV = 8

ADAM_LR = 0.001
ADAM_B1 = 0.9
ADAM_B2 = 0.999
ADAM_EPS = 1e-08
ADAM_WD = 0.01
ADAM_STEP = 10

MESH = pl.DeviceIdType.MESH
VMEM_LIMIT_MB = 56

_NT = (((1,), (1,)), ((), ()))
_TN = (((0,), (0,)), ((), ()))


def _dot(a, b):
    return jnp.dot(a, b, preferred_element_type=F32)


def _dot_nt(a, b):
    return lax.dot_general(a, b, _NT, preferred_element_type=F32)


def _dot_tn(a, b):
    return lax.dot_general(a, b, _TN, preferred_element_type=F32)


def _params(dims=None, vmem_mb=VMEM_LIMIT_MB):
    kw = dict(vmem_limit_bytes=vmem_mb << 20)
    if dims is not None:
        kw["dimension_semantics"] = dims
    return pltpu.CompilerParams(**kw)


def _row_tile(n, prefs=(384, 256, 128)):
    for t in prefs:
        if n % t == 0:
            return t
    raise ValueError(f"no row tile for {n}")


def _iota(shape, dim):
    return lax.broadcasted_iota(jnp.int32, shape, dim)


def _pick_row(tile, row):
    sub = _iota(tile.shape, 0)
    return jnp.sum(jnp.where(sub == row, tile, 0.0), axis=0, keepdims=True)


def _split3(x):
    hi = x.astype(BF16)
    r1 = x - hi.astype(F32)
    mid = r1.astype(BF16)
    lo = (r1 - mid.astype(F32)).astype(BF16)
    return hi, mid, lo


def _full(shape):
    nd = len(shape)
    return pl.BlockSpec(shape, lambda *_: (0,) * nd)


def _in_perm():
    cols = list(range(RET_W))
    for p in range(FOX_HEADS // 2):
        for part in range(3):
            start = RET_W + part * 512 + p * BLK
            cols += list(range(start, start + BLK))
    return np.asarray(cols, np.int32)


def _rotary_tables(L):
    half = HEAD_LANES // 2
    inv = 1.0 / (ROPE_BASE ** (jnp.arange(half, dtype=F32) / half))
    ang = jnp.arange(L).astype(F32)[:, None] * inv[None, :]
    cos, sin = jnp.cos(ang), jnp.sin(ang)
    cos_t = jnp.tile(cos, (1, 4))
    sin_t = jnp.tile(jnp.concatenate([-sin, sin], axis=1), (1, 2))
    return cos_t, sin_t


def _decay_tables():
    gam = 1.0 - 2.0 ** (-5.0 - np.arange(RET_HEADS, dtype=np.float64))
    n = np.arange(BLK)
    same_or_past = (n[:, None] // CHUNK) >= (n[None, :] // CHUNK)
    dist = np.abs(n[:, None] - n[None, :])
    dmat = np.stack([np.where(same_or_past, g ** dist, 0.0) for g in gam]).astype(np.float32)
    lane_head = np.arange(BLK) // HEAD_LANES
    wq = np.stack([gam[2 * p + lane_head][None, :] ** (n[:, None] + 1.0) for p in range(2)]).astype(np.float32)
    wk = np.stack([gam[2 * p + lane_head][None, :] ** (BLK - 1.0 - n[:, None]) for p in range(2)]).astype(np.float32)
    g_blk = tuple(float(g ** BLK) for g in gam)
    return jnp.asarray(dmat), jnp.asarray(wq), jnp.asarray(wk), g_blk


def _shifted_blocks(tm):
    nb = tm // BLK
    return [pl.BlockSpec((BLK, D_MODEL), lambda i, j=j: (jnp.maximum(nb * i + j - 1, 0), 0)) for j in range(nb)]


def _rms_inproj(head, x, g, w_main, w_ff):
    L = x.shape[0] + BLK
    tm = _row_tile(L)
    nb = tm // BLK

    def body(head_ref, *refs):
        x_refs, (g_ref, wm_ref, wf_ref, h_ref, n_ref, p_ref, ff_ref) = refs[:nb], refs[nb:]
        parts = [r[...] for r in x_refs]
        parts[0] = jnp.where(pl.program_id(0) == 0, head_ref[...], parts[0])
        h = jnp.concatenate(parts, axis=0)
        h_ref[...] = h
        r = lax.rsqrt(jnp.mean(h * h, axis=-1, keepdims=True) + EPS)
        n = (h * r * g_ref[...]).astype(BF16)
        n_ref[...] = n
        p_ref[...] = _dot(n, wm_ref[...]).astype(BF16)
        ff_ref[...] = _dot(n, wf_ref[...])

    rows = lambda w: pl.BlockSpec((tm, w), lambda i: (i, 0))
    return pl.pallas_call(
        body, name="f_inproj", grid=(L // tm,),
        in_specs=[_full((BLK, D_MODEL))] + _shifted_blocks(tm)
        + [_full((1, D_MODEL)), _full((D_MODEL, MAIN_W)), _full((D_MODEL, BLK))],
        out_specs=[rows(D_MODEL), rows(D_MODEL), rows(MAIN_W), rows(BLK)],
        out_shape=[jax.ShapeDtypeStruct((L, D_MODEL), F32), jax.ShapeDtypeStruct((L, D_MODEL), BF16),
                   jax.ShapeDtypeStruct((L, MAIN_W), BF16), jax.ShapeDtypeStruct((L, BLK), F32)],
        compiler_params=_params(("parallel",)),
    )(head, *([x] * nb), g, w_main, w_ff)


SMALL_GROUP = 11


def _block_group(nblk, most=3):
    return next(g for g in range(most, 0, -1) if nblk % g == 0)


def _fox_prep(ff, fb):
    L = ff.shape[0]
    nblk = L // BLK
    G = _block_group(nblk, SMALL_GROUP)

    def body(ff_ref, b_ref, c_ref, ct_ref, carry):
        @pl.when(pl.program_id(0) == 0)
        def _():
            carry[...] = jnp.zeros_like(carry)

        tri = (_iota((BLK, BLK), 0) >= _iota((BLK, BLK), 1)).astype(BF16)
        live = _iota((BLK, BLK), 1) < FOX_HEADS
        run = carry[...]
        for b in range(G):
            z = ff_ref[b * BLK:(b + 1) * BLK, :] + b_ref[...]
            lf = jnp.where(live, jnp.minimum(z, 0.0) - jnp.log1p(jnp.exp(-jnp.abs(z))), 0.0)
            hi, mid, lo = _split3(lf)
            cs = (_dot(tri, hi) + _dot(tri, mid) + _dot(tri, lo) + run) * LOG2E
            c_ref[b * BLK:(b + 1) * BLK, :] = cs
            ct_ref[b] = cs.T[0:8, :]
            run = run + jnp.sum(lf, axis=0, keepdims=True)
        carry[...] = run

    return pl.pallas_call(
        body, name="f_foxprep", grid=(nblk // G,),
        in_specs=[pl.BlockSpec((G * BLK, BLK), lambda i: (i, 0)), _full((1, BLK))],
        out_specs=[pl.BlockSpec((G * BLK, BLK), lambda i: (i, 0)), pl.BlockSpec((G, 8, BLK), lambda i: (i, 0, 0))],
        out_shape=[jax.ShapeDtypeStruct((L, BLK), F32), jax.ShapeDtypeStruct((nblk, 8, BLK), F32)],
        scratch_shapes=[pltpu.VMEM((1, BLK), F32)],
        compiler_params=_params(("arbitrary",)),
    )(ff, fb)


def _rot_fns(cos, sin):
    lane = _iota((BLK, BLK), 1)
    first = (lane & (HEAD_LANES - 1)) < HEAD_LANES // 2

    def swap(x):
        return jnp.where(first, pltpu.roll(x, BLK - 32, 1), pltpu.roll(x, 32, 1))

    def rot(x):
        return x * cos + swap(x) * sin

    def rot_t(dy):
        return dy * cos + swap(dy * sin)

    return rot, rot_t


def _retention_fwd(proj, cos_t, sin_t, ret_g):
    L = proj.shape[0]
    nblk = L // BLK
    G = _block_group(nblk, RET_GROUP)
    dmat, wq_t, wk_t, g_blk = _decay_tables()

    def body(q_ref, k_ref, v_ref, gate_ref, cos_ref, sin_ref, d_ref, wq_ref, wk_ref, rg_ref,
             mix_ref, o_ref, rs_ref, state):
        @pl.when(pl.program_id(0) == 0)
        def _():
            state[...] = jnp.zeros_like(state)

        lane = _iota((BLK, BLK), 1)
        sub = _iota((BLK, BLK), 0)
        for b in range(G):
            rows = slice(b * BLK, (b + 1) * BLK)
            rot, _ = _rot_fns(cos_ref[rows, :], sin_ref[rows, :])
            for p in range(2):
                qr = rot(q_ref[rows, p * BLK:(p + 1) * BLK].astype(F32))
                kr = rot(k_ref[rows, p * BLK:(p + 1) * BLK].astype(F32)) * (HEAD_LANES ** -0.5)
                kr_b = kr.astype(BF16)
                qw = (qr * wq_ref[p]).astype(BF16)
                kw = (kr * wk_ref[p]).astype(BF16)
                for e in range(2):
                    h = 2 * p + e
                    cols = slice(h * BLK, (h + 1) * BLK)
                    qm = jnp.where((lane >> 6) == e, qr, 0.0).astype(BF16)
                    s = _dot_nt(qm, kr_b) * d_ref[h]
                    vh = v_ref[rows, cols]
                    st = state[h]
                    rs_ref[b, h] = st
                    o = _dot(s.astype(BF16), vh) + _dot(qw, st.astype(BF16))
                    u = jnp.where((sub >> 6) == e, _dot_tn(kw, vh), 0.0)
                    state[h] = g_blk[h] * st + u
                    rn = lax.rsqrt(jnp.mean(o * o, axis=-1, keepdims=True) + EPS)
                    gate = gate_ref[rows, cols].astype(F32)
                    o_ref[rows, cols] = o
                    mix_ref[rows, cols] = (o * rn * rg_ref[:, cols] * (gate * jax.nn.sigmoid(gate))).astype(BF16)

    row = lambda c: (lambda i: (i, c))
    return pl.pallas_call(
        body, name="f_retention", grid=(nblk // G,),
        in_specs=[pl.BlockSpec((G * BLK, 256), row(0)), pl.BlockSpec((G * BLK, 256), row(1)),
                  pl.BlockSpec((G * BLK, 512), row(1)), pl.BlockSpec((G * BLK, 512), row(2)),
                  pl.BlockSpec((G * BLK, BLK), row(0)), pl.BlockSpec((G * BLK, BLK), row(0)),
                  _full((RET_HEADS, BLK, BLK)), _full((2, BLK, BLK)), _full((2, BLK, BLK)), _full((1, 512))],
        out_specs=[pl.BlockSpec((G * BLK, 512), row(0)), pl.BlockSpec((G * BLK, 512), row(0)),
                   pl.BlockSpec((G, RET_HEADS, BLK, BLK), lambda i: (i, 0, 0, 0))],
        out_shape=[jax.ShapeDtypeStruct((L, 512), BF16), jax.ShapeDtypeStruct((L, 512), F32),
                   jax.ShapeDtypeStruct((nblk, RET_HEADS, BLK, BLK), F32)],
        scratch_shapes=[pltpu.VMEM((RET_HEADS, BLK, BLK), F32)],
        compiler_params=_params(("arbitrary",)),
    )(proj, proj, proj, proj, cos_t, sin_t, dmat, wq_t, wk_t, ret_g)


def _fox_units(L):
    nblk = L // BLK
    assert L % BLK == 0 and nblk % 2 == 1, "sequence must be one 128-row block plus whole 256-row tiles"
    return nblk, (nblk - 1) // 2


def _fox_tile_masks():
    sub, lane = _iota((BLK, BLK), 0), _iota((BLK, BLK), 1)
    valid = _iota((BLK, UNIT), 0) >= N_PAD
    diag = _iota((UNIT, UNIT), 0) <= _iota((UNIT, UNIT), 1)
    r, q = _iota((BLK + UNIT, UNIT), 0), _iota((BLK + UNIT, UNIT), 1)
    first_and_diag = ((r < BLK) & (r >= N_PAD)) | ((r >= BLK) & (r - BLK <= q))
    return dict(first=(sub <= lane) & (sub >= N_PAD), valid=valid, diag=diag, first_and_diag=first_and_diag)


def _fox_fwd(proj, c, ctb, gather=()):
    L = proj.shape[0]
    nblk, nu = _fox_units(L)
    scale = HEAD_LANES ** -0.5 * LOG2E
    ng = len(gather)
    steps = FOX_HEADS // (2 * FOX_PAIRS)

    def body(qkv_ref, c_ref, ct_ref, *rest):
        g_in, (of_ref, lse_ref), g_out = rest[:ng], rest[ng:ng + 2], rest[ng + 2:2 * ng + 2]
        vt, csb = rest[2 * ng + 2:2 * ng + 4]
        p = pl.program_id(0)
        heads = [(pp, e, 2 * FOX_PAIRS * p + 2 * pp + e) for pp in range(FOX_PAIRS) for e in range(2)]

        @pl.when(p == 0)
        def _():
            lse_ref[...] = jnp.zeros_like(lse_ref)
            if ng:
                local, sends, _ = _allgather_copies(g_in, g_out, *rest[2 * ng + 4:])
                for cp in local + sends:
                    cp.start()

        lane = _iota((BLK, BLK), 1)
        sub8 = _iota((8, BLK), 0)
        masks = _fox_tile_masks()

        def pre(j, carry):
            off = pl.multiple_of(j * BLK, BLK)
            ct = c_ref[pl.ds(off, BLK), :]
            for pp in range(FOX_PAIRS):
                vt[pp, j] = qkv_ref[pl.ds(off, BLK), pp * 384 + 2 * BLK:pp * 384 + 3 * BLK].astype(F32).T.astype(BF16)
            for hh, (_, _, h) in enumerate(heads):
                col = jnp.sum(jnp.where(lane == h, ct, 0.0), axis=1, keepdims=True)
                csb[hh, j] = jnp.broadcast_to(col, (BLK, BLK))
            return carry

        lax.fori_loop(0, nblk, pre, 0)

        def attend(qblk, nq, n_whole):
            qlen = nq * BLK
            qoff = pl.multiple_of(qblk * BLK, BLK)
            qlane = _iota((qlen, BLK), 1)
            qs = [qkv_ref[pl.ds(qoff, qlen), pp * 384:pp * 384 + BLK].astype(F32) * scale for pp in range(FOX_PAIRS)]
            qm = [jnp.where((qlane >> 6) == e, qs[pp], 0.0).astype(BF16) for pp, e, _ in heads]
            ct_row = [jnp.concatenate([_pick_row(ct_ref[qblk + a], h) for a in range(nq)], axis=1) for _, _, h in heads]

            def step(segs, mask, st):
                blocks = [kblk + b for kblk, nk in segs for b in range(nk)]
                kts = []
                for pp in range(FOX_PAIRS):
                    kt = [qkv_ref[pl.ds(pl.multiple_of(kblk * BLK, BLK), nk * BLK), pp * 384 + BLK:pp * 384 + 2 * BLK]
                          for kblk, nk in segs]
                    kts.append(kt[0] if len(kt) == 1 else jnp.concatenate(kt, axis=0))
                out = []
                for hh, (pp, e, _) in enumerate(heads):
                    m, l, acc = st[3 * hh:3 * hh + 3]
                    s = _dot_nt(kts[pp], qm[hh])
                    t = jnp.concatenate([s[b * BLK:(b + 1) * BLK] - jnp.concatenate([csb[hh, blk]] * nq, axis=1)
                                         for b, blk in enumerate(blocks)], axis=0)
                    if mask is not None:
                        t = jnp.where(mask, t, NEG)
                    m_new = jnp.maximum(m, jnp.max(t, axis=0, keepdims=True) + ct_row[hh])
                    alpha = jnp.exp2(m - m_new)
                    pr = jnp.exp2(t - (m_new - ct_row[hh]))
                    l = alpha * l + jnp.sum(pr, axis=0, keepdims=True)
                    pr_b = pr.astype(BF16)
                    pv = None
                    for b, blk in enumerate(blocks):
                        part = _dot(vt[pp, blk, e * HEAD_LANES:(e + 1) * HEAD_LANES, :], pr_b[b * BLK:(b + 1) * BLK])
                        pv = part if pv is None else pv + part
                    out += [m_new, l, alpha * acc + pv]
                return tuple(out)

            st = (jnp.full((1, qlen), NEG, F32), jnp.zeros((1, qlen), F32),
                  jnp.zeros((HEAD_LANES, qlen), F32)) * len(heads)
            if nq == 1:
                st = step([(0, 1)], masks["first"], st)
            else:
                st = step([(0, 1), (qblk, 2)], masks["first_and_diag"], st)
                n_wide = n_whole // WIDE
                st = lax.fori_loop(0, n_wide, lambda j, s_: step([(1 + 2 * WIDE * j, 2 * WIDE)], None, s_), st)
                rest = 1 + 2 * WIDE * n_wide
                st = lax.cond((n_whole & 2) != 0, lambda s_: step([(rest, 4)], None, s_), lambda s_: s_, st)
                st = lax.cond((n_whole & 1) != 0, lambda s_: step([(rest + 2 * (n_whole & 2), 2)], None, s_),
                              lambda s_: s_, st)
            for pp in range(FOX_PAIRS):
                lo, hi = st[6 * pp:6 * pp + 3], st[6 * pp + 3:6 * pp + 6]
                o_t = jnp.concatenate([lo[2] * (1.0 / lo[1]), hi[2] * (1.0 / hi[1])], axis=0)
                of_ref[pl.ds(qoff, qlen), pp * BLK:(pp + 1) * BLK] = o_t.T.astype(BF16)
            lse = [st[3 * hh] + jnp.log(st[3 * hh + 1]) * LOG2E for hh in range(len(heads))]
            for a in range(nq):
                upd = jnp.zeros((8, BLK), F32)
                for hh, (_, _, h) in enumerate(heads):
                    upd = upd + jnp.where(sub8 == h, lse[hh][:, a * BLK:(a + 1) * BLK], 0.0)
                lse_ref[qblk + a] = lse_ref[qblk + a] + upd

        attend(0, 1, 0)

        def q_loop(u, carry):
            attend(1 + 2 * u, 2, u)
            return carry

        lax.fori_loop(0, nu, q_loop, 0)

        if ng:
            @pl.when(p == steps - 1)
            def _():
                local, sends, recvs = _allgather_copies(g_in, g_out, *rest[2 * ng + 4:])
                for cp in recvs:
                    cp.wait_recv()
                for cp in sends:
                    cp.wait_send()
                for cp in local:
                    cp.wait()

    width = 384 * FOX_PAIRS
    return pl.pallas_call(
        body, name="f_fox", grid=(steps,),
        in_specs=[pl.BlockSpec((L, width), lambda p: (0, RET_W // width + p), pipeline_mode=pl.Buffered(1)),
                  _full((L, BLK)), _full((nblk, 8, BLK))]
        + [_ANY] * ng,
        out_specs=[pl.BlockSpec((L, FOX_PAIRS * BLK), lambda p: (0, p)), _full((nblk, 8, BLK))] + [_ANY] * ng,
        out_shape=[jax.ShapeDtypeStruct((L, 512), BF16), jax.ShapeDtypeStruct((nblk, 8, BLK), F32)]
        + [jax.ShapeDtypeStruct((N_CHIPS,) + a.shape, a.dtype) for a in gather],
        scratch_shapes=[pltpu.VMEM((FOX_PAIRS, nblk, BLK, BLK), BF16), pltpu.VMEM((2 * FOX_PAIRS, nblk, BLK, BLK), F32)]
        + _allgather_semaphores(ng),
        compiler_params=_params(("arbitrary",)),
    )(proj, c, ctb, *gather)


def _outproj_up(mix_r, o_f, h0, w_out, ffn_g, w_up, conv_w, conv_b):
    L = h0.shape[0]
    tm = _row_tile(L)
    shard = w_up.shape[2]
    assert 2 * shard == D_FF
    cw = [conv_w[j:j + 1] for j in range(3)]
    resident = lambda shape: pl.BlockSpec(shape, lambda i: (0,) * len(shape), pipeline_mode=pl.Buffered(1))

    def body(mr_ref, of_ref, h0_ref, wo_ref, g_ref, wu_ref, cw0, cw1, cw2, cb_ref,
             h1_ref, n2_ref, up_ref, act_ref, acc_ref, halo):
        i = pl.program_id(0)

        @pl.when(i == 0)
        def _():
            halo[...] = jnp.zeros_like(halo)

        h1 = h0_ref[...] + _dot(mr_ref[...], wo_ref[0:512, :]) + _dot(of_ref[...], wo_ref[512:1024, :])
        h1_ref[...] = h1
        r = lax.rsqrt(jnp.mean(h1 * h1, axis=-1, keepdims=True) + EPS)
        n2 = (h1 * r * g_ref[...]).astype(BF16)
        n2_ref[...] = n2
        live = i * tm + _iota((tm, 1), 0) >= N_PAD
        for half in range(2):
            cols = slice(half * shard, (half + 1) * shard)
            a_b = _dot(n2, wu_ref[half]).astype(BF16)
            b_b = _dot(n2, wu_ref[2 + half]).astype(BF16)
            up_ref[:, cols] = a_b
            up_ref[:, D_FF + half * shard:D_FF + (half + 1) * shard] = b_b
            a = jnp.where(live, a_b.astype(F32), 0.0)
            _, _, acc = _conv_taps(a, halo[:, cols], [cw0[:, cols], cw1[:, cols], cw2[:, cols]], cb_ref[:, cols])
            act_ref[:, cols] = (acc * jax.nn.sigmoid(acc) * b_b.astype(F32)).astype(BF16)
            acc_ref[:, cols] = acc.astype(BF16)
            halo[:, cols] = a[tm - 8:tm, :]

    rows = lambda w: pl.BlockSpec((tm, w), lambda i: (i, 0))
    return pl.pallas_call(
        body, name="f_outproj_up", grid=(L // tm,),
        in_specs=[rows(512), rows(512), rows(D_MODEL), resident((D_MODEL, D_MODEL)), _full((1, D_MODEL)),
                  resident((N_CHIPS, D_MODEL, shard)), _full((1, D_FF)), _full((1, D_FF)), _full((1, D_FF)),
                  _full((1, D_FF))],
        out_specs=[rows(D_MODEL), rows(D_MODEL), rows(2 * D_FF), rows(D_FF), rows(D_FF)],
        out_shape=[jax.ShapeDtypeStruct((L, D_MODEL), F32), jax.ShapeDtypeStruct((L, D_MODEL), BF16),
                   jax.ShapeDtypeStruct((L, 2 * D_FF), BF16), jax.ShapeDtypeStruct((L, D_FF), BF16),
                   jax.ShapeDtypeStruct((L, D_FF), BF16)],
        scratch_shapes=[pltpu.VMEM((8, D_FF), F32)],
        compiler_params=_params(("arbitrary",)),
    )(mix_r, o_f, h0, w_out, ffn_g, w_up, cw[0], cw[1], cw[2], conv_b)


def _conv_taps(a, halo, cw, cb):
    sub = _iota((a.shape[0], 1), 0)
    a1 = jnp.where(sub == 0, _pick_row(halo, 7), pltpu.roll(a, 1, 0))
    a2 = jnp.where(sub == 0, _pick_row(halo, 6), jnp.where(sub == 1, _pick_row(halo, 7), pltpu.roll(a, 2, 0)))
    acc = cb + a2 * cw[0]
    acc = acc + a1 * cw[1]
    acc = acc + a * cw[2]
    return a1, a2, acc


def _ffn_down_loss(g_act, w_down, h1, final_g, target, acc_saved, up):
    L = h1.shape[0]
    tm = _row_tile(L)
    nb = tm // BLK
    half_w = D_FF // 2

    def body(g_ref, wd_ref, h1_ref, gf_ref, acc_ref, b_ref, *refs):
        t_refs, (dh_ref, dhb_ref, dgf_ref, loss_ref, dacc_ref, db_ref) = refs[:nb], refs[nb:]
        i = pl.program_id(0)

        @pl.when(i == 0)
        def _():
            dgf_ref[...] = jnp.zeros_like(dgf_ref)
            loss_ref[...] = jnp.zeros_like(loss_ref)

        h2 = h1_ref[...] + _dot(g_ref[...], wd_ref[...])
        r = lax.rsqrt(jnp.mean(h2 * h2, axis=-1, keepdims=True) + EPS)
        yn = h2 * r
        gf = gf_ref[...]
        live = i * tm + _iota((tm, 1), 0) >= PREFIX
        target = jnp.concatenate([t[...] for t in t_refs], axis=0)
        err = jnp.where(live, yn * gf - target, 0.0)
        loss_ref[...] = loss_ref[...] + 0.5 * jnp.sum(jnp.mean(err * err, axis=-1, keepdims=True))
        dy = err * (1.0 / D_MODEL)
        dgf_ref[...] = dgf_ref[...] + jnp.sum(dy * yn, axis=0, keepdims=True)
        dyn = dy * gf
        dh = r * (dyn - yn * jnp.mean(dyn * yn, axis=-1, keepdims=True))
        dh_ref[...] = dh
        dhb = dh.astype(BF16)
        dhb_ref[...] = dhb
        for half in range(2):
            cols = slice(half * half_w, (half + 1) * half_w)
            acc = acc_ref[:, cols].astype(F32)
            dg = _dot_nt(dhb, wd_ref[cols, :])
            sg = jax.nn.sigmoid(acc)
            silu = acc * sg
            db_ref[:, cols] = (dg * silu).astype(BF16)
            dacc_ref[:, cols] = (dg * b_ref[:, cols].astype(F32) * (sg + silu * (1.0 - sg))).astype(BF16)

    rows = lambda w, c=0: pl.BlockSpec((tm, w), lambda i: (i, c))
    return pl.pallas_call(
        body, name="f_ffn_down_loss", grid=(L // tm,),
        in_specs=[rows(D_FF), pl.BlockSpec((D_FF, D_MODEL), lambda i: (0, 0), pipeline_mode=pl.Buffered(1)),
                  rows(D_MODEL), _full((1, D_MODEL)), rows(D_FF), rows(D_FF, 1)] + _shifted_blocks(tm),
        out_specs=[rows(D_MODEL), rows(D_MODEL), _full((1, D_MODEL)), _full((1, BLK)), rows(D_FF), rows(D_FF)],
        out_shape=[jax.ShapeDtypeStruct((L, D_MODEL), F32), jax.ShapeDtypeStruct((L, D_MODEL), BF16),
                   jax.ShapeDtypeStruct((1, D_MODEL), F32), jax.ShapeDtypeStruct((1, BLK), F32),
                   jax.ShapeDtypeStruct((L, D_FF), BF16), jax.ShapeDtypeStruct((L, D_FF), BF16)],
        compiler_params=_params(("arbitrary",)),
    )(g_act, w_down, h1, final_g, acc_saved, up, *([target] * nb))


def _ffn_bwd_up(dacc, db, up, conv_w, w_up, h1, ffn_g, dh2, w_out):
    L = h1.shape[0]
    tm = _row_tile(L)
    nt = L // tm
    shard = w_up.shape[2]
    cw = [conv_w[j:j + 1] for j in range(3)]

    def body(da_ref, halo_ref, db_ref, a_ref, cw0, cw1, cw2, wu_ref, h1_ref, g_ref, dh2_ref, wo_ref,
             dup_ref, dh1_ref, dh1b_ref, dmix_ref, dg_ref, dcw_ref):
        i = pl.program_id(0)

        @pl.when(i == 0)
        def _():
            dg_ref[...] = jnp.zeros_like(dg_ref)
            dcw_ref[...] = jnp.zeros_like(dcw_ref)

        sub = _iota((tm, 1), 0)
        sub8 = _iota((8, 1), 0)
        last_tile = i == nt - 1
        dbv = db_ref[...]
        dup_ref[:, D_FF:2 * D_FF] = dbv
        dn = _dot_nt(dbv[:, 0:shard], wu_ref[2]) + _dot_nt(dbv[:, shard:2 * shard], wu_ref[3])
        for half in range(2):
            cols = slice(half * shard, (half + 1) * shard)
            d0 = da_ref[:, cols].astype(F32)
            halo = jnp.where(last_tile, 0.0, halo_ref[:, cols].astype(F32))
            d1 = jnp.where(sub == tm - 1, _pick_row(halo, 0), pltpu.roll(d0, tm - 1, 0))
            d2 = jnp.where(sub == tm - 2, _pick_row(halo, 0),
                           jnp.where(sub == tm - 1, _pick_row(halo, 1), pltpu.roll(d0, tm - 2, 0)))
            a = a_ref[:, cols].astype(F32)
            upd = jnp.zeros((8, shard), F32)
            for j, t in enumerate((d2 * a, d1 * a, d0 * a, d0)):
                upd = upd + jnp.where(sub8 == j, jnp.sum(t, axis=0, keepdims=True), 0.0)
            dcw_ref[:, cols] = dcw_ref[:, cols] + upd
            da = (d0 * cw2[:, cols] + d1 * cw1[:, cols] + d2 * cw0[:, cols]).astype(BF16)
            dup_ref[:, cols] = da
            dn = dn + _dot_nt(da, wu_ref[half])
        h1 = h1_ref[...]
        r = lax.rsqrt(jnp.mean(h1 * h1, axis=-1, keepdims=True) + EPS)
        yn = h1 * r
        dg_ref[...] = dg_ref[...] + jnp.sum(dn * yn, axis=0, keepdims=True)
        dyn = dn * g_ref[...]
        dh1 = dh2_ref[...] + r * (dyn - yn * jnp.mean(dyn * yn, axis=-1, keepdims=True))
        dh1_ref[...] = dh1
        dh1b = dh1.astype(BF16)
        dh1b_ref[...] = dh1b
        dmix_ref[...] = _dot_nt(dh1b, wo_ref[...]).astype(BF16)

    rows = lambda w: pl.BlockSpec((tm, w), lambda i: (i, 0))
    halo = pl.BlockSpec((8, D_FF), lambda i: (jnp.minimum((i + 1) * (tm // 8), L // 8 - 1), 0))
    return pl.pallas_call(
        body, name="b_ffn_up", grid=(nt,),
        in_specs=[rows(D_FF), halo, rows(D_FF), rows(D_FF), _full((1, D_FF)), _full((1, D_FF)), _full((1, D_FF)),
                  _full((N_CHIPS, D_MODEL, shard)), rows(D_MODEL), _full((1, D_MODEL)), rows(D_MODEL),
                  _full((D_MODEL, D_MODEL))],
        out_specs=[rows(2 * D_FF), rows(D_MODEL), rows(D_MODEL), rows(D_MODEL), _full((1, D_MODEL)),
                   _full((8, D_FF))],
        out_shape=[jax.ShapeDtypeStruct((L, 2 * D_FF), BF16), jax.ShapeDtypeStruct((L, D_MODEL), F32),
                   jax.ShapeDtypeStruct((L, D_MODEL), BF16), jax.ShapeDtypeStruct((L, D_MODEL), BF16),
                   jax.ShapeDtypeStruct((1, D_MODEL), F32), jax.ShapeDtypeStruct((8, D_FF), F32)],
        compiler_params=_params(("arbitrary",)),
    )(dacc, dacc, db, up, cw[0], cw[1], cw[2], w_up, h1, ffn_g, dh2, w_out)


def _wgrad(a, b, name, tn=None, tk=None, out_dtype=F32):
    L, K = a.shape
    N = b.shape[1]
    tn = N if tn is None else tn
    tk = K if tk is None else tk
    tl = _row_tile(L, (1408, 768, 512, 256, 128))
    nl = L // tl

    def body(a_ref, b_ref, o_ref, acc):
        step = pl.program_id(2)

        @pl.when(step == 0)
        def _():
            acc[...] = jnp.zeros_like(acc)

        acc[...] = acc[...] + _dot_tn(a_ref[...], b_ref[...])

        @pl.when(step == nl - 1)
        def _():
            o_ref[0] = acc[...].astype(out_dtype)

    return pl.pallas_call(
        body, name=name, grid=(N // tn, K // tk, L // tl),
        in_specs=[pl.BlockSpec((tl, tk), lambda n, k, l: (l, k)), pl.BlockSpec((tl, tn), lambda n, k, l: (l, n))],
        out_specs=pl.BlockSpec((1, tk, tn), lambda n, k, l: (n, k, 0)),
        out_shape=jax.ShapeDtypeStruct((N // tn, K, tn), out_dtype),
        scratch_shapes=[pltpu.VMEM((tk, tn), F32)],
        compiler_params=_params(("parallel", "parallel", "arbitrary")),
    )(a, b)


def _retention_bwd(dmix, o, proj, cos_t, sin_t, ret_g, states, exchange=()):
    L = proj.shape[0]
    nblk = L // BLK
    G = _block_group(nblk, RET_GROUP)
    steps = nblk // G
    nx = len(exchange)
    dmat, wq_t, wk_t, g_blk = _decay_tables()

    def body(dm_ref, o_ref, q_ref, k_ref, v_ref, gate_ref, cos_ref, sin_ref, d_ref, wq_ref, wk_ref, rg_ref, rs_ref,
             *rest):
        x_in, (dp_ref, drg_ref), x_out, gstate = rest[:nx], rest[nx:nx + 2], rest[nx + 2:2 * nx + 2], rest[2 * nx + 2]

        @pl.when(pl.program_id(0) == 0)
        def _():
            if nx:
                for cp in _sibling_half_copies(x_in, x_out, *rest[2 * nx + 3:])[0]:
                    cp.start()
            gstate[...] = jnp.zeros_like(gstate)
            drg_ref[...] = jnp.zeros_like(drg_ref)

        lane = _iota((BLK, BLK), 1)
        sub = _iota((BLK, BLK), 0)
        scale = HEAD_LANES ** -0.5
        for b in reversed(range(G)):
            rows = slice(b * BLK, (b + 1) * BLK)
            rot, rot_t = _rot_fns(cos_ref[rows, :], sin_ref[rows, :])
            for p in range(2):
                qr = rot(q_ref[rows, p * BLK:(p + 1) * BLK].astype(F32))
                kr = rot(k_ref[rows, p * BLK:(p + 1) * BLK].astype(F32)) * scale
                kr_b = kr.astype(BF16)
                qw = (qr * wq_ref[p]).astype(BF16)
                kw = (kr * wk_ref[p]).astype(BF16)
                dqr = jnp.zeros((BLK, BLK), F32)
                dkr = jnp.zeros((BLK, BLK), F32)
                for e in range(2):
                    h = 2 * p + e
                    cols = slice(h * BLK, (h + 1) * BLK)
                    head_lanes = (lane >> 6) == e
                    o = o_ref[rows, cols]
                    rn = lax.rsqrt(jnp.mean(o * o, axis=-1, keepdims=True) + EPS)
                    y = o * rn
                    gate = gate_ref[rows, cols].astype(F32)
                    sg = jax.nn.sigmoid(gate)
                    dm = dm_ref[rows, cols].astype(F32)
                    rgain = rg_ref[:, cols]
                    drg_ref[:, cols] = drg_ref[:, cols] + jnp.sum(dm * y * (gate * sg), axis=0, keepdims=True)
                    dp_ref[rows, 1024 + h * BLK:1024 + (h + 1) * BLK] = (
                        dm * y * rgain * (sg * (1.0 + gate * (1.0 - sg)))).astype(BF16)
                    dy = dm * rgain * (gate * sg)
                    do = (rn * (dy - y * jnp.mean(dy * y, axis=-1, keepdims=True))).astype(BF16)
                    vh = v_ref[rows, cols]
                    qm = jnp.where(head_lanes, qr, 0.0).astype(BF16)
                    dmh = d_ref[h]
                    s = (_dot_nt(qm, kr_b) * dmh).astype(BF16)
                    ds = (_dot_nt(do, vh) * dmh).astype(BF16)
                    st = rs_ref[b, h].astype(BF16)
                    gs = gstate[h]
                    gs_b = gs.astype(BF16)
                    dqr = dqr + jnp.where(head_lanes, _dot(ds, kr_b), 0.0) + _dot_nt(do, st) * wq_ref[p]
                    dkr = dkr + _dot_tn(ds, qm) + _dot_nt(vh, gs_b) * wk_ref[p]
                    dp_ref[rows, 512 + h * BLK:512 + (h + 1) * BLK] = (_dot_tn(s, do) + _dot(kw, gs_b)).astype(BF16)
                    dr = jnp.where((sub >> 6) == e, _dot_tn(qw, do), 0.0)
                    gstate[h] = dr + g_blk[h] * gs
                dp_ref[rows, p * BLK:(p + 1) * BLK] = rot_t(dqr).astype(BF16)
                dp_ref[rows, 256 + p * BLK:256 + (p + 1) * BLK] = (rot_t(dkr) * scale).astype(BF16)

        if nx:
            @pl.when(pl.program_id(0) == steps - 1)
            def _():
                sends, recvs = _sibling_half_copies(x_in, x_out, *rest[2 * nx + 3:])
                for cp in recvs:
                    cp.wait_recv()
                for cp in sends:
                    cp.wait_send()

    row = lambda c: (lambda i: (steps - 1 - i, c))
    return pl.pallas_call(
        body, name="b_retention", grid=(steps,),
        in_specs=[pl.BlockSpec((G * BLK, 512), row(0)), pl.BlockSpec((G * BLK, 512), row(0)),
                  pl.BlockSpec((G * BLK, 256), row(0)), pl.BlockSpec((G * BLK, 256), row(1)),
                  pl.BlockSpec((G * BLK, 512), row(1)), pl.BlockSpec((G * BLK, 512), row(2)),
                  pl.BlockSpec((G * BLK, BLK), row(0)), pl.BlockSpec((G * BLK, BLK), row(0)),
                  _full((RET_HEADS, BLK, BLK)), _full((2, BLK, BLK)), _full((2, BLK, BLK)), _full((1, 512)),
                  pl.BlockSpec((G, RET_HEADS, BLK, BLK), lambda i: (steps - 1 - i, 0, 0, 0))] + [_ANY] * nx,
        out_specs=[pl.BlockSpec((G * BLK, RET_W), row(0)), _full((1, 512))] + [_ANY] * nx,
        out_shape=[jax.ShapeDtypeStruct((L, RET_W), BF16), jax.ShapeDtypeStruct((1, 512), F32)]
        + _sibling_half_shapes(exchange),
        scratch_shapes=[pltpu.VMEM((RET_HEADS, BLK, BLK), F32)] + _sibling_half_semaphores(nx),
        compiler_params=_params(("arbitrary",)),
    )(dmix, o, proj, proj, proj, proj, cos_t, sin_t, dmat, wq_t, wk_t, ret_g, states, *exchange)


def _fox_delta(dmix, o_f):
    L = o_f.shape[0]
    nblk = L // BLK
    G = _block_group(nblk, SMALL_GROUP)

    def body(do_ref, o_ref, d_ref):
        sel = ((_iota((8, 512), 1) >> 6) == _iota((8, 512), 0)).astype(BF16)
        for b in range(G):
            rows = slice(b * BLK, (b + 1) * BLK)
            prod = do_ref[rows, :].astype(F32) * o_ref[rows, :].astype(F32)
            hi = prod.astype(BF16)
            lo = (prod - hi.astype(F32)).astype(BF16)
            d_ref[b] = _dot_nt(sel, hi) + _dot_nt(sel, lo)

    return pl.pallas_call(
        body, name="b_foxdelta", grid=(nblk // G,),
        in_specs=[pl.BlockSpec((G * BLK, 512), lambda i: (i, 1)), pl.BlockSpec((G * BLK, 512), lambda i: (i, 0))],
        out_specs=pl.BlockSpec((G, 8, BLK), lambda i: (i, 0, 0)),
        out_shape=jax.ShapeDtypeStruct((nblk, 8, BLK), F32),
        compiler_params=_params(("parallel",)),
    )(dmix, o_f)


def _fox_bwd(proj, dmix, c, ctb, lse, delta, scatter=()):
    L = proj.shape[0]
    nblk, nu = _fox_units(L)
    scale = HEAD_LANES ** -0.5
    ns = len(scatter)
    steps = FOX_HEADS // (2 * FOX_PAIRS)

    def body(qkv_ref, do_ref, c_ref, ct_ref, lse_ref, dl_ref, *rest):
        s_in, (dp_ref, dc_ref, dcq_ref), s_out = rest[:ns], rest[ns:ns + 3], rest[ns + 3:2 * ns + 3]
        ktt, dqt, dk_acc, dv_acc, dcs_acc = rest[2 * ns + 3:2 * ns + 8]
        p = pl.program_id(0)
        heads = [(pp, e, 2 * FOX_PAIRS * p + 2 * pp + e) for pp in range(FOX_PAIRS) for e in range(2)]

        @pl.when(p == 0)
        def _():
            dc_ref[...] = jnp.zeros_like(dc_ref)
            dcq_ref[...] = jnp.zeros_like(dcq_ref)
            if ns:
                for cp in _scatter_copies(s_in, s_out, *rest[2 * ns + 8:]):
                    cp.start()

        sub8 = _iota((8, BLK), 0)
        masks = _fox_tile_masks()

        def pre(j, carry):
            off = pl.multiple_of(j * BLK, BLK)
            for pp in range(FOX_PAIRS):
                ktt[pp, j] = qkv_ref[pl.ds(off, BLK), pp * 384 + BLK:pp * 384 + 2 * BLK].astype(F32).T.astype(BF16)
                dqt[pp, j] = jnp.zeros((BLK, BLK), F32)
            return carry

        lax.fori_loop(0, nblk, pre, 0)

        def kv_pass(kblk, nk, n_later):
            klen = nk * BLK
            koff = pl.multiple_of(kblk * BLK, BLK)
            kt = [qkv_ref[pl.ds(koff, klen), pp * 384 + BLK:pp * 384 + 2 * BLK] for pp in range(FOX_PAIRS)]
            vtile = [qkv_ref[pl.ds(koff, klen), pp * 384 + 2 * BLK:pp * 384 + 3 * BLK] for pp in range(FOX_PAIRS)]
            ct = c_ref[pl.ds(koff, klen), :]
            klane = _iota((klen, BLK), 1)
            cs = [jnp.broadcast_to(jnp.sum(jnp.where(klane == h, ct, 0.0), axis=1, keepdims=True), (klen, WIDE * UNIT))
                  for _, _, h in heads]
            k_t = [jnp.concatenate([ktt[pp, kblk + b, e * HEAD_LANES:(e + 1) * HEAD_LANES, :] for b in range(nk)], axis=1)
                   for pp, e, _ in heads]
            for pp in range(FOX_PAIRS):
                dk_acc[pp, 0:klen] = jnp.zeros((klen, BLK), F32)
                dv_acc[pp, 0:klen] = jnp.zeros((klen, BLK), F32)
            for hh in range(len(heads)):
                dcs_acc[hh, 0:klen] = jnp.zeros((klen, BLK), F32)

            def tile(qblk, nq, mask):
                qlen = nq * BLK
                if mask == "valid":
                    mask = _iota((klen, qlen), 0) >= N_PAD
                qoff = pl.multiple_of(qblk * BLK, BLK)
                qlane = _iota((qlen, BLK), 1)
                qs = [qkv_ref[pl.ds(qoff, qlen), pp * 384:pp * 384 + BLK].astype(F32) * (scale * LOG2E)
                      for pp in range(FOX_PAIRS)]
                dot_ = [do_ref[pl.ds(qoff, qlen), pp * BLK:(pp + 1) * BLK] for pp in range(FOX_PAIRS)]
                stats = [[ref[qblk + a] for a in range(nq)] for ref in (ct_ref, lse_ref, dl_ref)]
                dcq = [jnp.zeros((8, BLK), F32) for _ in range(nq)]
                for hh, (pp, e, h) in enumerate(heads):
                    head = (qlane >> 6) == e
                    ct_row, lse_row, dl_row = [jnp.concatenate([_pick_row(t, h) for t in ts], axis=1) for ts in stats]
                    qm = jnp.where(head, qs[pp], 0.0).astype(BF16)
                    dom = jnp.where(head, dot_[pp], jnp.zeros_like(dot_[pp]))
                    t = _dot_nt(kt[pp], qm) - cs[hh][:, 0:qlen]
                    if mask is not None:
                        t = jnp.where(mask, t, NEG)
                    pr = jnp.exp2(t + (ct_row - lse_row))
                    dv_acc[pp, 0:klen] = dv_acc[pp, 0:klen] + _dot(pr.astype(BF16), dom)
                    dsv = pr * (_dot_nt(vtile[pp], dom) - dl_row)
                    ds_b = dsv.astype(BF16)
                    dk_acc[pp, 0:klen] = dk_acc[pp, 0:klen] + _dot(ds_b, qm)
                    rows = slice(e * HEAD_LANES, (e + 1) * HEAD_LANES)
                    dq_t = _dot(k_t[hh], ds_b)
                    key_side = dsv[:, 0:BLK]
                    for a in range(1, nq):
                        key_side = key_side + dsv[:, a * BLK:(a + 1) * BLK]
                    dcs_acc[hh, 0:klen] = dcs_acc[hh, 0:klen] + key_side
                    query_side = jnp.sum(dsv, axis=0, keepdims=True)
                    for a in range(nq):
                        cols = slice(a * BLK, (a + 1) * BLK)
                        dqt[pp, qblk + a, rows, :] = dqt[pp, qblk + a, rows, :] + dq_t[:, cols]
                        dcq[a] = dcq[a] + jnp.where(sub8 == h, query_side[:, cols], 0.0)
                for a in range(nq):
                    dcq_ref[qblk + a] = dcq_ref[qblk + a] + dcq[a]

            later_mask = "valid" if nk == 1 else None
            n_later = jnp.asarray(n_later, jnp.int32)
            n_wide = n_later // WIDE

            def later_wide(i, carry):
                tile(kblk + nk + 2 * WIDE * i, 2 * WIDE, later_mask)
                return carry

            tile(kblk, nk, masks["first"] if nk == 1 else masks["diag"])
            lax.fori_loop(0, n_wide, later_wide, 0)
            rest_blk = kblk + nk + 2 * WIDE * n_wide

            @pl.when((n_later & 2) != 0)
            def _():
                tile(rest_blk, 4, later_mask)

            @pl.when((n_later & 1) != 0)
            def _():
                tile(rest_blk + 2 * (n_later & 2), 2, later_mask)

            upd = jnp.zeros((klen, BLK), F32)
            for hh, (_, _, h) in enumerate(heads):
                upd = upd + jnp.where(klane == h, -jnp.sum(dcs_acc[hh, 0:klen], axis=1, keepdims=True), 0.0)
            dc_ref[pl.ds(koff, klen), :] = dc_ref[pl.ds(koff, klen), :] + upd
            for pp in range(FOX_PAIRS):
                dp_ref[pl.ds(koff, klen), pp * 384 + BLK:pp * 384 + 2 * BLK] = (
                    dk_acc[pp, 0:klen] * (1.0 / LOG2E)).astype(BF16)
                dp_ref[pl.ds(koff, klen), pp * 384 + 2 * BLK:pp * 384 + 3 * BLK] = dv_acc[pp, 0:klen].astype(BF16)

        kv_pass(0, 1, nu)

        def k_loop(u, carry):
            kv_pass(1 + 2 * u, 2, nu - 1 - u)
            return carry

        lax.fori_loop(0, nu, k_loop, 0)

        def flush(j, carry):
            off = pl.multiple_of(j * BLK, BLK)
            for pp in range(FOX_PAIRS):
                dp_ref[pl.ds(off, BLK), pp * 384:pp * 384 + BLK] = (dqt[pp, j].T * scale).astype(BF16)
            return carry

        lax.fori_loop(0, nblk, flush, 0)

        if ns:
            @pl.when(p == steps - 1)
            def _():
                copies = _scatter_copies(s_in, s_out, *rest[2 * ns + 8:])
                for cp in copies:
                    cp.wait_recv()
                for cp in copies:
                    cp.wait_send()

    width = 384 * FOX_PAIRS
    once = lambda shape, index: pl.BlockSpec(shape, index, pipeline_mode=pl.Buffered(1))
    stat = once((nblk, 8, BLK), lambda p: (0, 0, 0))
    return pl.pallas_call(
        body, name="b_fox", grid=(steps,),
        in_specs=[once((L, width), lambda p: (0, RET_W // width + p)),
                  once((L, FOX_PAIRS * BLK), lambda p: (0, 4 // FOX_PAIRS + p)),
                  once((L, BLK), lambda p: (0, 0)), stat, stat, stat] + [_ANY] * ns,
        out_specs=[once((L, width), lambda p: (0, p)), _full((L, BLK)), _full((nblk, 8, BLK))] + [_ANY] * ns,
        out_shape=[jax.ShapeDtypeStruct((L, FOX_W), BF16), jax.ShapeDtypeStruct((L, BLK), F32),
                   jax.ShapeDtypeStruct((nblk, 8, BLK), F32)] + _scatter_shapes(scatter),
        scratch_shapes=[pltpu.VMEM((FOX_PAIRS, nblk, BLK, BLK), BF16), pltpu.VMEM((FOX_PAIRS, nblk, BLK, BLK), F32),
                        pltpu.VMEM((FOX_PAIRS, UNIT, BLK), F32), pltpu.VMEM((FOX_PAIRS, UNIT, BLK), F32),
                        pltpu.VMEM((2 * FOX_PAIRS, UNIT, BLK), F32)]
        + _scatter_semaphores(ns),
        compiler_params=_params(("arbitrary",)),
    )(proj, dmix, c, ctb, lse, delta, *scatter)


def _fox_post(dc, dcq, ff, fb):
    L = dc.shape[0]
    nblk = L // BLK
    G = _block_group(nblk, SMALL_GROUP)
    steps = nblk // G

    def body(dc_ref, dcq_ref, ff_ref, b_ref, dff_ref, dffb_ref, dfb_ref, carry):
        @pl.when(pl.program_id(0) == 0)
        def _():
            carry[...] = jnp.zeros_like(carry)
            dfb_ref[...] = jnp.zeros_like(dfb_ref)

        tri = (_iota((BLK, BLK), 0) <= _iota((BLK, BLK), 1)).astype(BF16)
        live = _iota((BLK, BLK), 1) < FOX_HEADS
        run, dfb = carry[...], dfb_ref[...]
        for b in reversed(range(G)):
            rows = slice(b * BLK, (b + 1) * BLK)
            d = dc_ref[rows, :] + jnp.concatenate([dcq_ref[b], jnp.zeros((BLK - 8, BLK), F32)], axis=0).T
            hi, mid, lo = _split3(d)
            dlf = _dot(tri, hi) + _dot(tri, mid) + _dot(tri, lo) + run
            run = run + jnp.sum(d, axis=0, keepdims=True)
            z = ff_ref[rows, :] + b_ref[...]
            dff = jnp.where(live, dlf * jax.nn.sigmoid(-z), 0.0)
            dff_ref[rows, :] = dff
            dffb_ref[rows, :] = dff.astype(BF16)
            dfb = dfb + jnp.sum(dff, axis=0, keepdims=True)
        carry[...] = run
        dfb_ref[...] = dfb

    rev = lambda i: (steps - 1 - i, 0)
    return pl.pallas_call(
        body, name="b_foxpost", grid=(steps,),
        in_specs=[pl.BlockSpec((G * BLK, BLK), rev), pl.BlockSpec((G, 8, BLK), lambda i: (steps - 1 - i, 0, 0)),
                  pl.BlockSpec((G * BLK, BLK), rev), _full((1, BLK))],
        out_specs=[pl.BlockSpec((G * BLK, BLK), rev), pl.BlockSpec((G * BLK, BLK), rev), _full((1, BLK))],
        out_shape=[jax.ShapeDtypeStruct((L, BLK), F32), jax.ShapeDtypeStruct((L, BLK), BF16),
                   jax.ShapeDtypeStruct((1, BLK), F32)],
        scratch_shapes=[pltpu.VMEM((1, BLK), F32)],
        compiler_params=_params(("arbitrary",)),
    )(dc, dcq, ff, fb)


def _inproj_bwd(dpr, dpf, dffb, w_main, w_ff, h0, g, dh1, scatter=()):
    L = h0.shape[0]
    S = L - BLK
    tm = _row_tile(S, (512, 256, 128))
    nt = S // tm
    ns = len(scatter)
    operands = (dpr, dpf, dffb, h0, dh1)

    def body(*refs):
        lead, tile = refs[0:5], refs[5:10]
        wm_ref, wf_ref, g_ref = refs[10:13]
        rest = refs[13:]
        s_in, (dlead_ref, dx_ref, dg_ref), s_out = rest[:ns], rest[ns:ns + 3], rest[ns + 3:2 * ns + 3]
        i = pl.program_id(0)

        def rows_bwd(dpr_ref, dpf_ref, dff_ref, h_ref, dh1_ref):
            dn = (_dot_nt(dpr_ref[...], wm_ref[:, 0:RET_W]) + _dot_nt(dpf_ref[...], wm_ref[:, RET_W:MAIN_W])
                  + _dot_nt(dff_ref[...], wf_ref[...]))
            h = h_ref[...]
            r = lax.rsqrt(jnp.mean(h * h, axis=-1, keepdims=True) + EPS)
            yn = h * r
            dyn = dn * g_ref[...]
            dh0 = dh1_ref[...] + r * (dyn - yn * jnp.mean(dyn * yn, axis=-1, keepdims=True))
            return dh0, jnp.sum(dn * yn, axis=0, keepdims=True)

        @pl.when(i == 0)
        def _():
            if ns:
                for cp in _scatter_copies(s_in, s_out, *rest[2 * ns + 3:]):
                    cp.start()
            dlead_ref[...], dg_ref[...] = rows_bwd(*lead)

        dx_ref[...], dg_tile = rows_bwd(*tile)
        dg_ref[...] = dg_ref[...] + dg_tile

        if ns:
            @pl.when(i == nt - 1)
            def _():
                copies = _scatter_copies(s_in, s_out, *rest[2 * ns + 3:])
                for cp in copies:
                    cp.wait_recv()
                for cp in copies:
                    cp.wait_send()

    lead_spec = lambda a: pl.BlockSpec((BLK, a.shape[1]), lambda i: (0, 0))
    tile_spec = lambda a: pl.BlockSpec((pl.Element(tm), pl.Element(a.shape[1])),
                                       lambda i: (pl.multiple_of(BLK + i * tm, BLK), 0))
    return pl.pallas_call(
        body, name="b_inproj", grid=(nt,),
        in_specs=[lead_spec(a) for a in operands] + [tile_spec(a) for a in operands]
        + [_full((D_MODEL, MAIN_W)), _full((D_MODEL, BLK)), _full((1, D_MODEL))] + [_ANY] * ns,
        out_specs=[_full((BLK, D_MODEL)), pl.BlockSpec((tm, D_MODEL), lambda i: (i, 0)), _full((1, D_MODEL))]
        + [_ANY] * ns,
        out_shape=[jax.ShapeDtypeStruct((BLK, D_MODEL), F32), jax.ShapeDtypeStruct((S, D_MODEL), F32),
                   jax.ShapeDtypeStruct((1, D_MODEL), F32)] + _scatter_shapes(scatter),
        scratch_shapes=_scatter_semaphores(ns),
        compiler_params=_params(("arbitrary",)),
    )(*operands, *operands, w_main, w_ff, g, *scatter)


def _local_step(x, target, meta, attn_g, w_main, w_ff, fox_b, ret_g, w_out, ffn_g, w_up, conv_w, conv_b, w_down, final_g,
                late=None, mid=None, last=None, wire=F32):
    S = x.shape[0]
    L = S + PREFIX
    head = jnp.concatenate([jnp.zeros((N_PAD, D_MODEL), F32), meta], axis=0)
    fb = jnp.pad(fox_b, ((0, 0), (0, BLK - FOX_HEADS)))
    cos_t, sin_t = _rotary_tables(L)

    h0, n1, proj, ff = _rms_inproj(head, x, attn_g, w_main, w_ff)
    c, ctb = _fox_prep(ff, fb)
    mix_r, o_ret, states = _retention_fwd(proj, cos_t, sin_t, ret_g)
    if late is None:
        o_f, lse = _fox_fwd(proj, c, ctb)
    else:
        o_f, lse, *gathered = _fox_fwd(proj, c, ctb, gather=late[0])
        w_out, w_up, w_down = late[1](gathered)
    h1, n2, up, g_act, acc_saved = _outproj_up(mix_r, o_f, h0, w_out, ffn_g, w_up, conv_w, conv_b)
    dh2, dh2b, d_final_g, loss, dacc, db = _ffn_down_loss(g_act, w_down, h1, final_g, target, acc_saved, up)

    dup, dh1, dh1b, dmix, d_ffn_g, dconv = _ffn_bwd_up(dacc, db, up, conv_w, w_up, h1, ffn_g, dh2, w_out)
    d_w_down = _wgrad(g_act, dh2b, "wgrad_down", tk=D_FF // 2, out_dtype=wire)[0]
    d_w_up = _wgrad(n2, dup, "wgrad_up", tn=w_up.shape[2], out_dtype=wire)
    d_w_out = jnp.concatenate([_wgrad(mix_r, dh1b, "wgrad_out_r", out_dtype=wire)[0],
                               _wgrad(o_f, dh1b, "wgrad_out_f", out_dtype=wire)[0]], axis=0)

    early = () if mid is None else mid[0](d_w_out, d_w_up, d_w_down)
    dpr, d_ret_g, *from_sibling = _retention_bwd(dmix, o_ret, proj, cos_t, sin_t, ret_g, states, exchange=early)
    delta = _fox_delta(dmix, o_f)
    scatter = () if mid is None else mid[1](early, from_sibling)
    dpf, dc, dcq, *received = _fox_bwd(proj, dmix, c, ctb, lse, delta, scatter=scatter)
    dff, dffb, d_fox_b = _fox_post(dc, dcq, ff, fb)
    d_w_ret, d_w_fox = _wgrad(n1, dpr, "wgrad_in_r")[0], _wgrad(n1, dpf, "wgrad_in_f")[0]
    d_w_ff = _wgrad(n1, dffb, "wgrad_in_ff")[0][:, :FOX_HEADS]
    scatter_in = () if last is None else last(d_w_ret, d_w_fox, d_w_ff)
    dlead, dx, d_attn_g, *received_in = _inproj_bwd(dpr, dpf, dffb, w_main, w_ff, h0, attn_g, dh1, scatter=scatter_in)

    return dict(
        loss=loss[0, 0], dx=dx, dmeta=dlead[N_PAD:], attn_g=d_attn_g, w_main=jnp.concatenate([d_w_ret, d_w_fox], axis=1),
        w_ff=d_w_ff, fox_b=d_fox_b[:, :FOX_HEADS], ret_g=d_ret_g, w_out=d_w_out, ffn_g=d_ffn_g,
        w_up=d_w_up, conv_w=dconv[0:3], conv_b=dconv[3:4], w_down=d_w_down, final_g=d_final_g,
        scatter=list(scatter_in) + list(scatter), received=list(received_in) + list(received))


_ANY = pl.BlockSpec(memory_space=pl.ANY)


def _place():
    return lax.axis_index("x"), lax.axis_index("y"), lax.axis_index("c")


def _other_chips(x, y):
    return [(1 - x, y), (x, 1 - y), (1 - x, 1 - y)]


def _allgather_semaphores(n):
    if n == 0:
        return []
    return [pltpu.SemaphoreType.DMA((3 * n,)), pltpu.SemaphoreType.DMA((3 * n,)), pltpu.SemaphoreType.DMA((n,))]


def _allgather_copies(ins, outs, send, recv, loc):
    n = len(ins)
    x, y, c = _place()
    mine = 2 * x + y
    peers = _other_chips(x, y)

    def remote(a, k, slot):
        return pltpu.make_async_remote_copy(
            src_ref=ins[a], dst_ref=outs[a].at[slot], send_sem=send.at[3 * a + k], recv_sem=recv.at[3 * a + k],
            device_id=(peers[k][0], peers[k][1], c), device_id_type=MESH)

    local = [pltpu.make_async_copy(ins[a], outs[a].at[mine], loc.at[a]) for a in range(n)]
    sends = [remote(a, k, mine) for a in range(n) for k in range(3)]
    recvs = [remote(a, k, 2 * peers[k][0] + peers[k][1]) for a in range(n) for k in range(3)]
    return local, sends, recvs


def _chip_allgather_halves(w, small):
    half = w.shape[0] // 2

    def body(w_ref, s_ref, wo_ref, so_ref, send, recv, fsend, frecv, ssend, srecv, loc):
        x, y, c = _place()
        mine = 2 * x + y
        peers = _other_chips(x, y)

        def fetch(k, slot):
            return pltpu.make_async_remote_copy(
                src_ref=w_ref.at[pl.ds(c * half, half)], dst_ref=wo_ref.at[slot, c], send_sem=send.at[k],
                recv_sem=recv.at[k], device_id=(peers[k][0], peers[k][1], c), device_id_type=MESH)

        def forward(k, which):
            slot = 2 * peers[k][0] + peers[k][1]
            return pltpu.make_async_remote_copy(
                src_ref=wo_ref.at[slot, which], dst_ref=wo_ref.at[slot, which], send_sem=fsend.at[k],
                recv_sem=frecv.at[k], device_id=(x, y, 1 - c), device_id_type=MESH)

        def small_copy(k, slot):
            return pltpu.make_async_remote_copy(
                src_ref=s_ref, dst_ref=so_ref.at[slot], send_sem=ssend.at[k], recv_sem=srecv.at[k],
                device_id=(peers[k][0], peers[k][1], c), device_id_type=MESH)

        local = pltpu.make_async_copy(s_ref, so_ref.at[mine], loc.at[0])
        sends = [fetch(k, mine) for k in range(3)] + [small_copy(k, mine) for k in range(3)]
        local.start()
        for cp in sends:
            cp.start()
        forwards = []
        for k in range(3):
            fetch(k, 2 * peers[k][0] + peers[k][1]).wait_recv()
            forwards.append(forward(k, c))
            forwards[-1].start()
        for k in range(3):
            forward(k, 1 - c).wait_recv()
            small_copy(k, 2 * peers[k][0] + peers[k][1]).wait_recv()
        for cp in sends + forwards:
            cp.wait_send()
        local.wait()

    three = pltpu.SemaphoreType.DMA((3,))
    return pl.pallas_call(
        body, name="ag_weights", in_specs=[_ANY] * 2, out_specs=[_ANY] * 2,
        out_shape=[jax.ShapeDtypeStruct((N_CHIPS, 2, half, w.shape[1]), w.dtype),
                   jax.ShapeDtypeStruct((N_CHIPS,) + small.shape, small.dtype)],
        scratch_shapes=[three, three, three, three, three, three, pltpu.SemaphoreType.DMA((1,))],
    )(w, small)


def _chip_allgather(arrays):
    n = len(arrays)

    def body(*refs):
        local, sends, recvs = _allgather_copies(refs[:n], refs[n:2 * n], *refs[2 * n:])
        for cp in local + sends:
            cp.start()
        for cp in recvs:
            cp.wait_recv()
        for cp in sends:
            cp.wait_send()
        for cp in local:
            cp.wait()

    return pl.pallas_call(
        body, name="ag_weights", in_specs=[_ANY] * n, out_specs=[_ANY] * n,
        out_shape=[jax.ShapeDtypeStruct((N_CHIPS,) + a.shape, a.dtype) for a in arrays],
        scratch_shapes=_allgather_semaphores(n),
    )(*arrays)


def _sibling_halves(grads):
    n = len(grads)

    def body(*refs):
        sends, recvs = _sibling_half_copies(refs[:n], refs[n:2 * n], *refs[2 * n:])
        for cp in sends:
            cp.start()
        for cp in recvs:
            cp.wait_recv()
        for cp in sends:
            cp.wait_send()

    return pl.pallas_call(
        body, name="rs_sibling", in_specs=[_ANY] * n, out_specs=[_ANY] * n,
        out_shape=_sibling_half_shapes(grads), scratch_shapes=_sibling_half_semaphores(n),
    )(*grads)


def _sibling_half_shapes(grads):
    return [jax.ShapeDtypeStruct((N_CHIPS, g.shape[1] // 2, g.shape[2]), g.dtype) for g in grads]


def _sibling_half_semaphores(n):
    return [pltpu.SemaphoreType.DMA((n,)), pltpu.SemaphoreType.DMA((n,))] if n else []


def _sibling_half_copies(ins, outs, send, recv):
    x, y, c = _place()

    def half_copy(a, which):
        half = ins[a].shape[1] // 2
        return pltpu.make_async_remote_copy(
            src_ref=ins[a].at[pl.ds(0, N_CHIPS), pl.ds(which * half, half)], dst_ref=outs[a],
            send_sem=send.at[a], recv_sem=recv.at[a], device_id=(x, y, 1 - c), device_id_type=MESH)

    return [half_copy(a, 1 - c) for a in range(len(ins))], [half_copy(a, c) for a in range(len(ins))]


def _scatter_shapes(parts):
    return [jax.ShapeDtypeStruct((3,) + p.shape[1:], p.dtype) for p in parts]


def _scatter_semaphores(n):
    return [pltpu.SemaphoreType.DMA((3 * n,)), pltpu.SemaphoreType.DMA((3 * n,))] if n else []


def _scatter_copies(ins, outs, send, recv):
    x, y, c = _place()
    peers = _other_chips(x, y)
    return [pltpu.make_async_remote_copy(
        src_ref=ins[a].at[2 * peers[k][0] + peers[k][1]], dst_ref=outs[a].at[k], send_sem=send.at[3 * a + k],
        recv_sem=recv.at[3 * a + k], device_id=(peers[k][0], peers[k][1], c), device_id_type=MESH)
        for a in range(len(ins)) for k in range(3)]


def _sibling_allgather(bufs, small):
    n = len(bufs)

    def body(*refs):
        small_in, outs, small_out = refs[n], refs[n + 1:2 * n + 1], refs[2 * n + 1]
        send, recv, s_send, s_recv, loc = refs[2 * n + 2:]
        x, y, c = _place()
        me = 4 * x + 2 * y + c

        def remote(a, which):
            return pltpu.make_async_remote_copy(
                src_ref=outs[a].at[which], dst_ref=outs[a].at[which], send_sem=send.at[a], recv_sem=recv.at[a],
                device_id=(x, y, 1 - c), device_id_type=MESH)

        def peer_of(r):
            return tuple(1 - v if (r >> b) & 1 else v for v, b in ((x, 2), (y, 1), (c, 0)))

        def small_copy(r, slot):
            return pltpu.make_async_remote_copy(
                src_ref=small_in, dst_ref=small_out.at[slot], send_sem=s_send.at[r - 1], recv_sem=s_recv.at[r - 1],
                device_id=peer_of(r), device_id_type=MESH)

        local = pltpu.make_async_copy(small_in, small_out.at[me], loc.at[0])
        sends = [remote(a, c) for a in range(n)] + [small_copy(r, me) for r in range(1, N_DEV)]
        local.start()
        for cp in sends:
            cp.start()
        for r in range(1, N_DEV):
            px, py, pc = peer_of(r)
            small_copy(r, 4 * px + 2 * py + pc).wait_recv()
        for a in range(n):
            remote(a, 1 - c).wait_recv()
        for cp in sends:
            cp.wait_send()
        local.wait()

    outs = pl.pallas_call(
        body, name="ag_sibling", in_specs=[_ANY] * (n + 1), out_specs=[_ANY] * (n + 1),
        out_shape=[jax.ShapeDtypeStruct(b.shape, b.dtype) for b in bufs]
        + [jax.ShapeDtypeStruct((N_DEV,) + small.shape, small.dtype)],
        input_output_aliases={a: a for a in range(n)},
        scratch_shapes=[pltpu.SemaphoreType.DMA((n,)), pltpu.SemaphoreType.DMA((n,)),
                        pltpu.SemaphoreType.DMA((N_DEV - 1,)), pltpu.SemaphoreType.DMA((N_DEV - 1,)),
                        pltpu.SemaphoreType.DMA((1,))],
    )(*bufs, small)
    return [o.reshape(2 * o.shape[1], o.shape[2]) for o in outs[:n]], outs[n]


def _pair_add(full, recv, core, name):
    _, R, C = full.shape
    half = R // 2

    def body(core_ref, a_ref, b_ref, o_ref):
        o_ref[...] = (a_ref[...].astype(F32) + b_ref[...].astype(F32)).astype(BF16)

    return pl.pallas_call(
        body, name=name,
        grid_spec=pltpu.PrefetchScalarGridSpec(
            num_scalar_prefetch=1, grid=(N_CHIPS,),
            in_specs=[pl.BlockSpec((1, half, C), lambda j, core_ref: (j, core_ref[0], 0)),
                      pl.BlockSpec((1, half, C), lambda j, core_ref: (j, 0, 0))],
            out_specs=pl.BlockSpec((1, half, C), lambda j, core_ref: (j, 0, 0))),
        out_shape=jax.ShapeDtypeStruct((N_CHIPS, half, C), BF16),
        compiler_params=_params(("parallel",)),
    )(core, full, recv)


def _sum_partials(own_all, recv, place, name, tiles=2):
    _, R, C = own_all.shape
    tr = R // tiles

    def body(place_ref, own_ref, r_ref, o_ref):
        acc = own_ref[0].astype(F32)
        for k in range(3):
            acc = acc + r_ref[k].astype(F32)
        o_ref[0] = acc

    return pl.pallas_call(
        body, name=name,
        grid_spec=pltpu.PrefetchScalarGridSpec(
            num_scalar_prefetch=1, grid=(tiles,),
            in_specs=[pl.BlockSpec((1, tr, C), lambda i, place_ref: (place_ref[0], i, 0)),
                      pl.BlockSpec((3, tr, C), lambda i, place_ref: (0, i, 0))],
            out_specs=pl.BlockSpec((1, tr, C), lambda i, place_ref: (place_ref[1], i, 0))),
        out_shape=jax.ShapeDtypeStruct((2, R, C), F32),
        compiler_params=_params(("parallel",)),
    )(place, own_all, recv)


def _adamw_math(w, g, m, v):
    m2 = ADAM_B1 * m + (1.0 - ADAM_B1) * g
    v2 = ADAM_B2 * v + (1.0 - ADAM_B2) * (g * g)
    m_hat = m2 / (1.0 - ADAM_B1 ** ADAM_STEP)
    v_hat = v2 / (1.0 - ADAM_B2 ** ADAM_STEP)
    return -ADAM_LR * (m_hat / (jnp.sqrt(v_hat) + ADAM_EPS) + ADAM_WD * w), m2, v2


ROW_ATTN_G, ROW_FFN_G, ROW_FINAL_G, ROW_MISC, ROW_CONV_B, ROW_CONV_W, ROW_META, SMALL_ROWS = 0, 1, 2, 3, 4, 8, 24, 40
MISC_FOX_B, MISC_LOSS = 512, 640


def _small_pack(out):
    def rows(a, n):
        a = a.astype(F32)
        return jnp.pad(a, ((0, n - a.shape[0]), (0, D_MODEL - a.shape[1])))

    misc = jnp.concatenate([out["ret_g"], out["fox_b"], jnp.zeros((1, MISC_LOSS - MISC_FOX_B - FOX_HEADS), F32),
                            out["loss"].reshape(1, 1)], axis=1)
    conv_b = jnp.pad(out["conv_b"], ((0, 0), (0, (-D_FF) % D_MODEL))).reshape(-1, D_MODEL)
    conv_w = out["conv_w"].reshape(3, N_CHIPS, -1).transpose(1, 0, 2).reshape(3 * N_CHIPS, -1)
    return jnp.concatenate([
        rows(out["attn_g"], 1), rows(out["ffn_g"], 1), rows(out["final_g"], 1), rows(misc, 1),
        rows(conv_b, ROW_CONV_W - ROW_CONV_B), rows(conv_w, ROW_META - ROW_CONV_W), rows(out["dmeta"], N_META)], axis=0)


def _small_update(packs, chip, ws, ms, vs):
    n = len(ws)
    meta_w, conv_sw = ws[0].shape[1], ws[5].shape[2]
    assert packs.shape == (N_DEV, SMALL_ROWS, D_MODEL) and ws[0].shape[0] == N_META and ws[5].shape[:2] == (3, 1)

    def body(chip_ref, p_ref, *refs):
        w_refs, m_refs, v_refs = refs[:n], refs[n:2 * n], refs[2 * n:3 * n]
        loss_ref, out_refs, tot = refs[3 * n], refs[3 * n + 1:7 * n + 1], refs[7 * n + 1]
        acc = p_ref[0]
        for d in range(1, N_DEV):
            acc = acc + p_ref[d]
        tot[...] = acc

        def of_chip(pieces):
            val = pieces[-1]
            for j in range(N_CHIPS - 2, -1, -1):
                val = jnp.where(chip_ref[0] == j, pieces[j], val)
            return val

        row = lambda r, lo=0, hi=D_MODEL: tot[r:r + 1, lo:hi]
        grads = [
            of_chip([tot[ROW_META:ROW_META + N_META, j * meta_w:(j + 1) * meta_w] for j in range(N_CHIPS)]),
            row(ROW_ATTN_G), row(ROW_MISC, MISC_FOX_B, MISC_FOX_B + FOX_HEADS), row(ROW_MISC, 0, MISC_FOX_B),
            row(ROW_FFN_G),
            of_chip([tot[ROW_CONV_W + 3 * j:ROW_CONV_W + 3 * j + 3, 0:conv_sw] for j in range(N_CHIPS)]),
            jnp.concatenate([row(ROW_CONV_B), row(ROW_CONV_B + 1), row(ROW_CONV_B + 2, 0, D_FF - 2 * D_MODEL)], axis=1),
            row(ROW_FINAL_G)]
        loss_ref[...] = row(ROW_MISC, MISC_LOSS, MISC_LOSS + BLK)
        for k in range(n):
            parts = [((Ellipsis,), grads[k])]
            if len(ws[k].shape) == 3:
                parts = [((t,), grads[k][t:t + 1]) for t in range(ws[k].shape[0])]
            for at, g in parts:
                res = (g,) + _adamw_math(w_refs[k][at], g, m_refs[k][at], v_refs[k][at])
                for kind in range(4):
                    out_refs[kind * n + k][at] = res[kind]

    res = pl.pallas_call(
        body, name="small_update",
        grid_spec=pltpu.PrefetchScalarGridSpec(
            num_scalar_prefetch=1, grid=(1,),
            in_specs=[_full(packs.shape)] + [_full(a.shape) for a in list(ws) * 3],
            out_specs=[_full((1, BLK))] + [_full(a.shape) for a in list(ws) * 4],
            scratch_shapes=[pltpu.VMEM((SMALL_ROWS, D_MODEL), F32)]),
        out_shape=[jax.ShapeDtypeStruct((1, BLK), F32)] + [jax.ShapeDtypeStruct(a.shape, F32) for a in list(ws) * 4],
        compiler_params=_params(("arbitrary",)),
    )(chip, packs, *ws, *ms, *vs)
    return res[0], res[1:n + 1], res[n + 1:2 * n + 1], res[2 * n + 1:3 * n + 1], res[3 * n + 1:]


def _adamw(w, g, m, v, name, tiles=4):
    R, tail = w.shape[0], w.shape[1:]
    assert R % tiles == 0
    tr = R // tiles

    def body(w_ref, g_ref, m_ref, v_ref, go_ref, d_ref, m2_ref, v2_ref):
        g_ = g_ref[...]
        go_ref[...] = g_
        d_ref[...], m2_ref[...], v2_ref[...] = _adamw_math(w_ref[...], g_, m_ref[...], v_ref[...])

    spec = pl.BlockSpec((tr,) + tail, lambda i: (i,) + (0,) * len(tail))
    return pl.pallas_call(
        body, name=name, grid=(tiles,), in_specs=[spec] * 4, out_specs=[spec] * 4,
        out_shape=[jax.ShapeDtypeStruct(w.shape, F32)] * 4,
        compiler_params=_params(("parallel",)),
    )(w, g, m, v)


def _row_vector_tiles(n, most=80):
    return next(t for t in range(1, n + 1) if n % t == 0 and n // t <= most)


def _pack_rows(pieces, rows):
    flat = jnp.concatenate([jnp.pad(p.reshape(-1).astype(F32), (0, (-p.size) % D_MODEL)) for p in pieces])
    return jnp.pad(flat, (0, rows * D_MODEL - flat.size)).reshape(rows, D_MODEL)


def _unpack_rows(pack, shapes):
    flat = pack.reshape(-1)
    out, off = [], 0
    for shp in shapes:
        size = int(np.prod(shp))
        out.append(flat[off:off + size].reshape(shp))
        off += size + (-size) % D_MODEL
    return out


IN_PADDED = IN_WIDTH + (-IN_WIDTH) % BLK


def _fox_column_blocks():
    return [(RET_W + part * 512 + p * BLK, RET_W + 384 * p + part * BLK)
            for part in range(3) for p in range(FOX_HEADS // 2)]


def _w_in_kernel_order(gathered, own, chip):
    n, R, C = gathered.shape
    tr = R // 4

    def body(chip_ref, g_ref, own_ref, wm_ref, wf_ref, full):
        for j in range(n):
            @pl.when(chip_ref[0] == j)
            def _(j=j):
                full[:, j * C:(j + 1) * C] = own_ref[...]

            @pl.when(chip_ref[0] != j)
            def _(j=j):
                full[:, j * C:(j + 1) * C] = g_ref[j]

        full[:, n * C:] = jnp.zeros((tr, IN_PADDED - n * C), BF16)
        wm_ref[:, 0:RET_W] = full[:, 0:RET_W]
        for src, dst in _fox_column_blocks():
            wm_ref[:, dst:dst + BLK] = full[:, src:src + BLK]
        wf_ref[...] = full[:, MAIN_W:MAIN_W + BLK]

    return pl.pallas_call(
        body, name="w_in_kernel_order",
        grid_spec=pltpu.PrefetchScalarGridSpec(
            num_scalar_prefetch=1, grid=(R // tr,),
            in_specs=[pl.BlockSpec((n, tr, C), lambda i, c: (0, i, 0)), pl.BlockSpec((tr, C), lambda i, c: (i, 0))],
            out_specs=[pl.BlockSpec((tr, MAIN_W), lambda i, c: (i, 0)), pl.BlockSpec((tr, BLK), lambda i, c: (i, 0))],
            scratch_shapes=[pltpu.VMEM((tr, IN_PADDED), BF16)]),
        out_shape=[jax.ShapeDtypeStruct((R, MAIN_W), BF16), jax.ShapeDtypeStruct((R, BLK), BF16)],
        compiler_params=_params(("arbitrary",)),
    )(chip, gathered, own)


def _w_in_grad_shards(g_ret, g_fox, g_ff):
    R = g_ret.shape[0]
    C = IN_WIDTH // N_CHIPS
    tr = R // 4

    def body(gr_ref, gx_ref, gf_ref, o_ref, full):
        full[:, 0:RET_W] = gr_ref[...]
        for src, dst in _fox_column_blocks():
            full[:, src:src + BLK] = gx_ref[:, dst - RET_W:dst - RET_W + BLK]
        full[:, MAIN_W:MAIN_W + FOX_HEADS] = gf_ref[...]
        for j in range(N_CHIPS):
            o_ref[j] = full[:, j * C:(j + 1) * C].astype(BF16)

    rows = lambda w: pl.BlockSpec((tr, w), lambda i: (i, 0))
    return pl.pallas_call(
        body, name="w_in_grad_shards", grid=(R // tr,),
        in_specs=[rows(RET_W), rows(FOX_W), rows(FOX_HEADS)],
        out_specs=pl.BlockSpec((N_CHIPS, tr, C), lambda i: (0, i, 0)),
        out_shape=jax.ShapeDtypeStruct((N_CHIPS, R, C), BF16),
        scratch_shapes=[pltpu.VMEM((tr, IN_PADDED), F32)],
        compiler_params=_params(("parallel",)),
    )(g_ret, g_fox, g_ff)


def kernel(x, meta_tokens, attn_norm_g, w_in, fox_forget_b, ret_norm_g, w_out, ffn_norm_g, w_up, conv_w, conv_b, w_down, final_norm_g, loss_target, m_meta_tokens, m_attn_norm_g, m_w_in, m_fox_forget_b, m_ret_norm_g, m_w_out, m_ffn_norm_g, m_w_up, m_conv_w, m_conv_b, m_w_down, m_final_norm_g, v_meta_tokens, v_attn_norm_g, v_w_in, v_fox_forget_b, v_ret_norm_g, v_w_out, v_ffn_norm_g, v_w_up, v_conv_w, v_conv_b, v_w_down, v_final_norm_g):
    chip = 2 * lax.axis_index("x") + lax.axis_index("y")
    core = lax.axis_index("c")

    small_w = _pack_rows([meta_tokens, conv_w[0]], 8)
    w_in_b = w_in[0].astype(BF16)
    g_in, g_small = _chip_allgather_halves(w_in_b, small_w)
    chip_idx = chip.reshape(1).astype(jnp.int32)
    w_main, w_ff = _w_in_kernel_order(g_in.reshape((N_CHIPS,) + w_in_b.shape), w_in_b, chip_idx)
    small_parts = [_unpack_rows(g_small[j], [meta_tokens.shape, conv_w.shape[1:]]) for j in range(N_CHIPS)]
    meta_full = jnp.concatenate([sp[0] for sp in small_parts], axis=1)
    conv_w_full = jnp.concatenate([sp[1] for sp in small_parts], axis=1)

    core_idx = core.reshape(1).astype(jnp.int32)
    place = jnp.stack([chip, core]).astype(jnp.int32)

    def assemble(gathered):
        g_out, g_up, g_down = gathered
        return g_out.reshape(D_MODEL, D_MODEL), g_up, g_down.reshape(D_FF, D_MODEL)

    def early_arrays(d_w_out, d_w_up, d_w_down):
        return [d_w_out.reshape(N_CHIPS, -1, D_MODEL), d_w_up, d_w_down.reshape(N_CHIPS, -1, D_MODEL)]

    def in_sums(d_w_ret, d_w_fox, d_w_ff):
        g_in_full = _w_in_grad_shards(d_w_ret, d_w_fox, d_w_ff)
        (from_sib,) = _sibling_halves([g_in_full])
        return [_pair_add(g_in_full, from_sib, core_idx, "pair_add_in")]

    def early_sums(early, from_sib):
        return [_pair_add(g, r, core_idx, "pair_add_" + nm) for g, r, nm in zip(early, from_sib, ("out", "up", "down"))]

    out = _local_step(x[0], loss_target[0], meta_full, attn_norm_g, w_main, w_ff, fox_forget_b, ret_norm_g,
                      None, ffn_norm_g, None, conv_w_full, conv_b, None, final_norm_g[None],
                      late=([w_out[0].astype(BF16), w_up[0].astype(BF16), w_down[0].astype(BF16)], assemble),
                      mid=(early_arrays, early_sums), last=in_sums, wire=BF16)

    names = ("in", "out", "up", "down")
    totals = [_sum_partials(s, q, place, "sum_chips_" + nm) for s, q, nm in zip(out["scatter"], out["received"], names)]
    (grad_in, grad_out, grad_up, grad_down), small_all = _sibling_allgather(totals, _small_pack(out))

    big_w = [(w_out, m_w_out, v_w_out, grad_out, "adamw_out"), (w_up, m_w_up, v_w_up, grad_up, "adamw_up"),
             (w_down, m_w_down, v_w_down, grad_down, "adamw_down")]
    big_res = [[r[None] for r in _adamw(w[0], g, m[0], v[0], nm)] for w, m, v, g, nm in big_w]
    as_rows = lambda a: jnp.transpose(a, (2, 0, 1))
    in_rows = _adamw(as_rows(w_in), grad_in.T[:, None, :], as_rows(m_w_in), as_rows(v_w_in), "adamw_in",
                     tiles=_row_vector_tiles(w_in.shape[2]))
    big_res.insert(0, [jnp.transpose(r, (1, 2, 0)) for r in in_rows])
    tap_rows = lambda a: jnp.transpose(a, (1, 0, 2))
    small_p = [meta_tokens, attn_norm_g, fox_forget_b, ret_norm_g, ffn_norm_g, tap_rows(conv_w), conv_b, final_norm_g[None]]
    small_m = [m_meta_tokens, m_attn_norm_g, m_fox_forget_b, m_ret_norm_g, m_ffn_norm_g, tap_rows(m_conv_w), m_conv_b,
               m_final_norm_g[None]]
    small_v = [v_meta_tokens, v_attn_norm_g, v_fox_forget_b, v_ret_norm_g, v_ffn_norm_g, tap_rows(v_conv_w), v_conv_b,
               v_final_norm_g[None]]
    loss_row, *small_res = _small_update(small_all, chip_idx, small_p, small_m, small_v)
    loss = loss_row[0, 0]

    def ordered(kind):
        sm = list(small_res[kind][:-1]) + [small_res[kind][-1][0]]
        sm[5] = tap_rows(sm[5])
        bg = [r[kind] for r in big_res]
        return [sm[0], sm[1], bg[0], sm[2], sm[3], bg[1], sm[4], bg[2], sm[5], sm[6], bg[3], sm[7]]

    return (loss, out["dx"][None], *ordered(0), *ordered(1), *ordered(2), *ordered(3))
```

```python
import functools

import numpy as np
import jax
import jax.numpy as jnp
from jax import lax
from jax.experimental import pallas as pl
from jax.experimental.pallas import tpu as pltpu

F32 = jnp.float32
BF16 = jnp.bfloat16

D_MODEL = 1024
N_META = 16
BLK = 128
UNIT = 2 * BLK
FOX_PAIRS = 4
RET_GROUP = 3
RING = 3
WIDE = 4
CHUNK = 64
N_PAD = BLK - N_META
PREFIX = BLK
RET_HEADS = 4
FOX_HEADS = 8
HEAD_LANES = 64
D_FF = 2816
ROPE_BASE = 10000.0
EPS = 1e-6
NEG = -1e30
LOG2E = 1.4426950408889634
RET_W = 1536
FOX_W = 1536
MAIN_W = RET_W + FOX_W
IN_WIDTH = MAIN_W + FOX_HEADS
N_CHIPS = 4
N_DEV = 8

ADAM_LR = 0.001
ADAM_B1 = 0.9
ADAM_B2 = 0.999
ADAM_EPS = 1e-08
ADAM_WD = 0.01
ADAM_STEP = 10

MESH = pl.DeviceIdType.MESH
VMEM_LIMIT_MB = 56

_NT = (((1,), (1,)), ((), ()))
_TN = (((0,), (0,)), ((), ()))


def _dot(a, b):
    return jnp.dot(a, b, preferred_element_type=F32)


def _dot_nt(a, b):
    return lax.dot_general(a, b, _NT, preferred_element_type=F32)


def _dot_tn(a, b):
    return lax.dot_general(a, b, _TN, preferred_element_type=F32)


def _params(dims=None, vmem_mb=VMEM_LIMIT_MB):
    kw = dict(vmem_limit_bytes=vmem_mb << 20)
    if dims is not None:
        kw["dimension_semantics"] = dims
    return pltpu.CompilerParams(**kw)


def _row_tile(n, prefs=(384, 256, 128)):
    for t in prefs:
        if n % t == 0:
            return t
    raise ValueError(f"no row tile for {n}")


def _iota(shape, dim):
    return lax.broadcasted_iota(jnp.int32, shape, dim)


def _pick_row(tile, row):
    sub = _iota(tile.shape, 0)
    return jnp.sum(jnp.where(sub == row, tile, 0.0), axis=0, keepdims=True)


def _split3(x):
    hi = x.astype(BF16)
    r1 = x - hi.astype(F32)
    mid = r1.astype(BF16)
    lo = (r1 - mid.astype(F32)).astype(BF16)
    return hi, mid, lo


def _full(shape):
    nd = len(shape)
    return pl.BlockSpec(shape, lambda *_: (0,) * nd)


def _in_perm():
    cols = list(range(RET_W))
    for p in range(FOX_HEADS // 2):
        for part in range(3):
            start = RET_W + part * 512 + p * BLK
            cols += list(range(start, start + BLK))
    return np.asarray(cols, np.int32)


def _rotary_tables(L):
    half = HEAD_LANES // 2
    inv = 1.0 / (ROPE_BASE ** (jnp.arange(half, dtype=F32) / half))
    ang = jnp.arange(L).astype(F32)[:, None] * inv[None, :]
    cos, sin = jnp.cos(ang), jnp.sin(ang)
    cos_t = jnp.tile(cos, (1, 4))
    sin_t = jnp.tile(jnp.concatenate([-sin, sin], axis=1), (1, 2))
    return cos_t, sin_t


def _decay_tables():
    gam = 1.0 - 2.0 ** (-5.0 - np.arange(RET_HEADS, dtype=np.float64))
    n = np.arange(BLK)
    same_or_past = (n[:, None] // CHUNK) >= (n[None, :] // CHUNK)
    dist = np.abs(n[:, None] - n[None, :])
    dmat = np.stack([np.where(same_or_past, g ** dist, 0.0) for g in gam]).astype(np.float32)
    lane_head = np.arange(BLK) // HEAD_LANES
    wq = np.stack([gam[2 * p + lane_head][None, :] ** (n[:, None] + 1.0) for p in range(2)]).astype(np.float32)
    wk = np.stack([gam[2 * p + lane_head][None, :] ** (BLK - 1.0 - n[:, None]) for p in range(2)]).astype(np.float32)
    g_blk = tuple(float(g ** BLK) for g in gam)
    return jnp.asarray(dmat), jnp.asarray(wq), jnp.asarray(wk), g_blk


def _shifted_blocks(tm):
    nb = tm // BLK
    return [pl.BlockSpec((BLK, D_MODEL), lambda i, j=j: (jnp.maximum(nb * i + j - 1, 0), 0)) for j in range(nb)]


def _rms_inproj(head, x, g, w_main, w_ff):
    L = x.shape[0] + BLK
    tm = _row_tile(L)
    nb = tm // BLK

    def body(head_ref, *refs):
        x_refs, (g_ref, wm_ref, wf_ref, h_ref, n_ref, p_ref, ff_ref) = refs[:nb], refs[nb:]
        parts = [r[...] for r in x_refs]
        parts[0] = jnp.where(pl.program_id(0) == 0, head_ref[...], parts[0])
        h = jnp.concatenate(parts, axis=0)
        h_ref[...] = h
        r = lax.rsqrt(jnp.mean(h * h, axis=-1, keepdims=True) + EPS)
        n = (h * r * g_ref[...]).astype(BF16)
        n_ref[...] = n
        p_ref[...] = _dot(n, wm_ref[...]).astype(BF16)
        ff_ref[...] = _dot(n, wf_ref[...])

    rows = lambda w: pl.BlockSpec((tm, w), lambda i: (i, 0))
    return pl.pallas_call(
        body, name="f_inproj", grid=(L // tm,),
        in_specs=[_full((BLK, D_MODEL))] + _shifted_blocks(tm)
        + [_full((1, D_MODEL)), _full((D_MODEL, MAIN_W)), _full((D_MODEL, BLK))],
        out_specs=[rows(D_MODEL), rows(D_MODEL), rows(MAIN_W), rows(BLK)],
        out_shape=[jax.ShapeDtypeStruct((L, D_MODEL), F32), jax.ShapeDtypeStruct((L, D_MODEL), BF16),
                   jax.ShapeDtypeStruct((L, MAIN_W), BF16), jax.ShapeDtypeStruct((L, BLK), F32)],
        compiler_params=_params(("parallel",)),
    )(head, *([x] * nb), g, w_main, w_ff)


SMALL_GROUP = 11


def _block_group(nblk, most=3):
    return next(g for g in range(most, 0, -1) if nblk % g == 0)


def _fox_prep(ff, fb):
    L = ff.shape[0]
    nblk = L // BLK
    G = _block_group(nblk, SMALL_GROUP)

    def body(ff_ref, b_ref, c_ref, ct_ref, carry):
        @pl.when(pl.program_id(0) == 0)
        def _():
            carry[...] = jnp.zeros_like(carry)

        tri = (_iota((BLK, BLK), 0) >= _iota((BLK, BLK), 1)).astype(BF16)
        live = _iota((BLK, BLK), 1) < FOX_HEADS
        run = carry[...]
        for b in range(G):
            z = ff_ref[b * BLK:(b + 1) * BLK, :] + b_ref[...]
            lf = jnp.where(live, jnp.minimum(z, 0.0) - jnp.log1p(jnp.exp(-jnp.abs(z))), 0.0)
            hi, mid, lo = _split3(lf)
            cs = (_dot(tri, hi) + _dot(tri, mid) + _dot(tri, lo) + run) * LOG2E
            c_ref[b * BLK:(b + 1) * BLK, :] = cs
            ct_ref[b] = cs.T[0:8, :]
            run = run + jnp.sum(lf, axis=0, keepdims=True)
        carry[...] = run

    return pl.pallas_call(
        body, name="f_foxprep", grid=(nblk // G,),
        in_specs=[pl.BlockSpec((G * BLK, BLK), lambda i: (i, 0)), _full((1, BLK))],
        out_specs=[pl.BlockSpec((G * BLK, BLK), lambda i: (i, 0)), pl.BlockSpec((G, 8, BLK), lambda i: (i, 0, 0))],
        out_shape=[jax.ShapeDtypeStruct((L, BLK), F32), jax.ShapeDtypeStruct((nblk, 8, BLK), F32)],
        scratch_shapes=[pltpu.VMEM((1, BLK), F32)],
        compiler_params=_params(("arbitrary",)),
    )(ff, fb)


def _rot_fns(cos, sin):
    lane = _iota((BLK, BLK), 1)
    first = (lane & (HEAD_LANES - 1)) < HEAD_LANES // 2

    def swap(x):
        return jnp.where(first, pltpu.roll(x, BLK - 32, 1), pltpu.roll(x, 32, 1))

    def rot(x):
        return x * cos + swap(x) * sin

    def rot_t(dy):
        return dy * cos + swap(dy * sin)

    return rot, rot_t


def _retention_fwd(proj, cos_t, sin_t, ret_g):
    L = proj.shape[0]
    nblk = L // BLK
    G = _block_group(nblk, RET_GROUP)
    dmat, wq_t, wk_t, g_blk = _decay_tables()

    def body(q_ref, k_ref, v_ref, gate_ref, cos_ref, sin_ref, d_ref, wq_ref, wk_ref, rg_ref,
             mix_ref, o_ref, rs_ref, state):
        @pl.when(pl.program_id(0) == 0)
        def _():
            state[...] = jnp.zeros_like(state)

        lane = _iota((BLK, BLK), 1)
        sub = _iota((BLK, BLK), 0)
        for b in range(G):
            rows = slice(b * BLK, (b + 1) * BLK)
            rot, _ = _rot_fns(cos_ref[rows, :], sin_ref[rows, :])
            for p in range(2):
                qr = rot(q_ref[rows, p * BLK:(p + 1) * BLK].astype(F32))
                kr = rot(k_ref[rows, p * BLK:(p + 1) * BLK].astype(F32)) * (HEAD_LANES ** -0.5)
                kr_b = kr.astype(BF16)
                qw = (qr * wq_ref[p]).astype(BF16)
                kw = (kr * wk_ref[p]).astype(BF16)
                for e in range(2):
                    h = 2 * p + e
                    cols = slice(h * BLK, (h + 1) * BLK)
                    qm = jnp.where((lane >> 6) == e, qr, 0.0).astype(BF16)
                    s = _dot_nt(qm, kr_b) * d_ref[h]
                    vh = v_ref[rows, cols]
                    st = state[h]
                    rs_ref[b, h] = st
                    o = _dot(s.astype(BF16), vh) + _dot(qw, st.astype(BF16))
                    u = jnp.where((sub >> 6) == e, _dot_tn(kw, vh), 0.0)
                    state[h] = g_blk[h] * st + u
                    rn = lax.rsqrt(jnp.mean(o * o, axis=-1, keepdims=True) + EPS)
                    gate = gate_ref[rows, cols].astype(F32)
                    o_ref[rows, cols] = o
                    mix_ref[rows, cols] = (o * rn * rg_ref[:, cols] * (gate * jax.nn.sigmoid(gate))).astype(BF16)

    row = lambda c: (lambda i: (i, c))
    return pl.pallas_call(
        body, name="f_retention", grid=(nblk // G,),
        in_specs=[pl.BlockSpec((G * BLK, 256), row(0)), pl.BlockSpec((G * BLK, 256), row(1)),
                  pl.BlockSpec((G * BLK, 512), row(1)), pl.BlockSpec((G * BLK, 512), row(2)),
                  pl.BlockSpec((G * BLK, BLK), row(0)), pl.BlockSpec((G * BLK, BLK), row(0)),
                  _full((RET_HEADS, BLK, BLK)), _full((2, BLK, BLK)), _full((2, BLK, BLK)), _full((1, 512))],
        out_specs=[pl.BlockSpec((G * BLK, 512), row(0)), pl.BlockSpec((G * BLK, 512), row(0)),
                   pl.BlockSpec((G, RET_HEADS, BLK, BLK), lambda i: (i, 0, 0, 0))],
        out_shape=[jax.ShapeDtypeStruct((L, 512), BF16), jax.ShapeDtypeStruct((L, 512), F32),
                   jax.ShapeDtypeStruct((nblk, RET_HEADS, BLK, BLK), F32)],
        scratch_shapes=[pltpu.VMEM((RET_HEADS, BLK, BLK), F32)],
        compiler_params=_params(("arbitrary",)),
    )(proj, proj, proj, proj, cos_t, sin_t, dmat, wq_t, wk_t, ret_g)


def _fox_units(L):
    nblk = L // BLK
    assert L % BLK == 0 and nblk % 2 == 1, "sequence must be one 128-row block plus whole 256-row tiles"
    return nblk, (nblk - 1) // 2


def _fox_tile_masks():
    sub, lane = _iota((BLK, BLK), 0), _iota((BLK, BLK), 1)
    valid = _iota((BLK, UNIT), 0) >= N_PAD
    diag = _iota((UNIT, UNIT), 0) <= _iota((UNIT, UNIT), 1)
    r, q = _iota((BLK + UNIT, UNIT), 0), _iota((BLK + UNIT, UNIT), 1)
    first_and_diag = ((r < BLK) & (r >= N_PAD)) | ((r >= BLK) & (r - BLK <= q))
    return dict(first=(sub <= lane) & (sub >= N_PAD), valid=valid, diag=diag, first_and_diag=first_and_diag)


def _fox_fwd(proj, c, ctb, gather=()):
    L = proj.shape[0]
    nblk, nu = _fox_units(L)
    scale = HEAD_LANES ** -0.5 * LOG2E
    ng = len(gather)
    steps = FOX_HEADS // (2 * FOX_PAIRS)

    def body(qkv_ref, c_ref, ct_ref, *rest):
        g_in, (of_ref, lse_ref), g_out = rest[:ng], rest[ng:ng + 2], rest[ng + 2:2 * ng + 2]
        vt, csb = rest[2 * ng + 2:2 * ng + 4]
        p = pl.program_id(0)
        heads = [(pp, e, 2 * FOX_PAIRS * p + 2 * pp + e) for pp in range(FOX_PAIRS) for e in range(2)]

        @pl.when(p == 0)
        def _():
            lse_ref[...] = jnp.zeros_like(lse_ref)
            if ng:
                local, sends, _ = _allgather_copies(g_in, g_out, *rest[2 * ng + 4:])
                for cp in local + sends:
                    cp.start()

        lane = _iota((BLK, BLK), 1)
        sub8 = _iota((8, BLK), 0)
        masks = _fox_tile_masks()

        def pre(j, carry):
            off = pl.multiple_of(j * BLK, BLK)
            ct = c_ref[pl.ds(off, BLK), :]
            for pp in range(FOX_PAIRS):
                vt[pp, j] = qkv_ref[pl.ds(off, BLK), pp * 384 + 2 * BLK:pp * 384 + 3 * BLK].astype(F32).T.astype(BF16)
            for hh, (_, _, h) in enumerate(heads):
                col = jnp.sum(jnp.where(lane == h, ct, 0.0), axis=1, keepdims=True)
                csb[hh, j] = jnp.broadcast_to(col, (BLK, BLK))
            return carry

        lax.fori_loop(0, nblk, pre, 0)

        def attend(qblk, nq, n_whole):
            qlen = nq * BLK
            qoff = pl.multiple_of(qblk * BLK, BLK)
            qlane = _iota((qlen, BLK), 1)
            qs = [qkv_ref[pl.ds(qoff, qlen), pp * 384:pp * 384 + BLK].astype(F32) * scale for pp in range(FOX_PAIRS)]
            qm = [jnp.where((qlane >> 6) == e, qs[pp], 0.0).astype(BF16) for pp, e, _ in heads]
            ct_row = [jnp.concatenate([_pick_row(ct_ref[qblk + a], h) for a in range(nq)], axis=1) for _, _, h in heads]

            def step(segs, mask, st):
                blocks = [kblk + b for kblk, nk in segs for b in range(nk)]
                kts = []
                for pp in range(FOX_PAIRS):
                    kt = [qkv_ref[pl.ds(pl.multiple_of(kblk * BLK, BLK), nk * BLK), pp * 384 + BLK:pp * 384 + 2 * BLK]
                          for kblk, nk in segs]
                    kts.append(kt[0] if len(kt) == 1 else jnp.concatenate(kt, axis=0))
                out = []
                for hh, (pp, e, _) in enumerate(heads):
                    m, l, acc = st[3 * hh:3 * hh + 3]
                    s = _dot_nt(kts[pp], qm[hh])
                    t = jnp.concatenate([s[b * BLK:(b + 1) * BLK] - jnp.concatenate([csb[hh, blk]] * nq, axis=1)
                                         for b, blk in enumerate(blocks)], axis=0)
                    if mask is not None:
                        t = jnp.where(mask, t, NEG)
                    m_new = jnp.maximum(m, jnp.max(t, axis=0, keepdims=True) + ct_row[hh])
                    alpha = jnp.exp2(m - m_new)
                    pr = jnp.exp2(t - (m_new - ct_row[hh]))
                    l = alpha * l + jnp.sum(pr, axis=0, keepdims=True)
                    pr_b = pr.astype(BF16)
                    pv = None
                    for b, blk in enumerate(blocks):
                        part = _dot(vt[pp, blk, e * HEAD_LANES:(e + 1) * HEAD_LANES, :], pr_b[b * BLK:(b + 1) * BLK])
                        pv = part if pv is None else pv + part
                    out += [m_new, l, alpha * acc + pv]
                return tuple(out)

            st = (jnp.full((1, qlen), NEG, F32), jnp.zeros((1, qlen), F32),
                  jnp.zeros((HEAD_LANES, qlen), F32)) * len(heads)
            if nq == 1:
                st = step([(0, 1)], masks["first"], st)
            else:
                st = step([(0, 1), (qblk, 2)], masks["first_and_diag"], st)
                n_wide = n_whole // WIDE
                st = lax.fori_loop(0, n_wide, lambda j, s_: step([(1 + 2 * WIDE * j, 2 * WIDE)], None, s_), st)
                rest = 1 + 2 * WIDE * n_wide
                st = lax.cond((n_whole & 2) != 0, lambda s_: step([(rest, 4)], None, s_), lambda s_: s_, st)
                st = lax.cond((n_whole & 1) != 0, lambda s_: step([(rest + 2 * (n_whole & 2), 2)], None, s_),
                              lambda s_: s_, st)
            for pp in range(FOX_PAIRS):
                lo, hi = st[6 * pp:6 * pp + 3], st[6 * pp + 3:6 * pp + 6]
                o_t = jnp.concatenate([lo[2] * (1.0 / lo[1]), hi[2] * (1.0 / hi[1])], axis=0)
                of_ref[pl.ds(qoff, qlen), pp * BLK:(pp + 1) * BLK] = o_t.T.astype(BF16)
            lse = [st[3 * hh] + jnp.log(st[3 * hh + 1]) * LOG2E for hh in range(len(heads))]
            for a in range(nq):
                upd = jnp.zeros((8, BLK), F32)
                for hh, (_, _, h) in enumerate(heads):
                    upd = upd + jnp.where(sub8 == h, lse[hh][:, a * BLK:(a + 1) * BLK], 0.0)
                lse_ref[qblk + a] = lse_ref[qblk + a] + upd

        attend(0, 1, 0)

        def q_loop(u, carry):
            attend(1 + 2 * u, 2, u)
            return carry

        lax.fori_loop(0, nu, q_loop, 0)

        if ng:
            @pl.when(p == steps - 1)
            def _():
                local, sends, recvs = _allgather_copies(g_in, g_out, *rest[2 * ng + 4:])
                for cp in recvs:
                    cp.wait_recv()
                for cp in sends:
                    cp.wait_send()
                for cp in local:
                    cp.wait()

    width = 384 * FOX_PAIRS
    return pl.pallas_call(
        body, name="f_fox", grid=(steps,),
        in_specs=[pl.BlockSpec((L, width), lambda p: (0, RET_W // width + p), pipeline_mode=pl.Buffered(1)),
                  _full((L, BLK)), _full((nblk, 8, BLK))]
        + [_ANY] * ng,
        out_specs=[pl.BlockSpec((L, FOX_PAIRS * BLK), lambda p: (0, p)), _full((nblk, 8, BLK))] + [_ANY] * ng,
        out_shape=[jax.ShapeDtypeStruct((L, 512), BF16), jax.ShapeDtypeStruct((nblk, 8, BLK), F32)]
        + [jax.ShapeDtypeStruct((N_CHIPS,) + a.shape, a.dtype) for a in gather],
        scratch_shapes=[pltpu.VMEM((FOX_PAIRS, nblk, BLK, BLK), BF16), pltpu.VMEM((2 * FOX_PAIRS, nblk, BLK, BLK), F32)]
        + _allgather_semaphores(ng),
        compiler_params=_params(("arbitrary",)),
    )(proj, c, ctb, *gather)


def _outproj_up(mix_r, o_f, h0, w_out, ffn_g, w_up, conv_w, conv_b):
    L = h0.shape[0]
    tm = _row_tile(L)
    shard = w_up.shape[2]
    assert 2 * shard == D_FF
    cw = [conv_w[j:j + 1] for j in range(3)]
    resident = lambda shape: pl.BlockSpec(shape, lambda i: (0,) * len(shape), pipeline_mode=pl.Buffered(1))

    def body(mr_ref, of_ref, h0_ref, wo_ref, g_ref, wu_ref, cw0, cw1, cw2, cb_ref,
             h1_ref, n2_ref, up_ref, act_ref, acc_ref, halo):
        i = pl.program_id(0)

        @pl.when(i == 0)
        def _():
            halo[...] = jnp.zeros_like(halo)

        h1 = h0_ref[...] + _dot(mr_ref[...], wo_ref[0:512, :]) + _dot(of_ref[...], wo_ref[512:1024, :])
        h1_ref[...] = h1
        r = lax.rsqrt(jnp.mean(h1 * h1, axis=-1, keepdims=True) + EPS)
        n2 = (h1 * r * g_ref[...]).astype(BF16)
        n2_ref[...] = n2
        live = i * tm + _iota((tm, 1), 0) >= N_PAD
        for half in range(2):
            cols = slice(half * shard, (half + 1) * shard)
            a_b = _dot(n2, wu_ref[half]).astype(BF16)
            b_b = _dot(n2, wu_ref[2 + half]).astype(BF16)
            up_ref[:, cols] = a_b
            up_ref[:, D_FF + half * shard:D_FF + (half + 1) * shard] = b_b
            a = jnp.where(live, a_b.astype(F32), 0.0)
            _, _, acc = _conv_taps(a, halo[:, cols], [cw0[:, cols], cw1[:, cols], cw2[:, cols]], cb_ref[:, cols])
            act_ref[:, cols] = (acc * jax.nn.sigmoid(acc) * b_b.astype(F32)).astype(BF16)
            acc_ref[:, cols] = acc.astype(BF16)
            halo[:, cols] = a[tm - 8:tm, :]

    rows = lambda w: pl.BlockSpec((tm, w), lambda i: (i, 0))
    return pl.pallas_call(
        body, name="f_outproj_up", grid=(L // tm,),
        in_specs=[rows(512), rows(512), rows(D_MODEL), resident((D_MODEL, D_MODEL)), _full((1, D_MODEL)),
                  resident((N_CHIPS, D_MODEL, shard)), _full((1, D_FF)), _full((1, D_FF)), _full((1, D_FF)),
                  _full((1, D_FF))],
        out_specs=[rows(D_MODEL), rows(D_MODEL), rows(2 * D_FF), rows(D_FF), rows(D_FF)],
        out_shape=[jax.ShapeDtypeStruct((L, D_MODEL), F32), jax.ShapeDtypeStruct((L, D_MODEL), BF16),
                   jax.ShapeDtypeStruct((L, 2 * D_FF), BF16), jax.ShapeDtypeStruct((L, D_FF), BF16),
                   jax.ShapeDtypeStruct((L, D_FF), BF16)],
        scratch_shapes=[pltpu.VMEM((8, D_FF), F32)],
        compiler_params=_params(("arbitrary",)),
    )(mix_r, o_f, h0, w_out, ffn_g, w_up, cw[0], cw[1], cw[2], conv_b)


def _conv_taps(a, halo, cw, cb):
    sub = _iota((a.shape[0], 1), 0)
    a1 = jnp.where(sub == 0, _pick_row(halo, 7), pltpu.roll(a, 1, 0))
    a2 = jnp.where(sub == 0, _pick_row(halo, 6), jnp.where(sub == 1, _pick_row(halo, 7), pltpu.roll(a, 2, 0)))
    acc = cb + a2 * cw[0]
    acc = acc + a1 * cw[1]
    acc = acc + a * cw[2]
    return a1, a2, acc


def _ffn_down_loss(g_act, w_down, h1, final_g, target, acc_saved, up):
    L = h1.shape[0]
    tm = _row_tile(L)
    nb = tm // BLK
    half_w = D_FF // 2

    nt = L // tm

    def body(wd_ref, h1_ref, gf_ref, *refs):
        t_refs, streams = refs[:nb], refs[nb:nb + 3]
        dh_ref, dhb_ref, dgf_ref, loss_ref, dacc_ref, db_ref, ring, sem = refs[nb + 3:]
        i = pl.program_id(0)

        def fetch(step, slot):
            rows = pl.ds(pl.multiple_of(step * tm, BLK), tm)
            srcs = (streams[0].at[rows, :], streams[1].at[rows, :], streams[2].at[rows, pl.ds(D_FF, D_FF)])
            return [pltpu.make_async_copy(src, ring.at[k, slot], sem.at[k, slot]) for k, src in enumerate(srcs)]

        @pl.when(i == 0)
        def _():
            dgf_ref[...] = jnp.zeros_like(dgf_ref)
            loss_ref[...] = jnp.zeros_like(loss_ref)
            for s in range(min(RING - 1, nt)):
                for cp in fetch(s, s):
                    cp.start()

        @pl.when(i + RING - 1 < nt)
        def _():
            for cp in fetch(i + RING - 1, (i + RING - 1) % RING):
                cp.start()

        slot = i % RING
        for cp in fetch(i, slot):
            cp.wait()

        h2 = h1_ref[...] + _dot(ring[0, slot], wd_ref[...])
        r = lax.rsqrt(jnp.mean(h2 * h2, axis=-1, keepdims=True) + EPS)
        yn = h2 * r
        gf = gf_ref[...]
        live = i * tm + _iota((tm, 1), 0) >= PREFIX
        target = jnp.concatenate([t[...] for t in t_refs], axis=0)
        err = jnp.where(live, yn * gf - target, 0.0)
        loss_ref[...] = loss_ref[...] + 0.5 * jnp.sum(jnp.mean(err * err, axis=-1, keepdims=True))
        dy = err * (1.0 / D_MODEL)
        dgf_ref[...] = dgf_ref[...] + jnp.sum(dy * yn, axis=0, keepdims=True)
        dyn = dy * gf
        dh = r * (dyn - yn * jnp.mean(dyn * yn, axis=-1, keepdims=True))
        dh_ref[...] = dh
        dhb = dh.astype(BF16)
        dhb_ref[...] = dhb
        for half in range(2):
            cols = slice(half * half_w, (half + 1) * half_w)
            acc = ring[1, slot, :, cols].astype(F32)
            dg = _dot_nt(dhb, wd_ref[cols, :])
            sg = jax.nn.sigmoid(acc)
            silu = acc * sg
            db_ref[:, cols] = (dg * silu).astype(BF16)
            dacc_ref[:, cols] = (dg * ring[2, slot, :, cols].astype(F32) * (sg + silu * (1.0 - sg))).astype(BF16)

    rows = lambda w: pl.BlockSpec((tm, w), lambda i: (i, 0))
    return pl.pallas_call(
        body, name="f_ffn_down_loss", grid=(nt,),
        in_specs=[pl.BlockSpec((D_FF, D_MODEL), lambda i: (0, 0), pipeline_mode=pl.Buffered(1)),
                  rows(D_MODEL), _full((1, D_MODEL))] + _shifted_blocks(tm) + [_ANY] * 3,
        out_specs=[rows(D_MODEL), rows(D_MODEL), _full((1, D_MODEL)), _full((1, BLK)), rows(D_FF), rows(D_FF)],
        out_shape=[jax.ShapeDtypeStruct((L, D_MODEL), F32), jax.ShapeDtypeStruct((L, D_MODEL), BF16),
                   jax.ShapeDtypeStruct((1, D_MODEL), F32), jax.ShapeDtypeStruct((1, BLK), F32),
                   jax.ShapeDtypeStruct((L, D_FF), BF16), jax.ShapeDtypeStruct((L, D_FF), BF16)],
        scratch_shapes=[pltpu.VMEM((3, RING, tm, D_FF), BF16), pltpu.SemaphoreType.DMA((3, RING))],
        compiler_params=_params(("arbitrary",)),
    )(w_down, h1, final_g, *([target] * nb), g_act, acc_saved, up)


def _ffn_bwd_up(dacc, db, up, conv_w, w_up, h1, ffn_g, dh2, w_out):
    L = h1.shape[0]
    tm = _row_tile(L)
    nt = L // tm
    shard = w_up.shape[2]
    cw = [conv_w[j:j + 1] for j in range(3)]

    def body(da_ref, halo_ref, db_ref, a_ref, cw0, cw1, cw2, wu_ref, h1_ref, g_ref, dh2_ref, wo_ref,
             dup_ref, dh1_ref, dh1b_ref, dmix_ref, dg_ref, dcw_ref):
        i = pl.program_id(0)

        @pl.when(i == 0)
        def _():
            dg_ref[...] = jnp.zeros_like(dg_ref)
            dcw_ref[...] = jnp.zeros_like(dcw_ref)

        sub = _iota((tm, 1), 0)
        sub8 = _iota((8, 1), 0)
        last_tile = i == nt - 1
        dbv = db_ref[...]
        dup_ref[:, D_FF:2 * D_FF] = dbv
        dn = _dot_nt(dbv[:, 0:shard], wu_ref[2]) + _dot_nt(dbv[:, shard:2 * shard], wu_ref[3])
        for half in range(2):
            cols = slice(half * shard, (half + 1) * shard)
            d0 = da_ref[:, cols].astype(F32)
            halo = jnp.where(last_tile, 0.0, halo_ref[:, cols].astype(F32))
            d1 = jnp.where(sub == tm - 1, _pick_row(halo, 0), pltpu.roll(d0, tm - 1, 0))
            d2 = jnp.where(sub == tm - 2, _pick_row(halo, 0),
                           jnp.where(sub == tm - 1, _pick_row(halo, 1), pltpu.roll(d0, tm - 2, 0)))
            a = a_ref[:, cols].astype(F32)
            upd = jnp.zeros((8, shard), F32)
            for j, t in enumerate((d2 * a, d1 * a, d0 * a, d0)):
                upd = upd + jnp.where(sub8 == j, jnp.sum(t, axis=0, keepdims=True), 0.0)
            dcw_ref[:, cols] = dcw_ref[:, cols] + upd
            da = (d0 * cw2[:, cols] + d1 * cw1[:, cols] + d2 * cw0[:, cols]).astype(BF16)
            dup_ref[:, cols] = da
            dn = dn + _dot_nt(da, wu_ref[half])
        h1 = h1_ref[...]
        r = lax.rsqrt(jnp.mean(h1 * h1, axis=-1, keepdims=True) + EPS)
        yn = h1 * r
        dg_ref[...] = dg_ref[...] + jnp.sum(dn * yn, axis=0, keepdims=True)
        dyn = dn * g_ref[...]
        dh1 = dh2_ref[...] + r * (dyn - yn * jnp.mean(dyn * yn, axis=-1, keepdims=True))
        dh1_ref[...] = dh1
        dh1b = dh1.astype(BF16)
        dh1b_ref[...] = dh1b
        dmix_ref[...] = _dot_nt(dh1b, wo_ref[...]).astype(BF16)

    rows = lambda w: pl.BlockSpec((tm, w), lambda i: (i, 0))
    halo = pl.BlockSpec((8, D_FF), lambda i: (jnp.minimum((i + 1) * (tm // 8), L // 8 - 1), 0))
    return pl.pallas_call(
        body, name="b_ffn_up", grid=(nt,),
        in_specs=[rows(D_FF), halo, rows(D_FF), rows(D_FF), _full((1, D_FF)), _full((1, D_FF)), _full((1, D_FF)),
                  _full((N_CHIPS, D_MODEL, shard)), rows(D_MODEL), _full((1, D_MODEL)), rows(D_MODEL),
                  _full((D_MODEL, D_MODEL))],
        out_specs=[rows(2 * D_FF), rows(D_MODEL), rows(D_MODEL), rows(D_MODEL), _full((1, D_MODEL)),
                   _full((8, D_FF))],
        out_shape=[jax.ShapeDtypeStruct((L, 2 * D_FF), BF16), jax.ShapeDtypeStruct((L, D_MODEL), F32),
                   jax.ShapeDtypeStruct((L, D_MODEL), BF16), jax.ShapeDtypeStruct((L, D_MODEL), BF16),
                   jax.ShapeDtypeStruct((1, D_MODEL), F32), jax.ShapeDtypeStruct((8, D_FF), F32)],
        compiler_params=_params(("arbitrary",)),
    )(dacc, dacc, db, up, cw[0], cw[1], cw[2], w_up, h1, ffn_g, dh2, w_out)


def _wgrad(a, b, name, tn=None, tk=None, out_dtype=F32):
    L, K = a.shape
    N = b.shape[1]
    tn = N if tn is None else tn
    tk = K if tk is None else tk
    tl = _row_tile(L, (1408, 768, 512, 256, 128))
    nl = L // tl

    def body(a_ref, b_ref, o_ref, acc):
        step = pl.program_id(2)

        @pl.when(step == 0)
        def _():
            acc[...] = jnp.zeros_like(acc)

        acc[...] = acc[...] + _dot_tn(a_ref[...], b_ref[...])

        @pl.when(step == nl - 1)
        def _():
            o_ref[0] = acc[...].astype(out_dtype)

    return pl.pallas_call(
        body, name=name, grid=(N // tn, K // tk, L // tl),
        in_specs=[pl.BlockSpec((tl, tk), lambda n, k, l: (l, k)), pl.BlockSpec((tl, tn), lambda n, k, l: (l, n))],
        out_specs=pl.BlockSpec((1, tk, tn), lambda n, k, l: (n, k, 0)),
        out_shape=jax.ShapeDtypeStruct((N // tn, K, tn), out_dtype),
        scratch_shapes=[pltpu.VMEM((tk, tn), F32)],
        compiler_params=_params(("parallel", "parallel", "arbitrary")),
    )(a, b)


def _retention_bwd(dmix, o, proj, cos_t, sin_t, ret_g, states, exchange=()):
    L = proj.shape[0]
    nblk = L // BLK
    G = _block_group(nblk, RET_GROUP)
    steps = nblk // G
    nx = len(exchange)
    dmat, wq_t, wk_t, g_blk = _decay_tables()

    def body(dm_ref, o_ref, q_ref, k_ref, v_ref, gate_ref, cos_ref, sin_ref, d_ref, wq_ref, wk_ref, rg_ref, rs_ref,
             *rest):
        x_in, (dp_ref, drg_ref), x_out, gstate = rest[:nx], rest[nx:nx + 2], rest[nx + 2:2 * nx + 2], rest[2 * nx + 2]

        @pl.when(pl.program_id(0) == 0)
        def _():
            if nx:
                for cp in _sibling_half_copies(x_in, x_out, *rest[2 * nx + 3:])[0]:
                    cp.start()
            gstate[...] = jnp.zeros_like(gstate)
            drg_ref[...] = jnp.zeros_like(drg_ref)

        lane = _iota((BLK, BLK), 1)
        sub = _iota((BLK, BLK), 0)
        scale = HEAD_LANES ** -0.5
        for b in reversed(range(G)):
            rows = slice(b * BLK, (b + 1) * BLK)
            rot, rot_t = _rot_fns(cos_ref[rows, :], sin_ref[rows, :])
            for p in range(2):
                qr = rot(q_ref[rows, p * BLK:(p + 1) * BLK].astype(F32))
                kr = rot(k_ref[rows, p * BLK:(p + 1) * BLK].astype(F32)) * scale
                kr_b = kr.astype(BF16)
                qw = (qr * wq_ref[p]).astype(BF16)
                kw = (kr * wk_ref[p]).astype(BF16)
                dqr = jnp.zeros((BLK, BLK), F32)
                dkr = jnp.zeros((BLK, BLK), F32)
                for e in range(2):
                    h = 2 * p + e
                    cols = slice(h * BLK, (h + 1) * BLK)
                    head_lanes = (lane >> 6) == e
                    o = o_ref[rows, cols]
                    rn = lax.rsqrt(jnp.mean(o * o, axis=-1, keepdims=True) + EPS)
                    y = o * rn
                    gate = gate_ref[rows, cols].astype(F32)
                    sg = jax.nn.sigmoid(gate)
                    dm = dm_ref[rows, cols].astype(F32)
                    rgain = rg_ref[:, cols]
                    drg_ref[:, cols] = drg_ref[:, cols] + jnp.sum(dm * y * (gate * sg), axis=0, keepdims=True)
                    dp_ref[rows, 1024 + h * BLK:1024 + (h + 1) * BLK] = (
                        dm * y * rgain * (sg * (1.0 + gate * (1.0 - sg)))).astype(BF16)
                    dy = dm * rgain * (gate * sg)
                    do = (rn * (dy - y * jnp.mean(dy * y, axis=-1, keepdims=True))).astype(BF16)
                    vh = v_ref[rows, cols]
                    qm = jnp.where(head_lanes, qr, 0.0).astype(BF16)
                    dmh = d_ref[h]
                    s = (_dot_nt(qm, kr_b) * dmh).astype(BF16)
                    ds = (_dot_nt(do, vh) * dmh).astype(BF16)
                    st = rs_ref[b, h].astype(BF16)
                    gs = gstate[h]
                    gs_b = gs.astype(BF16)
                    dqr = dqr + jnp.where(head_lanes, _dot(ds, kr_b), 0.0) + _dot_nt(do, st) * wq_ref[p]
                    dkr = dkr + _dot_tn(ds, qm) + _dot_nt(vh, gs_b) * wk_ref[p]
                    dp_ref[rows, 512 + h * BLK:512 + (h + 1) * BLK] = (_dot_tn(s, do) + _dot(kw, gs_b)).astype(BF16)
                    dr = jnp.where((sub >> 6) == e, _dot_tn(qw, do), 0.0)
                    gstate[h] = dr + g_blk[h] * gs
                dp_ref[rows, p * BLK:(p + 1) * BLK] = rot_t(dqr).astype(BF16)
                dp_ref[rows, 256 + p * BLK:256 + (p + 1) * BLK] = (rot_t(dkr) * scale).astype(BF16)

        if nx:
            @pl.when(pl.program_id(0) == steps - 1)
            def _():
                sends, recvs = _sibling_half_copies(x_in, x_out, *rest[2 * nx + 3:])
                for cp in recvs:
                    cp.wait_recv()
                for cp in sends:
                    cp.wait_send()

    row = lambda c: (lambda i: (steps - 1 - i, c))
    return pl.pallas_call(
        body, name="b_retention", grid=(steps,),
        in_specs=[pl.BlockSpec((G * BLK, 512), row(0)), pl.BlockSpec((G * BLK, 512), row(0)),
                  pl.BlockSpec((G * BLK, 256), row(0)), pl.BlockSpec((G * BLK, 256), row(1)),
                  pl.BlockSpec((G * BLK, 512), row(1)), pl.BlockSpec((G * BLK, 512), row(2)),
                  pl.BlockSpec((G * BLK, BLK), row(0)), pl.BlockSpec((G * BLK, BLK), row(0)),
                  _full((RET_HEADS, BLK, BLK)), _full((2, BLK, BLK)), _full((2, BLK, BLK)), _full((1, 512)),
                  pl.BlockSpec((G, RET_HEADS, BLK, BLK), lambda i: (steps - 1 - i, 0, 0, 0))] + [_ANY] * nx,
        out_specs=[pl.BlockSpec((G * BLK, RET_W), row(0)), _full((1, 512))] + [_ANY] * nx,
        out_shape=[jax.ShapeDtypeStruct((L, RET_W), BF16), jax.ShapeDtypeStruct((1, 512), F32)]
        + _sibling_half_shapes(exchange),
        scratch_shapes=[pltpu.VMEM((RET_HEADS, BLK, BLK), F32)] + _sibling_half_semaphores(nx),
        compiler_params=_params(("arbitrary",)),
    )(dmix, o, proj, proj, proj, proj, cos_t, sin_t, dmat, wq_t, wk_t, ret_g, states, *exchange)


def _fox_delta(dmix, o_f):
    L = o_f.shape[0]
    nblk = L // BLK
    G = _block_group(nblk, SMALL_GROUP)

    def body(do_ref, o_ref, d_ref):
        sel = ((_iota((8, 512), 1) >> 6) == _iota((8, 512), 0)).astype(BF16)
        for b in range(G):
            rows = slice(b * BLK, (b + 1) * BLK)
            prod = do_ref[rows, :].astype(F32) * o_ref[rows, :].astype(F32)
            hi = prod.astype(BF16)
            lo = (prod - hi.astype(F32)).astype(BF16)
            d_ref[b] = _dot_nt(sel, hi) + _dot_nt(sel, lo)

    return pl.pallas_call(
        body, name="b_foxdelta", grid=(nblk // G,),
        in_specs=[pl.BlockSpec((G * BLK, 512), lambda i: (i, 1)), pl.BlockSpec((G * BLK, 512), lambda i: (i, 0))],
        out_specs=pl.BlockSpec((G, 8, BLK), lambda i: (i, 0, 0)),
        out_shape=jax.ShapeDtypeStruct((nblk, 8, BLK), F32),
        compiler_params=_params(("parallel",)),
    )(dmix, o_f)


def _fox_bwd(proj, dmix, c, ctb, lse, delta, scatter=()):
    L = proj.shape[0]
    nblk, nu = _fox_units(L)
    scale = HEAD_LANES ** -0.5
    ns = len(scatter)
    steps = FOX_HEADS // (2 * FOX_PAIRS)

    def body(qkv_ref, do_ref, c_ref, ct_ref, lse_ref, dl_ref, *rest):
        s_in, (dp_ref, dc_ref, dcq_ref), s_out = rest[:ns], rest[ns:ns + 3], rest[ns + 3:2 * ns + 3]
        ktt, dqt, dk_acc, dv_acc, dcs_acc = rest[2 * ns + 3:2 * ns + 8]
        p = pl.program_id(0)
        heads = [(pp, e, 2 * FOX_PAIRS * p + 2 * pp + e) for pp in range(FOX_PAIRS) for e in range(2)]

        @pl.when(p == 0)
        def _():
            dc_ref[...] = jnp.zeros_like(dc_ref)
            dcq_ref[...] = jnp.zeros_like(dcq_ref)
            if ns:
                for cp in _scatter_copies(s_in, s_out, *rest[2 * ns + 8:]):
                    cp.start()

        sub8 = _iota((8, BLK), 0)
        masks = _fox_tile_masks()

        def pre(j, carry):
            off = pl.multiple_of(j * BLK, BLK)
            for pp in range(FOX_PAIRS):
                ktt[pp, j] = qkv_ref[pl.ds(off, BLK), pp * 384 + BLK:pp * 384 + 2 * BLK].astype(F32).T.astype(BF16)
                dqt[pp, j] = jnp.zeros((BLK, BLK), F32)
            return carry

        lax.fori_loop(0, nblk, pre, 0)

        def kv_pass(kblk, nk, n_later):
            klen = nk * BLK
            koff = pl.multiple_of(kblk * BLK, BLK)
            kt = [qkv_ref[pl.ds(koff, klen), pp * 384 + BLK:pp * 384 + 2 * BLK] for pp in range(FOX_PAIRS)]
            vtile = [qkv_ref[pl.ds(koff, klen), pp * 384 + 2 * BLK:pp * 384 + 3 * BLK] for pp in range(FOX_PAIRS)]
            ct = c_ref[pl.ds(koff, klen), :]
            klane = _iota((klen, BLK), 1)
            cs = [jnp.broadcast_to(jnp.sum(jnp.where(klane == h, ct, 0.0), axis=1, keepdims=True), (klen, WIDE * UNIT))
                  for _, _, h in heads]
            k_t = [jnp.concatenate([ktt[pp, kblk + b, e * HEAD_LANES:(e + 1) * HEAD_LANES, :] for b in range(nk)], axis=1)
                   for pp, e, _ in heads]
            for pp in range(FOX_PAIRS):
                dk_acc[pp, 0:klen] = jnp.zeros((klen, BLK), F32)
                dv_acc[pp, 0:klen] = jnp.zeros((klen, BLK), F32)
            for hh in range(len(heads)):
                dcs_acc[hh, 0:klen] = jnp.zeros((klen, BLK), F32)

            def tile(qblk, nq, mask):
                qlen = nq * BLK
                if mask == "valid":
                    mask = _iota((klen, qlen), 0) >= N_PAD
                qoff = pl.multiple_of(qblk * BLK, BLK)
                qlane = _iota((qlen, BLK), 1)
                qs = [qkv_ref[pl.ds(qoff, qlen), pp * 384:pp * 384 + BLK].astype(F32) * (scale * LOG2E)
                      for pp in range(FOX_PAIRS)]
                dot_ = [do_ref[pl.ds(qoff, qlen), pp * BLK:(pp + 1) * BLK] for pp in range(FOX_PAIRS)]
                stats = [[ref[qblk + a] for a in range(nq)] for ref in (ct_ref, lse_ref, dl_ref)]
                dcq = [jnp.zeros((8, BLK), F32) for _ in range(nq)]
                for hh, (pp, e, h) in enumerate(heads):
                    head = (qlane >> 6) == e
                    ct_row, lse_row, dl_row = [jnp.concatenate([_pick_row(t, h) for t in ts], axis=1) for ts in stats]
                    qm = jnp.where(head, qs[pp], 0.0).astype(BF16)
                    dom = jnp.where(head, dot_[pp], jnp.zeros_like(dot_[pp]))
                    t = _dot_nt(kt[pp], qm) - cs[hh][:, 0:qlen]
                    if mask is not None:
                        t = jnp.where(mask, t, NEG)
                    pr = jnp.exp2(t + (ct_row - lse_row))
                    dv_acc[pp, 0:klen] = dv_acc[pp, 0:klen] + _dot(pr.astype(BF16), dom)
                    dsv = pr * (_dot_nt(vtile[pp], dom) - dl_row)
                    ds_b = dsv.astype(BF16)
                    dk_acc[pp, 0:klen] = dk_acc[pp, 0:klen] + _dot(ds_b, qm)
                    rows = slice(e * HEAD_LANES, (e + 1) * HEAD_LANES)
                    dq_t = _dot(k_t[hh], ds_b)
                    key_side = dsv[:, 0:BLK]
                    for a in range(1, nq):
                        key_side = key_side + dsv[:, a * BLK:(a + 1) * BLK]
                    dcs_acc[hh, 0:klen] = dcs_acc[hh, 0:klen] + key_side
                    query_side = jnp.sum(dsv, axis=0, keepdims=True)
                    for a in range(nq):
                        cols = slice(a * BLK, (a + 1) * BLK)
                        dqt[pp, qblk + a, rows, :] = dqt[pp, qblk + a, rows, :] + dq_t[:, cols]
                        dcq[a] = dcq[a] + jnp.where(sub8 == h, query_side[:, cols], 0.0)
                for a in range(nq):
                    dcq_ref[qblk + a] = dcq_ref[qblk + a] + dcq[a]

            later_mask = "valid" if nk == 1 else None
            n_later = jnp.asarray(n_later, jnp.int32)
            n_wide = n_later // WIDE

            def later_wide(i, carry):
                tile(kblk + nk + 2 * WIDE * i, 2 * WIDE, later_mask)
                return carry

            tile(kblk, nk, masks["first"] if nk == 1 else masks["diag"])
            lax.fori_loop(0, n_wide, later_wide, 0)
            rest_blk = kblk + nk + 2 * WIDE * n_wide

            @pl.when((n_later & 2) != 0)
            def _():
                tile(rest_blk, 4, later_mask)

            @pl.when((n_later & 1) != 0)
            def _():
                tile(rest_blk + 2 * (n_later & 2), 2, later_mask)

            upd = jnp.zeros((klen, BLK), F32)
            for hh, (_, _, h) in enumerate(heads):
                upd = upd + jnp.where(klane == h, -jnp.sum(dcs_acc[hh, 0:klen], axis=1, keepdims=True), 0.0)
            dc_ref[pl.ds(koff, klen), :] = dc_ref[pl.ds(koff, klen), :] + upd
            for pp in range(FOX_PAIRS):
                dp_ref[pl.ds(koff, klen), pp * 384 + BLK:pp * 384 + 2 * BLK] = (
                    dk_acc[pp, 0:klen] * (1.0 / LOG2E)).astype(BF16)
                dp_ref[pl.ds(koff, klen), pp * 384 + 2 * BLK:pp * 384 + 3 * BLK] = dv_acc[pp, 0:klen].astype(BF16)

        kv_pass(0, 1, nu)

        def k_loop(u, carry):
            kv_pass(1 + 2 * u, 2, nu - 1 - u)
            return carry

        lax.fori_loop(0, nu, k_loop, 0)

        def flush(j, carry):
            off = pl.multiple_of(j * BLK, BLK)
            for pp in range(FOX_PAIRS):
                dp_ref[pl.ds(off, BLK), pp * 384:pp * 384 + BLK] = (dqt[pp, j].T * scale).astype(BF16)
            return carry

        lax.fori_loop(0, nblk, flush, 0)

        if ns:
            @pl.when(p == steps - 1)
            def _():
                copies = _scatter_copies(s_in, s_out, *rest[2 * ns + 8:])
                for cp in copies:
                    cp.wait_recv()
                for cp in copies:
                    cp.wait_send()

    width = 384 * FOX_PAIRS
    once = lambda shape, index: pl.BlockSpec(shape, index, pipeline_mode=pl.Buffered(1))
    stat = once((nblk, 8, BLK), lambda p: (0, 0, 0))
    return pl.pallas_call(
        body, name="b_fox", grid=(steps,),
        in_specs=[once((L, width), lambda p: (0, RET_W // width + p)),
                  once((L, FOX_PAIRS * BLK), lambda p: (0, 4 // FOX_PAIRS + p)),
                  once((L, BLK), lambda p: (0, 0)), stat, stat, stat] + [_ANY] * ns,
        out_specs=[once((L, width), lambda p: (0, p)), _full((L, BLK)), _full((nblk, 8, BLK))] + [_ANY] * ns,
        out_shape=[jax.ShapeDtypeStruct((L, FOX_W), BF16), jax.ShapeDtypeStruct((L, BLK), F32),
                   jax.ShapeDtypeStruct((nblk, 8, BLK), F32)] + _scatter_shapes(scatter),
        scratch_shapes=[pltpu.VMEM((FOX_PAIRS, nblk, BLK, BLK), BF16), pltpu.VMEM((FOX_PAIRS, nblk, BLK, BLK), F32),
                        pltpu.VMEM((FOX_PAIRS, UNIT, BLK), F32), pltpu.VMEM((FOX_PAIRS, UNIT, BLK), F32),
                        pltpu.VMEM((2 * FOX_PAIRS, UNIT, BLK), F32)]
        + _scatter_semaphores(ns),
        compiler_params=_params(("arbitrary",)),
    )(proj, dmix, c, ctb, lse, delta, *scatter)


def _fox_post(dc, dcq, ff, fb):
    L = dc.shape[0]
    nblk = L // BLK
    G = _block_group(nblk, SMALL_GROUP)
    steps = nblk // G

    def body(dc_ref, dcq_ref, ff_ref, b_ref, dff_ref, dffb_ref, dfb_ref, carry):
        @pl.when(pl.program_id(0) == 0)
        def _():
            carry[...] = jnp.zeros_like(carry)
            dfb_ref[...] = jnp.zeros_like(dfb_ref)

        tri = (_iota((BLK, BLK), 0) <= _iota((BLK, BLK), 1)).astype(BF16)
        live = _iota((BLK, BLK), 1) < FOX_HEADS
        run, dfb = carry[...], dfb_ref[...]
        for b in reversed(range(G)):
            rows = slice(b * BLK, (b + 1) * BLK)
            d = dc_ref[rows, :] + jnp.concatenate([dcq_ref[b], jnp.zeros((BLK - 8, BLK), F32)], axis=0).T
            hi, mid, lo = _split3(d)
            dlf = _dot(tri, hi) + _dot(tri, mid) + _dot(tri, lo) + run
            run = run + jnp.sum(d, axis=0, keepdims=True)
            z = ff_ref[rows, :] + b_ref[...]
            dff = jnp.where(live, dlf * jax.nn.sigmoid(-z), 0.0)
            dff_ref[rows, :] = dff
            dffb_ref[rows, :] = dff.astype(BF16)
            dfb = dfb + jnp.sum(dff, axis=0, keepdims=True)
        carry[...] = run
        dfb_ref[...] = dfb

    rev = lambda i: (steps - 1 - i, 0)
    return pl.pallas_call(
        body, name="b_foxpost", grid=(steps,),
        in_specs=[pl.BlockSpec((G * BLK, BLK), rev), pl.BlockSpec((G, 8, BLK), lambda i: (steps - 1 - i, 0, 0)),
                  pl.BlockSpec((G * BLK, BLK), rev), _full((1, BLK))],
        out_specs=[pl.BlockSpec((G * BLK, BLK), rev), pl.BlockSpec((G * BLK, BLK), rev), _full((1, BLK))],
        out_shape=[jax.ShapeDtypeStruct((L, BLK), F32), jax.ShapeDtypeStruct((L, BLK), BF16),
                   jax.ShapeDtypeStruct((1, BLK), F32)],
        scratch_shapes=[pltpu.VMEM((1, BLK), F32)],
        compiler_params=_params(("arbitrary",)),
    )(dc, dcq, ff, fb)


def _inproj_bwd(dpr, dpf, dffb, w_main, w_ff, h0, g, dh1, scatter=()):
    L = h0.shape[0]
    S = L - BLK
    tm = _row_tile(S, (512, 256, 128))
    nt = S // tm
    ns = len(scatter)
    operands = (dpr, dpf, dffb, h0, dh1)

    def body(*refs):
        lead, tile = refs[0:5], refs[5:10]
        wm_ref, wf_ref, g_ref = refs[10:13]
        rest = refs[13:]
        s_in, (dlead_ref, dx_ref, dg_ref), s_out = rest[:ns], rest[ns:ns + 3], rest[ns + 3:2 * ns + 3]
        i = pl.program_id(0)

        def rows_bwd(dpr_ref, dpf_ref, dff_ref, h_ref, dh1_ref):
            dn = (_dot_nt(dpr_ref[...], wm_ref[:, 0:RET_W]) + _dot_nt(dpf_ref[...], wm_ref[:, RET_W:MAIN_W])
                  + _dot_nt(dff_ref[...], wf_ref[...]))
            h = h_ref[...]
            r = lax.rsqrt(jnp.mean(h * h, axis=-1, keepdims=True) + EPS)
            yn = h * r
            dyn = dn * g_ref[...]
            dh0 = dh1_ref[...] + r * (dyn - yn * jnp.mean(dyn * yn, axis=-1, keepdims=True))
            return dh0, jnp.sum(dn * yn, axis=0, keepdims=True)

        @pl.when(i == 0)
        def _():
            if ns:
                for cp in _scatter_copies(s_in, s_out, *rest[2 * ns + 3:]):
                    cp.start()
            dlead_ref[...], dg_ref[...] = rows_bwd(*lead)

        dx_ref[...], dg_tile = rows_bwd(*tile)
        dg_ref[...] = dg_ref[...] + dg_tile

        if ns:
            @pl.when(i == nt - 1)
            def _():
                copies = _scatter_copies(s_in, s_out, *rest[2 * ns + 3:])
                for cp in copies:
                    cp.wait_recv()
                for cp in copies:
                    cp.wait_send()

    lead_spec = lambda a: pl.BlockSpec((BLK, a.shape[1]), lambda i: (0, 0))
    tile_spec = lambda a: pl.BlockSpec((pl.Element(tm), pl.Element(a.shape[1])),
                                       lambda i: (pl.multiple_of(BLK + i * tm, BLK), 0))
    return pl.pallas_call(
        body, name="b_inproj", grid=(nt,),
        in_specs=[lead_spec(a) for a in operands] + [tile_spec(a) for a in operands]
        + [_full((D_MODEL, MAIN_W)), _full((D_MODEL, BLK)), _full((1, D_MODEL))] + [_ANY] * ns,
        out_specs=[_full((BLK, D_MODEL)), pl.BlockSpec((tm, D_MODEL), lambda i: (i, 0)), _full((1, D_MODEL))]
        + [_ANY] * ns,
        out_shape=[jax.ShapeDtypeStruct((BLK, D_MODEL), F32), jax.ShapeDtypeStruct((S, D_MODEL), F32),
                   jax.ShapeDtypeStruct((1, D_MODEL), F32)] + _scatter_shapes(scatter),
        scratch_shapes=_scatter_semaphores(ns),
        compiler_params=_params(("arbitrary",)),
    )(*operands, *operands, w_main, w_ff, g, *scatter)


def _local_step(x, target, meta, attn_g, w_main, w_ff, fox_b, ret_g, w_out, ffn_g, w_up, conv_w, conv_b, w_down, final_g,
                late=None, mid=None, last=None, wire=F32):
    S = x.shape[0]
    L = S + PREFIX
    head = jnp.concatenate([jnp.zeros((N_PAD, D_MODEL), F32), meta], axis=0)
    fb = jnp.pad(fox_b, ((0, 0), (0, BLK - FOX_HEADS)))
    cos_t, sin_t = _rotary_tables(L)

    h0, n1, proj, ff = _rms_inproj(head, x, attn_g, w_main, w_ff)
    c, ctb = _fox_prep(ff, fb)
    mix_r, o_ret, states = _retention_fwd(proj, cos_t, sin_t, ret_g)
    if late is None:
        o_f, lse = _fox_fwd(proj, c, ctb)
    else:
        o_f, lse, *gathered = _fox_fwd(proj, c, ctb, gather=late[0])
        w_out, w_up, w_down = late[1](gathered)
    h1, n2, up, g_act, acc_saved = _outproj_up(mix_r, o_f, h0, w_out, ffn_g, w_up, conv_w, conv_b)
    dh2, dh2b, d_final_g, loss, dacc, db = _ffn_down_loss(g_act, w_down, h1, final_g, target, acc_saved, up)

    dup, dh1, dh1b, dmix, d_ffn_g, dconv = _ffn_bwd_up(dacc, db, up, conv_w, w_up, h1, ffn_g, dh2, w_out)
    d_w_down = _wgrad(g_act, dh2b, "wgrad_down", tk=D_FF // 2, out_dtype=wire)[0]
    d_w_up = _wgrad(n2, dup, "wgrad_up", tn=w_up.shape[2], out_dtype=wire)
    d_w_out = jnp.concatenate([_wgrad(mix_r, dh1b, "wgrad_out_r", out_dtype=wire)[0],
                               _wgrad(o_f, dh1b, "wgrad_out_f", out_dtype=wire)[0]], axis=0)

    early = () if mid is None else mid[0](d_w_out, d_w_up, d_w_down)
    dpr, d_ret_g, *from_sibling = _retention_bwd(dmix, o_ret, proj, cos_t, sin_t, ret_g, states, exchange=early)
    delta = _fox_delta(dmix, o_f)
    scatter = () if mid is None else mid[1](early, from_sibling)
    dpf, dc, dcq, *received = _fox_bwd(proj, dmix, c, ctb, lse, delta, scatter=scatter)
    dff, dffb, d_fox_b = _fox_post(dc, dcq, ff, fb)
    d_w_ret, d_w_fox = _wgrad(n1, dpr, "wgrad_in_r")[0], _wgrad(n1, dpf, "wgrad_in_f")[0]
    d_w_ff = _wgrad(n1, dffb, "wgrad_in_ff")[0][:, :FOX_HEADS]
    scatter_in = () if last is None else last(d_w_ret, d_w_fox, d_w_ff)
    dlead, dx, d_attn_g, *received_in = _inproj_bwd(dpr, dpf, dffb, w_main, w_ff, h0, attn_g, dh1, scatter=scatter_in)

    return dict(
        loss=loss[0, 0], dx=dx, dmeta=dlead[N_PAD:], attn_g=d_attn_g, w_main=jnp.concatenate([d_w_ret, d_w_fox], axis=1),
        w_ff=d_w_ff, fox_b=d_fox_b[:, :FOX_HEADS], ret_g=d_ret_g, w_out=d_w_out, ffn_g=d_ffn_g,
        w_up=d_w_up, conv_w=dconv[0:3], conv_b=dconv[3:4], w_down=d_w_down, final_g=d_final_g,
        scatter=list(scatter_in) + list(scatter), received=list(received_in) + list(received))


_ANY = pl.BlockSpec(memory_space=pl.ANY)


def _place():
    return lax.axis_index("x"), lax.axis_index("y"), lax.axis_index("c")


def _other_chips(x, y):
    return [(1 - x, y), (x, 1 - y), (1 - x, 1 - y)]


def _allgather_semaphores(n):
    if n == 0:
        return []
    return [pltpu.SemaphoreType.DMA((3 * n,)), pltpu.SemaphoreType.DMA((3 * n,)), pltpu.SemaphoreType.DMA((n,))]


def _allgather_copies(ins, outs, send, recv, loc):
    n = len(ins)
    x, y, c = _place()
    mine = 2 * x + y
    peers = _other_chips(x, y)

    def remote(a, k, slot):
        return pltpu.make_async_remote_copy(
            src_ref=ins[a], dst_ref=outs[a].at[slot], send_sem=send.at[3 * a + k], recv_sem=recv.at[3 * a + k],
            device_id=(peers[k][0], peers[k][1], c), device_id_type=MESH)

    local = [pltpu.make_async_copy(ins[a], outs[a].at[mine], loc.at[a]) for a in range(n)]
    sends = [remote(a, k, mine) for a in range(n) for k in range(3)]
    recvs = [remote(a, k, 2 * peers[k][0] + peers[k][1]) for a in range(n) for k in range(3)]
    return local, sends, recvs


def _chip_allgather_halves(w, small):
    half = w.shape[0] // 2

    def body(w_ref, s_ref, wo_ref, so_ref, send, recv, fsend, frecv, ssend, srecv, loc):
        x, y, c = _place()
        mine = 2 * x + y
        peers = _other_chips(x, y)

        def fetch(k, slot):
            return pltpu.make_async_remote_copy(
                src_ref=w_ref.at[pl.ds(c * half, half)], dst_ref=wo_ref.at[slot, c], send_sem=send.at[k],
                recv_sem=recv.at[k], device_id=(peers[k][0], peers[k][1], c), device_id_type=MESH)

        def forward(k, which):
            slot = 2 * peers[k][0] + peers[k][1]
            return pltpu.make_async_remote_copy(
                src_ref=wo_ref.at[slot, which], dst_ref=wo_ref.at[slot, which], send_sem=fsend.at[k],
                recv_sem=frecv.at[k], device_id=(x, y, 1 - c), device_id_type=MESH)

        def small_copy(k, slot):
            return pltpu.make_async_remote_copy(
                src_ref=s_ref, dst_ref=so_ref.at[slot], send_sem=ssend.at[k], recv_sem=srecv.at[k],
                device_id=(peers[k][0], peers[k][1], c), device_id_type=MESH)

        local = pltpu.make_async_copy(s_ref, so_ref.at[mine], loc.at[0])
        sends = [fetch(k, mine) for k in range(3)] + [small_copy(k, mine) for k in range(3)]
        local.start()
        for cp in sends:
            cp.start()
        forwards = []
        for k in range(3):
            fetch(k, 2 * peers[k][0] + peers[k][1]).wait_recv()
            forwards.append(forward(k, c))
            forwards[-1].start()
        for k in range(3):
            forward(k, 1 - c).wait_recv()
            small_copy(k, 2 * peers[k][0] + peers[k][1]).wait_recv()
        for cp in sends + forwards:
            cp.wait_send()
        local.wait()

    three = pltpu.SemaphoreType.DMA((3,))
    return pl.pallas_call(
        body, name="ag_weights", in_specs=[_ANY] * 2, out_specs=[_ANY] * 2,
        out_shape=[jax.ShapeDtypeStruct((N_CHIPS, 2, half, w.shape[1]), w.dtype),
                   jax.ShapeDtypeStruct((N_CHIPS,) + small.shape, small.dtype)],
        scratch_shapes=[three, three, three, three, three, three, pltpu.SemaphoreType.DMA((1,))],
    )(w, small)


def _sibling_halves(grads):
    n = len(grads)

    def body(*refs):
        sends, recvs = _sibling_half_copies(refs[:n], refs[n:2 * n], *refs[2 * n:])
        for cp in sends:
            cp.start()
        for cp in recvs:
            cp.wait_recv()
        for cp in sends:
            cp.wait_send()

    return pl.pallas_call(
        body, name="rs_sibling", in_specs=[_ANY] * n, out_specs=[_ANY] * n,
        out_shape=_sibling_half_shapes(grads), scratch_shapes=_sibling_half_semaphores(n),
    )(*grads)


def _sibling_half_shapes(grads):
    return [jax.ShapeDtypeStruct((N_CHIPS, g.shape[1] // 2, g.shape[2]), g.dtype) for g in grads]


def _sibling_half_semaphores(n):
    return [pltpu.SemaphoreType.DMA((n,)), pltpu.SemaphoreType.DMA((n,))] if n else []


def _sibling_half_copies(ins, outs, send, recv):
    x, y, c = _place()

    def half_copy(a, which):
        half = ins[a].shape[1] // 2
        return pltpu.make_async_remote_copy(
            src_ref=ins[a].at[pl.ds(0, N_CHIPS), pl.ds(which * half, half)], dst_ref=outs[a],
            send_sem=send.at[a], recv_sem=recv.at[a], device_id=(x, y, 1 - c), device_id_type=MESH)

    return [half_copy(a, 1 - c) for a in range(len(ins))], [half_copy(a, c) for a in range(len(ins))]


def _scatter_shapes(parts):
    return [jax.ShapeDtypeStruct((3,) + p.shape[1:], p.dtype) for p in parts]


def _scatter_semaphores(n):
    return [pltpu.SemaphoreType.DMA((3 * n,)), pltpu.SemaphoreType.DMA((3 * n,))] if n else []


def _scatter_copies(ins, outs, send, recv):
    x, y, c = _place()
    peers = _other_chips(x, y)
    return [pltpu.make_async_remote_copy(
        src_ref=ins[a].at[2 * peers[k][0] + peers[k][1]], dst_ref=outs[a].at[k], send_sem=send.at[3 * a + k],
        recv_sem=recv.at[3 * a + k], device_id=(peers[k][0], peers[k][1], c), device_id_type=MESH)
        for a in range(len(ins)) for k in range(3)]


def _sibling_allgather(bufs, small):
    n = len(bufs)

    def body(*refs):
        small_in, outs, small_out = refs[n], refs[n + 1:2 * n + 1], refs[2 * n + 1]
        send, recv, s_send, s_recv, loc = refs[2 * n + 2:]
        x, y, c = _place()
        me = 4 * x + 2 * y + c

        def remote(a, which):
            return pltpu.make_async_remote_copy(
                src_ref=outs[a].at[which], dst_ref=outs[a].at[which], send_sem=send.at[a], recv_sem=recv.at[a],
                device_id=(x, y, 1 - c), device_id_type=MESH)

        def peer_of(r):
            return tuple(1 - v if (r >> b) & 1 else v for v, b in ((x, 2), (y, 1), (c, 0)))

        def small_copy(r, slot):
            return pltpu.make_async_remote_copy(
                src_ref=small_in, dst_ref=small_out.at[slot], send_sem=s_send.at[r - 1], recv_sem=s_recv.at[r - 1],
                device_id=peer_of(r), device_id_type=MESH)

        local = pltpu.make_async_copy(small_in, small_out.at[me], loc.at[0])
        sends = [remote(a, c) for a in range(n)] + [small_copy(r, me) for r in range(1, N_DEV)]
        local.start()
        for cp in sends:
            cp.start()
        for r in range(1, N_DEV):
            px, py, pc = peer_of(r)
            small_copy(r, 4 * px + 2 * py + pc).wait_recv()
        for a in range(n):
            remote(a, 1 - c).wait_recv()
        for cp in sends:
            cp.wait_send()
        local.wait()

    outs = pl.pallas_call(
        body, name="ag_sibling", in_specs=[_ANY] * (n + 1), out_specs=[_ANY] * (n + 1),
        out_shape=[jax.ShapeDtypeStruct(b.shape, b.dtype) for b in bufs]
        + [jax.ShapeDtypeStruct((N_DEV,) + small.shape, small.dtype)],
        input_output_aliases={a: a for a in range(n)},
        scratch_shapes=[pltpu.SemaphoreType.DMA((n,)), pltpu.SemaphoreType.DMA((n,)),
                        pltpu.SemaphoreType.DMA((N_DEV - 1,)), pltpu.SemaphoreType.DMA((N_DEV - 1,)),
                        pltpu.SemaphoreType.DMA((1,))],
    )(*bufs, small)
    return [o.reshape(2 * o.shape[1], o.shape[2]) for o in outs[:n]], outs[n]


def _pair_add(full, recv, core, name):
    _, R, C = full.shape
    half = R // 2

    def body(core_ref, a_ref, b_ref, o_ref):
        o_ref[...] = (a_ref[...].astype(F32) + b_ref[...].astype(F32)).astype(BF16)

    return pl.pallas_call(
        body, name=name,
        grid_spec=pltpu.PrefetchScalarGridSpec(
            num_scalar_prefetch=1, grid=(N_CHIPS,),
            in_specs=[pl.BlockSpec((1, half, C), lambda j, core_ref: (j, core_ref[0], 0)),
                      pl.BlockSpec((1, half, C), lambda j, core_ref: (j, 0, 0))],
            out_specs=pl.BlockSpec((1, half, C), lambda j, core_ref: (j, 0, 0))),
        out_shape=jax.ShapeDtypeStruct((N_CHIPS, half, C), BF16),
        compiler_params=_params(("parallel",)),
    )(core, full, recv)


def _sum_partials(own_all, recv, place, name, tiles=2):
    _, R, C = own_all.shape
    tr = R // tiles

    def body(place_ref, own_ref, r_ref, o_ref):
        acc = own_ref[0].astype(F32)
        for k in range(3):
            acc = acc + r_ref[k].astype(F32)
        o_ref[0] = acc

    return pl.pallas_call(
        body, name=name,
        grid_spec=pltpu.PrefetchScalarGridSpec(
            num_scalar_prefetch=1, grid=(tiles,),
            in_specs=[pl.BlockSpec((1, tr, C), lambda i, place_ref: (place_ref[0], i, 0)),
                      pl.BlockSpec((3, tr, C), lambda i, place_ref: (0, i, 0))],
            out_specs=pl.BlockSpec((1, tr, C), lambda i, place_ref: (place_ref[1], i, 0))),
        out_shape=jax.ShapeDtypeStruct((2, R, C), F32),
        compiler_params=_params(("parallel",)),
    )(place, own_all, recv)


def _adamw_math(w, g, m, v):
    m2 = ADAM_B1 * m + (1.0 - ADAM_B1) * g
    v2 = ADAM_B2 * v + (1.0 - ADAM_B2) * (g * g)
    m_hat = m2 / (1.0 - ADAM_B1 ** ADAM_STEP)
    v_hat = v2 / (1.0 - ADAM_B2 ** ADAM_STEP)
    return -ADAM_LR * (m_hat / (jnp.sqrt(v_hat) + ADAM_EPS) + ADAM_WD * w), m2, v2


ROW_ATTN_G, ROW_FFN_G, ROW_FINAL_G, ROW_MISC, ROW_CONV_B, ROW_CONV_W, ROW_META, SMALL_ROWS = 0, 1, 2, 3, 4, 8, 24, 40
MISC_FOX_B, MISC_LOSS = 512, 640


def _small_pack(out):
    def rows(a, n):
        a = a.astype(F32)
        return jnp.pad(a, ((0, n - a.shape[0]), (0, D_MODEL - a.shape[1])))

    misc = jnp.concatenate([out["ret_g"], out["fox_b"], jnp.zeros((1, MISC_LOSS - MISC_FOX_B - FOX_HEADS), F32),
                            out["loss"].reshape(1, 1)], axis=1)
    conv_b = jnp.pad(out["conv_b"], ((0, 0), (0, (-D_FF) % D_MODEL))).reshape(-1, D_MODEL)
    conv_w = out["conv_w"].reshape(3, N_CHIPS, -1).transpose(1, 0, 2).reshape(3 * N_CHIPS, -1)
    return jnp.concatenate([
        rows(out["attn_g"], 1), rows(out["ffn_g"], 1), rows(out["final_g"], 1), rows(misc, 1),
        rows(conv_b, ROW_CONV_W - ROW_CONV_B), rows(conv_w, ROW_META - ROW_CONV_W), rows(out["dmeta"], N_META)], axis=0)


def _small_update(packs, chip, ws, ms, vs):
    n = len(ws)
    meta_w, conv_sw = ws[0].shape[1], ws[5].shape[2]
    assert packs.shape == (N_DEV, SMALL_ROWS, D_MODEL) and ws[0].shape[0] == N_META and ws[5].shape[:2] == (3, 1)

    def body(chip_ref, p_ref, *refs):
        w_refs, m_refs, v_refs = refs[:n], refs[n:2 * n], refs[2 * n:3 * n]
        loss_ref, out_refs, tot = refs[3 * n], refs[3 * n + 1:7 * n + 1], refs[7 * n + 1]
        acc = p_ref[0]
        for d in range(1, N_DEV):
            acc = acc + p_ref[d]
        tot[...] = acc

        def of_chip(pieces):
            val = pieces[-1]
            for j in range(N_CHIPS - 2, -1, -1):
                val = jnp.where(chip_ref[0] == j, pieces[j], val)
            return val

        row = lambda r, lo=0, hi=D_MODEL: tot[r:r + 1, lo:hi]
        grads = [
            of_chip([tot[ROW_META:ROW_META + N_META, j * meta_w:(j + 1) * meta_w] for j in range(N_CHIPS)]),
            row(ROW_ATTN_G), row(ROW_MISC, MISC_FOX_B, MISC_FOX_B + FOX_HEADS), row(ROW_MISC, 0, MISC_FOX_B),
            row(ROW_FFN_G),
            of_chip([tot[ROW_CONV_W + 3 * j:ROW_CONV_W + 3 * j + 3, 0:conv_sw] for j in range(N_CHIPS)]),
            jnp.concatenate([row(ROW_CONV_B), row(ROW_CONV_B + 1), row(ROW_CONV_B + 2, 0, D_FF - 2 * D_MODEL)], axis=1),
            row(ROW_FINAL_G)]
        loss_ref[...] = row(ROW_MISC, MISC_LOSS, MISC_LOSS + BLK)
        for k in range(n):
            parts = [((Ellipsis,), grads[k])]
            if len(ws[k].shape) == 3:
                parts = [((t,), grads[k][t:t + 1]) for t in range(ws[k].shape[0])]
            for at, g in parts:
                res = (g,) + _adamw_math(w_refs[k][at], g, m_refs[k][at], v_refs[k][at])
                for kind in range(4):
                    out_refs[kind * n + k][at] = res[kind]

    res = pl.pallas_call(
        body, name="small_update",
        grid_spec=pltpu.PrefetchScalarGridSpec(
            num_scalar_prefetch=1, grid=(1,),
            in_specs=[_full(packs.shape)] + [_full(a.shape) for a in list(ws) * 3],
            out_specs=[_full((1, BLK))] + [_full(a.shape) for a in list(ws) * 4],
            scratch_shapes=[pltpu.VMEM((SMALL_ROWS, D_MODEL), F32)]),
        out_shape=[jax.ShapeDtypeStruct((1, BLK), F32)] + [jax.ShapeDtypeStruct(a.shape, F32) for a in list(ws) * 4],
        compiler_params=_params(("arbitrary",)),
    )(chip, packs, *ws, *ms, *vs)
    return res[0], res[1:n + 1], res[n + 1:2 * n + 1], res[2 * n + 1:3 * n + 1], res[3 * n + 1:]


def _adamw(w, g, m, v, name, tiles=4):
    R, tail = w.shape[0], w.shape[1:]
    assert R % tiles == 0
    tr = R // tiles

    def body(w_ref, g_ref, m_ref, v_ref, go_ref, d_ref, m2_ref, v2_ref):
        g_ = g_ref[...]
        go_ref[...] = g_
        d_ref[...], m2_ref[...], v2_ref[...] = _adamw_math(w_ref[...], g_, m_ref[...], v_ref[...])

    spec = pl.BlockSpec((tr,) + tail, lambda i: (i,) + (0,) * len(tail))
    return pl.pallas_call(
        body, name=name, grid=(tiles,), in_specs=[spec] * 4, out_specs=[spec] * 4,
        out_shape=[jax.ShapeDtypeStruct(w.shape, F32)] * 4,
        compiler_params=_params(("parallel",)),
    )(w, g, m, v)


def _row_vector_tiles(n, most=80):
    return next(t for t in range(1, n + 1) if n % t == 0 and n // t <= most)


def _pack_rows(pieces, rows):
    flat = jnp.concatenate([jnp.pad(p.reshape(-1).astype(F32), (0, (-p.size) % D_MODEL)) for p in pieces])
    return jnp.pad(flat, (0, rows * D_MODEL - flat.size)).reshape(rows, D_MODEL)


def _unpack_rows(pack, shapes):
    flat = pack.reshape(-1)
    out, off = [], 0
    for shp in shapes:
        size = int(np.prod(shp))
        out.append(flat[off:off + size].reshape(shp))
        off += size + (-size) % D_MODEL
    return out


IN_PADDED = IN_WIDTH + (-IN_WIDTH) % BLK


def _fox_column_blocks():
    return [(RET_W + part * 512 + p * BLK, RET_W + 384 * p + part * BLK)
            for part in range(3) for p in range(FOX_HEADS // 2)]


def _w_in_kernel_order(gathered, own, chip):
    n, R, C = gathered.shape
    tr = R // 4

    def body(chip_ref, g_ref, own_ref, wm_ref, wf_ref, full):
        for j in range(n):
            @pl.when(chip_ref[0] == j)
            def _(j=j):
                full[:, j * C:(j + 1) * C] = own_ref[...]

            @pl.when(chip_ref[0] != j)
            def _(j=j):
                full[:, j * C:(j + 1) * C] = g_ref[j]

        full[:, n * C:] = jnp.zeros((tr, IN_PADDED - n * C), BF16)
        wm_ref[:, 0:RET_W] = full[:, 0:RET_W]
        for src, dst in _fox_column_blocks():
            wm_ref[:, dst:dst + BLK] = full[:, src:src + BLK]
        wf_ref[...] = full[:, MAIN_W:MAIN_W + BLK]

    return pl.pallas_call(
        body, name="w_in_kernel_order",
        grid_spec=pltpu.PrefetchScalarGridSpec(
            num_scalar_prefetch=1, grid=(R // tr,),
            in_specs=[pl.BlockSpec((n, tr, C), lambda i, c: (0, i, 0)), pl.BlockSpec((tr, C), lambda i, c: (i, 0))],
            out_specs=[pl.BlockSpec((tr, MAIN_W), lambda i, c: (i, 0)), pl.BlockSpec((tr, BLK), lambda i, c: (i, 0))],
            scratch_shapes=[pltpu.VMEM((tr, IN_PADDED), BF16)]),
        out_shape=[jax.ShapeDtypeStruct((R, MAIN_W), BF16), jax.ShapeDtypeStruct((R, BLK), BF16)],
        compiler_params=_params(("arbitrary",)),
    )(chip, gathered, own)


def _w_in_grad_shards(g_ret, g_fox, g_ff):
    R = g_ret.shape[0]
    C = IN_WIDTH // N_CHIPS
    tr = R // 4

    def body(gr_ref, gx_ref, gf_ref, o_ref, full):
        full[:, 0:RET_W] = gr_ref[...]
        for src, dst in _fox_column_blocks():
            full[:, src:src + BLK] = gx_ref[:, dst - RET_W:dst - RET_W + BLK]
        full[:, MAIN_W:MAIN_W + FOX_HEADS] = gf_ref[...]
        for j in range(N_CHIPS):
            o_ref[j] = full[:, j * C:(j + 1) * C].astype(BF16)

    rows = lambda w: pl.BlockSpec((tr, w), lambda i: (i, 0))
    return pl.pallas_call(
        body, name="w_in_grad_shards", grid=(R // tr,),
        in_specs=[rows(RET_W), rows(FOX_W), rows(FOX_HEADS)],
        out_specs=pl.BlockSpec((N_CHIPS, tr, C), lambda i: (0, i, 0)),
        out_shape=jax.ShapeDtypeStruct((N_CHIPS, R, C), BF16),
        scratch_shapes=[pltpu.VMEM((tr, IN_PADDED), F32)],
        compiler_params=_params(("parallel",)),
    )(g_ret, g_fox, g_ff)


def kernel(x, meta_tokens, attn_norm_g, w_in, fox_forget_b, ret_norm_g, w_out, ffn_norm_g, w_up, conv_w, conv_b, w_down, final_norm_g, loss_target, m_meta_tokens, m_attn_norm_g, m_w_in, m_fox_forget_b, m_ret_norm_g, m_w_out, m_ffn_norm_g, m_w_up, m_conv_w, m_conv_b, m_w_down, m_final_norm_g, v_meta_tokens, v_attn_norm_g, v_w_in, v_fox_forget_b, v_ret_norm_g, v_w_out, v_ffn_norm_g, v_w_up, v_conv_w, v_conv_b, v_w_down, v_final_norm_g):
    chip = 2 * lax.axis_index("x") + lax.axis_index("y")
    core = lax.axis_index("c")

    small_w = _pack_rows([meta_tokens, conv_w[0]], 8)
    w_in_b = w_in[0].astype(BF16)
    g_in, g_small = _chip_allgather_halves(w_in_b, small_w)
    chip_idx = chip.reshape(1).astype(jnp.int32)
    w_main, w_ff = _w_in_kernel_order(g_in.reshape((N_CHIPS,) + w_in_b.shape), w_in_b, chip_idx)
    small_parts = [_unpack_rows(g_small[j], [meta_tokens.shape, conv_w.shape[1:]]) for j in range(N_CHIPS)]
    meta_full = jnp.concatenate([sp[0] for sp in small_parts], axis=1)
    conv_w_full = jnp.concatenate([sp[1] for sp in small_parts], axis=1)

    core_idx = core.reshape(1).astype(jnp.int32)
    place = jnp.stack([chip, core]).astype(jnp.int32)

    def assemble(gathered):
        g_out, g_up, g_down = gathered
        return g_out.reshape(D_MODEL, D_MODEL), g_up, g_down.reshape(D_FF, D_MODEL)

    def early_arrays(d_w_out, d_w_up, d_w_down):
        return [d_w_out.reshape(N_CHIPS, -1, D_MODEL), d_w_up, d_w_down.reshape(N_CHIPS, -1, D_MODEL)]

    def in_sums(d_w_ret, d_w_fox, d_w_ff):
        g_in_full = _w_in_grad_shards(d_w_ret, d_w_fox, d_w_ff)
        (from_sib,) = _sibling_halves([g_in_full])
        return [_pair_add(g_in_full, from_sib, core_idx, "pair_add_in")]

    def early_sums(early, from_sib):
        return [_pair_add(g, r, core_idx, "pair_add_" + nm) for g, r, nm in zip(early, from_sib, ("out", "up", "down"))]

    out = _local_step(x[0], loss_target[0], meta_full, attn_norm_g, w_main, w_ff, fox_forget_b, ret_norm_g,
                      None, ffn_norm_g, None, conv_w_full, conv_b, None, final_norm_g[None],
                      late=([w_out[0].astype(BF16), w_up[0].astype(BF16), w_down[0].astype(BF16)], assemble),
                      mid=(early_arrays, early_sums), last=in_sums, wire=BF16)

    names = ("in", "out", "up", "down")
    totals = [_sum_partials(s, q, place, "sum_chips_" + nm) for s, q, nm in zip(out["scatter"], out["received"], names)]
    (grad_in, grad_out, grad_up, grad_down), small_all = _sibling_allgather(totals, _small_pack(out))

    big_w = [(w_out, m_w_out, v_w_out, grad_out, "adamw_out"), (w_up, m_w_up, v_w_up, grad_up, "adamw_up"),
             (w_down, m_w_down, v_w_down, grad_down, "adamw_down")]
    big_res = [[r[None] for r in _adamw(w[0], g, m[0], v[0], nm)] for w, m, v, g, nm in big_w]
    as_rows = lambda a: jnp.transpose(a, (2, 0, 1))
    in_rows = _adamw(as_rows(w_in), grad_in.T[:, None, :], as_rows(m_w_in), as_rows(v_w_in), "adamw_in",
                     tiles=_row_vector_tiles(w_in.shape[2]))
    big_res.insert(0, [jnp.transpose(r, (1, 2, 0)) for r in in_rows])
    tap_rows = lambda a: jnp.transpose(a, (1, 0, 2))
    small_p = [meta_tokens, attn_norm_g, fox_forget_b, ret_norm_g, ffn_norm_g, tap_rows(conv_w), conv_b, final_norm_g[None]]
    small_m = [m_meta_tokens, m_attn_norm_g, m_fox_forget_b, m_ret_norm_g, m_ffn_norm_g, tap_rows(m_conv_w), m_conv_b,
               m_final_norm_g[None]]
    small_v = [v_meta_tokens, v_attn_norm_g, v_fox_forget_b, v_ret_norm_g, v_ffn_norm_g, tap_rows(v_conv_w), v_conv_b,
               v_final_norm_g[None]]
    loss_row, *small_res = _small_update(small_all, chip_idx, small_p, small_m, small_v)
    loss = loss_row[0, 0]

    def ordered(kind):
        sm = list(small_res[kind][:-1]) + [small_res[kind][-1][0]]
        sm[5] = tap_rows(sm[5])
        bg = [r[kind] for r in big_res]
        return [sm[0], sm[1], bg[0], sm[2], sm[3], bg[1], sm[4], bg[2], sm[5], sm[6], bg[3], sm[7]]

    return (loss, out["dx"][None], *ordered(0), *ordered(1), *ordered(2), *ordered(3))
```
